```python
import math
import jax, jax.numpy as jnp
from jax import lax
import numpy as np

D_MODEL = 1024
BATCH = 8
SEQ = 4096
DEPTH = 1

HEAD_DIM = 64
N_HEADS = D_MODEL // HEAD_DIM
A_Q_HEADS = N_HEADS // 2
A_KV_HEADS = 2
A_GROUP = A_Q_HEADS // A_KV_HEADS
A_MAX_DIST = 127
B_HEADS = N_HEADS - A_Q_HEADS
B_PATTERNS = ((128, 1), (512, 4), (2048, 16))
BLOCK = 128
D_FF = 2816
CONV_WIDTH = 3
ALPHA = (2.0 * DEPTH) ** 0.25
BETA = (8.0 * DEPTH) ** -0.25
LN_EPS = 1e-5
RMS_EPS = 1e-6

A_Q_W = A_Q_HEADS * HEAD_DIM
A_KV_W = A_KV_HEADS * HEAD_DIM
B_W = B_HEADS * HEAD_DIM
IN_SPLITS = (A_Q_W, A_KV_W, A_KV_W, B_W, B_W, B_W)
IN_W = sum(IN_SPLITS)

kernel_name = "hymba_swa_sink_dilated_convffn_deepnorm"


def alibi_slopes(n):
    return jnp.asarray(np.array([2.0 ** (-8.0 * (i + 1) / n) for i in range(n)], dtype=np.float32))


def layer_norm(x, g, b):
    xf = x.astype(jnp.float32)
    mu = jnp.mean(xf, -1, keepdims=True)
    var = jnp.mean(jnp.square(xf - mu), -1, keepdims=True)
    return ((xf - mu) * lax.rsqrt(var + LN_EPS) * g.astype(jnp.float32) + b.astype(jnp.float32)).astype(x.dtype)


def rms_norm(x, g):
    xf = x.astype(jnp.float32)
    return (xf * lax.rsqrt(jnp.mean(jnp.square(xf), -1, keepdims=True) + RMS_EPS) * g.astype(jnp.float32)).astype(x.dtype)


def banded_attention(q, k, v, slope, max_dist, dist_unit, sink=None):
    L, dh = q.shape[-2], q.shape[-1]
    nb = -(-L // BLOCK)
    pad = nb * BLOCK - L
    q = jnp.pad(q, [(0, 0)] * (q.ndim - 2) + [(0, pad), (0, 0)])
    kv_pad = [(0, 0)] * (k.ndim - 2) + [(BLOCK, pad), (0, 0)]
    k = jnp.pad(k, kv_pad)
    v = jnp.pad(v, kv_pad)
    lead = k.shape[:-2]
    kb = k.reshape(*lead, nb + 1, BLOCK, dh)
    vb = v.reshape(*lead, nb + 1, BLOCK, dh)
    kw = jnp.concatenate([kb[..., :-1, :, :], kb[..., 1:, :, :]], axis=-2)
    vw = jnp.concatenate([vb[..., :-1, :, :], vb[..., 1:, :, :]], axis=-2)
    qb = q.reshape(*q.shape[:-2], nb, BLOCK, dh)
    s = jnp.einsum('...gnqd,...nkd->...gnqk', qb, kw,
                   preferred_element_type=jnp.float32) * (1.0 / math.sqrt(dh))
    qi = jnp.arange(BLOCK)[:, None]
    ki = jnp.arange(2 * BLOCK)[None, :]
    dist = BLOCK + qi - ki
    key_pos = jnp.arange(nb)[:, None, None] * BLOCK - BLOCK + ki[None]
    mask = (dist >= 0)[None] & (dist <= max_dist)[None] & (key_pos >= 0)
    s = s - slope.astype(jnp.float32) * (dist * dist_unit).astype(jnp.float32)
    s = jnp.where(mask, s, -jnp.inf)
    m = jnp.max(s, -1, keepdims=True)
    if sink is not None:
        sk = sink.astype(jnp.float32)
        m = jnp.maximum(m, sk)
        p = jnp.exp(s - m)
        l = jnp.sum(p, -1, keepdims=True) + jnp.exp(sk - m)
    else:
        p = jnp.exp(s - m)
        l = jnp.sum(p, -1, keepdims=True)
    o = jnp.einsum('...gnqk,...nkd->...gnqd', p, vw.astype(jnp.float32)) / l
    lse = (m + jnp.log(l))[..., 0]
    o = o.reshape(*o.shape[:-3], nb * BLOCK, dh)[..., :L, :].astype(q.dtype)
    lse = lse.reshape(*lse.shape[:-2], nb * BLOCK)[..., :L]
    return o, lse


def mixer_a(qa, ka, va, sinks):
    Bn, S, _ = qa.shape
    q = qa.reshape(Bn, S, A_KV_HEADS, A_GROUP, HEAD_DIM).transpose(0, 2, 3, 1, 4)
    k = ka.reshape(Bn, S, A_KV_HEADS, HEAD_DIM).transpose(0, 2, 1, 3)
    v = va.reshape(Bn, S, A_KV_HEADS, HEAD_DIM).transpose(0, 2, 1, 3)
    slope = alibi_slopes(A_Q_HEADS).reshape(A_KV_HEADS, A_GROUP, 1, 1, 1)
    sink = sinks.reshape(A_KV_HEADS, A_GROUP, 1, 1, 1)
    o, _ = banded_attention(q, k, v, slope, A_MAX_DIST, 1, sink)
    return o.transpose(0, 3, 1, 2, 4).reshape(Bn, S, A_Q_W)


def mixer_b(qb, kb, vb):
    Bn, S, _ = qb.shape
    q = qb.reshape(Bn, S, B_HEADS, HEAD_DIM).transpose(0, 2, 1, 3)
    k = kb.reshape(Bn, S, B_HEADS, HEAD_DIM).transpose(0, 2, 1, 3)
    v = vb.reshape(Bn, S, B_HEADS, HEAD_DIM).transpose(0, 2, 1, 3)
    slope = alibi_slopes(B_HEADS).reshape(B_HEADS, 1, 1, 1, 1, 1)
    outs, lses = [], []
    for (w, r) in B_PATTERNS:
        Lr = S // r
        qr = q.reshape(Bn, B_HEADS, Lr, r, HEAD_DIM).swapaxes(2, 3)[:, :, :, None]
        kr = k.reshape(Bn, B_HEADS, Lr, r, HEAD_DIM).swapaxes(2, 3)
        vr = v.reshape(Bn, B_HEADS, Lr, r, HEAD_DIM).swapaxes(2, 3)
        o, lse = banded_attention(qr, kr, vr, slope, w // r, r)
        outs.append(o[:, :, :, 0].swapaxes(2, 3).reshape(Bn, B_HEADS, S, HEAD_DIM))
        lses.append(lse[:, :, :, 0].swapaxes(2, 3).reshape(Bn, B_HEADS, S))
    wts = jax.nn.softmax(jnp.stack(lses, 0), axis=0)
    o = jnp.sum(wts[..., None] * jnp.stack(outs, 0).astype(jnp.float32), axis=0)
    return o.astype(qb.dtype).transpose(0, 2, 1, 3).reshape(Bn, S, B_W)


def causal_dwconv(u, w, b):
    K = w.shape[0]
    S = u.shape[1]
    up = jnp.pad(u, ((0, 0), (K - 1, 0), (0, 0)))
    y = up[:, 0:S, :] * w[0]
    for j in range(1, K):
        y = y + up[:, j:j + S, :] * w[j]
    return y + b


def _fwd_setup_inputs(seed: int = 0) -> dict:
    key = jax.random.key(seed)
    ks = jax.random.split(key, 16)
    f32 = jnp.float32
    x = jax.random.normal(ks[0], (BATCH, SEQ, D_MODEL), f32)
    col_scale = jnp.concatenate([
        jnp.ones((A_Q_W + A_KV_W,), f32), jnp.full((A_KV_W,), BETA, f32),
        jnp.ones((2 * B_W,), f32), jnp.full((B_W,), BETA, f32)])
    w_in = jax.random.normal(ks[1], (D_MODEL, IN_W), f32) * D_MODEL ** -0.5 * col_scale
    norm_a_g = 1.0 + 0.02 * jax.random.normal(ks[2], (A_Q_W,), f32)
    norm_b_g = 1.0 + 0.02 * jax.random.normal(ks[3], (B_W,), f32)
    sinks_a = 0.5 * jax.random.normal(ks[4], (A_Q_HEADS,), f32)
    w_o = jax.random.normal(ks[5], (D_MODEL, D_MODEL), f32) * D_MODEL ** -0.5 * BETA
    ln1_g = 1.0 + 0.02 * jax.random.normal(ks[6], (D_MODEL,), f32)
    ln1_b = 0.02 * jax.random.normal(ks[7], (D_MODEL,), f32)
    w_up = jax.random.normal(ks[8], (D_MODEL, 2 * D_FF), f32) * D_MODEL ** -0.5 * BETA
    conv_w = jax.random.normal(ks[9], (CONV_WIDTH, 2 * D_FF), f32) * CONV_WIDTH ** -0.5
    conv_b = 0.02 * jax.random.normal(ks[10], (2 * D_FF,), f32)
    w_down = jax.random.normal(ks[11], (D_FF, D_MODEL), f32) * D_FF ** -0.5 * BETA
    ln2_g = 1.0 + 0.02 * jax.random.normal(ks[12], (D_MODEL,), f32)
    ln2_b = 0.02 * jax.random.normal(ks[13], (D_MODEL,), f32)
    return {"x": x, "w_in": w_in, "norm_a_g": norm_a_g, "norm_b_g": norm_b_g,
            "sinks_a": sinks_a, "w_o": w_o, "ln1_g": ln1_g, "ln1_b": ln1_b,
            "w_up": w_up, "conv_w": conv_w, "conv_b": conv_b, "w_down": w_down,
            "ln2_g": ln2_g, "ln2_b": ln2_b}


def _fwd_reference(x, w_in, norm_a_g, norm_b_g, sinks_a, w_o, ln1_g, ln1_b,
              w_up, conv_w, conv_b, w_down, ln2_g, ln2_b):
    h = x
    for _ in range(DEPTH):
        proj = h @ w_in
        offs = np.cumsum((0,) + IN_SPLITS)
        qa, ka, va, qb, kb, vb = [proj[..., offs[i]:offs[i + 1]] for i in range(len(IN_SPLITS))]
        oa = rms_norm(mixer_a(qa, ka, va, sinks_a), norm_a_g)
        ob = rms_norm(mixer_b(qb, kb, vb), norm_b_g)
        mix = jnp.concatenate([oa, ob], axis=-1) @ w_o
        h = layer_norm(ALPHA * h + mix, ln1_g, ln1_b)
        u = causal_dwconv(h @ w_up, conv_w, conv_b)
        gate, val = u[..., :D_FF], u[..., D_FF:]
        ff = (jax.nn.gelu(gate) * val) @ w_down
        h = layer_norm(ALPHA * h + ff, ln2_g, ln2_b)
    return h


import jax as _jax
import jax.numpy as _jnp

TWIN_FORMAT = 'train_step'
FWD_PARAMS = ['x', 'w_in', 'norm_a_g', 'norm_b_g', 'sinks_a', 'w_o', 'ln1_g', 'ln1_b', 'w_up', 'conv_w', 'conv_b', 'w_down', 'ln2_g', 'ln2_b']
TWIN_WEIGHTS = ['w_in', 'norm_a_g', 'norm_b_g', 'sinks_a', 'w_o', 'ln1_g', 'ln1_b', 'w_up', 'conv_w', 'conv_b', 'w_down', 'ln2_g', 'ln2_b']
TWIN_DIFF_INPUT = 'x'
TWIN_INPUTS = ['x', 'w_in', 'norm_a_g', 'norm_b_g', 'sinks_a', 'w_o', 'ln1_g', 'ln1_b', 'w_up', 'conv_w', 'conv_b', 'w_down', 'ln2_g', 'ln2_b', 'loss_target', 'm_w_in', 'm_norm_a_g', 'm_norm_b_g', 'm_sinks_a', 'm_w_o', 'm_ln1_g', 'm_ln1_b', 'm_w_up', 'm_conv_w', 'm_conv_b', 'm_w_down', 'm_ln2_g', 'm_ln2_b', 'v_w_in', 'v_norm_a_g', 'v_norm_b_g', 'v_sinks_a', 'v_w_o', 'v_ln1_g', 'v_ln1_b', 'v_w_up', 'v_conv_w', 'v_conv_b', 'v_w_down', 'v_ln2_g', 'v_ln2_b']
TWIN_OUTPUTS = ['loss', 'grad_x', 'grad_w_in', 'grad_norm_a_g', 'grad_norm_b_g', 'grad_sinks_a', 'grad_w_o', 'grad_ln1_g', 'grad_ln1_b', 'grad_w_up', 'grad_conv_w', 'grad_conv_b', 'grad_w_down', 'grad_ln2_g', 'grad_ln2_b', 'delta_w_in', 'delta_norm_a_g', 'delta_norm_b_g', 'delta_sinks_a', 'delta_w_o', 'delta_ln1_g', 'delta_ln1_b', 'delta_w_up', 'delta_conv_w', 'delta_conv_b', 'delta_w_down', 'delta_ln2_g', 'delta_ln2_b', 'new_m_w_in', 'new_m_norm_a_g', 'new_m_norm_b_g', 'new_m_sinks_a', 'new_m_w_o', 'new_m_ln1_g', 'new_m_ln1_b', 'new_m_w_up', 'new_m_conv_w', 'new_m_conv_b', 'new_m_w_down', 'new_m_ln2_g', 'new_m_ln2_b', 'new_v_w_in', 'new_v_norm_a_g', 'new_v_norm_b_g', 'new_v_sinks_a', 'new_v_w_o', 'new_v_ln1_g', 'new_v_ln1_b', 'new_v_w_up', 'new_v_conv_w', 'new_v_conv_b', 'new_v_w_down', 'new_v_ln2_g', 'new_v_ln2_b']
TWIN_LEAF_KINDS = {'loss': 'loss', 'grad_x': 'grad_x', 'grad_w_in': 'grad_w', 'grad_norm_a_g': 'grad_w', 'grad_norm_b_g': 'grad_w', 'grad_sinks_a': 'grad_w', 'grad_w_o': 'grad_w', 'grad_ln1_g': 'grad_w', 'grad_ln1_b': 'grad_w', 'grad_w_up': 'grad_w', 'grad_conv_w': 'grad_w', 'grad_conv_b': 'grad_w', 'grad_w_down': 'grad_w', 'grad_ln2_g': 'grad_w', 'grad_ln2_b': 'grad_w', 'delta_w_in': 'delta_w', 'delta_norm_a_g': 'delta_w', 'delta_norm_b_g': 'delta_w', 'delta_sinks_a': 'delta_w', 'delta_w_o': 'delta_w', 'delta_ln1_g': 'delta_w', 'delta_ln1_b': 'delta_w', 'delta_w_up': 'delta_w', 'delta_conv_w': 'delta_w', 'delta_conv_b': 'delta_w', 'delta_w_down': 'delta_w', 'delta_ln2_g': 'delta_w', 'delta_ln2_b': 'delta_w', 'new_m_w_in': 'new_m', 'new_m_norm_a_g': 'new_m', 'new_m_norm_b_g': 'new_m', 'new_m_sinks_a': 'new_m', 'new_m_w_o': 'new_m', 'new_m_ln1_g': 'new_m', 'new_m_ln1_b': 'new_m', 'new_m_w_up': 'new_m', 'new_m_conv_w': 'new_m', 'new_m_conv_b': 'new_m', 'new_m_w_down': 'new_m', 'new_m_ln2_g': 'new_m', 'new_m_ln2_b': 'new_m', 'new_v_w_in': 'new_v', 'new_v_norm_a_g': 'new_v', 'new_v_norm_b_g': 'new_v', 'new_v_sinks_a': 'new_v', 'new_v_w_o': 'new_v', 'new_v_ln1_g': 'new_v', 'new_v_ln1_b': 'new_v', 'new_v_w_up': 'new_v', 'new_v_conv_w': 'new_v', 'new_v_conv_b': 'new_v', 'new_v_w_down': 'new_v', 'new_v_ln2_g': 'new_v', 'new_v_ln2_b': 'new_v'}


def _forward(args):
    return _fwd_reference(*[args[k] for k in FWD_PARAMS])


def _output_shape():
    out = _jax.eval_shape(lambda: _forward(_fwd_setup_inputs(0)))
    return out.shape, out.dtype

N_MICROBATCH = 1
ADAM_LR = 0.001
ADAM_B1 = 0.9
ADAM_B2 = 0.999
ADAM_EPS = 1e-08
ADAM_WD = 0.01
ADAM_STEP = 10
PER_EXAMPLE_BATCH_AXIS = {'x': 0, 'loss_target': 0}
SHARED_INPUTS = []
_WEIGHT_DTYPES = {'w_in': _jnp.float32, 'norm_a_g': _jnp.float32, 'norm_b_g': _jnp.float32, 'sinks_a': _jnp.float32, 'w_o': _jnp.float32, 'ln1_g': _jnp.float32, 'ln1_b': _jnp.float32, 'w_up': _jnp.float32, 'conv_w': _jnp.float32, 'conv_b': _jnp.float32, 'w_down': _jnp.float32, 'ln2_g': _jnp.float32, 'ln2_b': _jnp.float32}
MOMENT_SCALE = {'w_in': 1.051936e-01, 'norm_a_g': 8.539704e-02, 'norm_b_g': 9.774825e-02, 'sinks_a': 1.174035e-01, 'w_o': 1.353675e-01, 'ln1_g': 1.002465e+00, 'ln1_b': 4.783396e-01, 'w_up': 1.981785e-02, 'conv_w': 1.184776e-02, 'conv_b': 2.188213e-02, 'w_down': 3.218460e-02, 'ln2_g': 3.204415e+01, 'ln2_b': 6.293606e-01}


def _to_microbatches(a, axis):
    t = _jnp.moveaxis(a, axis, 0)
    t = t.reshape((N_MICROBATCH, t.shape[0] // N_MICROBATCH) + t.shape[1:])
    return _jnp.moveaxis(t, 1, axis + 1)


def setup_inputs(seed: int = 0) -> dict:
    inp = _fwd_setup_inputs(seed)
    key = _jax.random.fold_in(_jax.random.key(seed), 7919)
    shape, _ = _output_shape()
    out = dict(inp)
    out["loss_target"] = _jax.random.normal(_jax.random.fold_in(key, 0), shape, _jnp.float32)
    for i, name in enumerate(TWIN_WEIGHTS):
        w = inp[name].astype(_jnp.float32)
        if MOMENT_SCALE is None:
            s = _jnp.sqrt(_jnp.mean(_jnp.square(w)) + 1e-30)
        else:
            s = MOMENT_SCALE[name]
        km, kv = _jax.random.split(_jax.random.fold_in(key, i + 1))
        out[name] = w
        out["m_" + name] = s * _jax.random.normal(km, w.shape, _jnp.float32)
        out["v_" + name] = (s * s) * _jax.random.uniform(kv, w.shape, _jnp.float32, 0.5, 1.5)
    if N_MICROBATCH > 1:
        for name, axis in PER_EXAMPLE_BATCH_AXIS.items():
            out[name] = _to_microbatches(out[name], axis)
    return {'x': out['x'], 'w_in': out['w_in'], 'norm_a_g': out['norm_a_g'], 'norm_b_g': out['norm_b_g'], 'sinks_a': out['sinks_a'], 'w_o': out['w_o'], 'ln1_g': out['ln1_g'], 'ln1_b': out['ln1_b'], 'w_up': out['w_up'], 'conv_w': out['conv_w'], 'conv_b': out['conv_b'], 'w_down': out['w_down'], 'ln2_g': out['ln2_g'], 'ln2_b': out['ln2_b'], 'loss_target': out['loss_target'], 'm_w_in': out['m_w_in'], 'm_norm_a_g': out['m_norm_a_g'], 'm_norm_b_g': out['m_norm_b_g'], 'm_sinks_a': out['m_sinks_a'], 'm_w_o': out['m_w_o'], 'm_ln1_g': out['m_ln1_g'], 'm_ln1_b': out['m_ln1_b'], 'm_w_up': out['m_w_up'], 'm_conv_w': out['m_conv_w'], 'm_conv_b': out['m_conv_b'], 'm_w_down': out['m_w_down'], 'm_ln2_g': out['m_ln2_g'], 'm_ln2_b': out['m_ln2_b'], 'v_w_in': out['v_w_in'], 'v_norm_a_g': out['v_norm_a_g'], 'v_norm_b_g': out['v_norm_b_g'], 'v_sinks_a': out['v_sinks_a'], 'v_w_o': out['v_w_o'], 'v_ln1_g': out['v_ln1_g'], 'v_ln1_b': out['v_ln1_b'], 'v_w_up': out['v_w_up'], 'v_conv_w': out['v_conv_w'], 'v_conv_b': out['v_conv_b'], 'v_w_down': out['v_w_down'], 'v_ln2_g': out['v_ln2_g'], 'v_ln2_b': out['v_ln2_b']}


def _loss(weights, diff, rest, loss_target):
    with _jax.named_scope("forward"):
        args = {**rest, TWIN_DIFF_INPUT: diff, **{k: w.astype(_WEIGHT_DTYPES[k]) for k, w in weights.items()}}
        y = _forward(args)
    with _jax.named_scope("loss_head"):
        err = _jnp.square(y.astype(_jnp.float32) - loss_target)
        return 0.5 * _jnp.sum(_jnp.mean(err, axis=-1)) if err.ndim else 0.5 * err


def _adamw(w, g, m, v):
    m = ADAM_B1 * m + (1.0 - ADAM_B1) * g
    v = ADAM_B2 * v + (1.0 - ADAM_B2) * _jnp.square(g)
    m_hat = m / (1.0 - ADAM_B1 ** ADAM_STEP)
    v_hat = v / (1.0 - ADAM_B2 ** ADAM_STEP)
    delta = -ADAM_LR * (m_hat / (_jnp.sqrt(v_hat) + ADAM_EPS) + ADAM_WD * w)
    return delta, m, v


def reference(x, w_in, norm_a_g, norm_b_g, sinks_a, w_o, ln1_g, ln1_b, w_up, conv_w, conv_b, w_down, ln2_g, ln2_b, loss_target, m_w_in, m_norm_a_g, m_norm_b_g, m_sinks_a, m_w_o, m_ln1_g, m_ln1_b, m_w_up, m_conv_w, m_conv_b, m_w_down, m_ln2_g, m_ln2_b, v_w_in, v_norm_a_g, v_norm_b_g, v_sinks_a, v_w_o, v_ln1_g, v_ln1_b, v_w_up, v_conv_w, v_conv_b, v_w_down, v_ln2_g, v_ln2_b):
    given = dict(x=x, w_in=w_in, norm_a_g=norm_a_g, norm_b_g=norm_b_g, sinks_a=sinks_a, w_o=w_o, ln1_g=ln1_g, ln1_b=ln1_b, w_up=w_up, conv_w=conv_w, conv_b=conv_b, w_down=w_down, ln2_g=ln2_g, ln2_b=ln2_b, loss_target=loss_target, m_w_in=m_w_in, m_norm_a_g=m_norm_a_g, m_norm_b_g=m_norm_b_g, m_sinks_a=m_sinks_a, m_w_o=m_w_o, m_ln1_g=m_ln1_g, m_ln1_b=m_ln1_b, m_w_up=m_w_up, m_conv_w=m_conv_w, m_conv_b=m_conv_b, m_w_down=m_w_down, m_ln2_g=m_ln2_g, m_ln2_b=m_ln2_b, v_w_in=v_w_in, v_norm_a_g=v_norm_a_g, v_norm_b_g=v_norm_b_g, v_sinks_a=v_sinks_a, v_w_o=v_w_o, v_ln1_g=v_ln1_g, v_ln1_b=v_ln1_b, v_w_up=v_w_up, v_conv_w=v_conv_w, v_conv_b=v_conv_b, v_w_down=v_w_down, v_ln2_g=v_ln2_g, v_ln2_b=v_ln2_b)
    weights = {n: given[n] for n in TWIN_WEIGHTS}
    shared = {n: given[n] for n in SHARED_INPUTS}
    per_example = {n: given[n] for n in ['x']}
    grad_fn = _jax.value_and_grad(_loss, argnums=(0, 1))

    def one_microbatch(ex, loss_target):
        ex = dict(ex)
        diff = ex.pop(TWIN_DIFF_INPUT)
        return grad_fn(weights, diff, {**shared, **ex}, loss_target)

    if N_MICROBATCH == 1:
        loss, (grad_w, grad_x) = one_microbatch(per_example, given["loss_target"])
    else:
        def body(carry, xs):
            loss_sum, grad_sum = carry
            l_k, (gw_k, gx_k) = one_microbatch(xs[0], xs[1])
            with _jax.named_scope("update"):
                return (loss_sum + l_k, _jax.tree.map(_jnp.add, grad_sum, gw_k)), gx_k

        init = (_jnp.zeros((), _jnp.float32), _jax.tree.map(_jnp.zeros_like, weights))
        (loss, grad_w), grad_x = _jax.lax.scan(body, init, (per_example, given["loss_target"]))
    with _jax.named_scope("update"):
        delta_w, new_m, new_v = {}, {}, {}
        for n in TWIN_WEIGHTS:
            delta_w[n], new_m[n], new_v[n] = _adamw(weights[n], grad_w[n], given["m_" + n], given["v_" + n])
    return (loss, grad_x, *[grad_w[n] for n in TWIN_WEIGHTS], *[delta_w[n] for n in TWIN_WEIGHTS],
            *[new_m[n] for n in TWIN_WEIGHTS], *[new_v[n] for n in TWIN_WEIGHTS])
```

```python
import functools
import math

import jax
import jax.numpy as jnp
from jax import lax
from jax.experimental import pallas as pl
from jax.experimental.pallas import tpu as pltpu

F32, BF16, I32 = jnp.float32, jnp.bfloat16, jnp.int32

D = 1024
FF = 2816
HD = 64
NH = 8
WA, WB = 768, 1536
WIN = WA + WB
BLK = 128
ALPHA = 2.0 ** 0.25
LN_EPS, RMS_EPS = 1e-5, 1e-6
SCALE = 1.0 / math.sqrt(HD)
A_MAX_DIST, B_MAX_DIST = 127, 128
B_DILATIONS = (1, 4, 16)
SLOPES = tuple(2.0 ** (-(i + 1)) for i in range(NH))
SHARD_ROWS = (WIN // 4, D // 4, 2 * FF // 4, FF // 4)
N_CHIPS = 4
ADAM_LR, ADAM_B1, ADAM_B2, ADAM_EPS, ADAM_WD, ADAM_STEP = 0.001, 0.9, 0.999, 1e-08, 0.01, 10
MESH = pl.DeviceIdType.MESH
ANY = pl.BlockSpec(memory_space=pl.ANY)
SMEM = pl.BlockSpec(memory_space=pltpu.SMEM)


def _cp(sem, mb=48):
    return pltpu.CompilerParams(dimension_semantics=sem, vmem_limit_bytes=mb << 20)


def _nn(a, b):
    return lax.dot_general(a, b, (((1,), (0,)), ((), ())), preferred_element_type=F32)


def _nt(a, b):
    return lax.dot_general(a, b, (((1,), (1,)), ((), ())), preferred_element_type=F32)


def _tn(a, b):
    return lax.dot_general(a, b, (((0,), (0,)), ((), ())), preferred_element_type=F32)


def _resident(shape):
    n = len(shape)
    return pl.BlockSpec(shape, lambda *_: (0,) * n, pipeline_mode=pl.Buffered(1))


def _const(shape):
    n = len(shape)
    return pl.BlockSpec(shape, lambda *_: (0,) * n)


def _proj(x, w_t, name, tm=512, tn=256):
    s = x.shape[0]
    n = w_t.shape[0]

    def body(x_ref, w_ref, o_ref):
        o_ref[...] = _nt(x_ref[...].astype(BF16), w_ref[...])

    return pl.pallas_call(
        body, name=name, grid=(s // tm, n // tn),
        in_specs=[pl.BlockSpec((tm, D), lambda i, j: (i, 0)), pl.BlockSpec((tn, D), lambda i, j: (j, 0))],
        out_specs=pl.BlockSpec((tm, tn), lambda i, j: (i, j)),
        out_shape=jax.ShapeDtypeStruct((s, n), F32),
        compiler_params=_cp(("parallel", "parallel")),
    )(x, w_t)


def _grad_w(lhs, rhs, name, tm, tk=512, lhs_halves=False):
    s = rhs.shape[0]
    if lhs_halves:
        per_half = lhs.shape[2] // tm
        n = 2 * lhs.shape[2]
        lhs_spec = pl.BlockSpec((None, tk, tm), lambda i, k: (i // per_half, k, i % per_half))
    else:
        n = lhs.shape[1]
        lhs_spec = pl.BlockSpec((tk, tm), lambda i, k: (k, i))
    nk = s // tk

    def body(l_ref, r_ref, o_ref, ob_ref):
        k = pl.program_id(1)

        @pl.when(k == 0)
        def _():
            o_ref[...] = jnp.zeros_like(o_ref)

        o_ref[...] += _tn(l_ref[...].astype(BF16), r_ref[...].astype(BF16))

        @pl.when(k == nk - 1)
        def _():
            ob_ref[...] = o_ref[...].astype(BF16)

    return pl.pallas_call(
        body, name=name, grid=(n // tm, nk),
        in_specs=[lhs_spec, pl.BlockSpec((tk, D), lambda i, k: (k, 0))],
        out_specs=[pl.BlockSpec((tm, D), lambda i, k: (i, 0))] * 2,
        out_shape=[jax.ShapeDtypeStruct((n, D), F32), jax.ShapeDtypeStruct((n, D), BF16)],
        compiler_params=_cp(("parallel", "arbitrary")),
    )(lhs, rhs)


def _band_masks(first_block):
    row = lax.broadcasted_iota(I32, (BLK, 2 * BLK), 0)
    col = lax.broadcasted_iota(I32, (BLK, 2 * BLK), 1)
    dist = BLK + row - col
    return dist, col, first_block


def _valid(dist, col, not_first, max_dist):
    return (dist >= 0) & (dist <= max_dist) & ((col >= BLK) | not_first)


def _half_mask(shape, e):
    lane = lax.broadcasted_iota(I32, shape, 1)
    return (lane < HD) if e == 0 else (lane >= HD)


def _kv_variant(x, e, g):
    if g != e:
        x = pltpu.roll(x, HD, 1)
    return jnp.where(_half_mask(x.shape, e), x, 0.0).astype(BF16)


def _pair_fwd(q2, k2, v2, valid, distf, slopes, kv_heads, sinks):
    o2 = jnp.zeros((BLK, 2 * HD), F32)
    lse2 = jnp.zeros((BLK, 2 * HD), F32)
    for e in (0, 1):
        kk = _kv_variant(k2, e, kv_heads[e])
        vv = _kv_variant(v2, e, kv_heads[e])
        s = _nt(q2, kk) * SCALE - slopes[e] * distf
        s = jnp.where(valid, s, -jnp.inf)
        m = jnp.max(s, axis=1, keepdims=True)
        if sinks is not None:
            m = jnp.maximum(m, sinks[e])
        p = jnp.exp(s - m)
        l = jnp.sum(p, axis=1, keepdims=True)
        if sinks is not None:
            l = l + jnp.exp(sinks[e] - m)
        o2 = o2 + _nn(p.astype(BF16), vv) / l
        lse2 = jnp.where(_half_mask(lse2.shape, e), m + jnp.log(l), lse2)
    return o2, lse2


def _pair_bwd(q2, k2, v2, do2, o2, lse2, valid, distf, slopes, kv_heads, sinks):
    dq2 = jnp.zeros((BLK, 2 * HD), F32)
    dk2 = jnp.zeros((2 * BLK, 2 * HD), F32)
    dv2 = jnp.zeros((2 * BLK, 2 * HD), F32)
    dob = do2.astype(BF16)
    prod = do2 * o2
    dsinks = []
    for e in (0, 1):
        hq = _half_mask((BLK, 2 * HD), e)
        hk = _half_mask((2 * BLK, 2 * HD), e)
        lse = jnp.max(jnp.where(hq, lse2, -jnp.inf), axis=1, keepdims=True)
        delta = jnp.sum(jnp.where(hq, prod, 0.0), axis=1, keepdims=True)
        kk = _kv_variant(k2, e, kv_heads[e])
        vv = _kv_variant(v2, e, kv_heads[e])
        s = _nt(q2, kk) * SCALE - slopes[e] * distf
        s = jnp.where(valid, s, -jnp.inf)
        p = jnp.exp(s - lse)
        dp = _nt(dob, vv)
        ds = (p * (dp - delta)).astype(BF16)
        dq2 = dq2 + _nn(ds, kk) * SCALE
        dkc = jnp.where(hk, _tn(ds, q2) * SCALE, 0.0)
        dvc = jnp.where(hk, _tn(p.astype(BF16), dob), 0.0)
        if kv_heads[e] != e:
            dkc = pltpu.roll(dkc, HD, 1)
            dvc = pltpu.roll(dvc, HD, 1)
        dk2 = dk2 + dkc
        dv2 = dv2 + dvc
        if sinks is not None:
            dsinks.append(jnp.sum(-jnp.exp(sinks[e] - lse) * delta, axis=0, keepdims=True))
    return dq2, dk2, dv2, dsinks


def _attn_a_fwd(proj, sinks):
    s = proj.shape[0]
    nb = s // BLK

    def body(sink_ref, q_ref, kp_ref, kc_ref, vp_ref, vc_ref, o_ref, lse_ref):
        n = pl.program_id(0)
        dist, col, _ = _band_masks(None)
        valid = _valid(dist, col, n > 0, A_MAX_DIST)
        distf = dist.astype(F32)
        k2 = jnp.concatenate([kp_ref[...], kc_ref[...]], axis=0)
        v2 = jnp.concatenate([vp_ref[...], vc_ref[...]], axis=0)
        for j in range(NH // 2):
            g = j // 2
            q2 = q_ref[:, 128 * j:128 * (j + 1)].astype(BF16)
            o2, lse2 = _pair_fwd(q2, k2, v2, valid, distf, (SLOPES[2 * j], SLOPES[2 * j + 1]), (g, g),
                                 (sink_ref[2 * j], sink_ref[2 * j + 1]))
            o_ref[:, 128 * j:128 * (j + 1)] = o2
            lse_ref[:, 128 * j:128 * (j + 1)] = lse2

    prev = lambda n: jnp.maximum(n - 1, 0)
    return pl.pallas_call(
        body, name="attn_a_fwd", grid=(nb,),
        in_specs=[SMEM,
                  pl.BlockSpec((BLK, 512), lambda n: (n, 0)),
                  pl.BlockSpec((BLK, 128), lambda n: (prev(n), 4)), pl.BlockSpec((BLK, 128), lambda n: (n, 4)),
                  pl.BlockSpec((BLK, 128), lambda n: (prev(n), 5)), pl.BlockSpec((BLK, 128), lambda n: (n, 5))],
        out_specs=[pl.BlockSpec((BLK, 512), lambda n: (n, 0))] * 2,
        out_shape=[jax.ShapeDtypeStruct((s, 512), F32)] * 2,
        compiler_params=_cp(("parallel",)),
    )(sinks, proj, proj, proj, proj, proj)


def _attn_a_bwd(proj, sinks, d_o, o, lse):
    s = proj.shape[0]
    nb = s // BLK

    def body(sink_ref, q_ref, kp_ref, kc_ref, vp_ref, vc_ref, do_ref, o_ref, lse_ref,
             dq_ref, dk_ref, dv_ref, dsink_ref, kcar, vcar):
        n = pl.program_id(0)

        @pl.when(n == 0)
        def _():
            kcar[...] = jnp.zeros_like(kcar)
            vcar[...] = jnp.zeros_like(vcar)
            dsink_ref[...] = jnp.zeros_like(dsink_ref)

        @pl.when(n < nb)
        def _():
            dist, col, _ = _band_masks(None)
            valid = _valid(dist, col, n > 0, A_MAX_DIST)
            distf = dist.astype(F32)
            k2 = jnp.concatenate([kp_ref[...], kc_ref[...]], axis=0)
            v2 = jnp.concatenate([vp_ref[...], vc_ref[...]], axis=0)
            dk_win = jnp.zeros((2 * BLK, 128), F32)
            dv_win = jnp.zeros((2 * BLK, 128), F32)
            for j in range(NH // 2):
                g = j // 2
                sl = slice(128 * j, 128 * (j + 1))
                q2 = q_ref[:, sl].astype(BF16)
                dq2, dk2, dv2, dsk = _pair_bwd(q2, k2, v2, do_ref[:, sl], o_ref[:, sl], lse_ref[:, sl], valid, distf,
                                               (SLOPES[2 * j], SLOPES[2 * j + 1]), (g, g),
                                               (sink_ref[2 * j], sink_ref[2 * j + 1]))
                dq_ref[:, sl] = dq2
                dk_win = dk_win + dk2
                dv_win = dv_win + dv2
                for e in (0, 1):
                    h = 2 * j + e
                    dsink_ref[h:h + 1, :] += jnp.broadcast_to(dsk[e], (1, 128))
            dk_ref[...] = kcar[...] + dk_win[:BLK]
            dv_ref[...] = vcar[...] + dv_win[:BLK]
            kcar[...] = dk_win[BLK:]
            vcar[...] = dv_win[BLK:]

        @pl.when(n == nb)
        def _():
            dk_ref[...] = kcar[...]
            dv_ref[...] = vcar[...]

    cur = lambda n: jnp.minimum(n, nb - 1)
    prev = lambda n: jnp.maximum(cur(n) - 1, 0)
    out_prev = lambda n: jnp.maximum(n - 1, 0)
    return pl.pallas_call(
        body, name="attn_a_bwd", grid=(nb + 1,),
        in_specs=[SMEM,
                  pl.BlockSpec((BLK, 512), lambda n: (cur(n), 0)),
                  pl.BlockSpec((BLK, 128), lambda n: (prev(n), 4)), pl.BlockSpec((BLK, 128), lambda n: (cur(n), 4)),
                  pl.BlockSpec((BLK, 128), lambda n: (prev(n), 5)), pl.BlockSpec((BLK, 128), lambda n: (cur(n), 5)),
                  pl.BlockSpec((BLK, 512), lambda n: (cur(n), 0)),
                  pl.BlockSpec((BLK, 512), lambda n: (cur(n), 0)),
                  pl.BlockSpec((BLK, 512), lambda n: (cur(n), 0))],
        out_specs=[pl.BlockSpec((BLK, 512), lambda n: (cur(n), 0)),
                   pl.BlockSpec((BLK, 128), lambda n: (out_prev(n), 0)),
                   pl.BlockSpec((BLK, 128), lambda n: (out_prev(n), 0)),
                   pl.BlockSpec((NH, 128), lambda n: (0, 0))],
        out_shape=[jax.ShapeDtypeStruct((s, 512), F32), jax.ShapeDtypeStruct((s, 128), F32),
                   jax.ShapeDtypeStruct((s, 128), F32), jax.ShapeDtypeStruct((NH, 128), F32)],
        scratch_shapes=[pltpu.VMEM((BLK, 128), F32), pltpu.VMEM((BLK, 128), F32)],
        compiler_params=_cp(("arbitrary",)),
    )(sinks, proj, proj, proj, proj, proj, d_o, o, lse)


def _strided(rho, r):
    return pl.ds(rho, BLK, stride=r) if r > 1 else pl.ds(0, BLK)


def _attn_b_fwd(proj, slopes, r):
    s = proj.shape[0]
    rows = BLK * r
    nsb = s // rows
    qc, kc, vc = WA // 128, WA // 128 + 4, WA // 128 + 8

    def body(slope_ref, q_ref, kp_ref, kc_ref, vp_ref, vc_ref, o_ref, lse_ref):
        j = pl.program_id(0)
        sb = pl.program_id(1)
        dist, col, _ = _band_masks(None)
        valid = _valid(dist, col, sb > 0, B_MAX_DIST)
        distf = dist.astype(F32) * float(r)
        sl2 = (slope_ref[2 * j], slope_ref[2 * j + 1])

        def per_rho(rho, carry):
            sl = _strided(rho, r)
            q2 = q_ref[sl, :].astype(BF16)
            k2 = jnp.concatenate([kp_ref[sl, :], kc_ref[sl, :]], axis=0)
            v2 = jnp.concatenate([vp_ref[sl, :], vc_ref[sl, :]], axis=0)
            o2, lse2 = _pair_fwd(q2, k2, v2, valid, distf, sl2, (0, 1), None)
            o_ref[sl, :] = o2
            lse_ref[sl, :] = lse2
            return carry

        if r > 1:
            lax.fori_loop(0, r, per_rho, 0)
        else:
            per_rho(0, 0)

    prev = lambda sb: jnp.maximum(sb - 1, 0)
    return pl.pallas_call(
        body, name=f"attn_b_fwd_r{r}", grid=(NH // 2, nsb),
        in_specs=[SMEM,
                  pl.BlockSpec((rows, 128), lambda j, sb: (sb, qc + j)),
                  pl.BlockSpec((rows, 128), lambda j, sb: (prev(sb), kc + j)),
                  pl.BlockSpec((rows, 128), lambda j, sb: (sb, kc + j)),
                  pl.BlockSpec((rows, 128), lambda j, sb: (prev(sb), vc + j)),
                  pl.BlockSpec((rows, 128), lambda j, sb: (sb, vc + j))],
        out_specs=[pl.BlockSpec((rows, 128), lambda j, sb: (sb, j))] * 2,
        out_shape=[jax.ShapeDtypeStruct((s, 512), F32)] * 2,
        compiler_params=_cp(("parallel", "parallel")),
    )(slopes, proj, proj, proj, proj, proj)


def _attn_b_bwd(proj, slopes, d_o, o, lse, r):
    s = proj.shape[0]
    rows = BLK * r
    nsb = s // rows
    qc, kc, vc = WA // 128, WA // 128 + 4, WA // 128 + 8

    def body(slope_ref, q_ref, kp_ref, kc_ref, vp_ref, vc_ref, do_ref, o_ref, lse_ref,
             dq_ref, dk_ref, dv_ref, kcar, vcar):
        j = pl.program_id(0)
        sb = pl.program_id(1)

        @pl.when(sb == 0)
        def _():
            kcar[...] = jnp.zeros_like(kcar)
            vcar[...] = jnp.zeros_like(vcar)

        @pl.when(sb < nsb)
        def _():
            dist, col, _ = _band_masks(None)
            valid = _valid(dist, col, sb > 0, B_MAX_DIST)
            distf = dist.astype(F32) * float(r)
            sl2 = (slope_ref[2 * j], slope_ref[2 * j + 1])

            def per_rho(rho, carry):
                sl = _strided(rho, r)
                q2 = q_ref[sl, :].astype(BF16)
                k2 = jnp.concatenate([kp_ref[sl, :], kc_ref[sl, :]], axis=0)
                v2 = jnp.concatenate([vp_ref[sl, :], vc_ref[sl, :]], axis=0)
                dq2, dk2, dv2, _ = _pair_bwd(q2, k2, v2, do_ref[sl, :], o_ref[sl, :], lse_ref[sl, :], valid, distf,
                                             sl2, (0, 1), None)
                dq_ref[sl, :] = dq2
                dk_ref[sl, :] = kcar[sl, :] + dk2[:BLK]
                dv_ref[sl, :] = vcar[sl, :] + dv2[:BLK]
                kcar[sl, :] = dk2[BLK:]
                vcar[sl, :] = dv2[BLK:]
                return carry

            if r > 1:
                lax.fori_loop(0, r, per_rho, 0)
            else:
                per_rho(0, 0)

        @pl.when(sb == nsb)
        def _():
            dk_ref[...] = kcar[...]
            dv_ref[...] = vcar[...]

    cur = lambda sb: jnp.minimum(sb, nsb - 1)
    prev = lambda sb: jnp.maximum(cur(sb) - 1, 0)
    out_prev = lambda sb: jnp.maximum(sb - 1, 0)
    return pl.pallas_call(
        body, name=f"attn_b_bwd_r{r}", grid=(NH // 2, nsb + 1),
        in_specs=[SMEM,
                  pl.BlockSpec((rows, 128), lambda j, sb: (cur(sb), qc + j)),
                  pl.BlockSpec((rows, 128), lambda j, sb: (prev(sb), kc + j)),
                  pl.BlockSpec((rows, 128), lambda j, sb: (cur(sb), kc + j)),
                  pl.BlockSpec((rows, 128), lambda j, sb: (prev(sb), vc + j)),
                  pl.BlockSpec((rows, 128), lambda j, sb: (cur(sb), vc + j)),
                  pl.BlockSpec((rows, 128), lambda j, sb: (cur(sb), j)),
                  pl.BlockSpec((rows, 128), lambda j, sb: (cur(sb), j)),
                  pl.BlockSpec((rows, 128), lambda j, sb: (cur(sb), j))],
        out_specs=[pl.BlockSpec((rows, 128), lambda j, sb: (cur(sb), j)),
                   pl.BlockSpec((rows, 128), lambda j, sb: (out_prev(sb), j)),
                   pl.BlockSpec((rows, 128), lambda j, sb: (out_prev(sb), j))],
        out_shape=[jax.ShapeDtypeStruct((s, 512), F32)] * 3,
        scratch_shapes=[pltpu.VMEM((rows, 128), F32), pltpu.VMEM((rows, 128), F32)],
        compiler_params=_cp(("parallel", "arbitrary")),
    )(slopes, proj, proj, proj, proj, proj, d_o, o, lse)


def _row(v):
    return v.reshape(1, -1)


def _layer_norm_stats(z):
    mu = jnp.mean(z, axis=-1, keepdims=True)
    zc = z - mu
    var = jnp.mean(zc * zc, axis=-1, keepdims=True)
    rstd = lax.rsqrt(var + LN_EPS)
    return zc * rstd, rstd


def _layer_norm_bwd(dh, zh, rstd, g):
    dzh = dh * g
    return rstd * (dzh - jnp.mean(dzh, axis=-1, keepdims=True) - zh * jnp.mean(dzh * zh, axis=-1, keepdims=True))


def _rms(o):
    return lax.rsqrt(jnp.mean(o * o, axis=-1, keepdims=True) + RMS_EPS)


def _mix_ln1(x, o_a, o_b, lse_b, norm_a_g, norm_b_g, w_o, ln1_g, ln1_b, tm=256):
    s = x.shape[0]

    def body(x_ref, oa_ref, ob1, ob2, ob3, l1, l2, l3, ga_ref, gb_ref, wo_ref, g_ref, b_ref,
             obm_ref, lse_ref, cat_ref, z1_ref, h1_ref, h1b_ref):
        la, lb, lc = l1[...], l2[...], l3[...]
        m = jnp.maximum(jnp.maximum(la, lb), lc)
        ea, eb, ec = jnp.exp(la - m), jnp.exp(lb - m), jnp.exp(lc - m)
        den = ea + eb + ec
        obm = (ea / den) * ob1[...] + (eb / den) * ob2[...] + (ec / den) * ob3[...]
        obm_ref[...] = obm
        lse_ref[...] = m + jnp.log(den)
        oa = oa_ref[...]
        na = oa * _rms(oa) * ga_ref[...]
        nb_ = obm * _rms(obm) * gb_ref[...]
        cat = jnp.concatenate([na, nb_], axis=1).astype(BF16)
        cat_ref[...] = cat
        z1 = ALPHA * x_ref[...] + _nn(cat, wo_ref[...])
        z1_ref[...] = z1
        zh, _ = _layer_norm_stats(z1)
        h1 = zh * g_ref[...] + b_ref[...]
        h1_ref[...] = h1
        h1b_ref[...] = h1.astype(BF16)

    t512 = pl.BlockSpec((tm, 512), lambda i: (i, 0))
    td = pl.BlockSpec((tm, D), lambda i: (i, 0))
    return pl.pallas_call(
        body, name="mix_ln1", grid=(s // tm,),
        in_specs=[td] + [t512] * 7 + [_const((1, 512))] * 2 + [_resident((D, D))] + [_const((1, D))] * 2,
        out_specs=[t512, t512, td, td, td, td],
        out_shape=[jax.ShapeDtypeStruct((s, 512), F32), jax.ShapeDtypeStruct((s, 512), F32),
                   jax.ShapeDtypeStruct((s, D), BF16), jax.ShapeDtypeStruct((s, D), F32),
                   jax.ShapeDtypeStruct((s, D), F32), jax.ShapeDtypeStruct((s, D), BF16)],
        compiler_params=_cp(("parallel",)),
    )(x, o_a, *o_b, *lse_b, _row(norm_a_g), _row(norm_b_g), w_o, _row(ln1_g), _row(ln1_b))


def _gelu(x):
    c = math.sqrt(2.0 / math.pi)
    return 0.5 * x * (1.0 + jnp.tanh(c * (x + 0.044715 * x * x * x)))


def _gelu_and_grad(x):
    c = math.sqrt(2.0 / math.pi)
    t = jnp.tanh(c * (x + 0.044715 * x * x * x))
    g = 0.5 * x * (1.0 + t)
    dg = 0.5 * (1.0 + t) + 0.5 * x * (1.0 - t * t) * c * (1.0 + 3.0 * 0.044715 * x * x)
    return g, dg


def _shift_down(u, before, tm):
    row = lax.broadcasted_iota(I32, u.shape, 0)
    r1 = jnp.where(row == 0, before[7:8], pltpu.roll(u, 1, 0))
    r2 = jnp.where(row == 0, before[6:7], jnp.where(row == 1, before[7:8], pltpu.roll(u, 2, 0)))
    return r1, r2


def _shift_up(u, after, tm):
    row = lax.broadcasted_iota(I32, u.shape, 0)
    l1 = jnp.where(row == tm - 1, after[0:1], pltpu.roll(u, tm - 1, 0))
    l2 = jnp.where(row == tm - 1, after[1:2], jnp.where(row == tm - 2, after[0:1], pltpu.roll(u, tm - 2, 0)))
    return l1, l2


def _up_conv_gelu(h1b, w_up_t, cwb, tm=512, tn=256):
    s = h1b.shape[0]

    def body(h_ref, w_ref, c_ref, up_ref, a_ref, carry):
        i = pl.program_id(1)

        @pl.when(i == 0)
        def _():
            carry[...] = jnp.zeros_like(carry)

        h = h_ref[...]
        u = []
        for half in (0, 1):
            up = _nt(h, w_ref[half])
            up_ref[half] = up
            r1, r2 = _shift_down(up, carry[half], tm)
            u.append(r2 * c_ref[0, half:half + 1, :] + r1 * c_ref[1, half:half + 1, :]
                     + up * c_ref[2, half:half + 1, :] + c_ref[3, half:half + 1, :])
            carry[half] = up[tm - 8:tm]
        a_ref[...] = (_gelu(u[0]) * u[1]).astype(BF16)

    return pl.pallas_call(
        body, name="up_conv_gelu", grid=(FF // tn, s // tm),
        in_specs=[pl.BlockSpec((tm, D), lambda j, i: (i, 0)),
                  pl.BlockSpec((2, tn, D), lambda j, i: (0, j, 0)),
                  pl.BlockSpec((4, 2, tn), lambda j, i: (0, 0, j))],
        out_specs=[pl.BlockSpec((2, tm, tn), lambda j, i: (0, i, j)), pl.BlockSpec((tm, tn), lambda j, i: (i, j))],
        out_shape=[jax.ShapeDtypeStruct((2, s, FF), F32), jax.ShapeDtypeStruct((s, FF), BF16)],
        scratch_shapes=[pltpu.VMEM((2, 8, tn), F32)],
        compiler_params=_cp(("parallel", "arbitrary")),
    )(h1b, w_up_t, cwb)


def _down_ln2_loss(a, w_down, h1, target, ln2_g, ln2_b, tm=256):
    s = a.shape[0]

    def body(a_ref, w_ref, h_ref, t_ref, g_ref, b_ref, dz_ref, dzb_ref, st_ref):
        @pl.when(pl.program_id(0) == 0)
        def _():
            st_ref[...] = jnp.zeros_like(st_ref)

        z2 = ALPHA * h_ref[...] + _nn(a_ref[...], w_ref[...])
        zh, rstd = _layer_norm_stats(z2)
        diff = zh * g_ref[...] + b_ref[...] - t_ref[...]
        part = 0.5 * jnp.sum(jnp.mean(diff * diff, axis=-1, keepdims=True), axis=0, keepdims=True)
        dy = diff * (1.0 / D)
        st_ref[0:1, :] += jnp.sum(dy * zh, axis=0, keepdims=True)
        st_ref[1:2, :] += jnp.sum(dy, axis=0, keepdims=True)
        st_ref[2:3, :] += jnp.broadcast_to(part, (1, D))
        dz = _layer_norm_bwd(dy, zh, rstd, g_ref[...])
        dz_ref[...] = dz
        dzb_ref[...] = dz.astype(BF16)

    td = pl.BlockSpec((tm, D), lambda i: (i, 0))
    return pl.pallas_call(
        body, name="down_ln2_loss", grid=(s // tm,),
        in_specs=[pl.BlockSpec((tm, FF), lambda i: (i, 0)), _resident((FF, D)), td, td, _const((1, D)), _const((1, D))],
        out_specs=[td, td, _const((8, D))],
        out_shape=[jax.ShapeDtypeStruct((s, D), F32), jax.ShapeDtypeStruct((s, D), BF16),
                   jax.ShapeDtypeStruct((8, D), F32)],
        compiler_params=_cp(("arbitrary",)),
    )(a, w_down, h1, target, _row(ln2_g), _row(ln2_b))


def _conv_gelu_bwd(dz2b, w_down, up, cwb, tm=512, tn=256):
    s = dz2b.shape[0]
    n_i = s // tm

    def body(dz_ref, w_ref, up_ref, halo_ref, c_ref, dup_ref, dc_ref, carry):
        ii = pl.program_id(1)
        i = n_i - 1 - ii

        @pl.when(ii == 0)
        def _():
            carry[...] = jnp.zeros_like(carry)
            dc_ref[...] = jnp.zeros_like(dc_ref)

        da = _nt(dz_ref[...], w_ref[...])
        ups, r1s, r2s, us = [], [], [], []
        for half in (0, 1):
            up_h = up_ref[half]
            before = jnp.where(i > 0, halo_ref[half], 0.0)
            r1, r2 = _shift_down(up_h, before, tm)
            us.append(r2 * c_ref[0, half:half + 1, :] + r1 * c_ref[1, half:half + 1, :]
                      + up_h * c_ref[2, half:half + 1, :] + c_ref[3, half:half + 1, :])
            ups.append(up_h)
            r1s.append(r1)
            r2s.append(r2)
        g, dg = _gelu_and_grad(us[0])
        dus = (da * us[1] * dg, da * g)
        for half in (0, 1):
            du = dus[half]
            l1, l2 = _shift_up(du, carry[half], tm)
            dup = du * c_ref[2, half:half + 1, :] + l1 * c_ref[1, half:half + 1, :] + l2 * c_ref[0, half:half + 1, :]
            dup_ref[half] = dup.astype(BF16)
            dc_ref[0, half:half + 1, :] += jnp.sum(du * r2s[half], axis=0, keepdims=True)
            dc_ref[1, half:half + 1, :] += jnp.sum(du * r1s[half], axis=0, keepdims=True)
            dc_ref[2, half:half + 1, :] += jnp.sum(du * ups[half], axis=0, keepdims=True)
            dc_ref[3, half:half + 1, :] += jnp.sum(du, axis=0, keepdims=True)
            carry[half] = du[0:8]

    rev = lambda ii: n_i - 1 - ii
    return pl.pallas_call(
        body, name="conv_gelu_bwd", grid=(FF // tn, n_i),
        in_specs=[pl.BlockSpec((tm, D), lambda j, ii: (rev(ii), 0)),
                  pl.BlockSpec((tn, D), lambda j, ii: (j, 0)),
                  pl.BlockSpec((2, tm, tn), lambda j, ii: (0, rev(ii), j)),
                  pl.BlockSpec((2, 8, tn), lambda j, ii: (0, jnp.maximum(rev(ii) * (tm // 8) - 1, 0), j)),
                  pl.BlockSpec((4, 2, tn), lambda j, ii: (0, 0, j))],
        out_specs=[pl.BlockSpec((2, tm, tn), lambda j, ii: (0, rev(ii), j)),
                   pl.BlockSpec((4, 2, tn), lambda j, ii: (0, 0, j))],
        out_shape=[jax.ShapeDtypeStruct((2, s, FF), BF16), jax.ShapeDtypeStruct((4, 2, FF), F32)],
        scratch_shapes=[pltpu.VMEM((2, 8, tn), F32)],
        compiler_params=_cp(("parallel", "arbitrary")),
    )(dz2b, w_down, up, up, cwb)


def _dh1_ln1_bwd(dz2, dup, w_up_t, z1, ln1_g, tm=256):
    s = dz2.shape[0]

    def body(dz2_ref, dup_ref, w_ref, z1_ref, g_ref, dz1_ref, dz1b_ref, st_ref):
        @pl.when(pl.program_id(0) == 0)
        def _():
            st_ref[...] = jnp.zeros_like(st_ref)

        dh = ALPHA * dz2_ref[...] + _nn(dup_ref[0], w_ref[0]) + _nn(dup_ref[1], w_ref[1])
        zh, rstd = _layer_norm_stats(z1_ref[...])
        st_ref[0:1, :] += jnp.sum(dh * zh, axis=0, keepdims=True)
        st_ref[1:2, :] += jnp.sum(dh, axis=0, keepdims=True)
        dz = _layer_norm_bwd(dh, zh, rstd, g_ref[...])
        dz1_ref[...] = dz
        dz1b_ref[...] = dz.astype(BF16)

    td = pl.BlockSpec((tm, D), lambda i: (i, 0))
    return pl.pallas_call(
        body, name="dh1_ln1_bwd", grid=(s // tm,),
        in_specs=[td, pl.BlockSpec((2, tm, FF), lambda i: (0, i, 0)), _resident((2, FF, D)), td, _const((1, D))],
        out_specs=[td, td, _const((8, D))],
        out_shape=[jax.ShapeDtypeStruct((s, D), F32), jax.ShapeDtypeStruct((s, D), BF16),
                   jax.ShapeDtypeStruct((8, D), F32)],
        compiler_params=_cp(("arbitrary",)),
    )(dz2, dup, w_up_t, z1, _row(ln1_g))


def _dcat_rms_bwd(dz1b, w_o, o_a, o_b, norm_a_g, norm_b_g, tm=256):
    s = dz1b.shape[0]

    def body(dz_ref, w_ref, oa_ref, ob_ref, ga_ref, gb_ref, da_ref, db_ref, st_ref):
        @pl.when(pl.program_id(0) == 0)
        def _():
            st_ref[...] = jnp.zeros_like(st_ref)

        dcat = _nt(dz_ref[...], w_ref[...])
        for k, (o_ref, g_ref, d_ref) in enumerate(((oa_ref, ga_ref, da_ref), (ob_ref, gb_ref, db_ref))):
            o = o_ref[...]
            dn = dcat[:, 512 * k:512 * (k + 1)]
            rr = _rms(o)
            oh = o * rr
            st_ref[k:k + 1, :] += jnp.sum(dn * oh, axis=0, keepdims=True)
            doh = dn * g_ref[...]
            d_ref[...] = rr * (doh - oh * jnp.mean(doh * oh, axis=-1, keepdims=True))

    t512 = pl.BlockSpec((tm, 512), lambda i: (i, 0))
    return pl.pallas_call(
        body, name="dcat_rms_bwd", grid=(s // tm,),
        in_specs=[pl.BlockSpec((tm, D), lambda i: (i, 0)), _resident((D, D)), t512, t512,
                  _const((1, 512)), _const((1, 512))],
        out_specs=[t512, t512, _const((8, 512))],
        out_shape=[jax.ShapeDtypeStruct((s, 512), F32), jax.ShapeDtypeStruct((s, 512), F32),
                   jax.ShapeDtypeStruct((8, 512), F32)],
        compiler_params=_cp(("arbitrary",)),
    )(dz1b, w_o, o_a, o_b, _row(norm_a_g), _row(norm_b_g))


def _dproj_combine(dqa, dka, dva, dqkv_b, tm=256):
    s = dqa.shape[0]

    def body(qa, ka, va, q1, k1, v1, q2, k2, v2, q3, k3, v3, o_ref):
        o_ref[:, 0:512] = qa[...].astype(BF16)
        o_ref[:, 512:640] = ka[...].astype(BF16)
        o_ref[:, 640:768] = va[...].astype(BF16)
        o_ref[:, 768:1280] = (q1[...] + q2[...] + q3[...]).astype(BF16)
        o_ref[:, 1280:1792] = (k1[...] + k2[...] + k3[...]).astype(BF16)
        o_ref[:, 1792:2304] = (v1[...] + v2[...] + v3[...]).astype(BF16)

    t512 = pl.BlockSpec((tm, 512), lambda i: (i, 0))
    t128 = pl.BlockSpec((tm, 128), lambda i: (i, 0))
    flat = [a for trio in dqkv_b for a in trio]
    return pl.pallas_call(
        body, name="dproj_combine", grid=(s // tm,),
        in_specs=[t512, t128, t128] + [t512] * 9,
        out_specs=pl.BlockSpec((tm, WIN), lambda i: (i, 0)),
        out_shape=jax.ShapeDtypeStruct((s, WIN), BF16),
        compiler_params=_cp(("parallel",)),
    )(dqa, dka, dva, *flat)


def _grad_x(dz1, dproj, w_in_t, tm=256):
    s = dz1.shape[0]

    def body(dz_ref, dp_ref, w_ref, o_ref):
        o_ref[...] = ALPHA * dz_ref[...] + _nn(dp_ref[...], w_ref[...])

    td = pl.BlockSpec((tm, D), lambda i: (i, 0))
    return pl.pallas_call(
        body, name="grad_x", grid=(s // tm,),
        in_specs=[td, pl.BlockSpec((tm, WIN), lambda i: (i, 0)), _resident((WIN, D))],
        out_specs=td, out_shape=jax.ShapeDtypeStruct((s, D), F32),
        compiler_params=_cp(("parallel",)),
    )(dz1, dproj, w_in_t)


def _place():
    return lax.axis_index("x"), lax.axis_index("y"), lax.axis_index("c")


def _other_chips(x, y):
    return [(1 - x, y), (x, 1 - y), (1 - x, 1 - y)]


def _gather_weights(shards, conv_w):
    n = len(shards)

    def body(*refs):
        src, conv_src, out, conv_out = refs[:n], refs[n], refs[n + 1:2 * n + 1], refs[2 * n + 1]
        send_sems, recv_sems, local_sems = refs[2 * n + 2:]
        x, y, c = _place()
        b = 2 * x + y
        sibling = (x, y, 1 - c)
        chips = _other_chips(x, y)

        def rows(k, chip_b, core):
            half = SHARD_ROWS[k] // 2
            return out[k].at[pl.ds(pl.multiple_of(chip_b * SHARD_ROWS[k] + core * half, 16), half)]

        local = []
        for k in range(n):
            cp = pltpu.make_async_copy(src[k], out[k].at[pl.ds(pl.multiple_of(b * SHARD_ROWS[k], 16), SHARD_ROWS[k])],
                                       local_sems.at[k])
            cp.start()
            local.append(cp)

        def copy(idx, k, chip_b, core, to, from_shard=False):
            half = SHARD_ROWS[k] // 2
            s_ref = src[k].at[pl.ds(pl.multiple_of(core * half, 16), half)] if from_shard else rows(k, chip_b, core)
            return pltpu.make_async_remote_copy(src_ref=s_ref, dst_ref=rows(k, chip_b, core),
                                                send_sem=send_sems.at[idx], recv_sem=recv_sems.at[idx],
                                                device_id=to, device_id_type=MESH)

        cp = pltpu.make_async_copy(conv_src, conv_out.at[b], local_sems.at[n])
        cp.start()
        local.append(cp)
        started = []
        for jn, chip in enumerate(chips):
            for k in range(n):
                cp = copy(jn * n + k, k, b, c, (chip[0], chip[1], c), from_shard=True)
                cp.start()
                started.append(cp)
            cp = pltpu.make_async_remote_copy(src_ref=conv_src, dst_ref=conv_out.at[b], send_sem=send_sems.at[6 * n + jn],
                                              recv_sem=recv_sems.at[6 * n + jn], device_id=(chip[0], chip[1], c),
                                              device_id_type=MESH)
            cp.start()
            started.append(cp)
        for jn, chip in enumerate(chips):
            cb = 2 * chip[0] + chip[1]
            for k in range(n):
                copy(jn * n + k, k, cb, c, (chip[0], chip[1], c)).wait_recv()
                cp = copy(3 * n + jn * n + k, k, cb, c, sibling)
                cp.start()
                started.append(cp)
        for jn, chip in enumerate(chips):
            cb = 2 * chip[0] + chip[1]
            for k in range(n):
                copy(3 * n + jn * n + k, k, cb, 1 - c, sibling).wait_recv()
        for jn, chip in enumerate(chips):
            cb = 2 * chip[0] + chip[1]
            pltpu.make_async_remote_copy(src_ref=conv_src, dst_ref=conv_out.at[cb], send_sem=send_sems.at[6 * n + jn],
                                         recv_sem=recv_sems.at[6 * n + jn], device_id=(chip[0], chip[1], c),
                                         device_id_type=MESH).wait_recv()
        for cp in started:
            cp.wait_send()
        for cp in local:
            cp.wait()

    return pl.pallas_call(
        body, name="gather_weights",
        in_specs=[ANY] * (n + 1), out_specs=[ANY] * (n + 1),
        out_shape=[jax.ShapeDtypeStruct((N_CHIPS * sh.shape[0], D), BF16) for sh in shards]
        + [jax.ShapeDtypeStruct((N_CHIPS,) + conv_w.shape, F32)],
        scratch_shapes=[pltpu.SemaphoreType.DMA((6 * n + 3,)), pltpu.SemaphoreType.DMA((6 * n + 3,)),
                        pltpu.SemaphoreType.DMA((n + 1,))],
        compiler_params=pltpu.CompilerParams(has_side_effects=True),
    )(*shards, conv_w)


def _sibling_swap(grads_b):
    n = len(grads_b)

    def body(*refs):
        src, out = refs[:n], refs[n:2 * n]
        send_sems, recv_sems = refs[2 * n:]
        x, y, c = _place()
        cps = []
        for k in range(n):
            cp = pltpu.make_async_remote_copy(src_ref=src[k].at[:, 1 - c], dst_ref=out[k],
                                              send_sem=send_sems.at[k], recv_sem=recv_sems.at[k],
                                              device_id=(x, y, 1 - c), device_id_type=MESH)
            cp.start()
            cps.append(cp)
        for cp in cps:
            cp.wait()

    return pl.pallas_call(
        body, name="grad_sibling_swap",
        in_specs=[ANY] * n, out_specs=[ANY] * n,
        out_shape=[jax.ShapeDtypeStruct((N_CHIPS, g.shape[2], D), BF16) for g in grads_b],
        scratch_shapes=[pltpu.SemaphoreType.DMA((n,)), pltpu.SemaphoreType.DMA((n,))],
        compiler_params=pltpu.CompilerParams(has_side_effects=True),
    )(*grads_b)


def _chip_partial(grad4, got, c, name, tr):
    h = grad4.shape[2]

    def body(c_ref, g_ref, r_ref, pb_ref):
        pb_ref[...] = (g_ref[...] + r_ref[...].astype(F32)).astype(BF16)

    return pl.pallas_call(
        body, name=name,
        grid_spec=pltpu.PrefetchScalarGridSpec(
            num_scalar_prefetch=1, grid=(N_CHIPS, h // tr),
            in_specs=[pl.BlockSpec((None, None, tr, D), lambda cb, i, c_ref: (cb, c_ref[0], i, 0)),
                      pl.BlockSpec((None, tr, D), lambda cb, i, c_ref: (cb, i, 0))],
            out_specs=pl.BlockSpec((None, tr, D), lambda cb, i, c_ref: (cb, i, 0))),
        out_shape=jax.ShapeDtypeStruct((N_CHIPS, h, D), BF16),
        compiler_params=_cp(("arbitrary", "arbitrary")),
    )(c, grad4, got)


def _chip_exchange(partials):
    n = len(partials)

    def body(*refs):
        src, out = refs[:n], refs[n:2 * n]
        send_sems, recv_sems = refs[2 * n:]
        x, y, c = _place()
        cps = []
        for jn, chip in enumerate(_other_chips(x, y)):
            cb = 2 * chip[0] + chip[1]
            for k in range(n):
                cp = pltpu.make_async_remote_copy(src_ref=src[k].at[cb], dst_ref=out[k].at[jn],
                                                  send_sem=send_sems.at[jn * n + k], recv_sem=recv_sems.at[jn * n + k],
                                                  device_id=(chip[0], chip[1], c), device_id_type=MESH)
                cp.start()
                cps.append(cp)
        for cp in cps:
            cp.wait()

    return pl.pallas_call(
        body, name="grad_chip_exchange",
        in_specs=[ANY] * n, out_specs=[ANY] * n,
        out_shape=[jax.ShapeDtypeStruct((3, p.shape[1], D), BF16) for p in partials],
        scratch_shapes=[pltpu.SemaphoreType.DMA((3 * n,)), pltpu.SemaphoreType.DMA((3 * n,))],
        compiler_params=pltpu.CompilerParams(has_side_effects=True),
    )(*partials)


def _sum_partials(grad4, got, got_chips, cb, name, tr):
    h = grad4.shape[2]

    def body(cb_ref, g_ref, s_ref, o_ref, out_ref):
        acc = g_ref[...] + s_ref[...].astype(F32)
        for j in range(3):
            acc = acc + o_ref[j].astype(F32)
        out_ref[...] = acc

    return pl.pallas_call(
        body, name=name,
        grid_spec=pltpu.PrefetchScalarGridSpec(
            num_scalar_prefetch=1, grid=(h // tr,),
            in_specs=[pl.BlockSpec((None, None, tr, D), lambda i, cb_ref: (cb_ref[1], cb_ref[0], i, 0)),
                      pl.BlockSpec((None, tr, D), lambda i, cb_ref: (cb_ref[1], i, 0)),
                      pl.BlockSpec((3, tr, D), lambda i, cb_ref: (0, i, 0))],
            out_specs=pl.BlockSpec((tr, D), lambda i, cb_ref: (i, 0))),
        out_shape=jax.ShapeDtypeStruct((h, D), F32),
        compiler_params=_cp(("arbitrary",)),
    )(cb, grad4, got, got_chips)


def _share_halves(halves, small):
    n = len(halves)
    rows = small.shape[0]

    def body(*refs):
        src, small_ref = refs[:n], refs[n]
        out, total_ref = refs[n + 1:2 * n + 1], refs[2 * n + 1]
        all_ref, send_sems, recv_sems, local_sems, ssend, srecv = refs[2 * n + 2:]
        x, y, c = _place()
        me = 4 * x + 2 * y + c
        cps = []
        for k in range(n):
            h = halves[k].shape[0]
            dst = out[k].at[pl.ds(pl.multiple_of(c * h, 8), h)]
            lc = pltpu.make_async_copy(src[k], dst, local_sems.at[k])
            lc.start()
            cp = pltpu.make_async_remote_copy(src_ref=src[k], dst_ref=dst, send_sem=send_sems.at[k],
                                              recv_sem=recv_sems.at[k], device_id=(x, y, 1 - c), device_id_type=MESH)
            cp.start()
            cps.append((lc, cp))
        all_ref[me] = small_ref[...]
        peers = []
        for d in range(1, 8):
            px, py, pc = x ^ (d >> 2), y ^ ((d >> 1) & 1), c ^ (d & 1)
            cp = pltpu.make_async_remote_copy(src_ref=small_ref, dst_ref=all_ref.at[me],
                                              send_sem=ssend.at[d - 1], recv_sem=srecv.at[d - 1],
                                              device_id=(px, py, pc), device_id_type=MESH)
            cp.start()
            peers.append(cp)
        for cp in peers:
            cp.wait()
        acc = all_ref[0]
        for d in range(1, 8):
            acc = acc + all_ref[d]
        total_ref[...] = acc
        for lc, cp in cps:
            cp.wait()
            lc.wait()

    vm = pl.BlockSpec(memory_space=pltpu.VMEM)
    return pl.pallas_call(
        body, name="share_halves",
        in_specs=[ANY] * n + [vm], out_specs=[ANY] * n + [vm],
        out_shape=[jax.ShapeDtypeStruct((2 * hv.shape[0], D), F32) for hv in halves]
        + [jax.ShapeDtypeStruct((rows, D), F32)],
        scratch_shapes=[pltpu.VMEM((8, rows, D), F32), pltpu.SemaphoreType.DMA((n,)), pltpu.SemaphoreType.DMA((n,)),
                        pltpu.SemaphoreType.DMA((n,)), pltpu.SemaphoreType.DMA((7,)), pltpu.SemaphoreType.DMA((7,))],
        compiler_params=pltpu.CompilerParams(has_side_effects=True),
    )(*halves, small)


def _adamw(w, g, m, v, name, tr):
    rows, cols = w.shape

    def body(w_ref, g_ref, m_ref, v_ref, d_ref, nm_ref, nv_ref):
        g_ = g_ref[...]
        nm = ADAM_B1 * m_ref[...] + (1.0 - ADAM_B1) * g_
        nv = ADAM_B2 * v_ref[...] + (1.0 - ADAM_B2) * (g_ * g_)
        m_hat = nm / (1.0 - ADAM_B1 ** ADAM_STEP)
        v_hat = nv / (1.0 - ADAM_B2 ** ADAM_STEP)
        d_ref[...] = -ADAM_LR * (m_hat / (jnp.sqrt(v_hat) + ADAM_EPS) + ADAM_WD * w_ref[...])
        nm_ref[...] = nm
        nv_ref[...] = nv

    spec = pl.BlockSpec((tr, cols), lambda i: (i, 0))
    return pl.pallas_call(
        body, name=name, grid=(rows // tr,),
        in_specs=[spec] * 4, out_specs=[spec] * 3,
        out_shape=[jax.ShapeDtypeStruct((rows, cols), F32)] * 3,
        compiler_params=_cp(("parallel",)),
    )(w, g, m, v)


def _local_step(x, target, w_in_t, w_o, w_up_t, w_down, norm_a_g, norm_b_g, sinks_a, ln1_g, ln1_b,
                conv_w, conv_b, ln2_g, ln2_b):
    slopes = jnp.asarray(SLOPES, F32)
    w_up3 = w_up_t.reshape(2, FF, D)
    cwb = jnp.concatenate([conv_w, conv_b[None]], axis=0).reshape(4, 2, FF)

    proj = _proj(x, w_in_t, "proj")
    o_a, lse_a = _attn_a_fwd(proj, sinks_a)
    fwd_b = [_attn_b_fwd(proj, slopes, r) for r in B_DILATIONS]
    o_b, lse_b, cat, z1, h1, h1b = _mix_ln1(x, o_a, [f[0] for f in fwd_b], [f[1] for f in fwd_b],
                                           norm_a_g, norm_b_g, w_o, ln1_g, ln1_b)
    up, a = _up_conv_gelu(h1b, w_up3, cwb)
    dz2, dz2b, st2 = _down_ln2_loss(a, w_down, h1, target, ln2_g, ln2_b)

    gw_down, gw_down_b = _grad_w(a, dz2b, "grad_w_down", tm=FF // 2)
    dup, dconv = _conv_gelu_bwd(dz2b, w_down, up, cwb)
    gw_up, gw_up_b = _grad_w(dup, h1b, "grad_w_up", tm=FF // 2, lhs_halves=True)
    dz1, dz1b, st1 = _dh1_ln1_bwd(dz2, dup, w_up3, z1, ln1_g)
    gw_o, gw_o_b = _grad_w(cat, dz1b, "grad_w_o", tm=512)
    d_oa, d_ob, st_n = _dcat_rms_bwd(dz1b, w_o, o_a, o_b, norm_a_g, norm_b_g)
    dqa, dka, dva, dsink = _attn_a_bwd(proj, sinks_a, d_oa, o_a, lse_a)
    bwd_b = [_attn_b_bwd(proj, slopes, d_ob, o_b, lse_b, r) for r in B_DILATIONS]
    dproj = _dproj_combine(dqa, dka, dva, bwd_b)
    gw_in, gw_in_b = _grad_w(dproj, x, "grad_w_in", tm=WA)
    gx = _grad_x(dz1, dproj, w_in_t)

    dconv = dconv.reshape(4, 2 * FF)
    small = dict(loss=st2[2, 0:1], norm_a_g=st_n[0], norm_b_g=st_n[1], sinks_a=dsink[:, 0],
                 ln1_g=st1[0], ln1_b=st1[1], conv_w=dconv[0:3].reshape(-1), conv_b=dconv[3],
                 ln2_g=st2[0], ln2_b=st2[1])
    return gx, (gw_in, gw_o, gw_up, gw_down), (gw_in_b, gw_o_b, gw_up_b, gw_down_b), small


SMALL_ORDER = ("loss", "norm_a_g", "norm_b_g", "sinks_a", "ln1_g", "ln1_b", "conv_b", "ln2_g", "ln2_b", "conv_w")
SMALL_SIZES = dict(loss=1, norm_a_g=512, norm_b_g=512, sinks_a=8, ln1_g=D, ln1_b=D, conv_b=2 * FF, ln2_g=D, ln2_b=D,
                   conv_w=3 * 2 * FF)


def _pack(parts, rows):
    flat = jnp.concatenate([parts[k].reshape(-1).astype(F32) for k in parts])
    return jnp.pad(flat, (0, rows * D - flat.shape[0])).reshape(rows, D)


def _unpack(buf, names, sizes):
    flat = buf.reshape(-1)
    out, at = {}, 0
    for k in names:
        out[k] = flat[at:at + sizes[k]]
        at += sizes[k]
    return out


def kernel(x, w_in, norm_a_g, norm_b_g, sinks_a, w_o, ln1_g, ln1_b, w_up, conv_w, conv_b, w_down, ln2_g, ln2_b, loss_target, m_w_in, m_norm_a_g, m_norm_b_g, m_sinks_a, m_w_o, m_ln1_g, m_ln1_b, m_w_up, m_conv_w, m_conv_b, m_w_down, m_ln2_g, m_ln2_b, v_w_in, v_norm_a_g, v_norm_b_g, v_sinks_a, v_w_o, v_ln1_g, v_ln1_b, v_w_up, v_conv_w, v_conv_b, v_w_down, v_ln2_g, v_ln2_b):
    xi, yi, ci = _place()
    chip = (2 * xi + yi).astype(I32)
    core = ci.astype(I32)

    shards = (w_in.T.astype(BF16), w_o.astype(BF16), w_up.T.astype(BF16), w_down.astype(BF16))
    w_in_t, w_o_f, w_up_t, w_down_f, conv_w4 = _gather_weights(shards, conv_w)
    conv_w_f = conv_w4.transpose(1, 0, 2).reshape(3, 2 * FF)

    gx, grads, grads_b, small = _local_step(
        x[0], loss_target[0], w_in_t, w_o_f, w_up_t, w_down_f, norm_a_g, norm_b_g, sinks_a, ln1_g, ln1_b,
        conv_w_f, conv_b, ln2_g, ln2_b)

    halves_rows = [r // 2 for r in SHARD_ROWS]
    got = _sibling_swap([g.reshape(N_CHIPS, 2, h, D) for g, h in zip(grads_b, halves_rows)])
    tiles = (96, 128, 352, 176)
    grads4 = [g.reshape(N_CHIPS, 2, h, D) for g, h in zip(grads, halves_rows)]
    parts = [_chip_partial(grads4[k], got[k], core.reshape(1), f"chip_partial_{k}", tiles[k]) for k in range(4)]
    got2 = _chip_exchange(parts)
    core_chip = jnp.stack([core, chip])
    halves = [_sum_partials(grads4[k], got[k], got2[k], core_chip, f"sum_partials_{k}", tiles[k]) for k in range(4)]
    small_rows = 32
    *full, totals = _share_halves(halves, _pack({k: small[k] for k in SMALL_ORDER}, small_rows))
    tot = _unpack(totals, SMALL_ORDER, SMALL_SIZES)

    g_w_in, g_w_o, g_w_up, g_w_down = full[0].T, full[1], full[2].T, full[3]
    loss = tot["loss"][0]
    cols = 2 * FF // N_CHIPS
    g_conv_w = lax.dynamic_slice(tot["conv_w"].reshape(3, 2 * FF), (0, chip * cols), (3, cols))
    g_small = dict(norm_a_g=tot["norm_a_g"], norm_b_g=tot["norm_b_g"], sinks_a=tot["sinks_a"], ln1_g=tot["ln1_g"],
                   ln1_b=tot["ln1_b"], conv_w=g_conv_w, conv_b=tot["conv_b"], ln2_g=tot["ln2_g"], ln2_b=tot["ln2_b"])

    weights = dict(w_in=w_in, norm_a_g=norm_a_g, norm_b_g=norm_b_g, sinks_a=sinks_a, w_o=w_o, ln1_g=ln1_g, ln1_b=ln1_b,
                   w_up=w_up, conv_w=conv_w, conv_b=conv_b, w_down=w_down, ln2_g=ln2_g, ln2_b=ln2_b)
    ms = dict(w_in=m_w_in, norm_a_g=m_norm_a_g, norm_b_g=m_norm_b_g, sinks_a=m_sinks_a, w_o=m_w_o, ln1_g=m_ln1_g,
              ln1_b=m_ln1_b, w_up=m_w_up, conv_w=m_conv_w, conv_b=m_conv_b, w_down=m_w_down, ln2_g=m_ln2_g, ln2_b=m_ln2_b)
    vs = dict(w_in=v_w_in, norm_a_g=v_norm_a_g, norm_b_g=v_norm_b_g, sinks_a=v_sinks_a, w_o=v_w_o, ln1_g=v_ln1_g,
              ln1_b=v_ln1_b, w_up=v_w_up, conv_w=v_conv_w, conv_b=v_conv_b, w_down=v_w_down, ln2_g=v_ln2_g, ln2_b=v_ln2_b)
    order = list(weights)
    grad = dict(g_small, w_in=g_w_in, w_o=g_w_o, w_up=g_w_up, w_down=g_w_down)

    delta, new_m, new_v = {}, {}, {}
    for k, tr in (("w_in", 256), ("w_o", 128), ("w_up", 256), ("w_down", 176)):
        delta[k], new_m[k], new_v[k] = _adamw(weights[k], grad[k], ms[k], vs[k], f"adamw_{k}", tr)
    small_names = [k for k in order if k not in delta]
    sizes = {k: weights[k].size for k in small_names}
    rows = 16
    packed = [_pack({k: src[k] for k in small_names}, rows) for src in (weights, grad, ms, vs)]
    for res, buf in zip((delta, new_m, new_v), _adamw(*packed, "adamw_small", rows)):
        for k, val in _unpack(buf, small_names, sizes).items():
            res[k] = val.reshape(weights[k].shape)

    return (loss, gx[None], *[grad[k] for k in order], *[delta[k] for k in order],
            *[new_m[k] for k in order], *[new_v[k] for k in order])
```

```python
import functools
import math

import jax
import jax.numpy as jnp
from jax import lax
from jax.experimental import pallas as pl
from jax.experimental.pallas import tpu as pltpu

F32, BF16, I32 = jnp.float32, jnp.bfloat16, jnp.int32

D = 1024
FF = 2816
HD = 64
NH = 8
WA, WB = 768, 1536
WIN = WA + WB
BLK = 128
ALPHA = 2.0 ** 0.25
LN_EPS, RMS_EPS = 1e-5, 1e-6
SCALE = 1.0 / math.sqrt(HD)
A_MAX_DIST, B_MAX_DIST = 127, 128
B_DILATIONS = (1, 4, 16)
SLOPES = tuple(2.0 ** (-(i + 1)) for i in range(NH))
SHARD_ROWS = (WIN // 4, D // 4, 2 * FF // 4, FF // 4)
N_CHIPS = 4
ADAM_LR, ADAM_B1, ADAM_B2, ADAM_EPS, ADAM_WD, ADAM_STEP = 0.001, 0.9, 0.999, 1e-08, 0.01, 10
MESH = pl.DeviceIdType.MESH
ANY = pl.BlockSpec(memory_space=pl.ANY)
SMEM = pl.BlockSpec(memory_space=pltpu.SMEM)
VMEM = pl.BlockSpec(memory_space=pltpu.VMEM)
HBM = pl.BlockSpec(memory_space=pltpu.HBM)
SEM = pl.BlockSpec(memory_space=pltpu.SEMAPHORE)
DATAFLOW = pltpu.SideEffectType.DATAFLOW_SIDE_EFFECTING


def _cp(sem, mb=48):
    return pltpu.CompilerParams(dimension_semantics=sem, vmem_limit_bytes=mb << 20)


def _nn(a, b):
    return lax.dot_general(a, b, (((1,), (0,)), ((), ())), preferred_element_type=F32)


def _nt(a, b):
    return lax.dot_general(a, b, (((1,), (1,)), ((), ())), preferred_element_type=F32)


def _tn(a, b):
    return lax.dot_general(a, b, (((0,), (0,)), ((), ())), preferred_element_type=F32)


def _resident(shape):
    n = len(shape)
    return pl.BlockSpec(shape, lambda *_: (0,) * n, pipeline_mode=pl.Buffered(1))


def _const(shape):
    n = len(shape)
    return pl.BlockSpec(shape, lambda *_: (0,) * n)


def _proj(x, w_t, name, tm=512):
    s = x.shape[0]
    n = w_t.shape[0]

    def body(x_ref, w_ref, o_ref):
        o_ref[...] = _nt(x_ref[...].astype(BF16), w_ref[...])

    return pl.pallas_call(
        body, name=name, grid=(s // tm,),
        in_specs=[pl.BlockSpec((tm, D), lambda i: (i, 0)), _resident((n, D))],
        out_specs=pl.BlockSpec((tm, n), lambda i: (i, 0)),
        out_shape=jax.ShapeDtypeStruct((s, n), F32),
        compiler_params=_cp(("parallel",)),
    )(x, w_t)


def _grad_w(lhs, rhs, name, tm, tk=512, lhs_halves=False):
    s = rhs.shape[0]
    if lhs_halves:
        per_half = lhs.shape[2] // tm
        n = 2 * lhs.shape[2]
        lhs_spec = pl.BlockSpec((None, tk, tm), lambda i, k: (i // per_half, k, i % per_half))
    else:
        n = lhs.shape[1]
        lhs_spec = pl.BlockSpec((tk, tm), lambda i, k: (k, i))
    nk = s // tk

    def body(l_ref, r_ref, o_ref, ob_ref):
        k = pl.program_id(1)

        @pl.when(k == 0)
        def _():
            o_ref[...] = jnp.zeros_like(o_ref)

        o_ref[...] += _tn(l_ref[...].astype(BF16), r_ref[...].astype(BF16))

        @pl.when(k == nk - 1)
        def _():
            ob_ref[...] = o_ref[...].astype(BF16)

    return pl.pallas_call(
        body, name=name, grid=(n // tm, nk),
        in_specs=[lhs_spec, pl.BlockSpec((tk, D), lambda i, k: (k, 0))],
        out_specs=[pl.BlockSpec((tm, D), lambda i, k: (i, 0))] * 2,
        out_shape=[jax.ShapeDtypeStruct((n, D), F32), jax.ShapeDtypeStruct((n, D), BF16)],
        compiler_params=_cp(("parallel", "arbitrary")),
    )(lhs, rhs)


def _band_masks(first_block):
    row = lax.broadcasted_iota(I32, (BLK, 2 * BLK), 0)
    col = lax.broadcasted_iota(I32, (BLK, 2 * BLK), 1)
    dist = BLK + row - col
    return dist, col, first_block


def _valid(dist, col, not_first, max_dist):
    return (dist >= 0) & (dist <= max_dist) & ((col >= BLK) | not_first)


def _half_mask(shape, e):
    lane = lax.broadcasted_iota(I32, shape, 1)
    return (lane < HD) if e == 0 else (lane >= HD)


def _kv_variant(x, e, g):
    if g != e:
        x = pltpu.roll(x, HD, 1)
    return jnp.where(_half_mask(x.shape, e), x, 0.0).astype(BF16)


def _pair_fwd(q2, k2, v2, valid, distf, slopes, kv_heads, sinks):
    o2 = jnp.zeros((BLK, 2 * HD), F32)
    lse2 = jnp.zeros((BLK, 2 * HD), F32)
    for e in (0, 1):
        kk = _kv_variant(k2, e, kv_heads[e])
        vv = _kv_variant(v2, e, kv_heads[e])
        s = _nt(q2, kk) * SCALE - slopes[e] * distf
        s = jnp.where(valid, s, -jnp.inf)
        m = jnp.max(s, axis=1, keepdims=True)
        if sinks is not None:
            m = jnp.maximum(m, sinks[e])
        p = jnp.exp(s - m)
        l = jnp.sum(p, axis=1, keepdims=True)
        if sinks is not None:
            l = l + jnp.exp(sinks[e] - m)
        o2 = o2 + _nn(p.astype(BF16), vv) / l
        lse2 = jnp.where(_half_mask(lse2.shape, e), m + jnp.log(l), lse2)
    return o2, lse2


def _pair_bwd(q2, k2, v2, do2, o2, lse2, valid, distf, slopes, kv_heads, sinks):
    dq2 = jnp.zeros((BLK, 2 * HD), F32)
    dk2 = jnp.zeros((2 * BLK, 2 * HD), F32)
    dv2 = jnp.zeros((2 * BLK, 2 * HD), F32)
    dob = do2.astype(BF16)
    prod = do2 * o2
    dsinks = []
    for e in (0, 1):
        hq = _half_mask((BLK, 2 * HD), e)
        hk = _half_mask((2 * BLK, 2 * HD), e)
        lse = jnp.max(jnp.where(hq, lse2, -jnp.inf), axis=1, keepdims=True)
        delta = jnp.sum(jnp.where(hq, prod, 0.0), axis=1, keepdims=True)
        kk = _kv_variant(k2, e, kv_heads[e])
        vv = _kv_variant(v2, e, kv_heads[e])
        s = _nt(q2, kk) * SCALE - slopes[e] * distf
        s = jnp.where(valid, s, -jnp.inf)
        p = jnp.exp(s - lse)
        dp = _nt(dob, vv)
        ds = (p * (dp - delta)).astype(BF16)
        dq2 = dq2 + _nn(ds, kk) * SCALE
        dkc = jnp.where(hk, _tn(ds, q2) * SCALE, 0.0)
        dvc = jnp.where(hk, _tn(p.astype(BF16), dob), 0.0)
        if kv_heads[e] != e:
            dkc = pltpu.roll(dkc, HD, 1)
            dvc = pltpu.roll(dvc, HD, 1)
        dk2 = dk2 + dkc
        dv2 = dv2 + dvc
        if sinks is not None:
            dsinks.append(jnp.sum(-jnp.exp(sinks[e] - lse) * delta, axis=0, keepdims=True))
    return dq2, dk2, dv2, dsinks


def _attn_a_fwd(proj, sinks):
    s = proj.shape[0]
    nb = s // BLK

    def body(sink_ref, q_ref, kp_ref, kc_ref, vp_ref, vc_ref, o_ref, lse_ref):
        n = pl.program_id(0)
        dist, col, _ = _band_masks(None)
        valid = _valid(dist, col, n > 0, A_MAX_DIST)
        distf = dist.astype(F32)
        k2 = jnp.concatenate([kp_ref[...], kc_ref[...]], axis=0)
        v2 = jnp.concatenate([vp_ref[...], vc_ref[...]], axis=0)
        for j in range(NH // 2):
            g = j // 2
            q2 = q_ref[:, 128 * j:128 * (j + 1)].astype(BF16)
            o2, lse2 = _pair_fwd(q2, k2, v2, valid, distf, (SLOPES[2 * j], SLOPES[2 * j + 1]), (g, g),
                                 (sink_ref[2 * j], sink_ref[2 * j + 1]))
            o_ref[:, 128 * j:128 * (j + 1)] = o2
            lse_ref[:, 128 * j:128 * (j + 1)] = lse2

    prev = lambda n: jnp.maximum(n - 1, 0)
    return pl.pallas_call(
        body, name="attn_a_fwd", grid=(nb,),
        in_specs=[SMEM,
                  pl.BlockSpec((BLK, 512), lambda n: (n, 0)),
                  pl.BlockSpec((BLK, 128), lambda n: (prev(n), 4)), pl.BlockSpec((BLK, 128), lambda n: (n, 4)),
                  pl.BlockSpec((BLK, 128), lambda n: (prev(n), 5)), pl.BlockSpec((BLK, 128), lambda n: (n, 5))],
        out_specs=[pl.BlockSpec((BLK, 512), lambda n: (n, 0))] * 2,
        out_shape=[jax.ShapeDtypeStruct((s, 512), F32)] * 2,
        compiler_params=_cp(("parallel",)),
    )(sinks, proj, proj, proj, proj, proj)


def _attn_a_bwd(proj, sinks, d_o, o, lse):
    s = proj.shape[0]
    nb = s // BLK

    def body(sink_ref, q_ref, kp_ref, kc_ref, vp_ref, vc_ref, do_ref, o_ref, lse_ref,
             dq_ref, dk_ref, dv_ref, dsink_ref, kcar, vcar):
        n = pl.program_id(0)

        @pl.when(n == 0)
        def _():
            kcar[...] = jnp.zeros_like(kcar)
            vcar[...] = jnp.zeros_like(vcar)
            dsink_ref[...] = jnp.zeros_like(dsink_ref)

        @pl.when(n < nb)
        def _():
            dist, col, _ = _band_masks(None)
            valid = _valid(dist, col, n > 0, A_MAX_DIST)
            distf = dist.astype(F32)
            k2 = jnp.concatenate([kp_ref[...], kc_ref[...]], axis=0)
            v2 = jnp.concatenate([vp_ref[...], vc_ref[...]], axis=0)
            dk_win = jnp.zeros((2 * BLK, 128), F32)
            dv_win = jnp.zeros((2 * BLK, 128), F32)
            for j in range(NH // 2):
                g = j // 2
                sl = slice(128 * j, 128 * (j + 1))
                q2 = q_ref[:, sl].astype(BF16)
                dq2, dk2, dv2, dsk = _pair_bwd(q2, k2, v2, do_ref[:, sl], o_ref[:, sl], lse_ref[:, sl], valid, distf,
                                               (SLOPES[2 * j], SLOPES[2 * j + 1]), (g, g),
                                               (sink_ref[2 * j], sink_ref[2 * j + 1]))
                dq_ref[:, sl] = dq2
                dk_win = dk_win + dk2
                dv_win = dv_win + dv2
                for e in (0, 1):
                    h = 2 * j + e
                    dsink_ref[h:h + 1, :] += jnp.broadcast_to(dsk[e], (1, 128))
            dk_ref[...] = kcar[...] + dk_win[:BLK]
            dv_ref[...] = vcar[...] + dv_win[:BLK]
            kcar[...] = dk_win[BLK:]
            vcar[...] = dv_win[BLK:]

        @pl.when(n == nb)
        def _():
            dk_ref[...] = kcar[...]
            dv_ref[...] = vcar[...]

    cur = lambda n: jnp.minimum(n, nb - 1)
    prev = lambda n: jnp.maximum(cur(n) - 1, 0)
    out_prev = lambda n: jnp.maximum(n - 1, 0)
    return pl.pallas_call(
        body, name="attn_a_bwd", grid=(nb + 1,),
        in_specs=[SMEM,
                  pl.BlockSpec((BLK, 512), lambda n: (cur(n), 0)),
                  pl.BlockSpec((BLK, 128), lambda n: (prev(n), 4)), pl.BlockSpec((BLK, 128), lambda n: (cur(n), 4)),
                  pl.BlockSpec((BLK, 128), lambda n: (prev(n), 5)), pl.BlockSpec((BLK, 128), lambda n: (cur(n), 5)),
                  pl.BlockSpec((BLK, 512), lambda n: (cur(n), 0)),
                  pl.BlockSpec((BLK, 512), lambda n: (cur(n), 0)),
                  pl.BlockSpec((BLK, 512), lambda n: (cur(n), 0))],
        out_specs=[pl.BlockSpec((BLK, 512), lambda n: (cur(n), 0)),
                   pl.BlockSpec((BLK, 128), lambda n: (out_prev(n), 0)),
                   pl.BlockSpec((BLK, 128), lambda n: (out_prev(n), 0)),
                   pl.BlockSpec((NH, 128), lambda n: (0, 0))],
        out_shape=[jax.ShapeDtypeStruct((s, 512), F32), jax.ShapeDtypeStruct((s, 128), F32),
                   jax.ShapeDtypeStruct((s, 128), F32), jax.ShapeDtypeStruct((NH, 128), F32)],
        scratch_shapes=[pltpu.VMEM((BLK, 128), F32), pltpu.VMEM((BLK, 128), F32)],
        compiler_params=_cp(("arbitrary",)),
    )(sinks, proj, proj, proj, proj, proj, d_o, o, lse)


def _strided(rho, r):
    return pl.ds(rho, BLK, stride=r) if r > 1 else pl.ds(0, BLK)


def _attn_b_fwd(proj, slopes, r):
    s = proj.shape[0]
    rows = BLK * r
    nsb = s // rows
    qc, kc, vc = WA // 128, WA // 128 + 4, WA // 128 + 8

    def body(slope_ref, q_ref, kp_ref, kc_ref, vp_ref, vc_ref, o_ref, lse_ref):
        j = pl.program_id(0)
        sb = pl.program_id(1)
        dist, col, _ = _band_masks(None)
        valid = _valid(dist, col, sb > 0, B_MAX_DIST)
        distf = dist.astype(F32) * float(r)
        sl2 = (slope_ref[2 * j], slope_ref[2 * j + 1])

        def per_rho(rho, carry):
            sl = _strided(rho, r)
            q2 = q_ref[sl, :].astype(BF16)
            k2 = jnp.concatenate([kp_ref[sl, :], kc_ref[sl, :]], axis=0)
            v2 = jnp.concatenate([vp_ref[sl, :], vc_ref[sl, :]], axis=0)
            o2, lse2 = _pair_fwd(q2, k2, v2, valid, distf, sl2, (0, 1), None)
            o_ref[sl, :] = o2
            lse_ref[sl, :] = lse2
            return carry

        if r > 1:
            lax.fori_loop(0, r, per_rho, 0)
        else:
            per_rho(0, 0)

    prev = lambda sb: jnp.maximum(sb - 1, 0)
    return pl.pallas_call(
        body, name=f"attn_b_fwd_r{r}", grid=(NH // 2, nsb),
        in_specs=[SMEM,
                  pl.BlockSpec((rows, 128), lambda j, sb: (sb, qc + j)),
                  pl.BlockSpec((rows, 128), lambda j, sb: (prev(sb), kc + j)),
                  pl.BlockSpec((rows, 128), lambda j, sb: (sb, kc + j)),
                  pl.BlockSpec((rows, 128), lambda j, sb: (prev(sb), vc + j)),
                  pl.BlockSpec((rows, 128), lambda j, sb: (sb, vc + j))],
        out_specs=[pl.BlockSpec((rows, 128), lambda j, sb: (sb, j))] * 2,
        out_shape=[jax.ShapeDtypeStruct((s, 512), F32)] * 2,
        compiler_params=_cp(("parallel", "parallel")),
    )(slopes, proj, proj, proj, proj, proj)


def _attn_b_bwd(proj, slopes, d_o, o, lse, r):
    s = proj.shape[0]
    rows = BLK * r
    nsb = s // rows
    qc, kc, vc = WA // 128, WA // 128 + 4, WA // 128 + 8

    def body(slope_ref, q_ref, kp_ref, kc_ref, vp_ref, vc_ref, do_ref, o_ref, lse_ref,
             dq_ref, dk_ref, dv_ref, kcar, vcar):
        j = pl.program_id(0)
        sb = pl.program_id(1)

        @pl.when(sb == 0)
        def _():
            kcar[...] = jnp.zeros_like(kcar)
            vcar[...] = jnp.zeros_like(vcar)

        @pl.when(sb < nsb)
        def _():
            dist, col, _ = _band_masks(None)
            valid = _valid(dist, col, sb > 0, B_MAX_DIST)
            distf = dist.astype(F32) * float(r)
            sl2 = (slope_ref[2 * j], slope_ref[2 * j + 1])

            def per_rho(rho, carry):
                sl = _strided(rho, r)
                q2 = q_ref[sl, :].astype(BF16)
                k2 = jnp.concatenate([kp_ref[sl, :], kc_ref[sl, :]], axis=0)
                v2 = jnp.concatenate([vp_ref[sl, :], vc_ref[sl, :]], axis=0)
                dq2, dk2, dv2, _ = _pair_bwd(q2, k2, v2, do_ref[sl, :], o_ref[sl, :], lse_ref[sl, :], valid, distf,
                                             sl2, (0, 1), None)
                dq_ref[sl, :] = dq2
                dk_ref[sl, :] = kcar[sl, :] + dk2[:BLK]
                dv_ref[sl, :] = vcar[sl, :] + dv2[:BLK]
                kcar[sl, :] = dk2[BLK:]
                vcar[sl, :] = dv2[BLK:]
                return carry

            if r > 1:
                lax.fori_loop(0, r, per_rho, 0)
            else:
                per_rho(0, 0)

        @pl.when(sb == nsb)
        def _():
            dk_ref[...] = kcar[...]
            dv_ref[...] = vcar[...]

    cur = lambda sb: jnp.minimum(sb, nsb - 1)
    prev = lambda sb: jnp.maximum(cur(sb) - 1, 0)
    out_prev = lambda sb: jnp.maximum(sb - 1, 0)
    return pl.pallas_call(
        body, name=f"attn_b_bwd_r{r}", grid=(NH // 2, nsb + 1),
        in_specs=[SMEM,
                  pl.BlockSpec((rows, 128), lambda j, sb: (cur(sb), qc + j)),
                  pl.BlockSpec((rows, 128), lambda j, sb: (prev(sb), kc + j)),
                  pl.BlockSpec((rows, 128), lambda j, sb: (cur(sb), kc + j)),
                  pl.BlockSpec((rows, 128), lambda j, sb: (prev(sb), vc + j)),
                  pl.BlockSpec((rows, 128), lambda j, sb: (cur(sb), vc + j)),
                  pl.BlockSpec((rows, 128), lambda j, sb: (cur(sb), j)),
                  pl.BlockSpec((rows, 128), lambda j, sb: (cur(sb), j)),
                  pl.BlockSpec((rows, 128), lambda j, sb: (cur(sb), j))],
        out_specs=[pl.BlockSpec((rows, 128), lambda j, sb: (cur(sb), j)),
                   pl.BlockSpec((rows, 128), lambda j, sb: (out_prev(sb), j)),
                   pl.BlockSpec((rows, 128), lambda j, sb: (out_prev(sb), j))],
        out_shape=[jax.ShapeDtypeStruct((s, 512), F32)] * 3,
        scratch_shapes=[pltpu.VMEM((rows, 128), F32), pltpu.VMEM((rows, 128), F32)],
        compiler_params=_cp(("parallel", "arbitrary")),
    )(slopes, proj, proj, proj, proj, proj, d_o, o, lse)


def _row(v):
    return v.reshape(1, -1)


def _layer_norm_stats(z):
    mu = jnp.mean(z, axis=-1, keepdims=True)
    zc = z - mu
    var = jnp.mean(zc * zc, axis=-1, keepdims=True)
    rstd = lax.rsqrt(var + LN_EPS)
    return zc * rstd, rstd


def _layer_norm_bwd(dh, zh, rstd, g):
    dzh = dh * g
    return rstd * (dzh - jnp.mean(dzh, axis=-1, keepdims=True) - zh * jnp.mean(dzh * zh, axis=-1, keepdims=True))


def _rms(o):
    return lax.rsqrt(jnp.mean(o * o, axis=-1, keepdims=True) + RMS_EPS)


def _mix_ln1(x, o_a, o_b, lse_b, norm_a_g, norm_b_g, w_o, ln1_g, ln1_b, tm=256):
    s = x.shape[0]

    def body(x_ref, oa_ref, ob1, ob2, ob3, l1, l2, l3, ga_ref, gb_ref, wo_ref, g_ref, b_ref,
             obm_ref, lse_ref, cat_ref, z1_ref, h1_ref, h1b_ref):
        la, lb, lc = l1[...], l2[...], l3[...]
        m = jnp.maximum(jnp.maximum(la, lb), lc)
        ea, eb, ec = jnp.exp(la - m), jnp.exp(lb - m), jnp.exp(lc - m)
        den = ea + eb + ec
        obm = (ea / den) * ob1[...] + (eb / den) * ob2[...] + (ec / den) * ob3[...]
        obm_ref[...] = obm
        lse_ref[...] = m + jnp.log(den)
        oa = oa_ref[...]
        na = oa * _rms(oa) * ga_ref[...]
        nb_ = obm * _rms(obm) * gb_ref[...]
        cat = jnp.concatenate([na, nb_], axis=1).astype(BF16)
        cat_ref[...] = cat
        z1 = ALPHA * x_ref[...] + _nn(cat, wo_ref[...])
        z1_ref[...] = z1
        zh, _ = _layer_norm_stats(z1)
        h1 = zh * g_ref[...] + b_ref[...]
        h1_ref[...] = h1
        h1b_ref[...] = h1.astype(BF16)

    t512 = pl.BlockSpec((tm, 512), lambda i: (i, 0))
    td = pl.BlockSpec((tm, D), lambda i: (i, 0))
    return pl.pallas_call(
        body, name="mix_ln1", grid=(s // tm,),
        in_specs=[td] + [t512] * 7 + [_const((1, 512))] * 2 + [_resident((D, D))] + [_const((1, D))] * 2,
        out_specs=[t512, t512, td, td, td, td],
        out_shape=[jax.ShapeDtypeStruct((s, 512), F32), jax.ShapeDtypeStruct((s, 512), F32),
                   jax.ShapeDtypeStruct((s, D), BF16), jax.ShapeDtypeStruct((s, D), F32),
                   jax.ShapeDtypeStruct((s, D), F32), jax.ShapeDtypeStruct((s, D), BF16)],
        compiler_params=_cp(("parallel",)),
    )(x, o_a, *o_b, *lse_b, _row(norm_a_g), _row(norm_b_g), w_o, _row(ln1_g), _row(ln1_b))


def _gelu(x):
    c = math.sqrt(2.0 / math.pi)
    return 0.5 * x * (1.0 + jnp.tanh(c * (x + 0.044715 * x * x * x)))


def _gelu_and_grad(x):
    c = math.sqrt(2.0 / math.pi)
    t = jnp.tanh(c * (x + 0.044715 * x * x * x))
    g = 0.5 * x * (1.0 + t)
    dg = 0.5 * (1.0 + t) + 0.5 * x * (1.0 - t * t) * c * (1.0 + 3.0 * 0.044715 * x * x)
    return g, dg


def _shift_down(u, before, tm):
    row = lax.broadcasted_iota(I32, u.shape, 0)
    r1 = jnp.where(row == 0, before[7:8], pltpu.roll(u, 1, 0))
    r2 = jnp.where(row == 0, before[6:7], jnp.where(row == 1, before[7:8], pltpu.roll(u, 2, 0)))
    return r1, r2


def _shift_up(u, after, tm):
    row = lax.broadcasted_iota(I32, u.shape, 0)
    l1 = jnp.where(row == tm - 1, after[0:1], pltpu.roll(u, tm - 1, 0))
    l2 = jnp.where(row == tm - 1, after[1:2], jnp.where(row == tm - 2, after[0:1], pltpu.roll(u, tm - 2, 0)))
    return l1, l2


def _up_conv_gelu(h1b, w_up_t, cwb, tm=512, tn=256):
    s = h1b.shape[0]

    def body(h_ref, w_ref, c_ref, up_ref, a_ref, carry):
        i = pl.program_id(1)

        @pl.when(i == 0)
        def _():
            carry[...] = jnp.zeros_like(carry)

        h = h_ref[...]
        u = []
        for half in (0, 1):
            up = _nt(h, w_ref[half])
            up_ref[half] = up
            r1, r2 = _shift_down(up, carry[half], tm)
            u.append(r2 * c_ref[0, half:half + 1, :] + r1 * c_ref[1, half:half + 1, :]
                     + up * c_ref[2, half:half + 1, :] + c_ref[3, half:half + 1, :])
            carry[half] = up[tm - 8:tm]
        a_ref[...] = (_gelu(u[0]) * u[1]).astype(BF16)

    return pl.pallas_call(
        body, name="up_conv_gelu", grid=(FF // tn, s // tm),
        in_specs=[pl.BlockSpec((tm, D), lambda j, i: (i, 0)),
                  pl.BlockSpec((2, tn, D), lambda j, i: (0, j, 0)),
                  pl.BlockSpec((4, 2, tn), lambda j, i: (0, 0, j))],
        out_specs=[pl.BlockSpec((2, tm, tn), lambda j, i: (0, i, j)), pl.BlockSpec((tm, tn), lambda j, i: (i, j))],
        out_shape=[jax.ShapeDtypeStruct((2, s, FF), F32), jax.ShapeDtypeStruct((s, FF), BF16)],
        scratch_shapes=[pltpu.VMEM((2, 8, tn), F32)],
        compiler_params=_cp(("parallel", "arbitrary")),
    )(h1b, w_up_t, cwb)


def _down_ln2_loss(a, w_down, h1, target, ln2_g, ln2_b, tm=256):
    s = a.shape[0]

    def body(a_ref, w_ref, h_ref, t_ref, g_ref, b_ref, dz_ref, dzb_ref, st_ref):
        @pl.when(pl.program_id(0) == 0)
        def _():
            st_ref[...] = jnp.zeros_like(st_ref)

        z2 = ALPHA * h_ref[...] + _nn(a_ref[...], w_ref[...])
        zh, rstd = _layer_norm_stats(z2)
        diff = zh * g_ref[...] + b_ref[...] - t_ref[...]
        part = 0.5 * jnp.sum(jnp.mean(diff * diff, axis=-1, keepdims=True), axis=0, keepdims=True)
        dy = diff * (1.0 / D)
        st_ref[0:1, :] += jnp.sum(dy * zh, axis=0, keepdims=True)
        st_ref[1:2, :] += jnp.sum(dy, axis=0, keepdims=True)
        st_ref[2:3, :] += jnp.broadcast_to(part, (1, D))
        dz = _layer_norm_bwd(dy, zh, rstd, g_ref[...])
        dz_ref[...] = dz
        dzb_ref[...] = dz.astype(BF16)

    td = pl.BlockSpec((tm, D), lambda i: (i, 0))
    return pl.pallas_call(
        body, name="down_ln2_loss", grid=(s // tm,),
        in_specs=[pl.BlockSpec((tm, FF), lambda i: (i, 0)), _resident((FF, D)), td, td, _const((1, D)), _const((1, D))],
        out_specs=[td, td, _const((8, D))],
        out_shape=[jax.ShapeDtypeStruct((s, D), F32), jax.ShapeDtypeStruct((s, D), BF16),
                   jax.ShapeDtypeStruct((8, D), F32)],
        compiler_params=_cp(("arbitrary",)),
    )(a, w_down, h1, target, _row(ln2_g), _row(ln2_b))


def _conv_gelu_bwd(dz2b, w_down, up, cwb, tm=512, tn=256):
    s = dz2b.shape[0]
    n_i = s // tm

    def body(dz_ref, w_ref, up_ref, halo_ref, c_ref, dup_ref, dc_ref, carry):
        ii = pl.program_id(1)
        i = n_i - 1 - ii

        @pl.when(ii == 0)
        def _():
            carry[...] = jnp.zeros_like(carry)
            dc_ref[...] = jnp.zeros_like(dc_ref)

        da = _nt(dz_ref[...], w_ref[...])
        ups, r1s, r2s, us = [], [], [], []
        for half in (0, 1):
            up_h = up_ref[half]
            before = jnp.where(i > 0, halo_ref[half], 0.0)
            r1, r2 = _shift_down(up_h, before, tm)
            us.append(r2 * c_ref[0, half:half + 1, :] + r1 * c_ref[1, half:half + 1, :]
                      + up_h * c_ref[2, half:half + 1, :] + c_ref[3, half:half + 1, :])
            ups.append(up_h)
            r1s.append(r1)
            r2s.append(r2)
        g, dg = _gelu_and_grad(us[0])
        dus = (da * us[1] * dg, da * g)
        for half in (0, 1):
            du = dus[half]
            l1, l2 = _shift_up(du, carry[half], tm)
            dup = du * c_ref[2, half:half + 1, :] + l1 * c_ref[1, half:half + 1, :] + l2 * c_ref[0, half:half + 1, :]
            dup_ref[half] = dup.astype(BF16)
            dc_ref[0, half:half + 1, :] += jnp.sum(du * r2s[half], axis=0, keepdims=True)
            dc_ref[1, half:half + 1, :] += jnp.sum(du * r1s[half], axis=0, keepdims=True)
            dc_ref[2, half:half + 1, :] += jnp.sum(du * ups[half], axis=0, keepdims=True)
            dc_ref[3, half:half + 1, :] += jnp.sum(du, axis=0, keepdims=True)
            carry[half] = du[0:8]

    rev = lambda ii: n_i - 1 - ii
    return pl.pallas_call(
        body, name="conv_gelu_bwd", grid=(FF // tn, n_i),
        in_specs=[pl.BlockSpec((tm, D), lambda j, ii: (rev(ii), 0)),
                  pl.BlockSpec((tn, D), lambda j, ii: (j, 0)),
                  pl.BlockSpec((2, tm, tn), lambda j, ii: (0, rev(ii), j)),
                  pl.BlockSpec((2, 8, tn), lambda j, ii: (0, jnp.maximum(rev(ii) * (tm // 8) - 1, 0), j)),
                  pl.BlockSpec((4, 2, tn), lambda j, ii: (0, 0, j))],
        out_specs=[pl.BlockSpec((2, tm, tn), lambda j, ii: (0, rev(ii), j)),
                   pl.BlockSpec((4, 2, tn), lambda j, ii: (0, 0, j))],
        out_shape=[jax.ShapeDtypeStruct((2, s, FF), BF16), jax.ShapeDtypeStruct((4, 2, FF), F32)],
        scratch_shapes=[pltpu.VMEM((2, 8, tn), F32)],
        compiler_params=_cp(("parallel", "arbitrary")),
    )(dz2b, w_down, up, up, cwb)


def _dh1_ln1_bwd(dz2, dup, w_up_t, z1, ln1_g, tm=256):
    s = dz2.shape[0]

    def body(dz2_ref, dup_ref, w_ref, z1_ref, g_ref, dz1_ref, dz1b_ref, st_ref):
        @pl.when(pl.program_id(0) == 0)
        def _():
            st_ref[...] = jnp.zeros_like(st_ref)

        dh = ALPHA * dz2_ref[...] + _nn(dup_ref[0], w_ref[0]) + _nn(dup_ref[1], w_ref[1])
        zh, rstd = _layer_norm_stats(z1_ref[...])
        st_ref[0:1, :] += jnp.sum(dh * zh, axis=0, keepdims=True)
        st_ref[1:2, :] += jnp.sum(dh, axis=0, keepdims=True)
        dz = _layer_norm_bwd(dh, zh, rstd, g_ref[...])
        dz1_ref[...] = dz
        dz1b_ref[...] = dz.astype(BF16)

    td = pl.BlockSpec((tm, D), lambda i: (i, 0))
    return pl.pallas_call(
        body, name="dh1_ln1_bwd", grid=(s // tm,),
        in_specs=[td, pl.BlockSpec((2, tm, FF), lambda i: (0, i, 0)), _resident((2, FF, D)), td, _const((1, D))],
        out_specs=[td, td, _const((8, D))],
        out_shape=[jax.ShapeDtypeStruct((s, D), F32), jax.ShapeDtypeStruct((s, D), BF16),
                   jax.ShapeDtypeStruct((8, D), F32)],
        compiler_params=_cp(("arbitrary",)),
    )(dz2, dup, w_up_t, z1, _row(ln1_g))


def _dcat_rms_bwd(dz1b, w_o, o_a, o_b, norm_a_g, norm_b_g, tm=256):
    s = dz1b.shape[0]

    def body(dz_ref, w_ref, oa_ref, ob_ref, ga_ref, gb_ref, da_ref, db_ref, st_ref):
        @pl.when(pl.program_id(0) == 0)
        def _():
            st_ref[...] = jnp.zeros_like(st_ref)

        dcat = _nt(dz_ref[...], w_ref[...])
        for k, (o_ref, g_ref, d_ref) in enumerate(((oa_ref, ga_ref, da_ref), (ob_ref, gb_ref, db_ref))):
            o = o_ref[...]
            dn = dcat[:, 512 * k:512 * (k + 1)]
            rr = _rms(o)
            oh = o * rr
            st_ref[k:k + 1, :] += jnp.sum(dn * oh, axis=0, keepdims=True)
            doh = dn * g_ref[...]
            d_ref[...] = rr * (doh - oh * jnp.mean(doh * oh, axis=-1, keepdims=True))

    t512 = pl.BlockSpec((tm, 512), lambda i: (i, 0))
    return pl.pallas_call(
        body, name="dcat_rms_bwd", grid=(s // tm,),
        in_specs=[pl.BlockSpec((tm, D), lambda i: (i, 0)), _resident((D, D)), t512, t512,
                  _const((1, 512)), _const((1, 512))],
        out_specs=[t512, t512, _const((8, 512))],
        out_shape=[jax.ShapeDtypeStruct((s, 512), F32), jax.ShapeDtypeStruct((s, 512), F32),
                   jax.ShapeDtypeStruct((8, 512), F32)],
        compiler_params=_cp(("arbitrary",)),
    )(dz1b, w_o, o_a, o_b, _row(norm_a_g), _row(norm_b_g))


def _dproj_combine(dqa, dka, dva, dqkv_b, tm=256):
    s = dqa.shape[0]

    def body(qa, ka, va, q1, k1, v1, q2, k2, v2, q3, k3, v3, o_ref):
        o_ref[:, 0:512] = qa[...].astype(BF16)
        o_ref[:, 512:640] = ka[...].astype(BF16)
        o_ref[:, 640:768] = va[...].astype(BF16)
        o_ref[:, 768:1280] = (q1[...] + q2[...] + q3[...]).astype(BF16)
        o_ref[:, 1280:1792] = (k1[...] + k2[...] + k3[...]).astype(BF16)
        o_ref[:, 1792:2304] = (v1[...] + v2[...] + v3[...]).astype(BF16)

    t512 = pl.BlockSpec((tm, 512), lambda i: (i, 0))
    t128 = pl.BlockSpec((tm, 128), lambda i: (i, 0))
    flat = [a for trio in dqkv_b for a in trio]
    return pl.pallas_call(
        body, name="dproj_combine", grid=(s // tm,),
        in_specs=[t512, t128, t128] + [t512] * 9,
        out_specs=pl.BlockSpec((tm, WIN), lambda i: (i, 0)),
        out_shape=jax.ShapeDtypeStruct((s, WIN), BF16),
        compiler_params=_cp(("parallel",)),
    )(dqa, dka, dva, *flat)


def _grad_x(dz1, dproj, w_in_t, zero, tm=256):
    s = dz1.shape[0]

    def body(dz_ref, dp_ref, w_ref, z_ref, o_ref):
        o_ref[...] = ALPHA * dz_ref[...] + _nn(dp_ref[...], w_ref[...]) + z_ref[0:1, 0:1]

    td = pl.BlockSpec((tm, D), lambda i: (i, 0))
    return pl.pallas_call(
        body, name="grad_x", grid=(s // tm,),
        in_specs=[td, pl.BlockSpec((tm, WIN), lambda i: (i, 0)), _resident((WIN, D)), _const((8, 128))],
        out_specs=td, out_shape=jax.ShapeDtypeStruct((s, D), F32),
        compiler_params=_cp(("parallel",)),
    )(dz1, dproj, w_in_t, zero)


def _place():
    return lax.axis_index("x"), lax.axis_index("y"), lax.axis_index("c")


def _other_chips(x, y):
    return [(1 - x, y), (x, 1 - y), (1 - x, 1 - y)]


def _hbm(a):
    return pltpu.with_memory_space_constraint(a, pltpu.HBM)


def _gather_w_in(land, conv_land):
    rows_k = SHARD_ROWS[0]
    half = rows_k // 2

    def body(land_in, conv_in, out, conv_out, send_sems, recv_sems):
        del land_in, conv_in
        x, y, c = _place()
        b = 2 * x + y
        sibling = (x, y, 1 - c)
        chips = _other_chips(x, y)

        def copy(idx, chip_b, core, to):
            rows = out.at[pl.ds(pl.multiple_of(chip_b * rows_k + core * half, 16), half)]
            return pltpu.make_async_remote_copy(src_ref=rows, dst_ref=rows, send_sem=send_sems.at[idx],
                                                recv_sem=recv_sems.at[idx], device_id=to, device_id_type=MESH)

        def conv_copy(jn, chip_b, to):
            return pltpu.make_async_remote_copy(src_ref=conv_out.at[chip_b], dst_ref=conv_out.at[chip_b],
                                                send_sem=send_sems.at[6 + jn], recv_sem=recv_sems.at[6 + jn],
                                                device_id=to, device_id_type=MESH)

        started = []
        for jn, chip in enumerate(chips):
            for cp in (copy(jn, b, c, (chip[0], chip[1], c)), conv_copy(jn, b, (chip[0], chip[1], c))):
                cp.start()
                started.append(cp)
        for jn, chip in enumerate(chips):
            cb = 2 * chip[0] + chip[1]
            copy(jn, cb, c, (chip[0], chip[1], c)).wait_recv()
            cp = copy(3 + jn, cb, c, sibling)
            cp.start()
            started.append(cp)
        for jn, chip in enumerate(chips):
            cb = 2 * chip[0] + chip[1]
            copy(3 + jn, cb, 1 - c, sibling).wait_recv()
            conv_copy(jn, cb, (chip[0], chip[1], c)).wait_recv()
        for cp in started:
            cp.wait_send()

    return pl.pallas_call(
        body, name="gather_w_in",
        in_specs=[ANY, ANY], out_specs=[ANY, ANY],
        out_shape=[jax.ShapeDtypeStruct(land.shape, land.dtype), jax.ShapeDtypeStruct(conv_land.shape, conv_land.dtype)],
        input_output_aliases={0: 0, 1: 1},
        scratch_shapes=[pltpu.SemaphoreType.DMA((9,)), pltpu.SemaphoreType.DMA((9,))],
        compiler_params=pltpu.CompilerParams(has_side_effects=True),
    )(land, conv_land)


def _weight_copies(shards, lands, send_sems, recv_sems, arrivals):
    x, y, c = _place()
    b = 2 * x + y
    cps = []
    for jn, chip in enumerate(_other_chips(x, y)):
        at = 2 * chip[0] + chip[1] if arrivals else b
        for k, (shard, land) in enumerate(zip(shards, lands)):
            rows_k = SHARD_ROWS[k + 1]
            cps.append(pltpu.make_async_remote_copy(
                src_ref=shard, dst_ref=land.at[pl.ds(pl.multiple_of(at * rows_k, 16), rows_k)],
                send_sem=send_sems.at[3 * jn + k], recv_sem=recv_sems.at[3 * jn + k],
                device_id=(chip[0], chip[1], c), device_id_type=MESH))
    return cps


def _weights_start(shards, lands):
    def body(s0, s1, s2, l0, l1, l2, send_sems, recv_sems, *outs):
        for send in _weight_copies((s0, s1, s2), (l0, l1, l2), send_sems, recv_sems, False):
            send.start()
        outs[-1][...] = jnp.zeros_like(outs[-1])

    res = pl.pallas_call(
        body, name="weights_start",
        in_specs=[HBM] * 6, out_specs=[SEM, SEM] + [HBM] * 6 + [VMEM],
        out_shape=[pltpu.SemaphoreType.DMA((9,)), pltpu.SemaphoreType.DMA((9,))]
        + [pltpu.HBM(a.shape, a.dtype) for a in (*shards, *lands)] + [jax.ShapeDtypeStruct((8, 128), F32)],
        input_output_aliases={i: i + 2 for i in range(6)},
        compiler_params=pltpu.CompilerParams(has_side_effects=DATAFLOW),
    )(*[_hbm(a) for a in (*shards, *lands)])
    return res[0], res[1], res[2:5], res[5:8], res[8]


def _weights_wait(send_sems, recv_sems, shards, lands, after):
    def body(s0, s1, s2, l0, l1, l2, send_sems, recv_sems, after_ref, *outs):
        for cp in _weight_copies((s0, s1, s2), (l0, l1, l2), send_sems, recv_sems, True):
            cp.wait_send()
            cp.wait_recv()

    res = pl.pallas_call(
        body, name="weights_wait",
        in_specs=[HBM] * 6 + [SEM, SEM, ANY], out_specs=[HBM] * 6,
        out_shape=[pltpu.HBM(a.shape, a.dtype) for a in (*shards, *lands)],
        input_output_aliases={i: i for i in range(6)},
        compiler_params=pltpu.CompilerParams(has_side_effects=DATAFLOW),
    )(*shards, *lands, send_sems, recv_sems, after)
    return res[3:6]


def _grad_copies(g_ref, land_ref, send_sems, recv_sems):
    x, y, c = _place()
    cps = []
    for d in range(1, 8):
        px, py, pc = x ^ (d >> 2), y ^ ((d >> 1) & 1), c ^ (d & 1)
        cps.append(pltpu.make_async_remote_copy(
            src_ref=g_ref.at[2 * px + py, pc], dst_ref=land_ref.at[d - 1], send_sem=send_sems.at[d - 1],
            recv_sem=recv_sems.at[d - 1], device_id=(px, py, pc), device_id_type=MESH))
    return cps


def _grads_start(grad_b, name):
    h = grad_b.shape[2]

    def body(g_ref, land_ref, send_sems, recv_sems, g_thru, land_thru, token):
        for cp in _grad_copies(g_ref, land_ref, send_sems, recv_sems):
            cp.start()
        token[...] = jnp.zeros_like(token)

    return pl.pallas_call(
        body, name=name,
        in_specs=[HBM, HBM], out_specs=[SEM, SEM, HBM, HBM, VMEM],
        out_shape=[pltpu.SemaphoreType.DMA((7,)), pltpu.SemaphoreType.DMA((7,)), pltpu.HBM(grad_b.shape, BF16),
                   pltpu.HBM((7, h, D), BF16), jax.ShapeDtypeStruct((8, 128), F32)],
        input_output_aliases={0: 2, 1: 3},
        compiler_params=pltpu.CompilerParams(has_side_effects=DATAFLOW),
    )(_hbm(grad_b), _hbm(lax.empty((7, h, D), BF16)))


def _grads_wait(started, after):
    n = len(started)

    def body(*refs):
        g, land = refs[:n], refs[n:2 * n]
        send_sems, recv_sems = refs[2 * n:3 * n], refs[3 * n:4 * n]
        for k in range(n):
            for cp in _grad_copies(g[k], land[k], send_sems[k], recv_sems[k]):
                cp.wait_send()
                cp.wait_recv()

    gs = [st[2] for st in started]
    lands = [st[3] for st in started]
    res = pl.pallas_call(
        body, name="grads_wait",
        in_specs=[HBM] * (2 * n) + [SEM] * (2 * n) + [ANY], out_specs=[HBM] * (2 * n),
        out_shape=[pltpu.HBM(a.shape, a.dtype) for a in (*gs, *lands)],
        input_output_aliases={i: i for i in range(2 * n)},
        compiler_params=pltpu.CompilerParams(has_side_effects=DATAFLOW),
    )(*gs, *lands, *[st[0] for st in started], *[st[1] for st in started], after)
    return res[n:]


def _sum_partials(grad4, got, cb, name, tr):
    h = grad4.shape[2]
    per_half = h // tr

    def body(cb_ref, g_ref, o_ref, out_ref):
        acc = g_ref[...]
        for j in range(7):
            acc = acc + o_ref[j].astype(F32)
        out_ref[...] = acc

    return pl.pallas_call(
        body, name=name,
        grid_spec=pltpu.PrefetchScalarGridSpec(
            num_scalar_prefetch=1, grid=(per_half,),
            in_specs=[pl.BlockSpec((None, None, tr, D), lambda i, cb_ref: (cb_ref[1], cb_ref[0], i, 0)),
                      pl.BlockSpec((7, tr, D), lambda i, cb_ref: (0, i, 0))],
            out_specs=pl.BlockSpec((tr, D), lambda i, cb_ref: (cb_ref[0] * per_half + i, 0))),
        out_shape=jax.ShapeDtypeStruct((2 * h, D), F32),
        compiler_params=_cp(("arbitrary",)),
    )(cb, grad4, got)


def _share_halves(shards, small):
    n = len(shards)
    rows = small.shape[0]

    def body(*refs):
        small_ref = refs[n]
        out, total_ref = refs[n + 1:2 * n + 1], refs[2 * n + 1]
        all_ref, send_sems, recv_sems, ssend, srecv = refs[2 * n + 2:]
        x, y, c = _place()
        me = 4 * x + 2 * y + c
        cps = []
        for k in range(n):
            h = shards[k].shape[0] // 2
            mine = out[k].at[pl.ds(pl.multiple_of(c * h, 8), h)]
            cp = pltpu.make_async_remote_copy(src_ref=mine, dst_ref=mine, send_sem=send_sems.at[k],
                                              recv_sem=recv_sems.at[k], device_id=(x, y, 1 - c), device_id_type=MESH)
            cp.start()
            cps.append(cp)
        all_ref[me] = small_ref[...]
        peers = []
        for d in range(1, 8):
            px, py, pc = x ^ (d >> 2), y ^ ((d >> 1) & 1), c ^ (d & 1)
            cp = pltpu.make_async_remote_copy(src_ref=small_ref, dst_ref=all_ref.at[me],
                                              send_sem=ssend.at[d - 1], recv_sem=srecv.at[d - 1],
                                              device_id=(px, py, pc), device_id_type=MESH)
            cp.start()
            peers.append(cp)
        for cp in peers:
            cp.wait()
        acc = all_ref[0]
        for d in range(1, 8):
            acc = acc + all_ref[d]
        total_ref[...] = acc
        for cp in cps:
            cp.wait()

    return pl.pallas_call(
        body, name="share_halves",
        in_specs=[ANY] * n + [VMEM], out_specs=[ANY] * n + [VMEM],
        out_shape=[jax.ShapeDtypeStruct(sh.shape, F32) for sh in shards] + [jax.ShapeDtypeStruct((rows, D), F32)],
        input_output_aliases={k: k for k in range(n)},
        scratch_shapes=[pltpu.VMEM((8, rows, D), F32), pltpu.SemaphoreType.DMA((n,)), pltpu.SemaphoreType.DMA((n,)),
                        pltpu.SemaphoreType.DMA((7,)), pltpu.SemaphoreType.DMA((7,))],
        compiler_params=pltpu.CompilerParams(has_side_effects=True),
    )(*shards, small)


def _adamw(w, g, m, v, name, tr):
    rows, cols = w.shape

    def body(w_ref, g_ref, m_ref, v_ref, d_ref, nm_ref, nv_ref):
        g_ = g_ref[...]
        nm = ADAM_B1 * m_ref[...] + (1.0 - ADAM_B1) * g_
        nv = ADAM_B2 * v_ref[...] + (1.0 - ADAM_B2) * (g_ * g_)
        m_hat = nm / (1.0 - ADAM_B1 ** ADAM_STEP)
        v_hat = nv / (1.0 - ADAM_B2 ** ADAM_STEP)
        d_ref[...] = -ADAM_LR * (m_hat / (jnp.sqrt(v_hat) + ADAM_EPS) + ADAM_WD * w_ref[...])
        nm_ref[...] = nm
        nv_ref[...] = nv

    spec = pl.BlockSpec((tr, cols), lambda i: (i, 0))
    return pl.pallas_call(
        body, name=name, grid=(rows // tr,),
        in_specs=[spec] * 4, out_specs=[spec] * 3,
        out_shape=[jax.ShapeDtypeStruct((rows, cols), F32)] * 3,
        compiler_params=_cp(("parallel",)),
    )(w, g, m, v)


def _local_step(x, target, w_in_t, late_weights, norm_a_g, norm_b_g, sinks_a, ln1_g, ln1_b,
                conv_w, conv_b, ln2_g, ln2_b, slopes, on_grad):
    cwb = jnp.concatenate([conv_w, conv_b[None]], axis=0).reshape(4, 2, FF)

    proj = _proj(x, w_in_t, "proj")
    o_a, lse_a = _attn_a_fwd(proj, sinks_a)
    fwd_b = [_attn_b_fwd(proj, slopes, r) for r in B_DILATIONS]
    w_o, w_up_t, w_down = late_weights(fwd_b[-1][1])
    w_up3 = w_up_t.reshape(2, FF, D)
    o_b, lse_b, cat, z1, h1, h1b = _mix_ln1(x, o_a, [f[0] for f in fwd_b], [f[1] for f in fwd_b],
                                           norm_a_g, norm_b_g, w_o, ln1_g, ln1_b)
    up, a = _up_conv_gelu(h1b, w_up3, cwb)
    dz2, dz2b, st2 = _down_ln2_loss(a, w_down, h1, target, ln2_g, ln2_b)

    tok = on_grad(3, *_grad_w(a, dz2b, "grad_w_down", tm=FF // 2))
    dup, dconv = _conv_gelu_bwd(dz2b, w_down, up, cwb + tok[0, 0])
    tok = on_grad(2, *_grad_w(dup, h1b, "grad_w_up", tm=FF // 2, lhs_halves=True))
    dz1, dz1b, st1 = _dh1_ln1_bwd(dz2, dup, w_up3, z1, ln1_g + tok[0, 0])
    tok = on_grad(1, *_grad_w(cat, dz1b, "grad_w_o", tm=512))
    d_oa, d_ob, st_n = _dcat_rms_bwd(dz1b, w_o, o_a, o_b, norm_a_g + tok[0, 0], norm_b_g)
    dqa, dka, dva, dsink = _attn_a_bwd(proj, sinks_a, d_oa, o_a, lse_a)
    bwd_b = [_attn_b_bwd(proj, slopes, d_ob, o_b, lse_b, r) for r in B_DILATIONS]
    dproj = _dproj_combine(dqa, dka, dva, bwd_b)
    tok = on_grad(0, *_grad_w(dproj, x, "grad_w_in", tm=WA))
    gx = _grad_x(dz1, dproj, w_in_t, tok)

    dconv = dconv.reshape(4, 2 * FF)
    small = dict(loss=st2[2, 0:1], norm_a_g=st_n[0], norm_b_g=st_n[1], sinks_a=dsink[:, 0],
                 ln1_g=st1[0], ln1_b=st1[1], conv_w=dconv[0:3].reshape(-1), conv_b=dconv[3],
                 ln2_g=st2[0], ln2_b=st2[1])
    return gx, small


SMALL_ORDER = ("loss", "norm_a_g", "norm_b_g", "sinks_a", "ln1_g", "ln1_b", "conv_b", "ln2_g", "ln2_b", "conv_w")
SMALL_SIZES = dict(loss=1, norm_a_g=512, norm_b_g=512, sinks_a=8, ln1_g=D, ln1_b=D, conv_b=2 * FF, ln2_g=D, ln2_b=D,
                   conv_w=3 * 2 * FF)


def _pack(parts, rows):
    flat = jnp.concatenate([parts[k].reshape(-1).astype(F32) for k in parts])
    return jnp.pad(flat, (0, rows * D - flat.shape[0])).reshape(rows, D)


def _unpack(buf, names, sizes):
    flat = buf.reshape(-1)
    out, at = {}, 0
    for k in names:
        out[k] = flat[at:at + sizes[k]]
        at += sizes[k]
    return out


def kernel(x, w_in, norm_a_g, norm_b_g, sinks_a, w_o, ln1_g, ln1_b, w_up, conv_w, conv_b, w_down, ln2_g, ln2_b, loss_target, m_w_in, m_norm_a_g, m_norm_b_g, m_sinks_a, m_w_o, m_ln1_g, m_ln1_b, m_w_up, m_conv_w, m_conv_b, m_w_down, m_ln2_g, m_ln2_b, v_w_in, v_norm_a_g, v_norm_b_g, v_sinks_a, v_w_o, v_ln1_g, v_ln1_b, v_w_up, v_conv_w, v_conv_b, v_w_down, v_ln2_g, v_ln2_b):
    xi, yi, ci = _place()
    chip = (2 * xi + yi).astype(I32)
    core = ci.astype(I32)

    shards = (w_in.T.astype(BF16), w_o.astype(BF16), w_up.T.astype(BF16), w_down.astype(BF16))
    lands = [lax.dynamic_update_slice(jnp.zeros((N_CHIPS * r, D), BF16), sh, (chip * r, 0))
             for sh, r in zip(shards, SHARD_ROWS)]
    conv_land = lax.dynamic_update_slice(jnp.zeros((N_CHIPS,) + conv_w.shape, F32), conv_w[None], (chip, 0, 0))
    w_in_t, conv_w4 = _gather_w_in(lands[0], conv_land)
    conv_w_f = conv_w4.transpose(1, 0, 2).reshape(3, 2 * FF)
    w_send, w_recv, w_shards, w_lands, w_tok = _weights_start(shards[1:], lands[1:])
    slopes = jnp.asarray(SLOPES, F32) + w_tok[0, 0]

    halves_rows = [r // 2 for r in SHARD_ROWS]
    grads4, started = [None] * 4, [None] * 4

    def on_grad(k, g, g_b):
        grads4[k] = g.reshape(N_CHIPS, 2, halves_rows[k], D)
        started[k] = _grads_start(g_b.reshape(N_CHIPS, 2, halves_rows[k], D), f"grads_start_{k}")
        return started[k][4]

    gx, small = _local_step(
        x[0], loss_target[0], w_in_t, lambda after: _weights_wait(w_send, w_recv, w_shards, w_lands, after),
        norm_a_g, norm_b_g, sinks_a, ln1_g, ln1_b, conv_w_f, conv_b, ln2_g, ln2_b, slopes, on_grad)

    got = _grads_wait(started, gx)
    tiles = (96, 128, 352, 176)
    core_chip = jnp.stack([core, chip])
    halves = [_sum_partials(grads4[k], got[k], core_chip, f"sum_partials_{k}", tiles[k]) for k in range(4)]
    small_rows = 32
    *full, totals = _share_halves(halves, _pack({k: small[k] for k in SMALL_ORDER}, small_rows))
    tot = _unpack(totals, SMALL_ORDER, SMALL_SIZES)

    g_w_in, g_w_o, g_w_up, g_w_down = full[0].T, full[1], full[2].T, full[3]
    loss = tot["loss"][0]
    cols = 2 * FF // N_CHIPS
    g_conv_w = lax.dynamic_slice(tot["conv_w"].reshape(3, 2 * FF), (0, chip * cols), (3, cols))
    g_small = dict(norm_a_g=tot["norm_a_g"], norm_b_g=tot["norm_b_g"], sinks_a=tot["sinks_a"], ln1_g=tot["ln1_g"],
                   ln1_b=tot["ln1_b"], conv_w=g_conv_w, conv_b=tot["conv_b"], ln2_g=tot["ln2_g"], ln2_b=tot["ln2_b"])

    weights = dict(w_in=w_in, norm_a_g=norm_a_g, norm_b_g=norm_b_g, sinks_a=sinks_a, w_o=w_o, ln1_g=ln1_g, ln1_b=ln1_b,
                   w_up=w_up, conv_w=conv_w, conv_b=conv_b, w_down=w_down, ln2_g=ln2_g, ln2_b=ln2_b)
    ms = dict(w_in=m_w_in, norm_a_g=m_norm_a_g, norm_b_g=m_norm_b_g, sinks_a=m_sinks_a, w_o=m_w_o, ln1_g=m_ln1_g,
              ln1_b=m_ln1_b, w_up=m_w_up, conv_w=m_conv_w, conv_b=m_conv_b, w_down=m_w_down, ln2_g=m_ln2_g, ln2_b=m_ln2_b)
    vs = dict(w_in=v_w_in, norm_a_g=v_norm_a_g, norm_b_g=v_norm_b_g, sinks_a=v_sinks_a, w_o=v_w_o, ln1_g=v_ln1_g,
              ln1_b=v_ln1_b, w_up=v_w_up, conv_w=v_conv_w, conv_b=v_conv_b, w_down=v_w_down, ln2_g=v_ln2_g, ln2_b=v_ln2_b)
    order = list(weights)
    grad = dict(g_small, w_in=g_w_in, w_o=g_w_o, w_up=g_w_up, w_down=g_w_down)

    delta, new_m, new_v = {}, {}, {}
    for k, tr in (("w_in", 256), ("w_o", 128), ("w_up", 256), ("w_down", 176)):
        delta[k], new_m[k], new_v[k] = _adamw(weights[k], grad[k], ms[k], vs[k], f"adamw_{k}", tr)
    small_names = [k for k in order if k not in delta]
    sizes = {k: weights[k].size for k in small_names}
    rows = 16
    packed = [_pack({k: src[k] for k in small_names}, rows) for src in (weights, grad, ms, vs)]
    for res, buf in zip((delta, new_m, new_v), _adamw(*packed, "adamw_small", rows)):
        for k, val in _unpack(buf, small_names, sizes).items():
            res[k] = val.reshape(weights[k].shape)

    return (loss, gx[None], *[grad[k] for k in order], *[delta[k] for k in order],
            *[new_m[k] for k in order], *[new_v[k] for k in order])
```

```python
import functools
import math

import jax
import jax.numpy as jnp
from jax import lax
from jax.experimental import pallas as pl
from jax.experimental.pallas import tpu as pltpu

F32, BF16, I32 = jnp.float32, jnp.bfloat16, jnp.int32

D = 1024
FF = 2816
HD = 64
NH = 8
WA, WB = 768, 1536
WIN = WA + WB
BLK = 128
ALPHA = 2.0 ** 0.25
LN_EPS, RMS_EPS = 1e-5, 1e-6
SCALE = 1.0 / math.sqrt(HD)
A_MAX_DIST, B_MAX_DIST = 127, 128
B_DILATIONS = (1, 4, 16)
SLOPES = tuple(2.0 ** (-(i + 1)) for i in range(NH))
SHARD_ROWS = (WIN // 4, D // 4, 2 * FF // 4, FF // 4)
N_CHIPS = 4
ADAM_LR, ADAM_B1, ADAM_B2, ADAM_EPS, ADAM_WD, ADAM_STEP = 0.001, 0.9, 0.999, 1e-08, 0.01, 10
MESH = pl.DeviceIdType.MESH
ANY = pl.BlockSpec(memory_space=pl.ANY)
SMEM = pl.BlockSpec(memory_space=pltpu.SMEM)
VMEM = pl.BlockSpec(memory_space=pltpu.VMEM)
HBM = pl.BlockSpec(memory_space=pltpu.HBM)
SEM = pl.BlockSpec(memory_space=pltpu.SEMAPHORE)
DATAFLOW = pltpu.SideEffectType.DATAFLOW_SIDE_EFFECTING


def _cp(sem, mb=48):
    return pltpu.CompilerParams(dimension_semantics=sem, vmem_limit_bytes=mb << 20)


def _nn(a, b):
    return lax.dot_general(a, b, (((1,), (0,)), ((), ())), preferred_element_type=F32)


def _nt(a, b):
    return lax.dot_general(a, b, (((1,), (1,)), ((), ())), preferred_element_type=F32)


def _tn(a, b):
    return lax.dot_general(a, b, (((0,), (0,)), ((), ())), preferred_element_type=F32)


def _resident(shape):
    n = len(shape)
    return pl.BlockSpec(shape, lambda *_: (0,) * n, pipeline_mode=pl.Buffered(1))


def _const(shape):
    n = len(shape)
    return pl.BlockSpec(shape, lambda *_: (0,) * n)


def _proj(x, w_t, name, tm=512):
    s = x.shape[0]
    n = w_t.shape[0]

    def body(x_ref, w_ref, o_ref):
        o_ref[...] = _nt(x_ref[...].astype(BF16), w_ref[...])

    return pl.pallas_call(
        body, name=name, grid=(s // tm,),
        in_specs=[pl.BlockSpec((tm, D), lambda i: (i, 0)), _resident((n, D))],
        out_specs=pl.BlockSpec((tm, n), lambda i: (i, 0)),
        out_shape=jax.ShapeDtypeStruct((s, n), F32),
        compiler_params=_cp(("parallel",)),
    )(x, w_t)


def _grad_w(lhs, rhs, name, tm, tk=512, lhs_halves=False):
    s = rhs.shape[0]
    if lhs_halves:
        per_half = lhs.shape[2] // tm
        n = 2 * lhs.shape[2]
        lhs_spec = pl.BlockSpec((None, tk, tm), lambda i, k: (i // per_half, k, i % per_half))
    else:
        n = lhs.shape[1]
        lhs_spec = pl.BlockSpec((tk, tm), lambda i, k: (k, i))
    nk = s // tk

    def body(l_ref, r_ref, o_ref, ob_ref):
        k = pl.program_id(1)

        @pl.when(k == 0)
        def _():
            o_ref[...] = jnp.zeros_like(o_ref)

        o_ref[...] += _tn(l_ref[...].astype(BF16), r_ref[...].astype(BF16))

        @pl.when(k == nk - 1)
        def _():
            ob_ref[...] = o_ref[...].astype(BF16)

    return pl.pallas_call(
        body, name=name, grid=(n // tm, nk),
        in_specs=[lhs_spec, pl.BlockSpec((tk, D), lambda i, k: (k, 0))],
        out_specs=[pl.BlockSpec((tm, D), lambda i, k: (i, 0))] * 2,
        out_shape=[jax.ShapeDtypeStruct((n, D), F32), jax.ShapeDtypeStruct((n, D), BF16)],
        compiler_params=_cp(("parallel", "arbitrary")),
    )(lhs, rhs)


def _band_base(max_dist, dist_unit, first):
    row = lax.broadcasted_iota(I32, (BLK, 2 * BLK), 0)
    col = lax.broadcasted_iota(I32, (BLK, 2 * BLK), 1)
    dist = BLK + row - col
    ok = (dist >= 0) & (dist <= max_dist)
    if first:
        ok = ok & (col >= BLK)
    return jnp.where(ok, dist.astype(F32) * (-float(dist_unit)), -jnp.inf)


def _half_mask(shape, e):
    lane = lax.broadcasted_iota(I32, shape, 1)
    return (lane < HD) if e == 0 else (lane >= HD)


def _to_half(x, e, g):
    if g != e:
        x = pltpu.roll(x, HD, 1)
    return jnp.where(_half_mask(x.shape, g), x, 0.0)


def _pair_fwd(q2, kb, vb, base, slopes, kv_heads, sinks):
    lo = _half_mask((BLK, 2 * HD), 0)
    o2 = lse2 = None
    for e in (0, 1):
        g = kv_heads[e]
        qv = (_to_half(q2, e, g) * SCALE).astype(BF16)
        s = _nt(qv, kb) + slopes[e] * base
        m = jnp.max(s, axis=1, keepdims=True)
        if sinks is not None:
            m = jnp.maximum(m, sinks[e])
        p = jnp.exp(s - m)
        l = jnp.sum(p, axis=1, keepdims=True)
        if sinks is not None:
            l = l + jnp.exp(sinks[e] - m)
        oh = _nn(p.astype(BF16), vb) / l
        if g != e:
            oh = pltpu.roll(oh, HD, 1)
        lse = jnp.broadcast_to(m + jnp.log(l), (BLK, 2 * HD))
        o2 = oh if e == 0 else jnp.where(lo, o2, oh)
        lse2 = lse if e == 0 else jnp.where(lo, lse2, lse)
    return o2, lse2


def _pair_bwd(q2, kb, vb, do2, o2, lse2, base, slopes, kv_heads, sinks):
    lo = _half_mask((BLK, 2 * HD), 0)
    dq2 = dk2 = dv2 = None
    prod = do2 * o2
    dsinks = []
    for e in (0, 1):
        g = kv_heads[e]
        hq = _half_mask((BLK, 2 * HD), e)
        lse = jnp.max(jnp.where(hq, lse2, -jnp.inf), axis=1, keepdims=True)
        delta = jnp.sum(jnp.where(hq, prod, 0.0), axis=1, keepdims=True)
        qv = (_to_half(q2, e, g) * SCALE).astype(BF16)
        dov = _to_half(do2, e, g).astype(BF16)
        p = jnp.exp(_nt(qv, kb) + slopes[e] * base - lse)
        ds = (p * (_nt(dov, vb) - delta)).astype(BF16)
        dqh = _nn(ds, kb) * SCALE
        if g != e:
            dqh = pltpu.roll(dqh, HD, 1)
        dq2 = dqh if e == 0 else jnp.where(lo, dq2, dqh)
        dkh = _tn(ds, qv)
        dvh = _tn(p.astype(BF16), dov)
        dk2 = dkh if e == 0 else dk2 + dkh
        dv2 = dvh if e == 0 else dv2 + dvh
        if sinks is not None:
            dsinks.append(jnp.sum(-jnp.exp(sinks[e] - lse) * delta, axis=0, keepdims=True))
    return dq2, dk2, dv2, dsinks


def _attn_a_fwd(proj, sinks):
    s = proj.shape[0]
    nb = s // BLK

    def body(sink_ref, q_ref, kp_ref, kc_ref, vp_ref, vc_ref, o_ref, lse_ref):
        n = pl.program_id(0)
        base = jnp.where(n > 0, _band_base(A_MAX_DIST, 1, False), _band_base(A_MAX_DIST, 1, True))
        kb = jnp.concatenate([kp_ref[...], kc_ref[...]], axis=0).astype(BF16)
        vb = jnp.concatenate([vp_ref[...], vc_ref[...]], axis=0).astype(BF16)
        for j in range(NH // 2):
            g = j // 2
            o2, lse2 = _pair_fwd(q_ref[:, 128 * j:128 * (j + 1)], kb, vb, base, (SLOPES[2 * j], SLOPES[2 * j + 1]),
                                 (g, g), (sink_ref[2 * j], sink_ref[2 * j + 1]))
            o_ref[:, 128 * j:128 * (j + 1)] = o2
            lse_ref[:, 128 * j:128 * (j + 1)] = lse2

    prev = lambda n: jnp.maximum(n - 1, 0)
    return pl.pallas_call(
        body, name="attn_a_fwd", grid=(nb,),
        in_specs=[SMEM,
                  pl.BlockSpec((BLK, 512), lambda n: (n, 0)),
                  pl.BlockSpec((BLK, 128), lambda n: (prev(n), 4)), pl.BlockSpec((BLK, 128), lambda n: (n, 4)),
                  pl.BlockSpec((BLK, 128), lambda n: (prev(n), 5)), pl.BlockSpec((BLK, 128), lambda n: (n, 5))],
        out_specs=[pl.BlockSpec((BLK, 512), lambda n: (n, 0))] * 2,
        out_shape=[jax.ShapeDtypeStruct((s, 512), F32)] * 2,
        compiler_params=_cp(("parallel",)),
    )(sinks, proj, proj, proj, proj, proj)


def _attn_a_bwd(proj, sinks, d_o, o, lse):
    s = proj.shape[0]
    nb = s // BLK

    def body(sink_ref, q_ref, kp_ref, kc_ref, vp_ref, vc_ref, do_ref, o_ref, lse_ref,
             dq_ref, dk_ref, dv_ref, dsink_ref, kcar, vcar):
        n = pl.program_id(0)

        @pl.when(n == 0)
        def _():
            kcar[...] = jnp.zeros_like(kcar)
            vcar[...] = jnp.zeros_like(vcar)
            dsink_ref[...] = jnp.zeros_like(dsink_ref)

        @pl.when(n < nb)
        def _():
            base = jnp.where(n > 0, _band_base(A_MAX_DIST, 1, False), _band_base(A_MAX_DIST, 1, True))
            kb = jnp.concatenate([kp_ref[...], kc_ref[...]], axis=0).astype(BF16)
            vb = jnp.concatenate([vp_ref[...], vc_ref[...]], axis=0).astype(BF16)
            dk_win = dv_win = None
            for j in range(NH // 2):
                g = j // 2
                sl = slice(128 * j, 128 * (j + 1))
                dq2, dk2, dv2, dsk = _pair_bwd(q_ref[:, sl], kb, vb, do_ref[:, sl], o_ref[:, sl], lse_ref[:, sl], base,
                                               (SLOPES[2 * j], SLOPES[2 * j + 1]), (g, g),
                                               (sink_ref[2 * j], sink_ref[2 * j + 1]))
                dq_ref[:, sl] = dq2
                dk_win = dk2 if j == 0 else dk_win + dk2
                dv_win = dv2 if j == 0 else dv_win + dv2
                for e in (0, 1):
                    h = 2 * j + e
                    dsink_ref[h:h + 1, :] += jnp.broadcast_to(dsk[e], (1, 128))
            dk_ref[...] = kcar[...] + dk_win[:BLK]
            dv_ref[...] = vcar[...] + dv_win[:BLK]
            kcar[...] = dk_win[BLK:]
            vcar[...] = dv_win[BLK:]

        @pl.when(n == nb)
        def _():
            dk_ref[...] = kcar[...]
            dv_ref[...] = vcar[...]

    cur = lambda n: jnp.minimum(n, nb - 1)
    prev = lambda n: jnp.maximum(cur(n) - 1, 0)
    out_prev = lambda n: jnp.maximum(n - 1, 0)
    return pl.pallas_call(
        body, name="attn_a_bwd", grid=(nb + 1,),
        in_specs=[SMEM,
                  pl.BlockSpec((BLK, 512), lambda n: (cur(n), 0)),
                  pl.BlockSpec((BLK, 128), lambda n: (prev(n), 4)), pl.BlockSpec((BLK, 128), lambda n: (cur(n), 4)),
                  pl.BlockSpec((BLK, 128), lambda n: (prev(n), 5)), pl.BlockSpec((BLK, 128), lambda n: (cur(n), 5)),
                  pl.BlockSpec((BLK, 512), lambda n: (cur(n), 0)),
                  pl.BlockSpec((BLK, 512), lambda n: (cur(n), 0)),
                  pl.BlockSpec((BLK, 512), lambda n: (cur(n), 0))],
        out_specs=[pl.BlockSpec((BLK, 512), lambda n: (cur(n), 0)),
                   pl.BlockSpec((BLK, 128), lambda n: (out_prev(n), 0)),
                   pl.BlockSpec((BLK, 128), lambda n: (out_prev(n), 0)),
                   pl.BlockSpec((NH, 128), lambda n: (0, 0))],
        out_shape=[jax.ShapeDtypeStruct((s, 512), F32), jax.ShapeDtypeStruct((s, 128), F32),
                   jax.ShapeDtypeStruct((s, 128), F32), jax.ShapeDtypeStruct((NH, 128), F32)],
        scratch_shapes=[pltpu.VMEM((BLK, 128), F32), pltpu.VMEM((BLK, 128), F32)],
        compiler_params=_cp(("arbitrary",)),
    )(sinks, proj, proj, proj, proj, proj, d_o, o, lse)


def _stream(rho, i, r):
    start = i * BLK * r + rho
    return pl.ds(start, BLK, stride=r) if r > 1 else pl.ds(start, BLK)


def _for_streams(r, fn):
    if r <= 4:
        for rho in range(r):
            fn(rho)
    else:
        def four(it, carry):
            for u in range(4):
                fn(4 * it + u)
            return carry

        lax.fori_loop(0, r // 4, four, 0)


B_BLOCKS_PER_STEP = {1: 4, 4: 1, 16: 1}


def _attn_b_fwd(proj, slopes, r):
    s = proj.shape[0]
    nq = B_BLOCKS_PER_STEP[r]
    rows = BLK * r * nq
    steps = s // rows
    qc, kc, vc = WA // 128, WA // 128 + 4, WA // 128 + 8

    def body(slope_ref, q_ref, kp_ref, kc_ref, vp_ref, vc_ref, o_ref, lse_ref):
        j = pl.program_id(0)
        sb = pl.program_id(1)
        base_rest = _band_base(B_MAX_DIST, r, False)
        base_0 = jnp.where(sb > 0, base_rest, _band_base(B_MAX_DIST, r, True))
        sl2 = (slope_ref[2 * j], slope_ref[2 * j + 1])

        def stream(rho):
            for i in range(nq):
                cur = _stream(rho, i, r)
                k_prev = kc_ref[_stream(rho, i - 1, r), :] if i > 0 else kp_ref[_stream(rho, 0, r), :]
                v_prev = vc_ref[_stream(rho, i - 1, r), :] if i > 0 else vp_ref[_stream(rho, 0, r), :]
                kb = jnp.concatenate([k_prev, kc_ref[cur, :]], axis=0).astype(BF16)
                vb = jnp.concatenate([v_prev, vc_ref[cur, :]], axis=0).astype(BF16)
                o2, lse2 = _pair_fwd(q_ref[cur, :], kb, vb, base_rest if i > 0 else base_0, sl2, (0, 1), None)
                o_ref[cur, :] = o2
                lse_ref[cur, :] = lse2

        _for_streams(r, stream)

    before = lambda sb: jnp.maximum(sb * nq - 1, 0)
    return pl.pallas_call(
        body, name=f"attn_b_fwd_r{r}", grid=(NH // 2, steps),
        in_specs=[SMEM,
                  pl.BlockSpec((rows, 128), lambda j, sb: (sb, qc + j)),
                  pl.BlockSpec((BLK * r, 128), lambda j, sb: (before(sb), kc + j)),
                  pl.BlockSpec((rows, 128), lambda j, sb: (sb, kc + j)),
                  pl.BlockSpec((BLK * r, 128), lambda j, sb: (before(sb), vc + j)),
                  pl.BlockSpec((rows, 128), lambda j, sb: (sb, vc + j))],
        out_specs=[pl.BlockSpec((rows, 128), lambda j, sb: (sb, j))] * 2,
        out_shape=[jax.ShapeDtypeStruct((s, 512), F32)] * 2,
        compiler_params=_cp(("parallel", "parallel")),
    )(slopes, proj, proj, proj, proj, proj)


def _attn_b_bwd(proj, slopes, d_o, o, lse, r):
    s = proj.shape[0]
    nq = B_BLOCKS_PER_STEP[r]
    rows = BLK * r * nq
    steps = s // rows
    qc, kc, vc = WA // 128, WA // 128 + 4, WA // 128 + 8

    def body(slope_ref, q_ref, kp_ref, kc_ref, vp_ref, vc_ref, do_ref, o_ref, lse_ref,
             dq_ref, dk_ref, dv_ref, kcar, vcar):
        j = pl.program_id(0)
        sb = pl.program_id(1)

        @pl.when(sb == 0)
        def _():
            kcar[...] = jnp.zeros_like(kcar)
            vcar[...] = jnp.zeros_like(vcar)

        dk_ref[...] = kcar[...]
        dv_ref[...] = vcar[...]

        @pl.when(sb < steps)
        def _():
            base_rest = _band_base(B_MAX_DIST, r, False)
            base_0 = jnp.where(sb > 0, base_rest, _band_base(B_MAX_DIST, r, True))
            sl2 = (slope_ref[2 * j], slope_ref[2 * j + 1])

            def stream(rho):
                for i in range(nq):
                    cur = _stream(rho, i, r)
                    k_prev = kc_ref[_stream(rho, i - 1, r), :] if i > 0 else kp_ref[_stream(rho, 0, r), :]
                    v_prev = vc_ref[_stream(rho, i - 1, r), :] if i > 0 else vp_ref[_stream(rho, 0, r), :]
                    kb = jnp.concatenate([k_prev, kc_ref[cur, :]], axis=0).astype(BF16)
                    vb = jnp.concatenate([v_prev, vc_ref[cur, :]], axis=0).astype(BF16)
                    dq2, dk2, dv2, _ = _pair_bwd(q_ref[cur, :], kb, vb, do_ref[cur, :], o_ref[cur, :], lse_ref[cur, :],
                                                 base_rest if i > 0 else base_0, sl2, (0, 1), None)
                    dq_ref[cur, :] = dq2
                    if i == 0:
                        last = _stream(rho, nq - 1, r)
                        dk_ref[last, :] += dk2[:BLK]
                        dv_ref[last, :] += dv2[:BLK]
                    else:
                        kcar[_stream(rho, i - 1, r), :] += dk2[:BLK]
                        vcar[_stream(rho, i - 1, r), :] += dv2[:BLK]
                    kcar[cur, :] = dk2[BLK:]
                    vcar[cur, :] = dv2[BLK:]

            _for_streams(r, stream)

    cur_step = lambda sb: jnp.minimum(sb, steps - 1)
    before = lambda sb: jnp.maximum(cur_step(sb) * nq - 1, 0)
    out_prev = lambda sb: jnp.maximum(sb - 1, 0)
    tile = lambda col: pl.BlockSpec((rows, 128), lambda j, sb: (cur_step(sb), col + j))
    edge = lambda col: pl.BlockSpec((BLK * r, 128), lambda j, sb: (before(sb), col + j))
    return pl.pallas_call(
        body, name=f"attn_b_bwd_r{r}", grid=(NH // 2, steps + 1),
        in_specs=[SMEM, tile(qc), edge(kc), tile(kc), edge(vc), tile(vc), tile(0), tile(0), tile(0)],
        out_specs=[tile(0),
                   pl.BlockSpec((rows, 128), lambda j, sb: (out_prev(sb), j)),
                   pl.BlockSpec((rows, 128), lambda j, sb: (out_prev(sb), j))],
        out_shape=[jax.ShapeDtypeStruct((s, 512), F32)] * 3,
        scratch_shapes=[pltpu.VMEM((rows, 128), F32), pltpu.VMEM((rows, 128), F32)],
        compiler_params=_cp(("parallel", "arbitrary")),
    )(slopes, proj, proj, proj, proj, proj, d_o, o, lse)


def _row(v):
    return v.reshape(1, -1)


def _layer_norm_stats(z):
    mu = jnp.mean(z, axis=-1, keepdims=True)
    zc = z - mu
    var = jnp.mean(zc * zc, axis=-1, keepdims=True)
    rstd = lax.rsqrt(var + LN_EPS)
    return zc * rstd, rstd


def _layer_norm_bwd(dh, zh, rstd, g):
    dzh = dh * g
    return rstd * (dzh - jnp.mean(dzh, axis=-1, keepdims=True) - zh * jnp.mean(dzh * zh, axis=-1, keepdims=True))


def _rms(o):
    return lax.rsqrt(jnp.mean(o * o, axis=-1, keepdims=True) + RMS_EPS)


def _mix_ln1(x, o_a, o_b, lse_b, norm_a_g, norm_b_g, w_o, ln1_g, ln1_b, tm=256):
    s = x.shape[0]

    def body(x_ref, oa_ref, ob1, ob2, ob3, l1, l2, l3, ga_ref, gb_ref, wo_ref, g_ref, b_ref,
             obm_ref, lse_ref, cat_ref, z1_ref, h1_ref, h1b_ref):
        la, lb, lc = l1[...], l2[...], l3[...]
        m = jnp.maximum(jnp.maximum(la, lb), lc)
        ea, eb, ec = jnp.exp(la - m), jnp.exp(lb - m), jnp.exp(lc - m)
        den = ea + eb + ec
        obm = (ea / den) * ob1[...] + (eb / den) * ob2[...] + (ec / den) * ob3[...]
        obm_ref[...] = obm
        lse_ref[...] = m + jnp.log(den)
        oa = oa_ref[...]
        na = oa * _rms(oa) * ga_ref[...]
        nb_ = obm * _rms(obm) * gb_ref[...]
        cat = jnp.concatenate([na, nb_], axis=1).astype(BF16)
        cat_ref[...] = cat
        z1 = ALPHA * x_ref[...] + _nn(cat, wo_ref[...])
        z1_ref[...] = z1
        zh, _ = _layer_norm_stats(z1)
        h1 = zh * g_ref[...] + b_ref[...]
        h1_ref[...] = h1
        h1b_ref[...] = h1.astype(BF16)

    t512 = pl.BlockSpec((tm, 512), lambda i: (i, 0))
    td = pl.BlockSpec((tm, D), lambda i: (i, 0))
    return pl.pallas_call(
        body, name="mix_ln1", grid=(s // tm,),
        in_specs=[td] + [t512] * 7 + [_const((1, 512))] * 2 + [_resident((D, D))] + [_const((1, D))] * 2,
        out_specs=[t512, t512, td, td, td, td],
        out_shape=[jax.ShapeDtypeStruct((s, 512), F32), jax.ShapeDtypeStruct((s, 512), F32),
                   jax.ShapeDtypeStruct((s, D), BF16), jax.ShapeDtypeStruct((s, D), F32),
                   jax.ShapeDtypeStruct((s, D), F32), jax.ShapeDtypeStruct((s, D), BF16)],
        compiler_params=_cp(("parallel",)),
    )(x, o_a, *o_b, *lse_b, _row(norm_a_g), _row(norm_b_g), w_o, _row(ln1_g), _row(ln1_b))


def _gelu(x):
    c = math.sqrt(2.0 / math.pi)
    return 0.5 * x * (1.0 + jnp.tanh(c * (x + 0.044715 * x * x * x)))


def _gelu_and_grad(x):
    c = math.sqrt(2.0 / math.pi)
    t = jnp.tanh(c * (x + 0.044715 * x * x * x))
    g = 0.5 * x * (1.0 + t)
    dg = 0.5 * (1.0 + t) + 0.5 * x * (1.0 - t * t) * c * (1.0 + 3.0 * 0.044715 * x * x)
    return g, dg


def _shift_down(u, before, tm):
    row = lax.broadcasted_iota(I32, u.shape, 0)
    r1 = jnp.where(row == 0, before[7:8], pltpu.roll(u, 1, 0))
    r2 = jnp.where(row == 0, before[6:7], jnp.where(row == 1, before[7:8], pltpu.roll(u, 2, 0)))
    return r1, r2


def _shift_up(u, after, tm):
    row = lax.broadcasted_iota(I32, u.shape, 0)
    l1 = jnp.where(row == tm - 1, after[0:1], pltpu.roll(u, tm - 1, 0))
    l2 = jnp.where(row == tm - 1, after[1:2], jnp.where(row == tm - 2, after[0:1], pltpu.roll(u, tm - 2, 0)))
    return l1, l2


def _up_conv_gelu(h1b, w_up_t, cwb, tm=512, tn=256):
    s = h1b.shape[0]

    def body(h_ref, w_ref, c_ref, up_ref, a_ref, carry):
        i = pl.program_id(1)

        @pl.when(i == 0)
        def _():
            carry[...] = jnp.zeros_like(carry)

        h = h_ref[...]
        u = []
        for half in (0, 1):
            up = _nt(h, w_ref[half])
            up_ref[half] = up
            r1, r2 = _shift_down(up, carry[half], tm)
            u.append(r2 * c_ref[0, half:half + 1, :] + r1 * c_ref[1, half:half + 1, :]
                     + up * c_ref[2, half:half + 1, :] + c_ref[3, half:half + 1, :])
            carry[half] = up[tm - 8:tm]
        a_ref[...] = (_gelu(u[0]) * u[1]).astype(BF16)

    return pl.pallas_call(
        body, name="up_conv_gelu", grid=(FF // tn, s // tm),
        in_specs=[pl.BlockSpec((tm, D), lambda j, i: (i, 0)),
                  pl.BlockSpec((2, tn, D), lambda j, i: (0, j, 0)),
                  pl.BlockSpec((4, 2, tn), lambda j, i: (0, 0, j))],
        out_specs=[pl.BlockSpec((2, tm, tn), lambda j, i: (0, i, j)), pl.BlockSpec((tm, tn), lambda j, i: (i, j))],
        out_shape=[jax.ShapeDtypeStruct((2, s, FF), F32), jax.ShapeDtypeStruct((s, FF), BF16)],
        scratch_shapes=[pltpu.VMEM((2, 8, tn), F32)],
        compiler_params=_cp(("parallel", "arbitrary")),
    )(h1b, w_up_t, cwb)


def _down_ln2_loss(a, w_down, h1, target, ln2_g, ln2_b, tm=256):
    s = a.shape[0]

    def body(a_ref, w_ref, h_ref, t_ref, g_ref, b_ref, dz_ref, dzb_ref, st_ref):
        @pl.when(pl.program_id(0) == 0)
        def _():
            st_ref[...] = jnp.zeros_like(st_ref)

        z2 = ALPHA * h_ref[...] + _nn(a_ref[...], w_ref[...])
        zh, rstd = _layer_norm_stats(z2)
        diff = zh * g_ref[...] + b_ref[...] - t_ref[...]
        part = 0.5 * jnp.sum(jnp.mean(diff * diff, axis=-1, keepdims=True), axis=0, keepdims=True)
        dy = diff * (1.0 / D)
        st_ref[0:1, :] += jnp.sum(dy * zh, axis=0, keepdims=True)
        st_ref[1:2, :] += jnp.sum(dy, axis=0, keepdims=True)
        st_ref[2:3, :] += jnp.broadcast_to(part, (1, D))
        dz = _layer_norm_bwd(dy, zh, rstd, g_ref[...])
        dz_ref[...] = dz
        dzb_ref[...] = dz.astype(BF16)

    td = pl.BlockSpec((tm, D), lambda i: (i, 0))
    return pl.pallas_call(
        body, name="down_ln2_loss", grid=(s // tm,),
        in_specs=[pl.BlockSpec((tm, FF), lambda i: (i, 0)), _resident((FF, D)), td, td, _const((1, D)), _const((1, D))],
        out_specs=[td, td, _const((8, D))],
        out_shape=[jax.ShapeDtypeStruct((s, D), F32), jax.ShapeDtypeStruct((s, D), BF16),
                   jax.ShapeDtypeStruct((8, D), F32)],
        compiler_params=_cp(("arbitrary",)),
    )(a, w_down, h1, target, _row(ln2_g), _row(ln2_b))


def _conv_gelu_bwd(dz2b, w_down, up, cwb, tm=512, tn=256):
    s = dz2b.shape[0]
    n_i = s // tm

    def body(dz_ref, w_ref, up_ref, halo_ref, c_ref, dup_ref, dc_ref, carry):
        ii = pl.program_id(1)
        i = n_i - 1 - ii

        @pl.when(ii == 0)
        def _():
            carry[...] = jnp.zeros_like(carry)
            dc_ref[...] = jnp.zeros_like(dc_ref)

        da = _nt(dz_ref[...], w_ref[...])
        ups, r1s, r2s, us = [], [], [], []
        for half in (0, 1):
            up_h = up_ref[half]
            before = jnp.where(i > 0, halo_ref[half], 0.0)
            r1, r2 = _shift_down(up_h, before, tm)
            us.append(r2 * c_ref[0, half:half + 1, :] + r1 * c_ref[1, half:half + 1, :]
                      + up_h * c_ref[2, half:half + 1, :] + c_ref[3, half:half + 1, :])
            ups.append(up_h)
            r1s.append(r1)
            r2s.append(r2)
        g, dg = _gelu_and_grad(us[0])
        dus = (da * us[1] * dg, da * g)
        for half in (0, 1):
            du = dus[half]
            l1, l2 = _shift_up(du, carry[half], tm)
            dup = du * c_ref[2, half:half + 1, :] + l1 * c_ref[1, half:half + 1, :] + l2 * c_ref[0, half:half + 1, :]
            dup_ref[half] = dup.astype(BF16)
            dc_ref[0, half:half + 1, :] += jnp.sum(du * r2s[half], axis=0, keepdims=True)
            dc_ref[1, half:half + 1, :] += jnp.sum(du * r1s[half], axis=0, keepdims=True)
            dc_ref[2, half:half + 1, :] += jnp.sum(du * ups[half], axis=0, keepdims=True)
            dc_ref[3, half:half + 1, :] += jnp.sum(du, axis=0, keepdims=True)
            carry[half] = du[0:8]

    rev = lambda ii: n_i - 1 - ii
    return pl.pallas_call(
        body, name="conv_gelu_bwd", grid=(FF // tn, n_i),
        in_specs=[pl.BlockSpec((tm, D), lambda j, ii: (rev(ii), 0)),
                  pl.BlockSpec((tn, D), lambda j, ii: (j, 0)),
                  pl.BlockSpec((2, tm, tn), lambda j, ii: (0, rev(ii), j)),
                  pl.BlockSpec((2, 8, tn), lambda j, ii: (0, jnp.maximum(rev(ii) * (tm // 8) - 1, 0), j)),
                  pl.BlockSpec((4, 2, tn), lambda j, ii: (0, 0, j))],
        out_specs=[pl.BlockSpec((2, tm, tn), lambda j, ii: (0, rev(ii), j)),
                   pl.BlockSpec((4, 2, tn), lambda j, ii: (0, 0, j))],
        out_shape=[jax.ShapeDtypeStruct((2, s, FF), BF16), jax.ShapeDtypeStruct((4, 2, FF), F32)],
        scratch_shapes=[pltpu.VMEM((2, 8, tn), F32)],
        compiler_params=_cp(("parallel", "arbitrary")),
    )(dz2b, w_down, up, up, cwb)


def _dh1_ln1_bwd(dz2, dup, w_up_t, z1, ln1_g, tm=256):
    s = dz2.shape[0]

    def body(dz2_ref, dup_ref, w_ref, z1_ref, g_ref, dz1_ref, dz1b_ref, st_ref):
        @pl.when(pl.program_id(0) == 0)
        def _():
            st_ref[...] = jnp.zeros_like(st_ref)

        dh = ALPHA * dz2_ref[...] + _nn(dup_ref[0], w_ref[0]) + _nn(dup_ref[1], w_ref[1])
        zh, rstd = _layer_norm_stats(z1_ref[...])
        st_ref[0:1, :] += jnp.sum(dh * zh, axis=0, keepdims=True)
        st_ref[1:2, :] += jnp.sum(dh, axis=0, keepdims=True)
        dz = _layer_norm_bwd(dh, zh, rstd, g_ref[...])
        dz1_ref[...] = dz
        dz1b_ref[...] = dz.astype(BF16)

    td = pl.BlockSpec((tm, D), lambda i: (i, 0))
    return pl.pallas_call(
        body, name="dh1_ln1_bwd", grid=(s // tm,),
        in_specs=[td, pl.BlockSpec((2, tm, FF), lambda i: (0, i, 0)), _resident((2, FF, D)), td, _const((1, D))],
        out_specs=[td, td, _const((8, D))],
        out_shape=[jax.ShapeDtypeStruct((s, D), F32), jax.ShapeDtypeStruct((s, D), BF16),
                   jax.ShapeDtypeStruct((8, D), F32)],
        compiler_params=_cp(("arbitrary",)),
    )(dz2, dup, w_up_t, z1, _row(ln1_g))


def _dcat_rms_bwd(dz1b, w_o, o_a, o_b, norm_a_g, norm_b_g, tm=256):
    s = dz1b.shape[0]

    def body(dz_ref, w_ref, oa_ref, ob_ref, ga_ref, gb_ref, da_ref, db_ref, st_ref):
        @pl.when(pl.program_id(0) == 0)
        def _():
            st_ref[...] = jnp.zeros_like(st_ref)

        dcat = _nt(dz_ref[...], w_ref[...])
        for k, (o_ref, g_ref, d_ref) in enumerate(((oa_ref, ga_ref, da_ref), (ob_ref, gb_ref, db_ref))):
            o = o_ref[...]
            dn = dcat[:, 512 * k:512 * (k + 1)]
            rr = _rms(o)
            oh = o * rr
            st_ref[k:k + 1, :] += jnp.sum(dn * oh, axis=0, keepdims=True)
            doh = dn * g_ref[...]
            d_ref[...] = rr * (doh - oh * jnp.mean(doh * oh, axis=-1, keepdims=True))

    t512 = pl.BlockSpec((tm, 512), lambda i: (i, 0))
    return pl.pallas_call(
        body, name="dcat_rms_bwd", grid=(s // tm,),
        in_specs=[pl.BlockSpec((tm, D), lambda i: (i, 0)), _resident((D, D)), t512, t512,
                  _const((1, 512)), _const((1, 512))],
        out_specs=[t512, t512, _const((8, 512))],
        out_shape=[jax.ShapeDtypeStruct((s, 512), F32), jax.ShapeDtypeStruct((s, 512), F32),
                   jax.ShapeDtypeStruct((8, 512), F32)],
        compiler_params=_cp(("arbitrary",)),
    )(dz1b, w_o, o_a, o_b, _row(norm_a_g), _row(norm_b_g))


def _dproj_combine(dqa, dka, dva, dqkv_b, tm=256):
    s = dqa.shape[0]

    def body(qa, ka, va, q1, k1, v1, q2, k2, v2, q3, k3, v3, o_ref):
        o_ref[:, 0:512] = qa[...].astype(BF16)
        o_ref[:, 512:640] = ka[...].astype(BF16)
        o_ref[:, 640:768] = va[...].astype(BF16)
        o_ref[:, 768:1280] = (q1[...] + q2[...] + q3[...]).astype(BF16)
        o_ref[:, 1280:1792] = (k1[...] + k2[...] + k3[...]).astype(BF16)
        o_ref[:, 1792:2304] = (v1[...] + v2[...] + v3[...]).astype(BF16)

    t512 = pl.BlockSpec((tm, 512), lambda i: (i, 0))
    t128 = pl.BlockSpec((tm, 128), lambda i: (i, 0))
    flat = [a for trio in dqkv_b for a in trio]
    return pl.pallas_call(
        body, name="dproj_combine", grid=(s // tm,),
        in_specs=[t512, t128, t128] + [t512] * 9,
        out_specs=pl.BlockSpec((tm, WIN), lambda i: (i, 0)),
        out_shape=jax.ShapeDtypeStruct((s, WIN), BF16),
        compiler_params=_cp(("parallel",)),
    )(dqa, dka, dva, *flat)


def _grad_x(dz1, dproj, w_in_t, zero, tm=256):
    s = dz1.shape[0]

    def body(dz_ref, dp_ref, w_ref, z_ref, o_ref):
        o_ref[...] = ALPHA * dz_ref[...] + _nn(dp_ref[...], w_ref[...]) + z_ref[0:1, 0:1]

    td = pl.BlockSpec((tm, D), lambda i: (i, 0))
    return pl.pallas_call(
        body, name="grad_x", grid=(s // tm,),
        in_specs=[td, pl.BlockSpec((tm, WIN), lambda i: (i, 0)), _resident((WIN, D)), _const((8, 128))],
        out_specs=td, out_shape=jax.ShapeDtypeStruct((s, D), F32),
        compiler_params=_cp(("parallel",)),
    )(dz1, dproj, w_in_t, zero)


def _place():
    return lax.axis_index("x"), lax.axis_index("y"), lax.axis_index("c")


def _other_chips(x, y):
    return [(1 - x, y), (x, 1 - y), (1 - x, 1 - y)]


def _hbm(a):
    return pltpu.with_memory_space_constraint(a, pltpu.HBM)


def _gather_w_in(land, conv_land):
    rows_k = SHARD_ROWS[0]
    half = rows_k // 2

    def body(land_in, conv_in, out, conv_out, send_sems, recv_sems):
        del land_in, conv_in
        x, y, c = _place()
        b = 2 * x + y
        sibling = (x, y, 1 - c)
        chips = _other_chips(x, y)

        def copy(idx, chip_b, core, to):
            rows = out.at[pl.ds(pl.multiple_of(chip_b * rows_k + core * half, 16), half)]
            return pltpu.make_async_remote_copy(src_ref=rows, dst_ref=rows, send_sem=send_sems.at[idx],
                                                recv_sem=recv_sems.at[idx], device_id=to, device_id_type=MESH)

        def conv_copy(jn, chip_b, to):
            return pltpu.make_async_remote_copy(src_ref=conv_out.at[chip_b], dst_ref=conv_out.at[chip_b],
                                                send_sem=send_sems.at[6 + jn], recv_sem=recv_sems.at[6 + jn],
                                                device_id=to, device_id_type=MESH)

        started = []
        for jn, chip in enumerate(chips):
            for cp in (copy(jn, b, c, (chip[0], chip[1], c)), conv_copy(jn, b, (chip[0], chip[1], c))):
                cp.start()
                started.append(cp)
        for jn, chip in enumerate(chips):
            cb = 2 * chip[0] + chip[1]
            copy(jn, cb, c, (chip[0], chip[1], c)).wait_recv()
            cp = copy(3 + jn, cb, c, sibling)
            cp.start()
            started.append(cp)
        for jn, chip in enumerate(chips):
            cb = 2 * chip[0] + chip[1]
            copy(3 + jn, cb, 1 - c, sibling).wait_recv()
            conv_copy(jn, cb, (chip[0], chip[1], c)).wait_recv()
        for cp in started:
            cp.wait_send()

    return pl.pallas_call(
        body, name="gather_w_in",
        in_specs=[ANY, ANY], out_specs=[ANY, ANY],
        out_shape=[jax.ShapeDtypeStruct(land.shape, land.dtype), jax.ShapeDtypeStruct(conv_land.shape, conv_land.dtype)],
        input_output_aliases={0: 0, 1: 1},
        scratch_shapes=[pltpu.SemaphoreType.DMA((9,)), pltpu.SemaphoreType.DMA((9,))],
        compiler_params=pltpu.CompilerParams(has_side_effects=True),
    )(land, conv_land)


def _weight_copies(shards, lands, send_sems, recv_sems, arrivals):
    x, y, c = _place()
    b = 2 * x + y
    cps = []
    for jn, chip in enumerate(_other_chips(x, y)):
        at = 2 * chip[0] + chip[1] if arrivals else b
        for k, (shard, land) in enumerate(zip(shards, lands)):
            rows_k = SHARD_ROWS[k + 1]
            cps.append(pltpu.make_async_remote_copy(
                src_ref=shard, dst_ref=land.at[pl.ds(pl.multiple_of(at * rows_k, 16), rows_k)],
                send_sem=send_sems.at[3 * jn + k], recv_sem=recv_sems.at[3 * jn + k],
                device_id=(chip[0], chip[1], c), device_id_type=MESH))
    return cps


def _weights_start(shards, lands):
    def body(s0, s1, s2, l0, l1, l2, send_sems, recv_sems, *outs):
        for send in _weight_copies((s0, s1, s2), (l0, l1, l2), send_sems, recv_sems, False):
            send.start()
        outs[-1][...] = jnp.zeros_like(outs[-1])

    res = pl.pallas_call(
        body, name="weights_start",
        in_specs=[HBM] * 6, out_specs=[SEM, SEM] + [HBM] * 6 + [VMEM],
        out_shape=[pltpu.SemaphoreType.DMA((9,)), pltpu.SemaphoreType.DMA((9,))]
        + [pltpu.HBM(a.shape, a.dtype) for a in (*shards, *lands)] + [jax.ShapeDtypeStruct((8, 128), F32)],
        input_output_aliases={i: i + 2 for i in range(6)},
        compiler_params=pltpu.CompilerParams(has_side_effects=DATAFLOW),
    )(*[_hbm(a) for a in (*shards, *lands)])
    return res[0], res[1], res[2:5], res[5:8], res[8]


def _weights_wait(send_sems, recv_sems, shards, lands, after):
    def body(s0, s1, s2, l0, l1, l2, send_sems, recv_sems, after_ref, *outs):
        for cp in _weight_copies((s0, s1, s2), (l0, l1, l2), send_sems, recv_sems, True):
            cp.wait_send()
            cp.wait_recv()

    res = pl.pallas_call(
        body, name="weights_wait",
        in_specs=[HBM] * 6 + [SEM, SEM, ANY], out_specs=[HBM] * 6,
        out_shape=[pltpu.HBM(a.shape, a.dtype) for a in (*shards, *lands)],
        input_output_aliases={i: i for i in range(6)},
        compiler_params=pltpu.CompilerParams(has_side_effects=DATAFLOW),
    )(*shards, *lands, send_sems, recv_sems, after)
    return res[3:6]


def _grad_copies(g_ref, land_ref, send_sems, recv_sems):
    x, y, c = _place()
    cps = []
    for d in range(1, 8):
        px, py, pc = x ^ (d >> 2), y ^ ((d >> 1) & 1), c ^ (d & 1)
        cps.append(pltpu.make_async_remote_copy(
            src_ref=g_ref.at[2 * px + py, pc], dst_ref=land_ref.at[d - 1], send_sem=send_sems.at[d - 1],
            recv_sem=recv_sems.at[d - 1], device_id=(px, py, pc), device_id_type=MESH))
    return cps


def _grads_start(grad_b, name):
    h = grad_b.shape[2]

    def body(g_ref, land_ref, send_sems, recv_sems, g_thru, land_thru, token):
        for cp in _grad_copies(g_ref, land_ref, send_sems, recv_sems):
            cp.start()
        token[...] = jnp.zeros_like(token)

    return pl.pallas_call(
        body, name=name,
        in_specs=[HBM, HBM], out_specs=[SEM, SEM, HBM, HBM, VMEM],
        out_shape=[pltpu.SemaphoreType.DMA((7,)), pltpu.SemaphoreType.DMA((7,)), pltpu.HBM(grad_b.shape, BF16),
                   pltpu.HBM((7, h, D), BF16), jax.ShapeDtypeStruct((8, 128), F32)],
        input_output_aliases={0: 2, 1: 3},
        compiler_params=pltpu.CompilerParams(has_side_effects=DATAFLOW),
    )(_hbm(grad_b), _hbm(lax.empty((7, h, D), BF16)))


def _grads_wait(started, after):
    n = len(started)

    def body(*refs):
        g, land = refs[:n], refs[n:2 * n]
        send_sems, recv_sems = refs[2 * n:3 * n], refs[3 * n:4 * n]
        for k in range(n):
            for cp in _grad_copies(g[k], land[k], send_sems[k], recv_sems[k]):
                cp.wait_send()
                cp.wait_recv()

    gs = [st[2] for st in started]
    lands = [st[3] for st in started]
    res = pl.pallas_call(
        body, name="grads_wait",
        in_specs=[HBM] * (2 * n) + [SEM] * (2 * n) + [ANY], out_specs=[HBM] * (2 * n),
        out_shape=[pltpu.HBM(a.shape, a.dtype) for a in (*gs, *lands)],
        input_output_aliases={i: i for i in range(2 * n)},
        compiler_params=pltpu.CompilerParams(has_side_effects=DATAFLOW),
    )(*gs, *lands, *[st[0] for st in started], *[st[1] for st in started], after)
    return res[n:]


def _sum_partials(grad4, got, cb, name, tr):
    h = grad4.shape[2]
    per_half = h // tr

    def body(cb_ref, g_ref, o_ref, out_ref):
        acc = g_ref[...]
        for j in range(7):
            acc = acc + o_ref[j].astype(F32)
        out_ref[...] = acc

    return pl.pallas_call(
        body, name=name,
        grid_spec=pltpu.PrefetchScalarGridSpec(
            num_scalar_prefetch=1, grid=(per_half,),
            in_specs=[pl.BlockSpec((None, None, tr, D), lambda i, cb_ref: (cb_ref[1], cb_ref[0], i, 0)),
                      pl.BlockSpec((7, tr, D), lambda i, cb_ref: (0, i, 0))],
            out_specs=pl.BlockSpec((tr, D), lambda i, cb_ref: (cb_ref[0] * per_half + i, 0))),
        out_shape=jax.ShapeDtypeStruct((2 * h, D), F32),
        compiler_params=_cp(("arbitrary",)),
    )(cb, grad4, got)


def _share_halves(shards, small):
    n = len(shards)
    rows = small.shape[0]

    def body(*refs):
        small_ref = refs[n]
        out, total_ref = refs[n + 1:2 * n + 1], refs[2 * n + 1]
        all_ref, send_sems, recv_sems, ssend, srecv = refs[2 * n + 2:]
        x, y, c = _place()
        me = 4 * x + 2 * y + c
        cps = []
        for k in range(n):
            h = shards[k].shape[0] // 2
            mine = out[k].at[pl.ds(pl.multiple_of(c * h, 8), h)]
            cp = pltpu.make_async_remote_copy(src_ref=mine, dst_ref=mine, send_sem=send_sems.at[k],
                                              recv_sem=recv_sems.at[k], device_id=(x, y, 1 - c), device_id_type=MESH)
            cp.start()
            cps.append(cp)
        all_ref[me] = small_ref[...]
        peers = []
        for d in range(1, 8):
            px, py, pc = x ^ (d >> 2), y ^ ((d >> 1) & 1), c ^ (d & 1)
            cp = pltpu.make_async_remote_copy(src_ref=small_ref, dst_ref=all_ref.at[me],
                                              send_sem=ssend.at[d - 1], recv_sem=srecv.at[d - 1],
                                              device_id=(px, py, pc), device_id_type=MESH)
            cp.start()
            peers.append(cp)
        for cp in peers:
            cp.wait()
        acc = all_ref[0]
        for d in range(1, 8):
            acc = acc + all_ref[d]
        total_ref[...] = acc
        for cp in cps:
            cp.wait()

    return pl.pallas_call(
        body, name="share_halves",
        in_specs=[ANY] * n + [VMEM], out_specs=[ANY] * n + [VMEM],
        out_shape=[jax.ShapeDtypeStruct(sh.shape, F32) for sh in shards] + [jax.ShapeDtypeStruct((rows, D), F32)],
        input_output_aliases={k: k for k in range(n)},
        scratch_shapes=[pltpu.VMEM((8, rows, D), F32), pltpu.SemaphoreType.DMA((n,)), pltpu.SemaphoreType.DMA((n,)),
                        pltpu.SemaphoreType.DMA((7,)), pltpu.SemaphoreType.DMA((7,))],
        compiler_params=pltpu.CompilerParams(has_side_effects=True),
    )(*shards, small)


def _adamw(w, g, m, v, name, tr):
    rows, cols = w.shape

    def body(w_ref, g_ref, m_ref, v_ref, d_ref, nm_ref, nv_ref):
        g_ = g_ref[...]
        nm = ADAM_B1 * m_ref[...] + (1.0 - ADAM_B1) * g_
        nv = ADAM_B2 * v_ref[...] + (1.0 - ADAM_B2) * (g_ * g_)
        m_hat = nm / (1.0 - ADAM_B1 ** ADAM_STEP)
        v_hat = nv / (1.0 - ADAM_B2 ** ADAM_STEP)
        d_ref[...] = -ADAM_LR * (m_hat / (jnp.sqrt(v_hat) + ADAM_EPS) + ADAM_WD * w_ref[...])
        nm_ref[...] = nm
        nv_ref[...] = nv

    spec = pl.BlockSpec((tr, cols), lambda i: (i, 0))
    return pl.pallas_call(
        body, name=name, grid=(rows // tr,),
        in_specs=[spec] * 4, out_specs=[spec] * 3,
        out_shape=[jax.ShapeDtypeStruct((rows, cols), F32)] * 3,
        compiler_params=_cp(("parallel",)),
    )(w, g, m, v)


def _local_step(x, target, w_in_t, late_weights, norm_a_g, norm_b_g, sinks_a, ln1_g, ln1_b,
                conv_w, conv_b, ln2_g, ln2_b, slopes, on_grad):
    cwb = jnp.concatenate([conv_w, conv_b[None]], axis=0).reshape(4, 2, FF)

    proj = _proj(x, w_in_t, "proj")
    o_a, lse_a = _attn_a_fwd(proj, sinks_a)
    fwd_b = [_attn_b_fwd(proj, slopes, r) for r in B_DILATIONS]
    w_o, w_up_t, w_down = late_weights(fwd_b[-1][1])
    w_up3 = w_up_t.reshape(2, FF, D)
    o_b, lse_b, cat, z1, h1, h1b = _mix_ln1(x, o_a, [f[0] for f in fwd_b], [f[1] for f in fwd_b],
                                           norm_a_g, norm_b_g, w_o, ln1_g, ln1_b)
    up, a = _up_conv_gelu(h1b, w_up3, cwb)
    dz2, dz2b, st2 = _down_ln2_loss(a, w_down, h1, target, ln2_g, ln2_b)

    tok = on_grad(3, *_grad_w(a, dz2b, "grad_w_down", tm=FF // 2))
    dup, dconv = _conv_gelu_bwd(dz2b, w_down, up, cwb + tok[0, 0])
    tok = on_grad(2, *_grad_w(dup, h1b, "grad_w_up", tm=FF // 2, lhs_halves=True))
    dz1, dz1b, st1 = _dh1_ln1_bwd(dz2, dup, w_up3, z1, ln1_g + tok[0, 0])
    tok = on_grad(1, *_grad_w(cat, dz1b, "grad_w_o", tm=512))
    d_oa, d_ob, st_n = _dcat_rms_bwd(dz1b, w_o, o_a, o_b, norm_a_g + tok[0, 0], norm_b_g)
    dqa, dka, dva, dsink = _attn_a_bwd(proj, sinks_a, d_oa, o_a, lse_a)
    bwd_b = [_attn_b_bwd(proj, slopes, d_ob, o_b, lse_b, r) for r in B_DILATIONS]
    dproj = _dproj_combine(dqa, dka, dva, bwd_b)
    tok = on_grad(0, *_grad_w(dproj, x, "grad_w_in", tm=WA))
    gx = _grad_x(dz1, dproj, w_in_t, tok)

    dconv = dconv.reshape(4, 2 * FF)
    small = dict(loss=st2[2, 0:1], norm_a_g=st_n[0], norm_b_g=st_n[1], sinks_a=dsink[:, 0],
                 ln1_g=st1[0], ln1_b=st1[1], conv_w=dconv[0:3].reshape(-1), conv_b=dconv[3],
                 ln2_g=st2[0], ln2_b=st2[1])
    return gx, small


SMALL_ORDER = ("loss", "norm_a_g", "norm_b_g", "sinks_a", "ln1_g", "ln1_b", "conv_b", "ln2_g", "ln2_b", "conv_w")
SMALL_SIZES = dict(loss=1, norm_a_g=512, norm_b_g=512, sinks_a=8, ln1_g=D, ln1_b=D, conv_b=2 * FF, ln2_g=D, ln2_b=D,
                   conv_w=3 * 2 * FF)


def _pack(parts, rows):
    flat = jnp.concatenate([parts[k].reshape(-1).astype(F32) for k in parts])
    return jnp.pad(flat, (0, rows * D - flat.shape[0])).reshape(rows, D)


def _unpack(buf, names, sizes):
    flat = buf.reshape(-1)
    out, at = {}, 0
    for k in names:
        out[k] = flat[at:at + sizes[k]]
        at += sizes[k]
    return out


def kernel(x, w_in, norm_a_g, norm_b_g, sinks_a, w_o, ln1_g, ln1_b, w_up, conv_w, conv_b, w_down, ln2_g, ln2_b, loss_target, m_w_in, m_norm_a_g, m_norm_b_g, m_sinks_a, m_w_o, m_ln1_g, m_ln1_b, m_w_up, m_conv_w, m_conv_b, m_w_down, m_ln2_g, m_ln2_b, v_w_in, v_norm_a_g, v_norm_b_g, v_sinks_a, v_w_o, v_ln1_g, v_ln1_b, v_w_up, v_conv_w, v_conv_b, v_w_down, v_ln2_g, v_ln2_b):
    xi, yi, ci = _place()
    chip = (2 * xi + yi).astype(I32)
    core = ci.astype(I32)

    shards = (w_in.T.astype(BF16), w_o.astype(BF16), w_up.T.astype(BF16), w_down.astype(BF16))
    lands = [lax.dynamic_update_slice(jnp.zeros((N_CHIPS * r, D), BF16), sh, (chip * r, 0))
             for sh, r in zip(shards, SHARD_ROWS)]
    conv_land = lax.dynamic_update_slice(jnp.zeros((N_CHIPS,) + conv_w.shape, F32), conv_w[None], (chip, 0, 0))
    w_in_t, conv_w4 = _gather_w_in(lands[0], conv_land)
    conv_w_f = conv_w4.transpose(1, 0, 2).reshape(3, 2 * FF)
    w_send, w_recv, w_shards, w_lands, w_tok = _weights_start(shards[1:], lands[1:])
    slopes = jnp.asarray(SLOPES, F32) + w_tok[0, 0]

    halves_rows = [r // 2 for r in SHARD_ROWS]
    grads4, started = [None] * 4, [None] * 4

    def on_grad(k, g, g_b):
        grads4[k] = g.reshape(N_CHIPS, 2, halves_rows[k], D)
        started[k] = _grads_start(g_b.reshape(N_CHIPS, 2, halves_rows[k], D), f"grads_start_{k}")
        return started[k][4]

    gx, small = _local_step(
        x[0], loss_target[0], w_in_t, lambda after: _weights_wait(w_send, w_recv, w_shards, w_lands, after),
        norm_a_g, norm_b_g, sinks_a, ln1_g, ln1_b, conv_w_f, conv_b, ln2_g, ln2_b, slopes, on_grad)

    got = _grads_wait(started, gx)
    tiles = (96, 128, 352, 176)
    core_chip = jnp.stack([core, chip])
    halves = [_sum_partials(grads4[k], got[k], core_chip, f"sum_partials_{k}", tiles[k]) for k in range(4)]
    small_rows = 32
    *full, totals = _share_halves(halves, _pack({k: small[k] for k in SMALL_ORDER}, small_rows))
    tot = _unpack(totals, SMALL_ORDER, SMALL_SIZES)

    g_w_in, g_w_o, g_w_up, g_w_down = full[0].T, full[1], full[2].T, full[3]
    loss = tot["loss"][0]
    cols = 2 * FF // N_CHIPS
    g_conv_w = lax.dynamic_slice(tot["conv_w"].reshape(3, 2 * FF), (0, chip * cols), (3, cols))
    g_small = dict(norm_a_g=tot["norm_a_g"], norm_b_g=tot["norm_b_g"], sinks_a=tot["sinks_a"], ln1_g=tot["ln1_g"],
                   ln1_b=tot["ln1_b"], conv_w=g_conv_w, conv_b=tot["conv_b"], ln2_g=tot["ln2_g"], ln2_b=tot["ln2_b"])

    weights = dict(w_in=w_in, norm_a_g=norm_a_g, norm_b_g=norm_b_g, sinks_a=sinks_a, w_o=w_o, ln1_g=ln1_g, ln1_b=ln1_b,
                   w_up=w_up, conv_w=conv_w, conv_b=conv_b, w_down=w_down, ln2_g=ln2_g, ln2_b=ln2_b)
    ms = dict(w_in=m_w_in, norm_a_g=m_norm_a_g, norm_b_g=m_norm_b_g, sinks_a=m_sinks_a, w_o=m_w_o, ln1_g=m_ln1_g,
              ln1_b=m_ln1_b, w_up=m_w_up, conv_w=m_conv_w, conv_b=m_conv_b, w_down=m_w_down, ln2_g=m_ln2_g, ln2_b=m_ln2_b)
    vs = dict(w_in=v_w_in, norm_a_g=v_norm_a_g, norm_b_g=v_norm_b_g, sinks_a=v_sinks_a, w_o=v_w_o, ln1_g=v_ln1_g,
              ln1_b=v_ln1_b, w_up=v_w_up, conv_w=v_conv_w, conv_b=v_conv_b, w_down=v_w_down, ln2_g=v_ln2_g, ln2_b=v_ln2_b)
    order = list(weights)
    grad = dict(g_small, w_in=g_w_in, w_o=g_w_o, w_up=g_w_up, w_down=g_w_down)

    delta, new_m, new_v = {}, {}, {}
    for k, tr in (("w_in", 256), ("w_o", 128), ("w_up", 256), ("w_down", 176)):
        delta[k], new_m[k], new_v[k] = _adamw(weights[k], grad[k], ms[k], vs[k], f"adamw_{k}", tr)
    small_names = [k for k in order if k not in delta]
    sizes = {k: weights[k].size for k in small_names}
    rows = 16
    packed = [_pack({k: src[k] for k in small_names}, rows) for src in (weights, grad, ms, vs)]
    for res, buf in zip((delta, new_m, new_v), _adamw(*packed, "adamw_small", rows)):
        for k, val in _unpack(buf, small_names, sizes).items():
            res[k] = val.reshape(weights[k].shape)

    return (loss, gx[None], *[grad[k] for k in order], *[delta[k] for k in order],
            *[new_m[k] for k in order], *[new_v[k] for k in order])
```

```python
import functools
import math

import jax
import jax.numpy as jnp
from jax import lax
from jax.experimental import pallas as pl
from jax.experimental.pallas import tpu as pltpu

F32, BF16, I32 = jnp.float32, jnp.bfloat16, jnp.int32

D = 1024
FF = 2816
HD = 64
NH = 8
WA, WB = 768, 1536
WIN = WA + WB
BLK = 128
ALPHA = 2.0 ** 0.25
LN_EPS, RMS_EPS = 1e-5, 1e-6
SCALE = 1.0 / math.sqrt(HD)
A_MAX_DIST, B_MAX_DIST = 127, 128
B_DILATIONS = (1, 4, 16)
SLOPES = tuple(2.0 ** (-(i + 1)) for i in range(NH))
SHARD_ROWS = (WIN // 4, D // 4, 2 * FF // 4, FF // 4)
N_CHIPS = 4
ADAM_LR, ADAM_B1, ADAM_B2, ADAM_EPS, ADAM_WD, ADAM_STEP = 0.001, 0.9, 0.999, 1e-08, 0.01, 10
MESH = pl.DeviceIdType.MESH
ANY = pl.BlockSpec(memory_space=pl.ANY)
SMEM = pl.BlockSpec(memory_space=pltpu.SMEM)
VMEM = pl.BlockSpec(memory_space=pltpu.VMEM)
HBM = pl.BlockSpec(memory_space=pltpu.HBM)
SEM = pl.BlockSpec(memory_space=pltpu.SEMAPHORE)
DATAFLOW = pltpu.SideEffectType.DATAFLOW_SIDE_EFFECTING


def _cp(sem, mb=48):
    return pltpu.CompilerParams(dimension_semantics=sem, vmem_limit_bytes=mb << 20)


def _nn(a, b):
    return lax.dot_general(a, b, (((1,), (0,)), ((), ())), preferred_element_type=F32)


def _nt(a, b):
    return lax.dot_general(a, b, (((1,), (1,)), ((), ())), preferred_element_type=F32)


def _tn(a, b):
    return lax.dot_general(a, b, (((0,), (0,)), ((), ())), preferred_element_type=F32)


def _resident(shape):
    n = len(shape)
    return pl.BlockSpec(shape, lambda *_: (0,) * n, pipeline_mode=pl.Buffered(1))


def _const(shape):
    n = len(shape)
    return pl.BlockSpec(shape, lambda *_: (0,) * n)


def _proj(x, w_t, name, tm=512):
    s = x.shape[0]
    n = w_t.shape[0]

    def body(x_ref, w_ref, o_ref):
        o_ref[...] = _nt(x_ref[...].astype(BF16), w_ref[...])

    return pl.pallas_call(
        body, name=name, grid=(s // tm,),
        in_specs=[pl.BlockSpec((tm, D), lambda i: (i, 0)), _resident((n, D))],
        out_specs=pl.BlockSpec((tm, n), lambda i: (i, 0)),
        out_shape=jax.ShapeDtypeStruct((s, n), F32),
        compiler_params=_cp(("parallel",)),
    )(x, w_t)


def _grad_w(lhs, rhs, name, tm, tk=512, lhs_halves=False):
    s = rhs.shape[0]
    if lhs_halves:
        per_half = lhs.shape[2] // tm
        n = 2 * lhs.shape[2]
        lhs_spec = pl.BlockSpec((None, tk, tm), lambda i, k: (i // per_half, k, i % per_half))
    else:
        n = lhs.shape[1]
        lhs_spec = pl.BlockSpec((tk, tm), lambda i, k: (k, i))
    nk = s // tk

    def body(l_ref, r_ref, o_ref, ob_ref):
        k = pl.program_id(1)

        @pl.when(k == 0)
        def _():
            o_ref[...] = jnp.zeros_like(o_ref)

        o_ref[...] += _tn(l_ref[...].astype(BF16), r_ref[...].astype(BF16))

        @pl.when(k == nk - 1)
        def _():
            ob_ref[...] = o_ref[...].astype(BF16)

    return pl.pallas_call(
        body, name=name, grid=(n // tm, nk),
        in_specs=[lhs_spec, pl.BlockSpec((tk, D), lambda i, k: (k, 0))],
        out_specs=[pl.BlockSpec((tm, D), lambda i, k: (i, 0))] * 2,
        out_shape=[jax.ShapeDtypeStruct((n, D), F32), jax.ShapeDtypeStruct((n, D), BF16)],
        compiler_params=_cp(("parallel", "arbitrary")),
    )(lhs, rhs)


def _band_base(max_dist, dist_unit, first):
    row = lax.broadcasted_iota(I32, (BLK, 2 * BLK), 0)
    col = lax.broadcasted_iota(I32, (BLK, 2 * BLK), 1)
    dist = BLK + row - col
    ok = (dist >= 0) & (dist <= max_dist)
    if first:
        ok = ok & (col >= BLK)
    return jnp.where(ok, dist.astype(F32) * (-float(dist_unit)), -jnp.inf)


def _half_mask(shape, e):
    lane = lax.broadcasted_iota(I32, shape, 1)
    return (lane < HD) if e == 0 else (lane >= HD)


def _to_half(x, e, g):
    if g != e:
        x = pltpu.roll(x, HD, 1)
    return jnp.where(_half_mask(x.shape, g), x, 0.0)


def _pair_fwd(q2, kb, vb, base, slopes, kv_heads, sinks):
    lo = _half_mask((BLK, 2 * HD), 0)
    o2 = lse2 = None
    for e in (0, 1):
        g = kv_heads[e]
        qv = (_to_half(q2, e, g) * SCALE).astype(BF16)
        s = _nt(qv, kb) + slopes[e] * base
        m = jnp.max(s, axis=1, keepdims=True)
        if sinks is not None:
            m = jnp.maximum(m, sinks[e])
        p = jnp.exp(s - m)
        l = jnp.sum(p, axis=1, keepdims=True)
        if sinks is not None:
            l = l + jnp.exp(sinks[e] - m)
        oh = _nn(p.astype(BF16), vb) / l
        if g != e:
            oh = pltpu.roll(oh, HD, 1)
        lse = jnp.broadcast_to(m + jnp.log(l), (BLK, 2 * HD))
        o2 = oh if e == 0 else jnp.where(lo, o2, oh)
        lse2 = lse if e == 0 else jnp.where(lo, lse2, lse)
    return o2, lse2


def _pair_bwd(q2, kb, vb, do2, o2, lse2, base, slopes, kv_heads, sinks):
    lo = _half_mask((BLK, 2 * HD), 0)
    dq2 = dk2 = dv2 = None
    prod = do2 * o2
    dsinks = []
    for e in (0, 1):
        g = kv_heads[e]
        hq = _half_mask((BLK, 2 * HD), e)
        lse = jnp.max(jnp.where(hq, lse2, -jnp.inf), axis=1, keepdims=True)
        delta = jnp.sum(jnp.where(hq, prod, 0.0), axis=1, keepdims=True)
        qv = (_to_half(q2, e, g) * SCALE).astype(BF16)
        dov = _to_half(do2, e, g).astype(BF16)
        p = jnp.exp(_nt(qv, kb) + slopes[e] * base - lse)
        ds = (p * (_nt(dov, vb) - delta)).astype(BF16)
        dqh = _nn(ds, kb) * SCALE
        if g != e:
            dqh = pltpu.roll(dqh, HD, 1)
        dq2 = dqh if e == 0 else jnp.where(lo, dq2, dqh)
        dkh = _tn(ds, qv)
        dvh = _tn(p.astype(BF16), dov)
        dk2 = dkh if e == 0 else dk2 + dkh
        dv2 = dvh if e == 0 else dv2 + dvh
        if sinks is not None:
            dsinks.append(jnp.sum(-jnp.exp(sinks[e] - lse) * delta, axis=0, keepdims=True))
    return dq2, dk2, dv2, dsinks


def _attn_a_fwd(proj, sinks):
    s = proj.shape[0]
    nb = s // BLK

    def body(sink_ref, q_ref, kp_ref, kc_ref, vp_ref, vc_ref, o_ref, lse_ref):
        n = pl.program_id(0)
        base = jnp.where(n > 0, _band_base(A_MAX_DIST, 1, False), _band_base(A_MAX_DIST, 1, True))
        kb = jnp.concatenate([kp_ref[...], kc_ref[...]], axis=0).astype(BF16)
        vb = jnp.concatenate([vp_ref[...], vc_ref[...]], axis=0).astype(BF16)
        for j in range(NH // 2):
            g = j // 2
            o2, lse2 = _pair_fwd(q_ref[:, 128 * j:128 * (j + 1)], kb, vb, base, (SLOPES[2 * j], SLOPES[2 * j + 1]),
                                 (g, g), (sink_ref[2 * j], sink_ref[2 * j + 1]))
            o_ref[:, 128 * j:128 * (j + 1)] = o2
            lse_ref[:, 128 * j:128 * (j + 1)] = lse2

    prev = lambda n: jnp.maximum(n - 1, 0)
    return pl.pallas_call(
        body, name="attn_a_fwd", grid=(nb,),
        in_specs=[SMEM,
                  pl.BlockSpec((BLK, 512), lambda n: (n, 0)),
                  pl.BlockSpec((BLK, 128), lambda n: (prev(n), 4)), pl.BlockSpec((BLK, 128), lambda n: (n, 4)),
                  pl.BlockSpec((BLK, 128), lambda n: (prev(n), 5)), pl.BlockSpec((BLK, 128), lambda n: (n, 5))],
        out_specs=[pl.BlockSpec((BLK, 512), lambda n: (n, 0))] * 2,
        out_shape=[jax.ShapeDtypeStruct((s, 512), F32)] * 2,
        compiler_params=_cp(("parallel",)),
    )(sinks, proj, proj, proj, proj, proj)


def _attn_a_bwd(proj, sinks, d_o, o, lse):
    s = proj.shape[0]
    nb = s // BLK

    def body(sink_ref, q_ref, kp_ref, kc_ref, vp_ref, vc_ref, do_ref, o_ref, lse_ref,
             dq_ref, dk_ref, dv_ref, dsink_ref, kcar, vcar):
        n = pl.program_id(0)

        @pl.when(n == 0)
        def _():
            kcar[...] = jnp.zeros_like(kcar)
            vcar[...] = jnp.zeros_like(vcar)
            dsink_ref[...] = jnp.zeros_like(dsink_ref)

        @pl.when(n < nb)
        def _():
            base = jnp.where(n > 0, _band_base(A_MAX_DIST, 1, False), _band_base(A_MAX_DIST, 1, True))
            kb = jnp.concatenate([kp_ref[...], kc_ref[...]], axis=0).astype(BF16)
            vb = jnp.concatenate([vp_ref[...], vc_ref[...]], axis=0).astype(BF16)
            dk_win = dv_win = None
            for j in range(NH // 2):
                g = j // 2
                sl = slice(128 * j, 128 * (j + 1))
                dq2, dk2, dv2, dsk = _pair_bwd(q_ref[:, sl], kb, vb, do_ref[:, sl], o_ref[:, sl], lse_ref[:, sl], base,
                                               (SLOPES[2 * j], SLOPES[2 * j + 1]), (g, g),
                                               (sink_ref[2 * j], sink_ref[2 * j + 1]))
                dq_ref[:, sl] = dq2
                dk_win = dk2 if j == 0 else dk_win + dk2
                dv_win = dv2 if j == 0 else dv_win + dv2
                for e in (0, 1):
                    h = 2 * j + e
                    dsink_ref[h:h + 1, :] += jnp.broadcast_to(dsk[e], (1, 128))
            dk_ref[...] = kcar[...] + dk_win[:BLK]
            dv_ref[...] = vcar[...] + dv_win[:BLK]
            kcar[...] = dk_win[BLK:]
            vcar[...] = dv_win[BLK:]

        @pl.when(n == nb)
        def _():
            dk_ref[...] = kcar[...]
            dv_ref[...] = vcar[...]

    cur = lambda n: jnp.minimum(n, nb - 1)
    prev = lambda n: jnp.maximum(cur(n) - 1, 0)
    out_prev = lambda n: jnp.maximum(n - 1, 0)
    return pl.pallas_call(
        body, name="attn_a_bwd", grid=(nb + 1,),
        in_specs=[SMEM,
                  pl.BlockSpec((BLK, 512), lambda n: (cur(n), 0)),
                  pl.BlockSpec((BLK, 128), lambda n: (prev(n), 4)), pl.BlockSpec((BLK, 128), lambda n: (cur(n), 4)),
                  pl.BlockSpec((BLK, 128), lambda n: (prev(n), 5)), pl.BlockSpec((BLK, 128), lambda n: (cur(n), 5)),
                  pl.BlockSpec((BLK, 512), lambda n: (cur(n), 0)),
                  pl.BlockSpec((BLK, 512), lambda n: (cur(n), 0)),
                  pl.BlockSpec((BLK, 512), lambda n: (cur(n), 0))],
        out_specs=[pl.BlockSpec((BLK, 512), lambda n: (cur(n), 0)),
                   pl.BlockSpec((BLK, 128), lambda n: (out_prev(n), 0)),
                   pl.BlockSpec((BLK, 128), lambda n: (out_prev(n), 0)),
                   pl.BlockSpec((NH, 128), lambda n: (0, 0))],
        out_shape=[jax.ShapeDtypeStruct((s, 512), F32), jax.ShapeDtypeStruct((s, 128), F32),
                   jax.ShapeDtypeStruct((s, 128), F32), jax.ShapeDtypeStruct((NH, 128), F32)],
        scratch_shapes=[pltpu.VMEM((BLK, 128), F32), pltpu.VMEM((BLK, 128), F32)],
        compiler_params=_cp(("arbitrary",)),
    )(sinks, proj, proj, proj, proj, proj, d_o, o, lse)


def _stream(rho, i, r):
    start = i * BLK * r + rho
    return pl.ds(start, BLK, stride=r) if r > 1 else pl.ds(start, BLK)


def _for_streams(r, fn):
    if r <= 4:
        for rho in range(r):
            fn(rho)
    else:
        def four(it, carry):
            for u in range(4):
                fn(4 * it + u)
            return carry

        lax.fori_loop(0, r // 4, four, 0)


B_BLOCKS_PER_STEP = {1: 4, 4: 1, 16: 1}


def _attn_b_fwd(proj, slopes, r):
    s = proj.shape[0]
    nq = B_BLOCKS_PER_STEP[r]
    rows = BLK * r * nq
    steps = s // rows
    qc, kc, vc = WA // 128, WA // 128 + 4, WA // 128 + 8

    def body(slope_ref, q_ref, kp_ref, kc_ref, vp_ref, vc_ref, o_ref, lse_ref):
        j = pl.program_id(0)
        sb = pl.program_id(1)
        base_rest = _band_base(B_MAX_DIST, r, False)
        base_0 = jnp.where(sb > 0, base_rest, _band_base(B_MAX_DIST, r, True))
        sl2 = (slope_ref[2 * j], slope_ref[2 * j + 1])

        def stream(rho):
            for i in range(nq):
                cur = _stream(rho, i, r)
                k_prev = kc_ref[_stream(rho, i - 1, r), :] if i > 0 else kp_ref[_stream(rho, 0, r), :]
                v_prev = vc_ref[_stream(rho, i - 1, r), :] if i > 0 else vp_ref[_stream(rho, 0, r), :]
                kb = jnp.concatenate([k_prev, kc_ref[cur, :]], axis=0).astype(BF16)
                vb = jnp.concatenate([v_prev, vc_ref[cur, :]], axis=0).astype(BF16)
                o2, lse2 = _pair_fwd(q_ref[cur, :], kb, vb, base_rest if i > 0 else base_0, sl2, (0, 1), None)
                o_ref[cur, :] = o2
                lse_ref[cur, :] = lse2

        _for_streams(r, stream)

    before = lambda sb: jnp.maximum(sb * nq - 1, 0)
    return pl.pallas_call(
        body, name=f"attn_b_fwd_r{r}", grid=(NH // 2, steps),
        in_specs=[SMEM,
                  pl.BlockSpec((rows, 128), lambda j, sb: (sb, qc + j)),
                  pl.BlockSpec((BLK * r, 128), lambda j, sb: (before(sb), kc + j)),
                  pl.BlockSpec((rows, 128), lambda j, sb: (sb, kc + j)),
                  pl.BlockSpec((BLK * r, 128), lambda j, sb: (before(sb), vc + j)),
                  pl.BlockSpec((rows, 128), lambda j, sb: (sb, vc + j))],
        out_specs=[pl.BlockSpec((rows, 128), lambda j, sb: (sb, j))] * 2,
        out_shape=[jax.ShapeDtypeStruct((s, 512), F32)] * 2,
        compiler_params=_cp(("parallel", "parallel")),
    )(slopes, proj, proj, proj, proj, proj)


def _attn_b_bwd(proj, slopes, d_o, o, lse, r):
    s = proj.shape[0]
    nq = B_BLOCKS_PER_STEP[r]
    rows = BLK * r * nq
    steps = s // rows
    qc, kc, vc = WA // 128, WA // 128 + 4, WA // 128 + 8

    def body(slope_ref, q_ref, kp_ref, kc_ref, vp_ref, vc_ref, do_ref, o_ref, lse_ref,
             dq_ref, dk_ref, dv_ref, kcar, vcar):
        j = pl.program_id(0)
        sb = pl.program_id(1)

        @pl.when(sb == 0)
        def _():
            kcar[...] = jnp.zeros_like(kcar)
            vcar[...] = jnp.zeros_like(vcar)

        dk_ref[...] = kcar[...]
        dv_ref[...] = vcar[...]

        @pl.when(sb < steps)
        def _():
            base_rest = _band_base(B_MAX_DIST, r, False)
            base_0 = jnp.where(sb > 0, base_rest, _band_base(B_MAX_DIST, r, True))
            sl2 = (slope_ref[2 * j], slope_ref[2 * j + 1])

            def stream(rho):
                for i in range(nq):
                    cur = _stream(rho, i, r)
                    k_prev = kc_ref[_stream(rho, i - 1, r), :] if i > 0 else kp_ref[_stream(rho, 0, r), :]
                    v_prev = vc_ref[_stream(rho, i - 1, r), :] if i > 0 else vp_ref[_stream(rho, 0, r), :]
                    kb = jnp.concatenate([k_prev, kc_ref[cur, :]], axis=0).astype(BF16)
                    vb = jnp.concatenate([v_prev, vc_ref[cur, :]], axis=0).astype(BF16)
                    dq2, dk2, dv2, _ = _pair_bwd(q_ref[cur, :], kb, vb, do_ref[cur, :], o_ref[cur, :], lse_ref[cur, :],
                                                 base_rest if i > 0 else base_0, sl2, (0, 1), None)
                    dq_ref[cur, :] = dq2
                    if i == 0:
                        last = _stream(rho, nq - 1, r)
                        dk_ref[last, :] += dk2[:BLK]
                        dv_ref[last, :] += dv2[:BLK]
                    else:
                        kcar[_stream(rho, i - 1, r), :] += dk2[:BLK]
                        vcar[_stream(rho, i - 1, r), :] += dv2[:BLK]
                    kcar[cur, :] = dk2[BLK:]
                    vcar[cur, :] = dv2[BLK:]

            _for_streams(r, stream)

    cur_step = lambda sb: jnp.minimum(sb, steps - 1)
    before = lambda sb: jnp.maximum(cur_step(sb) * nq - 1, 0)
    out_prev = lambda sb: jnp.maximum(sb - 1, 0)
    tile = lambda col: pl.BlockSpec((rows, 128), lambda j, sb: (cur_step(sb), col + j))
    edge = lambda col: pl.BlockSpec((BLK * r, 128), lambda j, sb: (before(sb), col + j))
    return pl.pallas_call(
        body, name=f"attn_b_bwd_r{r}", grid=(NH // 2, steps + 1),
        in_specs=[SMEM, tile(qc), edge(kc), tile(kc), edge(vc), tile(vc), tile(0), tile(0), tile(0)],
        out_specs=[tile(0),
                   pl.BlockSpec((rows, 128), lambda j, sb: (out_prev(sb), j)),
                   pl.BlockSpec((rows, 128), lambda j, sb: (out_prev(sb), j))],
        out_shape=[jax.ShapeDtypeStruct((s, 512), F32)] * 3,
        scratch_shapes=[pltpu.VMEM((rows, 128), F32), pltpu.VMEM((rows, 128), F32)],
        compiler_params=_cp(("parallel", "arbitrary")),
    )(slopes, proj, proj, proj, proj, proj, d_o, o, lse)


def _row(v):
    return v.reshape(1, -1)


def _layer_norm_stats(z):
    mu = jnp.mean(z, axis=-1, keepdims=True)
    zc = z - mu
    var = jnp.mean(zc * zc, axis=-1, keepdims=True)
    rstd = lax.rsqrt(var + LN_EPS)
    return zc * rstd, rstd


def _layer_norm_bwd(dh, zh, rstd, g):
    dzh = dh * g
    return rstd * (dzh - jnp.mean(dzh, axis=-1, keepdims=True) - zh * jnp.mean(dzh * zh, axis=-1, keepdims=True))


def _rms(o):
    return lax.rsqrt(jnp.mean(o * o, axis=-1, keepdims=True) + RMS_EPS)


def _mix_ln1(x, o_a, o_b, lse_b, norm_a_g, norm_b_g, w_o, ln1_g, ln1_b, tm=256):
    s = x.shape[0]

    def body(x_ref, oa_ref, ob1, ob2, ob3, l1, l2, l3, ga_ref, gb_ref, wo_ref, g_ref, b_ref,
             obm_ref, lse_ref, cat_ref, z1_ref, h1_ref, h1b_ref):
        la, lb, lc = l1[...], l2[...], l3[...]
        m = jnp.maximum(jnp.maximum(la, lb), lc)
        ea, eb, ec = jnp.exp(la - m), jnp.exp(lb - m), jnp.exp(lc - m)
        den = ea + eb + ec
        obm = (ea / den) * ob1[...] + (eb / den) * ob2[...] + (ec / den) * ob3[...]
        obm_ref[...] = obm
        lse_ref[...] = m + jnp.log(den)
        oa = oa_ref[...]
        na = oa * _rms(oa) * ga_ref[...]
        nb_ = obm * _rms(obm) * gb_ref[...]
        cat = jnp.concatenate([na, nb_], axis=1).astype(BF16)
        cat_ref[...] = cat
        z1 = ALPHA * x_ref[...] + _nn(cat, wo_ref[...])
        z1_ref[...] = z1
        zh, _ = _layer_norm_stats(z1)
        h1 = zh * g_ref[...] + b_ref[...]
        h1_ref[...] = h1
        h1b_ref[...] = h1.astype(BF16)

    t512 = pl.BlockSpec((tm, 512), lambda i: (i, 0))
    td = pl.BlockSpec((tm, D), lambda i: (i, 0))
    return pl.pallas_call(
        body, name="mix_ln1", grid=(s // tm,),
        in_specs=[td] + [t512] * 7 + [_const((1, 512))] * 2 + [_resident((D, D))] + [_const((1, D))] * 2,
        out_specs=[t512, t512, td, td, td, td],
        out_shape=[jax.ShapeDtypeStruct((s, 512), F32), jax.ShapeDtypeStruct((s, 512), F32),
                   jax.ShapeDtypeStruct((s, D), BF16), jax.ShapeDtypeStruct((s, D), F32),
                   jax.ShapeDtypeStruct((s, D), F32), jax.ShapeDtypeStruct((s, D), BF16)],
        compiler_params=_cp(("parallel",)),
    )(x, o_a, *o_b, *lse_b, _row(norm_a_g), _row(norm_b_g), w_o, _row(ln1_g), _row(ln1_b))


def _gelu_and_grad(x):
    c = math.sqrt(2.0 / math.pi)
    x2 = x * x
    cx = c * x
    t = jnp.tanh(cx * (1.0 + 0.044715 * x2))
    q = 1.0 + t
    g = (0.5 * x) * q
    dg = 0.5 * q + ((0.5 * cx) * (1.0 - t * t)) * (1.0 + (3.0 * 0.044715) * x2)
    return g, dg


CONV_CHUNK = 64


def _shift_down(u, before):
    n = u.shape[0]
    ext = jnp.concatenate([before, u], axis=0)
    return pltpu.roll(ext, 1, 0)[8:], pltpu.roll(ext, 2, 0)[8:]


def _shift_up(u, after):
    n = u.shape[0]
    ext = jnp.concatenate([u, after], axis=0)
    return pltpu.roll(ext, n + 7, 0)[:n], pltpu.roll(ext, n + 6, 0)[:n]


def _up_conv_gelu(h1b, w_up, cwb, tm=512, tn=256):
    s = h1b.shape[0]
    n_i = s // tm
    n_t = (FF // tn) * n_i

    def body(h_ref, wg_ref, wv_ref, c_ref, up_ref, a_ref, g_ref, a1_ref, pend_a, pend_b, carry):
        t = pl.program_id(0)
        row_tile = jnp.maximum(t - 1, 0) % n_i
        w_refs = (wg_ref, wv_ref)

        @pl.when(t == 0)
        def _():
            pend_b[...] = jnp.zeros_like(pend_b)
            carry[...] = jnp.zeros_like(carry)

        def step(dst, src):
            def chunk(c, before):
                rows = pl.ds(c * CONV_CHUNK, CONV_CHUNK)
                u, last = [], []
                for half in (0, 1):
                    up = src[half, rows, :]
                    r1, r2 = _shift_down(up, before[half])
                    u.append(r2 * c_ref[0, half:half + 1, :] + r1 * c_ref[1, half:half + 1, :]
                             + up * c_ref[2, half:half + 1, :] + c_ref[3, half:half + 1, :])
                    last.append(up[CONV_CHUNK - 8:])
                g, dg = _gelu_and_grad(u[0])
                a_ref[rows, :] = (g * u[1]).astype(BF16)
                g_ref[rows, :] = g.astype(BF16)
                a1_ref[rows, :] = (u[1] * dg).astype(BF16)
                return tuple(last)

            edge = tuple(jnp.where(row_tile > 0, carry[half], 0.0) for half in (0, 1))
            n_c = tm // CONV_CHUNK
            n_k = n_c // 2
            tk = D // n_k
            for half in (0, 1):
                up = None
                for kq in range(n_k):
                    ks = slice(kq * tk, (kq + 1) * tk)
                    part = _nn(h_ref[:, ks], w_refs[half][ks, :])
                    up = part if kq == 0 else up + part
                    edge = chunk(half * n_k + kq, edge)
                up_ref[half] = up
                dst[half] = up
            for half in (0, 1):
                carry[half] = edge[half]

        @pl.when(t % 2 == 0)
        def _():
            step(pend_a, pend_b)

        @pl.when(t % 2 == 1)
        def _():
            step(pend_b, pend_a)

    mm = lambda t: jnp.minimum(t, n_t - 1)
    ew = lambda t: jnp.maximum(t - 1, 0)
    out_tile = pl.BlockSpec((tm, tn), lambda t: (ew(t) % n_i, ew(t) // n_i))
    return pl.pallas_call(
        body, name="up_conv_gelu", grid=(n_t + 1,),
        in_specs=[pl.BlockSpec((tm, D), lambda t: (mm(t) % n_i, 0)),
                  pl.BlockSpec((D, tn), lambda t: (0, mm(t) // n_i)),
                  pl.BlockSpec((D, tn), lambda t: (0, FF // tn + mm(t) // n_i)),
                  pl.BlockSpec((4, 2, tn), lambda t: (0, 0, ew(t) // n_i))],
        out_specs=[pl.BlockSpec((2, tm, tn), lambda t: (0, mm(t) % n_i, mm(t) // n_i)), out_tile, out_tile, out_tile],
        out_shape=[jax.ShapeDtypeStruct((2, s, FF), F32)] + [jax.ShapeDtypeStruct((s, FF), BF16)] * 3,
        scratch_shapes=[pltpu.VMEM((2, tm, tn), F32), pltpu.VMEM((2, tm, tn), F32), pltpu.VMEM((2, 8, tn), F32)],
        compiler_params=_cp(("arbitrary",)),
    )(h1b, w_up, w_up, cwb)


def _down_ln2_loss(a, w_down, h1, target, ln2_g, ln2_b, tm=256):
    s = a.shape[0]

    def body(a_ref, w_ref, h_ref, t_ref, g_ref, b_ref, dz_ref, dzb_ref, st_ref):
        @pl.when(pl.program_id(0) == 0)
        def _():
            st_ref[...] = jnp.zeros_like(st_ref)

        z2 = ALPHA * h_ref[...] + _nn(a_ref[...], w_ref[...])
        zh, rstd = _layer_norm_stats(z2)
        diff = zh * g_ref[...] + b_ref[...] - t_ref[...]
        part = 0.5 * jnp.sum(jnp.mean(diff * diff, axis=-1, keepdims=True), axis=0, keepdims=True)
        dy = diff * (1.0 / D)
        st_ref[0:1, :] += jnp.sum(dy * zh, axis=0, keepdims=True)
        st_ref[1:2, :] += jnp.sum(dy, axis=0, keepdims=True)
        st_ref[2:3, :] += jnp.broadcast_to(part, (1, D))
        dz = _layer_norm_bwd(dy, zh, rstd, g_ref[...])
        dz_ref[...] = dz
        dzb_ref[...] = dz.astype(BF16)

    td = pl.BlockSpec((tm, D), lambda i: (i, 0))
    return pl.pallas_call(
        body, name="down_ln2_loss", grid=(s // tm,),
        in_specs=[pl.BlockSpec((tm, FF), lambda i: (i, 0)), _resident((FF, D)), td, td, _const((1, D)), _const((1, D))],
        out_specs=[td, td, _const((8, D))],
        out_shape=[jax.ShapeDtypeStruct((s, D), F32), jax.ShapeDtypeStruct((s, D), BF16),
                   jax.ShapeDtypeStruct((8, D), F32)],
        compiler_params=_cp(("arbitrary",)),
    )(a, w_down, h1, target, _row(ln2_g), _row(ln2_b))


def _conv_gelu_bwd(dz2b, w_down_t, up, g, a1, cwb, tm=512, tn=256):
    s = dz2b.shape[0]
    n_i = s // tm
    n_t = (FF // tn) * n_i

    def body(dz_ref, w_ref, up_ref, g_ref, a1_ref, c_ref, dup_ref, dc_ref, pend_a, pend_b, carry):
        t = pl.program_id(0)
        first = jnp.maximum(t - 1, 0) % n_i == 0

        @pl.when(t == 0)
        def _():
            pend_b[...] = jnp.zeros_like(pend_b)

        @pl.when(first)
        def _():
            carry[...] = jnp.zeros_like(carry)
            dc_ref[...] = jnp.zeros_like(dc_ref)

        def step(dst, src):
            n_c = tm // CONV_CHUNK

            def chunk(cc, after):
                rows = pl.ds((n_c - 1 - cc) * CONV_CHUNK, CONV_CHUNK)
                da = src[rows, :]
                dus = (da * a1_ref[rows, :].astype(F32), da * g_ref[rows, :].astype(F32))
                head = []
                for half in (0, 1):
                    du = dus[half]
                    up = up_ref[half, rows, :]
                    l1, l2 = _shift_up(du, after[half])
                    dup = (du * c_ref[2, half:half + 1, :] + l1 * c_ref[1, half:half + 1, :]
                           + l2 * c_ref[0, half:half + 1, :])
                    dup_ref[half, rows, :] = dup.astype(BF16)
                    dc_ref[0, half:half + 1, :] += jnp.sum(l2 * up, axis=0, keepdims=True)
                    dc_ref[1, half:half + 1, :] += jnp.sum(l1 * up, axis=0, keepdims=True)
                    dc_ref[2, half:half + 1, :] += jnp.sum(du * up, axis=0, keepdims=True)
                    dc_ref[3, half:half + 1, :] += jnp.sum(du, axis=0, keepdims=True)
                    head.append(du[:8])
                return tuple(head)

            head = (carry[0], carry[1])
            n_k = n_c // 2
            tk = D // n_k
            da = None
            for kq in range(n_k):
                ks = slice(kq * tk, (kq + 1) * tk)
                part = _nn(dz_ref[:, ks], w_ref[ks, :])
                da = part if kq == 0 else da + part
                head = chunk(2 * kq, head)
                head = chunk(2 * kq + 1, head)
            for half in (0, 1):
                carry[half] = head[half]
            dst[...] = da

        @pl.when(t % 2 == 0)
        def _():
            step(pend_a, pend_b)

        @pl.when(t % 2 == 1)
        def _():
            step(pend_b, pend_a)

    mm = lambda t: jnp.minimum(t, n_t - 1)
    ew = lambda t: jnp.maximum(t - 1, 0)
    row = lambda t: n_i - 1 - t % n_i
    ew_tile = pl.BlockSpec((tm, tn), lambda t: (row(ew(t)), ew(t) // n_i))
    ew_pair = pl.BlockSpec((2, tm, tn), lambda t: (0, row(ew(t)), ew(t) // n_i))
    per_col = pl.BlockSpec((4, 2, tn), lambda t: (0, 0, ew(t) // n_i))
    return pl.pallas_call(
        body, name="conv_gelu_bwd", grid=(n_t + 1,),
        in_specs=[pl.BlockSpec((tm, D), lambda t: (row(mm(t)), 0)),
                  pl.BlockSpec((D, tn), lambda t: (0, mm(t) // n_i)),
                  ew_pair, ew_tile, ew_tile, per_col],
        out_specs=[ew_pair, per_col],
        out_shape=[jax.ShapeDtypeStruct((2, s, FF), BF16), jax.ShapeDtypeStruct((4, 2, FF), F32)],
        scratch_shapes=[pltpu.VMEM((tm, tn), F32), pltpu.VMEM((tm, tn), F32), pltpu.VMEM((2, 8, tn), F32)],
        compiler_params=_cp(("arbitrary",)),
    )(dz2b, w_down_t, up, g, a1, cwb)


def _dh1_ln1_bwd(dz2, dup, w_up_t, z1, ln1_g, tm=256):
    s = dz2.shape[0]

    def body(dz2_ref, dup_ref, w_ref, z1_ref, g_ref, dz1_ref, dz1b_ref, st_ref):
        @pl.when(pl.program_id(0) == 0)
        def _():
            st_ref[...] = jnp.zeros_like(st_ref)

        dh = ALPHA * dz2_ref[...] + _nn(dup_ref[0], w_ref[0]) + _nn(dup_ref[1], w_ref[1])
        zh, rstd = _layer_norm_stats(z1_ref[...])
        st_ref[0:1, :] += jnp.sum(dh * zh, axis=0, keepdims=True)
        st_ref[1:2, :] += jnp.sum(dh, axis=0, keepdims=True)
        dz = _layer_norm_bwd(dh, zh, rstd, g_ref[...])
        dz1_ref[...] = dz
        dz1b_ref[...] = dz.astype(BF16)

    td = pl.BlockSpec((tm, D), lambda i: (i, 0))
    return pl.pallas_call(
        body, name="dh1_ln1_bwd", grid=(s // tm,),
        in_specs=[td, pl.BlockSpec((2, tm, FF), lambda i: (0, i, 0)), _resident((2, FF, D)), td, _const((1, D))],
        out_specs=[td, td, _const((8, D))],
        out_shape=[jax.ShapeDtypeStruct((s, D), F32), jax.ShapeDtypeStruct((s, D), BF16),
                   jax.ShapeDtypeStruct((8, D), F32)],
        compiler_params=_cp(("arbitrary",)),
    )(dz2, dup, w_up_t, z1, _row(ln1_g))


def _dcat_rms_bwd(dz1b, w_o, o_a, o_b, norm_a_g, norm_b_g, tm=256):
    s = dz1b.shape[0]

    def body(dz_ref, w_ref, oa_ref, ob_ref, ga_ref, gb_ref, da_ref, db_ref, st_ref):
        @pl.when(pl.program_id(0) == 0)
        def _():
            st_ref[...] = jnp.zeros_like(st_ref)

        dcat = _nt(dz_ref[...], w_ref[...])
        for k, (o_ref, g_ref, d_ref) in enumerate(((oa_ref, ga_ref, da_ref), (ob_ref, gb_ref, db_ref))):
            o = o_ref[...]
            dn = dcat[:, 512 * k:512 * (k + 1)]
            rr = _rms(o)
            oh = o * rr
            st_ref[k:k + 1, :] += jnp.sum(dn * oh, axis=0, keepdims=True)
            doh = dn * g_ref[...]
            d_ref[...] = rr * (doh - oh * jnp.mean(doh * oh, axis=-1, keepdims=True))

    t512 = pl.BlockSpec((tm, 512), lambda i: (i, 0))
    return pl.pallas_call(
        body, name="dcat_rms_bwd", grid=(s // tm,),
        in_specs=[pl.BlockSpec((tm, D), lambda i: (i, 0)), _resident((D, D)), t512, t512,
                  _const((1, 512)), _const((1, 512))],
        out_specs=[t512, t512, _const((8, 512))],
        out_shape=[jax.ShapeDtypeStruct((s, 512), F32), jax.ShapeDtypeStruct((s, 512), F32),
                   jax.ShapeDtypeStruct((8, 512), F32)],
        compiler_params=_cp(("arbitrary",)),
    )(dz1b, w_o, o_a, o_b, _row(norm_a_g), _row(norm_b_g))


def _dproj_combine(dqa, dka, dva, dqkv_b, tm=256):
    s = dqa.shape[0]

    def body(qa, ka, va, q1, k1, v1, q2, k2, v2, q3, k3, v3, o_ref):
        o_ref[:, 0:512] = qa[...].astype(BF16)
        o_ref[:, 512:640] = ka[...].astype(BF16)
        o_ref[:, 640:768] = va[...].astype(BF16)
        o_ref[:, 768:1280] = (q1[...] + q2[...] + q3[...]).astype(BF16)
        o_ref[:, 1280:1792] = (k1[...] + k2[...] + k3[...]).astype(BF16)
        o_ref[:, 1792:2304] = (v1[...] + v2[...] + v3[...]).astype(BF16)

    t512 = pl.BlockSpec((tm, 512), lambda i: (i, 0))
    t128 = pl.BlockSpec((tm, 128), lambda i: (i, 0))
    flat = [a for trio in dqkv_b for a in trio]
    return pl.pallas_call(
        body, name="dproj_combine", grid=(s // tm,),
        in_specs=[t512, t128, t128] + [t512] * 9,
        out_specs=pl.BlockSpec((tm, WIN), lambda i: (i, 0)),
        out_shape=jax.ShapeDtypeStruct((s, WIN), BF16),
        compiler_params=_cp(("parallel",)),
    )(dqa, dka, dva, *flat)


def _grad_x(dz1, dproj, w_in_t, zero, tm=256):
    s = dz1.shape[0]

    def body(dz_ref, dp_ref, w_ref, z_ref, o_ref):
        o_ref[...] = ALPHA * dz_ref[...] + _nn(dp_ref[...], w_ref[...]) + z_ref[0:1, 0:1]

    td = pl.BlockSpec((tm, D), lambda i: (i, 0))
    return pl.pallas_call(
        body, name="grad_x", grid=(s // tm,),
        in_specs=[td, pl.BlockSpec((tm, WIN), lambda i: (i, 0)), _resident((WIN, D)), _const((8, 128))],
        out_specs=td, out_shape=jax.ShapeDtypeStruct((s, D), F32),
        compiler_params=_cp(("parallel",)),
    )(dz1, dproj, w_in_t, zero)


def _place():
    return lax.axis_index("x"), lax.axis_index("y"), lax.axis_index("c")


def _other_chips(x, y):
    return [(1 - x, y), (x, 1 - y), (1 - x, 1 - y)]


def _hbm(a):
    return pltpu.with_memory_space_constraint(a, pltpu.HBM)


def _gather_w_in(land, conv_land):
    rows_k = SHARD_ROWS[0]
    half = rows_k // 2

    def body(land_in, conv_in, out, conv_out, send_sems, recv_sems):
        del land_in, conv_in
        x, y, c = _place()
        b = 2 * x + y
        sibling = (x, y, 1 - c)
        chips = _other_chips(x, y)

        def copy(idx, chip_b, core, to):
            rows = out.at[pl.ds(pl.multiple_of(chip_b * rows_k + core * half, 16), half)]
            return pltpu.make_async_remote_copy(src_ref=rows, dst_ref=rows, send_sem=send_sems.at[idx],
                                                recv_sem=recv_sems.at[idx], device_id=to, device_id_type=MESH)

        def conv_copy(jn, chip_b, to):
            return pltpu.make_async_remote_copy(src_ref=conv_out.at[chip_b], dst_ref=conv_out.at[chip_b],
                                                send_sem=send_sems.at[6 + jn], recv_sem=recv_sems.at[6 + jn],
                                                device_id=to, device_id_type=MESH)

        started = []
        for jn, chip in enumerate(chips):
            for cp in (copy(jn, b, c, (chip[0], chip[1], c)), conv_copy(jn, b, (chip[0], chip[1], c))):
                cp.start()
                started.append(cp)
        for jn, chip in enumerate(chips):
            cb = 2 * chip[0] + chip[1]
            copy(jn, cb, c, (chip[0], chip[1], c)).wait_recv()
            cp = copy(3 + jn, cb, c, sibling)
            cp.start()
            started.append(cp)
        for jn, chip in enumerate(chips):
            cb = 2 * chip[0] + chip[1]
            copy(3 + jn, cb, 1 - c, sibling).wait_recv()
            conv_copy(jn, cb, (chip[0], chip[1], c)).wait_recv()
        for cp in started:
            cp.wait_send()

    return pl.pallas_call(
        body, name="gather_w_in",
        in_specs=[ANY, ANY], out_specs=[ANY, ANY],
        out_shape=[jax.ShapeDtypeStruct(land.shape, land.dtype), jax.ShapeDtypeStruct(conv_land.shape, conv_land.dtype)],
        input_output_aliases={0: 0, 1: 1},
        scratch_shapes=[pltpu.SemaphoreType.DMA((9,)), pltpu.SemaphoreType.DMA((9,))],
        compiler_params=pltpu.CompilerParams(has_side_effects=True),
    )(land, conv_land)


def _weight_copies(shard, land, send_sems, recv_sems, arrivals):
    x, y, c = _place()
    b = 2 * x + y
    rows_k = shard.shape[0]
    cps = []
    for jn, chip in enumerate(_other_chips(x, y)):
        at = 2 * chip[0] + chip[1] if arrivals else b
        cps.append(pltpu.make_async_remote_copy(
            src_ref=shard, dst_ref=land.at[pl.ds(pl.multiple_of(at * rows_k, 16), rows_k)],
            send_sem=send_sems.at[jn], recv_sem=recv_sems.at[jn],
            device_id=(chip[0], chip[1], c), device_id_type=MESH))
    return cps


def _weights_start(shards, lands):
    n = len(shards)

    def body(*refs):
        src, land = refs[:n], refs[n:2 * n]
        send_sems, recv_sems = refs[2 * n:3 * n], refs[3 * n:4 * n]
        for k in range(n):
            for send in _weight_copies(src[k], land[k], send_sems[k], recv_sems[k], False):
                send.start()
        refs[-1][...] = jnp.zeros_like(refs[-1])

    res = pl.pallas_call(
        body, name="weights_start",
        in_specs=[HBM] * (2 * n), out_specs=[SEM] * (2 * n) + [HBM] * (2 * n) + [VMEM],
        out_shape=[pltpu.SemaphoreType.DMA((3,))] * (2 * n)
        + [pltpu.HBM(a.shape, a.dtype) for a in (*shards, *lands)] + [jax.ShapeDtypeStruct((8, 128), F32)],
        input_output_aliases={i: i + 2 * n for i in range(2 * n)},
        compiler_params=pltpu.CompilerParams(has_side_effects=DATAFLOW),
    )(*[_hbm(a) for a in (*shards, *lands)])
    return [(res[k], res[n + k], res[2 * n + k], res[3 * n + k]) for k in range(n)], res[-1]


def _weights_wait(started, after, name):
    send_sems, recv_sems, shard, land = started

    def body(s_ref, l_ref, send_ref, recv_ref, after_ref, s_out, l_out):
        for cp in _weight_copies(s_ref, l_ref, send_ref, recv_ref, True):
            cp.wait_send()
            cp.wait_recv()

    return pl.pallas_call(
        body, name=name,
        in_specs=[HBM, HBM, SEM, SEM, ANY], out_specs=[HBM, HBM],
        out_shape=[pltpu.HBM(shard.shape, shard.dtype), pltpu.HBM(land.shape, land.dtype)],
        input_output_aliases={0: 0, 1: 1},
        compiler_params=pltpu.CompilerParams(has_side_effects=DATAFLOW),
    )(shard, land, send_sems, recv_sems, after)[1]


def _grad_copies(g_ref, land_ref, send_sems, recv_sems):
    x, y, c = _place()
    cps = []
    for d in range(1, 8):
        px, py, pc = x ^ (d >> 2), y ^ ((d >> 1) & 1), c ^ (d & 1)
        cps.append(pltpu.make_async_remote_copy(
            src_ref=g_ref.at[2 * px + py, pc], dst_ref=land_ref.at[d - 1], send_sem=send_sems.at[d - 1],
            recv_sem=recv_sems.at[d - 1], device_id=(px, py, pc), device_id_type=MESH))
    return cps


def _grads_start(grad_b, name):
    h = grad_b.shape[2]

    def body(g_ref, land_ref, send_sems, recv_sems, g_thru, land_thru, token):
        for cp in _grad_copies(g_ref, land_ref, send_sems, recv_sems):
            cp.start()
        token[...] = jnp.zeros_like(token)

    return pl.pallas_call(
        body, name=name,
        in_specs=[HBM, HBM], out_specs=[SEM, SEM, HBM, HBM, VMEM],
        out_shape=[pltpu.SemaphoreType.DMA((7,)), pltpu.SemaphoreType.DMA((7,)), pltpu.HBM(grad_b.shape, BF16),
                   pltpu.HBM((7, h, D), BF16), jax.ShapeDtypeStruct((8, 128), F32)],
        input_output_aliases={0: 2, 1: 3},
        compiler_params=pltpu.CompilerParams(has_side_effects=DATAFLOW),
    )(_hbm(grad_b), _hbm(lax.empty((7, h, D), BF16)))


def _grads_wait(started, after):
    n = len(started)

    def body(*refs):
        g, land = refs[:n], refs[n:2 * n]
        send_sems, recv_sems = refs[2 * n:3 * n], refs[3 * n:4 * n]
        for k in range(n):
            for cp in _grad_copies(g[k], land[k], send_sems[k], recv_sems[k]):
                cp.wait_send()
                cp.wait_recv()

    gs = [st[2] for st in started]
    lands = [st[3] for st in started]
    res = pl.pallas_call(
        body, name="grads_wait",
        in_specs=[HBM] * (2 * n) + [SEM] * (2 * n) + [ANY], out_specs=[HBM] * (2 * n),
        out_shape=[pltpu.HBM(a.shape, a.dtype) for a in (*gs, *lands)],
        input_output_aliases={i: i for i in range(2 * n)},
        compiler_params=pltpu.CompilerParams(has_side_effects=DATAFLOW),
    )(*gs, *lands, *[st[0] for st in started], *[st[1] for st in started], after)
    return res[n:]


def _sum_partials(grad4, got, cb, name, tr):
    h = grad4.shape[2]
    per_half = h // tr

    def body(cb_ref, g_ref, o_ref, out_ref):
        acc = g_ref[...]
        for j in range(7):
            acc = acc + o_ref[j].astype(F32)
        out_ref[...] = acc

    return pl.pallas_call(
        body, name=name,
        grid_spec=pltpu.PrefetchScalarGridSpec(
            num_scalar_prefetch=1, grid=(per_half,),
            in_specs=[pl.BlockSpec((None, None, tr, D), lambda i, cb_ref: (cb_ref[1], cb_ref[0], i, 0)),
                      pl.BlockSpec((7, tr, D), lambda i, cb_ref: (0, i, 0))],
            out_specs=pl.BlockSpec((tr, D), lambda i, cb_ref: (cb_ref[0] * per_half + i, 0))),
        out_shape=jax.ShapeDtypeStruct((2 * h, D), F32),
        compiler_params=_cp(("arbitrary",)),
    )(cb, grad4, got)


def _share_halves(shards, small):
    n = len(shards)
    rows = small.shape[0]

    def body(*refs):
        small_ref = refs[n]
        out, total_ref = refs[n + 1:2 * n + 1], refs[2 * n + 1]
        all_ref, send_sems, recv_sems, ssend, srecv = refs[2 * n + 2:]
        x, y, c = _place()
        me = 4 * x + 2 * y + c
        cps = []
        for k in range(n):
            h = shards[k].shape[0] // 2
            mine = out[k].at[pl.ds(pl.multiple_of(c * h, 8), h)]
            cp = pltpu.make_async_remote_copy(src_ref=mine, dst_ref=mine, send_sem=send_sems.at[k],
                                              recv_sem=recv_sems.at[k], device_id=(x, y, 1 - c), device_id_type=MESH)
            cp.start()
            cps.append(cp)
        all_ref[me] = small_ref[...]
        peers = []
        for d in range(1, 8):
            px, py, pc = x ^ (d >> 2), y ^ ((d >> 1) & 1), c ^ (d & 1)
            cp = pltpu.make_async_remote_copy(src_ref=small_ref, dst_ref=all_ref.at[me],
                                              send_sem=ssend.at[d - 1], recv_sem=srecv.at[d - 1],
                                              device_id=(px, py, pc), device_id_type=MESH)
            cp.start()
            peers.append(cp)
        for cp in peers:
            cp.wait()
        acc = all_ref[0]
        for d in range(1, 8):
            acc = acc + all_ref[d]
        total_ref[...] = acc
        for cp in cps:
            cp.wait()

    return pl.pallas_call(
        body, name="share_halves",
        in_specs=[ANY] * n + [VMEM], out_specs=[ANY] * n + [VMEM],
        out_shape=[jax.ShapeDtypeStruct(sh.shape, F32) for sh in shards] + [jax.ShapeDtypeStruct((rows, D), F32)],
        input_output_aliases={k: k for k in range(n)},
        scratch_shapes=[pltpu.VMEM((8, rows, D), F32), pltpu.SemaphoreType.DMA((n,)), pltpu.SemaphoreType.DMA((n,)),
                        pltpu.SemaphoreType.DMA((7,)), pltpu.SemaphoreType.DMA((7,))],
        compiler_params=pltpu.CompilerParams(has_side_effects=True),
    )(*shards, small)


def _adamw(w, g, m, v, name, tr):
    rows, cols = w.shape

    def body(w_ref, g_ref, m_ref, v_ref, d_ref, nm_ref, nv_ref):
        g_ = g_ref[...]
        nm = ADAM_B1 * m_ref[...] + (1.0 - ADAM_B1) * g_
        nv = ADAM_B2 * v_ref[...] + (1.0 - ADAM_B2) * (g_ * g_)
        m_hat = nm / (1.0 - ADAM_B1 ** ADAM_STEP)
        v_hat = nv / (1.0 - ADAM_B2 ** ADAM_STEP)
        d_ref[...] = -ADAM_LR * (m_hat / (jnp.sqrt(v_hat) + ADAM_EPS) + ADAM_WD * w_ref[...])
        nm_ref[...] = nm
        nv_ref[...] = nv

    spec = pl.BlockSpec((tr, cols), lambda i: (i, 0))
    return pl.pallas_call(
        body, name=name, grid=(rows // tr,),
        in_specs=[spec] * 4, out_specs=[spec] * 3,
        out_shape=[jax.ShapeDtypeStruct((rows, cols), F32)] * 3,
        compiler_params=_cp(("parallel",)),
    )(w, g, m, v)


def _local_step(x, target, w_in_t, late_weights, norm_a_g, norm_b_g, sinks_a, ln1_g, ln1_b,
                conv_w, conv_b, ln2_g, ln2_b, slopes, on_grad):
    cwb = jnp.concatenate([conv_w, conv_b[None]], axis=0).reshape(4, 2, FF)

    proj = _proj(x, w_in_t, "proj")
    o_a, lse_a = _attn_a_fwd(proj, sinks_a)
    fwd_b = [_attn_b_fwd(proj, slopes, r) for r in B_DILATIONS]
    w_o = late_weights(1, fwd_b[-1][1])
    o_b, lse_b, cat, z1, h1, h1b = _mix_ln1(x, o_a, [f[0] for f in fwd_b], [f[1] for f in fwd_b],
                                           norm_a_g, norm_b_g, w_o, ln1_g, ln1_b)
    w_up_t = late_weights(2, h1b)
    w_up3 = w_up_t.reshape(2, FF, D)
    up, a, gate, a1 = _up_conv_gelu(h1b, w_up_t.T, cwb)
    w_down = late_weights(3, a)
    dz2, dz2b, st2 = _down_ln2_loss(a, w_down, h1, target, ln2_g, ln2_b)

    tok = on_grad(3, *_grad_w(a, dz2b, "grad_w_down", tm=FF // 2))
    dup, dconv = _conv_gelu_bwd(dz2b, w_down.T, up, gate, a1, cwb + tok[0, 0])
    tok = on_grad(2, *_grad_w(dup, h1b, "grad_w_up", tm=FF // 2, lhs_halves=True))
    dz1, dz1b, st1 = _dh1_ln1_bwd(dz2, dup, w_up3, z1, ln1_g + tok[0, 0])
    tok = on_grad(1, *_grad_w(cat, dz1b, "grad_w_o", tm=512))
    d_oa, d_ob, st_n = _dcat_rms_bwd(dz1b, w_o, o_a, o_b, norm_a_g + tok[0, 0], norm_b_g)
    dqa, dka, dva, dsink = _attn_a_bwd(proj, sinks_a, d_oa, o_a, lse_a)
    bwd_b = [_attn_b_bwd(proj, slopes, d_ob, o_b, lse_b, r) for r in B_DILATIONS]
    dproj = _dproj_combine(dqa, dka, dva, bwd_b)
    tok = on_grad(0, *_grad_w(dproj, x, "grad_w_in", tm=WA))
    gx = _grad_x(dz1, dproj, w_in_t, tok)

    dconv = dconv.reshape(4, 2 * FF)
    small = dict(loss=st2[2, 0:1], norm_a_g=st_n[0], norm_b_g=st_n[1], sinks_a=dsink[:, 0],
                 ln1_g=st1[0], ln1_b=st1[1], conv_w=dconv[0:3].reshape(-1), conv_b=dconv[3],
                 ln2_g=st2[0], ln2_b=st2[1])
    return gx, small


SMALL_ORDER = ("loss", "norm_a_g", "norm_b_g", "sinks_a", "ln1_g", "ln1_b", "conv_b", "ln2_g", "ln2_b", "conv_w")
SMALL_SIZES = dict(loss=1, norm_a_g=512, norm_b_g=512, sinks_a=8, ln1_g=D, ln1_b=D, conv_b=2 * FF, ln2_g=D, ln2_b=D,
                   conv_w=3 * 2 * FF)


def _pack(parts, rows):
    flat = jnp.concatenate([parts[k].reshape(-1).astype(F32) for k in parts])
    return jnp.pad(flat, (0, rows * D - flat.shape[0])).reshape(rows, D)


def _unpack(buf, names, sizes):
    flat = buf.reshape(-1)
    out, at = {}, 0
    for k in names:
        out[k] = flat[at:at + sizes[k]]
        at += sizes[k]
    return out


def kernel(x, w_in, norm_a_g, norm_b_g, sinks_a, w_o, ln1_g, ln1_b, w_up, conv_w, conv_b, w_down, ln2_g, ln2_b, loss_target, m_w_in, m_norm_a_g, m_norm_b_g, m_sinks_a, m_w_o, m_ln1_g, m_ln1_b, m_w_up, m_conv_w, m_conv_b, m_w_down, m_ln2_g, m_ln2_b, v_w_in, v_norm_a_g, v_norm_b_g, v_sinks_a, v_w_o, v_ln1_g, v_ln1_b, v_w_up, v_conv_w, v_conv_b, v_w_down, v_ln2_g, v_ln2_b):
    xi, yi, ci = _place()
    chip = (2 * xi + yi).astype(I32)
    core = ci.astype(I32)

    shards = (w_in.T.astype(BF16), w_o.astype(BF16), w_up.T.astype(BF16), w_down.astype(BF16))
    lands = [lax.dynamic_update_slice(lax.empty((N_CHIPS * r, D), BF16), sh, (chip * r, 0))
             for sh, r in zip(shards, SHARD_ROWS)]
    conv_land = lax.dynamic_update_slice(lax.empty((N_CHIPS,) + conv_w.shape, F32), conv_w[None], (chip, 0, 0))
    w_in_t, conv_w4 = _gather_w_in(lands[0], conv_land)
    conv_w_f = conv_w4.transpose(1, 0, 2).reshape(3, 2 * FF)
    w_started, w_tok = _weights_start(shards[1:], lands[1:])
    slopes = jnp.asarray(SLOPES, F32) + w_tok[0, 0]

    halves_rows = [r // 2 for r in SHARD_ROWS]
    grads4, started = [None] * 4, [None] * 4

    def on_grad(k, g, g_b):
        grads4[k] = g.reshape(N_CHIPS, 2, halves_rows[k], D)
        started[k] = _grads_start(g_b.reshape(N_CHIPS, 2, halves_rows[k], D), f"grads_start_{k}")
        return started[k][4]

    gx, small = _local_step(
        x[0], loss_target[0], w_in_t, lambda k, after: _weights_wait(w_started[k - 1], after, f"weights_wait_{k}"),
        norm_a_g, norm_b_g, sinks_a, ln1_g, ln1_b, conv_w_f, conv_b, ln2_g, ln2_b, slopes, on_grad)

    got = _grads_wait(started, gx)
    tiles = (96, 128, 352, 176)
    core_chip = jnp.stack([core, chip])
    halves = [_sum_partials(grads4[k], got[k], core_chip, f"sum_partials_{k}", tiles[k]) for k in range(4)]
    small_rows = 32
    *full, totals = _share_halves(halves, _pack({k: small[k] for k in SMALL_ORDER}, small_rows))
    tot = _unpack(totals, SMALL_ORDER, SMALL_SIZES)

    g_w_in, g_w_o, g_w_up, g_w_down = full[0].T, full[1], full[2].T, full[3]
    loss = tot["loss"][0]
    cols = 2 * FF // N_CHIPS
    g_conv_w = lax.dynamic_slice(tot["conv_w"].reshape(3, 2 * FF), (0, chip * cols), (3, cols))
    g_small = dict(norm_a_g=tot["norm_a_g"], norm_b_g=tot["norm_b_g"], sinks_a=tot["sinks_a"], ln1_g=tot["ln1_g"],
                   ln1_b=tot["ln1_b"], conv_w=g_conv_w, conv_b=tot["conv_b"], ln2_g=tot["ln2_g"], ln2_b=tot["ln2_b"])

    weights = dict(w_in=w_in, norm_a_g=norm_a_g, norm_b_g=norm_b_g, sinks_a=sinks_a, w_o=w_o, ln1_g=ln1_g, ln1_b=ln1_b,
                   w_up=w_up, conv_w=conv_w, conv_b=conv_b, w_down=w_down, ln2_g=ln2_g, ln2_b=ln2_b)
    ms = dict(w_in=m_w_in, norm_a_g=m_norm_a_g, norm_b_g=m_norm_b_g, sinks_a=m_sinks_a, w_o=m_w_o, ln1_g=m_ln1_g,
              ln1_b=m_ln1_b, w_up=m_w_up, conv_w=m_conv_w, conv_b=m_conv_b, w_down=m_w_down, ln2_g=m_ln2_g, ln2_b=m_ln2_b)
    vs = dict(w_in=v_w_in, norm_a_g=v_norm_a_g, norm_b_g=v_norm_b_g, sinks_a=v_sinks_a, w_o=v_w_o, ln1_g=v_ln1_g,
              ln1_b=v_ln1_b, w_up=v_w_up, conv_w=v_conv_w, conv_b=v_conv_b, w_down=v_w_down, ln2_g=v_ln2_g, ln2_b=v_ln2_b)
    order = list(weights)
    grad = dict(g_small, w_in=g_w_in, w_o=g_w_o, w_up=g_w_up, w_down=g_w_down)

    delta, new_m, new_v = {}, {}, {}
    for k, tr in (("w_in", 256), ("w_o", 128), ("w_up", 256), ("w_down", 176)):
        delta[k], new_m[k], new_v[k] = _adamw(weights[k], grad[k], ms[k], vs[k], f"adamw_{k}", tr)
    small_names = [k for k in order if k not in delta]
    sizes = {k: weights[k].size for k in small_names}
    rows = 16
    packed = [_pack({k: src[k] for k in small_names}, rows) for src in (weights, grad, ms, vs)]
    for res, buf in zip((delta, new_m, new_v), _adamw(*packed, "adamw_small", rows)):
        for k, val in _unpack(buf, small_names, sizes).items():
            res[k] = val.reshape(weights[k].shape)

    return (loss, gx[None], *[grad[k] for k in order], *[delta[k] for k in order],
            *[new_m[k] for k in order], *[new_v[k] for k in order])
```

```python
import functools
import math

import jax
import jax.numpy as jnp
from jax import lax
from jax.experimental import pallas as pl
from jax.experimental.pallas import tpu as pltpu

F32, BF16, I32 = jnp.float32, jnp.bfloat16, jnp.int32

D = 1024
FF = 2816
HD = 64
NH = 8
WA, WB = 768, 1536
WIN = WA + WB
BLK = 128
ALPHA = 2.0 ** 0.25
LN_EPS, RMS_EPS = 1e-5, 1e-6
SCALE = 1.0 / math.sqrt(HD)
A_MAX_DIST, B_MAX_DIST = 127, 128
B_DILATIONS = (1, 4, 16)
SLOPES = tuple(2.0 ** (-(i + 1)) for i in range(NH))
SHARD_ROWS = (WIN // 4, D // 4, 2 * FF // 4, FF // 4)
N_CHIPS = 4
ADAM_LR, ADAM_B1, ADAM_B2, ADAM_EPS, ADAM_WD, ADAM_STEP = 0.001, 0.9, 0.999, 1e-08, 0.01, 10
MESH = pl.DeviceIdType.MESH
ANY = pl.BlockSpec(memory_space=pl.ANY)
SMEM = pl.BlockSpec(memory_space=pltpu.SMEM)
VMEM = pl.BlockSpec(memory_space=pltpu.VMEM)
HBM = pl.BlockSpec(memory_space=pltpu.HBM)
SEM = pl.BlockSpec(memory_space=pltpu.SEMAPHORE)
DATAFLOW = pltpu.SideEffectType.DATAFLOW_SIDE_EFFECTING


def _cp(sem, mb=48):
    return pltpu.CompilerParams(dimension_semantics=sem, vmem_limit_bytes=mb << 20)


def _nn(a, b):
    return lax.dot_general(a, b, (((1,), (0,)), ((), ())), preferred_element_type=F32)


def _nt(a, b):
    return lax.dot_general(a, b, (((1,), (1,)), ((), ())), preferred_element_type=F32)


def _tn(a, b):
    return lax.dot_general(a, b, (((0,), (0,)), ((), ())), preferred_element_type=F32)


def _resident(shape):
    n = len(shape)
    return pl.BlockSpec(shape, lambda *_: (0,) * n, pipeline_mode=pl.Buffered(1))


def _const(shape):
    n = len(shape)
    return pl.BlockSpec(shape, lambda *_: (0,) * n)


def _proj(x, w_t, name, tm=512):
    s = x.shape[0]
    n = w_t.shape[0]

    def body(x_ref, w_ref, o_ref):
        o_ref[...] = _nt(x_ref[...].astype(BF16), w_ref[...])

    return pl.pallas_call(
        body, name=name, grid=(s // tm,),
        in_specs=[pl.BlockSpec((tm, D), lambda i: (i, 0)), _resident((n, D))],
        out_specs=pl.BlockSpec((tm, n), lambda i: (i, 0)),
        out_shape=jax.ShapeDtypeStruct((s, n), F32),
        compiler_params=_cp(("parallel",)),
    )(x, w_t)


def _grad_w(lhs, rhs, name, tm, tk=512, lhs_halves=False):
    s = rhs.shape[0]
    if lhs_halves:
        per_half = lhs.shape[2] // tm
        n = 2 * lhs.shape[2]
        lhs_spec = pl.BlockSpec((None, tk, tm), lambda i, k: (i // per_half, k, i % per_half))
    else:
        n = lhs.shape[1]
        lhs_spec = pl.BlockSpec((tk, tm), lambda i, k: (k, i))
    nk = s // tk

    def body(l_ref, r_ref, o_ref, ob_ref):
        k = pl.program_id(1)

        @pl.when(k == 0)
        def _():
            o_ref[...] = jnp.zeros_like(o_ref)

        o_ref[...] += _tn(l_ref[...].astype(BF16), r_ref[...].astype(BF16))

        @pl.when(k == nk - 1)
        def _():
            ob_ref[...] = o_ref[...].astype(BF16)

    return pl.pallas_call(
        body, name=name, grid=(n // tm, nk),
        in_specs=[lhs_spec, pl.BlockSpec((tk, D), lambda i, k: (k, 0))],
        out_specs=[pl.BlockSpec((tm, D), lambda i, k: (i, 0))] * 2,
        out_shape=[jax.ShapeDtypeStruct((n, D), F32), jax.ShapeDtypeStruct((n, D), BF16)],
        compiler_params=_cp(("parallel", "arbitrary")),
    )(lhs, rhs)


def _band_base(max_dist, dist_unit, first):
    row = lax.broadcasted_iota(I32, (BLK, 2 * BLK), 0)
    col = lax.broadcasted_iota(I32, (BLK, 2 * BLK), 1)
    dist = BLK + row - col
    ok = (dist >= 0) & (dist <= max_dist)
    if first:
        ok = ok & (col >= BLK)
    return jnp.where(ok, dist.astype(F32) * (-float(dist_unit)), -jnp.inf)


def _half_mask(shape, e):
    lane = lax.broadcasted_iota(I32, shape, 1)
    return (lane < HD) if e == 0 else (lane >= HD)


def _to_half(x, e, g):
    if g != e:
        x = pltpu.roll(x, HD, 1)
    return jnp.where(_half_mask(x.shape, g), x, 0.0)


def _pair_fwd(q2, kb, vb, base, slopes, kv_heads, sinks):
    lo = _half_mask((BLK, 2 * HD), 0)
    o2 = lse2 = None
    for e in (0, 1):
        g = kv_heads[e]
        qv = (_to_half(q2, e, g) * SCALE).astype(BF16)
        s = _nt(qv, kb) + slopes[e] * base
        m = jnp.max(s, axis=1, keepdims=True)
        if sinks is not None:
            m = jnp.maximum(m, sinks[e])
        p = jnp.exp(s - m)
        l = jnp.sum(p, axis=1, keepdims=True)
        if sinks is not None:
            l = l + jnp.exp(sinks[e] - m)
        oh = _nn(p.astype(BF16), vb) / l
        if g != e:
            oh = pltpu.roll(oh, HD, 1)
        lse = jnp.broadcast_to(m + jnp.log(l), (BLK, 2 * HD))
        o2 = oh if e == 0 else jnp.where(lo, o2, oh)
        lse2 = lse if e == 0 else jnp.where(lo, lse2, lse)
    return o2, lse2


def _pair_bwd(q2, kb, vb, do2, o2, lse2, base, slopes, kv_heads, sinks):
    lo = _half_mask((BLK, 2 * HD), 0)
    dq2 = dk2 = dv2 = None
    prod = do2 * o2
    dsinks = []
    for e in (0, 1):
        g = kv_heads[e]
        hq = _half_mask((BLK, 2 * HD), e)
        lse = jnp.max(jnp.where(hq, lse2, -jnp.inf), axis=1, keepdims=True)
        delta = jnp.sum(jnp.where(hq, prod, 0.0), axis=1, keepdims=True)
        qv = (_to_half(q2, e, g) * SCALE).astype(BF16)
        dov = _to_half(do2, e, g).astype(BF16)
        p = jnp.exp(_nt(qv, kb) + slopes[e] * base - lse)
        ds = (p * (_nt(dov, vb) - delta)).astype(BF16)
        dqh = _nn(ds, kb) * SCALE
        if g != e:
            dqh = pltpu.roll(dqh, HD, 1)
        dq2 = dqh if e == 0 else jnp.where(lo, dq2, dqh)
        dkh = _tn(ds, qv)
        dvh = _tn(p.astype(BF16), dov)
        dk2 = dkh if e == 0 else dk2 + dkh
        dv2 = dvh if e == 0 else dv2 + dvh
        if sinks is not None:
            dsinks.append(jnp.sum(-jnp.exp(sinks[e] - lse) * delta, axis=0, keepdims=True))
    return dq2, dk2, dv2, dsinks


def _attn_a_fwd(proj, sinks):
    s = proj.shape[0]
    nb = s // BLK

    def body(sink_ref, q_ref, kp_ref, kc_ref, vp_ref, vc_ref, o_ref, lse_ref):
        n = pl.program_id(0)
        base = jnp.where(n > 0, _band_base(A_MAX_DIST, 1, False), _band_base(A_MAX_DIST, 1, True))
        kb = jnp.concatenate([kp_ref[...], kc_ref[...]], axis=0).astype(BF16)
        vb = jnp.concatenate([vp_ref[...], vc_ref[...]], axis=0).astype(BF16)
        for j in range(NH // 2):
            g = j // 2
            o2, lse2 = _pair_fwd(q_ref[:, 128 * j:128 * (j + 1)], kb, vb, base, (SLOPES[2 * j], SLOPES[2 * j + 1]),
                                 (g, g), (sink_ref[2 * j], sink_ref[2 * j + 1]))
            o_ref[:, 128 * j:128 * (j + 1)] = o2
            lse_ref[:, 128 * j:128 * (j + 1)] = lse2

    prev = lambda n: jnp.maximum(n - 1, 0)
    return pl.pallas_call(
        body, name="attn_a_fwd", grid=(nb,),
        in_specs=[SMEM,
                  pl.BlockSpec((BLK, 512), lambda n: (n, 0)),
                  pl.BlockSpec((BLK, 128), lambda n: (prev(n), 4)), pl.BlockSpec((BLK, 128), lambda n: (n, 4)),
                  pl.BlockSpec((BLK, 128), lambda n: (prev(n), 5)), pl.BlockSpec((BLK, 128), lambda n: (n, 5))],
        out_specs=[pl.BlockSpec((BLK, 512), lambda n: (n, 0))] * 2,
        out_shape=[jax.ShapeDtypeStruct((s, 512), F32)] * 2,
        compiler_params=_cp(("parallel",)),
    )(sinks, proj, proj, proj, proj, proj)


def _attn_a_bwd(proj, sinks, d_o, o, lse):
    s = proj.shape[0]
    nb = s // BLK

    def body(sink_ref, q_ref, kp_ref, kc_ref, vp_ref, vc_ref, do_ref, o_ref, lse_ref,
             dq_ref, dk_ref, dv_ref, dsink_ref, kcar, vcar):
        n = pl.program_id(0)

        @pl.when(n == 0)
        def _():
            kcar[...] = jnp.zeros_like(kcar)
            vcar[...] = jnp.zeros_like(vcar)
            dsink_ref[...] = jnp.zeros_like(dsink_ref)

        @pl.when(n < nb)
        def _():
            base = jnp.where(n > 0, _band_base(A_MAX_DIST, 1, False), _band_base(A_MAX_DIST, 1, True))
            kb = jnp.concatenate([kp_ref[...], kc_ref[...]], axis=0).astype(BF16)
            vb = jnp.concatenate([vp_ref[...], vc_ref[...]], axis=0).astype(BF16)
            dk_win = dv_win = None
            for j in range(NH // 2):
                g = j // 2
                sl = slice(128 * j, 128 * (j + 1))
                dq2, dk2, dv2, dsk = _pair_bwd(q_ref[:, sl], kb, vb, do_ref[:, sl], o_ref[:, sl], lse_ref[:, sl], base,
                                               (SLOPES[2 * j], SLOPES[2 * j + 1]), (g, g),
                                               (sink_ref[2 * j], sink_ref[2 * j + 1]))
                dq_ref[:, sl] = dq2
                dk_win = dk2 if j == 0 else dk_win + dk2
                dv_win = dv2 if j == 0 else dv_win + dv2
                for e in (0, 1):
                    h = 2 * j + e
                    dsink_ref[h:h + 1, :] += jnp.broadcast_to(dsk[e], (1, 128))
            dk_ref[...] = kcar[...] + dk_win[:BLK]
            dv_ref[...] = vcar[...] + dv_win[:BLK]
            kcar[...] = dk_win[BLK:]
            vcar[...] = dv_win[BLK:]

        @pl.when(n == nb)
        def _():
            dk_ref[...] = kcar[...]
            dv_ref[...] = vcar[...]

    cur = lambda n: jnp.minimum(n, nb - 1)
    prev = lambda n: jnp.maximum(cur(n) - 1, 0)
    out_prev = lambda n: jnp.maximum(n - 1, 0)
    return pl.pallas_call(
        body, name="attn_a_bwd", grid=(nb + 1,),
        in_specs=[SMEM,
                  pl.BlockSpec((BLK, 512), lambda n: (cur(n), 0)),
                  pl.BlockSpec((BLK, 128), lambda n: (prev(n), 4)), pl.BlockSpec((BLK, 128), lambda n: (cur(n), 4)),
                  pl.BlockSpec((BLK, 128), lambda n: (prev(n), 5)), pl.BlockSpec((BLK, 128), lambda n: (cur(n), 5)),
                  pl.BlockSpec((BLK, 512), lambda n: (cur(n), 0)),
                  pl.BlockSpec((BLK, 512), lambda n: (cur(n), 0)),
                  pl.BlockSpec((BLK, 512), lambda n: (cur(n), 0))],
        out_specs=[pl.BlockSpec((BLK, 512), lambda n: (cur(n), 0)),
                   pl.BlockSpec((BLK, 128), lambda n: (out_prev(n), 0)),
                   pl.BlockSpec((BLK, 128), lambda n: (out_prev(n), 0)),
                   pl.BlockSpec((NH, 128), lambda n: (0, 0))],
        out_shape=[jax.ShapeDtypeStruct((s, 512), F32), jax.ShapeDtypeStruct((s, 128), F32),
                   jax.ShapeDtypeStruct((s, 128), F32), jax.ShapeDtypeStruct((NH, 128), F32)],
        scratch_shapes=[pltpu.VMEM((BLK, 128), F32), pltpu.VMEM((BLK, 128), F32)],
        compiler_params=_cp(("arbitrary",)),
    )(sinks, proj, proj, proj, proj, proj, d_o, o, lse)


def _stream(rho, i, r):
    start = i * BLK * r + rho
    return pl.ds(start, BLK, stride=r) if r > 1 else pl.ds(start, BLK)


def _for_streams(r, fn):
    if r <= 4:
        for rho in range(r):
            fn(rho)
    else:
        def four(it, carry):
            for u in range(4):
                fn(4 * it + u)
            return carry

        lax.fori_loop(0, r // 4, four, 0)


B_BLOCKS_PER_STEP = {1: 4, 4: 1, 16: 1}


def _attn_b_fwd(proj, slopes, r):
    s = proj.shape[0]
    nq = B_BLOCKS_PER_STEP[r]
    rows = BLK * r * nq
    steps = s // rows
    qc, kc, vc = WA // 128, WA // 128 + 4, WA // 128 + 8

    def body(slope_ref, q_ref, kp_ref, kc_ref, vp_ref, vc_ref, o_ref, lse_ref):
        j = pl.program_id(0)
        sb = pl.program_id(1)
        base_rest = _band_base(B_MAX_DIST, r, False)
        base_0 = jnp.where(sb > 0, base_rest, _band_base(B_MAX_DIST, r, True))
        sl2 = (slope_ref[2 * j], slope_ref[2 * j + 1])

        def stream(rho):
            for i in range(nq):
                cur = _stream(rho, i, r)
                k_prev = kc_ref[_stream(rho, i - 1, r), :] if i > 0 else kp_ref[_stream(rho, 0, r), :]
                v_prev = vc_ref[_stream(rho, i - 1, r), :] if i > 0 else vp_ref[_stream(rho, 0, r), :]
                kb = jnp.concatenate([k_prev, kc_ref[cur, :]], axis=0).astype(BF16)
                vb = jnp.concatenate([v_prev, vc_ref[cur, :]], axis=0).astype(BF16)
                o2, lse2 = _pair_fwd(q_ref[cur, :], kb, vb, base_rest if i > 0 else base_0, sl2, (0, 1), None)
                o_ref[cur, :] = o2
                lse_ref[cur, :] = lse2

        _for_streams(r, stream)

    before = lambda sb: jnp.maximum(sb * nq - 1, 0)
    return pl.pallas_call(
        body, name=f"attn_b_fwd_r{r}", grid=(NH // 2, steps),
        in_specs=[SMEM,
                  pl.BlockSpec((rows, 128), lambda j, sb: (sb, qc + j)),
                  pl.BlockSpec((BLK * r, 128), lambda j, sb: (before(sb), kc + j)),
                  pl.BlockSpec((rows, 128), lambda j, sb: (sb, kc + j)),
                  pl.BlockSpec((BLK * r, 128), lambda j, sb: (before(sb), vc + j)),
                  pl.BlockSpec((rows, 128), lambda j, sb: (sb, vc + j))],
        out_specs=[pl.BlockSpec((rows, 128), lambda j, sb: (sb, j))] * 2,
        out_shape=[jax.ShapeDtypeStruct((s, 512), F32)] * 2,
        compiler_params=_cp(("parallel", "parallel")),
    )(slopes, proj, proj, proj, proj, proj)


def _attn_b_bwd(proj, slopes, d_o, o, lse, r):
    s = proj.shape[0]
    nq = B_BLOCKS_PER_STEP[r]
    rows = BLK * r * nq
    steps = s // rows
    qc, kc, vc = WA // 128, WA // 128 + 4, WA // 128 + 8

    def body(slope_ref, q_ref, kp_ref, kc_ref, vp_ref, vc_ref, do_ref, o_ref, lse_ref,
             dq_ref, dk_ref, dv_ref, kcar, vcar):
        j = pl.program_id(0)
        sb = pl.program_id(1)

        @pl.when(sb == 0)
        def _():
            kcar[...] = jnp.zeros_like(kcar)
            vcar[...] = jnp.zeros_like(vcar)

        dk_ref[...] = kcar[...]
        dv_ref[...] = vcar[...]

        @pl.when(sb < steps)
        def _():
            base_rest = _band_base(B_MAX_DIST, r, False)
            base_0 = jnp.where(sb > 0, base_rest, _band_base(B_MAX_DIST, r, True))
            sl2 = (slope_ref[2 * j], slope_ref[2 * j + 1])

            def stream(rho):
                for i in range(nq):
                    cur = _stream(rho, i, r)
                    k_prev = kc_ref[_stream(rho, i - 1, r), :] if i > 0 else kp_ref[_stream(rho, 0, r), :]
                    v_prev = vc_ref[_stream(rho, i - 1, r), :] if i > 0 else vp_ref[_stream(rho, 0, r), :]
                    kb = jnp.concatenate([k_prev, kc_ref[cur, :]], axis=0).astype(BF16)
                    vb = jnp.concatenate([v_prev, vc_ref[cur, :]], axis=0).astype(BF16)
                    dq2, dk2, dv2, _ = _pair_bwd(q_ref[cur, :], kb, vb, do_ref[cur, :], o_ref[cur, :], lse_ref[cur, :],
                                                 base_rest if i > 0 else base_0, sl2, (0, 1), None)
                    dq_ref[cur, :] = dq2
                    if i == 0:
                        last = _stream(rho, nq - 1, r)
                        dk_ref[last, :] += dk2[:BLK]
                        dv_ref[last, :] += dv2[:BLK]
                    else:
                        kcar[_stream(rho, i - 1, r), :] += dk2[:BLK]
                        vcar[_stream(rho, i - 1, r), :] += dv2[:BLK]
                    kcar[cur, :] = dk2[BLK:]
                    vcar[cur, :] = dv2[BLK:]

            _for_streams(r, stream)

    cur_step = lambda sb: jnp.minimum(sb, steps - 1)
    before = lambda sb: jnp.maximum(cur_step(sb) * nq - 1, 0)
    out_prev = lambda sb: jnp.maximum(sb - 1, 0)
    tile = lambda col: pl.BlockSpec((rows, 128), lambda j, sb: (cur_step(sb), col + j))
    edge = lambda col: pl.BlockSpec((BLK * r, 128), lambda j, sb: (before(sb), col + j))
    return pl.pallas_call(
        body, name=f"attn_b_bwd_r{r}", grid=(NH // 2, steps + 1),
        in_specs=[SMEM, tile(qc), edge(kc), tile(kc), edge(vc), tile(vc), tile(0), tile(0), tile(0)],
        out_specs=[tile(0),
                   pl.BlockSpec((rows, 128), lambda j, sb: (out_prev(sb), j)),
                   pl.BlockSpec((rows, 128), lambda j, sb: (out_prev(sb), j))],
        out_shape=[jax.ShapeDtypeStruct((s, 512), F32)] * 3,
        scratch_shapes=[pltpu.VMEM((rows, 128), F32), pltpu.VMEM((rows, 128), F32)],
        compiler_params=_cp(("parallel", "arbitrary")),
    )(slopes, proj, proj, proj, proj, proj, d_o, o, lse)


def _row(v):
    return v.reshape(1, -1)


def _layer_norm_stats(z):
    mu = jnp.mean(z, axis=-1, keepdims=True)
    zc = z - mu
    var = jnp.mean(zc * zc, axis=-1, keepdims=True)
    rstd = lax.rsqrt(var + LN_EPS)
    return zc * rstd, rstd


def _layer_norm_bwd(dh, zh, rstd, g):
    dzh = dh * g
    return rstd * (dzh - jnp.mean(dzh, axis=-1, keepdims=True) - zh * jnp.mean(dzh * zh, axis=-1, keepdims=True))


def _rms(o):
    return lax.rsqrt(jnp.mean(o * o, axis=-1, keepdims=True) + RMS_EPS)


def _mix_ln1(x, o_a, o_b, lse_b, norm_a_g, norm_b_g, w_o, ln1_g, ln1_b, tm=256):
    s = x.shape[0]

    def body(x_ref, oa_ref, ob1, ob2, ob3, l1, l2, l3, ga_ref, gb_ref, wo_ref, g_ref, b_ref,
             obm_ref, lse_ref, cat_ref, z1_ref, h1_ref, h1b_ref):
        la, lb, lc = l1[...], l2[...], l3[...]
        m = jnp.maximum(jnp.maximum(la, lb), lc)
        ea, eb, ec = jnp.exp(la - m), jnp.exp(lb - m), jnp.exp(lc - m)
        den = ea + eb + ec
        obm = (ea / den) * ob1[...] + (eb / den) * ob2[...] + (ec / den) * ob3[...]
        obm_ref[...] = obm
        lse_ref[...] = m + jnp.log(den)
        oa = oa_ref[...]
        na = oa * _rms(oa) * ga_ref[...]
        nb_ = obm * _rms(obm) * gb_ref[...]
        cat = jnp.concatenate([na, nb_], axis=1).astype(BF16)
        cat_ref[...] = cat
        z1 = ALPHA * x_ref[...] + _nn(cat, wo_ref[...])
        z1_ref[...] = z1
        zh, _ = _layer_norm_stats(z1)
        h1 = zh * g_ref[...] + b_ref[...]
        h1_ref[...] = h1
        h1b_ref[...] = h1.astype(BF16)

    t512 = pl.BlockSpec((tm, 512), lambda i: (i, 0))
    td = pl.BlockSpec((tm, D), lambda i: (i, 0))
    return pl.pallas_call(
        body, name="mix_ln1", grid=(s // tm,),
        in_specs=[td] + [t512] * 7 + [_const((1, 512))] * 2 + [_resident((D, D))] + [_const((1, D))] * 2,
        out_specs=[t512, t512, td, td, td, td],
        out_shape=[jax.ShapeDtypeStruct((s, 512), F32), jax.ShapeDtypeStruct((s, 512), F32),
                   jax.ShapeDtypeStruct((s, D), BF16), jax.ShapeDtypeStruct((s, D), F32),
                   jax.ShapeDtypeStruct((s, D), F32), jax.ShapeDtypeStruct((s, D), BF16)],
        compiler_params=_cp(("parallel",)),
    )(x, o_a, *o_b, *lse_b, _row(norm_a_g), _row(norm_b_g), w_o, _row(ln1_g), _row(ln1_b))


def _gelu_and_grad(x):
    c = math.sqrt(2.0 / math.pi)
    x2 = x * x
    cx = c * x
    t = jnp.tanh(cx * (1.0 + 0.044715 * x2))
    q = 1.0 + t
    g = (0.5 * x) * q
    dg = 0.5 * q + ((0.5 * cx) * (1.0 - t * t)) * (1.0 + (3.0 * 0.044715) * x2)
    return g, dg


CONV_CHUNK = 64


def _shift_down(u, before):
    n = u.shape[0]
    ext = jnp.concatenate([before, u], axis=0)
    return pltpu.roll(ext, 1, 0)[8:], pltpu.roll(ext, 2, 0)[8:]


def _shift_up(u, after):
    n = u.shape[0]
    ext = jnp.concatenate([u, after], axis=0)
    return pltpu.roll(ext, n + 7, 0)[:n], pltpu.roll(ext, n + 6, 0)[:n]


def _up_conv_gelu(h1b, w_up, cwb, tm=512, tn=256):
    s = h1b.shape[0]
    n_i = s // tm
    n_t = (FF // tn) * n_i

    def body(h_ref, wg_ref, wv_ref, c_ref, up_ref, a_ref, g_ref, a1_ref, pend_a, pend_b, carry):
        t = pl.program_id(0)
        row_tile = jnp.maximum(t - 1, 0) % n_i
        w_refs = (wg_ref, wv_ref)

        @pl.when(t == 0)
        def _():
            pend_b[...] = jnp.zeros_like(pend_b)
            carry[...] = jnp.zeros_like(carry)

        def step(dst, src):
            def chunk(c, before):
                rows = pl.ds(c * CONV_CHUNK, CONV_CHUNK)
                u, last = [], []
                for half in (0, 1):
                    up = src[half, rows, :]
                    r1, r2 = _shift_down(up, before[half])
                    u.append(r2 * c_ref[0, half:half + 1, :] + r1 * c_ref[1, half:half + 1, :]
                             + up * c_ref[2, half:half + 1, :] + c_ref[3, half:half + 1, :])
                    last.append(up[CONV_CHUNK - 8:])
                g, dg = _gelu_and_grad(u[0])
                a_ref[rows, :] = (g * u[1]).astype(BF16)
                g_ref[rows, :] = g.astype(BF16)
                a1_ref[rows, :] = (u[1] * dg).astype(BF16)
                return tuple(last)

            edge = tuple(jnp.where(row_tile > 0, carry[half], 0.0) for half in (0, 1))
            n_c = tm // CONV_CHUNK
            n_k = n_c // 2
            tk = D // n_k
            for half in (0, 1):
                up = None
                for kq in range(n_k):
                    ks = slice(kq * tk, (kq + 1) * tk)
                    part = _nn(h_ref[:, ks], w_refs[half][ks, :])
                    up = part if kq == 0 else up + part
                    edge = chunk(half * n_k + kq, edge)
                up_ref[half] = up
                dst[half] = up
            for half in (0, 1):
                carry[half] = edge[half]

        @pl.when(t % 2 == 0)
        def _():
            step(pend_a, pend_b)

        @pl.when(t % 2 == 1)
        def _():
            step(pend_b, pend_a)

    mm = lambda t: jnp.minimum(t, n_t - 1)
    ew = lambda t: jnp.maximum(t - 1, 0)
    out_tile = pl.BlockSpec((tm, tn), lambda t: (ew(t) % n_i, ew(t) // n_i))
    return pl.pallas_call(
        body, name="up_conv_gelu", grid=(n_t + 1,),
        in_specs=[pl.BlockSpec((tm, D), lambda t: (mm(t) % n_i, 0)),
                  pl.BlockSpec((D, tn), lambda t: (0, mm(t) // n_i)),
                  pl.BlockSpec((D, tn), lambda t: (0, FF // tn + mm(t) // n_i)),
                  pl.BlockSpec((4, 2, tn), lambda t: (0, 0, ew(t) // n_i))],
        out_specs=[pl.BlockSpec((2, tm, tn), lambda t: (0, mm(t) % n_i, mm(t) // n_i)), out_tile, out_tile, out_tile],
        out_shape=[jax.ShapeDtypeStruct((2, s, FF), F32)] + [jax.ShapeDtypeStruct((s, FF), BF16)] * 3,
        scratch_shapes=[pltpu.VMEM((2, tm, tn), F32), pltpu.VMEM((2, tm, tn), F32), pltpu.VMEM((2, 8, tn), F32)],
        compiler_params=_cp(("arbitrary",)),
    )(h1b, w_up, w_up, cwb)


def _down_ln2_loss(a, w_down, h1, target, ln2_g, ln2_b, tm=256):
    s = a.shape[0]

    def body(a_ref, w_ref, h_ref, t_ref, g_ref, b_ref, dz_ref, dzb_ref, st_ref):
        @pl.when(pl.program_id(0) == 0)
        def _():
            st_ref[...] = jnp.zeros_like(st_ref)

        z2 = ALPHA * h_ref[...] + _nn(a_ref[...], w_ref[...])
        zh, rstd = _layer_norm_stats(z2)
        diff = zh * g_ref[...] + b_ref[...] - t_ref[...]
        part = 0.5 * jnp.sum(jnp.mean(diff * diff, axis=-1, keepdims=True), axis=0, keepdims=True)
        dy = diff * (1.0 / D)
        st_ref[0:1, :] += jnp.sum(dy * zh, axis=0, keepdims=True)
        st_ref[1:2, :] += jnp.sum(dy, axis=0, keepdims=True)
        st_ref[2:3, :] += jnp.broadcast_to(part, (1, D))
        dz = _layer_norm_bwd(dy, zh, rstd, g_ref[...])
        dz_ref[...] = dz
        dzb_ref[...] = dz.astype(BF16)

    td = pl.BlockSpec((tm, D), lambda i: (i, 0))
    return pl.pallas_call(
        body, name="down_ln2_loss", grid=(s // tm,),
        in_specs=[pl.BlockSpec((tm, FF), lambda i: (i, 0)), _resident((FF, D)), td, td, _const((1, D)), _const((1, D))],
        out_specs=[td, td, _const((8, D))],
        out_shape=[jax.ShapeDtypeStruct((s, D), F32), jax.ShapeDtypeStruct((s, D), BF16),
                   jax.ShapeDtypeStruct((8, D), F32)],
        compiler_params=_cp(("arbitrary",)),
    )(a, w_down, h1, target, _row(ln2_g), _row(ln2_b))


def _conv_gelu_bwd(dz2b, w_down_t, up, g, a1, cwb, tm=512, tn=256):
    s = dz2b.shape[0]
    n_i = s // tm
    n_t = (FF // tn) * n_i

    def body(dz_ref, w_ref, up_ref, g_ref, a1_ref, c_ref, dup_ref, dc_ref, pend_a, pend_b, carry):
        t = pl.program_id(0)
        first = jnp.maximum(t - 1, 0) % n_i == 0

        @pl.when(t == 0)
        def _():
            pend_b[...] = jnp.zeros_like(pend_b)

        @pl.when(first)
        def _():
            carry[...] = jnp.zeros_like(carry)
            dc_ref[...] = jnp.zeros_like(dc_ref)

        def step(dst, src):
            n_c = tm // CONV_CHUNK

            def chunk(cc, after):
                rows = pl.ds((n_c - 1 - cc) * CONV_CHUNK, CONV_CHUNK)
                da = src[rows, :]
                dus = (da * a1_ref[rows, :].astype(F32), da * g_ref[rows, :].astype(F32))
                head = []
                for half in (0, 1):
                    du = dus[half]
                    up = up_ref[half, rows, :]
                    l1, l2 = _shift_up(du, after[half])
                    dup = (du * c_ref[2, half:half + 1, :] + l1 * c_ref[1, half:half + 1, :]
                           + l2 * c_ref[0, half:half + 1, :])
                    dup_ref[half, rows, :] = dup.astype(BF16)
                    dc_ref[0, half:half + 1, :] += jnp.sum(l2 * up, axis=0, keepdims=True)
                    dc_ref[1, half:half + 1, :] += jnp.sum(l1 * up, axis=0, keepdims=True)
                    dc_ref[2, half:half + 1, :] += jnp.sum(du * up, axis=0, keepdims=True)
                    dc_ref[3, half:half + 1, :] += jnp.sum(du, axis=0, keepdims=True)
                    head.append(du[:8])
                return tuple(head)

            head = (carry[0], carry[1])
            n_k = n_c // 2
            tk = D // n_k
            da = None
            for kq in range(n_k):
                ks = slice(kq * tk, (kq + 1) * tk)
                part = _nn(dz_ref[:, ks], w_ref[ks, :])
                da = part if kq == 0 else da + part
                head = chunk(2 * kq, head)
                head = chunk(2 * kq + 1, head)
            for half in (0, 1):
                carry[half] = head[half]
            dst[...] = da

        @pl.when(t % 2 == 0)
        def _():
            step(pend_a, pend_b)

        @pl.when(t % 2 == 1)
        def _():
            step(pend_b, pend_a)

    mm = lambda t: jnp.minimum(t, n_t - 1)
    ew = lambda t: jnp.maximum(t - 1, 0)
    row = lambda t: n_i - 1 - t % n_i
    ew_tile = pl.BlockSpec((tm, tn), lambda t: (row(ew(t)), ew(t) // n_i))
    ew_pair = pl.BlockSpec((2, tm, tn), lambda t: (0, row(ew(t)), ew(t) // n_i))
    per_col = pl.BlockSpec((4, 2, tn), lambda t: (0, 0, ew(t) // n_i))
    return pl.pallas_call(
        body, name="conv_gelu_bwd", grid=(n_t + 1,),
        in_specs=[pl.BlockSpec((tm, D), lambda t: (row(mm(t)), 0)),
                  pl.BlockSpec((D, tn), lambda t: (0, mm(t) // n_i)),
                  ew_pair, ew_tile, ew_tile, per_col],
        out_specs=[ew_pair, per_col],
        out_shape=[jax.ShapeDtypeStruct((2, s, FF), BF16), jax.ShapeDtypeStruct((4, 2, FF), F32)],
        scratch_shapes=[pltpu.VMEM((tm, tn), F32), pltpu.VMEM((tm, tn), F32), pltpu.VMEM((2, 8, tn), F32)],
        compiler_params=_cp(("arbitrary",)),
    )(dz2b, w_down_t, up, g, a1, cwb)


def _dh1_ln1_bwd(dz2, dup, w_up_t, z1, ln1_g, tm=256):
    s = dz2.shape[0]

    def body(dz2_ref, dup_ref, w_ref, z1_ref, g_ref, dz1_ref, dz1b_ref, st_ref):
        @pl.when(pl.program_id(0) == 0)
        def _():
            st_ref[...] = jnp.zeros_like(st_ref)

        dh = ALPHA * dz2_ref[...] + _nn(dup_ref[0], w_ref[0]) + _nn(dup_ref[1], w_ref[1])
        zh, rstd = _layer_norm_stats(z1_ref[...])
        st_ref[0:1, :] += jnp.sum(dh * zh, axis=0, keepdims=True)
        st_ref[1:2, :] += jnp.sum(dh, axis=0, keepdims=True)
        dz = _layer_norm_bwd(dh, zh, rstd, g_ref[...])
        dz1_ref[...] = dz
        dz1b_ref[...] = dz.astype(BF16)

    td = pl.BlockSpec((tm, D), lambda i: (i, 0))
    return pl.pallas_call(
        body, name="dh1_ln1_bwd", grid=(s // tm,),
        in_specs=[td, pl.BlockSpec((2, tm, FF), lambda i: (0, i, 0)), _resident((2, FF, D)), td, _const((1, D))],
        out_specs=[td, td, _const((8, D))],
        out_shape=[jax.ShapeDtypeStruct((s, D), F32), jax.ShapeDtypeStruct((s, D), BF16),
                   jax.ShapeDtypeStruct((8, D), F32)],
        compiler_params=_cp(("arbitrary",)),
    )(dz2, dup, w_up_t, z1, _row(ln1_g))


def _dcat_rms_bwd(dz1b, w_o, o_a, o_b, norm_a_g, norm_b_g, tm=256):
    s = dz1b.shape[0]

    def body(dz_ref, w_ref, oa_ref, ob_ref, ga_ref, gb_ref, da_ref, db_ref, st_ref):
        @pl.when(pl.program_id(0) == 0)
        def _():
            st_ref[...] = jnp.zeros_like(st_ref)

        dcat = _nt(dz_ref[...], w_ref[...])
        for k, (o_ref, g_ref, d_ref) in enumerate(((oa_ref, ga_ref, da_ref), (ob_ref, gb_ref, db_ref))):
            o = o_ref[...]
            dn = dcat[:, 512 * k:512 * (k + 1)]
            rr = _rms(o)
            oh = o * rr
            st_ref[k:k + 1, :] += jnp.sum(dn * oh, axis=0, keepdims=True)
            doh = dn * g_ref[...]
            d_ref[...] = rr * (doh - oh * jnp.mean(doh * oh, axis=-1, keepdims=True))

    t512 = pl.BlockSpec((tm, 512), lambda i: (i, 0))
    return pl.pallas_call(
        body, name="dcat_rms_bwd", grid=(s // tm,),
        in_specs=[pl.BlockSpec((tm, D), lambda i: (i, 0)), _resident((D, D)), t512, t512,
                  _const((1, 512)), _const((1, 512))],
        out_specs=[t512, t512, _const((8, 512))],
        out_shape=[jax.ShapeDtypeStruct((s, 512), F32), jax.ShapeDtypeStruct((s, 512), F32),
                   jax.ShapeDtypeStruct((8, 512), F32)],
        compiler_params=_cp(("arbitrary",)),
    )(dz1b, w_o, o_a, o_b, _row(norm_a_g), _row(norm_b_g))


def _dproj_combine(dqa, dka, dva, dqkv_b, tm=256):
    s = dqa.shape[0]

    def body(qa, ka, va, q1, k1, v1, q2, k2, v2, q3, k3, v3, o_ref):
        o_ref[:, 0:512] = qa[...].astype(BF16)
        o_ref[:, 512:640] = ka[...].astype(BF16)
        o_ref[:, 640:768] = va[...].astype(BF16)
        o_ref[:, 768:1280] = (q1[...] + q2[...] + q3[...]).astype(BF16)
        o_ref[:, 1280:1792] = (k1[...] + k2[...] + k3[...]).astype(BF16)
        o_ref[:, 1792:2304] = (v1[...] + v2[...] + v3[...]).astype(BF16)

    t512 = pl.BlockSpec((tm, 512), lambda i: (i, 0))
    t128 = pl.BlockSpec((tm, 128), lambda i: (i, 0))
    flat = [a for trio in dqkv_b for a in trio]
    return pl.pallas_call(
        body, name="dproj_combine", grid=(s // tm,),
        in_specs=[t512, t128, t128] + [t512] * 9,
        out_specs=pl.BlockSpec((tm, WIN), lambda i: (i, 0)),
        out_shape=jax.ShapeDtypeStruct((s, WIN), BF16),
        compiler_params=_cp(("parallel",)),
    )(dqa, dka, dva, *flat)


def _grad_x(dz1, dproj, w_in_t, zero, tm=256):
    s = dz1.shape[0]

    def body(dz_ref, dp_ref, w_ref, z_ref, o_ref):
        o_ref[...] = ALPHA * dz_ref[...] + _nn(dp_ref[...], w_ref[...]) + z_ref[0:1, 0:1]

    td = pl.BlockSpec((tm, D), lambda i: (i, 0))
    return pl.pallas_call(
        body, name="grad_x", grid=(s // tm,),
        in_specs=[td, pl.BlockSpec((tm, WIN), lambda i: (i, 0)), _resident((WIN, D)), _const((8, 128))],
        out_specs=td, out_shape=jax.ShapeDtypeStruct((s, D), F32),
        compiler_params=_cp(("parallel",)),
    )(dz1, dproj, w_in_t, zero)


def _place():
    return lax.axis_index("x"), lax.axis_index("y"), lax.axis_index("c")


def _other_chips(x, y):
    return [(1 - x, y), (x, 1 - y), (1 - x, 1 - y)]


def _hbm(a):
    return pltpu.with_memory_space_constraint(a, pltpu.HBM)


def _gather_w_in(shard, conv_w):
    rows_k = shard.shape[0]
    half = rows_k // 2

    def body(src, conv_src, out, conv_out, send_sems, recv_sems):
        x, y, c = _place()
        b = 2 * x + y
        sibling = (x, y, 1 - c)
        chips = _other_chips(x, y)

        def copy(idx, chip_b, core, to, first_hop=False):
            rows = out.at[pl.ds(pl.multiple_of(chip_b * rows_k + core * half, 16), half)]
            s_ref = src.at[pl.ds(pl.multiple_of(core * half, 16), half)] if first_hop else rows
            return pltpu.make_async_remote_copy(src_ref=s_ref, dst_ref=rows, send_sem=send_sems.at[idx],
                                                recv_sem=recv_sems.at[idx], device_id=to, device_id_type=MESH)

        def own_copy():
            return pltpu.make_async_remote_copy(
                src_ref=src, dst_ref=out.at[pl.ds(pl.multiple_of(b * rows_k, 16), rows_k)], send_sem=send_sems.at[6],
                recv_sem=recv_sems.at[6], device_id=sibling, device_id_type=MESH)

        def conv_copy(idx, chip_b, to):
            return pltpu.make_async_remote_copy(src_ref=conv_src, dst_ref=conv_out.at[chip_b],
                                                send_sem=send_sems.at[7 + idx], recv_sem=recv_sems.at[7 + idx],
                                                device_id=to, device_id_type=MESH)

        started = [own_copy(), conv_copy(3, b, sibling)]
        for jn, chip in enumerate(chips):
            started += [copy(jn, b, c, (chip[0], chip[1], c), first_hop=True), conv_copy(jn, b, (chip[0], chip[1], c))]
        for cp in started:
            cp.start()
        for jn, chip in enumerate(chips):
            cb = 2 * chip[0] + chip[1]
            copy(jn, cb, c, (chip[0], chip[1], c)).wait_recv()
            cp = copy(3 + jn, cb, c, sibling)
            cp.start()
            started.append(cp)
        for jn, chip in enumerate(chips):
            cb = 2 * chip[0] + chip[1]
            copy(3 + jn, cb, 1 - c, sibling).wait_recv()
            conv_copy(jn, cb, (chip[0], chip[1], c)).wait_recv()
        own_copy().wait_recv()
        conv_copy(3, b, sibling).wait_recv()
        for cp in started:
            cp.wait_send()

    return pl.pallas_call(
        body, name="gather_w_in",
        in_specs=[ANY, ANY], out_specs=[ANY, ANY],
        out_shape=[jax.ShapeDtypeStruct((N_CHIPS * rows_k, D), BF16), jax.ShapeDtypeStruct((N_CHIPS,) + conv_w.shape, F32)],
        scratch_shapes=[pltpu.SemaphoreType.DMA((11,)), pltpu.SemaphoreType.DMA((11,))],
        compiler_params=pltpu.CompilerParams(has_side_effects=True),
    )(shard, conv_w)


def _weight_copies(shard, land, send_sems, recv_sems, arrivals):
    x, y, c = _place()
    rows_k = shard.shape[0]
    peers = [(px, py, c) for px, py in _other_chips(x, y)] + [(x, y, 1 - c)]
    cps = []
    for jn, peer in enumerate(peers):
        at = 2 * peer[0] + peer[1] if arrivals else 2 * x + y
        cps.append(pltpu.make_async_remote_copy(
            src_ref=shard, dst_ref=land.at[pl.ds(pl.multiple_of(at * rows_k, 16), rows_k)],
            send_sem=send_sems.at[jn], recv_sem=recv_sems.at[jn], device_id=peer, device_id_type=MESH))
    return cps


def _weights_start(shards):
    n = len(shards)
    lands = [lax.empty((N_CHIPS * sh.shape[0], D), BF16) for sh in shards]

    def body(*refs):
        src, land = refs[:n], refs[n:2 * n]
        send_sems, recv_sems = refs[2 * n:3 * n], refs[3 * n:4 * n]
        for k in range(n):
            for send in _weight_copies(src[k], land[k], send_sems[k], recv_sems[k], False):
                send.start()
        refs[-1][...] = jnp.zeros_like(refs[-1])

    res = pl.pallas_call(
        body, name="weights_start",
        in_specs=[HBM] * (2 * n), out_specs=[SEM] * (2 * n) + [HBM] * (2 * n) + [VMEM],
        out_shape=[pltpu.SemaphoreType.DMA((4,))] * (2 * n)
        + [pltpu.HBM(a.shape, a.dtype) for a in (*shards, *lands)] + [jax.ShapeDtypeStruct((8, 128), F32)],
        input_output_aliases={i: i + 2 * n for i in range(2 * n)},
        compiler_params=pltpu.CompilerParams(has_side_effects=DATAFLOW),
    )(*[_hbm(a) for a in (*shards, *lands)])
    return [(res[k], res[n + k], res[2 * n + k], res[3 * n + k]) for k in range(n)], res[-1]


def _weights_wait(started, after, name):
    send_sems, recv_sems, shard, land = started

    def body(s_ref, l_ref, send_ref, recv_ref, after_ref, s_out, l_out):
        for cp in _weight_copies(s_ref, l_ref, send_ref, recv_ref, True):
            cp.wait_send()
            cp.wait_recv()

    return pl.pallas_call(
        body, name=name,
        in_specs=[HBM, HBM, SEM, SEM, ANY], out_specs=[HBM, HBM],
        out_shape=[pltpu.HBM(shard.shape, shard.dtype), pltpu.HBM(land.shape, land.dtype)],
        input_output_aliases={0: 0, 1: 1},
        compiler_params=pltpu.CompilerParams(has_side_effects=DATAFLOW),
    )(shard, land, send_sems, recv_sems, after)[1]


def _grad_copies(g_ref, land_ref, send_sems, recv_sems):
    x, y, c = _place()
    cps = []
    for d in range(1, 8):
        px, py, pc = x ^ (d >> 2), y ^ ((d >> 1) & 1), c ^ (d & 1)
        cps.append(pltpu.make_async_remote_copy(
            src_ref=g_ref.at[2 * px + py, pc], dst_ref=land_ref.at[d - 1], send_sem=send_sems.at[d - 1],
            recv_sem=recv_sems.at[d - 1], device_id=(px, py, pc), device_id_type=MESH))
    return cps


def _grads_start(grads_b, name):
    n = len(grads_b)
    lands = [lax.empty((7, g.shape[2], D), BF16) for g in grads_b]

    def body(*refs):
        g, land = refs[:n], refs[n:2 * n]
        send_sems, recv_sems = refs[2 * n:3 * n], refs[3 * n:4 * n]
        for k in range(n):
            for cp in _grad_copies(g[k], land[k], send_sems[k], recv_sems[k]):
                cp.start()
        refs[-1][...] = jnp.zeros_like(refs[-1])

    res = pl.pallas_call(
        body, name=name,
        in_specs=[HBM] * (2 * n), out_specs=[SEM] * (2 * n) + [HBM] * (2 * n) + [VMEM],
        out_shape=[pltpu.SemaphoreType.DMA((7,))] * (2 * n)
        + [pltpu.HBM(a.shape, a.dtype) for a in (*grads_b, *lands)] + [jax.ShapeDtypeStruct((8, 128), F32)],
        input_output_aliases={i: i + 2 * n for i in range(2 * n)},
        compiler_params=pltpu.CompilerParams(has_side_effects=DATAFLOW),
    )(*[_hbm(a) for a in (*grads_b, *lands)])
    return [(res[k], res[n + k], res[2 * n + k], res[3 * n + k]) for k in range(n)], res[-1]


def _grads_wait(started, after):
    n = len(started)

    def body(*refs):
        g, land = refs[:n], refs[n:2 * n]
        send_sems, recv_sems = refs[2 * n:3 * n], refs[3 * n:4 * n]
        for k in range(n):
            for cp in _grad_copies(g[k], land[k], send_sems[k], recv_sems[k]):
                cp.wait_send()
                cp.wait_recv()

    gs = [st[2] for st in started]
    lands = [st[3] for st in started]
    res = pl.pallas_call(
        body, name="grads_wait",
        in_specs=[HBM] * (2 * n) + [SEM] * (2 * n) + [ANY], out_specs=[HBM] * (2 * n),
        out_shape=[pltpu.HBM(a.shape, a.dtype) for a in (*gs, *lands)],
        input_output_aliases={i: i for i in range(2 * n)},
        compiler_params=pltpu.CompilerParams(has_side_effects=DATAFLOW),
    )(*gs, *lands, *[st[0] for st in started], *[st[1] for st in started], after)
    return res[n:]


def _sum_partials(grad4, got, cb, name, tr):
    h = grad4.shape[2]
    per_half = h // tr

    def body(cb_ref, g_ref, o_ref, out_ref):
        acc = g_ref[...]
        for j in range(7):
            acc = acc + o_ref[j].astype(F32)
        out_ref[...] = acc

    return pl.pallas_call(
        body, name=name,
        grid_spec=pltpu.PrefetchScalarGridSpec(
            num_scalar_prefetch=1, grid=(per_half,),
            in_specs=[pl.BlockSpec((None, None, tr, D), lambda i, cb_ref: (cb_ref[1], cb_ref[0], i, 0)),
                      pl.BlockSpec((7, tr, D), lambda i, cb_ref: (0, i, 0))],
            out_specs=pl.BlockSpec((tr, D), lambda i, cb_ref: (cb_ref[0] * per_half + i, 0))),
        out_shape=jax.ShapeDtypeStruct((2 * h, D), F32),
        compiler_params=_cp(("arbitrary",)),
    )(cb, grad4, got)


def _share_halves(shards, small):
    n = len(shards)
    rows = small.shape[0]

    def body(*refs):
        small_ref = refs[n]
        out, total_ref = refs[n + 1:2 * n + 1], refs[2 * n + 1]
        all_ref, send_sems, recv_sems, ssend, srecv = refs[2 * n + 2:]
        x, y, c = _place()
        me = 4 * x + 2 * y + c
        cps = []
        for k in range(n):
            h = shards[k].shape[0] // 2
            mine = out[k].at[pl.ds(pl.multiple_of(c * h, 8), h)]
            cp = pltpu.make_async_remote_copy(src_ref=mine, dst_ref=mine, send_sem=send_sems.at[k],
                                              recv_sem=recv_sems.at[k], device_id=(x, y, 1 - c), device_id_type=MESH)
            cp.start()
            cps.append(cp)
        all_ref[me] = small_ref[...]
        peers = []
        for d in range(1, 8):
            px, py, pc = x ^ (d >> 2), y ^ ((d >> 1) & 1), c ^ (d & 1)
            cp = pltpu.make_async_remote_copy(src_ref=small_ref, dst_ref=all_ref.at[me],
                                              send_sem=ssend.at[d - 1], recv_sem=srecv.at[d - 1],
                                              device_id=(px, py, pc), device_id_type=MESH)
            cp.start()
            peers.append(cp)
        for cp in peers:
            cp.wait()
        acc = all_ref[0]
        for d in range(1, 8):
            acc = acc + all_ref[d]
        total_ref[...] = acc
        for cp in cps:
            cp.wait()

    return pl.pallas_call(
        body, name="share_halves",
        in_specs=[ANY] * n + [VMEM], out_specs=[ANY] * n + [VMEM],
        out_shape=[jax.ShapeDtypeStruct(sh.shape, F32) for sh in shards] + [jax.ShapeDtypeStruct((rows, D), F32)],
        input_output_aliases={k: k for k in range(n)},
        scratch_shapes=[pltpu.VMEM((8, rows, D), F32), pltpu.SemaphoreType.DMA((n,)), pltpu.SemaphoreType.DMA((n,)),
                        pltpu.SemaphoreType.DMA((7,)), pltpu.SemaphoreType.DMA((7,))],
        compiler_params=pltpu.CompilerParams(has_side_effects=True),
    )(*shards, small)


def _adamw(w, g, m, v, name, tr):
    rows, cols = w.shape

    def body(w_ref, g_ref, m_ref, v_ref, d_ref, nm_ref, nv_ref):
        g_ = g_ref[...]
        nm = ADAM_B1 * m_ref[...] + (1.0 - ADAM_B1) * g_
        nv = ADAM_B2 * v_ref[...] + (1.0 - ADAM_B2) * (g_ * g_)
        m_hat = nm / (1.0 - ADAM_B1 ** ADAM_STEP)
        v_hat = nv / (1.0 - ADAM_B2 ** ADAM_STEP)
        d_ref[...] = -ADAM_LR * (m_hat / (jnp.sqrt(v_hat) + ADAM_EPS) + ADAM_WD * w_ref[...])
        nm_ref[...] = nm
        nv_ref[...] = nv

    spec = pl.BlockSpec((tr, cols), lambda i: (i, 0))
    return pl.pallas_call(
        body, name=name, grid=(rows // tr,),
        in_specs=[spec] * 4, out_specs=[spec] * 3,
        out_shape=[jax.ShapeDtypeStruct((rows, cols), F32)] * 3,
        compiler_params=_cp(("parallel",)),
    )(w, g, m, v)


def _local_step(x, target, w_in_t, late_weights, norm_a_g, norm_b_g, sinks_a, ln1_g, ln1_b,
                conv_w, conv_b, ln2_g, ln2_b, slopes, on_grad):
    cwb = jnp.concatenate([conv_w, conv_b[None]], axis=0).reshape(4, 2, FF)

    proj = _proj(x, w_in_t, "proj")
    o_a, lse_a = _attn_a_fwd(proj, sinks_a)
    fwd_b = [_attn_b_fwd(proj, slopes, r) for r in B_DILATIONS]
    w_o = late_weights(1, fwd_b[-1][1])
    o_b, lse_b, cat, z1, h1, h1b = _mix_ln1(x, o_a, [f[0] for f in fwd_b], [f[1] for f in fwd_b],
                                           norm_a_g, norm_b_g, w_o, ln1_g, ln1_b)
    w_up_t = late_weights(2, h1b)
    w_up3 = w_up_t.reshape(2, FF, D)
    up, a, gate, a1 = _up_conv_gelu(h1b, w_up_t.T, cwb)
    w_down = late_weights(3, a)
    dz2, dz2b, st2 = _down_ln2_loss(a, w_down, h1, target, ln2_g, ln2_b)

    on_grad(3, *_grad_w(a, dz2b, "grad_w_down", tm=FF // 2))
    dup, dconv = _conv_gelu_bwd(dz2b, w_down.T, up, gate, a1, cwb)
    on_grad(2, *_grad_w(dup, h1b, "grad_w_up", tm=FF // 2, lhs_halves=True))
    dz1, dz1b, st1 = _dh1_ln1_bwd(dz2, dup, w_up3, z1, ln1_g)
    tok = on_grad(1, *_grad_w(cat, dz1b, "grad_w_o", tm=512))
    d_oa, d_ob, st_n = _dcat_rms_bwd(dz1b, w_o, o_a, o_b, norm_a_g + tok[0, 0], norm_b_g)
    dqa, dka, dva, dsink = _attn_a_bwd(proj, sinks_a, d_oa, o_a, lse_a)
    bwd_b = [_attn_b_bwd(proj, slopes, d_ob, o_b, lse_b, r) for r in B_DILATIONS]
    dproj = _dproj_combine(dqa, dka, dva, bwd_b)
    tok = on_grad(0, *_grad_w(dproj, x, "grad_w_in", tm=WA))
    gx = _grad_x(dz1, dproj, w_in_t, tok)

    dconv = dconv.reshape(4, 2 * FF)
    small = dict(loss=st2[2, 0:1], norm_a_g=st_n[0], norm_b_g=st_n[1], sinks_a=dsink[:, 0],
                 ln1_g=st1[0], ln1_b=st1[1], conv_w=dconv[0:3].reshape(-1), conv_b=dconv[3],
                 ln2_g=st2[0], ln2_b=st2[1])
    return gx, small


SMALL_ORDER = ("loss", "norm_a_g", "norm_b_g", "sinks_a", "ln1_g", "ln1_b", "conv_b", "ln2_g", "ln2_b", "conv_w")
SMALL_SIZES = dict(loss=1, norm_a_g=512, norm_b_g=512, sinks_a=8, ln1_g=D, ln1_b=D, conv_b=2 * FF, ln2_g=D, ln2_b=D,
                   conv_w=3 * 2 * FF)


def _pack(parts, rows):
    flat = jnp.concatenate([parts[k].reshape(-1).astype(F32) for k in parts])
    return jnp.pad(flat, (0, rows * D - flat.shape[0])).reshape(rows, D)


def _unpack(buf, names, sizes):
    flat = buf.reshape(-1)
    out, at = {}, 0
    for k in names:
        out[k] = flat[at:at + sizes[k]]
        at += sizes[k]
    return out


def kernel(x, w_in, norm_a_g, norm_b_g, sinks_a, w_o, ln1_g, ln1_b, w_up, conv_w, conv_b, w_down, ln2_g, ln2_b, loss_target, m_w_in, m_norm_a_g, m_norm_b_g, m_sinks_a, m_w_o, m_ln1_g, m_ln1_b, m_w_up, m_conv_w, m_conv_b, m_w_down, m_ln2_g, m_ln2_b, v_w_in, v_norm_a_g, v_norm_b_g, v_sinks_a, v_w_o, v_ln1_g, v_ln1_b, v_w_up, v_conv_w, v_conv_b, v_w_down, v_ln2_g, v_ln2_b):
    xi, yi, ci = _place()
    chip = (2 * xi + yi).astype(I32)
    core = ci.astype(I32)

    w_in_rows, m_w_in_rows, v_w_in_rows = w_in.T, m_w_in.T, v_w_in.T
    shards = (w_in_rows.astype(BF16), w_o.astype(BF16), w_up.T.astype(BF16), w_down.astype(BF16))
    w_in_t, conv_w4 = _gather_w_in(shards[0], conv_w)
    conv_w_f = conv_w4.transpose(1, 0, 2).reshape(3, 2 * FF)
    w_started, w_tok = _weights_start(shards[1:])
    slopes = jnp.asarray(SLOPES, F32) + w_tok[0, 0]

    halves_rows = [r // 2 for r in SHARD_ROWS]
    grads4, grads_b4, started = [None] * 4, [None] * 4, [None] * 4

    def on_grad(k, g, g_b):
        grads4[k] = g.reshape(N_CHIPS, 2, halves_rows[k], D)
        grads_b4[k] = g_b.reshape(N_CHIPS, 2, halves_rows[k], D)
        if k > 1:
            return None
        group = (1, 2, 3) if k == 1 else (0,)
        sts, tok = _grads_start([grads_b4[i] for i in group], f"grads_start_{k}")
        for i, st in zip(group, sts):
            started[i] = st
        return tok

    gx, small = _local_step(
        x[0], loss_target[0], w_in_t, lambda k, after: _weights_wait(w_started[k - 1], after, f"weights_wait_{k}"),
        norm_a_g, norm_b_g, sinks_a, ln1_g, ln1_b, conv_w_f, conv_b, ln2_g, ln2_b, slopes, on_grad)

    got = _grads_wait(started, gx)
    tiles = (96, 128, 352, 176)
    core_chip = jnp.stack([core, chip])
    halves = [_sum_partials(grads4[k], got[k], core_chip, f"sum_partials_{k}", tiles[k]) for k in range(4)]
    small_rows = 32
    *full, totals = _share_halves(halves, _pack({k: small[k] for k in SMALL_ORDER}, small_rows))
    tot = _unpack(totals, SMALL_ORDER, SMALL_SIZES)

    g_w_in_rows, g_w_o, g_w_up, g_w_down = full[0], full[1], full[2].T, full[3]
    loss = tot["loss"][0]
    cols = 2 * FF // N_CHIPS
    g_conv_w = lax.dynamic_slice(tot["conv_w"].reshape(3, 2 * FF), (0, chip * cols), (3, cols))
    g_small = dict(norm_a_g=tot["norm_a_g"], norm_b_g=tot["norm_b_g"], sinks_a=tot["sinks_a"], ln1_g=tot["ln1_g"],
                   ln1_b=tot["ln1_b"], conv_w=g_conv_w, conv_b=tot["conv_b"], ln2_g=tot["ln2_g"], ln2_b=tot["ln2_b"])

    weights = dict(w_in=w_in, norm_a_g=norm_a_g, norm_b_g=norm_b_g, sinks_a=sinks_a, w_o=w_o, ln1_g=ln1_g, ln1_b=ln1_b,
                   w_up=w_up, conv_w=conv_w, conv_b=conv_b, w_down=w_down, ln2_g=ln2_g, ln2_b=ln2_b)
    ms = dict(w_in=m_w_in, norm_a_g=m_norm_a_g, norm_b_g=m_norm_b_g, sinks_a=m_sinks_a, w_o=m_w_o, ln1_g=m_ln1_g,
              ln1_b=m_ln1_b, w_up=m_w_up, conv_w=m_conv_w, conv_b=m_conv_b, w_down=m_w_down, ln2_g=m_ln2_g, ln2_b=m_ln2_b)
    vs = dict(w_in=v_w_in, norm_a_g=v_norm_a_g, norm_b_g=v_norm_b_g, sinks_a=v_sinks_a, w_o=v_w_o, ln1_g=v_ln1_g,
              ln1_b=v_ln1_b, w_up=v_w_up, conv_w=v_conv_w, conv_b=v_conv_b, w_down=v_w_down, ln2_g=v_ln2_g, ln2_b=v_ln2_b)
    order = list(weights)
    grad = dict(g_small, w_in=g_w_in_rows.T, w_o=g_w_o, w_up=g_w_up, w_down=g_w_down)

    delta, new_m, new_v = {}, {}, {}
    delta["w_in"], new_m["w_in"], new_v["w_in"] = [
        a.T for a in _adamw(w_in_rows, g_w_in_rows, m_w_in_rows, v_w_in_rows, "adamw_w_in", 144)]
    for k, tr in (("w_o", 128), ("w_up", 256), ("w_down", 176)):
        delta[k], new_m[k], new_v[k] = _adamw(weights[k], grad[k], ms[k], vs[k], f"adamw_{k}", tr)
    small_names = [k for k in order if k not in delta]
    sizes = {k: weights[k].size for k in small_names}
    rows = 16
    packed = [_pack({k: src[k] for k in small_names}, rows) for src in (weights, grad, ms, vs)]
    for res, buf in zip((delta, new_m, new_v), _adamw(*packed, "adamw_small", rows)):
        for k, val in _unpack(buf, small_names, sizes).items():
            res[k] = val.reshape(weights[k].shape)

    return (loss, gx[None], *[grad[k] for k in order], *[delta[k] for k in order],
            *[new_m[k] for k in order], *[new_v[k] for k in order])
```

```python
import functools
import math

import jax
import jax.numpy as jnp
from jax import lax
from jax.experimental import pallas as pl
from jax.experimental.pallas import tpu as pltpu

F32, BF16, I32 = jnp.float32, jnp.bfloat16, jnp.int32

D = 1024
FF = 2816
HD = 64
NH = 8
WA, WB = 768, 1536
WIN = WA + WB
BLK = 128
ALPHA = 2.0 ** 0.25
LN_EPS, RMS_EPS = 1e-5, 1e-6
SCALE = 1.0 / math.sqrt(HD)
A_MAX_DIST, B_MAX_DIST = 127, 128
B_DILATIONS = (1, 4, 16)
SLOPES = tuple(2.0 ** (-(i + 1)) for i in range(NH))
SHARD_ROWS = (WIN // 4, D // 4, 2 * FF // 4, FF // 4)
N_CHIPS = 4
ADAM_LR, ADAM_B1, ADAM_B2, ADAM_EPS, ADAM_WD, ADAM_STEP = 0.001, 0.9, 0.999, 1e-08, 0.01, 10
MESH = pl.DeviceIdType.MESH
ANY = pl.BlockSpec(memory_space=pl.ANY)
SMEM = pl.BlockSpec(memory_space=pltpu.SMEM)
VMEM = pl.BlockSpec(memory_space=pltpu.VMEM)
HBM = pl.BlockSpec(memory_space=pltpu.HBM)
SEM = pl.BlockSpec(memory_space=pltpu.SEMAPHORE)
DATAFLOW = pltpu.SideEffectType.DATAFLOW_SIDE_EFFECTING


def _cp(sem, mb=48):
    return pltpu.CompilerParams(dimension_semantics=sem, vmem_limit_bytes=mb << 20)


def _nn(a, b):
    return lax.dot_general(a, b, (((1,), (0,)), ((), ())), preferred_element_type=F32)


def _nt(a, b):
    return lax.dot_general(a, b, (((1,), (1,)), ((), ())), preferred_element_type=F32)


def _tn(a, b):
    return lax.dot_general(a, b, (((0,), (0,)), ((), ())), preferred_element_type=F32)


def _resident(shape):
    n = len(shape)
    return pl.BlockSpec(shape, lambda *_: (0,) * n, pipeline_mode=pl.Buffered(1))


def _const(shape):
    n = len(shape)
    return pl.BlockSpec(shape, lambda *_: (0,) * n)


def _proj(x, w_t, name, tm=512):
    s = x.shape[0]
    n = w_t.shape[0]

    def body(x_ref, w_ref, o_ref):
        o_ref[...] = _nt(x_ref[...].astype(BF16), w_ref[...])

    return pl.pallas_call(
        body, name=name, grid=(s // tm,),
        in_specs=[pl.BlockSpec((tm, D), lambda i: (i, 0)), _resident((n, D))],
        out_specs=pl.BlockSpec((tm, n), lambda i: (i, 0)),
        out_shape=jax.ShapeDtypeStruct((s, n), F32),
        compiler_params=_cp(("parallel",)),
    )(x, w_t)


def _grad_w(lhs, rhs, name, tm, tk=512, lhs_halves=False):
    s = rhs.shape[0]
    if lhs_halves:
        per_half = lhs.shape[2] // tm
        n = 2 * lhs.shape[2]
        lhs_spec = pl.BlockSpec((None, tk, tm), lambda i, k: (i // per_half, k, i % per_half))
    else:
        n = lhs.shape[1]
        lhs_spec = pl.BlockSpec((tk, tm), lambda i, k: (k, i))
    nk = s // tk

    def body(l_ref, r_ref, o_ref, ob_ref):
        k = pl.program_id(1)

        @pl.when(k == 0)
        def _():
            o_ref[...] = jnp.zeros_like(o_ref)

        o_ref[...] += _tn(l_ref[...].astype(BF16), r_ref[...].astype(BF16))

        @pl.when(k == nk - 1)
        def _():
            ob_ref[...] = o_ref[...].astype(BF16)

    return pl.pallas_call(
        body, name=name, grid=(n // tm, nk),
        in_specs=[lhs_spec, pl.BlockSpec((tk, D), lambda i, k: (k, 0))],
        out_specs=[pl.BlockSpec((tm, D), lambda i, k: (i, 0))] * 2,
        out_shape=[jax.ShapeDtypeStruct((n, D), F32), jax.ShapeDtypeStruct((n, D), BF16)],
        compiler_params=_cp(("parallel", "arbitrary")),
    )(lhs, rhs)


def _band_base(max_dist, dist_unit, first):
    row = lax.broadcasted_iota(I32, (BLK, 2 * BLK), 0)
    col = lax.broadcasted_iota(I32, (BLK, 2 * BLK), 1)
    dist = BLK + row - col
    ok = (dist >= 0) & (dist <= max_dist)
    if first:
        ok = ok & (col >= BLK)
    return jnp.where(ok, dist.astype(F32) * (-float(dist_unit)), -jnp.inf)


def _half_mask(shape, e):
    lane = lax.broadcasted_iota(I32, shape, 1)
    return (lane < HD) if e == 0 else (lane >= HD)


def _to_half(x, e, g):
    if g != e:
        x = pltpu.roll(x, HD, 1)
    return jnp.where(_half_mask(x.shape, g), x, 0.0)


def _pair_fwd(q2, kb, vb, base, slopes, kv_heads, sinks):
    lo = _half_mask((BLK, 2 * HD), 0)
    o2 = lse2 = None
    for e in (0, 1):
        g = kv_heads[e]
        qv = (_to_half(q2, e, g) * SCALE).astype(BF16)
        s = _nt(qv, kb) + slopes[e] * base
        m = jnp.max(s, axis=1, keepdims=True)
        if sinks is not None:
            m = jnp.maximum(m, sinks[e])
        p = jnp.exp(s - m)
        l = jnp.sum(p, axis=1, keepdims=True)
        if sinks is not None:
            l = l + jnp.exp(sinks[e] - m)
        oh = _nn(p.astype(BF16), vb) / l
        if g != e:
            oh = pltpu.roll(oh, HD, 1)
        lse = jnp.broadcast_to(m + jnp.log(l), (BLK, 2 * HD))
        o2 = oh if e == 0 else jnp.where(lo, o2, oh)
        lse2 = lse if e == 0 else jnp.where(lo, lse2, lse)
    return o2, lse2


def _pair_bwd(q2, kb, vb, do2, o2, lse2, base, slopes, kv_heads, sinks):
    lo = _half_mask((BLK, 2 * HD), 0)
    dq2 = dk2 = dv2 = None
    prod = do2 * o2
    dsinks = []
    for e in (0, 1):
        g = kv_heads[e]
        hq = _half_mask((BLK, 2 * HD), e)
        lse = jnp.max(jnp.where(hq, lse2, -jnp.inf), axis=1, keepdims=True)
        delta = jnp.sum(jnp.where(hq, prod, 0.0), axis=1, keepdims=True)
        qv = (_to_half(q2, e, g) * SCALE).astype(BF16)
        dov = _to_half(do2, e, g).astype(BF16)
        p = jnp.exp(_nt(qv, kb) + slopes[e] * base - lse)
        ds = (p * (_nt(dov, vb) - delta)).astype(BF16)
        dqh = _nn(ds, kb) * SCALE
        if g != e:
            dqh = pltpu.roll(dqh, HD, 1)
        dq2 = dqh if e == 0 else jnp.where(lo, dq2, dqh)
        dkh = _tn(ds, qv)
        dvh = _tn(p.astype(BF16), dov)
        dk2 = dkh if e == 0 else dk2 + dkh
        dv2 = dvh if e == 0 else dv2 + dvh
        if sinks is not None:
            dsinks.append(jnp.sum(-jnp.exp(sinks[e] - lse) * delta, axis=0, keepdims=True))
    return dq2, dk2, dv2, dsinks


def _attn_a_fwd(proj, sinks):
    s = proj.shape[0]
    nb = s // BLK

    def body(sink_ref, q_ref, kp_ref, kc_ref, vp_ref, vc_ref, o_ref, lse_ref):
        n = pl.program_id(0)
        base = jnp.where(n > 0, _band_base(A_MAX_DIST, 1, False), _band_base(A_MAX_DIST, 1, True))
        kb = jnp.concatenate([kp_ref[...], kc_ref[...]], axis=0).astype(BF16)
        vb = jnp.concatenate([vp_ref[...], vc_ref[...]], axis=0).astype(BF16)
        for j in range(NH // 2):
            g = j // 2
            o2, lse2 = _pair_fwd(q_ref[:, 128 * j:128 * (j + 1)], kb, vb, base, (SLOPES[2 * j], SLOPES[2 * j + 1]),
                                 (g, g), (sink_ref[2 * j], sink_ref[2 * j + 1]))
            o_ref[:, 128 * j:128 * (j + 1)] = o2
            lse_ref[:, 128 * j:128 * (j + 1)] = lse2

    prev = lambda n: jnp.maximum(n - 1, 0)
    return pl.pallas_call(
        body, name="attn_a_fwd", grid=(nb,),
        in_specs=[SMEM,
                  pl.BlockSpec((BLK, 512), lambda n: (n, 0)),
                  pl.BlockSpec((BLK, 128), lambda n: (prev(n), 4)), pl.BlockSpec((BLK, 128), lambda n: (n, 4)),
                  pl.BlockSpec((BLK, 128), lambda n: (prev(n), 5)), pl.BlockSpec((BLK, 128), lambda n: (n, 5))],
        out_specs=[pl.BlockSpec((BLK, 512), lambda n: (n, 0))] * 2,
        out_shape=[jax.ShapeDtypeStruct((s, 512), F32)] * 2,
        compiler_params=_cp(("parallel",)),
    )(sinks, proj, proj, proj, proj, proj)


def _attn_a_bwd(proj, sinks, d_o, o, lse):
    s = proj.shape[0]
    nb = s // BLK

    def body(sink_ref, q_ref, kp_ref, kc_ref, vp_ref, vc_ref, do_ref, o_ref, lse_ref,
             dq_ref, dk_ref, dv_ref, dsink_ref, kcar, vcar):
        n = pl.program_id(0)

        @pl.when(n == 0)
        def _():
            kcar[...] = jnp.zeros_like(kcar)
            vcar[...] = jnp.zeros_like(vcar)
            dsink_ref[...] = jnp.zeros_like(dsink_ref)

        @pl.when(n < nb)
        def _():
            base = jnp.where(n > 0, _band_base(A_MAX_DIST, 1, False), _band_base(A_MAX_DIST, 1, True))
            kb = jnp.concatenate([kp_ref[...], kc_ref[...]], axis=0).astype(BF16)
            vb = jnp.concatenate([vp_ref[...], vc_ref[...]], axis=0).astype(BF16)
            dk_win = dv_win = None
            for j in range(NH // 2):
                g = j // 2
                sl = slice(128 * j, 128 * (j + 1))
                dq2, dk2, dv2, dsk = _pair_bwd(q_ref[:, sl], kb, vb, do_ref[:, sl], o_ref[:, sl], lse_ref[:, sl], base,
                                               (SLOPES[2 * j], SLOPES[2 * j + 1]), (g, g),
                                               (sink_ref[2 * j], sink_ref[2 * j + 1]))
                dq_ref[:, sl] = dq2
                dk_win = dk2 if j == 0 else dk_win + dk2
                dv_win = dv2 if j == 0 else dv_win + dv2
                for e in (0, 1):
                    h = 2 * j + e
                    dsink_ref[h:h + 1, :] += jnp.broadcast_to(dsk[e], (1, 128))
            dk_ref[...] = kcar[...] + dk_win[:BLK]
            dv_ref[...] = vcar[...] + dv_win[:BLK]
            kcar[...] = dk_win[BLK:]
            vcar[...] = dv_win[BLK:]

        @pl.when(n == nb)
        def _():
            dk_ref[...] = kcar[...]
            dv_ref[...] = vcar[...]

    cur = lambda n: jnp.minimum(n, nb - 1)
    prev = lambda n: jnp.maximum(cur(n) - 1, 0)
    out_prev = lambda n: jnp.maximum(n - 1, 0)
    return pl.pallas_call(
        body, name="attn_a_bwd", grid=(nb + 1,),
        in_specs=[SMEM,
                  pl.BlockSpec((BLK, 512), lambda n: (cur(n), 0)),
                  pl.BlockSpec((BLK, 128), lambda n: (prev(n), 4)), pl.BlockSpec((BLK, 128), lambda n: (cur(n), 4)),
                  pl.BlockSpec((BLK, 128), lambda n: (prev(n), 5)), pl.BlockSpec((BLK, 128), lambda n: (cur(n), 5)),
                  pl.BlockSpec((BLK, 512), lambda n: (cur(n), 0)),
                  pl.BlockSpec((BLK, 512), lambda n: (cur(n), 0)),
                  pl.BlockSpec((BLK, 512), lambda n: (cur(n), 0))],
        out_specs=[pl.BlockSpec((BLK, 512), lambda n: (cur(n), 0)),
                   pl.BlockSpec((BLK, 128), lambda n: (out_prev(n), 0)),
                   pl.BlockSpec((BLK, 128), lambda n: (out_prev(n), 0)),
                   pl.BlockSpec((NH, 128), lambda n: (0, 0))],
        out_shape=[jax.ShapeDtypeStruct((s, 512), F32), jax.ShapeDtypeStruct((s, 128), F32),
                   jax.ShapeDtypeStruct((s, 128), F32), jax.ShapeDtypeStruct((NH, 128), F32)],
        scratch_shapes=[pltpu.VMEM((BLK, 128), F32), pltpu.VMEM((BLK, 128), F32)],
        compiler_params=_cp(("arbitrary",)),
    )(sinks, proj, proj, proj, proj, proj, d_o, o, lse)


def _stream(rho, i, r):
    start = i * BLK * r + rho
    return pl.ds(start, BLK, stride=r) if r > 1 else pl.ds(start, BLK)


def _for_streams(r, fn):
    if r <= 4:
        for rho in range(r):
            fn(rho)
    else:
        def four(it, carry):
            for u in range(4):
                fn(4 * it + u)
            return carry

        lax.fori_loop(0, r // 4, four, 0)


B_BLOCKS_PER_STEP = {1: 4, 4: 1, 16: 1}


def _attn_b_fwd(proj, slopes, r):
    s = proj.shape[0]
    nq = B_BLOCKS_PER_STEP[r]
    rows = BLK * r * nq
    steps = s // rows
    qc, kc, vc = WA // 128, WA // 128 + 4, WA // 128 + 8

    def body(slope_ref, q_ref, kp_ref, kc_ref, vp_ref, vc_ref, o_ref, lse_ref):
        j = pl.program_id(0)
        sb = pl.program_id(1)
        base_rest = _band_base(B_MAX_DIST, r, False)
        base_0 = jnp.where(sb > 0, base_rest, _band_base(B_MAX_DIST, r, True))
        sl2 = (slope_ref[2 * j], slope_ref[2 * j + 1])

        def stream(rho):
            for i in range(nq):
                cur = _stream(rho, i, r)
                k_prev = kc_ref[_stream(rho, i - 1, r), :] if i > 0 else kp_ref[_stream(rho, 0, r), :]
                v_prev = vc_ref[_stream(rho, i - 1, r), :] if i > 0 else vp_ref[_stream(rho, 0, r), :]
                kb = jnp.concatenate([k_prev, kc_ref[cur, :]], axis=0).astype(BF16)
                vb = jnp.concatenate([v_prev, vc_ref[cur, :]], axis=0).astype(BF16)
                o2, lse2 = _pair_fwd(q_ref[cur, :], kb, vb, base_rest if i > 0 else base_0, sl2, (0, 1), None)
                o_ref[cur, :] = o2
                lse_ref[cur, :] = lse2

        _for_streams(r, stream)

    before = lambda sb: jnp.maximum(sb * nq - 1, 0)
    return pl.pallas_call(
        body, name=f"attn_b_fwd_r{r}", grid=(NH // 2, steps),
        in_specs=[SMEM,
                  pl.BlockSpec((rows, 128), lambda j, sb: (sb, qc + j)),
                  pl.BlockSpec((BLK * r, 128), lambda j, sb: (before(sb), kc + j)),
                  pl.BlockSpec((rows, 128), lambda j, sb: (sb, kc + j)),
                  pl.BlockSpec((BLK * r, 128), lambda j, sb: (before(sb), vc + j)),
                  pl.BlockSpec((rows, 128), lambda j, sb: (sb, vc + j))],
        out_specs=[pl.BlockSpec((rows, 128), lambda j, sb: (sb, j))] * 2,
        out_shape=[jax.ShapeDtypeStruct((s, 512), F32)] * 2,
        compiler_params=_cp(("parallel", "parallel")),
    )(slopes, proj, proj, proj, proj, proj)


def _attn_b_bwd(proj, slopes, d_o, o, lse, r):
    s = proj.shape[0]
    nq = B_BLOCKS_PER_STEP[r]
    rows = BLK * r * nq
    steps = s // rows
    qc, kc, vc = WA // 128, WA // 128 + 4, WA // 128 + 8

    def body(slope_ref, q_ref, kp_ref, kc_ref, vp_ref, vc_ref, do_ref, o_ref, lse_ref,
             dq_ref, dk_ref, dv_ref, kcar, vcar):
        j = pl.program_id(0)
        sb = pl.program_id(1)

        @pl.when(sb == 0)
        def _():
            kcar[...] = jnp.zeros_like(kcar)
            vcar[...] = jnp.zeros_like(vcar)

        dk_ref[...] = kcar[...]
        dv_ref[...] = vcar[...]

        @pl.when(sb < steps)
        def _():
            base_rest = _band_base(B_MAX_DIST, r, False)
            base_0 = jnp.where(sb > 0, base_rest, _band_base(B_MAX_DIST, r, True))
            sl2 = (slope_ref[2 * j], slope_ref[2 * j + 1])

            def stream(rho):
                for i in range(nq):
                    cur = _stream(rho, i, r)
                    k_prev = kc_ref[_stream(rho, i - 1, r), :] if i > 0 else kp_ref[_stream(rho, 0, r), :]
                    v_prev = vc_ref[_stream(rho, i - 1, r), :] if i > 0 else vp_ref[_stream(rho, 0, r), :]
                    kb = jnp.concatenate([k_prev, kc_ref[cur, :]], axis=0).astype(BF16)
                    vb = jnp.concatenate([v_prev, vc_ref[cur, :]], axis=0).astype(BF16)
                    dq2, dk2, dv2, _ = _pair_bwd(q_ref[cur, :], kb, vb, do_ref[cur, :], o_ref[cur, :], lse_ref[cur, :],
                                                 base_rest if i > 0 else base_0, sl2, (0, 1), None)
                    dq_ref[cur, :] = dq2
                    if i == 0:
                        last = _stream(rho, nq - 1, r)
                        dk_ref[last, :] += dk2[:BLK]
                        dv_ref[last, :] += dv2[:BLK]
                    else:
                        kcar[_stream(rho, i - 1, r), :] += dk2[:BLK]
                        vcar[_stream(rho, i - 1, r), :] += dv2[:BLK]
                    kcar[cur, :] = dk2[BLK:]
                    vcar[cur, :] = dv2[BLK:]

            _for_streams(r, stream)

    cur_step = lambda sb: jnp.minimum(sb, steps - 1)
    before = lambda sb: jnp.maximum(cur_step(sb) * nq - 1, 0)
    out_prev = lambda sb: jnp.maximum(sb - 1, 0)
    tile = lambda col: pl.BlockSpec((rows, 128), lambda j, sb: (cur_step(sb), col + j))
    edge = lambda col: pl.BlockSpec((BLK * r, 128), lambda j, sb: (before(sb), col + j))
    return pl.pallas_call(
        body, name=f"attn_b_bwd_r{r}", grid=(NH // 2, steps + 1),
        in_specs=[SMEM, tile(qc), edge(kc), tile(kc), edge(vc), tile(vc), tile(0), tile(0), tile(0)],
        out_specs=[tile(0),
                   pl.BlockSpec((rows, 128), lambda j, sb: (out_prev(sb), j)),
                   pl.BlockSpec((rows, 128), lambda j, sb: (out_prev(sb), j))],
        out_shape=[jax.ShapeDtypeStruct((s, 512), F32)] * 3,
        scratch_shapes=[pltpu.VMEM((rows, 128), F32), pltpu.VMEM((rows, 128), F32)],
        compiler_params=_cp(("parallel", "arbitrary")),
    )(slopes, proj, proj, proj, proj, proj, d_o, o, lse)


def _row(v):
    return v.reshape(1, -1)


def _layer_norm_stats(z):
    mu = jnp.mean(z, axis=-1, keepdims=True)
    zc = z - mu
    var = jnp.mean(zc * zc, axis=-1, keepdims=True)
    rstd = lax.rsqrt(var + LN_EPS)
    return zc * rstd, rstd


def _layer_norm_bwd(dh, zh, rstd, g):
    dzh = dh * g
    return rstd * (dzh - jnp.mean(dzh, axis=-1, keepdims=True) - zh * jnp.mean(dzh * zh, axis=-1, keepdims=True))


def _rms(o):
    return lax.rsqrt(jnp.mean(o * o, axis=-1, keepdims=True) + RMS_EPS)


def _mix_ln1(x, o_a, o_b, lse_b, norm_a_g, norm_b_g, w_o, ln1_g, ln1_b, tm=256):
    s = x.shape[0]

    def body(x_ref, oa_ref, ob1, ob2, ob3, l1, l2, l3, ga_ref, gb_ref, wo_ref, g_ref, b_ref,
             obm_ref, lse_ref, cat_ref, z1_ref, h1_ref, h1b_ref):
        la, lb, lc = l1[...], l2[...], l3[...]
        m = jnp.maximum(jnp.maximum(la, lb), lc)
        ea, eb, ec = jnp.exp(la - m), jnp.exp(lb - m), jnp.exp(lc - m)
        den = ea + eb + ec
        obm = (ea / den) * ob1[...] + (eb / den) * ob2[...] + (ec / den) * ob3[...]
        obm_ref[...] = obm
        lse_ref[...] = m + jnp.log(den)
        oa = oa_ref[...]
        na = oa * _rms(oa) * ga_ref[...]
        nb_ = obm * _rms(obm) * gb_ref[...]
        cat = jnp.concatenate([na, nb_], axis=1).astype(BF16)
        cat_ref[...] = cat
        z1 = ALPHA * x_ref[...] + _nn(cat, wo_ref[...])
        z1_ref[...] = z1
        zh, _ = _layer_norm_stats(z1)
        h1 = zh * g_ref[...] + b_ref[...]
        h1_ref[...] = h1
        h1b_ref[...] = h1.astype(BF16)

    t512 = pl.BlockSpec((tm, 512), lambda i: (i, 0))
    td = pl.BlockSpec((tm, D), lambda i: (i, 0))
    return pl.pallas_call(
        body, name="mix_ln1", grid=(s // tm,),
        in_specs=[td] + [t512] * 7 + [_const((1, 512))] * 2 + [_resident((D, D))] + [_const((1, D))] * 2,
        out_specs=[t512, t512, td, td, td, td],
        out_shape=[jax.ShapeDtypeStruct((s, 512), F32), jax.ShapeDtypeStruct((s, 512), F32),
                   jax.ShapeDtypeStruct((s, D), BF16), jax.ShapeDtypeStruct((s, D), F32),
                   jax.ShapeDtypeStruct((s, D), F32), jax.ShapeDtypeStruct((s, D), BF16)],
        compiler_params=_cp(("parallel",)),
    )(x, o_a, *o_b, *lse_b, _row(norm_a_g), _row(norm_b_g), w_o, _row(ln1_g), _row(ln1_b))


def _gelu_and_grad(x):
    c = math.sqrt(2.0 / math.pi)
    x2 = x * x
    cx = c * x
    t = jnp.tanh(cx * (1.0 + 0.044715 * x2))
    q = 1.0 + t
    g = (0.5 * x) * q
    dg = 0.5 * q + ((0.5 * cx) * (1.0 - t * t)) * (1.0 + (3.0 * 0.044715) * x2)
    return g, dg


CONV_CHUNK = 64


def _shift_down(u, before):
    n = u.shape[0]
    ext = jnp.concatenate([before, u], axis=0)
    return pltpu.roll(ext, 1, 0)[8:], pltpu.roll(ext, 2, 0)[8:]


def _shift_up(u, after):
    n = u.shape[0]
    ext = jnp.concatenate([u, after], axis=0)
    return pltpu.roll(ext, n + 7, 0)[:n], pltpu.roll(ext, n + 6, 0)[:n]


def _up_conv_gelu(h1b, w_up, cwb, tm=512, tn=256):
    s = h1b.shape[0]
    n_i = s // tm
    n_t = (FF // tn) * n_i

    def body(h_ref, wg_ref, wv_ref, c_ref, up_ref, a_ref, g_ref, a1_ref, pend_a, pend_b, carry):
        t = pl.program_id(0)
        row_tile = jnp.maximum(t - 1, 0) % n_i
        w_refs = (wg_ref, wv_ref)

        @pl.when(t == 0)
        def _():
            pend_b[...] = jnp.zeros_like(pend_b)
            carry[...] = jnp.zeros_like(carry)

        def step(dst, src):
            def chunk(c, before):
                rows = pl.ds(c * CONV_CHUNK, CONV_CHUNK)
                u, last = [], []
                for half in (0, 1):
                    up = src[half, rows, :]
                    r1, r2 = _shift_down(up, before[half])
                    u.append(r2 * c_ref[0, half:half + 1, :] + r1 * c_ref[1, half:half + 1, :]
                             + up * c_ref[2, half:half + 1, :] + c_ref[3, half:half + 1, :])
                    last.append(up[CONV_CHUNK - 8:])
                g, dg = _gelu_and_grad(u[0])
                a_ref[rows, :] = (g * u[1]).astype(BF16)
                g_ref[rows, :] = g.astype(BF16)
                a1_ref[rows, :] = (u[1] * dg).astype(BF16)
                return tuple(last)

            edge = tuple(jnp.where(row_tile > 0, carry[half], 0.0) for half in (0, 1))
            n_c = tm // CONV_CHUNK
            n_k = n_c // 2
            tk = D // n_k
            for half in (0, 1):
                up = None
                for kq in range(n_k):
                    ks = slice(kq * tk, (kq + 1) * tk)
                    part = _nn(h_ref[:, ks], w_refs[half][ks, :])
                    up = part if kq == 0 else up + part
                    edge = chunk(half * n_k + kq, edge)
                up_ref[half] = up
                dst[half] = up
            for half in (0, 1):
                carry[half] = edge[half]

        @pl.when(t % 2 == 0)
        def _():
            step(pend_a, pend_b)

        @pl.when(t % 2 == 1)
        def _():
            step(pend_b, pend_a)

    mm = lambda t: jnp.minimum(t, n_t - 1)
    ew = lambda t: jnp.maximum(t - 1, 0)
    out_tile = pl.BlockSpec((tm, tn), lambda t: (ew(t) % n_i, ew(t) // n_i))
    return pl.pallas_call(
        body, name="up_conv_gelu", grid=(n_t + 1,),
        in_specs=[pl.BlockSpec((tm, D), lambda t: (mm(t) % n_i, 0)),
                  pl.BlockSpec((D, tn), lambda t: (0, mm(t) // n_i)),
                  pl.BlockSpec((D, tn), lambda t: (0, FF // tn + mm(t) // n_i)),
                  pl.BlockSpec((4, 2, tn), lambda t: (0, 0, ew(t) // n_i))],
        out_specs=[pl.BlockSpec((2, tm, tn), lambda t: (0, mm(t) % n_i, mm(t) // n_i)), out_tile, out_tile, out_tile],
        out_shape=[jax.ShapeDtypeStruct((2, s, FF), F32)] + [jax.ShapeDtypeStruct((s, FF), BF16)] * 3,
        scratch_shapes=[pltpu.VMEM((2, tm, tn), F32), pltpu.VMEM((2, tm, tn), F32), pltpu.VMEM((2, 8, tn), F32)],
        compiler_params=_cp(("arbitrary",)),
    )(h1b, w_up, w_up, cwb)


def _down_ln2_loss(a, w_down, h1, target, ln2_g, ln2_b, tm=256):
    s = a.shape[0]

    def body(a_ref, w_ref, h_ref, t_ref, g_ref, b_ref, dz_ref, dzb_ref, st_ref):
        @pl.when(pl.program_id(0) == 0)
        def _():
            st_ref[...] = jnp.zeros_like(st_ref)

        z2 = ALPHA * h_ref[...] + _nn(a_ref[...], w_ref[...])
        zh, rstd = _layer_norm_stats(z2)
        diff = zh * g_ref[...] + b_ref[...] - t_ref[...]
        part = 0.5 * jnp.sum(jnp.mean(diff * diff, axis=-1, keepdims=True), axis=0, keepdims=True)
        dy = diff * (1.0 / D)
        st_ref[0:1, :] += jnp.sum(dy * zh, axis=0, keepdims=True)
        st_ref[1:2, :] += jnp.sum(dy, axis=0, keepdims=True)
        st_ref[2:3, :] += jnp.broadcast_to(part, (1, D))
        dz = _layer_norm_bwd(dy, zh, rstd, g_ref[...])
        dz_ref[...] = dz
        dzb_ref[...] = dz.astype(BF16)

    td = pl.BlockSpec((tm, D), lambda i: (i, 0))
    return pl.pallas_call(
        body, name="down_ln2_loss", grid=(s // tm,),
        in_specs=[pl.BlockSpec((tm, FF), lambda i: (i, 0)), _resident((FF, D)), td, td, _const((1, D)), _const((1, D))],
        out_specs=[td, td, _const((8, D))],
        out_shape=[jax.ShapeDtypeStruct((s, D), F32), jax.ShapeDtypeStruct((s, D), BF16),
                   jax.ShapeDtypeStruct((8, D), F32)],
        compiler_params=_cp(("arbitrary",)),
    )(a, w_down, h1, target, _row(ln2_g), _row(ln2_b))


def _conv_gelu_bwd(dz2b, w_down_t, up, g, a1, cwb, tm=512, tn=256):
    s = dz2b.shape[0]
    n_i = s // tm
    n_t = (FF // tn) * n_i

    def body(dz_ref, w_ref, up_ref, g_ref, a1_ref, c_ref, dup_ref, dc_ref, pend_a, pend_b, carry):
        t = pl.program_id(0)
        first = jnp.maximum(t - 1, 0) % n_i == 0

        @pl.when(t == 0)
        def _():
            pend_b[...] = jnp.zeros_like(pend_b)

        @pl.when(first)
        def _():
            carry[...] = jnp.zeros_like(carry)
            dc_ref[...] = jnp.zeros_like(dc_ref)

        def step(dst, src):
            n_c = tm // CONV_CHUNK

            def chunk(cc, after):
                rows = pl.ds((n_c - 1 - cc) * CONV_CHUNK, CONV_CHUNK)
                da = src[rows, :]
                dus = (da * a1_ref[rows, :].astype(F32), da * g_ref[rows, :].astype(F32))
                head = []
                for half in (0, 1):
                    du = dus[half]
                    up = up_ref[half, rows, :]
                    l1, l2 = _shift_up(du, after[half])
                    dup = (du * c_ref[2, half:half + 1, :] + l1 * c_ref[1, half:half + 1, :]
                           + l2 * c_ref[0, half:half + 1, :])
                    dup_ref[half, rows, :] = dup.astype(BF16)
                    dc_ref[0, half:half + 1, :] += jnp.sum(l2 * up, axis=0, keepdims=True)
                    dc_ref[1, half:half + 1, :] += jnp.sum(l1 * up, axis=0, keepdims=True)
                    dc_ref[2, half:half + 1, :] += jnp.sum(du * up, axis=0, keepdims=True)
                    dc_ref[3, half:half + 1, :] += jnp.sum(du, axis=0, keepdims=True)
                    head.append(du[:8])
                return tuple(head)

            head = (carry[0], carry[1])
            n_k = n_c // 2
            tk = D // n_k
            da = None
            for kq in range(n_k):
                ks = slice(kq * tk, (kq + 1) * tk)
                part = _nn(dz_ref[:, ks], w_ref[ks, :])
                da = part if kq == 0 else da + part
                head = chunk(2 * kq, head)
                head = chunk(2 * kq + 1, head)
            for half in (0, 1):
                carry[half] = head[half]
            dst[...] = da

        @pl.when(t % 2 == 0)
        def _():
            step(pend_a, pend_b)

        @pl.when(t % 2 == 1)
        def _():
            step(pend_b, pend_a)

    mm = lambda t: jnp.minimum(t, n_t - 1)
    ew = lambda t: jnp.maximum(t - 1, 0)
    row = lambda t: n_i - 1 - t % n_i
    ew_tile = pl.BlockSpec((tm, tn), lambda t: (row(ew(t)), ew(t) // n_i))
    ew_pair = pl.BlockSpec((2, tm, tn), lambda t: (0, row(ew(t)), ew(t) // n_i))
    per_col = pl.BlockSpec((4, 2, tn), lambda t: (0, 0, ew(t) // n_i))
    return pl.pallas_call(
        body, name="conv_gelu_bwd", grid=(n_t + 1,),
        in_specs=[pl.BlockSpec((tm, D), lambda t: (row(mm(t)), 0)),
                  pl.BlockSpec((D, tn), lambda t: (0, mm(t) // n_i)),
                  ew_pair, ew_tile, ew_tile, per_col],
        out_specs=[ew_pair, per_col],
        out_shape=[jax.ShapeDtypeStruct((2, s, FF), BF16), jax.ShapeDtypeStruct((4, 2, FF), F32)],
        scratch_shapes=[pltpu.VMEM((tm, tn), F32), pltpu.VMEM((tm, tn), F32), pltpu.VMEM((2, 8, tn), F32)],
        compiler_params=_cp(("arbitrary",)),
    )(dz2b, w_down_t, up, g, a1, cwb)


def _dh1_ln1_bwd(dz2, dup, w_up_t, z1, ln1_g, tm=256):
    s = dz2.shape[0]

    def body(dz2_ref, dup_ref, w_ref, z1_ref, g_ref, dz1_ref, dz1b_ref, st_ref):
        @pl.when(pl.program_id(0) == 0)
        def _():
            st_ref[...] = jnp.zeros_like(st_ref)

        dh = ALPHA * dz2_ref[...] + _nn(dup_ref[0], w_ref[0]) + _nn(dup_ref[1], w_ref[1])
        zh, rstd = _layer_norm_stats(z1_ref[...])
        st_ref[0:1, :] += jnp.sum(dh * zh, axis=0, keepdims=True)
        st_ref[1:2, :] += jnp.sum(dh, axis=0, keepdims=True)
        dz = _layer_norm_bwd(dh, zh, rstd, g_ref[...])
        dz1_ref[...] = dz
        dz1b_ref[...] = dz.astype(BF16)

    td = pl.BlockSpec((tm, D), lambda i: (i, 0))
    return pl.pallas_call(
        body, name="dh1_ln1_bwd", grid=(s // tm,),
        in_specs=[td, pl.BlockSpec((2, tm, FF), lambda i: (0, i, 0)), _resident((2, FF, D)), td, _const((1, D))],
        out_specs=[td, td, _const((8, D))],
        out_shape=[jax.ShapeDtypeStruct((s, D), F32), jax.ShapeDtypeStruct((s, D), BF16),
                   jax.ShapeDtypeStruct((8, D), F32)],
        compiler_params=_cp(("arbitrary",)),
    )(dz2, dup, w_up_t, z1, _row(ln1_g))


def _dcat_rms_bwd(dz1b, w_o, o_a, o_b, norm_a_g, norm_b_g, tm=256):
    s = dz1b.shape[0]

    def body(dz_ref, w_ref, oa_ref, ob_ref, ga_ref, gb_ref, da_ref, db_ref, st_ref):
        @pl.when(pl.program_id(0) == 0)
        def _():
            st_ref[...] = jnp.zeros_like(st_ref)

        dcat = _nt(dz_ref[...], w_ref[...])
        for k, (o_ref, g_ref, d_ref) in enumerate(((oa_ref, ga_ref, da_ref), (ob_ref, gb_ref, db_ref))):
            o = o_ref[...]
            dn = dcat[:, 512 * k:512 * (k + 1)]
            rr = _rms(o)
            oh = o * rr
            st_ref[k:k + 1, :] += jnp.sum(dn * oh, axis=0, keepdims=True)
            doh = dn * g_ref[...]
            d_ref[...] = rr * (doh - oh * jnp.mean(doh * oh, axis=-1, keepdims=True))

    t512 = pl.BlockSpec((tm, 512), lambda i: (i, 0))
    return pl.pallas_call(
        body, name="dcat_rms_bwd", grid=(s // tm,),
        in_specs=[pl.BlockSpec((tm, D), lambda i: (i, 0)), _resident((D, D)), t512, t512,
                  _const((1, 512)), _const((1, 512))],
        out_specs=[t512, t512, _const((8, 512))],
        out_shape=[jax.ShapeDtypeStruct((s, 512), F32), jax.ShapeDtypeStruct((s, 512), F32),
                   jax.ShapeDtypeStruct((8, 512), F32)],
        compiler_params=_cp(("arbitrary",)),
    )(dz1b, w_o, o_a, o_b, _row(norm_a_g), _row(norm_b_g))


def _dproj_combine(dqa, dka, dva, dqkv_b, tm=256):
    s = dqa.shape[0]

    def body(qa, ka, va, q1, k1, v1, q2, k2, v2, q3, k3, v3, o_ref):
        o_ref[:, 0:512] = qa[...].astype(BF16)
        o_ref[:, 512:640] = ka[...].astype(BF16)
        o_ref[:, 640:768] = va[...].astype(BF16)
        o_ref[:, 768:1280] = (q1[...] + q2[...] + q3[...]).astype(BF16)
        o_ref[:, 1280:1792] = (k1[...] + k2[...] + k3[...]).astype(BF16)
        o_ref[:, 1792:2304] = (v1[...] + v2[...] + v3[...]).astype(BF16)

    t512 = pl.BlockSpec((tm, 512), lambda i: (i, 0))
    t128 = pl.BlockSpec((tm, 128), lambda i: (i, 0))
    flat = [a for trio in dqkv_b for a in trio]
    return pl.pallas_call(
        body, name="dproj_combine", grid=(s // tm,),
        in_specs=[t512, t128, t128] + [t512] * 9,
        out_specs=pl.BlockSpec((tm, WIN), lambda i: (i, 0)),
        out_shape=jax.ShapeDtypeStruct((s, WIN), BF16),
        compiler_params=_cp(("parallel",)),
    )(dqa, dka, dva, *flat)


def _grad_x(dz1, dproj, w_in_t, zero, tm=256):
    s = dz1.shape[0]

    def body(dz_ref, dp_ref, w_ref, z_ref, o_ref):
        o_ref[...] = ALPHA * dz_ref[...] + _nn(dp_ref[...], w_ref[...]) + z_ref[0:1, 0:1]

    td = pl.BlockSpec((tm, D), lambda i: (i, 0))
    return pl.pallas_call(
        body, name="grad_x", grid=(s // tm,),
        in_specs=[td, pl.BlockSpec((tm, WIN), lambda i: (i, 0)), _resident((WIN, D)), _const((8, 128))],
        out_specs=td, out_shape=jax.ShapeDtypeStruct((s, D), F32),
        compiler_params=_cp(("parallel",)),
    )(dz1, dproj, w_in_t, zero)


def _place():
    return lax.axis_index("x"), lax.axis_index("y"), lax.axis_index("c")


def _other_chips(x, y):
    return [(1 - x, y), (x, 1 - y), (1 - x, 1 - y)]


def _hbm(a):
    return pltpu.with_memory_space_constraint(a, pltpu.HBM)


def _gather_w_in(shard, conv_w):
    rows_k = shard.shape[0]
    half = rows_k // 2

    def body(src, conv_src, out, conv_out, send_sems, recv_sems):
        x, y, c = _place()
        b = 2 * x + y
        sibling = (x, y, 1 - c)
        chips = _other_chips(x, y)

        def copy(idx, chip_b, core, to, first_hop=False):
            rows = out.at[pl.ds(pl.multiple_of(chip_b * rows_k + core * half, 16), half)]
            s_ref = src.at[pl.ds(pl.multiple_of(core * half, 16), half)] if first_hop else rows
            return pltpu.make_async_remote_copy(src_ref=s_ref, dst_ref=rows, send_sem=send_sems.at[idx],
                                                recv_sem=recv_sems.at[idx], device_id=to, device_id_type=MESH)

        def own_copy():
            return pltpu.make_async_remote_copy(
                src_ref=src, dst_ref=out.at[pl.ds(pl.multiple_of(b * rows_k, 16), rows_k)], send_sem=send_sems.at[6],
                recv_sem=recv_sems.at[6], device_id=sibling, device_id_type=MESH)

        def conv_copy(idx, chip_b, to):
            return pltpu.make_async_remote_copy(src_ref=conv_src, dst_ref=conv_out.at[chip_b],
                                                send_sem=send_sems.at[7 + idx], recv_sem=recv_sems.at[7 + idx],
                                                device_id=to, device_id_type=MESH)

        started = [own_copy(), conv_copy(3, b, sibling)]
        for jn, chip in enumerate(chips):
            started += [copy(jn, b, c, (chip[0], chip[1], c), first_hop=True), conv_copy(jn, b, (chip[0], chip[1], c))]
        for cp in started:
            cp.start()
        for jn, chip in enumerate(chips):
            cb = 2 * chip[0] + chip[1]
            copy(jn, cb, c, (chip[0], chip[1], c)).wait_recv()
            cp = copy(3 + jn, cb, c, sibling)
            cp.start()
            started.append(cp)
        for jn, chip in enumerate(chips):
            cb = 2 * chip[0] + chip[1]
            copy(3 + jn, cb, 1 - c, sibling).wait_recv()
            conv_copy(jn, cb, (chip[0], chip[1], c)).wait_recv()
        own_copy().wait_recv()
        conv_copy(3, b, sibling).wait_recv()
        for cp in started:
            cp.wait_send()

    return pl.pallas_call(
        body, name="gather_w_in",
        in_specs=[ANY, ANY], out_specs=[ANY, ANY],
        out_shape=[jax.ShapeDtypeStruct((N_CHIPS * rows_k, D), BF16), jax.ShapeDtypeStruct((N_CHIPS,) + conv_w.shape, F32)],
        scratch_shapes=[pltpu.SemaphoreType.DMA((11,)), pltpu.SemaphoreType.DMA((11,))],
        compiler_params=pltpu.CompilerParams(has_side_effects=True),
    )(shard, conv_w)


def _weight_copies(shard, land, send_sems, recv_sems, arrivals):
    x, y, c = _place()
    rows_k = shard.shape[0]
    peers = [(px, py, c) for px, py in _other_chips(x, y)] + [(x, y, 1 - c)]
    cps = []
    for jn, peer in enumerate(peers):
        at = 2 * peer[0] + peer[1] if arrivals else 2 * x + y
        cps.append(pltpu.make_async_remote_copy(
            src_ref=shard, dst_ref=land.at[pl.ds(pl.multiple_of(at * rows_k, 16), rows_k)],
            send_sem=send_sems.at[jn], recv_sem=recv_sems.at[jn], device_id=peer, device_id_type=MESH))
    return cps


def _weights_start(shards, after):
    n = len(shards)
    lands = [lax.empty((N_CHIPS * sh.shape[0], D), BF16) for sh in shards]

    def body(*refs):
        src, land = refs[:n], refs[n:2 * n]
        send_sems, recv_sems = refs[2 * n + 1:3 * n + 1], refs[3 * n + 1:4 * n + 1]
        for k in range(n):
            for send in _weight_copies(src[k], land[k], send_sems[k], recv_sems[k], False):
                send.start()
        refs[-1][...] = jnp.zeros_like(refs[-1])

    res = pl.pallas_call(
        body, name="weights_start",
        in_specs=[HBM] * (2 * n) + [ANY], out_specs=[SEM] * (2 * n) + [HBM] * (2 * n) + [VMEM],
        out_shape=[pltpu.SemaphoreType.DMA((4,))] * (2 * n)
        + [pltpu.HBM(a.shape, a.dtype) for a in (*shards, *lands)] + [jax.ShapeDtypeStruct((8, 128), F32)],
        input_output_aliases={i: i + 2 * n for i in range(2 * n)},
        compiler_params=pltpu.CompilerParams(has_side_effects=DATAFLOW),
    )(*[_hbm(a) for a in (*shards, *lands)], after)
    return [(res[k], res[n + k], res[2 * n + k], res[3 * n + k]) for k in range(n)], res[-1]


def _weights_wait(started, after, name):
    send_sems, recv_sems, shard, land = started

    def body(s_ref, l_ref, send_ref, recv_ref, after_ref, s_out, l_out):
        for cp in _weight_copies(s_ref, l_ref, send_ref, recv_ref, True):
            cp.wait_send()
            cp.wait_recv()

    return pl.pallas_call(
        body, name=name,
        in_specs=[HBM, HBM, SEM, SEM, ANY], out_specs=[HBM, HBM],
        out_shape=[pltpu.HBM(shard.shape, shard.dtype), pltpu.HBM(land.shape, land.dtype)],
        input_output_aliases={0: 0, 1: 1},
        compiler_params=pltpu.CompilerParams(has_side_effects=DATAFLOW),
    )(shard, land, send_sems, recv_sems, after)[1]


def _grad_copies(g_ref, land_ref, send_sems, recv_sems):
    x, y, c = _place()
    cps = []
    for d in range(1, 8):
        px, py, pc = x ^ (d >> 2), y ^ ((d >> 1) & 1), c ^ (d & 1)
        cps.append(pltpu.make_async_remote_copy(
            src_ref=g_ref.at[2 * px + py, pc], dst_ref=land_ref.at[d - 1], send_sem=send_sems.at[d - 1],
            recv_sem=recv_sems.at[d - 1], device_id=(px, py, pc), device_id_type=MESH))
    return cps


def _grads_start(grads_b, name):
    n = len(grads_b)
    lands = [lax.empty((7, g.shape[2], D), BF16) for g in grads_b]

    def body(*refs):
        g, land = refs[:n], refs[n:2 * n]
        send_sems, recv_sems = refs[2 * n:3 * n], refs[3 * n:4 * n]
        for k in range(n):
            for cp in _grad_copies(g[k], land[k], send_sems[k], recv_sems[k]):
                cp.start()
        refs[-1][...] = jnp.zeros_like(refs[-1])

    res = pl.pallas_call(
        body, name=name,
        in_specs=[HBM] * (2 * n), out_specs=[SEM] * (2 * n) + [HBM] * (2 * n) + [VMEM],
        out_shape=[pltpu.SemaphoreType.DMA((7,))] * (2 * n)
        + [pltpu.HBM(a.shape, a.dtype) for a in (*grads_b, *lands)] + [jax.ShapeDtypeStruct((8, 128), F32)],
        input_output_aliases={i: i + 2 * n for i in range(2 * n)},
        compiler_params=pltpu.CompilerParams(has_side_effects=DATAFLOW),
    )(*[_hbm(a) for a in (*grads_b, *lands)])
    return [(res[k], res[n + k], res[2 * n + k], res[3 * n + k]) for k in range(n)], res[-1]


def _grads_wait(started, after):
    n = len(started)

    def body(*refs):
        g, land = refs[:n], refs[n:2 * n]
        send_sems, recv_sems = refs[2 * n:3 * n], refs[3 * n:4 * n]
        for k in range(n):
            for cp in _grad_copies(g[k], land[k], send_sems[k], recv_sems[k]):
                cp.wait_send()
                cp.wait_recv()

    gs = [st[2] for st in started]
    lands = [st[3] for st in started]
    res = pl.pallas_call(
        body, name="grads_wait",
        in_specs=[HBM] * (2 * n) + [SEM] * (2 * n) + [ANY], out_specs=[HBM] * (2 * n),
        out_shape=[pltpu.HBM(a.shape, a.dtype) for a in (*gs, *lands)],
        input_output_aliases={i: i for i in range(2 * n)},
        compiler_params=pltpu.CompilerParams(has_side_effects=DATAFLOW),
    )(*gs, *lands, *[st[0] for st in started], *[st[1] for st in started], after)
    return res[n:]


def _sum_partials(grad4, got, cb, name, tr):
    h = grad4.shape[2]
    per_half = h // tr

    def body(cb_ref, g_ref, o_ref, out_ref):
        acc = g_ref[...]
        for j in range(7):
            acc = acc + o_ref[j].astype(F32)
        out_ref[...] = acc

    return pl.pallas_call(
        body, name=name,
        grid_spec=pltpu.PrefetchScalarGridSpec(
            num_scalar_prefetch=1, grid=(per_half,),
            in_specs=[pl.BlockSpec((None, None, tr, D), lambda i, cb_ref: (cb_ref[1], cb_ref[0], i, 0)),
                      pl.BlockSpec((7, tr, D), lambda i, cb_ref: (0, i, 0))],
            out_specs=pl.BlockSpec((tr, D), lambda i, cb_ref: (cb_ref[0] * per_half + i, 0))),
        out_shape=jax.ShapeDtypeStruct((2 * h, D), F32),
        compiler_params=_cp(("arbitrary",)),
    )(cb, grad4, got)


def _share_halves(shards, small):
    n = len(shards)
    rows = small.shape[0]

    def body(*refs):
        small_ref = refs[n]
        out, total_ref = refs[n + 1:2 * n + 1], refs[2 * n + 1]
        all_ref, send_sems, recv_sems, ssend, srecv = refs[2 * n + 2:]
        x, y, c = _place()
        me = 4 * x + 2 * y + c
        cps = []
        for k in range(n):
            h = shards[k].shape[0] // 2
            mine = out[k].at[pl.ds(pl.multiple_of(c * h, 8), h)]
            cp = pltpu.make_async_remote_copy(src_ref=mine, dst_ref=mine, send_sem=send_sems.at[k],
                                              recv_sem=recv_sems.at[k], device_id=(x, y, 1 - c), device_id_type=MESH)
            cp.start()
            cps.append(cp)
        all_ref[me] = small_ref[...]
        peers = []
        for d in range(1, 8):
            px, py, pc = x ^ (d >> 2), y ^ ((d >> 1) & 1), c ^ (d & 1)
            cp = pltpu.make_async_remote_copy(src_ref=small_ref, dst_ref=all_ref.at[me],
                                              send_sem=ssend.at[d - 1], recv_sem=srecv.at[d - 1],
                                              device_id=(px, py, pc), device_id_type=MESH)
            cp.start()
            peers.append(cp)
        for cp in peers:
            cp.wait()
        acc = all_ref[0]
        for d in range(1, 8):
            acc = acc + all_ref[d]
        total_ref[...] = acc
        for cp in cps:
            cp.wait()

    return pl.pallas_call(
        body, name="share_halves",
        in_specs=[ANY] * n + [VMEM], out_specs=[ANY] * n + [VMEM],
        out_shape=[jax.ShapeDtypeStruct(sh.shape, F32) for sh in shards] + [jax.ShapeDtypeStruct((rows, D), F32)],
        input_output_aliases={k: k for k in range(n)},
        scratch_shapes=[pltpu.VMEM((8, rows, D), F32), pltpu.SemaphoreType.DMA((n,)), pltpu.SemaphoreType.DMA((n,)),
                        pltpu.SemaphoreType.DMA((7,)), pltpu.SemaphoreType.DMA((7,))],
        compiler_params=pltpu.CompilerParams(has_side_effects=True),
    )(*shards, small)


def _adamw(w, g, m, v, name, tr):
    rows, cols = w.shape

    def body(w_ref, g_ref, m_ref, v_ref, d_ref, nm_ref, nv_ref):
        g_ = g_ref[...]
        nm = ADAM_B1 * m_ref[...] + (1.0 - ADAM_B1) * g_
        nv = ADAM_B2 * v_ref[...] + (1.0 - ADAM_B2) * (g_ * g_)
        m_hat = nm / (1.0 - ADAM_B1 ** ADAM_STEP)
        v_hat = nv / (1.0 - ADAM_B2 ** ADAM_STEP)
        d_ref[...] = -ADAM_LR * (m_hat / (jnp.sqrt(v_hat) + ADAM_EPS) + ADAM_WD * w_ref[...])
        nm_ref[...] = nm
        nv_ref[...] = nv

    spec = pl.BlockSpec((tr, cols), lambda i: (i, 0))
    return pl.pallas_call(
        body, name=name, grid=(rows // tr,),
        in_specs=[spec] * 4, out_specs=[spec] * 3,
        out_shape=[jax.ShapeDtypeStruct((rows, cols), F32)] * 3,
        compiler_params=_cp(("parallel",)),
    )(w, g, m, v)


def _local_step(x, target, w_in_t, late_weights, norm_a_g, norm_b_g, sinks_a, ln1_g, ln1_b,
                conv_w, conv_b, ln2_g, ln2_b, slopes, on_grad):
    cwb = jnp.concatenate([conv_w, conv_b[None]], axis=0).reshape(4, 2, FF)

    proj = _proj(x, w_in_t, "proj")
    o_a, lse_a = _attn_a_fwd(proj, sinks_a)
    fwd_b = [_attn_b_fwd(proj, slopes, r) for r in B_DILATIONS]
    w_o = late_weights(1, fwd_b[-1][1])
    o_b, lse_b, cat, z1, h1, h1b = _mix_ln1(x, o_a, [f[0] for f in fwd_b], [f[1] for f in fwd_b],
                                           norm_a_g, norm_b_g, w_o, ln1_g, ln1_b)
    w_up_t = late_weights(2, h1b)
    w_up3 = w_up_t.reshape(2, FF, D)
    up, a, gate, a1 = _up_conv_gelu(h1b, w_up_t.T, cwb)
    w_down = late_weights(3, a)
    dz2, dz2b, st2 = _down_ln2_loss(a, w_down, h1, target, ln2_g, ln2_b)

    on_grad(3, *_grad_w(a, dz2b, "grad_w_down", tm=FF // 2))
    dup, dconv = _conv_gelu_bwd(dz2b, w_down.T, up, gate, a1, cwb)
    on_grad(2, *_grad_w(dup, h1b, "grad_w_up", tm=FF // 2, lhs_halves=True))
    dz1, dz1b, st1 = _dh1_ln1_bwd(dz2, dup, w_up3, z1, ln1_g)
    tok = on_grad(1, *_grad_w(cat, dz1b, "grad_w_o", tm=512))
    d_oa, d_ob, st_n = _dcat_rms_bwd(dz1b, w_o, o_a, o_b, norm_a_g + tok[0, 0], norm_b_g)
    dqa, dka, dva, dsink = _attn_a_bwd(proj, sinks_a, d_oa, o_a, lse_a)
    bwd_b = [_attn_b_bwd(proj, slopes, d_ob, o_b, lse_b, r) for r in B_DILATIONS]
    dproj = _dproj_combine(dqa, dka, dva, bwd_b)
    tok = on_grad(0, *_grad_w(dproj, x, "grad_w_in", tm=WA))
    gx = _grad_x(dz1, dproj, w_in_t, tok)

    dconv = dconv.reshape(4, 2 * FF)
    small = dict(loss=st2[2, 0:1], norm_a_g=st_n[0], norm_b_g=st_n[1], sinks_a=dsink[:, 0],
                 ln1_g=st1[0], ln1_b=st1[1], conv_w=dconv[0:3].reshape(-1), conv_b=dconv[3],
                 ln2_g=st2[0], ln2_b=st2[1])
    return gx, small


SMALL_ORDER = ("loss", "norm_a_g", "norm_b_g", "sinks_a", "ln1_g", "ln1_b", "conv_b", "ln2_g", "ln2_b", "conv_w")
SMALL_SIZES = dict(loss=1, norm_a_g=512, norm_b_g=512, sinks_a=8, ln1_g=D, ln1_b=D, conv_b=2 * FF, ln2_g=D, ln2_b=D,
                   conv_w=3 * 2 * FF)


def _pack(parts, rows):
    flat = jnp.concatenate([parts[k].reshape(-1).astype(F32) for k in parts])
    return jnp.pad(flat, (0, rows * D - flat.shape[0])).reshape(rows, D)


def _unpack(buf, names, sizes):
    flat = buf.reshape(-1)
    out, at = {}, 0
    for k in names:
        out[k] = flat[at:at + sizes[k]]
        at += sizes[k]
    return out


def kernel(x, w_in, norm_a_g, norm_b_g, sinks_a, w_o, ln1_g, ln1_b, w_up, conv_w, conv_b, w_down, ln2_g, ln2_b, loss_target, m_w_in, m_norm_a_g, m_norm_b_g, m_sinks_a, m_w_o, m_ln1_g, m_ln1_b, m_w_up, m_conv_w, m_conv_b, m_w_down, m_ln2_g, m_ln2_b, v_w_in, v_norm_a_g, v_norm_b_g, v_sinks_a, v_w_o, v_ln1_g, v_ln1_b, v_w_up, v_conv_w, v_conv_b, v_w_down, v_ln2_g, v_ln2_b):
    xi, yi, ci = _place()
    chip = (2 * xi + yi).astype(I32)
    core = ci.astype(I32)

    w_in_rows, m_w_in_rows, v_w_in_rows = w_in.T, m_w_in.T, v_w_in.T
    shards = (w_in_rows.astype(BF16), w_o.astype(BF16), w_up.T.astype(BF16), w_down.astype(BF16))
    w_in_t, conv_w4 = _gather_w_in(shards[0], conv_w)
    conv_w_f = conv_w4.transpose(1, 0, 2).reshape(3, 2 * FF)
    w_started, w_tok = _weights_start(shards[1:], conv_w4)
    slopes = jnp.asarray(SLOPES, F32) + w_tok[0, 0]

    halves_rows = [r // 2 for r in SHARD_ROWS]
    grads4, grads_b4, started = [None] * 4, [None] * 4, [None] * 4

    def on_grad(k, g, g_b):
        grads4[k] = g.reshape(N_CHIPS, 2, halves_rows[k], D)
        grads_b4[k] = g_b.reshape(N_CHIPS, 2, halves_rows[k], D)
        if k > 1:
            return None
        group = (1, 2, 3) if k == 1 else (0,)
        sts, tok = _grads_start([grads_b4[i] for i in group], f"grads_start_{k}")
        for i, st in zip(group, sts):
            started[i] = st
        return tok

    gx, small = _local_step(
        x[0], loss_target[0], w_in_t, lambda k, after: _weights_wait(w_started[k - 1], after, f"weights_wait_{k}"),
        norm_a_g, norm_b_g, sinks_a, ln1_g, ln1_b, conv_w_f, conv_b, ln2_g, ln2_b, slopes, on_grad)

    got = _grads_wait(started, gx)
    tiles = (96, 128, 352, 176)
    core_chip = jnp.stack([core, chip])
    halves = [_sum_partials(grads4[k], got[k], core_chip, f"sum_partials_{k}", tiles[k]) for k in range(4)]
    small_rows = 32
    *full, totals = _share_halves(halves, _pack({k: small[k] for k in SMALL_ORDER}, small_rows))
    tot = _unpack(totals, SMALL_ORDER, SMALL_SIZES)

    g_w_in_rows, g_w_o, g_w_up, g_w_down = full[0], full[1], full[2].T, full[3]
    loss = tot["loss"][0]
    cols = 2 * FF // N_CHIPS
    g_conv_w = lax.dynamic_slice(tot["conv_w"].reshape(3, 2 * FF), (0, chip * cols), (3, cols))
    g_small = dict(norm_a_g=tot["norm_a_g"], norm_b_g=tot["norm_b_g"], sinks_a=tot["sinks_a"], ln1_g=tot["ln1_g"],
                   ln1_b=tot["ln1_b"], conv_w=g_conv_w, conv_b=tot["conv_b"], ln2_g=tot["ln2_g"], ln2_b=tot["ln2_b"])

    weights = dict(w_in=w_in, norm_a_g=norm_a_g, norm_b_g=norm_b_g, sinks_a=sinks_a, w_o=w_o, ln1_g=ln1_g, ln1_b=ln1_b,
                   w_up=w_up, conv_w=conv_w, conv_b=conv_b, w_down=w_down, ln2_g=ln2_g, ln2_b=ln2_b)
    ms = dict(w_in=m_w_in, norm_a_g=m_norm_a_g, norm_b_g=m_norm_b_g, sinks_a=m_sinks_a, w_o=m_w_o, ln1_g=m_ln1_g,
              ln1_b=m_ln1_b, w_up=m_w_up, conv_w=m_conv_w, conv_b=m_conv_b, w_down=m_w_down, ln2_g=m_ln2_g, ln2_b=m_ln2_b)
    vs = dict(w_in=v_w_in, norm_a_g=v_norm_a_g, norm_b_g=v_norm_b_g, sinks_a=v_sinks_a, w_o=v_w_o, ln1_g=v_ln1_g,
              ln1_b=v_ln1_b, w_up=v_w_up, conv_w=v_conv_w, conv_b=v_conv_b, w_down=v_w_down, ln2_g=v_ln2_g, ln2_b=v_ln2_b)
    order = list(weights)
    grad = dict(g_small, w_in=g_w_in_rows.T, w_o=g_w_o, w_up=g_w_up, w_down=g_w_down)

    delta, new_m, new_v = {}, {}, {}
    delta["w_in"], new_m["w_in"], new_v["w_in"] = [
        a.T for a in _adamw(w_in_rows, g_w_in_rows, m_w_in_rows, v_w_in_rows, "adamw_w_in", 144)]
    for k, tr in (("w_o", 128), ("w_up", 256), ("w_down", 176)):
        delta[k], new_m[k], new_v[k] = _adamw(weights[k], grad[k], ms[k], vs[k], f"adamw_{k}", tr)
    small_names = [k for k in order if k not in delta]
    sizes = {k: weights[k].size for k in small_names}
    rows = 16
    packed = [_pack({k: src[k] for k in small_names}, rows) for src in (weights, grad, ms, vs)]
    for res, buf in zip((delta, new_m, new_v), _adamw(*packed, "adamw_small", rows)):
        for k, val in _unpack(buf, small_names, sizes).items():
            res[k] = val.reshape(weights[k].shape)

    return (loss, gx[None], *[grad[k] for k in order], *[delta[k] for k in order],
            *[new_m[k] for k in order], *[new_v[k] for k in order])
```

```python
import functools
import math

import jax
import jax.numpy as jnp
from jax import lax
from jax.experimental import pallas as pl
from jax.experimental.pallas import tpu as pltpu

F32, BF16, I32 = jnp.float32, jnp.bfloat16, jnp.int32

D = 1024
FF = 2816
HD = 64
NH = 8
WA, WB = 768, 1536
WIN = WA + WB
BLK = 128
ALPHA = 2.0 ** 0.25
LN_EPS, RMS_EPS = 1e-5, 1e-6
SCALE = 1.0 / math.sqrt(HD)
A_MAX_DIST, B_MAX_DIST = 127, 128
B_DILATIONS = (1, 4, 16)
SLOPES = tuple(2.0 ** (-(i + 1)) for i in range(NH))
SHARD_ROWS = (WIN // 4, D // 4, 2 * FF // 4, FF // 4)
N_CHIPS = 4
ADAM_LR, ADAM_B1, ADAM_B2, ADAM_EPS, ADAM_WD, ADAM_STEP = 0.001, 0.9, 0.999, 1e-08, 0.01, 10
MESH = pl.DeviceIdType.MESH
ANY = pl.BlockSpec(memory_space=pl.ANY)
SMEM = pl.BlockSpec(memory_space=pltpu.SMEM)
VMEM = pl.BlockSpec(memory_space=pltpu.VMEM)
HBM = pl.BlockSpec(memory_space=pltpu.HBM)
SEM = pl.BlockSpec(memory_space=pltpu.SEMAPHORE)
DATAFLOW = pltpu.SideEffectType.DATAFLOW_SIDE_EFFECTING


def _cp(sem, mb=48):
    return pltpu.CompilerParams(dimension_semantics=sem, vmem_limit_bytes=mb << 20)


def _nn(a, b):
    return lax.dot_general(a, b, (((1,), (0,)), ((), ())), preferred_element_type=F32)


def _nt(a, b):
    return lax.dot_general(a, b, (((1,), (1,)), ((), ())), preferred_element_type=F32)


def _tn(a, b):
    return lax.dot_general(a, b, (((0,), (0,)), ((), ())), preferred_element_type=F32)


def _resident(shape):
    n = len(shape)
    return pl.BlockSpec(shape, lambda *_: (0,) * n, pipeline_mode=pl.Buffered(1))


def _const(shape):
    n = len(shape)
    return pl.BlockSpec(shape, lambda *_: (0,) * n)


def _proj(x, w_t, name, tm=512):
    s = x.shape[0]
    n = w_t.shape[0]

    def body(x_ref, w_ref, o_ref):
        o_ref[...] = _nt(x_ref[...].astype(BF16), w_ref[...])

    return pl.pallas_call(
        body, name=name, grid=(s // tm,),
        in_specs=[pl.BlockSpec((tm, D), lambda i: (i, 0)), _resident((n, D))],
        out_specs=pl.BlockSpec((tm, n), lambda i: (i, 0)),
        out_shape=jax.ShapeDtypeStruct((s, n), F32),
        compiler_params=_cp(("parallel",)),
    )(x, w_t)


def _grad_w(lhs, rhs, name, tm, tk=512, lhs_halves=False):
    s = rhs.shape[0]
    if lhs_halves:
        per_half = lhs.shape[2] // tm
        n = 2 * lhs.shape[2]
        lhs_spec = pl.BlockSpec((None, tk, tm), lambda i, k: (i // per_half, k, i % per_half))
    else:
        n = lhs.shape[1]
        lhs_spec = pl.BlockSpec((tk, tm), lambda i, k: (k, i))
    nk = s // tk

    def body(l_ref, r_ref, o_ref, ob_ref):
        k = pl.program_id(1)

        @pl.when(k == 0)
        def _():
            o_ref[...] = jnp.zeros_like(o_ref)

        o_ref[...] += _tn(l_ref[...].astype(BF16), r_ref[...].astype(BF16))

        @pl.when(k == nk - 1)
        def _():
            ob_ref[...] = o_ref[...].astype(BF16)

    return pl.pallas_call(
        body, name=name, grid=(n // tm, nk),
        in_specs=[lhs_spec, pl.BlockSpec((tk, D), lambda i, k: (k, 0))],
        out_specs=[pl.BlockSpec((tm, D), lambda i, k: (i, 0))] * 2,
        out_shape=[jax.ShapeDtypeStruct((n, D), F32), jax.ShapeDtypeStruct((n, D), BF16)],
        compiler_params=_cp(("parallel", "arbitrary")),
    )(lhs, rhs)


def _band_base(max_dist, dist_unit, first):
    row = lax.broadcasted_iota(I32, (BLK, 2 * BLK), 0)
    col = lax.broadcasted_iota(I32, (BLK, 2 * BLK), 1)
    dist = BLK + row - col
    ok = (dist >= 0) & (dist <= max_dist)
    if first:
        ok = ok & (col >= BLK)
    return jnp.where(ok, dist.astype(F32) * (-float(dist_unit)), -jnp.inf)


def _half_mask(shape, e):
    lane = lax.broadcasted_iota(I32, shape, 1)
    return (lane < HD) if e == 0 else (lane >= HD)


def _to_half(x, e, g):
    if g != e:
        x = pltpu.roll(x, HD, 1)
    return jnp.where(_half_mask(x.shape, g), x, 0.0)


def _stack_heads(scalars, tile):
    return jnp.concatenate([scalars[0] * tile, scalars[1] * tile], axis=0)


def _pair_fwd(q2, kb, vb, base, slopes, kv_heads, sinks):
    lo = _half_mask((BLK, 2 * HD), 0)
    if sinks is not None:
        o2 = lse2 = None
        for e in (0, 1):
            g = kv_heads[e]
            qv = (_to_half(q2, e, g) * SCALE).astype(BF16)
            s = _nt(qv, kb) + slopes[e] * base
            m = jnp.maximum(jnp.max(s, axis=1, keepdims=True), sinks[e])
            p = jnp.exp(s - m)
            l = jnp.sum(p, axis=1, keepdims=True) + jnp.exp(sinks[e] - m)
            oh = _nn(p.astype(BF16), vb) / l
            if g != e:
                oh = pltpu.roll(oh, HD, 1)
            lse = jnp.broadcast_to(m + jnp.log(l), (BLK, 2 * HD))
            o2 = oh if e == 0 else jnp.where(lo, o2, oh)
            lse2 = lse if e == 0 else jnp.where(lo, lse2, lse)
        return o2, lse2
    qs = jnp.concatenate([_to_half(q2, e, kv_heads[e]) * SCALE for e in (0, 1)], axis=0).astype(BF16)
    s = _nt(qs, kb) + (base if slopes is None else _stack_heads(slopes, base))
    m = jnp.max(s, axis=1, keepdims=True)
    p = jnp.exp(s - m)
    l = jnp.sum(p, axis=1, keepdims=True)
    o = _nn(p.astype(BF16), vb) / l
    lse = m + jnp.log(l)
    halves = []
    for e in (0, 1):
        oh = o[e * BLK:(e + 1) * BLK]
        halves.append(pltpu.roll(oh, HD, 1) if kv_heads[e] != e else oh)
    o2 = jnp.where(lo, halves[0], halves[1])
    lse2 = jnp.where(lo, jnp.broadcast_to(lse[:BLK], (BLK, 2 * HD)), jnp.broadcast_to(lse[BLK:], (BLK, 2 * HD)))
    return o2, lse2


def _pair_bwd(q2, kb, vb, do2, o2, lse2, base, slopes, kv_heads, sinks):
    lo = _half_mask((BLK, 2 * HD), 0)
    prod = do2 * o2
    lses, deltas = [], []
    for e in (0, 1):
        hq = _half_mask((BLK, 2 * HD), e)
        lses.append(jnp.max(jnp.where(hq, lse2, -jnp.inf), axis=1, keepdims=True))
        deltas.append(jnp.sum(jnp.where(hq, prod, 0.0), axis=1, keepdims=True))
    lse = jnp.concatenate(lses, axis=0)
    delta = jnp.concatenate(deltas, axis=0)
    qs = jnp.concatenate([_to_half(q2, e, kv_heads[e]) * SCALE for e in (0, 1)], axis=0).astype(BF16)
    dos = jnp.concatenate([_to_half(do2, e, kv_heads[e]) for e in (0, 1)], axis=0).astype(BF16)
    p = jnp.exp(_nt(qs, kb) + (base if slopes is None else _stack_heads(slopes, base)) - lse)
    ds = (p * (_nt(dos, vb) - delta)).astype(BF16)
    dq = _nn(ds, kb) * SCALE
    halves = []
    for e in (0, 1):
        dqh = dq[e * BLK:(e + 1) * BLK]
        halves.append(pltpu.roll(dqh, HD, 1) if kv_heads[e] != e else dqh)
    dq2 = jnp.where(lo, halves[0], halves[1])
    dk2 = _tn(ds, qs)
    dv2 = _tn(p.astype(BF16), dos)
    dsinks = []
    if sinks is not None:
        for e in (0, 1):
            dsinks.append(jnp.sum(-jnp.exp(sinks[e] - lses[e]) * deltas[e], axis=0, keepdims=True))
    return dq2, dk2, dv2, dsinks


def _attn_a_fwd(proj, sinks):
    s = proj.shape[0]
    nb = s // BLK

    def body(sink_ref, q_ref, kp_ref, kc_ref, vp_ref, vc_ref, o_ref, lse_ref):
        n = pl.program_id(0)
        base = jnp.where(n > 0, _band_base(A_MAX_DIST, 1, False), _band_base(A_MAX_DIST, 1, True))
        kb = jnp.concatenate([kp_ref[...], kc_ref[...]], axis=0).astype(BF16)
        vb = jnp.concatenate([vp_ref[...], vc_ref[...]], axis=0).astype(BF16)
        for j in range(NH // 2):
            g = j // 2
            o2, lse2 = _pair_fwd(q_ref[:, 128 * j:128 * (j + 1)], kb, vb, base, (SLOPES[2 * j], SLOPES[2 * j + 1]),
                                 (g, g), (sink_ref[2 * j], sink_ref[2 * j + 1]))
            o_ref[:, 128 * j:128 * (j + 1)] = o2
            lse_ref[:, 128 * j:128 * (j + 1)] = lse2

    prev = lambda n: jnp.maximum(n - 1, 0)
    return pl.pallas_call(
        body, name="attn_a_fwd", grid=(nb,),
        in_specs=[SMEM,
                  pl.BlockSpec((BLK, 512), lambda n: (n, 0)),
                  pl.BlockSpec((BLK, 128), lambda n: (prev(n), 4)), pl.BlockSpec((BLK, 128), lambda n: (n, 4)),
                  pl.BlockSpec((BLK, 128), lambda n: (prev(n), 5)), pl.BlockSpec((BLK, 128), lambda n: (n, 5))],
        out_specs=[pl.BlockSpec((BLK, 512), lambda n: (n, 0))] * 2,
        out_shape=[jax.ShapeDtypeStruct((s, 512), F32)] * 2,
        compiler_params=_cp(("parallel",)),
    )(sinks, proj, proj, proj, proj, proj)


def _attn_a_bwd(proj, sinks, d_o, o, lse):
    s = proj.shape[0]
    nb = s // BLK

    def body(sink_ref, q_ref, kp_ref, kc_ref, vp_ref, vc_ref, do_ref, o_ref, lse_ref,
             dq_ref, dk_ref, dv_ref, dsink_ref, kcar, vcar):
        n = pl.program_id(0)

        @pl.when(n == 0)
        def _():
            kcar[...] = jnp.zeros_like(kcar)
            vcar[...] = jnp.zeros_like(vcar)
            dsink_ref[...] = jnp.zeros_like(dsink_ref)

        @pl.when(n < nb)
        def _():
            base = jnp.where(n > 0, _band_base(A_MAX_DIST, 1, False), _band_base(A_MAX_DIST, 1, True))
            kb = jnp.concatenate([kp_ref[...], kc_ref[...]], axis=0).astype(BF16)
            vb = jnp.concatenate([vp_ref[...], vc_ref[...]], axis=0).astype(BF16)
            dk_win = dv_win = None
            for j in range(NH // 2):
                g = j // 2
                sl = slice(128 * j, 128 * (j + 1))
                dq2, dk2, dv2, dsk = _pair_bwd(q_ref[:, sl], kb, vb, do_ref[:, sl], o_ref[:, sl], lse_ref[:, sl], base,
                                               (SLOPES[2 * j], SLOPES[2 * j + 1]), (g, g),
                                               (sink_ref[2 * j], sink_ref[2 * j + 1]))
                dq_ref[:, sl] = dq2
                dk_win = dk2 if j == 0 else dk_win + dk2
                dv_win = dv2 if j == 0 else dv_win + dv2
                for e in (0, 1):
                    h = 2 * j + e
                    dsink_ref[h:h + 1, :] += jnp.broadcast_to(dsk[e], (1, 128))
            dk_ref[...] = kcar[...] + dk_win[:BLK]
            dv_ref[...] = vcar[...] + dv_win[:BLK]
            kcar[...] = dk_win[BLK:]
            vcar[...] = dv_win[BLK:]

        @pl.when(n == nb)
        def _():
            dk_ref[...] = kcar[...]
            dv_ref[...] = vcar[...]

    cur = lambda n: jnp.minimum(n, nb - 1)
    prev = lambda n: jnp.maximum(cur(n) - 1, 0)
    out_prev = lambda n: jnp.maximum(n - 1, 0)
    return pl.pallas_call(
        body, name="attn_a_bwd", grid=(nb + 1,),
        in_specs=[SMEM,
                  pl.BlockSpec((BLK, 512), lambda n: (cur(n), 0)),
                  pl.BlockSpec((BLK, 128), lambda n: (prev(n), 4)), pl.BlockSpec((BLK, 128), lambda n: (cur(n), 4)),
                  pl.BlockSpec((BLK, 128), lambda n: (prev(n), 5)), pl.BlockSpec((BLK, 128), lambda n: (cur(n), 5)),
                  pl.BlockSpec((BLK, 512), lambda n: (cur(n), 0)),
                  pl.BlockSpec((BLK, 512), lambda n: (cur(n), 0)),
                  pl.BlockSpec((BLK, 512), lambda n: (cur(n), 0))],
        out_specs=[pl.BlockSpec((BLK, 512), lambda n: (cur(n), 0)),
                   pl.BlockSpec((BLK, 128), lambda n: (out_prev(n), 0)),
                   pl.BlockSpec((BLK, 128), lambda n: (out_prev(n), 0)),
                   pl.BlockSpec((NH, 128), lambda n: (0, 0))],
        out_shape=[jax.ShapeDtypeStruct((s, 512), F32), jax.ShapeDtypeStruct((s, 128), F32),
                   jax.ShapeDtypeStruct((s, 128), F32), jax.ShapeDtypeStruct((NH, 128), F32)],
        scratch_shapes=[pltpu.VMEM((BLK, 128), F32), pltpu.VMEM((BLK, 128), F32)],
        compiler_params=_cp(("arbitrary",)),
    )(sinks, proj, proj, proj, proj, proj, d_o, o, lse)


def _stream(rho, i, r):
    start = i * BLK * r + rho
    return pl.ds(start, BLK, stride=r) if r > 1 else pl.ds(start, BLK)


def _for_streams(r, fn):
    if r <= 4:
        for rho in range(r):
            fn(rho)
    else:
        def four(it, carry):
            for u in range(4):
                fn(4 * it + u)
            return carry

        lax.fori_loop(0, r // 4, four, 0)


B_BLOCKS_PER_STEP = {1: 4, 4: 1, 16: 1}


def _attn_b_fwd(proj, slopes, r):
    s = proj.shape[0]
    nq = B_BLOCKS_PER_STEP[r]
    rows = BLK * r * nq
    steps = s // rows
    qc, kc, vc = WA // 128, WA // 128 + 4, WA // 128 + 8

    def body(slope_ref, q_ref, kp_ref, kc_ref, vp_ref, vc_ref, o_ref, lse_ref):
        j = pl.program_id(0)
        sb = pl.program_id(1)
        sl2 = (slope_ref[2 * j], slope_ref[2 * j + 1])
        bias_rest = _stack_heads(sl2, _band_base(B_MAX_DIST, r, False))
        bias_0 = jnp.where(sb > 0, bias_rest, _stack_heads(sl2, _band_base(B_MAX_DIST, r, True)))

        def stream(rho):
            for i in range(nq):
                cur = _stream(rho, i, r)
                k_prev = kc_ref[_stream(rho, i - 1, r), :] if i > 0 else kp_ref[_stream(rho, 0, r), :]
                v_prev = vc_ref[_stream(rho, i - 1, r), :] if i > 0 else vp_ref[_stream(rho, 0, r), :]
                kb = jnp.concatenate([k_prev, kc_ref[cur, :]], axis=0).astype(BF16)
                vb = jnp.concatenate([v_prev, vc_ref[cur, :]], axis=0).astype(BF16)
                o2, lse2 = _pair_fwd(q_ref[cur, :], kb, vb, bias_rest if i > 0 else bias_0, None, (0, 1), None)
                o_ref[cur, :] = o2
                lse_ref[cur, :] = lse2

        _for_streams(r, stream)

    before = lambda sb: jnp.maximum(sb * nq - 1, 0)
    return pl.pallas_call(
        body, name=f"attn_b_fwd_r{r}", grid=(NH // 2, steps),
        in_specs=[SMEM,
                  pl.BlockSpec((rows, 128), lambda j, sb: (sb, qc + j)),
                  pl.BlockSpec((BLK * r, 128), lambda j, sb: (before(sb), kc + j)),
                  pl.BlockSpec((rows, 128), lambda j, sb: (sb, kc + j)),
                  pl.BlockSpec((BLK * r, 128), lambda j, sb: (before(sb), vc + j)),
                  pl.BlockSpec((rows, 128), lambda j, sb: (sb, vc + j))],
        out_specs=[pl.BlockSpec((rows, 128), lambda j, sb: (sb, j))] * 2,
        out_shape=[jax.ShapeDtypeStruct((s, 512), F32)] * 2,
        compiler_params=_cp(("parallel", "parallel")),
    )(slopes, proj, proj, proj, proj, proj)


def _attn_b_bwd(proj, slopes, d_o, o, lse, r):
    s = proj.shape[0]
    nq = B_BLOCKS_PER_STEP[r]
    rows = BLK * r * nq
    steps = s // rows
    qc, kc, vc = WA // 128, WA // 128 + 4, WA // 128 + 8

    def body(slope_ref, q_ref, kp_ref, kc_ref, vp_ref, vc_ref, do_ref, o_ref, lse_ref,
             dq_ref, dk_ref, dv_ref, kcar, vcar):
        j = pl.program_id(0)
        sb = pl.program_id(1)

        @pl.when(sb == 0)
        def _():
            kcar[...] = jnp.zeros_like(kcar)
            vcar[...] = jnp.zeros_like(vcar)

        dk_ref[...] = kcar[...]
        dv_ref[...] = vcar[...]

        @pl.when(sb < steps)
        def _():
            sl2 = (slope_ref[2 * j], slope_ref[2 * j + 1])
            bias_rest = _stack_heads(sl2, _band_base(B_MAX_DIST, r, False))
            bias_0 = jnp.where(sb > 0, bias_rest, _stack_heads(sl2, _band_base(B_MAX_DIST, r, True)))

            def stream(rho):
                for i in range(nq):
                    cur = _stream(rho, i, r)
                    k_prev = kc_ref[_stream(rho, i - 1, r), :] if i > 0 else kp_ref[_stream(rho, 0, r), :]
                    v_prev = vc_ref[_stream(rho, i - 1, r), :] if i > 0 else vp_ref[_stream(rho, 0, r), :]
                    kb = jnp.concatenate([k_prev, kc_ref[cur, :]], axis=0).astype(BF16)
                    vb = jnp.concatenate([v_prev, vc_ref[cur, :]], axis=0).astype(BF16)
                    dq2, dk2, dv2, _ = _pair_bwd(q_ref[cur, :], kb, vb, do_ref[cur, :], o_ref[cur, :], lse_ref[cur, :],
                                                 bias_rest if i > 0 else bias_0, None, (0, 1), None)
                    dq_ref[cur, :] = dq2
                    if i == 0:
                        last = _stream(rho, nq - 1, r)
                        dk_ref[last, :] += dk2[:BLK]
                        dv_ref[last, :] += dv2[:BLK]
                    else:
                        kcar[_stream(rho, i - 1, r), :] += dk2[:BLK]
                        vcar[_stream(rho, i - 1, r), :] += dv2[:BLK]
                    kcar[cur, :] = dk2[BLK:]
                    vcar[cur, :] = dv2[BLK:]

            _for_streams(r, stream)

    cur_step = lambda sb: jnp.minimum(sb, steps - 1)
    before = lambda sb: jnp.maximum(cur_step(sb) * nq - 1, 0)
    out_prev = lambda sb: jnp.maximum(sb - 1, 0)
    tile = lambda col: pl.BlockSpec((rows, 128), lambda j, sb: (cur_step(sb), col + j))
    edge = lambda col: pl.BlockSpec((BLK * r, 128), lambda j, sb: (before(sb), col + j))
    return pl.pallas_call(
        body, name=f"attn_b_bwd_r{r}", grid=(NH // 2, steps + 1),
        in_specs=[SMEM, tile(qc), edge(kc), tile(kc), edge(vc), tile(vc), tile(0), tile(0), tile(0)],
        out_specs=[tile(0),
                   pl.BlockSpec((rows, 128), lambda j, sb: (out_prev(sb), j)),
                   pl.BlockSpec((rows, 128), lambda j, sb: (out_prev(sb), j))],
        out_shape=[jax.ShapeDtypeStruct((s, 512), F32)] * 3,
        scratch_shapes=[pltpu.VMEM((rows, 128), F32), pltpu.VMEM((rows, 128), F32)],
        compiler_params=_cp(("parallel", "arbitrary")),
    )(slopes, proj, proj, proj, proj, proj, d_o, o, lse)


def _row(v):
    return v.reshape(1, -1)


def _layer_norm_stats(z):
    mu = jnp.mean(z, axis=-1, keepdims=True)
    zc = z - mu
    var = jnp.mean(zc * zc, axis=-1, keepdims=True)
    rstd = lax.rsqrt(var + LN_EPS)
    return zc * rstd, rstd


def _layer_norm_bwd(dh, zh, rstd, g):
    dzh = dh * g
    return rstd * (dzh - jnp.mean(dzh, axis=-1, keepdims=True) - zh * jnp.mean(dzh * zh, axis=-1, keepdims=True))


def _rms(o):
    return lax.rsqrt(jnp.mean(o * o, axis=-1, keepdims=True) + RMS_EPS)


def _mix_ln1(x, o_a, o_b, lse_b, norm_a_g, norm_b_g, w_o, ln1_g, ln1_b, tm=256):
    s = x.shape[0]

    def body(x_ref, oa_ref, ob1, ob2, ob3, l1, l2, l3, ga_ref, gb_ref, wo_ref, g_ref, b_ref,
             obm_ref, lse_ref, cat_ref, z1_ref, h1_ref, h1b_ref):
        la, lb, lc = l1[...], l2[...], l3[...]
        m = jnp.maximum(jnp.maximum(la, lb), lc)
        ea, eb, ec = jnp.exp(la - m), jnp.exp(lb - m), jnp.exp(lc - m)
        den = ea + eb + ec
        obm = (ea / den) * ob1[...] + (eb / den) * ob2[...] + (ec / den) * ob3[...]
        obm_ref[...] = obm
        lse_ref[...] = m + jnp.log(den)
        oa = oa_ref[...]
        na = oa * _rms(oa) * ga_ref[...]
        nb_ = obm * _rms(obm) * gb_ref[...]
        cat = jnp.concatenate([na, nb_], axis=1).astype(BF16)
        cat_ref[...] = cat
        z1 = ALPHA * x_ref[...] + _nn(cat, wo_ref[...])
        z1_ref[...] = z1
        zh, _ = _layer_norm_stats(z1)
        h1 = zh * g_ref[...] + b_ref[...]
        h1_ref[...] = h1
        h1b_ref[...] = h1.astype(BF16)

    t512 = pl.BlockSpec((tm, 512), lambda i: (i, 0))
    td = pl.BlockSpec((tm, D), lambda i: (i, 0))
    return pl.pallas_call(
        body, name="mix_ln1", grid=(s // tm,),
        in_specs=[td] + [t512] * 7 + [_const((1, 512))] * 2 + [_resident((D, D))] + [_const((1, D))] * 2,
        out_specs=[t512, t512, td, td, td, td],
        out_shape=[jax.ShapeDtypeStruct((s, 512), F32), jax.ShapeDtypeStruct((s, 512), F32),
                   jax.ShapeDtypeStruct((s, D), BF16), jax.ShapeDtypeStruct((s, D), F32),
                   jax.ShapeDtypeStruct((s, D), F32), jax.ShapeDtypeStruct((s, D), BF16)],
        compiler_params=_cp(("parallel",)),
    )(x, o_a, *o_b, *lse_b, _row(norm_a_g), _row(norm_b_g), w_o, _row(ln1_g), _row(ln1_b))


def _gelu_and_grad(x):
    c = math.sqrt(2.0 / math.pi)
    x2 = x * x
    cx = c * x
    t = jnp.tanh(cx * (1.0 + 0.044715 * x2))
    q = 1.0 + t
    g = (0.5 * x) * q
    dg = 0.5 * q + ((0.5 * cx) * (1.0 - t * t)) * (1.0 + (3.0 * 0.044715) * x2)
    return g, dg


CONV_CHUNK = 64


def _shift_down(u, before):
    n = u.shape[0]
    ext = jnp.concatenate([before, u], axis=0)
    return pltpu.roll(ext, 1, 0)[8:], pltpu.roll(ext, 2, 0)[8:]


def _shift_up(u, after):
    n = u.shape[0]
    ext = jnp.concatenate([u, after], axis=0)
    return pltpu.roll(ext, n + 7, 0)[:n], pltpu.roll(ext, n + 6, 0)[:n]


def _up_conv_gelu(h1b, w_up, cwb, tm=512, tn=256):
    s = h1b.shape[0]
    n_i = s // tm
    n_t = (FF // tn) * n_i

    def body(h_ref, wg_ref, wv_ref, c_ref, up_ref, a_ref, g_ref, a1_ref, pend_a, pend_b, carry):
        t = pl.program_id(0)
        row_tile = jnp.maximum(t - 1, 0) % n_i
        w_refs = (wg_ref, wv_ref)

        @pl.when(t == 0)
        def _():
            pend_b[...] = jnp.zeros_like(pend_b)
            carry[...] = jnp.zeros_like(carry)

        def step(dst, src):
            def chunk(c, before):
                rows = pl.ds(c * CONV_CHUNK, CONV_CHUNK)
                u, last = [], []
                for half in (0, 1):
                    up = src[half, rows, :]
                    r1, r2 = _shift_down(up, before[half])
                    u.append(r2 * c_ref[0, half:half + 1, :] + r1 * c_ref[1, half:half + 1, :]
                             + up * c_ref[2, half:half + 1, :] + c_ref[3, half:half + 1, :])
                    last.append(up[CONV_CHUNK - 8:])
                g, dg = _gelu_and_grad(u[0])
                a_ref[rows, :] = (g * u[1]).astype(BF16)
                g_ref[rows, :] = g.astype(BF16)
                a1_ref[rows, :] = (u[1] * dg).astype(BF16)
                return tuple(last)

            edge = tuple(jnp.where(row_tile > 0, carry[half], 0.0) for half in (0, 1))
            n_c = tm // CONV_CHUNK
            n_k = n_c // 2
            tk = D // n_k
            for half in (0, 1):
                up = None
                for kq in range(n_k):
                    ks = slice(kq * tk, (kq + 1) * tk)
                    part = _nn(h_ref[:, ks], w_refs[half][ks, :])
                    up = part if kq == 0 else up + part
                    edge = chunk(half * n_k + kq, edge)
                up_ref[half] = up
                dst[half] = up
            for half in (0, 1):
                carry[half] = edge[half]

        @pl.when(t % 2 == 0)
        def _():
            step(pend_a, pend_b)

        @pl.when(t % 2 == 1)
        def _():
            step(pend_b, pend_a)

    mm = lambda t: jnp.minimum(t, n_t - 1)
    ew = lambda t: jnp.maximum(t - 1, 0)
    out_tile = pl.BlockSpec((tm, tn), lambda t: (ew(t) % n_i, ew(t) // n_i))
    return pl.pallas_call(
        body, name="up_conv_gelu", grid=(n_t + 1,),
        in_specs=[pl.BlockSpec((tm, D), lambda t: (mm(t) % n_i, 0)),
                  pl.BlockSpec((D, tn), lambda t: (0, mm(t) // n_i)),
                  pl.BlockSpec((D, tn), lambda t: (0, FF // tn + mm(t) // n_i)),
                  pl.BlockSpec((4, 2, tn), lambda t: (0, 0, ew(t) // n_i))],
        out_specs=[pl.BlockSpec((2, tm, tn), lambda t: (0, mm(t) % n_i, mm(t) // n_i)), out_tile, out_tile, out_tile],
        out_shape=[jax.ShapeDtypeStruct((2, s, FF), F32)] + [jax.ShapeDtypeStruct((s, FF), BF16)] * 3,
        scratch_shapes=[pltpu.VMEM((2, tm, tn), F32), pltpu.VMEM((2, tm, tn), F32), pltpu.VMEM((2, 8, tn), F32)],
        compiler_params=_cp(("arbitrary",)),
    )(h1b, w_up, w_up, cwb)


def _down_ln2_loss(a, w_down, h1, target, ln2_g, ln2_b, tm=256):
    s = a.shape[0]

    def body(a_ref, w_ref, h_ref, t_ref, g_ref, b_ref, dz_ref, dzb_ref, st_ref):
        @pl.when(pl.program_id(0) == 0)
        def _():
            st_ref[...] = jnp.zeros_like(st_ref)

        z2 = ALPHA * h_ref[...] + _nn(a_ref[...], w_ref[...])
        zh, rstd = _layer_norm_stats(z2)
        diff = zh * g_ref[...] + b_ref[...] - t_ref[...]
        part = 0.5 * jnp.sum(jnp.mean(diff * diff, axis=-1, keepdims=True), axis=0, keepdims=True)
        dy = diff * (1.0 / D)
        st_ref[0:1, :] += jnp.sum(dy * zh, axis=0, keepdims=True)
        st_ref[1:2, :] += jnp.sum(dy, axis=0, keepdims=True)
        st_ref[2:3, :] += jnp.broadcast_to(part, (1, D))
        dz = _layer_norm_bwd(dy, zh, rstd, g_ref[...])
        dz_ref[...] = dz
        dzb_ref[...] = dz.astype(BF16)

    td = pl.BlockSpec((tm, D), lambda i: (i, 0))
    return pl.pallas_call(
        body, name="down_ln2_loss", grid=(s // tm,),
        in_specs=[pl.BlockSpec((tm, FF), lambda i: (i, 0)), _resident((FF, D)), td, td, _const((1, D)), _const((1, D))],
        out_specs=[td, td, _const((8, D))],
        out_shape=[jax.ShapeDtypeStruct((s, D), F32), jax.ShapeDtypeStruct((s, D), BF16),
                   jax.ShapeDtypeStruct((8, D), F32)],
        compiler_params=_cp(("arbitrary",)),
    )(a, w_down, h1, target, _row(ln2_g), _row(ln2_b))


def _conv_gelu_bwd(dz2b, w_down_t, up, g, a1, cwb, tm=512, tn=256):
    s = dz2b.shape[0]
    n_i = s // tm
    n_t = (FF // tn) * n_i

    def body(dz_ref, w_ref, up_ref, g_ref, a1_ref, c_ref, dup_ref, dc_ref, pend_a, pend_b, carry):
        t = pl.program_id(0)
        first = jnp.maximum(t - 1, 0) % n_i == 0

        @pl.when(t == 0)
        def _():
            pend_b[...] = jnp.zeros_like(pend_b)

        @pl.when(first)
        def _():
            carry[...] = jnp.zeros_like(carry)
            dc_ref[...] = jnp.zeros_like(dc_ref)

        def step(dst, src):
            n_c = tm // CONV_CHUNK

            def chunk(cc, after):
                rows = pl.ds((n_c - 1 - cc) * CONV_CHUNK, CONV_CHUNK)
                da = src[rows, :]
                dus = (da * a1_ref[rows, :].astype(F32), da * g_ref[rows, :].astype(F32))
                head = []
                for half in (0, 1):
                    du = dus[half]
                    up = up_ref[half, rows, :]
                    l1, l2 = _shift_up(du, after[half])
                    dup = (du * c_ref[2, half:half + 1, :] + l1 * c_ref[1, half:half + 1, :]
                           + l2 * c_ref[0, half:half + 1, :])
                    dup_ref[half, rows, :] = dup.astype(BF16)
                    dc_ref[0, half:half + 1, :] += jnp.sum(l2 * up, axis=0, keepdims=True)
                    dc_ref[1, half:half + 1, :] += jnp.sum(l1 * up, axis=0, keepdims=True)
                    dc_ref[2, half:half + 1, :] += jnp.sum(du * up, axis=0, keepdims=True)
                    dc_ref[3, half:half + 1, :] += jnp.sum(du, axis=0, keepdims=True)
                    head.append(du[:8])
                return tuple(head)

            head = (carry[0], carry[1])
            n_k = n_c // 2
            tk = D // n_k
            da = None
            for kq in range(n_k):
                ks = slice(kq * tk, (kq + 1) * tk)
                part = _nn(dz_ref[:, ks], w_ref[ks, :])
                da = part if kq == 0 else da + part
                head = chunk(2 * kq, head)
                head = chunk(2 * kq + 1, head)
            for half in (0, 1):
                carry[half] = head[half]
            dst[...] = da

        @pl.when(t % 2 == 0)
        def _():
            step(pend_a, pend_b)

        @pl.when(t % 2 == 1)
        def _():
            step(pend_b, pend_a)

    mm = lambda t: jnp.minimum(t, n_t - 1)
    ew = lambda t: jnp.maximum(t - 1, 0)
    row = lambda t: n_i - 1 - t % n_i
    ew_tile = pl.BlockSpec((tm, tn), lambda t: (row(ew(t)), ew(t) // n_i))
    ew_pair = pl.BlockSpec((2, tm, tn), lambda t: (0, row(ew(t)), ew(t) // n_i))
    per_col = pl.BlockSpec((4, 2, tn), lambda t: (0, 0, ew(t) // n_i))
    return pl.pallas_call(
        body, name="conv_gelu_bwd", grid=(n_t + 1,),
        in_specs=[pl.BlockSpec((tm, D), lambda t: (row(mm(t)), 0)),
                  pl.BlockSpec((D, tn), lambda t: (0, mm(t) // n_i)),
                  ew_pair, ew_tile, ew_tile, per_col],
        out_specs=[ew_pair, per_col],
        out_shape=[jax.ShapeDtypeStruct((2, s, FF), BF16), jax.ShapeDtypeStruct((4, 2, FF), F32)],
        scratch_shapes=[pltpu.VMEM((tm, tn), F32), pltpu.VMEM((tm, tn), F32), pltpu.VMEM((2, 8, tn), F32)],
        compiler_params=_cp(("arbitrary",)),
    )(dz2b, w_down_t, up, g, a1, cwb)


def _dh1_ln1_bwd(dz2, dup, w_up_t, z1, ln1_g, tm=256):
    s = dz2.shape[0]

    def body(dz2_ref, dup_ref, w_ref, z1_ref, g_ref, dz1_ref, dz1b_ref, st_ref):
        @pl.when(pl.program_id(0) == 0)
        def _():
            st_ref[...] = jnp.zeros_like(st_ref)

        dh = ALPHA * dz2_ref[...] + _nn(dup_ref[0], w_ref[0]) + _nn(dup_ref[1], w_ref[1])
        zh, rstd = _layer_norm_stats(z1_ref[...])
        st_ref[0:1, :] += jnp.sum(dh * zh, axis=0, keepdims=True)
        st_ref[1:2, :] += jnp.sum(dh, axis=0, keepdims=True)
        dz = _layer_norm_bwd(dh, zh, rstd, g_ref[...])
        dz1_ref[...] = dz
        dz1b_ref[...] = dz.astype(BF16)

    td = pl.BlockSpec((tm, D), lambda i: (i, 0))
    return pl.pallas_call(
        body, name="dh1_ln1_bwd", grid=(s // tm,),
        in_specs=[td, pl.BlockSpec((2, tm, FF), lambda i: (0, i, 0)), _resident((2, FF, D)), td, _const((1, D))],
        out_specs=[td, td, _const((8, D))],
        out_shape=[jax.ShapeDtypeStruct((s, D), F32), jax.ShapeDtypeStruct((s, D), BF16),
                   jax.ShapeDtypeStruct((8, D), F32)],
        compiler_params=_cp(("arbitrary",)),
    )(dz2, dup, w_up_t, z1, _row(ln1_g))


def _dcat_rms_bwd(dz1b, w_o, o_a, o_b, norm_a_g, norm_b_g, tm=256):
    s = dz1b.shape[0]

    def body(dz_ref, w_ref, oa_ref, ob_ref, ga_ref, gb_ref, da_ref, db_ref, st_ref):
        @pl.when(pl.program_id(0) == 0)
        def _():
            st_ref[...] = jnp.zeros_like(st_ref)

        dcat = _nt(dz_ref[...], w_ref[...])
        for k, (o_ref, g_ref, d_ref) in enumerate(((oa_ref, ga_ref, da_ref), (ob_ref, gb_ref, db_ref))):
            o = o_ref[...]
            dn = dcat[:, 512 * k:512 * (k + 1)]
            rr = _rms(o)
            oh = o * rr
            st_ref[k:k + 1, :] += jnp.sum(dn * oh, axis=0, keepdims=True)
            doh = dn * g_ref[...]
            d_ref[...] = rr * (doh - oh * jnp.mean(doh * oh, axis=-1, keepdims=True))

    t512 = pl.BlockSpec((tm, 512), lambda i: (i, 0))
    return pl.pallas_call(
        body, name="dcat_rms_bwd", grid=(s // tm,),
        in_specs=[pl.BlockSpec((tm, D), lambda i: (i, 0)), _resident((D, D)), t512, t512,
                  _const((1, 512)), _const((1, 512))],
        out_specs=[t512, t512, _const((8, 512))],
        out_shape=[jax.ShapeDtypeStruct((s, 512), F32), jax.ShapeDtypeStruct((s, 512), F32),
                   jax.ShapeDtypeStruct((8, 512), F32)],
        compiler_params=_cp(("arbitrary",)),
    )(dz1b, w_o, o_a, o_b, _row(norm_a_g), _row(norm_b_g))


def _dproj_combine(dqa, dka, dva, dqkv_b, tm=256):
    s = dqa.shape[0]

    def body(qa, ka, va, q1, k1, v1, q2, k2, v2, q3, k3, v3, o_ref):
        o_ref[:, 0:512] = qa[...].astype(BF16)
        o_ref[:, 512:640] = ka[...].astype(BF16)
        o_ref[:, 640:768] = va[...].astype(BF16)
        o_ref[:, 768:1280] = (q1[...] + q2[...] + q3[...]).astype(BF16)
        o_ref[:, 1280:1792] = (k1[...] + k2[...] + k3[...]).astype(BF16)
        o_ref[:, 1792:2304] = (v1[...] + v2[...] + v3[...]).astype(BF16)

    t512 = pl.BlockSpec((tm, 512), lambda i: (i, 0))
    t128 = pl.BlockSpec((tm, 128), lambda i: (i, 0))
    flat = [a for trio in dqkv_b for a in trio]
    return pl.pallas_call(
        body, name="dproj_combine", grid=(s // tm,),
        in_specs=[t512, t128, t128] + [t512] * 9,
        out_specs=pl.BlockSpec((tm, WIN), lambda i: (i, 0)),
        out_shape=jax.ShapeDtypeStruct((s, WIN), BF16),
        compiler_params=_cp(("parallel",)),
    )(dqa, dka, dva, *flat)


def _grad_x(dz1, dproj, w_in_t, zero, tm=256):
    s = dz1.shape[0]

    def body(dz_ref, dp_ref, w_ref, z_ref, o_ref):
        o_ref[...] = ALPHA * dz_ref[...] + _nn(dp_ref[...], w_ref[...]) + z_ref[0:1, 0:1]

    td = pl.BlockSpec((tm, D), lambda i: (i, 0))
    return pl.pallas_call(
        body, name="grad_x", grid=(s // tm,),
        in_specs=[td, pl.BlockSpec((tm, WIN), lambda i: (i, 0)), _resident((WIN, D)), _const((8, 128))],
        out_specs=td, out_shape=jax.ShapeDtypeStruct((s, D), F32),
        compiler_params=_cp(("parallel",)),
    )(dz1, dproj, w_in_t, zero)


def _place():
    return lax.axis_index("x"), lax.axis_index("y"), lax.axis_index("c")


def _other_chips(x, y):
    return [(1 - x, y), (x, 1 - y), (1 - x, 1 - y)]


def _hbm(a):
    return pltpu.with_memory_space_constraint(a, pltpu.HBM)


def _gather_w_in(shard, conv_w):
    rows_k = shard.shape[0]
    half = rows_k // 2

    def body(src, conv_src, out, conv_out, send_sems, recv_sems):
        x, y, c = _place()
        b = 2 * x + y
        sibling = (x, y, 1 - c)
        chips = _other_chips(x, y)

        def copy(idx, chip_b, core, to, first_hop=False):
            rows = out.at[pl.ds(pl.multiple_of(chip_b * rows_k + core * half, 16), half)]
            s_ref = src.at[pl.ds(pl.multiple_of(core * half, 16), half)] if first_hop else rows
            return pltpu.make_async_remote_copy(src_ref=s_ref, dst_ref=rows, send_sem=send_sems.at[idx],
                                                recv_sem=recv_sems.at[idx], device_id=to, device_id_type=MESH)

        def own_copy():
            return pltpu.make_async_remote_copy(
                src_ref=src, dst_ref=out.at[pl.ds(pl.multiple_of(b * rows_k, 16), rows_k)], send_sem=send_sems.at[6],
                recv_sem=recv_sems.at[6], device_id=sibling, device_id_type=MESH)

        def conv_copy(idx, chip_b, to):
            return pltpu.make_async_remote_copy(src_ref=conv_src, dst_ref=conv_out.at[chip_b],
                                                send_sem=send_sems.at[7 + idx], recv_sem=recv_sems.at[7 + idx],
                                                device_id=to, device_id_type=MESH)

        started = [own_copy(), conv_copy(3, b, sibling)]
        for jn, chip in enumerate(chips):
            started += [copy(jn, b, c, (chip[0], chip[1], c), first_hop=True), conv_copy(jn, b, (chip[0], chip[1], c))]
        for cp in started:
            cp.start()
        for jn, chip in enumerate(chips):
            cb = 2 * chip[0] + chip[1]
            copy(jn, cb, c, (chip[0], chip[1], c)).wait_recv()
            cp = copy(3 + jn, cb, c, sibling)
            cp.start()
            started.append(cp)
        for jn, chip in enumerate(chips):
            cb = 2 * chip[0] + chip[1]
            copy(3 + jn, cb, 1 - c, sibling).wait_recv()
            conv_copy(jn, cb, (chip[0], chip[1], c)).wait_recv()
        own_copy().wait_recv()
        conv_copy(3, b, sibling).wait_recv()
        for cp in started:
            cp.wait_send()

    return pl.pallas_call(
        body, name="gather_w_in",
        in_specs=[ANY, ANY], out_specs=[ANY, ANY],
        out_shape=[jax.ShapeDtypeStruct((N_CHIPS * rows_k, D), BF16), jax.ShapeDtypeStruct((N_CHIPS,) + conv_w.shape, F32)],
        scratch_shapes=[pltpu.SemaphoreType.DMA((11,)), pltpu.SemaphoreType.DMA((11,))],
        compiler_params=pltpu.CompilerParams(has_side_effects=True),
    )(shard, conv_w)


def _weight_copies(shard, land, send_sems, recv_sems, arrivals):
    x, y, c = _place()
    rows_k = shard.shape[0]
    peers = [(px, py, c) for px, py in _other_chips(x, y)] + [(x, y, 1 - c)]
    cps = []
    for jn, peer in enumerate(peers):
        at = 2 * peer[0] + peer[1] if arrivals else 2 * x + y
        cps.append(pltpu.make_async_remote_copy(
            src_ref=shard, dst_ref=land.at[pl.ds(pl.multiple_of(at * rows_k, 16), rows_k)],
            send_sem=send_sems.at[jn], recv_sem=recv_sems.at[jn], device_id=peer, device_id_type=MESH))
    return cps


def _weights_start(shards, after):
    n = len(shards)
    lands = [lax.empty((N_CHIPS * sh.shape[0], D), BF16) for sh in shards]

    def body(*refs):
        src, land = refs[:n], refs[n:2 * n]
        send_sems, recv_sems = refs[2 * n + 1:3 * n + 1], refs[3 * n + 1:4 * n + 1]
        for k in range(n):
            for send in _weight_copies(src[k], land[k], send_sems[k], recv_sems[k], False):
                send.start()
        refs[-1][...] = jnp.zeros_like(refs[-1])

    res = pl.pallas_call(
        body, name="weights_start",
        in_specs=[HBM] * (2 * n) + [ANY], out_specs=[SEM] * (2 * n) + [HBM] * (2 * n) + [VMEM],
        out_shape=[pltpu.SemaphoreType.DMA((4,))] * (2 * n)
        + [pltpu.HBM(a.shape, a.dtype) for a in (*shards, *lands)] + [jax.ShapeDtypeStruct((8, 128), F32)],
        input_output_aliases={i: i + 2 * n for i in range(2 * n)},
        compiler_params=pltpu.CompilerParams(has_side_effects=DATAFLOW),
    )(*[_hbm(a) for a in (*shards, *lands)], after)
    return [(res[k], res[n + k], res[2 * n + k], res[3 * n + k]) for k in range(n)], res[-1]


def _weights_wait(started, after, name):
    send_sems, recv_sems, shard, land = started

    def body(s_ref, l_ref, send_ref, recv_ref, after_ref, s_out, l_out):
        for cp in _weight_copies(s_ref, l_ref, send_ref, recv_ref, True):
            cp.wait_send()
            cp.wait_recv()

    return pl.pallas_call(
        body, name=name,
        in_specs=[HBM, HBM, SEM, SEM, ANY], out_specs=[HBM, HBM],
        out_shape=[pltpu.HBM(shard.shape, shard.dtype), pltpu.HBM(land.shape, land.dtype)],
        input_output_aliases={0: 0, 1: 1},
        compiler_params=pltpu.CompilerParams(has_side_effects=DATAFLOW),
    )(shard, land, send_sems, recv_sems, after)[1]


def _grad_copies(g_ref, land_ref, send_sems, recv_sems):
    x, y, c = _place()
    cps = []
    for d in range(1, 8):
        px, py, pc = x ^ (d >> 2), y ^ ((d >> 1) & 1), c ^ (d & 1)
        cps.append(pltpu.make_async_remote_copy(
            src_ref=g_ref.at[2 * px + py, pc], dst_ref=land_ref.at[d - 1], send_sem=send_sems.at[d - 1],
            recv_sem=recv_sems.at[d - 1], device_id=(px, py, pc), device_id_type=MESH))
    return cps


def _grads_start(grads_b, name):
    n = len(grads_b)
    lands = [lax.empty((7, g.shape[2], D), BF16) for g in grads_b]

    def body(*refs):
        g, land = refs[:n], refs[n:2 * n]
        send_sems, recv_sems = refs[2 * n:3 * n], refs[3 * n:4 * n]
        for k in range(n):
            for cp in _grad_copies(g[k], land[k], send_sems[k], recv_sems[k]):
                cp.start()
        refs[-1][...] = jnp.zeros_like(refs[-1])

    res = pl.pallas_call(
        body, name=name,
        in_specs=[HBM] * (2 * n), out_specs=[SEM] * (2 * n) + [HBM] * (2 * n) + [VMEM],
        out_shape=[pltpu.SemaphoreType.DMA((7,))] * (2 * n)
        + [pltpu.HBM(a.shape, a.dtype) for a in (*grads_b, *lands)] + [jax.ShapeDtypeStruct((8, 128), F32)],
        input_output_aliases={i: i + 2 * n for i in range(2 * n)},
        compiler_params=pltpu.CompilerParams(has_side_effects=DATAFLOW),
    )(*[_hbm(a) for a in (*grads_b, *lands)])
    return [(res[k], res[n + k], res[2 * n + k], res[3 * n + k]) for k in range(n)], res[-1]


def _grads_wait(started, after, name):
    n = len(started)

    def body(*refs):
        g, land = refs[:n], refs[n:2 * n]
        send_sems, recv_sems = refs[2 * n:3 * n], refs[3 * n:4 * n]
        for k in range(n):
            for cp in _grad_copies(g[k], land[k], send_sems[k], recv_sems[k]):
                cp.wait_send()
                cp.wait_recv()

    gs = [st[2] for st in started]
    lands = [st[3] for st in started]
    res = pl.pallas_call(
        body, name=name,
        in_specs=[HBM] * (2 * n) + [SEM] * (2 * n) + [ANY], out_specs=[HBM] * (2 * n),
        out_shape=[pltpu.HBM(a.shape, a.dtype) for a in (*gs, *lands)],
        input_output_aliases={i: i for i in range(2 * n)},
        compiler_params=pltpu.CompilerParams(has_side_effects=DATAFLOW),
    )(*gs, *lands, *[st[0] for st in started], *[st[1] for st in started], after)
    return res[n:]


def _sum_partials(grad4, got, cb, name, tr):
    h = grad4.shape[2]
    per_half = h // tr

    def body(cb_ref, g_ref, o_ref, out_ref):
        acc = g_ref[...]
        for j in range(7):
            acc = acc + o_ref[j].astype(F32)
        out_ref[...] = acc

    return pl.pallas_call(
        body, name=name,
        grid_spec=pltpu.PrefetchScalarGridSpec(
            num_scalar_prefetch=1, grid=(per_half,),
            in_specs=[pl.BlockSpec((None, None, tr, D), lambda i, cb_ref: (cb_ref[1], cb_ref[0], i, 0)),
                      pl.BlockSpec((7, tr, D), lambda i, cb_ref: (0, i, 0))],
            out_specs=pl.BlockSpec((tr, D), lambda i, cb_ref: (cb_ref[0] * per_half + i, 0))),
        out_shape=jax.ShapeDtypeStruct((2 * h, D), F32),
        compiler_params=_cp(("arbitrary",)),
    )(cb, grad4, got)


def _swap_halves(shards, name):
    n = len(shards)

    def body(*refs):
        out, send_sems, recv_sems = refs[n:2 * n], refs[2 * n], refs[2 * n + 1]
        x, y, c = _place()
        cps = []
        for k in range(n):
            h = shards[k].shape[0] // 2
            mine = out[k].at[pl.ds(pl.multiple_of(c * h, 8), h)]
            cp = pltpu.make_async_remote_copy(src_ref=mine, dst_ref=mine, send_sem=send_sems.at[k],
                                              recv_sem=recv_sems.at[k], device_id=(x, y, 1 - c), device_id_type=MESH)
            cp.start()
            cps.append(cp)
        for cp in cps:
            cp.wait()

    return pl.pallas_call(
        body, name=name,
        in_specs=[ANY] * n, out_specs=[ANY] * n,
        out_shape=[jax.ShapeDtypeStruct(sh.shape, F32) for sh in shards],
        input_output_aliases={k: k for k in range(n)},
        scratch_shapes=[pltpu.SemaphoreType.DMA((n,)), pltpu.SemaphoreType.DMA((n,))],
        compiler_params=pltpu.CompilerParams(has_side_effects=True),
    )(*shards)


def _share_halves(shards, small):
    n = len(shards)
    rows = small.shape[0]

    def body(*refs):
        small_ref = refs[n]
        out, total_ref = refs[n + 1:2 * n + 1], refs[2 * n + 1]
        all_ref, send_sems, recv_sems, ssend, srecv = refs[2 * n + 2:]
        x, y, c = _place()
        me = 4 * x + 2 * y + c
        cps = []
        for k in range(n):
            h = shards[k].shape[0] // 2
            mine = out[k].at[pl.ds(pl.multiple_of(c * h, 8), h)]
            cp = pltpu.make_async_remote_copy(src_ref=mine, dst_ref=mine, send_sem=send_sems.at[k],
                                              recv_sem=recv_sems.at[k], device_id=(x, y, 1 - c), device_id_type=MESH)
            cp.start()
            cps.append(cp)
        all_ref[me] = small_ref[...]
        peers = []
        for d in range(1, 8):
            px, py, pc = x ^ (d >> 2), y ^ ((d >> 1) & 1), c ^ (d & 1)
            cp = pltpu.make_async_remote_copy(src_ref=small_ref, dst_ref=all_ref.at[me],
                                              send_sem=ssend.at[d - 1], recv_sem=srecv.at[d - 1],
                                              device_id=(px, py, pc), device_id_type=MESH)
            cp.start()
            peers.append(cp)
        for cp in peers:
            cp.wait()
        acc = all_ref[0]
        for d in range(1, 8):
            acc = acc + all_ref[d]
        total_ref[...] = acc
        for cp in cps:
            cp.wait()

    return pl.pallas_call(
        body, name="share_halves",
        in_specs=[ANY] * n + [VMEM], out_specs=[ANY] * n + [VMEM],
        out_shape=[jax.ShapeDtypeStruct(sh.shape, F32) for sh in shards] + [jax.ShapeDtypeStruct((rows, D), F32)],
        input_output_aliases={k: k for k in range(n)},
        scratch_shapes=[pltpu.VMEM((8, rows, D), F32), pltpu.SemaphoreType.DMA((n,)), pltpu.SemaphoreType.DMA((n,)),
                        pltpu.SemaphoreType.DMA((7,)), pltpu.SemaphoreType.DMA((7,))],
        compiler_params=pltpu.CompilerParams(has_side_effects=True),
    )(*shards, small)


def _adamw(w, g, m, v, name, tr):
    rows, cols = w.shape

    def body(w_ref, g_ref, m_ref, v_ref, d_ref, nm_ref, nv_ref):
        g_ = g_ref[...]
        nm = ADAM_B1 * m_ref[...] + (1.0 - ADAM_B1) * g_
        nv = ADAM_B2 * v_ref[...] + (1.0 - ADAM_B2) * (g_ * g_)
        m_hat = nm / (1.0 - ADAM_B1 ** ADAM_STEP)
        v_hat = nv / (1.0 - ADAM_B2 ** ADAM_STEP)
        d_ref[...] = -ADAM_LR * (m_hat / (jnp.sqrt(v_hat) + ADAM_EPS) + ADAM_WD * w_ref[...])
        nm_ref[...] = nm
        nv_ref[...] = nv

    spec = pl.BlockSpec((tr, cols), lambda i: (i, 0))
    return pl.pallas_call(
        body, name=name, grid=(rows // tr,),
        in_specs=[spec] * 4, out_specs=[spec] * 3,
        out_shape=[jax.ShapeDtypeStruct((rows, cols), F32)] * 3,
        compiler_params=_cp(("parallel",)),
    )(w, g, m, v)


def _local_step(x, target, w_in_t, late_weights, norm_a_g, norm_b_g, sinks_a, ln1_g, ln1_b,
                conv_w, conv_b, ln2_g, ln2_b, slopes, on_grad):
    cwb = jnp.concatenate([conv_w, conv_b[None]], axis=0).reshape(4, 2, FF)

    proj = _proj(x, w_in_t, "proj")
    o_a, lse_a = _attn_a_fwd(proj, sinks_a)
    fwd_b = [_attn_b_fwd(proj, slopes, r) for r in B_DILATIONS]
    w_o = late_weights(1, fwd_b[-1][1])
    o_b, lse_b, cat, z1, h1, h1b = _mix_ln1(x, o_a, [f[0] for f in fwd_b], [f[1] for f in fwd_b],
                                           norm_a_g, norm_b_g, w_o, ln1_g, ln1_b)
    w_up_t = late_weights(2, h1b)
    w_up3 = w_up_t.reshape(2, FF, D)
    up, a, gate, a1 = _up_conv_gelu(h1b, w_up_t.T, cwb)
    w_down = late_weights(3, a)
    dz2, dz2b, st2 = _down_ln2_loss(a, w_down, h1, target, ln2_g, ln2_b)

    on_grad(3, *_grad_w(a, dz2b, "grad_w_down", tm=FF // 2))
    dup, dconv = _conv_gelu_bwd(dz2b, w_down.T, up, gate, a1, cwb)
    on_grad(2, *_grad_w(dup, h1b, "grad_w_up", tm=FF // 2, lhs_halves=True))
    dz1, dz1b, st1 = _dh1_ln1_bwd(dz2, dup, w_up3, z1, ln1_g)
    tok = on_grad(1, *_grad_w(cat, dz1b, "grad_w_o", tm=512))
    d_oa, d_ob, st_n = _dcat_rms_bwd(dz1b, w_o, o_a, o_b, norm_a_g + tok[0, 0], norm_b_g)
    dqa, dka, dva, dsink = _attn_a_bwd(proj, sinks_a, d_oa, o_a, lse_a)
    bwd_b = [_attn_b_bwd(proj, slopes, d_ob, o_b, lse_b, r) for r in B_DILATIONS]
    dproj = _dproj_combine(dqa, dka, dva, bwd_b)
    tok = on_grad(0, *_grad_w(dproj, x, "grad_w_in", tm=WA))
    gx = _grad_x(dz1, dproj, w_in_t, tok)

    dconv = dconv.reshape(4, 2 * FF)
    small = dict(loss=st2[2, 0:1], norm_a_g=st_n[0], norm_b_g=st_n[1], sinks_a=dsink[:, 0],
                 ln1_g=st1[0], ln1_b=st1[1], conv_w=dconv[0:3].reshape(-1), conv_b=dconv[3],
                 ln2_g=st2[0], ln2_b=st2[1])
    return gx, small


SMALL_ORDER = ("loss", "norm_a_g", "norm_b_g", "sinks_a", "ln1_g", "ln1_b", "conv_b", "ln2_g", "ln2_b", "conv_w")
SMALL_SIZES = dict(loss=1, norm_a_g=512, norm_b_g=512, sinks_a=8, ln1_g=D, ln1_b=D, conv_b=2 * FF, ln2_g=D, ln2_b=D,
                   conv_w=3 * 2 * FF)


def _pack(parts, rows):
    flat = jnp.concatenate([parts[k].reshape(-1).astype(F32) for k in parts])
    return jnp.pad(flat, (0, rows * D - flat.shape[0])).reshape(rows, D)


def _unpack(buf, names, sizes):
    flat = buf.reshape(-1)
    out, at = {}, 0
    for k in names:
        out[k] = flat[at:at + sizes[k]]
        at += sizes[k]
    return out


def kernel(x, w_in, norm_a_g, norm_b_g, sinks_a, w_o, ln1_g, ln1_b, w_up, conv_w, conv_b, w_down, ln2_g, ln2_b, loss_target, m_w_in, m_norm_a_g, m_norm_b_g, m_sinks_a, m_w_o, m_ln1_g, m_ln1_b, m_w_up, m_conv_w, m_conv_b, m_w_down, m_ln2_g, m_ln2_b, v_w_in, v_norm_a_g, v_norm_b_g, v_sinks_a, v_w_o, v_ln1_g, v_ln1_b, v_w_up, v_conv_w, v_conv_b, v_w_down, v_ln2_g, v_ln2_b):
    xi, yi, ci = _place()
    chip = (2 * xi + yi).astype(I32)
    core = ci.astype(I32)

    w_in_rows, m_w_in_rows, v_w_in_rows = w_in.T, m_w_in.T, v_w_in.T
    shards = (w_in_rows.astype(BF16), w_o.astype(BF16), w_up.T.astype(BF16), w_down.astype(BF16))
    w_in_t, conv_w4 = _gather_w_in(shards[0], conv_w)
    conv_w_f = conv_w4.transpose(1, 0, 2).reshape(3, 2 * FF)
    w_started, w_tok = _weights_start(shards[1:], conv_w4)
    slopes = jnp.asarray(SLOPES, F32) + w_tok[0, 0]

    halves_rows = [r // 2 for r in SHARD_ROWS]
    grads4, grads_b4, started = [None] * 4, [None] * 4, [None] * 4

    def on_grad(k, g, g_b):
        grads4[k] = g.reshape(N_CHIPS, 2, halves_rows[k], D)
        grads_b4[k] = g_b.reshape(N_CHIPS, 2, halves_rows[k], D)
        if k > 1:
            return None
        group = (1, 2, 3) if k == 1 else (0,)
        sts, tok = _grads_start([grads_b4[i] for i in group], f"grads_start_{k}")
        for i, st in zip(group, sts):
            started[i] = st
        return tok

    gx, small = _local_step(
        x[0], loss_target[0], w_in_t, lambda k, after: _weights_wait(w_started[k - 1], after, f"weights_wait_{k}"),
        norm_a_g, norm_b_g, sinks_a, ln1_g, ln1_b, conv_w_f, conv_b, ln2_g, ln2_b, slopes, on_grad)

    tiles = (96, 128, 352, 176)
    core_chip = jnp.stack([core, chip])
    got = _grads_wait(started[1:], gx, "grads_wait_1")
    halves = [_sum_partials(grads4[k], got[k - 1], core_chip, f"sum_partials_{k}", tiles[k]) for k in (1, 2, 3)]
    g_w_o, g_w_up_rows, g_w_down = _swap_halves(halves, "swap_halves")
    g_w_up = g_w_up_rows.T
    delta, new_m, new_v = {}, {}, {}
    for k, g, tr in (("w_o", g_w_o, 128), ("w_up", g_w_up, 256), ("w_down", g_w_down, 176)):
        delta[k], new_m[k], new_v[k] = _adamw(dict(w_o=w_o, w_up=w_up, w_down=w_down)[k], g,
                                              dict(w_o=m_w_o, w_up=m_w_up, w_down=m_w_down)[k],
                                              dict(w_o=v_w_o, w_up=v_w_up, w_down=v_w_down)[k], f"adamw_{k}", tr)

    got = _grads_wait(started[:1], delta["w_up"], "grads_wait_0")
    half_in = _sum_partials(grads4[0], got[0], core_chip, "sum_partials_0", tiles[0])
    small_rows = 32
    g_w_in_rows, totals = _share_halves([half_in], _pack({k: small[k] for k in SMALL_ORDER}, small_rows))
    tot = _unpack(totals, SMALL_ORDER, SMALL_SIZES)
    loss = tot["loss"][0]
    cols = 2 * FF // N_CHIPS
    g_conv_w = lax.dynamic_slice(tot["conv_w"].reshape(3, 2 * FF), (0, chip * cols), (3, cols))
    g_small = dict(norm_a_g=tot["norm_a_g"], norm_b_g=tot["norm_b_g"], sinks_a=tot["sinks_a"], ln1_g=tot["ln1_g"],
                   ln1_b=tot["ln1_b"], conv_w=g_conv_w, conv_b=tot["conv_b"], ln2_g=tot["ln2_g"], ln2_b=tot["ln2_b"])

    weights = dict(w_in=w_in, norm_a_g=norm_a_g, norm_b_g=norm_b_g, sinks_a=sinks_a, w_o=w_o, ln1_g=ln1_g, ln1_b=ln1_b,
                   w_up=w_up, conv_w=conv_w, conv_b=conv_b, w_down=w_down, ln2_g=ln2_g, ln2_b=ln2_b)
    ms = dict(w_in=m_w_in, norm_a_g=m_norm_a_g, norm_b_g=m_norm_b_g, sinks_a=m_sinks_a, w_o=m_w_o, ln1_g=m_ln1_g,
              ln1_b=m_ln1_b, w_up=m_w_up, conv_w=m_conv_w, conv_b=m_conv_b, w_down=m_w_down, ln2_g=m_ln2_g, ln2_b=m_ln2_b)
    vs = dict(w_in=v_w_in, norm_a_g=v_norm_a_g, norm_b_g=v_norm_b_g, sinks_a=v_sinks_a, w_o=v_w_o, ln1_g=v_ln1_g,
              ln1_b=v_ln1_b, w_up=v_w_up, conv_w=v_conv_w, conv_b=v_conv_b, w_down=v_w_down, ln2_g=v_ln2_g, ln2_b=v_ln2_b)
    order = list(weights)
    grad = dict(g_small, w_in=g_w_in_rows.T, w_o=g_w_o, w_up=g_w_up, w_down=g_w_down)

    delta["w_in"], new_m["w_in"], new_v["w_in"] = [
        a.T for a in _adamw(w_in_rows, g_w_in_rows, m_w_in_rows, v_w_in_rows, "adamw_w_in", 144)]
    small_names = [k for k in order if k not in delta]
    sizes = {k: weights[k].size for k in small_names}
    rows = 16
    packed = [_pack({k: src[k] for k in small_names}, rows) for src in (weights, grad, ms, vs)]
    for res, buf in zip((delta, new_m, new_v), _adamw(*packed, "adamw_small", rows)):
        for k, val in _unpack(buf, small_names, sizes).items():
            res[k] = val.reshape(weights[k].shape)

    return (loss, gx[None], *[grad[k] for k in order], *[delta[k] for k in order],
            *[new_m[k] for k in order], *[new_v[k] for k in order])
```

```python
import functools
import math

import jax
import jax.numpy as jnp
from jax import lax
from jax.experimental import pallas as pl
from jax.experimental.pallas import tpu as pltpu

F32, BF16, I32 = jnp.float32, jnp.bfloat16, jnp.int32

D = 1024
FF = 2816
HD = 64
NH = 8
WA, WB = 768, 1536
WIN = WA + WB
BLK = 128
ALPHA = 2.0 ** 0.25
LN_EPS, RMS_EPS = 1e-5, 1e-6
SCALE = 1.0 / math.sqrt(HD)
A_MAX_DIST, B_MAX_DIST = 127, 128
B_DILATIONS = (1, 4, 16)
SLOPES = tuple(2.0 ** (-(i + 1)) for i in range(NH))
SHARD_ROWS = (WIN // 4, D // 4, 2 * FF // 4, FF // 4)
N_CHIPS = 4
ADAM_LR, ADAM_B1, ADAM_B2, ADAM_EPS, ADAM_WD, ADAM_STEP = 0.001, 0.9, 0.999, 1e-08, 0.01, 10
MESH = pl.DeviceIdType.MESH
ANY = pl.BlockSpec(memory_space=pl.ANY)
SMEM = pl.BlockSpec(memory_space=pltpu.SMEM)
VMEM = pl.BlockSpec(memory_space=pltpu.VMEM)
HBM = pl.BlockSpec(memory_space=pltpu.HBM)
SEM = pl.BlockSpec(memory_space=pltpu.SEMAPHORE)
DATAFLOW = pltpu.SideEffectType.DATAFLOW_SIDE_EFFECTING


def _cp(sem, mb=48):
    return pltpu.CompilerParams(dimension_semantics=sem, vmem_limit_bytes=mb << 20)


def _nn(a, b):
    return lax.dot_general(a, b, (((1,), (0,)), ((), ())), preferred_element_type=F32)


def _nt(a, b):
    return lax.dot_general(a, b, (((1,), (1,)), ((), ())), preferred_element_type=F32)


def _tn(a, b):
    return lax.dot_general(a, b, (((0,), (0,)), ((), ())), preferred_element_type=F32)


def _resident(shape):
    n = len(shape)
    return pl.BlockSpec(shape, lambda *_: (0,) * n, pipeline_mode=pl.Buffered(1))


def _const(shape):
    n = len(shape)
    return pl.BlockSpec(shape, lambda *_: (0,) * n)


def _proj(x, w_t, name, tm=512):
    s = x.shape[0]
    n = w_t.shape[0]

    def body(x_ref, w_ref, o_ref, xb_ref):
        xb = x_ref[...].astype(BF16)
        xb_ref[...] = xb
        o_ref[...] = _nt(xb, w_ref[...])

    return pl.pallas_call(
        body, name=name, grid=(s // tm,),
        in_specs=[pl.BlockSpec((tm, D), lambda i: (i, 0)), _resident((n, D))],
        out_specs=[pl.BlockSpec((tm, n), lambda i: (i, 0)), pl.BlockSpec((tm, D), lambda i: (i, 0))],
        out_shape=[jax.ShapeDtypeStruct((s, n), F32), jax.ShapeDtypeStruct((s, D), BF16)],
        compiler_params=_cp(("parallel",)),
    )(x, w_t)


def _grad_w(lhs, rhs, name, tm, tk=512, lhs_halves=False):
    s = rhs.shape[0]
    if lhs_halves:
        per_half = lhs.shape[2] // tm
        n = 2 * lhs.shape[2]
        lhs_spec = pl.BlockSpec((None, tk, tm), lambda i, k: (i // per_half, k, i % per_half))
    else:
        n = lhs.shape[1]
        lhs_spec = pl.BlockSpec((tk, tm), lambda i, k: (k, i))
    nk = s // tk

    def body(l_ref, r_ref, o_ref, ob_ref):
        k = pl.program_id(1)

        @pl.when(k == 0)
        def _():
            o_ref[...] = jnp.zeros_like(o_ref)

        o_ref[...] += _tn(l_ref[...].astype(BF16), r_ref[...].astype(BF16))

        @pl.when(k == nk - 1)
        def _():
            ob_ref[...] = o_ref[...].astype(BF16)

    return pl.pallas_call(
        body, name=name, grid=(n // tm, nk),
        in_specs=[lhs_spec, pl.BlockSpec((tk, D), lambda i, k: (k, 0))],
        out_specs=[pl.BlockSpec((tm, D), lambda i, k: (i, 0))] * 2,
        out_shape=[jax.ShapeDtypeStruct((n, D), F32), jax.ShapeDtypeStruct((n, D), BF16)],
        compiler_params=_cp(("parallel", "arbitrary")),
    )(lhs, rhs)


def _band_base(max_dist, dist_unit, first):
    row = lax.broadcasted_iota(I32, (BLK, 2 * BLK), 0)
    col = lax.broadcasted_iota(I32, (BLK, 2 * BLK), 1)
    dist = BLK + row - col
    ok = (dist >= 0) & (dist <= max_dist)
    if first:
        ok = ok & (col >= BLK)
    return jnp.where(ok, dist.astype(F32) * (-float(dist_unit)), -jnp.inf)


def _half_mask(shape, e):
    lane = lax.broadcasted_iota(I32, shape, 1)
    return (lane < HD) if e == 0 else (lane >= HD)


def _to_half(x, e, g):
    if g != e:
        x = pltpu.roll(x, HD, 1)
    return jnp.where(_half_mask(x.shape, g), x, 0.0)


def _stack_heads(scalars, tile):
    return jnp.concatenate([scalars[0] * tile, scalars[1] * tile], axis=0)


def _pair_fwd(q2, kb, vb, base, slopes, kv_heads, sinks):
    lo = _half_mask((BLK, 2 * HD), 0)
    if sinks is not None:
        o2 = lse2 = None
        for e in (0, 1):
            g = kv_heads[e]
            qv = (_to_half(q2, e, g) * SCALE).astype(BF16)
            s = _nt(qv, kb) + slopes[e] * base
            m = jnp.maximum(jnp.max(s, axis=1, keepdims=True), sinks[e])
            p = jnp.exp(s - m)
            l = jnp.sum(p, axis=1, keepdims=True) + jnp.exp(sinks[e] - m)
            oh = _nn(p.astype(BF16), vb) / l
            if g != e:
                oh = pltpu.roll(oh, HD, 1)
            lse = jnp.broadcast_to(m + jnp.log(l), (BLK, 2 * HD))
            o2 = oh if e == 0 else jnp.where(lo, o2, oh)
            lse2 = lse if e == 0 else jnp.where(lo, lse2, lse)
        return o2, lse2
    qs = jnp.concatenate([_to_half(q2, e, kv_heads[e]) * SCALE for e in (0, 1)], axis=0).astype(BF16)
    s = _nt(qs, kb) + (base if slopes is None else _stack_heads(slopes, base))
    m = jnp.max(s, axis=1, keepdims=True)
    p = jnp.exp(s - m)
    l = jnp.sum(p, axis=1, keepdims=True)
    o = _nn(p.astype(BF16), vb) / l
    lse = m + jnp.log(l)
    halves = []
    for e in (0, 1):
        oh = o[e * BLK:(e + 1) * BLK]
        halves.append(pltpu.roll(oh, HD, 1) if kv_heads[e] != e else oh)
    o2 = jnp.where(lo, halves[0], halves[1])
    lse2 = jnp.where(lo, jnp.broadcast_to(lse[:BLK], (BLK, 2 * HD)), jnp.broadcast_to(lse[BLK:], (BLK, 2 * HD)))
    return o2, lse2


def _pair_bwd(q2, kb, vb, do2, o2, lse2, base, slopes, kv_heads, sinks):
    lo = _half_mask((BLK, 2 * HD), 0)
    prod = do2 * o2
    lses, deltas = [], []
    for e in (0, 1):
        hq = _half_mask((BLK, 2 * HD), e)
        lses.append(jnp.max(jnp.where(hq, lse2, -jnp.inf), axis=1, keepdims=True))
        deltas.append(jnp.sum(jnp.where(hq, prod, 0.0), axis=1, keepdims=True))
    lse = jnp.concatenate(lses, axis=0)
    delta = jnp.concatenate(deltas, axis=0)
    qs = jnp.concatenate([_to_half(q2, e, kv_heads[e]) * SCALE for e in (0, 1)], axis=0).astype(BF16)
    dos = jnp.concatenate([_to_half(do2, e, kv_heads[e]) for e in (0, 1)], axis=0).astype(BF16)
    p = jnp.exp(_nt(qs, kb) + (base if slopes is None else _stack_heads(slopes, base)) - lse)
    ds = (p * (_nt(dos, vb) - delta)).astype(BF16)
    dq = _nn(ds, kb) * SCALE
    halves = []
    for e in (0, 1):
        dqh = dq[e * BLK:(e + 1) * BLK]
        halves.append(pltpu.roll(dqh, HD, 1) if kv_heads[e] != e else dqh)
    dq2 = jnp.where(lo, halves[0], halves[1])
    dk2 = _tn(ds, qs)
    dv2 = _tn(p.astype(BF16), dos)
    dsinks = []
    if sinks is not None:
        for e in (0, 1):
            dsinks.append(jnp.sum(-jnp.exp(sinks[e] - lses[e]) * deltas[e], axis=0, keepdims=True))
    return dq2, dk2, dv2, dsinks


def _attn_a_fwd(proj, sinks):
    s = proj.shape[0]
    nb = s // BLK

    def body(sink_ref, q_ref, kp_ref, kc_ref, vp_ref, vc_ref, o_ref, lse_ref):
        n = pl.program_id(0)
        base = jnp.where(n > 0, _band_base(A_MAX_DIST, 1, False), _band_base(A_MAX_DIST, 1, True))
        kb = jnp.concatenate([kp_ref[...], kc_ref[...]], axis=0).astype(BF16)
        vb = jnp.concatenate([vp_ref[...], vc_ref[...]], axis=0).astype(BF16)
        for j in range(NH // 2):
            g = j // 2
            o2, lse2 = _pair_fwd(q_ref[:, 128 * j:128 * (j + 1)], kb, vb, base, (SLOPES[2 * j], SLOPES[2 * j + 1]),
                                 (g, g), (sink_ref[2 * j], sink_ref[2 * j + 1]))
            o_ref[:, 128 * j:128 * (j + 1)] = o2
            lse_ref[:, 128 * j:128 * (j + 1)] = lse2

    prev = lambda n: jnp.maximum(n - 1, 0)
    return pl.pallas_call(
        body, name="attn_a_fwd", grid=(nb,),
        in_specs=[SMEM,
                  pl.BlockSpec((BLK, 512), lambda n: (n, 0)),
                  pl.BlockSpec((BLK, 128), lambda n: (prev(n), 4)), pl.BlockSpec((BLK, 128), lambda n: (n, 4)),
                  pl.BlockSpec((BLK, 128), lambda n: (prev(n), 5)), pl.BlockSpec((BLK, 128), lambda n: (n, 5))],
        out_specs=[pl.BlockSpec((BLK, 512), lambda n: (n, 0))] * 2,
        out_shape=[jax.ShapeDtypeStruct((s, 512), F32)] * 2,
        compiler_params=_cp(("parallel",)),
    )(sinks, proj, proj, proj, proj, proj)


def _attn_a_bwd(proj, sinks, d_o, o, lse):
    s = proj.shape[0]
    nb = s // BLK

    def body(sink_ref, q_ref, kp_ref, kc_ref, vp_ref, vc_ref, do_ref, o_ref, lse_ref,
             dq_ref, dk_ref, dv_ref, dsink_ref, kcar, vcar):
        n = pl.program_id(0)

        @pl.when(n == 0)
        def _():
            kcar[...] = jnp.zeros_like(kcar)
            vcar[...] = jnp.zeros_like(vcar)
            dsink_ref[...] = jnp.zeros_like(dsink_ref)

        @pl.when(n < nb)
        def _():
            base = jnp.where(n > 0, _band_base(A_MAX_DIST, 1, False), _band_base(A_MAX_DIST, 1, True))
            kb = jnp.concatenate([kp_ref[...], kc_ref[...]], axis=0).astype(BF16)
            vb = jnp.concatenate([vp_ref[...], vc_ref[...]], axis=0).astype(BF16)
            dk_win = dv_win = None
            for j in range(NH // 2):
                g = j // 2
                sl = slice(128 * j, 128 * (j + 1))
                dq2, dk2, dv2, dsk = _pair_bwd(q_ref[:, sl], kb, vb, do_ref[:, sl], o_ref[:, sl], lse_ref[:, sl], base,
                                               (SLOPES[2 * j], SLOPES[2 * j + 1]), (g, g),
                                               (sink_ref[2 * j], sink_ref[2 * j + 1]))
                dq_ref[:, sl] = dq2
                dk_win = dk2 if j == 0 else dk_win + dk2
                dv_win = dv2 if j == 0 else dv_win + dv2
                for e in (0, 1):
                    h = 2 * j + e
                    dsink_ref[h:h + 1, :] += jnp.broadcast_to(dsk[e], (1, 128))
            dk_ref[...] = kcar[...] + dk_win[:BLK]
            dv_ref[...] = vcar[...] + dv_win[:BLK]
            kcar[...] = dk_win[BLK:]
            vcar[...] = dv_win[BLK:]

        @pl.when(n == nb)
        def _():
            dk_ref[...] = kcar[...]
            dv_ref[...] = vcar[...]

    cur = lambda n: jnp.minimum(n, nb - 1)
    prev = lambda n: jnp.maximum(cur(n) - 1, 0)
    out_prev = lambda n: jnp.maximum(n - 1, 0)
    return pl.pallas_call(
        body, name="attn_a_bwd", grid=(nb + 1,),
        in_specs=[SMEM,
                  pl.BlockSpec((BLK, 512), lambda n: (cur(n), 0)),
                  pl.BlockSpec((BLK, 128), lambda n: (prev(n), 4)), pl.BlockSpec((BLK, 128), lambda n: (cur(n), 4)),
                  pl.BlockSpec((BLK, 128), lambda n: (prev(n), 5)), pl.BlockSpec((BLK, 128), lambda n: (cur(n), 5)),
                  pl.BlockSpec((BLK, 512), lambda n: (cur(n), 0)),
                  pl.BlockSpec((BLK, 512), lambda n: (cur(n), 0)),
                  pl.BlockSpec((BLK, 512), lambda n: (cur(n), 0))],
        out_specs=[pl.BlockSpec((BLK, 512), lambda n: (cur(n), 0)),
                   pl.BlockSpec((BLK, 128), lambda n: (out_prev(n), 0)),
                   pl.BlockSpec((BLK, 128), lambda n: (out_prev(n), 0)),
                   pl.BlockSpec((NH, 128), lambda n: (0, 0))],
        out_shape=[jax.ShapeDtypeStruct((s, 512), F32), jax.ShapeDtypeStruct((s, 128), F32),
                   jax.ShapeDtypeStruct((s, 128), F32), jax.ShapeDtypeStruct((NH, 128), F32)],
        scratch_shapes=[pltpu.VMEM((BLK, 128), F32), pltpu.VMEM((BLK, 128), F32)],
        compiler_params=_cp(("arbitrary",)),
    )(sinks, proj, proj, proj, proj, proj, d_o, o, lse)


def _stream(rho, i, r):
    start = i * BLK * r + rho
    return pl.ds(start, BLK, stride=r) if r > 1 else pl.ds(start, BLK)


def _for_streams(r, fn):
    if r <= 4:
        for rho in range(r):
            fn(rho)
    else:
        def four(it, carry):
            for u in range(4):
                fn(4 * it + u)
            return carry

        lax.fori_loop(0, r // 4, four, 0)


B_BLOCKS_PER_STEP = {1: 4, 4: 1, 16: 1}


def _attn_b_fwd(proj, slopes, r):
    s = proj.shape[0]
    nq = B_BLOCKS_PER_STEP[r]
    rows = BLK * r * nq
    steps = s // rows
    qc, kc, vc = WA // 128, WA // 128 + 4, WA // 128 + 8

    def body(slope_ref, q_ref, kp_ref, kc_ref, vp_ref, vc_ref, o_ref, lse_ref):
        j = pl.program_id(0)
        sb = pl.program_id(1)
        sl2 = (slope_ref[2 * j], slope_ref[2 * j + 1])
        bias_rest = _stack_heads(sl2, _band_base(B_MAX_DIST, r, False))
        bias_0 = jnp.where(sb > 0, bias_rest, _stack_heads(sl2, _band_base(B_MAX_DIST, r, True)))

        def stream(rho):
            for i in range(nq):
                cur = _stream(rho, i, r)
                k_prev = kc_ref[_stream(rho, i - 1, r), :] if i > 0 else kp_ref[_stream(rho, 0, r), :]
                v_prev = vc_ref[_stream(rho, i - 1, r), :] if i > 0 else vp_ref[_stream(rho, 0, r), :]
                kb = jnp.concatenate([k_prev, kc_ref[cur, :]], axis=0).astype(BF16)
                vb = jnp.concatenate([v_prev, vc_ref[cur, :]], axis=0).astype(BF16)
                o2, lse2 = _pair_fwd(q_ref[cur, :], kb, vb, bias_rest if i > 0 else bias_0, None, (0, 1), None)
                o_ref[cur, :] = o2
                lse_ref[cur, :] = lse2

        _for_streams(r, stream)

    before = lambda sb: jnp.maximum(sb * nq - 1, 0)
    return pl.pallas_call(
        body, name=f"attn_b_fwd_r{r}", grid=(NH // 2, steps),
        in_specs=[SMEM,
                  pl.BlockSpec((rows, 128), lambda j, sb: (sb, qc + j)),
                  pl.BlockSpec((BLK * r, 128), lambda j, sb: (before(sb), kc + j)),
                  pl.BlockSpec((rows, 128), lambda j, sb: (sb, kc + j)),
                  pl.BlockSpec((BLK * r, 128), lambda j, sb: (before(sb), vc + j)),
                  pl.BlockSpec((rows, 128), lambda j, sb: (sb, vc + j))],
        out_specs=[pl.BlockSpec((rows, 128), lambda j, sb: (sb, j))] * 2,
        out_shape=[jax.ShapeDtypeStruct((s, 512), F32)] * 2,
        compiler_params=_cp(("parallel", "parallel")),
    )(slopes, proj, proj, proj, proj, proj)


def _attn_b_bwd(proj, slopes, d_o, o, lse, r, so_far=None):
    s = proj.shape[0]
    nq = B_BLOCKS_PER_STEP[r]
    rows = BLK * r * nq
    steps = s // rows
    qc, kc, vc = WA // 128, WA // 128 + 4, WA // 128 + 8
    chained = so_far is not None

    def body(slope_ref, q_ref, kp_ref, kc_ref, vp_ref, vc_ref, do_ref, o_ref, lse_ref, *rest):
        if chained:
            pq_ref, pk_ref, pv_ref, dq_ref, dk_ref, dv_ref, kcar, vcar = rest
        else:
            dq_ref, dk_ref, dv_ref, kcar, vcar = rest
        j = pl.program_id(0)
        sb = pl.program_id(1)

        @pl.when(sb == 0)
        def _():
            kcar[...] = jnp.zeros_like(kcar)
            vcar[...] = jnp.zeros_like(vcar)

        if chained:
            dk_ref[...] = kcar[...] + pk_ref[...]
            dv_ref[...] = vcar[...] + pv_ref[...]
        else:
            dk_ref[...] = kcar[...]
            dv_ref[...] = vcar[...]

        @pl.when(sb < steps)
        def _():
            sl2 = (slope_ref[2 * j], slope_ref[2 * j + 1])
            bias_rest = _stack_heads(sl2, _band_base(B_MAX_DIST, r, False))
            bias_0 = jnp.where(sb > 0, bias_rest, _stack_heads(sl2, _band_base(B_MAX_DIST, r, True)))

            def stream(rho):
                for i in range(nq):
                    cur = _stream(rho, i, r)
                    k_prev = kc_ref[_stream(rho, i - 1, r), :] if i > 0 else kp_ref[_stream(rho, 0, r), :]
                    v_prev = vc_ref[_stream(rho, i - 1, r), :] if i > 0 else vp_ref[_stream(rho, 0, r), :]
                    kb = jnp.concatenate([k_prev, kc_ref[cur, :]], axis=0).astype(BF16)
                    vb = jnp.concatenate([v_prev, vc_ref[cur, :]], axis=0).astype(BF16)
                    dq2, dk2, dv2, _ = _pair_bwd(q_ref[cur, :], kb, vb, do_ref[cur, :], o_ref[cur, :], lse_ref[cur, :],
                                                 bias_rest if i > 0 else bias_0, None, (0, 1), None)
                    dq_ref[cur, :] = dq2 + pq_ref[cur, :] if chained else dq2
                    if i == 0:
                        last = _stream(rho, nq - 1, r)
                        dk_ref[last, :] += dk2[:BLK]
                        dv_ref[last, :] += dv2[:BLK]
                    else:
                        kcar[_stream(rho, i - 1, r), :] += dk2[:BLK]
                        vcar[_stream(rho, i - 1, r), :] += dv2[:BLK]
                    kcar[cur, :] = dk2[BLK:]
                    vcar[cur, :] = dv2[BLK:]

            _for_streams(r, stream)

    cur_step = lambda sb: jnp.minimum(sb, steps - 1)
    before = lambda sb: jnp.maximum(cur_step(sb) * nq - 1, 0)
    out_prev = lambda sb: jnp.maximum(sb - 1, 0)
    tile = lambda col: pl.BlockSpec((rows, 128), lambda j, sb: (cur_step(sb), col + j))
    edge = lambda col: pl.BlockSpec((BLK * r, 128), lambda j, sb: (before(sb), col + j))
    late = pl.BlockSpec((rows, 128), lambda j, sb: (out_prev(sb), j))
    grads = [tile(0), late, late]
    return pl.pallas_call(
        body, name=f"attn_b_bwd_r{r}", grid=(NH // 2, steps + 1),
        in_specs=[SMEM, tile(qc), edge(kc), tile(kc), edge(vc), tile(vc), tile(0), tile(0), tile(0)]
        + (grads if chained else []),
        out_specs=grads,
        out_shape=[jax.ShapeDtypeStruct((s, 512), F32)] * 3,
        scratch_shapes=[pltpu.VMEM((rows, 128), F32), pltpu.VMEM((rows, 128), F32)],
        compiler_params=_cp(("parallel", "arbitrary")),
    )(slopes, proj, proj, proj, proj, proj, d_o, o, lse, *(so_far if chained else ()))


def _row(v):
    return v.reshape(1, -1)


def _layer_norm_stats(z):
    mu = jnp.mean(z, axis=-1, keepdims=True)
    zc = z - mu
    var = jnp.mean(zc * zc, axis=-1, keepdims=True)
    rstd = lax.rsqrt(var + LN_EPS)
    return zc * rstd, rstd


def _layer_norm_bwd(dh, zh, rstd, g):
    dzh = dh * g
    return rstd * (dzh - jnp.mean(dzh, axis=-1, keepdims=True) - zh * jnp.mean(dzh * zh, axis=-1, keepdims=True))


def _rms(o):
    return lax.rsqrt(jnp.mean(o * o, axis=-1, keepdims=True) + RMS_EPS)


def _mix_ln1(x, o_a, o_b, lse_b, norm_a_g, norm_b_g, w_o, ln1_g, ln1_b, tm=256):
    s = x.shape[0]

    def body(x_ref, oa_ref, ob1, ob2, ob3, l1, l2, l3, ga_ref, gb_ref, wo_ref, g_ref, b_ref,
             obm_ref, lse_ref, cat_ref, z1_ref, h1_ref, h1b_ref):
        la, lb, lc = l1[...], l2[...], l3[...]
        m = jnp.maximum(jnp.maximum(la, lb), lc)
        ea, eb, ec = jnp.exp(la - m), jnp.exp(lb - m), jnp.exp(lc - m)
        den = ea + eb + ec
        obm = (ea / den) * ob1[...] + (eb / den) * ob2[...] + (ec / den) * ob3[...]
        obm_ref[...] = obm
        lse_ref[...] = m + jnp.log(den)
        oa = oa_ref[...]
        na = oa * _rms(oa) * ga_ref[...]
        nb_ = obm * _rms(obm) * gb_ref[...]
        cat = jnp.concatenate([na, nb_], axis=1).astype(BF16)
        cat_ref[...] = cat
        z1 = ALPHA * x_ref[...] + _nn(cat, wo_ref[...])
        z1_ref[...] = z1
        zh, _ = _layer_norm_stats(z1)
        h1 = zh * g_ref[...] + b_ref[...]
        h1_ref[...] = h1
        h1b_ref[...] = h1.astype(BF16)

    t512 = pl.BlockSpec((tm, 512), lambda i: (i, 0))
    td = pl.BlockSpec((tm, D), lambda i: (i, 0))
    return pl.pallas_call(
        body, name="mix_ln1", grid=(s // tm,),
        in_specs=[td] + [t512] * 7 + [_const((1, 512))] * 2 + [_resident((D, D))] + [_const((1, D))] * 2,
        out_specs=[t512, t512, td, td, td, td],
        out_shape=[jax.ShapeDtypeStruct((s, 512), F32), jax.ShapeDtypeStruct((s, 512), F32),
                   jax.ShapeDtypeStruct((s, D), BF16), jax.ShapeDtypeStruct((s, D), F32),
                   jax.ShapeDtypeStruct((s, D), F32), jax.ShapeDtypeStruct((s, D), BF16)],
        compiler_params=_cp(("parallel",)),
    )(x, o_a, *o_b, *lse_b, _row(norm_a_g), _row(norm_b_g), w_o, _row(ln1_g), _row(ln1_b))


def _gelu_and_grad(x):
    c = math.sqrt(2.0 / math.pi)
    x2 = x * x
    cx = c * x
    t = jnp.tanh(cx * (1.0 + 0.044715 * x2))
    q = 1.0 + t
    g = (0.5 * x) * q
    dg = 0.5 * q + ((0.5 * cx) * (1.0 - t * t)) * (1.0 + (3.0 * 0.044715) * x2)
    return g, dg


CONV_CHUNK = 64


def _shift_down(u, before):
    n = u.shape[0]
    ext = jnp.concatenate([before, u], axis=0)
    return pltpu.roll(ext, 1, 0)[8:], pltpu.roll(ext, 2, 0)[8:]


def _shift_up(u, after):
    n = u.shape[0]
    ext = jnp.concatenate([u, after], axis=0)
    return pltpu.roll(ext, n + 7, 0)[:n], pltpu.roll(ext, n + 6, 0)[:n]


def _up_conv_gelu(h1b, w_up, cwb, tm=512, tn=256):
    s = h1b.shape[0]
    n_i = s // tm
    n_t = (FF // tn) * n_i

    def body(h_ref, wg_ref, wv_ref, c_ref, up_ref, a_ref, g_ref, a1_ref, pend_a, pend_b, carry):
        t = pl.program_id(0)
        row_tile = jnp.maximum(t - 1, 0) % n_i
        w_refs = (wg_ref, wv_ref)

        @pl.when(t == 0)
        def _():
            pend_b[...] = jnp.zeros_like(pend_b)
            carry[...] = jnp.zeros_like(carry)

        def step(dst, src):
            def chunk(c, before):
                rows = pl.ds(c * CONV_CHUNK, CONV_CHUNK)
                u, last = [], []
                for half in (0, 1):
                    up = src[half, rows, :]
                    r1, r2 = _shift_down(up, before[half])
                    u.append(r2 * c_ref[0, half:half + 1, :] + r1 * c_ref[1, half:half + 1, :]
                             + up * c_ref[2, half:half + 1, :] + c_ref[3, half:half + 1, :])
                    last.append(up[CONV_CHUNK - 8:])
                g, dg = _gelu_and_grad(u[0])
                a_ref[rows, :] = (g * u[1]).astype(BF16)
                g_ref[rows, :] = g.astype(BF16)
                a1_ref[rows, :] = (u[1] * dg).astype(BF16)
                return tuple(last)

            edge = tuple(jnp.where(row_tile > 0, carry[half], 0.0) for half in (0, 1))
            n_c = tm // CONV_CHUNK
            n_k = n_c // 2
            tk = D // n_k
            for half in (0, 1):
                up = None
                for kq in range(n_k):
                    ks = slice(kq * tk, (kq + 1) * tk)
                    part = _nn(h_ref[:, ks], w_refs[half][ks, :])
                    up = part if kq == 0 else up + part
                    edge = chunk(half * n_k + kq, edge)
                up_ref[half] = up
                dst[half] = up
            for half in (0, 1):
                carry[half] = edge[half]

        @pl.when(t % 2 == 0)
        def _():
            step(pend_a, pend_b)

        @pl.when(t % 2 == 1)
        def _():
            step(pend_b, pend_a)

    mm = lambda t: jnp.minimum(t, n_t - 1)
    ew = lambda t: jnp.maximum(t - 1, 0)
    out_tile = pl.BlockSpec((tm, tn), lambda t: (ew(t) % n_i, ew(t) // n_i))
    return pl.pallas_call(
        body, name="up_conv_gelu", grid=(n_t + 1,),
        in_specs=[pl.BlockSpec((tm, D), lambda t: (mm(t) % n_i, 0)),
                  pl.BlockSpec((D, tn), lambda t: (0, mm(t) // n_i)),
                  pl.BlockSpec((D, tn), lambda t: (0, FF // tn + mm(t) // n_i)),
                  pl.BlockSpec((4, 2, tn), lambda t: (0, 0, ew(t) // n_i))],
        out_specs=[pl.BlockSpec((2, tm, tn), lambda t: (0, mm(t) % n_i, mm(t) // n_i)), out_tile, out_tile, out_tile],
        out_shape=[jax.ShapeDtypeStruct((2, s, FF), F32)] + [jax.ShapeDtypeStruct((s, FF), BF16)] * 3,
        scratch_shapes=[pltpu.VMEM((2, tm, tn), F32), pltpu.VMEM((2, tm, tn), F32), pltpu.VMEM((2, 8, tn), F32)],
        compiler_params=_cp(("arbitrary",)),
    )(h1b, w_up, w_up, cwb)


def _down_ln2_loss(a, w_down, h1, target, ln2_g, ln2_b, tm=256):
    s = a.shape[0]

    def body(a_ref, w_ref, h_ref, t_ref, g_ref, b_ref, dz_ref, dzb_ref, st_ref):
        @pl.when(pl.program_id(0) == 0)
        def _():
            st_ref[...] = jnp.zeros_like(st_ref)

        z2 = ALPHA * h_ref[...] + _nn(a_ref[...], w_ref[...])
        zh, rstd = _layer_norm_stats(z2)
        diff = zh * g_ref[...] + b_ref[...] - t_ref[...]
        part = 0.5 * jnp.sum(jnp.mean(diff * diff, axis=-1, keepdims=True), axis=0, keepdims=True)
        dy = diff * (1.0 / D)
        st_ref[0:1, :] += jnp.sum(dy * zh, axis=0, keepdims=True)
        st_ref[1:2, :] += jnp.sum(dy, axis=0, keepdims=True)
        st_ref[2:3, :] += jnp.broadcast_to(part, (1, D))
        dz = _layer_norm_bwd(dy, zh, rstd, g_ref[...])
        dz_ref[...] = dz
        dzb_ref[...] = dz.astype(BF16)

    td = pl.BlockSpec((tm, D), lambda i: (i, 0))
    return pl.pallas_call(
        body, name="down_ln2_loss", grid=(s // tm,),
        in_specs=[pl.BlockSpec((tm, FF), lambda i: (i, 0)), _resident((FF, D)), td, td, _const((1, D)), _const((1, D))],
        out_specs=[td, td, _const((8, D))],
        out_shape=[jax.ShapeDtypeStruct((s, D), F32), jax.ShapeDtypeStruct((s, D), BF16),
                   jax.ShapeDtypeStruct((8, D), F32)],
        compiler_params=_cp(("arbitrary",)),
    )(a, w_down, h1, target, _row(ln2_g), _row(ln2_b))


def _conv_gelu_bwd(dz2b, w_down_t, up, g, a1, cwb, tm=512, tn=256):
    s = dz2b.shape[0]
    n_i = s // tm
    n_t = (FF // tn) * n_i

    def body(dz_ref, w_ref, up_ref, g_ref, a1_ref, c_ref, dup_ref, dc_ref, pend_a, pend_b, carry):
        t = pl.program_id(0)
        first = jnp.maximum(t - 1, 0) % n_i == 0

        @pl.when(t == 0)
        def _():
            pend_b[...] = jnp.zeros_like(pend_b)

        @pl.when(first)
        def _():
            carry[...] = jnp.zeros_like(carry)
            dc_ref[...] = jnp.zeros_like(dc_ref)

        def step(dst, src):
            n_c = tm // CONV_CHUNK

            def chunk(cc, after):
                rows = pl.ds((n_c - 1 - cc) * CONV_CHUNK, CONV_CHUNK)
                da = src[rows, :]
                dus = (da * a1_ref[rows, :].astype(F32), da * g_ref[rows, :].astype(F32))
                head = []
                for half in (0, 1):
                    du = dus[half]
                    up = up_ref[half, rows, :]
                    l1, l2 = _shift_up(du, after[half])
                    dup = (du * c_ref[2, half:half + 1, :] + l1 * c_ref[1, half:half + 1, :]
                           + l2 * c_ref[0, half:half + 1, :])
                    dup_ref[half, rows, :] = dup.astype(BF16)
                    dc_ref[0, half:half + 1, :] += jnp.sum(l2 * up, axis=0, keepdims=True)
                    dc_ref[1, half:half + 1, :] += jnp.sum(l1 * up, axis=0, keepdims=True)
                    dc_ref[2, half:half + 1, :] += jnp.sum(du * up, axis=0, keepdims=True)
                    dc_ref[3, half:half + 1, :] += jnp.sum(du, axis=0, keepdims=True)
                    head.append(du[:8])
                return tuple(head)

            head = (carry[0], carry[1])
            n_k = n_c // 2
            tk = D // n_k
            da = None
            for kq in range(n_k):
                ks = slice(kq * tk, (kq + 1) * tk)
                part = _nn(dz_ref[:, ks], w_ref[ks, :])
                da = part if kq == 0 else da + part
                head = chunk(2 * kq, head)
                head = chunk(2 * kq + 1, head)
            for half in (0, 1):
                carry[half] = head[half]
            dst[...] = da

        @pl.when(t % 2 == 0)
        def _():
            step(pend_a, pend_b)

        @pl.when(t % 2 == 1)
        def _():
            step(pend_b, pend_a)

    mm = lambda t: jnp.minimum(t, n_t - 1)
    ew = lambda t: jnp.maximum(t - 1, 0)
    row = lambda t: n_i - 1 - t % n_i
    ew_tile = pl.BlockSpec((tm, tn), lambda t: (row(ew(t)), ew(t) // n_i))
    ew_pair = pl.BlockSpec((2, tm, tn), lambda t: (0, row(ew(t)), ew(t) // n_i))
    per_col = pl.BlockSpec((4, 2, tn), lambda t: (0, 0, ew(t) // n_i))
    return pl.pallas_call(
        body, name="conv_gelu_bwd", grid=(n_t + 1,),
        in_specs=[pl.BlockSpec((tm, D), lambda t: (row(mm(t)), 0)),
                  pl.BlockSpec((D, tn), lambda t: (0, mm(t) // n_i)),
                  ew_pair, ew_tile, ew_tile, per_col],
        out_specs=[ew_pair, per_col],
        out_shape=[jax.ShapeDtypeStruct((2, s, FF), BF16), jax.ShapeDtypeStruct((4, 2, FF), F32)],
        scratch_shapes=[pltpu.VMEM((tm, tn), F32), pltpu.VMEM((tm, tn), F32), pltpu.VMEM((2, 8, tn), F32)],
        compiler_params=_cp(("arbitrary",)),
    )(dz2b, w_down_t, up, g, a1, cwb)


def _dh1_ln1_bwd(dz2, dup, w_up, z1, ln1_g, tm=256):
    s = dz2.shape[0]

    def body(dz2_ref, dup_ref, w_ref, z1_ref, g_ref, dz1_ref, dz1b_ref, st_ref):
        @pl.when(pl.program_id(0) == 0)
        def _():
            st_ref[...] = jnp.zeros_like(st_ref)

        dh = ALPHA * dz2_ref[...] + _nt(dup_ref[0], w_ref[:, :FF]) + _nt(dup_ref[1], w_ref[:, FF:])
        zh, rstd = _layer_norm_stats(z1_ref[...])
        st_ref[0:1, :] += jnp.sum(dh * zh, axis=0, keepdims=True)
        st_ref[1:2, :] += jnp.sum(dh, axis=0, keepdims=True)
        dz = _layer_norm_bwd(dh, zh, rstd, g_ref[...])
        dz1_ref[...] = dz
        dz1b_ref[...] = dz.astype(BF16)

    td = pl.BlockSpec((tm, D), lambda i: (i, 0))
    return pl.pallas_call(
        body, name="dh1_ln1_bwd", grid=(s // tm,),
        in_specs=[td, pl.BlockSpec((2, tm, FF), lambda i: (0, i, 0)), _resident((D, 2 * FF)), td, _const((1, D))],
        out_specs=[td, td, _const((8, D))],
        out_shape=[jax.ShapeDtypeStruct((s, D), F32), jax.ShapeDtypeStruct((s, D), BF16),
                   jax.ShapeDtypeStruct((8, D), F32)],
        compiler_params=_cp(("arbitrary",)),
    )(dz2, dup, w_up, z1, _row(ln1_g))


def _dcat_rms_bwd(dz1b, w_o, o_a, o_b, norm_a_g, norm_b_g, tm=256):
    s = dz1b.shape[0]

    def body(dz_ref, w_ref, oa_ref, ob_ref, ga_ref, gb_ref, da_ref, db_ref, st_ref):
        @pl.when(pl.program_id(0) == 0)
        def _():
            st_ref[...] = jnp.zeros_like(st_ref)

        dcat = _nt(dz_ref[...], w_ref[...])
        for k, (o_ref, g_ref, d_ref) in enumerate(((oa_ref, ga_ref, da_ref), (ob_ref, gb_ref, db_ref))):
            o = o_ref[...]
            dn = dcat[:, 512 * k:512 * (k + 1)]
            rr = _rms(o)
            oh = o * rr
            st_ref[k:k + 1, :] += jnp.sum(dn * oh, axis=0, keepdims=True)
            doh = dn * g_ref[...]
            d_ref[...] = rr * (doh - oh * jnp.mean(doh * oh, axis=-1, keepdims=True))

    t512 = pl.BlockSpec((tm, 512), lambda i: (i, 0))
    return pl.pallas_call(
        body, name="dcat_rms_bwd", grid=(s // tm,),
        in_specs=[pl.BlockSpec((tm, D), lambda i: (i, 0)), _resident((D, D)), t512, t512,
                  _const((1, 512)), _const((1, 512))],
        out_specs=[t512, t512, _const((8, 512))],
        out_shape=[jax.ShapeDtypeStruct((s, 512), F32), jax.ShapeDtypeStruct((s, 512), F32),
                   jax.ShapeDtypeStruct((8, 512), F32)],
        compiler_params=_cp(("arbitrary",)),
    )(dz1b, w_o, o_a, o_b, _row(norm_a_g), _row(norm_b_g))


def _dproj_combine(dqa, dka, dva, dqkv_b, tm=256):
    s = dqa.shape[0]

    def body(qa, ka, va, qb, kb, vb, o_ref):
        o_ref[:, 0:512] = qa[...].astype(BF16)
        o_ref[:, 512:640] = ka[...].astype(BF16)
        o_ref[:, 640:768] = va[...].astype(BF16)
        o_ref[:, 768:1280] = qb[...].astype(BF16)
        o_ref[:, 1280:1792] = kb[...].astype(BF16)
        o_ref[:, 1792:2304] = vb[...].astype(BF16)

    t512 = pl.BlockSpec((tm, 512), lambda i: (i, 0))
    t128 = pl.BlockSpec((tm, 128), lambda i: (i, 0))
    return pl.pallas_call(
        body, name="dproj_combine", grid=(s // tm,),
        in_specs=[t512, t128, t128] + [t512] * 3,
        out_specs=pl.BlockSpec((tm, WIN), lambda i: (i, 0)),
        out_shape=jax.ShapeDtypeStruct((s, WIN), BF16),
        compiler_params=_cp(("parallel",)),
    )(dqa, dka, dva, *dqkv_b)


def _grad_x(dz1, dproj, w_in_t, zero, tm=256):
    s = dz1.shape[0]

    def body(dz_ref, dp_ref, w_ref, z_ref, o_ref):
        o_ref[...] = ALPHA * dz_ref[...] + _nn(dp_ref[...], w_ref[...]) + z_ref[0:1, 0:1]

    td = pl.BlockSpec((tm, D), lambda i: (i, 0))
    return pl.pallas_call(
        body, name="grad_x", grid=(s // tm,),
        in_specs=[td, pl.BlockSpec((tm, WIN), lambda i: (i, 0)), _resident((WIN, D)), _const((8, 128))],
        out_specs=td, out_shape=jax.ShapeDtypeStruct((s, D), F32),
        compiler_params=_cp(("parallel",)),
    )(dz1, dproj, w_in_t, zero)


def _place():
    return lax.axis_index("x"), lax.axis_index("y"), lax.axis_index("c")


def _other_chips(x, y):
    return [(1 - x, y), (x, 1 - y), (1 - x, 1 - y)]


def _hbm(a):
    return pltpu.with_memory_space_constraint(a, pltpu.HBM)


def _gather_w_in(shard, conv_w):
    rows_k = shard.shape[0]
    half = rows_k // 2

    def body(src, conv_src, out, conv_out, send_sems, recv_sems):
        x, y, c = _place()
        b = 2 * x + y
        sibling = (x, y, 1 - c)
        chips = _other_chips(x, y)

        def copy(idx, chip_b, core, to, first_hop=False):
            rows = out.at[pl.ds(pl.multiple_of(chip_b * rows_k + core * half, 16), half)]
            s_ref = src.at[pl.ds(pl.multiple_of(core * half, 16), half)] if first_hop else rows
            return pltpu.make_async_remote_copy(src_ref=s_ref, dst_ref=rows, send_sem=send_sems.at[idx],
                                                recv_sem=recv_sems.at[idx], device_id=to, device_id_type=MESH)

        def own_copy():
            return pltpu.make_async_remote_copy(
                src_ref=src, dst_ref=out.at[pl.ds(pl.multiple_of(b * rows_k, 16), rows_k)], send_sem=send_sems.at[6],
                recv_sem=recv_sems.at[6], device_id=sibling, device_id_type=MESH)

        def conv_copy(idx, chip_b, to):
            return pltpu.make_async_remote_copy(src_ref=conv_src, dst_ref=conv_out.at[chip_b],
                                                send_sem=send_sems.at[7 + idx], recv_sem=recv_sems.at[7 + idx],
                                                device_id=to, device_id_type=MESH)

        started = [own_copy(), conv_copy(3, b, sibling)]
        for jn, chip in enumerate(chips):
            started += [copy(jn, b, c, (chip[0], chip[1], c), first_hop=True), conv_copy(jn, b, (chip[0], chip[1], c))]
        for cp in started:
            cp.start()
        for jn, chip in enumerate(chips):
            cb = 2 * chip[0] + chip[1]
            copy(jn, cb, c, (chip[0], chip[1], c)).wait_recv()
            cp = copy(3 + jn, cb, c, sibling)
            cp.start()
            started.append(cp)
        for jn, chip in enumerate(chips):
            cb = 2 * chip[0] + chip[1]
            copy(3 + jn, cb, 1 - c, sibling).wait_recv()
            conv_copy(jn, cb, (chip[0], chip[1], c)).wait_recv()
        own_copy().wait_recv()
        conv_copy(3, b, sibling).wait_recv()
        for cp in started:
            cp.wait_send()

    return pl.pallas_call(
        body, name="gather_w_in",
        in_specs=[ANY, ANY], out_specs=[ANY, ANY],
        out_shape=[jax.ShapeDtypeStruct((N_CHIPS * rows_k, D), BF16), jax.ShapeDtypeStruct((N_CHIPS,) + conv_w.shape, F32)],
        scratch_shapes=[pltpu.SemaphoreType.DMA((11,)), pltpu.SemaphoreType.DMA((11,))],
        compiler_params=pltpu.CompilerParams(has_side_effects=True),
    )(shard, conv_w)


def _weight_copies(shard, land, send_sems, recv_sems, arrivals):
    x, y, c = _place()
    n_rows, n_cols = shard.shape
    peers = [(px, py, c) for px, py in _other_chips(x, y)] + [(x, y, 1 - c)]
    cps = []
    for jn, peer in enumerate(peers):
        at = 2 * peer[0] + peer[1] if arrivals else 2 * x + y
        if land.shape[1] == n_cols:
            dst = land.at[pl.ds(pl.multiple_of(at * n_rows, 16), n_rows)]
        else:
            dst = land.at[:, pl.ds(pl.multiple_of(at * n_cols, 128), n_cols)]
        cps.append(pltpu.make_async_remote_copy(src_ref=shard, dst_ref=dst, send_sem=send_sems.at[jn],
                                                recv_sem=recv_sems.at[jn], device_id=peer, device_id_type=MESH))
    return cps


def _weights_start(shards, after):
    n = len(shards)
    lands = [lax.empty((N_CHIPS * sh.shape[0], D) if sh.shape[1] == D else (D, N_CHIPS * sh.shape[1]), BF16)
             for sh in shards]

    def body(*refs):
        src, land = refs[:n], refs[n:2 * n]
        send_sems, recv_sems = refs[2 * n + 1:3 * n + 1], refs[3 * n + 1:4 * n + 1]
        for k in range(n):
            for send in _weight_copies(src[k], land[k], send_sems[k], recv_sems[k], False):
                send.start()
        refs[-1][...] = jnp.zeros_like(refs[-1])

    res = pl.pallas_call(
        body, name="weights_start",
        in_specs=[HBM] * (2 * n) + [ANY], out_specs=[SEM] * (2 * n) + [HBM] * (2 * n) + [VMEM],
        out_shape=[pltpu.SemaphoreType.DMA((4,))] * (2 * n)
        + [pltpu.HBM(a.shape, a.dtype) for a in (*shards, *lands)] + [jax.ShapeDtypeStruct((8, 128), F32)],
        input_output_aliases={i: i + 2 * n for i in range(2 * n)},
        compiler_params=pltpu.CompilerParams(has_side_effects=DATAFLOW),
    )(*[_hbm(a) for a in (*shards, *lands)], after)
    return [(res[k], res[n + k], res[2 * n + k], res[3 * n + k]) for k in range(n)], res[-1]


def _weights_wait(started, after, name):
    send_sems, recv_sems, shard, land = started

    def body(s_ref, l_ref, send_ref, recv_ref, after_ref, s_out, l_out):
        for cp in _weight_copies(s_ref, l_ref, send_ref, recv_ref, True):
            cp.wait_send()
            cp.wait_recv()

    return pl.pallas_call(
        body, name=name,
        in_specs=[HBM, HBM, SEM, SEM, ANY], out_specs=[HBM, HBM],
        out_shape=[pltpu.HBM(shard.shape, shard.dtype), pltpu.HBM(land.shape, land.dtype)],
        input_output_aliases={0: 0, 1: 1},
        compiler_params=pltpu.CompilerParams(has_side_effects=DATAFLOW),
    )(shard, land, send_sems, recv_sems, after)[1]


def _grad_copies(g_ref, land_ref, send_sems, recv_sems):
    x, y, c = _place()
    cps = []
    for d in range(1, 8):
        px, py, pc = x ^ (d >> 2), y ^ ((d >> 1) & 1), c ^ (d & 1)
        cps.append(pltpu.make_async_remote_copy(
            src_ref=g_ref.at[2 * px + py, pc], dst_ref=land_ref.at[d - 1], send_sem=send_sems.at[d - 1],
            recv_sem=recv_sems.at[d - 1], device_id=(px, py, pc), device_id_type=MESH))
    return cps


def _grads_start(grads_b, name):
    n = len(grads_b)
    lands = [lax.empty((7, g.shape[2], D), BF16) for g in grads_b]

    def body(*refs):
        g, land = refs[:n], refs[n:2 * n]
        send_sems, recv_sems = refs[2 * n:3 * n], refs[3 * n:4 * n]
        for k in range(n):
            for cp in _grad_copies(g[k], land[k], send_sems[k], recv_sems[k]):
                cp.start()
        refs[-1][...] = jnp.zeros_like(refs[-1])

    res = pl.pallas_call(
        body, name=name,
        in_specs=[HBM] * (2 * n), out_specs=[SEM] * (2 * n) + [HBM] * (2 * n) + [VMEM],
        out_shape=[pltpu.SemaphoreType.DMA((7,))] * (2 * n)
        + [pltpu.HBM(a.shape, a.dtype) for a in (*grads_b, *lands)] + [jax.ShapeDtypeStruct((8, 128), F32)],
        input_output_aliases={i: i + 2 * n for i in range(2 * n)},
        compiler_params=pltpu.CompilerParams(has_side_effects=DATAFLOW),
    )(*[_hbm(a) for a in (*grads_b, *lands)])
    return [(res[k], res[n + k], res[2 * n + k], res[3 * n + k]) for k in range(n)], res[-1]


def _grads_wait(started, after, name):
    n = len(started)

    def body(*refs):
        g, land = refs[:n], refs[n:2 * n]
        send_sems, recv_sems = refs[2 * n:3 * n], refs[3 * n:4 * n]
        for k in range(n):
            for cp in _grad_copies(g[k], land[k], send_sems[k], recv_sems[k]):
                cp.wait_send()
                cp.wait_recv()

    gs = [st[2] for st in started]
    lands = [st[3] for st in started]
    res = pl.pallas_call(
        body, name=name,
        in_specs=[HBM] * (2 * n) + [SEM] * (2 * n) + [ANY], out_specs=[HBM] * (2 * n),
        out_shape=[pltpu.HBM(a.shape, a.dtype) for a in (*gs, *lands)],
        input_output_aliases={i: i for i in range(2 * n)},
        compiler_params=pltpu.CompilerParams(has_side_effects=DATAFLOW),
    )(*gs, *lands, *[st[0] for st in started], *[st[1] for st in started], after)
    return res[n:]


def _sum_partials(grad4, got, cb, name, tr):
    h = grad4.shape[2]
    per_half = h // tr

    def body(cb_ref, g_ref, o_ref, out_ref):
        acc = g_ref[...]
        for j in range(7):
            acc = acc + o_ref[j].astype(F32)
        out_ref[...] = acc

    return pl.pallas_call(
        body, name=name,
        grid_spec=pltpu.PrefetchScalarGridSpec(
            num_scalar_prefetch=1, grid=(per_half,),
            in_specs=[pl.BlockSpec((None, None, tr, D), lambda i, cb_ref: (cb_ref[1], cb_ref[0], i, 0)),
                      pl.BlockSpec((7, tr, D), lambda i, cb_ref: (0, i, 0))],
            out_specs=pl.BlockSpec((tr, D), lambda i, cb_ref: (cb_ref[0] * per_half + i, 0))),
        out_shape=jax.ShapeDtypeStruct((2 * h, D), F32),
        compiler_params=_cp(("arbitrary",)),
    )(cb, grad4, got)


def _swap_halves(shards, name):
    n = len(shards)

    def body(*refs):
        out, send_sems, recv_sems = refs[n:2 * n], refs[2 * n], refs[2 * n + 1]
        x, y, c = _place()
        cps = []
        for k in range(n):
            h = shards[k].shape[0] // 2
            mine = out[k].at[pl.ds(pl.multiple_of(c * h, 8), h)]
            cp = pltpu.make_async_remote_copy(src_ref=mine, dst_ref=mine, send_sem=send_sems.at[k],
                                              recv_sem=recv_sems.at[k], device_id=(x, y, 1 - c), device_id_type=MESH)
            cp.start()
            cps.append(cp)
        for cp in cps:
            cp.wait()

    return pl.pallas_call(
        body, name=name,
        in_specs=[ANY] * n, out_specs=[ANY] * n,
        out_shape=[jax.ShapeDtypeStruct(sh.shape, F32) for sh in shards],
        input_output_aliases={k: k for k in range(n)},
        scratch_shapes=[pltpu.SemaphoreType.DMA((n,)), pltpu.SemaphoreType.DMA((n,))],
        compiler_params=pltpu.CompilerParams(has_side_effects=True),
    )(*shards)


def _share_halves(shards, small):
    n = len(shards)
    rows = small.shape[0]

    def body(*refs):
        small_ref = refs[n]
        out, total_ref = refs[n + 1:2 * n + 1], refs[2 * n + 1]
        all_ref, send_sems, recv_sems, ssend, srecv = refs[2 * n + 2:]
        x, y, c = _place()
        me = 4 * x + 2 * y + c
        cps = []
        for k in range(n):
            h = shards[k].shape[0] // 2
            mine = out[k].at[pl.ds(pl.multiple_of(c * h, 8), h)]
            cp = pltpu.make_async_remote_copy(src_ref=mine, dst_ref=mine, send_sem=send_sems.at[k],
                                              recv_sem=recv_sems.at[k], device_id=(x, y, 1 - c), device_id_type=MESH)
            cp.start()
            cps.append(cp)
        all_ref[me] = small_ref[...]
        peers = []
        for d in range(1, 8):
            px, py, pc = x ^ (d >> 2), y ^ ((d >> 1) & 1), c ^ (d & 1)
            cp = pltpu.make_async_remote_copy(src_ref=small_ref, dst_ref=all_ref.at[me],
                                              send_sem=ssend.at[d - 1], recv_sem=srecv.at[d - 1],
                                              device_id=(px, py, pc), device_id_type=MESH)
            cp.start()
            peers.append(cp)
        for cp in peers:
            cp.wait()
        acc = all_ref[0]
        for d in range(1, 8):
            acc = acc + all_ref[d]
        total_ref[...] = acc
        for cp in cps:
            cp.wait()

    return pl.pallas_call(
        body, name="share_halves",
        in_specs=[ANY] * n + [VMEM], out_specs=[ANY] * n + [VMEM],
        out_shape=[jax.ShapeDtypeStruct(sh.shape, F32) for sh in shards] + [jax.ShapeDtypeStruct((rows, D), F32)],
        input_output_aliases={k: k for k in range(n)},
        scratch_shapes=[pltpu.VMEM((8, rows, D), F32), pltpu.SemaphoreType.DMA((n,)), pltpu.SemaphoreType.DMA((n,)),
                        pltpu.SemaphoreType.DMA((7,)), pltpu.SemaphoreType.DMA((7,))],
        compiler_params=pltpu.CompilerParams(has_side_effects=True),
    )(*shards, small)


def _adamw(w, g, m, v, name, tr):
    rows, cols = w.shape

    def body(w_ref, g_ref, m_ref, v_ref, d_ref, nm_ref, nv_ref):
        g_ = g_ref[...]
        nm = ADAM_B1 * m_ref[...] + (1.0 - ADAM_B1) * g_
        nv = ADAM_B2 * v_ref[...] + (1.0 - ADAM_B2) * (g_ * g_)
        m_hat = nm / (1.0 - ADAM_B1 ** ADAM_STEP)
        v_hat = nv / (1.0 - ADAM_B2 ** ADAM_STEP)
        d_ref[...] = -ADAM_LR * (m_hat / (jnp.sqrt(v_hat) + ADAM_EPS) + ADAM_WD * w_ref[...])
        nm_ref[...] = nm
        nv_ref[...] = nv

    spec = pl.BlockSpec((tr, cols), lambda i: (i, 0))
    return pl.pallas_call(
        body, name=name, grid=(rows // tr,),
        in_specs=[spec] * 4, out_specs=[spec] * 3,
        out_shape=[jax.ShapeDtypeStruct((rows, cols), F32)] * 3,
        compiler_params=_cp(("parallel",)),
    )(w, g, m, v)


def _local_step(x, target, w_in_t, late_weights, norm_a_g, norm_b_g, sinks_a, ln1_g, ln1_b,
                conv_w, conv_b, ln2_g, ln2_b, slopes, on_grad):
    cwb = jnp.concatenate([conv_w, conv_b[None]], axis=0).reshape(4, 2, FF)

    proj, xb = _proj(x, w_in_t, "proj")
    o_a, lse_a = _attn_a_fwd(proj, sinks_a)
    fwd_b = [_attn_b_fwd(proj, slopes, r) for r in B_DILATIONS]
    w_o = late_weights(1, fwd_b[-1][1])
    o_b, lse_b, cat, z1, h1, h1b = _mix_ln1(x, o_a, [f[0] for f in fwd_b], [f[1] for f in fwd_b],
                                           norm_a_g, norm_b_g, w_o, ln1_g, ln1_b)
    w_up = late_weights(2, h1b)
    up, a, gate, a1 = _up_conv_gelu(h1b, w_up, cwb)
    w_down = late_weights(3, a)
    dz2, dz2b, st2 = _down_ln2_loss(a, w_down, h1, target, ln2_g, ln2_b)

    on_grad(3, *_grad_w(a, dz2b, "grad_w_down", tm=FF // 2))
    dup, dconv = _conv_gelu_bwd(dz2b, w_down.T, up, gate, a1, cwb)
    on_grad(2, *_grad_w(dup, h1b, "grad_w_up", tm=FF // 2, lhs_halves=True))
    dz1, dz1b, st1 = _dh1_ln1_bwd(dz2, dup, w_up, z1, ln1_g)
    tok = on_grad(1, *_grad_w(cat, dz1b, "grad_w_o", tm=512))
    d_oa, d_ob, st_n = _dcat_rms_bwd(dz1b, w_o, o_a, o_b, norm_a_g + tok[0, 0], norm_b_g)
    dqa, dka, dva, dsink = _attn_a_bwd(proj, sinks_a, d_oa, o_a, lse_a)
    bwd_b = None
    for r in B_DILATIONS:
        bwd_b = _attn_b_bwd(proj, slopes, d_ob, o_b, lse_b, r, bwd_b)
    dproj = _dproj_combine(dqa, dka, dva, bwd_b)
    tok = on_grad(0, *_grad_w(dproj, xb, "grad_w_in", tm=WA))
    gx = _grad_x(dz1, dproj, w_in_t, tok)

    dconv = dconv.reshape(4, 2 * FF)
    small = dict(loss=st2[2, 0:1], norm_a_g=st_n[0], norm_b_g=st_n[1], sinks_a=dsink[:, 0],
                 ln1_g=st1[0], ln1_b=st1[1], conv_w=dconv[0:3].reshape(-1), conv_b=dconv[3],
                 ln2_g=st2[0], ln2_b=st2[1])
    return gx, small


SMALL_ORDER = ("loss", "norm_a_g", "norm_b_g", "sinks_a", "ln1_g", "ln1_b", "conv_b", "ln2_g", "ln2_b", "conv_w")
SMALL_SIZES = dict(loss=1, norm_a_g=512, norm_b_g=512, sinks_a=8, ln1_g=D, ln1_b=D, conv_b=2 * FF, ln2_g=D, ln2_b=D,
                   conv_w=3 * 2 * FF)


def _pack(parts, rows):
    flat = jnp.concatenate([parts[k].reshape(-1).astype(F32) for k in parts])
    return jnp.pad(flat, (0, rows * D - flat.shape[0])).reshape(rows, D)


def _unpack(buf, names, sizes):
    flat = buf.reshape(-1)
    out, at = {}, 0
    for k in names:
        out[k] = flat[at:at + sizes[k]]
        at += sizes[k]
    return out


def kernel(x, w_in, norm_a_g, norm_b_g, sinks_a, w_o, ln1_g, ln1_b, w_up, conv_w, conv_b, w_down, ln2_g, ln2_b, loss_target, m_w_in, m_norm_a_g, m_norm_b_g, m_sinks_a, m_w_o, m_ln1_g, m_ln1_b, m_w_up, m_conv_w, m_conv_b, m_w_down, m_ln2_g, m_ln2_b, v_w_in, v_norm_a_g, v_norm_b_g, v_sinks_a, v_w_o, v_ln1_g, v_ln1_b, v_w_up, v_conv_w, v_conv_b, v_w_down, v_ln2_g, v_ln2_b):
    xi, yi, ci = _place()
    chip = (2 * xi + yi).astype(I32)
    core = ci.astype(I32)

    w_in_rows, m_w_in_rows, v_w_in_rows = w_in.T, m_w_in.T, v_w_in.T
    shards = (w_in_rows.astype(BF16), w_o.astype(BF16), w_up.astype(BF16), w_down.astype(BF16))
    w_in_t, conv_w4 = _gather_w_in(shards[0], conv_w)
    conv_w_f = conv_w4.transpose(1, 0, 2).reshape(3, 2 * FF)
    w_started, w_tok = _weights_start(shards[1:], conv_w4)
    slopes = jnp.asarray(SLOPES, F32) + w_tok[0, 0]

    halves_rows = [r // 2 for r in SHARD_ROWS]
    grads4, grads_b4, started = [None] * 4, [None] * 4, [None] * 4

    def on_grad(k, g, g_b):
        grads4[k] = g.reshape(N_CHIPS, 2, halves_rows[k], D)
        grads_b4[k] = g_b.reshape(N_CHIPS, 2, halves_rows[k], D)
        if k > 1:
            return None
        group = (1, 2, 3) if k == 1 else (0,)
        sts, tok = _grads_start([grads_b4[i] for i in group], f"grads_start_{k}")
        for i, st in zip(group, sts):
            started[i] = st
        return tok

    gx, small = _local_step(
        x[0], loss_target[0], w_in_t, lambda k, after: _weights_wait(w_started[k - 1], after, f"weights_wait_{k}"),
        norm_a_g, norm_b_g, sinks_a, ln1_g, ln1_b, conv_w_f, conv_b, ln2_g, ln2_b, slopes, on_grad)

    tiles = (96, 128, 352, 176)
    core_chip = jnp.stack([core, chip])
    got = _grads_wait(started[1:], gx, "grads_wait_1")
    halves = [_sum_partials(grads4[k], got[k - 1], core_chip, f"sum_partials_{k}", tiles[k]) for k in (1, 2, 3)]
    g_w_o, g_w_up_rows, g_w_down = _swap_halves(halves, "swap_halves")
    g_w_up = g_w_up_rows.T
    delta, new_m, new_v = {}, {}, {}
    for k, g, tr in (("w_o", g_w_o, 128), ("w_up", g_w_up, 256), ("w_down", g_w_down, 176)):
        delta[k], new_m[k], new_v[k] = _adamw(dict(w_o=w_o, w_up=w_up, w_down=w_down)[k], g,
                                              dict(w_o=m_w_o, w_up=m_w_up, w_down=m_w_down)[k],
                                              dict(w_o=v_w_o, w_up=v_w_up, w_down=v_w_down)[k], f"adamw_{k}", tr)

    got = _grads_wait(started[:1], delta["w_up"], "grads_wait_0")
    half_in = _sum_partials(grads4[0], got[0], core_chip, "sum_partials_0", tiles[0])
    small_rows = 32
    g_w_in_rows, totals = _share_halves([half_in], _pack({k: small[k] for k in SMALL_ORDER}, small_rows))
    tot = _unpack(totals, SMALL_ORDER, SMALL_SIZES)
    loss = tot["loss"][0]
    cols = 2 * FF // N_CHIPS
    g_conv_w = lax.dynamic_slice(tot["conv_w"].reshape(3, 2 * FF), (0, chip * cols), (3, cols))
    g_small = dict(norm_a_g=tot["norm_a_g"], norm_b_g=tot["norm_b_g"], sinks_a=tot["sinks_a"], ln1_g=tot["ln1_g"],
                   ln1_b=tot["ln1_b"], conv_w=g_conv_w, conv_b=tot["conv_b"], ln2_g=tot["ln2_g"], ln2_b=tot["ln2_b"])

    weights = dict(w_in=w_in, norm_a_g=norm_a_g, norm_b_g=norm_b_g, sinks_a=sinks_a, w_o=w_o, ln1_g=ln1_g, ln1_b=ln1_b,
                   w_up=w_up, conv_w=conv_w, conv_b=conv_b, w_down=w_down, ln2_g=ln2_g, ln2_b=ln2_b)
    ms = dict(w_in=m_w_in, norm_a_g=m_norm_a_g, norm_b_g=m_norm_b_g, sinks_a=m_sinks_a, w_o=m_w_o, ln1_g=m_ln1_g,
              ln1_b=m_ln1_b, w_up=m_w_up, conv_w=m_conv_w, conv_b=m_conv_b, w_down=m_w_down, ln2_g=m_ln2_g, ln2_b=m_ln2_b)
    vs = dict(w_in=v_w_in, norm_a_g=v_norm_a_g, norm_b_g=v_norm_b_g, sinks_a=v_sinks_a, w_o=v_w_o, ln1_g=v_ln1_g,
              ln1_b=v_ln1_b, w_up=v_w_up, conv_w=v_conv_w, conv_b=v_conv_b, w_down=v_w_down, ln2_g=v_ln2_g, ln2_b=v_ln2_b)
    order = list(weights)
    grad = dict(g_small, w_in=g_w_in_rows.T, w_o=g_w_o, w_up=g_w_up, w_down=g_w_down)

    delta["w_in"], new_m["w_in"], new_v["w_in"] = [
        a.T for a in _adamw(w_in_rows, g_w_in_rows, m_w_in_rows, v_w_in_rows, "adamw_w_in", 144)]
    small_names = [k for k in order if k not in delta]
    sizes = {k: weights[k].size for k in small_names}
    rows = 16
    packed = [_pack({k: src[k] for k in small_names}, rows) for src in (weights, grad, ms, vs)]
    for res, buf in zip((delta, new_m, new_v), _adamw(*packed, "adamw_small", rows)):
        for k, val in _unpack(buf, small_names, sizes).items():
            res[k] = val.reshape(weights[k].shape)

    return (loss, gx[None], *[grad[k] for k in order], *[delta[k] for k in order],
            *[new_m[k] for k in order], *[new_v[k] for k in order])
```

```python
import functools
import math

import jax
import jax.numpy as jnp
from jax import lax
from jax.experimental import pallas as pl
from jax.experimental.pallas import tpu as pltpu

F32, BF16, I32 = jnp.float32, jnp.bfloat16, jnp.int32

D = 1024
FF = 2816
HD = 64
NH = 8
WA, WB = 768, 1536
WIN = WA + WB
BLK = 128
ALPHA = 2.0 ** 0.25
LN_EPS, RMS_EPS = 1e-5, 1e-6
SCALE = 1.0 / math.sqrt(HD)
A_MAX_DIST, B_MAX_DIST = 127, 128
B_DILATIONS = (1, 4, 16)
SLOPES = tuple(2.0 ** (-(i + 1)) for i in range(NH))
SHARD_ROWS = (WIN // 4, D // 4, 2 * FF // 4, FF // 4)
N_CHIPS = 4
ADAM_LR, ADAM_B1, ADAM_B2, ADAM_EPS, ADAM_WD, ADAM_STEP = 0.001, 0.9, 0.999, 1e-08, 0.01, 10
MESH = pl.DeviceIdType.MESH
ANY = pl.BlockSpec(memory_space=pl.ANY)
SMEM = pl.BlockSpec(memory_space=pltpu.SMEM)
VMEM = pl.BlockSpec(memory_space=pltpu.VMEM)
HBM = pl.BlockSpec(memory_space=pltpu.HBM)
SEM = pl.BlockSpec(memory_space=pltpu.SEMAPHORE)
DATAFLOW = pltpu.SideEffectType.DATAFLOW_SIDE_EFFECTING


def _cp(sem, mb=48):
    return pltpu.CompilerParams(dimension_semantics=sem, vmem_limit_bytes=mb << 20)


def _nn(a, b):
    return lax.dot_general(a, b, (((1,), (0,)), ((), ())), preferred_element_type=F32)


def _nt(a, b):
    return lax.dot_general(a, b, (((1,), (1,)), ((), ())), preferred_element_type=F32)


def _tn(a, b):
    return lax.dot_general(a, b, (((0,), (0,)), ((), ())), preferred_element_type=F32)


def _resident(shape):
    n = len(shape)
    return pl.BlockSpec(shape, lambda *_: (0,) * n, pipeline_mode=pl.Buffered(1))


def _const(shape):
    n = len(shape)
    return pl.BlockSpec(shape, lambda *_: (0,) * n)


def _proj(x, w_t, name, tm=512):
    s = x.shape[0]
    n = w_t.shape[0]

    def body(x_ref, w_ref, o_ref, xb_ref):
        xb = x_ref[...].astype(BF16)
        xb_ref[...] = xb
        o_ref[...] = _nt(xb, w_ref[...])

    return pl.pallas_call(
        body, name=name, grid=(s // tm,),
        in_specs=[pl.BlockSpec((tm, D), lambda i: (i, 0)), _resident((n, D))],
        out_specs=[pl.BlockSpec((tm, n), lambda i: (i, 0)), pl.BlockSpec((tm, D), lambda i: (i, 0))],
        out_shape=[jax.ShapeDtypeStruct((s, n), F32), jax.ShapeDtypeStruct((s, D), BF16)],
        compiler_params=_cp(("parallel",)),
    )(x, w_t)


def _grad_w(lhs, rhs, name, tm, tk=512, lhs_halves=False):
    s = rhs.shape[0]
    if lhs_halves:
        per_half = lhs.shape[2] // tm
        n = 2 * lhs.shape[2]
        lhs_spec = pl.BlockSpec((None, tk, tm), lambda i, k: (i // per_half, k, i % per_half))
    else:
        n = lhs.shape[1]
        lhs_spec = pl.BlockSpec((tk, tm), lambda i, k: (k, i))
    nk = s // tk

    def body(l_ref, r_ref, o_ref, ob_ref):
        k = pl.program_id(1)

        @pl.when(k == 0)
        def _():
            o_ref[...] = jnp.zeros_like(o_ref)

        o_ref[...] += _tn(l_ref[...].astype(BF16), r_ref[...].astype(BF16))

        @pl.when(k == nk - 1)
        def _():
            ob_ref[...] = o_ref[...].astype(BF16)

    return pl.pallas_call(
        body, name=name, grid=(n // tm, nk),
        in_specs=[lhs_spec, pl.BlockSpec((tk, D), lambda i, k: (k, 0))],
        out_specs=[pl.BlockSpec((tm, D), lambda i, k: (i, 0))] * 2,
        out_shape=[jax.ShapeDtypeStruct((n, D), F32), jax.ShapeDtypeStruct((n, D), BF16)],
        compiler_params=_cp(("parallel", "arbitrary")),
    )(lhs, rhs)


def _band_base(max_dist, dist_unit, first):
    row = lax.broadcasted_iota(I32, (BLK, 2 * BLK), 0)
    col = lax.broadcasted_iota(I32, (BLK, 2 * BLK), 1)
    dist = BLK + row - col
    ok = (dist >= 0) & (dist <= max_dist)
    if first:
        ok = ok & (col >= BLK)
    return jnp.where(ok, dist.astype(F32) * (-float(dist_unit)), -jnp.inf)


def _half_mask(shape, e):
    lane = lax.broadcasted_iota(I32, shape, 1)
    return (lane < HD) if e == 0 else (lane >= HD)


def _to_half(x, e, g):
    if g != e:
        x = pltpu.roll(x, HD, 1)
    return jnp.where(_half_mask(x.shape, g), x, 0.0)


def _stack_heads(scalars, tile):
    return jnp.concatenate([scalars[0] * tile, scalars[1] * tile], axis=0)


def _pair_fwd(q2, kb, vb, base, slopes, kv_heads, sinks):
    lo = _half_mask((BLK, 2 * HD), 0)
    if sinks is not None:
        o2 = lse2 = None
        for e in (0, 1):
            g = kv_heads[e]
            qv = (_to_half(q2, e, g) * SCALE).astype(BF16)
            s = _nt(qv, kb) + slopes[e] * base
            m = jnp.maximum(jnp.max(s, axis=1, keepdims=True), sinks[e])
            p = jnp.exp(s - m)
            l = jnp.sum(p, axis=1, keepdims=True) + jnp.exp(sinks[e] - m)
            oh = _nn(p.astype(BF16), vb) / l
            if g != e:
                oh = pltpu.roll(oh, HD, 1)
            lse = jnp.broadcast_to(m + jnp.log(l), (BLK, 2 * HD))
            o2 = oh if e == 0 else jnp.where(lo, o2, oh)
            lse2 = lse if e == 0 else jnp.where(lo, lse2, lse)
        return o2, lse2
    qs = jnp.concatenate([_to_half(q2, e, kv_heads[e]) * SCALE for e in (0, 1)], axis=0).astype(BF16)
    s = _nt(qs, kb) + (base if slopes is None else _stack_heads(slopes, base))
    m = jnp.max(s, axis=1, keepdims=True)
    p = jnp.exp(s - m)
    l = jnp.sum(p, axis=1, keepdims=True)
    o = _nn(p.astype(BF16), vb) / l
    lse = m + jnp.log(l)
    halves = []
    for e in (0, 1):
        oh = o[e * BLK:(e + 1) * BLK]
        halves.append(pltpu.roll(oh, HD, 1) if kv_heads[e] != e else oh)
    o2 = jnp.where(lo, halves[0], halves[1])
    lse2 = jnp.where(lo, jnp.broadcast_to(lse[:BLK], (BLK, 2 * HD)), jnp.broadcast_to(lse[BLK:], (BLK, 2 * HD)))
    return o2, lse2


def _pair_bwd(q2, kb, vb, do2, o2, lse2, base, slopes, kv_heads, sinks):
    lo = _half_mask((BLK, 2 * HD), 0)
    prod = do2 * o2
    lses, deltas = [], []
    for e in (0, 1):
        hq = _half_mask((BLK, 2 * HD), e)
        lses.append(jnp.max(jnp.where(hq, lse2, -jnp.inf), axis=1, keepdims=True))
        deltas.append(jnp.sum(jnp.where(hq, prod, 0.0), axis=1, keepdims=True))
    lse = jnp.concatenate(lses, axis=0)
    delta = jnp.concatenate(deltas, axis=0)
    qs = jnp.concatenate([_to_half(q2, e, kv_heads[e]) * SCALE for e in (0, 1)], axis=0).astype(BF16)
    dos = jnp.concatenate([_to_half(do2, e, kv_heads[e]) for e in (0, 1)], axis=0).astype(BF16)
    p = jnp.exp(_nt(qs, kb) + (base if slopes is None else _stack_heads(slopes, base)) - lse)
    ds = (p * (_nt(dos, vb) - delta)).astype(BF16)
    dq = _nn(ds, kb) * SCALE
    halves = []
    for e in (0, 1):
        dqh = dq[e * BLK:(e + 1) * BLK]
        halves.append(pltpu.roll(dqh, HD, 1) if kv_heads[e] != e else dqh)
    dq2 = jnp.where(lo, halves[0], halves[1])
    dk2 = _tn(ds, qs)
    dv2 = _tn(p.astype(BF16), dos)
    dsinks = []
    if sinks is not None:
        for e in (0, 1):
            dsinks.append(jnp.sum(-jnp.exp(sinks[e] - lses[e]) * deltas[e], axis=0, keepdims=True))
    return dq2, dk2, dv2, dsinks


A_BLOCKS_PER_STEP = 2
A_BLOCKS_PER_STEP_BWD = 1


def _attn_a_fwd(proj, sinks):
    s = proj.shape[0]
    nq = A_BLOCKS_PER_STEP
    rows = BLK * nq
    steps = s // rows

    def body(sink_ref, q_ref, kp_ref, kc_ref, vp_ref, vc_ref, o_ref, lse_ref):
        n = pl.program_id(0)
        base_rest = _band_base(A_MAX_DIST, 1, False)
        base_0 = jnp.where(n > 0, base_rest, _band_base(A_MAX_DIST, 1, True))
        for i in range(nq):
            cur = pl.ds(i * BLK, BLK)
            k_prev = kc_ref[pl.ds((i - 1) * BLK, BLK), :] if i > 0 else kp_ref[...]
            v_prev = vc_ref[pl.ds((i - 1) * BLK, BLK), :] if i > 0 else vp_ref[...]
            kb = jnp.concatenate([k_prev, kc_ref[cur, :]], axis=0).astype(BF16)
            vb = jnp.concatenate([v_prev, vc_ref[cur, :]], axis=0).astype(BF16)
            for j in range(NH // 2):
                g = j // 2
                o2, lse2 = _pair_fwd(q_ref[cur, 128 * j:128 * (j + 1)], kb, vb, base_rest if i > 0 else base_0,
                                     (SLOPES[2 * j], SLOPES[2 * j + 1]), (g, g), (sink_ref[2 * j], sink_ref[2 * j + 1]))
                o_ref[cur, 128 * j:128 * (j + 1)] = o2
                lse_ref[cur, 128 * j:128 * (j + 1)] = lse2

    before = lambda n: jnp.maximum(n * nq - 1, 0)
    return pl.pallas_call(
        body, name="attn_a_fwd", grid=(steps,),
        in_specs=[SMEM,
                  pl.BlockSpec((rows, 512), lambda n: (n, 0)),
                  pl.BlockSpec((BLK, 128), lambda n: (before(n), 4)), pl.BlockSpec((rows, 128), lambda n: (n, 4)),
                  pl.BlockSpec((BLK, 128), lambda n: (before(n), 5)), pl.BlockSpec((rows, 128), lambda n: (n, 5))],
        out_specs=[pl.BlockSpec((rows, 512), lambda n: (n, 0))] * 2,
        out_shape=[jax.ShapeDtypeStruct((s, 512), F32)] * 2,
        compiler_params=_cp(("parallel",)),
    )(sinks, proj, proj, proj, proj, proj)


def _attn_a_bwd(proj, sinks, d_o, o, lse):
    s = proj.shape[0]
    nq = A_BLOCKS_PER_STEP_BWD
    rows = BLK * nq
    steps = s // rows

    def body(sink_ref, q_ref, kp_ref, kc_ref, vp_ref, vc_ref, do_ref, o_ref, lse_ref,
             dq_ref, dk_ref, dv_ref, dsink_ref, kcar, vcar):
        n = pl.program_id(0)

        @pl.when(n == 0)
        def _():
            kcar[...] = jnp.zeros_like(kcar)
            vcar[...] = jnp.zeros_like(vcar)
            dsink_ref[...] = jnp.zeros_like(dsink_ref)

        dk_ref[...] = kcar[...]
        dv_ref[...] = vcar[...]

        @pl.when(n < steps)
        def _():
            base_rest = _band_base(A_MAX_DIST, 1, False)
            base_0 = jnp.where(n > 0, base_rest, _band_base(A_MAX_DIST, 1, True))
            for i in range(nq):
                cur = pl.ds(i * BLK, BLK)
                k_prev = kc_ref[pl.ds((i - 1) * BLK, BLK), :] if i > 0 else kp_ref[...]
                v_prev = vc_ref[pl.ds((i - 1) * BLK, BLK), :] if i > 0 else vp_ref[...]
                kb = jnp.concatenate([k_prev, kc_ref[cur, :]], axis=0).astype(BF16)
                vb = jnp.concatenate([v_prev, vc_ref[cur, :]], axis=0).astype(BF16)
                dk_win = dv_win = None
                for j in range(NH // 2):
                    g = j // 2
                    sl = slice(128 * j, 128 * (j + 1))
                    dq2, dk2, dv2, dsk = _pair_bwd(q_ref[cur, sl], kb, vb, do_ref[cur, sl], o_ref[cur, sl],
                                                   lse_ref[cur, sl], base_rest if i > 0 else base_0,
                                                   (SLOPES[2 * j], SLOPES[2 * j + 1]), (g, g),
                                                   (sink_ref[2 * j], sink_ref[2 * j + 1]))
                    dq_ref[cur, sl] = dq2
                    dk_win = dk2 if j == 0 else dk_win + dk2
                    dv_win = dv2 if j == 0 else dv_win + dv2
                    for e in (0, 1):
                        h = 2 * j + e
                        dsink_ref[h:h + 1, :] += jnp.broadcast_to(dsk[e], (1, 128))
                if i == 0:
                    last = pl.ds((nq - 1) * BLK, BLK)
                    dk_ref[last, :] += dk_win[:BLK]
                    dv_ref[last, :] += dv_win[:BLK]
                else:
                    kcar[pl.ds((i - 1) * BLK, BLK), :] += dk_win[:BLK]
                    vcar[pl.ds((i - 1) * BLK, BLK), :] += dv_win[:BLK]
                kcar[cur, :] = dk_win[BLK:]
                vcar[cur, :] = dv_win[BLK:]

    cur_step = lambda n: jnp.minimum(n, steps - 1)
    before = lambda n: jnp.maximum(cur_step(n) * nq - 1, 0)
    out_prev = lambda n: jnp.maximum(n - 1, 0)
    wide = pl.BlockSpec((rows, 512), lambda n: (cur_step(n), 0))
    return pl.pallas_call(
        body, name="attn_a_bwd", grid=(steps + 1,),
        in_specs=[SMEM, wide,
                  pl.BlockSpec((BLK, 128), lambda n: (before(n), 4)), pl.BlockSpec((rows, 128), lambda n: (cur_step(n), 4)),
                  pl.BlockSpec((BLK, 128), lambda n: (before(n), 5)), pl.BlockSpec((rows, 128), lambda n: (cur_step(n), 5)),
                  wide, wide, wide],
        out_specs=[wide,
                   pl.BlockSpec((rows, 128), lambda n: (out_prev(n), 0)),
                   pl.BlockSpec((rows, 128), lambda n: (out_prev(n), 0)),
                   pl.BlockSpec((NH, 128), lambda n: (0, 0))],
        out_shape=[jax.ShapeDtypeStruct((s, 512), F32), jax.ShapeDtypeStruct((s, 128), F32),
                   jax.ShapeDtypeStruct((s, 128), F32), jax.ShapeDtypeStruct((NH, 128), F32)],
        scratch_shapes=[pltpu.VMEM((rows, 128), F32), pltpu.VMEM((rows, 128), F32)],
        compiler_params=_cp(("arbitrary",)),
    )(sinks, proj, proj, proj, proj, proj, d_o, o, lse)


def _stream(rho, i, r):
    start = i * BLK * r + rho
    return pl.ds(start, BLK, stride=r) if r > 1 else pl.ds(start, BLK)


def _for_streams(r, fn, side_by_side=4):
    if r <= side_by_side:
        for rho in range(r):
            fn(rho)
    else:
        def group(it, carry):
            for u in range(side_by_side):
                fn(side_by_side * it + u)
            return carry

        lax.fori_loop(0, r // side_by_side, group, 0)


B_BLOCKS_PER_STEP = {1: 8, 4: 2, 16: 1}
B_BLOCKS_PER_STEP_FWD = {1: 8, 4: 2, 16: 1}


def _attn_b_fwd(proj, slopes, r):
    s = proj.shape[0]
    nq = B_BLOCKS_PER_STEP_FWD[r]
    rows = BLK * r * nq
    steps = s // rows
    qc, kc, vc = WA // 128, WA // 128 + 4, WA // 128 + 8

    def body(slope_ref, q_ref, kp_ref, kc_ref, vp_ref, vc_ref, o_ref, lse_ref):
        j = pl.program_id(0)
        sb = pl.program_id(1)
        sl2 = (slope_ref[2 * j], slope_ref[2 * j + 1])
        bias_rest = _stack_heads(sl2, _band_base(B_MAX_DIST, r, False))
        bias_0 = jnp.where(sb > 0, bias_rest, _stack_heads(sl2, _band_base(B_MAX_DIST, r, True)))

        def stream(rho):
            for i in range(nq):
                cur = _stream(rho, i, r)
                k_prev = kc_ref[_stream(rho, i - 1, r), :] if i > 0 else kp_ref[_stream(rho, 0, r), :]
                v_prev = vc_ref[_stream(rho, i - 1, r), :] if i > 0 else vp_ref[_stream(rho, 0, r), :]
                kb = jnp.concatenate([k_prev, kc_ref[cur, :]], axis=0).astype(BF16)
                vb = jnp.concatenate([v_prev, vc_ref[cur, :]], axis=0).astype(BF16)
                o2, lse2 = _pair_fwd(q_ref[cur, :], kb, vb, bias_rest if i > 0 else bias_0, None, (0, 1), None)
                o_ref[cur, :] = o2
                lse_ref[cur, :] = lse2

        _for_streams(r, stream, side_by_side=8)

    before = lambda sb: jnp.maximum(sb * nq - 1, 0)
    return pl.pallas_call(
        body, name=f"attn_b_fwd_r{r}", grid=(NH // 2, steps),
        in_specs=[SMEM,
                  pl.BlockSpec((rows, 128), lambda j, sb: (sb, qc + j)),
                  pl.BlockSpec((BLK * r, 128), lambda j, sb: (before(sb), kc + j)),
                  pl.BlockSpec((rows, 128), lambda j, sb: (sb, kc + j)),
                  pl.BlockSpec((BLK * r, 128), lambda j, sb: (before(sb), vc + j)),
                  pl.BlockSpec((rows, 128), lambda j, sb: (sb, vc + j))],
        out_specs=[pl.BlockSpec((rows, 128), lambda j, sb: (sb, j))] * 2,
        out_shape=[jax.ShapeDtypeStruct((s, 512), F32)] * 2,
        compiler_params=_cp(("parallel", "parallel")),
    )(slopes, proj, proj, proj, proj, proj)


def _attn_b_bwd(proj, slopes, d_o, o, lse, r, so_far=None):
    s = proj.shape[0]
    nq = B_BLOCKS_PER_STEP[r]
    rows = BLK * r * nq
    steps = s // rows
    qc, kc, vc = WA // 128, WA // 128 + 4, WA // 128 + 8
    chained = so_far is not None

    def body(slope_ref, q_ref, kp_ref, kc_ref, vp_ref, vc_ref, do_ref, o_ref, lse_ref, *rest):
        if chained:
            pq_ref, pk_ref, pv_ref, dq_ref, dk_ref, dv_ref, kcar, vcar = rest
        else:
            dq_ref, dk_ref, dv_ref, kcar, vcar = rest
        j = pl.program_id(0)
        sb = pl.program_id(1)

        @pl.when(sb == 0)
        def _():
            kcar[...] = jnp.zeros_like(kcar)
            vcar[...] = jnp.zeros_like(vcar)

        if chained:
            dk_ref[...] = kcar[...] + pk_ref[...]
            dv_ref[...] = vcar[...] + pv_ref[...]
        else:
            dk_ref[...] = kcar[...]
            dv_ref[...] = vcar[...]

        @pl.when(sb < steps)
        def _():
            sl2 = (slope_ref[2 * j], slope_ref[2 * j + 1])
            bias_rest = _stack_heads(sl2, _band_base(B_MAX_DIST, r, False))
            bias_0 = jnp.where(sb > 0, bias_rest, _stack_heads(sl2, _band_base(B_MAX_DIST, r, True)))

            def stream(rho):
                for i in range(nq):
                    cur = _stream(rho, i, r)
                    k_prev = kc_ref[_stream(rho, i - 1, r), :] if i > 0 else kp_ref[_stream(rho, 0, r), :]
                    v_prev = vc_ref[_stream(rho, i - 1, r), :] if i > 0 else vp_ref[_stream(rho, 0, r), :]
                    kb = jnp.concatenate([k_prev, kc_ref[cur, :]], axis=0).astype(BF16)
                    vb = jnp.concatenate([v_prev, vc_ref[cur, :]], axis=0).astype(BF16)
                    dq2, dk2, dv2, _ = _pair_bwd(q_ref[cur, :], kb, vb, do_ref[cur, :], o_ref[cur, :], lse_ref[cur, :],
                                                 bias_rest if i > 0 else bias_0, None, (0, 1), None)
                    dq_ref[cur, :] = dq2 + pq_ref[cur, :] if chained else dq2
                    if i == 0:
                        last = _stream(rho, nq - 1, r)
                        dk_ref[last, :] += dk2[:BLK]
                        dv_ref[last, :] += dv2[:BLK]
                    else:
                        kcar[_stream(rho, i - 1, r), :] += dk2[:BLK]
                        vcar[_stream(rho, i - 1, r), :] += dv2[:BLK]
                    kcar[cur, :] = dk2[BLK:]
                    vcar[cur, :] = dv2[BLK:]

            _for_streams(r, stream, side_by_side=8)

    cur_step = lambda sb: jnp.minimum(sb, steps - 1)
    before = lambda sb: jnp.maximum(cur_step(sb) * nq - 1, 0)
    out_prev = lambda sb: jnp.maximum(sb - 1, 0)
    tile = lambda col: pl.BlockSpec((rows, 128), lambda j, sb: (cur_step(sb), col + j))
    edge = lambda col: pl.BlockSpec((BLK * r, 128), lambda j, sb: (before(sb), col + j))
    late = pl.BlockSpec((rows, 128), lambda j, sb: (out_prev(sb), j))
    grads = [tile(0), late, late]
    return pl.pallas_call(
        body, name=f"attn_b_bwd_r{r}", grid=(NH // 2, steps + 1),
        in_specs=[SMEM, tile(qc), edge(kc), tile(kc), edge(vc), tile(vc), tile(0), tile(0), tile(0)]
        + (grads if chained else []),
        out_specs=grads,
        out_shape=[jax.ShapeDtypeStruct((s, 512), F32)] * 3,
        scratch_shapes=[pltpu.VMEM((rows, 128), F32), pltpu.VMEM((rows, 128), F32)],
        compiler_params=_cp(("parallel", "arbitrary")),
    )(slopes, proj, proj, proj, proj, proj, d_o, o, lse, *(so_far if chained else ()))


def _row(v):
    return v.reshape(1, -1)


def _layer_norm_stats(z):
    mu = jnp.mean(z, axis=-1, keepdims=True)
    zc = z - mu
    var = jnp.mean(zc * zc, axis=-1, keepdims=True)
    rstd = lax.rsqrt(var + LN_EPS)
    return zc * rstd, rstd


def _layer_norm_bwd(dh, zh, rstd, g):
    dzh = dh * g
    return rstd * (dzh - jnp.mean(dzh, axis=-1, keepdims=True) - zh * jnp.mean(dzh * zh, axis=-1, keepdims=True))


def _rms(o):
    return lax.rsqrt(jnp.mean(o * o, axis=-1, keepdims=True) + RMS_EPS)


def _mix_ln1(x, o_a, o_b, lse_b, norm_a_g, norm_b_g, w_o, ln1_g, ln1_b, tm=256):
    s = x.shape[0]

    def body(x_ref, oa_ref, ob1, ob2, ob3, l1, l2, l3, ga_ref, gb_ref, wo_ref, g_ref, b_ref,
             obm_ref, lse_ref, cat_ref, z1_ref, h1_ref, h1b_ref):
        la, lb, lc = l1[...], l2[...], l3[...]
        m = jnp.maximum(jnp.maximum(la, lb), lc)
        ea, eb, ec = jnp.exp(la - m), jnp.exp(lb - m), jnp.exp(lc - m)
        den = ea + eb + ec
        obm = (ea / den) * ob1[...] + (eb / den) * ob2[...] + (ec / den) * ob3[...]
        obm_ref[...] = obm
        lse_ref[...] = m + jnp.log(den)
        oa = oa_ref[...]
        na = oa * _rms(oa) * ga_ref[...]
        nb_ = obm * _rms(obm) * gb_ref[...]
        cat = jnp.concatenate([na, nb_], axis=1).astype(BF16)
        cat_ref[...] = cat
        z1 = ALPHA * x_ref[...] + _nn(cat, wo_ref[...])
        z1_ref[...] = z1
        zh, _ = _layer_norm_stats(z1)
        h1 = zh * g_ref[...] + b_ref[...]
        h1_ref[...] = h1
        h1b_ref[...] = h1.astype(BF16)

    t512 = pl.BlockSpec((tm, 512), lambda i: (i, 0))
    td = pl.BlockSpec((tm, D), lambda i: (i, 0))
    return pl.pallas_call(
        body, name="mix_ln1", grid=(s // tm,),
        in_specs=[td] + [t512] * 7 + [_const((1, 512))] * 2 + [_resident((D, D))] + [_const((1, D))] * 2,
        out_specs=[t512, t512, td, td, td, td],
        out_shape=[jax.ShapeDtypeStruct((s, 512), F32), jax.ShapeDtypeStruct((s, 512), F32),
                   jax.ShapeDtypeStruct((s, D), BF16), jax.ShapeDtypeStruct((s, D), F32),
                   jax.ShapeDtypeStruct((s, D), F32), jax.ShapeDtypeStruct((s, D), BF16)],
        compiler_params=_cp(("parallel",)),
    )(x, o_a, *o_b, *lse_b, _row(norm_a_g), _row(norm_b_g), w_o, _row(ln1_g), _row(ln1_b))


def _gelu_and_grad(x):
    c = math.sqrt(2.0 / math.pi)
    x2 = x * x
    cx = c * x
    t = jnp.tanh(cx * (1.0 + 0.044715 * x2))
    q = 1.0 + t
    g = (0.5 * x) * q
    dg = 0.5 * q + ((0.5 * cx) * (1.0 - t * t)) * (1.0 + (3.0 * 0.044715) * x2)
    return g, dg


CONV_CHUNK = 64


def _shift_down(u, before):
    n = u.shape[0]
    ext = jnp.concatenate([before, u], axis=0)
    return pltpu.roll(ext, 1, 0)[8:], pltpu.roll(ext, 2, 0)[8:]


def _shift_up(u, after):
    n = u.shape[0]
    ext = jnp.concatenate([u, after], axis=0)
    return pltpu.roll(ext, n + 7, 0)[:n], pltpu.roll(ext, n + 6, 0)[:n]


def _up_conv_gelu(h1b, w_up, cwb, tm=512, tn=256):
    s = h1b.shape[0]
    n_i = s // tm
    n_t = (FF // tn) * n_i

    def body(h_ref, wg_ref, wv_ref, c_ref, up_ref, a_ref, g_ref, a1_ref, pend_a, pend_b, carry):
        t = pl.program_id(0)
        row_tile = jnp.maximum(t - 1, 0) % n_i
        w_refs = (wg_ref, wv_ref)

        @pl.when(t == 0)
        def _():
            pend_b[...] = jnp.zeros_like(pend_b)
            carry[...] = jnp.zeros_like(carry)

        def step(dst, src):
            def chunk(c, before):
                rows = pl.ds(c * CONV_CHUNK, CONV_CHUNK)
                u, last = [], []
                for half in (0, 1):
                    up = src[half, rows, :]
                    r1, r2 = _shift_down(up, before[half])
                    u.append(r2 * c_ref[0, half:half + 1, :] + r1 * c_ref[1, half:half + 1, :]
                             + up * c_ref[2, half:half + 1, :] + c_ref[3, half:half + 1, :])
                    last.append(up[CONV_CHUNK - 8:])
                g, dg = _gelu_and_grad(u[0])
                a_ref[rows, :] = (g * u[1]).astype(BF16)
                g_ref[rows, :] = g.astype(BF16)
                a1_ref[rows, :] = (u[1] * dg).astype(BF16)
                return tuple(last)

            edge = tuple(jnp.where(row_tile > 0, carry[half], 0.0) for half in (0, 1))
            n_c = tm // CONV_CHUNK
            n_k = n_c // 2
            tk = D // n_k
            for half in (0, 1):
                up = None
                for kq in range(n_k):
                    ks = slice(kq * tk, (kq + 1) * tk)
                    part = _nn(h_ref[:, ks], w_refs[half][ks, :])
                    up = part if kq == 0 else up + part
                    edge = chunk(half * n_k + kq, edge)
                up_ref[half] = up
                dst[half] = up
            for half in (0, 1):
                carry[half] = edge[half]

        @pl.when(t % 2 == 0)
        def _():
            step(pend_a, pend_b)

        @pl.when(t % 2 == 1)
        def _():
            step(pend_b, pend_a)

    mm = lambda t: jnp.minimum(t, n_t - 1)
    ew = lambda t: jnp.maximum(t - 1, 0)
    out_tile = pl.BlockSpec((tm, tn), lambda t: (ew(t) % n_i, ew(t) // n_i))
    return pl.pallas_call(
        body, name="up_conv_gelu", grid=(n_t + 1,),
        in_specs=[pl.BlockSpec((tm, D), lambda t: (mm(t) % n_i, 0)),
                  pl.BlockSpec((D, tn), lambda t: (0, mm(t) // n_i)),
                  pl.BlockSpec((D, tn), lambda t: (0, FF // tn + mm(t) // n_i)),
                  pl.BlockSpec((4, 2, tn), lambda t: (0, 0, ew(t) // n_i))],
        out_specs=[pl.BlockSpec((2, tm, tn), lambda t: (0, mm(t) % n_i, mm(t) // n_i)), out_tile, out_tile, out_tile],
        out_shape=[jax.ShapeDtypeStruct((2, s, FF), F32)] + [jax.ShapeDtypeStruct((s, FF), BF16)] * 3,
        scratch_shapes=[pltpu.VMEM((2, tm, tn), F32), pltpu.VMEM((2, tm, tn), F32), pltpu.VMEM((2, 8, tn), F32)],
        compiler_params=_cp(("arbitrary",)),
    )(h1b, w_up, w_up, cwb)


def _down_ln2_loss(a, w_down, h1, target, ln2_g, ln2_b, tm=256):
    s = a.shape[0]

    def body(a_ref, w_ref, h_ref, t_ref, g_ref, b_ref, dz_ref, dzb_ref, st_ref):
        @pl.when(pl.program_id(0) == 0)
        def _():
            st_ref[...] = jnp.zeros_like(st_ref)

        z2 = ALPHA * h_ref[...] + _nn(a_ref[...], w_ref[...])
        zh, rstd = _layer_norm_stats(z2)
        diff = zh * g_ref[...] + b_ref[...] - t_ref[...]
        part = 0.5 * jnp.sum(jnp.mean(diff * diff, axis=-1, keepdims=True), axis=0, keepdims=True)
        dy = diff * (1.0 / D)
        st_ref[0:1, :] += jnp.sum(dy * zh, axis=0, keepdims=True)
        st_ref[1:2, :] += jnp.sum(dy, axis=0, keepdims=True)
        st_ref[2:3, :] += jnp.broadcast_to(part, (1, D))
        dz = _layer_norm_bwd(dy, zh, rstd, g_ref[...])
        dz_ref[...] = dz
        dzb_ref[...] = dz.astype(BF16)

    td = pl.BlockSpec((tm, D), lambda i: (i, 0))
    return pl.pallas_call(
        body, name="down_ln2_loss", grid=(s // tm,),
        in_specs=[pl.BlockSpec((tm, FF), lambda i: (i, 0)), _resident((FF, D)), td, td, _const((1, D)), _const((1, D))],
        out_specs=[td, td, _const((8, D))],
        out_shape=[jax.ShapeDtypeStruct((s, D), F32), jax.ShapeDtypeStruct((s, D), BF16),
                   jax.ShapeDtypeStruct((8, D), F32)],
        compiler_params=_cp(("arbitrary",)),
    )(a, w_down, h1, target, _row(ln2_g), _row(ln2_b))


def _conv_gelu_bwd(dz2b, w_down_t, up, g, a1, cwb, tm=512, tn=256):
    s = dz2b.shape[0]
    n_i = s // tm
    n_t = (FF // tn) * n_i

    def body(dz_ref, w_ref, up_ref, g_ref, a1_ref, c_ref, dup_ref, dc_ref, pend_a, pend_b, carry):
        t = pl.program_id(0)
        first = jnp.maximum(t - 1, 0) % n_i == 0

        @pl.when(t == 0)
        def _():
            pend_b[...] = jnp.zeros_like(pend_b)

        @pl.when(first)
        def _():
            carry[...] = jnp.zeros_like(carry)
            dc_ref[...] = jnp.zeros_like(dc_ref)

        def step(dst, src):
            n_c = tm // CONV_CHUNK

            def fold(v):
                return jnp.sum(v.reshape(CONV_CHUNK // 8, 8, v.shape[1]), axis=0)

            def chunk(cc, state):
                after, sums = state
                rows = pl.ds((n_c - 1 - cc) * CONV_CHUNK, CONV_CHUNK)
                da = src[rows, :]
                dus = (da * a1_ref[rows, :].astype(F32), da * g_ref[rows, :].astype(F32))
                head, new_sums = [], []
                for half in (0, 1):
                    du = dus[half]
                    up = up_ref[half, rows, :]
                    l1, l2 = _shift_up(du, after[half])
                    dup = (du * c_ref[2, half:half + 1, :] + l1 * c_ref[1, half:half + 1, :]
                           + l2 * c_ref[0, half:half + 1, :])
                    dup_ref[half, rows, :] = dup.astype(BF16)
                    parts = (fold(l2 * up), fold(l1 * up), fold(du * up), fold(du))
                    new_sums.append(parts if sums is None else tuple(a + b for a, b in zip(sums[half], parts)))
                    head.append(du[:8])
                return tuple(head), new_sums

            state = ((carry[0], carry[1]), None)
            n_k = n_c // 2
            tk = D // n_k
            da = None
            for kq in range(n_k):
                ks = slice(kq * tk, (kq + 1) * tk)
                part = _nn(dz_ref[:, ks], w_ref[ks, :])
                da = part if kq == 0 else da + part
                state = chunk(2 * kq, state)
                state = chunk(2 * kq + 1, state)
            head, sums = state
            for half in (0, 1):
                carry[half] = head[half]
                for k in range(4):
                    dc_ref[k, half:half + 1, :] += jnp.sum(sums[half][k], axis=0, keepdims=True)
            dst[...] = da

        @pl.when(t % 2 == 0)
        def _():
            step(pend_a, pend_b)

        @pl.when(t % 2 == 1)
        def _():
            step(pend_b, pend_a)

    mm = lambda t: jnp.minimum(t, n_t - 1)
    ew = lambda t: jnp.maximum(t - 1, 0)
    row = lambda t: n_i - 1 - t % n_i
    ew_tile = pl.BlockSpec((tm, tn), lambda t: (row(ew(t)), ew(t) // n_i))
    ew_pair = pl.BlockSpec((2, tm, tn), lambda t: (0, row(ew(t)), ew(t) // n_i))
    per_col = pl.BlockSpec((4, 2, tn), lambda t: (0, 0, ew(t) // n_i))
    return pl.pallas_call(
        body, name="conv_gelu_bwd", grid=(n_t + 1,),
        in_specs=[pl.BlockSpec((tm, D), lambda t: (row(mm(t)), 0)),
                  pl.BlockSpec((D, tn), lambda t: (0, mm(t) // n_i)),
                  ew_pair, ew_tile, ew_tile, per_col],
        out_specs=[ew_pair, per_col],
        out_shape=[jax.ShapeDtypeStruct((2, s, FF), BF16), jax.ShapeDtypeStruct((4, 2, FF), F32)],
        scratch_shapes=[pltpu.VMEM((tm, tn), F32), pltpu.VMEM((tm, tn), F32), pltpu.VMEM((2, 8, tn), F32)],
        compiler_params=_cp(("arbitrary",)),
    )(dz2b, w_down_t, up, g, a1, cwb)


def _dh1_ln1_bwd(dz2, dup, w_up, z1, ln1_g, tm=256):
    s = dz2.shape[0]

    def body(dz2_ref, dup_ref, w_ref, z1_ref, g_ref, dz1_ref, dz1b_ref, st_ref):
        @pl.when(pl.program_id(0) == 0)
        def _():
            st_ref[...] = jnp.zeros_like(st_ref)

        dh = ALPHA * dz2_ref[...] + _nt(dup_ref[0], w_ref[:, :FF]) + _nt(dup_ref[1], w_ref[:, FF:])
        zh, rstd = _layer_norm_stats(z1_ref[...])
        st_ref[0:1, :] += jnp.sum(dh * zh, axis=0, keepdims=True)
        st_ref[1:2, :] += jnp.sum(dh, axis=0, keepdims=True)
        dz = _layer_norm_bwd(dh, zh, rstd, g_ref[...])
        dz1_ref[...] = dz
        dz1b_ref[...] = dz.astype(BF16)

    td = pl.BlockSpec((tm, D), lambda i: (i, 0))
    return pl.pallas_call(
        body, name="dh1_ln1_bwd", grid=(s // tm,),
        in_specs=[td, pl.BlockSpec((2, tm, FF), lambda i: (0, i, 0)), _resident((D, 2 * FF)), td, _const((1, D))],
        out_specs=[td, td, _const((8, D))],
        out_shape=[jax.ShapeDtypeStruct((s, D), F32), jax.ShapeDtypeStruct((s, D), BF16),
                   jax.ShapeDtypeStruct((8, D), F32)],
        compiler_params=_cp(("arbitrary",)),
    )(dz2, dup, w_up, z1, _row(ln1_g))


def _dcat_rms_bwd(dz1b, w_o, o_a, o_b, norm_a_g, norm_b_g, tm=256):
    s = dz1b.shape[0]

    def body(dz_ref, w_ref, oa_ref, ob_ref, ga_ref, gb_ref, da_ref, db_ref, st_ref):
        @pl.when(pl.program_id(0) == 0)
        def _():
            st_ref[...] = jnp.zeros_like(st_ref)

        dcat = _nt(dz_ref[...], w_ref[...])
        for k, (o_ref, g_ref, d_ref) in enumerate(((oa_ref, ga_ref, da_ref), (ob_ref, gb_ref, db_ref))):
            o = o_ref[...]
            dn = dcat[:, 512 * k:512 * (k + 1)]
            rr = _rms(o)
            oh = o * rr
            st_ref[k:k + 1, :] += jnp.sum(dn * oh, axis=0, keepdims=True)
            doh = dn * g_ref[...]
            d_ref[...] = rr * (doh - oh * jnp.mean(doh * oh, axis=-1, keepdims=True))

    t512 = pl.BlockSpec((tm, 512), lambda i: (i, 0))
    return pl.pallas_call(
        body, name="dcat_rms_bwd", grid=(s // tm,),
        in_specs=[pl.BlockSpec((tm, D), lambda i: (i, 0)), _resident((D, D)), t512, t512,
                  _const((1, 512)), _const((1, 512))],
        out_specs=[t512, t512, _const((8, 512))],
        out_shape=[jax.ShapeDtypeStruct((s, 512), F32), jax.ShapeDtypeStruct((s, 512), F32),
                   jax.ShapeDtypeStruct((8, 512), F32)],
        compiler_params=_cp(("arbitrary",)),
    )(dz1b, w_o, o_a, o_b, _row(norm_a_g), _row(norm_b_g))


def _dproj_combine(dqa, dka, dva, dqkv_b, tm=256):
    s = dqa.shape[0]

    def body(qa, ka, va, qb, kb, vb, o_ref):
        o_ref[:, 0:512] = qa[...].astype(BF16)
        o_ref[:, 512:640] = ka[...].astype(BF16)
        o_ref[:, 640:768] = va[...].astype(BF16)
        o_ref[:, 768:1280] = qb[...].astype(BF16)
        o_ref[:, 1280:1792] = kb[...].astype(BF16)
        o_ref[:, 1792:2304] = vb[...].astype(BF16)

    t512 = pl.BlockSpec((tm, 512), lambda i: (i, 0))
    t128 = pl.BlockSpec((tm, 128), lambda i: (i, 0))
    return pl.pallas_call(
        body, name="dproj_combine", grid=(s // tm,),
        in_specs=[t512, t128, t128] + [t512] * 3,
        out_specs=pl.BlockSpec((tm, WIN), lambda i: (i, 0)),
        out_shape=jax.ShapeDtypeStruct((s, WIN), BF16),
        compiler_params=_cp(("parallel",)),
    )(dqa, dka, dva, *dqkv_b)


def _grad_x(dz1, dproj, w_in_t, zero, tm=256):
    s = dz1.shape[0]

    def body(dz_ref, dp_ref, w_ref, z_ref, o_ref):
        o_ref[...] = ALPHA * dz_ref[...] + _nn(dp_ref[...], w_ref[...]) + z_ref[0:1, 0:1]

    td = pl.BlockSpec((tm, D), lambda i: (i, 0))
    return pl.pallas_call(
        body, name="grad_x", grid=(s // tm,),
        in_specs=[td, pl.BlockSpec((tm, WIN), lambda i: (i, 0)), _resident((WIN, D)), _const((8, 128))],
        out_specs=td, out_shape=jax.ShapeDtypeStruct((s, D), F32),
        compiler_params=_cp(("parallel",)),
    )(dz1, dproj, w_in_t, zero)


def _place():
    return lax.axis_index("x"), lax.axis_index("y"), lax.axis_index("c")


def _other_chips(x, y):
    return [(1 - x, y), (x, 1 - y), (1 - x, 1 - y)]


def _hbm(a):
    return pltpu.with_memory_space_constraint(a, pltpu.HBM)


def _gather_w_in(shard, conv_w):
    rows_k = shard.shape[0]
    half = rows_k // 2

    def body(src, conv_src, out, conv_out, send_sems, recv_sems):
        x, y, c = _place()
        b = 2 * x + y
        sibling = (x, y, 1 - c)
        chips = _other_chips(x, y)

        def copy(idx, chip_b, core, to, first_hop=False):
            rows = out.at[pl.ds(pl.multiple_of(chip_b * rows_k + core * half, 16), half)]
            s_ref = src.at[pl.ds(pl.multiple_of(core * half, 16), half)] if first_hop else rows
            return pltpu.make_async_remote_copy(src_ref=s_ref, dst_ref=rows, send_sem=send_sems.at[idx],
                                                recv_sem=recv_sems.at[idx], device_id=to, device_id_type=MESH)

        def own_copy():
            return pltpu.make_async_remote_copy(
                src_ref=src, dst_ref=out.at[pl.ds(pl.multiple_of(b * rows_k, 16), rows_k)], send_sem=send_sems.at[6],
                recv_sem=recv_sems.at[6], device_id=sibling, device_id_type=MESH)

        def conv_copy(idx, chip_b, to):
            return pltpu.make_async_remote_copy(src_ref=conv_src, dst_ref=conv_out.at[chip_b],
                                                send_sem=send_sems.at[7 + idx], recv_sem=recv_sems.at[7 + idx],
                                                device_id=to, device_id_type=MESH)

        started = [own_copy(), conv_copy(3, b, sibling)]
        for jn, chip in enumerate(chips):
            started += [copy(jn, b, c, (chip[0], chip[1], c), first_hop=True), conv_copy(jn, b, (chip[0], chip[1], c))]
        for cp in started:
            cp.start()
        for jn, chip in enumerate(chips):
            cb = 2 * chip[0] + chip[1]
            copy(jn, cb, c, (chip[0], chip[1], c)).wait_recv()
            cp = copy(3 + jn, cb, c, sibling)
            cp.start()
            started.append(cp)
        for jn, chip in enumerate(chips):
            cb = 2 * chip[0] + chip[1]
            copy(3 + jn, cb, 1 - c, sibling).wait_recv()
            conv_copy(jn, cb, (chip[0], chip[1], c)).wait_recv()
        own_copy().wait_recv()
        conv_copy(3, b, sibling).wait_recv()
        for cp in started:
            cp.wait_send()

    return pl.pallas_call(
        body, name="gather_w_in",
        in_specs=[ANY, ANY], out_specs=[ANY, ANY],
        out_shape=[jax.ShapeDtypeStruct((N_CHIPS * rows_k, D), BF16), jax.ShapeDtypeStruct((N_CHIPS,) + conv_w.shape, F32)],
        scratch_shapes=[pltpu.SemaphoreType.DMA((11,)), pltpu.SemaphoreType.DMA((11,))],
        compiler_params=pltpu.CompilerParams(has_side_effects=True),
    )(shard, conv_w)


def _weight_copies(shard, land, send_sems, recv_sems, arrivals):
    x, y, c = _place()
    n_rows, n_cols = shard.shape
    peers = [(px, py, c) for px, py in _other_chips(x, y)] + [(x, y, 1 - c)]
    cps = []
    for jn, peer in enumerate(peers):
        at = 2 * peer[0] + peer[1] if arrivals else 2 * x + y
        if land.shape[1] == n_cols:
            dst = land.at[pl.ds(pl.multiple_of(at * n_rows, 16), n_rows)]
        else:
            dst = land.at[:, pl.ds(pl.multiple_of(at * n_cols, 128), n_cols)]
        cps.append(pltpu.make_async_remote_copy(src_ref=shard, dst_ref=dst, send_sem=send_sems.at[jn],
                                                recv_sem=recv_sems.at[jn], device_id=peer, device_id_type=MESH))
    return cps


def _weights_start(shards, after):
    n = len(shards)
    lands = [lax.empty((N_CHIPS * sh.shape[0], D) if sh.shape[1] == D else (D, N_CHIPS * sh.shape[1]), BF16)
             for sh in shards]

    def body(*refs):
        src, land = refs[:n], refs[n:2 * n]
        send_sems, recv_sems = refs[2 * n + 1:3 * n + 1], refs[3 * n + 1:4 * n + 1]
        for k in range(n):
            for send in _weight_copies(src[k], land[k], send_sems[k], recv_sems[k], False):
                send.start()
        refs[-1][...] = jnp.zeros_like(refs[-1])

    res = pl.pallas_call(
        body, name="weights_start",
        in_specs=[HBM] * (2 * n) + [ANY], out_specs=[SEM] * (2 * n) + [HBM] * (2 * n) + [VMEM],
        out_shape=[pltpu.SemaphoreType.DMA((4,))] * (2 * n)
        + [pltpu.HBM(a.shape, a.dtype) for a in (*shards, *lands)] + [jax.ShapeDtypeStruct((8, 128), F32)],
        input_output_aliases={i: i + 2 * n for i in range(2 * n)},
        compiler_params=pltpu.CompilerParams(has_side_effects=DATAFLOW),
    )(*[_hbm(a) for a in (*shards, *lands)], after)
    return [(res[k], res[n + k], res[2 * n + k], res[3 * n + k]) for k in range(n)], res[-1]


def _weights_wait(started, after, name):
    send_sems, recv_sems, shard, land = started

    def body(s_ref, l_ref, send_ref, recv_ref, after_ref, s_out, l_out):
        for cp in _weight_copies(s_ref, l_ref, send_ref, recv_ref, True):
            cp.wait_send()
            cp.wait_recv()

    return pl.pallas_call(
        body, name=name,
        in_specs=[HBM, HBM, SEM, SEM, ANY], out_specs=[HBM, HBM],
        out_shape=[pltpu.HBM(shard.shape, shard.dtype), pltpu.HBM(land.shape, land.dtype)],
        input_output_aliases={0: 0, 1: 1},
        compiler_params=pltpu.CompilerParams(has_side_effects=DATAFLOW),
    )(shard, land, send_sems, recv_sems, after)[1]


def _grad_copies(g_ref, land_ref, send_sems, recv_sems):
    x, y, c = _place()
    cps = []
    for d in range(1, 8):
        px, py, pc = x ^ (d >> 2), y ^ ((d >> 1) & 1), c ^ (d & 1)
        cps.append(pltpu.make_async_remote_copy(
            src_ref=g_ref.at[2 * px + py, pc], dst_ref=land_ref.at[d - 1], send_sem=send_sems.at[d - 1],
            recv_sem=recv_sems.at[d - 1], device_id=(px, py, pc), device_id_type=MESH))
    return cps


def _grads_start(grads_b, name):
    n = len(grads_b)
    lands = [lax.empty((7, g.shape[2], D), BF16) for g in grads_b]

    def body(*refs):
        g, land = refs[:n], refs[n:2 * n]
        send_sems, recv_sems = refs[2 * n:3 * n], refs[3 * n:4 * n]
        for k in range(n):
            for cp in _grad_copies(g[k], land[k], send_sems[k], recv_sems[k]):
                cp.start()
        refs[-1][...] = jnp.zeros_like(refs[-1])

    res = pl.pallas_call(
        body, name=name,
        in_specs=[HBM] * (2 * n), out_specs=[SEM] * (2 * n) + [HBM] * (2 * n) + [VMEM],
        out_shape=[pltpu.SemaphoreType.DMA((7,))] * (2 * n)
        + [pltpu.HBM(a.shape, a.dtype) for a in (*grads_b, *lands)] + [jax.ShapeDtypeStruct((8, 128), F32)],
        input_output_aliases={i: i + 2 * n for i in range(2 * n)},
        compiler_params=pltpu.CompilerParams(has_side_effects=DATAFLOW),
    )(*[_hbm(a) for a in (*grads_b, *lands)])
    return [(res[k], res[n + k], res[2 * n + k], res[3 * n + k]) for k in range(n)], res[-1]


def _grads_wait(started, after, name):
    n = len(started)

    def body(*refs):
        g, land = refs[:n], refs[n:2 * n]
        send_sems, recv_sems = refs[2 * n:3 * n], refs[3 * n:4 * n]
        for k in range(n):
            for cp in _grad_copies(g[k], land[k], send_sems[k], recv_sems[k]):
                cp.wait_send()
                cp.wait_recv()

    gs = [st[2] for st in started]
    lands = [st[3] for st in started]
    res = pl.pallas_call(
        body, name=name,
        in_specs=[HBM] * (2 * n) + [SEM] * (2 * n) + [ANY], out_specs=[HBM] * (2 * n),
        out_shape=[pltpu.HBM(a.shape, a.dtype) for a in (*gs, *lands)],
        input_output_aliases={i: i for i in range(2 * n)},
        compiler_params=pltpu.CompilerParams(has_side_effects=DATAFLOW),
    )(*gs, *lands, *[st[0] for st in started], *[st[1] for st in started], after)
    return res[n:]


def _sum_partials(grad4, got, cb, name, tr):
    h = grad4.shape[2]
    per_half = h // tr

    def body(cb_ref, g_ref, o_ref, out_ref):
        acc = g_ref[...]
        for j in range(7):
            acc = acc + o_ref[j].astype(F32)
        out_ref[...] = acc

    return pl.pallas_call(
        body, name=name,
        grid_spec=pltpu.PrefetchScalarGridSpec(
            num_scalar_prefetch=1, grid=(per_half,),
            in_specs=[pl.BlockSpec((None, None, tr, D), lambda i, cb_ref: (cb_ref[1], cb_ref[0], i, 0)),
                      pl.BlockSpec((7, tr, D), lambda i, cb_ref: (0, i, 0))],
            out_specs=pl.BlockSpec((tr, D), lambda i, cb_ref: (cb_ref[0] * per_half + i, 0))),
        out_shape=jax.ShapeDtypeStruct((2 * h, D), F32),
        compiler_params=_cp(("arbitrary",)),
    )(cb, grad4, got)


def _swap_halves(shards, name):
    n = len(shards)

    def body(*refs):
        out, send_sems, recv_sems = refs[n:2 * n], refs[2 * n], refs[2 * n + 1]
        x, y, c = _place()
        cps = []
        for k in range(n):
            h = shards[k].shape[0] // 2
            mine = out[k].at[pl.ds(pl.multiple_of(c * h, 8), h)]
            cp = pltpu.make_async_remote_copy(src_ref=mine, dst_ref=mine, send_sem=send_sems.at[k],
                                              recv_sem=recv_sems.at[k], device_id=(x, y, 1 - c), device_id_type=MESH)
            cp.start()
            cps.append(cp)
        for cp in cps:
            cp.wait()

    return pl.pallas_call(
        body, name=name,
        in_specs=[ANY] * n, out_specs=[ANY] * n,
        out_shape=[jax.ShapeDtypeStruct(sh.shape, F32) for sh in shards],
        input_output_aliases={k: k for k in range(n)},
        scratch_shapes=[pltpu.SemaphoreType.DMA((n,)), pltpu.SemaphoreType.DMA((n,))],
        compiler_params=pltpu.CompilerParams(has_side_effects=True),
    )(*shards)


def _share_halves(shards, small):
    n = len(shards)
    rows = small.shape[0]

    def body(*refs):
        small_ref = refs[n]
        out, total_ref = refs[n + 1:2 * n + 1], refs[2 * n + 1]
        all_ref, send_sems, recv_sems, ssend, srecv = refs[2 * n + 2:]
        x, y, c = _place()
        me = 4 * x + 2 * y + c
        cps = []
        for k in range(n):
            h = shards[k].shape[0] // 2
            mine = out[k].at[pl.ds(pl.multiple_of(c * h, 8), h)]
            cp = pltpu.make_async_remote_copy(src_ref=mine, dst_ref=mine, send_sem=send_sems.at[k],
                                              recv_sem=recv_sems.at[k], device_id=(x, y, 1 - c), device_id_type=MESH)
            cp.start()
            cps.append(cp)
        all_ref[me] = small_ref[...]
        peers = []
        for d in range(1, 8):
            px, py, pc = x ^ (d >> 2), y ^ ((d >> 1) & 1), c ^ (d & 1)
            cp = pltpu.make_async_remote_copy(src_ref=small_ref, dst_ref=all_ref.at[me],
                                              send_sem=ssend.at[d - 1], recv_sem=srecv.at[d - 1],
                                              device_id=(px, py, pc), device_id_type=MESH)
            cp.start()
            peers.append(cp)
        for cp in peers:
            cp.wait()
        acc = all_ref[0]
        for d in range(1, 8):
            acc = acc + all_ref[d]
        total_ref[...] = acc
        for cp in cps:
            cp.wait()

    return pl.pallas_call(
        body, name="share_halves",
        in_specs=[ANY] * n + [VMEM], out_specs=[ANY] * n + [VMEM],
        out_shape=[jax.ShapeDtypeStruct(sh.shape, F32) for sh in shards] + [jax.ShapeDtypeStruct((rows, D), F32)],
        input_output_aliases={k: k for k in range(n)},
        scratch_shapes=[pltpu.VMEM((8, rows, D), F32), pltpu.SemaphoreType.DMA((n,)), pltpu.SemaphoreType.DMA((n,)),
                        pltpu.SemaphoreType.DMA((7,)), pltpu.SemaphoreType.DMA((7,))],
        compiler_params=pltpu.CompilerParams(has_side_effects=True),
    )(*shards, small)


def _adamw(w, g, m, v, name, tr):
    rows, cols = w.shape

    def body(w_ref, g_ref, m_ref, v_ref, d_ref, nm_ref, nv_ref):
        g_ = g_ref[...]
        nm = ADAM_B1 * m_ref[...] + (1.0 - ADAM_B1) * g_
        nv = ADAM_B2 * v_ref[...] + (1.0 - ADAM_B2) * (g_ * g_)
        m_hat = nm / (1.0 - ADAM_B1 ** ADAM_STEP)
        v_hat = nv / (1.0 - ADAM_B2 ** ADAM_STEP)
        d_ref[...] = -ADAM_LR * (m_hat / (jnp.sqrt(v_hat) + ADAM_EPS) + ADAM_WD * w_ref[...])
        nm_ref[...] = nm
        nv_ref[...] = nv

    spec = pl.BlockSpec((tr, cols), lambda i: (i, 0))
    return pl.pallas_call(
        body, name=name, grid=(rows // tr,),
        in_specs=[spec] * 4, out_specs=[spec] * 3,
        out_shape=[jax.ShapeDtypeStruct((rows, cols), F32)] * 3,
        compiler_params=_cp(("parallel",)),
    )(w, g, m, v)


def _local_step(x, target, w_in_t, late_weights, norm_a_g, norm_b_g, sinks_a, ln1_g, ln1_b,
                conv_w, conv_b, ln2_g, ln2_b, slopes, on_grad):
    cwb = jnp.concatenate([conv_w, conv_b[None]], axis=0).reshape(4, 2, FF)

    proj, xb = _proj(x, w_in_t, "proj")
    o_a, lse_a = _attn_a_fwd(proj, sinks_a)
    fwd_b = [_attn_b_fwd(proj, slopes, r) for r in B_DILATIONS]
    w_o = late_weights(1, fwd_b[-1][1])
    o_b, lse_b, cat, z1, h1, h1b = _mix_ln1(x, o_a, [f[0] for f in fwd_b], [f[1] for f in fwd_b],
                                           norm_a_g, norm_b_g, w_o, ln1_g, ln1_b)
    w_up = late_weights(2, h1b)
    up, a, gate, a1 = _up_conv_gelu(h1b, w_up, cwb)
    w_down = late_weights(3, a)
    dz2, dz2b, st2 = _down_ln2_loss(a, w_down, h1, target, ln2_g, ln2_b)

    on_grad(3, *_grad_w(a, dz2b, "grad_w_down", tm=FF // 2))
    dup, dconv = _conv_gelu_bwd(dz2b, w_down.T, up, gate, a1, cwb)
    on_grad(2, *_grad_w(dup, h1b, "grad_w_up", tm=FF // 2, lhs_halves=True))
    dz1, dz1b, st1 = _dh1_ln1_bwd(dz2, dup, w_up, z1, ln1_g)
    tok = on_grad(1, *_grad_w(cat, dz1b, "grad_w_o", tm=512))
    d_oa, d_ob, st_n = _dcat_rms_bwd(dz1b, w_o, o_a, o_b, norm_a_g + tok[0, 0], norm_b_g)
    dqa, dka, dva, dsink = _attn_a_bwd(proj, sinks_a, d_oa, o_a, lse_a)
    bwd_b = None
    for r in B_DILATIONS:
        bwd_b = _attn_b_bwd(proj, slopes, d_ob, o_b, lse_b, r, bwd_b)
    dproj = _dproj_combine(dqa, dka, dva, bwd_b)
    tok = on_grad(0, *_grad_w(dproj, xb, "grad_w_in", tm=WA))
    gx = _grad_x(dz1, dproj, w_in_t, tok)

    dconv = dconv.reshape(4, 2 * FF)
    small = dict(loss=st2[2, 0:1], norm_a_g=st_n[0], norm_b_g=st_n[1], sinks_a=dsink[:, 0],
                 ln1_g=st1[0], ln1_b=st1[1], conv_w=dconv[0:3].reshape(-1), conv_b=dconv[3],
                 ln2_g=st2[0], ln2_b=st2[1])
    return gx, small


SMALL_ORDER = ("loss", "norm_a_g", "norm_b_g", "sinks_a", "ln1_g", "ln1_b", "conv_b", "ln2_g", "ln2_b", "conv_w")
SMALL_SIZES = dict(loss=1, norm_a_g=512, norm_b_g=512, sinks_a=8, ln1_g=D, ln1_b=D, conv_b=2 * FF, ln2_g=D, ln2_b=D,
                   conv_w=3 * 2 * FF)


def _pack(parts, rows):
    flat = jnp.concatenate([parts[k].reshape(-1).astype(F32) for k in parts])
    return jnp.pad(flat, (0, rows * D - flat.shape[0])).reshape(rows, D)


def _unpack(buf, names, sizes):
    flat = buf.reshape(-1)
    out, at = {}, 0
    for k in names:
        out[k] = flat[at:at + sizes[k]]
        at += sizes[k]
    return out


def kernel(x, w_in, norm_a_g, norm_b_g, sinks_a, w_o, ln1_g, ln1_b, w_up, conv_w, conv_b, w_down, ln2_g, ln2_b, loss_target, m_w_in, m_norm_a_g, m_norm_b_g, m_sinks_a, m_w_o, m_ln1_g, m_ln1_b, m_w_up, m_conv_w, m_conv_b, m_w_down, m_ln2_g, m_ln2_b, v_w_in, v_norm_a_g, v_norm_b_g, v_sinks_a, v_w_o, v_ln1_g, v_ln1_b, v_w_up, v_conv_w, v_conv_b, v_w_down, v_ln2_g, v_ln2_b):
    xi, yi, ci = _place()
    chip = (2 * xi + yi).astype(I32)
    core = ci.astype(I32)

    w_in_rows, m_w_in_rows, v_w_in_rows = w_in.T, m_w_in.T, v_w_in.T
    shards = (w_in_rows.astype(BF16), w_o.astype(BF16), w_up.astype(BF16), w_down.astype(BF16))
    w_in_t, conv_w4 = _gather_w_in(shards[0], conv_w)
    conv_w_f = conv_w4.transpose(1, 0, 2).reshape(3, 2 * FF)
    w_started, w_tok = _weights_start(shards[1:], conv_w4)
    slopes = jnp.asarray(SLOPES, F32) + w_tok[0, 0]

    halves_rows = [r // 2 for r in SHARD_ROWS]
    grads4, grads_b4, started = [None] * 4, [None] * 4, [None] * 4

    def on_grad(k, g, g_b):
        grads4[k] = g.reshape(N_CHIPS, 2, halves_rows[k], D)
        grads_b4[k] = g_b.reshape(N_CHIPS, 2, halves_rows[k], D)
        if k > 1:
            return None
        group = (1, 2, 3) if k == 1 else (0,)
        sts, tok = _grads_start([grads_b4[i] for i in group], f"grads_start_{k}")
        for i, st in zip(group, sts):
            started[i] = st
        return tok

    gx, small = _local_step(
        x[0], loss_target[0], w_in_t, lambda k, after: _weights_wait(w_started[k - 1], after, f"weights_wait_{k}"),
        norm_a_g, norm_b_g, sinks_a, ln1_g, ln1_b, conv_w_f, conv_b, ln2_g, ln2_b, slopes, on_grad)

    tiles = (96, 128, 352, 176)
    core_chip = jnp.stack([core, chip])
    got = _grads_wait(started[1:], gx, "grads_wait_1")
    halves = [_sum_partials(grads4[k], got[k - 1], core_chip, f"sum_partials_{k}", tiles[k]) for k in (1, 2, 3)]
    g_w_o, g_w_up_rows, g_w_down = _swap_halves(halves, "swap_halves")
    g_w_up = g_w_up_rows.T
    delta, new_m, new_v = {}, {}, {}
    for k, g, tr in (("w_o", g_w_o, 128), ("w_up", g_w_up, 256), ("w_down", g_w_down, 176)):
        delta[k], new_m[k], new_v[k] = _adamw(dict(w_o=w_o, w_up=w_up, w_down=w_down)[k], g,
                                              dict(w_o=m_w_o, w_up=m_w_up, w_down=m_w_down)[k],
                                              dict(w_o=v_w_o, w_up=v_w_up, w_down=v_w_down)[k], f"adamw_{k}", tr)

    got = _grads_wait(started[:1], delta["w_up"], "grads_wait_0")
    half_in = _sum_partials(grads4[0], got[0], core_chip, "sum_partials_0", tiles[0])
    small_rows = 32
    g_w_in_rows, totals = _share_halves([half_in], _pack({k: small[k] for k in SMALL_ORDER}, small_rows))
    tot = _unpack(totals, SMALL_ORDER, SMALL_SIZES)
    loss = tot["loss"][0]
    cols = 2 * FF // N_CHIPS
    g_conv_w = lax.dynamic_slice(tot["conv_w"].reshape(3, 2 * FF), (0, chip * cols), (3, cols))
    g_small = dict(norm_a_g=tot["norm_a_g"], norm_b_g=tot["norm_b_g"], sinks_a=tot["sinks_a"], ln1_g=tot["ln1_g"],
                   ln1_b=tot["ln1_b"], conv_w=g_conv_w, conv_b=tot["conv_b"], ln2_g=tot["ln2_g"], ln2_b=tot["ln2_b"])

    weights = dict(w_in=w_in, norm_a_g=norm_a_g, norm_b_g=norm_b_g, sinks_a=sinks_a, w_o=w_o, ln1_g=ln1_g, ln1_b=ln1_b,
                   w_up=w_up, conv_w=conv_w, conv_b=conv_b, w_down=w_down, ln2_g=ln2_g, ln2_b=ln2_b)
    ms = dict(w_in=m_w_in, norm_a_g=m_norm_a_g, norm_b_g=m_norm_b_g, sinks_a=m_sinks_a, w_o=m_w_o, ln1_g=m_ln1_g,
              ln1_b=m_ln1_b, w_up=m_w_up, conv_w=m_conv_w, conv_b=m_conv_b, w_down=m_w_down, ln2_g=m_ln2_g, ln2_b=m_ln2_b)
    vs = dict(w_in=v_w_in, norm_a_g=v_norm_a_g, norm_b_g=v_norm_b_g, sinks_a=v_sinks_a, w_o=v_w_o, ln1_g=v_ln1_g,
              ln1_b=v_ln1_b, w_up=v_w_up, conv_w=v_conv_w, conv_b=v_conv_b, w_down=v_w_down, ln2_g=v_ln2_g, ln2_b=v_ln2_b)
    order = list(weights)
    grad = dict(g_small, w_in=g_w_in_rows.T, w_o=g_w_o, w_up=g_w_up, w_down=g_w_down)

    delta["w_in"], new_m["w_in"], new_v["w_in"] = [
        a.T for a in _adamw(w_in_rows, g_w_in_rows, m_w_in_rows, v_w_in_rows, "adamw_w_in", 144)]
    small_names = [k for k in order if k not in delta]
    sizes = {k: weights[k].size for k in small_names}
    rows = 16
    packed = [_pack({k: src[k] for k in small_names}, rows) for src in (weights, grad, ms, vs)]
    for res, buf in zip((delta, new_m, new_v), _adamw(*packed, "adamw_small", rows)):
        for k, val in _unpack(buf, small_names, sizes).items():
            res[k] = val.reshape(weights[k].shape)

    return (loss, gx[None], *[grad[k] for k in order], *[delta[k] for k in order],
            *[new_m[k] for k in order], *[new_v[k] for k in order])
```

```python
import functools
import math

import jax
import jax.numpy as jnp
from jax import lax
from jax.experimental import pallas as pl
from jax.experimental.pallas import tpu as pltpu

F32, BF16, I32 = jnp.float32, jnp.bfloat16, jnp.int32

D = 1024
FF = 2816
HD = 64
NH = 8
WA, WB = 768, 1536
WIN = WA + WB
BLK = 128
ALPHA = 2.0 ** 0.25
LN_EPS, RMS_EPS = 1e-5, 1e-6
SCALE = 1.0 / math.sqrt(HD)
A_MAX_DIST, B_MAX_DIST = 127, 128
B_DILATIONS = (1, 4, 16)
SLOPES = tuple(2.0 ** (-(i + 1)) for i in range(NH))
SHARD_ROWS = (WIN // 4, D // 4, 2 * FF // 4, FF // 4)
N_CHIPS = 4
ADAM_LR, ADAM_B1, ADAM_B2, ADAM_EPS, ADAM_WD, ADAM_STEP = 0.001, 0.9, 0.999, 1e-08, 0.01, 10
MESH = pl.DeviceIdType.MESH
ANY = pl.BlockSpec(memory_space=pl.ANY)
SMEM = pl.BlockSpec(memory_space=pltpu.SMEM)
VMEM = pl.BlockSpec(memory_space=pltpu.VMEM)
HBM = pl.BlockSpec(memory_space=pltpu.HBM)
SEM = pl.BlockSpec(memory_space=pltpu.SEMAPHORE)
DATAFLOW = pltpu.SideEffectType.DATAFLOW_SIDE_EFFECTING


def _cp(sem, mb=48):
    return pltpu.CompilerParams(dimension_semantics=sem, vmem_limit_bytes=mb << 20)


def _nn(a, b):
    return lax.dot_general(a, b, (((1,), (0,)), ((), ())), preferred_element_type=F32)


def _nt(a, b):
    return lax.dot_general(a, b, (((1,), (1,)), ((), ())), preferred_element_type=F32)


def _tn(a, b):
    return lax.dot_general(a, b, (((0,), (0,)), ((), ())), preferred_element_type=F32)


def _resident(shape):
    n = len(shape)
    return pl.BlockSpec(shape, lambda *_: (0,) * n, pipeline_mode=pl.Buffered(1))


def _const(shape):
    n = len(shape)
    return pl.BlockSpec(shape, lambda *_: (0,) * n)


def _proj(x, w_t, name, tm=512):
    s = x.shape[0]
    n = w_t.shape[0]

    def body(x_ref, w_ref, o_ref, xb_ref):
        xb = x_ref[...].astype(BF16)
        xb_ref[...] = xb
        o_ref[...] = _nt(xb, w_ref[...])

    return pl.pallas_call(
        body, name=name, grid=(s // tm,),
        in_specs=[pl.BlockSpec((tm, D), lambda i: (i, 0)), _resident((n, D))],
        out_specs=[pl.BlockSpec((tm, n), lambda i: (i, 0)), pl.BlockSpec((tm, D), lambda i: (i, 0))],
        out_shape=[jax.ShapeDtypeStruct((s, n), F32), jax.ShapeDtypeStruct((s, D), BF16)],
        compiler_params=_cp(("parallel",)),
    )(x, w_t)


def _grad_w(lhs, rhs, name, tm, tk=512, lhs_halves=False):
    s = rhs.shape[0]
    if lhs_halves:
        per_half = lhs.shape[2] // tm
        n = 2 * lhs.shape[2]
        lhs_spec = pl.BlockSpec((None, tk, tm), lambda i, k: (i // per_half, k, i % per_half))
    else:
        n = lhs.shape[1]
        lhs_spec = pl.BlockSpec((tk, tm), lambda i, k: (k, i))
    nk = s // tk

    def body(l_ref, r_ref, o_ref, ob_ref):
        k = pl.program_id(1)

        @pl.when(k == 0)
        def _():
            o_ref[...] = jnp.zeros_like(o_ref)

        o_ref[...] += _tn(l_ref[...].astype(BF16), r_ref[...].astype(BF16))

        @pl.when(k == nk - 1)
        def _():
            ob_ref[...] = o_ref[...].astype(BF16)

    return pl.pallas_call(
        body, name=name, grid=(n // tm, nk),
        in_specs=[lhs_spec, pl.BlockSpec((tk, D), lambda i, k: (k, 0))],
        out_specs=[pl.BlockSpec((tm, D), lambda i, k: (i, 0))] * 2,
        out_shape=[jax.ShapeDtypeStruct((n, D), F32), jax.ShapeDtypeStruct((n, D), BF16)],
        compiler_params=_cp(("parallel", "arbitrary")),
    )(lhs, rhs)


def _band_base(max_dist, dist_unit, first):
    row = lax.broadcasted_iota(I32, (BLK, 2 * BLK), 0)
    col = lax.broadcasted_iota(I32, (BLK, 2 * BLK), 1)
    dist = BLK + row - col
    ok = (dist >= 0) & (dist <= max_dist)
    if first:
        ok = ok & (col >= BLK)
    return jnp.where(ok, dist.astype(F32) * (-float(dist_unit)), -jnp.inf)


def _half_mask(shape, e):
    lane = lax.broadcasted_iota(I32, shape, 1)
    return (lane < HD) if e == 0 else (lane >= HD)


def _to_half(x, e, g):
    if g != e:
        x = pltpu.roll(x, HD, 1)
    return jnp.where(_half_mask(x.shape, g), x, 0.0)


def _stack_heads(scalars, tile):
    return jnp.concatenate([scalars[0] * tile, scalars[1] * tile], axis=0)


def _pair_fwd(q2, kb, vb, base, slopes, kv_heads, sinks):
    lo = _half_mask((BLK, 2 * HD), 0)
    if sinks is not None:
        o2 = lse2 = None
        for e in (0, 1):
            g = kv_heads[e]
            qv = (_to_half(q2, e, g) * SCALE).astype(BF16)
            s = _nt(qv, kb) + slopes[e] * base
            m = jnp.maximum(jnp.max(s, axis=1, keepdims=True), sinks[e])
            p = jnp.exp(s - m)
            l = jnp.sum(p, axis=1, keepdims=True) + jnp.exp(sinks[e] - m)
            oh = _nn(p.astype(BF16), vb) / l
            if g != e:
                oh = pltpu.roll(oh, HD, 1)
            lse = jnp.broadcast_to(m + jnp.log(l), (BLK, 2 * HD))
            o2 = oh if e == 0 else jnp.where(lo, o2, oh)
            lse2 = lse if e == 0 else jnp.where(lo, lse2, lse)
        return o2, lse2
    qs = jnp.concatenate([_to_half(q2, e, kv_heads[e]) * SCALE for e in (0, 1)], axis=0).astype(BF16)
    s = _nt(qs, kb) + (base if slopes is None else _stack_heads(slopes, base))
    m = jnp.max(s, axis=1, keepdims=True)
    p = jnp.exp(s - m)
    l = jnp.sum(p, axis=1, keepdims=True)
    o = _nn(p.astype(BF16), vb) / l
    lse = m + jnp.log(l)
    halves = []
    for e in (0, 1):
        oh = o[e * BLK:(e + 1) * BLK]
        halves.append(pltpu.roll(oh, HD, 1) if kv_heads[e] != e else oh)
    o2 = jnp.where(lo, halves[0], halves[1])
    lse2 = jnp.where(lo, jnp.broadcast_to(lse[:BLK], (BLK, 2 * HD)), jnp.broadcast_to(lse[BLK:], (BLK, 2 * HD)))
    return o2, lse2


def _pair_bwd(q2, kb, vb, do2, o2, lse2, base, slopes, kv_heads, sinks):
    lo = _half_mask((BLK, 2 * HD), 0)
    prod = do2 * o2
    lses, deltas = [], []
    for e in (0, 1):
        hq = _half_mask((BLK, 2 * HD), e)
        lses.append(jnp.max(jnp.where(hq, lse2, -jnp.inf), axis=1, keepdims=True))
        deltas.append(jnp.sum(jnp.where(hq, prod, 0.0), axis=1, keepdims=True))
    lse = jnp.concatenate(lses, axis=0)
    delta = jnp.concatenate(deltas, axis=0)
    qs = jnp.concatenate([_to_half(q2, e, kv_heads[e]) * SCALE for e in (0, 1)], axis=0).astype(BF16)
    dos = jnp.concatenate([_to_half(do2, e, kv_heads[e]) for e in (0, 1)], axis=0).astype(BF16)
    p = jnp.exp(_nt(qs, kb) + (base if slopes is None else _stack_heads(slopes, base)) - lse)
    ds = (p * (_nt(dos, vb) - delta)).astype(BF16)
    dq = _nn(ds, kb) * SCALE
    halves = []
    for e in (0, 1):
        dqh = dq[e * BLK:(e + 1) * BLK]
        halves.append(pltpu.roll(dqh, HD, 1) if kv_heads[e] != e else dqh)
    dq2 = jnp.where(lo, halves[0], halves[1])
    dk2 = _tn(ds, qs)
    dv2 = _tn(p.astype(BF16), dos)
    dsinks = []
    if sinks is not None:
        for e in (0, 1):
            dsinks.append(jnp.sum(-jnp.exp(sinks[e] - lses[e]) * deltas[e], axis=0, keepdims=True))
    return dq2, dk2, dv2, dsinks


A_BLOCKS_PER_STEP = 2
A_BLOCKS_PER_STEP_BWD = 1


def _attn_a_fwd(proj, sinks):
    s = proj.shape[0]
    nq = A_BLOCKS_PER_STEP
    rows = BLK * nq
    steps = s // rows

    def body(sink_ref, q_ref, kp_ref, kc_ref, vp_ref, vc_ref, o_ref, lse_ref):
        n = pl.program_id(0)
        base_rest = _band_base(A_MAX_DIST, 1, False)
        base_0 = jnp.where(n > 0, base_rest, _band_base(A_MAX_DIST, 1, True))
        for i in range(nq):
            cur = pl.ds(i * BLK, BLK)
            k_prev = kc_ref[pl.ds((i - 1) * BLK, BLK), :] if i > 0 else kp_ref[...]
            v_prev = vc_ref[pl.ds((i - 1) * BLK, BLK), :] if i > 0 else vp_ref[...]
            kb = jnp.concatenate([k_prev, kc_ref[cur, :]], axis=0).astype(BF16)
            vb = jnp.concatenate([v_prev, vc_ref[cur, :]], axis=0).astype(BF16)
            for j in range(NH // 2):
                g = j // 2
                o2, lse2 = _pair_fwd(q_ref[cur, 128 * j:128 * (j + 1)], kb, vb, base_rest if i > 0 else base_0,
                                     (SLOPES[2 * j], SLOPES[2 * j + 1]), (g, g), (sink_ref[2 * j], sink_ref[2 * j + 1]))
                o_ref[cur, 128 * j:128 * (j + 1)] = o2
                lse_ref[cur, 128 * j:128 * (j + 1)] = lse2

    before = lambda n: jnp.maximum(n * nq - 1, 0)
    return pl.pallas_call(
        body, name="attn_a_fwd", grid=(steps,),
        in_specs=[SMEM,
                  pl.BlockSpec((rows, 512), lambda n: (n, 0)),
                  pl.BlockSpec((BLK, 128), lambda n: (before(n), 4)), pl.BlockSpec((rows, 128), lambda n: (n, 4)),
                  pl.BlockSpec((BLK, 128), lambda n: (before(n), 5)), pl.BlockSpec((rows, 128), lambda n: (n, 5))],
        out_specs=[pl.BlockSpec((rows, 512), lambda n: (n, 0))] * 2,
        out_shape=[jax.ShapeDtypeStruct((s, 512), F32)] * 2,
        compiler_params=_cp(("parallel",)),
    )(sinks, proj, proj, proj, proj, proj)


def _attn_a_bwd(proj, sinks, d_o, o, lse):
    s = proj.shape[0]
    nq = A_BLOCKS_PER_STEP_BWD
    rows = BLK * nq
    steps = s // rows

    def body(sink_ref, q_ref, kp_ref, kc_ref, vp_ref, vc_ref, do_ref, o_ref, lse_ref,
             dq_ref, dk_ref, dv_ref, dsink_ref, kcar, vcar):
        n = pl.program_id(0)

        @pl.when(n == 0)
        def _():
            kcar[...] = jnp.zeros_like(kcar)
            vcar[...] = jnp.zeros_like(vcar)
            dsink_ref[...] = jnp.zeros_like(dsink_ref)

        dk_ref[...] = kcar[...]
        dv_ref[...] = vcar[...]

        @pl.when(n < steps)
        def _():
            base_rest = _band_base(A_MAX_DIST, 1, False)
            base_0 = jnp.where(n > 0, base_rest, _band_base(A_MAX_DIST, 1, True))
            for i in range(nq):
                cur = pl.ds(i * BLK, BLK)
                k_prev = kc_ref[pl.ds((i - 1) * BLK, BLK), :] if i > 0 else kp_ref[...]
                v_prev = vc_ref[pl.ds((i - 1) * BLK, BLK), :] if i > 0 else vp_ref[...]
                kb = jnp.concatenate([k_prev, kc_ref[cur, :]], axis=0).astype(BF16)
                vb = jnp.concatenate([v_prev, vc_ref[cur, :]], axis=0).astype(BF16)
                dk_win = dv_win = None
                for j in range(NH // 2):
                    g = j // 2
                    sl = slice(128 * j, 128 * (j + 1))
                    dq2, dk2, dv2, dsk = _pair_bwd(q_ref[cur, sl], kb, vb, do_ref[cur, sl], o_ref[cur, sl],
                                                   lse_ref[cur, sl], base_rest if i > 0 else base_0,
                                                   (SLOPES[2 * j], SLOPES[2 * j + 1]), (g, g),
                                                   (sink_ref[2 * j], sink_ref[2 * j + 1]))
                    dq_ref[cur, sl] = dq2
                    dk_win = dk2 if j == 0 else dk_win + dk2
                    dv_win = dv2 if j == 0 else dv_win + dv2
                    for e in (0, 1):
                        h = 2 * j + e
                        dsink_ref[h:h + 1, :] += jnp.broadcast_to(dsk[e], (1, 128))
                if i == 0:
                    last = pl.ds((nq - 1) * BLK, BLK)
                    dk_ref[last, :] += dk_win[:BLK]
                    dv_ref[last, :] += dv_win[:BLK]
                else:
                    kcar[pl.ds((i - 1) * BLK, BLK), :] += dk_win[:BLK]
                    vcar[pl.ds((i - 1) * BLK, BLK), :] += dv_win[:BLK]
                kcar[cur, :] = dk_win[BLK:]
                vcar[cur, :] = dv_win[BLK:]

    cur_step = lambda n: jnp.minimum(n, steps - 1)
    before = lambda n: jnp.maximum(cur_step(n) * nq - 1, 0)
    out_prev = lambda n: jnp.maximum(n - 1, 0)
    wide = pl.BlockSpec((rows, 512), lambda n: (cur_step(n), 0))
    return pl.pallas_call(
        body, name="attn_a_bwd", grid=(steps + 1,),
        in_specs=[SMEM, wide,
                  pl.BlockSpec((BLK, 128), lambda n: (before(n), 4)), pl.BlockSpec((rows, 128), lambda n: (cur_step(n), 4)),
                  pl.BlockSpec((BLK, 128), lambda n: (before(n), 5)), pl.BlockSpec((rows, 128), lambda n: (cur_step(n), 5)),
                  wide, wide, wide],
        out_specs=[wide,
                   pl.BlockSpec((rows, 128), lambda n: (out_prev(n), 0)),
                   pl.BlockSpec((rows, 128), lambda n: (out_prev(n), 0)),
                   pl.BlockSpec((NH, 128), lambda n: (0, 0))],
        out_shape=[jax.ShapeDtypeStruct((s, 512), F32), jax.ShapeDtypeStruct((s, 128), F32),
                   jax.ShapeDtypeStruct((s, 128), F32), jax.ShapeDtypeStruct((NH, 128), F32)],
        scratch_shapes=[pltpu.VMEM((rows, 128), F32), pltpu.VMEM((rows, 128), F32)],
        compiler_params=_cp(("arbitrary",)),
    )(sinks, proj, proj, proj, proj, proj, d_o, o, lse)


def _stream(rho, i, r):
    start = i * BLK * r + rho
    return pl.ds(start, BLK, stride=r) if r > 1 else pl.ds(start, BLK)


def _for_streams(r, fn, side_by_side=4):
    if r <= side_by_side:
        for rho in range(r):
            fn(rho)
    else:
        def group(it, carry):
            for u in range(side_by_side):
                fn(side_by_side * it + u)
            return carry

        lax.fori_loop(0, r // side_by_side, group, 0)


B_BLOCKS_PER_STEP = {1: 8, 4: 2, 16: 1}
B_BLOCKS_PER_STEP_FWD = {1: 8, 4: 2, 16: 1}


def _attn_b_fwd(proj, slopes, r):
    s = proj.shape[0]
    nq = B_BLOCKS_PER_STEP_FWD[r]
    rows = BLK * r * nq
    steps = s // rows
    qc, kc, vc = WA // 128, WA // 128 + 4, WA // 128 + 8

    def body(slope_ref, q_ref, kp_ref, kc_ref, vp_ref, vc_ref, o_ref, lse_ref):
        j = pl.program_id(0)
        sb = pl.program_id(1)
        sl2 = (slope_ref[2 * j], slope_ref[2 * j + 1])
        bias_rest = _stack_heads(sl2, _band_base(B_MAX_DIST, r, False))
        bias_0 = jnp.where(sb > 0, bias_rest, _stack_heads(sl2, _band_base(B_MAX_DIST, r, True)))

        def stream(rho):
            for i in range(nq):
                cur = _stream(rho, i, r)
                k_prev = kc_ref[_stream(rho, i - 1, r), :] if i > 0 else kp_ref[_stream(rho, 0, r), :]
                v_prev = vc_ref[_stream(rho, i - 1, r), :] if i > 0 else vp_ref[_stream(rho, 0, r), :]
                kb = jnp.concatenate([k_prev, kc_ref[cur, :]], axis=0).astype(BF16)
                vb = jnp.concatenate([v_prev, vc_ref[cur, :]], axis=0).astype(BF16)
                o2, lse2 = _pair_fwd(q_ref[cur, :], kb, vb, bias_rest if i > 0 else bias_0, None, (0, 1), None)
                o_ref[cur, :] = o2
                lse_ref[cur, :] = lse2

        _for_streams(r, stream, side_by_side=8)

    before = lambda sb: jnp.maximum(sb * nq - 1, 0)
    return pl.pallas_call(
        body, name=f"attn_b_fwd_r{r}", grid=(NH // 2, steps),
        in_specs=[SMEM,
                  pl.BlockSpec((rows, 128), lambda j, sb: (sb, qc + j)),
                  pl.BlockSpec((BLK * r, 128), lambda j, sb: (before(sb), kc + j)),
                  pl.BlockSpec((rows, 128), lambda j, sb: (sb, kc + j)),
                  pl.BlockSpec((BLK * r, 128), lambda j, sb: (before(sb), vc + j)),
                  pl.BlockSpec((rows, 128), lambda j, sb: (sb, vc + j))],
        out_specs=[pl.BlockSpec((rows, 128), lambda j, sb: (sb, j))] * 2,
        out_shape=[jax.ShapeDtypeStruct((s, 512), F32)] * 2,
        compiler_params=_cp(("parallel", "parallel")),
    )(slopes, proj, proj, proj, proj, proj)


def _attn_b_bwd(proj, slopes, d_o, o, lse, r, so_far=None):
    s = proj.shape[0]
    nq = B_BLOCKS_PER_STEP[r]
    rows = BLK * r * nq
    steps = s // rows
    qc, kc, vc = WA // 128, WA // 128 + 4, WA // 128 + 8
    chained = so_far is not None

    def body(slope_ref, q_ref, kp_ref, kc_ref, vp_ref, vc_ref, do_ref, o_ref, lse_ref, *rest):
        if chained:
            pq_ref, pk_ref, pv_ref, dq_ref, dk_ref, dv_ref, kcar, vcar = rest
        else:
            dq_ref, dk_ref, dv_ref, kcar, vcar = rest
        j = pl.program_id(0)
        sb = pl.program_id(1)

        @pl.when(sb == 0)
        def _():
            kcar[...] = jnp.zeros_like(kcar)
            vcar[...] = jnp.zeros_like(vcar)

        if chained:
            dk_ref[...] = kcar[...] + pk_ref[...]
            dv_ref[...] = vcar[...] + pv_ref[...]
        else:
            dk_ref[...] = kcar[...]
            dv_ref[...] = vcar[...]

        @pl.when(sb < steps)
        def _():
            sl2 = (slope_ref[2 * j], slope_ref[2 * j + 1])
            bias_rest = _stack_heads(sl2, _band_base(B_MAX_DIST, r, False))
            bias_0 = jnp.where(sb > 0, bias_rest, _stack_heads(sl2, _band_base(B_MAX_DIST, r, True)))

            def stream(rho):
                for i in range(nq):
                    cur = _stream(rho, i, r)
                    k_prev = kc_ref[_stream(rho, i - 1, r), :] if i > 0 else kp_ref[_stream(rho, 0, r), :]
                    v_prev = vc_ref[_stream(rho, i - 1, r), :] if i > 0 else vp_ref[_stream(rho, 0, r), :]
                    kb = jnp.concatenate([k_prev, kc_ref[cur, :]], axis=0).astype(BF16)
                    vb = jnp.concatenate([v_prev, vc_ref[cur, :]], axis=0).astype(BF16)
                    dq2, dk2, dv2, _ = _pair_bwd(q_ref[cur, :], kb, vb, do_ref[cur, :], o_ref[cur, :], lse_ref[cur, :],
                                                 bias_rest if i > 0 else bias_0, None, (0, 1), None)
                    dq_ref[cur, :] = dq2 + pq_ref[cur, :] if chained else dq2
                    if i == 0:
                        last = _stream(rho, nq - 1, r)
                        dk_ref[last, :] += dk2[:BLK]
                        dv_ref[last, :] += dv2[:BLK]
                    else:
                        kcar[_stream(rho, i - 1, r), :] += dk2[:BLK]
                        vcar[_stream(rho, i - 1, r), :] += dv2[:BLK]
                    kcar[cur, :] = dk2[BLK:]
                    vcar[cur, :] = dv2[BLK:]

            _for_streams(r, stream, side_by_side=8)

    cur_step = lambda sb: jnp.minimum(sb, steps - 1)
    before = lambda sb: jnp.maximum(cur_step(sb) * nq - 1, 0)
    out_prev = lambda sb: jnp.maximum(sb - 1, 0)
    tile = lambda col: pl.BlockSpec((rows, 128), lambda j, sb: (cur_step(sb), col + j))
    edge = lambda col: pl.BlockSpec((BLK * r, 128), lambda j, sb: (before(sb), col + j))
    late = pl.BlockSpec((rows, 128), lambda j, sb: (out_prev(sb), j))
    grads = [tile(0), late, late]
    return pl.pallas_call(
        body, name=f"attn_b_bwd_r{r}", grid=(NH // 2, steps + 1),
        in_specs=[SMEM, tile(qc), edge(kc), tile(kc), edge(vc), tile(vc), tile(0), tile(0), tile(0)]
        + (grads if chained else []),
        out_specs=grads,
        out_shape=[jax.ShapeDtypeStruct((s, 512), F32)] * 3,
        scratch_shapes=[pltpu.VMEM((rows, 128), F32), pltpu.VMEM((rows, 128), F32)],
        compiler_params=_cp(("parallel", "arbitrary")),
    )(slopes, proj, proj, proj, proj, proj, d_o, o, lse, *(so_far if chained else ()))


def _row(v):
    return v.reshape(1, -1)


def _layer_norm_stats(z):
    mu = jnp.mean(z, axis=-1, keepdims=True)
    zc = z - mu
    var = jnp.mean(zc * zc, axis=-1, keepdims=True)
    rstd = lax.rsqrt(var + LN_EPS)
    return zc * rstd, rstd


def _layer_norm_bwd(dh, zh, rstd, g):
    dzh = dh * g
    return rstd * (dzh - jnp.mean(dzh, axis=-1, keepdims=True) - zh * jnp.mean(dzh * zh, axis=-1, keepdims=True))


def _rms(o):
    return lax.rsqrt(jnp.mean(o * o, axis=-1, keepdims=True) + RMS_EPS)


def _mix_ln1(x, o_a, o_b, lse_b, norm_a_g, norm_b_g, w_o, ln1_g, ln1_b, tm=256):
    s = x.shape[0]

    def body(x_ref, oa_ref, ob1, ob2, ob3, l1, l2, l3, ga_ref, gb_ref, wo_ref, g_ref, b_ref,
             obm_ref, lse_ref, cat_ref, z1_ref, h1_ref, h1b_ref):
        la, lb, lc = l1[...], l2[...], l3[...]
        m = jnp.maximum(jnp.maximum(la, lb), lc)
        ea, eb, ec = jnp.exp(la - m), jnp.exp(lb - m), jnp.exp(lc - m)
        den = ea + eb + ec
        obm = (ea / den) * ob1[...] + (eb / den) * ob2[...] + (ec / den) * ob3[...]
        obm_ref[...] = obm
        lse_ref[...] = m + jnp.log(den)
        oa = oa_ref[...]
        na = oa * _rms(oa) * ga_ref[...]
        nb_ = obm * _rms(obm) * gb_ref[...]
        cat = jnp.concatenate([na, nb_], axis=1).astype(BF16)
        cat_ref[...] = cat
        z1 = ALPHA * x_ref[...] + _nn(cat, wo_ref[...])
        z1_ref[...] = z1
        zh, _ = _layer_norm_stats(z1)
        h1 = zh * g_ref[...] + b_ref[...]
        h1_ref[...] = h1
        h1b_ref[...] = h1.astype(BF16)

    t512 = pl.BlockSpec((tm, 512), lambda i: (i, 0))
    td = pl.BlockSpec((tm, D), lambda i: (i, 0))
    return pl.pallas_call(
        body, name="mix_ln1", grid=(s // tm,),
        in_specs=[td] + [t512] * 7 + [_const((1, 512))] * 2 + [_resident((D, D))] + [_const((1, D))] * 2,
        out_specs=[t512, t512, td, td, td, td],
        out_shape=[jax.ShapeDtypeStruct((s, 512), F32), jax.ShapeDtypeStruct((s, 512), F32),
                   jax.ShapeDtypeStruct((s, D), BF16), jax.ShapeDtypeStruct((s, D), F32),
                   jax.ShapeDtypeStruct((s, D), F32), jax.ShapeDtypeStruct((s, D), BF16)],
        compiler_params=_cp(("parallel",)),
    )(x, o_a, *o_b, *lse_b, _row(norm_a_g), _row(norm_b_g), w_o, _row(ln1_g), _row(ln1_b))


def _gelu_and_grad(x):
    c = math.sqrt(2.0 / math.pi)
    x2 = x * x
    cx = c * x
    t = jnp.tanh(cx * (1.0 + 0.044715 * x2))
    q = 1.0 + t
    g = (0.5 * x) * q
    dg = 0.5 * q + ((0.5 * cx) * (1.0 - t * t)) * (1.0 + (3.0 * 0.044715) * x2)
    return g, dg


CONV_CHUNK = 64
CHUNKS_PER_PIECE = 2


def _shift_down(u, before):
    n = u.shape[0]
    ext = jnp.concatenate([before, u], axis=0)
    return pltpu.roll(ext, 1, 0)[8:], pltpu.roll(ext, 2, 0)[8:]


def _shift_up(u, after):
    n = u.shape[0]
    ext = jnp.concatenate([u, after], axis=0)
    return pltpu.roll(ext, n + 7, 0)[:n], pltpu.roll(ext, n + 6, 0)[:n]


def _up_conv_gelu(h1b, w_up, cwb, tm=512, tn=256):
    s = h1b.shape[0]
    n_i = s // tm
    n_t = (FF // tn) * n_i

    def body(h_ref, wg_ref, wv_ref, c_ref, up_ref, a_ref, g_ref, a1_ref, pend_a, pend_b, carry):
        t = pl.program_id(0)
        row_tile = jnp.maximum(t - 1, 0) % n_i
        w_refs = (wg_ref, wv_ref)

        @pl.when(t == 0)
        def _():
            pend_b[...] = jnp.zeros_like(pend_b)
            carry[...] = jnp.zeros_like(carry)

        def step(dst, src):
            def chunk(c, before):
                rows = pl.ds(c * CONV_CHUNK, CONV_CHUNK)
                u, last = [], []
                for half in (0, 1):
                    up = src[half, rows, :]
                    r1, r2 = _shift_down(up, before[half])
                    u.append(r2 * c_ref[0, half:half + 1, :] + r1 * c_ref[1, half:half + 1, :]
                             + up * c_ref[2, half:half + 1, :] + c_ref[3, half:half + 1, :])
                    last.append(up[CONV_CHUNK - 8:])
                g, dg = _gelu_and_grad(u[0])
                a_ref[rows, :] = (g * u[1]).astype(BF16)
                g_ref[rows, :] = g.astype(BF16)
                a1_ref[rows, :] = (u[1] * dg).astype(BF16)
                return tuple(last)

            edge = tuple(jnp.where(row_tile > 0, carry[half], 0.0) for half in (0, 1))
            n_c = tm // CONV_CHUNK
            rows_m = 2 * tm // n_c
            for c in range(n_c):
                half, mq = c % 2, c // 2
                piece = pl.ds(mq * rows_m, rows_m)
                up = _nn(h_ref[piece, :], w_refs[half][...])
                up_ref[half, piece, :] = up
                dst[half, piece, :] = up
                edge = chunk(c, edge)
            for half in (0, 1):
                carry[half] = edge[half]

        @pl.when(t % 2 == 0)
        def _():
            step(pend_a, pend_b)

        @pl.when(t % 2 == 1)
        def _():
            step(pend_b, pend_a)

    mm = lambda t: jnp.minimum(t, n_t - 1)
    ew = lambda t: jnp.maximum(t - 1, 0)
    out_tile = pl.BlockSpec((tm, tn), lambda t: (ew(t) % n_i, ew(t) // n_i))
    return pl.pallas_call(
        body, name="up_conv_gelu", grid=(n_t + 1,),
        in_specs=[pl.BlockSpec((tm, D), lambda t: (mm(t) % n_i, 0)),
                  pl.BlockSpec((D, tn), lambda t: (0, mm(t) // n_i)),
                  pl.BlockSpec((D, tn), lambda t: (0, FF // tn + mm(t) // n_i)),
                  pl.BlockSpec((4, 2, tn), lambda t: (0, 0, ew(t) // n_i))],
        out_specs=[pl.BlockSpec((2, tm, tn), lambda t: (0, mm(t) % n_i, mm(t) // n_i)), out_tile, out_tile, out_tile],
        out_shape=[jax.ShapeDtypeStruct((2, s, FF), F32)] + [jax.ShapeDtypeStruct((s, FF), BF16)] * 3,
        scratch_shapes=[pltpu.VMEM((2, tm, tn), F32), pltpu.VMEM((2, tm, tn), F32), pltpu.VMEM((2, 8, tn), F32)],
        compiler_params=_cp(("arbitrary",)),
    )(h1b, w_up, w_up, cwb)


def _down_ln2_loss(a, w_down, h1, target, ln2_g, ln2_b, tm=256):
    s = a.shape[0]

    def body(a_ref, w_ref, h_ref, t_ref, g_ref, b_ref, dz_ref, dzb_ref, st_ref):
        @pl.when(pl.program_id(0) == 0)
        def _():
            st_ref[...] = jnp.zeros_like(st_ref)

        z2 = ALPHA * h_ref[...] + _nn(a_ref[...], w_ref[...])
        zh, rstd = _layer_norm_stats(z2)
        diff = zh * g_ref[...] + b_ref[...] - t_ref[...]
        part = 0.5 * jnp.sum(jnp.mean(diff * diff, axis=-1, keepdims=True), axis=0, keepdims=True)
        dy = diff * (1.0 / D)
        st_ref[0:1, :] += jnp.sum(dy * zh, axis=0, keepdims=True)
        st_ref[1:2, :] += jnp.sum(dy, axis=0, keepdims=True)
        st_ref[2:3, :] += jnp.broadcast_to(part, (1, D))
        dz = _layer_norm_bwd(dy, zh, rstd, g_ref[...])
        dz_ref[...] = dz
        dzb_ref[...] = dz.astype(BF16)

    td = pl.BlockSpec((tm, D), lambda i: (i, 0))
    return pl.pallas_call(
        body, name="down_ln2_loss", grid=(s // tm,),
        in_specs=[pl.BlockSpec((tm, FF), lambda i: (i, 0)), _resident((FF, D)), td, td, _const((1, D)), _const((1, D))],
        out_specs=[td, td, _const((8, D))],
        out_shape=[jax.ShapeDtypeStruct((s, D), F32), jax.ShapeDtypeStruct((s, D), BF16),
                   jax.ShapeDtypeStruct((8, D), F32)],
        compiler_params=_cp(("arbitrary",)),
    )(a, w_down, h1, target, _row(ln2_g), _row(ln2_b))


def _conv_gelu_bwd(dz2b, w_down_t, up, g, a1, cwb, tm=512, tn=256):
    s = dz2b.shape[0]
    n_i = s // tm
    n_t = (FF // tn) * n_i

    def body(dz_ref, w_ref, up_ref, g_ref, a1_ref, c_ref, dup_ref, dc_ref, pend_a, pend_b, carry):
        t = pl.program_id(0)
        first = jnp.maximum(t - 1, 0) % n_i == 0

        @pl.when(t == 0)
        def _():
            pend_b[...] = jnp.zeros_like(pend_b)

        @pl.when(first)
        def _():
            carry[...] = jnp.zeros_like(carry)
            dc_ref[...] = jnp.zeros_like(dc_ref)

        def step(dst, src):
            n_c = tm // CONV_CHUNK

            def fold(v):
                return jnp.sum(v.reshape(CONV_CHUNK // 8, 8, v.shape[1]), axis=0)

            def chunk(cc, state):
                after, sums = state
                rows = pl.ds((n_c - 1 - cc) * CONV_CHUNK, CONV_CHUNK)
                da = src[rows, :]
                dus = (da * a1_ref[rows, :].astype(F32), da * g_ref[rows, :].astype(F32))
                head, new_sums = [], []
                for half in (0, 1):
                    du = dus[half]
                    up = up_ref[half, rows, :]
                    l1, l2 = _shift_up(du, after[half])
                    dup = (du * c_ref[2, half:half + 1, :] + l1 * c_ref[1, half:half + 1, :]
                           + l2 * c_ref[0, half:half + 1, :])
                    dup_ref[half, rows, :] = dup.astype(BF16)
                    parts = (fold(l2 * up), fold(l1 * up), fold(du * up), fold(du))
                    new_sums.append(parts if sums is None else tuple(a + b for a, b in zip(sums[half], parts)))
                    head.append(du[:8])
                return tuple(head), new_sums

            state = ((carry[0], carry[1]), None)
            n_m = n_c // CHUNKS_PER_PIECE
            rows_m = tm // n_m
            for mq in range(n_m):
                piece = pl.ds(mq * rows_m, rows_m)
                dst[piece, :] = _nn(dz_ref[piece, :], w_ref[...])
                for c in range(CHUNKS_PER_PIECE):
                    state = chunk(CHUNKS_PER_PIECE * mq + c, state)
            head, sums = state
            for half in (0, 1):
                carry[half] = head[half]
                for k in range(4):
                    dc_ref[k, half:half + 1, :] += jnp.sum(sums[half][k], axis=0, keepdims=True)

        @pl.when(t % 2 == 0)
        def _():
            step(pend_a, pend_b)

        @pl.when(t % 2 == 1)
        def _():
            step(pend_b, pend_a)

    mm = lambda t: jnp.minimum(t, n_t - 1)
    ew = lambda t: jnp.maximum(t - 1, 0)
    row = lambda t: n_i - 1 - t % n_i
    ew_tile = pl.BlockSpec((tm, tn), lambda t: (row(ew(t)), ew(t) // n_i))
    ew_pair = pl.BlockSpec((2, tm, tn), lambda t: (0, row(ew(t)), ew(t) // n_i))
    per_col = pl.BlockSpec((4, 2, tn), lambda t: (0, 0, ew(t) // n_i))
    return pl.pallas_call(
        body, name="conv_gelu_bwd", grid=(n_t + 1,),
        in_specs=[pl.BlockSpec((tm, D), lambda t: (row(mm(t)), 0)),
                  pl.BlockSpec((D, tn), lambda t: (0, mm(t) // n_i)),
                  ew_pair, ew_tile, ew_tile, per_col],
        out_specs=[ew_pair, per_col],
        out_shape=[jax.ShapeDtypeStruct((2, s, FF), BF16), jax.ShapeDtypeStruct((4, 2, FF), F32)],
        scratch_shapes=[pltpu.VMEM((tm, tn), F32), pltpu.VMEM((tm, tn), F32), pltpu.VMEM((2, 8, tn), F32)],
        compiler_params=_cp(("arbitrary",)),
    )(dz2b, w_down_t, up, g, a1, cwb)


def _dh1_ln1_bwd(dz2, dup, w_up, z1, ln1_g, tm=256):
    s = dz2.shape[0]

    def body(dz2_ref, dup_ref, w_ref, z1_ref, g_ref, dz1_ref, dz1b_ref, st_ref):
        @pl.when(pl.program_id(0) == 0)
        def _():
            st_ref[...] = jnp.zeros_like(st_ref)

        dh = ALPHA * dz2_ref[...] + _nt(dup_ref[0], w_ref[:, :FF]) + _nt(dup_ref[1], w_ref[:, FF:])
        zh, rstd = _layer_norm_stats(z1_ref[...])
        st_ref[0:1, :] += jnp.sum(dh * zh, axis=0, keepdims=True)
        st_ref[1:2, :] += jnp.sum(dh, axis=0, keepdims=True)
        dz = _layer_norm_bwd(dh, zh, rstd, g_ref[...])
        dz1_ref[...] = dz
        dz1b_ref[...] = dz.astype(BF16)

    td = pl.BlockSpec((tm, D), lambda i: (i, 0))
    return pl.pallas_call(
        body, name="dh1_ln1_bwd", grid=(s // tm,),
        in_specs=[td, pl.BlockSpec((2, tm, FF), lambda i: (0, i, 0)), _resident((D, 2 * FF)), td, _const((1, D))],
        out_specs=[td, td, _const((8, D))],
        out_shape=[jax.ShapeDtypeStruct((s, D), F32), jax.ShapeDtypeStruct((s, D), BF16),
                   jax.ShapeDtypeStruct((8, D), F32)],
        compiler_params=_cp(("arbitrary",)),
    )(dz2, dup, w_up, z1, _row(ln1_g))


def _dcat_rms_bwd(dz1b, w_o, o_a, o_b, norm_a_g, norm_b_g, tm=256):
    s = dz1b.shape[0]

    def body(dz_ref, w_ref, oa_ref, ob_ref, ga_ref, gb_ref, da_ref, db_ref, st_ref):
        @pl.when(pl.program_id(0) == 0)
        def _():
            st_ref[...] = jnp.zeros_like(st_ref)

        dcat = _nt(dz_ref[...], w_ref[...])
        for k, (o_ref, g_ref, d_ref) in enumerate(((oa_ref, ga_ref, da_ref), (ob_ref, gb_ref, db_ref))):
            o = o_ref[...]
            dn = dcat[:, 512 * k:512 * (k + 1)]
            rr = _rms(o)
            oh = o * rr
            st_ref[k:k + 1, :] += jnp.sum(dn * oh, axis=0, keepdims=True)
            doh = dn * g_ref[...]
            d_ref[...] = rr * (doh - oh * jnp.mean(doh * oh, axis=-1, keepdims=True))

    t512 = pl.BlockSpec((tm, 512), lambda i: (i, 0))
    return pl.pallas_call(
        body, name="dcat_rms_bwd", grid=(s // tm,),
        in_specs=[pl.BlockSpec((tm, D), lambda i: (i, 0)), _resident((D, D)), t512, t512,
                  _const((1, 512)), _const((1, 512))],
        out_specs=[t512, t512, _const((8, 512))],
        out_shape=[jax.ShapeDtypeStruct((s, 512), F32), jax.ShapeDtypeStruct((s, 512), F32),
                   jax.ShapeDtypeStruct((8, 512), F32)],
        compiler_params=_cp(("arbitrary",)),
    )(dz1b, w_o, o_a, o_b, _row(norm_a_g), _row(norm_b_g))


def _dproj_combine(dqa, dka, dva, dqkv_b, tm=256):
    s = dqa.shape[0]

    def body(qa, ka, va, qb, kb, vb, o_ref):
        o_ref[:, 0:512] = qa[...].astype(BF16)
        o_ref[:, 512:640] = ka[...].astype(BF16)
        o_ref[:, 640:768] = va[...].astype(BF16)
        o_ref[:, 768:1280] = qb[...].astype(BF16)
        o_ref[:, 1280:1792] = kb[...].astype(BF16)
        o_ref[:, 1792:2304] = vb[...].astype(BF16)

    t512 = pl.BlockSpec((tm, 512), lambda i: (i, 0))
    t128 = pl.BlockSpec((tm, 128), lambda i: (i, 0))
    return pl.pallas_call(
        body, name="dproj_combine", grid=(s // tm,),
        in_specs=[t512, t128, t128] + [t512] * 3,
        out_specs=pl.BlockSpec((tm, WIN), lambda i: (i, 0)),
        out_shape=jax.ShapeDtypeStruct((s, WIN), BF16),
        compiler_params=_cp(("parallel",)),
    )(dqa, dka, dva, *dqkv_b)


def _grad_x(dz1, dproj, w_in_t, zero, tm=256):
    s = dz1.shape[0]

    def body(dz_ref, dp_ref, w_ref, z_ref, o_ref):
        o_ref[...] = ALPHA * dz_ref[...] + _nn(dp_ref[...], w_ref[...]) + z_ref[0:1, 0:1]

    td = pl.BlockSpec((tm, D), lambda i: (i, 0))
    return pl.pallas_call(
        body, name="grad_x", grid=(s // tm,),
        in_specs=[td, pl.BlockSpec((tm, WIN), lambda i: (i, 0)), _resident((WIN, D)), _const((8, 128))],
        out_specs=td, out_shape=jax.ShapeDtypeStruct((s, D), F32),
        compiler_params=_cp(("parallel",)),
    )(dz1, dproj, w_in_t, zero)


def _place():
    return lax.axis_index("x"), lax.axis_index("y"), lax.axis_index("c")


def _other_chips(x, y):
    return [(1 - x, y), (x, 1 - y), (1 - x, 1 - y)]


def _hbm(a):
    return pltpu.with_memory_space_constraint(a, pltpu.HBM)


def _gather_w_in(shard, conv_w):
    rows_k = shard.shape[0]
    half = rows_k // 2

    def body(src, conv_src, out, conv_out, send_sems, recv_sems):
        x, y, c = _place()
        b = 2 * x + y
        sibling = (x, y, 1 - c)
        chips = _other_chips(x, y)

        def copy(idx, chip_b, core, to, first_hop=False):
            rows = out.at[pl.ds(pl.multiple_of(chip_b * rows_k + core * half, 16), half)]
            s_ref = src.at[pl.ds(pl.multiple_of(core * half, 16), half)] if first_hop else rows
            return pltpu.make_async_remote_copy(src_ref=s_ref, dst_ref=rows, send_sem=send_sems.at[idx],
                                                recv_sem=recv_sems.at[idx], device_id=to, device_id_type=MESH)

        def own_copy():
            return pltpu.make_async_remote_copy(
                src_ref=src, dst_ref=out.at[pl.ds(pl.multiple_of(b * rows_k, 16), rows_k)], send_sem=send_sems.at[6],
                recv_sem=recv_sems.at[6], device_id=sibling, device_id_type=MESH)

        def conv_copy(idx, chip_b, to):
            return pltpu.make_async_remote_copy(src_ref=conv_src, dst_ref=conv_out.at[chip_b],
                                                send_sem=send_sems.at[7 + idx], recv_sem=recv_sems.at[7 + idx],
                                                device_id=to, device_id_type=MESH)

        started = [own_copy(), conv_copy(3, b, sibling)]
        for jn, chip in enumerate(chips):
            started += [copy(jn, b, c, (chip[0], chip[1], c), first_hop=True), conv_copy(jn, b, (chip[0], chip[1], c))]
        for cp in started:
            cp.start()
        for jn, chip in enumerate(chips):
            cb = 2 * chip[0] + chip[1]
            copy(jn, cb, c, (chip[0], chip[1], c)).wait_recv()
            cp = copy(3 + jn, cb, c, sibling)
            cp.start()
            started.append(cp)
        for jn, chip in enumerate(chips):
            cb = 2 * chip[0] + chip[1]
            copy(3 + jn, cb, 1 - c, sibling).wait_recv()
            conv_copy(jn, cb, (chip[0], chip[1], c)).wait_recv()
        own_copy().wait_recv()
        conv_copy(3, b, sibling).wait_recv()
        for cp in started:
            cp.wait_send()

    return pl.pallas_call(
        body, name="gather_w_in",
        in_specs=[ANY, ANY], out_specs=[ANY, ANY],
        out_shape=[jax.ShapeDtypeStruct((N_CHIPS * rows_k, D), BF16), jax.ShapeDtypeStruct((N_CHIPS,) + conv_w.shape, F32)],
        scratch_shapes=[pltpu.SemaphoreType.DMA((11,)), pltpu.SemaphoreType.DMA((11,))],
        compiler_params=pltpu.CompilerParams(has_side_effects=True),
    )(shard, conv_w)


def _weight_copies(shard, land, send_sems, recv_sems, arrivals):
    x, y, c = _place()
    n_rows, n_cols = shard.shape
    peers = [(px, py, c) for px, py in _other_chips(x, y)] + [(x, y, 1 - c)]
    cps = []
    for jn, peer in enumerate(peers):
        at = 2 * peer[0] + peer[1] if arrivals else 2 * x + y
        if land.shape[1] == n_cols:
            dst = land.at[pl.ds(pl.multiple_of(at * n_rows, 16), n_rows)]
        else:
            dst = land.at[:, pl.ds(pl.multiple_of(at * n_cols, 128), n_cols)]
        cps.append(pltpu.make_async_remote_copy(src_ref=shard, dst_ref=dst, send_sem=send_sems.at[jn],
                                                recv_sem=recv_sems.at[jn], device_id=peer, device_id_type=MESH))
    return cps


def _weights_start(shards, after):
    n = len(shards)
    lands = [lax.empty((N_CHIPS * sh.shape[0], D) if sh.shape[1] == D else (D, N_CHIPS * sh.shape[1]), BF16)
             for sh in shards]

    def body(*refs):
        src, land = refs[:n], refs[n:2 * n]
        send_sems, recv_sems = refs[2 * n + 1:3 * n + 1], refs[3 * n + 1:4 * n + 1]
        for k in range(n):
            for send in _weight_copies(src[k], land[k], send_sems[k], recv_sems[k], False):
                send.start()
        refs[-1][...] = jnp.zeros_like(refs[-1])

    res = pl.pallas_call(
        body, name="weights_start",
        in_specs=[HBM] * (2 * n) + [ANY], out_specs=[SEM] * (2 * n) + [HBM] * (2 * n) + [VMEM],
        out_shape=[pltpu.SemaphoreType.DMA((4,))] * (2 * n)
        + [pltpu.HBM(a.shape, a.dtype) for a in (*shards, *lands)] + [jax.ShapeDtypeStruct((8, 128), F32)],
        input_output_aliases={i: i + 2 * n for i in range(2 * n)},
        compiler_params=pltpu.CompilerParams(has_side_effects=DATAFLOW),
    )(*[_hbm(a) for a in (*shards, *lands)], after)
    return [(res[k], res[n + k], res[2 * n + k], res[3 * n + k]) for k in range(n)], res[-1]


def _weights_wait(started, after, name):
    send_sems, recv_sems, shard, land = started

    def body(s_ref, l_ref, send_ref, recv_ref, after_ref, s_out, l_out):
        for cp in _weight_copies(s_ref, l_ref, send_ref, recv_ref, True):
            cp.wait_send()
            cp.wait_recv()

    return pl.pallas_call(
        body, name=name,
        in_specs=[HBM, HBM, SEM, SEM, ANY], out_specs=[HBM, HBM],
        out_shape=[pltpu.HBM(shard.shape, shard.dtype), pltpu.HBM(land.shape, land.dtype)],
        input_output_aliases={0: 0, 1: 1},
        compiler_params=pltpu.CompilerParams(has_side_effects=DATAFLOW),
    )(shard, land, send_sems, recv_sems, after)[1]


def _grad_copies(g_ref, land_ref, send_sems, recv_sems):
    x, y, c = _place()
    cps = []
    for d in range(1, 8):
        px, py, pc = x ^ (d >> 2), y ^ ((d >> 1) & 1), c ^ (d & 1)
        cps.append(pltpu.make_async_remote_copy(
            src_ref=g_ref.at[2 * px + py, pc], dst_ref=land_ref.at[d - 1], send_sem=send_sems.at[d - 1],
            recv_sem=recv_sems.at[d - 1], device_id=(px, py, pc), device_id_type=MESH))
    return cps


def _grads_start(grads_b, name):
    n = len(grads_b)
    lands = [lax.empty((7, g.shape[2], D), BF16) for g in grads_b]

    def body(*refs):
        g, land = refs[:n], refs[n:2 * n]
        send_sems, recv_sems = refs[2 * n:3 * n], refs[3 * n:4 * n]
        for k in range(n):
            for cp in _grad_copies(g[k], land[k], send_sems[k], recv_sems[k]):
                cp.start()
        refs[-1][...] = jnp.zeros_like(refs[-1])

    res = pl.pallas_call(
        body, name=name,
        in_specs=[HBM] * (2 * n), out_specs=[SEM] * (2 * n) + [HBM] * (2 * n) + [VMEM],
        out_shape=[pltpu.SemaphoreType.DMA((7,))] * (2 * n)
        + [pltpu.HBM(a.shape, a.dtype) for a in (*grads_b, *lands)] + [jax.ShapeDtypeStruct((8, 128), F32)],
        input_output_aliases={i: i + 2 * n for i in range(2 * n)},
        compiler_params=pltpu.CompilerParams(has_side_effects=DATAFLOW),
    )(*[_hbm(a) for a in (*grads_b, *lands)])
    return [(res[k], res[n + k], res[2 * n + k], res[3 * n + k]) for k in range(n)], res[-1]


def _grads_wait(started, after, name):
    n = len(started)

    def body(*refs):
        g, land = refs[:n], refs[n:2 * n]
        send_sems, recv_sems = refs[2 * n:3 * n], refs[3 * n:4 * n]
        for k in range(n):
            for cp in _grad_copies(g[k], land[k], send_sems[k], recv_sems[k]):
                cp.wait_send()
                cp.wait_recv()

    gs = [st[2] for st in started]
    lands = [st[3] for st in started]
    res = pl.pallas_call(
        body, name=name,
        in_specs=[HBM] * (2 * n) + [SEM] * (2 * n) + [ANY], out_specs=[HBM] * (2 * n),
        out_shape=[pltpu.HBM(a.shape, a.dtype) for a in (*gs, *lands)],
        input_output_aliases={i: i for i in range(2 * n)},
        compiler_params=pltpu.CompilerParams(has_side_effects=DATAFLOW),
    )(*gs, *lands, *[st[0] for st in started], *[st[1] for st in started], after)
    return res[n:]


def _sum_partials(grad4, got, cb, name, tr):
    h = grad4.shape[2]
    per_half = h // tr

    def body(cb_ref, g_ref, o_ref, out_ref):
        acc = g_ref[...]
        for j in range(7):
            acc = acc + o_ref[j].astype(F32)
        out_ref[...] = acc

    return pl.pallas_call(
        body, name=name,
        grid_spec=pltpu.PrefetchScalarGridSpec(
            num_scalar_prefetch=1, grid=(per_half,),
            in_specs=[pl.BlockSpec((None, None, tr, D), lambda i, cb_ref: (cb_ref[1], cb_ref[0], i, 0)),
                      pl.BlockSpec((7, tr, D), lambda i, cb_ref: (0, i, 0))],
            out_specs=pl.BlockSpec((tr, D), lambda i, cb_ref: (cb_ref[0] * per_half + i, 0))),
        out_shape=jax.ShapeDtypeStruct((2 * h, D), F32),
        compiler_params=_cp(("arbitrary",)),
    )(cb, grad4, got)


def _swap_halves(shards, name):
    n = len(shards)

    def body(*refs):
        out, send_sems, recv_sems = refs[n:2 * n], refs[2 * n], refs[2 * n + 1]
        x, y, c = _place()
        cps = []
        for k in range(n):
            h = shards[k].shape[0] // 2
            mine = out[k].at[pl.ds(pl.multiple_of(c * h, 8), h)]
            cp = pltpu.make_async_remote_copy(src_ref=mine, dst_ref=mine, send_sem=send_sems.at[k],
                                              recv_sem=recv_sems.at[k], device_id=(x, y, 1 - c), device_id_type=MESH)
            cp.start()
            cps.append(cp)
        for cp in cps:
            cp.wait()

    return pl.pallas_call(
        body, name=name,
        in_specs=[ANY] * n, out_specs=[ANY] * n,
        out_shape=[jax.ShapeDtypeStruct(sh.shape, F32) for sh in shards],
        input_output_aliases={k: k for k in range(n)},
        scratch_shapes=[pltpu.SemaphoreType.DMA((n,)), pltpu.SemaphoreType.DMA((n,))],
        compiler_params=pltpu.CompilerParams(has_side_effects=True),
    )(*shards)


def _share_halves(shards, small):
    n = len(shards)
    rows = small.shape[0]

    def body(*refs):
        small_ref = refs[n]
        out, total_ref = refs[n + 1:2 * n + 1], refs[2 * n + 1]
        all_ref, send_sems, recv_sems, ssend, srecv = refs[2 * n + 2:]
        x, y, c = _place()
        me = 4 * x + 2 * y + c
        cps = []
        for k in range(n):
            h = shards[k].shape[0] // 2
            mine = out[k].at[pl.ds(pl.multiple_of(c * h, 8), h)]
            cp = pltpu.make_async_remote_copy(src_ref=mine, dst_ref=mine, send_sem=send_sems.at[k],
                                              recv_sem=recv_sems.at[k], device_id=(x, y, 1 - c), device_id_type=MESH)
            cp.start()
            cps.append(cp)
        all_ref[me] = small_ref[...]
        peers = []
        for d in range(1, 8):
            px, py, pc = x ^ (d >> 2), y ^ ((d >> 1) & 1), c ^ (d & 1)
            cp = pltpu.make_async_remote_copy(src_ref=small_ref, dst_ref=all_ref.at[me],
                                              send_sem=ssend.at[d - 1], recv_sem=srecv.at[d - 1],
                                              device_id=(px, py, pc), device_id_type=MESH)
            cp.start()
            peers.append(cp)
        for cp in peers:
            cp.wait()
        acc = all_ref[0]
        for d in range(1, 8):
            acc = acc + all_ref[d]
        total_ref[...] = acc
        for cp in cps:
            cp.wait()

    return pl.pallas_call(
        body, name="share_halves",
        in_specs=[ANY] * n + [VMEM], out_specs=[ANY] * n + [VMEM],
        out_shape=[jax.ShapeDtypeStruct(sh.shape, F32) for sh in shards] + [jax.ShapeDtypeStruct((rows, D), F32)],
        input_output_aliases={k: k for k in range(n)},
        scratch_shapes=[pltpu.VMEM((8, rows, D), F32), pltpu.SemaphoreType.DMA((n,)), pltpu.SemaphoreType.DMA((n,)),
                        pltpu.SemaphoreType.DMA((7,)), pltpu.SemaphoreType.DMA((7,))],
        compiler_params=pltpu.CompilerParams(has_side_effects=True),
    )(*shards, small)


def _adamw(w, g, m, v, name, tr):
    rows, cols = w.shape

    def body(w_ref, g_ref, m_ref, v_ref, d_ref, nm_ref, nv_ref):
        g_ = g_ref[...]
        nm = ADAM_B1 * m_ref[...] + (1.0 - ADAM_B1) * g_
        nv = ADAM_B2 * v_ref[...] + (1.0 - ADAM_B2) * (g_ * g_)
        m_hat = nm / (1.0 - ADAM_B1 ** ADAM_STEP)
        v_hat = nv / (1.0 - ADAM_B2 ** ADAM_STEP)
        d_ref[...] = -ADAM_LR * (m_hat / (jnp.sqrt(v_hat) + ADAM_EPS) + ADAM_WD * w_ref[...])
        nm_ref[...] = nm
        nv_ref[...] = nv

    spec = pl.BlockSpec((tr, cols), lambda i: (i, 0))
    return pl.pallas_call(
        body, name=name, grid=(rows // tr,),
        in_specs=[spec] * 4, out_specs=[spec] * 3,
        out_shape=[jax.ShapeDtypeStruct((rows, cols), F32)] * 3,
        compiler_params=_cp(("parallel",)),
    )(w, g, m, v)


def _local_step(x, target, w_in_t, late_weights, norm_a_g, norm_b_g, sinks_a, ln1_g, ln1_b,
                conv_w, conv_b, ln2_g, ln2_b, slopes, on_grad):
    cwb = jnp.concatenate([conv_w, conv_b[None]], axis=0).reshape(4, 2, FF)

    proj, xb = _proj(x, w_in_t, "proj")
    o_a, lse_a = _attn_a_fwd(proj, sinks_a)
    fwd_b = [_attn_b_fwd(proj, slopes, r) for r in B_DILATIONS]
    w_o = late_weights(1, fwd_b[-1][1])
    o_b, lse_b, cat, z1, h1, h1b = _mix_ln1(x, o_a, [f[0] for f in fwd_b], [f[1] for f in fwd_b],
                                           norm_a_g, norm_b_g, w_o, ln1_g, ln1_b)
    w_up = late_weights(2, h1b)
    up, a, gate, a1 = _up_conv_gelu(h1b, w_up, cwb)
    w_down = late_weights(3, a)
    dz2, dz2b, st2 = _down_ln2_loss(a, w_down, h1, target, ln2_g, ln2_b)

    on_grad(3, *_grad_w(a, dz2b, "grad_w_down", tm=FF // 2))
    dup, dconv = _conv_gelu_bwd(dz2b, w_down.T, up, gate, a1, cwb)
    on_grad(2, *_grad_w(dup, h1b, "grad_w_up", tm=FF // 2, lhs_halves=True))
    dz1, dz1b, st1 = _dh1_ln1_bwd(dz2, dup, w_up, z1, ln1_g)
    tok = on_grad(1, *_grad_w(cat, dz1b, "grad_w_o", tm=512))
    d_oa, d_ob, st_n = _dcat_rms_bwd(dz1b, w_o, o_a, o_b, norm_a_g + tok[0, 0], norm_b_g)
    dqa, dka, dva, dsink = _attn_a_bwd(proj, sinks_a, d_oa, o_a, lse_a)
    bwd_b = None
    for r in B_DILATIONS:
        bwd_b = _attn_b_bwd(proj, slopes, d_ob, o_b, lse_b, r, bwd_b)
    dproj = _dproj_combine(dqa, dka, dva, bwd_b)
    tok = on_grad(0, *_grad_w(dproj, xb, "grad_w_in", tm=WA))
    gx = _grad_x(dz1, dproj, w_in_t, tok)

    dconv = dconv.reshape(4, 2 * FF)
    small = dict(loss=st2[2, 0:1], norm_a_g=st_n[0], norm_b_g=st_n[1], sinks_a=dsink[:, 0],
                 ln1_g=st1[0], ln1_b=st1[1], conv_w=dconv[0:3].reshape(-1), conv_b=dconv[3],
                 ln2_g=st2[0], ln2_b=st2[1])
    return gx, small


SMALL_ORDER = ("loss", "norm_a_g", "norm_b_g", "sinks_a", "ln1_g", "ln1_b", "conv_b", "ln2_g", "ln2_b", "conv_w")
SMALL_SIZES = dict(loss=1, norm_a_g=512, norm_b_g=512, sinks_a=8, ln1_g=D, ln1_b=D, conv_b=2 * FF, ln2_g=D, ln2_b=D,
                   conv_w=3 * 2 * FF)


def _pack(parts, rows):
    flat = jnp.concatenate([parts[k].reshape(-1).astype(F32) for k in parts])
    return jnp.pad(flat, (0, rows * D - flat.shape[0])).reshape(rows, D)


def _unpack(buf, names, sizes):
    flat = buf.reshape(-1)
    out, at = {}, 0
    for k in names:
        out[k] = flat[at:at + sizes[k]]
        at += sizes[k]
    return out


def kernel(x, w_in, norm_a_g, norm_b_g, sinks_a, w_o, ln1_g, ln1_b, w_up, conv_w, conv_b, w_down, ln2_g, ln2_b, loss_target, m_w_in, m_norm_a_g, m_norm_b_g, m_sinks_a, m_w_o, m_ln1_g, m_ln1_b, m_w_up, m_conv_w, m_conv_b, m_w_down, m_ln2_g, m_ln2_b, v_w_in, v_norm_a_g, v_norm_b_g, v_sinks_a, v_w_o, v_ln1_g, v_ln1_b, v_w_up, v_conv_w, v_conv_b, v_w_down, v_ln2_g, v_ln2_b):
    xi, yi, ci = _place()
    chip = (2 * xi + yi).astype(I32)
    core = ci.astype(I32)

    w_in_rows, m_w_in_rows, v_w_in_rows = w_in.T, m_w_in.T, v_w_in.T
    shards = (w_in_rows.astype(BF16), w_o.astype(BF16), w_up.astype(BF16), w_down.astype(BF16))
    w_in_t, conv_w4 = _gather_w_in(shards[0], conv_w)
    conv_w_f = conv_w4.transpose(1, 0, 2).reshape(3, 2 * FF)
    w_started, w_tok = _weights_start(shards[1:], conv_w4)
    slopes = jnp.asarray(SLOPES, F32) + w_tok[0, 0]

    halves_rows = [r // 2 for r in SHARD_ROWS]
    grads4, grads_b4, started = [None] * 4, [None] * 4, [None] * 4

    def on_grad(k, g, g_b):
        grads4[k] = g.reshape(N_CHIPS, 2, halves_rows[k], D)
        grads_b4[k] = g_b.reshape(N_CHIPS, 2, halves_rows[k], D)
        if k > 1:
            return None
        group = (1, 2, 3) if k == 1 else (0,)
        sts, tok = _grads_start([grads_b4[i] for i in group], f"grads_start_{k}")
        for i, st in zip(group, sts):
            started[i] = st
        return tok

    gx, small = _local_step(
        x[0], loss_target[0], w_in_t, lambda k, after: _weights_wait(w_started[k - 1], after, f"weights_wait_{k}"),
        norm_a_g, norm_b_g, sinks_a, ln1_g, ln1_b, conv_w_f, conv_b, ln2_g, ln2_b, slopes, on_grad)

    tiles = (96, 128, 352, 176)
    core_chip = jnp.stack([core, chip])
    got = _grads_wait(started[1:], gx, "grads_wait_1")
    halves = [_sum_partials(grads4[k], got[k - 1], core_chip, f"sum_partials_{k}", tiles[k]) for k in (1, 2, 3)]
    g_w_o, g_w_up_rows, g_w_down = _swap_halves(halves, "swap_halves")
    g_w_up = g_w_up_rows.T
    delta, new_m, new_v = {}, {}, {}
    for k, g, tr in (("w_o", g_w_o, 128), ("w_up", g_w_up, 256), ("w_down", g_w_down, 176)):
        delta[k], new_m[k], new_v[k] = _adamw(dict(w_o=w_o, w_up=w_up, w_down=w_down)[k], g,
                                              dict(w_o=m_w_o, w_up=m_w_up, w_down=m_w_down)[k],
                                              dict(w_o=v_w_o, w_up=v_w_up, w_down=v_w_down)[k], f"adamw_{k}", tr)

    got = _grads_wait(started[:1], delta["w_up"], "grads_wait_0")
    half_in = _sum_partials(grads4[0], got[0], core_chip, "sum_partials_0", tiles[0])
    small_rows = 32
    g_w_in_rows, totals = _share_halves([half_in], _pack({k: small[k] for k in SMALL_ORDER}, small_rows))
    tot = _unpack(totals, SMALL_ORDER, SMALL_SIZES)
    loss = tot["loss"][0]
    cols = 2 * FF // N_CHIPS
    g_conv_w = lax.dynamic_slice(tot["conv_w"].reshape(3, 2 * FF), (0, chip * cols), (3, cols))
    g_small = dict(norm_a_g=tot["norm_a_g"], norm_b_g=tot["norm_b_g"], sinks_a=tot["sinks_a"], ln1_g=tot["ln1_g"],
                   ln1_b=tot["ln1_b"], conv_w=g_conv_w, conv_b=tot["conv_b"], ln2_g=tot["ln2_g"], ln2_b=tot["ln2_b"])

    weights = dict(w_in=w_in, norm_a_g=norm_a_g, norm_b_g=norm_b_g, sinks_a=sinks_a, w_o=w_o, ln1_g=ln1_g, ln1_b=ln1_b,
                   w_up=w_up, conv_w=conv_w, conv_b=conv_b, w_down=w_down, ln2_g=ln2_g, ln2_b=ln2_b)
    ms = dict(w_in=m_w_in, norm_a_g=m_norm_a_g, norm_b_g=m_norm_b_g, sinks_a=m_sinks_a, w_o=m_w_o, ln1_g=m_ln1_g,
              ln1_b=m_ln1_b, w_up=m_w_up, conv_w=m_conv_w, conv_b=m_conv_b, w_down=m_w_down, ln2_g=m_ln2_g, ln2_b=m_ln2_b)
    vs = dict(w_in=v_w_in, norm_a_g=v_norm_a_g, norm_b_g=v_norm_b_g, sinks_a=v_sinks_a, w_o=v_w_o, ln1_g=v_ln1_g,
              ln1_b=v_ln1_b, w_up=v_w_up, conv_w=v_conv_w, conv_b=v_conv_b, w_down=v_w_down, ln2_g=v_ln2_g, ln2_b=v_ln2_b)
    order = list(weights)
    grad = dict(g_small, w_in=g_w_in_rows.T, w_o=g_w_o, w_up=g_w_up, w_down=g_w_down)

    delta["w_in"], new_m["w_in"], new_v["w_in"] = [
        a.T for a in _adamw(w_in_rows, g_w_in_rows, m_w_in_rows, v_w_in_rows, "adamw_w_in", 144)]
    small_names = [k for k in order if k not in delta]
    sizes = {k: weights[k].size for k in small_names}
    rows = 16
    packed = [_pack({k: src[k] for k in small_names}, rows) for src in (weights, grad, ms, vs)]
    for res, buf in zip((delta, new_m, new_v), _adamw(*packed, "adamw_small", rows)):
        for k, val in _unpack(buf, small_names, sizes).items():
            res[k] = val.reshape(weights[k].shape)

    return (loss, gx[None], *[grad[k] for k in order], *[delta[k] for k in order],
            *[new_m[k] for k in order], *[new_v[k] for k in order])
```

```python
import functools
import math

import jax
import jax.numpy as jnp
from jax import lax
from jax.experimental import pallas as pl
from jax.experimental.pallas import tpu as pltpu

F32, BF16, I32 = jnp.float32, jnp.bfloat16, jnp.int32

D = 1024
FF = 2816
HD = 64
NH = 8
WA, WB = 768, 1536
WIN = WA + WB
BLK = 128
ALPHA = 2.0 ** 0.25
LN_EPS, RMS_EPS = 1e-5, 1e-6
SCALE = 1.0 / math.sqrt(HD)
A_MAX_DIST, B_MAX_DIST = 127, 128
B_DILATIONS = (1, 4, 16)
SLOPES = tuple(2.0 ** (-(i + 1)) for i in range(NH))
SHARD_ROWS = (WIN // 4, D // 4, 2 * FF // 4, FF // 4)
N_CHIPS = 4
ADAM_LR, ADAM_B1, ADAM_B2, ADAM_EPS, ADAM_WD, ADAM_STEP = 0.001, 0.9, 0.999, 1e-08, 0.01, 10
MESH = pl.DeviceIdType.MESH
ANY = pl.BlockSpec(memory_space=pl.ANY)
SMEM = pl.BlockSpec(memory_space=pltpu.SMEM)
VMEM = pl.BlockSpec(memory_space=pltpu.VMEM)
HBM = pl.BlockSpec(memory_space=pltpu.HBM)
SEM = pl.BlockSpec(memory_space=pltpu.SEMAPHORE)
DATAFLOW = pltpu.SideEffectType.DATAFLOW_SIDE_EFFECTING


def _cp(sem, mb=48):
    return pltpu.CompilerParams(dimension_semantics=sem, vmem_limit_bytes=mb << 20)


def _nn(a, b):
    return lax.dot_general(a, b, (((1,), (0,)), ((), ())), preferred_element_type=F32)


def _nt(a, b):
    return lax.dot_general(a, b, (((1,), (1,)), ((), ())), preferred_element_type=F32)


def _tn(a, b):
    return lax.dot_general(a, b, (((0,), (0,)), ((), ())), preferred_element_type=F32)


def _resident(shape):
    n = len(shape)
    return pl.BlockSpec(shape, lambda *_: (0,) * n, pipeline_mode=pl.Buffered(1))


def _const(shape):
    n = len(shape)
    return pl.BlockSpec(shape, lambda *_: (0,) * n)


def _proj(x, w_t, name, tm=512):
    s = x.shape[0]
    n = w_t.shape[0]

    def body(x_ref, w_ref, o_ref, xb_ref):
        xb = x_ref[...].astype(BF16)
        xb_ref[...] = xb
        o_ref[...] = _nt(xb, w_ref[...])

    return pl.pallas_call(
        body, name=name, grid=(s // tm,),
        in_specs=[pl.BlockSpec((tm, D), lambda i: (i, 0)), _resident((n, D))],
        out_specs=[pl.BlockSpec((tm, n), lambda i: (i, 0)), pl.BlockSpec((tm, D), lambda i: (i, 0))],
        out_shape=[jax.ShapeDtypeStruct((s, n), F32), jax.ShapeDtypeStruct((s, D), BF16)],
        compiler_params=_cp(("parallel",)),
    )(x, w_t)


def _grad_w(lhs, rhs, name, tm, tk=512, lhs_halves=False):
    s = rhs.shape[0]
    if lhs_halves:
        per_half = lhs.shape[2] // tm
        n = 2 * lhs.shape[2]
        lhs_spec = pl.BlockSpec((None, tk, tm), lambda i, k: (i // per_half, k, i % per_half))
    else:
        n = lhs.shape[1]
        lhs_spec = pl.BlockSpec((tk, tm), lambda i, k: (k, i))
    nk = s // tk

    def body(l_ref, r_ref, o_ref, ob_ref):
        k = pl.program_id(1)

        @pl.when(k == 0)
        def _():
            o_ref[...] = jnp.zeros_like(o_ref)

        o_ref[...] += _tn(l_ref[...].astype(BF16), r_ref[...].astype(BF16))

        @pl.when(k == nk - 1)
        def _():
            ob_ref[...] = o_ref[...].astype(BF16)

    return pl.pallas_call(
        body, name=name, grid=(n // tm, nk),
        in_specs=[lhs_spec, pl.BlockSpec((tk, D), lambda i, k: (k, 0))],
        out_specs=[pl.BlockSpec((tm, D), lambda i, k: (i, 0))] * 2,
        out_shape=[jax.ShapeDtypeStruct((n, D), F32), jax.ShapeDtypeStruct((n, D), BF16)],
        compiler_params=_cp(("parallel", "arbitrary")),
    )(lhs, rhs)


def _band_base(max_dist, dist_unit, first):
    row = lax.broadcasted_iota(I32, (BLK, 2 * BLK), 0)
    col = lax.broadcasted_iota(I32, (BLK, 2 * BLK), 1)
    dist = BLK + row - col
    ok = (dist >= 0) & (dist <= max_dist)
    if first:
        ok = ok & (col >= BLK)
    return jnp.where(ok, dist.astype(F32) * (-float(dist_unit)), -jnp.inf)


def _half_mask(shape, e):
    lane = lax.broadcasted_iota(I32, shape, 1)
    return (lane < HD) if e == 0 else (lane >= HD)


def _to_half(x, e, g):
    if g != e:
        x = pltpu.roll(x, HD, 1)
    return jnp.where(_half_mask(x.shape, g), x, 0.0)


def _stack_heads(scalars, tile):
    return jnp.concatenate([scalars[0] * tile, scalars[1] * tile], axis=0)


def _pair_fwd(q2, kb, vb, base, slopes, kv_heads, sinks):
    lo = _half_mask((BLK, 2 * HD), 0)
    if sinks is not None:
        o2 = lse2 = None
        for e in (0, 1):
            g = kv_heads[e]
            qv = (_to_half(q2, e, g) * SCALE).astype(BF16)
            s = _nt(qv, kb) + slopes[e] * base
            m = jnp.maximum(jnp.max(s, axis=1, keepdims=True), sinks[e])
            p = jnp.exp(s - m)
            l = jnp.sum(p, axis=1, keepdims=True) + jnp.exp(sinks[e] - m)
            oh = _nn(p.astype(BF16), vb) / l
            if g != e:
                oh = pltpu.roll(oh, HD, 1)
            lse = jnp.broadcast_to(m + jnp.log(l), (BLK, 2 * HD))
            o2 = oh if e == 0 else jnp.where(lo, o2, oh)
            lse2 = lse if e == 0 else jnp.where(lo, lse2, lse)
        return o2, lse2
    qs = jnp.concatenate([_to_half(q2, e, kv_heads[e]) * SCALE for e in (0, 1)], axis=0).astype(BF16)
    s = _nt(qs, kb) + (base if slopes is None else _stack_heads(slopes, base))
    m = jnp.max(s, axis=1, keepdims=True)
    p = jnp.exp(s - m)
    l = jnp.sum(p, axis=1, keepdims=True)
    o = _nn(p.astype(BF16), vb) / l
    lse = m + jnp.log(l)
    halves = []
    for e in (0, 1):
        oh = o[e * BLK:(e + 1) * BLK]
        halves.append(pltpu.roll(oh, HD, 1) if kv_heads[e] != e else oh)
    o2 = jnp.where(lo, halves[0], halves[1])
    lse2 = jnp.where(lo, jnp.broadcast_to(lse[:BLK], (BLK, 2 * HD)), jnp.broadcast_to(lse[BLK:], (BLK, 2 * HD)))
    return o2, lse2


def _pair_bwd(q2, kb, vb, do2, o2, lse2, base, slopes, kv_heads, sinks):
    lo = _half_mask((BLK, 2 * HD), 0)
    prod = do2 * o2
    lses, deltas = [], []
    for e in (0, 1):
        hq = _half_mask((BLK, 2 * HD), e)
        lses.append(jnp.max(jnp.where(hq, lse2, -jnp.inf), axis=1, keepdims=True))
        deltas.append(jnp.sum(jnp.where(hq, prod, 0.0), axis=1, keepdims=True))
    lse = jnp.concatenate(lses, axis=0)
    delta = jnp.concatenate(deltas, axis=0)
    qs = jnp.concatenate([_to_half(q2, e, kv_heads[e]) * SCALE for e in (0, 1)], axis=0).astype(BF16)
    dos = jnp.concatenate([_to_half(do2, e, kv_heads[e]) for e in (0, 1)], axis=0).astype(BF16)
    p = jnp.exp(_nt(qs, kb) + (base if slopes is None else _stack_heads(slopes, base)) - lse)
    ds = (p * (_nt(dos, vb) - delta)).astype(BF16)
    dq = _nn(ds, kb) * SCALE
    halves = []
    for e in (0, 1):
        dqh = dq[e * BLK:(e + 1) * BLK]
        halves.append(pltpu.roll(dqh, HD, 1) if kv_heads[e] != e else dqh)
    dq2 = jnp.where(lo, halves[0], halves[1])
    dk2 = _tn(ds, qs)
    dv2 = _tn(p.astype(BF16), dos)
    dsinks = []
    if sinks is not None:
        for e in (0, 1):
            dsinks.append(jnp.sum(-jnp.exp(sinks[e] - lses[e]) * deltas[e], axis=0, keepdims=True))
    return dq2, dk2, dv2, dsinks


A_BLOCKS_PER_STEP = 2
A_BLOCKS_PER_STEP_BWD = 1


def _attn_a_fwd(proj, sinks):
    s = proj.shape[0]
    nq = A_BLOCKS_PER_STEP
    rows = BLK * nq
    steps = s // rows

    def body(sink_ref, q_ref, kp_ref, kc_ref, vp_ref, vc_ref, o_ref, lse_ref):
        n = pl.program_id(0)
        base_rest = _band_base(A_MAX_DIST, 1, False)
        base_0 = jnp.where(n > 0, base_rest, _band_base(A_MAX_DIST, 1, True))
        for i in range(nq):
            cur = pl.ds(i * BLK, BLK)
            k_prev = kc_ref[pl.ds((i - 1) * BLK, BLK), :] if i > 0 else kp_ref[...]
            v_prev = vc_ref[pl.ds((i - 1) * BLK, BLK), :] if i > 0 else vp_ref[...]
            kb = jnp.concatenate([k_prev, kc_ref[cur, :]], axis=0).astype(BF16)
            vb = jnp.concatenate([v_prev, vc_ref[cur, :]], axis=0).astype(BF16)
            for j in range(NH // 2):
                g = j // 2
                o2, lse2 = _pair_fwd(q_ref[cur, 128 * j:128 * (j + 1)], kb, vb, base_rest if i > 0 else base_0,
                                     (SLOPES[2 * j], SLOPES[2 * j + 1]), (g, g), (sink_ref[2 * j], sink_ref[2 * j + 1]))
                o_ref[cur, 128 * j:128 * (j + 1)] = o2
                lse_ref[cur, 128 * j:128 * (j + 1)] = lse2

    before = lambda n: jnp.maximum(n * nq - 1, 0)
    return pl.pallas_call(
        body, name="attn_a_fwd", grid=(steps,),
        in_specs=[SMEM,
                  pl.BlockSpec((rows, 512), lambda n: (n, 0)),
                  pl.BlockSpec((BLK, 128), lambda n: (before(n), 4)), pl.BlockSpec((rows, 128), lambda n: (n, 4)),
                  pl.BlockSpec((BLK, 128), lambda n: (before(n), 5)), pl.BlockSpec((rows, 128), lambda n: (n, 5))],
        out_specs=[pl.BlockSpec((rows, 512), lambda n: (n, 0))] * 2,
        out_shape=[jax.ShapeDtypeStruct((s, 512), F32)] * 2,
        compiler_params=_cp(("parallel",)),
    )(sinks, proj, proj, proj, proj, proj)


def _attn_a_bwd(proj, sinks, d_o, o, lse):
    s = proj.shape[0]
    nq = A_BLOCKS_PER_STEP_BWD
    rows = BLK * nq
    steps = s // rows

    def body(sink_ref, q_ref, kp_ref, kc_ref, vp_ref, vc_ref, do_ref, o_ref, lse_ref,
             dq_ref, dk_ref, dv_ref, dsink_ref, kcar, vcar):
        n = pl.program_id(0)

        @pl.when(n == 0)
        def _():
            kcar[...] = jnp.zeros_like(kcar)
            vcar[...] = jnp.zeros_like(vcar)
            dsink_ref[...] = jnp.zeros_like(dsink_ref)

        dk_ref[...] = kcar[...]
        dv_ref[...] = vcar[...]

        @pl.when(n < steps)
        def _():
            base_rest = _band_base(A_MAX_DIST, 1, False)
            base_0 = jnp.where(n > 0, base_rest, _band_base(A_MAX_DIST, 1, True))
            for i in range(nq):
                cur = pl.ds(i * BLK, BLK)
                k_prev = kc_ref[pl.ds((i - 1) * BLK, BLK), :] if i > 0 else kp_ref[...]
                v_prev = vc_ref[pl.ds((i - 1) * BLK, BLK), :] if i > 0 else vp_ref[...]
                kb = jnp.concatenate([k_prev, kc_ref[cur, :]], axis=0).astype(BF16)
                vb = jnp.concatenate([v_prev, vc_ref[cur, :]], axis=0).astype(BF16)
                dk_win = dv_win = None
                for j in range(NH // 2):
                    g = j // 2
                    sl = slice(128 * j, 128 * (j + 1))
                    dq2, dk2, dv2, dsk = _pair_bwd(q_ref[cur, sl], kb, vb, do_ref[cur, sl], o_ref[cur, sl],
                                                   lse_ref[cur, sl], base_rest if i > 0 else base_0,
                                                   (SLOPES[2 * j], SLOPES[2 * j + 1]), (g, g),
                                                   (sink_ref[2 * j], sink_ref[2 * j + 1]))
                    dq_ref[cur, sl] = dq2
                    dk_win = dk2 if j == 0 else dk_win + dk2
                    dv_win = dv2 if j == 0 else dv_win + dv2
                    for e in (0, 1):
                        h = 2 * j + e
                        dsink_ref[h:h + 1, :] += jnp.broadcast_to(dsk[e], (1, 128))
                if i == 0:
                    last = pl.ds((nq - 1) * BLK, BLK)
                    dk_ref[last, :] += dk_win[:BLK]
                    dv_ref[last, :] += dv_win[:BLK]
                else:
                    kcar[pl.ds((i - 1) * BLK, BLK), :] += dk_win[:BLK]
                    vcar[pl.ds((i - 1) * BLK, BLK), :] += dv_win[:BLK]
                kcar[cur, :] = dk_win[BLK:]
                vcar[cur, :] = dv_win[BLK:]

    cur_step = lambda n: jnp.minimum(n, steps - 1)
    before = lambda n: jnp.maximum(cur_step(n) * nq - 1, 0)
    out_prev = lambda n: jnp.maximum(n - 1, 0)
    wide = pl.BlockSpec((rows, 512), lambda n: (cur_step(n), 0))
    return pl.pallas_call(
        body, name="attn_a_bwd", grid=(steps + 1,),
        in_specs=[SMEM, wide,
                  pl.BlockSpec((BLK, 128), lambda n: (before(n), 4)), pl.BlockSpec((rows, 128), lambda n: (cur_step(n), 4)),
                  pl.BlockSpec((BLK, 128), lambda n: (before(n), 5)), pl.BlockSpec((rows, 128), lambda n: (cur_step(n), 5)),
                  wide, wide, wide],
        out_specs=[wide,
                   pl.BlockSpec((rows, 128), lambda n: (out_prev(n), 0)),
                   pl.BlockSpec((rows, 128), lambda n: (out_prev(n), 0)),
                   pl.BlockSpec((NH, 128), lambda n: (0, 0))],
        out_shape=[jax.ShapeDtypeStruct((s, 512), F32), jax.ShapeDtypeStruct((s, 128), F32),
                   jax.ShapeDtypeStruct((s, 128), F32), jax.ShapeDtypeStruct((NH, 128), F32)],
        scratch_shapes=[pltpu.VMEM((rows, 128), F32), pltpu.VMEM((rows, 128), F32)],
        compiler_params=_cp(("arbitrary",)),
    )(sinks, proj, proj, proj, proj, proj, d_o, o, lse)


def _stream(rho, i, r):
    start = i * BLK * r + rho
    return pl.ds(start, BLK, stride=r) if r > 1 else pl.ds(start, BLK)


def _for_streams(r, fn, side_by_side=4):
    if r <= side_by_side:
        for rho in range(r):
            fn(rho)
    else:
        def group(it, carry):
            for u in range(side_by_side):
                fn(side_by_side * it + u)
            return carry

        lax.fori_loop(0, r // side_by_side, group, 0)


B_BLOCKS_PER_STEP = {1: 8, 4: 2, 16: 1}
B_BLOCKS_PER_STEP_FWD = {1: 8, 4: 2, 16: 1}


def _attn_b_fwd(proj, slopes, r):
    s = proj.shape[0]
    nq = B_BLOCKS_PER_STEP_FWD[r]
    rows = BLK * r * nq
    steps = s // rows
    qc, kc, vc = WA // 128, WA // 128 + 4, WA // 128 + 8

    def body(slope_ref, q_ref, kp_ref, kc_ref, vp_ref, vc_ref, o_ref, lse_ref):
        j = pl.program_id(0)
        sb = pl.program_id(1)
        sl2 = (slope_ref[2 * j], slope_ref[2 * j + 1])
        bias_rest = _stack_heads(sl2, _band_base(B_MAX_DIST, r, False))
        bias_0 = jnp.where(sb > 0, bias_rest, _stack_heads(sl2, _band_base(B_MAX_DIST, r, True)))

        def stream(rho):
            for i in range(nq):
                cur = _stream(rho, i, r)
                k_prev = kc_ref[_stream(rho, i - 1, r), :] if i > 0 else kp_ref[_stream(rho, 0, r), :]
                v_prev = vc_ref[_stream(rho, i - 1, r), :] if i > 0 else vp_ref[_stream(rho, 0, r), :]
                kb = jnp.concatenate([k_prev, kc_ref[cur, :]], axis=0).astype(BF16)
                vb = jnp.concatenate([v_prev, vc_ref[cur, :]], axis=0).astype(BF16)
                o2, lse2 = _pair_fwd(q_ref[cur, :], kb, vb, bias_rest if i > 0 else bias_0, None, (0, 1), None)
                o_ref[cur, :] = o2
                lse_ref[cur, :] = lse2

        _for_streams(r, stream, side_by_side=8)

    before = lambda sb: jnp.maximum(sb * nq - 1, 0)
    return pl.pallas_call(
        body, name=f"attn_b_fwd_r{r}", grid=(NH // 2, steps),
        in_specs=[SMEM,
                  pl.BlockSpec((rows, 128), lambda j, sb: (sb, qc + j)),
                  pl.BlockSpec((BLK * r, 128), lambda j, sb: (before(sb), kc + j)),
                  pl.BlockSpec((rows, 128), lambda j, sb: (sb, kc + j)),
                  pl.BlockSpec((BLK * r, 128), lambda j, sb: (before(sb), vc + j)),
                  pl.BlockSpec((rows, 128), lambda j, sb: (sb, vc + j))],
        out_specs=[pl.BlockSpec((rows, 128), lambda j, sb: (sb, j))] * 2,
        out_shape=[jax.ShapeDtypeStruct((s, 512), F32)] * 2,
        compiler_params=_cp(("parallel", "parallel")),
    )(slopes, proj, proj, proj, proj, proj)


def _attn_b_bwd(proj, slopes, d_o, o, lse, r, so_far=None):
    s = proj.shape[0]
    nq = B_BLOCKS_PER_STEP[r]
    rows = BLK * r * nq
    steps = s // rows
    qc, kc, vc = WA // 128, WA // 128 + 4, WA // 128 + 8
    chained = so_far is not None

    def body(slope_ref, q_ref, kp_ref, kc_ref, vp_ref, vc_ref, do_ref, o_ref, lse_ref, *rest):
        if chained:
            pq_ref, pk_ref, pv_ref, dq_ref, dk_ref, dv_ref, kcar, vcar = rest
        else:
            dq_ref, dk_ref, dv_ref, kcar, vcar = rest
        j = pl.program_id(0)
        sb = pl.program_id(1)

        @pl.when(sb == 0)
        def _():
            kcar[...] = jnp.zeros_like(kcar)
            vcar[...] = jnp.zeros_like(vcar)

        if chained:
            dk_ref[...] = kcar[...] + pk_ref[...]
            dv_ref[...] = vcar[...] + pv_ref[...]
        else:
            dk_ref[...] = kcar[...]
            dv_ref[...] = vcar[...]

        @pl.when(sb < steps)
        def _():
            sl2 = (slope_ref[2 * j], slope_ref[2 * j + 1])
            bias_rest = _stack_heads(sl2, _band_base(B_MAX_DIST, r, False))
            bias_0 = jnp.where(sb > 0, bias_rest, _stack_heads(sl2, _band_base(B_MAX_DIST, r, True)))

            def stream(rho):
                for i in range(nq):
                    cur = _stream(rho, i, r)
                    k_prev = kc_ref[_stream(rho, i - 1, r), :] if i > 0 else kp_ref[_stream(rho, 0, r), :]
                    v_prev = vc_ref[_stream(rho, i - 1, r), :] if i > 0 else vp_ref[_stream(rho, 0, r), :]
                    kb = jnp.concatenate([k_prev, kc_ref[cur, :]], axis=0).astype(BF16)
                    vb = jnp.concatenate([v_prev, vc_ref[cur, :]], axis=0).astype(BF16)
                    dq2, dk2, dv2, _ = _pair_bwd(q_ref[cur, :], kb, vb, do_ref[cur, :], o_ref[cur, :], lse_ref[cur, :],
                                                 bias_rest if i > 0 else bias_0, None, (0, 1), None)
                    dq_ref[cur, :] = dq2 + pq_ref[cur, :] if chained else dq2
                    if i == 0:
                        last = _stream(rho, nq - 1, r)
                        dk_ref[last, :] += dk2[:BLK]
                        dv_ref[last, :] += dv2[:BLK]
                    else:
                        kcar[_stream(rho, i - 1, r), :] += dk2[:BLK]
                        vcar[_stream(rho, i - 1, r), :] += dv2[:BLK]
                    kcar[cur, :] = dk2[BLK:]
                    vcar[cur, :] = dv2[BLK:]

            _for_streams(r, stream, side_by_side=8)

    cur_step = lambda sb: jnp.minimum(sb, steps - 1)
    before = lambda sb: jnp.maximum(cur_step(sb) * nq - 1, 0)
    out_prev = lambda sb: jnp.maximum(sb - 1, 0)
    tile = lambda col: pl.BlockSpec((rows, 128), lambda j, sb: (cur_step(sb), col + j))
    edge = lambda col: pl.BlockSpec((BLK * r, 128), lambda j, sb: (before(sb), col + j))
    late = pl.BlockSpec((rows, 128), lambda j, sb: (out_prev(sb), j))
    grads = [tile(0), late, late]
    return pl.pallas_call(
        body, name=f"attn_b_bwd_r{r}", grid=(NH // 2, steps + 1),
        in_specs=[SMEM, tile(qc), edge(kc), tile(kc), edge(vc), tile(vc), tile(0), tile(0), tile(0)]
        + (grads if chained else []),
        out_specs=grads,
        out_shape=[jax.ShapeDtypeStruct((s, 512), F32)] * 3,
        scratch_shapes=[pltpu.VMEM((rows, 128), F32), pltpu.VMEM((rows, 128), F32)],
        compiler_params=_cp(("parallel", "arbitrary")),
    )(slopes, proj, proj, proj, proj, proj, d_o, o, lse, *(so_far if chained else ()))


def _row(v):
    return v.reshape(1, -1)


def _layer_norm_stats(z):
    mu = jnp.mean(z, axis=-1, keepdims=True)
    zc = z - mu
    var = jnp.mean(zc * zc, axis=-1, keepdims=True)
    rstd = lax.rsqrt(var + LN_EPS)
    return zc * rstd, rstd


def _layer_norm_bwd(dh, zh, rstd, g):
    dzh = dh * g
    return rstd * (dzh - jnp.mean(dzh, axis=-1, keepdims=True) - zh * jnp.mean(dzh * zh, axis=-1, keepdims=True))


def _rms(o):
    return lax.rsqrt(jnp.mean(o * o, axis=-1, keepdims=True) + RMS_EPS)


def _mix_ln1(x, o_a, o_b, lse_b, norm_a_g, norm_b_g, w_o, ln1_g, ln1_b, tm=256):
    s = x.shape[0]

    def body(x_ref, oa_ref, ob1, ob2, ob3, l1, l2, l3, ga_ref, gb_ref, wo_ref, g_ref, b_ref,
             obm_ref, lse_ref, cat_ref, z1_ref, h1_ref, h1b_ref):
        la, lb, lc = l1[...], l2[...], l3[...]
        m = jnp.maximum(jnp.maximum(la, lb), lc)
        ea, eb, ec = jnp.exp(la - m), jnp.exp(lb - m), jnp.exp(lc - m)
        den = ea + eb + ec
        obm = (ea / den) * ob1[...] + (eb / den) * ob2[...] + (ec / den) * ob3[...]
        obm_ref[...] = obm
        lse_ref[...] = m + jnp.log(den)
        oa = oa_ref[...]
        na = oa * _rms(oa) * ga_ref[...]
        nb_ = obm * _rms(obm) * gb_ref[...]
        cat = jnp.concatenate([na, nb_], axis=1).astype(BF16)
        cat_ref[...] = cat
        z1 = ALPHA * x_ref[...] + _nn(cat, wo_ref[...])
        z1_ref[...] = z1
        zh, _ = _layer_norm_stats(z1)
        h1 = zh * g_ref[...] + b_ref[...]
        h1_ref[...] = h1
        h1b_ref[...] = h1.astype(BF16)

    t512 = pl.BlockSpec((tm, 512), lambda i: (i, 0))
    td = pl.BlockSpec((tm, D), lambda i: (i, 0))
    return pl.pallas_call(
        body, name="mix_ln1", grid=(s // tm,),
        in_specs=[td] + [t512] * 7 + [_const((1, 512))] * 2 + [_resident((D, D))] + [_const((1, D))] * 2,
        out_specs=[t512, t512, td, td, td, td],
        out_shape=[jax.ShapeDtypeStruct((s, 512), F32), jax.ShapeDtypeStruct((s, 512), F32),
                   jax.ShapeDtypeStruct((s, D), BF16), jax.ShapeDtypeStruct((s, D), F32),
                   jax.ShapeDtypeStruct((s, D), F32), jax.ShapeDtypeStruct((s, D), BF16)],
        compiler_params=_cp(("parallel",)),
    )(x, o_a, *o_b, *lse_b, _row(norm_a_g), _row(norm_b_g), w_o, _row(ln1_g), _row(ln1_b))


def _gelu_and_grad(x):
    c = math.sqrt(2.0 / math.pi)
    x2 = x * x
    cx = c * x
    t = jnp.tanh(cx * (1.0 + 0.044715 * x2))
    q = 1.0 + t
    g = (0.5 * x) * q
    dg = 0.5 * q + ((0.5 * cx) * (1.0 - t * t)) * (1.0 + (3.0 * 0.044715) * x2)
    return g, dg


CONV_CHUNK = 64


def _shift_down(u, before):
    n = u.shape[0]
    ext = jnp.concatenate([before, u], axis=0)
    return pltpu.roll(ext, 1, 0)[8:], pltpu.roll(ext, 2, 0)[8:]


def _shift_up(u, after):
    n = u.shape[0]
    ext = jnp.concatenate([u, after], axis=0)
    return pltpu.roll(ext, n + 7, 0)[:n], pltpu.roll(ext, n + 6, 0)[:n]


def _up_conv_gelu(h1b, w_up, cwb, tm=512, tn=256):
    s = h1b.shape[0]
    n_i = s // tm
    n_t = (FF // tn) * n_i

    def body(h_ref, wg_ref, wv_ref, c_ref, up_ref, a_ref, g_ref, a1_ref, pend_a, pend_b, carry):
        t = pl.program_id(0)
        row_tile = jnp.maximum(t - 1, 0) % n_i
        w_refs = (wg_ref, wv_ref)

        @pl.when(t == 0)
        def _():
            pend_b[...] = jnp.zeros_like(pend_b)
            carry[...] = jnp.zeros_like(carry)

        def step(dst, src):
            def chunk(c, before):
                rows = pl.ds(c * CONV_CHUNK, CONV_CHUNK)
                u, last = [], []
                for half in (0, 1):
                    up = src[half, rows, :]
                    r1, r2 = _shift_down(up, before[half])
                    u.append(r2 * c_ref[0, half:half + 1, :] + r1 * c_ref[1, half:half + 1, :]
                             + up * c_ref[2, half:half + 1, :] + c_ref[3, half:half + 1, :])
                    last.append(up[CONV_CHUNK - 8:])
                g, dg = _gelu_and_grad(u[0])
                a_ref[rows, :] = (g * u[1]).astype(BF16)
                g_ref[rows, :] = g.astype(BF16)
                a1_ref[rows, :] = (u[1] * dg).astype(BF16)
                return tuple(last)

            edge = tuple(jnp.where(row_tile > 0, carry[half], 0.0) for half in (0, 1))
            n_c = tm // CONV_CHUNK
            n_k = n_c // 2
            tk = D // n_k
            for half in (0, 1):
                up = None
                for kq in range(n_k):
                    ks = slice(kq * tk, (kq + 1) * tk)
                    part = _nn(h_ref[:, ks], w_refs[half][ks, :])
                    up = part if kq == 0 else up + part
                    edge = chunk(half * n_k + kq, edge)
                up_ref[half] = up.astype(BF16)
                dst[half] = up
            for half in (0, 1):
                carry[half] = edge[half]

        @pl.when(t % 2 == 0)
        def _():
            step(pend_a, pend_b)

        @pl.when(t % 2 == 1)
        def _():
            step(pend_b, pend_a)

    mm = lambda t: jnp.minimum(t, n_t - 1)
    ew = lambda t: jnp.maximum(t - 1, 0)
    out_tile = pl.BlockSpec((tm, tn), lambda t: (ew(t) % n_i, ew(t) // n_i))
    return pl.pallas_call(
        body, name="up_conv_gelu", grid=(n_t + 1,),
        in_specs=[pl.BlockSpec((tm, D), lambda t: (mm(t) % n_i, 0)),
                  pl.BlockSpec((D, tn), lambda t: (0, mm(t) // n_i)),
                  pl.BlockSpec((D, tn), lambda t: (0, FF // tn + mm(t) // n_i)),
                  pl.BlockSpec((4, 2, tn), lambda t: (0, 0, ew(t) // n_i))],
        out_specs=[pl.BlockSpec((2, tm, tn), lambda t: (0, mm(t) % n_i, mm(t) // n_i)), out_tile, out_tile, out_tile],
        out_shape=[jax.ShapeDtypeStruct((2, s, FF), BF16)] + [jax.ShapeDtypeStruct((s, FF), BF16)] * 3,
        scratch_shapes=[pltpu.VMEM((2, tm, tn), F32), pltpu.VMEM((2, tm, tn), F32), pltpu.VMEM((2, 8, tn), F32)],
        compiler_params=_cp(("arbitrary",)),
    )(h1b, w_up, w_up, cwb)


def _down_ln2_loss(a, w_down, h1, target, ln2_g, ln2_b, tm=256):
    s = a.shape[0]

    def body(a_ref, w_ref, h_ref, t_ref, g_ref, b_ref, dz_ref, dzb_ref, st_ref):
        @pl.when(pl.program_id(0) == 0)
        def _():
            st_ref[...] = jnp.zeros_like(st_ref)

        z2 = ALPHA * h_ref[...] + _nn(a_ref[...], w_ref[...])
        zh, rstd = _layer_norm_stats(z2)
        diff = zh * g_ref[...] + b_ref[...] - t_ref[...]
        part = 0.5 * jnp.sum(jnp.mean(diff * diff, axis=-1, keepdims=True), axis=0, keepdims=True)
        dy = diff * (1.0 / D)
        st_ref[0:1, :] += jnp.sum(dy * zh, axis=0, keepdims=True)
        st_ref[1:2, :] += jnp.sum(dy, axis=0, keepdims=True)
        st_ref[2:3, :] += jnp.broadcast_to(part, (1, D))
        dz = _layer_norm_bwd(dy, zh, rstd, g_ref[...])
        dz_ref[...] = dz
        dzb_ref[...] = dz.astype(BF16)

    td = pl.BlockSpec((tm, D), lambda i: (i, 0))
    return pl.pallas_call(
        body, name="down_ln2_loss", grid=(s // tm,),
        in_specs=[pl.BlockSpec((tm, FF), lambda i: (i, 0)), _resident((FF, D)), td, td, _const((1, D)), _const((1, D))],
        out_specs=[td, td, _const((8, D))],
        out_shape=[jax.ShapeDtypeStruct((s, D), F32), jax.ShapeDtypeStruct((s, D), BF16),
                   jax.ShapeDtypeStruct((8, D), F32)],
        compiler_params=_cp(("arbitrary",)),
    )(a, w_down, h1, target, _row(ln2_g), _row(ln2_b))


def _conv_gelu_bwd(dz2b, w_down_t, up, g, a1, cwb, tm=512, tn=256):
    s = dz2b.shape[0]
    n_i = s // tm
    n_t = (FF // tn) * n_i

    def body(dz_ref, w_ref, up_ref, g_ref, a1_ref, c_ref, dup_ref, dc_ref, pend_a, pend_b, carry):
        t = pl.program_id(0)
        first = jnp.maximum(t - 1, 0) % n_i == 0

        @pl.when(t == 0)
        def _():
            pend_b[...] = jnp.zeros_like(pend_b)

        @pl.when(first)
        def _():
            carry[...] = jnp.zeros_like(carry)
            dc_ref[...] = jnp.zeros_like(dc_ref)

        def step(dst, src):
            n_c = tm // CONV_CHUNK

            def fold(v):
                return jnp.sum(v.reshape(CONV_CHUNK // 8, 8, v.shape[1]), axis=0)

            def chunk(cc, state):
                after, sums = state
                rows = pl.ds((n_c - 1 - cc) * CONV_CHUNK, CONV_CHUNK)
                da = src[rows, :]
                dus = (da * a1_ref[rows, :].astype(F32), da * g_ref[rows, :].astype(F32))
                head, new_sums = [], []
                for half in (0, 1):
                    du = dus[half]
                    up = up_ref[half, rows, :].astype(F32)
                    l1, l2 = _shift_up(du, after[half])
                    dup = (du * c_ref[2, half:half + 1, :] + l1 * c_ref[1, half:half + 1, :]
                           + l2 * c_ref[0, half:half + 1, :])
                    dup_ref[half, rows, :] = dup.astype(BF16)
                    parts = (fold(l2 * up), fold(l1 * up), fold(du * up), fold(du))
                    new_sums.append(parts if sums is None else tuple(a + b for a, b in zip(sums[half], parts)))
                    head.append(du[:8])
                return tuple(head), new_sums

            state = ((carry[0], carry[1]), None)
            n_k = n_c // 2
            tk = D // n_k
            da = None
            for kq in range(n_k):
                ks = slice(kq * tk, (kq + 1) * tk)
                part = _nn(dz_ref[:, ks], w_ref[ks, :])
                da = part if kq == 0 else da + part
                state = chunk(2 * kq, state)
                state = chunk(2 * kq + 1, state)
            head, sums = state
            for half in (0, 1):
                carry[half] = head[half]
                for k in range(4):
                    dc_ref[k, half:half + 1, :] += jnp.sum(sums[half][k], axis=0, keepdims=True)
            dst[...] = da

        @pl.when(t % 2 == 0)
        def _():
            step(pend_a, pend_b)

        @pl.when(t % 2 == 1)
        def _():
            step(pend_b, pend_a)

    mm = lambda t: jnp.minimum(t, n_t - 1)
    ew = lambda t: jnp.maximum(t - 1, 0)
    row = lambda t: n_i - 1 - t % n_i
    ew_tile = pl.BlockSpec((tm, tn), lambda t: (row(ew(t)), ew(t) // n_i))
    ew_pair = pl.BlockSpec((2, tm, tn), lambda t: (0, row(ew(t)), ew(t) // n_i))
    per_col = pl.BlockSpec((4, 2, tn), lambda t: (0, 0, ew(t) // n_i))
    return pl.pallas_call(
        body, name="conv_gelu_bwd", grid=(n_t + 1,),
        in_specs=[pl.BlockSpec((tm, D), lambda t: (row(mm(t)), 0)),
                  pl.BlockSpec((D, tn), lambda t: (0, mm(t) // n_i)),
                  ew_pair, ew_tile, ew_tile, per_col],
        out_specs=[ew_pair, per_col],
        out_shape=[jax.ShapeDtypeStruct((2, s, FF), BF16), jax.ShapeDtypeStruct((4, 2, FF), F32)],
        scratch_shapes=[pltpu.VMEM((tm, tn), F32), pltpu.VMEM((tm, tn), F32), pltpu.VMEM((2, 8, tn), F32)],
        compiler_params=_cp(("arbitrary",)),
    )(dz2b, w_down_t, up, g, a1, cwb)


def _dh1_ln1_bwd(dz2, dup, w_up, z1, ln1_g, tm=256):
    s = dz2.shape[0]

    def body(dz2_ref, dup_ref, w_ref, z1_ref, g_ref, dz1_ref, dz1b_ref, st_ref):
        @pl.when(pl.program_id(0) == 0)
        def _():
            st_ref[...] = jnp.zeros_like(st_ref)

        dh = ALPHA * dz2_ref[...] + _nt(dup_ref[0], w_ref[:, :FF]) + _nt(dup_ref[1], w_ref[:, FF:])
        zh, rstd = _layer_norm_stats(z1_ref[...])
        st_ref[0:1, :] += jnp.sum(dh * zh, axis=0, keepdims=True)
        st_ref[1:2, :] += jnp.sum(dh, axis=0, keepdims=True)
        dz = _layer_norm_bwd(dh, zh, rstd, g_ref[...])
        dz1_ref[...] = dz
        dz1b_ref[...] = dz.astype(BF16)

    td = pl.BlockSpec((tm, D), lambda i: (i, 0))
    return pl.pallas_call(
        body, name="dh1_ln1_bwd", grid=(s // tm,),
        in_specs=[td, pl.BlockSpec((2, tm, FF), lambda i: (0, i, 0)), _resident((D, 2 * FF)), td, _const((1, D))],
        out_specs=[td, td, _const((8, D))],
        out_shape=[jax.ShapeDtypeStruct((s, D), F32), jax.ShapeDtypeStruct((s, D), BF16),
                   jax.ShapeDtypeStruct((8, D), F32)],
        compiler_params=_cp(("arbitrary",)),
    )(dz2, dup, w_up, z1, _row(ln1_g))


def _dcat_rms_bwd(dz1b, w_o, o_a, o_b, norm_a_g, norm_b_g, tm=256):
    s = dz1b.shape[0]

    def body(dz_ref, w_ref, oa_ref, ob_ref, ga_ref, gb_ref, da_ref, db_ref, st_ref):
        @pl.when(pl.program_id(0) == 0)
        def _():
            st_ref[...] = jnp.zeros_like(st_ref)

        dcat = _nt(dz_ref[...], w_ref[...])
        for k, (o_ref, g_ref, d_ref) in enumerate(((oa_ref, ga_ref, da_ref), (ob_ref, gb_ref, db_ref))):
            o = o_ref[...]
            dn = dcat[:, 512 * k:512 * (k + 1)]
            rr = _rms(o)
            oh = o * rr
            st_ref[k:k + 1, :] += jnp.sum(dn * oh, axis=0, keepdims=True)
            doh = dn * g_ref[...]
            d_ref[...] = rr * (doh - oh * jnp.mean(doh * oh, axis=-1, keepdims=True))

    t512 = pl.BlockSpec((tm, 512), lambda i: (i, 0))
    return pl.pallas_call(
        body, name="dcat_rms_bwd", grid=(s // tm,),
        in_specs=[pl.BlockSpec((tm, D), lambda i: (i, 0)), _resident((D, D)), t512, t512,
                  _const((1, 512)), _const((1, 512))],
        out_specs=[t512, t512, _const((8, 512))],
        out_shape=[jax.ShapeDtypeStruct((s, 512), F32), jax.ShapeDtypeStruct((s, 512), F32),
                   jax.ShapeDtypeStruct((8, 512), F32)],
        compiler_params=_cp(("arbitrary",)),
    )(dz1b, w_o, o_a, o_b, _row(norm_a_g), _row(norm_b_g))


def _dproj_combine(dqa, dka, dva, dqkv_b, tm=256):
    s = dqa.shape[0]

    def body(qa, ka, va, qb, kb, vb, o_ref):
        o_ref[:, 0:512] = qa[...].astype(BF16)
        o_ref[:, 512:640] = ka[...].astype(BF16)
        o_ref[:, 640:768] = va[...].astype(BF16)
        o_ref[:, 768:1280] = qb[...].astype(BF16)
        o_ref[:, 1280:1792] = kb[...].astype(BF16)
        o_ref[:, 1792:2304] = vb[...].astype(BF16)

    t512 = pl.BlockSpec((tm, 512), lambda i: (i, 0))
    t128 = pl.BlockSpec((tm, 128), lambda i: (i, 0))
    return pl.pallas_call(
        body, name="dproj_combine", grid=(s // tm,),
        in_specs=[t512, t128, t128] + [t512] * 3,
        out_specs=pl.BlockSpec((tm, WIN), lambda i: (i, 0)),
        out_shape=jax.ShapeDtypeStruct((s, WIN), BF16),
        compiler_params=_cp(("parallel",)),
    )(dqa, dka, dva, *dqkv_b)


def _grad_x(dz1, dproj, w_in_t, zero, tm=256):
    s = dz1.shape[0]

    def body(dz_ref, dp_ref, w_ref, z_ref, o_ref):
        o_ref[...] = ALPHA * dz_ref[...] + _nn(dp_ref[...], w_ref[...]) + z_ref[0:1, 0:1]

    td = pl.BlockSpec((tm, D), lambda i: (i, 0))
    return pl.pallas_call(
        body, name="grad_x", grid=(s // tm,),
        in_specs=[td, pl.BlockSpec((tm, WIN), lambda i: (i, 0)), _resident((WIN, D)), _const((8, 128))],
        out_specs=td, out_shape=jax.ShapeDtypeStruct((s, D), F32),
        compiler_params=_cp(("parallel",)),
    )(dz1, dproj, w_in_t, zero)


def _place():
    return lax.axis_index("x"), lax.axis_index("y"), lax.axis_index("c")


def _other_chips(x, y):
    return [(1 - x, y), (x, 1 - y), (1 - x, 1 - y)]


def _hbm(a):
    return pltpu.with_memory_space_constraint(a, pltpu.HBM)


def _gather_w_in(shard, conv_w):
    rows_k = shard.shape[0]
    half = rows_k // 2

    def body(src, conv_src, out, conv_out, send_sems, recv_sems):
        x, y, c = _place()
        b = 2 * x + y
        sibling = (x, y, 1 - c)
        chips = _other_chips(x, y)

        def copy(idx, chip_b, core, to, first_hop=False):
            rows = out.at[pl.ds(pl.multiple_of(chip_b * rows_k + core * half, 16), half)]
            s_ref = src.at[pl.ds(pl.multiple_of(core * half, 16), half)] if first_hop else rows
            return pltpu.make_async_remote_copy(src_ref=s_ref, dst_ref=rows, send_sem=send_sems.at[idx],
                                                recv_sem=recv_sems.at[idx], device_id=to, device_id_type=MESH)

        def own_copy():
            return pltpu.make_async_remote_copy(
                src_ref=src, dst_ref=out.at[pl.ds(pl.multiple_of(b * rows_k, 16), rows_k)], send_sem=send_sems.at[6],
                recv_sem=recv_sems.at[6], device_id=sibling, device_id_type=MESH)

        def conv_copy(idx, chip_b, to):
            return pltpu.make_async_remote_copy(src_ref=conv_src, dst_ref=conv_out.at[chip_b],
                                                send_sem=send_sems.at[7 + idx], recv_sem=recv_sems.at[7 + idx],
                                                device_id=to, device_id_type=MESH)

        started = [own_copy(), conv_copy(3, b, sibling)]
        for jn, chip in enumerate(chips):
            started += [copy(jn, b, c, (chip[0], chip[1], c), first_hop=True), conv_copy(jn, b, (chip[0], chip[1], c))]
        for cp in started:
            cp.start()
        for jn, chip in enumerate(chips):
            cb = 2 * chip[0] + chip[1]
            copy(jn, cb, c, (chip[0], chip[1], c)).wait_recv()
            cp = copy(3 + jn, cb, c, sibling)
            cp.start()
            started.append(cp)
        for jn, chip in enumerate(chips):
            cb = 2 * chip[0] + chip[1]
            copy(3 + jn, cb, 1 - c, sibling).wait_recv()
            conv_copy(jn, cb, (chip[0], chip[1], c)).wait_recv()
        own_copy().wait_recv()
        conv_copy(3, b, sibling).wait_recv()
        for cp in started:
            cp.wait_send()

    return pl.pallas_call(
        body, name="gather_w_in",
        in_specs=[ANY, ANY], out_specs=[ANY, ANY],
        out_shape=[jax.ShapeDtypeStruct((N_CHIPS * rows_k, D), BF16), jax.ShapeDtypeStruct((N_CHIPS,) + conv_w.shape, F32)],
        scratch_shapes=[pltpu.SemaphoreType.DMA((11,)), pltpu.SemaphoreType.DMA((11,))],
        compiler_params=pltpu.CompilerParams(has_side_effects=True),
    )(shard, conv_w)


def _weight_copies(shard, land, send_sems, recv_sems, arrivals):
    x, y, c = _place()
    n_rows, n_cols = shard.shape
    peers = [(px, py, c) for px, py in _other_chips(x, y)] + [(x, y, 1 - c)]
    cps = []
    for jn, peer in enumerate(peers):
        at = 2 * peer[0] + peer[1] if arrivals else 2 * x + y
        if land.shape[1] == n_cols:
            dst = land.at[pl.ds(pl.multiple_of(at * n_rows, 16), n_rows)]
        else:
            dst = land.at[:, pl.ds(pl.multiple_of(at * n_cols, 128), n_cols)]
        cps.append(pltpu.make_async_remote_copy(src_ref=shard, dst_ref=dst, send_sem=send_sems.at[jn],
                                                recv_sem=recv_sems.at[jn], device_id=peer, device_id_type=MESH))
    return cps


def _weights_start(shards, after):
    n = len(shards)
    lands = [lax.empty((N_CHIPS * sh.shape[0], D) if sh.shape[1] == D else (D, N_CHIPS * sh.shape[1]), BF16)
             for sh in shards]

    def body(*refs):
        src, land = refs[:n], refs[n:2 * n]
        send_sems, recv_sems = refs[2 * n + 1:3 * n + 1], refs[3 * n + 1:4 * n + 1]
        for k in range(n):
            for send in _weight_copies(src[k], land[k], send_sems[k], recv_sems[k], False):
                send.start()
        refs[-1][...] = jnp.zeros_like(refs[-1])

    res = pl.pallas_call(
        body, name="weights_start",
        in_specs=[HBM] * (2 * n) + [ANY], out_specs=[SEM] * (2 * n) + [HBM] * (2 * n) + [VMEM],
        out_shape=[pltpu.SemaphoreType.DMA((4,))] * (2 * n)
        + [pltpu.HBM(a.shape, a.dtype) for a in (*shards, *lands)] + [jax.ShapeDtypeStruct((8, 128), F32)],
        input_output_aliases={i: i + 2 * n for i in range(2 * n)},
        compiler_params=pltpu.CompilerParams(has_side_effects=DATAFLOW),
    )(*[_hbm(a) for a in (*shards, *lands)], after)
    return [(res[k], res[n + k], res[2 * n + k], res[3 * n + k]) for k in range(n)], res[-1]


def _weights_wait(started, after, name):
    send_sems, recv_sems, shard, land = started

    def body(s_ref, l_ref, send_ref, recv_ref, after_ref, s_out, l_out):
        for cp in _weight_copies(s_ref, l_ref, send_ref, recv_ref, True):
            cp.wait_send()
            cp.wait_recv()

    return pl.pallas_call(
        body, name=name,
        in_specs=[HBM, HBM, SEM, SEM, ANY], out_specs=[HBM, HBM],
        out_shape=[pltpu.HBM(shard.shape, shard.dtype), pltpu.HBM(land.shape, land.dtype)],
        input_output_aliases={0: 0, 1: 1},
        compiler_params=pltpu.CompilerParams(has_side_effects=DATAFLOW),
    )(shard, land, send_sems, recv_sems, after)[1]


def _grad_copies(g_ref, land_ref, send_sems, recv_sems):
    x, y, c = _place()
    cps = []
    for d in range(1, 8):
        px, py, pc = x ^ (d >> 2), y ^ ((d >> 1) & 1), c ^ (d & 1)
        cps.append(pltpu.make_async_remote_copy(
            src_ref=g_ref.at[2 * px + py, pc], dst_ref=land_ref.at[d - 1], send_sem=send_sems.at[d - 1],
            recv_sem=recv_sems.at[d - 1], device_id=(px, py, pc), device_id_type=MESH))
    return cps


def _grads_start(grads_b, name):
    n = len(grads_b)
    lands = [lax.empty((7, g.shape[2], D), BF16) for g in grads_b]

    def body(*refs):
        g, land = refs[:n], refs[n:2 * n]
        send_sems, recv_sems = refs[2 * n:3 * n], refs[3 * n:4 * n]
        for k in range(n):
            for cp in _grad_copies(g[k], land[k], send_sems[k], recv_sems[k]):
                cp.start()
        refs[-1][...] = jnp.zeros_like(refs[-1])

    res = pl.pallas_call(
        body, name=name,
        in_specs=[HBM] * (2 * n), out_specs=[SEM] * (2 * n) + [HBM] * (2 * n) + [VMEM],
        out_shape=[pltpu.SemaphoreType.DMA((7,))] * (2 * n)
        + [pltpu.HBM(a.shape, a.dtype) for a in (*grads_b, *lands)] + [jax.ShapeDtypeStruct((8, 128), F32)],
        input_output_aliases={i: i + 2 * n for i in range(2 * n)},
        compiler_params=pltpu.CompilerParams(has_side_effects=DATAFLOW),
    )(*[_hbm(a) for a in (*grads_b, *lands)])
    return [(res[k], res[n + k], res[2 * n + k], res[3 * n + k]) for k in range(n)], res[-1]


def _grads_wait(started, after, name):
    n = len(started)

    def body(*refs):
        g, land = refs[:n], refs[n:2 * n]
        send_sems, recv_sems = refs[2 * n:3 * n], refs[3 * n:4 * n]
        for k in range(n):
            for cp in _grad_copies(g[k], land[k], send_sems[k], recv_sems[k]):
                cp.wait_send()
                cp.wait_recv()

    gs = [st[2] for st in started]
    lands = [st[3] for st in started]
    res = pl.pallas_call(
        body, name=name,
        in_specs=[HBM] * (2 * n) + [SEM] * (2 * n) + [ANY], out_specs=[HBM] * (2 * n),
        out_shape=[pltpu.HBM(a.shape, a.dtype) for a in (*gs, *lands)],
        input_output_aliases={i: i for i in range(2 * n)},
        compiler_params=pltpu.CompilerParams(has_side_effects=DATAFLOW),
    )(*gs, *lands, *[st[0] for st in started], *[st[1] for st in started], after)
    return res[n:]


def _sum_partials(grad4, got, cb, name, tr):
    h = grad4.shape[2]
    per_half = h // tr

    def body(cb_ref, g_ref, o_ref, out_ref):
        acc = g_ref[...]
        for j in range(7):
            acc = acc + o_ref[j].astype(F32)
        out_ref[...] = acc

    return pl.pallas_call(
        body, name=name,
        grid_spec=pltpu.PrefetchScalarGridSpec(
            num_scalar_prefetch=1, grid=(per_half,),
            in_specs=[pl.BlockSpec((None, None, tr, D), lambda i, cb_ref: (cb_ref[1], cb_ref[0], i, 0)),
                      pl.BlockSpec((7, tr, D), lambda i, cb_ref: (0, i, 0))],
            out_specs=pl.BlockSpec((tr, D), lambda i, cb_ref: (cb_ref[0] * per_half + i, 0))),
        out_shape=jax.ShapeDtypeStruct((2 * h, D), F32),
        compiler_params=_cp(("arbitrary",)),
    )(cb, grad4, got)


def _swap_halves(shards, name):
    n = len(shards)

    def body(*refs):
        out, send_sems, recv_sems = refs[n:2 * n], refs[2 * n], refs[2 * n + 1]
        x, y, c = _place()
        cps = []
        for k in range(n):
            h = shards[k].shape[0] // 2
            mine = out[k].at[pl.ds(pl.multiple_of(c * h, 8), h)]
            cp = pltpu.make_async_remote_copy(src_ref=mine, dst_ref=mine, send_sem=send_sems.at[k],
                                              recv_sem=recv_sems.at[k], device_id=(x, y, 1 - c), device_id_type=MESH)
            cp.start()
            cps.append(cp)
        for cp in cps:
            cp.wait()

    return pl.pallas_call(
        body, name=name,
        in_specs=[ANY] * n, out_specs=[ANY] * n,
        out_shape=[jax.ShapeDtypeStruct(sh.shape, F32) for sh in shards],
        input_output_aliases={k: k for k in range(n)},
        scratch_shapes=[pltpu.SemaphoreType.DMA((n,)), pltpu.SemaphoreType.DMA((n,))],
        compiler_params=pltpu.CompilerParams(has_side_effects=True),
    )(*shards)


def _share_halves(shards, small):
    n = len(shards)
    rows = small.shape[0]

    def body(*refs):
        small_ref = refs[n]
        out, total_ref = refs[n + 1:2 * n + 1], refs[2 * n + 1]
        all_ref, send_sems, recv_sems, ssend, srecv = refs[2 * n + 2:]
        x, y, c = _place()
        me = 4 * x + 2 * y + c
        cps = []
        for k in range(n):
            h = shards[k].shape[0] // 2
            mine = out[k].at[pl.ds(pl.multiple_of(c * h, 8), h)]
            cp = pltpu.make_async_remote_copy(src_ref=mine, dst_ref=mine, send_sem=send_sems.at[k],
                                              recv_sem=recv_sems.at[k], device_id=(x, y, 1 - c), device_id_type=MESH)
            cp.start()
            cps.append(cp)
        all_ref[me] = small_ref[...]
        peers = []
        for d in range(1, 8):
            px, py, pc = x ^ (d >> 2), y ^ ((d >> 1) & 1), c ^ (d & 1)
            cp = pltpu.make_async_remote_copy(src_ref=small_ref, dst_ref=all_ref.at[me],
                                              send_sem=ssend.at[d - 1], recv_sem=srecv.at[d - 1],
                                              device_id=(px, py, pc), device_id_type=MESH)
            cp.start()
            peers.append(cp)
        for cp in peers:
            cp.wait()
        acc = all_ref[0]
        for d in range(1, 8):
            acc = acc + all_ref[d]
        total_ref[...] = acc
        for cp in cps:
            cp.wait()

    return pl.pallas_call(
        body, name="share_halves",
        in_specs=[ANY] * n + [VMEM], out_specs=[ANY] * n + [VMEM],
        out_shape=[jax.ShapeDtypeStruct(sh.shape, F32) for sh in shards] + [jax.ShapeDtypeStruct((rows, D), F32)],
        input_output_aliases={k: k for k in range(n)},
        scratch_shapes=[pltpu.VMEM((8, rows, D), F32), pltpu.SemaphoreType.DMA((n,)), pltpu.SemaphoreType.DMA((n,)),
                        pltpu.SemaphoreType.DMA((7,)), pltpu.SemaphoreType.DMA((7,))],
        compiler_params=pltpu.CompilerParams(has_side_effects=True),
    )(*shards, small)


def _adamw(w, g, m, v, name, tr):
    rows, cols = w.shape

    def body(w_ref, g_ref, m_ref, v_ref, d_ref, nm_ref, nv_ref):
        g_ = g_ref[...]
        nm = ADAM_B1 * m_ref[...] + (1.0 - ADAM_B1) * g_
        nv = ADAM_B2 * v_ref[...] + (1.0 - ADAM_B2) * (g_ * g_)
        m_hat = nm / (1.0 - ADAM_B1 ** ADAM_STEP)
        v_hat = nv / (1.0 - ADAM_B2 ** ADAM_STEP)
        d_ref[...] = -ADAM_LR * (m_hat / (jnp.sqrt(v_hat) + ADAM_EPS) + ADAM_WD * w_ref[...])
        nm_ref[...] = nm
        nv_ref[...] = nv

    spec = pl.BlockSpec((tr, cols), lambda i: (i, 0))
    return pl.pallas_call(
        body, name=name, grid=(rows // tr,),
        in_specs=[spec] * 4, out_specs=[spec] * 3,
        out_shape=[jax.ShapeDtypeStruct((rows, cols), F32)] * 3,
        compiler_params=_cp(("parallel",)),
    )(w, g, m, v)


def _local_step(x, target, w_in_t, late_weights, norm_a_g, norm_b_g, sinks_a, ln1_g, ln1_b,
                conv_w, conv_b, ln2_g, ln2_b, slopes, on_grad):
    cwb = jnp.concatenate([conv_w, conv_b[None]], axis=0).reshape(4, 2, FF)

    proj, xb = _proj(x, w_in_t, "proj")
    o_a, lse_a = _attn_a_fwd(proj, sinks_a)
    fwd_b = [_attn_b_fwd(proj, slopes, r) for r in B_DILATIONS]
    w_o = late_weights(1, fwd_b[-1][1])
    o_b, lse_b, cat, z1, h1, h1b = _mix_ln1(x, o_a, [f[0] for f in fwd_b], [f[1] for f in fwd_b],
                                           norm_a_g, norm_b_g, w_o, ln1_g, ln1_b)
    w_up = late_weights(2, h1b)
    up, a, gate, a1 = _up_conv_gelu(h1b, w_up, cwb)
    w_down = late_weights(3, a)
    dz2, dz2b, st2 = _down_ln2_loss(a, w_down, h1, target, ln2_g, ln2_b)

    on_grad(3, *_grad_w(a, dz2b, "grad_w_down", tm=FF // 2))
    dup, dconv = _conv_gelu_bwd(dz2b, w_down.T, up, gate, a1, cwb)
    on_grad(2, *_grad_w(dup, h1b, "grad_w_up", tm=FF // 2, lhs_halves=True))
    dz1, dz1b, st1 = _dh1_ln1_bwd(dz2, dup, w_up, z1, ln1_g)
    tok = on_grad(1, *_grad_w(cat, dz1b, "grad_w_o", tm=512))
    d_oa, d_ob, st_n = _dcat_rms_bwd(dz1b, w_o, o_a, o_b, norm_a_g + tok[0, 0], norm_b_g)
    dqa, dka, dva, dsink = _attn_a_bwd(proj, sinks_a, d_oa, o_a, lse_a)
    bwd_b = None
    for r in B_DILATIONS:
        bwd_b = _attn_b_bwd(proj, slopes, d_ob, o_b, lse_b, r, bwd_b)
    dproj = _dproj_combine(dqa, dka, dva, bwd_b)
    tok = on_grad(0, *_grad_w(dproj, xb, "grad_w_in", tm=WA))
    gx = _grad_x(dz1, dproj, w_in_t, tok)

    dconv = dconv.reshape(4, 2 * FF)
    small = dict(loss=st2[2, 0:1], norm_a_g=st_n[0], norm_b_g=st_n[1], sinks_a=dsink[:, 0],
                 ln1_g=st1[0], ln1_b=st1[1], conv_w=dconv[0:3].reshape(-1), conv_b=dconv[3],
                 ln2_g=st2[0], ln2_b=st2[1])
    return gx, small


SMALL_ORDER = ("loss", "norm_a_g", "norm_b_g", "sinks_a", "ln1_g", "ln1_b", "conv_b", "ln2_g", "ln2_b", "conv_w")
SMALL_SIZES = dict(loss=1, norm_a_g=512, norm_b_g=512, sinks_a=8, ln1_g=D, ln1_b=D, conv_b=2 * FF, ln2_g=D, ln2_b=D,
                   conv_w=3 * 2 * FF)


def _pack(parts, rows):
    flat = jnp.concatenate([parts[k].reshape(-1).astype(F32) for k in parts])
    return jnp.pad(flat, (0, rows * D - flat.shape[0])).reshape(rows, D)


def _unpack(buf, names, sizes):
    flat = buf.reshape(-1)
    out, at = {}, 0
    for k in names:
        out[k] = flat[at:at + sizes[k]]
        at += sizes[k]
    return out


def kernel(x, w_in, norm_a_g, norm_b_g, sinks_a, w_o, ln1_g, ln1_b, w_up, conv_w, conv_b, w_down, ln2_g, ln2_b, loss_target, m_w_in, m_norm_a_g, m_norm_b_g, m_sinks_a, m_w_o, m_ln1_g, m_ln1_b, m_w_up, m_conv_w, m_conv_b, m_w_down, m_ln2_g, m_ln2_b, v_w_in, v_norm_a_g, v_norm_b_g, v_sinks_a, v_w_o, v_ln1_g, v_ln1_b, v_w_up, v_conv_w, v_conv_b, v_w_down, v_ln2_g, v_ln2_b):
    xi, yi, ci = _place()
    chip = (2 * xi + yi).astype(I32)
    core = ci.astype(I32)

    w_in_rows, m_w_in_rows, v_w_in_rows = w_in.T, m_w_in.T, v_w_in.T
    shards = (w_in_rows.astype(BF16), w_o.astype(BF16), w_up.astype(BF16), w_down.astype(BF16))
    w_in_t, conv_w4 = _gather_w_in(shards[0], conv_w)
    conv_w_f = conv_w4.transpose(1, 0, 2).reshape(3, 2 * FF)
    w_started, w_tok = _weights_start(shards[1:], conv_w4)
    slopes = jnp.asarray(SLOPES, F32) + w_tok[0, 0]

    halves_rows = [r // 2 for r in SHARD_ROWS]
    grads4, grads_b4, started = [None] * 4, [None] * 4, [None] * 4

    def on_grad(k, g, g_b):
        grads4[k] = g.reshape(N_CHIPS, 2, halves_rows[k], D)
        grads_b4[k] = g_b.reshape(N_CHIPS, 2, halves_rows[k], D)
        if k > 1:
            return None
        group = (1, 2, 3) if k == 1 else (0,)
        sts, tok = _grads_start([grads_b4[i] for i in group], f"grads_start_{k}")
        for i, st in zip(group, sts):
            started[i] = st
        return tok

    gx, small = _local_step(
        x[0], loss_target[0], w_in_t, lambda k, after: _weights_wait(w_started[k - 1], after, f"weights_wait_{k}"),
        norm_a_g, norm_b_g, sinks_a, ln1_g, ln1_b, conv_w_f, conv_b, ln2_g, ln2_b, slopes, on_grad)

    tiles = (96, 128, 352, 176)
    core_chip = jnp.stack([core, chip])
    got = _grads_wait(started[1:], gx, "grads_wait_1")
    halves = [_sum_partials(grads4[k], got[k - 1], core_chip, f"sum_partials_{k}", tiles[k]) for k in (1, 2, 3)]
    g_w_o, g_w_up_rows, g_w_down = _swap_halves(halves, "swap_halves")
    g_w_up = g_w_up_rows.T
    delta, new_m, new_v = {}, {}, {}
    for k, g, tr in (("w_o", g_w_o, 128), ("w_up", g_w_up, 256), ("w_down", g_w_down, 176)):
        delta[k], new_m[k], new_v[k] = _adamw(dict(w_o=w_o, w_up=w_up, w_down=w_down)[k], g,
                                              dict(w_o=m_w_o, w_up=m_w_up, w_down=m_w_down)[k],
                                              dict(w_o=v_w_o, w_up=v_w_up, w_down=v_w_down)[k], f"adamw_{k}", tr)

    got = _grads_wait(started[:1], delta["w_up"], "grads_wait_0")
    half_in = _sum_partials(grads4[0], got[0], core_chip, "sum_partials_0", tiles[0])
    small_rows = 32
    g_w_in_rows, totals = _share_halves([half_in], _pack({k: small[k] for k in SMALL_ORDER}, small_rows))
    tot = _unpack(totals, SMALL_ORDER, SMALL_SIZES)
    loss = tot["loss"][0]
    cols = 2 * FF // N_CHIPS
    g_conv_w = lax.dynamic_slice(tot["conv_w"].reshape(3, 2 * FF), (0, chip * cols), (3, cols))
    g_small = dict(norm_a_g=tot["norm_a_g"], norm_b_g=tot["norm_b_g"], sinks_a=tot["sinks_a"], ln1_g=tot["ln1_g"],
                   ln1_b=tot["ln1_b"], conv_w=g_conv_w, conv_b=tot["conv_b"], ln2_g=tot["ln2_g"], ln2_b=tot["ln2_b"])

    weights = dict(w_in=w_in, norm_a_g=norm_a_g, norm_b_g=norm_b_g, sinks_a=sinks_a, w_o=w_o, ln1_g=ln1_g, ln1_b=ln1_b,
                   w_up=w_up, conv_w=conv_w, conv_b=conv_b, w_down=w_down, ln2_g=ln2_g, ln2_b=ln2_b)
    ms = dict(w_in=m_w_in, norm_a_g=m_norm_a_g, norm_b_g=m_norm_b_g, sinks_a=m_sinks_a, w_o=m_w_o, ln1_g=m_ln1_g,
              ln1_b=m_ln1_b, w_up=m_w_up, conv_w=m_conv_w, conv_b=m_conv_b, w_down=m_w_down, ln2_g=m_ln2_g, ln2_b=m_ln2_b)
    vs = dict(w_in=v_w_in, norm_a_g=v_norm_a_g, norm_b_g=v_norm_b_g, sinks_a=v_sinks_a, w_o=v_w_o, ln1_g=v_ln1_g,
              ln1_b=v_ln1_b, w_up=v_w_up, conv_w=v_conv_w, conv_b=v_conv_b, w_down=v_w_down, ln2_g=v_ln2_g, ln2_b=v_ln2_b)
    order = list(weights)
    grad = dict(g_small, w_in=g_w_in_rows.T, w_o=g_w_o, w_up=g_w_up, w_down=g_w_down)

    delta["w_in"], new_m["w_in"], new_v["w_in"] = [
        a.T for a in _adamw(w_in_rows, g_w_in_rows, m_w_in_rows, v_w_in_rows, "adamw_w_in", 144)]
    small_names = [k for k in order if k not in delta]
    sizes = {k: weights[k].size for k in small_names}
    rows = 16
    packed = [_pack({k: src[k] for k in small_names}, rows) for src in (weights, grad, ms, vs)]
    for res, buf in zip((delta, new_m, new_v), _adamw(*packed, "adamw_small", rows)):
        for k, val in _unpack(buf, small_names, sizes).items():
            res[k] = val.reshape(weights[k].shape)

    return (loss, gx[None], *[grad[k] for k in order], *[delta[k] for k in order],
            *[new_m[k] for k in order], *[new_v[k] for k in order])
```

```python
import functools
import math

import jax
import jax.numpy as jnp
from jax import lax
from jax.experimental import pallas as pl
from jax.experimental.pallas import tpu as pltpu

F32, BF16, I32 = jnp.float32, jnp.bfloat16, jnp.int32

D = 1024
FF = 2816
HD = 64
NH = 8
WA, WB = 768, 1536
WIN = WA + WB
BLK = 128
ALPHA = 2.0 ** 0.25
LN_EPS, RMS_EPS = 1e-5, 1e-6
SCALE = 1.0 / math.sqrt(HD)
A_MAX_DIST, B_MAX_DIST = 127, 128
B_DILATIONS = (1, 4, 16)
SLOPES = tuple(2.0 ** (-(i + 1)) for i in range(NH))
SHARD_ROWS = (WIN // 4, D // 4, 2 * FF // 4, FF // 4)
N_CHIPS = 4
ADAM_LR, ADAM_B1, ADAM_B2, ADAM_EPS, ADAM_WD, ADAM_STEP = 0.001, 0.9, 0.999, 1e-08, 0.01, 10
MESH = pl.DeviceIdType.MESH
ANY = pl.BlockSpec(memory_space=pl.ANY)
SMEM = pl.BlockSpec(memory_space=pltpu.SMEM)
VMEM = pl.BlockSpec(memory_space=pltpu.VMEM)
HBM = pl.BlockSpec(memory_space=pltpu.HBM)
SEM = pl.BlockSpec(memory_space=pltpu.SEMAPHORE)
DATAFLOW = pltpu.SideEffectType.DATAFLOW_SIDE_EFFECTING


def _cp(sem, mb=48):
    return pltpu.CompilerParams(dimension_semantics=sem, vmem_limit_bytes=mb << 20)


def _nn(a, b):
    return lax.dot_general(a, b, (((1,), (0,)), ((), ())), preferred_element_type=F32)


def _nt(a, b):
    return lax.dot_general(a, b, (((1,), (1,)), ((), ())), preferred_element_type=F32)


def _tn(a, b):
    return lax.dot_general(a, b, (((0,), (0,)), ((), ())), preferred_element_type=F32)


def _resident(shape):
    n = len(shape)
    return pl.BlockSpec(shape, lambda *_: (0,) * n, pipeline_mode=pl.Buffered(1))


def _const(shape):
    n = len(shape)
    return pl.BlockSpec(shape, lambda *_: (0,) * n)


def _proj(x, w_t, name, tm=512):
    s = x.shape[0]
    n = w_t.shape[0]

    def body(x_ref, w_ref, o_ref, xb_ref):
        xb = x_ref[...].astype(BF16)
        xb_ref[...] = xb
        o_ref[...] = _nt(xb, w_ref[...])

    return pl.pallas_call(
        body, name=name, grid=(s // tm,),
        in_specs=[pl.BlockSpec((tm, D), lambda i: (i, 0)), _resident((n, D))],
        out_specs=[pl.BlockSpec((tm, n), lambda i: (i, 0)), pl.BlockSpec((tm, D), lambda i: (i, 0))],
        out_shape=[jax.ShapeDtypeStruct((s, n), F32), jax.ShapeDtypeStruct((s, D), BF16)],
        compiler_params=_cp(("parallel",)),
    )(x, w_t)


def _grad_w(lhs, rhs, name, tm, tk=512, lhs_halves=False):
    s = rhs.shape[0]
    if lhs_halves:
        per_half = lhs.shape[2] // tm
        n = 2 * lhs.shape[2]
        lhs_spec = pl.BlockSpec((None, tk, tm), lambda i, k: (i // per_half, k, i % per_half))
    else:
        n = lhs.shape[1]
        lhs_spec = pl.BlockSpec((tk, tm), lambda i, k: (k, i))
    nk = s // tk

    def body(l_ref, r_ref, o_ref, ob_ref):
        k = pl.program_id(1)

        @pl.when(k == 0)
        def _():
            o_ref[...] = jnp.zeros_like(o_ref)

        o_ref[...] += _tn(l_ref[...].astype(BF16), r_ref[...].astype(BF16))

        @pl.when(k == nk - 1)
        def _():
            ob_ref[...] = o_ref[...].astype(BF16)

    return pl.pallas_call(
        body, name=name, grid=(n // tm, nk),
        in_specs=[lhs_spec, pl.BlockSpec((tk, D), lambda i, k: (k, 0))],
        out_specs=[pl.BlockSpec((tm, D), lambda i, k: (i, 0))] * 2,
        out_shape=[jax.ShapeDtypeStruct((n, D), F32), jax.ShapeDtypeStruct((n, D), BF16)],
        compiler_params=_cp(("parallel", "arbitrary")),
    )(lhs, rhs)


def _band_base(max_dist, dist_unit, first):
    row = lax.broadcasted_iota(I32, (BLK, 2 * BLK), 0)
    col = lax.broadcasted_iota(I32, (BLK, 2 * BLK), 1)
    dist = BLK + row - col
    ok = (dist >= 0) & (dist <= max_dist)
    if first:
        ok = ok & (col >= BLK)
    return jnp.where(ok, dist.astype(F32) * (-float(dist_unit)), -jnp.inf)


def _half_mask(shape, e):
    lane = lax.broadcasted_iota(I32, shape, 1)
    return (lane < HD) if e == 0 else (lane >= HD)


def _to_half(x, e, g):
    if g != e:
        x = pltpu.roll(x, HD, 1)
    return jnp.where(_half_mask(x.shape, g), x, 0.0)


def _stack_heads(scalars, tile):
    return jnp.concatenate([scalars[0] * tile, scalars[1] * tile], axis=0)


def _pair_fwd(q2, kb, vb, base, slopes, kv_heads, sinks):
    lo = _half_mask((BLK, 2 * HD), 0)
    if sinks is not None:
        o2 = lse2 = None
        for e in (0, 1):
            g = kv_heads[e]
            qv = (_to_half(q2, e, g) * SCALE).astype(BF16)
            s = _nt(qv, kb) + slopes[e] * base
            m = jnp.maximum(jnp.max(s, axis=1, keepdims=True), sinks[e])
            p = jnp.exp(s - m)
            l = jnp.sum(p, axis=1, keepdims=True) + jnp.exp(sinks[e] - m)
            oh = _nn(p.astype(BF16), vb) / l
            if g != e:
                oh = pltpu.roll(oh, HD, 1)
            lse = jnp.broadcast_to(m + jnp.log(l), (BLK, 2 * HD))
            o2 = oh if e == 0 else jnp.where(lo, o2, oh)
            lse2 = lse if e == 0 else jnp.where(lo, lse2, lse)
        return o2, lse2
    qs = jnp.concatenate([_to_half(q2, e, kv_heads[e]) * SCALE for e in (0, 1)], axis=0).astype(BF16)
    s = _nt(qs, kb) + (base if slopes is None else _stack_heads(slopes, base))
    m = jnp.max(s, axis=1, keepdims=True)
    p = jnp.exp(s - m)
    l = jnp.sum(p, axis=1, keepdims=True)
    o = _nn(p.astype(BF16), vb) / l
    lse = m + jnp.log(l)
    halves = []
    for e in (0, 1):
        oh = o[e * BLK:(e + 1) * BLK]
        halves.append(pltpu.roll(oh, HD, 1) if kv_heads[e] != e else oh)
    o2 = jnp.where(lo, halves[0], halves[1])
    lse2 = jnp.where(lo, jnp.broadcast_to(lse[:BLK], (BLK, 2 * HD)), jnp.broadcast_to(lse[BLK:], (BLK, 2 * HD)))
    return o2, lse2


def _pair_bwd(q2, kb, vb, do2, o2, lse2, base, slopes, kv_heads, sinks):
    lo = _half_mask((BLK, 2 * HD), 0)
    prod = do2 * o2
    lses, deltas = [], []
    for e in (0, 1):
        hq = _half_mask((BLK, 2 * HD), e)
        lses.append(jnp.max(jnp.where(hq, lse2, -jnp.inf), axis=1, keepdims=True))
        deltas.append(jnp.sum(jnp.where(hq, prod, 0.0), axis=1, keepdims=True))
    lse = jnp.concatenate(lses, axis=0)
    delta = jnp.concatenate(deltas, axis=0)
    qs = jnp.concatenate([_to_half(q2, e, kv_heads[e]) * SCALE for e in (0, 1)], axis=0).astype(BF16)
    dos = jnp.concatenate([_to_half(do2, e, kv_heads[e]) for e in (0, 1)], axis=0).astype(BF16)
    p = jnp.exp(_nt(qs, kb) + (base if slopes is None else _stack_heads(slopes, base)) - lse)
    ds = (p * (_nt(dos, vb) - delta)).astype(BF16)
    dq = _nn(ds, kb) * SCALE
    halves = []
    for e in (0, 1):
        dqh = dq[e * BLK:(e + 1) * BLK]
        halves.append(pltpu.roll(dqh, HD, 1) if kv_heads[e] != e else dqh)
    dq2 = jnp.where(lo, halves[0], halves[1])
    dk2 = _tn(ds, qs)
    dv2 = _tn(p.astype(BF16), dos)
    dsinks = []
    if sinks is not None:
        for e in (0, 1):
            dsinks.append(jnp.sum(-jnp.exp(sinks[e] - lses[e]) * deltas[e], axis=0, keepdims=True))
    return dq2, dk2, dv2, dsinks


A_BLOCKS_PER_STEP = 2
A_BLOCKS_PER_STEP_BWD = 1


def _attn_a_fwd(proj, sinks):
    s = proj.shape[0]
    nq = A_BLOCKS_PER_STEP
    rows = BLK * nq
    steps = s // rows

    def body(sink_ref, q_ref, kp_ref, kc_ref, vp_ref, vc_ref, o_ref, lse_ref):
        n = pl.program_id(0)
        base_rest = _band_base(A_MAX_DIST, 1, False)
        base_0 = jnp.where(n > 0, base_rest, _band_base(A_MAX_DIST, 1, True))
        for i in range(nq):
            cur = pl.ds(i * BLK, BLK)
            k_prev = kc_ref[pl.ds((i - 1) * BLK, BLK), :] if i > 0 else kp_ref[...]
            v_prev = vc_ref[pl.ds((i - 1) * BLK, BLK), :] if i > 0 else vp_ref[...]
            kb = jnp.concatenate([k_prev, kc_ref[cur, :]], axis=0).astype(BF16)
            vb = jnp.concatenate([v_prev, vc_ref[cur, :]], axis=0).astype(BF16)
            for j in range(NH // 2):
                g = j // 2
                o2, lse2 = _pair_fwd(q_ref[cur, 128 * j:128 * (j + 1)], kb, vb, base_rest if i > 0 else base_0,
                                     (SLOPES[2 * j], SLOPES[2 * j + 1]), (g, g), (sink_ref[2 * j], sink_ref[2 * j + 1]))
                o_ref[cur, 128 * j:128 * (j + 1)] = o2
                lse_ref[cur, 128 * j:128 * (j + 1)] = lse2

    before = lambda n: jnp.maximum(n * nq - 1, 0)
    return pl.pallas_call(
        body, name="attn_a_fwd", grid=(steps,),
        in_specs=[SMEM,
                  pl.BlockSpec((rows, 512), lambda n: (n, 0)),
                  pl.BlockSpec((BLK, 128), lambda n: (before(n), 4)), pl.BlockSpec((rows, 128), lambda n: (n, 4)),
                  pl.BlockSpec((BLK, 128), lambda n: (before(n), 5)), pl.BlockSpec((rows, 128), lambda n: (n, 5))],
        out_specs=[pl.BlockSpec((rows, 512), lambda n: (n, 0))] * 2,
        out_shape=[jax.ShapeDtypeStruct((s, 512), F32)] * 2,
        compiler_params=_cp(("parallel",)),
    )(sinks, proj, proj, proj, proj, proj)


def _attn_a_bwd(proj, sinks, d_o, o, lse):
    s = proj.shape[0]
    nq = A_BLOCKS_PER_STEP_BWD
    rows = BLK * nq
    steps = s // rows

    def body(sink_ref, q_ref, kp_ref, kc_ref, vp_ref, vc_ref, do_ref, o_ref, lse_ref,
             dq_ref, dk_ref, dv_ref, dsink_ref, kcar, vcar):
        n = pl.program_id(0)

        @pl.when(n == 0)
        def _():
            kcar[...] = jnp.zeros_like(kcar)
            vcar[...] = jnp.zeros_like(vcar)
            dsink_ref[...] = jnp.zeros_like(dsink_ref)

        dk_ref[...] = kcar[...]
        dv_ref[...] = vcar[...]

        @pl.when(n < steps)
        def _():
            base_rest = _band_base(A_MAX_DIST, 1, False)
            base_0 = jnp.where(n > 0, base_rest, _band_base(A_MAX_DIST, 1, True))
            for i in range(nq):
                cur = pl.ds(i * BLK, BLK)
                k_prev = kc_ref[pl.ds((i - 1) * BLK, BLK), :] if i > 0 else kp_ref[...]
                v_prev = vc_ref[pl.ds((i - 1) * BLK, BLK), :] if i > 0 else vp_ref[...]
                kb = jnp.concatenate([k_prev, kc_ref[cur, :]], axis=0).astype(BF16)
                vb = jnp.concatenate([v_prev, vc_ref[cur, :]], axis=0).astype(BF16)
                dk_win = dv_win = None
                for j in range(NH // 2):
                    g = j // 2
                    sl = slice(128 * j, 128 * (j + 1))
                    dq2, dk2, dv2, dsk = _pair_bwd(q_ref[cur, sl], kb, vb, do_ref[cur, sl], o_ref[cur, sl],
                                                   lse_ref[cur, sl], base_rest if i > 0 else base_0,
                                                   (SLOPES[2 * j], SLOPES[2 * j + 1]), (g, g),
                                                   (sink_ref[2 * j], sink_ref[2 * j + 1]))
                    dq_ref[cur, sl] = dq2
                    dk_win = dk2 if j == 0 else dk_win + dk2
                    dv_win = dv2 if j == 0 else dv_win + dv2
                    for e in (0, 1):
                        h = 2 * j + e
                        dsink_ref[h:h + 1, :] += jnp.broadcast_to(dsk[e], (1, 128))
                if i == 0:
                    last = pl.ds((nq - 1) * BLK, BLK)
                    dk_ref[last, :] += dk_win[:BLK]
                    dv_ref[last, :] += dv_win[:BLK]
                else:
                    kcar[pl.ds((i - 1) * BLK, BLK), :] += dk_win[:BLK]
                    vcar[pl.ds((i - 1) * BLK, BLK), :] += dv_win[:BLK]
                kcar[cur, :] = dk_win[BLK:]
                vcar[cur, :] = dv_win[BLK:]

    cur_step = lambda n: jnp.minimum(n, steps - 1)
    before = lambda n: jnp.maximum(cur_step(n) * nq - 1, 0)
    out_prev = lambda n: jnp.maximum(n - 1, 0)
    wide = pl.BlockSpec((rows, 512), lambda n: (cur_step(n), 0))
    return pl.pallas_call(
        body, name="attn_a_bwd", grid=(steps + 1,),
        in_specs=[SMEM, wide,
                  pl.BlockSpec((BLK, 128), lambda n: (before(n), 4)), pl.BlockSpec((rows, 128), lambda n: (cur_step(n), 4)),
                  pl.BlockSpec((BLK, 128), lambda n: (before(n), 5)), pl.BlockSpec((rows, 128), lambda n: (cur_step(n), 5)),
                  wide, wide, wide],
        out_specs=[wide,
                   pl.BlockSpec((rows, 128), lambda n: (out_prev(n), 0)),
                   pl.BlockSpec((rows, 128), lambda n: (out_prev(n), 0)),
                   pl.BlockSpec((NH, 128), lambda n: (0, 0))],
        out_shape=[jax.ShapeDtypeStruct((s, 512), F32), jax.ShapeDtypeStruct((s, 128), F32),
                   jax.ShapeDtypeStruct((s, 128), F32), jax.ShapeDtypeStruct((NH, 128), F32)],
        scratch_shapes=[pltpu.VMEM((rows, 128), F32), pltpu.VMEM((rows, 128), F32)],
        compiler_params=_cp(("arbitrary",)),
    )(sinks, proj, proj, proj, proj, proj, d_o, o, lse)


def _stream(rho, i, r):
    start = i * BLK * r + rho
    return pl.ds(start, BLK, stride=r) if r > 1 else pl.ds(start, BLK)


def _for_streams(r, fn, side_by_side=4):
    if r <= side_by_side:
        for rho in range(r):
            fn(rho)
    else:
        def group(it, carry):
            for u in range(side_by_side):
                fn(side_by_side * it + u)
            return carry

        lax.fori_loop(0, r // side_by_side, group, 0)


B_BLOCKS_PER_STEP = {1: 8, 4: 2, 16: 1}
B_BLOCKS_PER_STEP_FWD = {1: 8, 4: 2, 16: 1}


def _attn_b_fwd(proj, slopes, r):
    s = proj.shape[0]
    nq = B_BLOCKS_PER_STEP_FWD[r]
    rows = BLK * r * nq
    steps = s // rows
    qc, kc, vc = WA // 128, WA // 128 + 4, WA // 128 + 8

    def body(slope_ref, q_ref, kp_ref, kc_ref, vp_ref, vc_ref, o_ref, lse_ref):
        j = pl.program_id(0)
        sb = pl.program_id(1)
        sl2 = (slope_ref[2 * j], slope_ref[2 * j + 1])
        bias_rest = _stack_heads(sl2, _band_base(B_MAX_DIST, r, False))
        bias_0 = jnp.where(sb > 0, bias_rest, _stack_heads(sl2, _band_base(B_MAX_DIST, r, True)))

        def stream(rho):
            for i in range(nq):
                cur = _stream(rho, i, r)
                k_prev = kc_ref[_stream(rho, i - 1, r), :] if i > 0 else kp_ref[_stream(rho, 0, r), :]
                v_prev = vc_ref[_stream(rho, i - 1, r), :] if i > 0 else vp_ref[_stream(rho, 0, r), :]
                kb = jnp.concatenate([k_prev, kc_ref[cur, :]], axis=0).astype(BF16)
                vb = jnp.concatenate([v_prev, vc_ref[cur, :]], axis=0).astype(BF16)
                o2, lse2 = _pair_fwd(q_ref[cur, :], kb, vb, bias_rest if i > 0 else bias_0, None, (0, 1), None)
                o_ref[cur, :] = o2
                lse_ref[cur, :] = lse2

        _for_streams(r, stream, side_by_side=8)

    before = lambda sb: jnp.maximum(sb * nq - 1, 0)
    return pl.pallas_call(
        body, name=f"attn_b_fwd_r{r}", grid=(NH // 2, steps),
        in_specs=[SMEM,
                  pl.BlockSpec((rows, 128), lambda j, sb: (sb, qc + j)),
                  pl.BlockSpec((BLK * r, 128), lambda j, sb: (before(sb), kc + j)),
                  pl.BlockSpec((rows, 128), lambda j, sb: (sb, kc + j)),
                  pl.BlockSpec((BLK * r, 128), lambda j, sb: (before(sb), vc + j)),
                  pl.BlockSpec((rows, 128), lambda j, sb: (sb, vc + j))],
        out_specs=[pl.BlockSpec((rows, 128), lambda j, sb: (sb, j))] * 2,
        out_shape=[jax.ShapeDtypeStruct((s, 512), F32)] * 2,
        compiler_params=_cp(("parallel", "parallel")),
    )(slopes, proj, proj, proj, proj, proj)


def _attn_b_bwd(proj, slopes, d_o, o, lse, r, so_far=None):
    s = proj.shape[0]
    nq = B_BLOCKS_PER_STEP[r]
    rows = BLK * r * nq
    steps = s // rows
    qc, kc, vc = WA // 128, WA // 128 + 4, WA // 128 + 8
    chained = so_far is not None

    def body(slope_ref, q_ref, kp_ref, kc_ref, vp_ref, vc_ref, do_ref, o_ref, lse_ref, *rest):
        if chained:
            pq_ref, pk_ref, pv_ref, dq_ref, dk_ref, dv_ref, kcar, vcar = rest
        else:
            dq_ref, dk_ref, dv_ref, kcar, vcar = rest
        j = pl.program_id(0)
        sb = pl.program_id(1)

        @pl.when(sb == 0)
        def _():
            kcar[...] = jnp.zeros_like(kcar)
            vcar[...] = jnp.zeros_like(vcar)

        if chained:
            dk_ref[...] = kcar[...] + pk_ref[...]
            dv_ref[...] = vcar[...] + pv_ref[...]
        else:
            dk_ref[...] = kcar[...]
            dv_ref[...] = vcar[...]

        @pl.when(sb < steps)
        def _():
            sl2 = (slope_ref[2 * j], slope_ref[2 * j + 1])
            bias_rest = _stack_heads(sl2, _band_base(B_MAX_DIST, r, False))
            bias_0 = jnp.where(sb > 0, bias_rest, _stack_heads(sl2, _band_base(B_MAX_DIST, r, True)))

            def stream(rho):
                for i in range(nq):
                    cur = _stream(rho, i, r)
                    k_prev = kc_ref[_stream(rho, i - 1, r), :] if i > 0 else kp_ref[_stream(rho, 0, r), :]
                    v_prev = vc_ref[_stream(rho, i - 1, r), :] if i > 0 else vp_ref[_stream(rho, 0, r), :]
                    kb = jnp.concatenate([k_prev, kc_ref[cur, :]], axis=0).astype(BF16)
                    vb = jnp.concatenate([v_prev, vc_ref[cur, :]], axis=0).astype(BF16)
                    dq2, dk2, dv2, _ = _pair_bwd(q_ref[cur, :], kb, vb, do_ref[cur, :], o_ref[cur, :], lse_ref[cur, :],
                                                 bias_rest if i > 0 else bias_0, None, (0, 1), None)
                    dq_ref[cur, :] = dq2 + pq_ref[cur, :] if chained else dq2
                    if i == 0:
                        last = _stream(rho, nq - 1, r)
                        dk_ref[last, :] += dk2[:BLK]
                        dv_ref[last, :] += dv2[:BLK]
                    else:
                        kcar[_stream(rho, i - 1, r), :] += dk2[:BLK]
                        vcar[_stream(rho, i - 1, r), :] += dv2[:BLK]
                    kcar[cur, :] = dk2[BLK:]
                    vcar[cur, :] = dv2[BLK:]

            _for_streams(r, stream, side_by_side=8)

    cur_step = lambda sb: jnp.minimum(sb, steps - 1)
    before = lambda sb: jnp.maximum(cur_step(sb) * nq - 1, 0)
    out_prev = lambda sb: jnp.maximum(sb - 1, 0)
    tile = lambda col: pl.BlockSpec((rows, 128), lambda j, sb: (cur_step(sb), col + j))
    edge = lambda col: pl.BlockSpec((BLK * r, 128), lambda j, sb: (before(sb), col + j))
    late = pl.BlockSpec((rows, 128), lambda j, sb: (out_prev(sb), j))
    grads = [tile(0), late, late]
    return pl.pallas_call(
        body, name=f"attn_b_bwd_r{r}", grid=(NH // 2, steps + 1),
        in_specs=[SMEM, tile(qc), edge(kc), tile(kc), edge(vc), tile(vc), tile(0), tile(0), tile(0)]
        + (grads if chained else []),
        out_specs=grads,
        out_shape=[jax.ShapeDtypeStruct((s, 512), F32)] * 3,
        scratch_shapes=[pltpu.VMEM((rows, 128), F32), pltpu.VMEM((rows, 128), F32)],
        compiler_params=_cp(("parallel", "arbitrary")),
    )(slopes, proj, proj, proj, proj, proj, d_o, o, lse, *(so_far if chained else ()))


def _row(v):
    return v.reshape(1, -1)


def _layer_norm_stats(z):
    mu = jnp.mean(z, axis=-1, keepdims=True)
    zc = z - mu
    var = jnp.mean(zc * zc, axis=-1, keepdims=True)
    rstd = lax.rsqrt(var + LN_EPS)
    return zc * rstd, rstd


def _layer_norm_bwd(dh, zh, rstd, g):
    dzh = dh * g
    return rstd * (dzh - jnp.mean(dzh, axis=-1, keepdims=True) - zh * jnp.mean(dzh * zh, axis=-1, keepdims=True))


def _rms(o):
    return lax.rsqrt(jnp.mean(o * o, axis=-1, keepdims=True) + RMS_EPS)


def _mix_ln1(x, o_a, o_b, lse_b, norm_a_g, norm_b_g, w_o, ln1_g, ln1_b, tm=256):
    s = x.shape[0]

    def body(x_ref, oa_ref, ob1, ob2, ob3, l1, l2, l3, ga_ref, gb_ref, wo_ref, g_ref, b_ref,
             obm_ref, lse_ref, cat_ref, z1_ref, h1_ref, h1b_ref):
        la, lb, lc = l1[...], l2[...], l3[...]
        m = jnp.maximum(jnp.maximum(la, lb), lc)
        ea, eb, ec = jnp.exp(la - m), jnp.exp(lb - m), jnp.exp(lc - m)
        den = ea + eb + ec
        obm = (ea / den) * ob1[...] + (eb / den) * ob2[...] + (ec / den) * ob3[...]
        obm_ref[...] = obm
        lse_ref[...] = m + jnp.log(den)
        oa = oa_ref[...]
        na = oa * _rms(oa) * ga_ref[...]
        nb_ = obm * _rms(obm) * gb_ref[...]
        cat = jnp.concatenate([na, nb_], axis=1).astype(BF16)
        cat_ref[...] = cat
        z1 = ALPHA * x_ref[...] + _nn(cat, wo_ref[...])
        z1_ref[...] = z1
        zh, _ = _layer_norm_stats(z1)
        h1 = zh * g_ref[...] + b_ref[...]
        h1_ref[...] = h1
        h1b_ref[...] = h1.astype(BF16)

    t512 = pl.BlockSpec((tm, 512), lambda i: (i, 0))
    td = pl.BlockSpec((tm, D), lambda i: (i, 0))
    return pl.pallas_call(
        body, name="mix_ln1", grid=(s // tm,),
        in_specs=[td] + [t512] * 7 + [_const((1, 512))] * 2 + [_resident((D, D))] + [_const((1, D))] * 2,
        out_specs=[t512, t512, td, td, td, td],
        out_shape=[jax.ShapeDtypeStruct((s, 512), F32), jax.ShapeDtypeStruct((s, 512), F32),
                   jax.ShapeDtypeStruct((s, D), BF16), jax.ShapeDtypeStruct((s, D), F32),
                   jax.ShapeDtypeStruct((s, D), F32), jax.ShapeDtypeStruct((s, D), BF16)],
        compiler_params=_cp(("parallel",)),
    )(x, o_a, *o_b, *lse_b, _row(norm_a_g), _row(norm_b_g), w_o, _row(ln1_g), _row(ln1_b))


def _gelu_and_grad(x):
    c = math.sqrt(2.0 / math.pi)
    x2 = x * x
    cx = c * x
    t = jnp.tanh(cx * (1.0 + 0.044715 * x2))
    q = 1.0 + t
    g = (0.5 * x) * q
    dg = 0.5 * q + ((0.5 * cx) * (1.0 - t * t)) * (1.0 + (3.0 * 0.044715) * x2)
    return g, dg


CONV_CHUNK = 64


def _shift_down(u, before):
    n = u.shape[0]
    ext = jnp.concatenate([before, u], axis=0)
    return pltpu.roll(ext, 1, 0)[8:], pltpu.roll(ext, 2, 0)[8:]


def _shift_up(u, after):
    n = u.shape[0]
    ext = jnp.concatenate([u, after], axis=0)
    return pltpu.roll(ext, n + 7, 0)[:n], pltpu.roll(ext, n + 6, 0)[:n]


def _up_conv_gelu(h1b, w_up, cwb, tm=512, tn=256):
    s = h1b.shape[0]
    n_i = s // tm
    n_t = (FF // tn) * n_i

    def body(h_ref, wg_ref, wv_ref, c_ref, up_ref, a_ref, g_ref, a1_ref, pend_a, pend_b, carry):
        t = pl.program_id(0)
        row_tile = jnp.maximum(t - 1, 0) % n_i
        w_refs = (wg_ref, wv_ref)

        @pl.when(t == 0)
        def _():
            pend_b[...] = jnp.zeros_like(pend_b)
            carry[...] = jnp.zeros_like(carry)

        def step(dst, src):
            def chunk(c, before):
                rows = pl.ds(c * CONV_CHUNK, CONV_CHUNK)
                u, last = [], []
                for half in (0, 1):
                    up = src[half, rows, :]
                    r1, r2 = _shift_down(up, before[half])
                    u.append(r2 * c_ref[0, half:half + 1, :] + r1 * c_ref[1, half:half + 1, :]
                             + up * c_ref[2, half:half + 1, :] + c_ref[3, half:half + 1, :])
                    last.append(up[CONV_CHUNK - 8:])
                g, dg = _gelu_and_grad(u[0])
                a_ref[rows, :] = (g * u[1]).astype(BF16)
                g_ref[rows, :] = g.astype(BF16)
                a1_ref[rows, :] = (u[1] * dg).astype(BF16)
                return tuple(last)

            edge = tuple(jnp.where(row_tile > 0, carry[half], 0.0) for half in (0, 1))
            n_c = tm // CONV_CHUNK
            n_k = n_c // 2
            tk = D // n_k
            for half in (0, 1):
                up = None
                for kq in range(n_k):
                    ks = slice(kq * tk, (kq + 1) * tk)
                    part = _nn(h_ref[:, ks], w_refs[half][ks, :])
                    up = part if kq == 0 else up + part
                    edge = chunk(half * n_k + kq, edge)
                up_ref[half] = up.astype(BF16)
                dst[half] = up
            for half in (0, 1):
                carry[half] = edge[half]

        @pl.when(t % 2 == 0)
        def _():
            step(pend_a, pend_b)

        @pl.when(t % 2 == 1)
        def _():
            step(pend_b, pend_a)

    mm = lambda t: jnp.minimum(t, n_t - 1)
    ew = lambda t: jnp.maximum(t - 1, 0)
    out_tile = pl.BlockSpec((tm, tn), lambda t: (ew(t) % n_i, ew(t) // n_i))
    return pl.pallas_call(
        body, name="up_conv_gelu", grid=(n_t + 1,),
        in_specs=[pl.BlockSpec((tm, D), lambda t: (mm(t) % n_i, 0)),
                  pl.BlockSpec((D, tn), lambda t: (0, mm(t) // n_i)),
                  pl.BlockSpec((D, tn), lambda t: (0, FF // tn + mm(t) // n_i)),
                  pl.BlockSpec((4, 2, tn), lambda t: (0, 0, ew(t) // n_i))],
        out_specs=[pl.BlockSpec((2, tm, tn), lambda t: (0, mm(t) % n_i, mm(t) // n_i)), out_tile, out_tile, out_tile],
        out_shape=[jax.ShapeDtypeStruct((2, s, FF), BF16)] + [jax.ShapeDtypeStruct((s, FF), BF16)] * 3,
        scratch_shapes=[pltpu.VMEM((2, tm, tn), F32), pltpu.VMEM((2, tm, tn), F32), pltpu.VMEM((2, 8, tn), F32)],
        compiler_params=_cp(("arbitrary",)),
    )(h1b, w_up, w_up, cwb)


def _down_ln2_loss(a, w_down, h1, target, ln2_g, ln2_b, tm=256):
    s = a.shape[0]

    def body(a_ref, w_ref, h_ref, t_ref, g_ref, b_ref, dz_ref, dzb_ref, st_ref):
        @pl.when(pl.program_id(0) == 0)
        def _():
            st_ref[...] = jnp.zeros_like(st_ref)

        z2 = ALPHA * h_ref[...] + _nn(a_ref[...], w_ref[...])
        zh, rstd = _layer_norm_stats(z2)
        diff = zh * g_ref[...] + b_ref[...] - t_ref[...]
        part = 0.5 * jnp.sum(jnp.mean(diff * diff, axis=-1, keepdims=True), axis=0, keepdims=True)
        dy = diff * (1.0 / D)
        st_ref[0:1, :] += jnp.sum(dy * zh, axis=0, keepdims=True)
        st_ref[1:2, :] += jnp.sum(dy, axis=0, keepdims=True)
        st_ref[2:3, :] += jnp.broadcast_to(part, (1, D))
        dz = _layer_norm_bwd(dy, zh, rstd, g_ref[...])
        dz_ref[...] = dz
        dzb_ref[...] = dz.astype(BF16)

    td = pl.BlockSpec((tm, D), lambda i: (i, 0))
    return pl.pallas_call(
        body, name="down_ln2_loss", grid=(s // tm,),
        in_specs=[pl.BlockSpec((tm, FF), lambda i: (i, 0)), _resident((FF, D)), td, td, _const((1, D)), _const((1, D))],
        out_specs=[td, td, _const((8, D))],
        out_shape=[jax.ShapeDtypeStruct((s, D), F32), jax.ShapeDtypeStruct((s, D), BF16),
                   jax.ShapeDtypeStruct((8, D), F32)],
        compiler_params=_cp(("arbitrary",)),
    )(a, w_down, h1, target, _row(ln2_g), _row(ln2_b))


def _d_act(dz2b, w_down, tm=512):
    s = dz2b.shape[0]

    def body(dz_ref, w_ref, o_ref):
        o_ref[...] = _nt(dz_ref[...], w_ref[...])

    return pl.pallas_call(
        body, name="d_act", grid=(s // tm,),
        in_specs=[pl.BlockSpec((tm, D), lambda i: (i, 0)), _resident((FF, D))],
        out_specs=pl.BlockSpec((tm, FF), lambda i: (i, 0)),
        out_shape=jax.ShapeDtypeStruct((s, FF), F32),
        compiler_params=_cp(("parallel",)),
    )(dz2b, w_down)


def _conv_gelu_bwd(da, up, g, a1, cwb, tm=512, tn=256):
    s = da.shape[0]
    n_i = s // tm
    n_c = tm // CONV_CHUNK

    def body(da_ref, up_ref, g_ref, a1_ref, c_ref, dup_ref, dc_ref, carry):
        @pl.when(pl.program_id(1) == 0)
        def _():
            carry[...] = jnp.zeros_like(carry)
            dc_ref[...] = jnp.zeros_like(dc_ref)

        def fold(v):
            return jnp.sum(v.reshape(CONV_CHUNK // 8, 8, v.shape[1]), axis=0)

        def chunk(cc, state):
            after, sums = state
            rows = pl.ds((n_c - 1 - cc) * CONV_CHUNK, CONV_CHUNK)
            da_c = da_ref[rows, :]
            dus = (da_c * a1_ref[rows, :].astype(F32), da_c * g_ref[rows, :].astype(F32))
            head, new_sums = [], []
            for half in (0, 1):
                du = dus[half]
                up = up_ref[half, rows, :].astype(F32)
                l1, l2 = _shift_up(du, after[half])
                dup = (du * c_ref[2, half:half + 1, :] + l1 * c_ref[1, half:half + 1, :]
                       + l2 * c_ref[0, half:half + 1, :])
                dup_ref[half, rows, :] = dup.astype(BF16)
                parts = (fold(l2 * up), fold(l1 * up), fold(du * up), fold(du))
                new_sums.append(parts if sums is None else tuple(a + b for a, b in zip(sums[half], parts)))
                head.append(du[:8])
            return tuple(head), new_sums

        state = ((carry[0], carry[1]), None)
        for cc in range(n_c):
            state = chunk(cc, state)
        head, sums = state
        for half in (0, 1):
            carry[half] = head[half]
            for k in range(4):
                dc_ref[k, half:half + 1, :] += jnp.sum(sums[half][k], axis=0, keepdims=True)

    rev = lambda ii: n_i - 1 - ii
    tile = pl.BlockSpec((tm, tn), lambda j, ii: (rev(ii), j))
    pair = pl.BlockSpec((2, tm, tn), lambda j, ii: (0, rev(ii), j))
    per_col = pl.BlockSpec((4, 2, tn), lambda j, ii: (0, 0, j))
    return pl.pallas_call(
        body, name="conv_gelu_bwd", grid=(FF // tn, n_i),
        in_specs=[tile, pair, tile, tile, per_col],
        out_specs=[pair, per_col],
        out_shape=[jax.ShapeDtypeStruct((2, s, FF), BF16), jax.ShapeDtypeStruct((4, 2, FF), F32)],
        scratch_shapes=[pltpu.VMEM((2, 8, tn), F32)],
        compiler_params=_cp(("parallel", "arbitrary")),
    )(da, up, g, a1, cwb)


def _dh1_ln1_bwd(dz2, dup, w_up, z1, ln1_g, tm=256):
    s = dz2.shape[0]

    def body(dz2_ref, dup_ref, w_ref, z1_ref, g_ref, dz1_ref, dz1b_ref, st_ref):
        @pl.when(pl.program_id(0) == 0)
        def _():
            st_ref[...] = jnp.zeros_like(st_ref)

        dh = ALPHA * dz2_ref[...] + _nt(dup_ref[0], w_ref[:, :FF]) + _nt(dup_ref[1], w_ref[:, FF:])
        zh, rstd = _layer_norm_stats(z1_ref[...])
        st_ref[0:1, :] += jnp.sum(dh * zh, axis=0, keepdims=True)
        st_ref[1:2, :] += jnp.sum(dh, axis=0, keepdims=True)
        dz = _layer_norm_bwd(dh, zh, rstd, g_ref[...])
        dz1_ref[...] = dz
        dz1b_ref[...] = dz.astype(BF16)

    td = pl.BlockSpec((tm, D), lambda i: (i, 0))
    return pl.pallas_call(
        body, name="dh1_ln1_bwd", grid=(s // tm,),
        in_specs=[td, pl.BlockSpec((2, tm, FF), lambda i: (0, i, 0)), _resident((D, 2 * FF)), td, _const((1, D))],
        out_specs=[td, td, _const((8, D))],
        out_shape=[jax.ShapeDtypeStruct((s, D), F32), jax.ShapeDtypeStruct((s, D), BF16),
                   jax.ShapeDtypeStruct((8, D), F32)],
        compiler_params=_cp(("arbitrary",)),
    )(dz2, dup, w_up, z1, _row(ln1_g))


def _dcat_rms_bwd(dz1b, w_o, o_a, o_b, norm_a_g, norm_b_g, tm=256):
    s = dz1b.shape[0]

    def body(dz_ref, w_ref, oa_ref, ob_ref, ga_ref, gb_ref, da_ref, db_ref, st_ref):
        @pl.when(pl.program_id(0) == 0)
        def _():
            st_ref[...] = jnp.zeros_like(st_ref)

        dcat = _nt(dz_ref[...], w_ref[...])
        for k, (o_ref, g_ref, d_ref) in enumerate(((oa_ref, ga_ref, da_ref), (ob_ref, gb_ref, db_ref))):
            o = o_ref[...]
            dn = dcat[:, 512 * k:512 * (k + 1)]
            rr = _rms(o)
            oh = o * rr
            st_ref[k:k + 1, :] += jnp.sum(dn * oh, axis=0, keepdims=True)
            doh = dn * g_ref[...]
            d_ref[...] = rr * (doh - oh * jnp.mean(doh * oh, axis=-1, keepdims=True))

    t512 = pl.BlockSpec((tm, 512), lambda i: (i, 0))
    return pl.pallas_call(
        body, name="dcat_rms_bwd", grid=(s // tm,),
        in_specs=[pl.BlockSpec((tm, D), lambda i: (i, 0)), _resident((D, D)), t512, t512,
                  _const((1, 512)), _const((1, 512))],
        out_specs=[t512, t512, _const((8, 512))],
        out_shape=[jax.ShapeDtypeStruct((s, 512), F32), jax.ShapeDtypeStruct((s, 512), F32),
                   jax.ShapeDtypeStruct((8, 512), F32)],
        compiler_params=_cp(("arbitrary",)),
    )(dz1b, w_o, o_a, o_b, _row(norm_a_g), _row(norm_b_g))


def _dproj_combine(dqa, dka, dva, dqkv_b, tm=256):
    s = dqa.shape[0]

    def body(qa, ka, va, qb, kb, vb, o_ref):
        o_ref[:, 0:512] = qa[...].astype(BF16)
        o_ref[:, 512:640] = ka[...].astype(BF16)
        o_ref[:, 640:768] = va[...].astype(BF16)
        o_ref[:, 768:1280] = qb[...].astype(BF16)
        o_ref[:, 1280:1792] = kb[...].astype(BF16)
        o_ref[:, 1792:2304] = vb[...].astype(BF16)

    t512 = pl.BlockSpec((tm, 512), lambda i: (i, 0))
    t128 = pl.BlockSpec((tm, 128), lambda i: (i, 0))
    return pl.pallas_call(
        body, name="dproj_combine", grid=(s // tm,),
        in_specs=[t512, t128, t128] + [t512] * 3,
        out_specs=pl.BlockSpec((tm, WIN), lambda i: (i, 0)),
        out_shape=jax.ShapeDtypeStruct((s, WIN), BF16),
        compiler_params=_cp(("parallel",)),
    )(dqa, dka, dva, *dqkv_b)


def _grad_x(dz1, dproj, w_in_t, zero, tm=256):
    s = dz1.shape[0]

    def body(dz_ref, dp_ref, w_ref, z_ref, o_ref):
        o_ref[...] = ALPHA * dz_ref[...] + _nn(dp_ref[...], w_ref[...]) + z_ref[0:1, 0:1]

    td = pl.BlockSpec((tm, D), lambda i: (i, 0))
    return pl.pallas_call(
        body, name="grad_x", grid=(s // tm,),
        in_specs=[td, pl.BlockSpec((tm, WIN), lambda i: (i, 0)), _resident((WIN, D)), _const((8, 128))],
        out_specs=td, out_shape=jax.ShapeDtypeStruct((s, D), F32),
        compiler_params=_cp(("parallel",)),
    )(dz1, dproj, w_in_t, zero)


def _place():
    return lax.axis_index("x"), lax.axis_index("y"), lax.axis_index("c")


def _other_chips(x, y):
    return [(1 - x, y), (x, 1 - y), (1 - x, 1 - y)]


def _hbm(a):
    return pltpu.with_memory_space_constraint(a, pltpu.HBM)


def _gather_w_in(shard, conv_w):
    rows_k = shard.shape[0]
    half = rows_k // 2

    def body(src, conv_src, out, conv_out, send_sems, recv_sems):
        x, y, c = _place()
        b = 2 * x + y
        sibling = (x, y, 1 - c)
        chips = _other_chips(x, y)

        def copy(idx, chip_b, core, to, first_hop=False):
            rows = out.at[pl.ds(pl.multiple_of(chip_b * rows_k + core * half, 16), half)]
            s_ref = src.at[pl.ds(pl.multiple_of(core * half, 16), half)] if first_hop else rows
            return pltpu.make_async_remote_copy(src_ref=s_ref, dst_ref=rows, send_sem=send_sems.at[idx],
                                                recv_sem=recv_sems.at[idx], device_id=to, device_id_type=MESH)

        def own_copy():
            return pltpu.make_async_remote_copy(
                src_ref=src, dst_ref=out.at[pl.ds(pl.multiple_of(b * rows_k, 16), rows_k)], send_sem=send_sems.at[6],
                recv_sem=recv_sems.at[6], device_id=sibling, device_id_type=MESH)

        def conv_copy(idx, chip_b, to):
            return pltpu.make_async_remote_copy(src_ref=conv_src, dst_ref=conv_out.at[chip_b],
                                                send_sem=send_sems.at[7 + idx], recv_sem=recv_sems.at[7 + idx],
                                                device_id=to, device_id_type=MESH)

        started = [own_copy(), conv_copy(3, b, sibling)]
        for jn, chip in enumerate(chips):
            started += [copy(jn, b, c, (chip[0], chip[1], c), first_hop=True), conv_copy(jn, b, (chip[0], chip[1], c))]
        for cp in started:
            cp.start()
        for jn, chip in enumerate(chips):
            cb = 2 * chip[0] + chip[1]
            copy(jn, cb, c, (chip[0], chip[1], c)).wait_recv()
            cp = copy(3 + jn, cb, c, sibling)
            cp.start()
            started.append(cp)
        for jn, chip in enumerate(chips):
            cb = 2 * chip[0] + chip[1]
            copy(3 + jn, cb, 1 - c, sibling).wait_recv()
            conv_copy(jn, cb, (chip[0], chip[1], c)).wait_recv()
        own_copy().wait_recv()
        conv_copy(3, b, sibling).wait_recv()
        for cp in started:
            cp.wait_send()

    return pl.pallas_call(
        body, name="gather_w_in",
        in_specs=[ANY, ANY], out_specs=[ANY, ANY],
        out_shape=[jax.ShapeDtypeStruct((N_CHIPS * rows_k, D), BF16), jax.ShapeDtypeStruct((N_CHIPS,) + conv_w.shape, F32)],
        scratch_shapes=[pltpu.SemaphoreType.DMA((11,)), pltpu.SemaphoreType.DMA((11,))],
        compiler_params=pltpu.CompilerParams(has_side_effects=True),
    )(shard, conv_w)


def _weight_copies(shard, land, send_sems, recv_sems, arrivals):
    x, y, c = _place()
    n_rows, n_cols = shard.shape
    peers = [(px, py, c) for px, py in _other_chips(x, y)] + [(x, y, 1 - c)]
    cps = []
    for jn, peer in enumerate(peers):
        at = 2 * peer[0] + peer[1] if arrivals else 2 * x + y
        if land.shape[1] == n_cols:
            dst = land.at[pl.ds(pl.multiple_of(at * n_rows, 16), n_rows)]
        else:
            dst = land.at[:, pl.ds(pl.multiple_of(at * n_cols, 128), n_cols)]
        cps.append(pltpu.make_async_remote_copy(src_ref=shard, dst_ref=dst, send_sem=send_sems.at[jn],
                                                recv_sem=recv_sems.at[jn], device_id=peer, device_id_type=MESH))
    return cps


def _weights_start(shards, after):
    n = len(shards)
    lands = [lax.empty((N_CHIPS * sh.shape[0], D) if sh.shape[1] == D else (D, N_CHIPS * sh.shape[1]), BF16)
             for sh in shards]

    def body(*refs):
        src, land = refs[:n], refs[n:2 * n]
        send_sems, recv_sems = refs[2 * n + 1:3 * n + 1], refs[3 * n + 1:4 * n + 1]
        for k in range(n):
            for send in _weight_copies(src[k], land[k], send_sems[k], recv_sems[k], False):
                send.start()
        refs[-1][...] = jnp.zeros_like(refs[-1])

    res = pl.pallas_call(
        body, name="weights_start",
        in_specs=[HBM] * (2 * n) + [ANY], out_specs=[SEM] * (2 * n) + [HBM] * (2 * n) + [VMEM],
        out_shape=[pltpu.SemaphoreType.DMA((4,))] * (2 * n)
        + [pltpu.HBM(a.shape, a.dtype) for a in (*shards, *lands)] + [jax.ShapeDtypeStruct((8, 128), F32)],
        input_output_aliases={i: i + 2 * n for i in range(2 * n)},
        compiler_params=pltpu.CompilerParams(has_side_effects=DATAFLOW),
    )(*[_hbm(a) for a in (*shards, *lands)], after)
    return [(res[k], res[n + k], res[2 * n + k], res[3 * n + k]) for k in range(n)], res[-1]


def _weights_wait(started, after, name):
    send_sems, recv_sems, shard, land = started

    def body(s_ref, l_ref, send_ref, recv_ref, after_ref, s_out, l_out):
        for cp in _weight_copies(s_ref, l_ref, send_ref, recv_ref, True):
            cp.wait_send()
            cp.wait_recv()

    return pl.pallas_call(
        body, name=name,
        in_specs=[HBM, HBM, SEM, SEM, ANY], out_specs=[HBM, HBM],
        out_shape=[pltpu.HBM(shard.shape, shard.dtype), pltpu.HBM(land.shape, land.dtype)],
        input_output_aliases={0: 0, 1: 1},
        compiler_params=pltpu.CompilerParams(has_side_effects=DATAFLOW),
    )(shard, land, send_sems, recv_sems, after)[1]


def _grad_copies(g_ref, land_ref, send_sems, recv_sems):
    x, y, c = _place()
    cps = []
    for d in range(1, 8):
        px, py, pc = x ^ (d >> 2), y ^ ((d >> 1) & 1), c ^ (d & 1)
        cps.append(pltpu.make_async_remote_copy(
            src_ref=g_ref.at[2 * px + py, pc], dst_ref=land_ref.at[d - 1], send_sem=send_sems.at[d - 1],
            recv_sem=recv_sems.at[d - 1], device_id=(px, py, pc), device_id_type=MESH))
    return cps


def _grads_start(grads_b, name):
    n = len(grads_b)
    lands = [lax.empty((7, g.shape[2], D), BF16) for g in grads_b]

    def body(*refs):
        g, land = refs[:n], refs[n:2 * n]
        send_sems, recv_sems = refs[2 * n:3 * n], refs[3 * n:4 * n]
        for k in range(n):
            for cp in _grad_copies(g[k], land[k], send_sems[k], recv_sems[k]):
                cp.start()
        refs[-1][...] = jnp.zeros_like(refs[-1])

    res = pl.pallas_call(
        body, name=name,
        in_specs=[HBM] * (2 * n), out_specs=[SEM] * (2 * n) + [HBM] * (2 * n) + [VMEM],
        out_shape=[pltpu.SemaphoreType.DMA((7,))] * (2 * n)
        + [pltpu.HBM(a.shape, a.dtype) for a in (*grads_b, *lands)] + [jax.ShapeDtypeStruct((8, 128), F32)],
        input_output_aliases={i: i + 2 * n for i in range(2 * n)},
        compiler_params=pltpu.CompilerParams(has_side_effects=DATAFLOW),
    )(*[_hbm(a) for a in (*grads_b, *lands)])
    return [(res[k], res[n + k], res[2 * n + k], res[3 * n + k]) for k in range(n)], res[-1]


def _grads_wait(started, after, name):
    n = len(started)

    def body(*refs):
        g, land = refs[:n], refs[n:2 * n]
        send_sems, recv_sems = refs[2 * n:3 * n], refs[3 * n:4 * n]
        for k in range(n):
            for cp in _grad_copies(g[k], land[k], send_sems[k], recv_sems[k]):
                cp.wait_send()
                cp.wait_recv()

    gs = [st[2] for st in started]
    lands = [st[3] for st in started]
    res = pl.pallas_call(
        body, name=name,
        in_specs=[HBM] * (2 * n) + [SEM] * (2 * n) + [ANY], out_specs=[HBM] * (2 * n),
        out_shape=[pltpu.HBM(a.shape, a.dtype) for a in (*gs, *lands)],
        input_output_aliases={i: i for i in range(2 * n)},
        compiler_params=pltpu.CompilerParams(has_side_effects=DATAFLOW),
    )(*gs, *lands, *[st[0] for st in started], *[st[1] for st in started], after)
    return res[n:]


def _sum_partials(grad4, got, cb, name, tr):
    h = grad4.shape[2]
    per_half = h // tr

    def body(cb_ref, g_ref, o_ref, out_ref):
        acc = g_ref[...]
        for j in range(7):
            acc = acc + o_ref[j].astype(F32)
        out_ref[...] = acc

    return pl.pallas_call(
        body, name=name,
        grid_spec=pltpu.PrefetchScalarGridSpec(
            num_scalar_prefetch=1, grid=(per_half,),
            in_specs=[pl.BlockSpec((None, None, tr, D), lambda i, cb_ref: (cb_ref[1], cb_ref[0], i, 0)),
                      pl.BlockSpec((7, tr, D), lambda i, cb_ref: (0, i, 0))],
            out_specs=pl.BlockSpec((tr, D), lambda i, cb_ref: (cb_ref[0] * per_half + i, 0))),
        out_shape=jax.ShapeDtypeStruct((2 * h, D), F32),
        compiler_params=_cp(("arbitrary",)),
    )(cb, grad4, got)


def _swap_halves(shards, name):
    n = len(shards)

    def body(*refs):
        out, send_sems, recv_sems = refs[n:2 * n], refs[2 * n], refs[2 * n + 1]
        x, y, c = _place()
        cps = []
        for k in range(n):
            h = shards[k].shape[0] // 2
            mine = out[k].at[pl.ds(pl.multiple_of(c * h, 8), h)]
            cp = pltpu.make_async_remote_copy(src_ref=mine, dst_ref=mine, send_sem=send_sems.at[k],
                                              recv_sem=recv_sems.at[k], device_id=(x, y, 1 - c), device_id_type=MESH)
            cp.start()
            cps.append(cp)
        for cp in cps:
            cp.wait()

    return pl.pallas_call(
        body, name=name,
        in_specs=[ANY] * n, out_specs=[ANY] * n,
        out_shape=[jax.ShapeDtypeStruct(sh.shape, F32) for sh in shards],
        input_output_aliases={k: k for k in range(n)},
        scratch_shapes=[pltpu.SemaphoreType.DMA((n,)), pltpu.SemaphoreType.DMA((n,))],
        compiler_params=pltpu.CompilerParams(has_side_effects=True),
    )(*shards)


def _share_halves(shards, small):
    n = len(shards)
    rows = small.shape[0]

    def body(*refs):
        small_ref = refs[n]
        out, total_ref = refs[n + 1:2 * n + 1], refs[2 * n + 1]
        all_ref, send_sems, recv_sems, ssend, srecv = refs[2 * n + 2:]
        x, y, c = _place()
        me = 4 * x + 2 * y + c
        cps = []
        for k in range(n):
            h = shards[k].shape[0] // 2
            mine = out[k].at[pl.ds(pl.multiple_of(c * h, 8), h)]
            cp = pltpu.make_async_remote_copy(src_ref=mine, dst_ref=mine, send_sem=send_sems.at[k],
                                              recv_sem=recv_sems.at[k], device_id=(x, y, 1 - c), device_id_type=MESH)
            cp.start()
            cps.append(cp)
        all_ref[me] = small_ref[...]
        peers = []
        for d in range(1, 8):
            px, py, pc = x ^ (d >> 2), y ^ ((d >> 1) & 1), c ^ (d & 1)
            cp = pltpu.make_async_remote_copy(src_ref=small_ref, dst_ref=all_ref.at[me],
                                              send_sem=ssend.at[d - 1], recv_sem=srecv.at[d - 1],
                                              device_id=(px, py, pc), device_id_type=MESH)
            cp.start()
            peers.append(cp)
        for cp in peers:
            cp.wait()
        acc = all_ref[0]
        for d in range(1, 8):
            acc = acc + all_ref[d]
        total_ref[...] = acc
        for cp in cps:
            cp.wait()

    return pl.pallas_call(
        body, name="share_halves",
        in_specs=[ANY] * n + [VMEM], out_specs=[ANY] * n + [VMEM],
        out_shape=[jax.ShapeDtypeStruct(sh.shape, F32) for sh in shards] + [jax.ShapeDtypeStruct((rows, D), F32)],
        input_output_aliases={k: k for k in range(n)},
        scratch_shapes=[pltpu.VMEM((8, rows, D), F32), pltpu.SemaphoreType.DMA((n,)), pltpu.SemaphoreType.DMA((n,)),
                        pltpu.SemaphoreType.DMA((7,)), pltpu.SemaphoreType.DMA((7,))],
        compiler_params=pltpu.CompilerParams(has_side_effects=True),
    )(*shards, small)


def _adamw(w, g, m, v, name, tr):
    rows, cols = w.shape

    def body(w_ref, g_ref, m_ref, v_ref, d_ref, nm_ref, nv_ref):
        g_ = g_ref[...]
        nm = ADAM_B1 * m_ref[...] + (1.0 - ADAM_B1) * g_
        nv = ADAM_B2 * v_ref[...] + (1.0 - ADAM_B2) * (g_ * g_)
        m_hat = nm / (1.0 - ADAM_B1 ** ADAM_STEP)
        v_hat = nv / (1.0 - ADAM_B2 ** ADAM_STEP)
        d_ref[...] = -ADAM_LR * (m_hat / (jnp.sqrt(v_hat) + ADAM_EPS) + ADAM_WD * w_ref[...])
        nm_ref[...] = nm
        nv_ref[...] = nv

    spec = pl.BlockSpec((tr, cols), lambda i: (i, 0))
    return pl.pallas_call(
        body, name=name, grid=(rows // tr,),
        in_specs=[spec] * 4, out_specs=[spec] * 3,
        out_shape=[jax.ShapeDtypeStruct((rows, cols), F32)] * 3,
        compiler_params=_cp(("parallel",)),
    )(w, g, m, v)


def _local_step(x, target, w_in_t, late_weights, norm_a_g, norm_b_g, sinks_a, ln1_g, ln1_b,
                conv_w, conv_b, ln2_g, ln2_b, slopes, on_grad):
    cwb = jnp.concatenate([conv_w, conv_b[None]], axis=0).reshape(4, 2, FF)

    proj, xb = _proj(x, w_in_t, "proj")
    o_a, lse_a = _attn_a_fwd(proj, sinks_a)
    fwd_b = [_attn_b_fwd(proj, slopes, r) for r in B_DILATIONS]
    w_o = late_weights(1, fwd_b[-1][1])
    o_b, lse_b, cat, z1, h1, h1b = _mix_ln1(x, o_a, [f[0] for f in fwd_b], [f[1] for f in fwd_b],
                                           norm_a_g, norm_b_g, w_o, ln1_g, ln1_b)
    w_up = late_weights(2, h1b)
    up, a, gate, a1 = _up_conv_gelu(h1b, w_up, cwb)
    w_down = late_weights(3, a)
    dz2, dz2b, st2 = _down_ln2_loss(a, w_down, h1, target, ln2_g, ln2_b)

    on_grad(3, *_grad_w(a, dz2b, "grad_w_down", tm=FF // 2))
    dup, dconv = _conv_gelu_bwd(_d_act(dz2b, w_down), up, gate, a1, cwb)
    on_grad(2, *_grad_w(dup, h1b, "grad_w_up", tm=FF // 2, lhs_halves=True))
    dz1, dz1b, st1 = _dh1_ln1_bwd(dz2, dup, w_up, z1, ln1_g)
    tok = on_grad(1, *_grad_w(cat, dz1b, "grad_w_o", tm=512))
    d_oa, d_ob, st_n = _dcat_rms_bwd(dz1b, w_o, o_a, o_b, norm_a_g + tok[0, 0], norm_b_g)
    dqa, dka, dva, dsink = _attn_a_bwd(proj, sinks_a, d_oa, o_a, lse_a)
    bwd_b = None
    for r in B_DILATIONS:
        bwd_b = _attn_b_bwd(proj, slopes, d_ob, o_b, lse_b, r, bwd_b)
    dproj = _dproj_combine(dqa, dka, dva, bwd_b)
    tok = on_grad(0, *_grad_w(dproj, xb, "grad_w_in", tm=WA))
    gx = _grad_x(dz1, dproj, w_in_t, tok)

    dconv = dconv.reshape(4, 2 * FF)
    small = dict(loss=st2[2, 0:1], norm_a_g=st_n[0], norm_b_g=st_n[1], sinks_a=dsink[:, 0],
                 ln1_g=st1[0], ln1_b=st1[1], conv_w=dconv[0:3].reshape(-1), conv_b=dconv[3],
                 ln2_g=st2[0], ln2_b=st2[1])
    return gx, small


SMALL_ORDER = ("loss", "norm_a_g", "norm_b_g", "sinks_a", "ln1_g", "ln1_b", "conv_b", "ln2_g", "ln2_b", "conv_w")
SMALL_SIZES = dict(loss=1, norm_a_g=512, norm_b_g=512, sinks_a=8, ln1_g=D, ln1_b=D, conv_b=2 * FF, ln2_g=D, ln2_b=D,
                   conv_w=3 * 2 * FF)


def _pack(parts, rows):
    flat = jnp.concatenate([parts[k].reshape(-1).astype(F32) for k in parts])
    return jnp.pad(flat, (0, rows * D - flat.shape[0])).reshape(rows, D)


def _unpack(buf, names, sizes):
    flat = buf.reshape(-1)
    out, at = {}, 0
    for k in names:
        out[k] = flat[at:at + sizes[k]]
        at += sizes[k]
    return out


def kernel(x, w_in, norm_a_g, norm_b_g, sinks_a, w_o, ln1_g, ln1_b, w_up, conv_w, conv_b, w_down, ln2_g, ln2_b, loss_target, m_w_in, m_norm_a_g, m_norm_b_g, m_sinks_a, m_w_o, m_ln1_g, m_ln1_b, m_w_up, m_conv_w, m_conv_b, m_w_down, m_ln2_g, m_ln2_b, v_w_in, v_norm_a_g, v_norm_b_g, v_sinks_a, v_w_o, v_ln1_g, v_ln1_b, v_w_up, v_conv_w, v_conv_b, v_w_down, v_ln2_g, v_ln2_b):
    xi, yi, ci = _place()
    chip = (2 * xi + yi).astype(I32)
    core = ci.astype(I32)

    w_in_rows, m_w_in_rows, v_w_in_rows = w_in.T, m_w_in.T, v_w_in.T
    shards = (w_in_rows.astype(BF16), w_o.astype(BF16), w_up.astype(BF16), w_down.astype(BF16))
    w_in_t, conv_w4 = _gather_w_in(shards[0], conv_w)
    conv_w_f = conv_w4.transpose(1, 0, 2).reshape(3, 2 * FF)
    w_started, w_tok = _weights_start(shards[1:], conv_w4)
    slopes = jnp.asarray(SLOPES, F32) + w_tok[0, 0]

    halves_rows = [r // 2 for r in SHARD_ROWS]
    grads4, grads_b4, started = [None] * 4, [None] * 4, [None] * 4

    def on_grad(k, g, g_b):
        grads4[k] = g.reshape(N_CHIPS, 2, halves_rows[k], D)
        grads_b4[k] = g_b.reshape(N_CHIPS, 2, halves_rows[k], D)
        if k > 1:
            return None
        group = (1, 2, 3) if k == 1 else (0,)
        sts, tok = _grads_start([grads_b4[i] for i in group], f"grads_start_{k}")
        for i, st in zip(group, sts):
            started[i] = st
        return tok

    gx, small = _local_step(
        x[0], loss_target[0], w_in_t, lambda k, after: _weights_wait(w_started[k - 1], after, f"weights_wait_{k}"),
        norm_a_g, norm_b_g, sinks_a, ln1_g, ln1_b, conv_w_f, conv_b, ln2_g, ln2_b, slopes, on_grad)

    tiles = (96, 128, 352, 176)
    core_chip = jnp.stack([core, chip])
    got = _grads_wait(started[1:], gx, "grads_wait_1")
    halves = [_sum_partials(grads4[k], got[k - 1], core_chip, f"sum_partials_{k}", tiles[k]) for k in (1, 2, 3)]
    g_w_o, g_w_up_rows, g_w_down = _swap_halves(halves, "swap_halves")
    g_w_up = g_w_up_rows.T
    delta, new_m, new_v = {}, {}, {}
    for k, g, tr in (("w_o", g_w_o, 128), ("w_up", g_w_up, 256), ("w_down", g_w_down, 176)):
        delta[k], new_m[k], new_v[k] = _adamw(dict(w_o=w_o, w_up=w_up, w_down=w_down)[k], g,
                                              dict(w_o=m_w_o, w_up=m_w_up, w_down=m_w_down)[k],
                                              dict(w_o=v_w_o, w_up=v_w_up, w_down=v_w_down)[k], f"adamw_{k}", tr)

    got = _grads_wait(started[:1], delta["w_up"], "grads_wait_0")
    half_in = _sum_partials(grads4[0], got[0], core_chip, "sum_partials_0", tiles[0])
    small_rows = 32
    g_w_in_rows, totals = _share_halves([half_in], _pack({k: small[k] for k in SMALL_ORDER}, small_rows))
    tot = _unpack(totals, SMALL_ORDER, SMALL_SIZES)
    loss = tot["loss"][0]
    cols = 2 * FF // N_CHIPS
    g_conv_w = lax.dynamic_slice(tot["conv_w"].reshape(3, 2 * FF), (0, chip * cols), (3, cols))
    g_small = dict(norm_a_g=tot["norm_a_g"], norm_b_g=tot["norm_b_g"], sinks_a=tot["sinks_a"], ln1_g=tot["ln1_g"],
                   ln1_b=tot["ln1_b"], conv_w=g_conv_w, conv_b=tot["conv_b"], ln2_g=tot["ln2_g"], ln2_b=tot["ln2_b"])

    weights = dict(w_in=w_in, norm_a_g=norm_a_g, norm_b_g=norm_b_g, sinks_a=sinks_a, w_o=w_o, ln1_g=ln1_g, ln1_b=ln1_b,
                   w_up=w_up, conv_w=conv_w, conv_b=conv_b, w_down=w_down, ln2_g=ln2_g, ln2_b=ln2_b)
    ms = dict(w_in=m_w_in, norm_a_g=m_norm_a_g, norm_b_g=m_norm_b_g, sinks_a=m_sinks_a, w_o=m_w_o, ln1_g=m_ln1_g,
              ln1_b=m_ln1_b, w_up=m_w_up, conv_w=m_conv_w, conv_b=m_conv_b, w_down=m_w_down, ln2_g=m_ln2_g, ln2_b=m_ln2_b)
    vs = dict(w_in=v_w_in, norm_a_g=v_norm_a_g, norm_b_g=v_norm_b_g, sinks_a=v_sinks_a, w_o=v_w_o, ln1_g=v_ln1_g,
              ln1_b=v_ln1_b, w_up=v_w_up, conv_w=v_conv_w, conv_b=v_conv_b, w_down=v_w_down, ln2_g=v_ln2_g, ln2_b=v_ln2_b)
    order = list(weights)
    grad = dict(g_small, w_in=g_w_in_rows.T, w_o=g_w_o, w_up=g_w_up, w_down=g_w_down)

    delta["w_in"], new_m["w_in"], new_v["w_in"] = [
        a.T for a in _adamw(w_in_rows, g_w_in_rows, m_w_in_rows, v_w_in_rows, "adamw_w_in", 144)]
    small_names = [k for k in order if k not in delta]
    sizes = {k: weights[k].size for k in small_names}
    rows = 16
    packed = [_pack({k: src[k] for k in small_names}, rows) for src in (weights, grad, ms, vs)]
    for res, buf in zip((delta, new_m, new_v), _adamw(*packed, "adamw_small", rows)):
        for k, val in _unpack(buf, small_names, sizes).items():
            res[k] = val.reshape(weights[k].shape)

    return (loss, gx[None], *[grad[k] for k in order], *[delta[k] for k in order],
            *[new_m[k] for k in order], *[new_v[k] for k in order])
```

```python
import functools
import math

import jax
import jax.numpy as jnp
from jax import lax
from jax.experimental import pallas as pl
from jax.experimental.pallas import tpu as pltpu

F32, BF16, I32 = jnp.float32, jnp.bfloat16, jnp.int32

D = 1024
FF = 2816
HD = 64
NH = 8
WA, WB = 768, 1536
WIN = WA + WB
BLK = 128
ALPHA = 2.0 ** 0.25
LN_EPS, RMS_EPS = 1e-5, 1e-6
SCALE = 1.0 / math.sqrt(HD)
A_MAX_DIST, B_MAX_DIST = 127, 128
B_DILATIONS = (1, 4, 16)
SLOPES = tuple(2.0 ** (-(i + 1)) for i in range(NH))
SHARD_ROWS = (WIN // 4, D // 4, 2 * FF // 4, FF // 4)
N_CHIPS = 4
ADAM_LR, ADAM_B1, ADAM_B2, ADAM_EPS, ADAM_WD, ADAM_STEP = 0.001, 0.9, 0.999, 1e-08, 0.01, 10
MESH = pl.DeviceIdType.MESH
ANY = pl.BlockSpec(memory_space=pl.ANY)
SMEM = pl.BlockSpec(memory_space=pltpu.SMEM)
VMEM = pl.BlockSpec(memory_space=pltpu.VMEM)
HBM = pl.BlockSpec(memory_space=pltpu.HBM)
SEM = pl.BlockSpec(memory_space=pltpu.SEMAPHORE)
DATAFLOW = pltpu.SideEffectType.DATAFLOW_SIDE_EFFECTING


def _cp(sem, mb=48):
    return pltpu.CompilerParams(dimension_semantics=sem, vmem_limit_bytes=mb << 20)


def _nn(a, b):
    return lax.dot_general(a, b, (((1,), (0,)), ((), ())), preferred_element_type=F32)


def _nt(a, b):
    return lax.dot_general(a, b, (((1,), (1,)), ((), ())), preferred_element_type=F32)


def _tn(a, b):
    return lax.dot_general(a, b, (((0,), (0,)), ((), ())), preferred_element_type=F32)


def _resident(shape):
    n = len(shape)
    return pl.BlockSpec(shape, lambda *_: (0,) * n, pipeline_mode=pl.Buffered(1))


def _const(shape):
    n = len(shape)
    return pl.BlockSpec(shape, lambda *_: (0,) * n)


def _proj(x, w_t, name, tm=512):
    s = x.shape[0]
    n = w_t.shape[0]

    def body(x_ref, w_ref, o_ref, xb_ref):
        xb = x_ref[...].astype(BF16)
        xb_ref[...] = xb
        o_ref[...] = _nt(xb, w_ref[...])

    return pl.pallas_call(
        body, name=name, grid=(s // tm,),
        in_specs=[pl.BlockSpec((tm, D), lambda i: (i, 0)), _resident((n, D))],
        out_specs=[pl.BlockSpec((tm, n), lambda i: (i, 0)), pl.BlockSpec((tm, D), lambda i: (i, 0))],
        out_shape=[jax.ShapeDtypeStruct((s, n), F32), jax.ShapeDtypeStruct((s, D), BF16)],
        compiler_params=_cp(("parallel",)),
    )(x, w_t)


def _grad_w(lhs, rhs, name, tm, tk=512, lhs_halves=False):
    s = rhs.shape[0]
    if lhs_halves:
        per_half = lhs.shape[2] // tm
        n = 2 * lhs.shape[2]
        lhs_spec = pl.BlockSpec((None, tk, tm), lambda i, k: (i // per_half, k, i % per_half))
    else:
        n = lhs.shape[1]
        lhs_spec = pl.BlockSpec((tk, tm), lambda i, k: (k, i))
    nk = s // tk

    def body(l_ref, r_ref, o_ref, ob_ref):
        k = pl.program_id(1)

        @pl.when(k == 0)
        def _():
            o_ref[...] = jnp.zeros_like(o_ref)

        o_ref[...] += _tn(l_ref[...].astype(BF16), r_ref[...].astype(BF16))

        @pl.when(k == nk - 1)
        def _():
            ob_ref[...] = o_ref[...].astype(BF16)

    return pl.pallas_call(
        body, name=name, grid=(n // tm, nk),
        in_specs=[lhs_spec, pl.BlockSpec((tk, D), lambda i, k: (k, 0))],
        out_specs=[pl.BlockSpec((tm, D), lambda i, k: (i, 0))] * 2,
        out_shape=[jax.ShapeDtypeStruct((n, D), F32), jax.ShapeDtypeStruct((n, D), BF16)],
        compiler_params=_cp(("parallel", "arbitrary")),
    )(lhs, rhs)


def _band_base(max_dist, dist_unit, first):
    row = lax.broadcasted_iota(I32, (BLK, 2 * BLK), 0)
    col = lax.broadcasted_iota(I32, (BLK, 2 * BLK), 1)
    dist = BLK + row - col
    ok = (dist >= 0) & (dist <= max_dist)
    if first:
        ok = ok & (col >= BLK)
    return jnp.where(ok, dist.astype(F32) * (-float(dist_unit)), -jnp.inf)


def _half_mask(shape, e):
    lane = lax.broadcasted_iota(I32, shape, 1)
    return (lane < HD) if e == 0 else (lane >= HD)


def _to_half(x, e, g):
    if g != e:
        x = pltpu.roll(x, HD, 1)
    return jnp.where(_half_mask(x.shape, g), x, 0.0)


def _stack_heads(scalars, tile):
    return jnp.concatenate([scalars[0] * tile, scalars[1] * tile], axis=0)


def _pair_fwd(q2, kb, vb, base, slopes, kv_heads, sinks):
    lo = _half_mask((BLK, 2 * HD), 0)
    if sinks is not None:
        o2 = lse2 = None
        for e in (0, 1):
            g = kv_heads[e]
            qv = (_to_half(q2, e, g) * SCALE).astype(BF16)
            s = _nt(qv, kb) + slopes[e] * base
            m = jnp.maximum(jnp.max(s, axis=1, keepdims=True), sinks[e])
            p = jnp.exp(s - m)
            l = jnp.sum(p, axis=1, keepdims=True) + jnp.exp(sinks[e] - m)
            oh = _nn(p.astype(BF16), vb) / l
            if g != e:
                oh = pltpu.roll(oh, HD, 1)
            lse = jnp.broadcast_to(m + jnp.log(l), (BLK, 2 * HD))
            o2 = oh if e == 0 else jnp.where(lo, o2, oh)
            lse2 = lse if e == 0 else jnp.where(lo, lse2, lse)
        return o2, lse2
    qs = jnp.concatenate([_to_half(q2, e, kv_heads[e]) * SCALE for e in (0, 1)], axis=0).astype(BF16)
    s = _nt(qs, kb) + (base if slopes is None else _stack_heads(slopes, base))
    m = jnp.max(s, axis=1, keepdims=True)
    p = jnp.exp(s - m)
    l = jnp.sum(p, axis=1, keepdims=True)
    o = _nn(p.astype(BF16), vb) / l
    lse = m + jnp.log(l)
    halves = []
    for e in (0, 1):
        oh = o[e * BLK:(e + 1) * BLK]
        halves.append(pltpu.roll(oh, HD, 1) if kv_heads[e] != e else oh)
    o2 = jnp.where(lo, halves[0], halves[1])
    lse2 = jnp.where(lo, jnp.broadcast_to(lse[:BLK], (BLK, 2 * HD)), jnp.broadcast_to(lse[BLK:], (BLK, 2 * HD)))
    return o2, lse2


def _pair_bwd(q2, kb, vb, do2, o2, lse2, base, slopes, kv_heads, sinks):
    lo = _half_mask((BLK, 2 * HD), 0)
    prod = do2 * o2
    lses, deltas = [], []
    for e in (0, 1):
        hq = _half_mask((BLK, 2 * HD), e)
        lses.append(jnp.max(jnp.where(hq, lse2, -jnp.inf), axis=1, keepdims=True))
        deltas.append(jnp.sum(jnp.where(hq, prod, 0.0), axis=1, keepdims=True))
    lse = jnp.concatenate(lses, axis=0)
    delta = jnp.concatenate(deltas, axis=0)
    qs = jnp.concatenate([_to_half(q2, e, kv_heads[e]) * SCALE for e in (0, 1)], axis=0).astype(BF16)
    dos = jnp.concatenate([_to_half(do2, e, kv_heads[e]) for e in (0, 1)], axis=0).astype(BF16)
    p = jnp.exp(_nt(qs, kb) + (base if slopes is None else _stack_heads(slopes, base)) - lse)
    ds = (p * (_nt(dos, vb) - delta)).astype(BF16)
    dq = _nn(ds, kb) * SCALE
    halves = []
    for e in (0, 1):
        dqh = dq[e * BLK:(e + 1) * BLK]
        halves.append(pltpu.roll(dqh, HD, 1) if kv_heads[e] != e else dqh)
    dq2 = jnp.where(lo, halves[0], halves[1])
    dk2 = _tn(ds, qs)
    dv2 = _tn(p.astype(BF16), dos)
    dsinks = []
    if sinks is not None:
        for e in (0, 1):
            dsinks.append(jnp.sum(-jnp.exp(sinks[e] - lses[e]) * deltas[e], axis=0, keepdims=True))
    return dq2, dk2, dv2, dsinks


A_BLOCKS_PER_STEP = 2
A_BLOCKS_PER_STEP_BWD = 1


def _attn_a_fwd(proj, sinks):
    s = proj.shape[0]
    nq = A_BLOCKS_PER_STEP
    rows = BLK * nq
    steps = s // rows

    def body(sink_ref, q_ref, kp_ref, kc_ref, vp_ref, vc_ref, o_ref, lse_ref):
        n = pl.program_id(0)
        base_rest = _band_base(A_MAX_DIST, 1, False)
        base_0 = jnp.where(n > 0, base_rest, _band_base(A_MAX_DIST, 1, True))
        for i in range(nq):
            cur = pl.ds(i * BLK, BLK)
            k_prev = kc_ref[pl.ds((i - 1) * BLK, BLK), :] if i > 0 else kp_ref[...]
            v_prev = vc_ref[pl.ds((i - 1) * BLK, BLK), :] if i > 0 else vp_ref[...]
            kb = jnp.concatenate([k_prev, kc_ref[cur, :]], axis=0).astype(BF16)
            vb = jnp.concatenate([v_prev, vc_ref[cur, :]], axis=0).astype(BF16)
            for j in range(NH // 2):
                g = j // 2
                o2, lse2 = _pair_fwd(q_ref[cur, 128 * j:128 * (j + 1)], kb, vb, base_rest if i > 0 else base_0,
                                     (SLOPES[2 * j], SLOPES[2 * j + 1]), (g, g), (sink_ref[2 * j], sink_ref[2 * j + 1]))
                o_ref[cur, 128 * j:128 * (j + 1)] = o2
                lse_ref[cur, 128 * j:128 * (j + 1)] = lse2

    before = lambda n: jnp.maximum(n * nq - 1, 0)
    return pl.pallas_call(
        body, name="attn_a_fwd", grid=(steps,),
        in_specs=[SMEM,
                  pl.BlockSpec((rows, 512), lambda n: (n, 0)),
                  pl.BlockSpec((BLK, 128), lambda n: (before(n), 4)), pl.BlockSpec((rows, 128), lambda n: (n, 4)),
                  pl.BlockSpec((BLK, 128), lambda n: (before(n), 5)), pl.BlockSpec((rows, 128), lambda n: (n, 5))],
        out_specs=[pl.BlockSpec((rows, 512), lambda n: (n, 0))] * 2,
        out_shape=[jax.ShapeDtypeStruct((s, 512), F32)] * 2,
        compiler_params=_cp(("parallel",)),
    )(sinks, proj, proj, proj, proj, proj)


def _attn_a_bwd(proj, sinks, d_o, o, lse):
    s = proj.shape[0]
    nq = A_BLOCKS_PER_STEP_BWD
    rows = BLK * nq
    steps = s // rows

    def body(sink_ref, q_ref, kp_ref, kc_ref, vp_ref, vc_ref, do_ref, o_ref, lse_ref,
             dq_ref, dk_ref, dv_ref, dsink_ref, kcar, vcar):
        n = pl.program_id(0)

        @pl.when(n == 0)
        def _():
            kcar[...] = jnp.zeros_like(kcar)
            vcar[...] = jnp.zeros_like(vcar)
            dsink_ref[...] = jnp.zeros_like(dsink_ref)

        dk_ref[...] = kcar[...]
        dv_ref[...] = vcar[...]

        @pl.when(n < steps)
        def _():
            base_rest = _band_base(A_MAX_DIST, 1, False)
            base_0 = jnp.where(n > 0, base_rest, _band_base(A_MAX_DIST, 1, True))
            for i in range(nq):
                cur = pl.ds(i * BLK, BLK)
                k_prev = kc_ref[pl.ds((i - 1) * BLK, BLK), :] if i > 0 else kp_ref[...]
                v_prev = vc_ref[pl.ds((i - 1) * BLK, BLK), :] if i > 0 else vp_ref[...]
                kb = jnp.concatenate([k_prev, kc_ref[cur, :]], axis=0).astype(BF16)
                vb = jnp.concatenate([v_prev, vc_ref[cur, :]], axis=0).astype(BF16)
                dk_win = dv_win = None
                for j in range(NH // 2):
                    g = j // 2
                    sl = slice(128 * j, 128 * (j + 1))
                    dq2, dk2, dv2, dsk = _pair_bwd(q_ref[cur, sl], kb, vb, do_ref[cur, sl], o_ref[cur, sl],
                                                   lse_ref[cur, sl], base_rest if i > 0 else base_0,
                                                   (SLOPES[2 * j], SLOPES[2 * j + 1]), (g, g),
                                                   (sink_ref[2 * j], sink_ref[2 * j + 1]))
                    dq_ref[cur, sl] = dq2
                    dk_win = dk2 if j == 0 else dk_win + dk2
                    dv_win = dv2 if j == 0 else dv_win + dv2
                    for e in (0, 1):
                        h = 2 * j + e
                        dsink_ref[h:h + 1, :] += jnp.broadcast_to(dsk[e], (1, 128))
                if i == 0:
                    last = pl.ds((nq - 1) * BLK, BLK)
                    dk_ref[last, :] += dk_win[:BLK]
                    dv_ref[last, :] += dv_win[:BLK]
                else:
                    kcar[pl.ds((i - 1) * BLK, BLK), :] += dk_win[:BLK]
                    vcar[pl.ds((i - 1) * BLK, BLK), :] += dv_win[:BLK]
                kcar[cur, :] = dk_win[BLK:]
                vcar[cur, :] = dv_win[BLK:]

    cur_step = lambda n: jnp.minimum(n, steps - 1)
    before = lambda n: jnp.maximum(cur_step(n) * nq - 1, 0)
    out_prev = lambda n: jnp.maximum(n - 1, 0)
    wide = pl.BlockSpec((rows, 512), lambda n: (cur_step(n), 0))
    return pl.pallas_call(
        body, name="attn_a_bwd", grid=(steps + 1,),
        in_specs=[SMEM, wide,
                  pl.BlockSpec((BLK, 128), lambda n: (before(n), 4)), pl.BlockSpec((rows, 128), lambda n: (cur_step(n), 4)),
                  pl.BlockSpec((BLK, 128), lambda n: (before(n), 5)), pl.BlockSpec((rows, 128), lambda n: (cur_step(n), 5)),
                  wide, wide, wide],
        out_specs=[wide,
                   pl.BlockSpec((rows, 128), lambda n: (out_prev(n), 0)),
                   pl.BlockSpec((rows, 128), lambda n: (out_prev(n), 0)),
                   pl.BlockSpec((NH, 128), lambda n: (0, 0))],
        out_shape=[jax.ShapeDtypeStruct((s, 512), F32), jax.ShapeDtypeStruct((s, 128), F32),
                   jax.ShapeDtypeStruct((s, 128), F32), jax.ShapeDtypeStruct((NH, 128), F32)],
        scratch_shapes=[pltpu.VMEM((rows, 128), F32), pltpu.VMEM((rows, 128), F32)],
        compiler_params=_cp(("arbitrary",)),
    )(sinks, proj, proj, proj, proj, proj, d_o, o, lse)


def _stream(rho, i, r):
    start = i * BLK * r + rho
    return pl.ds(start, BLK, stride=r) if r > 1 else pl.ds(start, BLK)


def _for_streams(r, fn, side_by_side=4):
    if r <= side_by_side:
        for rho in range(r):
            fn(rho)
    else:
        def group(it, carry):
            for u in range(side_by_side):
                fn(side_by_side * it + u)
            return carry

        lax.fori_loop(0, r // side_by_side, group, 0)


B_BLOCKS_PER_STEP = {1: 8, 4: 2, 16: 1}
B_BLOCKS_PER_STEP_FWD = {1: 8, 4: 2, 16: 1}


def _attn_b_fwd(proj, slopes, r):
    s = proj.shape[0]
    nq = B_BLOCKS_PER_STEP_FWD[r]
    rows = BLK * r * nq
    steps = s // rows
    qc, kc, vc = WA // 128, WA // 128 + 4, WA // 128 + 8

    def body(slope_ref, q_ref, kp_ref, kc_ref, vp_ref, vc_ref, o_ref, lse_ref):
        j = pl.program_id(0)
        sb = pl.program_id(1)
        sl2 = (slope_ref[2 * j], slope_ref[2 * j + 1])
        bias_rest = _stack_heads(sl2, _band_base(B_MAX_DIST, r, False))
        bias_0 = jnp.where(sb > 0, bias_rest, _stack_heads(sl2, _band_base(B_MAX_DIST, r, True)))

        def stream(rho):
            for i in range(nq):
                cur = _stream(rho, i, r)
                k_prev = kc_ref[_stream(rho, i - 1, r), :] if i > 0 else kp_ref[_stream(rho, 0, r), :]
                v_prev = vc_ref[_stream(rho, i - 1, r), :] if i > 0 else vp_ref[_stream(rho, 0, r), :]
                kb = jnp.concatenate([k_prev, kc_ref[cur, :]], axis=0).astype(BF16)
                vb = jnp.concatenate([v_prev, vc_ref[cur, :]], axis=0).astype(BF16)
                o2, lse2 = _pair_fwd(q_ref[cur, :], kb, vb, bias_rest if i > 0 else bias_0, None, (0, 1), None)
                o_ref[cur, :] = o2
                lse_ref[cur, :] = lse2

        _for_streams(r, stream, side_by_side=8)

    before = lambda sb: jnp.maximum(sb * nq - 1, 0)
    return pl.pallas_call(
        body, name=f"attn_b_fwd_r{r}", grid=(NH // 2, steps),
        in_specs=[SMEM,
                  pl.BlockSpec((rows, 128), lambda j, sb: (sb, qc + j)),
                  pl.BlockSpec((BLK * r, 128), lambda j, sb: (before(sb), kc + j)),
                  pl.BlockSpec((rows, 128), lambda j, sb: (sb, kc + j)),
                  pl.BlockSpec((BLK * r, 128), lambda j, sb: (before(sb), vc + j)),
                  pl.BlockSpec((rows, 128), lambda j, sb: (sb, vc + j))],
        out_specs=[pl.BlockSpec((rows, 128), lambda j, sb: (sb, j))] * 2,
        out_shape=[jax.ShapeDtypeStruct((s, 512), F32)] * 2,
        compiler_params=_cp(("parallel", "parallel")),
    )(slopes, proj, proj, proj, proj, proj)


def _attn_b_bwd(proj, slopes, d_o, o, lse, r, so_far=None):
    s = proj.shape[0]
    nq = B_BLOCKS_PER_STEP[r]
    rows = BLK * r * nq
    steps = s // rows
    qc, kc, vc = WA // 128, WA // 128 + 4, WA // 128 + 8
    chained = so_far is not None

    def body(slope_ref, q_ref, kp_ref, kc_ref, vp_ref, vc_ref, do_ref, o_ref, lse_ref, *rest):
        if chained:
            pq_ref, pk_ref, pv_ref, dq_ref, dk_ref, dv_ref, kcar, vcar = rest
        else:
            dq_ref, dk_ref, dv_ref, kcar, vcar = rest
        j = pl.program_id(0)
        sb = pl.program_id(1)

        @pl.when(sb == 0)
        def _():
            kcar[...] = jnp.zeros_like(kcar)
            vcar[...] = jnp.zeros_like(vcar)

        if chained:
            dk_ref[...] = kcar[...] + pk_ref[...]
            dv_ref[...] = vcar[...] + pv_ref[...]
        else:
            dk_ref[...] = kcar[...]
            dv_ref[...] = vcar[...]

        @pl.when(sb < steps)
        def _():
            sl2 = (slope_ref[2 * j], slope_ref[2 * j + 1])
            bias_rest = _stack_heads(sl2, _band_base(B_MAX_DIST, r, False))
            bias_0 = jnp.where(sb > 0, bias_rest, _stack_heads(sl2, _band_base(B_MAX_DIST, r, True)))

            def stream(rho):
                for i in range(nq):
                    cur = _stream(rho, i, r)
                    k_prev = kc_ref[_stream(rho, i - 1, r), :] if i > 0 else kp_ref[_stream(rho, 0, r), :]
                    v_prev = vc_ref[_stream(rho, i - 1, r), :] if i > 0 else vp_ref[_stream(rho, 0, r), :]
                    kb = jnp.concatenate([k_prev, kc_ref[cur, :]], axis=0).astype(BF16)
                    vb = jnp.concatenate([v_prev, vc_ref[cur, :]], axis=0).astype(BF16)
                    dq2, dk2, dv2, _ = _pair_bwd(q_ref[cur, :], kb, vb, do_ref[cur, :], o_ref[cur, :], lse_ref[cur, :],
                                                 bias_rest if i > 0 else bias_0, None, (0, 1), None)
                    dq_ref[cur, :] = dq2 + pq_ref[cur, :] if chained else dq2
                    if i == 0:
                        last = _stream(rho, nq - 1, r)
                        dk_ref[last, :] += dk2[:BLK]
                        dv_ref[last, :] += dv2[:BLK]
                    else:
                        kcar[_stream(rho, i - 1, r), :] += dk2[:BLK]
                        vcar[_stream(rho, i - 1, r), :] += dv2[:BLK]
                    kcar[cur, :] = dk2[BLK:]
                    vcar[cur, :] = dv2[BLK:]

            _for_streams(r, stream, side_by_side=8)

    cur_step = lambda sb: jnp.minimum(sb, steps - 1)
    before = lambda sb: jnp.maximum(cur_step(sb) * nq - 1, 0)
    out_prev = lambda sb: jnp.maximum(sb - 1, 0)
    tile = lambda col: pl.BlockSpec((rows, 128), lambda j, sb: (cur_step(sb), col + j))
    edge = lambda col: pl.BlockSpec((BLK * r, 128), lambda j, sb: (before(sb), col + j))
    late = pl.BlockSpec((rows, 128), lambda j, sb: (out_prev(sb), j))
    grads = [tile(0), late, late]
    return pl.pallas_call(
        body, name=f"attn_b_bwd_r{r}", grid=(NH // 2, steps + 1),
        in_specs=[SMEM, tile(qc), edge(kc), tile(kc), edge(vc), tile(vc), tile(0), tile(0), tile(0)]
        + (grads if chained else []),
        out_specs=grads,
        out_shape=[jax.ShapeDtypeStruct((s, 512), F32)] * 3,
        scratch_shapes=[pltpu.VMEM((rows, 128), F32), pltpu.VMEM((rows, 128), F32)],
        compiler_params=_cp(("parallel", "arbitrary")),
    )(slopes, proj, proj, proj, proj, proj, d_o, o, lse, *(so_far if chained else ()))


def _row(v):
    return v.reshape(1, -1)


def _layer_norm_stats(z):
    mu = jnp.mean(z, axis=-1, keepdims=True)
    zc = z - mu
    var = jnp.mean(zc * zc, axis=-1, keepdims=True)
    rstd = lax.rsqrt(var + LN_EPS)
    return zc * rstd, rstd


def _layer_norm_bwd(dh, zh, rstd, g):
    dzh = dh * g
    return rstd * (dzh - jnp.mean(dzh, axis=-1, keepdims=True) - zh * jnp.mean(dzh * zh, axis=-1, keepdims=True))


def _rms(o):
    return lax.rsqrt(jnp.mean(o * o, axis=-1, keepdims=True) + RMS_EPS)


def _mix_ln1(x, o_a, o_b, lse_b, norm_a_g, norm_b_g, w_o, ln1_g, ln1_b, tm=256):
    s = x.shape[0]

    def body(x_ref, oa_ref, ob1, ob2, ob3, l1, l2, l3, ga_ref, gb_ref, wo_ref, g_ref, b_ref,
             obm_ref, lse_ref, cat_ref, z1_ref, h1_ref, h1b_ref):
        la, lb, lc = l1[...], l2[...], l3[...]
        m = jnp.maximum(jnp.maximum(la, lb), lc)
        ea, eb, ec = jnp.exp(la - m), jnp.exp(lb - m), jnp.exp(lc - m)
        den = ea + eb + ec
        obm = (ea / den) * ob1[...] + (eb / den) * ob2[...] + (ec / den) * ob3[...]
        obm_ref[...] = obm
        lse_ref[...] = m + jnp.log(den)
        oa = oa_ref[...]
        na = oa * _rms(oa) * ga_ref[...]
        nb_ = obm * _rms(obm) * gb_ref[...]
        cat = jnp.concatenate([na, nb_], axis=1).astype(BF16)
        cat_ref[...] = cat
        z1 = ALPHA * x_ref[...] + _nn(cat, wo_ref[...])
        z1_ref[...] = z1
        zh, _ = _layer_norm_stats(z1)
        h1 = zh * g_ref[...] + b_ref[...]
        h1_ref[...] = h1
        h1b_ref[...] = h1.astype(BF16)

    t512 = pl.BlockSpec((tm, 512), lambda i: (i, 0))
    td = pl.BlockSpec((tm, D), lambda i: (i, 0))
    return pl.pallas_call(
        body, name="mix_ln1", grid=(s // tm,),
        in_specs=[td] + [t512] * 7 + [_const((1, 512))] * 2 + [_resident((D, D))] + [_const((1, D))] * 2,
        out_specs=[t512, t512, td, td, td, td],
        out_shape=[jax.ShapeDtypeStruct((s, 512), F32), jax.ShapeDtypeStruct((s, 512), F32),
                   jax.ShapeDtypeStruct((s, D), BF16), jax.ShapeDtypeStruct((s, D), F32),
                   jax.ShapeDtypeStruct((s, D), F32), jax.ShapeDtypeStruct((s, D), BF16)],
        compiler_params=_cp(("parallel",)),
    )(x, o_a, *o_b, *lse_b, _row(norm_a_g), _row(norm_b_g), w_o, _row(ln1_g), _row(ln1_b))


def _gelu_and_grad(x):
    c = math.sqrt(2.0 / math.pi)
    x2 = x * x
    cx = c * x
    t = jnp.tanh(cx * (1.0 + 0.044715 * x2))
    q = 1.0 + t
    g = (0.5 * x) * q
    dg = 0.5 * q + ((0.5 * cx) * (1.0 - t * t)) * (1.0 + (3.0 * 0.044715) * x2)
    return g, dg


CONV_CHUNK = 64


def _shift_down(u, before):
    n = u.shape[0]
    ext = jnp.concatenate([before, u], axis=0)
    return pltpu.roll(ext, 1, 0)[8:], pltpu.roll(ext, 2, 0)[8:]


def _shift_up(u, after):
    n = u.shape[0]
    ext = jnp.concatenate([u, after], axis=0)
    return pltpu.roll(ext, n + 7, 0)[:n], pltpu.roll(ext, n + 6, 0)[:n]


def _up_conv_gelu(h1b, w_up, cwb, tm=512, tn=256):
    s = h1b.shape[0]
    n_i = s // tm
    n_t = (FF // tn) * n_i

    def body(h_ref, wg_ref, wv_ref, c_ref, up_ref, a_ref, g_ref, a1_ref, pend_a, pend_b, carry):
        t = pl.program_id(0)
        row_tile = jnp.maximum(t - 1, 0) % n_i
        w_refs = (wg_ref, wv_ref)

        @pl.when(t == 0)
        def _():
            pend_b[...] = jnp.zeros_like(pend_b)
            carry[...] = jnp.zeros_like(carry)

        def step(dst, src):
            def chunk(c, before):
                rows = pl.ds(c * CONV_CHUNK, CONV_CHUNK)
                u, last = [], []
                for half in (0, 1):
                    up = src[half, rows, :]
                    r1, r2 = _shift_down(up, before[half])
                    u.append(r2 * c_ref[0, half:half + 1, :] + r1 * c_ref[1, half:half + 1, :]
                             + up * c_ref[2, half:half + 1, :] + c_ref[3, half:half + 1, :])
                    last.append(up[CONV_CHUNK - 8:])
                g, dg = _gelu_and_grad(u[0])
                a_ref[rows, :] = (g * u[1]).astype(BF16)
                g_ref[rows, :] = g.astype(BF16)
                a1_ref[rows, :] = (u[1] * dg).astype(BF16)
                return tuple(last)

            edge = tuple(jnp.where(row_tile > 0, carry[half], 0.0) for half in (0, 1))
            n_c = tm // CONV_CHUNK
            n_k = n_c // 2
            tk = D // n_k
            for half in (0, 1):
                up = None
                for kq in range(n_k):
                    ks = slice(kq * tk, (kq + 1) * tk)
                    part = _nn(h_ref[:, ks], w_refs[half][ks, :])
                    up = part if kq == 0 else up + part
                    edge = chunk(half * n_k + kq, edge)
                up_ref[half] = up.astype(BF16)
                dst[half] = up
            for half in (0, 1):
                carry[half] = edge[half]

        @pl.when(t % 2 == 0)
        def _():
            step(pend_a, pend_b)

        @pl.when(t % 2 == 1)
        def _():
            step(pend_b, pend_a)

    mm = lambda t: jnp.minimum(t, n_t - 1)
    ew = lambda t: jnp.maximum(t - 1, 0)
    out_tile = pl.BlockSpec((tm, tn), lambda t: (ew(t) % n_i, ew(t) // n_i))
    return pl.pallas_call(
        body, name="up_conv_gelu", grid=(n_t + 1,),
        in_specs=[pl.BlockSpec((tm, D), lambda t: (mm(t) % n_i, 0)),
                  pl.BlockSpec((D, tn), lambda t: (0, mm(t) // n_i)),
                  pl.BlockSpec((D, tn), lambda t: (0, FF // tn + mm(t) // n_i)),
                  pl.BlockSpec((4, 2, tn), lambda t: (0, 0, ew(t) // n_i))],
        out_specs=[pl.BlockSpec((2, tm, tn), lambda t: (0, mm(t) % n_i, mm(t) // n_i)), out_tile, out_tile, out_tile],
        out_shape=[jax.ShapeDtypeStruct((2, s, FF), BF16)] + [jax.ShapeDtypeStruct((s, FF), BF16)] * 3,
        scratch_shapes=[pltpu.VMEM((2, tm, tn), F32), pltpu.VMEM((2, tm, tn), F32), pltpu.VMEM((2, 8, tn), F32)],
        compiler_params=_cp(("arbitrary",)),
    )(h1b, w_up, w_up, cwb)


def _down_ln2_loss(a, w_down, h1, target, ln2_g, ln2_b, tm=256):
    s = a.shape[0]

    def body(a_ref, w_ref, h_ref, t_ref, g_ref, b_ref, dz_ref, dzb_ref, st_ref):
        @pl.when(pl.program_id(0) == 0)
        def _():
            st_ref[...] = jnp.zeros_like(st_ref)

        z2 = ALPHA * h_ref[...] + _nn(a_ref[...], w_ref[...])
        zh, rstd = _layer_norm_stats(z2)
        diff = zh * g_ref[...] + b_ref[...] - t_ref[...]
        part = 0.5 * jnp.sum(jnp.mean(diff * diff, axis=-1, keepdims=True), axis=0, keepdims=True)
        dy = diff * (1.0 / D)
        st_ref[0:1, :] += jnp.sum(dy * zh, axis=0, keepdims=True)
        st_ref[1:2, :] += jnp.sum(dy, axis=0, keepdims=True)
        st_ref[2:3, :] += jnp.broadcast_to(part, (1, D))
        dz = _layer_norm_bwd(dy, zh, rstd, g_ref[...])
        dz_ref[...] = dz
        dzb_ref[...] = dz.astype(BF16)

    td = pl.BlockSpec((tm, D), lambda i: (i, 0))
    return pl.pallas_call(
        body, name="down_ln2_loss", grid=(s // tm,),
        in_specs=[pl.BlockSpec((tm, FF), lambda i: (i, 0)), _resident((FF, D)), td, td, _const((1, D)), _const((1, D))],
        out_specs=[td, td, _const((8, D))],
        out_shape=[jax.ShapeDtypeStruct((s, D), F32), jax.ShapeDtypeStruct((s, D), BF16),
                   jax.ShapeDtypeStruct((8, D), F32)],
        compiler_params=_cp(("arbitrary",)),
    )(a, w_down, h1, target, _row(ln2_g), _row(ln2_b))


def _d_act(dz2b, w_down, tm=512):
    s = dz2b.shape[0]

    def body(dz_ref, w_ref, o_ref):
        o_ref[...] = _nt(dz_ref[...], w_ref[...])

    return pl.pallas_call(
        body, name="d_act", grid=(s // tm,),
        in_specs=[pl.BlockSpec((tm, D), lambda i: (i, 0)), _resident((FF, D))],
        out_specs=pl.BlockSpec((tm, FF), lambda i: (i, 0)),
        out_shape=jax.ShapeDtypeStruct((s, FF), F32),
        compiler_params=_cp(("parallel",)),
    )(dz2b, w_down)


def _conv_gelu_bwd(da, up, g, a1, cwb, tm=256, tn=FF // 2, chunk_rows=16):
    s = da.shape[0]
    n_i = s // tm
    n_c = tm // chunk_rows

    def body(da_ref, up_ref, g_ref, a1_ref, c_ref, dup_ref, dc_ref, carry):
        @pl.when(pl.program_id(1) == 0)
        def _():
            carry[...] = jnp.zeros_like(carry)
            dc_ref[...] = jnp.zeros_like(dc_ref)

        def fold(v):
            return jnp.sum(v.reshape(chunk_rows // 8, 8, v.shape[1]), axis=0)

        def chunk(cc, state):
            after, sums = state
            rows = pl.ds((n_c - 1 - cc) * chunk_rows, chunk_rows)
            da_c = da_ref[rows, :]
            dus = (da_c * a1_ref[rows, :].astype(F32), da_c * g_ref[rows, :].astype(F32))
            head, new_sums = [], []
            for half in (0, 1):
                du = dus[half]
                up = up_ref[half, rows, :].astype(F32)
                l1, l2 = _shift_up(du, after[half])
                dup = (du * c_ref[2, half:half + 1, :] + l1 * c_ref[1, half:half + 1, :]
                       + l2 * c_ref[0, half:half + 1, :])
                dup_ref[half, rows, :] = dup.astype(BF16)
                parts = (fold(l2 * up), fold(l1 * up), fold(du * up), fold(du))
                new_sums.append(parts if sums is None else tuple(a + b for a, b in zip(sums[half], parts)))
                head.append(du[:8])
            return tuple(head), new_sums

        state = ((carry[0], carry[1]), None)
        for cc in range(n_c):
            state = chunk(cc, state)
        head, sums = state
        for half in (0, 1):
            carry[half] = head[half]
            for k in range(4):
                dc_ref[k, half:half + 1, :] += jnp.sum(sums[half][k], axis=0, keepdims=True)

    rev = lambda ii: n_i - 1 - ii
    tile = pl.BlockSpec((tm, tn), lambda j, ii: (rev(ii), j))
    pair = pl.BlockSpec((2, tm, tn), lambda j, ii: (0, rev(ii), j))
    per_col = pl.BlockSpec((4, 2, tn), lambda j, ii: (0, 0, j))
    return pl.pallas_call(
        body, name="conv_gelu_bwd", grid=(FF // tn, n_i),
        in_specs=[tile, pair, tile, tile, per_col],
        out_specs=[pair, per_col],
        out_shape=[jax.ShapeDtypeStruct((2, s, FF), BF16), jax.ShapeDtypeStruct((4, 2, FF), F32)],
        scratch_shapes=[pltpu.VMEM((2, 8, tn), F32)],
        compiler_params=_cp(("parallel", "arbitrary")),
    )(da, up, g, a1, cwb)


def _dh1_ln1_bwd(dz2, dup, w_up, z1, ln1_g, tm=256):
    s = dz2.shape[0]

    def body(dz2_ref, dup_ref, w_ref, z1_ref, g_ref, dz1_ref, dz1b_ref, st_ref):
        @pl.when(pl.program_id(0) == 0)
        def _():
            st_ref[...] = jnp.zeros_like(st_ref)

        dh = ALPHA * dz2_ref[...] + _nt(dup_ref[0], w_ref[:, :FF]) + _nt(dup_ref[1], w_ref[:, FF:])
        zh, rstd = _layer_norm_stats(z1_ref[...])
        st_ref[0:1, :] += jnp.sum(dh * zh, axis=0, keepdims=True)
        st_ref[1:2, :] += jnp.sum(dh, axis=0, keepdims=True)
        dz = _layer_norm_bwd(dh, zh, rstd, g_ref[...])
        dz1_ref[...] = dz
        dz1b_ref[...] = dz.astype(BF16)

    td = pl.BlockSpec((tm, D), lambda i: (i, 0))
    return pl.pallas_call(
        body, name="dh1_ln1_bwd", grid=(s // tm,),
        in_specs=[td, pl.BlockSpec((2, tm, FF), lambda i: (0, i, 0)), _resident((D, 2 * FF)), td, _const((1, D))],
        out_specs=[td, td, _const((8, D))],
        out_shape=[jax.ShapeDtypeStruct((s, D), F32), jax.ShapeDtypeStruct((s, D), BF16),
                   jax.ShapeDtypeStruct((8, D), F32)],
        compiler_params=_cp(("arbitrary",)),
    )(dz2, dup, w_up, z1, _row(ln1_g))


def _dcat_rms_bwd(dz1b, w_o, o_a, o_b, norm_a_g, norm_b_g, tm=256):
    s = dz1b.shape[0]

    def body(dz_ref, w_ref, oa_ref, ob_ref, ga_ref, gb_ref, da_ref, db_ref, st_ref):
        @pl.when(pl.program_id(0) == 0)
        def _():
            st_ref[...] = jnp.zeros_like(st_ref)

        dcat = _nt(dz_ref[...], w_ref[...])
        for k, (o_ref, g_ref, d_ref) in enumerate(((oa_ref, ga_ref, da_ref), (ob_ref, gb_ref, db_ref))):
            o = o_ref[...]
            dn = dcat[:, 512 * k:512 * (k + 1)]
            rr = _rms(o)
            oh = o * rr
            st_ref[k:k + 1, :] += jnp.sum(dn * oh, axis=0, keepdims=True)
            doh = dn * g_ref[...]
            d_ref[...] = rr * (doh - oh * jnp.mean(doh * oh, axis=-1, keepdims=True))

    t512 = pl.BlockSpec((tm, 512), lambda i: (i, 0))
    return pl.pallas_call(
        body, name="dcat_rms_bwd", grid=(s // tm,),
        in_specs=[pl.BlockSpec((tm, D), lambda i: (i, 0)), _resident((D, D)), t512, t512,
                  _const((1, 512)), _const((1, 512))],
        out_specs=[t512, t512, _const((8, 512))],
        out_shape=[jax.ShapeDtypeStruct((s, 512), F32), jax.ShapeDtypeStruct((s, 512), F32),
                   jax.ShapeDtypeStruct((8, 512), F32)],
        compiler_params=_cp(("arbitrary",)),
    )(dz1b, w_o, o_a, o_b, _row(norm_a_g), _row(norm_b_g))


def _dproj_combine(dqa, dka, dva, dqkv_b, tm=256):
    s = dqa.shape[0]

    def body(qa, ka, va, qb, kb, vb, o_ref):
        o_ref[:, 0:512] = qa[...].astype(BF16)
        o_ref[:, 512:640] = ka[...].astype(BF16)
        o_ref[:, 640:768] = va[...].astype(BF16)
        o_ref[:, 768:1280] = qb[...].astype(BF16)
        o_ref[:, 1280:1792] = kb[...].astype(BF16)
        o_ref[:, 1792:2304] = vb[...].astype(BF16)

    t512 = pl.BlockSpec((tm, 512), lambda i: (i, 0))
    t128 = pl.BlockSpec((tm, 128), lambda i: (i, 0))
    return pl.pallas_call(
        body, name="dproj_combine", grid=(s // tm,),
        in_specs=[t512, t128, t128] + [t512] * 3,
        out_specs=pl.BlockSpec((tm, WIN), lambda i: (i, 0)),
        out_shape=jax.ShapeDtypeStruct((s, WIN), BF16),
        compiler_params=_cp(("parallel",)),
    )(dqa, dka, dva, *dqkv_b)


def _grad_x(dz1, dproj, w_in_t, zero, tm=256):
    s = dz1.shape[0]

    def body(dz_ref, dp_ref, w_ref, z_ref, o_ref):
        o_ref[...] = ALPHA * dz_ref[...] + _nn(dp_ref[...], w_ref[...]) + z_ref[0:1, 0:1]

    td = pl.BlockSpec((tm, D), lambda i: (i, 0))
    return pl.pallas_call(
        body, name="grad_x", grid=(s // tm,),
        in_specs=[td, pl.BlockSpec((tm, WIN), lambda i: (i, 0)), _resident((WIN, D)), _const((8, 128))],
        out_specs=td, out_shape=jax.ShapeDtypeStruct((s, D), F32),
        compiler_params=_cp(("parallel",)),
    )(dz1, dproj, w_in_t, zero)


def _place():
    return lax.axis_index("x"), lax.axis_index("y"), lax.axis_index("c")


def _other_chips(x, y):
    return [(1 - x, y), (x, 1 - y), (1 - x, 1 - y)]


def _hbm(a):
    return pltpu.with_memory_space_constraint(a, pltpu.HBM)


def _gather_w_in(shard, conv_w):
    rows_k = shard.shape[0]
    half = rows_k // 2

    def body(src, conv_src, out, conv_out, send_sems, recv_sems):
        x, y, c = _place()
        b = 2 * x + y
        sibling = (x, y, 1 - c)
        chips = _other_chips(x, y)

        def copy(idx, chip_b, core, to, first_hop=False):
            rows = out.at[pl.ds(pl.multiple_of(chip_b * rows_k + core * half, 16), half)]
            s_ref = src.at[pl.ds(pl.multiple_of(core * half, 16), half)] if first_hop else rows
            return pltpu.make_async_remote_copy(src_ref=s_ref, dst_ref=rows, send_sem=send_sems.at[idx],
                                                recv_sem=recv_sems.at[idx], device_id=to, device_id_type=MESH)

        def own_copy():
            return pltpu.make_async_remote_copy(
                src_ref=src, dst_ref=out.at[pl.ds(pl.multiple_of(b * rows_k, 16), rows_k)], send_sem=send_sems.at[6],
                recv_sem=recv_sems.at[6], device_id=sibling, device_id_type=MESH)

        def conv_copy(idx, chip_b, to):
            return pltpu.make_async_remote_copy(src_ref=conv_src, dst_ref=conv_out.at[chip_b],
                                                send_sem=send_sems.at[7 + idx], recv_sem=recv_sems.at[7 + idx],
                                                device_id=to, device_id_type=MESH)

        started = [own_copy(), conv_copy(3, b, sibling)]
        for jn, chip in enumerate(chips):
            started += [copy(jn, b, c, (chip[0], chip[1], c), first_hop=True), conv_copy(jn, b, (chip[0], chip[1], c))]
        for cp in started:
            cp.start()
        for jn, chip in enumerate(chips):
            cb = 2 * chip[0] + chip[1]
            copy(jn, cb, c, (chip[0], chip[1], c)).wait_recv()
            cp = copy(3 + jn, cb, c, sibling)
            cp.start()
            started.append(cp)
        for jn, chip in enumerate(chips):
            cb = 2 * chip[0] + chip[1]
            copy(3 + jn, cb, 1 - c, sibling).wait_recv()
            conv_copy(jn, cb, (chip[0], chip[1], c)).wait_recv()
        own_copy().wait_recv()
        conv_copy(3, b, sibling).wait_recv()
        for cp in started:
            cp.wait_send()

    return pl.pallas_call(
        body, name="gather_w_in",
        in_specs=[ANY, ANY], out_specs=[ANY, ANY],
        out_shape=[jax.ShapeDtypeStruct((N_CHIPS * rows_k, D), BF16), jax.ShapeDtypeStruct((N_CHIPS,) + conv_w.shape, F32)],
        scratch_shapes=[pltpu.SemaphoreType.DMA((11,)), pltpu.SemaphoreType.DMA((11,))],
        compiler_params=pltpu.CompilerParams(has_side_effects=True),
    )(shard, conv_w)


def _weight_copies(shard, land, send_sems, recv_sems, arrivals):
    x, y, c = _place()
    n_rows, n_cols = shard.shape
    peers = [(px, py, c) for px, py in _other_chips(x, y)] + [(x, y, 1 - c)]
    cps = []
    for jn, peer in enumerate(peers):
        at = 2 * peer[0] + peer[1] if arrivals else 2 * x + y
        if land.shape[1] == n_cols:
            dst = land.at[pl.ds(pl.multiple_of(at * n_rows, 16), n_rows)]
        else:
            dst = land.at[:, pl.ds(pl.multiple_of(at * n_cols, 128), n_cols)]
        cps.append(pltpu.make_async_remote_copy(src_ref=shard, dst_ref=dst, send_sem=send_sems.at[jn],
                                                recv_sem=recv_sems.at[jn], device_id=peer, device_id_type=MESH))
    return cps


def _weights_start(shards, after):
    n = len(shards)
    lands = [lax.empty((N_CHIPS * sh.shape[0], D) if sh.shape[1] == D else (D, N_CHIPS * sh.shape[1]), BF16)
             for sh in shards]

    def body(*refs):
        src, land = refs[:n], refs[n:2 * n]
        send_sems, recv_sems = refs[2 * n + 1:3 * n + 1], refs[3 * n + 1:4 * n + 1]
        for k in range(n):
            for send in _weight_copies(src[k], land[k], send_sems[k], recv_sems[k], False):
                send.start()
        refs[-1][...] = jnp.zeros_like(refs[-1])

    res = pl.pallas_call(
        body, name="weights_start",
        in_specs=[HBM] * (2 * n) + [ANY], out_specs=[SEM] * (2 * n) + [HBM] * (2 * n) + [VMEM],
        out_shape=[pltpu.SemaphoreType.DMA((4,))] * (2 * n)
        + [pltpu.HBM(a.shape, a.dtype) for a in (*shards, *lands)] + [jax.ShapeDtypeStruct((8, 128), F32)],
        input_output_aliases={i: i + 2 * n for i in range(2 * n)},
        compiler_params=pltpu.CompilerParams(has_side_effects=DATAFLOW),
    )(*[_hbm(a) for a in (*shards, *lands)], after)
    return [(res[k], res[n + k], res[2 * n + k], res[3 * n + k]) for k in range(n)], res[-1]


def _weights_wait(started, after, name):
    send_sems, recv_sems, shard, land = started

    def body(s_ref, l_ref, send_ref, recv_ref, after_ref, s_out, l_out):
        for cp in _weight_copies(s_ref, l_ref, send_ref, recv_ref, True):
            cp.wait_send()
            cp.wait_recv()

    return pl.pallas_call(
        body, name=name,
        in_specs=[HBM, HBM, SEM, SEM, ANY], out_specs=[HBM, HBM],
        out_shape=[pltpu.HBM(shard.shape, shard.dtype), pltpu.HBM(land.shape, land.dtype)],
        input_output_aliases={0: 0, 1: 1},
        compiler_params=pltpu.CompilerParams(has_side_effects=DATAFLOW),
    )(shard, land, send_sems, recv_sems, after)[1]


def _grad_copies(g_ref, land_ref, send_sems, recv_sems):
    x, y, c = _place()
    cps = []
    for d in range(1, 8):
        px, py, pc = x ^ (d >> 2), y ^ ((d >> 1) & 1), c ^ (d & 1)
        cps.append(pltpu.make_async_remote_copy(
            src_ref=g_ref.at[2 * px + py, pc], dst_ref=land_ref.at[d - 1], send_sem=send_sems.at[d - 1],
            recv_sem=recv_sems.at[d - 1], device_id=(px, py, pc), device_id_type=MESH))
    return cps


def _grads_start(grads_b, name):
    n = len(grads_b)
    lands = [lax.empty((7, g.shape[2], D), BF16) for g in grads_b]

    def body(*refs):
        g, land = refs[:n], refs[n:2 * n]
        send_sems, recv_sems = refs[2 * n:3 * n], refs[3 * n:4 * n]
        for k in range(n):
            for cp in _grad_copies(g[k], land[k], send_sems[k], recv_sems[k]):
                cp.start()
        refs[-1][...] = jnp.zeros_like(refs[-1])

    res = pl.pallas_call(
        body, name=name,
        in_specs=[HBM] * (2 * n), out_specs=[SEM] * (2 * n) + [HBM] * (2 * n) + [VMEM],
        out_shape=[pltpu.SemaphoreType.DMA((7,))] * (2 * n)
        + [pltpu.HBM(a.shape, a.dtype) for a in (*grads_b, *lands)] + [jax.ShapeDtypeStruct((8, 128), F32)],
        input_output_aliases={i: i + 2 * n for i in range(2 * n)},
        compiler_params=pltpu.CompilerParams(has_side_effects=DATAFLOW),
    )(*[_hbm(a) for a in (*grads_b, *lands)])
    return [(res[k], res[n + k], res[2 * n + k], res[3 * n + k]) for k in range(n)], res[-1]


def _grads_wait(started, after, name):
    n = len(started)

    def body(*refs):
        g, land = refs[:n], refs[n:2 * n]
        send_sems, recv_sems = refs[2 * n:3 * n], refs[3 * n:4 * n]
        for k in range(n):
            for cp in _grad_copies(g[k], land[k], send_sems[k], recv_sems[k]):
                cp.wait_send()
                cp.wait_recv()

    gs = [st[2] for st in started]
    lands = [st[3] for st in started]
    res = pl.pallas_call(
        body, name=name,
        in_specs=[HBM] * (2 * n) + [SEM] * (2 * n) + [ANY], out_specs=[HBM] * (2 * n),
        out_shape=[pltpu.HBM(a.shape, a.dtype) for a in (*gs, *lands)],
        input_output_aliases={i: i for i in range(2 * n)},
        compiler_params=pltpu.CompilerParams(has_side_effects=DATAFLOW),
    )(*gs, *lands, *[st[0] for st in started], *[st[1] for st in started], after)
    return res[n:]


def _sum_partials(grad4, got, cb, name, tr):
    h = grad4.shape[2]
    per_half = h // tr

    def body(cb_ref, g_ref, o_ref, out_ref):
        acc = g_ref[...]
        for j in range(7):
            acc = acc + o_ref[j].astype(F32)
        out_ref[...] = acc

    return pl.pallas_call(
        body, name=name,
        grid_spec=pltpu.PrefetchScalarGridSpec(
            num_scalar_prefetch=1, grid=(per_half,),
            in_specs=[pl.BlockSpec((None, None, tr, D), lambda i, cb_ref: (cb_ref[1], cb_ref[0], i, 0)),
                      pl.BlockSpec((7, tr, D), lambda i, cb_ref: (0, i, 0))],
            out_specs=pl.BlockSpec((tr, D), lambda i, cb_ref: (cb_ref[0] * per_half + i, 0))),
        out_shape=jax.ShapeDtypeStruct((2 * h, D), F32),
        compiler_params=_cp(("arbitrary",)),
    )(cb, grad4, got)


def _swap_halves(shards, name):
    n = len(shards)

    def body(*refs):
        out, send_sems, recv_sems = refs[n:2 * n], refs[2 * n], refs[2 * n + 1]
        x, y, c = _place()
        cps = []
        for k in range(n):
            h = shards[k].shape[0] // 2
            mine = out[k].at[pl.ds(pl.multiple_of(c * h, 8), h)]
            cp = pltpu.make_async_remote_copy(src_ref=mine, dst_ref=mine, send_sem=send_sems.at[k],
                                              recv_sem=recv_sems.at[k], device_id=(x, y, 1 - c), device_id_type=MESH)
            cp.start()
            cps.append(cp)
        for cp in cps:
            cp.wait()

    return pl.pallas_call(
        body, name=name,
        in_specs=[ANY] * n, out_specs=[ANY] * n,
        out_shape=[jax.ShapeDtypeStruct(sh.shape, F32) for sh in shards],
        input_output_aliases={k: k for k in range(n)},
        scratch_shapes=[pltpu.SemaphoreType.DMA((n,)), pltpu.SemaphoreType.DMA((n,))],
        compiler_params=pltpu.CompilerParams(has_side_effects=True),
    )(*shards)


def _share_halves(shards, small):
    n = len(shards)
    rows = small.shape[0]

    def body(*refs):
        small_ref = refs[n]
        out, total_ref = refs[n + 1:2 * n + 1], refs[2 * n + 1]
        all_ref, send_sems, recv_sems, ssend, srecv = refs[2 * n + 2:]
        x, y, c = _place()
        me = 4 * x + 2 * y + c
        cps = []
        for k in range(n):
            h = shards[k].shape[0] // 2
            mine = out[k].at[pl.ds(pl.multiple_of(c * h, 8), h)]
            cp = pltpu.make_async_remote_copy(src_ref=mine, dst_ref=mine, send_sem=send_sems.at[k],
                                              recv_sem=recv_sems.at[k], device_id=(x, y, 1 - c), device_id_type=MESH)
            cp.start()
            cps.append(cp)
        all_ref[me] = small_ref[...]
        peers = []
        for d in range(1, 8):
            px, py, pc = x ^ (d >> 2), y ^ ((d >> 1) & 1), c ^ (d & 1)
            cp = pltpu.make_async_remote_copy(src_ref=small_ref, dst_ref=all_ref.at[me],
                                              send_sem=ssend.at[d - 1], recv_sem=srecv.at[d - 1],
                                              device_id=(px, py, pc), device_id_type=MESH)
            cp.start()
            peers.append(cp)
        for cp in peers:
            cp.wait()
        acc = all_ref[0]
        for d in range(1, 8):
            acc = acc + all_ref[d]
        total_ref[...] = acc
        for cp in cps:
            cp.wait()

    return pl.pallas_call(
        body, name="share_halves",
        in_specs=[ANY] * n + [VMEM], out_specs=[ANY] * n + [VMEM],
        out_shape=[jax.ShapeDtypeStruct(sh.shape, F32) for sh in shards] + [jax.ShapeDtypeStruct((rows, D), F32)],
        input_output_aliases={k: k for k in range(n)},
        scratch_shapes=[pltpu.VMEM((8, rows, D), F32), pltpu.SemaphoreType.DMA((n,)), pltpu.SemaphoreType.DMA((n,)),
                        pltpu.SemaphoreType.DMA((7,)), pltpu.SemaphoreType.DMA((7,))],
        compiler_params=pltpu.CompilerParams(has_side_effects=True),
    )(*shards, small)


def _adamw(w, g, m, v, name, tr):
    rows, cols = w.shape

    def body(w_ref, g_ref, m_ref, v_ref, d_ref, nm_ref, nv_ref):
        g_ = g_ref[...]
        nm = ADAM_B1 * m_ref[...] + (1.0 - ADAM_B1) * g_
        nv = ADAM_B2 * v_ref[...] + (1.0 - ADAM_B2) * (g_ * g_)
        m_hat = nm / (1.0 - ADAM_B1 ** ADAM_STEP)
        v_hat = nv / (1.0 - ADAM_B2 ** ADAM_STEP)
        d_ref[...] = -ADAM_LR * (m_hat / (jnp.sqrt(v_hat) + ADAM_EPS) + ADAM_WD * w_ref[...])
        nm_ref[...] = nm
        nv_ref[...] = nv

    spec = pl.BlockSpec((tr, cols), lambda i: (i, 0))
    return pl.pallas_call(
        body, name=name, grid=(rows // tr,),
        in_specs=[spec] * 4, out_specs=[spec] * 3,
        out_shape=[jax.ShapeDtypeStruct((rows, cols), F32)] * 3,
        compiler_params=_cp(("parallel",)),
    )(w, g, m, v)


def _local_step(x, target, w_in_t, late_weights, norm_a_g, norm_b_g, sinks_a, ln1_g, ln1_b,
                conv_w, conv_b, ln2_g, ln2_b, slopes, on_grad):
    cwb = jnp.concatenate([conv_w, conv_b[None]], axis=0).reshape(4, 2, FF)

    proj, xb = _proj(x, w_in_t, "proj")
    o_a, lse_a = _attn_a_fwd(proj, sinks_a)
    fwd_b = [_attn_b_fwd(proj, slopes, r) for r in B_DILATIONS]
    w_o = late_weights(1, fwd_b[-1][1])
    o_b, lse_b, cat, z1, h1, h1b = _mix_ln1(x, o_a, [f[0] for f in fwd_b], [f[1] for f in fwd_b],
                                           norm_a_g, norm_b_g, w_o, ln1_g, ln1_b)
    w_up = late_weights(2, h1b)
    up, a, gate, a1 = _up_conv_gelu(h1b, w_up, cwb)
    w_down = late_weights(3, a)
    dz2, dz2b, st2 = _down_ln2_loss(a, w_down, h1, target, ln2_g, ln2_b)

    on_grad(3, *_grad_w(a, dz2b, "grad_w_down", tm=FF // 2))
    dup, dconv = _conv_gelu_bwd(_d_act(dz2b, w_down), up, gate, a1, cwb)
    on_grad(2, *_grad_w(dup, h1b, "grad_w_up", tm=FF // 2, lhs_halves=True))
    dz1, dz1b, st1 = _dh1_ln1_bwd(dz2, dup, w_up, z1, ln1_g)
    tok = on_grad(1, *_grad_w(cat, dz1b, "grad_w_o", tm=512))
    d_oa, d_ob, st_n = _dcat_rms_bwd(dz1b, w_o, o_a, o_b, norm_a_g + tok[0, 0], norm_b_g)
    dqa, dka, dva, dsink = _attn_a_bwd(proj, sinks_a, d_oa, o_a, lse_a)
    bwd_b = None
    for r in B_DILATIONS:
        bwd_b = _attn_b_bwd(proj, slopes, d_ob, o_b, lse_b, r, bwd_b)
    dproj = _dproj_combine(dqa, dka, dva, bwd_b)
    tok = on_grad(0, *_grad_w(dproj, xb, "grad_w_in", tm=WA))
    gx = _grad_x(dz1, dproj, w_in_t, tok)

    dconv = dconv.reshape(4, 2 * FF)
    small = dict(loss=st2[2, 0:1], norm_a_g=st_n[0], norm_b_g=st_n[1], sinks_a=dsink[:, 0],
                 ln1_g=st1[0], ln1_b=st1[1], conv_w=dconv[0:3].reshape(-1), conv_b=dconv[3],
                 ln2_g=st2[0], ln2_b=st2[1])
    return gx, small


SMALL_ORDER = ("loss", "norm_a_g", "norm_b_g", "sinks_a", "ln1_g", "ln1_b", "conv_b", "ln2_g", "ln2_b", "conv_w")
SMALL_SIZES = dict(loss=1, norm_a_g=512, norm_b_g=512, sinks_a=8, ln1_g=D, ln1_b=D, conv_b=2 * FF, ln2_g=D, ln2_b=D,
                   conv_w=3 * 2 * FF)


def _pack(parts, rows):
    flat = jnp.concatenate([parts[k].reshape(-1).astype(F32) for k in parts])
    return jnp.pad(flat, (0, rows * D - flat.shape[0])).reshape(rows, D)


def _unpack(buf, names, sizes):
    flat = buf.reshape(-1)
    out, at = {}, 0
    for k in names:
        out[k] = flat[at:at + sizes[k]]
        at += sizes[k]
    return out


def kernel(x, w_in, norm_a_g, norm_b_g, sinks_a, w_o, ln1_g, ln1_b, w_up, conv_w, conv_b, w_down, ln2_g, ln2_b, loss_target, m_w_in, m_norm_a_g, m_norm_b_g, m_sinks_a, m_w_o, m_ln1_g, m_ln1_b, m_w_up, m_conv_w, m_conv_b, m_w_down, m_ln2_g, m_ln2_b, v_w_in, v_norm_a_g, v_norm_b_g, v_sinks_a, v_w_o, v_ln1_g, v_ln1_b, v_w_up, v_conv_w, v_conv_b, v_w_down, v_ln2_g, v_ln2_b):
    xi, yi, ci = _place()
    chip = (2 * xi + yi).astype(I32)
    core = ci.astype(I32)

    w_in_rows, m_w_in_rows, v_w_in_rows = w_in.T, m_w_in.T, v_w_in.T
    shards = (w_in_rows.astype(BF16), w_o.astype(BF16), w_up.astype(BF16), w_down.astype(BF16))
    w_in_t, conv_w4 = _gather_w_in(shards[0], conv_w)
    conv_w_f = conv_w4.transpose(1, 0, 2).reshape(3, 2 * FF)
    w_started, w_tok = _weights_start(shards[1:], conv_w4)
    slopes = jnp.asarray(SLOPES, F32) + w_tok[0, 0]

    halves_rows = [r // 2 for r in SHARD_ROWS]
    grads4, grads_b4, started = [None] * 4, [None] * 4, [None] * 4

    def on_grad(k, g, g_b):
        grads4[k] = g.reshape(N_CHIPS, 2, halves_rows[k], D)
        grads_b4[k] = g_b.reshape(N_CHIPS, 2, halves_rows[k], D)
        if k > 1:
            return None
        group = (1, 2, 3) if k == 1 else (0,)
        sts, tok = _grads_start([grads_b4[i] for i in group], f"grads_start_{k}")
        for i, st in zip(group, sts):
            started[i] = st
        return tok

    gx, small = _local_step(
        x[0], loss_target[0], w_in_t, lambda k, after: _weights_wait(w_started[k - 1], after, f"weights_wait_{k}"),
        norm_a_g, norm_b_g, sinks_a, ln1_g, ln1_b, conv_w_f, conv_b, ln2_g, ln2_b, slopes, on_grad)

    tiles = (96, 128, 352, 176)
    core_chip = jnp.stack([core, chip])
    got = _grads_wait(started[1:], gx, "grads_wait_1")
    halves = [_sum_partials(grads4[k], got[k - 1], core_chip, f"sum_partials_{k}", tiles[k]) for k in (1, 2, 3)]
    g_w_o, g_w_up_rows, g_w_down = _swap_halves(halves, "swap_halves")
    g_w_up = g_w_up_rows.T
    delta, new_m, new_v = {}, {}, {}
    for k, g, tr in (("w_o", g_w_o, 128), ("w_up", g_w_up, 256), ("w_down", g_w_down, 176)):
        delta[k], new_m[k], new_v[k] = _adamw(dict(w_o=w_o, w_up=w_up, w_down=w_down)[k], g,
                                              dict(w_o=m_w_o, w_up=m_w_up, w_down=m_w_down)[k],
                                              dict(w_o=v_w_o, w_up=v_w_up, w_down=v_w_down)[k], f"adamw_{k}", tr)

    got = _grads_wait(started[:1], delta["w_up"], "grads_wait_0")
    half_in = _sum_partials(grads4[0], got[0], core_chip, "sum_partials_0", tiles[0])
    small_rows = 32
    g_w_in_rows, totals = _share_halves([half_in], _pack({k: small[k] for k in SMALL_ORDER}, small_rows))
    tot = _unpack(totals, SMALL_ORDER, SMALL_SIZES)
    loss = tot["loss"][0]
    cols = 2 * FF // N_CHIPS
    g_conv_w = lax.dynamic_slice(tot["conv_w"].reshape(3, 2 * FF), (0, chip * cols), (3, cols))
    g_small = dict(norm_a_g=tot["norm_a_g"], norm_b_g=tot["norm_b_g"], sinks_a=tot["sinks_a"], ln1_g=tot["ln1_g"],
                   ln1_b=tot["ln1_b"], conv_w=g_conv_w, conv_b=tot["conv_b"], ln2_g=tot["ln2_g"], ln2_b=tot["ln2_b"])

    weights = dict(w_in=w_in, norm_a_g=norm_a_g, norm_b_g=norm_b_g, sinks_a=sinks_a, w_o=w_o, ln1_g=ln1_g, ln1_b=ln1_b,
                   w_up=w_up, conv_w=conv_w, conv_b=conv_b, w_down=w_down, ln2_g=ln2_g, ln2_b=ln2_b)
    ms = dict(w_in=m_w_in, norm_a_g=m_norm_a_g, norm_b_g=m_norm_b_g, sinks_a=m_sinks_a, w_o=m_w_o, ln1_g=m_ln1_g,
              ln1_b=m_ln1_b, w_up=m_w_up, conv_w=m_conv_w, conv_b=m_conv_b, w_down=m_w_down, ln2_g=m_ln2_g, ln2_b=m_ln2_b)
    vs = dict(w_in=v_w_in, norm_a_g=v_norm_a_g, norm_b_g=v_norm_b_g, sinks_a=v_sinks_a, w_o=v_w_o, ln1_g=v_ln1_g,
              ln1_b=v_ln1_b, w_up=v_w_up, conv_w=v_conv_w, conv_b=v_conv_b, w_down=v_w_down, ln2_g=v_ln2_g, ln2_b=v_ln2_b)
    order = list(weights)
    grad = dict(g_small, w_in=g_w_in_rows.T, w_o=g_w_o, w_up=g_w_up, w_down=g_w_down)

    delta["w_in"], new_m["w_in"], new_v["w_in"] = [
        a.T for a in _adamw(w_in_rows, g_w_in_rows, m_w_in_rows, v_w_in_rows, "adamw_w_in", 144)]
    small_names = [k for k in order if k not in delta]
    sizes = {k: weights[k].size for k in small_names}
    rows = 16
    packed = [_pack({k: src[k] for k in small_names}, rows) for src in (weights, grad, ms, vs)]
    for res, buf in zip((delta, new_m, new_v), _adamw(*packed, "adamw_small", rows)):
        for k, val in _unpack(buf, small_names, sizes).items():
            res[k] = val.reshape(weights[k].shape)

    return (loss, gx[None], *[grad[k] for k in order], *[delta[k] for k in order],
            *[new_m[k] for k in order], *[new_v[k] for k in order])
```

```python
import functools
import math

import jax
import jax.numpy as jnp
from jax import lax
from jax.experimental import pallas as pl
from jax.experimental.pallas import tpu as pltpu

F32, BF16, I32 = jnp.float32, jnp.bfloat16, jnp.int32

D = 1024
FF = 2816
HD = 64
NH = 8
WA, WB = 768, 1536
WIN = WA + WB
BLK = 128
ALPHA = 2.0 ** 0.25
LN_EPS, RMS_EPS = 1e-5, 1e-6
SCALE = 1.0 / math.sqrt(HD)
A_MAX_DIST, B_MAX_DIST = 127, 128
B_DILATIONS = (1, 4, 16)
SLOPES = tuple(2.0 ** (-(i + 1)) for i in range(NH))
SHARD_ROWS = (WIN // 4, D // 4, 2 * FF // 4, FF // 4)
N_CHIPS = 4
ADAM_LR, ADAM_B1, ADAM_B2, ADAM_EPS, ADAM_WD, ADAM_STEP = 0.001, 0.9, 0.999, 1e-08, 0.01, 10
MESH = pl.DeviceIdType.MESH
ANY = pl.BlockSpec(memory_space=pl.ANY)
SMEM = pl.BlockSpec(memory_space=pltpu.SMEM)
VMEM = pl.BlockSpec(memory_space=pltpu.VMEM)
HBM = pl.BlockSpec(memory_space=pltpu.HBM)
SEM = pl.BlockSpec(memory_space=pltpu.SEMAPHORE)
DATAFLOW = pltpu.SideEffectType.DATAFLOW_SIDE_EFFECTING


def _cp(sem, mb=48):
    return pltpu.CompilerParams(dimension_semantics=sem, vmem_limit_bytes=mb << 20)


def _nn(a, b):
    return lax.dot_general(a, b, (((1,), (0,)), ((), ())), preferred_element_type=F32)


def _nt(a, b):
    return lax.dot_general(a, b, (((1,), (1,)), ((), ())), preferred_element_type=F32)


def _tn(a, b):
    return lax.dot_general(a, b, (((0,), (0,)), ((), ())), preferred_element_type=F32)


def _resident(shape):
    n = len(shape)
    return pl.BlockSpec(shape, lambda *_: (0,) * n, pipeline_mode=pl.Buffered(1))


def _const(shape):
    n = len(shape)
    return pl.BlockSpec(shape, lambda *_: (0,) * n)


def _proj(x, w_t, name, tm=512):
    s = x.shape[0]
    n = w_t.shape[0]

    def body(x_ref, w_ref, o_ref, xb_ref):
        xb = x_ref[...].astype(BF16)
        xb_ref[...] = xb
        o_ref[...] = _nt(xb, w_ref[...])

    return pl.pallas_call(
        body, name=name, grid=(s // tm,),
        in_specs=[pl.BlockSpec((tm, D), lambda i: (i, 0)), _resident((n, D))],
        out_specs=[pl.BlockSpec((tm, n), lambda i: (i, 0)), pl.BlockSpec((tm, D), lambda i: (i, 0))],
        out_shape=[jax.ShapeDtypeStruct((s, n), F32), jax.ShapeDtypeStruct((s, D), BF16)],
        compiler_params=_cp(("parallel",)),
    )(x, w_t)


def _grad_w(lhs, rhs, name, tm, tk=512, lhs_halves=False):
    s = rhs.shape[0]
    if lhs_halves:
        per_half = lhs.shape[2] // tm
        n = 2 * lhs.shape[2]
        lhs_spec = pl.BlockSpec((None, tk, tm), lambda i, k: (i // per_half, k, i % per_half))
    else:
        n = lhs.shape[1]
        lhs_spec = pl.BlockSpec((tk, tm), lambda i, k: (k, i))
    nk = s // tk

    def body(l_ref, r_ref, o_ref, ob_ref):
        k = pl.program_id(1)

        @pl.when(k == 0)
        def _():
            o_ref[...] = jnp.zeros_like(o_ref)

        o_ref[...] += _tn(l_ref[...].astype(BF16), r_ref[...].astype(BF16))

        @pl.when(k == nk - 1)
        def _():
            ob_ref[...] = o_ref[...].astype(BF16)

    return pl.pallas_call(
        body, name=name, grid=(n // tm, nk),
        in_specs=[lhs_spec, pl.BlockSpec((tk, D), lambda i, k: (k, 0))],
        out_specs=[pl.BlockSpec((tm, D), lambda i, k: (i, 0))] * 2,
        out_shape=[jax.ShapeDtypeStruct((n, D), F32), jax.ShapeDtypeStruct((n, D), BF16)],
        compiler_params=_cp(("parallel", "arbitrary")),
    )(lhs, rhs)


def _band_base(max_dist, dist_unit, first):
    row = lax.broadcasted_iota(I32, (BLK, 2 * BLK), 0)
    col = lax.broadcasted_iota(I32, (BLK, 2 * BLK), 1)
    dist = BLK + row - col
    ok = (dist >= 0) & (dist <= max_dist)
    if first:
        ok = ok & (col >= BLK)
    return jnp.where(ok, dist.astype(F32) * (-float(dist_unit)), -jnp.inf)


def _half_mask(shape, e):
    lane = lax.broadcasted_iota(I32, shape, 1)
    return (lane < HD) if e == 0 else (lane >= HD)


def _to_half(x, e, g):
    if g != e:
        x = pltpu.roll(x, HD, 1)
    return jnp.where(_half_mask(x.shape, g), x, 0.0)


def _stack_heads(scalars, tile):
    return jnp.concatenate([scalars[0] * tile, scalars[1] * tile], axis=0)


def _pair_fwd(q2, kb, vb, base, slopes, kv_heads, sinks):
    lo = _half_mask((BLK, 2 * HD), 0)
    if sinks is not None:
        o2 = lse2 = None
        for e in (0, 1):
            g = kv_heads[e]
            qv = (_to_half(q2, e, g) * SCALE).astype(BF16)
            s = _nt(qv, kb) + slopes[e] * base
            m = jnp.maximum(jnp.max(s, axis=1, keepdims=True), sinks[e])
            p = jnp.exp(s - m)
            l = jnp.sum(p, axis=1, keepdims=True) + jnp.exp(sinks[e] - m)
            oh = _nn(p.astype(BF16), vb) / l
            if g != e:
                oh = pltpu.roll(oh, HD, 1)
            lse = jnp.broadcast_to(m + jnp.log(l), (BLK, 2 * HD))
            o2 = oh if e == 0 else jnp.where(lo, o2, oh)
            lse2 = lse if e == 0 else jnp.where(lo, lse2, lse)
        return o2, lse2
    qs = jnp.concatenate([_to_half(q2, e, kv_heads[e]) * SCALE for e in (0, 1)], axis=0).astype(BF16)
    s = _nt(qs, kb) + (base if slopes is None else _stack_heads(slopes, base))
    m = jnp.max(s, axis=1, keepdims=True)
    p = jnp.exp(s - m)
    l = jnp.sum(p, axis=1, keepdims=True)
    o = _nn(p.astype(BF16), vb) / l
    lse = m + jnp.log(l)
    halves = []
    for e in (0, 1):
        oh = o[e * BLK:(e + 1) * BLK]
        halves.append(pltpu.roll(oh, HD, 1) if kv_heads[e] != e else oh)
    o2 = jnp.where(lo, halves[0], halves[1])
    lse2 = jnp.where(lo, jnp.broadcast_to(lse[:BLK], (BLK, 2 * HD)), jnp.broadcast_to(lse[BLK:], (BLK, 2 * HD)))
    return o2, lse2


def _pair_bwd(q2, kb, vb, do2, o2, lse2, base, slopes, kv_heads, sinks):
    lo = _half_mask((BLK, 2 * HD), 0)
    prod = do2 * o2
    lses, deltas = [], []
    for e in (0, 1):
        hq = _half_mask((BLK, 2 * HD), e)
        lses.append(jnp.max(jnp.where(hq, lse2, -jnp.inf), axis=1, keepdims=True))
        deltas.append(jnp.sum(jnp.where(hq, prod, 0.0), axis=1, keepdims=True))
    lse = jnp.concatenate(lses, axis=0)
    delta = jnp.concatenate(deltas, axis=0)
    qs = jnp.concatenate([_to_half(q2, e, kv_heads[e]) * SCALE for e in (0, 1)], axis=0).astype(BF16)
    dos = jnp.concatenate([_to_half(do2, e, kv_heads[e]) for e in (0, 1)], axis=0).astype(BF16)
    p = jnp.exp(_nt(qs, kb) + (base if slopes is None else _stack_heads(slopes, base)) - lse)
    ds = (p * (_nt(dos, vb) - delta)).astype(BF16)
    dq = _nn(ds, kb) * SCALE
    halves = []
    for e in (0, 1):
        dqh = dq[e * BLK:(e + 1) * BLK]
        halves.append(pltpu.roll(dqh, HD, 1) if kv_heads[e] != e else dqh)
    dq2 = jnp.where(lo, halves[0], halves[1])
    dk2 = _tn(ds, qs)
    dv2 = _tn(p.astype(BF16), dos)
    dsinks = []
    if sinks is not None:
        for e in (0, 1):
            dsinks.append(jnp.sum(-jnp.exp(sinks[e] - lses[e]) * deltas[e], axis=0, keepdims=True))
    return dq2, dk2, dv2, dsinks


A_BLOCKS_PER_STEP = 2
A_BLOCKS_PER_STEP_BWD = 1


def _attn_a_fwd(proj, sinks):
    s = proj.shape[0]
    nq = A_BLOCKS_PER_STEP
    rows = BLK * nq
    steps = s // rows

    def body(sink_ref, q_ref, kp_ref, kc_ref, vp_ref, vc_ref, o_ref, lse_ref):
        n = pl.program_id(0)
        base_rest = _band_base(A_MAX_DIST, 1, False)
        base_0 = jnp.where(n > 0, base_rest, _band_base(A_MAX_DIST, 1, True))
        for i in range(nq):
            cur = pl.ds(i * BLK, BLK)
            k_prev = kc_ref[pl.ds((i - 1) * BLK, BLK), :] if i > 0 else kp_ref[...]
            v_prev = vc_ref[pl.ds((i - 1) * BLK, BLK), :] if i > 0 else vp_ref[...]
            kb = jnp.concatenate([k_prev, kc_ref[cur, :]], axis=0).astype(BF16)
            vb = jnp.concatenate([v_prev, vc_ref[cur, :]], axis=0).astype(BF16)
            for j in range(NH // 2):
                g = j // 2
                o2, lse2 = _pair_fwd(q_ref[cur, 128 * j:128 * (j + 1)], kb, vb, base_rest if i > 0 else base_0,
                                     (SLOPES[2 * j], SLOPES[2 * j + 1]), (g, g), (sink_ref[2 * j], sink_ref[2 * j + 1]))
                o_ref[cur, 128 * j:128 * (j + 1)] = o2
                lse_ref[cur, 128 * j:128 * (j + 1)] = lse2

    before = lambda n: jnp.maximum(n * nq - 1, 0)
    return pl.pallas_call(
        body, name="attn_a_fwd", grid=(steps,),
        in_specs=[SMEM,
                  pl.BlockSpec((rows, 512), lambda n: (n, 0)),
                  pl.BlockSpec((BLK, 128), lambda n: (before(n), 4)), pl.BlockSpec((rows, 128), lambda n: (n, 4)),
                  pl.BlockSpec((BLK, 128), lambda n: (before(n), 5)), pl.BlockSpec((rows, 128), lambda n: (n, 5))],
        out_specs=[pl.BlockSpec((rows, 512), lambda n: (n, 0))] * 2,
        out_shape=[jax.ShapeDtypeStruct((s, 512), F32)] * 2,
        compiler_params=_cp(("parallel",)),
    )(sinks, proj, proj, proj, proj, proj)


def _attn_a_bwd(proj, sinks, d_o, o, lse):
    s = proj.shape[0]
    nq = A_BLOCKS_PER_STEP_BWD
    rows = BLK * nq
    steps = s // rows

    def body(sink_ref, q_ref, kp_ref, kc_ref, vp_ref, vc_ref, do_ref, o_ref, lse_ref,
             dq_ref, dk_ref, dv_ref, dsink_ref, kcar, vcar):
        n = pl.program_id(0)

        @pl.when(n == 0)
        def _():
            kcar[...] = jnp.zeros_like(kcar)
            vcar[...] = jnp.zeros_like(vcar)
            dsink_ref[...] = jnp.zeros_like(dsink_ref)

        dk_ref[...] = kcar[...]
        dv_ref[...] = vcar[...]

        @pl.when(n < steps)
        def _():
            base_rest = _band_base(A_MAX_DIST, 1, False)
            base_0 = jnp.where(n > 0, base_rest, _band_base(A_MAX_DIST, 1, True))
            for i in range(nq):
                cur = pl.ds(i * BLK, BLK)
                k_prev = kc_ref[pl.ds((i - 1) * BLK, BLK), :] if i > 0 else kp_ref[...]
                v_prev = vc_ref[pl.ds((i - 1) * BLK, BLK), :] if i > 0 else vp_ref[...]
                kb = jnp.concatenate([k_prev, kc_ref[cur, :]], axis=0).astype(BF16)
                vb = jnp.concatenate([v_prev, vc_ref[cur, :]], axis=0).astype(BF16)
                dk_win = dv_win = None
                for j in range(NH // 2):
                    g = j // 2
                    sl = slice(128 * j, 128 * (j + 1))
                    dq2, dk2, dv2, dsk = _pair_bwd(q_ref[cur, sl], kb, vb, do_ref[cur, sl], o_ref[cur, sl],
                                                   lse_ref[cur, sl], base_rest if i > 0 else base_0,
                                                   (SLOPES[2 * j], SLOPES[2 * j + 1]), (g, g),
                                                   (sink_ref[2 * j], sink_ref[2 * j + 1]))
                    dq_ref[cur, sl] = dq2
                    dk_win = dk2 if j == 0 else dk_win + dk2
                    dv_win = dv2 if j == 0 else dv_win + dv2
                    for e in (0, 1):
                        h = 2 * j + e
                        dsink_ref[h:h + 1, :] += jnp.broadcast_to(dsk[e], (1, 128))
                if i == 0:
                    last = pl.ds((nq - 1) * BLK, BLK)
                    dk_ref[last, :] += dk_win[:BLK]
                    dv_ref[last, :] += dv_win[:BLK]
                else:
                    kcar[pl.ds((i - 1) * BLK, BLK), :] += dk_win[:BLK]
                    vcar[pl.ds((i - 1) * BLK, BLK), :] += dv_win[:BLK]
                kcar[cur, :] = dk_win[BLK:]
                vcar[cur, :] = dv_win[BLK:]

    cur_step = lambda n: jnp.minimum(n, steps - 1)
    before = lambda n: jnp.maximum(cur_step(n) * nq - 1, 0)
    out_prev = lambda n: jnp.maximum(n - 1, 0)
    wide = pl.BlockSpec((rows, 512), lambda n: (cur_step(n), 0))
    return pl.pallas_call(
        body, name="attn_a_bwd", grid=(steps + 1,),
        in_specs=[SMEM, wide,
                  pl.BlockSpec((BLK, 128), lambda n: (before(n), 4)), pl.BlockSpec((rows, 128), lambda n: (cur_step(n), 4)),
                  pl.BlockSpec((BLK, 128), lambda n: (before(n), 5)), pl.BlockSpec((rows, 128), lambda n: (cur_step(n), 5)),
                  wide, wide, wide],
        out_specs=[wide,
                   pl.BlockSpec((rows, 128), lambda n: (out_prev(n), 0)),
                   pl.BlockSpec((rows, 128), lambda n: (out_prev(n), 0)),
                   pl.BlockSpec((NH, 128), lambda n: (0, 0))],
        out_shape=[jax.ShapeDtypeStruct((s, 512), F32), jax.ShapeDtypeStruct((s, 128), F32),
                   jax.ShapeDtypeStruct((s, 128), F32), jax.ShapeDtypeStruct((NH, 128), F32)],
        scratch_shapes=[pltpu.VMEM((rows, 128), F32), pltpu.VMEM((rows, 128), F32)],
        compiler_params=_cp(("arbitrary",)),
    )(sinks, proj, proj, proj, proj, proj, d_o, o, lse)


def _stream(rho, i, r):
    start = i * BLK * r + rho
    return pl.ds(start, BLK, stride=r) if r > 1 else pl.ds(start, BLK)


def _for_streams(r, fn, side_by_side=4):
    if r <= side_by_side:
        for rho in range(r):
            fn(rho)
    else:
        def group(it, carry):
            for u in range(side_by_side):
                fn(side_by_side * it + u)
            return carry

        lax.fori_loop(0, r // side_by_side, group, 0)


B_BLOCKS_PER_STEP = {1: 8, 4: 2, 16: 1}
B_BLOCKS_PER_STEP_FWD = {1: 8, 4: 2, 16: 1}


def _attn_b_fwd(proj, slopes, r):
    s = proj.shape[0]
    nq = B_BLOCKS_PER_STEP_FWD[r]
    rows = BLK * r * nq
    steps = s // rows
    qc, kc, vc = WA // 128, WA // 128 + 4, WA // 128 + 8

    def body(slope_ref, q_ref, kp_ref, kc_ref, vp_ref, vc_ref, o_ref, lse_ref):
        j = pl.program_id(0)
        sb = pl.program_id(1)
        sl2 = (slope_ref[2 * j], slope_ref[2 * j + 1])
        bias_rest = _stack_heads(sl2, _band_base(B_MAX_DIST, r, False))
        bias_0 = jnp.where(sb > 0, bias_rest, _stack_heads(sl2, _band_base(B_MAX_DIST, r, True)))

        def stream(rho):
            for i in range(nq):
                cur = _stream(rho, i, r)
                k_prev = kc_ref[_stream(rho, i - 1, r), :] if i > 0 else kp_ref[_stream(rho, 0, r), :]
                v_prev = vc_ref[_stream(rho, i - 1, r), :] if i > 0 else vp_ref[_stream(rho, 0, r), :]
                kb = jnp.concatenate([k_prev, kc_ref[cur, :]], axis=0).astype(BF16)
                vb = jnp.concatenate([v_prev, vc_ref[cur, :]], axis=0).astype(BF16)
                o2, lse2 = _pair_fwd(q_ref[cur, :], kb, vb, bias_rest if i > 0 else bias_0, None, (0, 1), None)
                o_ref[cur, :] = o2
                lse_ref[cur, :] = lse2

        _for_streams(r, stream, side_by_side=8)

    before = lambda sb: jnp.maximum(sb * nq - 1, 0)
    return pl.pallas_call(
        body, name=f"attn_b_fwd_r{r}", grid=(NH // 2, steps),
        in_specs=[SMEM,
                  pl.BlockSpec((rows, 128), lambda j, sb: (sb, qc + j)),
                  pl.BlockSpec((BLK * r, 128), lambda j, sb: (before(sb), kc + j)),
                  pl.BlockSpec((rows, 128), lambda j, sb: (sb, kc + j)),
                  pl.BlockSpec((BLK * r, 128), lambda j, sb: (before(sb), vc + j)),
                  pl.BlockSpec((rows, 128), lambda j, sb: (sb, vc + j))],
        out_specs=[pl.BlockSpec((rows, 128), lambda j, sb: (sb, j))] * 2,
        out_shape=[jax.ShapeDtypeStruct((s, 512), F32)] * 2,
        compiler_params=_cp(("parallel", "parallel")),
    )(slopes, proj, proj, proj, proj, proj)


def _attn_b_bwd(proj, slopes, d_o, o, lse, r, so_far=None):
    s = proj.shape[0]
    nq = B_BLOCKS_PER_STEP[r]
    rows = BLK * r * nq
    steps = s // rows
    qc, kc, vc = WA // 128, WA // 128 + 4, WA // 128 + 8
    chained = so_far is not None

    def body(slope_ref, q_ref, kp_ref, kc_ref, vp_ref, vc_ref, do_ref, o_ref, lse_ref, *rest):
        if chained:
            pq_ref, pk_ref, pv_ref, dq_ref, dk_ref, dv_ref, kcar, vcar = rest
        else:
            dq_ref, dk_ref, dv_ref, kcar, vcar = rest
        j = pl.program_id(0)
        sb = pl.program_id(1)

        @pl.when(sb == 0)
        def _():
            kcar[...] = jnp.zeros_like(kcar)
            vcar[...] = jnp.zeros_like(vcar)

        if chained:
            dk_ref[...] = kcar[...] + pk_ref[...]
            dv_ref[...] = vcar[...] + pv_ref[...]
        else:
            dk_ref[...] = kcar[...]
            dv_ref[...] = vcar[...]

        @pl.when(sb < steps)
        def _():
            sl2 = (slope_ref[2 * j], slope_ref[2 * j + 1])
            bias_rest = _stack_heads(sl2, _band_base(B_MAX_DIST, r, False))
            bias_0 = jnp.where(sb > 0, bias_rest, _stack_heads(sl2, _band_base(B_MAX_DIST, r, True)))

            def stream(rho):
                for i in range(nq):
                    cur = _stream(rho, i, r)
                    k_prev = kc_ref[_stream(rho, i - 1, r), :] if i > 0 else kp_ref[_stream(rho, 0, r), :]
                    v_prev = vc_ref[_stream(rho, i - 1, r), :] if i > 0 else vp_ref[_stream(rho, 0, r), :]
                    kb = jnp.concatenate([k_prev, kc_ref[cur, :]], axis=0).astype(BF16)
                    vb = jnp.concatenate([v_prev, vc_ref[cur, :]], axis=0).astype(BF16)
                    dq2, dk2, dv2, _ = _pair_bwd(q_ref[cur, :], kb, vb, do_ref[cur, :], o_ref[cur, :], lse_ref[cur, :],
                                                 bias_rest if i > 0 else bias_0, None, (0, 1), None)
                    dq_ref[cur, :] = dq2 + pq_ref[cur, :] if chained else dq2
                    if i == 0:
                        last = _stream(rho, nq - 1, r)
                        dk_ref[last, :] += dk2[:BLK]
                        dv_ref[last, :] += dv2[:BLK]
                    else:
                        kcar[_stream(rho, i - 1, r), :] += dk2[:BLK]
                        vcar[_stream(rho, i - 1, r), :] += dv2[:BLK]
                    kcar[cur, :] = dk2[BLK:]
                    vcar[cur, :] = dv2[BLK:]

            _for_streams(r, stream, side_by_side=8)

    cur_step = lambda sb: jnp.minimum(sb, steps - 1)
    before = lambda sb: jnp.maximum(cur_step(sb) * nq - 1, 0)
    out_prev = lambda sb: jnp.maximum(sb - 1, 0)
    tile = lambda col: pl.BlockSpec((rows, 128), lambda j, sb: (cur_step(sb), col + j))
    edge = lambda col: pl.BlockSpec((BLK * r, 128), lambda j, sb: (before(sb), col + j))
    late = pl.BlockSpec((rows, 128), lambda j, sb: (out_prev(sb), j))
    grads = [tile(0), late, late]
    return pl.pallas_call(
        body, name=f"attn_b_bwd_r{r}", grid=(NH // 2, steps + 1),
        in_specs=[SMEM, tile(qc), edge(kc), tile(kc), edge(vc), tile(vc), tile(0), tile(0), tile(0)]
        + (grads if chained else []),
        out_specs=grads,
        out_shape=[jax.ShapeDtypeStruct((s, 512), F32)] * 3,
        scratch_shapes=[pltpu.VMEM((rows, 128), F32), pltpu.VMEM((rows, 128), F32)],
        compiler_params=_cp(("parallel", "arbitrary")),
    )(slopes, proj, proj, proj, proj, proj, d_o, o, lse, *(so_far if chained else ()))


def _row(v):
    return v.reshape(1, -1)


def _layer_norm_stats(z):
    mu = jnp.mean(z, axis=-1, keepdims=True)
    zc = z - mu
    var = jnp.mean(zc * zc, axis=-1, keepdims=True)
    rstd = lax.rsqrt(var + LN_EPS)
    return zc * rstd, rstd


def _layer_norm_bwd(dh, zh, rstd, g):
    dzh = dh * g
    return rstd * (dzh - jnp.mean(dzh, axis=-1, keepdims=True) - zh * jnp.mean(dzh * zh, axis=-1, keepdims=True))


def _rms(o):
    return lax.rsqrt(jnp.mean(o * o, axis=-1, keepdims=True) + RMS_EPS)


def _mix_ln1(x, o_a, o_b, lse_b, norm_a_g, norm_b_g, w_o, ln1_g, ln1_b, tm=256):
    s = x.shape[0]

    def body(x_ref, oa_ref, ob1, ob2, ob3, l1, l2, l3, ga_ref, gb_ref, wo_ref, g_ref, b_ref,
             obm_ref, lse_ref, cat_ref, z1_ref, h1_ref, h1b_ref):
        la, lb, lc = l1[...], l2[...], l3[...]
        m = jnp.maximum(jnp.maximum(la, lb), lc)
        ea, eb, ec = jnp.exp(la - m), jnp.exp(lb - m), jnp.exp(lc - m)
        den = ea + eb + ec
        obm = (ea / den) * ob1[...] + (eb / den) * ob2[...] + (ec / den) * ob3[...]
        obm_ref[...] = obm
        lse_ref[...] = m + jnp.log(den)
        oa = oa_ref[...]
        na = oa * _rms(oa) * ga_ref[...]
        nb_ = obm * _rms(obm) * gb_ref[...]
        cat = jnp.concatenate([na, nb_], axis=1).astype(BF16)
        cat_ref[...] = cat
        z1 = ALPHA * x_ref[...] + _nn(cat, wo_ref[...])
        z1_ref[...] = z1
        zh, _ = _layer_norm_stats(z1)
        h1 = zh * g_ref[...] + b_ref[...]
        h1_ref[...] = h1
        h1b_ref[...] = h1.astype(BF16)

    t512 = pl.BlockSpec((tm, 512), lambda i: (i, 0))
    td = pl.BlockSpec((tm, D), lambda i: (i, 0))
    return pl.pallas_call(
        body, name="mix_ln1", grid=(s // tm,),
        in_specs=[td] + [t512] * 7 + [_const((1, 512))] * 2 + [_resident((D, D))] + [_const((1, D))] * 2,
        out_specs=[t512, t512, td, td, td, td],
        out_shape=[jax.ShapeDtypeStruct((s, 512), F32), jax.ShapeDtypeStruct((s, 512), F32),
                   jax.ShapeDtypeStruct((s, D), BF16), jax.ShapeDtypeStruct((s, D), F32),
                   jax.ShapeDtypeStruct((s, D), F32), jax.ShapeDtypeStruct((s, D), BF16)],
        compiler_params=_cp(("parallel",)),
    )(x, o_a, *o_b, *lse_b, _row(norm_a_g), _row(norm_b_g), w_o, _row(ln1_g), _row(ln1_b))


def _gelu_and_grad(x):
    c = math.sqrt(2.0 / math.pi)
    x2 = x * x
    cx = c * x
    t = jnp.tanh(cx * (1.0 + 0.044715 * x2))
    q = 1.0 + t
    g = (0.5 * x) * q
    dg = 0.5 * q + ((0.5 * cx) * (1.0 - t * t)) * (1.0 + (3.0 * 0.044715) * x2)
    return g, dg


def _shift_down(u, before):
    n = u.shape[0]
    ext = jnp.concatenate([before, u], axis=0)
    return pltpu.roll(ext, 1, 0)[8:], pltpu.roll(ext, 2, 0)[8:]


def _shift_up(u, after):
    n = u.shape[0]
    ext = jnp.concatenate([u, after], axis=0)
    return pltpu.roll(ext, n + 7, 0)[:n], pltpu.roll(ext, n + 6, 0)[:n]


def _up_conv_gelu(h1b, w_up, cwb, tm=256, tn=FF // 2, chunk_rows=16, piece_cols=512):
    s = h1b.shape[0]
    n_i = s // tm
    n_t = (FF // tn) * n_i

    def body(h_ref, wg_ref, wv_ref, c_ref, up_ref, a_ref, g_ref, a1_ref, pend_a, pend_b, carry):
        t = pl.program_id(0)
        row_tile = jnp.maximum(t - 1, 0) % n_i
        w_refs = (wg_ref, wv_ref)

        @pl.when(t == 0)
        def _():
            pend_b[...] = jnp.zeros_like(pend_b)
            carry[...] = jnp.zeros_like(carry)

        def step(dst, src):
            def chunk(c, before):
                rows = pl.ds(c * chunk_rows, chunk_rows)
                u, last = [], []
                for half in (0, 1):
                    up = src[half, rows, :]
                    r1, r2 = _shift_down(up, before[half])
                    u.append(r2 * c_ref[0, half:half + 1, :] + r1 * c_ref[1, half:half + 1, :]
                             + up * c_ref[2, half:half + 1, :] + c_ref[3, half:half + 1, :])
                    last.append(up[chunk_rows - 8:])
                g, dg = _gelu_and_grad(u[0])
                a_ref[rows, :] = (g * u[1]).astype(BF16)
                g_ref[rows, :] = g.astype(BF16)
                a1_ref[rows, :] = (u[1] * dg).astype(BF16)
                return tuple(last)

            edge = tuple(jnp.where(row_tile > 0, carry[half], 0.0) for half in (0, 1))
            pieces = [(half, c0, min(piece_cols, tn - c0)) for half in (0, 1) for c0 in range(0, tn, piece_cols)]
            n_c = tm // chunk_rows
            done = 0
            for p, (half, c0, width) in enumerate(pieces):
                cols = slice(c0, c0 + width)
                up = _nn(h_ref[...], w_refs[half][:, cols])
                up_ref[half, :, cols] = up.astype(BF16)
                dst[half, :, cols] = up
                upto = n_c * (p + 1) // len(pieces)
                for c in range(done, upto):
                    edge = chunk(c, edge)
                done = upto
            for half in (0, 1):
                carry[half] = edge[half]

        @pl.when(t % 2 == 0)
        def _():
            step(pend_a, pend_b)

        @pl.when(t % 2 == 1)
        def _():
            step(pend_b, pend_a)

    mm = lambda t: jnp.minimum(t, n_t - 1)
    ew = lambda t: jnp.maximum(t - 1, 0)
    out_tile = pl.BlockSpec((tm, tn), lambda t: (ew(t) % n_i, ew(t) // n_i))
    return pl.pallas_call(
        body, name="up_conv_gelu", grid=(n_t + 1,),
        in_specs=[pl.BlockSpec((tm, D), lambda t: (mm(t) % n_i, 0)),
                  pl.BlockSpec((D, tn), lambda t: (0, mm(t) // n_i)),
                  pl.BlockSpec((D, tn), lambda t: (0, FF // tn + mm(t) // n_i)),
                  pl.BlockSpec((4, 2, tn), lambda t: (0, 0, ew(t) // n_i))],
        out_specs=[pl.BlockSpec((2, tm, tn), lambda t: (0, mm(t) % n_i, mm(t) // n_i)), out_tile, out_tile, out_tile],
        out_shape=[jax.ShapeDtypeStruct((2, s, FF), BF16)] + [jax.ShapeDtypeStruct((s, FF), BF16)] * 3,
        scratch_shapes=[pltpu.VMEM((2, tm, tn), F32), pltpu.VMEM((2, tm, tn), F32), pltpu.VMEM((2, 8, tn), F32)],
        compiler_params=_cp(("arbitrary",)),
    )(h1b, w_up, w_up, cwb)


def _down_ln2_loss(a, w_down, h1, target, ln2_g, ln2_b, tm=256):
    s = a.shape[0]

    def body(a_ref, w_ref, h_ref, t_ref, g_ref, b_ref, dz_ref, dzb_ref, st_ref):
        @pl.when(pl.program_id(0) == 0)
        def _():
            st_ref[...] = jnp.zeros_like(st_ref)

        z2 = ALPHA * h_ref[...] + _nn(a_ref[...], w_ref[...])
        zh, rstd = _layer_norm_stats(z2)
        diff = zh * g_ref[...] + b_ref[...] - t_ref[...]
        part = 0.5 * jnp.sum(jnp.mean(diff * diff, axis=-1, keepdims=True), axis=0, keepdims=True)
        dy = diff * (1.0 / D)
        st_ref[0:1, :] += jnp.sum(dy * zh, axis=0, keepdims=True)
        st_ref[1:2, :] += jnp.sum(dy, axis=0, keepdims=True)
        st_ref[2:3, :] += jnp.broadcast_to(part, (1, D))
        dz = _layer_norm_bwd(dy, zh, rstd, g_ref[...])
        dz_ref[...] = dz
        dzb_ref[...] = dz.astype(BF16)

    td = pl.BlockSpec((tm, D), lambda i: (i, 0))
    return pl.pallas_call(
        body, name="down_ln2_loss", grid=(s // tm,),
        in_specs=[pl.BlockSpec((tm, FF), lambda i: (i, 0)), _resident((FF, D)), td, td, _const((1, D)), _const((1, D))],
        out_specs=[td, td, _const((8, D))],
        out_shape=[jax.ShapeDtypeStruct((s, D), F32), jax.ShapeDtypeStruct((s, D), BF16),
                   jax.ShapeDtypeStruct((8, D), F32)],
        compiler_params=_cp(("arbitrary",)),
    )(a, w_down, h1, target, _row(ln2_g), _row(ln2_b))


def _d_act(dz2b, w_down, tm=512):
    s = dz2b.shape[0]

    def body(dz_ref, w_ref, o_ref):
        o_ref[...] = _nt(dz_ref[...], w_ref[...])

    return pl.pallas_call(
        body, name="d_act", grid=(s // tm,),
        in_specs=[pl.BlockSpec((tm, D), lambda i: (i, 0)), _resident((FF, D))],
        out_specs=pl.BlockSpec((tm, FF), lambda i: (i, 0)),
        out_shape=jax.ShapeDtypeStruct((s, FF), F32),
        compiler_params=_cp(("parallel",)),
    )(dz2b, w_down)


def _conv_gelu_bwd(da, up, g, a1, cwb, tm=256, tn=FF // 2, chunk_rows=16):
    s = da.shape[0]
    n_i = s // tm
    n_c = tm // chunk_rows

    def body(da_ref, up_ref, g_ref, a1_ref, c_ref, dup_ref, dc_ref, carry):
        @pl.when(pl.program_id(1) == 0)
        def _():
            carry[...] = jnp.zeros_like(carry)
            dc_ref[...] = jnp.zeros_like(dc_ref)

        def fold(v):
            return jnp.sum(v.reshape(chunk_rows // 8, 8, v.shape[1]), axis=0)

        def chunk(cc, state):
            after, sums = state
            rows = pl.ds((n_c - 1 - cc) * chunk_rows, chunk_rows)
            da_c = da_ref[rows, :]
            dus = (da_c * a1_ref[rows, :].astype(F32), da_c * g_ref[rows, :].astype(F32))
            head, new_sums = [], []
            for half in (0, 1):
                du = dus[half]
                up = up_ref[half, rows, :].astype(F32)
                l1, l2 = _shift_up(du, after[half])
                dup = (du * c_ref[2, half:half + 1, :] + l1 * c_ref[1, half:half + 1, :]
                       + l2 * c_ref[0, half:half + 1, :])
                dup_ref[half, rows, :] = dup.astype(BF16)
                parts = (fold(l2 * up), fold(l1 * up), fold(du * up), fold(du))
                new_sums.append(parts if sums is None else tuple(a + b for a, b in zip(sums[half], parts)))
                head.append(du[:8])
            return tuple(head), new_sums

        state = ((carry[0], carry[1]), None)
        for cc in range(n_c):
            state = chunk(cc, state)
        head, sums = state
        for half in (0, 1):
            carry[half] = head[half]
            for k in range(4):
                dc_ref[k, half:half + 1, :] += jnp.sum(sums[half][k], axis=0, keepdims=True)

    rev = lambda ii: n_i - 1 - ii
    tile = pl.BlockSpec((tm, tn), lambda j, ii: (rev(ii), j))
    pair = pl.BlockSpec((2, tm, tn), lambda j, ii: (0, rev(ii), j))
    per_col = pl.BlockSpec((4, 2, tn), lambda j, ii: (0, 0, j))
    return pl.pallas_call(
        body, name="conv_gelu_bwd", grid=(FF // tn, n_i),
        in_specs=[tile, pair, tile, tile, per_col],
        out_specs=[pair, per_col],
        out_shape=[jax.ShapeDtypeStruct((2, s, FF), BF16), jax.ShapeDtypeStruct((4, 2, FF), F32)],
        scratch_shapes=[pltpu.VMEM((2, 8, tn), F32)],
        compiler_params=_cp(("parallel", "arbitrary")),
    )(da, up, g, a1, cwb)


def _dh1_ln1_bwd(dz2, dup, w_up, z1, ln1_g, tm=256):
    s = dz2.shape[0]

    def body(dz2_ref, dup_ref, w_ref, z1_ref, g_ref, dz1_ref, dz1b_ref, st_ref):
        @pl.when(pl.program_id(0) == 0)
        def _():
            st_ref[...] = jnp.zeros_like(st_ref)

        dh = ALPHA * dz2_ref[...] + _nt(dup_ref[0], w_ref[:, :FF]) + _nt(dup_ref[1], w_ref[:, FF:])
        zh, rstd = _layer_norm_stats(z1_ref[...])
        st_ref[0:1, :] += jnp.sum(dh * zh, axis=0, keepdims=True)
        st_ref[1:2, :] += jnp.sum(dh, axis=0, keepdims=True)
        dz = _layer_norm_bwd(dh, zh, rstd, g_ref[...])
        dz1_ref[...] = dz
        dz1b_ref[...] = dz.astype(BF16)

    td = pl.BlockSpec((tm, D), lambda i: (i, 0))
    return pl.pallas_call(
        body, name="dh1_ln1_bwd", grid=(s // tm,),
        in_specs=[td, pl.BlockSpec((2, tm, FF), lambda i: (0, i, 0)), _resident((D, 2 * FF)), td, _const((1, D))],
        out_specs=[td, td, _const((8, D))],
        out_shape=[jax.ShapeDtypeStruct((s, D), F32), jax.ShapeDtypeStruct((s, D), BF16),
                   jax.ShapeDtypeStruct((8, D), F32)],
        compiler_params=_cp(("arbitrary",)),
    )(dz2, dup, w_up, z1, _row(ln1_g))


def _dcat_rms_bwd(dz1b, w_o, o_a, o_b, norm_a_g, norm_b_g, tm=256):
    s = dz1b.shape[0]

    def body(dz_ref, w_ref, oa_ref, ob_ref, ga_ref, gb_ref, da_ref, db_ref, st_ref):
        @pl.when(pl.program_id(0) == 0)
        def _():
            st_ref[...] = jnp.zeros_like(st_ref)

        dcat = _nt(dz_ref[...], w_ref[...])
        for k, (o_ref, g_ref, d_ref) in enumerate(((oa_ref, ga_ref, da_ref), (ob_ref, gb_ref, db_ref))):
            o = o_ref[...]
            dn = dcat[:, 512 * k:512 * (k + 1)]
            rr = _rms(o)
            oh = o * rr
            st_ref[k:k + 1, :] += jnp.sum(dn * oh, axis=0, keepdims=True)
            doh = dn * g_ref[...]
            d_ref[...] = rr * (doh - oh * jnp.mean(doh * oh, axis=-1, keepdims=True))

    t512 = pl.BlockSpec((tm, 512), lambda i: (i, 0))
    return pl.pallas_call(
        body, name="dcat_rms_bwd", grid=(s // tm,),
        in_specs=[pl.BlockSpec((tm, D), lambda i: (i, 0)), _resident((D, D)), t512, t512,
                  _const((1, 512)), _const((1, 512))],
        out_specs=[t512, t512, _const((8, 512))],
        out_shape=[jax.ShapeDtypeStruct((s, 512), F32), jax.ShapeDtypeStruct((s, 512), F32),
                   jax.ShapeDtypeStruct((8, 512), F32)],
        compiler_params=_cp(("arbitrary",)),
    )(dz1b, w_o, o_a, o_b, _row(norm_a_g), _row(norm_b_g))


def _dproj_combine(dqa, dka, dva, dqkv_b, tm=256):
    s = dqa.shape[0]

    def body(qa, ka, va, qb, kb, vb, o_ref):
        o_ref[:, 0:512] = qa[...].astype(BF16)
        o_ref[:, 512:640] = ka[...].astype(BF16)
        o_ref[:, 640:768] = va[...].astype(BF16)
        o_ref[:, 768:1280] = qb[...].astype(BF16)
        o_ref[:, 1280:1792] = kb[...].astype(BF16)
        o_ref[:, 1792:2304] = vb[...].astype(BF16)

    t512 = pl.BlockSpec((tm, 512), lambda i: (i, 0))
    t128 = pl.BlockSpec((tm, 128), lambda i: (i, 0))
    return pl.pallas_call(
        body, name="dproj_combine", grid=(s // tm,),
        in_specs=[t512, t128, t128] + [t512] * 3,
        out_specs=pl.BlockSpec((tm, WIN), lambda i: (i, 0)),
        out_shape=jax.ShapeDtypeStruct((s, WIN), BF16),
        compiler_params=_cp(("parallel",)),
    )(dqa, dka, dva, *dqkv_b)


def _grad_x(dz1, dproj, w_in_t, zero, tm=256):
    s = dz1.shape[0]

    def body(dz_ref, dp_ref, w_ref, z_ref, o_ref):
        o_ref[...] = ALPHA * dz_ref[...] + _nn(dp_ref[...], w_ref[...]) + z_ref[0:1, 0:1]

    td = pl.BlockSpec((tm, D), lambda i: (i, 0))
    return pl.pallas_call(
        body, name="grad_x", grid=(s // tm,),
        in_specs=[td, pl.BlockSpec((tm, WIN), lambda i: (i, 0)), _resident((WIN, D)), _const((8, 128))],
        out_specs=td, out_shape=jax.ShapeDtypeStruct((s, D), F32),
        compiler_params=_cp(("parallel",)),
    )(dz1, dproj, w_in_t, zero)


def _place():
    return lax.axis_index("x"), lax.axis_index("y"), lax.axis_index("c")


def _other_chips(x, y):
    return [(1 - x, y), (x, 1 - y), (1 - x, 1 - y)]


def _hbm(a):
    return pltpu.with_memory_space_constraint(a, pltpu.HBM)


def _gather_w_in(shard, conv_w):
    rows_k = shard.shape[0]
    half = rows_k // 2

    def body(src, conv_src, out, conv_out, send_sems, recv_sems):
        x, y, c = _place()
        b = 2 * x + y
        sibling = (x, y, 1 - c)
        chips = _other_chips(x, y)

        def copy(idx, chip_b, core, to, first_hop=False):
            rows = out.at[pl.ds(pl.multiple_of(chip_b * rows_k + core * half, 16), half)]
            s_ref = src.at[pl.ds(pl.multiple_of(core * half, 16), half)] if first_hop else rows
            return pltpu.make_async_remote_copy(src_ref=s_ref, dst_ref=rows, send_sem=send_sems.at[idx],
                                                recv_sem=recv_sems.at[idx], device_id=to, device_id_type=MESH)

        def own_copy():
            return pltpu.make_async_remote_copy(
                src_ref=src, dst_ref=out.at[pl.ds(pl.multiple_of(b * rows_k, 16), rows_k)], send_sem=send_sems.at[6],
                recv_sem=recv_sems.at[6], device_id=sibling, device_id_type=MESH)

        def conv_copy(idx, chip_b, to):
            return pltpu.make_async_remote_copy(src_ref=conv_src, dst_ref=conv_out.at[chip_b],
                                                send_sem=send_sems.at[7 + idx], recv_sem=recv_sems.at[7 + idx],
                                                device_id=to, device_id_type=MESH)

        started = [own_copy(), conv_copy(3, b, sibling)]
        for jn, chip in enumerate(chips):
            started += [copy(jn, b, c, (chip[0], chip[1], c), first_hop=True), conv_copy(jn, b, (chip[0], chip[1], c))]
        for cp in started:
            cp.start()
        for jn, chip in enumerate(chips):
            cb = 2 * chip[0] + chip[1]
            copy(jn, cb, c, (chip[0], chip[1], c)).wait_recv()
            cp = copy(3 + jn, cb, c, sibling)
            cp.start()
            started.append(cp)
        for jn, chip in enumerate(chips):
            cb = 2 * chip[0] + chip[1]
            copy(3 + jn, cb, 1 - c, sibling).wait_recv()
            conv_copy(jn, cb, (chip[0], chip[1], c)).wait_recv()
        own_copy().wait_recv()
        conv_copy(3, b, sibling).wait_recv()
        for cp in started:
            cp.wait_send()

    return pl.pallas_call(
        body, name="gather_w_in",
        in_specs=[ANY, ANY], out_specs=[ANY, ANY],
        out_shape=[jax.ShapeDtypeStruct((N_CHIPS * rows_k, D), BF16), jax.ShapeDtypeStruct((N_CHIPS,) + conv_w.shape, F32)],
        scratch_shapes=[pltpu.SemaphoreType.DMA((11,)), pltpu.SemaphoreType.DMA((11,))],
        compiler_params=pltpu.CompilerParams(has_side_effects=True),
    )(shard, conv_w)


def _weight_copies(shard, land, send_sems, recv_sems, arrivals):
    x, y, c = _place()
    n_rows, n_cols = shard.shape
    peers = [(px, py, c) for px, py in _other_chips(x, y)] + [(x, y, 1 - c)]
    cps = []
    for jn, peer in enumerate(peers):
        at = 2 * peer[0] + peer[1] if arrivals else 2 * x + y
        if land.shape[1] == n_cols:
            dst = land.at[pl.ds(pl.multiple_of(at * n_rows, 16), n_rows)]
        else:
            dst = land.at[:, pl.ds(pl.multiple_of(at * n_cols, 128), n_cols)]
        cps.append(pltpu.make_async_remote_copy(src_ref=shard, dst_ref=dst, send_sem=send_sems.at[jn],
                                                recv_sem=recv_sems.at[jn], device_id=peer, device_id_type=MESH))
    return cps


def _weights_start(shards, after):
    n = len(shards)
    lands = [lax.empty((N_CHIPS * sh.shape[0], D) if sh.shape[1] == D else (D, N_CHIPS * sh.shape[1]), BF16)
             for sh in shards]

    def body(*refs):
        src, land = refs[:n], refs[n:2 * n]
        send_sems, recv_sems = refs[2 * n + 1:3 * n + 1], refs[3 * n + 1:4 * n + 1]
        for k in range(n):
            for send in _weight_copies(src[k], land[k], send_sems[k], recv_sems[k], False):
                send.start()
        refs[-1][...] = jnp.zeros_like(refs[-1])

    res = pl.pallas_call(
        body, name="weights_start",
        in_specs=[HBM] * (2 * n) + [ANY], out_specs=[SEM] * (2 * n) + [HBM] * (2 * n) + [VMEM],
        out_shape=[pltpu.SemaphoreType.DMA((4,))] * (2 * n)
        + [pltpu.HBM(a.shape, a.dtype) for a in (*shards, *lands)] + [jax.ShapeDtypeStruct((8, 128), F32)],
        input_output_aliases={i: i + 2 * n for i in range(2 * n)},
        compiler_params=pltpu.CompilerParams(has_side_effects=DATAFLOW),
    )(*[_hbm(a) for a in (*shards, *lands)], after)
    return [(res[k], res[n + k], res[2 * n + k], res[3 * n + k]) for k in range(n)], res[-1]


def _weights_wait(started, after, name):
    send_sems, recv_sems, shard, land = started

    def body(s_ref, l_ref, send_ref, recv_ref, after_ref, s_out, l_out):
        for cp in _weight_copies(s_ref, l_ref, send_ref, recv_ref, True):
            cp.wait_send()
            cp.wait_recv()

    return pl.pallas_call(
        body, name=name,
        in_specs=[HBM, HBM, SEM, SEM, ANY], out_specs=[HBM, HBM],
        out_shape=[pltpu.HBM(shard.shape, shard.dtype), pltpu.HBM(land.shape, land.dtype)],
        input_output_aliases={0: 0, 1: 1},
        compiler_params=pltpu.CompilerParams(has_side_effects=DATAFLOW),
    )(shard, land, send_sems, recv_sems, after)[1]


def _grad_copies(g_ref, land_ref, send_sems, recv_sems):
    x, y, c = _place()
    cps = []
    for d in range(1, 8):
        px, py, pc = x ^ (d >> 2), y ^ ((d >> 1) & 1), c ^ (d & 1)
        cps.append(pltpu.make_async_remote_copy(
            src_ref=g_ref.at[2 * px + py, pc], dst_ref=land_ref.at[d - 1], send_sem=send_sems.at[d - 1],
            recv_sem=recv_sems.at[d - 1], device_id=(px, py, pc), device_id_type=MESH))
    return cps


def _grads_start(grads_b, name):
    n = len(grads_b)
    lands = [lax.empty((7, g.shape[2], D), BF16) for g in grads_b]

    def body(*refs):
        g, land = refs[:n], refs[n:2 * n]
        send_sems, recv_sems = refs[2 * n:3 * n], refs[3 * n:4 * n]
        for k in range(n):
            for cp in _grad_copies(g[k], land[k], send_sems[k], recv_sems[k]):
                cp.start()
        refs[-1][...] = jnp.zeros_like(refs[-1])

    res = pl.pallas_call(
        body, name=name,
        in_specs=[HBM] * (2 * n), out_specs=[SEM] * (2 * n) + [HBM] * (2 * n) + [VMEM],
        out_shape=[pltpu.SemaphoreType.DMA((7,))] * (2 * n)
        + [pltpu.HBM(a.shape, a.dtype) for a in (*grads_b, *lands)] + [jax.ShapeDtypeStruct((8, 128), F32)],
        input_output_aliases={i: i + 2 * n for i in range(2 * n)},
        compiler_params=pltpu.CompilerParams(has_side_effects=DATAFLOW),
    )(*[_hbm(a) for a in (*grads_b, *lands)])
    return [(res[k], res[n + k], res[2 * n + k], res[3 * n + k]) for k in range(n)], res[-1]


def _grads_wait(started, after, name):
    n = len(started)

    def body(*refs):
        g, land = refs[:n], refs[n:2 * n]
        send_sems, recv_sems = refs[2 * n:3 * n], refs[3 * n:4 * n]
        for k in range(n):
            for cp in _grad_copies(g[k], land[k], send_sems[k], recv_sems[k]):
                cp.wait_send()
                cp.wait_recv()

    gs = [st[2] for st in started]
    lands = [st[3] for st in started]
    res = pl.pallas_call(
        body, name=name,
        in_specs=[HBM] * (2 * n) + [SEM] * (2 * n) + [ANY], out_specs=[HBM] * (2 * n),
        out_shape=[pltpu.HBM(a.shape, a.dtype) for a in (*gs, *lands)],
        input_output_aliases={i: i for i in range(2 * n)},
        compiler_params=pltpu.CompilerParams(has_side_effects=DATAFLOW),
    )(*gs, *lands, *[st[0] for st in started], *[st[1] for st in started], after)
    return res[n:]


def _sum_partials(grad4, got, cb, name, tr):
    h = grad4.shape[2]
    per_half = h // tr

    def body(cb_ref, g_ref, o_ref, out_ref):
        acc = g_ref[...]
        for j in range(7):
            acc = acc + o_ref[j].astype(F32)
        out_ref[...] = acc

    return pl.pallas_call(
        body, name=name,
        grid_spec=pltpu.PrefetchScalarGridSpec(
            num_scalar_prefetch=1, grid=(per_half,),
            in_specs=[pl.BlockSpec((None, None, tr, D), lambda i, cb_ref: (cb_ref[1], cb_ref[0], i, 0)),
                      pl.BlockSpec((7, tr, D), lambda i, cb_ref: (0, i, 0))],
            out_specs=pl.BlockSpec((tr, D), lambda i, cb_ref: (cb_ref[0] * per_half + i, 0))),
        out_shape=jax.ShapeDtypeStruct((2 * h, D), F32),
        compiler_params=_cp(("arbitrary",)),
    )(cb, grad4, got)


def _swap_halves(shards, name):
    n = len(shards)

    def body(*refs):
        out, send_sems, recv_sems = refs[n:2 * n], refs[2 * n], refs[2 * n + 1]
        x, y, c = _place()
        cps = []
        for k in range(n):
            h = shards[k].shape[0] // 2
            mine = out[k].at[pl.ds(pl.multiple_of(c * h, 8), h)]
            cp = pltpu.make_async_remote_copy(src_ref=mine, dst_ref=mine, send_sem=send_sems.at[k],
                                              recv_sem=recv_sems.at[k], device_id=(x, y, 1 - c), device_id_type=MESH)
            cp.start()
            cps.append(cp)
        for cp in cps:
            cp.wait()

    return pl.pallas_call(
        body, name=name,
        in_specs=[ANY] * n, out_specs=[ANY] * n,
        out_shape=[jax.ShapeDtypeStruct(sh.shape, F32) for sh in shards],
        input_output_aliases={k: k for k in range(n)},
        scratch_shapes=[pltpu.SemaphoreType.DMA((n,)), pltpu.SemaphoreType.DMA((n,))],
        compiler_params=pltpu.CompilerParams(has_side_effects=True),
    )(*shards)


def _share_halves(shards, small):
    n = len(shards)
    rows = small.shape[0]

    def body(*refs):
        small_ref = refs[n]
        out, total_ref = refs[n + 1:2 * n + 1], refs[2 * n + 1]
        all_ref, send_sems, recv_sems, ssend, srecv = refs[2 * n + 2:]
        x, y, c = _place()
        me = 4 * x + 2 * y + c
        cps = []
        for k in range(n):
            h = shards[k].shape[0] // 2
            mine = out[k].at[pl.ds(pl.multiple_of(c * h, 8), h)]
            cp = pltpu.make_async_remote_copy(src_ref=mine, dst_ref=mine, send_sem=send_sems.at[k],
                                              recv_sem=recv_sems.at[k], device_id=(x, y, 1 - c), device_id_type=MESH)
            cp.start()
            cps.append(cp)
        all_ref[me] = small_ref[...]
        peers = []
        for d in range(1, 8):
            px, py, pc = x ^ (d >> 2), y ^ ((d >> 1) & 1), c ^ (d & 1)
            cp = pltpu.make_async_remote_copy(src_ref=small_ref, dst_ref=all_ref.at[me],
                                              send_sem=ssend.at[d - 1], recv_sem=srecv.at[d - 1],
                                              device_id=(px, py, pc), device_id_type=MESH)
            cp.start()
            peers.append(cp)
        for cp in peers:
            cp.wait()
        acc = all_ref[0]
        for d in range(1, 8):
            acc = acc + all_ref[d]
        total_ref[...] = acc
        for cp in cps:
            cp.wait()

    return pl.pallas_call(
        body, name="share_halves",
        in_specs=[ANY] * n + [VMEM], out_specs=[ANY] * n + [VMEM],
        out_shape=[jax.ShapeDtypeStruct(sh.shape, F32) for sh in shards] + [jax.ShapeDtypeStruct((rows, D), F32)],
        input_output_aliases={k: k for k in range(n)},
        scratch_shapes=[pltpu.VMEM((8, rows, D), F32), pltpu.SemaphoreType.DMA((n,)), pltpu.SemaphoreType.DMA((n,)),
                        pltpu.SemaphoreType.DMA((7,)), pltpu.SemaphoreType.DMA((7,))],
        compiler_params=pltpu.CompilerParams(has_side_effects=True),
    )(*shards, small)


def _adamw(w, g, m, v, name, tr):
    rows, cols = w.shape

    def body(w_ref, g_ref, m_ref, v_ref, d_ref, nm_ref, nv_ref):
        g_ = g_ref[...]
        nm = ADAM_B1 * m_ref[...] + (1.0 - ADAM_B1) * g_
        nv = ADAM_B2 * v_ref[...] + (1.0 - ADAM_B2) * (g_ * g_)
        m_hat = nm / (1.0 - ADAM_B1 ** ADAM_STEP)
        v_hat = nv / (1.0 - ADAM_B2 ** ADAM_STEP)
        d_ref[...] = -ADAM_LR * (m_hat / (jnp.sqrt(v_hat) + ADAM_EPS) + ADAM_WD * w_ref[...])
        nm_ref[...] = nm
        nv_ref[...] = nv

    spec = pl.BlockSpec((tr, cols), lambda i: (i, 0))
    return pl.pallas_call(
        body, name=name, grid=(rows // tr,),
        in_specs=[spec] * 4, out_specs=[spec] * 3,
        out_shape=[jax.ShapeDtypeStruct((rows, cols), F32)] * 3,
        compiler_params=_cp(("parallel",)),
    )(w, g, m, v)


def _local_step(x, target, w_in_t, late_weights, norm_a_g, norm_b_g, sinks_a, ln1_g, ln1_b,
                conv_w, conv_b, ln2_g, ln2_b, slopes, on_grad):
    cwb = jnp.concatenate([conv_w, conv_b[None]], axis=0).reshape(4, 2, FF)

    proj, xb = _proj(x, w_in_t, "proj")
    o_a, lse_a = _attn_a_fwd(proj, sinks_a)
    fwd_b = [_attn_b_fwd(proj, slopes, r) for r in B_DILATIONS]
    w_o = late_weights(1, fwd_b[-1][1])
    o_b, lse_b, cat, z1, h1, h1b = _mix_ln1(x, o_a, [f[0] for f in fwd_b], [f[1] for f in fwd_b],
                                           norm_a_g, norm_b_g, w_o, ln1_g, ln1_b)
    w_up = late_weights(2, h1b)
    up, a, gate, a1 = _up_conv_gelu(h1b, w_up, cwb)
    w_down = late_weights(3, a)
    dz2, dz2b, st2 = _down_ln2_loss(a, w_down, h1, target, ln2_g, ln2_b)

    on_grad(3, *_grad_w(a, dz2b, "grad_w_down", tm=FF // 2))
    dup, dconv = _conv_gelu_bwd(_d_act(dz2b, w_down), up, gate, a1, cwb)
    on_grad(2, *_grad_w(dup, h1b, "grad_w_up", tm=FF // 2, lhs_halves=True))
    dz1, dz1b, st1 = _dh1_ln1_bwd(dz2, dup, w_up, z1, ln1_g)
    tok = on_grad(1, *_grad_w(cat, dz1b, "grad_w_o", tm=512))
    d_oa, d_ob, st_n = _dcat_rms_bwd(dz1b, w_o, o_a, o_b, norm_a_g + tok[0, 0], norm_b_g)
    dqa, dka, dva, dsink = _attn_a_bwd(proj, sinks_a, d_oa, o_a, lse_a)
    bwd_b = None
    for r in B_DILATIONS:
        bwd_b = _attn_b_bwd(proj, slopes, d_ob, o_b, lse_b, r, bwd_b)
    dproj = _dproj_combine(dqa, dka, dva, bwd_b)
    tok = on_grad(0, *_grad_w(dproj, xb, "grad_w_in", tm=WA))
    gx = _grad_x(dz1, dproj, w_in_t, tok)

    dconv = dconv.reshape(4, 2 * FF)
    small = dict(loss=st2[2, 0:1], norm_a_g=st_n[0], norm_b_g=st_n[1], sinks_a=dsink[:, 0],
                 ln1_g=st1[0], ln1_b=st1[1], conv_w=dconv[0:3].reshape(-1), conv_b=dconv[3],
                 ln2_g=st2[0], ln2_b=st2[1])
    return gx, small


SMALL_ORDER = ("loss", "norm_a_g", "norm_b_g", "sinks_a", "ln1_g", "ln1_b", "conv_b", "ln2_g", "ln2_b", "conv_w")
SMALL_SIZES = dict(loss=1, norm_a_g=512, norm_b_g=512, sinks_a=8, ln1_g=D, ln1_b=D, conv_b=2 * FF, ln2_g=D, ln2_b=D,
                   conv_w=3 * 2 * FF)


def _pack(parts, rows):
    flat = jnp.concatenate([parts[k].reshape(-1).astype(F32) for k in parts])
    return jnp.pad(flat, (0, rows * D - flat.shape[0])).reshape(rows, D)


def _unpack(buf, names, sizes):
    flat = buf.reshape(-1)
    out, at = {}, 0
    for k in names:
        out[k] = flat[at:at + sizes[k]]
        at += sizes[k]
    return out


def kernel(x, w_in, norm_a_g, norm_b_g, sinks_a, w_o, ln1_g, ln1_b, w_up, conv_w, conv_b, w_down, ln2_g, ln2_b, loss_target, m_w_in, m_norm_a_g, m_norm_b_g, m_sinks_a, m_w_o, m_ln1_g, m_ln1_b, m_w_up, m_conv_w, m_conv_b, m_w_down, m_ln2_g, m_ln2_b, v_w_in, v_norm_a_g, v_norm_b_g, v_sinks_a, v_w_o, v_ln1_g, v_ln1_b, v_w_up, v_conv_w, v_conv_b, v_w_down, v_ln2_g, v_ln2_b):
    xi, yi, ci = _place()
    chip = (2 * xi + yi).astype(I32)
    core = ci.astype(I32)

    w_in_rows, m_w_in_rows, v_w_in_rows = w_in.T, m_w_in.T, v_w_in.T
    shards = (w_in_rows.astype(BF16), w_o.astype(BF16), w_up.astype(BF16), w_down.astype(BF16))
    w_in_t, conv_w4 = _gather_w_in(shards[0], conv_w)
    conv_w_f = conv_w4.transpose(1, 0, 2).reshape(3, 2 * FF)
    w_started, w_tok = _weights_start(shards[1:], conv_w4)
    slopes = jnp.asarray(SLOPES, F32) + w_tok[0, 0]

    halves_rows = [r // 2 for r in SHARD_ROWS]
    grads4, grads_b4, started = [None] * 4, [None] * 4, [None] * 4

    def on_grad(k, g, g_b):
        grads4[k] = g.reshape(N_CHIPS, 2, halves_rows[k], D)
        grads_b4[k] = g_b.reshape(N_CHIPS, 2, halves_rows[k], D)
        if k > 1:
            return None
        group = (1, 2, 3) if k == 1 else (0,)
        sts, tok = _grads_start([grads_b4[i] for i in group], f"grads_start_{k}")
        for i, st in zip(group, sts):
            started[i] = st
        return tok

    gx, small = _local_step(
        x[0], loss_target[0], w_in_t, lambda k, after: _weights_wait(w_started[k - 1], after, f"weights_wait_{k}"),
        norm_a_g, norm_b_g, sinks_a, ln1_g, ln1_b, conv_w_f, conv_b, ln2_g, ln2_b, slopes, on_grad)

    tiles = (96, 128, 352, 176)
    core_chip = jnp.stack([core, chip])
    got = _grads_wait(started[1:], gx, "grads_wait_1")
    halves = [_sum_partials(grads4[k], got[k - 1], core_chip, f"sum_partials_{k}", tiles[k]) for k in (1, 2, 3)]
    g_w_o, g_w_up_rows, g_w_down = _swap_halves(halves, "swap_halves")
    g_w_up = g_w_up_rows.T
    delta, new_m, new_v = {}, {}, {}
    for k, g, tr in (("w_o", g_w_o, 128), ("w_up", g_w_up, 256), ("w_down", g_w_down, 176)):
        delta[k], new_m[k], new_v[k] = _adamw(dict(w_o=w_o, w_up=w_up, w_down=w_down)[k], g,
                                              dict(w_o=m_w_o, w_up=m_w_up, w_down=m_w_down)[k],
                                              dict(w_o=v_w_o, w_up=v_w_up, w_down=v_w_down)[k], f"adamw_{k}", tr)

    got = _grads_wait(started[:1], delta["w_up"], "grads_wait_0")
    half_in = _sum_partials(grads4[0], got[0], core_chip, "sum_partials_0", tiles[0])
    small_rows = 32
    g_w_in_rows, totals = _share_halves([half_in], _pack({k: small[k] for k in SMALL_ORDER}, small_rows))
    tot = _unpack(totals, SMALL_ORDER, SMALL_SIZES)
    loss = tot["loss"][0]
    cols = 2 * FF // N_CHIPS
    g_conv_w = lax.dynamic_slice(tot["conv_w"].reshape(3, 2 * FF), (0, chip * cols), (3, cols))
    g_small = dict(norm_a_g=tot["norm_a_g"], norm_b_g=tot["norm_b_g"], sinks_a=tot["sinks_a"], ln1_g=tot["ln1_g"],
                   ln1_b=tot["ln1_b"], conv_w=g_conv_w, conv_b=tot["conv_b"], ln2_g=tot["ln2_g"], ln2_b=tot["ln2_b"])

    weights = dict(w_in=w_in, norm_a_g=norm_a_g, norm_b_g=norm_b_g, sinks_a=sinks_a, w_o=w_o, ln1_g=ln1_g, ln1_b=ln1_b,
                   w_up=w_up, conv_w=conv_w, conv_b=conv_b, w_down=w_down, ln2_g=ln2_g, ln2_b=ln2_b)
    ms = dict(w_in=m_w_in, norm_a_g=m_norm_a_g, norm_b_g=m_norm_b_g, sinks_a=m_sinks_a, w_o=m_w_o, ln1_g=m_ln1_g,
              ln1_b=m_ln1_b, w_up=m_w_up, conv_w=m_conv_w, conv_b=m_conv_b, w_down=m_w_down, ln2_g=m_ln2_g, ln2_b=m_ln2_b)
    vs = dict(w_in=v_w_in, norm_a_g=v_norm_a_g, norm_b_g=v_norm_b_g, sinks_a=v_sinks_a, w_o=v_w_o, ln1_g=v_ln1_g,
              ln1_b=v_ln1_b, w_up=v_w_up, conv_w=v_conv_w, conv_b=v_conv_b, w_down=v_w_down, ln2_g=v_ln2_g, ln2_b=v_ln2_b)
    order = list(weights)
    grad = dict(g_small, w_in=g_w_in_rows.T, w_o=g_w_o, w_up=g_w_up, w_down=g_w_down)

    delta["w_in"], new_m["w_in"], new_v["w_in"] = [
        a.T for a in _adamw(w_in_rows, g_w_in_rows, m_w_in_rows, v_w_in_rows, "adamw_w_in", 144)]
    small_names = [k for k in order if k not in delta]
    sizes = {k: weights[k].size for k in small_names}
    rows = 16
    packed = [_pack({k: src[k] for k in small_names}, rows) for src in (weights, grad, ms, vs)]
    for res, buf in zip((delta, new_m, new_v), _adamw(*packed, "adamw_small", rows)):
        for k, val in _unpack(buf, small_names, sizes).items():
            res[k] = val.reshape(weights[k].shape)

    return (loss, gx[None], *[grad[k] for k in order], *[delta[k] for k in order],
            *[new_m[k] for k in order], *[new_v[k] for k in order])
```

```python
import functools
import math

import jax
import jax.numpy as jnp
from jax import lax
from jax.experimental import pallas as pl
from jax.experimental.pallas import tpu as pltpu

F32, BF16, I32 = jnp.float32, jnp.bfloat16, jnp.int32

D = 1024
FF = 2816
HD = 64
NH = 8
WA, WB = 768, 1536
WIN = WA + WB
BLK = 128
ALPHA = 2.0 ** 0.25
LN_EPS, RMS_EPS = 1e-5, 1e-6
SCALE = 1.0 / math.sqrt(HD)
A_MAX_DIST, B_MAX_DIST = 127, 128
B_DILATIONS = (1, 4, 16)
SLOPES = tuple(2.0 ** (-(i + 1)) for i in range(NH))
SHARD_ROWS = (WIN // 4, D // 4, 2 * FF // 4, FF // 4)
N_CHIPS = 4
ADAM_LR, ADAM_B1, ADAM_B2, ADAM_EPS, ADAM_WD, ADAM_STEP = 0.001, 0.9, 0.999, 1e-08, 0.01, 10
MESH = pl.DeviceIdType.MESH
ANY = pl.BlockSpec(memory_space=pl.ANY)
SMEM = pl.BlockSpec(memory_space=pltpu.SMEM)
VMEM = pl.BlockSpec(memory_space=pltpu.VMEM)
HBM = pl.BlockSpec(memory_space=pltpu.HBM)
SEM = pl.BlockSpec(memory_space=pltpu.SEMAPHORE)
DATAFLOW = pltpu.SideEffectType.DATAFLOW_SIDE_EFFECTING


def _cp(sem, mb=48):
    return pltpu.CompilerParams(dimension_semantics=sem, vmem_limit_bytes=mb << 20)


def _nn(a, b):
    return lax.dot_general(a, b, (((1,), (0,)), ((), ())), preferred_element_type=F32)


def _nt(a, b):
    return lax.dot_general(a, b, (((1,), (1,)), ((), ())), preferred_element_type=F32)


def _tn(a, b):
    return lax.dot_general(a, b, (((0,), (0,)), ((), ())), preferred_element_type=F32)


def _resident(shape):
    n = len(shape)
    return pl.BlockSpec(shape, lambda *_: (0,) * n, pipeline_mode=pl.Buffered(1))


def _const(shape):
    n = len(shape)
    return pl.BlockSpec(shape, lambda *_: (0,) * n)


def _proj(x, w_t, name, tm=512):
    s = x.shape[0]
    n = w_t.shape[0]

    def body(x_ref, w_ref, o_ref, xb_ref):
        xb = x_ref[...].astype(BF16)
        xb_ref[...] = xb
        res = _nt(xb, w_ref[...])
        for g in range(n // 128):
            o_ref[g] = res[:, 128 * g:128 * (g + 1)]

    return pl.pallas_call(
        body, name=name, grid=(s // tm,),
        in_specs=[pl.BlockSpec((tm, D), lambda i: (i, 0)), _resident((n, D))],
        out_specs=[pl.BlockSpec((n // 128, tm, 128), lambda i: (0, i, 0)), pl.BlockSpec((tm, D), lambda i: (i, 0))],
        out_shape=[jax.ShapeDtypeStruct((n // 128, s, 128), F32), jax.ShapeDtypeStruct((s, D), BF16)],
        compiler_params=_cp(("parallel",)),
    )(x, w_t)


def _grad_w(lhs, rhs, name, tm, tk=512, lhs_halves=False):
    s = rhs.shape[0]
    if lhs_halves:
        per_half = lhs.shape[2] // tm
        n = 2 * lhs.shape[2]
        lhs_spec = pl.BlockSpec((None, tk, tm), lambda i, k: (i // per_half, k, i % per_half))
    else:
        n = lhs.shape[1]
        lhs_spec = pl.BlockSpec((tk, tm), lambda i, k: (k, i))
    nk = s // tk

    def body(l_ref, r_ref, o_ref, ob_ref):
        k = pl.program_id(1)

        @pl.when(k == 0)
        def _():
            o_ref[...] = jnp.zeros_like(o_ref)

        o_ref[...] += _tn(l_ref[...].astype(BF16), r_ref[...].astype(BF16))

        @pl.when(k == nk - 1)
        def _():
            ob_ref[...] = o_ref[...].astype(BF16)

    return pl.pallas_call(
        body, name=name, grid=(n // tm, nk),
        in_specs=[lhs_spec, pl.BlockSpec((tk, D), lambda i, k: (k, 0))],
        out_specs=[pl.BlockSpec((tm, D), lambda i, k: (i, 0))] * 2,
        out_shape=[jax.ShapeDtypeStruct((n, D), F32), jax.ShapeDtypeStruct((n, D), BF16)],
        compiler_params=_cp(("parallel", "arbitrary")),
    )(lhs, rhs)


def _band_base(max_dist, dist_unit, first):
    row = lax.broadcasted_iota(I32, (BLK, 2 * BLK), 0)
    col = lax.broadcasted_iota(I32, (BLK, 2 * BLK), 1)
    dist = BLK + row - col
    ok = (dist >= 0) & (dist <= max_dist)
    if first:
        ok = ok & (col >= BLK)
    return jnp.where(ok, dist.astype(F32) * (-float(dist_unit)), -jnp.inf)


def _half_mask(shape, e):
    lane = lax.broadcasted_iota(I32, shape, 1)
    return (lane < HD) if e == 0 else (lane >= HD)


def _to_half(x, e, g):
    if g != e:
        x = pltpu.roll(x, HD, 1)
    return jnp.where(_half_mask(x.shape, g), x, 0.0)


def _stack_heads(scalars, tile):
    return jnp.concatenate([scalars[0] * tile, scalars[1] * tile], axis=0)


def _pair_fwd(q2, kb, vb, base, slopes, kv_heads, sinks):
    lo = _half_mask((BLK, 2 * HD), 0)
    if sinks is not None:
        o2 = lse2 = None
        for e in (0, 1):
            g = kv_heads[e]
            qv = (_to_half(q2, e, g) * SCALE).astype(BF16)
            s = _nt(qv, kb) + slopes[e] * base
            m = jnp.maximum(jnp.max(s, axis=1, keepdims=True), sinks[e])
            p = jnp.exp(s - m)
            l = jnp.sum(p, axis=1, keepdims=True) + jnp.exp(sinks[e] - m)
            oh = _nn(p.astype(BF16), vb) / l
            if g != e:
                oh = pltpu.roll(oh, HD, 1)
            lse = jnp.broadcast_to(m + jnp.log(l), (BLK, 2 * HD))
            o2 = oh if e == 0 else jnp.where(lo, o2, oh)
            lse2 = lse if e == 0 else jnp.where(lo, lse2, lse)
        return o2, lse2
    qs = jnp.concatenate([_to_half(q2, e, kv_heads[e]) * SCALE for e in (0, 1)], axis=0).astype(BF16)
    s = _nt(qs, kb) + (base if slopes is None else _stack_heads(slopes, base))
    m = jnp.max(s, axis=1, keepdims=True)
    p = jnp.exp(s - m)
    l = jnp.sum(p, axis=1, keepdims=True)
    o = _nn(p.astype(BF16), vb) / l
    lse = m + jnp.log(l)
    halves = []
    for e in (0, 1):
        oh = o[e * BLK:(e + 1) * BLK]
        halves.append(pltpu.roll(oh, HD, 1) if kv_heads[e] != e else oh)
    o2 = jnp.where(lo, halves[0], halves[1])
    lse2 = jnp.where(lo, jnp.broadcast_to(lse[:BLK], (BLK, 2 * HD)), jnp.broadcast_to(lse[BLK:], (BLK, 2 * HD)))
    return o2, lse2


def _pair_bwd(q2, kb, vb, do2, o2, lse2, base, slopes, kv_heads, sinks):
    lo = _half_mask((BLK, 2 * HD), 0)
    prod = do2 * o2
    lses, deltas = [], []
    for e in (0, 1):
        hq = _half_mask((BLK, 2 * HD), e)
        lses.append(jnp.max(jnp.where(hq, lse2, -jnp.inf), axis=1, keepdims=True))
        deltas.append(jnp.sum(jnp.where(hq, prod, 0.0), axis=1, keepdims=True))
    lse = jnp.concatenate(lses, axis=0)
    delta = jnp.concatenate(deltas, axis=0)
    qs = jnp.concatenate([_to_half(q2, e, kv_heads[e]) * SCALE for e in (0, 1)], axis=0).astype(BF16)
    dos = jnp.concatenate([_to_half(do2, e, kv_heads[e]) for e in (0, 1)], axis=0).astype(BF16)
    p = jnp.exp(_nt(qs, kb) + (base if slopes is None else _stack_heads(slopes, base)) - lse)
    ds = (p * (_nt(dos, vb) - delta)).astype(BF16)
    dq = _nn(ds, kb) * SCALE
    halves = []
    for e in (0, 1):
        dqh = dq[e * BLK:(e + 1) * BLK]
        halves.append(pltpu.roll(dqh, HD, 1) if kv_heads[e] != e else dqh)
    dq2 = jnp.where(lo, halves[0], halves[1])
    dk2 = _tn(ds, qs)
    dv2 = _tn(p.astype(BF16), dos)
    dsinks = []
    if sinks is not None:
        for e in (0, 1):
            dsinks.append(jnp.sum(-jnp.exp(sinks[e] - lses[e]) * deltas[e], axis=0, keepdims=True))
    return dq2, dk2, dv2, dsinks


A_BLOCKS_PER_STEP = 2
A_BLOCKS_PER_STEP_BWD = 1


def _attn_a_fwd(proj, sinks):
    s = proj.shape[1]
    nq = A_BLOCKS_PER_STEP
    rows = BLK * nq
    steps = s // rows

    def body(sink_ref, q_ref, kp_ref, kc_ref, vp_ref, vc_ref, o_ref, lse_ref):
        n = pl.program_id(0)
        base_rest = _band_base(A_MAX_DIST, 1, False)
        base_0 = jnp.where(n > 0, base_rest, _band_base(A_MAX_DIST, 1, True))
        for i in range(nq):
            cur = pl.ds(i * BLK, BLK)
            k_prev = kc_ref[pl.ds((i - 1) * BLK, BLK), :] if i > 0 else kp_ref[...]
            v_prev = vc_ref[pl.ds((i - 1) * BLK, BLK), :] if i > 0 else vp_ref[...]
            kb = jnp.concatenate([k_prev, kc_ref[cur, :]], axis=0).astype(BF16)
            vb = jnp.concatenate([v_prev, vc_ref[cur, :]], axis=0).astype(BF16)
            for j in range(NH // 2):
                g = j // 2
                o2, lse2 = _pair_fwd(q_ref[j, cur, :], kb, vb, base_rest if i > 0 else base_0,
                                     (SLOPES[2 * j], SLOPES[2 * j + 1]), (g, g), (sink_ref[2 * j], sink_ref[2 * j + 1]))
                o_ref[j, cur, :] = o2
                lse_ref[j, cur, :] = lse2

    before = lambda n: jnp.maximum(n * nq - 1, 0)
    slab = lambda g: pl.BlockSpec((None, rows, 128), lambda n: (g, n, 0))
    edge = lambda g: pl.BlockSpec((None, BLK, 128), lambda n: (g, before(n), 0))
    quad = pl.BlockSpec((4, rows, 128), lambda n: (0, n, 0))
    return pl.pallas_call(
        body, name="attn_a_fwd", grid=(steps,),
        in_specs=[SMEM, quad, edge(4), slab(4), edge(5), slab(5)],
        out_specs=[quad, quad],
        out_shape=[jax.ShapeDtypeStruct((4, s, 128), F32)] * 2,
        compiler_params=_cp(("parallel",)),
    )(sinks, proj, proj, proj, proj, proj)


def _attn_a_bwd(proj, sinks, d_o, o, lse):
    s = proj.shape[1]
    nq = A_BLOCKS_PER_STEP_BWD
    rows = BLK * nq
    steps = s // rows

    def body(sink_ref, q_ref, kp_ref, kc_ref, vp_ref, vc_ref, do_ref, o_ref, lse_ref,
             dq_ref, dk_ref, dv_ref, dsink_ref, kcar, vcar):
        n = pl.program_id(0)

        @pl.when(n == 0)
        def _():
            kcar[...] = jnp.zeros_like(kcar)
            vcar[...] = jnp.zeros_like(vcar)
            dsink_ref[...] = jnp.zeros_like(dsink_ref)

        dk_ref[...] = kcar[...]
        dv_ref[...] = vcar[...]

        @pl.when(n < steps)
        def _():
            base_rest = _band_base(A_MAX_DIST, 1, False)
            base_0 = jnp.where(n > 0, base_rest, _band_base(A_MAX_DIST, 1, True))
            for i in range(nq):
                cur = pl.ds(i * BLK, BLK)
                k_prev = kc_ref[pl.ds((i - 1) * BLK, BLK), :] if i > 0 else kp_ref[...]
                v_prev = vc_ref[pl.ds((i - 1) * BLK, BLK), :] if i > 0 else vp_ref[...]
                kb = jnp.concatenate([k_prev, kc_ref[cur, :]], axis=0).astype(BF16)
                vb = jnp.concatenate([v_prev, vc_ref[cur, :]], axis=0).astype(BF16)
                dk_win = dv_win = None
                for j in range(NH // 2):
                    g = j // 2
                    dq2, dk2, dv2, dsk = _pair_bwd(q_ref[j, cur, :], kb, vb, do_ref[j, cur, :], o_ref[j, cur, :],
                                                   lse_ref[j, cur, :], base_rest if i > 0 else base_0,
                                                   (SLOPES[2 * j], SLOPES[2 * j + 1]), (g, g),
                                                   (sink_ref[2 * j], sink_ref[2 * j + 1]))
                    dq_ref[j, cur, :] = dq2
                    dk_win = dk2 if j == 0 else dk_win + dk2
                    dv_win = dv2 if j == 0 else dv_win + dv2
                    for e in (0, 1):
                        h = 2 * j + e
                        dsink_ref[h:h + 1, :] += jnp.broadcast_to(dsk[e], (1, 128))
                if i == 0:
                    last = pl.ds((nq - 1) * BLK, BLK)
                    dk_ref[last, :] += dk_win[:BLK]
                    dv_ref[last, :] += dv_win[:BLK]
                else:
                    kcar[pl.ds((i - 1) * BLK, BLK), :] += dk_win[:BLK]
                    vcar[pl.ds((i - 1) * BLK, BLK), :] += dv_win[:BLK]
                kcar[cur, :] = dk_win[BLK:]
                vcar[cur, :] = dv_win[BLK:]

    cur_step = lambda n: jnp.minimum(n, steps - 1)
    before = lambda n: jnp.maximum(cur_step(n) * nq - 1, 0)
    out_prev = lambda n: jnp.maximum(n - 1, 0)
    quad = pl.BlockSpec((4, rows, 128), lambda n: (0, cur_step(n), 0))
    slab = lambda g: pl.BlockSpec((None, rows, 128), lambda n: (g, cur_step(n), 0))
    edge = lambda g: pl.BlockSpec((None, BLK, 128), lambda n: (g, before(n), 0))
    return pl.pallas_call(
        body, name="attn_a_bwd", grid=(steps + 1,),
        in_specs=[SMEM, quad, edge(4), slab(4), edge(5), slab(5), quad, quad, quad],
        out_specs=[quad,
                   pl.BlockSpec((rows, 128), lambda n: (out_prev(n), 0)),
                   pl.BlockSpec((rows, 128), lambda n: (out_prev(n), 0)),
                   pl.BlockSpec((NH, 128), lambda n: (0, 0))],
        out_shape=[jax.ShapeDtypeStruct((4, s, 128), F32), jax.ShapeDtypeStruct((s, 128), F32),
                   jax.ShapeDtypeStruct((s, 128), F32), jax.ShapeDtypeStruct((NH, 128), F32)],
        scratch_shapes=[pltpu.VMEM((rows, 128), F32), pltpu.VMEM((rows, 128), F32)],
        compiler_params=_cp(("arbitrary",)),
    )(sinks, proj, proj, proj, proj, proj, d_o, o, lse)


def _stream(rho, i, r):
    start = i * BLK * r + rho
    return pl.ds(start, BLK, stride=r) if r > 1 else pl.ds(start, BLK)


def _for_streams(r, fn, side_by_side=4):
    if r <= side_by_side:
        for rho in range(r):
            fn(rho)
    else:
        def group(it, carry):
            for u in range(side_by_side):
                fn(side_by_side * it + u)
            return carry

        lax.fori_loop(0, r // side_by_side, group, 0)


B_BLOCKS_PER_STEP = {1: 8, 4: 2, 16: 1}
B_BLOCKS_PER_STEP_FWD = {1: 8, 4: 2, 16: 1}


def _attn_b_fwd(proj, slopes, r):
    s = proj.shape[1]
    nq = B_BLOCKS_PER_STEP_FWD[r]
    rows = BLK * r * nq
    steps = s // rows
    qc, kc, vc = WA // 128, WA // 128 + 4, WA // 128 + 8

    def body(slope_ref, q_ref, kp_ref, kc_ref, vp_ref, vc_ref, o_ref, lse_ref):
        j = pl.program_id(0)
        sb = pl.program_id(1)
        sl2 = (slope_ref[2 * j], slope_ref[2 * j + 1])
        bias_rest = _stack_heads(sl2, _band_base(B_MAX_DIST, r, False))
        bias_0 = jnp.where(sb > 0, bias_rest, _stack_heads(sl2, _band_base(B_MAX_DIST, r, True)))

        def stream(rho):
            for i in range(nq):
                cur = _stream(rho, i, r)
                k_prev = kc_ref[_stream(rho, i - 1, r), :] if i > 0 else kp_ref[_stream(rho, 0, r), :]
                v_prev = vc_ref[_stream(rho, i - 1, r), :] if i > 0 else vp_ref[_stream(rho, 0, r), :]
                kb = jnp.concatenate([k_prev, kc_ref[cur, :]], axis=0).astype(BF16)
                vb = jnp.concatenate([v_prev, vc_ref[cur, :]], axis=0).astype(BF16)
                o2, lse2 = _pair_fwd(q_ref[cur, :], kb, vb, bias_rest if i > 0 else bias_0, None, (0, 1), None)
                o_ref[cur, :] = o2
                lse_ref[cur, :] = lse2

        _for_streams(r, stream, side_by_side=8)

    before = lambda sb: jnp.maximum(sb * nq - 1, 0)
    return pl.pallas_call(
        body, name=f"attn_b_fwd_r{r}", grid=(NH // 2, steps),
        in_specs=[SMEM,
                  pl.BlockSpec((None, rows, 128), lambda j, sb: (qc + j, sb, 0)),
                  pl.BlockSpec((None, BLK * r, 128), lambda j, sb: (kc + j, before(sb), 0)),
                  pl.BlockSpec((None, rows, 128), lambda j, sb: (kc + j, sb, 0)),
                  pl.BlockSpec((None, BLK * r, 128), lambda j, sb: (vc + j, before(sb), 0)),
                  pl.BlockSpec((None, rows, 128), lambda j, sb: (vc + j, sb, 0))],
        out_specs=[pl.BlockSpec((None, rows, 128), lambda j, sb: (j, sb, 0))] * 2,
        out_shape=[jax.ShapeDtypeStruct((4, s, 128), F32)] * 2,
        compiler_params=_cp(("parallel", "parallel")),
    )(slopes, proj, proj, proj, proj, proj)


def _attn_b_bwd(proj, slopes, d_o, o, lse, r, so_far=None):
    s = proj.shape[1]
    nq = B_BLOCKS_PER_STEP[r]
    rows = BLK * r * nq
    steps = s // rows
    qc, kc, vc = WA // 128, WA // 128 + 4, WA // 128 + 8
    chained = so_far is not None

    def body(slope_ref, q_ref, kp_ref, kc_ref, vp_ref, vc_ref, do_ref, o_ref, lse_ref, *rest):
        if chained:
            pq_ref, pk_ref, pv_ref, dq_ref, dk_ref, dv_ref, kcar, vcar = rest
        else:
            dq_ref, dk_ref, dv_ref, kcar, vcar = rest
        j = pl.program_id(0)
        sb = pl.program_id(1)

        @pl.when(sb == 0)
        def _():
            kcar[...] = jnp.zeros_like(kcar)
            vcar[...] = jnp.zeros_like(vcar)

        if chained:
            dk_ref[...] = kcar[...] + pk_ref[...]
            dv_ref[...] = vcar[...] + pv_ref[...]
        else:
            dk_ref[...] = kcar[...]
            dv_ref[...] = vcar[...]

        @pl.when(sb < steps)
        def _():
            sl2 = (slope_ref[2 * j], slope_ref[2 * j + 1])
            bias_rest = _stack_heads(sl2, _band_base(B_MAX_DIST, r, False))
            bias_0 = jnp.where(sb > 0, bias_rest, _stack_heads(sl2, _band_base(B_MAX_DIST, r, True)))

            def stream(rho):
                for i in range(nq):
                    cur = _stream(rho, i, r)
                    k_prev = kc_ref[_stream(rho, i - 1, r), :] if i > 0 else kp_ref[_stream(rho, 0, r), :]
                    v_prev = vc_ref[_stream(rho, i - 1, r), :] if i > 0 else vp_ref[_stream(rho, 0, r), :]
                    kb = jnp.concatenate([k_prev, kc_ref[cur, :]], axis=0).astype(BF16)
                    vb = jnp.concatenate([v_prev, vc_ref[cur, :]], axis=0).astype(BF16)
                    dq2, dk2, dv2, _ = _pair_bwd(q_ref[cur, :], kb, vb, do_ref[cur, :], o_ref[cur, :], lse_ref[cur, :],
                                                 bias_rest if i > 0 else bias_0, None, (0, 1), None)
                    dq_ref[cur, :] = dq2 + pq_ref[cur, :] if chained else dq2
                    if i == 0:
                        last = _stream(rho, nq - 1, r)
                        dk_ref[last, :] += dk2[:BLK]
                        dv_ref[last, :] += dv2[:BLK]
                    else:
                        kcar[_stream(rho, i - 1, r), :] += dk2[:BLK]
                        vcar[_stream(rho, i - 1, r), :] += dv2[:BLK]
                    kcar[cur, :] = dk2[BLK:]
                    vcar[cur, :] = dv2[BLK:]

            _for_streams(r, stream, side_by_side=8)

    cur_step = lambda sb: jnp.minimum(sb, steps - 1)
    before = lambda sb: jnp.maximum(cur_step(sb) * nq - 1, 0)
    out_prev = lambda sb: jnp.maximum(sb - 1, 0)
    tile = lambda slab: pl.BlockSpec((None, rows, 128), lambda j, sb: (slab + j, cur_step(sb), 0))
    edge = lambda slab: pl.BlockSpec((None, BLK * r, 128), lambda j, sb: (slab + j, before(sb), 0))
    late = pl.BlockSpec((None, rows, 128), lambda j, sb: (j, out_prev(sb), 0))
    grads = [tile(0), late, late]
    return pl.pallas_call(
        body, name=f"attn_b_bwd_r{r}", grid=(NH // 2, steps + 1),
        in_specs=[SMEM, tile(qc), edge(kc), tile(kc), edge(vc), tile(vc), tile(0), tile(0), tile(0)]
        + (grads if chained else []),
        out_specs=grads,
        out_shape=[jax.ShapeDtypeStruct((4, s, 128), F32)] * 3,
        scratch_shapes=[pltpu.VMEM((rows, 128), F32), pltpu.VMEM((rows, 128), F32)],
        compiler_params=_cp(("parallel", "arbitrary")),
    )(slopes, proj, proj, proj, proj, proj, d_o, o, lse, *(so_far if chained else ()))


def _row(v):
    return v.reshape(1, -1)


def _layer_norm_stats(z):
    mu = jnp.mean(z, axis=-1, keepdims=True)
    zc = z - mu
    var = jnp.mean(zc * zc, axis=-1, keepdims=True)
    rstd = lax.rsqrt(var + LN_EPS)
    return zc * rstd, rstd


def _layer_norm_bwd(dh, zh, rstd, g):
    dzh = dh * g
    return rstd * (dzh - jnp.mean(dzh, axis=-1, keepdims=True) - zh * jnp.mean(dzh * zh, axis=-1, keepdims=True))


def _rms(o):
    return lax.rsqrt(jnp.mean(o * o, axis=-1, keepdims=True) + RMS_EPS)


def _mix_ln1(x, o_a, o_b, lse_b, norm_a_g, norm_b_g, w_o, ln1_g, ln1_b, tm=256):
    s = x.shape[0]

    def wide(ref):
        return jnp.concatenate([ref[j] for j in range(4)], axis=1)

    def body(x_ref, oa_ref, ob1, ob2, ob3, l1, l2, l3, ga_ref, gb_ref, wo_ref, g_ref, b_ref,
             obm_ref, lse_ref, cat_ref, z1_ref, h1_ref, h1b_ref):
        la, lb, lc = wide(l1), wide(l2), wide(l3)
        m = jnp.maximum(jnp.maximum(la, lb), lc)
        ea, eb, ec = jnp.exp(la - m), jnp.exp(lb - m), jnp.exp(lc - m)
        den = ea + eb + ec
        obm = (ea / den) * wide(ob1) + (eb / den) * wide(ob2) + (ec / den) * wide(ob3)
        lse = m + jnp.log(den)
        for j in range(4):
            obm_ref[j] = obm[:, 128 * j:128 * (j + 1)]
            lse_ref[j] = lse[:, 128 * j:128 * (j + 1)]
        oa = wide(oa_ref)
        na = oa * _rms(oa) * ga_ref[...]
        nb_ = obm * _rms(obm) * gb_ref[...]
        cat = jnp.concatenate([na, nb_], axis=1).astype(BF16)
        cat_ref[...] = cat
        z1 = ALPHA * x_ref[...] + _nn(cat, wo_ref[...])
        z1_ref[...] = z1
        zh, _ = _layer_norm_stats(z1)
        h1 = zh * g_ref[...] + b_ref[...]
        h1_ref[...] = h1
        h1b_ref[...] = h1.astype(BF16)

    t512 = pl.BlockSpec((4, tm, 128), lambda i: (0, i, 0))
    td = pl.BlockSpec((tm, D), lambda i: (i, 0))
    return pl.pallas_call(
        body, name="mix_ln1", grid=(s // tm,),
        in_specs=[td] + [t512] * 7 + [_const((1, 512))] * 2 + [_resident((D, D))] + [_const((1, D))] * 2,
        out_specs=[t512, t512, td, td, td, td],
        out_shape=[jax.ShapeDtypeStruct((4, s, 128), F32), jax.ShapeDtypeStruct((4, s, 128), F32),
                   jax.ShapeDtypeStruct((s, D), BF16), jax.ShapeDtypeStruct((s, D), F32),
                   jax.ShapeDtypeStruct((s, D), F32), jax.ShapeDtypeStruct((s, D), BF16)],
        compiler_params=_cp(("parallel",)),
    )(x, o_a, *o_b, *lse_b, _row(norm_a_g), _row(norm_b_g), w_o, _row(ln1_g), _row(ln1_b))


def _gelu_and_grad(x):
    c = math.sqrt(2.0 / math.pi)
    x2 = x * x
    cx = c * x
    t = jnp.tanh(cx * (1.0 + 0.044715 * x2))
    q = 1.0 + t
    g = (0.5 * x) * q
    dg = 0.5 * q + ((0.5 * cx) * (1.0 - t * t)) * (1.0 + (3.0 * 0.044715) * x2)
    return g, dg


def _shift_down(u, before):
    n = u.shape[0]
    ext = jnp.concatenate([before, u], axis=0)
    return pltpu.roll(ext, 1, 0)[8:], pltpu.roll(ext, 2, 0)[8:]


def _shift_up(u, after):
    n = u.shape[0]
    ext = jnp.concatenate([u, after], axis=0)
    return pltpu.roll(ext, n + 7, 0)[:n], pltpu.roll(ext, n + 6, 0)[:n]


def _up_conv_gelu(h1b, w_up, cwb, tm=256, tn=FF // 2, chunk_rows=16, piece_cols=512):
    s = h1b.shape[0]
    n_i = s // tm
    n_t = (FF // tn) * n_i

    def body(h_ref, wg_ref, wv_ref, c_ref, up_ref, a_ref, g_ref, a1_ref, pend_a, pend_b, carry):
        t = pl.program_id(0)
        row_tile = jnp.maximum(t - 1, 0) % n_i
        w_refs = (wg_ref, wv_ref)

        @pl.when(t == 0)
        def _():
            pend_b[...] = jnp.zeros_like(pend_b)
            carry[...] = jnp.zeros_like(carry)

        def step(dst, src):
            def chunk(c, before):
                rows = pl.ds(c * chunk_rows, chunk_rows)
                u, last = [], []
                for half in (0, 1):
                    up = src[half, rows, :]
                    r1, r2 = _shift_down(up, before[half])
                    u.append(r2 * c_ref[0, half:half + 1, :] + r1 * c_ref[1, half:half + 1, :]
                             + up * c_ref[2, half:half + 1, :] + c_ref[3, half:half + 1, :])
                    last.append(up[chunk_rows - 8:])
                g, dg = _gelu_and_grad(u[0])
                a_ref[rows, :] = (g * u[1]).astype(BF16)
                g_ref[rows, :] = g.astype(BF16)
                a1_ref[rows, :] = (u[1] * dg).astype(BF16)
                return tuple(last)

            edge = tuple(jnp.where(row_tile > 0, carry[half], 0.0) for half in (0, 1))
            pieces = [(half, c0, min(piece_cols, tn - c0)) for half in (0, 1) for c0 in range(0, tn, piece_cols)]
            n_c = tm // chunk_rows
            done = 0
            for p, (half, c0, width) in enumerate(pieces):
                cols = slice(c0, c0 + width)
                up = _nn(h_ref[...], w_refs[half][:, cols])
                up_ref[half, :, cols] = up.astype(BF16)
                dst[half, :, cols] = up
                upto = n_c * (p + 1) // len(pieces)
                for c in range(done, upto):
                    edge = chunk(c, edge)
                done = upto
            for half in (0, 1):
                carry[half] = edge[half]

        @pl.when(t % 2 == 0)
        def _():
            step(pend_a, pend_b)

        @pl.when(t % 2 == 1)
        def _():
            step(pend_b, pend_a)

    mm = lambda t: jnp.minimum(t, n_t - 1)
    ew = lambda t: jnp.maximum(t - 1, 0)
    out_tile = pl.BlockSpec((tm, tn), lambda t: (ew(t) % n_i, ew(t) // n_i))
    return pl.pallas_call(
        body, name="up_conv_gelu", grid=(n_t + 1,),
        in_specs=[pl.BlockSpec((tm, D), lambda t: (mm(t) % n_i, 0)),
                  pl.BlockSpec((D, tn), lambda t: (0, mm(t) // n_i)),
                  pl.BlockSpec((D, tn), lambda t: (0, FF // tn + mm(t) // n_i)),
                  pl.BlockSpec((4, 2, tn), lambda t: (0, 0, ew(t) // n_i))],
        out_specs=[pl.BlockSpec((2, tm, tn), lambda t: (0, mm(t) % n_i, mm(t) // n_i)), out_tile, out_tile, out_tile],
        out_shape=[jax.ShapeDtypeStruct((2, s, FF), BF16)] + [jax.ShapeDtypeStruct((s, FF), BF16)] * 3,
        scratch_shapes=[pltpu.VMEM((2, tm, tn), F32), pltpu.VMEM((2, tm, tn), F32), pltpu.VMEM((2, 8, tn), F32)],
        compiler_params=_cp(("arbitrary",)),
    )(h1b, w_up, w_up, cwb)


def _down_ln2_loss(a, w_down, h1, target, ln2_g, ln2_b, tm=256):
    s = a.shape[0]

    def body(a_ref, w_ref, h_ref, t_ref, g_ref, b_ref, dz_ref, dzb_ref, st_ref):
        @pl.when(pl.program_id(0) == 0)
        def _():
            st_ref[...] = jnp.zeros_like(st_ref)

        z2 = ALPHA * h_ref[...] + _nn(a_ref[...], w_ref[...])
        zh, rstd = _layer_norm_stats(z2)
        diff = zh * g_ref[...] + b_ref[...] - t_ref[...]
        part = 0.5 * jnp.sum(jnp.mean(diff * diff, axis=-1, keepdims=True), axis=0, keepdims=True)
        dy = diff * (1.0 / D)
        st_ref[0:1, :] += jnp.sum(dy * zh, axis=0, keepdims=True)
        st_ref[1:2, :] += jnp.sum(dy, axis=0, keepdims=True)
        st_ref[2:3, :] += jnp.broadcast_to(part, (1, D))
        dz = _layer_norm_bwd(dy, zh, rstd, g_ref[...])
        dz_ref[...] = dz
        dzb_ref[...] = dz.astype(BF16)

    td = pl.BlockSpec((tm, D), lambda i: (i, 0))
    return pl.pallas_call(
        body, name="down_ln2_loss", grid=(s // tm,),
        in_specs=[pl.BlockSpec((tm, FF), lambda i: (i, 0)), _resident((FF, D)), td, td, _const((1, D)), _const((1, D))],
        out_specs=[td, td, _const((8, D))],
        out_shape=[jax.ShapeDtypeStruct((s, D), F32), jax.ShapeDtypeStruct((s, D), BF16),
                   jax.ShapeDtypeStruct((8, D), F32)],
        compiler_params=_cp(("arbitrary",)),
    )(a, w_down, h1, target, _row(ln2_g), _row(ln2_b))


def _d_act(dz2b, w_down, tm=512):
    s = dz2b.shape[0]

    def body(dz_ref, w_ref, o_ref):
        o_ref[...] = _nt(dz_ref[...], w_ref[...])

    return pl.pallas_call(
        body, name="d_act", grid=(s // tm,),
        in_specs=[pl.BlockSpec((tm, D), lambda i: (i, 0)), _resident((FF, D))],
        out_specs=pl.BlockSpec((tm, FF), lambda i: (i, 0)),
        out_shape=jax.ShapeDtypeStruct((s, FF), F32),
        compiler_params=_cp(("parallel",)),
    )(dz2b, w_down)


def _conv_gelu_bwd(da, up, g, a1, cwb, tm=256, tn=FF // 2, chunk_rows=16):
    s = da.shape[0]
    n_i = s // tm
    n_c = tm // chunk_rows

    def body(da_ref, up_ref, g_ref, a1_ref, c_ref, dup_ref, dc_ref, carry):
        @pl.when(pl.program_id(1) == 0)
        def _():
            carry[...] = jnp.zeros_like(carry)
            dc_ref[...] = jnp.zeros_like(dc_ref)

        def fold(v):
            return jnp.sum(v.reshape(chunk_rows // 8, 8, v.shape[1]), axis=0)

        def chunk(cc, state):
            after, sums = state
            rows = pl.ds((n_c - 1 - cc) * chunk_rows, chunk_rows)
            da_c = da_ref[rows, :]
            dus = (da_c * a1_ref[rows, :].astype(F32), da_c * g_ref[rows, :].astype(F32))
            head, new_sums = [], []
            for half in (0, 1):
                du = dus[half]
                up = up_ref[half, rows, :].astype(F32)
                l1, l2 = _shift_up(du, after[half])
                dup = (du * c_ref[2, half:half + 1, :] + l1 * c_ref[1, half:half + 1, :]
                       + l2 * c_ref[0, half:half + 1, :])
                dup_ref[half, rows, :] = dup.astype(BF16)
                parts = (fold(l2 * up), fold(l1 * up), fold(du * up), fold(du))
                new_sums.append(parts if sums is None else tuple(a + b for a, b in zip(sums[half], parts)))
                head.append(du[:8])
            return tuple(head), new_sums

        state = ((carry[0], carry[1]), None)
        for cc in range(n_c):
            state = chunk(cc, state)
        head, sums = state
        for half in (0, 1):
            carry[half] = head[half]
            for k in range(4):
                dc_ref[k, half:half + 1, :] += jnp.sum(sums[half][k], axis=0, keepdims=True)

    rev = lambda ii: n_i - 1 - ii
    tile = pl.BlockSpec((tm, tn), lambda j, ii: (rev(ii), j))
    pair = pl.BlockSpec((2, tm, tn), lambda j, ii: (0, rev(ii), j))
    per_col = pl.BlockSpec((4, 2, tn), lambda j, ii: (0, 0, j))
    return pl.pallas_call(
        body, name="conv_gelu_bwd", grid=(FF // tn, n_i),
        in_specs=[tile, pair, tile, tile, per_col],
        out_specs=[pair, per_col],
        out_shape=[jax.ShapeDtypeStruct((2, s, FF), BF16), jax.ShapeDtypeStruct((4, 2, FF), F32)],
        scratch_shapes=[pltpu.VMEM((2, 8, tn), F32)],
        compiler_params=_cp(("parallel", "arbitrary")),
    )(da, up, g, a1, cwb)


def _dh1_ln1_bwd(dz2, dup, w_up, z1, ln1_g, tm=256):
    s = dz2.shape[0]

    def body(dz2_ref, dup_ref, w_ref, z1_ref, g_ref, dz1_ref, dz1b_ref, st_ref):
        @pl.when(pl.program_id(0) == 0)
        def _():
            st_ref[...] = jnp.zeros_like(st_ref)

        dh = ALPHA * dz2_ref[...] + _nt(dup_ref[0], w_ref[:, :FF]) + _nt(dup_ref[1], w_ref[:, FF:])
        zh, rstd = _layer_norm_stats(z1_ref[...])
        st_ref[0:1, :] += jnp.sum(dh * zh, axis=0, keepdims=True)
        st_ref[1:2, :] += jnp.sum(dh, axis=0, keepdims=True)
        dz = _layer_norm_bwd(dh, zh, rstd, g_ref[...])
        dz1_ref[...] = dz
        dz1b_ref[...] = dz.astype(BF16)

    td = pl.BlockSpec((tm, D), lambda i: (i, 0))
    return pl.pallas_call(
        body, name="dh1_ln1_bwd", grid=(s // tm,),
        in_specs=[td, pl.BlockSpec((2, tm, FF), lambda i: (0, i, 0)), _resident((D, 2 * FF)), td, _const((1, D))],
        out_specs=[td, td, _const((8, D))],
        out_shape=[jax.ShapeDtypeStruct((s, D), F32), jax.ShapeDtypeStruct((s, D), BF16),
                   jax.ShapeDtypeStruct((8, D), F32)],
        compiler_params=_cp(("arbitrary",)),
    )(dz2, dup, w_up, z1, _row(ln1_g))


def _dcat_rms_bwd(dz1b, w_o, o_a, o_b, norm_a_g, norm_b_g, tm=256):
    s = dz1b.shape[0]

    def body(dz_ref, w_ref, oa_ref, ob_ref, ga_ref, gb_ref, da_ref, db_ref, st_ref):
        @pl.when(pl.program_id(0) == 0)
        def _():
            st_ref[...] = jnp.zeros_like(st_ref)

        dcat = _nt(dz_ref[...], w_ref[...])
        for k, (o_ref, g_ref, d_ref) in enumerate(((oa_ref, ga_ref, da_ref), (ob_ref, gb_ref, db_ref))):
            o = jnp.concatenate([o_ref[j] for j in range(4)], axis=1)
            dn = dcat[:, 512 * k:512 * (k + 1)]
            rr = _rms(o)
            oh = o * rr
            st_ref[k:k + 1, :] += jnp.sum(dn * oh, axis=0, keepdims=True)
            doh = dn * g_ref[...]
            d_o = rr * (doh - oh * jnp.mean(doh * oh, axis=-1, keepdims=True))
            for j in range(4):
                d_ref[j] = d_o[:, 128 * j:128 * (j + 1)]

    t512 = pl.BlockSpec((4, tm, 128), lambda i: (0, i, 0))
    return pl.pallas_call(
        body, name="dcat_rms_bwd", grid=(s // tm,),
        in_specs=[pl.BlockSpec((tm, D), lambda i: (i, 0)), _resident((D, D)), t512, t512,
                  _const((1, 512)), _const((1, 512))],
        out_specs=[t512, t512, _const((8, 512))],
        out_shape=[jax.ShapeDtypeStruct((4, s, 128), F32), jax.ShapeDtypeStruct((4, s, 128), F32),
                   jax.ShapeDtypeStruct((8, 512), F32)],
        compiler_params=_cp(("arbitrary",)),
    )(dz1b, w_o, o_a, o_b, _row(norm_a_g), _row(norm_b_g))


def _dproj_combine(dqa, dka, dva, dqkv_b, tm=256):
    s = dka.shape[0]

    def body(qa, ka, va, qb, kb, vb, o_ref):
        for j in range(4):
            o_ref[:, 128 * j:128 * (j + 1)] = qa[j].astype(BF16)
            o_ref[:, 768 + 128 * j:768 + 128 * (j + 1)] = qb[j].astype(BF16)
            o_ref[:, 1280 + 128 * j:1280 + 128 * (j + 1)] = kb[j].astype(BF16)
            o_ref[:, 1792 + 128 * j:1792 + 128 * (j + 1)] = vb[j].astype(BF16)
        o_ref[:, 512:640] = ka[...].astype(BF16)
        o_ref[:, 640:768] = va[...].astype(BF16)

    t512 = pl.BlockSpec((4, tm, 128), lambda i: (0, i, 0))
    t128 = pl.BlockSpec((tm, 128), lambda i: (i, 0))
    return pl.pallas_call(
        body, name="dproj_combine", grid=(s // tm,),
        in_specs=[t512, t128, t128] + [t512] * 3,
        out_specs=pl.BlockSpec((tm, WIN), lambda i: (i, 0)),
        out_shape=jax.ShapeDtypeStruct((s, WIN), BF16),
        compiler_params=_cp(("parallel",)),
    )(dqa, dka, dva, *dqkv_b)


def _grad_x(dz1, dproj, w_in_t, zero, tm=256):
    s = dz1.shape[0]

    def body(dz_ref, dp_ref, w_ref, z_ref, o_ref):
        o_ref[...] = ALPHA * dz_ref[...] + _nn(dp_ref[...], w_ref[...]) + z_ref[0:1, 0:1]

    td = pl.BlockSpec((tm, D), lambda i: (i, 0))
    return pl.pallas_call(
        body, name="grad_x", grid=(s // tm,),
        in_specs=[td, pl.BlockSpec((tm, WIN), lambda i: (i, 0)), _resident((WIN, D)), _const((8, 128))],
        out_specs=td, out_shape=jax.ShapeDtypeStruct((s, D), F32),
        compiler_params=_cp(("parallel",)),
    )(dz1, dproj, w_in_t, zero)


def _place():
    return lax.axis_index("x"), lax.axis_index("y"), lax.axis_index("c")


def _other_chips(x, y):
    return [(1 - x, y), (x, 1 - y), (1 - x, 1 - y)]


def _hbm(a):
    return pltpu.with_memory_space_constraint(a, pltpu.HBM)


def _gather_w_in(shard, conv_w):
    rows_k = shard.shape[0]
    half = rows_k // 2

    def body(src, conv_src, out, conv_out, send_sems, recv_sems):
        x, y, c = _place()
        b = 2 * x + y
        sibling = (x, y, 1 - c)
        chips = _other_chips(x, y)

        def copy(idx, chip_b, core, to, first_hop=False):
            rows = out.at[pl.ds(pl.multiple_of(chip_b * rows_k + core * half, 16), half)]
            s_ref = src.at[pl.ds(pl.multiple_of(core * half, 16), half)] if first_hop else rows
            return pltpu.make_async_remote_copy(src_ref=s_ref, dst_ref=rows, send_sem=send_sems.at[idx],
                                                recv_sem=recv_sems.at[idx], device_id=to, device_id_type=MESH)

        def own_copy():
            return pltpu.make_async_remote_copy(
                src_ref=src, dst_ref=out.at[pl.ds(pl.multiple_of(b * rows_k, 16), rows_k)], send_sem=send_sems.at[6],
                recv_sem=recv_sems.at[6], device_id=sibling, device_id_type=MESH)

        def conv_copy(idx, chip_b, to):
            return pltpu.make_async_remote_copy(src_ref=conv_src, dst_ref=conv_out.at[chip_b],
                                                send_sem=send_sems.at[7 + idx], recv_sem=recv_sems.at[7 + idx],
                                                device_id=to, device_id_type=MESH)

        started = [own_copy(), conv_copy(3, b, sibling)]
        for jn, chip in enumerate(chips):
            started += [copy(jn, b, c, (chip[0], chip[1], c), first_hop=True), conv_copy(jn, b, (chip[0], chip[1], c))]
        for cp in started:
            cp.start()
        for jn, chip in enumerate(chips):
            cb = 2 * chip[0] + chip[1]
            copy(jn, cb, c, (chip[0], chip[1], c)).wait_recv()
            cp = copy(3 + jn, cb, c, sibling)
            cp.start()
            started.append(cp)
        for jn, chip in enumerate(chips):
            cb = 2 * chip[0] + chip[1]
            copy(3 + jn, cb, 1 - c, sibling).wait_recv()
            conv_copy(jn, cb, (chip[0], chip[1], c)).wait_recv()
        own_copy().wait_recv()
        conv_copy(3, b, sibling).wait_recv()
        for cp in started:
            cp.wait_send()

    return pl.pallas_call(
        body, name="gather_w_in",
        in_specs=[ANY, ANY], out_specs=[ANY, ANY],
        out_shape=[jax.ShapeDtypeStruct((N_CHIPS * rows_k, D), BF16), jax.ShapeDtypeStruct((N_CHIPS,) + conv_w.shape, F32)],
        scratch_shapes=[pltpu.SemaphoreType.DMA((11,)), pltpu.SemaphoreType.DMA((11,))],
        compiler_params=pltpu.CompilerParams(has_side_effects=True),
    )(shard, conv_w)


def _weight_copies(shard, land, send_sems, recv_sems, arrivals):
    x, y, c = _place()
    n_rows, n_cols = shard.shape
    peers = [(px, py, c) for px, py in _other_chips(x, y)] + [(x, y, 1 - c)]
    cps = []
    for jn, peer in enumerate(peers):
        at = 2 * peer[0] + peer[1] if arrivals else 2 * x + y
        if land.shape[1] == n_cols:
            dst = land.at[pl.ds(pl.multiple_of(at * n_rows, 16), n_rows)]
        else:
            dst = land.at[:, pl.ds(pl.multiple_of(at * n_cols, 128), n_cols)]
        cps.append(pltpu.make_async_remote_copy(src_ref=shard, dst_ref=dst, send_sem=send_sems.at[jn],
                                                recv_sem=recv_sems.at[jn], device_id=peer, device_id_type=MESH))
    return cps


def _weights_start(shards, after):
    n = len(shards)
    lands = [lax.empty((N_CHIPS * sh.shape[0], D) if sh.shape[1] == D else (D, N_CHIPS * sh.shape[1]), BF16)
             for sh in shards]

    def body(*refs):
        src, land = refs[:n], refs[n:2 * n]
        send_sems, recv_sems = refs[2 * n + 1:3 * n + 1], refs[3 * n + 1:4 * n + 1]
        for k in range(n):
            for send in _weight_copies(src[k], land[k], send_sems[k], recv_sems[k], False):
                send.start()
        refs[-1][...] = jnp.zeros_like(refs[-1])

    res = pl.pallas_call(
        body, name="weights_start",
        in_specs=[HBM] * (2 * n) + [ANY], out_specs=[SEM] * (2 * n) + [HBM] * (2 * n) + [VMEM],
        out_shape=[pltpu.SemaphoreType.DMA((4,))] * (2 * n)
        + [pltpu.HBM(a.shape, a.dtype) for a in (*shards, *lands)] + [jax.ShapeDtypeStruct((8, 128), F32)],
        input_output_aliases={i: i + 2 * n for i in range(2 * n)},
        compiler_params=pltpu.CompilerParams(has_side_effects=DATAFLOW),
    )(*[_hbm(a) for a in (*shards, *lands)], after)
    return [(res[k], res[n + k], res[2 * n + k], res[3 * n + k]) for k in range(n)], res[-1]


def _weights_wait(started, after, name):
    send_sems, recv_sems, shard, land = started

    def body(s_ref, l_ref, send_ref, recv_ref, after_ref, s_out, l_out):
        for cp in _weight_copies(s_ref, l_ref, send_ref, recv_ref, True):
            cp.wait_send()
            cp.wait_recv()

    return pl.pallas_call(
        body, name=name,
        in_specs=[HBM, HBM, SEM, SEM, ANY], out_specs=[HBM, HBM],
        out_shape=[pltpu.HBM(shard.shape, shard.dtype), pltpu.HBM(land.shape, land.dtype)],
        input_output_aliases={0: 0, 1: 1},
        compiler_params=pltpu.CompilerParams(has_side_effects=DATAFLOW),
    )(shard, land, send_sems, recv_sems, after)[1]


def _grad_copies(g_ref, land_ref, send_sems, recv_sems):
    x, y, c = _place()
    cps = []
    for d in range(1, 8):
        px, py, pc = x ^ (d >> 2), y ^ ((d >> 1) & 1), c ^ (d & 1)
        cps.append(pltpu.make_async_remote_copy(
            src_ref=g_ref.at[2 * px + py, pc], dst_ref=land_ref.at[d - 1], send_sem=send_sems.at[d - 1],
            recv_sem=recv_sems.at[d - 1], device_id=(px, py, pc), device_id_type=MESH))
    return cps


def _grads_start(grads_b, name):
    n = len(grads_b)
    lands = [lax.empty((7, g.shape[2], D), BF16) for g in grads_b]

    def body(*refs):
        g, land = refs[:n], refs[n:2 * n]
        send_sems, recv_sems = refs[2 * n:3 * n], refs[3 * n:4 * n]
        for k in range(n):
            for cp in _grad_copies(g[k], land[k], send_sems[k], recv_sems[k]):
                cp.start()
        refs[-1][...] = jnp.zeros_like(refs[-1])

    res = pl.pallas_call(
        body, name=name,
        in_specs=[HBM] * (2 * n), out_specs=[SEM] * (2 * n) + [HBM] * (2 * n) + [VMEM],
        out_shape=[pltpu.SemaphoreType.DMA((7,))] * (2 * n)
        + [pltpu.HBM(a.shape, a.dtype) for a in (*grads_b, *lands)] + [jax.ShapeDtypeStruct((8, 128), F32)],
        input_output_aliases={i: i + 2 * n for i in range(2 * n)},
        compiler_params=pltpu.CompilerParams(has_side_effects=DATAFLOW),
    )(*[_hbm(a) for a in (*grads_b, *lands)])
    return [(res[k], res[n + k], res[2 * n + k], res[3 * n + k]) for k in range(n)], res[-1]


def _grads_wait(started, after, name):
    n = len(started)

    def body(*refs):
        g, land = refs[:n], refs[n:2 * n]
        send_sems, recv_sems = refs[2 * n:3 * n], refs[3 * n:4 * n]
        for k in range(n):
            for cp in _grad_copies(g[k], land[k], send_sems[k], recv_sems[k]):
                cp.wait_send()
                cp.wait_recv()

    gs = [st[2] for st in started]
    lands = [st[3] for st in started]
    res = pl.pallas_call(
        body, name=name,
        in_specs=[HBM] * (2 * n) + [SEM] * (2 * n) + [ANY], out_specs=[HBM] * (2 * n),
        out_shape=[pltpu.HBM(a.shape, a.dtype) for a in (*gs, *lands)],
        input_output_aliases={i: i for i in range(2 * n)},
        compiler_params=pltpu.CompilerParams(has_side_effects=DATAFLOW),
    )(*gs, *lands, *[st[0] for st in started], *[st[1] for st in started], after)
    return res[n:]


def _sum_partials(grad4, got, cb, name, tr):
    h = grad4.shape[2]
    per_half = h // tr

    def body(cb_ref, g_ref, o_ref, out_ref):
        acc = g_ref[...]
        for j in range(7):
            acc = acc + o_ref[j].astype(F32)
        out_ref[...] = acc

    return pl.pallas_call(
        body, name=name,
        grid_spec=pltpu.PrefetchScalarGridSpec(
            num_scalar_prefetch=1, grid=(per_half,),
            in_specs=[pl.BlockSpec((None, None, tr, D), lambda i, cb_ref: (cb_ref[1], cb_ref[0], i, 0)),
                      pl.BlockSpec((7, tr, D), lambda i, cb_ref: (0, i, 0))],
            out_specs=pl.BlockSpec((tr, D), lambda i, cb_ref: (cb_ref[0] * per_half + i, 0))),
        out_shape=jax.ShapeDtypeStruct((2 * h, D), F32),
        compiler_params=_cp(("arbitrary",)),
    )(cb, grad4, got)


def _swap_halves(shards, name):
    n = len(shards)

    def body(*refs):
        out, send_sems, recv_sems = refs[n:2 * n], refs[2 * n], refs[2 * n + 1]
        x, y, c = _place()
        cps = []
        for k in range(n):
            h = shards[k].shape[0] // 2
            mine = out[k].at[pl.ds(pl.multiple_of(c * h, 8), h)]
            cp = pltpu.make_async_remote_copy(src_ref=mine, dst_ref=mine, send_sem=send_sems.at[k],
                                              recv_sem=recv_sems.at[k], device_id=(x, y, 1 - c), device_id_type=MESH)
            cp.start()
            cps.append(cp)
        for cp in cps:
            cp.wait()

    return pl.pallas_call(
        body, name=name,
        in_specs=[ANY] * n, out_specs=[ANY] * n,
        out_shape=[jax.ShapeDtypeStruct(sh.shape, F32) for sh in shards],
        input_output_aliases={k: k for k in range(n)},
        scratch_shapes=[pltpu.SemaphoreType.DMA((n,)), pltpu.SemaphoreType.DMA((n,))],
        compiler_params=pltpu.CompilerParams(has_side_effects=True),
    )(*shards)


def _share_halves(shards, small):
    n = len(shards)
    rows = small.shape[0]

    def body(*refs):
        small_ref = refs[n]
        out, total_ref = refs[n + 1:2 * n + 1], refs[2 * n + 1]
        all_ref, send_sems, recv_sems, ssend, srecv = refs[2 * n + 2:]
        x, y, c = _place()
        me = 4 * x + 2 * y + c
        cps = []
        for k in range(n):
            h = shards[k].shape[0] // 2
            mine = out[k].at[pl.ds(pl.multiple_of(c * h, 8), h)]
            cp = pltpu.make_async_remote_copy(src_ref=mine, dst_ref=mine, send_sem=send_sems.at[k],
                                              recv_sem=recv_sems.at[k], device_id=(x, y, 1 - c), device_id_type=MESH)
            cp.start()
            cps.append(cp)
        all_ref[me] = small_ref[...]
        peers = []
        for d in range(1, 8):
            px, py, pc = x ^ (d >> 2), y ^ ((d >> 1) & 1), c ^ (d & 1)
            cp = pltpu.make_async_remote_copy(src_ref=small_ref, dst_ref=all_ref.at[me],
                                              send_sem=ssend.at[d - 1], recv_sem=srecv.at[d - 1],
                                              device_id=(px, py, pc), device_id_type=MESH)
            cp.start()
            peers.append(cp)
        for cp in peers:
            cp.wait()
        acc = all_ref[0]
        for d in range(1, 8):
            acc = acc + all_ref[d]
        total_ref[...] = acc
        for cp in cps:
            cp.wait()

    return pl.pallas_call(
        body, name="share_halves",
        in_specs=[ANY] * n + [VMEM], out_specs=[ANY] * n + [VMEM],
        out_shape=[jax.ShapeDtypeStruct(sh.shape, F32) for sh in shards] + [jax.ShapeDtypeStruct((rows, D), F32)],
        input_output_aliases={k: k for k in range(n)},
        scratch_shapes=[pltpu.VMEM((8, rows, D), F32), pltpu.SemaphoreType.DMA((n,)), pltpu.SemaphoreType.DMA((n,)),
                        pltpu.SemaphoreType.DMA((7,)), pltpu.SemaphoreType.DMA((7,))],
        compiler_params=pltpu.CompilerParams(has_side_effects=True),
    )(*shards, small)


def _adamw(w, g, m, v, name, tr):
    rows, cols = w.shape

    def body(w_ref, g_ref, m_ref, v_ref, d_ref, nm_ref, nv_ref):
        g_ = g_ref[...]
        nm = ADAM_B1 * m_ref[...] + (1.0 - ADAM_B1) * g_
        nv = ADAM_B2 * v_ref[...] + (1.0 - ADAM_B2) * (g_ * g_)
        m_hat = nm / (1.0 - ADAM_B1 ** ADAM_STEP)
        v_hat = nv / (1.0 - ADAM_B2 ** ADAM_STEP)
        d_ref[...] = -ADAM_LR * (m_hat / (jnp.sqrt(v_hat) + ADAM_EPS) + ADAM_WD * w_ref[...])
        nm_ref[...] = nm
        nv_ref[...] = nv

    spec = pl.BlockSpec((tr, cols), lambda i: (i, 0))
    return pl.pallas_call(
        body, name=name, grid=(rows // tr,),
        in_specs=[spec] * 4, out_specs=[spec] * 3,
        out_shape=[jax.ShapeDtypeStruct((rows, cols), F32)] * 3,
        compiler_params=_cp(("parallel",)),
    )(w, g, m, v)


def _local_step(x, target, w_in_t, late_weights, norm_a_g, norm_b_g, sinks_a, ln1_g, ln1_b,
                conv_w, conv_b, ln2_g, ln2_b, slopes, on_grad):
    cwb = jnp.concatenate([conv_w, conv_b[None]], axis=0).reshape(4, 2, FF)

    proj, xb = _proj(x, w_in_t, "proj")
    o_a, lse_a = _attn_a_fwd(proj, sinks_a)
    fwd_b = [_attn_b_fwd(proj, slopes, r) for r in B_DILATIONS]
    w_o = late_weights(1, fwd_b[-1][1])
    o_b, lse_b, cat, z1, h1, h1b = _mix_ln1(x, o_a, [f[0] for f in fwd_b], [f[1] for f in fwd_b],
                                           norm_a_g, norm_b_g, w_o, ln1_g, ln1_b)
    w_up = late_weights(2, h1b)
    up, a, gate, a1 = _up_conv_gelu(h1b, w_up, cwb)
    w_down = late_weights(3, a)
    dz2, dz2b, st2 = _down_ln2_loss(a, w_down, h1, target, ln2_g, ln2_b)

    on_grad(3, *_grad_w(a, dz2b, "grad_w_down", tm=FF // 2))
    dup, dconv = _conv_gelu_bwd(_d_act(dz2b, w_down), up, gate, a1, cwb)
    on_grad(2, *_grad_w(dup, h1b, "grad_w_up", tm=FF // 2, lhs_halves=True))
    dz1, dz1b, st1 = _dh1_ln1_bwd(dz2, dup, w_up, z1, ln1_g)
    tok = on_grad(1, *_grad_w(cat, dz1b, "grad_w_o", tm=512))
    d_oa, d_ob, st_n = _dcat_rms_bwd(dz1b, w_o, o_a, o_b, norm_a_g + tok[0, 0], norm_b_g)
    dqa, dka, dva, dsink = _attn_a_bwd(proj, sinks_a, d_oa, o_a, lse_a)
    bwd_b = None
    for r in B_DILATIONS:
        bwd_b = _attn_b_bwd(proj, slopes, d_ob, o_b, lse_b, r, bwd_b)
    dproj = _dproj_combine(dqa, dka, dva, bwd_b)
    tok = on_grad(0, *_grad_w(dproj, xb, "grad_w_in", tm=WA))
    gx = _grad_x(dz1, dproj, w_in_t, tok)

    dconv = dconv.reshape(4, 2 * FF)
    small = dict(loss=st2[2, 0:1], norm_a_g=st_n[0], norm_b_g=st_n[1], sinks_a=dsink[:, 0],
                 ln1_g=st1[0], ln1_b=st1[1], conv_w=dconv[0:3].reshape(-1), conv_b=dconv[3],
                 ln2_g=st2[0], ln2_b=st2[1])
    return gx, small


SMALL_ORDER = ("loss", "norm_a_g", "norm_b_g", "sinks_a", "ln1_g", "ln1_b", "conv_b", "ln2_g", "ln2_b", "conv_w")
SMALL_SIZES = dict(loss=1, norm_a_g=512, norm_b_g=512, sinks_a=8, ln1_g=D, ln1_b=D, conv_b=2 * FF, ln2_g=D, ln2_b=D,
                   conv_w=3 * 2 * FF)


def _pack(parts, rows):
    flat = jnp.concatenate([parts[k].reshape(-1).astype(F32) for k in parts])
    return jnp.pad(flat, (0, rows * D - flat.shape[0])).reshape(rows, D)


def _unpack(buf, names, sizes):
    flat = buf.reshape(-1)
    out, at = {}, 0
    for k in names:
        out[k] = flat[at:at + sizes[k]]
        at += sizes[k]
    return out


def kernel(x, w_in, norm_a_g, norm_b_g, sinks_a, w_o, ln1_g, ln1_b, w_up, conv_w, conv_b, w_down, ln2_g, ln2_b, loss_target, m_w_in, m_norm_a_g, m_norm_b_g, m_sinks_a, m_w_o, m_ln1_g, m_ln1_b, m_w_up, m_conv_w, m_conv_b, m_w_down, m_ln2_g, m_ln2_b, v_w_in, v_norm_a_g, v_norm_b_g, v_sinks_a, v_w_o, v_ln1_g, v_ln1_b, v_w_up, v_conv_w, v_conv_b, v_w_down, v_ln2_g, v_ln2_b):
    xi, yi, ci = _place()
    chip = (2 * xi + yi).astype(I32)
    core = ci.astype(I32)

    w_in_rows, m_w_in_rows, v_w_in_rows = w_in.T, m_w_in.T, v_w_in.T
    shards = (w_in_rows.astype(BF16), w_o.astype(BF16), w_up.astype(BF16), w_down.astype(BF16))
    w_in_t, conv_w4 = _gather_w_in(shards[0], conv_w)
    conv_w_f = conv_w4.transpose(1, 0, 2).reshape(3, 2 * FF)
    w_started, w_tok = _weights_start(shards[1:], conv_w4)
    slopes = jnp.asarray(SLOPES, F32) + w_tok[0, 0]

    halves_rows = [r // 2 for r in SHARD_ROWS]
    grads4, grads_b4, started = [None] * 4, [None] * 4, [None] * 4

    def on_grad(k, g, g_b):
        grads4[k] = g.reshape(N_CHIPS, 2, halves_rows[k], D)
        grads_b4[k] = g_b.reshape(N_CHIPS, 2, halves_rows[k], D)
        if k > 1:
            return None
        group = (1, 2, 3) if k == 1 else (0,)
        sts, tok = _grads_start([grads_b4[i] for i in group], f"grads_start_{k}")
        for i, st in zip(group, sts):
            started[i] = st
        return tok

    gx, small = _local_step(
        x[0], loss_target[0], w_in_t, lambda k, after: _weights_wait(w_started[k - 1], after, f"weights_wait_{k}"),
        norm_a_g, norm_b_g, sinks_a, ln1_g, ln1_b, conv_w_f, conv_b, ln2_g, ln2_b, slopes, on_grad)

    tiles = (96, 128, 352, 176)
    core_chip = jnp.stack([core, chip])
    got = _grads_wait(started[1:], gx, "grads_wait_1")
    halves = [_sum_partials(grads4[k], got[k - 1], core_chip, f"sum_partials_{k}", tiles[k]) for k in (1, 2, 3)]
    g_w_o, g_w_up_rows, g_w_down = _swap_halves(halves, "swap_halves")
    g_w_up = g_w_up_rows.T
    delta, new_m, new_v = {}, {}, {}
    for k, g, tr in (("w_o", g_w_o, 128), ("w_up", g_w_up, 256), ("w_down", g_w_down, 176)):
        delta[k], new_m[k], new_v[k] = _adamw(dict(w_o=w_o, w_up=w_up, w_down=w_down)[k], g,
                                              dict(w_o=m_w_o, w_up=m_w_up, w_down=m_w_down)[k],
                                              dict(w_o=v_w_o, w_up=v_w_up, w_down=v_w_down)[k], f"adamw_{k}", tr)

    got = _grads_wait(started[:1], delta["w_up"], "grads_wait_0")
    half_in = _sum_partials(grads4[0], got[0], core_chip, "sum_partials_0", tiles[0])
    small_rows = 32
    g_w_in_rows, totals = _share_halves([half_in], _pack({k: small[k] for k in SMALL_ORDER}, small_rows))
    tot = _unpack(totals, SMALL_ORDER, SMALL_SIZES)
    loss = tot["loss"][0]
    cols = 2 * FF // N_CHIPS
    g_conv_w = lax.dynamic_slice(tot["conv_w"].reshape(3, 2 * FF), (0, chip * cols), (3, cols))
    g_small = dict(norm_a_g=tot["norm_a_g"], norm_b_g=tot["norm_b_g"], sinks_a=tot["sinks_a"], ln1_g=tot["ln1_g"],
                   ln1_b=tot["ln1_b"], conv_w=g_conv_w, conv_b=tot["conv_b"], ln2_g=tot["ln2_g"], ln2_b=tot["ln2_b"])

    weights = dict(w_in=w_in, norm_a_g=norm_a_g, norm_b_g=norm_b_g, sinks_a=sinks_a, w_o=w_o, ln1_g=ln1_g, ln1_b=ln1_b,
                   w_up=w_up, conv_w=conv_w, conv_b=conv_b, w_down=w_down, ln2_g=ln2_g, ln2_b=ln2_b)
    ms = dict(w_in=m_w_in, norm_a_g=m_norm_a_g, norm_b_g=m_norm_b_g, sinks_a=m_sinks_a, w_o=m_w_o, ln1_g=m_ln1_g,
              ln1_b=m_ln1_b, w_up=m_w_up, conv_w=m_conv_w, conv_b=m_conv_b, w_down=m_w_down, ln2_g=m_ln2_g, ln2_b=m_ln2_b)
    vs = dict(w_in=v_w_in, norm_a_g=v_norm_a_g, norm_b_g=v_norm_b_g, sinks_a=v_sinks_a, w_o=v_w_o, ln1_g=v_ln1_g,
              ln1_b=v_ln1_b, w_up=v_w_up, conv_w=v_conv_w, conv_b=v_conv_b, w_down=v_w_down, ln2_g=v_ln2_g, ln2_b=v_ln2_b)
    order = list(weights)
    grad = dict(g_small, w_in=g_w_in_rows.T, w_o=g_w_o, w_up=g_w_up, w_down=g_w_down)

    delta["w_in"], new_m["w_in"], new_v["w_in"] = [
        a.T for a in _adamw(w_in_rows, g_w_in_rows, m_w_in_rows, v_w_in_rows, "adamw_w_in", 144)]
    small_names = [k for k in order if k not in delta]
    sizes = {k: weights[k].size for k in small_names}
    rows = 16
    packed = [_pack({k: src[k] for k in small_names}, rows) for src in (weights, grad, ms, vs)]
    for res, buf in zip((delta, new_m, new_v), _adamw(*packed, "adamw_small", rows)):
        for k, val in _unpack(buf, small_names, sizes).items():
            res[k] = val.reshape(weights[k].shape)

    return (loss, gx[None], *[grad[k] for k in order], *[delta[k] for k in order],
            *[new_m[k] for k in order], *[new_v[k] for k in order])
```

```python
import functools
import math

import jax
import jax.numpy as jnp
from jax import lax
from jax.experimental import pallas as pl
from jax.experimental.pallas import tpu as pltpu

F32, BF16, I32 = jnp.float32, jnp.bfloat16, jnp.int32

D = 1024
FF = 2816
HD = 64
NH = 8
WA, WB = 768, 1536
WIN = WA + WB
BLK = 128
ALPHA = 2.0 ** 0.25
LN_EPS, RMS_EPS = 1e-5, 1e-6
SCALE = 1.0 / math.sqrt(HD)
A_MAX_DIST, B_MAX_DIST = 127, 128
B_DILATIONS = (1, 4, 16)
SLOPES = tuple(2.0 ** (-(i + 1)) for i in range(NH))
SHARD_ROWS = (WIN // 4, D // 4, 2 * FF // 4, FF // 4)
N_CHIPS = 4
ADAM_LR, ADAM_B1, ADAM_B2, ADAM_EPS, ADAM_WD, ADAM_STEP = 0.001, 0.9, 0.999, 1e-08, 0.01, 10
MESH = pl.DeviceIdType.MESH
ANY = pl.BlockSpec(memory_space=pl.ANY)
SMEM = pl.BlockSpec(memory_space=pltpu.SMEM)
VMEM = pl.BlockSpec(memory_space=pltpu.VMEM)
HBM = pl.BlockSpec(memory_space=pltpu.HBM)
SEM = pl.BlockSpec(memory_space=pltpu.SEMAPHORE)
DATAFLOW = pltpu.SideEffectType.DATAFLOW_SIDE_EFFECTING


def _cp(sem, mb=48):
    return pltpu.CompilerParams(dimension_semantics=sem, vmem_limit_bytes=mb << 20)


def _nn(a, b):
    return lax.dot_general(a, b, (((1,), (0,)), ((), ())), preferred_element_type=F32)


def _nt(a, b):
    return lax.dot_general(a, b, (((1,), (1,)), ((), ())), preferred_element_type=F32)


def _tn(a, b):
    return lax.dot_general(a, b, (((0,), (0,)), ((), ())), preferred_element_type=F32)


def _resident(shape):
    n = len(shape)
    return pl.BlockSpec(shape, lambda *_: (0,) * n, pipeline_mode=pl.Buffered(1))


def _const(shape):
    n = len(shape)
    return pl.BlockSpec(shape, lambda *_: (0,) * n)


def _proj(x, w_t, name, tm=512):
    s = x.shape[0]
    n = w_t.shape[0]

    def body(x_ref, w_ref, o_ref, xb_ref):
        xb = x_ref[...].astype(BF16)
        xb_ref[...] = xb
        res = _nt(xb, w_ref[...])
        for g in range(n // 128):
            o_ref[g] = res[:, 128 * g:128 * (g + 1)]

    return pl.pallas_call(
        body, name=name, grid=(s // tm,),
        in_specs=[pl.BlockSpec((tm, D), lambda i: (i, 0)), _resident((n, D))],
        out_specs=[pl.BlockSpec((n // 128, tm, 128), lambda i: (0, i, 0)), pl.BlockSpec((tm, D), lambda i: (i, 0))],
        out_shape=[jax.ShapeDtypeStruct((n // 128, s, 128), F32), jax.ShapeDtypeStruct((s, D), BF16)],
        compiler_params=_cp(("parallel",)),
    )(x, w_t)


def _grad_w(lhs, rhs, name, tm, tk=512, lhs_halves=False):
    s = rhs.shape[0]
    if lhs_halves:
        per_half = lhs.shape[2] // tm
        n = 2 * lhs.shape[2]
        lhs_spec = pl.BlockSpec((None, tk, tm), lambda i, k: (i // per_half, k, i % per_half))
    else:
        n = lhs.shape[1]
        lhs_spec = pl.BlockSpec((tk, tm), lambda i, k: (k, i))
    nk = s // tk

    def body(l_ref, r_ref, o_ref, ob_ref):
        k = pl.program_id(1)

        @pl.when(k == 0)
        def _():
            o_ref[...] = jnp.zeros_like(o_ref)

        o_ref[...] += _tn(l_ref[...].astype(BF16), r_ref[...].astype(BF16))

        @pl.when(k == nk - 1)
        def _():
            ob_ref[...] = o_ref[...].astype(BF16)

    return pl.pallas_call(
        body, name=name, grid=(n // tm, nk),
        in_specs=[lhs_spec, pl.BlockSpec((tk, D), lambda i, k: (k, 0))],
        out_specs=[pl.BlockSpec((tm, D), lambda i, k: (i, 0))] * 2,
        out_shape=[jax.ShapeDtypeStruct((n, D), F32), jax.ShapeDtypeStruct((n, D), BF16)],
        compiler_params=_cp(("parallel", "arbitrary")),
    )(lhs, rhs)


def _band_base(max_dist, dist_unit, first):
    row = lax.broadcasted_iota(I32, (BLK, 2 * BLK), 0)
    col = lax.broadcasted_iota(I32, (BLK, 2 * BLK), 1)
    dist = BLK + row - col
    ok = (dist >= 0) & (dist <= max_dist)
    if first:
        ok = ok & (col >= BLK)
    return jnp.where(ok, dist.astype(F32) * (-float(dist_unit)), -jnp.inf)


def _half_mask(shape, e):
    lane = lax.broadcasted_iota(I32, shape, 1)
    return (lane < HD) if e == 0 else (lane >= HD)


def _to_half(x, e, g):
    if g != e:
        x = pltpu.roll(x, HD, 1)
    return jnp.where(_half_mask(x.shape, g), x, 0.0)


def _stack_heads(scalars, tile):
    return jnp.concatenate([scalars[0] * tile, scalars[1] * tile], axis=0)


def _pair_fwd(q2, kb, vb, base, slopes, kv_heads, sinks):
    lo = _half_mask((BLK, 2 * HD), 0)
    if sinks is not None:
        o2 = lse2 = None
        for e in (0, 1):
            g = kv_heads[e]
            qv = (_to_half(q2, e, g) * SCALE).astype(BF16)
            s = _nt(qv, kb) + slopes[e] * base
            m = jnp.maximum(jnp.max(s, axis=1, keepdims=True), sinks[e])
            p = jnp.exp(s - m)
            l = jnp.sum(p, axis=1, keepdims=True) + jnp.exp(sinks[e] - m)
            oh = _nn(p.astype(BF16), vb) / l
            if g != e:
                oh = pltpu.roll(oh, HD, 1)
            lse = jnp.broadcast_to(m + jnp.log(l), (BLK, 2 * HD))
            o2 = oh if e == 0 else jnp.where(lo, o2, oh)
            lse2 = lse if e == 0 else jnp.where(lo, lse2, lse)
        return o2, lse2
    qs = jnp.concatenate([_to_half(q2, e, kv_heads[e]) * SCALE for e in (0, 1)], axis=0).astype(BF16)
    s = _nt(qs, kb) + (base if slopes is None else _stack_heads(slopes, base))
    m = jnp.max(s, axis=1, keepdims=True)
    p = jnp.exp(s - m)
    l = jnp.sum(p, axis=1, keepdims=True)
    o = _nn(p.astype(BF16), vb) / l
    lse = m + jnp.log(l)
    halves = []
    for e in (0, 1):
        oh = o[e * BLK:(e + 1) * BLK]
        halves.append(pltpu.roll(oh, HD, 1) if kv_heads[e] != e else oh)
    o2 = jnp.where(lo, halves[0], halves[1])
    lse2 = jnp.where(lo, jnp.broadcast_to(lse[:BLK], (BLK, 2 * HD)), jnp.broadcast_to(lse[BLK:], (BLK, 2 * HD)))
    return o2, lse2


def _pair_bwd(q2, kb, vb, do2, o2, lse2, base, slopes, kv_heads, sinks):
    lo = _half_mask((BLK, 2 * HD), 0)
    prod = do2 * o2
    lses, deltas = [], []
    for e in (0, 1):
        hq = _half_mask((BLK, 2 * HD), e)
        lses.append(jnp.max(jnp.where(hq, lse2, -jnp.inf), axis=1, keepdims=True))
        deltas.append(jnp.sum(jnp.where(hq, prod, 0.0), axis=1, keepdims=True))
    lse = jnp.concatenate(lses, axis=0)
    delta = jnp.concatenate(deltas, axis=0)
    qs = jnp.concatenate([_to_half(q2, e, kv_heads[e]) * SCALE for e in (0, 1)], axis=0).astype(BF16)
    dos = jnp.concatenate([_to_half(do2, e, kv_heads[e]) for e in (0, 1)], axis=0).astype(BF16)
    p = jnp.exp(_nt(qs, kb) + (base if slopes is None else _stack_heads(slopes, base)) - lse)
    ds = (p * (_nt(dos, vb) - delta)).astype(BF16)
    dq = _nn(ds, kb) * SCALE
    halves = []
    for e in (0, 1):
        dqh = dq[e * BLK:(e + 1) * BLK]
        halves.append(pltpu.roll(dqh, HD, 1) if kv_heads[e] != e else dqh)
    dq2 = jnp.where(lo, halves[0], halves[1])
    dk2 = _tn(ds, qs)
    dv2 = _tn(p.astype(BF16), dos)
    dsinks = []
    if sinks is not None:
        for e in (0, 1):
            dsinks.append(jnp.sum(-jnp.exp(sinks[e] - lses[e]) * deltas[e], axis=0, keepdims=True))
    return dq2, dk2, dv2, dsinks


A_BLOCKS_PER_STEP = 2
A_BLOCKS_PER_STEP_BWD = 1


def _attn_a_fwd(proj, sinks):
    s = proj.shape[1]
    nq = A_BLOCKS_PER_STEP
    rows = BLK * nq
    steps = s // rows

    def body(sink_ref, q_ref, kp_ref, kc_ref, vp_ref, vc_ref, o_ref, lse_ref):
        n = pl.program_id(0)
        base_rest = _band_base(A_MAX_DIST, 1, False)
        base_0 = jnp.where(n > 0, base_rest, _band_base(A_MAX_DIST, 1, True))
        for i in range(nq):
            cur = pl.ds(i * BLK, BLK)
            k_prev = kc_ref[pl.ds((i - 1) * BLK, BLK), :] if i > 0 else kp_ref[...]
            v_prev = vc_ref[pl.ds((i - 1) * BLK, BLK), :] if i > 0 else vp_ref[...]
            kb = jnp.concatenate([k_prev, kc_ref[cur, :]], axis=0).astype(BF16)
            vb = jnp.concatenate([v_prev, vc_ref[cur, :]], axis=0).astype(BF16)
            for j in range(NH // 2):
                g = j // 2
                o2, lse2 = _pair_fwd(q_ref[j, cur, :], kb, vb, base_rest if i > 0 else base_0,
                                     (SLOPES[2 * j], SLOPES[2 * j + 1]), (g, g), (sink_ref[2 * j], sink_ref[2 * j + 1]))
                o_ref[j, cur, :] = o2
                lse_ref[j, cur, :] = lse2

    before = lambda n: jnp.maximum(n * nq - 1, 0)
    slab = lambda g: pl.BlockSpec((None, rows, 128), lambda n: (g, n, 0))
    edge = lambda g: pl.BlockSpec((None, BLK, 128), lambda n: (g, before(n), 0))
    quad = pl.BlockSpec((4, rows, 128), lambda n: (0, n, 0))
    return pl.pallas_call(
        body, name="attn_a_fwd", grid=(steps,),
        in_specs=[SMEM, quad, edge(4), slab(4), edge(5), slab(5)],
        out_specs=[quad, quad],
        out_shape=[jax.ShapeDtypeStruct((4, s, 128), F32)] * 2,
        compiler_params=_cp(("parallel",)),
    )(sinks, proj, proj, proj, proj, proj)


def _attn_a_bwd(proj, sinks, d_o, o, lse):
    s = proj.shape[1]
    nq = A_BLOCKS_PER_STEP_BWD
    rows = BLK * nq
    steps = s // rows

    def body(sink_ref, q_ref, kp_ref, kc_ref, vp_ref, vc_ref, do_ref, o_ref, lse_ref,
             dq_ref, dk_ref, dv_ref, dsink_ref, kcar, vcar):
        n = pl.program_id(0)

        @pl.when(n == 0)
        def _():
            kcar[...] = jnp.zeros_like(kcar)
            vcar[...] = jnp.zeros_like(vcar)
            dsink_ref[...] = jnp.zeros_like(dsink_ref)

        dk_ref[...] = kcar[...]
        dv_ref[...] = vcar[...]

        @pl.when(n < steps)
        def _():
            base_rest = _band_base(A_MAX_DIST, 1, False)
            base_0 = jnp.where(n > 0, base_rest, _band_base(A_MAX_DIST, 1, True))
            for i in range(nq):
                cur = pl.ds(i * BLK, BLK)
                k_prev = kc_ref[pl.ds((i - 1) * BLK, BLK), :] if i > 0 else kp_ref[...]
                v_prev = vc_ref[pl.ds((i - 1) * BLK, BLK), :] if i > 0 else vp_ref[...]
                kb = jnp.concatenate([k_prev, kc_ref[cur, :]], axis=0).astype(BF16)
                vb = jnp.concatenate([v_prev, vc_ref[cur, :]], axis=0).astype(BF16)
                dk_win = dv_win = None
                for j in range(NH // 2):
                    g = j // 2
                    dq2, dk2, dv2, dsk = _pair_bwd(q_ref[j, cur, :], kb, vb, do_ref[j, cur, :], o_ref[j, cur, :],
                                                   lse_ref[j, cur, :], base_rest if i > 0 else base_0,
                                                   (SLOPES[2 * j], SLOPES[2 * j + 1]), (g, g),
                                                   (sink_ref[2 * j], sink_ref[2 * j + 1]))
                    dq_ref[j, cur, :] = dq2
                    dk_win = dk2 if j == 0 else dk_win + dk2
                    dv_win = dv2 if j == 0 else dv_win + dv2
                    for e in (0, 1):
                        h = 2 * j + e
                        dsink_ref[h:h + 1, :] += jnp.broadcast_to(dsk[e], (1, 128))
                if i == 0:
                    last = pl.ds((nq - 1) * BLK, BLK)
                    dk_ref[last, :] += dk_win[:BLK]
                    dv_ref[last, :] += dv_win[:BLK]
                else:
                    kcar[pl.ds((i - 1) * BLK, BLK), :] += dk_win[:BLK]
                    vcar[pl.ds((i - 1) * BLK, BLK), :] += dv_win[:BLK]
                kcar[cur, :] = dk_win[BLK:]
                vcar[cur, :] = dv_win[BLK:]

    cur_step = lambda n: jnp.minimum(n, steps - 1)
    before = lambda n: jnp.maximum(cur_step(n) * nq - 1, 0)
    out_prev = lambda n: jnp.maximum(n - 1, 0)
    quad = pl.BlockSpec((4, rows, 128), lambda n: (0, cur_step(n), 0))
    slab = lambda g: pl.BlockSpec((None, rows, 128), lambda n: (g, cur_step(n), 0))
    edge = lambda g: pl.BlockSpec((None, BLK, 128), lambda n: (g, before(n), 0))
    return pl.pallas_call(
        body, name="attn_a_bwd", grid=(steps + 1,),
        in_specs=[SMEM, quad, edge(4), slab(4), edge(5), slab(5), quad, quad, quad],
        out_specs=[quad,
                   pl.BlockSpec((rows, 128), lambda n: (out_prev(n), 0)),
                   pl.BlockSpec((rows, 128), lambda n: (out_prev(n), 0)),
                   pl.BlockSpec((NH, 128), lambda n: (0, 0))],
        out_shape=[jax.ShapeDtypeStruct((4, s, 128), F32), jax.ShapeDtypeStruct((s, 128), F32),
                   jax.ShapeDtypeStruct((s, 128), F32), jax.ShapeDtypeStruct((NH, 128), F32)],
        scratch_shapes=[pltpu.VMEM((rows, 128), F32), pltpu.VMEM((rows, 128), F32)],
        compiler_params=_cp(("arbitrary",)),
    )(sinks, proj, proj, proj, proj, proj, d_o, o, lse)


def _stream(rho, i, r):
    start = i * BLK * r + rho
    return pl.ds(start, BLK, stride=r) if r > 1 else pl.ds(start, BLK)


def _for_streams(r, fn, side_by_side=4):
    if r <= side_by_side:
        for rho in range(r):
            fn(rho)
    else:
        def group(it, carry):
            for u in range(side_by_side):
                fn(side_by_side * it + u)
            return carry

        lax.fori_loop(0, r // side_by_side, group, 0)


B_BLOCKS_PER_STEP = {1: 8, 4: 2, 16: 1}
B_BLOCKS_PER_STEP_FWD = {1: 8, 4: 2, 16: 1}


def _attn_b_fwd(proj, slopes, r):
    s = proj.shape[1]
    nq = B_BLOCKS_PER_STEP_FWD[r]
    rows = BLK * r * nq
    steps = s // rows
    qc, kc, vc = WA // 128, WA // 128 + 4, WA // 128 + 8

    def body(slope_ref, q_ref, kp_ref, kc_ref, vp_ref, vc_ref, o_ref, lse_ref):
        j = pl.program_id(0)
        sb = pl.program_id(1)
        sl2 = (slope_ref[2 * j], slope_ref[2 * j + 1])
        bias_rest = _stack_heads(sl2, _band_base(B_MAX_DIST, r, False))
        bias_0 = jnp.where(sb > 0, bias_rest, _stack_heads(sl2, _band_base(B_MAX_DIST, r, True)))

        def stream(rho):
            for i in range(nq):
                cur = _stream(rho, i, r)
                k_prev = kc_ref[_stream(rho, i - 1, r), :] if i > 0 else kp_ref[_stream(rho, 0, r), :]
                v_prev = vc_ref[_stream(rho, i - 1, r), :] if i > 0 else vp_ref[_stream(rho, 0, r), :]
                kb = jnp.concatenate([k_prev, kc_ref[cur, :]], axis=0).astype(BF16)
                vb = jnp.concatenate([v_prev, vc_ref[cur, :]], axis=0).astype(BF16)
                o2, lse2 = _pair_fwd(q_ref[cur, :], kb, vb, bias_rest if i > 0 else bias_0, None, (0, 1), None)
                o_ref[cur, :] = o2
                lse_ref[cur, :] = lse2

        _for_streams(r, stream, side_by_side=8)

    before = lambda sb: jnp.maximum(sb * nq - 1, 0)
    return pl.pallas_call(
        body, name=f"attn_b_fwd_r{r}", grid=(NH // 2, steps),
        in_specs=[SMEM,
                  pl.BlockSpec((None, rows, 128), lambda j, sb: (qc + j, sb, 0)),
                  pl.BlockSpec((None, BLK * r, 128), lambda j, sb: (kc + j, before(sb), 0)),
                  pl.BlockSpec((None, rows, 128), lambda j, sb: (kc + j, sb, 0)),
                  pl.BlockSpec((None, BLK * r, 128), lambda j, sb: (vc + j, before(sb), 0)),
                  pl.BlockSpec((None, rows, 128), lambda j, sb: (vc + j, sb, 0))],
        out_specs=[pl.BlockSpec((None, rows, 128), lambda j, sb: (j, sb, 0))] * 2,
        out_shape=[jax.ShapeDtypeStruct((4, s, 128), F32)] * 2,
        compiler_params=_cp(("parallel", "parallel")),
    )(slopes, proj, proj, proj, proj, proj)


def _attn_b_bwd(proj, slopes, d_o, o, lse, r, so_far=None):
    s = proj.shape[1]
    nq = B_BLOCKS_PER_STEP[r]
    rows = BLK * r * nq
    steps = s // rows
    qc, kc, vc = WA // 128, WA // 128 + 4, WA // 128 + 8
    chained = so_far is not None

    def body(slope_ref, q_ref, kp_ref, kc_ref, vp_ref, vc_ref, do_ref, o_ref, lse_ref, *rest):
        if chained:
            pq_ref, pk_ref, pv_ref, dq_ref, dk_ref, dv_ref, kcar, vcar = rest
        else:
            dq_ref, dk_ref, dv_ref, kcar, vcar = rest
        j = pl.program_id(0)
        sb = pl.program_id(1)

        @pl.when(sb == 0)
        def _():
            kcar[...] = jnp.zeros_like(kcar)
            vcar[...] = jnp.zeros_like(vcar)

        if chained:
            dk_ref[...] = kcar[...] + pk_ref[...]
            dv_ref[...] = vcar[...] + pv_ref[...]
        else:
            dk_ref[...] = kcar[...]
            dv_ref[...] = vcar[...]

        @pl.when(sb < steps)
        def _():
            sl2 = (slope_ref[2 * j], slope_ref[2 * j + 1])
            bias_rest = _stack_heads(sl2, _band_base(B_MAX_DIST, r, False))
            bias_0 = jnp.where(sb > 0, bias_rest, _stack_heads(sl2, _band_base(B_MAX_DIST, r, True)))

            def stream(rho):
                for i in range(nq):
                    cur = _stream(rho, i, r)
                    k_prev = kc_ref[_stream(rho, i - 1, r), :] if i > 0 else kp_ref[_stream(rho, 0, r), :]
                    v_prev = vc_ref[_stream(rho, i - 1, r), :] if i > 0 else vp_ref[_stream(rho, 0, r), :]
                    kb = jnp.concatenate([k_prev, kc_ref[cur, :]], axis=0).astype(BF16)
                    vb = jnp.concatenate([v_prev, vc_ref[cur, :]], axis=0).astype(BF16)
                    dq2, dk2, dv2, _ = _pair_bwd(q_ref[cur, :], kb, vb, do_ref[cur, :], o_ref[cur, :], lse_ref[cur, :],
                                                 bias_rest if i > 0 else bias_0, None, (0, 1), None)
                    dq_ref[cur, :] = dq2 + pq_ref[cur, :] if chained else dq2
                    if i == 0:
                        last = _stream(rho, nq - 1, r)
                        dk_ref[last, :] += dk2[:BLK]
                        dv_ref[last, :] += dv2[:BLK]
                    else:
                        kcar[_stream(rho, i - 1, r), :] += dk2[:BLK]
                        vcar[_stream(rho, i - 1, r), :] += dv2[:BLK]
                    kcar[cur, :] = dk2[BLK:]
                    vcar[cur, :] = dv2[BLK:]

            _for_streams(r, stream, side_by_side=8)

    cur_step = lambda sb: jnp.minimum(sb, steps - 1)
    before = lambda sb: jnp.maximum(cur_step(sb) * nq - 1, 0)
    out_prev = lambda sb: jnp.maximum(sb - 1, 0)
    tile = lambda slab: pl.BlockSpec((None, rows, 128), lambda j, sb: (slab + j, cur_step(sb), 0))
    edge = lambda slab: pl.BlockSpec((None, BLK * r, 128), lambda j, sb: (slab + j, before(sb), 0))
    late = pl.BlockSpec((None, rows, 128), lambda j, sb: (j, out_prev(sb), 0))
    grads = [tile(0), late, late]
    return pl.pallas_call(
        body, name=f"attn_b_bwd_r{r}", grid=(NH // 2, steps + 1),
        in_specs=[SMEM, tile(qc), edge(kc), tile(kc), edge(vc), tile(vc), tile(0), tile(0), tile(0)]
        + (grads if chained else []),
        out_specs=grads,
        out_shape=[jax.ShapeDtypeStruct((4, s, 128), F32)] * 3,
        scratch_shapes=[pltpu.VMEM((rows, 128), F32), pltpu.VMEM((rows, 128), F32)],
        compiler_params=_cp(("parallel", "arbitrary")),
    )(slopes, proj, proj, proj, proj, proj, d_o, o, lse, *(so_far if chained else ()))


def _row(v):
    return v.reshape(1, -1)


def _layer_norm_stats(z):
    mu = jnp.mean(z, axis=-1, keepdims=True)
    zc = z - mu
    var = jnp.mean(zc * zc, axis=-1, keepdims=True)
    rstd = lax.rsqrt(var + LN_EPS)
    return zc * rstd, rstd


def _layer_norm_bwd(dh, zh, rstd, g):
    dzh = dh * g
    return rstd * (dzh - jnp.mean(dzh, axis=-1, keepdims=True) - zh * jnp.mean(dzh * zh, axis=-1, keepdims=True))


def _rms(o):
    return lax.rsqrt(jnp.mean(o * o, axis=-1, keepdims=True) + RMS_EPS)


def _mix_ln1(x, o_a, o_b, lse_b, norm_a_g, norm_b_g, w_o, ln1_g, ln1_b, tm=256):
    s = x.shape[0]

    def wide(ref):
        return jnp.concatenate([ref[j] for j in range(4)], axis=1)

    def body(x_ref, oa_ref, ob1, ob2, ob3, l1, l2, l3, ga_ref, gb_ref, wo_ref, g_ref, b_ref,
             obm_ref, lse_ref, cat_ref, z1_ref, h1_ref, h1b_ref):
        la, lb, lc = wide(l1), wide(l2), wide(l3)
        m = jnp.maximum(jnp.maximum(la, lb), lc)
        ea, eb, ec = jnp.exp(la - m), jnp.exp(lb - m), jnp.exp(lc - m)
        den = ea + eb + ec
        obm = (ea / den) * wide(ob1) + (eb / den) * wide(ob2) + (ec / den) * wide(ob3)
        lse = m + jnp.log(den)
        for j in range(4):
            obm_ref[j] = obm[:, 128 * j:128 * (j + 1)]
            lse_ref[j] = lse[:, 128 * j:128 * (j + 1)]
        oa = wide(oa_ref)
        na = oa * _rms(oa) * ga_ref[...]
        nb_ = obm * _rms(obm) * gb_ref[...]
        cat = jnp.concatenate([na, nb_], axis=1).astype(BF16)
        cat_ref[...] = cat
        z1 = ALPHA * x_ref[...] + _nn(cat, wo_ref[...])
        z1_ref[...] = z1
        zh, _ = _layer_norm_stats(z1)
        h1 = zh * g_ref[...] + b_ref[...]
        h1_ref[...] = h1
        h1b_ref[...] = h1.astype(BF16)

    t512 = pl.BlockSpec((4, tm, 128), lambda i: (0, i, 0))
    td = pl.BlockSpec((tm, D), lambda i: (i, 0))
    return pl.pallas_call(
        body, name="mix_ln1", grid=(s // tm,),
        in_specs=[td] + [t512] * 7 + [_const((1, 512))] * 2 + [_resident((D, D))] + [_const((1, D))] * 2,
        out_specs=[t512, t512, td, td, td, td],
        out_shape=[jax.ShapeDtypeStruct((4, s, 128), F32), jax.ShapeDtypeStruct((4, s, 128), F32),
                   jax.ShapeDtypeStruct((s, D), BF16), jax.ShapeDtypeStruct((s, D), F32),
                   jax.ShapeDtypeStruct((s, D), F32), jax.ShapeDtypeStruct((s, D), BF16)],
        compiler_params=_cp(("parallel",)),
    )(x, o_a, *o_b, *lse_b, _row(norm_a_g), _row(norm_b_g), w_o, _row(ln1_g), _row(ln1_b))


def _gelu_and_grad(x):
    c = math.sqrt(2.0 / math.pi)
    x2 = x * x
    cx = c * x
    t = jnp.tanh(cx * (1.0 + 0.044715 * x2))
    q = 1.0 + t
    g = (0.5 * x) * q
    dg = 0.5 * q + ((0.5 * cx) * (1.0 - t * t)) * (1.0 + (3.0 * 0.044715) * x2)
    return g, dg


def _shift_down(u, before):
    n = u.shape[0]
    ext = jnp.concatenate([before, u], axis=0)
    return pltpu.roll(ext, 1, 0)[8:], pltpu.roll(ext, 2, 0)[8:]


def _shift_up(u, after):
    n = u.shape[0]
    ext = jnp.concatenate([u, after], axis=0)
    return pltpu.roll(ext, n + 7, 0)[:n], pltpu.roll(ext, n + 6, 0)[:n]


def _up_conv_gelu(h1b, w_up, cwb, tm=256, tn=FF // 2, chunk_rows=16, piece_cols=512):
    s = h1b.shape[0]
    n_i = s // tm
    n_t = (FF // tn) * n_i

    def body(h_ref, wg_ref, wv_ref, c_ref, up_ref, a_ref, g_ref, a1_ref, pend_a, pend_b, carry):
        t = pl.program_id(0)
        row_tile = jnp.maximum(t - 1, 0) % n_i
        w_refs = (wg_ref, wv_ref)

        @pl.when(t == 0)
        def _():
            pend_b[...] = jnp.zeros_like(pend_b)
            carry[...] = jnp.zeros_like(carry)

        def step(dst, src):
            def chunk(c, before):
                rows = pl.ds(c * chunk_rows, chunk_rows)
                u, last = [], []
                for half in (0, 1):
                    up = src[half, rows, :]
                    r1, r2 = _shift_down(up, before[half])
                    u.append(r2 * c_ref[0, half:half + 1, :] + r1 * c_ref[1, half:half + 1, :]
                             + up * c_ref[2, half:half + 1, :] + c_ref[3, half:half + 1, :])
                    last.append(up[chunk_rows - 8:])
                g, dg = _gelu_and_grad(u[0])
                a_ref[rows, :] = (g * u[1]).astype(BF16)
                g_ref[rows, :] = g.astype(BF16)
                a1_ref[rows, :] = (u[1] * dg).astype(BF16)
                return tuple(last)

            edge = tuple(jnp.where(row_tile > 0, carry[half], 0.0) for half in (0, 1))
            pieces = [(half, c0, min(piece_cols, tn - c0)) for half in (0, 1) for c0 in range(0, tn, piece_cols)]
            n_c = tm // chunk_rows
            done = 0
            for p, (half, c0, width) in enumerate(pieces):
                cols = slice(c0, c0 + width)
                up = _nn(h_ref[...], w_refs[half][:, cols])
                up_ref[half, :, cols] = up.astype(BF16)
                dst[half, :, cols] = up
                upto = n_c * (p + 1) // len(pieces)
                for c in range(done, upto):
                    edge = chunk(c, edge)
                done = upto
            for half in (0, 1):
                carry[half] = edge[half]

        @pl.when(t % 2 == 0)
        def _():
            step(pend_a, pend_b)

        @pl.when(t % 2 == 1)
        def _():
            step(pend_b, pend_a)

    mm = lambda t: jnp.minimum(t, n_t - 1)
    ew = lambda t: jnp.maximum(t - 1, 0)
    out_tile = pl.BlockSpec((tm, tn), lambda t: (ew(t) % n_i, ew(t) // n_i))
    return pl.pallas_call(
        body, name="up_conv_gelu", grid=(n_t + 1,),
        in_specs=[pl.BlockSpec((tm, D), lambda t: (mm(t) % n_i, 0)),
                  pl.BlockSpec((D, tn), lambda t: (0, mm(t) // n_i)),
                  pl.BlockSpec((D, tn), lambda t: (0, FF // tn + mm(t) // n_i)),
                  pl.BlockSpec((4, 2, tn), lambda t: (0, 0, ew(t) // n_i))],
        out_specs=[pl.BlockSpec((2, tm, tn), lambda t: (0, mm(t) % n_i, mm(t) // n_i)), out_tile, out_tile, out_tile],
        out_shape=[jax.ShapeDtypeStruct((2, s, FF), BF16)] + [jax.ShapeDtypeStruct((s, FF), BF16)] * 3,
        scratch_shapes=[pltpu.VMEM((2, tm, tn), F32), pltpu.VMEM((2, tm, tn), F32), pltpu.VMEM((2, 8, tn), F32)],
        compiler_params=_cp(("arbitrary",)),
    )(h1b, w_up, w_up, cwb)


def _down_ln2_loss(a, w_down, h1, target, ln2_g, ln2_b, tm=512):
    s = a.shape[0]

    def body(a_ref, w_ref, h_ref, t_ref, g_ref, b_ref, dz_ref, dzb_ref, st_ref):
        @pl.when(pl.program_id(0) == 0)
        def _():
            st_ref[...] = jnp.zeros_like(st_ref)

        z2 = ALPHA * h_ref[...] + _nn(a_ref[...], w_ref[...])
        zh, rstd = _layer_norm_stats(z2)
        diff = zh * g_ref[...] + b_ref[...] - t_ref[...]
        part = 0.5 * jnp.sum(jnp.mean(diff * diff, axis=-1, keepdims=True), axis=0, keepdims=True)
        dy = diff * (1.0 / D)
        st_ref[0:1, :] += jnp.sum(dy * zh, axis=0, keepdims=True)
        st_ref[1:2, :] += jnp.sum(dy, axis=0, keepdims=True)
        st_ref[2:3, :] += jnp.broadcast_to(part, (1, D))
        dz = _layer_norm_bwd(dy, zh, rstd, g_ref[...])
        dz_ref[...] = dz
        dzb_ref[...] = dz.astype(BF16)

    td = pl.BlockSpec((tm, D), lambda i: (i, 0))
    return pl.pallas_call(
        body, name="down_ln2_loss", grid=(s // tm,),
        in_specs=[pl.BlockSpec((tm, FF), lambda i: (i, 0)), _resident((FF, D)), td, td, _const((1, D)), _const((1, D))],
        out_specs=[td, td, _const((8, D))],
        out_shape=[jax.ShapeDtypeStruct((s, D), F32), jax.ShapeDtypeStruct((s, D), BF16),
                   jax.ShapeDtypeStruct((8, D), F32)],
        compiler_params=_cp(("arbitrary",)),
    )(a, w_down, h1, target, _row(ln2_g), _row(ln2_b))


def _d_act(dz2b, w_down, tm=512):
    s = dz2b.shape[0]

    def body(dz_ref, w_ref, o_ref):
        o_ref[...] = _nt(dz_ref[...], w_ref[...])

    return pl.pallas_call(
        body, name="d_act", grid=(s // tm,),
        in_specs=[pl.BlockSpec((tm, D), lambda i: (i, 0)), _resident((FF, D))],
        out_specs=pl.BlockSpec((tm, FF), lambda i: (i, 0)),
        out_shape=jax.ShapeDtypeStruct((s, FF), F32),
        compiler_params=_cp(("parallel",)),
    )(dz2b, w_down)


def _conv_gelu_bwd(da, up, g, a1, cwb, tm=256, tn=FF // 2, chunk_rows=16):
    s = da.shape[0]
    n_i = s // tm
    n_c = tm // chunk_rows

    def body(da_ref, up_ref, g_ref, a1_ref, c_ref, dup_ref, dc_ref, carry):
        @pl.when(pl.program_id(1) == 0)
        def _():
            carry[...] = jnp.zeros_like(carry)
            dc_ref[...] = jnp.zeros_like(dc_ref)

        def fold(v):
            return jnp.sum(v.reshape(chunk_rows // 8, 8, v.shape[1]), axis=0)

        def chunk(cc, state):
            after, sums = state
            rows = pl.ds((n_c - 1 - cc) * chunk_rows, chunk_rows)
            da_c = da_ref[rows, :]
            dus = (da_c * a1_ref[rows, :].astype(F32), da_c * g_ref[rows, :].astype(F32))
            head, new_sums = [], []
            for half in (0, 1):
                du = dus[half]
                up = up_ref[half, rows, :].astype(F32)
                l1, l2 = _shift_up(du, after[half])
                dup = (du * c_ref[2, half:half + 1, :] + l1 * c_ref[1, half:half + 1, :]
                       + l2 * c_ref[0, half:half + 1, :])
                dup_ref[half, rows, :] = dup.astype(BF16)
                parts = (fold(l2 * up), fold(l1 * up), fold(du * up), fold(du))
                new_sums.append(parts if sums is None else tuple(a + b for a, b in zip(sums[half], parts)))
                head.append(du[:8])
            return tuple(head), new_sums

        state = ((carry[0], carry[1]), None)
        for cc in range(n_c):
            state = chunk(cc, state)
        head, sums = state
        for half in (0, 1):
            carry[half] = head[half]
            for k in range(4):
                dc_ref[k, half:half + 1, :] += jnp.sum(sums[half][k], axis=0, keepdims=True)

    rev = lambda ii: n_i - 1 - ii
    tile = pl.BlockSpec((tm, tn), lambda j, ii: (rev(ii), j))
    pair = pl.BlockSpec((2, tm, tn), lambda j, ii: (0, rev(ii), j))
    per_col = pl.BlockSpec((4, 2, tn), lambda j, ii: (0, 0, j))
    return pl.pallas_call(
        body, name="conv_gelu_bwd", grid=(FF // tn, n_i),
        in_specs=[tile, pair, tile, tile, per_col],
        out_specs=[pair, per_col],
        out_shape=[jax.ShapeDtypeStruct((2, s, FF), BF16), jax.ShapeDtypeStruct((4, 2, FF), F32)],
        scratch_shapes=[pltpu.VMEM((2, 8, tn), F32)],
        compiler_params=_cp(("parallel", "arbitrary")),
    )(da, up, g, a1, cwb)


def _dh1_ln1_bwd(dz2, dup, w_up, z1, ln1_g, tm=512):
    s = dz2.shape[0]

    def body(dz2_ref, dup_ref, w_ref, z1_ref, g_ref, dz1_ref, dz1b_ref, st_ref):
        @pl.when(pl.program_id(0) == 0)
        def _():
            st_ref[...] = jnp.zeros_like(st_ref)

        dh = ALPHA * dz2_ref[...] + _nt(dup_ref[0], w_ref[:, :FF]) + _nt(dup_ref[1], w_ref[:, FF:])
        zh, rstd = _layer_norm_stats(z1_ref[...])
        st_ref[0:1, :] += jnp.sum(dh * zh, axis=0, keepdims=True)
        st_ref[1:2, :] += jnp.sum(dh, axis=0, keepdims=True)
        dz = _layer_norm_bwd(dh, zh, rstd, g_ref[...])
        dz1_ref[...] = dz
        dz1b_ref[...] = dz.astype(BF16)

    td = pl.BlockSpec((tm, D), lambda i: (i, 0))
    return pl.pallas_call(
        body, name="dh1_ln1_bwd", grid=(s // tm,),
        in_specs=[td, pl.BlockSpec((2, tm, FF), lambda i: (0, i, 0)), _resident((D, 2 * FF)), td, _const((1, D))],
        out_specs=[td, td, _const((8, D))],
        out_shape=[jax.ShapeDtypeStruct((s, D), F32), jax.ShapeDtypeStruct((s, D), BF16),
                   jax.ShapeDtypeStruct((8, D), F32)],
        compiler_params=_cp(("arbitrary",), 58),
    )(dz2, dup, w_up, z1, _row(ln1_g))


def _dcat_rms_bwd(dz1b, w_o, o_a, o_b, norm_a_g, norm_b_g, tm=512):
    s = dz1b.shape[0]

    def body(dz_ref, w_ref, oa_ref, ob_ref, ga_ref, gb_ref, da_ref, db_ref, st_ref):
        @pl.when(pl.program_id(0) == 0)
        def _():
            st_ref[...] = jnp.zeros_like(st_ref)

        dcat = _nt(dz_ref[...], w_ref[...])
        for k, (o_ref, g_ref, d_ref) in enumerate(((oa_ref, ga_ref, da_ref), (ob_ref, gb_ref, db_ref))):
            o = jnp.concatenate([o_ref[j] for j in range(4)], axis=1)
            dn = dcat[:, 512 * k:512 * (k + 1)]
            rr = _rms(o)
            oh = o * rr
            st_ref[k:k + 1, :] += jnp.sum(dn * oh, axis=0, keepdims=True)
            doh = dn * g_ref[...]
            d_o = rr * (doh - oh * jnp.mean(doh * oh, axis=-1, keepdims=True))
            for j in range(4):
                d_ref[j] = d_o[:, 128 * j:128 * (j + 1)]

    t512 = pl.BlockSpec((4, tm, 128), lambda i: (0, i, 0))
    return pl.pallas_call(
        body, name="dcat_rms_bwd", grid=(s // tm,),
        in_specs=[pl.BlockSpec((tm, D), lambda i: (i, 0)), _resident((D, D)), t512, t512,
                  _const((1, 512)), _const((1, 512))],
        out_specs=[t512, t512, _const((8, 512))],
        out_shape=[jax.ShapeDtypeStruct((4, s, 128), F32), jax.ShapeDtypeStruct((4, s, 128), F32),
                   jax.ShapeDtypeStruct((8, 512), F32)],
        compiler_params=_cp(("arbitrary",)),
    )(dz1b, w_o, o_a, o_b, _row(norm_a_g), _row(norm_b_g))


def _dproj_combine(dqa, dka, dva, dqkv_b, tm=256):
    s = dka.shape[0]

    def body(qa, ka, va, qb, kb, vb, o_ref):
        for j in range(4):
            o_ref[:, 128 * j:128 * (j + 1)] = qa[j].astype(BF16)
            o_ref[:, 768 + 128 * j:768 + 128 * (j + 1)] = qb[j].astype(BF16)
            o_ref[:, 1280 + 128 * j:1280 + 128 * (j + 1)] = kb[j].astype(BF16)
            o_ref[:, 1792 + 128 * j:1792 + 128 * (j + 1)] = vb[j].astype(BF16)
        o_ref[:, 512:640] = ka[...].astype(BF16)
        o_ref[:, 640:768] = va[...].astype(BF16)

    t512 = pl.BlockSpec((4, tm, 128), lambda i: (0, i, 0))
    t128 = pl.BlockSpec((tm, 128), lambda i: (i, 0))
    return pl.pallas_call(
        body, name="dproj_combine", grid=(s // tm,),
        in_specs=[t512, t128, t128] + [t512] * 3,
        out_specs=pl.BlockSpec((tm, WIN), lambda i: (i, 0)),
        out_shape=jax.ShapeDtypeStruct((s, WIN), BF16),
        compiler_params=_cp(("parallel",)),
    )(dqa, dka, dva, *dqkv_b)


def _grad_x(dz1, dproj, w_in_t, zero, tm=512):
    s = dz1.shape[0]

    def body(dz_ref, dp_ref, w_ref, z_ref, o_ref):
        o_ref[...] = ALPHA * dz_ref[...] + _nn(dp_ref[...], w_ref[...]) + z_ref[0:1, 0:1]

    td = pl.BlockSpec((tm, D), lambda i: (i, 0))
    return pl.pallas_call(
        body, name="grad_x", grid=(s // tm,),
        in_specs=[td, pl.BlockSpec((tm, WIN), lambda i: (i, 0)), _resident((WIN, D)), _const((8, 128))],
        out_specs=td, out_shape=jax.ShapeDtypeStruct((s, D), F32),
        compiler_params=_cp(("parallel",)),
    )(dz1, dproj, w_in_t, zero)


def _place():
    return lax.axis_index("x"), lax.axis_index("y"), lax.axis_index("c")


def _other_chips(x, y):
    return [(1 - x, y), (x, 1 - y), (1 - x, 1 - y)]


def _hbm(a):
    return pltpu.with_memory_space_constraint(a, pltpu.HBM)


def _gather_w_in(shard, conv_w):
    rows_k = shard.shape[0]
    half = rows_k // 2

    def body(src, conv_src, out, conv_out, send_sems, recv_sems):
        x, y, c = _place()
        b = 2 * x + y
        sibling = (x, y, 1 - c)
        chips = _other_chips(x, y)

        def copy(idx, chip_b, core, to, first_hop=False):
            rows = out.at[pl.ds(pl.multiple_of(chip_b * rows_k + core * half, 16), half)]
            s_ref = src.at[pl.ds(pl.multiple_of(core * half, 16), half)] if first_hop else rows
            return pltpu.make_async_remote_copy(src_ref=s_ref, dst_ref=rows, send_sem=send_sems.at[idx],
                                                recv_sem=recv_sems.at[idx], device_id=to, device_id_type=MESH)

        def own_copy():
            return pltpu.make_async_remote_copy(
                src_ref=src, dst_ref=out.at[pl.ds(pl.multiple_of(b * rows_k, 16), rows_k)], send_sem=send_sems.at[6],
                recv_sem=recv_sems.at[6], device_id=sibling, device_id_type=MESH)

        def conv_copy(idx, chip_b, to):
            return pltpu.make_async_remote_copy(src_ref=conv_src, dst_ref=conv_out.at[chip_b],
                                                send_sem=send_sems.at[7 + idx], recv_sem=recv_sems.at[7 + idx],
                                                device_id=to, device_id_type=MESH)

        started = [own_copy(), conv_copy(3, b, sibling)]
        for jn, chip in enumerate(chips):
            started += [copy(jn, b, c, (chip[0], chip[1], c), first_hop=True), conv_copy(jn, b, (chip[0], chip[1], c))]
        for cp in started:
            cp.start()
        for jn, chip in enumerate(chips):
            cb = 2 * chip[0] + chip[1]
            copy(jn, cb, c, (chip[0], chip[1], c)).wait_recv()
            cp = copy(3 + jn, cb, c, sibling)
            cp.start()
            started.append(cp)
        for jn, chip in enumerate(chips):
            cb = 2 * chip[0] + chip[1]
            copy(3 + jn, cb, 1 - c, sibling).wait_recv()
            conv_copy(jn, cb, (chip[0], chip[1], c)).wait_recv()
        own_copy().wait_recv()
        conv_copy(3, b, sibling).wait_recv()
        for cp in started:
            cp.wait_send()

    return pl.pallas_call(
        body, name="gather_w_in",
        in_specs=[ANY, ANY], out_specs=[ANY, ANY],
        out_shape=[jax.ShapeDtypeStruct((N_CHIPS * rows_k, D), BF16), jax.ShapeDtypeStruct((N_CHIPS,) + conv_w.shape, F32)],
        scratch_shapes=[pltpu.SemaphoreType.DMA((11,)), pltpu.SemaphoreType.DMA((11,))],
        compiler_params=pltpu.CompilerParams(has_side_effects=True),
    )(shard, conv_w)


def _weight_copies(shard, land, send_sems, recv_sems, arrivals):
    x, y, c = _place()
    n_rows, n_cols = shard.shape
    peers = [(px, py, c) for px, py in _other_chips(x, y)] + [(x, y, 1 - c)]
    cps = []
    for jn, peer in enumerate(peers):
        at = 2 * peer[0] + peer[1] if arrivals else 2 * x + y
        if land.shape[1] == n_cols:
            dst = land.at[pl.ds(pl.multiple_of(at * n_rows, 16), n_rows)]
        else:
            dst = land.at[:, pl.ds(pl.multiple_of(at * n_cols, 128), n_cols)]
        cps.append(pltpu.make_async_remote_copy(src_ref=shard, dst_ref=dst, send_sem=send_sems.at[jn],
                                                recv_sem=recv_sems.at[jn], device_id=peer, device_id_type=MESH))
    return cps


def _weights_start(shards, after):
    n = len(shards)
    lands = [lax.empty((N_CHIPS * sh.shape[0], D) if sh.shape[1] == D else (D, N_CHIPS * sh.shape[1]), BF16)
             for sh in shards]

    def body(*refs):
        src, land = refs[:n], refs[n:2 * n]
        send_sems, recv_sems = refs[2 * n + 1:3 * n + 1], refs[3 * n + 1:4 * n + 1]
        for k in range(n):
            for send in _weight_copies(src[k], land[k], send_sems[k], recv_sems[k], False):
                send.start()
        refs[-1][...] = jnp.zeros_like(refs[-1])

    res = pl.pallas_call(
        body, name="weights_start",
        in_specs=[HBM] * (2 * n) + [ANY], out_specs=[SEM] * (2 * n) + [HBM] * (2 * n) + [VMEM],
        out_shape=[pltpu.SemaphoreType.DMA((4,))] * (2 * n)
        + [pltpu.HBM(a.shape, a.dtype) for a in (*shards, *lands)] + [jax.ShapeDtypeStruct((8, 128), F32)],
        input_output_aliases={i: i + 2 * n for i in range(2 * n)},
        compiler_params=pltpu.CompilerParams(has_side_effects=DATAFLOW),
    )(*[_hbm(a) for a in (*shards, *lands)], after)
    return [(res[k], res[n + k], res[2 * n + k], res[3 * n + k]) for k in range(n)], res[-1]


def _weights_wait(started, after, name):
    send_sems, recv_sems, shard, land = started

    def body(s_ref, l_ref, send_ref, recv_ref, after_ref, s_out, l_out):
        for cp in _weight_copies(s_ref, l_ref, send_ref, recv_ref, True):
            cp.wait_send()
            cp.wait_recv()

    return pl.pallas_call(
        body, name=name,
        in_specs=[HBM, HBM, SEM, SEM, ANY], out_specs=[HBM, HBM],
        out_shape=[pltpu.HBM(shard.shape, shard.dtype), pltpu.HBM(land.shape, land.dtype)],
        input_output_aliases={0: 0, 1: 1},
        compiler_params=pltpu.CompilerParams(has_side_effects=DATAFLOW),
    )(shard, land, send_sems, recv_sems, after)[1]


def _grad_copies(g_ref, land_ref, send_sems, recv_sems):
    x, y, c = _place()
    cps = []
    for d in range(1, 8):
        px, py, pc = x ^ (d >> 2), y ^ ((d >> 1) & 1), c ^ (d & 1)
        cps.append(pltpu.make_async_remote_copy(
            src_ref=g_ref.at[2 * px + py, pc], dst_ref=land_ref.at[d - 1], send_sem=send_sems.at[d - 1],
            recv_sem=recv_sems.at[d - 1], device_id=(px, py, pc), device_id_type=MESH))
    return cps


def _grads_start(grads_b, name):
    n = len(grads_b)
    lands = [lax.empty((7, g.shape[2], D), BF16) for g in grads_b]

    def body(*refs):
        g, land = refs[:n], refs[n:2 * n]
        send_sems, recv_sems = refs[2 * n:3 * n], refs[3 * n:4 * n]
        for k in range(n):
            for cp in _grad_copies(g[k], land[k], send_sems[k], recv_sems[k]):
                cp.start()
        refs[-1][...] = jnp.zeros_like(refs[-1])

    res = pl.pallas_call(
        body, name=name,
        in_specs=[HBM] * (2 * n), out_specs=[SEM] * (2 * n) + [HBM] * (2 * n) + [VMEM],
        out_shape=[pltpu.SemaphoreType.DMA((7,))] * (2 * n)
        + [pltpu.HBM(a.shape, a.dtype) for a in (*grads_b, *lands)] + [jax.ShapeDtypeStruct((8, 128), F32)],
        input_output_aliases={i: i + 2 * n for i in range(2 * n)},
        compiler_params=pltpu.CompilerParams(has_side_effects=DATAFLOW),
    )(*[_hbm(a) for a in (*grads_b, *lands)])
    return [(res[k], res[n + k], res[2 * n + k], res[3 * n + k]) for k in range(n)], res[-1]


def _grads_wait(started, after, name):
    n = len(started)

    def body(*refs):
        g, land = refs[:n], refs[n:2 * n]
        send_sems, recv_sems = refs[2 * n:3 * n], refs[3 * n:4 * n]
        for k in range(n):
            for cp in _grad_copies(g[k], land[k], send_sems[k], recv_sems[k]):
                cp.wait_send()
                cp.wait_recv()

    gs = [st[2] for st in started]
    lands = [st[3] for st in started]
    res = pl.pallas_call(
        body, name=name,
        in_specs=[HBM] * (2 * n) + [SEM] * (2 * n) + [ANY], out_specs=[HBM] * (2 * n),
        out_shape=[pltpu.HBM(a.shape, a.dtype) for a in (*gs, *lands)],
        input_output_aliases={i: i for i in range(2 * n)},
        compiler_params=pltpu.CompilerParams(has_side_effects=DATAFLOW),
    )(*gs, *lands, *[st[0] for st in started], *[st[1] for st in started], after)
    return res[n:]


def _sum_partials(grad4, got, cb, name, tr):
    h = grad4.shape[2]
    per_half = h // tr

    def body(cb_ref, g_ref, o_ref, out_ref):
        acc = g_ref[...]
        for j in range(7):
            acc = acc + o_ref[j].astype(F32)
        out_ref[...] = acc

    return pl.pallas_call(
        body, name=name,
        grid_spec=pltpu.PrefetchScalarGridSpec(
            num_scalar_prefetch=1, grid=(per_half,),
            in_specs=[pl.BlockSpec((None, None, tr, D), lambda i, cb_ref: (cb_ref[1], cb_ref[0], i, 0)),
                      pl.BlockSpec((7, tr, D), lambda i, cb_ref: (0, i, 0))],
            out_specs=pl.BlockSpec((tr, D), lambda i, cb_ref: (cb_ref[0] * per_half + i, 0))),
        out_shape=jax.ShapeDtypeStruct((2 * h, D), F32),
        compiler_params=_cp(("arbitrary",)),
    )(cb, grad4, got)


def _swap_halves(shards, name):
    n = len(shards)

    def body(*refs):
        out, send_sems, recv_sems = refs[n:2 * n], refs[2 * n], refs[2 * n + 1]
        x, y, c = _place()
        cps = []
        for k in range(n):
            h = shards[k].shape[0] // 2
            mine = out[k].at[pl.ds(pl.multiple_of(c * h, 8), h)]
            cp = pltpu.make_async_remote_copy(src_ref=mine, dst_ref=mine, send_sem=send_sems.at[k],
                                              recv_sem=recv_sems.at[k], device_id=(x, y, 1 - c), device_id_type=MESH)
            cp.start()
            cps.append(cp)
        for cp in cps:
            cp.wait()

    return pl.pallas_call(
        body, name=name,
        in_specs=[ANY] * n, out_specs=[ANY] * n,
        out_shape=[jax.ShapeDtypeStruct(sh.shape, F32) for sh in shards],
        input_output_aliases={k: k for k in range(n)},
        scratch_shapes=[pltpu.SemaphoreType.DMA((n,)), pltpu.SemaphoreType.DMA((n,))],
        compiler_params=pltpu.CompilerParams(has_side_effects=True),
    )(*shards)


def _share_halves(shards, small):
    n = len(shards)
    rows = small.shape[0]

    def body(*refs):
        small_ref = refs[n]
        out, total_ref = refs[n + 1:2 * n + 1], refs[2 * n + 1]
        all_ref, send_sems, recv_sems, ssend, srecv = refs[2 * n + 2:]
        x, y, c = _place()
        me = 4 * x + 2 * y + c
        cps = []
        for k in range(n):
            h = shards[k].shape[0] // 2
            mine = out[k].at[pl.ds(pl.multiple_of(c * h, 8), h)]
            cp = pltpu.make_async_remote_copy(src_ref=mine, dst_ref=mine, send_sem=send_sems.at[k],
                                              recv_sem=recv_sems.at[k], device_id=(x, y, 1 - c), device_id_type=MESH)
            cp.start()
            cps.append(cp)
        all_ref[me] = small_ref[...]
        peers = []
        for d in range(1, 8):
            px, py, pc = x ^ (d >> 2), y ^ ((d >> 1) & 1), c ^ (d & 1)
            cp = pltpu.make_async_remote_copy(src_ref=small_ref, dst_ref=all_ref.at[me],
                                              send_sem=ssend.at[d - 1], recv_sem=srecv.at[d - 1],
                                              device_id=(px, py, pc), device_id_type=MESH)
            cp.start()
            peers.append(cp)
        for cp in peers:
            cp.wait()
        acc = all_ref[0]
        for d in range(1, 8):
            acc = acc + all_ref[d]
        total_ref[...] = acc
        for cp in cps:
            cp.wait()

    return pl.pallas_call(
        body, name="share_halves",
        in_specs=[ANY] * n + [VMEM], out_specs=[ANY] * n + [VMEM],
        out_shape=[jax.ShapeDtypeStruct(sh.shape, F32) for sh in shards] + [jax.ShapeDtypeStruct((rows, D), F32)],
        input_output_aliases={k: k for k in range(n)},
        scratch_shapes=[pltpu.VMEM((8, rows, D), F32), pltpu.SemaphoreType.DMA((n,)), pltpu.SemaphoreType.DMA((n,)),
                        pltpu.SemaphoreType.DMA((7,)), pltpu.SemaphoreType.DMA((7,))],
        compiler_params=pltpu.CompilerParams(has_side_effects=True),
    )(*shards, small)


def _adamw(w, g, m, v, name, tr):
    rows, cols = w.shape

    def body(w_ref, g_ref, m_ref, v_ref, d_ref, nm_ref, nv_ref):
        g_ = g_ref[...]
        nm = ADAM_B1 * m_ref[...] + (1.0 - ADAM_B1) * g_
        nv = ADAM_B2 * v_ref[...] + (1.0 - ADAM_B2) * (g_ * g_)
        m_hat = nm / (1.0 - ADAM_B1 ** ADAM_STEP)
        v_hat = nv / (1.0 - ADAM_B2 ** ADAM_STEP)
        d_ref[...] = -ADAM_LR * (m_hat / (jnp.sqrt(v_hat) + ADAM_EPS) + ADAM_WD * w_ref[...])
        nm_ref[...] = nm
        nv_ref[...] = nv

    spec = pl.BlockSpec((tr, cols), lambda i: (i, 0))
    return pl.pallas_call(
        body, name=name, grid=(rows // tr,),
        in_specs=[spec] * 4, out_specs=[spec] * 3,
        out_shape=[jax.ShapeDtypeStruct((rows, cols), F32)] * 3,
        compiler_params=_cp(("parallel",)),
    )(w, g, m, v)


def _local_step(x, target, w_in_t, late_weights, norm_a_g, norm_b_g, sinks_a, ln1_g, ln1_b,
                conv_w, conv_b, ln2_g, ln2_b, slopes, on_grad):
    cwb = jnp.concatenate([conv_w, conv_b[None]], axis=0).reshape(4, 2, FF)

    proj, xb = _proj(x, w_in_t, "proj")
    o_a, lse_a = _attn_a_fwd(proj, sinks_a)
    fwd_b = [_attn_b_fwd(proj, slopes, r) for r in B_DILATIONS]
    w_o = late_weights(1, fwd_b[-1][1])
    o_b, lse_b, cat, z1, h1, h1b = _mix_ln1(x, o_a, [f[0] for f in fwd_b], [f[1] for f in fwd_b],
                                           norm_a_g, norm_b_g, w_o, ln1_g, ln1_b)
    w_up = late_weights(2, h1b)
    up, a, gate, a1 = _up_conv_gelu(h1b, w_up, cwb)
    w_down = late_weights(3, a)
    dz2, dz2b, st2 = _down_ln2_loss(a, w_down, h1, target, ln2_g, ln2_b)

    on_grad(3, *_grad_w(a, dz2b, "grad_w_down", tm=FF // 2))
    dup, dconv = _conv_gelu_bwd(_d_act(dz2b, w_down), up, gate, a1, cwb)
    on_grad(2, *_grad_w(dup, h1b, "grad_w_up", tm=FF // 2, lhs_halves=True))
    dz1, dz1b, st1 = _dh1_ln1_bwd(dz2, dup, w_up, z1, ln1_g)
    tok = on_grad(1, *_grad_w(cat, dz1b, "grad_w_o", tm=512))
    d_oa, d_ob, st_n = _dcat_rms_bwd(dz1b, w_o, o_a, o_b, norm_a_g + tok[0, 0], norm_b_g)
    dqa, dka, dva, dsink = _attn_a_bwd(proj, sinks_a, d_oa, o_a, lse_a)
    bwd_b = None
    for r in B_DILATIONS:
        bwd_b = _attn_b_bwd(proj, slopes, d_ob, o_b, lse_b, r, bwd_b)
    dproj = _dproj_combine(dqa, dka, dva, bwd_b)
    tok = on_grad(0, *_grad_w(dproj, xb, "grad_w_in", tm=WA))
    gx = _grad_x(dz1, dproj, w_in_t, tok)

    dconv = dconv.reshape(4, 2 * FF)
    small = dict(loss=st2[2, 0:1], norm_a_g=st_n[0], norm_b_g=st_n[1], sinks_a=dsink[:, 0],
                 ln1_g=st1[0], ln1_b=st1[1], conv_w=dconv[0:3].reshape(-1), conv_b=dconv[3],
                 ln2_g=st2[0], ln2_b=st2[1])
    return gx, small


SMALL_ORDER = ("loss", "norm_a_g", "norm_b_g", "sinks_a", "ln1_g", "ln1_b", "conv_b", "ln2_g", "ln2_b", "conv_w")
SMALL_SIZES = dict(loss=1, norm_a_g=512, norm_b_g=512, sinks_a=8, ln1_g=D, ln1_b=D, conv_b=2 * FF, ln2_g=D, ln2_b=D,
                   conv_w=3 * 2 * FF)


def _pack(parts, rows):
    flat = jnp.concatenate([parts[k].reshape(-1).astype(F32) for k in parts])
    return jnp.pad(flat, (0, rows * D - flat.shape[0])).reshape(rows, D)


def _unpack(buf, names, sizes):
    flat = buf.reshape(-1)
    out, at = {}, 0
    for k in names:
        out[k] = flat[at:at + sizes[k]]
        at += sizes[k]
    return out


def kernel(x, w_in, norm_a_g, norm_b_g, sinks_a, w_o, ln1_g, ln1_b, w_up, conv_w, conv_b, w_down, ln2_g, ln2_b, loss_target, m_w_in, m_norm_a_g, m_norm_b_g, m_sinks_a, m_w_o, m_ln1_g, m_ln1_b, m_w_up, m_conv_w, m_conv_b, m_w_down, m_ln2_g, m_ln2_b, v_w_in, v_norm_a_g, v_norm_b_g, v_sinks_a, v_w_o, v_ln1_g, v_ln1_b, v_w_up, v_conv_w, v_conv_b, v_w_down, v_ln2_g, v_ln2_b):
    xi, yi, ci = _place()
    chip = (2 * xi + yi).astype(I32)
    core = ci.astype(I32)

    w_in_rows, m_w_in_rows, v_w_in_rows = w_in.T, m_w_in.T, v_w_in.T
    shards = (w_in_rows.astype(BF16), w_o.astype(BF16), w_up.astype(BF16), w_down.astype(BF16))
    w_in_t, conv_w4 = _gather_w_in(shards[0], conv_w)
    conv_w_f = conv_w4.transpose(1, 0, 2).reshape(3, 2 * FF)
    w_started, w_tok = _weights_start(shards[1:], conv_w4)
    slopes = jnp.asarray(SLOPES, F32) + w_tok[0, 0]

    halves_rows = [r // 2 for r in SHARD_ROWS]
    grads4, grads_b4, started = [None] * 4, [None] * 4, [None] * 4

    def on_grad(k, g, g_b):
        grads4[k] = g.reshape(N_CHIPS, 2, halves_rows[k], D)
        grads_b4[k] = g_b.reshape(N_CHIPS, 2, halves_rows[k], D)
        if k > 1:
            return None
        group = (1, 2, 3) if k == 1 else (0,)
        sts, tok = _grads_start([grads_b4[i] for i in group], f"grads_start_{k}")
        for i, st in zip(group, sts):
            started[i] = st
        return tok

    gx, small = _local_step(
        x[0], loss_target[0], w_in_t, lambda k, after: _weights_wait(w_started[k - 1], after, f"weights_wait_{k}"),
        norm_a_g, norm_b_g, sinks_a, ln1_g, ln1_b, conv_w_f, conv_b, ln2_g, ln2_b, slopes, on_grad)

    tiles = (96, 128, 352, 176)
    core_chip = jnp.stack([core, chip])
    got = _grads_wait(started[1:], gx, "grads_wait_1")
    halves = [_sum_partials(grads4[k], got[k - 1], core_chip, f"sum_partials_{k}", tiles[k]) for k in (1, 2, 3)]
    g_w_o, g_w_up_rows, g_w_down = _swap_halves(halves, "swap_halves")
    g_w_up = g_w_up_rows.T
    delta, new_m, new_v = {}, {}, {}
    for k, g, tr in (("w_o", g_w_o, 128), ("w_up", g_w_up, 256), ("w_down", g_w_down, 176)):
        delta[k], new_m[k], new_v[k] = _adamw(dict(w_o=w_o, w_up=w_up, w_down=w_down)[k], g,
                                              dict(w_o=m_w_o, w_up=m_w_up, w_down=m_w_down)[k],
                                              dict(w_o=v_w_o, w_up=v_w_up, w_down=v_w_down)[k], f"adamw_{k}", tr)

    got = _grads_wait(started[:1], delta["w_up"], "grads_wait_0")
    half_in = _sum_partials(grads4[0], got[0], core_chip, "sum_partials_0", tiles[0])
    small_rows = 32
    g_w_in_rows, totals = _share_halves([half_in], _pack({k: small[k] for k in SMALL_ORDER}, small_rows))
    tot = _unpack(totals, SMALL_ORDER, SMALL_SIZES)
    loss = tot["loss"][0]
    cols = 2 * FF // N_CHIPS
    g_conv_w = lax.dynamic_slice(tot["conv_w"].reshape(3, 2 * FF), (0, chip * cols), (3, cols))
    g_small = dict(norm_a_g=tot["norm_a_g"], norm_b_g=tot["norm_b_g"], sinks_a=tot["sinks_a"], ln1_g=tot["ln1_g"],
                   ln1_b=tot["ln1_b"], conv_w=g_conv_w, conv_b=tot["conv_b"], ln2_g=tot["ln2_g"], ln2_b=tot["ln2_b"])

    weights = dict(w_in=w_in, norm_a_g=norm_a_g, norm_b_g=norm_b_g, sinks_a=sinks_a, w_o=w_o, ln1_g=ln1_g, ln1_b=ln1_b,
                   w_up=w_up, conv_w=conv_w, conv_b=conv_b, w_down=w_down, ln2_g=ln2_g, ln2_b=ln2_b)
    ms = dict(w_in=m_w_in, norm_a_g=m_norm_a_g, norm_b_g=m_norm_b_g, sinks_a=m_sinks_a, w_o=m_w_o, ln1_g=m_ln1_g,
              ln1_b=m_ln1_b, w_up=m_w_up, conv_w=m_conv_w, conv_b=m_conv_b, w_down=m_w_down, ln2_g=m_ln2_g, ln2_b=m_ln2_b)
    vs = dict(w_in=v_w_in, norm_a_g=v_norm_a_g, norm_b_g=v_norm_b_g, sinks_a=v_sinks_a, w_o=v_w_o, ln1_g=v_ln1_g,
              ln1_b=v_ln1_b, w_up=v_w_up, conv_w=v_conv_w, conv_b=v_conv_b, w_down=v_w_down, ln2_g=v_ln2_g, ln2_b=v_ln2_b)
    order = list(weights)
    grad = dict(g_small, w_in=g_w_in_rows.T, w_o=g_w_o, w_up=g_w_up, w_down=g_w_down)

    delta["w_in"], new_m["w_in"], new_v["w_in"] = [
        a.T for a in _adamw(w_in_rows, g_w_in_rows, m_w_in_rows, v_w_in_rows, "adamw_w_in", 144)]
    small_names = [k for k in order if k not in delta]
    sizes = {k: weights[k].size for k in small_names}
    rows = 16
    packed = [_pack({k: src[k] for k in small_names}, rows) for src in (weights, grad, ms, vs)]
    for res, buf in zip((delta, new_m, new_v), _adamw(*packed, "adamw_small", rows)):
        for k, val in _unpack(buf, small_names, sizes).items():
            res[k] = val.reshape(weights[k].shape)

    return (loss, gx[None], *[grad[k] for k in order], *[delta[k] for k in order],
            *[new_m[k] for k in order], *[new_v[k] for k in order])
```

```python
import functools
import math

import jax
import jax.numpy as jnp
from jax import lax
from jax.experimental import pallas as pl
from jax.experimental.pallas import tpu as pltpu

F32, BF16, I32 = jnp.float32, jnp.bfloat16, jnp.int32

D = 1024
FF = 2816
HD = 64
NH = 8
WA, WB = 768, 1536
WIN = WA + WB
BLK = 128
ALPHA = 2.0 ** 0.25
LN_EPS, RMS_EPS = 1e-5, 1e-6
SCALE = 1.0 / math.sqrt(HD)
A_MAX_DIST, B_MAX_DIST = 127, 128
B_DILATIONS = (1, 4, 16)
SLOPES = tuple(2.0 ** (-(i + 1)) for i in range(NH))
SHARD_ROWS = (WIN // 4, D // 4, 2 * FF // 4, FF // 4)
N_CHIPS = 4
ADAM_LR, ADAM_B1, ADAM_B2, ADAM_EPS, ADAM_WD, ADAM_STEP = 0.001, 0.9, 0.999, 1e-08, 0.01, 10
MESH = pl.DeviceIdType.MESH
ANY = pl.BlockSpec(memory_space=pl.ANY)
SMEM = pl.BlockSpec(memory_space=pltpu.SMEM)
VMEM = pl.BlockSpec(memory_space=pltpu.VMEM)
HBM = pl.BlockSpec(memory_space=pltpu.HBM)
SEM = pl.BlockSpec(memory_space=pltpu.SEMAPHORE)
DATAFLOW = pltpu.SideEffectType.DATAFLOW_SIDE_EFFECTING


def _cp(sem, mb=48):
    return pltpu.CompilerParams(dimension_semantics=sem, vmem_limit_bytes=mb << 20)


def _nn(a, b):
    return lax.dot_general(a, b, (((1,), (0,)), ((), ())), preferred_element_type=F32)


def _nt(a, b):
    return lax.dot_general(a, b, (((1,), (1,)), ((), ())), preferred_element_type=F32)


def _tn(a, b):
    return lax.dot_general(a, b, (((0,), (0,)), ((), ())), preferred_element_type=F32)


def _resident(shape):
    n = len(shape)
    return pl.BlockSpec(shape, lambda *_: (0,) * n, pipeline_mode=pl.Buffered(1))


def _const(shape):
    n = len(shape)
    return pl.BlockSpec(shape, lambda *_: (0,) * n)


def _proj(x, w_t, name, tm=512):
    s = x.shape[0]
    n = w_t.shape[0]

    def body(x_ref, w_ref, o_ref, xb_ref):
        xb = x_ref[...].astype(BF16)
        xb_ref[...] = xb
        res = _nt(xb, w_ref[...])
        for g in range(n // 128):
            o_ref[g] = res[:, 128 * g:128 * (g + 1)]

    return pl.pallas_call(
        body, name=name, grid=(s // tm,),
        in_specs=[pl.BlockSpec((tm, D), lambda i: (i, 0)), _resident((n, D))],
        out_specs=[pl.BlockSpec((n // 128, tm, 128), lambda i: (0, i, 0)), pl.BlockSpec((tm, D), lambda i: (i, 0))],
        out_shape=[jax.ShapeDtypeStruct((n // 128, s, 128), F32), jax.ShapeDtypeStruct((s, D), BF16)],
        compiler_params=_cp(("parallel",)),
    )(x, w_t)


def _grad_w(lhs, rhs, name, tm, tk=1024, lhs_halves=False):
    s = rhs.shape[0]
    if lhs_halves:
        per_half = lhs.shape[2] // tm
        n = 2 * lhs.shape[2]
        lhs_spec = pl.BlockSpec((None, tk, tm), lambda i, k: (i // per_half, k, i % per_half))
    else:
        n = lhs.shape[1]
        lhs_spec = pl.BlockSpec((tk, tm), lambda i, k: (k, i))
    nk = s // tk

    def body(l_ref, r_ref, o_ref, ob_ref):
        k = pl.program_id(1)

        @pl.when(k == 0)
        def _():
            o_ref[...] = jnp.zeros_like(o_ref)

        o_ref[...] += _tn(l_ref[...].astype(BF16), r_ref[...].astype(BF16))

        @pl.when(k == nk - 1)
        def _():
            ob_ref[...] = o_ref[...].astype(BF16)

    return pl.pallas_call(
        body, name=name, grid=(n // tm, nk),
        in_specs=[lhs_spec, pl.BlockSpec((tk, D), lambda i, k: (k, 0))],
        out_specs=[pl.BlockSpec((tm, D), lambda i, k: (i, 0))] * 2,
        out_shape=[jax.ShapeDtypeStruct((n, D), F32), jax.ShapeDtypeStruct((n, D), BF16)],
        compiler_params=_cp(("parallel", "arbitrary")),
    )(lhs, rhs)


def _band_base(max_dist, dist_unit, first):
    row = lax.broadcasted_iota(I32, (BLK, 2 * BLK), 0)
    col = lax.broadcasted_iota(I32, (BLK, 2 * BLK), 1)
    dist = BLK + row - col
    ok = (dist >= 0) & (dist <= max_dist)
    if first:
        ok = ok & (col >= BLK)
    return jnp.where(ok, dist.astype(F32) * (-float(dist_unit)), -jnp.inf)


def _half_mask(shape, e):
    lane = lax.broadcasted_iota(I32, shape, 1)
    return (lane < HD) if e == 0 else (lane >= HD)


def _to_half(x, e, g):
    if g != e:
        x = pltpu.roll(x, HD, 1)
    return jnp.where(_half_mask(x.shape, g), x, 0.0)


def _stack_heads(scalars, tile):
    return jnp.concatenate([scalars[0] * tile, scalars[1] * tile], axis=0)


def _pair_fwd(q2, kb, vb, base, slopes, kv_heads, sinks):
    lo = _half_mask((BLK, 2 * HD), 0)
    if sinks is not None:
        o2 = lse2 = None
        for e in (0, 1):
            g = kv_heads[e]
            qv = (_to_half(q2, e, g) * SCALE).astype(BF16)
            s = _nt(qv, kb) + slopes[e] * base
            m = jnp.maximum(jnp.max(s, axis=1, keepdims=True), sinks[e])
            p = jnp.exp(s - m)
            l = jnp.sum(p, axis=1, keepdims=True) + jnp.exp(sinks[e] - m)
            oh = _nn(p.astype(BF16), vb) / l
            if g != e:
                oh = pltpu.roll(oh, HD, 1)
            lse = jnp.broadcast_to(m + jnp.log(l), (BLK, 2 * HD))
            o2 = oh if e == 0 else jnp.where(lo, o2, oh)
            lse2 = lse if e == 0 else jnp.where(lo, lse2, lse)
        return o2, lse2
    qs = jnp.concatenate([_to_half(q2, e, kv_heads[e]) * SCALE for e in (0, 1)], axis=0).astype(BF16)
    s = _nt(qs, kb) + (base if slopes is None else _stack_heads(slopes, base))
    m = jnp.max(s, axis=1, keepdims=True)
    p = jnp.exp(s - m)
    l = jnp.sum(p, axis=1, keepdims=True)
    o = _nn(p.astype(BF16), vb) / l
    lse = m + jnp.log(l)
    halves = []
    for e in (0, 1):
        oh = o[e * BLK:(e + 1) * BLK]
        halves.append(pltpu.roll(oh, HD, 1) if kv_heads[e] != e else oh)
    o2 = jnp.where(lo, halves[0], halves[1])
    lse2 = jnp.where(lo, jnp.broadcast_to(lse[:BLK], (BLK, 2 * HD)), jnp.broadcast_to(lse[BLK:], (BLK, 2 * HD)))
    return o2, lse2


def _pair_bwd(q2, kb, vb, do2, o2, lse2, base, slopes, kv_heads, sinks):
    lo = _half_mask((BLK, 2 * HD), 0)
    prod = do2 * o2
    lses, deltas = [], []
    for e in (0, 1):
        hq = _half_mask((BLK, 2 * HD), e)
        lses.append(jnp.max(jnp.where(hq, lse2, -jnp.inf), axis=1, keepdims=True))
        deltas.append(jnp.sum(jnp.where(hq, prod, 0.0), axis=1, keepdims=True))
    lse = jnp.concatenate(lses, axis=0)
    delta = jnp.concatenate(deltas, axis=0)
    qs = jnp.concatenate([_to_half(q2, e, kv_heads[e]) * SCALE for e in (0, 1)], axis=0).astype(BF16)
    dos = jnp.concatenate([_to_half(do2, e, kv_heads[e]) for e in (0, 1)], axis=0).astype(BF16)
    p = jnp.exp(_nt(qs, kb) + (base if slopes is None else _stack_heads(slopes, base)) - lse)
    ds = (p * (_nt(dos, vb) - delta)).astype(BF16)
    dq = _nn(ds, kb) * SCALE
    halves = []
    for e in (0, 1):
        dqh = dq[e * BLK:(e + 1) * BLK]
        halves.append(pltpu.roll(dqh, HD, 1) if kv_heads[e] != e else dqh)
    dq2 = jnp.where(lo, halves[0], halves[1])
    dk2 = _tn(ds, qs)
    dv2 = _tn(p.astype(BF16), dos)
    dsinks = []
    if sinks is not None:
        for e in (0, 1):
            dsinks.append(jnp.sum(-jnp.exp(sinks[e] - lses[e]) * deltas[e], axis=0, keepdims=True))
    return dq2, dk2, dv2, dsinks


A_BLOCKS_PER_STEP = 2
A_BLOCKS_PER_STEP_BWD = 1


def _attn_a_fwd(proj, sinks):
    s = proj.shape[1]
    nq = A_BLOCKS_PER_STEP
    rows = BLK * nq
    steps = s // rows

    def body(sink_ref, q_ref, kp_ref, kc_ref, vp_ref, vc_ref, o_ref, lse_ref):
        n = pl.program_id(0)
        base_rest = _band_base(A_MAX_DIST, 1, False)
        base_0 = jnp.where(n > 0, base_rest, _band_base(A_MAX_DIST, 1, True))
        for i in range(nq):
            cur = pl.ds(i * BLK, BLK)
            k_prev = kc_ref[pl.ds((i - 1) * BLK, BLK), :] if i > 0 else kp_ref[...]
            v_prev = vc_ref[pl.ds((i - 1) * BLK, BLK), :] if i > 0 else vp_ref[...]
            kb = jnp.concatenate([k_prev, kc_ref[cur, :]], axis=0).astype(BF16)
            vb = jnp.concatenate([v_prev, vc_ref[cur, :]], axis=0).astype(BF16)
            for j in range(NH // 2):
                g = j // 2
                o2, lse2 = _pair_fwd(q_ref[j, cur, :], kb, vb, base_rest if i > 0 else base_0,
                                     (SLOPES[2 * j], SLOPES[2 * j + 1]), (g, g), (sink_ref[2 * j], sink_ref[2 * j + 1]))
                o_ref[j, cur, :] = o2
                lse_ref[j, cur, :] = lse2

    before = lambda n: jnp.maximum(n * nq - 1, 0)
    slab = lambda g: pl.BlockSpec((None, rows, 128), lambda n: (g, n, 0))
    edge = lambda g: pl.BlockSpec((None, BLK, 128), lambda n: (g, before(n), 0))
    quad = pl.BlockSpec((4, rows, 128), lambda n: (0, n, 0))
    return pl.pallas_call(
        body, name="attn_a_fwd", grid=(steps,),
        in_specs=[SMEM, quad, edge(4), slab(4), edge(5), slab(5)],
        out_specs=[quad, quad],
        out_shape=[jax.ShapeDtypeStruct((4, s, 128), F32)] * 2,
        compiler_params=_cp(("parallel",)),
    )(sinks, proj, proj, proj, proj, proj)


def _attn_a_bwd(proj, sinks, d_o, o, lse):
    s = proj.shape[1]
    nq = A_BLOCKS_PER_STEP_BWD
    rows = BLK * nq
    steps = s // rows

    def body(sink_ref, q_ref, kp_ref, kc_ref, vp_ref, vc_ref, do_ref, o_ref, lse_ref,
             dq_ref, dk_ref, dv_ref, dsink_ref, kcar, vcar):
        n = pl.program_id(0)

        @pl.when(n == 0)
        def _():
            kcar[...] = jnp.zeros_like(kcar)
            vcar[...] = jnp.zeros_like(vcar)
            dsink_ref[...] = jnp.zeros_like(dsink_ref)

        dk_ref[...] = kcar[...]
        dv_ref[...] = vcar[...]

        @pl.when(n < steps)
        def _():
            base_rest = _band_base(A_MAX_DIST, 1, False)
            base_0 = jnp.where(n > 0, base_rest, _band_base(A_MAX_DIST, 1, True))
            for i in range(nq):
                cur = pl.ds(i * BLK, BLK)
                k_prev = kc_ref[pl.ds((i - 1) * BLK, BLK), :] if i > 0 else kp_ref[...]
                v_prev = vc_ref[pl.ds((i - 1) * BLK, BLK), :] if i > 0 else vp_ref[...]
                kb = jnp.concatenate([k_prev, kc_ref[cur, :]], axis=0).astype(BF16)
                vb = jnp.concatenate([v_prev, vc_ref[cur, :]], axis=0).astype(BF16)
                dk_win = dv_win = None
                for j in range(NH // 2):
                    g = j // 2
                    dq2, dk2, dv2, dsk = _pair_bwd(q_ref[j, cur, :], kb, vb, do_ref[j, cur, :], o_ref[j, cur, :],
                                                   lse_ref[j, cur, :], base_rest if i > 0 else base_0,
                                                   (SLOPES[2 * j], SLOPES[2 * j + 1]), (g, g),
                                                   (sink_ref[2 * j], sink_ref[2 * j + 1]))
                    dq_ref[j, cur, :] = dq2
                    dk_win = dk2 if j == 0 else dk_win + dk2
                    dv_win = dv2 if j == 0 else dv_win + dv2
                    for e in (0, 1):
                        h = 2 * j + e
                        dsink_ref[h:h + 1, :] += jnp.broadcast_to(dsk[e], (1, 128))
                if i == 0:
                    last = pl.ds((nq - 1) * BLK, BLK)
                    dk_ref[last, :] += dk_win[:BLK]
                    dv_ref[last, :] += dv_win[:BLK]
                else:
                    kcar[pl.ds((i - 1) * BLK, BLK), :] += dk_win[:BLK]
                    vcar[pl.ds((i - 1) * BLK, BLK), :] += dv_win[:BLK]
                kcar[cur, :] = dk_win[BLK:]
                vcar[cur, :] = dv_win[BLK:]

    cur_step = lambda n: jnp.minimum(n, steps - 1)
    before = lambda n: jnp.maximum(cur_step(n) * nq - 1, 0)
    out_prev = lambda n: jnp.maximum(n - 1, 0)
    quad = pl.BlockSpec((4, rows, 128), lambda n: (0, cur_step(n), 0))
    slab = lambda g: pl.BlockSpec((None, rows, 128), lambda n: (g, cur_step(n), 0))
    edge = lambda g: pl.BlockSpec((None, BLK, 128), lambda n: (g, before(n), 0))
    return pl.pallas_call(
        body, name="attn_a_bwd", grid=(steps + 1,),
        in_specs=[SMEM, quad, edge(4), slab(4), edge(5), slab(5), quad, quad, quad],
        out_specs=[quad,
                   pl.BlockSpec((rows, 128), lambda n: (out_prev(n), 0)),
                   pl.BlockSpec((rows, 128), lambda n: (out_prev(n), 0)),
                   pl.BlockSpec((NH, 128), lambda n: (0, 0))],
        out_shape=[jax.ShapeDtypeStruct((4, s, 128), F32), jax.ShapeDtypeStruct((s, 128), F32),
                   jax.ShapeDtypeStruct((s, 128), F32), jax.ShapeDtypeStruct((NH, 128), F32)],
        scratch_shapes=[pltpu.VMEM((rows, 128), F32), pltpu.VMEM((rows, 128), F32)],
        compiler_params=_cp(("arbitrary",)),
    )(sinks, proj, proj, proj, proj, proj, d_o, o, lse)


def _stream(rho, i, r):
    start = i * BLK * r + rho
    return pl.ds(start, BLK, stride=r) if r > 1 else pl.ds(start, BLK)


def _for_streams(r, fn, side_by_side=4):
    if r <= side_by_side:
        for rho in range(r):
            fn(rho)
    else:
        def group(it, carry):
            for u in range(side_by_side):
                fn(side_by_side * it + u)
            return carry

        lax.fori_loop(0, r // side_by_side, group, 0)


B_BLOCKS_PER_STEP = {1: 8, 4: 2, 16: 1}
B_BLOCKS_PER_STEP_FWD = {1: 8, 4: 2, 16: 1}


def _attn_b_fwd(proj, slopes, r):
    s = proj.shape[1]
    nq = B_BLOCKS_PER_STEP_FWD[r]
    rows = BLK * r * nq
    steps = s // rows
    qc, kc, vc = WA // 128, WA // 128 + 4, WA // 128 + 8

    def body(slope_ref, q_ref, kp_ref, kc_ref, vp_ref, vc_ref, o_ref, lse_ref):
        j = pl.program_id(0)
        sb = pl.program_id(1)
        sl2 = (slope_ref[2 * j], slope_ref[2 * j + 1])
        bias_rest = _stack_heads(sl2, _band_base(B_MAX_DIST, r, False))
        bias_0 = jnp.where(sb > 0, bias_rest, _stack_heads(sl2, _band_base(B_MAX_DIST, r, True)))

        def stream(rho):
            for i in range(nq):
                cur = _stream(rho, i, r)
                k_prev = kc_ref[_stream(rho, i - 1, r), :] if i > 0 else kp_ref[_stream(rho, 0, r), :]
                v_prev = vc_ref[_stream(rho, i - 1, r), :] if i > 0 else vp_ref[_stream(rho, 0, r), :]
                kb = jnp.concatenate([k_prev, kc_ref[cur, :]], axis=0).astype(BF16)
                vb = jnp.concatenate([v_prev, vc_ref[cur, :]], axis=0).astype(BF16)
                o2, lse2 = _pair_fwd(q_ref[cur, :], kb, vb, bias_rest if i > 0 else bias_0, None, (0, 1), None)
                o_ref[cur, :] = o2
                lse_ref[cur, :] = lse2

        _for_streams(r, stream, side_by_side=8)

    before = lambda sb: jnp.maximum(sb * nq - 1, 0)
    return pl.pallas_call(
        body, name=f"attn_b_fwd_r{r}", grid=(NH // 2, steps),
        in_specs=[SMEM,
                  pl.BlockSpec((None, rows, 128), lambda j, sb: (qc + j, sb, 0)),
                  pl.BlockSpec((None, BLK * r, 128), lambda j, sb: (kc + j, before(sb), 0)),
                  pl.BlockSpec((None, rows, 128), lambda j, sb: (kc + j, sb, 0)),
                  pl.BlockSpec((None, BLK * r, 128), lambda j, sb: (vc + j, before(sb), 0)),
                  pl.BlockSpec((None, rows, 128), lambda j, sb: (vc + j, sb, 0))],
        out_specs=[pl.BlockSpec((None, rows, 128), lambda j, sb: (j, sb, 0))] * 2,
        out_shape=[jax.ShapeDtypeStruct((4, s, 128), F32)] * 2,
        compiler_params=_cp(("parallel", "parallel")),
    )(slopes, proj, proj, proj, proj, proj)


def _attn_b_bwd(proj, slopes, d_o, o, lse, r, so_far=None):
    s = proj.shape[1]
    nq = B_BLOCKS_PER_STEP[r]
    rows = BLK * r * nq
    steps = s // rows
    qc, kc, vc = WA // 128, WA // 128 + 4, WA // 128 + 8
    chained = so_far is not None

    def body(slope_ref, q_ref, kp_ref, kc_ref, vp_ref, vc_ref, do_ref, o_ref, lse_ref, *rest):
        if chained:
            pq_ref, pk_ref, pv_ref, dq_ref, dk_ref, dv_ref, kcar, vcar = rest
        else:
            dq_ref, dk_ref, dv_ref, kcar, vcar = rest
        j = pl.program_id(0)
        sb = pl.program_id(1)

        @pl.when(sb == 0)
        def _():
            kcar[...] = jnp.zeros_like(kcar)
            vcar[...] = jnp.zeros_like(vcar)

        if chained:
            dk_ref[...] = kcar[...] + pk_ref[...]
            dv_ref[...] = vcar[...] + pv_ref[...]
        else:
            dk_ref[...] = kcar[...]
            dv_ref[...] = vcar[...]

        @pl.when(sb < steps)
        def _():
            sl2 = (slope_ref[2 * j], slope_ref[2 * j + 1])
            bias_rest = _stack_heads(sl2, _band_base(B_MAX_DIST, r, False))
            bias_0 = jnp.where(sb > 0, bias_rest, _stack_heads(sl2, _band_base(B_MAX_DIST, r, True)))

            def stream(rho):
                for i in range(nq):
                    cur = _stream(rho, i, r)
                    k_prev = kc_ref[_stream(rho, i - 1, r), :] if i > 0 else kp_ref[_stream(rho, 0, r), :]
                    v_prev = vc_ref[_stream(rho, i - 1, r), :] if i > 0 else vp_ref[_stream(rho, 0, r), :]
                    kb = jnp.concatenate([k_prev, kc_ref[cur, :]], axis=0).astype(BF16)
                    vb = jnp.concatenate([v_prev, vc_ref[cur, :]], axis=0).astype(BF16)
                    dq2, dk2, dv2, _ = _pair_bwd(q_ref[cur, :], kb, vb, do_ref[cur, :], o_ref[cur, :], lse_ref[cur, :],
                                                 bias_rest if i > 0 else bias_0, None, (0, 1), None)
                    dq_ref[cur, :] = dq2 + pq_ref[cur, :] if chained else dq2
                    if i == 0:
                        last = _stream(rho, nq - 1, r)
                        dk_ref[last, :] += dk2[:BLK]
                        dv_ref[last, :] += dv2[:BLK]
                    else:
                        kcar[_stream(rho, i - 1, r), :] += dk2[:BLK]
                        vcar[_stream(rho, i - 1, r), :] += dv2[:BLK]
                    kcar[cur, :] = dk2[BLK:]
                    vcar[cur, :] = dv2[BLK:]

            _for_streams(r, stream, side_by_side=8)

    cur_step = lambda sb: jnp.minimum(sb, steps - 1)
    before = lambda sb: jnp.maximum(cur_step(sb) * nq - 1, 0)
    out_prev = lambda sb: jnp.maximum(sb - 1, 0)
    tile = lambda slab: pl.BlockSpec((None, rows, 128), lambda j, sb: (slab + j, cur_step(sb), 0))
    edge = lambda slab: pl.BlockSpec((None, BLK * r, 128), lambda j, sb: (slab + j, before(sb), 0))
    late = pl.BlockSpec((None, rows, 128), lambda j, sb: (j, out_prev(sb), 0))
    grads = [tile(0), late, late]
    return pl.pallas_call(
        body, name=f"attn_b_bwd_r{r}", grid=(NH // 2, steps + 1),
        in_specs=[SMEM, tile(qc), edge(kc), tile(kc), edge(vc), tile(vc), tile(0), tile(0), tile(0)]
        + (grads if chained else []),
        out_specs=grads,
        out_shape=[jax.ShapeDtypeStruct((4, s, 128), F32)] * 3,
        scratch_shapes=[pltpu.VMEM((rows, 128), F32), pltpu.VMEM((rows, 128), F32)],
        compiler_params=_cp(("parallel", "arbitrary")),
    )(slopes, proj, proj, proj, proj, proj, d_o, o, lse, *(so_far if chained else ()))


def _row(v):
    return v.reshape(1, -1)


def _layer_norm_stats(z):
    mu = jnp.mean(z, axis=-1, keepdims=True)
    zc = z - mu
    var = jnp.mean(zc * zc, axis=-1, keepdims=True)
    rstd = lax.rsqrt(var + LN_EPS)
    return zc * rstd, rstd


def _layer_norm_bwd(dh, zh, rstd, g):
    dzh = dh * g
    return rstd * (dzh - jnp.mean(dzh, axis=-1, keepdims=True) - zh * jnp.mean(dzh * zh, axis=-1, keepdims=True))


def _rms(o):
    return lax.rsqrt(jnp.mean(o * o, axis=-1, keepdims=True) + RMS_EPS)


def _mix_ln1(x, o_a, o_b, lse_b, norm_a_g, norm_b_g, w_o, ln1_g, ln1_b, tm=256):
    s = x.shape[0]

    def wide(ref):
        return jnp.concatenate([ref[j] for j in range(4)], axis=1)

    def body(x_ref, oa_ref, ob1, ob2, ob3, l1, l2, l3, ga_ref, gb_ref, wo_ref, g_ref, b_ref,
             obm_ref, lse_ref, cat_ref, z1_ref, h1_ref, h1b_ref):
        la, lb, lc = wide(l1), wide(l2), wide(l3)
        m = jnp.maximum(jnp.maximum(la, lb), lc)
        ea, eb, ec = jnp.exp(la - m), jnp.exp(lb - m), jnp.exp(lc - m)
        den = ea + eb + ec
        obm = (ea / den) * wide(ob1) + (eb / den) * wide(ob2) + (ec / den) * wide(ob3)
        lse = m + jnp.log(den)
        for j in range(4):
            obm_ref[j] = obm[:, 128 * j:128 * (j + 1)]
            lse_ref[j] = lse[:, 128 * j:128 * (j + 1)]
        oa = wide(oa_ref)
        na = oa * _rms(oa) * ga_ref[...]
        nb_ = obm * _rms(obm) * gb_ref[...]
        cat = jnp.concatenate([na, nb_], axis=1).astype(BF16)
        cat_ref[...] = cat
        z1 = ALPHA * x_ref[...] + _nn(cat, wo_ref[...])
        z1_ref[...] = z1
        zh, _ = _layer_norm_stats(z1)
        h1 = zh * g_ref[...] + b_ref[...]
        h1_ref[...] = h1
        h1b_ref[...] = h1.astype(BF16)

    t512 = pl.BlockSpec((4, tm, 128), lambda i: (0, i, 0))
    td = pl.BlockSpec((tm, D), lambda i: (i, 0))
    return pl.pallas_call(
        body, name="mix_ln1", grid=(s // tm,),
        in_specs=[td] + [t512] * 7 + [_const((1, 512))] * 2 + [_resident((D, D))] + [_const((1, D))] * 2,
        out_specs=[t512, t512, td, td, td, td],
        out_shape=[jax.ShapeDtypeStruct((4, s, 128), F32), jax.ShapeDtypeStruct((4, s, 128), F32),
                   jax.ShapeDtypeStruct((s, D), BF16), jax.ShapeDtypeStruct((s, D), F32),
                   jax.ShapeDtypeStruct((s, D), F32), jax.ShapeDtypeStruct((s, D), BF16)],
        compiler_params=_cp(("parallel",)),
    )(x, o_a, *o_b, *lse_b, _row(norm_a_g), _row(norm_b_g), w_o, _row(ln1_g), _row(ln1_b))


def _gelu_and_grad(x):
    c = math.sqrt(2.0 / math.pi)
    x2 = x * x
    cx = c * x
    t = jnp.tanh(cx * (1.0 + 0.044715 * x2))
    q = 1.0 + t
    g = (0.5 * x) * q
    dg = 0.5 * q + ((0.5 * cx) * (1.0 - t * t)) * (1.0 + (3.0 * 0.044715) * x2)
    return g, dg


def _shift_down(u, before):
    n = u.shape[0]
    ext = jnp.concatenate([before, u], axis=0)
    return pltpu.roll(ext, 1, 0)[8:], pltpu.roll(ext, 2, 0)[8:]


def _shift_up(u, after):
    n = u.shape[0]
    ext = jnp.concatenate([u, after], axis=0)
    return pltpu.roll(ext, n + 7, 0)[:n], pltpu.roll(ext, n + 6, 0)[:n]


def _up_conv_gelu(h1b, w_up, cwb, tm=256, tn=FF // 2, chunk_rows=16, piece_cols=512):
    s = h1b.shape[0]
    n_i = s // tm
    n_t = (FF // tn) * n_i

    def body(h_ref, wg_ref, wv_ref, c_ref, up_ref, a_ref, g_ref, a1_ref, pend_a, pend_b, carry):
        t = pl.program_id(0)
        row_tile = jnp.maximum(t - 1, 0) % n_i
        w_refs = (wg_ref, wv_ref)

        @pl.when(t == 0)
        def _():
            pend_b[...] = jnp.zeros_like(pend_b)
            carry[...] = jnp.zeros_like(carry)

        def step(dst, src):
            def chunk(c, before):
                rows = pl.ds(c * chunk_rows, chunk_rows)
                u, last = [], []
                for half in (0, 1):
                    up = src[half, rows, :]
                    r1, r2 = _shift_down(up, before[half])
                    u.append(r2 * c_ref[0, half:half + 1, :] + r1 * c_ref[1, half:half + 1, :]
                             + up * c_ref[2, half:half + 1, :] + c_ref[3, half:half + 1, :])
                    last.append(up[chunk_rows - 8:])
                g, dg = _gelu_and_grad(u[0])
                a_ref[rows, :] = (g * u[1]).astype(BF16)
                g_ref[rows, :] = g.astype(BF16)
                a1_ref[rows, :] = (u[1] * dg).astype(BF16)
                return tuple(last)

            edge = tuple(jnp.where(row_tile > 0, carry[half], 0.0) for half in (0, 1))
            pieces = [(half, c0, min(piece_cols, tn - c0)) for half in (0, 1) for c0 in range(0, tn, piece_cols)]
            n_c = tm // chunk_rows
            done = 0
            for p, (half, c0, width) in enumerate(pieces):
                cols = slice(c0, c0 + width)
                up = _nn(h_ref[...], w_refs[half][:, cols])
                up_ref[half, :, cols] = up.astype(BF16)
                dst[half, :, cols] = up
                upto = n_c * (p + 1) // len(pieces)
                for c in range(done, upto):
                    edge = chunk(c, edge)
                done = upto
            for half in (0, 1):
                carry[half] = edge[half]

        @pl.when(t % 2 == 0)
        def _():
            step(pend_a, pend_b)

        @pl.when(t % 2 == 1)
        def _():
            step(pend_b, pend_a)

    mm = lambda t: jnp.minimum(t, n_t - 1)
    ew = lambda t: jnp.maximum(t - 1, 0)
    out_tile = pl.BlockSpec((tm, tn), lambda t: (ew(t) % n_i, ew(t) // n_i))
    return pl.pallas_call(
        body, name="up_conv_gelu", grid=(n_t + 1,),
        in_specs=[pl.BlockSpec((tm, D), lambda t: (mm(t) % n_i, 0)),
                  pl.BlockSpec((D, tn), lambda t: (0, mm(t) // n_i)),
                  pl.BlockSpec((D, tn), lambda t: (0, FF // tn + mm(t) // n_i)),
                  pl.BlockSpec((4, 2, tn), lambda t: (0, 0, ew(t) // n_i))],
        out_specs=[pl.BlockSpec((2, tm, tn), lambda t: (0, mm(t) % n_i, mm(t) // n_i)), out_tile, out_tile, out_tile],
        out_shape=[jax.ShapeDtypeStruct((2, s, FF), BF16)] + [jax.ShapeDtypeStruct((s, FF), BF16)] * 3,
        scratch_shapes=[pltpu.VMEM((2, tm, tn), F32), pltpu.VMEM((2, tm, tn), F32), pltpu.VMEM((2, 8, tn), F32)],
        compiler_params=_cp(("arbitrary",)),
    )(h1b, w_up, w_up, cwb)


def _down_ln2_loss(a, w_down, h1, target, ln2_g, ln2_b, tm=512):
    s = a.shape[0]

    def body(a_ref, w_ref, h_ref, t_ref, g_ref, b_ref, dz_ref, dzb_ref, st_ref):
        @pl.when(pl.program_id(0) == 0)
        def _():
            st_ref[...] = jnp.zeros_like(st_ref)

        z2 = ALPHA * h_ref[...] + _nn(a_ref[...], w_ref[...])
        zh, rstd = _layer_norm_stats(z2)
        diff = zh * g_ref[...] + b_ref[...] - t_ref[...]
        part = 0.5 * jnp.sum(jnp.mean(diff * diff, axis=-1, keepdims=True), axis=0, keepdims=True)
        dy = diff * (1.0 / D)
        st_ref[0:1, :] += jnp.sum(dy * zh, axis=0, keepdims=True)
        st_ref[1:2, :] += jnp.sum(dy, axis=0, keepdims=True)
        st_ref[2:3, :] += jnp.broadcast_to(part, (1, D))
        dz = _layer_norm_bwd(dy, zh, rstd, g_ref[...])
        dz_ref[...] = dz
        dzb_ref[...] = dz.astype(BF16)

    td = pl.BlockSpec((tm, D), lambda i: (i, 0))
    return pl.pallas_call(
        body, name="down_ln2_loss", grid=(s // tm,),
        in_specs=[pl.BlockSpec((tm, FF), lambda i: (i, 0)), _resident((FF, D)), td, td, _const((1, D)), _const((1, D))],
        out_specs=[td, td, _const((8, D))],
        out_shape=[jax.ShapeDtypeStruct((s, D), F32), jax.ShapeDtypeStruct((s, D), BF16),
                   jax.ShapeDtypeStruct((8, D), F32)],
        compiler_params=_cp(("arbitrary",)),
    )(a, w_down, h1, target, _row(ln2_g), _row(ln2_b))


def _d_act(dz2b, w_down, tm=512):
    s = dz2b.shape[0]

    def body(dz_ref, w_ref, o_ref):
        o_ref[...] = _nt(dz_ref[...], w_ref[...])

    return pl.pallas_call(
        body, name="d_act", grid=(s // tm,),
        in_specs=[pl.BlockSpec((tm, D), lambda i: (i, 0)), _resident((FF, D))],
        out_specs=pl.BlockSpec((tm, FF), lambda i: (i, 0)),
        out_shape=jax.ShapeDtypeStruct((s, FF), F32),
        compiler_params=_cp(("parallel",)),
    )(dz2b, w_down)


def _conv_gelu_bwd(da, up, g, a1, cwb, tm=256, tn=FF // 2, chunk_rows=16):
    s = da.shape[0]
    n_i = s // tm
    n_c = tm // chunk_rows

    def body(da_ref, up_ref, g_ref, a1_ref, c_ref, dup_ref, dc_ref, carry):
        @pl.when(pl.program_id(1) == 0)
        def _():
            carry[...] = jnp.zeros_like(carry)
            dc_ref[...] = jnp.zeros_like(dc_ref)

        def fold(v):
            return jnp.sum(v.reshape(chunk_rows // 8, 8, v.shape[1]), axis=0)

        def chunk(cc, state):
            after, sums = state
            rows = pl.ds((n_c - 1 - cc) * chunk_rows, chunk_rows)
            da_c = da_ref[rows, :]
            dus = (da_c * a1_ref[rows, :].astype(F32), da_c * g_ref[rows, :].astype(F32))
            head, new_sums = [], []
            for half in (0, 1):
                du = dus[half]
                up = up_ref[half, rows, :].astype(F32)
                l1, l2 = _shift_up(du, after[half])
                dup = (du * c_ref[2, half:half + 1, :] + l1 * c_ref[1, half:half + 1, :]
                       + l2 * c_ref[0, half:half + 1, :])
                dup_ref[half, rows, :] = dup.astype(BF16)
                parts = (fold(l2 * up), fold(l1 * up), fold(du * up), fold(du))
                new_sums.append(parts if sums is None else tuple(a + b for a, b in zip(sums[half], parts)))
                head.append(du[:8])
            return tuple(head), new_sums

        state = ((carry[0], carry[1]), None)
        for cc in range(n_c):
            state = chunk(cc, state)
        head, sums = state
        for half in (0, 1):
            carry[half] = head[half]
            for k in range(4):
                dc_ref[k, half:half + 1, :] += jnp.sum(sums[half][k], axis=0, keepdims=True)

    rev = lambda ii: n_i - 1 - ii
    tile = pl.BlockSpec((tm, tn), lambda j, ii: (rev(ii), j))
    pair = pl.BlockSpec((2, tm, tn), lambda j, ii: (0, rev(ii), j))
    per_col = pl.BlockSpec((4, 2, tn), lambda j, ii: (0, 0, j))
    return pl.pallas_call(
        body, name="conv_gelu_bwd", grid=(FF // tn, n_i),
        in_specs=[tile, pair, tile, tile, per_col],
        out_specs=[pair, per_col],
        out_shape=[jax.ShapeDtypeStruct((2, s, FF), BF16), jax.ShapeDtypeStruct((4, 2, FF), F32)],
        scratch_shapes=[pltpu.VMEM((2, 8, tn), F32)],
        compiler_params=_cp(("parallel", "arbitrary")),
    )(da, up, g, a1, cwb)


def _dh1_ln1_bwd(dz2, dup, w_up, z1, ln1_g, tm=512):
    s = dz2.shape[0]

    def body(dz2_ref, dup_ref, w_ref, z1_ref, g_ref, dz1_ref, dz1b_ref, st_ref):
        @pl.when(pl.program_id(0) == 0)
        def _():
            st_ref[...] = jnp.zeros_like(st_ref)

        dh = ALPHA * dz2_ref[...] + _nt(dup_ref[0], w_ref[:, :FF]) + _nt(dup_ref[1], w_ref[:, FF:])
        zh, rstd = _layer_norm_stats(z1_ref[...])
        st_ref[0:1, :] += jnp.sum(dh * zh, axis=0, keepdims=True)
        st_ref[1:2, :] += jnp.sum(dh, axis=0, keepdims=True)
        dz = _layer_norm_bwd(dh, zh, rstd, g_ref[...])
        dz1_ref[...] = dz
        dz1b_ref[...] = dz.astype(BF16)

    td = pl.BlockSpec((tm, D), lambda i: (i, 0))
    return pl.pallas_call(
        body, name="dh1_ln1_bwd", grid=(s // tm,),
        in_specs=[td, pl.BlockSpec((2, tm, FF), lambda i: (0, i, 0)), _resident((D, 2 * FF)), td, _const((1, D))],
        out_specs=[td, td, _const((8, D))],
        out_shape=[jax.ShapeDtypeStruct((s, D), F32), jax.ShapeDtypeStruct((s, D), BF16),
                   jax.ShapeDtypeStruct((8, D), F32)],
        compiler_params=_cp(("arbitrary",), 58),
    )(dz2, dup, w_up, z1, _row(ln1_g))


def _dcat_rms_bwd(dz1b, w_o, o_a, o_b, norm_a_g, norm_b_g, tm=512):
    s = dz1b.shape[0]

    def body(dz_ref, w_ref, oa_ref, ob_ref, ga_ref, gb_ref, da_ref, db_ref, st_ref):
        @pl.when(pl.program_id(0) == 0)
        def _():
            st_ref[...] = jnp.zeros_like(st_ref)

        dcat = _nt(dz_ref[...], w_ref[...])
        for k, (o_ref, g_ref, d_ref) in enumerate(((oa_ref, ga_ref, da_ref), (ob_ref, gb_ref, db_ref))):
            o = jnp.concatenate([o_ref[j] for j in range(4)], axis=1)
            dn = dcat[:, 512 * k:512 * (k + 1)]
            rr = _rms(o)
            oh = o * rr
            st_ref[k:k + 1, :] += jnp.sum(dn * oh, axis=0, keepdims=True)
            doh = dn * g_ref[...]
            d_o = rr * (doh - oh * jnp.mean(doh * oh, axis=-1, keepdims=True))
            for j in range(4):
                d_ref[j] = d_o[:, 128 * j:128 * (j + 1)]

    t512 = pl.BlockSpec((4, tm, 128), lambda i: (0, i, 0))
    return pl.pallas_call(
        body, name="dcat_rms_bwd", grid=(s // tm,),
        in_specs=[pl.BlockSpec((tm, D), lambda i: (i, 0)), _resident((D, D)), t512, t512,
                  _const((1, 512)), _const((1, 512))],
        out_specs=[t512, t512, _const((8, 512))],
        out_shape=[jax.ShapeDtypeStruct((4, s, 128), F32), jax.ShapeDtypeStruct((4, s, 128), F32),
                   jax.ShapeDtypeStruct((8, 512), F32)],
        compiler_params=_cp(("arbitrary",)),
    )(dz1b, w_o, o_a, o_b, _row(norm_a_g), _row(norm_b_g))


def _dproj_combine(dqa, dka, dva, dqkv_b, tm=256):
    s = dka.shape[0]

    def body(qa, ka, va, qb, kb, vb, o_ref):
        for j in range(4):
            o_ref[:, 128 * j:128 * (j + 1)] = qa[j].astype(BF16)
            o_ref[:, 768 + 128 * j:768 + 128 * (j + 1)] = qb[j].astype(BF16)
            o_ref[:, 1280 + 128 * j:1280 + 128 * (j + 1)] = kb[j].astype(BF16)
            o_ref[:, 1792 + 128 * j:1792 + 128 * (j + 1)] = vb[j].astype(BF16)
        o_ref[:, 512:640] = ka[...].astype(BF16)
        o_ref[:, 640:768] = va[...].astype(BF16)

    t512 = pl.BlockSpec((4, tm, 128), lambda i: (0, i, 0))
    t128 = pl.BlockSpec((tm, 128), lambda i: (i, 0))
    return pl.pallas_call(
        body, name="dproj_combine", grid=(s // tm,),
        in_specs=[t512, t128, t128] + [t512] * 3,
        out_specs=pl.BlockSpec((tm, WIN), lambda i: (i, 0)),
        out_shape=jax.ShapeDtypeStruct((s, WIN), BF16),
        compiler_params=_cp(("parallel",)),
    )(dqa, dka, dva, *dqkv_b)


def _grad_x(dz1, dproj, w_in_t, zero, tm=512):
    s = dz1.shape[0]

    def body(dz_ref, dp_ref, w_ref, z_ref, o_ref):
        o_ref[...] = ALPHA * dz_ref[...] + _nn(dp_ref[...], w_ref[...]) + z_ref[0:1, 0:1]

    td = pl.BlockSpec((tm, D), lambda i: (i, 0))
    return pl.pallas_call(
        body, name="grad_x", grid=(s // tm,),
        in_specs=[td, pl.BlockSpec((tm, WIN), lambda i: (i, 0)), _resident((WIN, D)), _const((8, 128))],
        out_specs=td, out_shape=jax.ShapeDtypeStruct((s, D), F32),
        compiler_params=_cp(("parallel",)),
    )(dz1, dproj, w_in_t, zero)


def _place():
    return lax.axis_index("x"), lax.axis_index("y"), lax.axis_index("c")


def _other_chips(x, y):
    return [(1 - x, y), (x, 1 - y), (1 - x, 1 - y)]


def _hbm(a):
    return pltpu.with_memory_space_constraint(a, pltpu.HBM)


def _gather_w_in(shard, conv_w):
    rows_k = shard.shape[0]
    half = rows_k // 2

    def body(src, conv_src, out, conv_out, send_sems, recv_sems):
        x, y, c = _place()
        b = 2 * x + y
        sibling = (x, y, 1 - c)
        chips = _other_chips(x, y)

        def copy(idx, chip_b, core, to, first_hop=False):
            rows = out.at[pl.ds(pl.multiple_of(chip_b * rows_k + core * half, 16), half)]
            s_ref = src.at[pl.ds(pl.multiple_of(core * half, 16), half)] if first_hop else rows
            return pltpu.make_async_remote_copy(src_ref=s_ref, dst_ref=rows, send_sem=send_sems.at[idx],
                                                recv_sem=recv_sems.at[idx], device_id=to, device_id_type=MESH)

        def own_copy():
            return pltpu.make_async_remote_copy(
                src_ref=src, dst_ref=out.at[pl.ds(pl.multiple_of(b * rows_k, 16), rows_k)], send_sem=send_sems.at[6],
                recv_sem=recv_sems.at[6], device_id=sibling, device_id_type=MESH)

        def conv_copy(idx, chip_b, to):
            return pltpu.make_async_remote_copy(src_ref=conv_src, dst_ref=conv_out.at[chip_b],
                                                send_sem=send_sems.at[7 + idx], recv_sem=recv_sems.at[7 + idx],
                                                device_id=to, device_id_type=MESH)

        started = [own_copy(), conv_copy(3, b, sibling)]
        for jn, chip in enumerate(chips):
            started += [copy(jn, b, c, (chip[0], chip[1], c), first_hop=True), conv_copy(jn, b, (chip[0], chip[1], c))]
        for cp in started:
            cp.start()
        for jn, chip in enumerate(chips):
            cb = 2 * chip[0] + chip[1]
            copy(jn, cb, c, (chip[0], chip[1], c)).wait_recv()
            cp = copy(3 + jn, cb, c, sibling)
            cp.start()
            started.append(cp)
        for jn, chip in enumerate(chips):
            cb = 2 * chip[0] + chip[1]
            copy(3 + jn, cb, 1 - c, sibling).wait_recv()
            conv_copy(jn, cb, (chip[0], chip[1], c)).wait_recv()
        own_copy().wait_recv()
        conv_copy(3, b, sibling).wait_recv()
        for cp in started:
            cp.wait_send()

    return pl.pallas_call(
        body, name="gather_w_in",
        in_specs=[ANY, ANY], out_specs=[ANY, ANY],
        out_shape=[jax.ShapeDtypeStruct((N_CHIPS * rows_k, D), BF16), jax.ShapeDtypeStruct((N_CHIPS,) + conv_w.shape, F32)],
        scratch_shapes=[pltpu.SemaphoreType.DMA((11,)), pltpu.SemaphoreType.DMA((11,))],
        compiler_params=pltpu.CompilerParams(has_side_effects=True),
    )(shard, conv_w)


def _weight_copies(shard, land, send_sems, recv_sems, arrivals):
    x, y, c = _place()
    n_rows, n_cols = shard.shape
    peers = [(px, py, c) for px, py in _other_chips(x, y)] + [(x, y, 1 - c)]
    cps = []
    for jn, peer in enumerate(peers):
        at = 2 * peer[0] + peer[1] if arrivals else 2 * x + y
        if land.shape[1] == n_cols:
            dst = land.at[pl.ds(pl.multiple_of(at * n_rows, 16), n_rows)]
        else:
            dst = land.at[:, pl.ds(pl.multiple_of(at * n_cols, 128), n_cols)]
        cps.append(pltpu.make_async_remote_copy(src_ref=shard, dst_ref=dst, send_sem=send_sems.at[jn],
                                                recv_sem=recv_sems.at[jn], device_id=peer, device_id_type=MESH))
    return cps


def _weights_start(shards, after):
    n = len(shards)
    lands = [lax.empty((N_CHIPS * sh.shape[0], D) if sh.shape[1] == D else (D, N_CHIPS * sh.shape[1]), BF16)
             for sh in shards]

    def body(*refs):
        src, land = refs[:n], refs[n:2 * n]
        send_sems, recv_sems = refs[2 * n + 1:3 * n + 1], refs[3 * n + 1:4 * n + 1]
        for k in range(n):
            for send in _weight_copies(src[k], land[k], send_sems[k], recv_sems[k], False):
                send.start()
        refs[-1][...] = jnp.zeros_like(refs[-1])

    res = pl.pallas_call(
        body, name="weights_start",
        in_specs=[HBM] * (2 * n) + [ANY], out_specs=[SEM] * (2 * n) + [HBM] * (2 * n) + [VMEM],
        out_shape=[pltpu.SemaphoreType.DMA((4,))] * (2 * n)
        + [pltpu.HBM(a.shape, a.dtype) for a in (*shards, *lands)] + [jax.ShapeDtypeStruct((8, 128), F32)],
        input_output_aliases={i: i + 2 * n for i in range(2 * n)},
        compiler_params=pltpu.CompilerParams(has_side_effects=DATAFLOW),
    )(*[_hbm(a) for a in (*shards, *lands)], after)
    return [(res[k], res[n + k], res[2 * n + k], res[3 * n + k]) for k in range(n)], res[-1]


def _weights_wait(started, after, name):
    send_sems, recv_sems, shard, land = started

    def body(s_ref, l_ref, send_ref, recv_ref, after_ref, s_out, l_out):
        for cp in _weight_copies(s_ref, l_ref, send_ref, recv_ref, True):
            cp.wait_send()
            cp.wait_recv()

    return pl.pallas_call(
        body, name=name,
        in_specs=[HBM, HBM, SEM, SEM, ANY], out_specs=[HBM, HBM],
        out_shape=[pltpu.HBM(shard.shape, shard.dtype), pltpu.HBM(land.shape, land.dtype)],
        input_output_aliases={0: 0, 1: 1},
        compiler_params=pltpu.CompilerParams(has_side_effects=DATAFLOW),
    )(shard, land, send_sems, recv_sems, after)[1]


def _grad_copies(g_ref, land_ref, send_sems, recv_sems):
    x, y, c = _place()
    cps = []
    for d in range(1, 8):
        px, py, pc = x ^ (d >> 2), y ^ ((d >> 1) & 1), c ^ (d & 1)
        cps.append(pltpu.make_async_remote_copy(
            src_ref=g_ref.at[2 * px + py, pc], dst_ref=land_ref.at[d - 1], send_sem=send_sems.at[d - 1],
            recv_sem=recv_sems.at[d - 1], device_id=(px, py, pc), device_id_type=MESH))
    return cps


def _grads_start(grads_b, name):
    n = len(grads_b)
    lands = [lax.empty((7, g.shape[2], D), BF16) for g in grads_b]

    def body(*refs):
        g, land = refs[:n], refs[n:2 * n]
        send_sems, recv_sems = refs[2 * n:3 * n], refs[3 * n:4 * n]
        for k in range(n):
            for cp in _grad_copies(g[k], land[k], send_sems[k], recv_sems[k]):
                cp.start()
        refs[-1][...] = jnp.zeros_like(refs[-1])

    res = pl.pallas_call(
        body, name=name,
        in_specs=[HBM] * (2 * n), out_specs=[SEM] * (2 * n) + [HBM] * (2 * n) + [VMEM],
        out_shape=[pltpu.SemaphoreType.DMA((7,))] * (2 * n)
        + [pltpu.HBM(a.shape, a.dtype) for a in (*grads_b, *lands)] + [jax.ShapeDtypeStruct((8, 128), F32)],
        input_output_aliases={i: i + 2 * n for i in range(2 * n)},
        compiler_params=pltpu.CompilerParams(has_side_effects=DATAFLOW),
    )(*[_hbm(a) for a in (*grads_b, *lands)])
    return [(res[k], res[n + k], res[2 * n + k], res[3 * n + k]) for k in range(n)], res[-1]


def _grads_wait(started, after, name):
    n = len(started)

    def body(*refs):
        g, land = refs[:n], refs[n:2 * n]
        send_sems, recv_sems = refs[2 * n:3 * n], refs[3 * n:4 * n]
        for k in range(n):
            for cp in _grad_copies(g[k], land[k], send_sems[k], recv_sems[k]):
                cp.wait_send()
                cp.wait_recv()

    gs = [st[2] for st in started]
    lands = [st[3] for st in started]
    res = pl.pallas_call(
        body, name=name,
        in_specs=[HBM] * (2 * n) + [SEM] * (2 * n) + [ANY], out_specs=[HBM] * (2 * n),
        out_shape=[pltpu.HBM(a.shape, a.dtype) for a in (*gs, *lands)],
        input_output_aliases={i: i for i in range(2 * n)},
        compiler_params=pltpu.CompilerParams(has_side_effects=DATAFLOW),
    )(*gs, *lands, *[st[0] for st in started], *[st[1] for st in started], after)
    return res[n:]


def _sum_partials(grad4, got, cb, name, tr):
    h = grad4.shape[2]
    per_half = h // tr

    def body(cb_ref, g_ref, o_ref, out_ref):
        acc = g_ref[...]
        for j in range(7):
            acc = acc + o_ref[j].astype(F32)
        out_ref[...] = acc

    return pl.pallas_call(
        body, name=name,
        grid_spec=pltpu.PrefetchScalarGridSpec(
            num_scalar_prefetch=1, grid=(per_half,),
            in_specs=[pl.BlockSpec((None, None, tr, D), lambda i, cb_ref: (cb_ref[1], cb_ref[0], i, 0)),
                      pl.BlockSpec((7, tr, D), lambda i, cb_ref: (0, i, 0))],
            out_specs=pl.BlockSpec((tr, D), lambda i, cb_ref: (cb_ref[0] * per_half + i, 0))),
        out_shape=jax.ShapeDtypeStruct((2 * h, D), F32),
        compiler_params=_cp(("arbitrary",)),
    )(cb, grad4, got)


def _swap_halves(shards, name):
    n = len(shards)

    def body(*refs):
        out, send_sems, recv_sems = refs[n:2 * n], refs[2 * n], refs[2 * n + 1]
        x, y, c = _place()
        cps = []
        for k in range(n):
            h = shards[k].shape[0] // 2
            mine = out[k].at[pl.ds(pl.multiple_of(c * h, 8), h)]
            cp = pltpu.make_async_remote_copy(src_ref=mine, dst_ref=mine, send_sem=send_sems.at[k],
                                              recv_sem=recv_sems.at[k], device_id=(x, y, 1 - c), device_id_type=MESH)
            cp.start()
            cps.append(cp)
        for cp in cps:
            cp.wait()

    return pl.pallas_call(
        body, name=name,
        in_specs=[ANY] * n, out_specs=[ANY] * n,
        out_shape=[jax.ShapeDtypeStruct(sh.shape, F32) for sh in shards],
        input_output_aliases={k: k for k in range(n)},
        scratch_shapes=[pltpu.SemaphoreType.DMA((n,)), pltpu.SemaphoreType.DMA((n,))],
        compiler_params=pltpu.CompilerParams(has_side_effects=True),
    )(*shards)


def _share_halves(shards, small):
    n = len(shards)
    rows = small.shape[0]

    def body(*refs):
        small_ref = refs[n]
        out, total_ref = refs[n + 1:2 * n + 1], refs[2 * n + 1]
        all_ref, send_sems, recv_sems, ssend, srecv = refs[2 * n + 2:]
        x, y, c = _place()
        me = 4 * x + 2 * y + c
        cps = []
        for k in range(n):
            h = shards[k].shape[0] // 2
            mine = out[k].at[pl.ds(pl.multiple_of(c * h, 8), h)]
            cp = pltpu.make_async_remote_copy(src_ref=mine, dst_ref=mine, send_sem=send_sems.at[k],
                                              recv_sem=recv_sems.at[k], device_id=(x, y, 1 - c), device_id_type=MESH)
            cp.start()
            cps.append(cp)
        all_ref[me] = small_ref[...]
        peers = []
        for d in range(1, 8):
            px, py, pc = x ^ (d >> 2), y ^ ((d >> 1) & 1), c ^ (d & 1)
            cp = pltpu.make_async_remote_copy(src_ref=small_ref, dst_ref=all_ref.at[me],
                                              send_sem=ssend.at[d - 1], recv_sem=srecv.at[d - 1],
                                              device_id=(px, py, pc), device_id_type=MESH)
            cp.start()
            peers.append(cp)
        for cp in peers:
            cp.wait()
        acc = all_ref[0]
        for d in range(1, 8):
            acc = acc + all_ref[d]
        total_ref[...] = acc
        for cp in cps:
            cp.wait()

    return pl.pallas_call(
        body, name="share_halves",
        in_specs=[ANY] * n + [VMEM], out_specs=[ANY] * n + [VMEM],
        out_shape=[jax.ShapeDtypeStruct(sh.shape, F32) for sh in shards] + [jax.ShapeDtypeStruct((rows, D), F32)],
        input_output_aliases={k: k for k in range(n)},
        scratch_shapes=[pltpu.VMEM((8, rows, D), F32), pltpu.SemaphoreType.DMA((n,)), pltpu.SemaphoreType.DMA((n,)),
                        pltpu.SemaphoreType.DMA((7,)), pltpu.SemaphoreType.DMA((7,))],
        compiler_params=pltpu.CompilerParams(has_side_effects=True),
    )(*shards, small)


def _adamw(w, g, m, v, name, tr):
    rows, cols = w.shape

    def body(w_ref, g_ref, m_ref, v_ref, d_ref, nm_ref, nv_ref):
        g_ = g_ref[...]
        nm = ADAM_B1 * m_ref[...] + (1.0 - ADAM_B1) * g_
        nv = ADAM_B2 * v_ref[...] + (1.0 - ADAM_B2) * (g_ * g_)
        m_hat = nm / (1.0 - ADAM_B1 ** ADAM_STEP)
        v_hat = nv / (1.0 - ADAM_B2 ** ADAM_STEP)
        d_ref[...] = -ADAM_LR * (m_hat / (jnp.sqrt(v_hat) + ADAM_EPS) + ADAM_WD * w_ref[...])
        nm_ref[...] = nm
        nv_ref[...] = nv

    spec = pl.BlockSpec((tr, cols), lambda i: (i, 0))
    return pl.pallas_call(
        body, name=name, grid=(rows // tr,),
        in_specs=[spec] * 4, out_specs=[spec] * 3,
        out_shape=[jax.ShapeDtypeStruct((rows, cols), F32)] * 3,
        compiler_params=_cp(("parallel",)),
    )(w, g, m, v)


def _local_step(x, target, w_in_t, late_weights, norm_a_g, norm_b_g, sinks_a, ln1_g, ln1_b,
                conv_w, conv_b, ln2_g, ln2_b, slopes, on_grad):
    cwb = jnp.concatenate([conv_w, conv_b[None]], axis=0).reshape(4, 2, FF)

    proj, xb = _proj(x, w_in_t, "proj")
    o_a, lse_a = _attn_a_fwd(proj, sinks_a)
    fwd_b = [_attn_b_fwd(proj, slopes, r) for r in B_DILATIONS]
    w_o = late_weights(1, fwd_b[-1][1])
    o_b, lse_b, cat, z1, h1, h1b = _mix_ln1(x, o_a, [f[0] for f in fwd_b], [f[1] for f in fwd_b],
                                           norm_a_g, norm_b_g, w_o, ln1_g, ln1_b)
    w_up = late_weights(2, h1b)
    up, a, gate, a1 = _up_conv_gelu(h1b, w_up, cwb)
    w_down = late_weights(3, a)
    dz2, dz2b, st2 = _down_ln2_loss(a, w_down, h1, target, ln2_g, ln2_b)

    on_grad(3, *_grad_w(a, dz2b, "grad_w_down", tm=FF // 2))
    dup, dconv = _conv_gelu_bwd(_d_act(dz2b, w_down), up, gate, a1, cwb)
    on_grad(2, *_grad_w(dup, h1b, "grad_w_up", tm=FF // 2, lhs_halves=True))
    dz1, dz1b, st1 = _dh1_ln1_bwd(dz2, dup, w_up, z1, ln1_g)
    tok = on_grad(1, *_grad_w(cat, dz1b, "grad_w_o", tm=512))
    d_oa, d_ob, st_n = _dcat_rms_bwd(dz1b, w_o, o_a, o_b, norm_a_g + tok[0, 0], norm_b_g)
    dqa, dka, dva, dsink = _attn_a_bwd(proj, sinks_a, d_oa, o_a, lse_a)
    bwd_b = None
    for r in B_DILATIONS:
        bwd_b = _attn_b_bwd(proj, slopes, d_ob, o_b, lse_b, r, bwd_b)
    dproj = _dproj_combine(dqa, dka, dva, bwd_b)
    tok = on_grad(0, *_grad_w(dproj, xb, "grad_w_in", tm=WA))
    gx = _grad_x(dz1, dproj, w_in_t, tok)

    dconv = dconv.reshape(4, 2 * FF)
    small = dict(loss=st2[2, 0:1], norm_a_g=st_n[0], norm_b_g=st_n[1], sinks_a=dsink[:, 0],
                 ln1_g=st1[0], ln1_b=st1[1], conv_w=dconv[0:3].reshape(-1), conv_b=dconv[3],
                 ln2_g=st2[0], ln2_b=st2[1])
    return gx, small


SMALL_ORDER = ("loss", "norm_a_g", "norm_b_g", "sinks_a", "ln1_g", "ln1_b", "conv_b", "ln2_g", "ln2_b", "conv_w")
SMALL_SIZES = dict(loss=1, norm_a_g=512, norm_b_g=512, sinks_a=8, ln1_g=D, ln1_b=D, conv_b=2 * FF, ln2_g=D, ln2_b=D,
                   conv_w=3 * 2 * FF)


def _pack(parts, rows):
    flat = jnp.concatenate([parts[k].reshape(-1).astype(F32) for k in parts])
    return jnp.pad(flat, (0, rows * D - flat.shape[0])).reshape(rows, D)


def _unpack(buf, names, sizes):
    flat = buf.reshape(-1)
    out, at = {}, 0
    for k in names:
        out[k] = flat[at:at + sizes[k]]
        at += sizes[k]
    return out


def kernel(x, w_in, norm_a_g, norm_b_g, sinks_a, w_o, ln1_g, ln1_b, w_up, conv_w, conv_b, w_down, ln2_g, ln2_b, loss_target, m_w_in, m_norm_a_g, m_norm_b_g, m_sinks_a, m_w_o, m_ln1_g, m_ln1_b, m_w_up, m_conv_w, m_conv_b, m_w_down, m_ln2_g, m_ln2_b, v_w_in, v_norm_a_g, v_norm_b_g, v_sinks_a, v_w_o, v_ln1_g, v_ln1_b, v_w_up, v_conv_w, v_conv_b, v_w_down, v_ln2_g, v_ln2_b):
    xi, yi, ci = _place()
    chip = (2 * xi + yi).astype(I32)
    core = ci.astype(I32)

    w_in_rows, m_w_in_rows, v_w_in_rows = w_in.T, m_w_in.T, v_w_in.T
    shards = (w_in_rows.astype(BF16), w_o.astype(BF16), w_up.astype(BF16), w_down.astype(BF16))
    w_in_t, conv_w4 = _gather_w_in(shards[0], conv_w)
    conv_w_f = conv_w4.transpose(1, 0, 2).reshape(3, 2 * FF)
    w_started, w_tok = _weights_start(shards[1:], conv_w4)
    slopes = jnp.asarray(SLOPES, F32) + w_tok[0, 0]

    halves_rows = [r // 2 for r in SHARD_ROWS]
    grads4, grads_b4, started = [None] * 4, [None] * 4, [None] * 4

    def on_grad(k, g, g_b):
        grads4[k] = g.reshape(N_CHIPS, 2, halves_rows[k], D)
        grads_b4[k] = g_b.reshape(N_CHIPS, 2, halves_rows[k], D)
        if k > 1:
            return None
        group = (1, 2, 3) if k == 1 else (0,)
        sts, tok = _grads_start([grads_b4[i] for i in group], f"grads_start_{k}")
        for i, st in zip(group, sts):
            started[i] = st
        return tok

    gx, small = _local_step(
        x[0], loss_target[0], w_in_t, lambda k, after: _weights_wait(w_started[k - 1], after, f"weights_wait_{k}"),
        norm_a_g, norm_b_g, sinks_a, ln1_g, ln1_b, conv_w_f, conv_b, ln2_g, ln2_b, slopes, on_grad)

    tiles = (96, 128, 352, 176)
    core_chip = jnp.stack([core, chip])
    got = _grads_wait(started[1:], gx, "grads_wait_1")
    halves = [_sum_partials(grads4[k], got[k - 1], core_chip, f"sum_partials_{k}", tiles[k]) for k in (1, 2, 3)]
    g_w_o, g_w_up_rows, g_w_down = _swap_halves(halves, "swap_halves")
    g_w_up = g_w_up_rows.T
    delta, new_m, new_v = {}, {}, {}
    for k, g, tr in (("w_o", g_w_o, 128), ("w_up", g_w_up, 256), ("w_down", g_w_down, 176)):
        delta[k], new_m[k], new_v[k] = _adamw(dict(w_o=w_o, w_up=w_up, w_down=w_down)[k], g,
                                              dict(w_o=m_w_o, w_up=m_w_up, w_down=m_w_down)[k],
                                              dict(w_o=v_w_o, w_up=v_w_up, w_down=v_w_down)[k], f"adamw_{k}", tr)

    got = _grads_wait(started[:1], delta["w_up"], "grads_wait_0")
    half_in = _sum_partials(grads4[0], got[0], core_chip, "sum_partials_0", tiles[0])
    small_rows = 32
    g_w_in_rows, totals = _share_halves([half_in], _pack({k: small[k] for k in SMALL_ORDER}, small_rows))
    tot = _unpack(totals, SMALL_ORDER, SMALL_SIZES)
    loss = tot["loss"][0]
    cols = 2 * FF // N_CHIPS
    g_conv_w = lax.dynamic_slice(tot["conv_w"].reshape(3, 2 * FF), (0, chip * cols), (3, cols))
    g_small = dict(norm_a_g=tot["norm_a_g"], norm_b_g=tot["norm_b_g"], sinks_a=tot["sinks_a"], ln1_g=tot["ln1_g"],
                   ln1_b=tot["ln1_b"], conv_w=g_conv_w, conv_b=tot["conv_b"], ln2_g=tot["ln2_g"], ln2_b=tot["ln2_b"])

    weights = dict(w_in=w_in, norm_a_g=norm_a_g, norm_b_g=norm_b_g, sinks_a=sinks_a, w_o=w_o, ln1_g=ln1_g, ln1_b=ln1_b,
                   w_up=w_up, conv_w=conv_w, conv_b=conv_b, w_down=w_down, ln2_g=ln2_g, ln2_b=ln2_b)
    ms = dict(w_in=m_w_in, norm_a_g=m_norm_a_g, norm_b_g=m_norm_b_g, sinks_a=m_sinks_a, w_o=m_w_o, ln1_g=m_ln1_g,
              ln1_b=m_ln1_b, w_up=m_w_up, conv_w=m_conv_w, conv_b=m_conv_b, w_down=m_w_down, ln2_g=m_ln2_g, ln2_b=m_ln2_b)
    vs = dict(w_in=v_w_in, norm_a_g=v_norm_a_g, norm_b_g=v_norm_b_g, sinks_a=v_sinks_a, w_o=v_w_o, ln1_g=v_ln1_g,
              ln1_b=v_ln1_b, w_up=v_w_up, conv_w=v_conv_w, conv_b=v_conv_b, w_down=v_w_down, ln2_g=v_ln2_g, ln2_b=v_ln2_b)
    order = list(weights)
    grad = dict(g_small, w_in=g_w_in_rows.T, w_o=g_w_o, w_up=g_w_up, w_down=g_w_down)

    delta["w_in"], new_m["w_in"], new_v["w_in"] = [
        a.T for a in _adamw(w_in_rows, g_w_in_rows, m_w_in_rows, v_w_in_rows, "adamw_w_in", 144)]
    small_names = [k for k in order if k not in delta]
    sizes = {k: weights[k].size for k in small_names}
    rows = 16
    packed = [_pack({k: src[k] for k in small_names}, rows) for src in (weights, grad, ms, vs)]
    for res, buf in zip((delta, new_m, new_v), _adamw(*packed, "adamw_small", rows)):
        for k, val in _unpack(buf, small_names, sizes).items():
            res[k] = val.reshape(weights[k].shape)

    return (loss, gx[None], *[grad[k] for k in order], *[delta[k] for k in order],
            *[new_m[k] for k in order], *[new_v[k] for k in order])
```

```python
import functools
import math

import jax
import jax.numpy as jnp
from jax import lax
from jax.experimental import pallas as pl
from jax.experimental.pallas import tpu as pltpu

F32, BF16, I32 = jnp.float32, jnp.bfloat16, jnp.int32

D = 1024
FF = 2816
HD = 64
NH = 8
WA, WB = 768, 1536
WIN = WA + WB
BLK = 128
ALPHA = 2.0 ** 0.25
LN_EPS, RMS_EPS = 1e-5, 1e-6
SCALE = 1.0 / math.sqrt(HD)
A_MAX_DIST, B_MAX_DIST = 127, 128
B_DILATIONS = (1, 4, 16)
SLOPES = tuple(2.0 ** (-(i + 1)) for i in range(NH))
SHARD_ROWS = (WIN // 4, D // 4, 2 * FF // 4, FF // 4)
N_CHIPS = 4
ADAM_LR, ADAM_B1, ADAM_B2, ADAM_EPS, ADAM_WD, ADAM_STEP = 0.001, 0.9, 0.999, 1e-08, 0.01, 10
MESH = pl.DeviceIdType.MESH
ANY = pl.BlockSpec(memory_space=pl.ANY)
SMEM = pl.BlockSpec(memory_space=pltpu.SMEM)
VMEM = pl.BlockSpec(memory_space=pltpu.VMEM)
HBM = pl.BlockSpec(memory_space=pltpu.HBM)
SEM = pl.BlockSpec(memory_space=pltpu.SEMAPHORE)
DATAFLOW = pltpu.SideEffectType.DATAFLOW_SIDE_EFFECTING


def _cp(sem, mb=48):
    return pltpu.CompilerParams(dimension_semantics=sem, vmem_limit_bytes=mb << 20)


def _nn(a, b):
    return lax.dot_general(a, b, (((1,), (0,)), ((), ())), preferred_element_type=F32)


def _nt(a, b):
    return lax.dot_general(a, b, (((1,), (1,)), ((), ())), preferred_element_type=F32)


def _tn(a, b):
    return lax.dot_general(a, b, (((0,), (0,)), ((), ())), preferred_element_type=F32)


def _resident(shape):
    n = len(shape)
    return pl.BlockSpec(shape, lambda *_: (0,) * n, pipeline_mode=pl.Buffered(1))


def _const(shape):
    n = len(shape)
    return pl.BlockSpec(shape, lambda *_: (0,) * n)


def _proj(x, w_t, name, tm=1024):
    s = x.shape[0]
    n = w_t.shape[0]

    def body(x_ref, w_ref, o_ref, xb_ref):
        xb = x_ref[...].astype(BF16)
        xb_ref[...] = xb
        res = _nt(xb, w_ref[...])
        for g in range(n // 128):
            o_ref[g] = res[:, 128 * g:128 * (g + 1)]

    return pl.pallas_call(
        body, name=name, grid=(s // tm,),
        in_specs=[pl.BlockSpec((tm, D), lambda i: (i, 0)), _resident((n, D))],
        out_specs=[pl.BlockSpec((n // 128, tm, 128), lambda i: (0, i, 0)), pl.BlockSpec((tm, D), lambda i: (i, 0))],
        out_shape=[jax.ShapeDtypeStruct((n // 128, s, 128), F32), jax.ShapeDtypeStruct((s, D), BF16)],
        compiler_params=_cp(("parallel",)),
    )(x, w_t)


def _grad_w(lhs, rhs, name, tm, tk=2048, lhs_halves=False):
    s = rhs.shape[0]
    if lhs_halves:
        per_half = lhs.shape[2] // tm
        n = 2 * lhs.shape[2]
        lhs_spec = pl.BlockSpec((None, tk, tm), lambda i, k: (i // per_half, k, i % per_half))
    else:
        n = lhs.shape[1]
        lhs_spec = pl.BlockSpec((tk, tm), lambda i, k: (k, i))
    nk = s // tk

    def body(l_ref, r_ref, o_ref, ob_ref):
        k = pl.program_id(1)

        @pl.when(k == 0)
        def _():
            o_ref[...] = jnp.zeros_like(o_ref)

        o_ref[...] += _tn(l_ref[...].astype(BF16), r_ref[...].astype(BF16))

        @pl.when(k == nk - 1)
        def _():
            ob_ref[...] = o_ref[...].astype(BF16)

    return pl.pallas_call(
        body, name=name, grid=(n // tm, nk),
        in_specs=[lhs_spec, pl.BlockSpec((tk, D), lambda i, k: (k, 0))],
        out_specs=[pl.BlockSpec((tm, D), lambda i, k: (i, 0))] * 2,
        out_shape=[jax.ShapeDtypeStruct((n, D), F32), jax.ShapeDtypeStruct((n, D), BF16)],
        compiler_params=_cp(("parallel", "arbitrary")),
    )(lhs, rhs)


def _band_base(max_dist, dist_unit, first):
    row = lax.broadcasted_iota(I32, (BLK, 2 * BLK), 0)
    col = lax.broadcasted_iota(I32, (BLK, 2 * BLK), 1)
    dist = BLK + row - col
    ok = (dist >= 0) & (dist <= max_dist)
    if first:
        ok = ok & (col >= BLK)
    return jnp.where(ok, dist.astype(F32) * (-float(dist_unit)), -jnp.inf)


def _half_mask(shape, e):
    lane = lax.broadcasted_iota(I32, shape, 1)
    return (lane < HD) if e == 0 else (lane >= HD)


def _to_half(x, e, g):
    if g != e:
        x = pltpu.roll(x, HD, 1)
    return jnp.where(_half_mask(x.shape, g), x, 0.0)


def _stack_heads(scalars, tile):
    return jnp.concatenate([scalars[0] * tile, scalars[1] * tile], axis=0)


def _pair_fwd(q2, kb, vb, base, slopes, kv_heads, sinks):
    lo = _half_mask((BLK, 2 * HD), 0)
    if sinks is not None:
        o2 = lse2 = None
        for e in (0, 1):
            g = kv_heads[e]
            qv = (_to_half(q2, e, g) * SCALE).astype(BF16)
            s = _nt(qv, kb) + slopes[e] * base
            m = jnp.maximum(jnp.max(s, axis=1, keepdims=True), sinks[e])
            p = jnp.exp(s - m)
            l = jnp.sum(p, axis=1, keepdims=True) + jnp.exp(sinks[e] - m)
            oh = _nn(p.astype(BF16), vb) / l
            if g != e:
                oh = pltpu.roll(oh, HD, 1)
            lse = jnp.broadcast_to(m + jnp.log(l), (BLK, 2 * HD))
            o2 = oh if e == 0 else jnp.where(lo, o2, oh)
            lse2 = lse if e == 0 else jnp.where(lo, lse2, lse)
        return o2, lse2
    qs = jnp.concatenate([_to_half(q2, e, kv_heads[e]) * SCALE for e in (0, 1)], axis=0).astype(BF16)
    s = _nt(qs, kb) + (base if slopes is None else _stack_heads(slopes, base))
    m = jnp.max(s, axis=1, keepdims=True)
    p = jnp.exp(s - m)
    l = jnp.sum(p, axis=1, keepdims=True)
    o = _nn(p.astype(BF16), vb) / l
    lse = m + jnp.log(l)
    halves = []
    for e in (0, 1):
        oh = o[e * BLK:(e + 1) * BLK]
        halves.append(pltpu.roll(oh, HD, 1) if kv_heads[e] != e else oh)
    o2 = jnp.where(lo, halves[0], halves[1])
    lse2 = jnp.where(lo, jnp.broadcast_to(lse[:BLK], (BLK, 2 * HD)), jnp.broadcast_to(lse[BLK:], (BLK, 2 * HD)))
    return o2, lse2


def _pair_bwd(q2, kb, vb, do2, o2, lse2, base, slopes, kv_heads, sinks):
    lo = _half_mask((BLK, 2 * HD), 0)
    prod = do2 * o2
    lses, deltas = [], []
    for e in (0, 1):
        hq = _half_mask((BLK, 2 * HD), e)
        lses.append(jnp.max(jnp.where(hq, lse2, -jnp.inf), axis=1, keepdims=True))
        deltas.append(jnp.sum(jnp.where(hq, prod, 0.0), axis=1, keepdims=True))
    lse = jnp.concatenate(lses, axis=0)
    delta = jnp.concatenate(deltas, axis=0)
    qs = jnp.concatenate([_to_half(q2, e, kv_heads[e]) * SCALE for e in (0, 1)], axis=0).astype(BF16)
    dos = jnp.concatenate([_to_half(do2, e, kv_heads[e]) for e in (0, 1)], axis=0).astype(BF16)
    p = jnp.exp(_nt(qs, kb) + (base if slopes is None else _stack_heads(slopes, base)) - lse)
    ds = (p * (_nt(dos, vb) - delta)).astype(BF16)
    dq = _nn(ds, kb) * SCALE
    halves = []
    for e in (0, 1):
        dqh = dq[e * BLK:(e + 1) * BLK]
        halves.append(pltpu.roll(dqh, HD, 1) if kv_heads[e] != e else dqh)
    dq2 = jnp.where(lo, halves[0], halves[1])
    dk2 = _tn(ds, qs)
    dv2 = _tn(p.astype(BF16), dos)
    dsinks = []
    if sinks is not None:
        for e in (0, 1):
            dsinks.append(jnp.sum(-jnp.exp(sinks[e] - lses[e]) * deltas[e], axis=0, keepdims=True))
    return dq2, dk2, dv2, dsinks


A_BLOCKS_PER_STEP = 2
A_BLOCKS_PER_STEP_BWD = 1


def _attn_a_fwd(proj, sinks):
    s = proj.shape[1]
    nq = A_BLOCKS_PER_STEP
    rows = BLK * nq
    steps = s // rows

    def body(sink_ref, q_ref, kp_ref, kc_ref, vp_ref, vc_ref, o_ref, lse_ref):
        n = pl.program_id(0)
        base_rest = _band_base(A_MAX_DIST, 1, False)
        base_0 = jnp.where(n > 0, base_rest, _band_base(A_MAX_DIST, 1, True))
        for i in range(nq):
            cur = pl.ds(i * BLK, BLK)
            k_prev = kc_ref[pl.ds((i - 1) * BLK, BLK), :] if i > 0 else kp_ref[...]
            v_prev = vc_ref[pl.ds((i - 1) * BLK, BLK), :] if i > 0 else vp_ref[...]
            kb = jnp.concatenate([k_prev, kc_ref[cur, :]], axis=0).astype(BF16)
            vb = jnp.concatenate([v_prev, vc_ref[cur, :]], axis=0).astype(BF16)
            for j in range(NH // 2):
                g = j // 2
                o2, lse2 = _pair_fwd(q_ref[j, cur, :], kb, vb, base_rest if i > 0 else base_0,
                                     (SLOPES[2 * j], SLOPES[2 * j + 1]), (g, g), (sink_ref[2 * j], sink_ref[2 * j + 1]))
                o_ref[j, cur, :] = o2
                lse_ref[j, cur, :] = lse2

    before = lambda n: jnp.maximum(n * nq - 1, 0)
    slab = lambda g: pl.BlockSpec((None, rows, 128), lambda n: (g, n, 0))
    edge = lambda g: pl.BlockSpec((None, BLK, 128), lambda n: (g, before(n), 0))
    quad = pl.BlockSpec((4, rows, 128), lambda n: (0, n, 0))
    return pl.pallas_call(
        body, name="attn_a_fwd", grid=(steps,),
        in_specs=[SMEM, quad, edge(4), slab(4), edge(5), slab(5)],
        out_specs=[quad, quad],
        out_shape=[jax.ShapeDtypeStruct((4, s, 128), F32)] * 2,
        compiler_params=_cp(("parallel",)),
    )(sinks, proj, proj, proj, proj, proj)


def _attn_a_bwd(proj, sinks, d_o, o, lse):
    s = proj.shape[1]
    nq = A_BLOCKS_PER_STEP_BWD
    rows = BLK * nq
    steps = s // rows

    def body(sink_ref, q_ref, kp_ref, kc_ref, vp_ref, vc_ref, do_ref, o_ref, lse_ref,
             dq_ref, dk_ref, dv_ref, dsink_ref, kcar, vcar):
        n = pl.program_id(0)

        @pl.when(n == 0)
        def _():
            kcar[...] = jnp.zeros_like(kcar)
            vcar[...] = jnp.zeros_like(vcar)
            dsink_ref[...] = jnp.zeros_like(dsink_ref)

        dk_ref[...] = kcar[...]
        dv_ref[...] = vcar[...]

        @pl.when(n < steps)
        def _():
            base_rest = _band_base(A_MAX_DIST, 1, False)
            base_0 = jnp.where(n > 0, base_rest, _band_base(A_MAX_DIST, 1, True))
            for i in range(nq):
                cur = pl.ds(i * BLK, BLK)
                k_prev = kc_ref[pl.ds((i - 1) * BLK, BLK), :] if i > 0 else kp_ref[...]
                v_prev = vc_ref[pl.ds((i - 1) * BLK, BLK), :] if i > 0 else vp_ref[...]
                kb = jnp.concatenate([k_prev, kc_ref[cur, :]], axis=0).astype(BF16)
                vb = jnp.concatenate([v_prev, vc_ref[cur, :]], axis=0).astype(BF16)
                dk_win = dv_win = None
                for j in range(NH // 2):
                    g = j // 2
                    dq2, dk2, dv2, dsk = _pair_bwd(q_ref[j, cur, :], kb, vb, do_ref[j, cur, :], o_ref[j, cur, :],
                                                   lse_ref[j, cur, :], base_rest if i > 0 else base_0,
                                                   (SLOPES[2 * j], SLOPES[2 * j + 1]), (g, g),
                                                   (sink_ref[2 * j], sink_ref[2 * j + 1]))
                    dq_ref[j, cur, :] = dq2
                    dk_win = dk2 if j == 0 else dk_win + dk2
                    dv_win = dv2 if j == 0 else dv_win + dv2
                    for e in (0, 1):
                        h = 2 * j + e
                        dsink_ref[h:h + 1, :] += jnp.broadcast_to(dsk[e], (1, 128))
                if i == 0:
                    last = pl.ds((nq - 1) * BLK, BLK)
                    dk_ref[last, :] += dk_win[:BLK]
                    dv_ref[last, :] += dv_win[:BLK]
                else:
                    kcar[pl.ds((i - 1) * BLK, BLK), :] += dk_win[:BLK]
                    vcar[pl.ds((i - 1) * BLK, BLK), :] += dv_win[:BLK]
                kcar[cur, :] = dk_win[BLK:]
                vcar[cur, :] = dv_win[BLK:]

    cur_step = lambda n: jnp.minimum(n, steps - 1)
    before = lambda n: jnp.maximum(cur_step(n) * nq - 1, 0)
    out_prev = lambda n: jnp.maximum(n - 1, 0)
    quad = pl.BlockSpec((4, rows, 128), lambda n: (0, cur_step(n), 0))
    slab = lambda g: pl.BlockSpec((None, rows, 128), lambda n: (g, cur_step(n), 0))
    edge = lambda g: pl.BlockSpec((None, BLK, 128), lambda n: (g, before(n), 0))
    return pl.pallas_call(
        body, name="attn_a_bwd", grid=(steps + 1,),
        in_specs=[SMEM, quad, edge(4), slab(4), edge(5), slab(5), quad, quad, quad],
        out_specs=[quad,
                   pl.BlockSpec((rows, 128), lambda n: (out_prev(n), 0)),
                   pl.BlockSpec((rows, 128), lambda n: (out_prev(n), 0)),
                   pl.BlockSpec((NH, 128), lambda n: (0, 0))],
        out_shape=[jax.ShapeDtypeStruct((4, s, 128), F32), jax.ShapeDtypeStruct((s, 128), F32),
                   jax.ShapeDtypeStruct((s, 128), F32), jax.ShapeDtypeStruct((NH, 128), F32)],
        scratch_shapes=[pltpu.VMEM((rows, 128), F32), pltpu.VMEM((rows, 128), F32)],
        compiler_params=_cp(("arbitrary",)),
    )(sinks, proj, proj, proj, proj, proj, d_o, o, lse)


def _stream(rho, i, r):
    start = i * BLK * r + rho
    return pl.ds(start, BLK, stride=r) if r > 1 else pl.ds(start, BLK)


def _for_streams(r, fn, side_by_side=4):
    if r <= side_by_side:
        for rho in range(r):
            fn(rho)
    else:
        def group(it, carry):
            for u in range(side_by_side):
                fn(side_by_side * it + u)
            return carry

        lax.fori_loop(0, r // side_by_side, group, 0)


B_BLOCKS_PER_STEP = {1: 8, 4: 2, 16: 1}
B_BLOCKS_PER_STEP_FWD = {1: 8, 4: 2, 16: 1}


def _attn_b_fwd(proj, slopes, r):
    s = proj.shape[1]
    nq = B_BLOCKS_PER_STEP_FWD[r]
    rows = BLK * r * nq
    steps = s // rows
    qc, kc, vc = WA // 128, WA // 128 + 4, WA // 128 + 8

    def body(slope_ref, q_ref, kp_ref, kc_ref, vp_ref, vc_ref, o_ref, lse_ref):
        j = pl.program_id(0)
        sb = pl.program_id(1)
        sl2 = (slope_ref[2 * j], slope_ref[2 * j + 1])
        bias_rest = _stack_heads(sl2, _band_base(B_MAX_DIST, r, False))
        bias_0 = jnp.where(sb > 0, bias_rest, _stack_heads(sl2, _band_base(B_MAX_DIST, r, True)))

        def stream(rho):
            for i in range(nq):
                cur = _stream(rho, i, r)
                k_prev = kc_ref[_stream(rho, i - 1, r), :] if i > 0 else kp_ref[_stream(rho, 0, r), :]
                v_prev = vc_ref[_stream(rho, i - 1, r), :] if i > 0 else vp_ref[_stream(rho, 0, r), :]
                kb = jnp.concatenate([k_prev, kc_ref[cur, :]], axis=0).astype(BF16)
                vb = jnp.concatenate([v_prev, vc_ref[cur, :]], axis=0).astype(BF16)
                o2, lse2 = _pair_fwd(q_ref[cur, :], kb, vb, bias_rest if i > 0 else bias_0, None, (0, 1), None)
                o_ref[cur, :] = o2
                lse_ref[cur, :] = lse2

        _for_streams(r, stream, side_by_side=8)

    before = lambda sb: jnp.maximum(sb * nq - 1, 0)
    return pl.pallas_call(
        body, name=f"attn_b_fwd_r{r}", grid=(NH // 2, steps),
        in_specs=[SMEM,
                  pl.BlockSpec((None, rows, 128), lambda j, sb: (qc + j, sb, 0)),
                  pl.BlockSpec((None, BLK * r, 128), lambda j, sb: (kc + j, before(sb), 0)),
                  pl.BlockSpec((None, rows, 128), lambda j, sb: (kc + j, sb, 0)),
                  pl.BlockSpec((None, BLK * r, 128), lambda j, sb: (vc + j, before(sb), 0)),
                  pl.BlockSpec((None, rows, 128), lambda j, sb: (vc + j, sb, 0))],
        out_specs=[pl.BlockSpec((None, rows, 128), lambda j, sb: (j, sb, 0))] * 2,
        out_shape=[jax.ShapeDtypeStruct((4, s, 128), F32)] * 2,
        compiler_params=_cp(("parallel", "parallel")),
    )(slopes, proj, proj, proj, proj, proj)


def _attn_b_bwd(proj, slopes, d_o, o, lse, r, so_far=None):
    s = proj.shape[1]
    nq = B_BLOCKS_PER_STEP[r]
    rows = BLK * r * nq
    steps = s // rows
    qc, kc, vc = WA // 128, WA // 128 + 4, WA // 128 + 8
    chained = so_far is not None

    def body(slope_ref, q_ref, kp_ref, kc_ref, vp_ref, vc_ref, do_ref, o_ref, lse_ref, *rest):
        if chained:
            pq_ref, pk_ref, pv_ref, dq_ref, dk_ref, dv_ref, kcar, vcar = rest
        else:
            dq_ref, dk_ref, dv_ref, kcar, vcar = rest
        j = pl.program_id(0)
        sb = pl.program_id(1)

        @pl.when(sb == 0)
        def _():
            kcar[...] = jnp.zeros_like(kcar)
            vcar[...] = jnp.zeros_like(vcar)

        if chained:
            dk_ref[...] = kcar[...] + pk_ref[...]
            dv_ref[...] = vcar[...] + pv_ref[...]
        else:
            dk_ref[...] = kcar[...]
            dv_ref[...] = vcar[...]

        @pl.when(sb < steps)
        def _():
            sl2 = (slope_ref[2 * j], slope_ref[2 * j + 1])
            bias_rest = _stack_heads(sl2, _band_base(B_MAX_DIST, r, False))
            bias_0 = jnp.where(sb > 0, bias_rest, _stack_heads(sl2, _band_base(B_MAX_DIST, r, True)))

            def stream(rho):
                for i in range(nq):
                    cur = _stream(rho, i, r)
                    k_prev = kc_ref[_stream(rho, i - 1, r), :] if i > 0 else kp_ref[_stream(rho, 0, r), :]
                    v_prev = vc_ref[_stream(rho, i - 1, r), :] if i > 0 else vp_ref[_stream(rho, 0, r), :]
                    kb = jnp.concatenate([k_prev, kc_ref[cur, :]], axis=0).astype(BF16)
                    vb = jnp.concatenate([v_prev, vc_ref[cur, :]], axis=0).astype(BF16)
                    dq2, dk2, dv2, _ = _pair_bwd(q_ref[cur, :], kb, vb, do_ref[cur, :], o_ref[cur, :], lse_ref[cur, :],
                                                 bias_rest if i > 0 else bias_0, None, (0, 1), None)
                    dq_ref[cur, :] = dq2 + pq_ref[cur, :] if chained else dq2
                    if i == 0:
                        last = _stream(rho, nq - 1, r)
                        dk_ref[last, :] += dk2[:BLK]
                        dv_ref[last, :] += dv2[:BLK]
                    else:
                        kcar[_stream(rho, i - 1, r), :] += dk2[:BLK]
                        vcar[_stream(rho, i - 1, r), :] += dv2[:BLK]
                    kcar[cur, :] = dk2[BLK:]
                    vcar[cur, :] = dv2[BLK:]

            _for_streams(r, stream, side_by_side=8)

    cur_step = lambda sb: jnp.minimum(sb, steps - 1)
    before = lambda sb: jnp.maximum(cur_step(sb) * nq - 1, 0)
    out_prev = lambda sb: jnp.maximum(sb - 1, 0)
    tile = lambda slab: pl.BlockSpec((None, rows, 128), lambda j, sb: (slab + j, cur_step(sb), 0))
    edge = lambda slab: pl.BlockSpec((None, BLK * r, 128), lambda j, sb: (slab + j, before(sb), 0))
    late = pl.BlockSpec((None, rows, 128), lambda j, sb: (j, out_prev(sb), 0))
    grads = [tile(0), late, late]
    return pl.pallas_call(
        body, name=f"attn_b_bwd_r{r}", grid=(NH // 2, steps + 1),
        in_specs=[SMEM, tile(qc), edge(kc), tile(kc), edge(vc), tile(vc), tile(0), tile(0), tile(0)]
        + (grads if chained else []),
        out_specs=grads,
        out_shape=[jax.ShapeDtypeStruct((4, s, 128), F32)] * 3,
        scratch_shapes=[pltpu.VMEM((rows, 128), F32), pltpu.VMEM((rows, 128), F32)],
        compiler_params=_cp(("parallel", "arbitrary")),
    )(slopes, proj, proj, proj, proj, proj, d_o, o, lse, *(so_far if chained else ()))


def _row(v):
    return v.reshape(1, -1)


def _layer_norm_stats(z):
    mu = jnp.mean(z, axis=-1, keepdims=True)
    zc = z - mu
    var = jnp.mean(zc * zc, axis=-1, keepdims=True)
    rstd = lax.rsqrt(var + LN_EPS)
    return zc * rstd, rstd


def _layer_norm_bwd(dh, zh, rstd, g):
    dzh = dh * g
    return rstd * (dzh - jnp.mean(dzh, axis=-1, keepdims=True) - zh * jnp.mean(dzh * zh, axis=-1, keepdims=True))


def _rms(o):
    return lax.rsqrt(jnp.mean(o * o, axis=-1, keepdims=True) + RMS_EPS)


def _mix_ln1(x, o_a, o_b, lse_b, norm_a_g, norm_b_g, w_o, ln1_g, ln1_b, tm=256):
    s = x.shape[0]

    def wide(ref):
        return jnp.concatenate([ref[j] for j in range(4)], axis=1)

    def body(x_ref, oa_ref, ob1, ob2, ob3, l1, l2, l3, ga_ref, gb_ref, wo_ref, g_ref, b_ref,
             obm_ref, lse_ref, cat_ref, z1_ref, h1_ref, h1b_ref):
        la, lb, lc = wide(l1), wide(l2), wide(l3)
        m = jnp.maximum(jnp.maximum(la, lb), lc)
        ea, eb, ec = jnp.exp(la - m), jnp.exp(lb - m), jnp.exp(lc - m)
        den = ea + eb + ec
        obm = (ea / den) * wide(ob1) + (eb / den) * wide(ob2) + (ec / den) * wide(ob3)
        lse = m + jnp.log(den)
        for j in range(4):
            obm_ref[j] = obm[:, 128 * j:128 * (j + 1)]
            lse_ref[j] = lse[:, 128 * j:128 * (j + 1)]
        oa = wide(oa_ref)
        na = oa * _rms(oa) * ga_ref[...]
        nb_ = obm * _rms(obm) * gb_ref[...]
        cat = jnp.concatenate([na, nb_], axis=1).astype(BF16)
        cat_ref[...] = cat
        z1 = ALPHA * x_ref[...] + _nn(cat, wo_ref[...])
        z1_ref[...] = z1
        zh, _ = _layer_norm_stats(z1)
        h1 = zh * g_ref[...] + b_ref[...]
        h1_ref[...] = h1
        h1b_ref[...] = h1.astype(BF16)

    t512 = pl.BlockSpec((4, tm, 128), lambda i: (0, i, 0))
    td = pl.BlockSpec((tm, D), lambda i: (i, 0))
    return pl.pallas_call(
        body, name="mix_ln1", grid=(s // tm,),
        in_specs=[td] + [t512] * 7 + [_const((1, 512))] * 2 + [_resident((D, D))] + [_const((1, D))] * 2,
        out_specs=[t512, t512, td, td, td, td],
        out_shape=[jax.ShapeDtypeStruct((4, s, 128), F32), jax.ShapeDtypeStruct((4, s, 128), F32),
                   jax.ShapeDtypeStruct((s, D), BF16), jax.ShapeDtypeStruct((s, D), F32),
                   jax.ShapeDtypeStruct((s, D), F32), jax.ShapeDtypeStruct((s, D), BF16)],
        compiler_params=_cp(("parallel",)),
    )(x, o_a, *o_b, *lse_b, _row(norm_a_g), _row(norm_b_g), w_o, _row(ln1_g), _row(ln1_b))


def _gelu_and_grad(x):
    c = math.sqrt(2.0 / math.pi)
    x2 = x * x
    cx = c * x
    t = jnp.tanh(cx * (1.0 + 0.044715 * x2))
    q = 1.0 + t
    g = (0.5 * x) * q
    dg = 0.5 * q + ((0.5 * cx) * (1.0 - t * t)) * (1.0 + (3.0 * 0.044715) * x2)
    return g, dg


def _shift_down(u, before):
    n = u.shape[0]
    ext = jnp.concatenate([before, u], axis=0)
    return pltpu.roll(ext, 1, 0)[8:], pltpu.roll(ext, 2, 0)[8:]


def _shift_up(u, after):
    n = u.shape[0]
    ext = jnp.concatenate([u, after], axis=0)
    return pltpu.roll(ext, n + 7, 0)[:n], pltpu.roll(ext, n + 6, 0)[:n]


def _up_conv_gelu(h1b, w_up, cwb, tm=256, tn=FF // 2, chunk_rows=16, piece_cols=512):
    s = h1b.shape[0]
    n_i = s // tm
    n_t = (FF // tn) * n_i

    def body(h_ref, wg_ref, wv_ref, c_ref, up_ref, a_ref, g_ref, a1_ref, pend_a, pend_b, carry):
        t = pl.program_id(0)
        row_tile = jnp.maximum(t - 1, 0) % n_i
        w_refs = (wg_ref, wv_ref)

        @pl.when(t == 0)
        def _():
            pend_b[...] = jnp.zeros_like(pend_b)
            carry[...] = jnp.zeros_like(carry)

        def step(dst, src):
            def chunk(c, before):
                rows = pl.ds(c * chunk_rows, chunk_rows)
                u, last = [], []
                for half in (0, 1):
                    up = src[half, rows, :]
                    r1, r2 = _shift_down(up, before[half])
                    u.append(r2 * c_ref[0, half:half + 1, :] + r1 * c_ref[1, half:half + 1, :]
                             + up * c_ref[2, half:half + 1, :] + c_ref[3, half:half + 1, :])
                    last.append(up[chunk_rows - 8:])
                g, dg = _gelu_and_grad(u[0])
                a_ref[rows, :] = (g * u[1]).astype(BF16)
                g_ref[rows, :] = g.astype(BF16)
                a1_ref[rows, :] = (u[1] * dg).astype(BF16)
                return tuple(last)

            edge = tuple(jnp.where(row_tile > 0, carry[half], 0.0) for half in (0, 1))
            pieces = [(half, c0, min(piece_cols, tn - c0)) for half in (0, 1) for c0 in range(0, tn, piece_cols)]
            n_c = tm // chunk_rows
            done = 0
            for p, (half, c0, width) in enumerate(pieces):
                cols = slice(c0, c0 + width)
                up = _nn(h_ref[...], w_refs[half][:, cols])
                up_ref[half, :, cols] = up.astype(BF16)
                dst[half, :, cols] = up
                upto = n_c * (p + 1) // len(pieces)
                for c in range(done, upto):
                    edge = chunk(c, edge)
                done = upto
            for half in (0, 1):
                carry[half] = edge[half]

        @pl.when(t % 2 == 0)
        def _():
            step(pend_a, pend_b)

        @pl.when(t % 2 == 1)
        def _():
            step(pend_b, pend_a)

    mm = lambda t: jnp.minimum(t, n_t - 1)
    ew = lambda t: jnp.maximum(t - 1, 0)
    out_tile = pl.BlockSpec((tm, tn), lambda t: (ew(t) % n_i, ew(t) // n_i))
    return pl.pallas_call(
        body, name="up_conv_gelu", grid=(n_t + 1,),
        in_specs=[pl.BlockSpec((tm, D), lambda t: (mm(t) % n_i, 0)),
                  pl.BlockSpec((D, tn), lambda t: (0, mm(t) // n_i)),
                  pl.BlockSpec((D, tn), lambda t: (0, FF // tn + mm(t) // n_i)),
                  pl.BlockSpec((4, 2, tn), lambda t: (0, 0, ew(t) // n_i))],
        out_specs=[pl.BlockSpec((2, tm, tn), lambda t: (0, mm(t) % n_i, mm(t) // n_i)), out_tile, out_tile, out_tile],
        out_shape=[jax.ShapeDtypeStruct((2, s, FF), BF16)] + [jax.ShapeDtypeStruct((s, FF), BF16)] * 3,
        scratch_shapes=[pltpu.VMEM((2, tm, tn), F32), pltpu.VMEM((2, tm, tn), F32), pltpu.VMEM((2, 8, tn), F32)],
        compiler_params=_cp(("arbitrary",)),
    )(h1b, w_up, w_up, cwb)


def _down_ln2_loss(a, w_down, h1, target, ln2_g, ln2_b, tm=512):
    s = a.shape[0]

    def body(a_ref, w_ref, h_ref, t_ref, g_ref, b_ref, dz_ref, dzb_ref, st_ref):
        @pl.when(pl.program_id(0) == 0)
        def _():
            st_ref[...] = jnp.zeros_like(st_ref)

        z2 = ALPHA * h_ref[...] + _nn(a_ref[...], w_ref[...])
        zh, rstd = _layer_norm_stats(z2)
        diff = zh * g_ref[...] + b_ref[...] - t_ref[...]
        part = 0.5 * jnp.sum(jnp.mean(diff * diff, axis=-1, keepdims=True), axis=0, keepdims=True)
        dy = diff * (1.0 / D)
        st_ref[0:1, :] += jnp.sum(dy * zh, axis=0, keepdims=True)
        st_ref[1:2, :] += jnp.sum(dy, axis=0, keepdims=True)
        st_ref[2:3, :] += jnp.broadcast_to(part, (1, D))
        dz = _layer_norm_bwd(dy, zh, rstd, g_ref[...])
        dz_ref[...] = dz
        dzb_ref[...] = dz.astype(BF16)

    td = pl.BlockSpec((tm, D), lambda i: (i, 0))
    return pl.pallas_call(
        body, name="down_ln2_loss", grid=(s // tm,),
        in_specs=[pl.BlockSpec((tm, FF), lambda i: (i, 0)), _resident((FF, D)), td, td, _const((1, D)), _const((1, D))],
        out_specs=[td, td, _const((8, D))],
        out_shape=[jax.ShapeDtypeStruct((s, D), F32), jax.ShapeDtypeStruct((s, D), BF16),
                   jax.ShapeDtypeStruct((8, D), F32)],
        compiler_params=_cp(("arbitrary",)),
    )(a, w_down, h1, target, _row(ln2_g), _row(ln2_b))


def _d_act(dz2b, w_down, tm=1024):
    s = dz2b.shape[0]

    def body(dz_ref, w_ref, o_ref):
        o_ref[...] = _nt(dz_ref[...], w_ref[...])

    return pl.pallas_call(
        body, name="d_act", grid=(s // tm,),
        in_specs=[pl.BlockSpec((tm, D), lambda i: (i, 0)), _resident((FF, D))],
        out_specs=pl.BlockSpec((tm, FF), lambda i: (i, 0)),
        out_shape=jax.ShapeDtypeStruct((s, FF), F32),
        compiler_params=_cp(("parallel",)),
    )(dz2b, w_down)


def _conv_gelu_bwd(da, up, g, a1, cwb, tm=256, tn=FF // 2, chunk_rows=16):
    s = da.shape[0]
    n_i = s // tm
    n_c = tm // chunk_rows

    def body(da_ref, up_ref, g_ref, a1_ref, c_ref, dup_ref, dc_ref, carry):
        @pl.when(pl.program_id(1) == 0)
        def _():
            carry[...] = jnp.zeros_like(carry)
            dc_ref[...] = jnp.zeros_like(dc_ref)

        def fold(v):
            return jnp.sum(v.reshape(chunk_rows // 8, 8, v.shape[1]), axis=0)

        def chunk(cc, state):
            after, sums = state
            rows = pl.ds((n_c - 1 - cc) * chunk_rows, chunk_rows)
            da_c = da_ref[rows, :]
            dus = (da_c * a1_ref[rows, :].astype(F32), da_c * g_ref[rows, :].astype(F32))
            head, new_sums = [], []
            for half in (0, 1):
                du = dus[half]
                up = up_ref[half, rows, :].astype(F32)
                l1, l2 = _shift_up(du, after[half])
                dup = (du * c_ref[2, half:half + 1, :] + l1 * c_ref[1, half:half + 1, :]
                       + l2 * c_ref[0, half:half + 1, :])
                dup_ref[half, rows, :] = dup.astype(BF16)
                parts = (fold(l2 * up), fold(l1 * up), fold(du * up), fold(du))
                new_sums.append(parts if sums is None else tuple(a + b for a, b in zip(sums[half], parts)))
                head.append(du[:8])
            return tuple(head), new_sums

        state = ((carry[0], carry[1]), None)
        for cc in range(n_c):
            state = chunk(cc, state)
        head, sums = state
        for half in (0, 1):
            carry[half] = head[half]
            for k in range(4):
                dc_ref[k, half:half + 1, :] += jnp.sum(sums[half][k], axis=0, keepdims=True)

    rev = lambda ii: n_i - 1 - ii
    tile = pl.BlockSpec((tm, tn), lambda j, ii: (rev(ii), j))
    pair = pl.BlockSpec((2, tm, tn), lambda j, ii: (0, rev(ii), j))
    per_col = pl.BlockSpec((4, 2, tn), lambda j, ii: (0, 0, j))
    return pl.pallas_call(
        body, name="conv_gelu_bwd", grid=(FF // tn, n_i),
        in_specs=[tile, pair, tile, tile, per_col],
        out_specs=[pair, per_col],
        out_shape=[jax.ShapeDtypeStruct((2, s, FF), BF16), jax.ShapeDtypeStruct((4, 2, FF), F32)],
        scratch_shapes=[pltpu.VMEM((2, 8, tn), F32)],
        compiler_params=_cp(("parallel", "arbitrary")),
    )(da, up, g, a1, cwb)


def _dh1_ln1_bwd(dz2, dup, w_up, z1, ln1_g, tm=512):
    s = dz2.shape[0]

    def body(dz2_ref, dup_ref, w_ref, z1_ref, g_ref, dz1_ref, dz1b_ref, st_ref):
        @pl.when(pl.program_id(0) == 0)
        def _():
            st_ref[...] = jnp.zeros_like(st_ref)

        dh = ALPHA * dz2_ref[...] + _nt(dup_ref[0], w_ref[:, :FF]) + _nt(dup_ref[1], w_ref[:, FF:])
        zh, rstd = _layer_norm_stats(z1_ref[...])
        st_ref[0:1, :] += jnp.sum(dh * zh, axis=0, keepdims=True)
        st_ref[1:2, :] += jnp.sum(dh, axis=0, keepdims=True)
        dz = _layer_norm_bwd(dh, zh, rstd, g_ref[...])
        dz1_ref[...] = dz
        dz1b_ref[...] = dz.astype(BF16)

    td = pl.BlockSpec((tm, D), lambda i: (i, 0))
    return pl.pallas_call(
        body, name="dh1_ln1_bwd", grid=(s // tm,),
        in_specs=[td, pl.BlockSpec((2, tm, FF), lambda i: (0, i, 0)), _resident((D, 2 * FF)), td, _const((1, D))],
        out_specs=[td, td, _const((8, D))],
        out_shape=[jax.ShapeDtypeStruct((s, D), F32), jax.ShapeDtypeStruct((s, D), BF16),
                   jax.ShapeDtypeStruct((8, D), F32)],
        compiler_params=_cp(("arbitrary",), 58),
    )(dz2, dup, w_up, z1, _row(ln1_g))


def _dcat_rms_bwd(dz1b, w_o, o_a, o_b, norm_a_g, norm_b_g, tm=512):
    s = dz1b.shape[0]

    def body(dz_ref, w_ref, oa_ref, ob_ref, ga_ref, gb_ref, da_ref, db_ref, st_ref):
        @pl.when(pl.program_id(0) == 0)
        def _():
            st_ref[...] = jnp.zeros_like(st_ref)

        dcat = _nt(dz_ref[...], w_ref[...])
        for k, (o_ref, g_ref, d_ref) in enumerate(((oa_ref, ga_ref, da_ref), (ob_ref, gb_ref, db_ref))):
            o = jnp.concatenate([o_ref[j] for j in range(4)], axis=1)
            dn = dcat[:, 512 * k:512 * (k + 1)]
            rr = _rms(o)
            oh = o * rr
            st_ref[k:k + 1, :] += jnp.sum(dn * oh, axis=0, keepdims=True)
            doh = dn * g_ref[...]
            d_o = rr * (doh - oh * jnp.mean(doh * oh, axis=-1, keepdims=True))
            for j in range(4):
                d_ref[j] = d_o[:, 128 * j:128 * (j + 1)]

    t512 = pl.BlockSpec((4, tm, 128), lambda i: (0, i, 0))
    return pl.pallas_call(
        body, name="dcat_rms_bwd", grid=(s // tm,),
        in_specs=[pl.BlockSpec((tm, D), lambda i: (i, 0)), _resident((D, D)), t512, t512,
                  _const((1, 512)), _const((1, 512))],
        out_specs=[t512, t512, _const((8, 512))],
        out_shape=[jax.ShapeDtypeStruct((4, s, 128), F32), jax.ShapeDtypeStruct((4, s, 128), F32),
                   jax.ShapeDtypeStruct((8, 512), F32)],
        compiler_params=_cp(("arbitrary",)),
    )(dz1b, w_o, o_a, o_b, _row(norm_a_g), _row(norm_b_g))


def _dproj_combine(dqa, dka, dva, dqkv_b, tm=256):
    s = dka.shape[0]

    def body(qa, ka, va, qb, kb, vb, o_ref):
        for j in range(4):
            o_ref[:, 128 * j:128 * (j + 1)] = qa[j].astype(BF16)
            o_ref[:, 768 + 128 * j:768 + 128 * (j + 1)] = qb[j].astype(BF16)
            o_ref[:, 1280 + 128 * j:1280 + 128 * (j + 1)] = kb[j].astype(BF16)
            o_ref[:, 1792 + 128 * j:1792 + 128 * (j + 1)] = vb[j].astype(BF16)
        o_ref[:, 512:640] = ka[...].astype(BF16)
        o_ref[:, 640:768] = va[...].astype(BF16)

    t512 = pl.BlockSpec((4, tm, 128), lambda i: (0, i, 0))
    t128 = pl.BlockSpec((tm, 128), lambda i: (i, 0))
    return pl.pallas_call(
        body, name="dproj_combine", grid=(s // tm,),
        in_specs=[t512, t128, t128] + [t512] * 3,
        out_specs=pl.BlockSpec((tm, WIN), lambda i: (i, 0)),
        out_shape=jax.ShapeDtypeStruct((s, WIN), BF16),
        compiler_params=_cp(("parallel",)),
    )(dqa, dka, dva, *dqkv_b)


def _grad_x(dz1, dproj, w_in_t, zero, tm=1024):
    s = dz1.shape[0]

    def body(dz_ref, dp_ref, w_ref, z_ref, o_ref):
        o_ref[...] = ALPHA * dz_ref[...] + _nn(dp_ref[...], w_ref[...]) + z_ref[0:1, 0:1]

    td = pl.BlockSpec((tm, D), lambda i: (i, 0))
    return pl.pallas_call(
        body, name="grad_x", grid=(s // tm,),
        in_specs=[td, pl.BlockSpec((tm, WIN), lambda i: (i, 0)), _resident((WIN, D)), _const((8, 128))],
        out_specs=td, out_shape=jax.ShapeDtypeStruct((s, D), F32),
        compiler_params=_cp(("parallel",)),
    )(dz1, dproj, w_in_t, zero)


def _place():
    return lax.axis_index("x"), lax.axis_index("y"), lax.axis_index("c")


def _other_chips(x, y):
    return [(1 - x, y), (x, 1 - y), (1 - x, 1 - y)]


def _hbm(a):
    return pltpu.with_memory_space_constraint(a, pltpu.HBM)


def _gather_w_in(shard, conv_w):
    rows_k = shard.shape[0]
    half = rows_k // 2

    def body(src, conv_src, out, conv_out, send_sems, recv_sems):
        x, y, c = _place()
        b = 2 * x + y
        sibling = (x, y, 1 - c)
        chips = _other_chips(x, y)

        def copy(idx, chip_b, core, to, first_hop=False):
            rows = out.at[pl.ds(pl.multiple_of(chip_b * rows_k + core * half, 16), half)]
            s_ref = src.at[pl.ds(pl.multiple_of(core * half, 16), half)] if first_hop else rows
            return pltpu.make_async_remote_copy(src_ref=s_ref, dst_ref=rows, send_sem=send_sems.at[idx],
                                                recv_sem=recv_sems.at[idx], device_id=to, device_id_type=MESH)

        def own_copy():
            return pltpu.make_async_remote_copy(
                src_ref=src, dst_ref=out.at[pl.ds(pl.multiple_of(b * rows_k, 16), rows_k)], send_sem=send_sems.at[6],
                recv_sem=recv_sems.at[6], device_id=sibling, device_id_type=MESH)

        def conv_copy(idx, chip_b, to):
            return pltpu.make_async_remote_copy(src_ref=conv_src, dst_ref=conv_out.at[chip_b],
                                                send_sem=send_sems.at[7 + idx], recv_sem=recv_sems.at[7 + idx],
                                                device_id=to, device_id_type=MESH)

        started = [own_copy(), conv_copy(3, b, sibling)]
        for jn, chip in enumerate(chips):
            started += [copy(jn, b, c, (chip[0], chip[1], c), first_hop=True), conv_copy(jn, b, (chip[0], chip[1], c))]
        for cp in started:
            cp.start()
        for jn, chip in enumerate(chips):
            cb = 2 * chip[0] + chip[1]
            copy(jn, cb, c, (chip[0], chip[1], c)).wait_recv()
            cp = copy(3 + jn, cb, c, sibling)
            cp.start()
            started.append(cp)
        for jn, chip in enumerate(chips):
            cb = 2 * chip[0] + chip[1]
            copy(3 + jn, cb, 1 - c, sibling).wait_recv()
            conv_copy(jn, cb, (chip[0], chip[1], c)).wait_recv()
        own_copy().wait_recv()
        conv_copy(3, b, sibling).wait_recv()
        for cp in started:
            cp.wait_send()

    return pl.pallas_call(
        body, name="gather_w_in",
        in_specs=[ANY, ANY], out_specs=[ANY, ANY],
        out_shape=[jax.ShapeDtypeStruct((N_CHIPS * rows_k, D), BF16), jax.ShapeDtypeStruct((N_CHIPS,) + conv_w.shape, F32)],
        scratch_shapes=[pltpu.SemaphoreType.DMA((11,)), pltpu.SemaphoreType.DMA((11,))],
        compiler_params=pltpu.CompilerParams(has_side_effects=True),
    )(shard, conv_w)


def _weight_copies(shard, land, send_sems, recv_sems, arrivals):
    x, y, c = _place()
    n_rows, n_cols = shard.shape
    peers = [(px, py, c) for px, py in _other_chips(x, y)] + [(x, y, 1 - c)]
    cps = []
    for jn, peer in enumerate(peers):
        at = 2 * peer[0] + peer[1] if arrivals else 2 * x + y
        if land.shape[1] == n_cols:
            dst = land.at[pl.ds(pl.multiple_of(at * n_rows, 16), n_rows)]
        else:
            dst = land.at[:, pl.ds(pl.multiple_of(at * n_cols, 128), n_cols)]
        cps.append(pltpu.make_async_remote_copy(src_ref=shard, dst_ref=dst, send_sem=send_sems.at[jn],
                                                recv_sem=recv_sems.at[jn], device_id=peer, device_id_type=MESH))
    return cps


def _weights_start(shards, after):
    n = len(shards)
    lands = [lax.empty((N_CHIPS * sh.shape[0], D) if sh.shape[1] == D else (D, N_CHIPS * sh.shape[1]), BF16)
             for sh in shards]

    def body(*refs):
        src, land = refs[:n], refs[n:2 * n]
        send_sems, recv_sems = refs[2 * n + 1:3 * n + 1], refs[3 * n + 1:4 * n + 1]
        for k in range(n):
            for send in _weight_copies(src[k], land[k], send_sems[k], recv_sems[k], False):
                send.start()
        refs[-1][...] = jnp.zeros_like(refs[-1])

    res = pl.pallas_call(
        body, name="weights_start",
        in_specs=[HBM] * (2 * n) + [ANY], out_specs=[SEM] * (2 * n) + [HBM] * (2 * n) + [VMEM],
        out_shape=[pltpu.SemaphoreType.DMA((4,))] * (2 * n)
        + [pltpu.HBM(a.shape, a.dtype) for a in (*shards, *lands)] + [jax.ShapeDtypeStruct((8, 128), F32)],
        input_output_aliases={i: i + 2 * n for i in range(2 * n)},
        compiler_params=pltpu.CompilerParams(has_side_effects=DATAFLOW),
    )(*[_hbm(a) for a in (*shards, *lands)], after)
    return [(res[k], res[n + k], res[2 * n + k], res[3 * n + k]) for k in range(n)], res[-1]


def _weights_wait(started, after, name):
    send_sems, recv_sems, shard, land = started

    def body(s_ref, l_ref, send_ref, recv_ref, after_ref, s_out, l_out):
        for cp in _weight_copies(s_ref, l_ref, send_ref, recv_ref, True):
            cp.wait_send()
            cp.wait_recv()

    return pl.pallas_call(
        body, name=name,
        in_specs=[HBM, HBM, SEM, SEM, ANY], out_specs=[HBM, HBM],
        out_shape=[pltpu.HBM(shard.shape, shard.dtype), pltpu.HBM(land.shape, land.dtype)],
        input_output_aliases={0: 0, 1: 1},
        compiler_params=pltpu.CompilerParams(has_side_effects=DATAFLOW),
    )(shard, land, send_sems, recv_sems, after)[1]


def _grad_copies(g_ref, land_ref, send_sems, recv_sems):
    x, y, c = _place()
    cps = []
    for d in range(1, 8):
        px, py, pc = x ^ (d >> 2), y ^ ((d >> 1) & 1), c ^ (d & 1)
        cps.append(pltpu.make_async_remote_copy(
            src_ref=g_ref.at[2 * px + py, pc], dst_ref=land_ref.at[d - 1], send_sem=send_sems.at[d - 1],
            recv_sem=recv_sems.at[d - 1], device_id=(px, py, pc), device_id_type=MESH))
    return cps


def _grads_start(grads_b, name):
    n = len(grads_b)
    lands = [lax.empty((7, g.shape[2], D), BF16) for g in grads_b]

    def body(*refs):
        g, land = refs[:n], refs[n:2 * n]
        send_sems, recv_sems = refs[2 * n:3 * n], refs[3 * n:4 * n]
        for k in range(n):
            for cp in _grad_copies(g[k], land[k], send_sems[k], recv_sems[k]):
                cp.start()
        refs[-1][...] = jnp.zeros_like(refs[-1])

    res = pl.pallas_call(
        body, name=name,
        in_specs=[HBM] * (2 * n), out_specs=[SEM] * (2 * n) + [HBM] * (2 * n) + [VMEM],
        out_shape=[pltpu.SemaphoreType.DMA((7,))] * (2 * n)
        + [pltpu.HBM(a.shape, a.dtype) for a in (*grads_b, *lands)] + [jax.ShapeDtypeStruct((8, 128), F32)],
        input_output_aliases={i: i + 2 * n for i in range(2 * n)},
        compiler_params=pltpu.CompilerParams(has_side_effects=DATAFLOW),
    )(*[_hbm(a) for a in (*grads_b, *lands)])
    return [(res[k], res[n + k], res[2 * n + k], res[3 * n + k]) for k in range(n)], res[-1]


def _grads_wait(started, after, name):
    n = len(started)

    def body(*refs):
        g, land = refs[:n], refs[n:2 * n]
        send_sems, recv_sems = refs[2 * n:3 * n], refs[3 * n:4 * n]
        for k in range(n):
            for cp in _grad_copies(g[k], land[k], send_sems[k], recv_sems[k]):
                cp.wait_send()
                cp.wait_recv()

    gs = [st[2] for st in started]
    lands = [st[3] for st in started]
    res = pl.pallas_call(
        body, name=name,
        in_specs=[HBM] * (2 * n) + [SEM] * (2 * n) + [ANY], out_specs=[HBM] * (2 * n),
        out_shape=[pltpu.HBM(a.shape, a.dtype) for a in (*gs, *lands)],
        input_output_aliases={i: i for i in range(2 * n)},
        compiler_params=pltpu.CompilerParams(has_side_effects=DATAFLOW),
    )(*gs, *lands, *[st[0] for st in started], *[st[1] for st in started], after)
    return res[n:]


def _sum_partials(grad4, got, cb, name, tr):
    h = grad4.shape[2]
    per_half = h // tr

    def body(cb_ref, g_ref, o_ref, out_ref):
        acc = g_ref[...]
        for j in range(7):
            acc = acc + o_ref[j].astype(F32)
        out_ref[...] = acc

    return pl.pallas_call(
        body, name=name,
        grid_spec=pltpu.PrefetchScalarGridSpec(
            num_scalar_prefetch=1, grid=(per_half,),
            in_specs=[pl.BlockSpec((None, None, tr, D), lambda i, cb_ref: (cb_ref[1], cb_ref[0], i, 0)),
                      pl.BlockSpec((7, tr, D), lambda i, cb_ref: (0, i, 0))],
            out_specs=pl.BlockSpec((tr, D), lambda i, cb_ref: (cb_ref[0] * per_half + i, 0))),
        out_shape=jax.ShapeDtypeStruct((2 * h, D), F32),
        compiler_params=_cp(("arbitrary",)),
    )(cb, grad4, got)


def _swap_halves(shards, name):
    n = len(shards)

    def body(*refs):
        out, send_sems, recv_sems = refs[n:2 * n], refs[2 * n], refs[2 * n + 1]
        x, y, c = _place()
        cps = []
        for k in range(n):
            h = shards[k].shape[0] // 2
            mine = out[k].at[pl.ds(pl.multiple_of(c * h, 8), h)]
            cp = pltpu.make_async_remote_copy(src_ref=mine, dst_ref=mine, send_sem=send_sems.at[k],
                                              recv_sem=recv_sems.at[k], device_id=(x, y, 1 - c), device_id_type=MESH)
            cp.start()
            cps.append(cp)
        for cp in cps:
            cp.wait()

    return pl.pallas_call(
        body, name=name,
        in_specs=[ANY] * n, out_specs=[ANY] * n,
        out_shape=[jax.ShapeDtypeStruct(sh.shape, F32) for sh in shards],
        input_output_aliases={k: k for k in range(n)},
        scratch_shapes=[pltpu.SemaphoreType.DMA((n,)), pltpu.SemaphoreType.DMA((n,))],
        compiler_params=pltpu.CompilerParams(has_side_effects=True),
    )(*shards)


def _share_halves(shards, small):
    n = len(shards)
    rows = small.shape[0]

    def body(*refs):
        small_ref = refs[n]
        out, total_ref = refs[n + 1:2 * n + 1], refs[2 * n + 1]
        all_ref, send_sems, recv_sems, ssend, srecv = refs[2 * n + 2:]
        x, y, c = _place()
        me = 4 * x + 2 * y + c
        cps = []
        for k in range(n):
            h = shards[k].shape[0] // 2
            mine = out[k].at[pl.ds(pl.multiple_of(c * h, 8), h)]
            cp = pltpu.make_async_remote_copy(src_ref=mine, dst_ref=mine, send_sem=send_sems.at[k],
                                              recv_sem=recv_sems.at[k], device_id=(x, y, 1 - c), device_id_type=MESH)
            cp.start()
            cps.append(cp)
        all_ref[me] = small_ref[...]
        peers = []
        for d in range(1, 8):
            px, py, pc = x ^ (d >> 2), y ^ ((d >> 1) & 1), c ^ (d & 1)
            cp = pltpu.make_async_remote_copy(src_ref=small_ref, dst_ref=all_ref.at[me],
                                              send_sem=ssend.at[d - 1], recv_sem=srecv.at[d - 1],
                                              device_id=(px, py, pc), device_id_type=MESH)
            cp.start()
            peers.append(cp)
        for cp in peers:
            cp.wait()
        acc = all_ref[0]
        for d in range(1, 8):
            acc = acc + all_ref[d]
        total_ref[...] = acc
        for cp in cps:
            cp.wait()

    return pl.pallas_call(
        body, name="share_halves",
        in_specs=[ANY] * n + [VMEM], out_specs=[ANY] * n + [VMEM],
        out_shape=[jax.ShapeDtypeStruct(sh.shape, F32) for sh in shards] + [jax.ShapeDtypeStruct((rows, D), F32)],
        input_output_aliases={k: k for k in range(n)},
        scratch_shapes=[pltpu.VMEM((8, rows, D), F32), pltpu.SemaphoreType.DMA((n,)), pltpu.SemaphoreType.DMA((n,)),
                        pltpu.SemaphoreType.DMA((7,)), pltpu.SemaphoreType.DMA((7,))],
        compiler_params=pltpu.CompilerParams(has_side_effects=True),
    )(*shards, small)


def _adamw(w, g, m, v, name, tr):
    rows, cols = w.shape

    def body(w_ref, g_ref, m_ref, v_ref, d_ref, nm_ref, nv_ref):
        g_ = g_ref[...]
        nm = ADAM_B1 * m_ref[...] + (1.0 - ADAM_B1) * g_
        nv = ADAM_B2 * v_ref[...] + (1.0 - ADAM_B2) * (g_ * g_)
        m_hat = nm / (1.0 - ADAM_B1 ** ADAM_STEP)
        v_hat = nv / (1.0 - ADAM_B2 ** ADAM_STEP)
        d_ref[...] = -ADAM_LR * (m_hat / (jnp.sqrt(v_hat) + ADAM_EPS) + ADAM_WD * w_ref[...])
        nm_ref[...] = nm
        nv_ref[...] = nv

    spec = pl.BlockSpec((tr, cols), lambda i: (i, 0))
    return pl.pallas_call(
        body, name=name, grid=(rows // tr,),
        in_specs=[spec] * 4, out_specs=[spec] * 3,
        out_shape=[jax.ShapeDtypeStruct((rows, cols), F32)] * 3,
        compiler_params=_cp(("parallel",)),
    )(w, g, m, v)


def _local_step(x, target, w_in_t, late_weights, norm_a_g, norm_b_g, sinks_a, ln1_g, ln1_b,
                conv_w, conv_b, ln2_g, ln2_b, slopes, on_grad):
    cwb = jnp.concatenate([conv_w, conv_b[None]], axis=0).reshape(4, 2, FF)

    proj, xb = _proj(x, w_in_t, "proj")
    o_a, lse_a = _attn_a_fwd(proj, sinks_a)
    fwd_b = [_attn_b_fwd(proj, slopes, r) for r in B_DILATIONS]
    w_o = late_weights(1, fwd_b[-1][1])
    o_b, lse_b, cat, z1, h1, h1b = _mix_ln1(x, o_a, [f[0] for f in fwd_b], [f[1] for f in fwd_b],
                                           norm_a_g, norm_b_g, w_o, ln1_g, ln1_b)
    w_up = late_weights(2, h1b)
    up, a, gate, a1 = _up_conv_gelu(h1b, w_up, cwb)
    w_down = late_weights(3, a)
    dz2, dz2b, st2 = _down_ln2_loss(a, w_down, h1, target, ln2_g, ln2_b)

    on_grad(3, *_grad_w(a, dz2b, "grad_w_down", tm=FF // 2))
    dup, dconv = _conv_gelu_bwd(_d_act(dz2b, w_down), up, gate, a1, cwb)
    on_grad(2, *_grad_w(dup, h1b, "grad_w_up", tm=FF // 2, lhs_halves=True))
    dz1, dz1b, st1 = _dh1_ln1_bwd(dz2, dup, w_up, z1, ln1_g)
    tok = on_grad(1, *_grad_w(cat, dz1b, "grad_w_o", tm=512))
    d_oa, d_ob, st_n = _dcat_rms_bwd(dz1b, w_o, o_a, o_b, norm_a_g + tok[0, 0], norm_b_g)
    dqa, dka, dva, dsink = _attn_a_bwd(proj, sinks_a, d_oa, o_a, lse_a)
    bwd_b = None
    for r in B_DILATIONS:
        bwd_b = _attn_b_bwd(proj, slopes, d_ob, o_b, lse_b, r, bwd_b)
    dproj = _dproj_combine(dqa, dka, dva, bwd_b)
    tok = on_grad(0, *_grad_w(dproj, xb, "grad_w_in", tm=WA))
    gx = _grad_x(dz1, dproj, w_in_t, tok)

    dconv = dconv.reshape(4, 2 * FF)
    small = dict(loss=st2[2, 0:1], norm_a_g=st_n[0], norm_b_g=st_n[1], sinks_a=dsink[:, 0],
                 ln1_g=st1[0], ln1_b=st1[1], conv_w=dconv[0:3].reshape(-1), conv_b=dconv[3],
                 ln2_g=st2[0], ln2_b=st2[1])
    return gx, small


SMALL_ORDER = ("loss", "norm_a_g", "norm_b_g", "sinks_a", "ln1_g", "ln1_b", "conv_b", "ln2_g", "ln2_b", "conv_w")
SMALL_SIZES = dict(loss=1, norm_a_g=512, norm_b_g=512, sinks_a=8, ln1_g=D, ln1_b=D, conv_b=2 * FF, ln2_g=D, ln2_b=D,
                   conv_w=3 * 2 * FF)


def _pack(parts, rows):
    flat = jnp.concatenate([parts[k].reshape(-1).astype(F32) for k in parts])
    return jnp.pad(flat, (0, rows * D - flat.shape[0])).reshape(rows, D)


def _unpack(buf, names, sizes):
    flat = buf.reshape(-1)
    out, at = {}, 0
    for k in names:
        out[k] = flat[at:at + sizes[k]]
        at += sizes[k]
    return out


def kernel(x, w_in, norm_a_g, norm_b_g, sinks_a, w_o, ln1_g, ln1_b, w_up, conv_w, conv_b, w_down, ln2_g, ln2_b, loss_target, m_w_in, m_norm_a_g, m_norm_b_g, m_sinks_a, m_w_o, m_ln1_g, m_ln1_b, m_w_up, m_conv_w, m_conv_b, m_w_down, m_ln2_g, m_ln2_b, v_w_in, v_norm_a_g, v_norm_b_g, v_sinks_a, v_w_o, v_ln1_g, v_ln1_b, v_w_up, v_conv_w, v_conv_b, v_w_down, v_ln2_g, v_ln2_b):
    xi, yi, ci = _place()
    chip = (2 * xi + yi).astype(I32)
    core = ci.astype(I32)

    w_in_rows, m_w_in_rows, v_w_in_rows = w_in.T, m_w_in.T, v_w_in.T
    shards = (w_in_rows.astype(BF16), w_o.astype(BF16), w_up.astype(BF16), w_down.astype(BF16))
    w_in_t, conv_w4 = _gather_w_in(shards[0], conv_w)
    conv_w_f = conv_w4.transpose(1, 0, 2).reshape(3, 2 * FF)
    w_started, w_tok = _weights_start(shards[1:], conv_w4)
    slopes = jnp.asarray(SLOPES, F32) + w_tok[0, 0]

    halves_rows = [r // 2 for r in SHARD_ROWS]
    grads4, grads_b4, started = [None] * 4, [None] * 4, [None] * 4

    def on_grad(k, g, g_b):
        grads4[k] = g.reshape(N_CHIPS, 2, halves_rows[k], D)
        grads_b4[k] = g_b.reshape(N_CHIPS, 2, halves_rows[k], D)
        if k > 1:
            return None
        group = (1, 2, 3) if k == 1 else (0,)
        sts, tok = _grads_start([grads_b4[i] for i in group], f"grads_start_{k}")
        for i, st in zip(group, sts):
            started[i] = st
        return tok

    gx, small = _local_step(
        x[0], loss_target[0], w_in_t, lambda k, after: _weights_wait(w_started[k - 1], after, f"weights_wait_{k}"),
        norm_a_g, norm_b_g, sinks_a, ln1_g, ln1_b, conv_w_f, conv_b, ln2_g, ln2_b, slopes, on_grad)

    tiles = (96, 128, 352, 176)
    core_chip = jnp.stack([core, chip])
    got = _grads_wait(started[1:], gx, "grads_wait_1")
    halves = [_sum_partials(grads4[k], got[k - 1], core_chip, f"sum_partials_{k}", tiles[k]) for k in (1, 2, 3)]
    g_w_o, g_w_up_rows, g_w_down = _swap_halves(halves, "swap_halves")
    g_w_up = g_w_up_rows.T
    delta, new_m, new_v = {}, {}, {}
    for k, g, tr in (("w_o", g_w_o, 128), ("w_up", g_w_up, 256), ("w_down", g_w_down, 176)):
        delta[k], new_m[k], new_v[k] = _adamw(dict(w_o=w_o, w_up=w_up, w_down=w_down)[k], g,
                                              dict(w_o=m_w_o, w_up=m_w_up, w_down=m_w_down)[k],
                                              dict(w_o=v_w_o, w_up=v_w_up, w_down=v_w_down)[k], f"adamw_{k}", tr)

    got = _grads_wait(started[:1], delta["w_up"], "grads_wait_0")
    half_in = _sum_partials(grads4[0], got[0], core_chip, "sum_partials_0", tiles[0])
    small_rows = 32
    g_w_in_rows, totals = _share_halves([half_in], _pack({k: small[k] for k in SMALL_ORDER}, small_rows))
    tot = _unpack(totals, SMALL_ORDER, SMALL_SIZES)
    loss = tot["loss"][0]
    cols = 2 * FF // N_CHIPS
    g_conv_w = lax.dynamic_slice(tot["conv_w"].reshape(3, 2 * FF), (0, chip * cols), (3, cols))
    g_small = dict(norm_a_g=tot["norm_a_g"], norm_b_g=tot["norm_b_g"], sinks_a=tot["sinks_a"], ln1_g=tot["ln1_g"],
                   ln1_b=tot["ln1_b"], conv_w=g_conv_w, conv_b=tot["conv_b"], ln2_g=tot["ln2_g"], ln2_b=tot["ln2_b"])

    weights = dict(w_in=w_in, norm_a_g=norm_a_g, norm_b_g=norm_b_g, sinks_a=sinks_a, w_o=w_o, ln1_g=ln1_g, ln1_b=ln1_b,
                   w_up=w_up, conv_w=conv_w, conv_b=conv_b, w_down=w_down, ln2_g=ln2_g, ln2_b=ln2_b)
    ms = dict(w_in=m_w_in, norm_a_g=m_norm_a_g, norm_b_g=m_norm_b_g, sinks_a=m_sinks_a, w_o=m_w_o, ln1_g=m_ln1_g,
              ln1_b=m_ln1_b, w_up=m_w_up, conv_w=m_conv_w, conv_b=m_conv_b, w_down=m_w_down, ln2_g=m_ln2_g, ln2_b=m_ln2_b)
    vs = dict(w_in=v_w_in, norm_a_g=v_norm_a_g, norm_b_g=v_norm_b_g, sinks_a=v_sinks_a, w_o=v_w_o, ln1_g=v_ln1_g,
              ln1_b=v_ln1_b, w_up=v_w_up, conv_w=v_conv_w, conv_b=v_conv_b, w_down=v_w_down, ln2_g=v_ln2_g, ln2_b=v_ln2_b)
    order = list(weights)
    grad = dict(g_small, w_in=g_w_in_rows.T, w_o=g_w_o, w_up=g_w_up, w_down=g_w_down)

    delta["w_in"], new_m["w_in"], new_v["w_in"] = [
        a.T for a in _adamw(w_in_rows, g_w_in_rows, m_w_in_rows, v_w_in_rows, "adamw_w_in", 144)]
    small_names = [k for k in order if k not in delta]
    sizes = {k: weights[k].size for k in small_names}
    rows = 16
    packed = [_pack({k: src[k] for k in small_names}, rows) for src in (weights, grad, ms, vs)]
    for res, buf in zip((delta, new_m, new_v), _adamw(*packed, "adamw_small", rows)):
        for k, val in _unpack(buf, small_names, sizes).items():
            res[k] = val.reshape(weights[k].shape)

    return (loss, gx[None], *[grad[k] for k in order], *[delta[k] for k in order],
            *[new_m[k] for k in order], *[new_v[k] for k in order])
```

```python
import functools
import math

import jax
import jax.numpy as jnp
from jax import lax
from jax.experimental import pallas as pl
from jax.experimental.pallas import tpu as pltpu

F32, BF16, I32 = jnp.float32, jnp.bfloat16, jnp.int32

D = 1024
FF = 2816
HD = 64
NH = 8
WA, WB = 768, 1536
WIN = WA + WB
BLK = 128
ALPHA = 2.0 ** 0.25
LN_EPS, RMS_EPS = 1e-5, 1e-6
SCALE = 1.0 / math.sqrt(HD)
A_MAX_DIST, B_MAX_DIST = 127, 128
B_DILATIONS = (1, 4, 16)
SLOPES = tuple(2.0 ** (-(i + 1)) for i in range(NH))
SHARD_ROWS = (WIN // 4, D // 4, 2 * FF // 4, FF // 4)
N_CHIPS = 4
ADAM_LR, ADAM_B1, ADAM_B2, ADAM_EPS, ADAM_WD, ADAM_STEP = 0.001, 0.9, 0.999, 1e-08, 0.01, 10
MESH = pl.DeviceIdType.MESH
ANY = pl.BlockSpec(memory_space=pl.ANY)
SMEM = pl.BlockSpec(memory_space=pltpu.SMEM)
VMEM = pl.BlockSpec(memory_space=pltpu.VMEM)
HBM = pl.BlockSpec(memory_space=pltpu.HBM)
SEM = pl.BlockSpec(memory_space=pltpu.SEMAPHORE)
DATAFLOW = pltpu.SideEffectType.DATAFLOW_SIDE_EFFECTING


def _cp(sem, mb=48):
    return pltpu.CompilerParams(dimension_semantics=sem, vmem_limit_bytes=mb << 20)


def _nn(a, b):
    return lax.dot_general(a, b, (((1,), (0,)), ((), ())), preferred_element_type=F32)


def _nt(a, b):
    return lax.dot_general(a, b, (((1,), (1,)), ((), ())), preferred_element_type=F32)


def _tn(a, b):
    return lax.dot_general(a, b, (((0,), (0,)), ((), ())), preferred_element_type=F32)


def _resident(shape):
    n = len(shape)
    return pl.BlockSpec(shape, lambda *_: (0,) * n, pipeline_mode=pl.Buffered(1))


def _const(shape):
    n = len(shape)
    return pl.BlockSpec(shape, lambda *_: (0,) * n)


def _proj(x, w_t, name, tm=512):
    s = x.shape[0]
    n = w_t.shape[0]

    def body(x_ref, w_ref, o_ref, xb_ref):
        xb = x_ref[...].astype(BF16)
        xb_ref[...] = xb
        res = _nt(xb, w_ref[...])
        for g in range(n // 128):
            o_ref[g] = res[:, 128 * g:128 * (g + 1)]

    return pl.pallas_call(
        body, name=name, grid=(s // tm,),
        in_specs=[pl.BlockSpec((tm, D), lambda i: (i, 0)), _resident((n, D))],
        out_specs=[pl.BlockSpec((n // 128, tm, 128), lambda i: (0, i, 0)), pl.BlockSpec((tm, D), lambda i: (i, 0))],
        out_shape=[jax.ShapeDtypeStruct((n // 128, s, 128), F32), jax.ShapeDtypeStruct((s, D), BF16)],
        compiler_params=_cp(("parallel",)),
    )(x, w_t)


def _grad_w(lhs, rhs, name, tm, tk=2048, lhs_halves=False):
    s = rhs.shape[0]
    if lhs_halves:
        per_half = lhs.shape[2] // tm
        n = 2 * lhs.shape[2]
        lhs_spec = pl.BlockSpec((None, tk, tm), lambda i, k: (i // per_half, k, i % per_half))
    else:
        n = lhs.shape[1]
        lhs_spec = pl.BlockSpec((tk, tm), lambda i, k: (k, i))
    nk = s // tk

    def body(l_ref, r_ref, o_ref, ob_ref):
        k = pl.program_id(1)
        part = _tn(l_ref[...], r_ref[...])

        @pl.when(k == 0)
        def _():
            o_ref[...] = part

        @pl.when(k > 0)
        def _():
            o_ref[...] += part

        @pl.when(k == nk - 1)
        def _():
            ob_ref[...] = o_ref[...].astype(BF16)

    return pl.pallas_call(
        body, name=name, grid=(n // tm, nk),
        in_specs=[lhs_spec, pl.BlockSpec((tk, D), lambda i, k: (k, 0))],
        out_specs=[pl.BlockSpec((tm, D), lambda i, k: (i, 0))] * 2,
        out_shape=[jax.ShapeDtypeStruct((n, D), F32), jax.ShapeDtypeStruct((n, D), BF16)],
        compiler_params=_cp(("parallel", "arbitrary")),
    )(lhs, rhs)


def _band_base(max_dist, dist_unit, first):
    row = lax.broadcasted_iota(I32, (BLK, 2 * BLK), 0)
    col = lax.broadcasted_iota(I32, (BLK, 2 * BLK), 1)
    dist = BLK + row - col
    ok = (dist >= 0) & (dist <= max_dist)
    if first:
        ok = ok & (col >= BLK)
    return jnp.where(ok, dist.astype(F32) * (-float(dist_unit)), -jnp.inf)


def _half_mask(shape, e):
    lane = lax.broadcasted_iota(I32, shape, 1)
    return (lane < HD) if e == 0 else (lane >= HD)


def _to_half(x, e, g):
    if g != e:
        x = pltpu.roll(x, HD, 1)
    return jnp.where(_half_mask(x.shape, g), x, 0.0)


def _stack_heads(scalars, tile):
    return jnp.concatenate([scalars[0] * tile, scalars[1] * tile], axis=0)


def _pair_fwd(q2, kb, vb, base, slopes, kv_heads, sinks):
    lo = _half_mask((BLK, 2 * HD), 0)
    if sinks is not None:
        o2 = lse2 = None
        for e in (0, 1):
            g = kv_heads[e]
            qv = (_to_half(q2, e, g) * SCALE).astype(BF16)
            s = _nt(qv, kb) + slopes[e] * base
            m = jnp.maximum(jnp.max(s, axis=1, keepdims=True), sinks[e])
            p = jnp.exp(s - m)
            l = jnp.sum(p, axis=1, keepdims=True) + jnp.exp(sinks[e] - m)
            oh = _nn(p.astype(BF16), vb) / l
            if g != e:
                oh = pltpu.roll(oh, HD, 1)
            lse = jnp.broadcast_to(m + jnp.log(l), (BLK, 2 * HD))
            o2 = oh if e == 0 else jnp.where(lo, o2, oh)
            lse2 = lse if e == 0 else jnp.where(lo, lse2, lse)
        return o2, lse2
    qs = jnp.concatenate([_to_half(q2, e, kv_heads[e]) * SCALE for e in (0, 1)], axis=0).astype(BF16)
    s = _nt(qs, kb) + (base if slopes is None else _stack_heads(slopes, base))
    m = jnp.max(s, axis=1, keepdims=True)
    p = jnp.exp(s - m)
    l = jnp.sum(p, axis=1, keepdims=True)
    o = _nn(p.astype(BF16), vb) / l
    lse = m + jnp.log(l)
    halves = []
    for e in (0, 1):
        oh = o[e * BLK:(e + 1) * BLK]
        halves.append(pltpu.roll(oh, HD, 1) if kv_heads[e] != e else oh)
    o2 = jnp.where(lo, halves[0], halves[1])
    lse2 = jnp.where(lo, jnp.broadcast_to(lse[:BLK], (BLK, 2 * HD)), jnp.broadcast_to(lse[BLK:], (BLK, 2 * HD)))
    return o2, lse2


def _pair_bwd(q2, kb, vb, do2, o2, lse2, base, slopes, kv_heads, sinks):
    lo = _half_mask((BLK, 2 * HD), 0)
    prod = do2 * o2
    lses, deltas = [], []
    for e in (0, 1):
        hq = _half_mask((BLK, 2 * HD), e)
        lses.append(jnp.max(jnp.where(hq, lse2, -jnp.inf), axis=1, keepdims=True))
        deltas.append(jnp.sum(jnp.where(hq, prod, 0.0), axis=1, keepdims=True))
    lse = jnp.concatenate(lses, axis=0)
    delta = jnp.concatenate(deltas, axis=0)
    qs = jnp.concatenate([_to_half(q2, e, kv_heads[e]) * SCALE for e in (0, 1)], axis=0).astype(BF16)
    dos = jnp.concatenate([_to_half(do2, e, kv_heads[e]) for e in (0, 1)], axis=0).astype(BF16)
    p = jnp.exp(_nt(qs, kb) + (base if slopes is None else _stack_heads(slopes, base)) - lse)
    ds = (p * (_nt(dos, vb) - delta)).astype(BF16)
    dq = _nn(ds, kb) * SCALE
    halves = []
    for e in (0, 1):
        dqh = dq[e * BLK:(e + 1) * BLK]
        halves.append(pltpu.roll(dqh, HD, 1) if kv_heads[e] != e else dqh)
    dq2 = jnp.where(lo, halves[0], halves[1])
    dk2 = _tn(ds, qs)
    dv2 = _tn(p.astype(BF16), dos)
    dsinks = []
    if sinks is not None:
        for e in (0, 1):
            dsinks.append(jnp.sum(-jnp.exp(sinks[e] - lses[e]) * deltas[e], axis=0, keepdims=True))
    return dq2, dk2, dv2, dsinks


A_BLOCKS_PER_STEP = 2
A_BLOCKS_PER_STEP_BWD = 1


def _attn_a_fwd(proj, sinks):
    s = proj.shape[1]
    nq = A_BLOCKS_PER_STEP
    rows = BLK * nq
    steps = s // rows

    def body(sink_ref, q_ref, kp_ref, kc_ref, vp_ref, vc_ref, o_ref, lse_ref):
        n = pl.program_id(0)
        base_rest = _band_base(A_MAX_DIST, 1, False)
        base_0 = jnp.where(n > 0, base_rest, _band_base(A_MAX_DIST, 1, True))
        for i in range(nq):
            cur = pl.ds(i * BLK, BLK)
            k_prev = kc_ref[pl.ds((i - 1) * BLK, BLK), :] if i > 0 else kp_ref[...]
            v_prev = vc_ref[pl.ds((i - 1) * BLK, BLK), :] if i > 0 else vp_ref[...]
            kb = jnp.concatenate([k_prev, kc_ref[cur, :]], axis=0).astype(BF16)
            vb = jnp.concatenate([v_prev, vc_ref[cur, :]], axis=0).astype(BF16)
            for j in range(NH // 2):
                g = j // 2
                o2, lse2 = _pair_fwd(q_ref[j, cur, :], kb, vb, base_rest if i > 0 else base_0,
                                     (SLOPES[2 * j], SLOPES[2 * j + 1]), (g, g), (sink_ref[2 * j], sink_ref[2 * j + 1]))
                o_ref[j, cur, :] = o2
                lse_ref[j, cur, :] = lse2

    before = lambda n: jnp.maximum(n * nq - 1, 0)
    slab = lambda g: pl.BlockSpec((None, rows, 128), lambda n: (g, n, 0))
    edge = lambda g: pl.BlockSpec((None, BLK, 128), lambda n: (g, before(n), 0))
    quad = pl.BlockSpec((4, rows, 128), lambda n: (0, n, 0))
    return pl.pallas_call(
        body, name="attn_a_fwd", grid=(steps,),
        in_specs=[SMEM, quad, edge(4), slab(4), edge(5), slab(5)],
        out_specs=[quad, quad],
        out_shape=[jax.ShapeDtypeStruct((4, s, 128), F32)] * 2,
        compiler_params=_cp(("parallel",)),
    )(sinks, proj, proj, proj, proj, proj)


def _attn_a_bwd(proj, sinks, d_o, o, lse):
    s = proj.shape[1]
    nq = A_BLOCKS_PER_STEP_BWD
    rows = BLK * nq
    steps = s // rows

    def body(sink_ref, q_ref, kp_ref, kc_ref, vp_ref, vc_ref, do_ref, o_ref, lse_ref,
             dq_ref, dk_ref, dv_ref, dsink_ref, kcar, vcar):
        n = pl.program_id(0)

        @pl.when(n == 0)
        def _():
            kcar[...] = jnp.zeros_like(kcar)
            vcar[...] = jnp.zeros_like(vcar)
            dsink_ref[...] = jnp.zeros_like(dsink_ref)

        dk_ref[...] = kcar[...]
        dv_ref[...] = vcar[...]

        @pl.when(n < steps)
        def _():
            base_rest = _band_base(A_MAX_DIST, 1, False)
            base_0 = jnp.where(n > 0, base_rest, _band_base(A_MAX_DIST, 1, True))
            for i in range(nq):
                cur = pl.ds(i * BLK, BLK)
                k_prev = kc_ref[pl.ds((i - 1) * BLK, BLK), :] if i > 0 else kp_ref[...]
                v_prev = vc_ref[pl.ds((i - 1) * BLK, BLK), :] if i > 0 else vp_ref[...]
                kb = jnp.concatenate([k_prev, kc_ref[cur, :]], axis=0).astype(BF16)
                vb = jnp.concatenate([v_prev, vc_ref[cur, :]], axis=0).astype(BF16)
                dk_win = dv_win = None
                for j in range(NH // 2):
                    g = j // 2
                    dq2, dk2, dv2, dsk = _pair_bwd(q_ref[j, cur, :], kb, vb, do_ref[j, cur, :], o_ref[j, cur, :],
                                                   lse_ref[j, cur, :], base_rest if i > 0 else base_0,
                                                   (SLOPES[2 * j], SLOPES[2 * j + 1]), (g, g),
                                                   (sink_ref[2 * j], sink_ref[2 * j + 1]))
                    dq_ref[j, cur, :] = dq2
                    dk_win = dk2 if j == 0 else dk_win + dk2
                    dv_win = dv2 if j == 0 else dv_win + dv2
                    for e in (0, 1):
                        h = 2 * j + e
                        dsink_ref[h:h + 1, :] += jnp.broadcast_to(dsk[e], (1, 128))
                if i == 0:
                    last = pl.ds((nq - 1) * BLK, BLK)
                    dk_ref[last, :] += dk_win[:BLK]
                    dv_ref[last, :] += dv_win[:BLK]
                else:
                    kcar[pl.ds((i - 1) * BLK, BLK), :] += dk_win[:BLK]
                    vcar[pl.ds((i - 1) * BLK, BLK), :] += dv_win[:BLK]
                kcar[cur, :] = dk_win[BLK:]
                vcar[cur, :] = dv_win[BLK:]

    cur_step = lambda n: jnp.minimum(n, steps - 1)
    before = lambda n: jnp.maximum(cur_step(n) * nq - 1, 0)
    out_prev = lambda n: jnp.maximum(n - 1, 0)
    quad = pl.BlockSpec((4, rows, 128), lambda n: (0, cur_step(n), 0))
    slab = lambda g: pl.BlockSpec((None, rows, 128), lambda n: (g, cur_step(n), 0))
    edge = lambda g: pl.BlockSpec((None, BLK, 128), lambda n: (g, before(n), 0))
    return pl.pallas_call(
        body, name="attn_a_bwd", grid=(steps + 1,),
        in_specs=[SMEM, quad, edge(4), slab(4), edge(5), slab(5), quad, quad, quad],
        out_specs=[quad,
                   pl.BlockSpec((rows, 128), lambda n: (out_prev(n), 0)),
                   pl.BlockSpec((rows, 128), lambda n: (out_prev(n), 0)),
                   pl.BlockSpec((NH, 128), lambda n: (0, 0))],
        out_shape=[jax.ShapeDtypeStruct((4, s, 128), F32), jax.ShapeDtypeStruct((s, 128), F32),
                   jax.ShapeDtypeStruct((s, 128), F32), jax.ShapeDtypeStruct((NH, 128), F32)],
        scratch_shapes=[pltpu.VMEM((rows, 128), F32), pltpu.VMEM((rows, 128), F32)],
        compiler_params=_cp(("arbitrary",)),
    )(sinks, proj, proj, proj, proj, proj, d_o, o, lse)


def _stream(rho, i, r):
    start = i * BLK * r + rho
    return pl.ds(start, BLK, stride=r) if r > 1 else pl.ds(start, BLK)


def _for_streams(r, fn, side_by_side=4):
    if r <= side_by_side:
        for rho in range(r):
            fn(rho)
    else:
        def group(it, carry):
            for u in range(side_by_side):
                fn(side_by_side * it + u)
            return carry

        lax.fori_loop(0, r // side_by_side, group, 0)


B_BLOCKS_PER_STEP = {1: 8, 4: 2, 16: 1}
B_BLOCKS_PER_STEP_FWD = {1: 8, 4: 2, 16: 1}


def _attn_b_fwd(proj, slopes, r):
    s = proj.shape[1]
    nq = B_BLOCKS_PER_STEP_FWD[r]
    rows = BLK * r * nq
    steps = s // rows
    qc, kc, vc = WA // 128, WA // 128 + 4, WA // 128 + 8

    def body(slope_ref, q_ref, kp_ref, kc_ref, vp_ref, vc_ref, o_ref, lse_ref):
        j = pl.program_id(0)
        sb = pl.program_id(1)
        sl2 = (slope_ref[2 * j], slope_ref[2 * j + 1])
        bias_rest = _stack_heads(sl2, _band_base(B_MAX_DIST, r, False))
        bias_0 = jnp.where(sb > 0, bias_rest, _stack_heads(sl2, _band_base(B_MAX_DIST, r, True)))

        def stream(rho):
            for i in range(nq):
                cur = _stream(rho, i, r)
                k_prev = kc_ref[_stream(rho, i - 1, r), :] if i > 0 else kp_ref[_stream(rho, 0, r), :]
                v_prev = vc_ref[_stream(rho, i - 1, r), :] if i > 0 else vp_ref[_stream(rho, 0, r), :]
                kb = jnp.concatenate([k_prev, kc_ref[cur, :]], axis=0).astype(BF16)
                vb = jnp.concatenate([v_prev, vc_ref[cur, :]], axis=0).astype(BF16)
                o2, lse2 = _pair_fwd(q_ref[cur, :], kb, vb, bias_rest if i > 0 else bias_0, None, (0, 1), None)
                o_ref[cur, :] = o2
                lse_ref[cur, :] = lse2

        _for_streams(r, stream, side_by_side=8)

    before = lambda sb: jnp.maximum(sb * nq - 1, 0)
    return pl.pallas_call(
        body, name=f"attn_b_fwd_r{r}", grid=(NH // 2, steps),
        in_specs=[SMEM,
                  pl.BlockSpec((None, rows, 128), lambda j, sb: (qc + j, sb, 0)),
                  pl.BlockSpec((None, BLK * r, 128), lambda j, sb: (kc + j, before(sb), 0)),
                  pl.BlockSpec((None, rows, 128), lambda j, sb: (kc + j, sb, 0)),
                  pl.BlockSpec((None, BLK * r, 128), lambda j, sb: (vc + j, before(sb), 0)),
                  pl.BlockSpec((None, rows, 128), lambda j, sb: (vc + j, sb, 0))],
        out_specs=[pl.BlockSpec((None, rows, 128), lambda j, sb: (j, sb, 0))] * 2,
        out_shape=[jax.ShapeDtypeStruct((4, s, 128), F32)] * 2,
        compiler_params=_cp(("parallel", "parallel")),
    )(slopes, proj, proj, proj, proj, proj)


def _attn_b_bwd(proj, slopes, d_o, o, lse, r, so_far=None):
    s = proj.shape[1]
    nq = B_BLOCKS_PER_STEP[r]
    rows = BLK * r * nq
    steps = s // rows
    qc, kc, vc = WA // 128, WA // 128 + 4, WA // 128 + 8
    chained = so_far is not None

    def body(slope_ref, q_ref, kp_ref, kc_ref, vp_ref, vc_ref, do_ref, o_ref, lse_ref, *rest):
        if chained:
            pq_ref, pk_ref, pv_ref, dq_ref, dk_ref, dv_ref, kcar, vcar = rest
        else:
            dq_ref, dk_ref, dv_ref, kcar, vcar = rest
        j = pl.program_id(0)
        sb = pl.program_id(1)

        @pl.when(sb == 0)
        def _():
            kcar[...] = jnp.zeros_like(kcar)
            vcar[...] = jnp.zeros_like(vcar)

        if chained:
            dk_ref[...] = kcar[...] + pk_ref[...]
            dv_ref[...] = vcar[...] + pv_ref[...]
        else:
            dk_ref[...] = kcar[...]
            dv_ref[...] = vcar[...]

        @pl.when(sb < steps)
        def _():
            sl2 = (slope_ref[2 * j], slope_ref[2 * j + 1])
            bias_rest = _stack_heads(sl2, _band_base(B_MAX_DIST, r, False))
            bias_0 = jnp.where(sb > 0, bias_rest, _stack_heads(sl2, _band_base(B_MAX_DIST, r, True)))

            def stream(rho):
                for i in range(nq):
                    cur = _stream(rho, i, r)
                    k_prev = kc_ref[_stream(rho, i - 1, r), :] if i > 0 else kp_ref[_stream(rho, 0, r), :]
                    v_prev = vc_ref[_stream(rho, i - 1, r), :] if i > 0 else vp_ref[_stream(rho, 0, r), :]
                    kb = jnp.concatenate([k_prev, kc_ref[cur, :]], axis=0).astype(BF16)
                    vb = jnp.concatenate([v_prev, vc_ref[cur, :]], axis=0).astype(BF16)
                    dq2, dk2, dv2, _ = _pair_bwd(q_ref[cur, :], kb, vb, do_ref[cur, :], o_ref[cur, :], lse_ref[cur, :],
                                                 bias_rest if i > 0 else bias_0, None, (0, 1), None)
                    dq_ref[cur, :] = dq2 + pq_ref[cur, :] if chained else dq2
                    if i == 0:
                        last = _stream(rho, nq - 1, r)
                        dk_ref[last, :] += dk2[:BLK]
                        dv_ref[last, :] += dv2[:BLK]
                    else:
                        kcar[_stream(rho, i - 1, r), :] += dk2[:BLK]
                        vcar[_stream(rho, i - 1, r), :] += dv2[:BLK]
                    kcar[cur, :] = dk2[BLK:]
                    vcar[cur, :] = dv2[BLK:]

            _for_streams(r, stream, side_by_side=8)

    cur_step = lambda sb: jnp.minimum(sb, steps - 1)
    before = lambda sb: jnp.maximum(cur_step(sb) * nq - 1, 0)
    out_prev = lambda sb: jnp.maximum(sb - 1, 0)
    tile = lambda slab: pl.BlockSpec((None, rows, 128), lambda j, sb: (slab + j, cur_step(sb), 0))
    edge = lambda slab: pl.BlockSpec((None, BLK * r, 128), lambda j, sb: (slab + j, before(sb), 0))
    late = pl.BlockSpec((None, rows, 128), lambda j, sb: (j, out_prev(sb), 0))
    grads = [tile(0), late, late]
    return pl.pallas_call(
        body, name=f"attn_b_bwd_r{r}", grid=(NH // 2, steps + 1),
        in_specs=[SMEM, tile(qc), edge(kc), tile(kc), edge(vc), tile(vc), tile(0), tile(0), tile(0)]
        + (grads if chained else []),
        out_specs=grads,
        out_shape=[jax.ShapeDtypeStruct((4, s, 128), F32)] * 3,
        scratch_shapes=[pltpu.VMEM((rows, 128), F32), pltpu.VMEM((rows, 128), F32)],
        compiler_params=_cp(("parallel", "arbitrary")),
    )(slopes, proj, proj, proj, proj, proj, d_o, o, lse, *(so_far if chained else ()))


def _row(v):
    return v.reshape(1, -1)


def _layer_norm_stats(z):
    mu = jnp.mean(z, axis=-1, keepdims=True)
    zc = z - mu
    var = jnp.mean(zc * zc, axis=-1, keepdims=True)
    rstd = lax.rsqrt(var + LN_EPS)
    return zc * rstd, rstd


def _layer_norm_bwd(dh, zh, rstd, g):
    dzh = dh * g
    return rstd * (dzh - jnp.mean(dzh, axis=-1, keepdims=True) - zh * jnp.mean(dzh * zh, axis=-1, keepdims=True))


def _rms(o):
    return lax.rsqrt(jnp.mean(o * o, axis=-1, keepdims=True) + RMS_EPS)


def _mix_ln1(x, o_a, o_b, lse_b, norm_a_g, norm_b_g, w_o, ln1_g, ln1_b, tm=256):
    s = x.shape[0]

    def wide(ref):
        return jnp.concatenate([ref[j] for j in range(4)], axis=1)

    def body(x_ref, oa_ref, ob1, ob2, ob3, l1, l2, l3, ga_ref, gb_ref, wo_ref, g_ref, b_ref,
             obm_ref, lse_ref, cat_ref, z1_ref, h1_ref, h1b_ref):
        la, lb, lc = wide(l1), wide(l2), wide(l3)
        m = jnp.maximum(jnp.maximum(la, lb), lc)
        ea, eb, ec = jnp.exp(la - m), jnp.exp(lb - m), jnp.exp(lc - m)
        den = ea + eb + ec
        obm = (ea / den) * wide(ob1) + (eb / den) * wide(ob2) + (ec / den) * wide(ob3)
        lse = m + jnp.log(den)
        for j in range(4):
            obm_ref[j] = obm[:, 128 * j:128 * (j + 1)]
            lse_ref[j] = lse[:, 128 * j:128 * (j + 1)]
        oa = wide(oa_ref)
        na = oa * _rms(oa) * ga_ref[...]
        nb_ = obm * _rms(obm) * gb_ref[...]
        cat = jnp.concatenate([na, nb_], axis=1).astype(BF16)
        cat_ref[...] = cat
        z1 = ALPHA * x_ref[...] + _nn(cat, wo_ref[...])
        z1_ref[...] = z1
        zh, _ = _layer_norm_stats(z1)
        h1 = zh * g_ref[...] + b_ref[...]
        h1_ref[...] = h1
        h1b_ref[...] = h1.astype(BF16)

    t512 = pl.BlockSpec((4, tm, 128), lambda i: (0, i, 0))
    td = pl.BlockSpec((tm, D), lambda i: (i, 0))
    return pl.pallas_call(
        body, name="mix_ln1", grid=(s // tm,),
        in_specs=[td] + [t512] * 7 + [_const((1, 512))] * 2 + [_resident((D, D))] + [_const((1, D))] * 2,
        out_specs=[t512, t512, td, td, td, td],
        out_shape=[jax.ShapeDtypeStruct((4, s, 128), F32), jax.ShapeDtypeStruct((4, s, 128), F32),
                   jax.ShapeDtypeStruct((s, D), BF16), jax.ShapeDtypeStruct((s, D), F32),
                   jax.ShapeDtypeStruct((s, D), F32), jax.ShapeDtypeStruct((s, D), BF16)],
        compiler_params=_cp(("parallel",)),
    )(x, o_a, *o_b, *lse_b, _row(norm_a_g), _row(norm_b_g), w_o, _row(ln1_g), _row(ln1_b))


def _gelu_and_grad(x):
    c = math.sqrt(2.0 / math.pi)
    x2 = x * x
    cx = c * x
    t = jnp.tanh(cx * (1.0 + 0.044715 * x2))
    q = 1.0 + t
    g = (0.5 * x) * q
    dg = 0.5 * q + ((0.5 * cx) * (1.0 - t * t)) * (1.0 + (3.0 * 0.044715) * x2)
    return g, dg


def _shift_down(u, before):
    n = u.shape[0]
    ext = jnp.concatenate([before, u], axis=0)
    return pltpu.roll(ext, 1, 0)[8:], pltpu.roll(ext, 2, 0)[8:]


def _shift_up(u, after):
    n = u.shape[0]
    ext = jnp.concatenate([u, after], axis=0)
    return pltpu.roll(ext, n + 7, 0)[:n], pltpu.roll(ext, n + 6, 0)[:n]


def _up_conv_gelu(h1b, w_up, cwb, tm=256, tn=FF // 2, chunk_rows=16, piece_cols=512):
    s = h1b.shape[0]
    n_i = s // tm
    n_t = (FF // tn) * n_i

    def body(h_ref, wg_ref, wv_ref, c_ref, up_ref, a_ref, g_ref, a1_ref, pend_a, pend_b, carry):
        t = pl.program_id(0)
        row_tile = jnp.maximum(t - 1, 0) % n_i
        w_refs = (wg_ref, wv_ref)

        @pl.when(t == 0)
        def _():
            pend_b[...] = jnp.zeros_like(pend_b)
            carry[...] = jnp.zeros_like(carry)

        def step(dst, src):
            def chunk(c, before):
                rows = pl.ds(c * chunk_rows, chunk_rows)
                u, last = [], []
                for half in (0, 1):
                    up = src[half, rows, :]
                    r1, r2 = _shift_down(up, before[half])
                    u.append(r2 * c_ref[0, half:half + 1, :] + r1 * c_ref[1, half:half + 1, :]
                             + up * c_ref[2, half:half + 1, :] + c_ref[3, half:half + 1, :])
                    last.append(up[chunk_rows - 8:])
                g, dg = _gelu_and_grad(u[0])
                a_ref[rows, :] = (g * u[1]).astype(BF16)
                g_ref[rows, :] = g.astype(BF16)
                a1_ref[rows, :] = (u[1] * dg).astype(BF16)
                return tuple(last)

            edge = tuple(jnp.where(row_tile > 0, carry[half], 0.0) for half in (0, 1))
            pieces = [(half, c0, min(piece_cols, tn - c0)) for half in (0, 1) for c0 in range(0, tn, piece_cols)]
            n_c = tm // chunk_rows
            done = 0
            for p, (half, c0, width) in enumerate(pieces):
                cols = slice(c0, c0 + width)
                up = _nn(h_ref[...], w_refs[half][:, cols])
                up_ref[half, :, cols] = up.astype(BF16)
                dst[half, :, cols] = up
                upto = n_c * (p + 1) // len(pieces)
                for c in range(done, upto):
                    edge = chunk(c, edge)
                done = upto
            for half in (0, 1):
                carry[half] = edge[half]

        @pl.when(t % 2 == 0)
        def _():
            step(pend_a, pend_b)

        @pl.when(t % 2 == 1)
        def _():
            step(pend_b, pend_a)

    mm = lambda t: jnp.minimum(t, n_t - 1)
    ew = lambda t: jnp.maximum(t - 1, 0)
    out_tile = pl.BlockSpec((tm, tn), lambda t: (ew(t) % n_i, ew(t) // n_i))
    return pl.pallas_call(
        body, name="up_conv_gelu", grid=(n_t + 1,),
        in_specs=[pl.BlockSpec((tm, D), lambda t: (mm(t) % n_i, 0)),
                  pl.BlockSpec((D, tn), lambda t: (0, mm(t) // n_i)),
                  pl.BlockSpec((D, tn), lambda t: (0, FF // tn + mm(t) // n_i)),
                  pl.BlockSpec((4, 2, tn), lambda t: (0, 0, ew(t) // n_i))],
        out_specs=[pl.BlockSpec((2, tm, tn), lambda t: (0, mm(t) % n_i, mm(t) // n_i)), out_tile, out_tile, out_tile],
        out_shape=[jax.ShapeDtypeStruct((2, s, FF), BF16)] + [jax.ShapeDtypeStruct((s, FF), BF16)] * 3,
        scratch_shapes=[pltpu.VMEM((2, tm, tn), F32), pltpu.VMEM((2, tm, tn), F32), pltpu.VMEM((2, 8, tn), F32)],
        compiler_params=_cp(("arbitrary",)),
    )(h1b, w_up, w_up, cwb)


def _down_ln2_loss(a, w_down, h1, target, ln2_g, ln2_b, tm=512):
    s = a.shape[0]

    def body(a_ref, w_ref, h_ref, t_ref, g_ref, b_ref, dz_ref, dzb_ref, st_ref):
        @pl.when(pl.program_id(0) == 0)
        def _():
            st_ref[...] = jnp.zeros_like(st_ref)

        z2 = ALPHA * h_ref[...] + _nn(a_ref[...], w_ref[...])
        zh, rstd = _layer_norm_stats(z2)
        diff = zh * g_ref[...] + b_ref[...] - t_ref[...]
        part = 0.5 * jnp.sum(jnp.mean(diff * diff, axis=-1, keepdims=True), axis=0, keepdims=True)
        dy = diff * (1.0 / D)
        st_ref[0:1, :] += jnp.sum(dy * zh, axis=0, keepdims=True)
        st_ref[1:2, :] += jnp.sum(dy, axis=0, keepdims=True)
        st_ref[2:3, :] += jnp.broadcast_to(part, (1, D))
        dz = _layer_norm_bwd(dy, zh, rstd, g_ref[...])
        dz_ref[...] = dz
        dzb_ref[...] = dz.astype(BF16)

    td = pl.BlockSpec((tm, D), lambda i: (i, 0))
    return pl.pallas_call(
        body, name="down_ln2_loss", grid=(s // tm,),
        in_specs=[pl.BlockSpec((tm, FF), lambda i: (i, 0)), _resident((FF, D)), td, td, _const((1, D)), _const((1, D))],
        out_specs=[td, td, _const((8, D))],
        out_shape=[jax.ShapeDtypeStruct((s, D), F32), jax.ShapeDtypeStruct((s, D), BF16),
                   jax.ShapeDtypeStruct((8, D), F32)],
        compiler_params=_cp(("arbitrary",)),
    )(a, w_down, h1, target, _row(ln2_g), _row(ln2_b))


def _d_act(dz2b, w_down, tm=512):
    s = dz2b.shape[0]

    def body(dz_ref, w_ref, o_ref):
        o_ref[...] = _nt(dz_ref[...], w_ref[...])

    return pl.pallas_call(
        body, name="d_act", grid=(s // tm,),
        in_specs=[pl.BlockSpec((tm, D), lambda i: (i, 0)), _resident((FF, D))],
        out_specs=pl.BlockSpec((tm, FF), lambda i: (i, 0)),
        out_shape=jax.ShapeDtypeStruct((s, FF), F32),
        compiler_params=_cp(("parallel",)),
    )(dz2b, w_down)


def _conv_gelu_bwd(da, up, g, a1, cwb, tm=256, tn=FF // 2, chunk_rows=16):
    s = da.shape[0]
    n_i = s // tm
    n_c = tm // chunk_rows

    def body(da_ref, up_ref, g_ref, a1_ref, c_ref, dup_ref, dc_ref, carry):
        @pl.when(pl.program_id(1) == 0)
        def _():
            carry[...] = jnp.zeros_like(carry)
            dc_ref[...] = jnp.zeros_like(dc_ref)

        def fold(v):
            return jnp.sum(v.reshape(chunk_rows // 8, 8, v.shape[1]), axis=0)

        def chunk(cc, state):
            after, sums = state
            rows = pl.ds((n_c - 1 - cc) * chunk_rows, chunk_rows)
            da_c = da_ref[rows, :]
            dus = (da_c * a1_ref[rows, :].astype(F32), da_c * g_ref[rows, :].astype(F32))
            head, new_sums = [], []
            for half in (0, 1):
                du = dus[half]
                up = up_ref[half, rows, :].astype(F32)
                l1, l2 = _shift_up(du, after[half])
                dup = (du * c_ref[2, half:half + 1, :] + l1 * c_ref[1, half:half + 1, :]
                       + l2 * c_ref[0, half:half + 1, :])
                dup_ref[half, rows, :] = dup.astype(BF16)
                parts = (fold(l2 * up), fold(l1 * up), fold(du * up), fold(du))
                new_sums.append(parts if sums is None else tuple(a + b for a, b in zip(sums[half], parts)))
                head.append(du[:8])
            return tuple(head), new_sums

        state = ((carry[0], carry[1]), None)
        for cc in range(n_c):
            state = chunk(cc, state)
        head, sums = state
        for half in (0, 1):
            carry[half] = head[half]
            for k in range(4):
                dc_ref[k, half:half + 1, :] += jnp.sum(sums[half][k], axis=0, keepdims=True)

    rev = lambda ii: n_i - 1 - ii
    tile = pl.BlockSpec((tm, tn), lambda j, ii: (rev(ii), j))
    pair = pl.BlockSpec((2, tm, tn), lambda j, ii: (0, rev(ii), j))
    per_col = pl.BlockSpec((4, 2, tn), lambda j, ii: (0, 0, j))
    return pl.pallas_call(
        body, name="conv_gelu_bwd", grid=(FF // tn, n_i),
        in_specs=[tile, pair, tile, tile, per_col],
        out_specs=[pair, per_col],
        out_shape=[jax.ShapeDtypeStruct((2, s, FF), BF16), jax.ShapeDtypeStruct((4, 2, FF), F32)],
        scratch_shapes=[pltpu.VMEM((2, 8, tn), F32)],
        compiler_params=_cp(("parallel", "arbitrary")),
    )(da, up, g, a1, cwb)


def _dh1_ln1_bwd(dz2, dup, w_up, z1, ln1_g, tm=512):
    s = dz2.shape[0]

    def body(dz2_ref, dup_ref, w_ref, z1_ref, g_ref, dz1_ref, dz1b_ref, st_ref):
        @pl.when(pl.program_id(0) == 0)
        def _():
            st_ref[...] = jnp.zeros_like(st_ref)

        dh = ALPHA * dz2_ref[...] + _nt(dup_ref[0], w_ref[:, :FF]) + _nt(dup_ref[1], w_ref[:, FF:])
        zh, rstd = _layer_norm_stats(z1_ref[...])
        st_ref[0:1, :] += jnp.sum(dh * zh, axis=0, keepdims=True)
        st_ref[1:2, :] += jnp.sum(dh, axis=0, keepdims=True)
        dz = _layer_norm_bwd(dh, zh, rstd, g_ref[...])
        dz1_ref[...] = dz
        dz1b_ref[...] = dz.astype(BF16)

    td = pl.BlockSpec((tm, D), lambda i: (i, 0))
    return pl.pallas_call(
        body, name="dh1_ln1_bwd", grid=(s // tm,),
        in_specs=[td, pl.BlockSpec((2, tm, FF), lambda i: (0, i, 0)), _resident((D, 2 * FF)), td, _const((1, D))],
        out_specs=[td, td, _const((8, D))],
        out_shape=[jax.ShapeDtypeStruct((s, D), F32), jax.ShapeDtypeStruct((s, D), BF16),
                   jax.ShapeDtypeStruct((8, D), F32)],
        compiler_params=_cp(("arbitrary",), 58),
    )(dz2, dup, w_up, z1, _row(ln1_g))


def _dcat_rms_bwd(dz1b, w_o, o_a, o_b, norm_a_g, norm_b_g, tm=512):
    s = dz1b.shape[0]

    def body(dz_ref, w_ref, oa_ref, ob_ref, ga_ref, gb_ref, da_ref, db_ref, st_ref):
        @pl.when(pl.program_id(0) == 0)
        def _():
            st_ref[...] = jnp.zeros_like(st_ref)

        dcat = _nt(dz_ref[...], w_ref[...])
        for k, (o_ref, g_ref, d_ref) in enumerate(((oa_ref, ga_ref, da_ref), (ob_ref, gb_ref, db_ref))):
            o = jnp.concatenate([o_ref[j] for j in range(4)], axis=1)
            dn = dcat[:, 512 * k:512 * (k + 1)]
            rr = _rms(o)
            oh = o * rr
            st_ref[k:k + 1, :] += jnp.sum(dn * oh, axis=0, keepdims=True)
            doh = dn * g_ref[...]
            d_o = rr * (doh - oh * jnp.mean(doh * oh, axis=-1, keepdims=True))
            for j in range(4):
                d_ref[j] = d_o[:, 128 * j:128 * (j + 1)]

    t512 = pl.BlockSpec((4, tm, 128), lambda i: (0, i, 0))
    return pl.pallas_call(
        body, name="dcat_rms_bwd", grid=(s // tm,),
        in_specs=[pl.BlockSpec((tm, D), lambda i: (i, 0)), _resident((D, D)), t512, t512,
                  _const((1, 512)), _const((1, 512))],
        out_specs=[t512, t512, _const((8, 512))],
        out_shape=[jax.ShapeDtypeStruct((4, s, 128), F32), jax.ShapeDtypeStruct((4, s, 128), F32),
                   jax.ShapeDtypeStruct((8, 512), F32)],
        compiler_params=_cp(("arbitrary",)),
    )(dz1b, w_o, o_a, o_b, _row(norm_a_g), _row(norm_b_g))


def _dproj_combine(dqa, dka, dva, dqkv_b, tm=256):
    s = dka.shape[0]

    def body(qa, ka, va, qb, kb, vb, o_ref):
        for j in range(4):
            o_ref[:, 128 * j:128 * (j + 1)] = qa[j].astype(BF16)
            o_ref[:, 768 + 128 * j:768 + 128 * (j + 1)] = qb[j].astype(BF16)
            o_ref[:, 1280 + 128 * j:1280 + 128 * (j + 1)] = kb[j].astype(BF16)
            o_ref[:, 1792 + 128 * j:1792 + 128 * (j + 1)] = vb[j].astype(BF16)
        o_ref[:, 512:640] = ka[...].astype(BF16)
        o_ref[:, 640:768] = va[...].astype(BF16)

    t512 = pl.BlockSpec((4, tm, 128), lambda i: (0, i, 0))
    t128 = pl.BlockSpec((tm, 128), lambda i: (i, 0))
    return pl.pallas_call(
        body, name="dproj_combine", grid=(s // tm,),
        in_specs=[t512, t128, t128] + [t512] * 3,
        out_specs=pl.BlockSpec((tm, WIN), lambda i: (i, 0)),
        out_shape=jax.ShapeDtypeStruct((s, WIN), BF16),
        compiler_params=_cp(("parallel",)),
    )(dqa, dka, dva, *dqkv_b)


def _grad_x(dz1, dproj, w_in_t, zero, tm=512):
    s = dz1.shape[0]

    def body(dz_ref, dp_ref, w_ref, z_ref, o_ref):
        o_ref[...] = ALPHA * dz_ref[...] + _nn(dp_ref[...], w_ref[...]) + z_ref[0:1, 0:1]

    td = pl.BlockSpec((tm, D), lambda i: (i, 0))
    return pl.pallas_call(
        body, name="grad_x", grid=(s // tm,),
        in_specs=[td, pl.BlockSpec((tm, WIN), lambda i: (i, 0)), _resident((WIN, D)), _const((8, 128))],
        out_specs=td, out_shape=jax.ShapeDtypeStruct((s, D), F32),
        compiler_params=_cp(("parallel",)),
    )(dz1, dproj, w_in_t, zero)


def _place():
    return lax.axis_index("x"), lax.axis_index("y"), lax.axis_index("c")


def _other_chips(x, y):
    return [(1 - x, y), (x, 1 - y), (1 - x, 1 - y)]


def _hbm(a):
    return pltpu.with_memory_space_constraint(a, pltpu.HBM)


def _gather_w_in(shard, conv_w):
    rows_k = shard.shape[0]
    half = rows_k // 2

    def body(src, conv_src, out, conv_out, send_sems, recv_sems):
        x, y, c = _place()
        b = 2 * x + y
        sibling = (x, y, 1 - c)
        chips = _other_chips(x, y)

        def copy(idx, chip_b, core, to, first_hop=False):
            rows = out.at[pl.ds(pl.multiple_of(chip_b * rows_k + core * half, 16), half)]
            s_ref = src.at[pl.ds(pl.multiple_of(core * half, 16), half)] if first_hop else rows
            return pltpu.make_async_remote_copy(src_ref=s_ref, dst_ref=rows, send_sem=send_sems.at[idx],
                                                recv_sem=recv_sems.at[idx], device_id=to, device_id_type=MESH)

        def own_copy():
            return pltpu.make_async_remote_copy(
                src_ref=src, dst_ref=out.at[pl.ds(pl.multiple_of(b * rows_k, 16), rows_k)], send_sem=send_sems.at[6],
                recv_sem=recv_sems.at[6], device_id=sibling, device_id_type=MESH)

        def conv_copy(idx, chip_b, to):
            return pltpu.make_async_remote_copy(src_ref=conv_src, dst_ref=conv_out.at[chip_b],
                                                send_sem=send_sems.at[7 + idx], recv_sem=recv_sems.at[7 + idx],
                                                device_id=to, device_id_type=MESH)

        started = [own_copy(), conv_copy(3, b, sibling)]
        for jn, chip in enumerate(chips):
            started += [copy(jn, b, c, (chip[0], chip[1], c), first_hop=True), conv_copy(jn, b, (chip[0], chip[1], c))]
        for cp in started:
            cp.start()
        for jn, chip in enumerate(chips):
            cb = 2 * chip[0] + chip[1]
            copy(jn, cb, c, (chip[0], chip[1], c)).wait_recv()
            cp = copy(3 + jn, cb, c, sibling)
            cp.start()
            started.append(cp)
        for jn, chip in enumerate(chips):
            cb = 2 * chip[0] + chip[1]
            copy(3 + jn, cb, 1 - c, sibling).wait_recv()
            conv_copy(jn, cb, (chip[0], chip[1], c)).wait_recv()
        own_copy().wait_recv()
        conv_copy(3, b, sibling).wait_recv()
        for cp in started:
            cp.wait_send()

    return pl.pallas_call(
        body, name="gather_w_in",
        in_specs=[ANY, ANY], out_specs=[ANY, ANY],
        out_shape=[jax.ShapeDtypeStruct((N_CHIPS * rows_k, D), BF16), jax.ShapeDtypeStruct((N_CHIPS,) + conv_w.shape, F32)],
        scratch_shapes=[pltpu.SemaphoreType.DMA((11,)), pltpu.SemaphoreType.DMA((11,))],
        compiler_params=pltpu.CompilerParams(has_side_effects=True),
    )(shard, conv_w)


def _weight_copies(shard, land, send_sems, recv_sems, arrivals):
    x, y, c = _place()
    n_rows, n_cols = shard.shape
    peers = [(px, py, c) for px, py in _other_chips(x, y)] + [(x, y, 1 - c)]
    cps = []
    for jn, peer in enumerate(peers):
        at = 2 * peer[0] + peer[1] if arrivals else 2 * x + y
        if land.shape[1] == n_cols:
            dst = land.at[pl.ds(pl.multiple_of(at * n_rows, 16), n_rows)]
        else:
            dst = land.at[:, pl.ds(pl.multiple_of(at * n_cols, 128), n_cols)]
        cps.append(pltpu.make_async_remote_copy(src_ref=shard, dst_ref=dst, send_sem=send_sems.at[jn],
                                                recv_sem=recv_sems.at[jn], device_id=peer, device_id_type=MESH))
    return cps


def _weights_start(shards, after):
    n = len(shards)
    lands = [lax.empty((N_CHIPS * sh.shape[0], D) if sh.shape[1] == D else (D, N_CHIPS * sh.shape[1]), BF16)
             for sh in shards]

    def body(*refs):
        src, land = refs[:n], refs[n:2 * n]
        send_sems, recv_sems = refs[2 * n + 1:3 * n + 1], refs[3 * n + 1:4 * n + 1]
        for k in range(n):
            for send in _weight_copies(src[k], land[k], send_sems[k], recv_sems[k], False):
                send.start()
        refs[-1][...] = jnp.zeros_like(refs[-1])

    res = pl.pallas_call(
        body, name="weights_start",
        in_specs=[HBM] * (2 * n) + [ANY], out_specs=[SEM] * (2 * n) + [HBM] * (2 * n) + [VMEM],
        out_shape=[pltpu.SemaphoreType.DMA((4,))] * (2 * n)
        + [pltpu.HBM(a.shape, a.dtype) for a in (*shards, *lands)] + [jax.ShapeDtypeStruct((8, 128), F32)],
        input_output_aliases={i: i + 2 * n for i in range(2 * n)},
        compiler_params=pltpu.CompilerParams(has_side_effects=DATAFLOW),
    )(*[_hbm(a) for a in (*shards, *lands)], after)
    return [(res[k], res[n + k], res[2 * n + k], res[3 * n + k]) for k in range(n)], res[-1]


def _weights_wait(started, after, name):
    send_sems, recv_sems, shard, land = started

    def body(s_ref, l_ref, send_ref, recv_ref, after_ref, s_out, l_out):
        for cp in _weight_copies(s_ref, l_ref, send_ref, recv_ref, True):
            cp.wait_send()
            cp.wait_recv()

    return pl.pallas_call(
        body, name=name,
        in_specs=[HBM, HBM, SEM, SEM, ANY], out_specs=[HBM, HBM],
        out_shape=[pltpu.HBM(shard.shape, shard.dtype), pltpu.HBM(land.shape, land.dtype)],
        input_output_aliases={0: 0, 1: 1},
        compiler_params=pltpu.CompilerParams(has_side_effects=DATAFLOW),
    )(shard, land, send_sems, recv_sems, after)[1]


def _grad_copies(g_ref, land_ref, send_sems, recv_sems):
    x, y, c = _place()
    cps = []
    for d in range(1, 8):
        px, py, pc = x ^ (d >> 2), y ^ ((d >> 1) & 1), c ^ (d & 1)
        cps.append(pltpu.make_async_remote_copy(
            src_ref=g_ref.at[2 * px + py, pc], dst_ref=land_ref.at[d - 1], send_sem=send_sems.at[d - 1],
            recv_sem=recv_sems.at[d - 1], device_id=(px, py, pc), device_id_type=MESH))
    return cps


def _grads_start(grads_b, name):
    n = len(grads_b)
    lands = [lax.empty((7, g.shape[2], D), BF16) for g in grads_b]

    def body(*refs):
        g, land = refs[:n], refs[n:2 * n]
        send_sems, recv_sems = refs[2 * n:3 * n], refs[3 * n:4 * n]
        for k in range(n):
            for cp in _grad_copies(g[k], land[k], send_sems[k], recv_sems[k]):
                cp.start()
        refs[-1][...] = jnp.zeros_like(refs[-1])

    res = pl.pallas_call(
        body, name=name,
        in_specs=[HBM] * (2 * n), out_specs=[SEM] * (2 * n) + [HBM] * (2 * n) + [VMEM],
        out_shape=[pltpu.SemaphoreType.DMA((7,))] * (2 * n)
        + [pltpu.HBM(a.shape, a.dtype) for a in (*grads_b, *lands)] + [jax.ShapeDtypeStruct((8, 128), F32)],
        input_output_aliases={i: i + 2 * n for i in range(2 * n)},
        compiler_params=pltpu.CompilerParams(has_side_effects=DATAFLOW),
    )(*[_hbm(a) for a in (*grads_b, *lands)])
    return [(res[k], res[n + k], res[2 * n + k], res[3 * n + k]) for k in range(n)], res[-1]


def _grads_wait(started, after, name):
    n = len(started)

    def body(*refs):
        g, land = refs[:n], refs[n:2 * n]
        send_sems, recv_sems = refs[2 * n:3 * n], refs[3 * n:4 * n]
        for k in range(n):
            for cp in _grad_copies(g[k], land[k], send_sems[k], recv_sems[k]):
                cp.wait_send()
                cp.wait_recv()

    gs = [st[2] for st in started]
    lands = [st[3] for st in started]
    res = pl.pallas_call(
        body, name=name,
        in_specs=[HBM] * (2 * n) + [SEM] * (2 * n) + [ANY], out_specs=[HBM] * (2 * n),
        out_shape=[pltpu.HBM(a.shape, a.dtype) for a in (*gs, *lands)],
        input_output_aliases={i: i for i in range(2 * n)},
        compiler_params=pltpu.CompilerParams(has_side_effects=DATAFLOW),
    )(*gs, *lands, *[st[0] for st in started], *[st[1] for st in started], after)
    return res[n:]


def _sum_partials(grad4, got, cb, name, tr):
    h = grad4.shape[2]
    per_half = h // tr

    def body(cb_ref, g_ref, o_ref, out_ref):
        acc = g_ref[...]
        for j in range(7):
            acc = acc + o_ref[j].astype(F32)
        out_ref[...] = acc

    return pl.pallas_call(
        body, name=name,
        grid_spec=pltpu.PrefetchScalarGridSpec(
            num_scalar_prefetch=1, grid=(per_half,),
            in_specs=[pl.BlockSpec((None, None, tr, D), lambda i, cb_ref: (cb_ref[1], cb_ref[0], i, 0)),
                      pl.BlockSpec((7, tr, D), lambda i, cb_ref: (0, i, 0))],
            out_specs=pl.BlockSpec((tr, D), lambda i, cb_ref: (cb_ref[0] * per_half + i, 0))),
        out_shape=jax.ShapeDtypeStruct((2 * h, D), F32),
        compiler_params=_cp(("arbitrary",)),
    )(cb, grad4, got)


def _swap_halves(shards, name):
    n = len(shards)

    def body(*refs):
        out, send_sems, recv_sems = refs[n:2 * n], refs[2 * n], refs[2 * n + 1]
        x, y, c = _place()
        cps = []
        for k in range(n):
            h = shards[k].shape[0] // 2
            mine = out[k].at[pl.ds(pl.multiple_of(c * h, 8), h)]
            cp = pltpu.make_async_remote_copy(src_ref=mine, dst_ref=mine, send_sem=send_sems.at[k],
                                              recv_sem=recv_sems.at[k], device_id=(x, y, 1 - c), device_id_type=MESH)
            cp.start()
            cps.append(cp)
        for cp in cps:
            cp.wait()

    return pl.pallas_call(
        body, name=name,
        in_specs=[ANY] * n, out_specs=[ANY] * n,
        out_shape=[jax.ShapeDtypeStruct(sh.shape, F32) for sh in shards],
        input_output_aliases={k: k for k in range(n)},
        scratch_shapes=[pltpu.SemaphoreType.DMA((n,)), pltpu.SemaphoreType.DMA((n,))],
        compiler_params=pltpu.CompilerParams(has_side_effects=True),
    )(*shards)


def _share_halves(shards, small):
    n = len(shards)
    rows = small.shape[0]

    def body(*refs):
        small_ref = refs[n]
        out, total_ref = refs[n + 1:2 * n + 1], refs[2 * n + 1]
        all_ref, send_sems, recv_sems, ssend, srecv = refs[2 * n + 2:]
        x, y, c = _place()
        me = 4 * x + 2 * y + c
        cps = []
        for k in range(n):
            h = shards[k].shape[0] // 2
            mine = out[k].at[pl.ds(pl.multiple_of(c * h, 8), h)]
            cp = pltpu.make_async_remote_copy(src_ref=mine, dst_ref=mine, send_sem=send_sems.at[k],
                                              recv_sem=recv_sems.at[k], device_id=(x, y, 1 - c), device_id_type=MESH)
            cp.start()
            cps.append(cp)
        all_ref[me] = small_ref[...]
        peers = []
        for d in range(1, 8):
            px, py, pc = x ^ (d >> 2), y ^ ((d >> 1) & 1), c ^ (d & 1)
            cp = pltpu.make_async_remote_copy(src_ref=small_ref, dst_ref=all_ref.at[me],
                                              send_sem=ssend.at[d - 1], recv_sem=srecv.at[d - 1],
                                              device_id=(px, py, pc), device_id_type=MESH)
            cp.start()
            peers.append(cp)
        for cp in peers:
            cp.wait()
        acc = all_ref[0]
        for d in range(1, 8):
            acc = acc + all_ref[d]
        total_ref[...] = acc
        for cp in cps:
            cp.wait()

    return pl.pallas_call(
        body, name="share_halves",
        in_specs=[ANY] * n + [VMEM], out_specs=[ANY] * n + [VMEM],
        out_shape=[jax.ShapeDtypeStruct(sh.shape, F32) for sh in shards] + [jax.ShapeDtypeStruct((rows, D), F32)],
        input_output_aliases={k: k for k in range(n)},
        scratch_shapes=[pltpu.VMEM((8, rows, D), F32), pltpu.SemaphoreType.DMA((n,)), pltpu.SemaphoreType.DMA((n,)),
                        pltpu.SemaphoreType.DMA((7,)), pltpu.SemaphoreType.DMA((7,))],
        compiler_params=pltpu.CompilerParams(has_side_effects=True),
    )(*shards, small)


def _adamw(w, g, m, v, name, tr):
    rows, cols = w.shape

    def body(w_ref, g_ref, m_ref, v_ref, d_ref, nm_ref, nv_ref):
        g_ = g_ref[...]
        nm = ADAM_B1 * m_ref[...] + (1.0 - ADAM_B1) * g_
        nv = ADAM_B2 * v_ref[...] + (1.0 - ADAM_B2) * (g_ * g_)
        m_hat = nm / (1.0 - ADAM_B1 ** ADAM_STEP)
        v_hat = nv / (1.0 - ADAM_B2 ** ADAM_STEP)
        d_ref[...] = -ADAM_LR * (m_hat / (jnp.sqrt(v_hat) + ADAM_EPS) + ADAM_WD * w_ref[...])
        nm_ref[...] = nm
        nv_ref[...] = nv

    spec = pl.BlockSpec((tr, cols), lambda i: (i, 0))
    return pl.pallas_call(
        body, name=name, grid=(rows // tr,),
        in_specs=[spec] * 4, out_specs=[spec] * 3,
        out_shape=[jax.ShapeDtypeStruct((rows, cols), F32)] * 3,
        compiler_params=_cp(("parallel",)),
    )(w, g, m, v)


def _local_step(x, target, w_in_t, late_weights, norm_a_g, norm_b_g, sinks_a, ln1_g, ln1_b,
                conv_w, conv_b, ln2_g, ln2_b, slopes, on_grad):
    cwb = jnp.concatenate([conv_w, conv_b[None]], axis=0).reshape(4, 2, FF)

    proj, xb = _proj(x, w_in_t, "proj")
    o_a, lse_a = _attn_a_fwd(proj, sinks_a)
    fwd_b = [_attn_b_fwd(proj, slopes, r) for r in B_DILATIONS]
    w_o = late_weights(1, fwd_b[-1][1])
    o_b, lse_b, cat, z1, h1, h1b = _mix_ln1(x, o_a, [f[0] for f in fwd_b], [f[1] for f in fwd_b],
                                           norm_a_g, norm_b_g, w_o, ln1_g, ln1_b)
    w_up = late_weights(2, h1b)
    up, a, gate, a1 = _up_conv_gelu(h1b, w_up, cwb)
    w_down = late_weights(3, a)
    dz2, dz2b, st2 = _down_ln2_loss(a, w_down, h1, target, ln2_g, ln2_b)

    on_grad(3, *_grad_w(a, dz2b, "grad_w_down", tm=FF // 2))
    dup, dconv = _conv_gelu_bwd(_d_act(dz2b, w_down), up, gate, a1, cwb)
    on_grad(2, *_grad_w(dup, h1b, "grad_w_up", tm=FF // 2, lhs_halves=True))
    dz1, dz1b, st1 = _dh1_ln1_bwd(dz2, dup, w_up, z1, ln1_g)
    tok = on_grad(1, *_grad_w(cat, dz1b, "grad_w_o", tm=512))
    d_oa, d_ob, st_n = _dcat_rms_bwd(dz1b, w_o, o_a, o_b, norm_a_g + tok[0, 0], norm_b_g)
    dqa, dka, dva, dsink = _attn_a_bwd(proj, sinks_a, d_oa, o_a, lse_a)
    bwd_b = None
    for r in B_DILATIONS:
        bwd_b = _attn_b_bwd(proj, slopes, d_ob, o_b, lse_b, r, bwd_b)
    dproj = _dproj_combine(dqa, dka, dva, bwd_b)
    tok = on_grad(0, *_grad_w(dproj, xb, "grad_w_in", tm=WA))
    gx = _grad_x(dz1, dproj, w_in_t, tok)

    dconv = dconv.reshape(4, 2 * FF)
    small = dict(loss=st2[2, 0:1], norm_a_g=st_n[0], norm_b_g=st_n[1], sinks_a=dsink[:, 0],
                 ln1_g=st1[0], ln1_b=st1[1], conv_w=dconv[0:3].reshape(-1), conv_b=dconv[3],
                 ln2_g=st2[0], ln2_b=st2[1])
    return gx, small


SMALL_ORDER = ("loss", "norm_a_g", "norm_b_g", "sinks_a", "ln1_g", "ln1_b", "conv_b", "ln2_g", "ln2_b", "conv_w")
SMALL_SIZES = dict(loss=1, norm_a_g=512, norm_b_g=512, sinks_a=8, ln1_g=D, ln1_b=D, conv_b=2 * FF, ln2_g=D, ln2_b=D,
                   conv_w=3 * 2 * FF)


def _pack(parts, rows):
    flat = jnp.concatenate([parts[k].reshape(-1).astype(F32) for k in parts])
    return jnp.pad(flat, (0, rows * D - flat.shape[0])).reshape(rows, D)


def _unpack(buf, names, sizes):
    flat = buf.reshape(-1)
    out, at = {}, 0
    for k in names:
        out[k] = flat[at:at + sizes[k]]
        at += sizes[k]
    return out


def kernel(x, w_in, norm_a_g, norm_b_g, sinks_a, w_o, ln1_g, ln1_b, w_up, conv_w, conv_b, w_down, ln2_g, ln2_b, loss_target, m_w_in, m_norm_a_g, m_norm_b_g, m_sinks_a, m_w_o, m_ln1_g, m_ln1_b, m_w_up, m_conv_w, m_conv_b, m_w_down, m_ln2_g, m_ln2_b, v_w_in, v_norm_a_g, v_norm_b_g, v_sinks_a, v_w_o, v_ln1_g, v_ln1_b, v_w_up, v_conv_w, v_conv_b, v_w_down, v_ln2_g, v_ln2_b):
    xi, yi, ci = _place()
    chip = (2 * xi + yi).astype(I32)
    core = ci.astype(I32)

    w_in_rows, m_w_in_rows, v_w_in_rows = w_in.T, m_w_in.T, v_w_in.T
    shards = (w_in_rows.astype(BF16), w_o.astype(BF16), w_up.astype(BF16), w_down.astype(BF16))
    w_in_t, conv_w4 = _gather_w_in(shards[0], conv_w)
    conv_w_f = conv_w4.transpose(1, 0, 2).reshape(3, 2 * FF)
    w_started, w_tok = _weights_start(shards[1:], conv_w4)
    slopes = jnp.asarray(SLOPES, F32) + w_tok[0, 0]

    halves_rows = [r // 2 for r in SHARD_ROWS]
    grads4, grads_b4, started = [None] * 4, [None] * 4, [None] * 4

    def on_grad(k, g, g_b):
        grads4[k] = g.reshape(N_CHIPS, 2, halves_rows[k], D)
        grads_b4[k] = g_b.reshape(N_CHIPS, 2, halves_rows[k], D)
        if k > 1:
            return None
        group = (1, 2, 3) if k == 1 else (0,)
        sts, tok = _grads_start([grads_b4[i] for i in group], f"grads_start_{k}")
        for i, st in zip(group, sts):
            started[i] = st
        return tok

    gx, small = _local_step(
        x[0], loss_target[0], w_in_t, lambda k, after: _weights_wait(w_started[k - 1], after, f"weights_wait_{k}"),
        norm_a_g, norm_b_g, sinks_a, ln1_g, ln1_b, conv_w_f, conv_b, ln2_g, ln2_b, slopes, on_grad)

    tiles = (96, 128, 352, 176)
    core_chip = jnp.stack([core, chip])
    got = _grads_wait(started[1:], gx, "grads_wait_1")
    halves = [_sum_partials(grads4[k], got[k - 1], core_chip, f"sum_partials_{k}", tiles[k]) for k in (1, 2, 3)]
    g_w_o, g_w_up_rows, g_w_down = _swap_halves(halves, "swap_halves")
    g_w_up = g_w_up_rows.T
    delta, new_m, new_v = {}, {}, {}
    for k, g, tr in (("w_o", g_w_o, 128), ("w_up", g_w_up, 256), ("w_down", g_w_down, 176)):
        delta[k], new_m[k], new_v[k] = _adamw(dict(w_o=w_o, w_up=w_up, w_down=w_down)[k], g,
                                              dict(w_o=m_w_o, w_up=m_w_up, w_down=m_w_down)[k],
                                              dict(w_o=v_w_o, w_up=v_w_up, w_down=v_w_down)[k], f"adamw_{k}", tr)

    got = _grads_wait(started[:1], delta["w_up"], "grads_wait_0")
    half_in = _sum_partials(grads4[0], got[0], core_chip, "sum_partials_0", tiles[0])
    small_rows = 32
    g_w_in_rows, totals = _share_halves([half_in], _pack({k: small[k] for k in SMALL_ORDER}, small_rows))
    tot = _unpack(totals, SMALL_ORDER, SMALL_SIZES)
    loss = tot["loss"][0]
    cols = 2 * FF // N_CHIPS
    g_conv_w = lax.dynamic_slice(tot["conv_w"].reshape(3, 2 * FF), (0, chip * cols), (3, cols))
    g_small = dict(norm_a_g=tot["norm_a_g"], norm_b_g=tot["norm_b_g"], sinks_a=tot["sinks_a"], ln1_g=tot["ln1_g"],
                   ln1_b=tot["ln1_b"], conv_w=g_conv_w, conv_b=tot["conv_b"], ln2_g=tot["ln2_g"], ln2_b=tot["ln2_b"])

    weights = dict(w_in=w_in, norm_a_g=norm_a_g, norm_b_g=norm_b_g, sinks_a=sinks_a, w_o=w_o, ln1_g=ln1_g, ln1_b=ln1_b,
                   w_up=w_up, conv_w=conv_w, conv_b=conv_b, w_down=w_down, ln2_g=ln2_g, ln2_b=ln2_b)
    ms = dict(w_in=m_w_in, norm_a_g=m_norm_a_g, norm_b_g=m_norm_b_g, sinks_a=m_sinks_a, w_o=m_w_o, ln1_g=m_ln1_g,
              ln1_b=m_ln1_b, w_up=m_w_up, conv_w=m_conv_w, conv_b=m_conv_b, w_down=m_w_down, ln2_g=m_ln2_g, ln2_b=m_ln2_b)
    vs = dict(w_in=v_w_in, norm_a_g=v_norm_a_g, norm_b_g=v_norm_b_g, sinks_a=v_sinks_a, w_o=v_w_o, ln1_g=v_ln1_g,
              ln1_b=v_ln1_b, w_up=v_w_up, conv_w=v_conv_w, conv_b=v_conv_b, w_down=v_w_down, ln2_g=v_ln2_g, ln2_b=v_ln2_b)
    order = list(weights)
    grad = dict(g_small, w_in=g_w_in_rows.T, w_o=g_w_o, w_up=g_w_up, w_down=g_w_down)

    delta["w_in"], new_m["w_in"], new_v["w_in"] = [
        a.T for a in _adamw(w_in_rows, g_w_in_rows, m_w_in_rows, v_w_in_rows, "adamw_w_in", 144)]
    small_names = [k for k in order if k not in delta]
    sizes = {k: weights[k].size for k in small_names}
    rows = 16
    packed = [_pack({k: src[k] for k in small_names}, rows) for src in (weights, grad, ms, vs)]
    for res, buf in zip((delta, new_m, new_v), _adamw(*packed, "adamw_small", rows)):
        for k, val in _unpack(buf, small_names, sizes).items():
            res[k] = val.reshape(weights[k].shape)

    return (loss, gx[None], *[grad[k] for k in order], *[delta[k] for k in order],
            *[new_m[k] for k in order], *[new_v[k] for k in order])
```

```python
import functools
import math

import jax
import jax.numpy as jnp
from jax import lax
from jax.experimental import pallas as pl
from jax.experimental.pallas import tpu as pltpu

F32, BF16, I32 = jnp.float32, jnp.bfloat16, jnp.int32

D = 1024
FF = 2816
HD = 64
NH = 8
WA, WB = 768, 1536
WIN = WA + WB
BLK = 128
ALPHA = 2.0 ** 0.25
LN_EPS, RMS_EPS = 1e-5, 1e-6
SCALE = 1.0 / math.sqrt(HD)
A_MAX_DIST, B_MAX_DIST = 127, 128
B_DILATIONS = (1, 4, 16)
SLOPES = tuple(2.0 ** (-(i + 1)) for i in range(NH))
SHARD_ROWS = (WIN // 4, D // 4, 2 * FF // 4, FF // 4)
N_CHIPS = 4
ADAM_LR, ADAM_B1, ADAM_B2, ADAM_EPS, ADAM_WD, ADAM_STEP = 0.001, 0.9, 0.999, 1e-08, 0.01, 10
MESH = pl.DeviceIdType.MESH
ANY = pl.BlockSpec(memory_space=pl.ANY)
SMEM = pl.BlockSpec(memory_space=pltpu.SMEM)
VMEM = pl.BlockSpec(memory_space=pltpu.VMEM)
HBM = pl.BlockSpec(memory_space=pltpu.HBM)
SEM = pl.BlockSpec(memory_space=pltpu.SEMAPHORE)
DATAFLOW = pltpu.SideEffectType.DATAFLOW_SIDE_EFFECTING


def _cp(sem, mb=48):
    return pltpu.CompilerParams(dimension_semantics=sem, vmem_limit_bytes=mb << 20)


def _nn(a, b):
    return lax.dot_general(a, b, (((1,), (0,)), ((), ())), preferred_element_type=F32)


def _nt(a, b):
    return lax.dot_general(a, b, (((1,), (1,)), ((), ())), preferred_element_type=F32)


def _tn(a, b):
    return lax.dot_general(a, b, (((0,), (0,)), ((), ())), preferred_element_type=F32)


def _resident(shape):
    n = len(shape)
    return pl.BlockSpec(shape, lambda *_: (0,) * n, pipeline_mode=pl.Buffered(1))


def _const(shape):
    n = len(shape)
    return pl.BlockSpec(shape, lambda *_: (0,) * n)


def _proj(x, w_t, name, tm=512):
    s = x.shape[0]
    n = w_t.shape[0]

    def body(x_ref, w_ref, o_ref, xb_ref):
        xb = x_ref[...].astype(BF16)
        xb_ref[...] = xb
        res = _nt(xb, w_ref[...])
        for g in range(n // 128):
            o_ref[g] = res[:, 128 * g:128 * (g + 1)]

    return pl.pallas_call(
        body, name=name, grid=(s // tm,),
        in_specs=[pl.BlockSpec((tm, D), lambda i: (i, 0)), _resident((n, D))],
        out_specs=[pl.BlockSpec((n // 128, tm, 128), lambda i: (0, i, 0)), pl.BlockSpec((tm, D), lambda i: (i, 0))],
        out_shape=[jax.ShapeDtypeStruct((n // 128, s, 128), F32), jax.ShapeDtypeStruct((s, D), BF16)],
        compiler_params=_cp(("parallel",)),
    )(x, w_t)


def _grad_w(lhs, rhs, name, tm, tk=2048, lhs_halves=False):
    s = rhs.shape[0]
    if lhs_halves:
        per_half = lhs.shape[2] // tm
        n = 2 * lhs.shape[2]
        lhs_spec = pl.BlockSpec((None, tk, tm), lambda i, k: (i // per_half, k, i % per_half))
    else:
        n = lhs.shape[1]
        lhs_spec = pl.BlockSpec((tk, tm), lambda i, k: (k, i))
    nk = s // tk

    def body(l_ref, r_ref, o_ref, ob_ref):
        k = pl.program_id(1)

        @pl.when(k == 0)
        def _():
            o_ref[...] = jnp.zeros_like(o_ref)

        o_ref[...] += _tn(l_ref[...], r_ref[...])

        @pl.when(k == nk - 1)
        def _():
            ob_ref[...] = o_ref[...].astype(BF16)

    return pl.pallas_call(
        body, name=name, grid=(n // tm, nk),
        in_specs=[lhs_spec, pl.BlockSpec((tk, D), lambda i, k: (k, 0))],
        out_specs=[pl.BlockSpec((tm, D), lambda i, k: (i, 0))] * 2,
        out_shape=[jax.ShapeDtypeStruct((n, D), F32), jax.ShapeDtypeStruct((n, D), BF16)],
        compiler_params=_cp(("parallel", "arbitrary")),
    )(lhs, rhs)


def _band_base(max_dist, dist_unit, first):
    row = lax.broadcasted_iota(I32, (BLK, 2 * BLK), 0)
    col = lax.broadcasted_iota(I32, (BLK, 2 * BLK), 1)
    dist = BLK + row - col
    ok = (dist >= 0) & (dist <= max_dist)
    if first:
        ok = ok & (col >= BLK)
    return jnp.where(ok, dist.astype(F32) * (-float(dist_unit)), -jnp.inf)


def _half_mask(shape, e):
    lane = lax.broadcasted_iota(I32, shape, 1)
    return (lane < HD) if e == 0 else (lane >= HD)


def _to_half(x, e, g):
    if g != e:
        x = pltpu.roll(x, HD, 1)
    return jnp.where(_half_mask(x.shape, g), x, 0.0)


def _stack_heads(scalars, tile):
    return jnp.concatenate([scalars[0] * tile, scalars[1] * tile], axis=0)


def _pair_fwd(q2, kb, vb, base, slopes, kv_heads, sinks):
    lo = _half_mask((BLK, 2 * HD), 0)
    if sinks is not None:
        o2 = lse2 = None
        for e in (0, 1):
            g = kv_heads[e]
            qv = (_to_half(q2, e, g) * SCALE).astype(BF16)
            s = _nt(qv, kb) + slopes[e] * base
            m = jnp.maximum(jnp.max(s, axis=1, keepdims=True), sinks[e])
            p = jnp.exp(s - m)
            l = jnp.sum(p, axis=1, keepdims=True) + jnp.exp(sinks[e] - m)
            oh = _nn(p.astype(BF16), vb) / l
            if g != e:
                oh = pltpu.roll(oh, HD, 1)
            lse = jnp.broadcast_to(m + jnp.log(l), (BLK, 2 * HD))
            o2 = oh if e == 0 else jnp.where(lo, o2, oh)
            lse2 = lse if e == 0 else jnp.where(lo, lse2, lse)
        return o2, lse2
    qs = jnp.concatenate([_to_half(q2, e, kv_heads[e]) * SCALE for e in (0, 1)], axis=0).astype(BF16)
    s = _nt(qs, kb) + (base if slopes is None else _stack_heads(slopes, base))
    m = jnp.max(s, axis=1, keepdims=True)
    p = jnp.exp(s - m)
    l = jnp.sum(p, axis=1, keepdims=True)
    o = _nn(p.astype(BF16), vb) / l
    lse = m + jnp.log(l)
    halves = []
    for e in (0, 1):
        oh = o[e * BLK:(e + 1) * BLK]
        halves.append(pltpu.roll(oh, HD, 1) if kv_heads[e] != e else oh)
    o2 = jnp.where(lo, halves[0], halves[1])
    lse2 = jnp.where(lo, jnp.broadcast_to(lse[:BLK], (BLK, 2 * HD)), jnp.broadcast_to(lse[BLK:], (BLK, 2 * HD)))
    return o2, lse2


def _pair_bwd(q2, kb, vb, do2, o2, lse2, base, slopes, kv_heads, sinks):
    lo = _half_mask((BLK, 2 * HD), 0)
    prod = do2 * o2
    lses, deltas = [], []
    for e in (0, 1):
        hq = _half_mask((BLK, 2 * HD), e)
        lses.append(jnp.max(jnp.where(hq, lse2, -jnp.inf), axis=1, keepdims=True))
        deltas.append(jnp.sum(jnp.where(hq, prod, 0.0), axis=1, keepdims=True))
    lse = jnp.concatenate(lses, axis=0)
    delta = jnp.concatenate(deltas, axis=0)
    qs = jnp.concatenate([_to_half(q2, e, kv_heads[e]) * SCALE for e in (0, 1)], axis=0).astype(BF16)
    dos = jnp.concatenate([_to_half(do2, e, kv_heads[e]) for e in (0, 1)], axis=0).astype(BF16)
    p = jnp.exp(_nt(qs, kb) + (base if slopes is None else _stack_heads(slopes, base)) - lse)
    ds = (p * (_nt(dos, vb) - delta)).astype(BF16)
    dq = _nn(ds, kb) * SCALE
    halves = []
    for e in (0, 1):
        dqh = dq[e * BLK:(e + 1) * BLK]
        halves.append(pltpu.roll(dqh, HD, 1) if kv_heads[e] != e else dqh)
    dq2 = jnp.where(lo, halves[0], halves[1])
    dk2 = _tn(ds, qs)
    dv2 = _tn(p.astype(BF16), dos)
    dsinks = []
    if sinks is not None:
        for e in (0, 1):
            dsinks.append(jnp.sum(-jnp.exp(sinks[e] - lses[e]) * deltas[e], axis=0, keepdims=True))
    return dq2, dk2, dv2, dsinks


A_BLOCKS_PER_STEP = 2
A_BLOCKS_PER_STEP_BWD = 1


def _attn_a_fwd(proj, sinks):
    s = proj.shape[1]
    nq = A_BLOCKS_PER_STEP
    rows = BLK * nq
    steps = s // rows

    def body(sink_ref, q_ref, kp_ref, kc_ref, vp_ref, vc_ref, o_ref, lse_ref):
        n = pl.program_id(0)
        base_rest = _band_base(A_MAX_DIST, 1, False)
        base_0 = jnp.where(n > 0, base_rest, _band_base(A_MAX_DIST, 1, True))
        for i in range(nq):
            cur = pl.ds(i * BLK, BLK)
            k_prev = kc_ref[pl.ds((i - 1) * BLK, BLK), :] if i > 0 else kp_ref[...]
            v_prev = vc_ref[pl.ds((i - 1) * BLK, BLK), :] if i > 0 else vp_ref[...]
            kb = jnp.concatenate([k_prev, kc_ref[cur, :]], axis=0).astype(BF16)
            vb = jnp.concatenate([v_prev, vc_ref[cur, :]], axis=0).astype(BF16)
            for j in range(NH // 2):
                g = j // 2
                o2, lse2 = _pair_fwd(q_ref[j, cur, :], kb, vb, base_rest if i > 0 else base_0,
                                     (SLOPES[2 * j], SLOPES[2 * j + 1]), (g, g), (sink_ref[2 * j], sink_ref[2 * j + 1]))
                o_ref[j, cur, :] = o2
                lse_ref[j, cur, :] = lse2

    before = lambda n: jnp.maximum(n * nq - 1, 0)
    slab = lambda g: pl.BlockSpec((None, rows, 128), lambda n: (g, n, 0))
    edge = lambda g: pl.BlockSpec((None, BLK, 128), lambda n: (g, before(n), 0))
    quad = pl.BlockSpec((4, rows, 128), lambda n: (0, n, 0))
    return pl.pallas_call(
        body, name="attn_a_fwd", grid=(steps,),
        in_specs=[SMEM, quad, edge(4), slab(4), edge(5), slab(5)],
        out_specs=[quad, quad],
        out_shape=[jax.ShapeDtypeStruct((4, s, 128), F32)] * 2,
        compiler_params=_cp(("parallel",)),
    )(sinks, proj, proj, proj, proj, proj)


def _attn_a_bwd(proj, sinks, d_o, o, lse):
    s = proj.shape[1]
    nq = A_BLOCKS_PER_STEP_BWD
    rows = BLK * nq
    steps = s // rows

    def body(sink_ref, q_ref, kp_ref, kc_ref, vp_ref, vc_ref, do_ref, o_ref, lse_ref,
             dq_ref, dk_ref, dv_ref, dsink_ref, kcar, vcar):
        n = pl.program_id(0)

        @pl.when(n == 0)
        def _():
            kcar[...] = jnp.zeros_like(kcar)
            vcar[...] = jnp.zeros_like(vcar)
            dsink_ref[...] = jnp.zeros_like(dsink_ref)

        dk_ref[...] = kcar[...]
        dv_ref[...] = vcar[...]

        @pl.when(n < steps)
        def _():
            base_rest = _band_base(A_MAX_DIST, 1, False)
            base_0 = jnp.where(n > 0, base_rest, _band_base(A_MAX_DIST, 1, True))
            for i in range(nq):
                cur = pl.ds(i * BLK, BLK)
                k_prev = kc_ref[pl.ds((i - 1) * BLK, BLK), :] if i > 0 else kp_ref[...]
                v_prev = vc_ref[pl.ds((i - 1) * BLK, BLK), :] if i > 0 else vp_ref[...]
                kb = jnp.concatenate([k_prev, kc_ref[cur, :]], axis=0).astype(BF16)
                vb = jnp.concatenate([v_prev, vc_ref[cur, :]], axis=0).astype(BF16)
                dk_win = dv_win = None
                for j in range(NH // 2):
                    g = j // 2
                    dq2, dk2, dv2, dsk = _pair_bwd(q_ref[j, cur, :], kb, vb, do_ref[j, cur, :], o_ref[j, cur, :],
                                                   lse_ref[j, cur, :], base_rest if i > 0 else base_0,
                                                   (SLOPES[2 * j], SLOPES[2 * j + 1]), (g, g),
                                                   (sink_ref[2 * j], sink_ref[2 * j + 1]))
                    dq_ref[j, cur, :] = dq2
                    dk_win = dk2 if j == 0 else dk_win + dk2
                    dv_win = dv2 if j == 0 else dv_win + dv2
                    for e in (0, 1):
                        h = 2 * j + e
                        dsink_ref[h:h + 1, :] += jnp.broadcast_to(dsk[e], (1, 128))
                if i == 0:
                    last = pl.ds((nq - 1) * BLK, BLK)
                    dk_ref[last, :] += dk_win[:BLK]
                    dv_ref[last, :] += dv_win[:BLK]
                else:
                    kcar[pl.ds((i - 1) * BLK, BLK), :] += dk_win[:BLK]
                    vcar[pl.ds((i - 1) * BLK, BLK), :] += dv_win[:BLK]
                kcar[cur, :] = dk_win[BLK:]
                vcar[cur, :] = dv_win[BLK:]

    cur_step = lambda n: jnp.minimum(n, steps - 1)
    before = lambda n: jnp.maximum(cur_step(n) * nq - 1, 0)
    out_prev = lambda n: jnp.maximum(n - 1, 0)
    quad = pl.BlockSpec((4, rows, 128), lambda n: (0, cur_step(n), 0))
    slab = lambda g: pl.BlockSpec((None, rows, 128), lambda n: (g, cur_step(n), 0))
    edge = lambda g: pl.BlockSpec((None, BLK, 128), lambda n: (g, before(n), 0))
    return pl.pallas_call(
        body, name="attn_a_bwd", grid=(steps + 1,),
        in_specs=[SMEM, quad, edge(4), slab(4), edge(5), slab(5), quad, quad, quad],
        out_specs=[quad,
                   pl.BlockSpec((rows, 128), lambda n: (out_prev(n), 0)),
                   pl.BlockSpec((rows, 128), lambda n: (out_prev(n), 0)),
                   pl.BlockSpec((NH, 128), lambda n: (0, 0))],
        out_shape=[jax.ShapeDtypeStruct((4, s, 128), F32), jax.ShapeDtypeStruct((s, 128), F32),
                   jax.ShapeDtypeStruct((s, 128), F32), jax.ShapeDtypeStruct((NH, 128), F32)],
        scratch_shapes=[pltpu.VMEM((rows, 128), F32), pltpu.VMEM((rows, 128), F32)],
        compiler_params=_cp(("arbitrary",)),
    )(sinks, proj, proj, proj, proj, proj, d_o, o, lse)


def _stream(rho, i, r):
    start = i * BLK * r + rho
    return pl.ds(start, BLK, stride=r) if r > 1 else pl.ds(start, BLK)


def _for_streams(r, fn, side_by_side=4):
    if r <= side_by_side:
        for rho in range(r):
            fn(rho)
    else:
        def group(it, carry):
            for u in range(side_by_side):
                fn(side_by_side * it + u)
            return carry

        lax.fori_loop(0, r // side_by_side, group, 0)


B_BLOCKS_PER_STEP = {1: 8, 4: 2, 16: 1}
B_BLOCKS_PER_STEP_FWD = {1: 8, 4: 2, 16: 1}


def _attn_b_fwd(proj, slopes, r):
    s = proj.shape[1]
    nq = B_BLOCKS_PER_STEP_FWD[r]
    rows = BLK * r * nq
    steps = s // rows
    qc, kc, vc = WA // 128, WA // 128 + 4, WA // 128 + 8

    def body(slope_ref, q_ref, kp_ref, kc_ref, vp_ref, vc_ref, o_ref, lse_ref):
        j = pl.program_id(0)
        sb = pl.program_id(1)
        sl2 = (slope_ref[2 * j], slope_ref[2 * j + 1])
        bias_rest = _stack_heads(sl2, _band_base(B_MAX_DIST, r, False))
        bias_0 = jnp.where(sb > 0, bias_rest, _stack_heads(sl2, _band_base(B_MAX_DIST, r, True)))

        def stream(rho):
            for i in range(nq):
                cur = _stream(rho, i, r)
                k_prev = kc_ref[_stream(rho, i - 1, r), :] if i > 0 else kp_ref[_stream(rho, 0, r), :]
                v_prev = vc_ref[_stream(rho, i - 1, r), :] if i > 0 else vp_ref[_stream(rho, 0, r), :]
                kb = jnp.concatenate([k_prev, kc_ref[cur, :]], axis=0).astype(BF16)
                vb = jnp.concatenate([v_prev, vc_ref[cur, :]], axis=0).astype(BF16)
                o2, lse2 = _pair_fwd(q_ref[cur, :], kb, vb, bias_rest if i > 0 else bias_0, None, (0, 1), None)
                o_ref[cur, :] = o2
                lse_ref[cur, :] = lse2

        _for_streams(r, stream, side_by_side=8)

    before = lambda sb: jnp.maximum(sb * nq - 1, 0)
    return pl.pallas_call(
        body, name=f"attn_b_fwd_r{r}", grid=(NH // 2, steps),
        in_specs=[SMEM,
                  pl.BlockSpec((None, rows, 128), lambda j, sb: (qc + j, sb, 0)),
                  pl.BlockSpec((None, BLK * r, 128), lambda j, sb: (kc + j, before(sb), 0)),
                  pl.BlockSpec((None, rows, 128), lambda j, sb: (kc + j, sb, 0)),
                  pl.BlockSpec((None, BLK * r, 128), lambda j, sb: (vc + j, before(sb), 0)),
                  pl.BlockSpec((None, rows, 128), lambda j, sb: (vc + j, sb, 0))],
        out_specs=[pl.BlockSpec((None, rows, 128), lambda j, sb: (j, sb, 0))] * 2,
        out_shape=[jax.ShapeDtypeStruct((4, s, 128), F32)] * 2,
        compiler_params=_cp(("parallel", "parallel")),
    )(slopes, proj, proj, proj, proj, proj)


def _attn_b_bwd(proj, slopes, d_o, o, lse, r, so_far=None):
    s = proj.shape[1]
    nq = B_BLOCKS_PER_STEP[r]
    rows = BLK * r * nq
    steps = s // rows
    qc, kc, vc = WA // 128, WA // 128 + 4, WA // 128 + 8
    chained = so_far is not None

    def body(slope_ref, q_ref, kp_ref, kc_ref, vp_ref, vc_ref, do_ref, o_ref, lse_ref, *rest):
        if chained:
            pq_ref, pk_ref, pv_ref, dq_ref, dk_ref, dv_ref, kcar, vcar = rest
        else:
            dq_ref, dk_ref, dv_ref, kcar, vcar = rest
        j = pl.program_id(0)
        sb = pl.program_id(1)

        @pl.when(sb == 0)
        def _():
            kcar[...] = jnp.zeros_like(kcar)
            vcar[...] = jnp.zeros_like(vcar)

        if chained:
            dk_ref[...] = kcar[...] + pk_ref[...]
            dv_ref[...] = vcar[...] + pv_ref[...]
        else:
            dk_ref[...] = kcar[...]
            dv_ref[...] = vcar[...]

        @pl.when(sb < steps)
        def _():
            sl2 = (slope_ref[2 * j], slope_ref[2 * j + 1])
            bias_rest = _stack_heads(sl2, _band_base(B_MAX_DIST, r, False))
            bias_0 = jnp.where(sb > 0, bias_rest, _stack_heads(sl2, _band_base(B_MAX_DIST, r, True)))

            def stream(rho):
                for i in range(nq):
                    cur = _stream(rho, i, r)
                    k_prev = kc_ref[_stream(rho, i - 1, r), :] if i > 0 else kp_ref[_stream(rho, 0, r), :]
                    v_prev = vc_ref[_stream(rho, i - 1, r), :] if i > 0 else vp_ref[_stream(rho, 0, r), :]
                    kb = jnp.concatenate([k_prev, kc_ref[cur, :]], axis=0).astype(BF16)
                    vb = jnp.concatenate([v_prev, vc_ref[cur, :]], axis=0).astype(BF16)
                    dq2, dk2, dv2, _ = _pair_bwd(q_ref[cur, :], kb, vb, do_ref[cur, :], o_ref[cur, :], lse_ref[cur, :],
                                                 bias_rest if i > 0 else bias_0, None, (0, 1), None)
                    dq_ref[cur, :] = dq2 + pq_ref[cur, :] if chained else dq2
                    if i == 0:
                        last = _stream(rho, nq - 1, r)
                        dk_ref[last, :] += dk2[:BLK]
                        dv_ref[last, :] += dv2[:BLK]
                    else:
                        kcar[_stream(rho, i - 1, r), :] += dk2[:BLK]
                        vcar[_stream(rho, i - 1, r), :] += dv2[:BLK]
                    kcar[cur, :] = dk2[BLK:]
                    vcar[cur, :] = dv2[BLK:]

            _for_streams(r, stream, side_by_side=8)

    cur_step = lambda sb: jnp.minimum(sb, steps - 1)
    before = lambda sb: jnp.maximum(cur_step(sb) * nq - 1, 0)
    out_prev = lambda sb: jnp.maximum(sb - 1, 0)
    tile = lambda slab: pl.BlockSpec((None, rows, 128), lambda j, sb: (slab + j, cur_step(sb), 0))
    edge = lambda slab: pl.BlockSpec((None, BLK * r, 128), lambda j, sb: (slab + j, before(sb), 0))
    late = pl.BlockSpec((None, rows, 128), lambda j, sb: (j, out_prev(sb), 0))
    grads = [tile(0), late, late]
    return pl.pallas_call(
        body, name=f"attn_b_bwd_r{r}", grid=(NH // 2, steps + 1),
        in_specs=[SMEM, tile(qc), edge(kc), tile(kc), edge(vc), tile(vc), tile(0), tile(0), tile(0)]
        + (grads if chained else []),
        out_specs=grads,
        out_shape=[jax.ShapeDtypeStruct((4, s, 128), F32)] * 3,
        scratch_shapes=[pltpu.VMEM((rows, 128), F32), pltpu.VMEM((rows, 128), F32)],
        compiler_params=_cp(("parallel", "arbitrary")),
    )(slopes, proj, proj, proj, proj, proj, d_o, o, lse, *(so_far if chained else ()))


def _row(v):
    return v.reshape(1, -1)


def _layer_norm_stats(z):
    mu = jnp.mean(z, axis=-1, keepdims=True)
    zc = z - mu
    var = jnp.mean(zc * zc, axis=-1, keepdims=True)
    rstd = lax.rsqrt(var + LN_EPS)
    return zc * rstd, rstd


def _layer_norm_bwd(dh, zh, rstd, g):
    dzh = dh * g
    return rstd * (dzh - jnp.mean(dzh, axis=-1, keepdims=True) - zh * jnp.mean(dzh * zh, axis=-1, keepdims=True))


def _rms(o):
    return lax.rsqrt(jnp.mean(o * o, axis=-1, keepdims=True) + RMS_EPS)


def _mix_ln1(x, o_a, o_b, lse_b, norm_a_g, norm_b_g, w_o, ln1_g, ln1_b, tm=256):
    s = x.shape[0]

    def wide(ref):
        return jnp.concatenate([ref[j] for j in range(4)], axis=1)

    def body(x_ref, oa_ref, ob1, ob2, ob3, l1, l2, l3, ga_ref, gb_ref, wo_ref, g_ref, b_ref,
             obm_ref, lse_ref, cat_ref, z1_ref, h1_ref, h1b_ref):
        la, lb, lc = wide(l1), wide(l2), wide(l3)
        m = jnp.maximum(jnp.maximum(la, lb), lc)
        ea, eb, ec = jnp.exp(la - m), jnp.exp(lb - m), jnp.exp(lc - m)
        den = ea + eb + ec
        obm = (ea / den) * wide(ob1) + (eb / den) * wide(ob2) + (ec / den) * wide(ob3)
        lse = m + jnp.log(den)
        for j in range(4):
            obm_ref[j] = obm[:, 128 * j:128 * (j + 1)]
            lse_ref[j] = lse[:, 128 * j:128 * (j + 1)]
        oa = wide(oa_ref)
        na = oa * _rms(oa) * ga_ref[...]
        nb_ = obm * _rms(obm) * gb_ref[...]
        cat = jnp.concatenate([na, nb_], axis=1).astype(BF16)
        cat_ref[...] = cat
        z1 = ALPHA * x_ref[...] + _nn(cat, wo_ref[...])
        z1_ref[...] = z1
        zh, _ = _layer_norm_stats(z1)
        h1 = zh * g_ref[...] + b_ref[...]
        h1_ref[...] = h1
        h1b_ref[...] = h1.astype(BF16)

    t512 = pl.BlockSpec((4, tm, 128), lambda i: (0, i, 0))
    td = pl.BlockSpec((tm, D), lambda i: (i, 0))
    return pl.pallas_call(
        body, name="mix_ln1", grid=(s // tm,),
        in_specs=[td] + [t512] * 7 + [_const((1, 512))] * 2 + [_resident((D, D))] + [_const((1, D))] * 2,
        out_specs=[t512, t512, td, td, td, td],
        out_shape=[jax.ShapeDtypeStruct((4, s, 128), F32), jax.ShapeDtypeStruct((4, s, 128), F32),
                   jax.ShapeDtypeStruct((s, D), BF16), jax.ShapeDtypeStruct((s, D), F32),
                   jax.ShapeDtypeStruct((s, D), F32), jax.ShapeDtypeStruct((s, D), BF16)],
        compiler_params=_cp(("parallel",)),
    )(x, o_a, *o_b, *lse_b, _row(norm_a_g), _row(norm_b_g), w_o, _row(ln1_g), _row(ln1_b))


def _gelu_and_grad(x):
    c = math.sqrt(2.0 / math.pi)
    x2 = x * x
    cx = c * x
    t = jnp.tanh(cx * (1.0 + 0.044715 * x2))
    q = 1.0 + t
    g = (0.5 * x) * q
    dg = 0.5 * q + ((0.5 * cx) * (1.0 - t * t)) * (1.0 + (3.0 * 0.044715) * x2)
    return g, dg


def _shift_down(u, before):
    n = u.shape[0]
    ext = jnp.concatenate([before, u], axis=0)
    return pltpu.roll(ext, 1, 0)[8:], pltpu.roll(ext, 2, 0)[8:]


def _shift_up(u, after):
    n = u.shape[0]
    ext = jnp.concatenate([u, after], axis=0)
    return pltpu.roll(ext, n + 7, 0)[:n], pltpu.roll(ext, n + 6, 0)[:n]


def _up_conv_gelu(h1b, w_up, cwb, tm=256, tn=FF // 2, chunk_rows=16, piece_cols=512):
    s = h1b.shape[0]
    n_i = s // tm
    n_t = (FF // tn) * n_i

    def body(h_ref, wg_ref, wv_ref, c_ref, up_ref, a_ref, g_ref, a1_ref, pend_a, pend_b, carry):
        t = pl.program_id(0)
        row_tile = jnp.maximum(t - 1, 0) % n_i
        w_refs = (wg_ref, wv_ref)

        @pl.when(t == 0)
        def _():
            pend_b[...] = jnp.zeros_like(pend_b)
            carry[...] = jnp.zeros_like(carry)

        def step(dst, src):
            def chunk(c, before):
                rows = pl.ds(c * chunk_rows, chunk_rows)
                u, last = [], []
                for half in (0, 1):
                    up = src[half, rows, :]
                    r1, r2 = _shift_down(up, before[half])
                    u.append(r2 * c_ref[0, half:half + 1, :] + r1 * c_ref[1, half:half + 1, :]
                             + up * c_ref[2, half:half + 1, :] + c_ref[3, half:half + 1, :])
                    last.append(up[chunk_rows - 8:])
                g, dg = _gelu_and_grad(u[0])
                a_ref[rows, :] = (g * u[1]).astype(BF16)
                g_ref[rows, :] = g.astype(BF16)
                a1_ref[rows, :] = (u[1] * dg).astype(BF16)
                return tuple(last)

            edge = tuple(jnp.where(row_tile > 0, carry[half], 0.0) for half in (0, 1))
            pieces = [(half, c0, min(piece_cols, tn - c0)) for half in (0, 1) for c0 in range(0, tn, piece_cols)]
            n_c = tm // chunk_rows
            done = 0
            for p, (half, c0, width) in enumerate(pieces):
                cols = slice(c0, c0 + width)
                up = _nn(h_ref[...], w_refs[half][:, cols])
                up_ref[half, :, cols] = up.astype(BF16)
                dst[half, :, cols] = up
                upto = n_c * (p + 1) // len(pieces)
                for c in range(done, upto):
                    edge = chunk(c, edge)
                done = upto
            for half in (0, 1):
                carry[half] = edge[half]

        @pl.when(t % 2 == 0)
        def _():
            step(pend_a, pend_b)

        @pl.when(t % 2 == 1)
        def _():
            step(pend_b, pend_a)

    mm = lambda t: jnp.minimum(t, n_t - 1)
    ew = lambda t: jnp.maximum(t - 1, 0)
    out_tile = pl.BlockSpec((tm, tn), lambda t: (ew(t) % n_i, ew(t) // n_i))
    return pl.pallas_call(
        body, name="up_conv_gelu", grid=(n_t + 1,),
        in_specs=[pl.BlockSpec((tm, D), lambda t: (mm(t) % n_i, 0)),
                  pl.BlockSpec((D, tn), lambda t: (0, mm(t) // n_i)),
                  pl.BlockSpec((D, tn), lambda t: (0, FF // tn + mm(t) // n_i)),
                  pl.BlockSpec((4, 2, tn), lambda t: (0, 0, ew(t) // n_i))],
        out_specs=[pl.BlockSpec((2, tm, tn), lambda t: (0, mm(t) % n_i, mm(t) // n_i)), out_tile, out_tile, out_tile],
        out_shape=[jax.ShapeDtypeStruct((2, s, FF), BF16)] + [jax.ShapeDtypeStruct((s, FF), BF16)] * 3,
        scratch_shapes=[pltpu.VMEM((2, tm, tn), F32), pltpu.VMEM((2, tm, tn), F32), pltpu.VMEM((2, 8, tn), F32)],
        compiler_params=_cp(("arbitrary",)),
    )(h1b, w_up, w_up, cwb)


def _down_ln2_loss(a, w_down, h1, target, ln2_g, ln2_b, tm=512):
    s = a.shape[0]

    def body(a_ref, w_ref, h_ref, t_ref, g_ref, b_ref, dz_ref, dzb_ref, st_ref):
        @pl.when(pl.program_id(0) == 0)
        def _():
            st_ref[...] = jnp.zeros_like(st_ref)

        z2 = ALPHA * h_ref[...] + _nn(a_ref[...], w_ref[...])
        zh, rstd = _layer_norm_stats(z2)
        diff = zh * g_ref[...] + b_ref[...] - t_ref[...]
        part = 0.5 * jnp.sum(jnp.mean(diff * diff, axis=-1, keepdims=True), axis=0, keepdims=True)
        dy = diff * (1.0 / D)
        st_ref[0:1, :] += jnp.sum(dy * zh, axis=0, keepdims=True)
        st_ref[1:2, :] += jnp.sum(dy, axis=0, keepdims=True)
        st_ref[2:3, :] += jnp.broadcast_to(part, (1, D))
        dz = _layer_norm_bwd(dy, zh, rstd, g_ref[...])
        dz_ref[...] = dz
        dzb_ref[...] = dz.astype(BF16)

    td = pl.BlockSpec((tm, D), lambda i: (i, 0))
    return pl.pallas_call(
        body, name="down_ln2_loss", grid=(s // tm,),
        in_specs=[pl.BlockSpec((tm, FF), lambda i: (i, 0)), _resident((FF, D)), td, td, _const((1, D)), _const((1, D))],
        out_specs=[td, td, _const((8, D))],
        out_shape=[jax.ShapeDtypeStruct((s, D), F32), jax.ShapeDtypeStruct((s, D), BF16),
                   jax.ShapeDtypeStruct((8, D), F32)],
        compiler_params=_cp(("arbitrary",)),
    )(a, w_down, h1, target, _row(ln2_g), _row(ln2_b))


def _d_act(dz2b, w_down, tm=512):
    s = dz2b.shape[0]

    def body(dz_ref, w_ref, o_ref):
        o_ref[...] = _nt(dz_ref[...], w_ref[...])

    return pl.pallas_call(
        body, name="d_act", grid=(s // tm,),
        in_specs=[pl.BlockSpec((tm, D), lambda i: (i, 0)), _resident((FF, D))],
        out_specs=pl.BlockSpec((tm, FF), lambda i: (i, 0)),
        out_shape=jax.ShapeDtypeStruct((s, FF), F32),
        compiler_params=_cp(("parallel",)),
    )(dz2b, w_down)


def _conv_gelu_bwd(da, up, g, a1, cwb, tm=256, tn=FF // 2, chunk_rows=16):
    s = da.shape[0]
    n_i = s // tm
    n_c = tm // chunk_rows

    def body(da_ref, up_ref, g_ref, a1_ref, c_ref, dup_ref, dc_ref, carry):
        @pl.when(pl.program_id(1) == 0)
        def _():
            carry[...] = jnp.zeros_like(carry)
            dc_ref[...] = jnp.zeros_like(dc_ref)

        def fold(v):
            return jnp.sum(v.reshape(chunk_rows // 8, 8, v.shape[1]), axis=0)

        def chunk(cc, state):
            after, sums = state
            rows = pl.ds((n_c - 1 - cc) * chunk_rows, chunk_rows)
            da_c = da_ref[rows, :]
            dus = (da_c * a1_ref[rows, :].astype(F32), da_c * g_ref[rows, :].astype(F32))
            head, new_sums = [], []
            for half in (0, 1):
                du = dus[half]
                up = up_ref[half, rows, :].astype(F32)
                l1, l2 = _shift_up(du, after[half])
                dup = (du * c_ref[2, half:half + 1, :] + l1 * c_ref[1, half:half + 1, :]
                       + l2 * c_ref[0, half:half + 1, :])
                dup_ref[half, rows, :] = dup.astype(BF16)
                parts = (fold(l2 * up), fold(l1 * up), fold(du * up), fold(du))
                new_sums.append(parts if sums is None else tuple(a + b for a, b in zip(sums[half], parts)))
                head.append(du[:8])
            return tuple(head), new_sums

        state = ((carry[0], carry[1]), None)
        for cc in range(n_c):
            state = chunk(cc, state)
        head, sums = state
        for half in (0, 1):
            carry[half] = head[half]
            for k in range(4):
                dc_ref[k, half:half + 1, :] += jnp.sum(sums[half][k], axis=0, keepdims=True)

    rev = lambda ii: n_i - 1 - ii
    tile = pl.BlockSpec((tm, tn), lambda j, ii: (rev(ii), j))
    pair = pl.BlockSpec((2, tm, tn), lambda j, ii: (0, rev(ii), j))
    per_col = pl.BlockSpec((4, 2, tn), lambda j, ii: (0, 0, j))
    return pl.pallas_call(
        body, name="conv_gelu_bwd", grid=(FF // tn, n_i),
        in_specs=[tile, pair, tile, tile, per_col],
        out_specs=[pair, per_col],
        out_shape=[jax.ShapeDtypeStruct((2, s, FF), BF16), jax.ShapeDtypeStruct((4, 2, FF), F32)],
        scratch_shapes=[pltpu.VMEM((2, 8, tn), F32)],
        compiler_params=_cp(("parallel", "arbitrary")),
    )(da, up, g, a1, cwb)


def _dh1_ln1_bwd(dz2, dup, w_up, z1, ln1_g, tm=512):
    s = dz2.shape[0]

    def body(dz2_ref, dup_ref, w_ref, z1_ref, g_ref, dz1_ref, dz1b_ref, st_ref):
        @pl.when(pl.program_id(0) == 0)
        def _():
            st_ref[...] = jnp.zeros_like(st_ref)

        dh = ALPHA * dz2_ref[...] + _nt(dup_ref[0], w_ref[:, :FF]) + _nt(dup_ref[1], w_ref[:, FF:])
        zh, rstd = _layer_norm_stats(z1_ref[...])
        st_ref[0:1, :] += jnp.sum(dh * zh, axis=0, keepdims=True)
        st_ref[1:2, :] += jnp.sum(dh, axis=0, keepdims=True)
        dz = _layer_norm_bwd(dh, zh, rstd, g_ref[...])
        dz1_ref[...] = dz
        dz1b_ref[...] = dz.astype(BF16)

    td = pl.BlockSpec((tm, D), lambda i: (i, 0))
    return pl.pallas_call(
        body, name="dh1_ln1_bwd", grid=(s // tm,),
        in_specs=[td, pl.BlockSpec((2, tm, FF), lambda i: (0, i, 0)), _resident((D, 2 * FF)), td, _const((1, D))],
        out_specs=[td, td, _const((8, D))],
        out_shape=[jax.ShapeDtypeStruct((s, D), F32), jax.ShapeDtypeStruct((s, D), BF16),
                   jax.ShapeDtypeStruct((8, D), F32)],
        compiler_params=_cp(("arbitrary",), 58),
    )(dz2, dup, w_up, z1, _row(ln1_g))


def _dcat_rms_bwd(dz1b, w_o, o_a, o_b, norm_a_g, norm_b_g, tm=512):
    s = dz1b.shape[0]

    def body(dz_ref, w_ref, oa_ref, ob_ref, ga_ref, gb_ref, da_ref, db_ref, st_ref):
        @pl.when(pl.program_id(0) == 0)
        def _():
            st_ref[...] = jnp.zeros_like(st_ref)

        dcat = _nt(dz_ref[...], w_ref[...])
        for k, (o_ref, g_ref, d_ref) in enumerate(((oa_ref, ga_ref, da_ref), (ob_ref, gb_ref, db_ref))):
            o = jnp.concatenate([o_ref[j] for j in range(4)], axis=1)
            dn = dcat[:, 512 * k:512 * (k + 1)]
            rr = _rms(o)
            oh = o * rr
            st_ref[k:k + 1, :] += jnp.sum(dn * oh, axis=0, keepdims=True)
            doh = dn * g_ref[...]
            d_o = rr * (doh - oh * jnp.mean(doh * oh, axis=-1, keepdims=True))
            for j in range(4):
                d_ref[j] = d_o[:, 128 * j:128 * (j + 1)]

    t512 = pl.BlockSpec((4, tm, 128), lambda i: (0, i, 0))
    return pl.pallas_call(
        body, name="dcat_rms_bwd", grid=(s // tm,),
        in_specs=[pl.BlockSpec((tm, D), lambda i: (i, 0)), _resident((D, D)), t512, t512,
                  _const((1, 512)), _const((1, 512))],
        out_specs=[t512, t512, _const((8, 512))],
        out_shape=[jax.ShapeDtypeStruct((4, s, 128), F32), jax.ShapeDtypeStruct((4, s, 128), F32),
                   jax.ShapeDtypeStruct((8, 512), F32)],
        compiler_params=_cp(("arbitrary",)),
    )(dz1b, w_o, o_a, o_b, _row(norm_a_g), _row(norm_b_g))


def _dproj_combine(dqa, dka, dva, dqkv_b, tm=256):
    s = dka.shape[0]

    def body(qa, ka, va, qb, kb, vb, o_ref):
        for j in range(4):
            o_ref[:, 128 * j:128 * (j + 1)] = qa[j].astype(BF16)
            o_ref[:, 768 + 128 * j:768 + 128 * (j + 1)] = qb[j].astype(BF16)
            o_ref[:, 1280 + 128 * j:1280 + 128 * (j + 1)] = kb[j].astype(BF16)
            o_ref[:, 1792 + 128 * j:1792 + 128 * (j + 1)] = vb[j].astype(BF16)
        o_ref[:, 512:640] = ka[...].astype(BF16)
        o_ref[:, 640:768] = va[...].astype(BF16)

    t512 = pl.BlockSpec((4, tm, 128), lambda i: (0, i, 0))
    t128 = pl.BlockSpec((tm, 128), lambda i: (i, 0))
    return pl.pallas_call(
        body, name="dproj_combine", grid=(s // tm,),
        in_specs=[t512, t128, t128] + [t512] * 3,
        out_specs=pl.BlockSpec((tm, WIN), lambda i: (i, 0)),
        out_shape=jax.ShapeDtypeStruct((s, WIN), BF16),
        compiler_params=_cp(("parallel",)),
    )(dqa, dka, dva, *dqkv_b)


def _grad_x(dz1, dproj, w_in_t, zero, tm=512):
    s = dz1.shape[0]

    def body(dz_ref, dp_ref, w_ref, z_ref, o_ref):
        o_ref[...] = ALPHA * dz_ref[...] + _nn(dp_ref[...], w_ref[...]) + z_ref[0:1, 0:1]

    td = pl.BlockSpec((tm, D), lambda i: (i, 0))
    return pl.pallas_call(
        body, name="grad_x", grid=(s // tm,),
        in_specs=[td, pl.BlockSpec((tm, WIN), lambda i: (i, 0)), _resident((WIN, D)), _const((8, 128))],
        out_specs=td, out_shape=jax.ShapeDtypeStruct((s, D), F32),
        compiler_params=_cp(("parallel",)),
    )(dz1, dproj, w_in_t, zero)


def _place():
    return lax.axis_index("x"), lax.axis_index("y"), lax.axis_index("c")


def _other_chips(x, y):
    return [(1 - x, y), (x, 1 - y), (1 - x, 1 - y)]


def _hbm(a):
    return pltpu.with_memory_space_constraint(a, pltpu.HBM)


def _gather_w_in(shard, conv_w):
    rows_k = shard.shape[0]
    half = rows_k // 2

    def body(src, conv_src, out, conv_out, send_sems, recv_sems):
        x, y, c = _place()
        b = 2 * x + y
        sibling = (x, y, 1 - c)
        chips = _other_chips(x, y)

        def copy(idx, chip_b, core, to, first_hop=False):
            rows = out.at[pl.ds(pl.multiple_of(chip_b * rows_k + core * half, 16), half)]
            s_ref = src.at[pl.ds(pl.multiple_of(core * half, 16), half)] if first_hop else rows
            return pltpu.make_async_remote_copy(src_ref=s_ref, dst_ref=rows, send_sem=send_sems.at[idx],
                                                recv_sem=recv_sems.at[idx], device_id=to, device_id_type=MESH)

        def own_copy():
            return pltpu.make_async_remote_copy(
                src_ref=src, dst_ref=out.at[pl.ds(pl.multiple_of(b * rows_k, 16), rows_k)], send_sem=send_sems.at[6],
                recv_sem=recv_sems.at[6], device_id=sibling, device_id_type=MESH)

        def conv_copy(idx, chip_b, to):
            return pltpu.make_async_remote_copy(src_ref=conv_src, dst_ref=conv_out.at[chip_b],
                                                send_sem=send_sems.at[7 + idx], recv_sem=recv_sems.at[7 + idx],
                                                device_id=to, device_id_type=MESH)

        started = [own_copy(), conv_copy(3, b, sibling)]
        for jn, chip in enumerate(chips):
            started += [copy(jn, b, c, (chip[0], chip[1], c), first_hop=True), conv_copy(jn, b, (chip[0], chip[1], c))]
        for cp in started:
            cp.start()
        for jn, chip in enumerate(chips):
            cb = 2 * chip[0] + chip[1]
            copy(jn, cb, c, (chip[0], chip[1], c)).wait_recv()
            cp = copy(3 + jn, cb, c, sibling)
            cp.start()
            started.append(cp)
        for jn, chip in enumerate(chips):
            cb = 2 * chip[0] + chip[1]
            copy(3 + jn, cb, 1 - c, sibling).wait_recv()
            conv_copy(jn, cb, (chip[0], chip[1], c)).wait_recv()
        own_copy().wait_recv()
        conv_copy(3, b, sibling).wait_recv()
        for cp in started:
            cp.wait_send()

    return pl.pallas_call(
        body, name="gather_w_in",
        in_specs=[ANY, ANY], out_specs=[ANY, ANY],
        out_shape=[jax.ShapeDtypeStruct((N_CHIPS * rows_k, D), BF16), jax.ShapeDtypeStruct((N_CHIPS,) + conv_w.shape, F32)],
        scratch_shapes=[pltpu.SemaphoreType.DMA((11,)), pltpu.SemaphoreType.DMA((11,))],
        compiler_params=pltpu.CompilerParams(has_side_effects=True),
    )(shard, conv_w)


def _weight_copies(shard, land, send_sems, recv_sems, arrivals):
    x, y, c = _place()
    n_rows, n_cols = shard.shape
    peers = [(px, py, c) for px, py in _other_chips(x, y)] + [(x, y, 1 - c)]
    cps = []
    for jn, peer in enumerate(peers):
        at = 2 * peer[0] + peer[1] if arrivals else 2 * x + y
        if land.shape[1] == n_cols:
            dst = land.at[pl.ds(pl.multiple_of(at * n_rows, 16), n_rows)]
        else:
            dst = land.at[:, pl.ds(pl.multiple_of(at * n_cols, 128), n_cols)]
        cps.append(pltpu.make_async_remote_copy(src_ref=shard, dst_ref=dst, send_sem=send_sems.at[jn],
                                                recv_sem=recv_sems.at[jn], device_id=peer, device_id_type=MESH))
    return cps


def _weights_start(shards, after):
    n = len(shards)
    lands = [lax.empty((N_CHIPS * sh.shape[0], D) if sh.shape[1] == D else (D, N_CHIPS * sh.shape[1]), BF16)
             for sh in shards]

    def body(*refs):
        src, land = refs[:n], refs[n:2 * n]
        send_sems, recv_sems = refs[2 * n + 1:3 * n + 1], refs[3 * n + 1:4 * n + 1]
        for k in range(n):
            for send in _weight_copies(src[k], land[k], send_sems[k], recv_sems[k], False):
                send.start()
        refs[-1][...] = jnp.zeros_like(refs[-1])

    res = pl.pallas_call(
        body, name="weights_start",
        in_specs=[HBM] * (2 * n) + [ANY], out_specs=[SEM] * (2 * n) + [HBM] * (2 * n) + [VMEM],
        out_shape=[pltpu.SemaphoreType.DMA((4,))] * (2 * n)
        + [pltpu.HBM(a.shape, a.dtype) for a in (*shards, *lands)] + [jax.ShapeDtypeStruct((8, 128), F32)],
        input_output_aliases={i: i + 2 * n for i in range(2 * n)},
        compiler_params=pltpu.CompilerParams(has_side_effects=DATAFLOW),
    )(*[_hbm(a) for a in (*shards, *lands)], after)
    return [(res[k], res[n + k], res[2 * n + k], res[3 * n + k]) for k in range(n)], res[-1]


def _weights_wait(started, after, name):
    send_sems, recv_sems, shard, land = started

    def body(s_ref, l_ref, send_ref, recv_ref, after_ref, s_out, l_out):
        for cp in _weight_copies(s_ref, l_ref, send_ref, recv_ref, True):
            cp.wait_send()
            cp.wait_recv()

    return pl.pallas_call(
        body, name=name,
        in_specs=[HBM, HBM, SEM, SEM, ANY], out_specs=[HBM, HBM],
        out_shape=[pltpu.HBM(shard.shape, shard.dtype), pltpu.HBM(land.shape, land.dtype)],
        input_output_aliases={0: 0, 1: 1},
        compiler_params=pltpu.CompilerParams(has_side_effects=DATAFLOW),
    )(shard, land, send_sems, recv_sems, after)[1]


def _grad_copies(g_ref, land_ref, send_sems, recv_sems):
    x, y, c = _place()
    cps = []
    for d in range(1, 8):
        px, py, pc = x ^ (d >> 2), y ^ ((d >> 1) & 1), c ^ (d & 1)
        cps.append(pltpu.make_async_remote_copy(
            src_ref=g_ref.at[2 * px + py, pc], dst_ref=land_ref.at[d - 1], send_sem=send_sems.at[d - 1],
            recv_sem=recv_sems.at[d - 1], device_id=(px, py, pc), device_id_type=MESH))
    return cps


def _grads_start(grads_b, name):
    n = len(grads_b)
    lands = [lax.empty((7, g.shape[2], D), BF16) for g in grads_b]

    def body(*refs):
        g, land = refs[:n], refs[n:2 * n]
        send_sems, recv_sems = refs[2 * n:3 * n], refs[3 * n:4 * n]
        for k in range(n):
            for cp in _grad_copies(g[k], land[k], send_sems[k], recv_sems[k]):
                cp.start()
        refs[-1][...] = jnp.zeros_like(refs[-1])

    res = pl.pallas_call(
        body, name=name,
        in_specs=[HBM] * (2 * n), out_specs=[SEM] * (2 * n) + [HBM] * (2 * n) + [VMEM],
        out_shape=[pltpu.SemaphoreType.DMA((7,))] * (2 * n)
        + [pltpu.HBM(a.shape, a.dtype) for a in (*grads_b, *lands)] + [jax.ShapeDtypeStruct((8, 128), F32)],
        input_output_aliases={i: i + 2 * n for i in range(2 * n)},
        compiler_params=pltpu.CompilerParams(has_side_effects=DATAFLOW),
    )(*[_hbm(a) for a in (*grads_b, *lands)])
    return [(res[k], res[n + k], res[2 * n + k], res[3 * n + k]) for k in range(n)], res[-1]


def _grads_wait(started, after, name):
    n = len(started)

    def body(*refs):
        g, land = refs[:n], refs[n:2 * n]
        send_sems, recv_sems = refs[2 * n:3 * n], refs[3 * n:4 * n]
        for k in range(n):
            for cp in _grad_copies(g[k], land[k], send_sems[k], recv_sems[k]):
                cp.wait_send()
                cp.wait_recv()

    gs = [st[2] for st in started]
    lands = [st[3] for st in started]
    res = pl.pallas_call(
        body, name=name,
        in_specs=[HBM] * (2 * n) + [SEM] * (2 * n) + [ANY], out_specs=[HBM] * (2 * n),
        out_shape=[pltpu.HBM(a.shape, a.dtype) for a in (*gs, *lands)],
        input_output_aliases={i: i for i in range(2 * n)},
        compiler_params=pltpu.CompilerParams(has_side_effects=DATAFLOW),
    )(*gs, *lands, *[st[0] for st in started], *[st[1] for st in started], after)
    return res[n:]


def _sum_partials(grad4, got, cb, name, tr):
    h = grad4.shape[2]
    per_half = h // tr

    def body(cb_ref, g_ref, o_ref, out_ref):
        acc = g_ref[...]
        for j in range(7):
            acc = acc + o_ref[j].astype(F32)
        out_ref[...] = acc

    return pl.pallas_call(
        body, name=name,
        grid_spec=pltpu.PrefetchScalarGridSpec(
            num_scalar_prefetch=1, grid=(per_half,),
            in_specs=[pl.BlockSpec((None, None, tr, D), lambda i, cb_ref: (cb_ref[1], cb_ref[0], i, 0)),
                      pl.BlockSpec((7, tr, D), lambda i, cb_ref: (0, i, 0))],
            out_specs=pl.BlockSpec((tr, D), lambda i, cb_ref: (cb_ref[0] * per_half + i, 0))),
        out_shape=jax.ShapeDtypeStruct((2 * h, D), F32),
        compiler_params=_cp(("arbitrary",)),
    )(cb, grad4, got)


def _swap_halves(shards, name):
    n = len(shards)

    def body(*refs):
        out, send_sems, recv_sems = refs[n:2 * n], refs[2 * n], refs[2 * n + 1]
        x, y, c = _place()
        cps = []
        for k in range(n):
            h = shards[k].shape[0] // 2
            mine = out[k].at[pl.ds(pl.multiple_of(c * h, 8), h)]
            cp = pltpu.make_async_remote_copy(src_ref=mine, dst_ref=mine, send_sem=send_sems.at[k],
                                              recv_sem=recv_sems.at[k], device_id=(x, y, 1 - c), device_id_type=MESH)
            cp.start()
            cps.append(cp)
        for cp in cps:
            cp.wait()

    return pl.pallas_call(
        body, name=name,
        in_specs=[ANY] * n, out_specs=[ANY] * n,
        out_shape=[jax.ShapeDtypeStruct(sh.shape, F32) for sh in shards],
        input_output_aliases={k: k for k in range(n)},
        scratch_shapes=[pltpu.SemaphoreType.DMA((n,)), pltpu.SemaphoreType.DMA((n,))],
        compiler_params=pltpu.CompilerParams(has_side_effects=True),
    )(*shards)


def _small_copies(small_ref, land_ref, send_sems, recv_sems):
    x, y, c = _place()
    me = 4 * x + 2 * y + c
    cps = []
    for d in range(1, 8):
        px, py, pc = x ^ (d >> 2), y ^ ((d >> 1) & 1), c ^ (d & 1)
        cps.append(pltpu.make_async_remote_copy(
            src_ref=small_ref, dst_ref=land_ref.at[me], send_sem=send_sems.at[d - 1], recv_sem=recv_sems.at[d - 1],
            device_id=(px, py, pc), device_id_type=MESH))
    return cps


def _small_start(small):
    land = lax.empty((8,) + small.shape, F32)

    def body(s_ref, l_ref, send_sems, recv_sems, s_thru, l_thru, token):
        for cp in _small_copies(s_ref, l_ref, send_sems, recv_sems):
            cp.start()
        token[...] = jnp.zeros_like(token)

    res = pl.pallas_call(
        body, name="small_start",
        in_specs=[HBM, HBM], out_specs=[SEM, SEM, HBM, HBM, VMEM],
        out_shape=[pltpu.SemaphoreType.DMA((7,)), pltpu.SemaphoreType.DMA((7,)), pltpu.HBM(small.shape, F32),
                   pltpu.HBM(land.shape, F32), jax.ShapeDtypeStruct((8, 128), F32)],
        input_output_aliases={0: 2, 1: 3},
        compiler_params=pltpu.CompilerParams(has_side_effects=DATAFLOW),
    )(_hbm(small), _hbm(land))
    return res[:4], res[4]


def _small_wait(started, after):
    send_sems, recv_sems, small, land = started

    def body(s_ref, l_ref, send_ref, recv_ref, after_ref, s_out, l_out):
        for cp in _small_copies(s_ref, l_ref, send_ref, recv_ref):
            cp.wait_send()
            cp.wait_recv()

    return pl.pallas_call(
        body, name="small_wait",
        in_specs=[HBM, HBM, SEM, SEM, ANY], out_specs=[HBM, HBM],
        out_shape=[pltpu.HBM(small.shape, F32), pltpu.HBM(land.shape, F32)],
        input_output_aliases={0: 0, 1: 1},
        compiler_params=pltpu.CompilerParams(has_side_effects=DATAFLOW),
    )(small, land, send_sems, recv_sems, after)


def _small_sum(small, land, me):
    rows = small.shape[0]

    def body(me_ref, s_ref, l_ref, o_ref):
        acc = None
        for k in range(8):
            term = jnp.where(me_ref[0] == k, s_ref[...], l_ref[k])
            acc = term if k == 0 else acc + term
        o_ref[...] = acc

    return pl.pallas_call(
        body, name="small_sum",
        in_specs=[SMEM, VMEM, VMEM], out_specs=VMEM,
        out_shape=jax.ShapeDtypeStruct((rows, D), F32),
    )(me, small, land)


def _adamw(w, g, m, v, name, tr):
    rows, cols = w.shape

    def body(w_ref, g_ref, m_ref, v_ref, d_ref, nm_ref, nv_ref):
        g_ = g_ref[...]
        nm = ADAM_B1 * m_ref[...] + (1.0 - ADAM_B1) * g_
        nv = ADAM_B2 * v_ref[...] + (1.0 - ADAM_B2) * (g_ * g_)
        m_hat = nm / (1.0 - ADAM_B1 ** ADAM_STEP)
        v_hat = nv / (1.0 - ADAM_B2 ** ADAM_STEP)
        d_ref[...] = -ADAM_LR * (m_hat / (jnp.sqrt(v_hat) + ADAM_EPS) + ADAM_WD * w_ref[...])
        nm_ref[...] = nm
        nv_ref[...] = nv

    spec = pl.BlockSpec((tr, cols), lambda i: (i, 0))
    return pl.pallas_call(
        body, name=name, grid=(rows // tr,),
        in_specs=[spec] * 4, out_specs=[spec] * 3,
        out_shape=[jax.ShapeDtypeStruct((rows, cols), F32)] * 3,
        compiler_params=_cp(("parallel",)),
    )(w, g, m, v)


def _local_step(x, target, w_in_t, late_weights, norm_a_g, norm_b_g, sinks_a, ln1_g, ln1_b,
                conv_w, conv_b, ln2_g, ln2_b, slopes, on_grad, on_small):
    cwb = jnp.concatenate([conv_w, conv_b[None]], axis=0).reshape(4, 2, FF)

    proj, xb = _proj(x, w_in_t, "proj")
    o_a, lse_a = _attn_a_fwd(proj, sinks_a)
    fwd_b = [_attn_b_fwd(proj, slopes, r) for r in B_DILATIONS]
    w_o = late_weights(1, fwd_b[-1][1])
    o_b, lse_b, cat, z1, h1, h1b = _mix_ln1(x, o_a, [f[0] for f in fwd_b], [f[1] for f in fwd_b],
                                           norm_a_g, norm_b_g, w_o, ln1_g, ln1_b)
    w_up = late_weights(2, h1b)
    up, a, gate, a1 = _up_conv_gelu(h1b, w_up, cwb)
    w_down = late_weights(3, a)
    dz2, dz2b, st2 = _down_ln2_loss(a, w_down, h1, target, ln2_g, ln2_b)

    on_grad(3, *_grad_w(a, dz2b, "grad_w_down", tm=FF // 2))
    dup, dconv = _conv_gelu_bwd(_d_act(dz2b, w_down), up, gate, a1, cwb)
    on_grad(2, *_grad_w(dup, h1b, "grad_w_up", tm=FF // 2, lhs_halves=True))
    dz1, dz1b, st1 = _dh1_ln1_bwd(dz2, dup, w_up, z1, ln1_g)
    tok = on_grad(1, *_grad_w(cat, dz1b, "grad_w_o", tm=512))
    d_oa, d_ob, st_n = _dcat_rms_bwd(dz1b, w_o, o_a, o_b, norm_a_g + tok[0, 0], norm_b_g)
    dqa, dka, dva, dsink = _attn_a_bwd(proj, sinks_a, d_oa, o_a, lse_a)
    dconv = dconv.reshape(4, 2 * FF)
    tok = on_small(dict(loss=st2[2, 0:1], norm_a_g=st_n[0], norm_b_g=st_n[1], sinks_a=dsink[:, 0],
                        ln1_g=st1[0], ln1_b=st1[1], conv_w=dconv[0:3].reshape(-1), conv_b=dconv[3],
                        ln2_g=st2[0], ln2_b=st2[1]))
    slopes = slopes + tok[0, 0]
    bwd_b = None
    for r in B_DILATIONS:
        bwd_b = _attn_b_bwd(proj, slopes, d_ob, o_b, lse_b, r, bwd_b)
    dproj = _dproj_combine(dqa, dka, dva, bwd_b)
    tok = on_grad(0, *_grad_w(dproj, xb, "grad_w_in", tm=WA))
    return _grad_x(dz1, dproj, w_in_t, tok)


SMALL_ORDER = ("loss", "norm_a_g", "norm_b_g", "sinks_a", "ln1_g", "ln1_b", "conv_b", "ln2_g", "ln2_b", "conv_w")
SMALL_SIZES = dict(loss=1, norm_a_g=512, norm_b_g=512, sinks_a=8, ln1_g=D, ln1_b=D, conv_b=2 * FF, ln2_g=D, ln2_b=D,
                   conv_w=3 * 2 * FF)


def _pack(parts, rows):
    flat = jnp.concatenate([parts[k].reshape(-1).astype(F32) for k in parts])
    return jnp.pad(flat, (0, rows * D - flat.shape[0])).reshape(rows, D)


def _unpack(buf, names, sizes):
    flat = buf.reshape(-1)
    out, at = {}, 0
    for k in names:
        out[k] = flat[at:at + sizes[k]]
        at += sizes[k]
    return out


def kernel(x, w_in, norm_a_g, norm_b_g, sinks_a, w_o, ln1_g, ln1_b, w_up, conv_w, conv_b, w_down, ln2_g, ln2_b, loss_target, m_w_in, m_norm_a_g, m_norm_b_g, m_sinks_a, m_w_o, m_ln1_g, m_ln1_b, m_w_up, m_conv_w, m_conv_b, m_w_down, m_ln2_g, m_ln2_b, v_w_in, v_norm_a_g, v_norm_b_g, v_sinks_a, v_w_o, v_ln1_g, v_ln1_b, v_w_up, v_conv_w, v_conv_b, v_w_down, v_ln2_g, v_ln2_b):
    xi, yi, ci = _place()
    chip = (2 * xi + yi).astype(I32)
    core = ci.astype(I32)

    w_in_rows, m_w_in_rows, v_w_in_rows = w_in.T, m_w_in.T, v_w_in.T
    shards = (w_in_rows.astype(BF16), w_o.astype(BF16), w_up.astype(BF16), w_down.astype(BF16))
    w_in_t, conv_w4 = _gather_w_in(shards[0], conv_w)
    conv_w_f = conv_w4.transpose(1, 0, 2).reshape(3, 2 * FF)
    w_started, w_tok = _weights_start(shards[1:], conv_w4)
    slopes = jnp.asarray(SLOPES, F32) + w_tok[0, 0]

    halves_rows = [r // 2 for r in SHARD_ROWS]
    grads4, grads_b4, started = [None] * 4, [None] * 4, [None] * 4

    def on_grad(k, g, g_b):
        grads4[k] = g.reshape(N_CHIPS, 2, halves_rows[k], D)
        grads_b4[k] = g_b.reshape(N_CHIPS, 2, halves_rows[k], D)
        if k > 1:
            return None
        group = (1, 2, 3) if k == 1 else (0,)
        sts, tok = _grads_start([grads_b4[i] for i in group], f"grads_start_{k}")
        for i, st in zip(group, sts):
            started[i] = st
        return tok

    small_rows = 32
    small_started = []

    def on_small(parts):
        st, tok = _small_start(_pack({k: parts[k] for k in SMALL_ORDER}, small_rows))
        small_started.append(st)
        return tok

    gx = _local_step(
        x[0], loss_target[0], w_in_t, lambda k, after: _weights_wait(w_started[k - 1], after, f"weights_wait_{k}"),
        norm_a_g, norm_b_g, sinks_a, ln1_g, ln1_b, conv_w_f, conv_b, ln2_g, ln2_b, slopes, on_grad, on_small)

    tiles = (96, 128, 352, 176)
    core_chip = jnp.stack([core, chip])
    got = _grads_wait(started[1:], gx, "grads_wait_1")
    halves = [_sum_partials(grads4[k], got[k - 1], core_chip, f"sum_partials_{k}", tiles[k]) for k in (1, 2, 3)]
    g_w_o, g_w_up_rows, g_w_down = _swap_halves(halves, "swap_halves")
    g_w_up = g_w_up_rows.T
    delta, new_m, new_v = {}, {}, {}
    for k, g, tr in (("w_o", g_w_o, 128), ("w_up", g_w_up, 256), ("w_down", g_w_down, 176)):
        delta[k], new_m[k], new_v[k] = _adamw(dict(w_o=w_o, w_up=w_up, w_down=w_down)[k], g,
                                              dict(w_o=m_w_o, w_up=m_w_up, w_down=m_w_down)[k],
                                              dict(w_o=v_w_o, w_up=v_w_up, w_down=v_w_down)[k], f"adamw_{k}", tr)

    got = _grads_wait(started[:1], delta["w_up"], "grads_wait_0")
    half_in = _sum_partials(grads4[0], got[0], core_chip, "sum_partials_0", tiles[0])
    (g_w_in_rows,) = _swap_halves([half_in], "swap_halves_in")
    small_mine, small_land = _small_wait(small_started[0], g_w_in_rows)
    totals = _small_sum(small_mine, small_land, (4 * xi + 2 * yi + ci).astype(I32).reshape(1))
    tot = _unpack(totals, SMALL_ORDER, SMALL_SIZES)
    loss = tot["loss"][0]
    cols = 2 * FF // N_CHIPS
    g_conv_w = lax.dynamic_slice(tot["conv_w"].reshape(3, 2 * FF), (0, chip * cols), (3, cols))
    g_small = dict(norm_a_g=tot["norm_a_g"], norm_b_g=tot["norm_b_g"], sinks_a=tot["sinks_a"], ln1_g=tot["ln1_g"],
                   ln1_b=tot["ln1_b"], conv_w=g_conv_w, conv_b=tot["conv_b"], ln2_g=tot["ln2_g"], ln2_b=tot["ln2_b"])

    weights = dict(w_in=w_in, norm_a_g=norm_a_g, norm_b_g=norm_b_g, sinks_a=sinks_a, w_o=w_o, ln1_g=ln1_g, ln1_b=ln1_b,
                   w_up=w_up, conv_w=conv_w, conv_b=conv_b, w_down=w_down, ln2_g=ln2_g, ln2_b=ln2_b)
    ms = dict(w_in=m_w_in, norm_a_g=m_norm_a_g, norm_b_g=m_norm_b_g, sinks_a=m_sinks_a, w_o=m_w_o, ln1_g=m_ln1_g,
              ln1_b=m_ln1_b, w_up=m_w_up, conv_w=m_conv_w, conv_b=m_conv_b, w_down=m_w_down, ln2_g=m_ln2_g, ln2_b=m_ln2_b)
    vs = dict(w_in=v_w_in, norm_a_g=v_norm_a_g, norm_b_g=v_norm_b_g, sinks_a=v_sinks_a, w_o=v_w_o, ln1_g=v_ln1_g,
              ln1_b=v_ln1_b, w_up=v_w_up, conv_w=v_conv_w, conv_b=v_conv_b, w_down=v_w_down, ln2_g=v_ln2_g, ln2_b=v_ln2_b)
    order = list(weights)
    grad = dict(g_small, w_in=g_w_in_rows.T, w_o=g_w_o, w_up=g_w_up, w_down=g_w_down)

    delta["w_in"], new_m["w_in"], new_v["w_in"] = [
        a.T for a in _adamw(w_in_rows, g_w_in_rows, m_w_in_rows, v_w_in_rows, "adamw_w_in", 144)]
    small_names = [k for k in order if k not in delta]
    sizes = {k: weights[k].size for k in small_names}
    rows = 16
    packed = [_pack({k: src[k] for k in small_names}, rows) for src in (weights, grad, ms, vs)]
    for res, buf in zip((delta, new_m, new_v), _adamw(*packed, "adamw_small", rows)):
        for k, val in _unpack(buf, small_names, sizes).items():
            res[k] = val.reshape(weights[k].shape)

    return (loss, gx[None], *[grad[k] for k in order], *[delta[k] for k in order],
            *[new_m[k] for k in order], *[new_v[k] for k in order])
```

```python
import functools
import math

import jax
import jax.numpy as jnp
from jax import lax
from jax.experimental import pallas as pl
from jax.experimental.pallas import tpu as pltpu

F32, BF16, I32 = jnp.float32, jnp.bfloat16, jnp.int32

D = 1024
FF = 2816
HD = 64
NH = 8
WA, WB = 768, 1536
WIN = WA + WB
BLK = 128
ALPHA = 2.0 ** 0.25
LN_EPS, RMS_EPS = 1e-5, 1e-6
SCALE = 1.0 / math.sqrt(HD)
A_MAX_DIST, B_MAX_DIST = 127, 128
B_DILATIONS = (1, 4, 16)
SLOPES = tuple(2.0 ** (-(i + 1)) for i in range(NH))
SHARD_ROWS = (WIN // 4, D // 4, 2 * FF // 4, FF // 4)
N_CHIPS = 4
ADAM_LR, ADAM_B1, ADAM_B2, ADAM_EPS, ADAM_WD, ADAM_STEP = 0.001, 0.9, 0.999, 1e-08, 0.01, 10
MESH = pl.DeviceIdType.MESH
ANY = pl.BlockSpec(memory_space=pl.ANY)
SMEM = pl.BlockSpec(memory_space=pltpu.SMEM)
VMEM = pl.BlockSpec(memory_space=pltpu.VMEM)
HBM = pl.BlockSpec(memory_space=pltpu.HBM)
SEM = pl.BlockSpec(memory_space=pltpu.SEMAPHORE)
DATAFLOW = pltpu.SideEffectType.DATAFLOW_SIDE_EFFECTING


def _cp(sem, mb=48):
    return pltpu.CompilerParams(dimension_semantics=sem, vmem_limit_bytes=mb << 20)


def _nn(a, b):
    return lax.dot_general(a, b, (((1,), (0,)), ((), ())), preferred_element_type=F32)


def _nt(a, b):
    return lax.dot_general(a, b, (((1,), (1,)), ((), ())), preferred_element_type=F32)


def _tn(a, b):
    return lax.dot_general(a, b, (((0,), (0,)), ((), ())), preferred_element_type=F32)


def _resident(shape):
    n = len(shape)
    return pl.BlockSpec(shape, lambda *_: (0,) * n, pipeline_mode=pl.Buffered(1))


def _const(shape):
    n = len(shape)
    return pl.BlockSpec(shape, lambda *_: (0,) * n)


def _proj(x, w_t, name, tm=512):
    s = x.shape[0]
    n = w_t.shape[0]

    def body(x_ref, w_ref, o_ref, xb_ref):
        xb = x_ref[...].astype(BF16)
        xb_ref[...] = xb
        res = _nt(xb, w_ref[...])
        for g in range(n // 128):
            o_ref[g] = res[:, 128 * g:128 * (g + 1)]

    return pl.pallas_call(
        body, name=name, grid=(s // tm,),
        in_specs=[pl.BlockSpec((tm, D), lambda i: (i, 0)), _resident((n, D))],
        out_specs=[pl.BlockSpec((n // 128, tm, 128), lambda i: (0, i, 0)), pl.BlockSpec((tm, D), lambda i: (i, 0))],
        out_shape=[jax.ShapeDtypeStruct((n // 128, s, 128), F32), jax.ShapeDtypeStruct((s, D), BF16)],
        compiler_params=_cp(("parallel",)),
    )(x, w_t)


def _grad_w(lhs, rhs, name, tm, tk=2048, lhs_halves=False):
    s = rhs.shape[0]
    if lhs_halves:
        per_half = lhs.shape[2] // tm
        n = 2 * lhs.shape[2]
        lhs_spec = pl.BlockSpec((None, tk, tm), lambda i, k: (i // per_half, k, i % per_half))
    else:
        n = lhs.shape[1]
        lhs_spec = pl.BlockSpec((tk, tm), lambda i, k: (k, i))
    nk = s // tk

    def body(l_ref, r_ref, o_ref, ob_ref):
        k = pl.program_id(1)

        @pl.when(k == 0)
        def _():
            o_ref[...] = jnp.zeros_like(o_ref)

        o_ref[...] += _tn(l_ref[...], r_ref[...])

        @pl.when(k == nk - 1)
        def _():
            ob_ref[...] = o_ref[...].astype(BF16)

    return pl.pallas_call(
        body, name=name, grid=(n // tm, nk),
        in_specs=[lhs_spec, pl.BlockSpec((tk, D), lambda i, k: (k, 0))],
        out_specs=[pl.BlockSpec((tm, D), lambda i, k: (i, 0))] * 2,
        out_shape=[jax.ShapeDtypeStruct((n, D), F32), jax.ShapeDtypeStruct((n, D), BF16)],
        compiler_params=_cp(("parallel", "arbitrary")),
    )(lhs, rhs)


def _band_base(max_dist, dist_unit, first):
    row = lax.broadcasted_iota(I32, (BLK, 2 * BLK), 0)
    col = lax.broadcasted_iota(I32, (BLK, 2 * BLK), 1)
    dist = BLK + row - col
    ok = (dist >= 0) & (dist <= max_dist)
    if first:
        ok = ok & (col >= BLK)
    return jnp.where(ok, dist.astype(F32) * (-float(dist_unit)), -jnp.inf)


def _half_mask(shape, e):
    lane = lax.broadcasted_iota(I32, shape, 1)
    return (lane < HD) if e == 0 else (lane >= HD)


def _to_half(x, e, g):
    if g != e:
        x = pltpu.roll(x, HD, 1)
    return jnp.where(_half_mask(x.shape, g), x, 0.0)


def _stack_heads(scalars, tile):
    return jnp.concatenate([scalars[0] * tile, scalars[1] * tile], axis=0)


def _pair_fwd(q2, kb, vb, base, slopes, kv_heads, sinks):
    lo = _half_mask((BLK, 2 * HD), 0)
    if sinks is not None:
        o2 = lse2 = None
        for e in (0, 1):
            g = kv_heads[e]
            qv = (_to_half(q2, e, g) * SCALE).astype(BF16)
            s = _nt(qv, kb) + slopes[e] * base
            m = jnp.maximum(jnp.max(s, axis=1, keepdims=True), sinks[e])
            p = jnp.exp(s - m)
            l = jnp.sum(p, axis=1, keepdims=True) + jnp.exp(sinks[e] - m)
            oh = _nn(p.astype(BF16), vb) / l
            if g != e:
                oh = pltpu.roll(oh, HD, 1)
            lse = jnp.broadcast_to(m + jnp.log(l), (BLK, 2 * HD))
            o2 = oh if e == 0 else jnp.where(lo, o2, oh)
            lse2 = lse if e == 0 else jnp.where(lo, lse2, lse)
        return o2, lse2
    qs = jnp.concatenate([_to_half(q2, e, kv_heads[e]) * SCALE for e in (0, 1)], axis=0).astype(BF16)
    s = _nt(qs, kb) + (base if slopes is None else _stack_heads(slopes, base))
    m = jnp.max(s, axis=1, keepdims=True)
    p = jnp.exp(s - m)
    l = jnp.sum(p, axis=1, keepdims=True)
    o = _nn(p.astype(BF16), vb) / l
    lse = m + jnp.log(l)
    halves = []
    for e in (0, 1):
        oh = o[e * BLK:(e + 1) * BLK]
        halves.append(pltpu.roll(oh, HD, 1) if kv_heads[e] != e else oh)
    o2 = jnp.where(lo, halves[0], halves[1])
    lse2 = jnp.where(lo, jnp.broadcast_to(lse[:BLK], (BLK, 2 * HD)), jnp.broadcast_to(lse[BLK:], (BLK, 2 * HD)))
    return o2, lse2


def _pair_bwd(q2, kb, vb, do2, o2, lse2, base, slopes, kv_heads, sinks):
    lo = _half_mask((BLK, 2 * HD), 0)
    prod = do2 * o2
    lses, deltas = [], []
    for e in (0, 1):
        hq = _half_mask((BLK, 2 * HD), e)
        lses.append(jnp.max(jnp.where(hq, lse2, -jnp.inf), axis=1, keepdims=True))
        deltas.append(jnp.sum(jnp.where(hq, prod, 0.0), axis=1, keepdims=True))
    lse = jnp.concatenate(lses, axis=0)
    delta = jnp.concatenate(deltas, axis=0)
    qs = jnp.concatenate([_to_half(q2, e, kv_heads[e]) * SCALE for e in (0, 1)], axis=0).astype(BF16)
    dos = jnp.concatenate([_to_half(do2, e, kv_heads[e]) for e in (0, 1)], axis=0).astype(BF16)
    p = jnp.exp(_nt(qs, kb) + (base if slopes is None else _stack_heads(slopes, base)) - lse)
    ds = (p * (_nt(dos, vb) - delta)).astype(BF16)
    dq = _nn(ds, kb) * SCALE
    halves = []
    for e in (0, 1):
        dqh = dq[e * BLK:(e + 1) * BLK]
        halves.append(pltpu.roll(dqh, HD, 1) if kv_heads[e] != e else dqh)
    dq2 = jnp.where(lo, halves[0], halves[1])
    dk2 = _tn(ds, qs)
    dv2 = _tn(p.astype(BF16), dos)
    dsinks = []
    if sinks is not None:
        for e in (0, 1):
            dsinks.append(jnp.sum(-jnp.exp(sinks[e] - lses[e]) * deltas[e], axis=0, keepdims=True))
    return dq2, dk2, dv2, dsinks


A_BLOCKS_PER_STEP = 2
A_BLOCKS_PER_STEP_BWD = 1


def _attn_a_fwd(proj, sinks):
    s = proj.shape[1]
    nq = A_BLOCKS_PER_STEP
    rows = BLK * nq
    steps = s // rows

    def body(sink_ref, q_ref, kp_ref, kc_ref, vp_ref, vc_ref, o_ref, lse_ref):
        n = pl.program_id(0)
        base_rest = _band_base(A_MAX_DIST, 1, False)
        base_0 = jnp.where(n > 0, base_rest, _band_base(A_MAX_DIST, 1, True))
        for i in range(nq):
            cur = pl.ds(i * BLK, BLK)
            k_prev = kc_ref[pl.ds((i - 1) * BLK, BLK), :] if i > 0 else kp_ref[...]
            v_prev = vc_ref[pl.ds((i - 1) * BLK, BLK), :] if i > 0 else vp_ref[...]
            kb = jnp.concatenate([k_prev, kc_ref[cur, :]], axis=0).astype(BF16)
            vb = jnp.concatenate([v_prev, vc_ref[cur, :]], axis=0).astype(BF16)
            for j in range(NH // 2):
                g = j // 2
                o2, lse2 = _pair_fwd(q_ref[j, cur, :], kb, vb, base_rest if i > 0 else base_0,
                                     (SLOPES[2 * j], SLOPES[2 * j + 1]), (g, g), (sink_ref[2 * j], sink_ref[2 * j + 1]))
                o_ref[j, cur, :] = o2
                lse_ref[j, cur, :] = lse2

    before = lambda n: jnp.maximum(n * nq - 1, 0)
    slab = lambda g: pl.BlockSpec((None, rows, 128), lambda n: (g, n, 0))
    edge = lambda g: pl.BlockSpec((None, BLK, 128), lambda n: (g, before(n), 0))
    quad = pl.BlockSpec((4, rows, 128), lambda n: (0, n, 0))
    return pl.pallas_call(
        body, name="attn_a_fwd", grid=(steps,),
        in_specs=[SMEM, quad, edge(4), slab(4), edge(5), slab(5)],
        out_specs=[quad, quad],
        out_shape=[jax.ShapeDtypeStruct((4, s, 128), F32)] * 2,
        compiler_params=_cp(("parallel",)),
    )(sinks, proj, proj, proj, proj, proj)


def _attn_a_bwd(proj, sinks, d_o, o, lse):
    s = proj.shape[1]
    nq = A_BLOCKS_PER_STEP_BWD
    rows = BLK * nq
    steps = s // rows

    def body(sink_ref, q_ref, kp_ref, kc_ref, vp_ref, vc_ref, do_ref, o_ref, lse_ref,
             dq_ref, dk_ref, dv_ref, dsink_ref, kcar, vcar):
        n = pl.program_id(0)

        @pl.when(n == 0)
        def _():
            kcar[...] = jnp.zeros_like(kcar)
            vcar[...] = jnp.zeros_like(vcar)
            dsink_ref[...] = jnp.zeros_like(dsink_ref)

        dk_ref[...] = kcar[...]
        dv_ref[...] = vcar[...]

        @pl.when(n < steps)
        def _():
            base_rest = _band_base(A_MAX_DIST, 1, False)
            base_0 = jnp.where(n > 0, base_rest, _band_base(A_MAX_DIST, 1, True))
            for i in range(nq):
                cur = pl.ds(i * BLK, BLK)
                k_prev = kc_ref[pl.ds((i - 1) * BLK, BLK), :] if i > 0 else kp_ref[...]
                v_prev = vc_ref[pl.ds((i - 1) * BLK, BLK), :] if i > 0 else vp_ref[...]
                kb = jnp.concatenate([k_prev, kc_ref[cur, :]], axis=0).astype(BF16)
                vb = jnp.concatenate([v_prev, vc_ref[cur, :]], axis=0).astype(BF16)
                dk_win = dv_win = None
                for j in range(NH // 2):
                    g = j // 2
                    dq2, dk2, dv2, dsk = _pair_bwd(q_ref[j, cur, :], kb, vb, do_ref[j, cur, :], o_ref[j, cur, :],
                                                   lse_ref[j, cur, :], base_rest if i > 0 else base_0,
                                                   (SLOPES[2 * j], SLOPES[2 * j + 1]), (g, g),
                                                   (sink_ref[2 * j], sink_ref[2 * j + 1]))
                    dq_ref[j, cur, :] = dq2
                    dk_win = dk2 if j == 0 else dk_win + dk2
                    dv_win = dv2 if j == 0 else dv_win + dv2
                    for e in (0, 1):
                        h = 2 * j + e
                        dsink_ref[h:h + 1, :] += jnp.broadcast_to(dsk[e], (1, 128))
                if i == 0:
                    last = pl.ds((nq - 1) * BLK, BLK)
                    dk_ref[last, :] += dk_win[:BLK]
                    dv_ref[last, :] += dv_win[:BLK]
                else:
                    kcar[pl.ds((i - 1) * BLK, BLK), :] += dk_win[:BLK]
                    vcar[pl.ds((i - 1) * BLK, BLK), :] += dv_win[:BLK]
                kcar[cur, :] = dk_win[BLK:]
                vcar[cur, :] = dv_win[BLK:]

    cur_step = lambda n: jnp.minimum(n, steps - 1)
    before = lambda n: jnp.maximum(cur_step(n) * nq - 1, 0)
    out_prev = lambda n: jnp.maximum(n - 1, 0)
    quad = pl.BlockSpec((4, rows, 128), lambda n: (0, cur_step(n), 0))
    slab = lambda g: pl.BlockSpec((None, rows, 128), lambda n: (g, cur_step(n), 0))
    edge = lambda g: pl.BlockSpec((None, BLK, 128), lambda n: (g, before(n), 0))
    return pl.pallas_call(
        body, name="attn_a_bwd", grid=(steps + 1,),
        in_specs=[SMEM, quad, edge(4), slab(4), edge(5), slab(5), quad, quad, quad],
        out_specs=[quad,
                   pl.BlockSpec((rows, 128), lambda n: (out_prev(n), 0)),
                   pl.BlockSpec((rows, 128), lambda n: (out_prev(n), 0)),
                   pl.BlockSpec((NH, 128), lambda n: (0, 0))],
        out_shape=[jax.ShapeDtypeStruct((4, s, 128), F32), jax.ShapeDtypeStruct((s, 128), F32),
                   jax.ShapeDtypeStruct((s, 128), F32), jax.ShapeDtypeStruct((NH, 128), F32)],
        scratch_shapes=[pltpu.VMEM((rows, 128), F32), pltpu.VMEM((rows, 128), F32)],
        compiler_params=_cp(("arbitrary",)),
    )(sinks, proj, proj, proj, proj, proj, d_o, o, lse)


def _stream(rho, i, r):
    start = i * BLK * r + rho
    return pl.ds(start, BLK, stride=r) if r > 1 else pl.ds(start, BLK)


def _for_streams(r, fn, side_by_side=4):
    if r <= side_by_side:
        for rho in range(r):
            fn(rho)
    else:
        def group(it, carry):
            for u in range(side_by_side):
                fn(side_by_side * it + u)
            return carry

        lax.fori_loop(0, r // side_by_side, group, 0)


B_BLOCKS_PER_STEP = {1: 8, 4: 2, 16: 1}
B_BLOCKS_PER_STEP_FWD = {1: 16, 4: 4, 16: 1}


def _attn_b_fwd(proj, slopes, r):
    s = proj.shape[1]
    nq = B_BLOCKS_PER_STEP_FWD[r]
    rows = BLK * r * nq
    steps = s // rows
    qc, kc, vc = WA // 128, WA // 128 + 4, WA // 128 + 8

    def body(slope_ref, q_ref, kp_ref, kc_ref, vp_ref, vc_ref, o_ref, lse_ref):
        j = pl.program_id(0)
        sb = pl.program_id(1)
        sl2 = (slope_ref[2 * j], slope_ref[2 * j + 1])
        bias_rest = _stack_heads(sl2, _band_base(B_MAX_DIST, r, False))
        bias_0 = jnp.where(sb > 0, bias_rest, _stack_heads(sl2, _band_base(B_MAX_DIST, r, True)))

        def stream(rho):
            for i in range(nq):
                cur = _stream(rho, i, r)
                k_prev = kc_ref[_stream(rho, i - 1, r), :] if i > 0 else kp_ref[_stream(rho, 0, r), :]
                v_prev = vc_ref[_stream(rho, i - 1, r), :] if i > 0 else vp_ref[_stream(rho, 0, r), :]
                kb = jnp.concatenate([k_prev, kc_ref[cur, :]], axis=0).astype(BF16)
                vb = jnp.concatenate([v_prev, vc_ref[cur, :]], axis=0).astype(BF16)
                o2, lse2 = _pair_fwd(q_ref[cur, :], kb, vb, bias_rest if i > 0 else bias_0, None, (0, 1), None)
                o_ref[cur, :] = o2
                lse_ref[cur, :] = lse2

        _for_streams(r, stream, side_by_side=16)

    before = lambda sb: jnp.maximum(sb * nq - 1, 0)
    return pl.pallas_call(
        body, name=f"attn_b_fwd_r{r}", grid=(NH // 2, steps),
        in_specs=[SMEM,
                  pl.BlockSpec((None, rows, 128), lambda j, sb: (qc + j, sb, 0)),
                  pl.BlockSpec((None, BLK * r, 128), lambda j, sb: (kc + j, before(sb), 0)),
                  pl.BlockSpec((None, rows, 128), lambda j, sb: (kc + j, sb, 0)),
                  pl.BlockSpec((None, BLK * r, 128), lambda j, sb: (vc + j, before(sb), 0)),
                  pl.BlockSpec((None, rows, 128), lambda j, sb: (vc + j, sb, 0))],
        out_specs=[pl.BlockSpec((None, rows, 128), lambda j, sb: (j, sb, 0))] * 2,
        out_shape=[jax.ShapeDtypeStruct((4, s, 128), F32)] * 2,
        compiler_params=_cp(("parallel", "parallel")),
    )(slopes, proj, proj, proj, proj, proj)


def _attn_b_bwd(proj, slopes, d_o, o, lse, r, so_far=None):
    s = proj.shape[1]
    nq = B_BLOCKS_PER_STEP[r]
    rows = BLK * r * nq
    steps = s // rows
    qc, kc, vc = WA // 128, WA // 128 + 4, WA // 128 + 8
    chained = so_far is not None

    def body(slope_ref, q_ref, kp_ref, kc_ref, vp_ref, vc_ref, do_ref, o_ref, lse_ref, *rest):
        if chained:
            pq_ref, pk_ref, pv_ref, dq_ref, dk_ref, dv_ref, kcar, vcar = rest
        else:
            dq_ref, dk_ref, dv_ref, kcar, vcar = rest
        j = pl.program_id(0)
        sb = pl.program_id(1)

        @pl.when(sb == 0)
        def _():
            kcar[...] = jnp.zeros_like(kcar)
            vcar[...] = jnp.zeros_like(vcar)

        if chained:
            dk_ref[...] = kcar[...] + pk_ref[...]
            dv_ref[...] = vcar[...] + pv_ref[...]
        else:
            dk_ref[...] = kcar[...]
            dv_ref[...] = vcar[...]

        @pl.when(sb < steps)
        def _():
            sl2 = (slope_ref[2 * j], slope_ref[2 * j + 1])
            bias_rest = _stack_heads(sl2, _band_base(B_MAX_DIST, r, False))
            bias_0 = jnp.where(sb > 0, bias_rest, _stack_heads(sl2, _band_base(B_MAX_DIST, r, True)))

            def stream(rho):
                for i in range(nq):
                    cur = _stream(rho, i, r)
                    k_prev = kc_ref[_stream(rho, i - 1, r), :] if i > 0 else kp_ref[_stream(rho, 0, r), :]
                    v_prev = vc_ref[_stream(rho, i - 1, r), :] if i > 0 else vp_ref[_stream(rho, 0, r), :]
                    kb = jnp.concatenate([k_prev, kc_ref[cur, :]], axis=0).astype(BF16)
                    vb = jnp.concatenate([v_prev, vc_ref[cur, :]], axis=0).astype(BF16)
                    dq2, dk2, dv2, _ = _pair_bwd(q_ref[cur, :], kb, vb, do_ref[cur, :], o_ref[cur, :], lse_ref[cur, :],
                                                 bias_rest if i > 0 else bias_0, None, (0, 1), None)
                    dq_ref[cur, :] = dq2 + pq_ref[cur, :] if chained else dq2
                    if i == 0:
                        last = _stream(rho, nq - 1, r)
                        dk_ref[last, :] += dk2[:BLK]
                        dv_ref[last, :] += dv2[:BLK]
                    else:
                        kcar[_stream(rho, i - 1, r), :] += dk2[:BLK]
                        vcar[_stream(rho, i - 1, r), :] += dv2[:BLK]
                    kcar[cur, :] = dk2[BLK:]
                    vcar[cur, :] = dv2[BLK:]

            _for_streams(r, stream, side_by_side=8)

    cur_step = lambda sb: jnp.minimum(sb, steps - 1)
    before = lambda sb: jnp.maximum(cur_step(sb) * nq - 1, 0)
    out_prev = lambda sb: jnp.maximum(sb - 1, 0)
    tile = lambda slab: pl.BlockSpec((None, rows, 128), lambda j, sb: (slab + j, cur_step(sb), 0))
    edge = lambda slab: pl.BlockSpec((None, BLK * r, 128), lambda j, sb: (slab + j, before(sb), 0))
    late = pl.BlockSpec((None, rows, 128), lambda j, sb: (j, out_prev(sb), 0))
    grads = [tile(0), late, late]
    return pl.pallas_call(
        body, name=f"attn_b_bwd_r{r}", grid=(NH // 2, steps + 1),
        in_specs=[SMEM, tile(qc), edge(kc), tile(kc), edge(vc), tile(vc), tile(0), tile(0), tile(0)]
        + (grads if chained else []),
        out_specs=grads,
        out_shape=[jax.ShapeDtypeStruct((4, s, 128), F32)] * 3,
        scratch_shapes=[pltpu.VMEM((rows, 128), F32), pltpu.VMEM((rows, 128), F32)],
        compiler_params=_cp(("parallel", "arbitrary")),
    )(slopes, proj, proj, proj, proj, proj, d_o, o, lse, *(so_far if chained else ()))


def _row(v):
    return v.reshape(1, -1)


def _layer_norm_stats(z):
    mu = jnp.mean(z, axis=-1, keepdims=True)
    zc = z - mu
    var = jnp.mean(zc * zc, axis=-1, keepdims=True)
    rstd = lax.rsqrt(var + LN_EPS)
    return zc * rstd, rstd


def _layer_norm_bwd(dh, zh, rstd, g):
    dzh = dh * g
    return rstd * (dzh - jnp.mean(dzh, axis=-1, keepdims=True) - zh * jnp.mean(dzh * zh, axis=-1, keepdims=True))


def _rms(o):
    return lax.rsqrt(jnp.mean(o * o, axis=-1, keepdims=True) + RMS_EPS)


def _mix_ln1(x, o_a, o_b, lse_b, norm_a_g, norm_b_g, w_o, ln1_g, ln1_b, tm=256):
    s = x.shape[0]

    def wide(ref):
        return jnp.concatenate([ref[j] for j in range(4)], axis=1)

    def body(x_ref, oa_ref, ob1, ob2, ob3, l1, l2, l3, ga_ref, gb_ref, wo_ref, g_ref, b_ref,
             obm_ref, lse_ref, cat_ref, z1_ref, h1_ref, h1b_ref):
        la, lb, lc = wide(l1), wide(l2), wide(l3)
        m = jnp.maximum(jnp.maximum(la, lb), lc)
        ea, eb, ec = jnp.exp(la - m), jnp.exp(lb - m), jnp.exp(lc - m)
        den = ea + eb + ec
        obm = (ea / den) * wide(ob1) + (eb / den) * wide(ob2) + (ec / den) * wide(ob3)
        lse = m + jnp.log(den)
        for j in range(4):
            obm_ref[j] = obm[:, 128 * j:128 * (j + 1)]
            lse_ref[j] = lse[:, 128 * j:128 * (j + 1)]
        oa = wide(oa_ref)
        na = oa * _rms(oa) * ga_ref[...]
        nb_ = obm * _rms(obm) * gb_ref[...]
        cat = jnp.concatenate([na, nb_], axis=1).astype(BF16)
        cat_ref[...] = cat
        z1 = ALPHA * x_ref[...] + _nn(cat, wo_ref[...])
        z1_ref[...] = z1
        zh, _ = _layer_norm_stats(z1)
        h1 = zh * g_ref[...] + b_ref[...]
        h1_ref[...] = h1
        h1b_ref[...] = h1.astype(BF16)

    t512 = pl.BlockSpec((4, tm, 128), lambda i: (0, i, 0))
    td = pl.BlockSpec((tm, D), lambda i: (i, 0))
    return pl.pallas_call(
        body, name="mix_ln1", grid=(s // tm,),
        in_specs=[td] + [t512] * 7 + [_const((1, 512))] * 2 + [_resident((D, D))] + [_const((1, D))] * 2,
        out_specs=[t512, t512, td, td, td, td],
        out_shape=[jax.ShapeDtypeStruct((4, s, 128), F32), jax.ShapeDtypeStruct((4, s, 128), F32),
                   jax.ShapeDtypeStruct((s, D), BF16), jax.ShapeDtypeStruct((s, D), F32),
                   jax.ShapeDtypeStruct((s, D), F32), jax.ShapeDtypeStruct((s, D), BF16)],
        compiler_params=_cp(("parallel",)),
    )(x, o_a, *o_b, *lse_b, _row(norm_a_g), _row(norm_b_g), w_o, _row(ln1_g), _row(ln1_b))


def _gelu_and_grad(x):
    c = math.sqrt(2.0 / math.pi)
    x2 = x * x
    cx = c * x
    t = jnp.tanh(cx * (1.0 + 0.044715 * x2))
    q = 1.0 + t
    g = (0.5 * x) * q
    dg = 0.5 * q + ((0.5 * cx) * (1.0 - t * t)) * (1.0 + (3.0 * 0.044715) * x2)
    return g, dg


def _shift_down(u, before):
    n = u.shape[0]
    ext = jnp.concatenate([before, u], axis=0)
    return pltpu.roll(ext, 1, 0)[8:], pltpu.roll(ext, 2, 0)[8:]


def _shift_up(u, after):
    n = u.shape[0]
    ext = jnp.concatenate([u, after], axis=0)
    return pltpu.roll(ext, n + 7, 0)[:n], pltpu.roll(ext, n + 6, 0)[:n]


def _up_conv_gelu(h1b, w_up, cwb, tm=256, tn=FF // 2, chunk_rows=16, piece_cols=512):
    s = h1b.shape[0]
    n_i = s // tm
    n_t = (FF // tn) * n_i

    def body(h_ref, wg_ref, wv_ref, c_ref, up_ref, a_ref, g_ref, a1_ref, pend_a, pend_b, carry):
        t = pl.program_id(0)
        row_tile = jnp.maximum(t - 1, 0) % n_i
        w_refs = (wg_ref, wv_ref)

        @pl.when(t == 0)
        def _():
            pend_b[...] = jnp.zeros_like(pend_b)
            carry[...] = jnp.zeros_like(carry)

        def step(dst, src):
            def chunk(c, before):
                rows = pl.ds(c * chunk_rows, chunk_rows)
                u, last = [], []
                for half in (0, 1):
                    up = src[half, rows, :]
                    r1, r2 = _shift_down(up, before[half])
                    u.append(r2 * c_ref[0, half:half + 1, :] + r1 * c_ref[1, half:half + 1, :]
                             + up * c_ref[2, half:half + 1, :] + c_ref[3, half:half + 1, :])
                    last.append(up[chunk_rows - 8:])
                g, dg = _gelu_and_grad(u[0])
                a_ref[rows, :] = (g * u[1]).astype(BF16)
                g_ref[rows, :] = g.astype(BF16)
                a1_ref[rows, :] = (u[1] * dg).astype(BF16)
                return tuple(last)

            edge = tuple(jnp.where(row_tile > 0, carry[half], 0.0) for half in (0, 1))
            pieces = [(half, c0, min(piece_cols, tn - c0)) for half in (0, 1) for c0 in range(0, tn, piece_cols)]
            n_c = tm // chunk_rows
            done = 0
            for p, (half, c0, width) in enumerate(pieces):
                cols = slice(c0, c0 + width)
                up = _nn(h_ref[...], w_refs[half][:, cols])
                up_ref[half, :, cols] = up.astype(BF16)
                dst[half, :, cols] = up
                upto = n_c * (p + 1) // len(pieces)
                for c in range(done, upto):
                    edge = chunk(c, edge)
                done = upto
            for half in (0, 1):
                carry[half] = edge[half]

        @pl.when(t % 2 == 0)
        def _():
            step(pend_a, pend_b)

        @pl.when(t % 2 == 1)
        def _():
            step(pend_b, pend_a)

    mm = lambda t: jnp.minimum(t, n_t - 1)
    ew = lambda t: jnp.maximum(t - 1, 0)
    out_tile = pl.BlockSpec((tm, tn), lambda t: (ew(t) % n_i, ew(t) // n_i))
    return pl.pallas_call(
        body, name="up_conv_gelu", grid=(n_t + 1,),
        in_specs=[pl.BlockSpec((tm, D), lambda t: (mm(t) % n_i, 0)),
                  pl.BlockSpec((D, tn), lambda t: (0, mm(t) // n_i)),
                  pl.BlockSpec((D, tn), lambda t: (0, FF // tn + mm(t) // n_i)),
                  pl.BlockSpec((4, 2, tn), lambda t: (0, 0, ew(t) // n_i))],
        out_specs=[pl.BlockSpec((2, tm, tn), lambda t: (0, mm(t) % n_i, mm(t) // n_i)), out_tile, out_tile, out_tile],
        out_shape=[jax.ShapeDtypeStruct((2, s, FF), BF16)] + [jax.ShapeDtypeStruct((s, FF), BF16)] * 3,
        scratch_shapes=[pltpu.VMEM((2, tm, tn), F32), pltpu.VMEM((2, tm, tn), F32), pltpu.VMEM((2, 8, tn), F32)],
        compiler_params=_cp(("arbitrary",)),
    )(h1b, w_up, w_up, cwb)


def _down_ln2_loss(a, w_down, h1, target, ln2_g, ln2_b, tm=512):
    s = a.shape[0]

    def body(a_ref, w_ref, h_ref, t_ref, g_ref, b_ref, dz_ref, dzb_ref, st_ref):
        @pl.when(pl.program_id(0) == 0)
        def _():
            st_ref[...] = jnp.zeros_like(st_ref)

        z2 = ALPHA * h_ref[...] + _nn(a_ref[...], w_ref[...])
        zh, rstd = _layer_norm_stats(z2)
        diff = zh * g_ref[...] + b_ref[...] - t_ref[...]
        part = 0.5 * jnp.sum(jnp.mean(diff * diff, axis=-1, keepdims=True), axis=0, keepdims=True)
        dy = diff * (1.0 / D)
        st_ref[0:1, :] += jnp.sum(dy * zh, axis=0, keepdims=True)
        st_ref[1:2, :] += jnp.sum(dy, axis=0, keepdims=True)
        st_ref[2:3, :] += jnp.broadcast_to(part, (1, D))
        dz = _layer_norm_bwd(dy, zh, rstd, g_ref[...])
        dz_ref[...] = dz
        dzb_ref[...] = dz.astype(BF16)

    td = pl.BlockSpec((tm, D), lambda i: (i, 0))
    return pl.pallas_call(
        body, name="down_ln2_loss", grid=(s // tm,),
        in_specs=[pl.BlockSpec((tm, FF), lambda i: (i, 0)), _resident((FF, D)), td, td, _const((1, D)), _const((1, D))],
        out_specs=[td, td, _const((8, D))],
        out_shape=[jax.ShapeDtypeStruct((s, D), F32), jax.ShapeDtypeStruct((s, D), BF16),
                   jax.ShapeDtypeStruct((8, D), F32)],
        compiler_params=_cp(("arbitrary",)),
    )(a, w_down, h1, target, _row(ln2_g), _row(ln2_b))


def _d_act(dz2b, w_down, tm=512):
    s = dz2b.shape[0]

    def body(dz_ref, w_ref, o_ref):
        o_ref[...] = _nt(dz_ref[...], w_ref[...])

    return pl.pallas_call(
        body, name="d_act", grid=(s // tm,),
        in_specs=[pl.BlockSpec((tm, D), lambda i: (i, 0)), _resident((FF, D))],
        out_specs=pl.BlockSpec((tm, FF), lambda i: (i, 0)),
        out_shape=jax.ShapeDtypeStruct((s, FF), F32),
        compiler_params=_cp(("parallel",)),
    )(dz2b, w_down)


def _conv_gelu_bwd(da, up, g, a1, cwb, tm=256, tn=FF // 2, chunk_rows=16):
    s = da.shape[0]
    n_i = s // tm
    n_c = tm // chunk_rows

    def body(da_ref, up_ref, g_ref, a1_ref, c_ref, dup_ref, dc_ref, carry):
        @pl.when(pl.program_id(1) == 0)
        def _():
            carry[...] = jnp.zeros_like(carry)
            dc_ref[...] = jnp.zeros_like(dc_ref)

        def fold(v):
            return jnp.sum(v.reshape(chunk_rows // 8, 8, v.shape[1]), axis=0)

        def chunk(cc, state):
            after, sums = state
            rows = pl.ds((n_c - 1 - cc) * chunk_rows, chunk_rows)
            da_c = da_ref[rows, :]
            dus = (da_c * a1_ref[rows, :].astype(F32), da_c * g_ref[rows, :].astype(F32))
            head, new_sums = [], []
            for half in (0, 1):
                du = dus[half]
                up = up_ref[half, rows, :].astype(F32)
                l1, l2 = _shift_up(du, after[half])
                dup = (du * c_ref[2, half:half + 1, :] + l1 * c_ref[1, half:half + 1, :]
                       + l2 * c_ref[0, half:half + 1, :])
                dup_ref[half, rows, :] = dup.astype(BF16)
                parts = (fold(l2 * up), fold(l1 * up), fold(du * up), fold(du))
                new_sums.append(parts if sums is None else tuple(a + b for a, b in zip(sums[half], parts)))
                head.append(du[:8])
            return tuple(head), new_sums

        state = ((carry[0], carry[1]), None)
        for cc in range(n_c):
            state = chunk(cc, state)
        head, sums = state
        for half in (0, 1):
            carry[half] = head[half]
            for k in range(4):
                dc_ref[k, half:half + 1, :] += jnp.sum(sums[half][k], axis=0, keepdims=True)

    rev = lambda ii: n_i - 1 - ii
    tile = pl.BlockSpec((tm, tn), lambda j, ii: (rev(ii), j))
    pair = pl.BlockSpec((2, tm, tn), lambda j, ii: (0, rev(ii), j))
    per_col = pl.BlockSpec((4, 2, tn), lambda j, ii: (0, 0, j))
    return pl.pallas_call(
        body, name="conv_gelu_bwd", grid=(FF // tn, n_i),
        in_specs=[tile, pair, tile, tile, per_col],
        out_specs=[pair, per_col],
        out_shape=[jax.ShapeDtypeStruct((2, s, FF), BF16), jax.ShapeDtypeStruct((4, 2, FF), F32)],
        scratch_shapes=[pltpu.VMEM((2, 8, tn), F32)],
        compiler_params=_cp(("parallel", "arbitrary")),
    )(da, up, g, a1, cwb)


def _dh1_ln1_bwd(dz2, dup, w_up, z1, ln1_g, tm=512):
    s = dz2.shape[0]

    def body(dz2_ref, dup_ref, w_ref, z1_ref, g_ref, dz1_ref, dz1b_ref, st_ref):
        @pl.when(pl.program_id(0) == 0)
        def _():
            st_ref[...] = jnp.zeros_like(st_ref)

        dh = ALPHA * dz2_ref[...] + _nt(dup_ref[0], w_ref[:, :FF]) + _nt(dup_ref[1], w_ref[:, FF:])
        zh, rstd = _layer_norm_stats(z1_ref[...])
        st_ref[0:1, :] += jnp.sum(dh * zh, axis=0, keepdims=True)
        st_ref[1:2, :] += jnp.sum(dh, axis=0, keepdims=True)
        dz = _layer_norm_bwd(dh, zh, rstd, g_ref[...])
        dz1_ref[...] = dz
        dz1b_ref[...] = dz.astype(BF16)

    td = pl.BlockSpec((tm, D), lambda i: (i, 0))
    return pl.pallas_call(
        body, name="dh1_ln1_bwd", grid=(s // tm,),
        in_specs=[td, pl.BlockSpec((2, tm, FF), lambda i: (0, i, 0)), _resident((D, 2 * FF)), td, _const((1, D))],
        out_specs=[td, td, _const((8, D))],
        out_shape=[jax.ShapeDtypeStruct((s, D), F32), jax.ShapeDtypeStruct((s, D), BF16),
                   jax.ShapeDtypeStruct((8, D), F32)],
        compiler_params=_cp(("arbitrary",), 58),
    )(dz2, dup, w_up, z1, _row(ln1_g))


def _dcat_rms_bwd(dz1b, w_o, o_a, o_b, norm_a_g, norm_b_g, tm=512):
    s = dz1b.shape[0]

    def body(dz_ref, w_ref, oa_ref, ob_ref, ga_ref, gb_ref, da_ref, db_ref, st_ref):
        @pl.when(pl.program_id(0) == 0)
        def _():
            st_ref[...] = jnp.zeros_like(st_ref)

        dcat = _nt(dz_ref[...], w_ref[...])
        for k, (o_ref, g_ref, d_ref) in enumerate(((oa_ref, ga_ref, da_ref), (ob_ref, gb_ref, db_ref))):
            o = jnp.concatenate([o_ref[j] for j in range(4)], axis=1)
            dn = dcat[:, 512 * k:512 * (k + 1)]
            rr = _rms(o)
            oh = o * rr
            st_ref[k:k + 1, :] += jnp.sum(dn * oh, axis=0, keepdims=True)
            doh = dn * g_ref[...]
            d_o = rr * (doh - oh * jnp.mean(doh * oh, axis=-1, keepdims=True))
            for j in range(4):
                d_ref[j] = d_o[:, 128 * j:128 * (j + 1)]

    t512 = pl.BlockSpec((4, tm, 128), lambda i: (0, i, 0))
    return pl.pallas_call(
        body, name="dcat_rms_bwd", grid=(s // tm,),
        in_specs=[pl.BlockSpec((tm, D), lambda i: (i, 0)), _resident((D, D)), t512, t512,
                  _const((1, 512)), _const((1, 512))],
        out_specs=[t512, t512, _const((8, 512))],
        out_shape=[jax.ShapeDtypeStruct((4, s, 128), F32), jax.ShapeDtypeStruct((4, s, 128), F32),
                   jax.ShapeDtypeStruct((8, 512), F32)],
        compiler_params=_cp(("arbitrary",)),
    )(dz1b, w_o, o_a, o_b, _row(norm_a_g), _row(norm_b_g))


def _dproj_combine(dqa, dka, dva, dqkv_b, tm=256):
    s = dka.shape[0]

    def body(qa, ka, va, qb, kb, vb, o_ref):
        for j in range(4):
            o_ref[:, 128 * j:128 * (j + 1)] = qa[j].astype(BF16)
            o_ref[:, 768 + 128 * j:768 + 128 * (j + 1)] = qb[j].astype(BF16)
            o_ref[:, 1280 + 128 * j:1280 + 128 * (j + 1)] = kb[j].astype(BF16)
            o_ref[:, 1792 + 128 * j:1792 + 128 * (j + 1)] = vb[j].astype(BF16)
        o_ref[:, 512:640] = ka[...].astype(BF16)
        o_ref[:, 640:768] = va[...].astype(BF16)

    t512 = pl.BlockSpec((4, tm, 128), lambda i: (0, i, 0))
    t128 = pl.BlockSpec((tm, 128), lambda i: (i, 0))
    return pl.pallas_call(
        body, name="dproj_combine", grid=(s // tm,),
        in_specs=[t512, t128, t128] + [t512] * 3,
        out_specs=pl.BlockSpec((tm, WIN), lambda i: (i, 0)),
        out_shape=jax.ShapeDtypeStruct((s, WIN), BF16),
        compiler_params=_cp(("parallel",)),
    )(dqa, dka, dva, *dqkv_b)


def _grad_x(dz1, dproj, w_in_t, zero, tm=512):
    s = dz1.shape[0]

    def body(dz_ref, dp_ref, w_ref, z_ref, o_ref):
        o_ref[...] = ALPHA * dz_ref[...] + _nn(dp_ref[...], w_ref[...]) + z_ref[0:1, 0:1]

    td = pl.BlockSpec((tm, D), lambda i: (i, 0))
    return pl.pallas_call(
        body, name="grad_x", grid=(s // tm,),
        in_specs=[td, pl.BlockSpec((tm, WIN), lambda i: (i, 0)), _resident((WIN, D)), _const((8, 128))],
        out_specs=td, out_shape=jax.ShapeDtypeStruct((s, D), F32),
        compiler_params=_cp(("parallel",)),
    )(dz1, dproj, w_in_t, zero)


def _place():
    return lax.axis_index("x"), lax.axis_index("y"), lax.axis_index("c")


def _other_chips(x, y):
    return [(1 - x, y), (x, 1 - y), (1 - x, 1 - y)]


def _hbm(a):
    return pltpu.with_memory_space_constraint(a, pltpu.HBM)


def _gather_w_in(shard, conv_w):
    rows_k = shard.shape[0]
    half = rows_k // 2

    def body(src, conv_src, out, conv_out, send_sems, recv_sems):
        x, y, c = _place()
        b = 2 * x + y
        sibling = (x, y, 1 - c)
        chips = _other_chips(x, y)

        def copy(idx, chip_b, core, to, first_hop=False):
            rows = out.at[pl.ds(pl.multiple_of(chip_b * rows_k + core * half, 16), half)]
            s_ref = src.at[pl.ds(pl.multiple_of(core * half, 16), half)] if first_hop else rows
            return pltpu.make_async_remote_copy(src_ref=s_ref, dst_ref=rows, send_sem=send_sems.at[idx],
                                                recv_sem=recv_sems.at[idx], device_id=to, device_id_type=MESH)

        def own_copy():
            return pltpu.make_async_remote_copy(
                src_ref=src, dst_ref=out.at[pl.ds(pl.multiple_of(b * rows_k, 16), rows_k)], send_sem=send_sems.at[6],
                recv_sem=recv_sems.at[6], device_id=sibling, device_id_type=MESH)

        def conv_copy(idx, chip_b, to):
            return pltpu.make_async_remote_copy(src_ref=conv_src, dst_ref=conv_out.at[chip_b],
                                                send_sem=send_sems.at[7 + idx], recv_sem=recv_sems.at[7 + idx],
                                                device_id=to, device_id_type=MESH)

        started = [own_copy(), conv_copy(3, b, sibling)]
        for jn, chip in enumerate(chips):
            started += [copy(jn, b, c, (chip[0], chip[1], c), first_hop=True), conv_copy(jn, b, (chip[0], chip[1], c))]
        for cp in started:
            cp.start()
        for jn, chip in enumerate(chips):
            cb = 2 * chip[0] + chip[1]
            copy(jn, cb, c, (chip[0], chip[1], c)).wait_recv()
            cp = copy(3 + jn, cb, c, sibling)
            cp.start()
            started.append(cp)
        for jn, chip in enumerate(chips):
            cb = 2 * chip[0] + chip[1]
            copy(3 + jn, cb, 1 - c, sibling).wait_recv()
            conv_copy(jn, cb, (chip[0], chip[1], c)).wait_recv()
        own_copy().wait_recv()
        conv_copy(3, b, sibling).wait_recv()
        for cp in started:
            cp.wait_send()

    return pl.pallas_call(
        body, name="gather_w_in",
        in_specs=[ANY, ANY], out_specs=[ANY, ANY],
        out_shape=[jax.ShapeDtypeStruct((N_CHIPS * rows_k, D), BF16), jax.ShapeDtypeStruct((N_CHIPS,) + conv_w.shape, F32)],
        scratch_shapes=[pltpu.SemaphoreType.DMA((11,)), pltpu.SemaphoreType.DMA((11,))],
        compiler_params=pltpu.CompilerParams(has_side_effects=True),
    )(shard, conv_w)


def _weight_copies(shard, land, send_sems, recv_sems, arrivals):
    x, y, c = _place()
    n_rows, n_cols = shard.shape
    peers = [(px, py, c) for px, py in _other_chips(x, y)] + [(x, y, 1 - c)]
    cps = []
    for jn, peer in enumerate(peers):
        at = 2 * peer[0] + peer[1] if arrivals else 2 * x + y
        if land.shape[1] == n_cols:
            dst = land.at[pl.ds(pl.multiple_of(at * n_rows, 16), n_rows)]
        else:
            dst = land.at[:, pl.ds(pl.multiple_of(at * n_cols, 128), n_cols)]
        cps.append(pltpu.make_async_remote_copy(src_ref=shard, dst_ref=dst, send_sem=send_sems.at[jn],
                                                recv_sem=recv_sems.at[jn], device_id=peer, device_id_type=MESH))
    return cps


def _weights_start(shards, after):
    n = len(shards)
    lands = [lax.empty((N_CHIPS * sh.shape[0], D) if sh.shape[1] == D else (D, N_CHIPS * sh.shape[1]), BF16)
             for sh in shards]

    def body(*refs):
        src, land = refs[:n], refs[n:2 * n]
        send_sems, recv_sems = refs[2 * n + 1:3 * n + 1], refs[3 * n + 1:4 * n + 1]
        for k in range(n):
            for send in _weight_copies(src[k], land[k], send_sems[k], recv_sems[k], False):
                send.start()
        refs[-1][...] = jnp.zeros_like(refs[-1])

    res = pl.pallas_call(
        body, name="weights_start",
        in_specs=[HBM] * (2 * n) + [ANY], out_specs=[SEM] * (2 * n) + [HBM] * (2 * n) + [VMEM],
        out_shape=[pltpu.SemaphoreType.DMA((4,))] * (2 * n)
        + [pltpu.HBM(a.shape, a.dtype) for a in (*shards, *lands)] + [jax.ShapeDtypeStruct((8, 128), F32)],
        input_output_aliases={i: i + 2 * n for i in range(2 * n)},
        compiler_params=pltpu.CompilerParams(has_side_effects=DATAFLOW),
    )(*[_hbm(a) for a in (*shards, *lands)], after)
    return [(res[k], res[n + k], res[2 * n + k], res[3 * n + k]) for k in range(n)], res[-1]


def _weights_wait(started, after, name):
    send_sems, recv_sems, shard, land = started

    def body(s_ref, l_ref, send_ref, recv_ref, after_ref, s_out, l_out):
        for cp in _weight_copies(s_ref, l_ref, send_ref, recv_ref, True):
            cp.wait_send()
            cp.wait_recv()

    return pl.pallas_call(
        body, name=name,
        in_specs=[HBM, HBM, SEM, SEM, ANY], out_specs=[HBM, HBM],
        out_shape=[pltpu.HBM(shard.shape, shard.dtype), pltpu.HBM(land.shape, land.dtype)],
        input_output_aliases={0: 0, 1: 1},
        compiler_params=pltpu.CompilerParams(has_side_effects=DATAFLOW),
    )(shard, land, send_sems, recv_sems, after)[1]


def _grad_copies(g_ref, land_ref, send_sems, recv_sems):
    x, y, c = _place()
    cps = []
    for d in range(1, 8):
        px, py, pc = x ^ (d >> 2), y ^ ((d >> 1) & 1), c ^ (d & 1)
        cps.append(pltpu.make_async_remote_copy(
            src_ref=g_ref.at[2 * px + py, pc], dst_ref=land_ref.at[d - 1], send_sem=send_sems.at[d - 1],
            recv_sem=recv_sems.at[d - 1], device_id=(px, py, pc), device_id_type=MESH))
    return cps


def _grads_start(grads_b, name):
    n = len(grads_b)
    lands = [lax.empty((7, g.shape[2], D), BF16) for g in grads_b]

    def body(*refs):
        g, land = refs[:n], refs[n:2 * n]
        send_sems, recv_sems = refs[2 * n:3 * n], refs[3 * n:4 * n]
        for k in range(n):
            for cp in _grad_copies(g[k], land[k], send_sems[k], recv_sems[k]):
                cp.start()
        refs[-1][...] = jnp.zeros_like(refs[-1])

    res = pl.pallas_call(
        body, name=name,
        in_specs=[HBM] * (2 * n), out_specs=[SEM] * (2 * n) + [HBM] * (2 * n) + [VMEM],
        out_shape=[pltpu.SemaphoreType.DMA((7,))] * (2 * n)
        + [pltpu.HBM(a.shape, a.dtype) for a in (*grads_b, *lands)] + [jax.ShapeDtypeStruct((8, 128), F32)],
        input_output_aliases={i: i + 2 * n for i in range(2 * n)},
        compiler_params=pltpu.CompilerParams(has_side_effects=DATAFLOW),
    )(*[_hbm(a) for a in (*grads_b, *lands)])
    return [(res[k], res[n + k], res[2 * n + k], res[3 * n + k]) for k in range(n)], res[-1]


def _grads_wait(started, after, name):
    n = len(started)

    def body(*refs):
        g, land = refs[:n], refs[n:2 * n]
        send_sems, recv_sems = refs[2 * n:3 * n], refs[3 * n:4 * n]
        for k in range(n):
            for cp in _grad_copies(g[k], land[k], send_sems[k], recv_sems[k]):
                cp.wait_send()
                cp.wait_recv()

    gs = [st[2] for st in started]
    lands = [st[3] for st in started]
    res = pl.pallas_call(
        body, name=name,
        in_specs=[HBM] * (2 * n) + [SEM] * (2 * n) + [ANY], out_specs=[HBM] * (2 * n),
        out_shape=[pltpu.HBM(a.shape, a.dtype) for a in (*gs, *lands)],
        input_output_aliases={i: i for i in range(2 * n)},
        compiler_params=pltpu.CompilerParams(has_side_effects=DATAFLOW),
    )(*gs, *lands, *[st[0] for st in started], *[st[1] for st in started], after)
    return res[n:]


def _sum_partials(grad4, got, cb, name, tr):
    h = grad4.shape[2]
    per_half = h // tr

    def body(cb_ref, g_ref, o_ref, out_ref):
        acc = g_ref[...]
        for j in range(7):
            acc = acc + o_ref[j].astype(F32)
        out_ref[...] = acc

    return pl.pallas_call(
        body, name=name,
        grid_spec=pltpu.PrefetchScalarGridSpec(
            num_scalar_prefetch=1, grid=(per_half,),
            in_specs=[pl.BlockSpec((None, None, tr, D), lambda i, cb_ref: (cb_ref[1], cb_ref[0], i, 0)),
                      pl.BlockSpec((7, tr, D), lambda i, cb_ref: (0, i, 0))],
            out_specs=pl.BlockSpec((tr, D), lambda i, cb_ref: (cb_ref[0] * per_half + i, 0))),
        out_shape=jax.ShapeDtypeStruct((2 * h, D), F32),
        compiler_params=_cp(("arbitrary",)),
    )(cb, grad4, got)


def _swap_halves(shards, name):
    n = len(shards)

    def body(*refs):
        out, send_sems, recv_sems = refs[n:2 * n], refs[2 * n], refs[2 * n + 1]
        x, y, c = _place()
        cps = []
        for k in range(n):
            h = shards[k].shape[0] // 2
            mine = out[k].at[pl.ds(pl.multiple_of(c * h, 8), h)]
            cp = pltpu.make_async_remote_copy(src_ref=mine, dst_ref=mine, send_sem=send_sems.at[k],
                                              recv_sem=recv_sems.at[k], device_id=(x, y, 1 - c), device_id_type=MESH)
            cp.start()
            cps.append(cp)
        for cp in cps:
            cp.wait()

    return pl.pallas_call(
        body, name=name,
        in_specs=[ANY] * n, out_specs=[ANY] * n,
        out_shape=[jax.ShapeDtypeStruct(sh.shape, F32) for sh in shards],
        input_output_aliases={k: k for k in range(n)},
        scratch_shapes=[pltpu.SemaphoreType.DMA((n,)), pltpu.SemaphoreType.DMA((n,))],
        compiler_params=pltpu.CompilerParams(has_side_effects=True),
    )(*shards)


def _small_copies(small_ref, land_ref, send_sems, recv_sems):
    x, y, c = _place()
    me = 4 * x + 2 * y + c
    cps = []
    for d in range(1, 8):
        px, py, pc = x ^ (d >> 2), y ^ ((d >> 1) & 1), c ^ (d & 1)
        cps.append(pltpu.make_async_remote_copy(
            src_ref=small_ref, dst_ref=land_ref.at[me], send_sem=send_sems.at[d - 1], recv_sem=recv_sems.at[d - 1],
            device_id=(px, py, pc), device_id_type=MESH))
    return cps


def _small_start(small):
    land = lax.empty((8,) + small.shape, F32)

    def body(s_ref, l_ref, send_sems, recv_sems, s_thru, l_thru, token):
        for cp in _small_copies(s_ref, l_ref, send_sems, recv_sems):
            cp.start()
        token[...] = jnp.zeros_like(token)

    res = pl.pallas_call(
        body, name="small_start",
        in_specs=[HBM, HBM], out_specs=[SEM, SEM, HBM, HBM, VMEM],
        out_shape=[pltpu.SemaphoreType.DMA((7,)), pltpu.SemaphoreType.DMA((7,)), pltpu.HBM(small.shape, F32),
                   pltpu.HBM(land.shape, F32), jax.ShapeDtypeStruct((8, 128), F32)],
        input_output_aliases={0: 2, 1: 3},
        compiler_params=pltpu.CompilerParams(has_side_effects=DATAFLOW),
    )(_hbm(small), _hbm(land))
    return res[:4], res[4]


def _small_wait(started, after):
    send_sems, recv_sems, small, land = started

    def body(s_ref, l_ref, send_ref, recv_ref, after_ref, s_out, l_out):
        for cp in _small_copies(s_ref, l_ref, send_ref, recv_ref):
            cp.wait_send()
            cp.wait_recv()

    return pl.pallas_call(
        body, name="small_wait",
        in_specs=[HBM, HBM, SEM, SEM, ANY], out_specs=[HBM, HBM],
        out_shape=[pltpu.HBM(small.shape, F32), pltpu.HBM(land.shape, F32)],
        input_output_aliases={0: 0, 1: 1},
        compiler_params=pltpu.CompilerParams(has_side_effects=DATAFLOW),
    )(small, land, send_sems, recv_sems, after)


def _small_sum(small, land, me):
    rows = small.shape[0]

    def body(me_ref, s_ref, l_ref, o_ref):
        acc = None
        for k in range(8):
            term = jnp.where(me_ref[0] == k, s_ref[...], l_ref[k])
            acc = term if k == 0 else acc + term
        o_ref[...] = acc

    return pl.pallas_call(
        body, name="small_sum",
        in_specs=[SMEM, VMEM, VMEM], out_specs=VMEM,
        out_shape=jax.ShapeDtypeStruct((rows, D), F32),
    )(me, small, land)


def _adamw(w, g, m, v, name, tr):
    rows, cols = w.shape

    def body(w_ref, g_ref, m_ref, v_ref, d_ref, nm_ref, nv_ref):
        g_ = g_ref[...]
        nm = ADAM_B1 * m_ref[...] + (1.0 - ADAM_B1) * g_
        nv = ADAM_B2 * v_ref[...] + (1.0 - ADAM_B2) * (g_ * g_)
        m_hat = nm / (1.0 - ADAM_B1 ** ADAM_STEP)
        v_hat = nv / (1.0 - ADAM_B2 ** ADAM_STEP)
        d_ref[...] = -ADAM_LR * (m_hat / (jnp.sqrt(v_hat) + ADAM_EPS) + ADAM_WD * w_ref[...])
        nm_ref[...] = nm
        nv_ref[...] = nv

    spec = pl.BlockSpec((tr, cols), lambda i: (i, 0))
    return pl.pallas_call(
        body, name=name, grid=(rows // tr,),
        in_specs=[spec] * 4, out_specs=[spec] * 3,
        out_shape=[jax.ShapeDtypeStruct((rows, cols), F32)] * 3,
        compiler_params=_cp(("parallel",)),
    )(w, g, m, v)


def _local_step(x, target, w_in_t, late_weights, norm_a_g, norm_b_g, sinks_a, ln1_g, ln1_b,
                conv_w, conv_b, ln2_g, ln2_b, slopes, on_grad, on_small):
    cwb = jnp.concatenate([conv_w, conv_b[None]], axis=0).reshape(4, 2, FF)

    proj, xb = _proj(x, w_in_t, "proj")
    o_a, lse_a = _attn_a_fwd(proj, sinks_a)
    fwd_b = [_attn_b_fwd(proj, slopes, r) for r in B_DILATIONS]
    w_o = late_weights(1, fwd_b[-1][1])
    o_b, lse_b, cat, z1, h1, h1b = _mix_ln1(x, o_a, [f[0] for f in fwd_b], [f[1] for f in fwd_b],
                                           norm_a_g, norm_b_g, w_o, ln1_g, ln1_b)
    w_up = late_weights(2, h1b)
    up, a, gate, a1 = _up_conv_gelu(h1b, w_up, cwb)
    w_down = late_weights(3, a)
    dz2, dz2b, st2 = _down_ln2_loss(a, w_down, h1, target, ln2_g, ln2_b)

    on_grad(3, *_grad_w(a, dz2b, "grad_w_down", tm=FF // 2))
    dup, dconv = _conv_gelu_bwd(_d_act(dz2b, w_down), up, gate, a1, cwb)
    on_grad(2, *_grad_w(dup, h1b, "grad_w_up", tm=FF // 2, lhs_halves=True))
    dz1, dz1b, st1 = _dh1_ln1_bwd(dz2, dup, w_up, z1, ln1_g)
    tok = on_grad(1, *_grad_w(cat, dz1b, "grad_w_o", tm=512))
    d_oa, d_ob, st_n = _dcat_rms_bwd(dz1b, w_o, o_a, o_b, norm_a_g + tok[0, 0], norm_b_g)
    dqa, dka, dva, dsink = _attn_a_bwd(proj, sinks_a, d_oa, o_a, lse_a)
    dconv = dconv.reshape(4, 2 * FF)
    tok = on_small(dict(loss=st2[2, 0:1], norm_a_g=st_n[0], norm_b_g=st_n[1], sinks_a=dsink[:, 0],
                        ln1_g=st1[0], ln1_b=st1[1], conv_w=dconv[0:3].reshape(-1), conv_b=dconv[3],
                        ln2_g=st2[0], ln2_b=st2[1]))
    slopes = slopes + tok[0, 0]
    bwd_b = None
    for r in reversed(B_DILATIONS):
        bwd_b = _attn_b_bwd(proj, slopes, d_ob, o_b, lse_b, r, bwd_b)
    dproj = _dproj_combine(dqa, dka, dva, bwd_b)
    tok = on_grad(0, *_grad_w(dproj, xb, "grad_w_in", tm=WA))
    return _grad_x(dz1, dproj, w_in_t, tok)


SMALL_ORDER = ("loss", "norm_a_g", "norm_b_g", "sinks_a", "ln1_g", "ln1_b", "conv_b", "ln2_g", "ln2_b", "conv_w")
SMALL_SIZES = dict(loss=1, norm_a_g=512, norm_b_g=512, sinks_a=8, ln1_g=D, ln1_b=D, conv_b=2 * FF, ln2_g=D, ln2_b=D,
                   conv_w=3 * 2 * FF)


def _pack(parts, rows):
    flat = jnp.concatenate([parts[k].reshape(-1).astype(F32) for k in parts])
    return jnp.pad(flat, (0, rows * D - flat.shape[0])).reshape(rows, D)


def _unpack(buf, names, sizes):
    flat = buf.reshape(-1)
    out, at = {}, 0
    for k in names:
        out[k] = flat[at:at + sizes[k]]
        at += sizes[k]
    return out


def kernel(x, w_in, norm_a_g, norm_b_g, sinks_a, w_o, ln1_g, ln1_b, w_up, conv_w, conv_b, w_down, ln2_g, ln2_b, loss_target, m_w_in, m_norm_a_g, m_norm_b_g, m_sinks_a, m_w_o, m_ln1_g, m_ln1_b, m_w_up, m_conv_w, m_conv_b, m_w_down, m_ln2_g, m_ln2_b, v_w_in, v_norm_a_g, v_norm_b_g, v_sinks_a, v_w_o, v_ln1_g, v_ln1_b, v_w_up, v_conv_w, v_conv_b, v_w_down, v_ln2_g, v_ln2_b):
    xi, yi, ci = _place()
    chip = (2 * xi + yi).astype(I32)
    core = ci.astype(I32)

    w_in_rows, m_w_in_rows, v_w_in_rows = w_in.T, m_w_in.T, v_w_in.T
    shards = (w_in_rows.astype(BF16), w_o.astype(BF16), w_up.astype(BF16), w_down.astype(BF16))
    w_in_t, conv_w4 = _gather_w_in(shards[0], conv_w)
    conv_w_f = conv_w4.transpose(1, 0, 2).reshape(3, 2 * FF)
    w_started, w_tok = _weights_start(shards[1:], conv_w4)
    slopes = jnp.asarray(SLOPES, F32) + w_tok[0, 0]

    halves_rows = [r // 2 for r in SHARD_ROWS]
    grads4, grads_b4, started = [None] * 4, [None] * 4, [None] * 4

    def on_grad(k, g, g_b):
        grads4[k] = g.reshape(N_CHIPS, 2, halves_rows[k], D)
        grads_b4[k] = g_b.reshape(N_CHIPS, 2, halves_rows[k], D)
        if k > 1:
            return None
        group = (1, 2, 3) if k == 1 else (0,)
        sts, tok = _grads_start([grads_b4[i] for i in group], f"grads_start_{k}")
        for i, st in zip(group, sts):
            started[i] = st
        return tok

    small_rows = 32
    small_started = []

    def on_small(parts):
        st, tok = _small_start(_pack({k: parts[k] for k in SMALL_ORDER}, small_rows))
        small_started.append(st)
        return tok

    gx = _local_step(
        x[0], loss_target[0], w_in_t, lambda k, after: _weights_wait(w_started[k - 1], after, f"weights_wait_{k}"),
        norm_a_g, norm_b_g, sinks_a, ln1_g, ln1_b, conv_w_f, conv_b, ln2_g, ln2_b, slopes, on_grad, on_small)

    tiles = (96, 128, 352, 176)
    core_chip = jnp.stack([core, chip])
    got = _grads_wait(started[1:], gx, "grads_wait_1")
    halves = [_sum_partials(grads4[k], got[k - 1], core_chip, f"sum_partials_{k}", tiles[k]) for k in (1, 2, 3)]
    g_w_o, g_w_up_rows, g_w_down = _swap_halves(halves, "swap_halves")
    g_w_up = g_w_up_rows.T
    delta, new_m, new_v = {}, {}, {}
    for k, g, tr in (("w_o", g_w_o, 128), ("w_up", g_w_up, 256), ("w_down", g_w_down, 176)):
        delta[k], new_m[k], new_v[k] = _adamw(dict(w_o=w_o, w_up=w_up, w_down=w_down)[k], g,
                                              dict(w_o=m_w_o, w_up=m_w_up, w_down=m_w_down)[k],
                                              dict(w_o=v_w_o, w_up=v_w_up, w_down=v_w_down)[k], f"adamw_{k}", tr)

    got = _grads_wait(started[:1], delta["w_up"], "grads_wait_0")
    half_in = _sum_partials(grads4[0], got[0], core_chip, "sum_partials_0", tiles[0])
    (g_w_in_rows,) = _swap_halves([half_in], "swap_halves_in")
    small_mine, small_land = _small_wait(small_started[0], g_w_in_rows)
    totals = _small_sum(small_mine, small_land, (4 * xi + 2 * yi + ci).astype(I32).reshape(1))
    tot = _unpack(totals, SMALL_ORDER, SMALL_SIZES)
    loss = tot["loss"][0]
    cols = 2 * FF // N_CHIPS
    g_conv_w = lax.dynamic_slice(tot["conv_w"].reshape(3, 2 * FF), (0, chip * cols), (3, cols))
    g_small = dict(norm_a_g=tot["norm_a_g"], norm_b_g=tot["norm_b_g"], sinks_a=tot["sinks_a"], ln1_g=tot["ln1_g"],
                   ln1_b=tot["ln1_b"], conv_w=g_conv_w, conv_b=tot["conv_b"], ln2_g=tot["ln2_g"], ln2_b=tot["ln2_b"])

    weights = dict(w_in=w_in, norm_a_g=norm_a_g, norm_b_g=norm_b_g, sinks_a=sinks_a, w_o=w_o, ln1_g=ln1_g, ln1_b=ln1_b,
                   w_up=w_up, conv_w=conv_w, conv_b=conv_b, w_down=w_down, ln2_g=ln2_g, ln2_b=ln2_b)
    ms = dict(w_in=m_w_in, norm_a_g=m_norm_a_g, norm_b_g=m_norm_b_g, sinks_a=m_sinks_a, w_o=m_w_o, ln1_g=m_ln1_g,
              ln1_b=m_ln1_b, w_up=m_w_up, conv_w=m_conv_w, conv_b=m_conv_b, w_down=m_w_down, ln2_g=m_ln2_g, ln2_b=m_ln2_b)
    vs = dict(w_in=v_w_in, norm_a_g=v_norm_a_g, norm_b_g=v_norm_b_g, sinks_a=v_sinks_a, w_o=v_w_o, ln1_g=v_ln1_g,
              ln1_b=v_ln1_b, w_up=v_w_up, conv_w=v_conv_w, conv_b=v_conv_b, w_down=v_w_down, ln2_g=v_ln2_g, ln2_b=v_ln2_b)
    order = list(weights)
    grad = dict(g_small, w_in=g_w_in_rows.T, w_o=g_w_o, w_up=g_w_up, w_down=g_w_down)

    delta["w_in"], new_m["w_in"], new_v["w_in"] = [
        a.T for a in _adamw(w_in_rows, g_w_in_rows, m_w_in_rows, v_w_in_rows, "adamw_w_in", 144)]
    small_names = [k for k in order if k not in delta]
    sizes = {k: weights[k].size for k in small_names}
    rows = 16
    packed = [_pack({k: src[k] for k in small_names}, rows) for src in (weights, grad, ms, vs)]
    for res, buf in zip((delta, new_m, new_v), _adamw(*packed, "adamw_small", rows)):
        for k, val in _unpack(buf, small_names, sizes).items():
            res[k] = val.reshape(weights[k].shape)

    return (loss, gx[None], *[grad[k] for k in order], *[delta[k] for k in order],
            *[new_m[k] for k in order], *[new_v[k] for k in order])
```

```python
import functools
import math

import jax
import jax.numpy as jnp
from jax import lax
from jax.experimental import pallas as pl
from jax.experimental.pallas import tpu as pltpu

F32, BF16, I32 = jnp.float32, jnp.bfloat16, jnp.int32

D = 1024
FF = 2816
HD = 64
NH = 8
WA, WB = 768, 1536
WIN = WA + WB
BLK = 128
ALPHA = 2.0 ** 0.25
LN_EPS, RMS_EPS = 1e-5, 1e-6
SCALE = 1.0 / math.sqrt(HD)
A_MAX_DIST, B_MAX_DIST = 127, 128
B_DILATIONS = (1, 4, 16)
SLOPES = tuple(2.0 ** (-(i + 1)) for i in range(NH))
SHARD_ROWS = (WIN // 4, D // 4, 2 * FF // 4, FF // 4)
N_CHIPS = 4
ADAM_LR, ADAM_B1, ADAM_B2, ADAM_EPS, ADAM_WD, ADAM_STEP = 0.001, 0.9, 0.999, 1e-08, 0.01, 10
MESH = pl.DeviceIdType.MESH
ANY = pl.BlockSpec(memory_space=pl.ANY)
SMEM = pl.BlockSpec(memory_space=pltpu.SMEM)
VMEM = pl.BlockSpec(memory_space=pltpu.VMEM)
HBM = pl.BlockSpec(memory_space=pltpu.HBM)
SEM = pl.BlockSpec(memory_space=pltpu.SEMAPHORE)
DATAFLOW = pltpu.SideEffectType.DATAFLOW_SIDE_EFFECTING


def _cp(sem, mb=48):
    return pltpu.CompilerParams(dimension_semantics=sem, vmem_limit_bytes=mb << 20)


def _nn(a, b):
    return lax.dot_general(a, b, (((1,), (0,)), ((), ())), preferred_element_type=F32)


def _nt(a, b):
    return lax.dot_general(a, b, (((1,), (1,)), ((), ())), preferred_element_type=F32)


def _tn(a, b):
    return lax.dot_general(a, b, (((0,), (0,)), ((), ())), preferred_element_type=F32)


def _resident(shape):
    n = len(shape)
    return pl.BlockSpec(shape, lambda *_: (0,) * n, pipeline_mode=pl.Buffered(1))


def _const(shape):
    n = len(shape)
    return pl.BlockSpec(shape, lambda *_: (0,) * n)


def _proj(x, w_t, name, tm=512):
    s = x.shape[0]
    n = w_t.shape[0]

    def body(x_ref, w_ref, o_ref, xb_ref):
        xb = x_ref[...].astype(BF16)
        xb_ref[...] = xb
        res = _nt(xb, w_ref[...])
        for g in range(n // 128):
            o_ref[g] = res[:, 128 * g:128 * (g + 1)]

    return pl.pallas_call(
        body, name=name, grid=(s // tm,),
        in_specs=[pl.BlockSpec((tm, D), lambda i: (i, 0)), _resident((n, D))],
        out_specs=[pl.BlockSpec((n // 128, tm, 128), lambda i: (0, i, 0)), pl.BlockSpec((tm, D), lambda i: (i, 0))],
        out_shape=[jax.ShapeDtypeStruct((n // 128, s, 128), F32), jax.ShapeDtypeStruct((s, D), BF16)],
        compiler_params=_cp(("parallel",)),
    )(x, w_t)


def _grad_w(lhs, rhs, name, tm, tk=2048, lhs_halves=False):
    s = rhs.shape[0]
    if lhs_halves:
        per_half = lhs.shape[2] // tm
        n = 2 * lhs.shape[2]
        lhs_spec = pl.BlockSpec((None, tk, tm), lambda i, k: (i // per_half, k, i % per_half))
    else:
        n = lhs.shape[1]
        lhs_spec = pl.BlockSpec((tk, tm), lambda i, k: (k, i))
    nk = s // tk

    def body(l_ref, r_ref, o_ref, ob_ref):
        k = pl.program_id(1)

        @pl.when(k == 0)
        def _():
            o_ref[...] = jnp.zeros_like(o_ref)

        o_ref[...] += _tn(l_ref[...], r_ref[...])

        @pl.when(k == nk - 1)
        def _():
            ob_ref[...] = o_ref[...].astype(BF16)

    return pl.pallas_call(
        body, name=name, grid=(n // tm, nk),
        in_specs=[lhs_spec, pl.BlockSpec((tk, D), lambda i, k: (k, 0))],
        out_specs=[pl.BlockSpec((tm, D), lambda i, k: (i, 0))] * 2,
        out_shape=[jax.ShapeDtypeStruct((n, D), F32), jax.ShapeDtypeStruct((n, D), BF16)],
        compiler_params=_cp(("parallel", "arbitrary")),
    )(lhs, rhs)


def _band_base(max_dist, dist_unit, first):
    row = lax.broadcasted_iota(I32, (BLK, 2 * BLK), 0)
    col = lax.broadcasted_iota(I32, (BLK, 2 * BLK), 1)
    dist = BLK + row - col
    ok = (dist >= 0) & (dist <= max_dist)
    if first:
        ok = ok & (col >= BLK)
    return jnp.where(ok, dist.astype(F32) * (-float(dist_unit)), -jnp.inf)


def _half_mask(shape, e):
    lane = lax.broadcasted_iota(I32, shape, 1)
    return (lane < HD) if e == 0 else (lane >= HD)


def _to_half(x, e, g):
    if g != e:
        x = pltpu.roll(x, HD, 1)
    return jnp.where(_half_mask(x.shape, g), x, 0.0)


def _stack_heads(scalars, tile):
    return jnp.concatenate([scalars[0] * tile, scalars[1] * tile], axis=0)


def _pair_fwd(q2, kb, vb, base, slopes, kv_heads, sinks):
    lo = _half_mask((BLK, 2 * HD), 0)
    if slopes is None:
        bias = base
    elif sinks is None:
        bias = _stack_heads(slopes, base)
    else:
        col0 = lax.broadcasted_iota(I32, base.shape, 1) == 0
        bias = jnp.concatenate([jnp.where(col0, sinks[e], slopes[e] * base) for e in (0, 1)], axis=0)
    qs = jnp.concatenate([_to_half(q2, e, kv_heads[e]) * SCALE for e in (0, 1)], axis=0).astype(BF16)
    s = _nt(qs, kb) + bias
    m = jnp.max(s, axis=1, keepdims=True)
    p = jnp.exp(s - m)
    l = jnp.sum(p, axis=1, keepdims=True)
    o = _nn(p.astype(BF16), vb) / l
    lse = m + jnp.log(l)
    halves = []
    for e in (0, 1):
        oh = o[e * BLK:(e + 1) * BLK]
        halves.append(pltpu.roll(oh, HD, 1) if kv_heads[e] != e else oh)
    o2 = jnp.where(lo, halves[0], halves[1])
    lse2 = jnp.where(lo, jnp.broadcast_to(lse[:BLK], (BLK, 2 * HD)), jnp.broadcast_to(lse[BLK:], (BLK, 2 * HD)))
    return o2, lse2


def _pair_bwd(q2, kb, vb, do2, o2, lse2, base, slopes, kv_heads, sinks):
    lo = _half_mask((BLK, 2 * HD), 0)
    prod = do2 * o2
    lses, deltas = [], []
    for e in (0, 1):
        hq = _half_mask((BLK, 2 * HD), e)
        lses.append(jnp.max(jnp.where(hq, lse2, -jnp.inf), axis=1, keepdims=True))
        deltas.append(jnp.sum(jnp.where(hq, prod, 0.0), axis=1, keepdims=True))
    lse = jnp.concatenate(lses, axis=0)
    delta = jnp.concatenate(deltas, axis=0)
    qs = jnp.concatenate([_to_half(q2, e, kv_heads[e]) * SCALE for e in (0, 1)], axis=0).astype(BF16)
    dos = jnp.concatenate([_to_half(do2, e, kv_heads[e]) for e in (0, 1)], axis=0).astype(BF16)
    p = jnp.exp(_nt(qs, kb) + (base if slopes is None else _stack_heads(slopes, base)) - lse)
    ds = (p * (_nt(dos, vb) - delta)).astype(BF16)
    dq = _nn(ds, kb) * SCALE
    halves = []
    for e in (0, 1):
        dqh = dq[e * BLK:(e + 1) * BLK]
        halves.append(pltpu.roll(dqh, HD, 1) if kv_heads[e] != e else dqh)
    dq2 = jnp.where(lo, halves[0], halves[1])
    dk2 = _tn(ds, qs)
    dv2 = _tn(p.astype(BF16), dos)
    dsinks = []
    if sinks is not None:
        for e in (0, 1):
            dsinks.append(jnp.sum(-jnp.exp(sinks[e] - lses[e]) * deltas[e], axis=0, keepdims=True))
    return dq2, dk2, dv2, dsinks


A_BLOCKS_PER_STEP = 2
A_BLOCKS_PER_STEP_BWD = 1


def _attn_a_fwd(proj, sinks):
    s = proj.shape[1]
    nq = A_BLOCKS_PER_STEP
    rows = BLK * nq
    steps = s // rows

    def body(sink_ref, q_ref, kp_ref, kc_ref, vp_ref, vc_ref, o_ref, lse_ref):
        n = pl.program_id(0)
        base_rest = _band_base(A_MAX_DIST, 1, False)
        base_0 = jnp.where(n > 0, base_rest, _band_base(A_MAX_DIST, 1, True))
        for i in range(nq):
            cur = pl.ds(i * BLK, BLK)
            k_prev = kc_ref[pl.ds((i - 1) * BLK, BLK), :] if i > 0 else kp_ref[...]
            v_prev = vc_ref[pl.ds((i - 1) * BLK, BLK), :] if i > 0 else vp_ref[...]
            first_key = lax.broadcasted_iota(I32, (2 * BLK, 128), 0) == 0
            kb = jnp.where(first_key, 0.0, jnp.concatenate([k_prev, kc_ref[cur, :]], axis=0)).astype(BF16)
            vb = jnp.where(first_key, 0.0, jnp.concatenate([v_prev, vc_ref[cur, :]], axis=0)).astype(BF16)
            for j in range(NH // 2):
                g = j // 2
                o2, lse2 = _pair_fwd(q_ref[j, cur, :], kb, vb, base_rest if i > 0 else base_0,
                                     (SLOPES[2 * j], SLOPES[2 * j + 1]), (g, g), (sink_ref[2 * j], sink_ref[2 * j + 1]))
                o_ref[j, cur, :] = o2
                lse_ref[j, cur, :] = lse2

    before = lambda n: jnp.maximum(n * nq - 1, 0)
    slab = lambda g: pl.BlockSpec((None, rows, 128), lambda n: (g, n, 0))
    edge = lambda g: pl.BlockSpec((None, BLK, 128), lambda n: (g, before(n), 0))
    quad = pl.BlockSpec((4, rows, 128), lambda n: (0, n, 0))
    return pl.pallas_call(
        body, name="attn_a_fwd", grid=(steps,),
        in_specs=[SMEM, quad, edge(4), slab(4), edge(5), slab(5)],
        out_specs=[quad, quad],
        out_shape=[jax.ShapeDtypeStruct((4, s, 128), F32)] * 2,
        compiler_params=_cp(("parallel",)),
    )(sinks, proj, proj, proj, proj, proj)


def _attn_a_bwd(proj, sinks, d_o, o, lse):
    s = proj.shape[1]
    nq = A_BLOCKS_PER_STEP_BWD
    rows = BLK * nq
    steps = s // rows

    def body(sink_ref, q_ref, kp_ref, kc_ref, vp_ref, vc_ref, do_ref, o_ref, lse_ref,
             dq_ref, dk_ref, dv_ref, dsink_ref, kcar, vcar):
        n = pl.program_id(0)

        @pl.when(n == 0)
        def _():
            kcar[...] = jnp.zeros_like(kcar)
            vcar[...] = jnp.zeros_like(vcar)
            dsink_ref[...] = jnp.zeros_like(dsink_ref)

        dk_ref[...] = kcar[...]
        dv_ref[...] = vcar[...]

        @pl.when(n < steps)
        def _():
            base_rest = _band_base(A_MAX_DIST, 1, False)
            base_0 = jnp.where(n > 0, base_rest, _band_base(A_MAX_DIST, 1, True))
            for i in range(nq):
                cur = pl.ds(i * BLK, BLK)
                k_prev = kc_ref[pl.ds((i - 1) * BLK, BLK), :] if i > 0 else kp_ref[...]
                v_prev = vc_ref[pl.ds((i - 1) * BLK, BLK), :] if i > 0 else vp_ref[...]
                kb = jnp.concatenate([k_prev, kc_ref[cur, :]], axis=0).astype(BF16)
                vb = jnp.concatenate([v_prev, vc_ref[cur, :]], axis=0).astype(BF16)
                dk_win = dv_win = None
                for j in range(NH // 2):
                    g = j // 2
                    dq2, dk2, dv2, dsk = _pair_bwd(q_ref[j, cur, :], kb, vb, do_ref[j, cur, :], o_ref[j, cur, :],
                                                   lse_ref[j, cur, :], base_rest if i > 0 else base_0,
                                                   (SLOPES[2 * j], SLOPES[2 * j + 1]), (g, g),
                                                   (sink_ref[2 * j], sink_ref[2 * j + 1]))
                    dq_ref[j, cur, :] = dq2
                    dk_win = dk2 if j == 0 else dk_win + dk2
                    dv_win = dv2 if j == 0 else dv_win + dv2
                    for e in (0, 1):
                        h = 2 * j + e
                        dsink_ref[h:h + 1, :] += jnp.broadcast_to(dsk[e], (1, 128))
                if i == 0:
                    last = pl.ds((nq - 1) * BLK, BLK)
                    dk_ref[last, :] += dk_win[:BLK]
                    dv_ref[last, :] += dv_win[:BLK]
                else:
                    kcar[pl.ds((i - 1) * BLK, BLK), :] += dk_win[:BLK]
                    vcar[pl.ds((i - 1) * BLK, BLK), :] += dv_win[:BLK]
                kcar[cur, :] = dk_win[BLK:]
                vcar[cur, :] = dv_win[BLK:]

    cur_step = lambda n: jnp.minimum(n, steps - 1)
    before = lambda n: jnp.maximum(cur_step(n) * nq - 1, 0)
    out_prev = lambda n: jnp.maximum(n - 1, 0)
    quad = pl.BlockSpec((4, rows, 128), lambda n: (0, cur_step(n), 0))
    slab = lambda g: pl.BlockSpec((None, rows, 128), lambda n: (g, cur_step(n), 0))
    edge = lambda g: pl.BlockSpec((None, BLK, 128), lambda n: (g, before(n), 0))
    return pl.pallas_call(
        body, name="attn_a_bwd", grid=(steps + 1,),
        in_specs=[SMEM, quad, edge(4), slab(4), edge(5), slab(5), quad, quad, quad],
        out_specs=[quad,
                   pl.BlockSpec((rows, 128), lambda n: (out_prev(n), 0)),
                   pl.BlockSpec((rows, 128), lambda n: (out_prev(n), 0)),
                   pl.BlockSpec((NH, 128), lambda n: (0, 0))],
        out_shape=[jax.ShapeDtypeStruct((4, s, 128), F32), jax.ShapeDtypeStruct((s, 128), F32),
                   jax.ShapeDtypeStruct((s, 128), F32), jax.ShapeDtypeStruct((NH, 128), F32)],
        scratch_shapes=[pltpu.VMEM((rows, 128), F32), pltpu.VMEM((rows, 128), F32)],
        compiler_params=_cp(("arbitrary",)),
    )(sinks, proj, proj, proj, proj, proj, d_o, o, lse)


def _stream(rho, i, r):
    start = i * BLK * r + rho
    return pl.ds(start, BLK, stride=r) if r > 1 else pl.ds(start, BLK)


def _for_streams(r, fn, side_by_side=4):
    if r <= side_by_side:
        for rho in range(r):
            fn(rho)
    else:
        def group(it, carry):
            for u in range(side_by_side):
                fn(side_by_side * it + u)
            return carry

        lax.fori_loop(0, r // side_by_side, group, 0)


B_BLOCKS_PER_STEP = {1: 8, 4: 2, 16: 1}
B_BLOCKS_PER_STEP_FWD = {1: 16, 4: 4, 16: 1}


def _attn_b_fwd(proj, slopes, r):
    s = proj.shape[1]
    nq = B_BLOCKS_PER_STEP_FWD[r]
    rows = BLK * r * nq
    steps = s // rows
    qc, kc, vc = WA // 128, WA // 128 + 4, WA // 128 + 8

    def body(slope_ref, q_ref, kp_ref, kc_ref, vp_ref, vc_ref, o_ref, lse_ref):
        j = pl.program_id(0)
        sb = pl.program_id(1)
        sl2 = (slope_ref[2 * j], slope_ref[2 * j + 1])
        bias_rest = _stack_heads(sl2, _band_base(B_MAX_DIST, r, False))
        bias_0 = jnp.where(sb > 0, bias_rest, _stack_heads(sl2, _band_base(B_MAX_DIST, r, True)))

        def stream(rho):
            for i in range(nq):
                cur = _stream(rho, i, r)
                k_prev = kc_ref[_stream(rho, i - 1, r), :] if i > 0 else kp_ref[_stream(rho, 0, r), :]
                v_prev = vc_ref[_stream(rho, i - 1, r), :] if i > 0 else vp_ref[_stream(rho, 0, r), :]
                kb = jnp.concatenate([k_prev, kc_ref[cur, :]], axis=0).astype(BF16)
                vb = jnp.concatenate([v_prev, vc_ref[cur, :]], axis=0).astype(BF16)
                o2, lse2 = _pair_fwd(q_ref[cur, :], kb, vb, bias_rest if i > 0 else bias_0, None, (0, 1), None)
                o_ref[cur, :] = o2
                lse_ref[cur, :] = lse2

        _for_streams(r, stream, side_by_side=16)

    before = lambda sb: jnp.maximum(sb * nq - 1, 0)
    return pl.pallas_call(
        body, name=f"attn_b_fwd_r{r}", grid=(NH // 2, steps),
        in_specs=[SMEM,
                  pl.BlockSpec((None, rows, 128), lambda j, sb: (qc + j, sb, 0)),
                  pl.BlockSpec((None, BLK * r, 128), lambda j, sb: (kc + j, before(sb), 0)),
                  pl.BlockSpec((None, rows, 128), lambda j, sb: (kc + j, sb, 0)),
                  pl.BlockSpec((None, BLK * r, 128), lambda j, sb: (vc + j, before(sb), 0)),
                  pl.BlockSpec((None, rows, 128), lambda j, sb: (vc + j, sb, 0))],
        out_specs=[pl.BlockSpec((None, rows, 128), lambda j, sb: (j, sb, 0))] * 2,
        out_shape=[jax.ShapeDtypeStruct((4, s, 128), F32)] * 2,
        compiler_params=_cp(("parallel", "parallel")),
    )(slopes, proj, proj, proj, proj, proj)


def _attn_b_bwd(proj, slopes, d_o, o, lse, r, so_far=None):
    s = proj.shape[1]
    nq = B_BLOCKS_PER_STEP[r]
    rows = BLK * r * nq
    steps = s // rows
    qc, kc, vc = WA // 128, WA // 128 + 4, WA // 128 + 8
    chained = so_far is not None

    def body(slope_ref, q_ref, kp_ref, kc_ref, vp_ref, vc_ref, do_ref, o_ref, lse_ref, *rest):
        if chained:
            pq_ref, pk_ref, pv_ref, dq_ref, dk_ref, dv_ref, kcar, vcar = rest
        else:
            dq_ref, dk_ref, dv_ref, kcar, vcar = rest
        j = pl.program_id(0)
        sb = pl.program_id(1)

        @pl.when(sb == 0)
        def _():
            kcar[...] = jnp.zeros_like(kcar)
            vcar[...] = jnp.zeros_like(vcar)

        if chained:
            dk_ref[...] = kcar[...] + pk_ref[...]
            dv_ref[...] = vcar[...] + pv_ref[...]
        else:
            dk_ref[...] = kcar[...]
            dv_ref[...] = vcar[...]

        @pl.when(sb < steps)
        def _():
            sl2 = (slope_ref[2 * j], slope_ref[2 * j + 1])
            bias_rest = _stack_heads(sl2, _band_base(B_MAX_DIST, r, False))
            bias_0 = jnp.where(sb > 0, bias_rest, _stack_heads(sl2, _band_base(B_MAX_DIST, r, True)))

            def stream(rho):
                for i in range(nq):
                    cur = _stream(rho, i, r)
                    k_prev = kc_ref[_stream(rho, i - 1, r), :] if i > 0 else kp_ref[_stream(rho, 0, r), :]
                    v_prev = vc_ref[_stream(rho, i - 1, r), :] if i > 0 else vp_ref[_stream(rho, 0, r), :]
                    kb = jnp.concatenate([k_prev, kc_ref[cur, :]], axis=0).astype(BF16)
                    vb = jnp.concatenate([v_prev, vc_ref[cur, :]], axis=0).astype(BF16)
                    dq2, dk2, dv2, _ = _pair_bwd(q_ref[cur, :], kb, vb, do_ref[cur, :], o_ref[cur, :], lse_ref[cur, :],
                                                 bias_rest if i > 0 else bias_0, None, (0, 1), None)
                    dq_ref[cur, :] = dq2 + pq_ref[cur, :] if chained else dq2
                    if i == 0:
                        last = _stream(rho, nq - 1, r)
                        dk_ref[last, :] += dk2[:BLK]
                        dv_ref[last, :] += dv2[:BLK]
                    else:
                        kcar[_stream(rho, i - 1, r), :] += dk2[:BLK]
                        vcar[_stream(rho, i - 1, r), :] += dv2[:BLK]
                    kcar[cur, :] = dk2[BLK:]
                    vcar[cur, :] = dv2[BLK:]

            _for_streams(r, stream, side_by_side=8)

    cur_step = lambda sb: jnp.minimum(sb, steps - 1)
    before = lambda sb: jnp.maximum(cur_step(sb) * nq - 1, 0)
    out_prev = lambda sb: jnp.maximum(sb - 1, 0)
    tile = lambda slab: pl.BlockSpec((None, rows, 128), lambda j, sb: (slab + j, cur_step(sb), 0))
    edge = lambda slab: pl.BlockSpec((None, BLK * r, 128), lambda j, sb: (slab + j, before(sb), 0))
    late = pl.BlockSpec((None, rows, 128), lambda j, sb: (j, out_prev(sb), 0))
    grads = [tile(0), late, late]
    return pl.pallas_call(
        body, name=f"attn_b_bwd_r{r}", grid=(NH // 2, steps + 1),
        in_specs=[SMEM, tile(qc), edge(kc), tile(kc), edge(vc), tile(vc), tile(0), tile(0), tile(0)]
        + (grads if chained else []),
        out_specs=grads,
        out_shape=[jax.ShapeDtypeStruct((4, s, 128), F32)] * 3,
        scratch_shapes=[pltpu.VMEM((rows, 128), F32), pltpu.VMEM((rows, 128), F32)],
        compiler_params=_cp(("parallel", "arbitrary")),
    )(slopes, proj, proj, proj, proj, proj, d_o, o, lse, *(so_far if chained else ()))


def _row(v):
    return v.reshape(1, -1)


def _layer_norm_stats(z):
    mu = jnp.mean(z, axis=-1, keepdims=True)
    zc = z - mu
    var = jnp.mean(zc * zc, axis=-1, keepdims=True)
    rstd = lax.rsqrt(var + LN_EPS)
    return zc * rstd, rstd


def _layer_norm_bwd(dh, zh, rstd, g):
    dzh = dh * g
    return rstd * (dzh - jnp.mean(dzh, axis=-1, keepdims=True) - zh * jnp.mean(dzh * zh, axis=-1, keepdims=True))


def _rms(o):
    return lax.rsqrt(jnp.mean(o * o, axis=-1, keepdims=True) + RMS_EPS)


def _mix_ln1(x, o_a, o_b, lse_b, norm_a_g, norm_b_g, w_o, ln1_g, ln1_b, tm=256):
    s = x.shape[0]

    def wide(ref):
        return jnp.concatenate([ref[j] for j in range(4)], axis=1)

    def body(x_ref, oa_ref, ob1, ob2, ob3, l1, l2, l3, ga_ref, gb_ref, wo_ref, g_ref, b_ref,
             obm_ref, lse_ref, cat_ref, z1_ref, h1_ref, h1b_ref):
        la, lb, lc = wide(l1), wide(l2), wide(l3)
        m = jnp.maximum(jnp.maximum(la, lb), lc)
        ea, eb, ec = jnp.exp(la - m), jnp.exp(lb - m), jnp.exp(lc - m)
        den = ea + eb + ec
        obm = (ea / den) * wide(ob1) + (eb / den) * wide(ob2) + (ec / den) * wide(ob3)
        lse = m + jnp.log(den)
        for j in range(4):
            obm_ref[j] = obm[:, 128 * j:128 * (j + 1)]
            lse_ref[j] = lse[:, 128 * j:128 * (j + 1)]
        oa = wide(oa_ref)
        na = oa * _rms(oa) * ga_ref[...]
        nb_ = obm * _rms(obm) * gb_ref[...]
        cat = jnp.concatenate([na, nb_], axis=1).astype(BF16)
        cat_ref[...] = cat
        z1 = ALPHA * x_ref[...] + _nn(cat, wo_ref[...])
        z1_ref[...] = z1
        zh, _ = _layer_norm_stats(z1)
        h1 = zh * g_ref[...] + b_ref[...]
        h1_ref[...] = h1
        h1b_ref[...] = h1.astype(BF16)

    t512 = pl.BlockSpec((4, tm, 128), lambda i: (0, i, 0))
    td = pl.BlockSpec((tm, D), lambda i: (i, 0))
    return pl.pallas_call(
        body, name="mix_ln1", grid=(s // tm,),
        in_specs=[td] + [t512] * 7 + [_const((1, 512))] * 2 + [_resident((D, D))] + [_const((1, D))] * 2,
        out_specs=[t512, t512, td, td, td, td],
        out_shape=[jax.ShapeDtypeStruct((4, s, 128), F32), jax.ShapeDtypeStruct((4, s, 128), F32),
                   jax.ShapeDtypeStruct((s, D), BF16), jax.ShapeDtypeStruct((s, D), F32),
                   jax.ShapeDtypeStruct((s, D), F32), jax.ShapeDtypeStruct((s, D), BF16)],
        compiler_params=_cp(("parallel",)),
    )(x, o_a, *o_b, *lse_b, _row(norm_a_g), _row(norm_b_g), w_o, _row(ln1_g), _row(ln1_b))


def _gelu_and_grad(x):
    c = math.sqrt(2.0 / math.pi)
    x2 = x * x
    cx = c * x
    t = jnp.tanh(cx * (1.0 + 0.044715 * x2))
    q = 1.0 + t
    g = (0.5 * x) * q
    dg = 0.5 * q + ((0.5 * cx) * (1.0 - t * t)) * (1.0 + (3.0 * 0.044715) * x2)
    return g, dg


def _shift_down(u, before):
    n = u.shape[0]
    ext = jnp.concatenate([before, u], axis=0)
    return pltpu.roll(ext, 1, 0)[8:], pltpu.roll(ext, 2, 0)[8:]


def _shift_up(u, after):
    n = u.shape[0]
    ext = jnp.concatenate([u, after], axis=0)
    return pltpu.roll(ext, n + 7, 0)[:n], pltpu.roll(ext, n + 6, 0)[:n]


def _up_conv_gelu(h1b, w_up, cwb, tm=256, tn=FF // 2, chunk_rows=16, piece_cols=512):
    s = h1b.shape[0]
    n_i = s // tm
    n_t = (FF // tn) * n_i

    def body(h_ref, wg_ref, wv_ref, c_ref, up_ref, a_ref, g_ref, a1_ref, pend_a, pend_b, carry):
        t = pl.program_id(0)
        row_tile = jnp.maximum(t - 1, 0) % n_i
        w_refs = (wg_ref, wv_ref)

        @pl.when(t == 0)
        def _():
            pend_b[...] = jnp.zeros_like(pend_b)
            carry[...] = jnp.zeros_like(carry)

        def step(dst, src):
            def chunk(c, before):
                rows = pl.ds(c * chunk_rows, chunk_rows)
                u, last = [], []
                for half in (0, 1):
                    up = src[half, rows, :]
                    r1, r2 = _shift_down(up, before[half])
                    u.append(r2 * c_ref[0, half:half + 1, :] + r1 * c_ref[1, half:half + 1, :]
                             + up * c_ref[2, half:half + 1, :] + c_ref[3, half:half + 1, :])
                    last.append(up[chunk_rows - 8:])
                g, dg = _gelu_and_grad(u[0])
                a_ref[rows, :] = (g * u[1]).astype(BF16)
                g_ref[rows, :] = g.astype(BF16)
                a1_ref[rows, :] = (u[1] * dg).astype(BF16)
                return tuple(last)

            edge = tuple(jnp.where(row_tile > 0, carry[half], 0.0) for half in (0, 1))
            pieces = [(half, c0, min(piece_cols, tn - c0)) for half in (0, 1) for c0 in range(0, tn, piece_cols)]
            n_c = tm // chunk_rows
            done = 0
            for p, (half, c0, width) in enumerate(pieces):
                cols = slice(c0, c0 + width)
                up = _nn(h_ref[...], w_refs[half][:, cols])
                up_ref[half, :, cols] = up.astype(BF16)
                dst[half, :, cols] = up
                upto = n_c * (p + 1) // len(pieces)
                for c in range(done, upto):
                    edge = chunk(c, edge)
                done = upto
            for half in (0, 1):
                carry[half] = edge[half]

        @pl.when(t % 2 == 0)
        def _():
            step(pend_a, pend_b)

        @pl.when(t % 2 == 1)
        def _():
            step(pend_b, pend_a)

    mm = lambda t: jnp.minimum(t, n_t - 1)
    ew = lambda t: jnp.maximum(t - 1, 0)
    out_tile = pl.BlockSpec((tm, tn), lambda t: (ew(t) % n_i, ew(t) // n_i))
    return pl.pallas_call(
        body, name="up_conv_gelu", grid=(n_t + 1,),
        in_specs=[pl.BlockSpec((tm, D), lambda t: (mm(t) % n_i, 0)),
                  pl.BlockSpec((D, tn), lambda t: (0, mm(t) // n_i)),
                  pl.BlockSpec((D, tn), lambda t: (0, FF // tn + mm(t) // n_i)),
                  pl.BlockSpec((4, 2, tn), lambda t: (0, 0, ew(t) // n_i))],
        out_specs=[pl.BlockSpec((2, tm, tn), lambda t: (0, mm(t) % n_i, mm(t) // n_i)), out_tile, out_tile, out_tile],
        out_shape=[jax.ShapeDtypeStruct((2, s, FF), BF16)] + [jax.ShapeDtypeStruct((s, FF), BF16)] * 3,
        scratch_shapes=[pltpu.VMEM((2, tm, tn), F32), pltpu.VMEM((2, tm, tn), F32), pltpu.VMEM((2, 8, tn), F32)],
        compiler_params=_cp(("arbitrary",)),
    )(h1b, w_up, w_up, cwb)


def _down_ln2_loss(a, w_down, h1, target, ln2_g, ln2_b, tm=512):
    s = a.shape[0]

    def body(a_ref, w_ref, h_ref, t_ref, g_ref, b_ref, dz_ref, dzb_ref, st_ref):
        @pl.when(pl.program_id(0) == 0)
        def _():
            st_ref[...] = jnp.zeros_like(st_ref)

        z2 = ALPHA * h_ref[...] + _nn(a_ref[...], w_ref[...])
        zh, rstd = _layer_norm_stats(z2)
        diff = zh * g_ref[...] + b_ref[...] - t_ref[...]
        part = 0.5 * jnp.sum(jnp.mean(diff * diff, axis=-1, keepdims=True), axis=0, keepdims=True)
        dy = diff * (1.0 / D)
        st_ref[0:1, :] += jnp.sum(dy * zh, axis=0, keepdims=True)
        st_ref[1:2, :] += jnp.sum(dy, axis=0, keepdims=True)
        st_ref[2:3, :] += jnp.broadcast_to(part, (1, D))
        dz = _layer_norm_bwd(dy, zh, rstd, g_ref[...])
        dz_ref[...] = dz
        dzb_ref[...] = dz.astype(BF16)

    td = pl.BlockSpec((tm, D), lambda i: (i, 0))
    return pl.pallas_call(
        body, name="down_ln2_loss", grid=(s // tm,),
        in_specs=[pl.BlockSpec((tm, FF), lambda i: (i, 0)), _resident((FF, D)), td, td, _const((1, D)), _const((1, D))],
        out_specs=[td, td, _const((8, D))],
        out_shape=[jax.ShapeDtypeStruct((s, D), F32), jax.ShapeDtypeStruct((s, D), BF16),
                   jax.ShapeDtypeStruct((8, D), F32)],
        compiler_params=_cp(("arbitrary",)),
    )(a, w_down, h1, target, _row(ln2_g), _row(ln2_b))


def _d_act(dz2b, w_down, tm=512):
    s = dz2b.shape[0]

    def body(dz_ref, w_ref, o_ref):
        o_ref[...] = _nt(dz_ref[...], w_ref[...])

    return pl.pallas_call(
        body, name="d_act", grid=(s // tm,),
        in_specs=[pl.BlockSpec((tm, D), lambda i: (i, 0)), _resident((FF, D))],
        out_specs=pl.BlockSpec((tm, FF), lambda i: (i, 0)),
        out_shape=jax.ShapeDtypeStruct((s, FF), F32),
        compiler_params=_cp(("parallel",)),
    )(dz2b, w_down)


def _conv_gelu_bwd(da, up, g, a1, cwb, tm=256, tn=FF // 2, chunk_rows=16):
    s = da.shape[0]
    n_i = s // tm
    n_c = tm // chunk_rows

    def body(da_ref, up_ref, g_ref, a1_ref, c_ref, dup_ref, dc_ref, carry):
        @pl.when(pl.program_id(1) == 0)
        def _():
            carry[...] = jnp.zeros_like(carry)
            dc_ref[...] = jnp.zeros_like(dc_ref)

        def fold(v):
            return jnp.sum(v.reshape(chunk_rows // 8, 8, v.shape[1]), axis=0)

        def chunk(cc, state):
            after, sums = state
            rows = pl.ds((n_c - 1 - cc) * chunk_rows, chunk_rows)
            da_c = da_ref[rows, :]
            dus = (da_c * a1_ref[rows, :].astype(F32), da_c * g_ref[rows, :].astype(F32))
            head, new_sums = [], []
            for half in (0, 1):
                du = dus[half]
                up = up_ref[half, rows, :].astype(F32)
                l1, l2 = _shift_up(du, after[half])
                dup = (du * c_ref[2, half:half + 1, :] + l1 * c_ref[1, half:half + 1, :]
                       + l2 * c_ref[0, half:half + 1, :])
                dup_ref[half, rows, :] = dup.astype(BF16)
                parts = (fold(l2 * up), fold(l1 * up), fold(du * up), fold(du))
                new_sums.append(parts if sums is None else tuple(a + b for a, b in zip(sums[half], parts)))
                head.append(du[:8])
            return tuple(head), new_sums

        state = ((carry[0], carry[1]), None)
        for cc in range(n_c):
            state = chunk(cc, state)
        head, sums = state
        for half in (0, 1):
            carry[half] = head[half]
            for k in range(4):
                dc_ref[k, half:half + 1, :] += jnp.sum(sums[half][k], axis=0, keepdims=True)

    rev = lambda ii: n_i - 1 - ii
    tile = pl.BlockSpec((tm, tn), lambda j, ii: (rev(ii), j))
    pair = pl.BlockSpec((2, tm, tn), lambda j, ii: (0, rev(ii), j))
    per_col = pl.BlockSpec((4, 2, tn), lambda j, ii: (0, 0, j))
    return pl.pallas_call(
        body, name="conv_gelu_bwd", grid=(FF // tn, n_i),
        in_specs=[tile, pair, tile, tile, per_col],
        out_specs=[pair, per_col],
        out_shape=[jax.ShapeDtypeStruct((2, s, FF), BF16), jax.ShapeDtypeStruct((4, 2, FF), F32)],
        scratch_shapes=[pltpu.VMEM((2, 8, tn), F32)],
        compiler_params=_cp(("parallel", "arbitrary")),
    )(da, up, g, a1, cwb)


def _dh1_ln1_bwd(dz2, dup, w_up, z1, ln1_g, tm=512):
    s = dz2.shape[0]

    def body(dz2_ref, dup_ref, w_ref, z1_ref, g_ref, dz1_ref, dz1b_ref, st_ref):
        @pl.when(pl.program_id(0) == 0)
        def _():
            st_ref[...] = jnp.zeros_like(st_ref)

        dh = ALPHA * dz2_ref[...] + _nt(dup_ref[0], w_ref[:, :FF]) + _nt(dup_ref[1], w_ref[:, FF:])
        zh, rstd = _layer_norm_stats(z1_ref[...])
        st_ref[0:1, :] += jnp.sum(dh * zh, axis=0, keepdims=True)
        st_ref[1:2, :] += jnp.sum(dh, axis=0, keepdims=True)
        dz = _layer_norm_bwd(dh, zh, rstd, g_ref[...])
        dz1_ref[...] = dz
        dz1b_ref[...] = dz.astype(BF16)

    td = pl.BlockSpec((tm, D), lambda i: (i, 0))
    return pl.pallas_call(
        body, name="dh1_ln1_bwd", grid=(s // tm,),
        in_specs=[td, pl.BlockSpec((2, tm, FF), lambda i: (0, i, 0)), _resident((D, 2 * FF)), td, _const((1, D))],
        out_specs=[td, td, _const((8, D))],
        out_shape=[jax.ShapeDtypeStruct((s, D), F32), jax.ShapeDtypeStruct((s, D), BF16),
                   jax.ShapeDtypeStruct((8, D), F32)],
        compiler_params=_cp(("arbitrary",), 58),
    )(dz2, dup, w_up, z1, _row(ln1_g))


def _dcat_rms_bwd(dz1b, w_o, o_a, o_b, norm_a_g, norm_b_g, tm=512):
    s = dz1b.shape[0]

    def body(dz_ref, w_ref, oa_ref, ob_ref, ga_ref, gb_ref, da_ref, db_ref, st_ref):
        @pl.when(pl.program_id(0) == 0)
        def _():
            st_ref[...] = jnp.zeros_like(st_ref)

        dcat = _nt(dz_ref[...], w_ref[...])
        for k, (o_ref, g_ref, d_ref) in enumerate(((oa_ref, ga_ref, da_ref), (ob_ref, gb_ref, db_ref))):
            o = jnp.concatenate([o_ref[j] for j in range(4)], axis=1)
            dn = dcat[:, 512 * k:512 * (k + 1)]
            rr = _rms(o)
            oh = o * rr
            st_ref[k:k + 1, :] += jnp.sum(dn * oh, axis=0, keepdims=True)
            doh = dn * g_ref[...]
            d_o = rr * (doh - oh * jnp.mean(doh * oh, axis=-1, keepdims=True))
            for j in range(4):
                d_ref[j] = d_o[:, 128 * j:128 * (j + 1)]

    t512 = pl.BlockSpec((4, tm, 128), lambda i: (0, i, 0))
    return pl.pallas_call(
        body, name="dcat_rms_bwd", grid=(s // tm,),
        in_specs=[pl.BlockSpec((tm, D), lambda i: (i, 0)), _resident((D, D)), t512, t512,
                  _const((1, 512)), _const((1, 512))],
        out_specs=[t512, t512, _const((8, 512))],
        out_shape=[jax.ShapeDtypeStruct((4, s, 128), F32), jax.ShapeDtypeStruct((4, s, 128), F32),
                   jax.ShapeDtypeStruct((8, 512), F32)],
        compiler_params=_cp(("arbitrary",)),
    )(dz1b, w_o, o_a, o_b, _row(norm_a_g), _row(norm_b_g))


def _dproj_combine(dqa, dka, dva, dqkv_b, tm=256):
    s = dka.shape[0]

    def body(qa, ka, va, qb, kb, vb, o_ref):
        for j in range(4):
            o_ref[:, 128 * j:128 * (j + 1)] = qa[j].astype(BF16)
            o_ref[:, 768 + 128 * j:768 + 128 * (j + 1)] = qb[j].astype(BF16)
            o_ref[:, 1280 + 128 * j:1280 + 128 * (j + 1)] = kb[j].astype(BF16)
            o_ref[:, 1792 + 128 * j:1792 + 128 * (j + 1)] = vb[j].astype(BF16)
        o_ref[:, 512:640] = ka[...].astype(BF16)
        o_ref[:, 640:768] = va[...].astype(BF16)

    t512 = pl.BlockSpec((4, tm, 128), lambda i: (0, i, 0))
    t128 = pl.BlockSpec((tm, 128), lambda i: (i, 0))
    return pl.pallas_call(
        body, name="dproj_combine", grid=(s // tm,),
        in_specs=[t512, t128, t128] + [t512] * 3,
        out_specs=pl.BlockSpec((tm, WIN), lambda i: (i, 0)),
        out_shape=jax.ShapeDtypeStruct((s, WIN), BF16),
        compiler_params=_cp(("parallel",)),
    )(dqa, dka, dva, *dqkv_b)


def _grad_x(dz1, dproj, w_in_t, zero, tm=512):
    s = dz1.shape[0]

    def body(dz_ref, dp_ref, w_ref, z_ref, o_ref):
        o_ref[...] = ALPHA * dz_ref[...] + _nn(dp_ref[...], w_ref[...]) + z_ref[0:1, 0:1]

    td = pl.BlockSpec((tm, D), lambda i: (i, 0))
    return pl.pallas_call(
        body, name="grad_x", grid=(s // tm,),
        in_specs=[td, pl.BlockSpec((tm, WIN), lambda i: (i, 0)), _resident((WIN, D)), _const((8, 128))],
        out_specs=td, out_shape=jax.ShapeDtypeStruct((s, D), F32),
        compiler_params=_cp(("parallel",)),
    )(dz1, dproj, w_in_t, zero)


def _place():
    return lax.axis_index("x"), lax.axis_index("y"), lax.axis_index("c")


def _other_chips(x, y):
    return [(1 - x, y), (x, 1 - y), (1 - x, 1 - y)]


def _hbm(a):
    return pltpu.with_memory_space_constraint(a, pltpu.HBM)


def _gather_w_in(shard, conv_w):
    rows_k = shard.shape[0]
    half = rows_k // 2

    def body(src, conv_src, out, conv_out, send_sems, recv_sems):
        x, y, c = _place()
        b = 2 * x + y
        sibling = (x, y, 1 - c)
        chips = _other_chips(x, y)

        def copy(idx, chip_b, core, to, first_hop=False):
            rows = out.at[pl.ds(pl.multiple_of(chip_b * rows_k + core * half, 16), half)]
            s_ref = src.at[pl.ds(pl.multiple_of(core * half, 16), half)] if first_hop else rows
            return pltpu.make_async_remote_copy(src_ref=s_ref, dst_ref=rows, send_sem=send_sems.at[idx],
                                                recv_sem=recv_sems.at[idx], device_id=to, device_id_type=MESH)

        def own_copy():
            return pltpu.make_async_remote_copy(
                src_ref=src, dst_ref=out.at[pl.ds(pl.multiple_of(b * rows_k, 16), rows_k)], send_sem=send_sems.at[6],
                recv_sem=recv_sems.at[6], device_id=sibling, device_id_type=MESH)

        def conv_copy(idx, chip_b, to):
            return pltpu.make_async_remote_copy(src_ref=conv_src, dst_ref=conv_out.at[chip_b],
                                                send_sem=send_sems.at[7 + idx], recv_sem=recv_sems.at[7 + idx],
                                                device_id=to, device_id_type=MESH)

        started = [own_copy(), conv_copy(3, b, sibling)]
        for jn, chip in enumerate(chips):
            started += [copy(jn, b, c, (chip[0], chip[1], c), first_hop=True), conv_copy(jn, b, (chip[0], chip[1], c))]
        for cp in started:
            cp.start()
        for jn, chip in enumerate(chips):
            cb = 2 * chip[0] + chip[1]
            copy(jn, cb, c, (chip[0], chip[1], c)).wait_recv()
            cp = copy(3 + jn, cb, c, sibling)
            cp.start()
            started.append(cp)
        for jn, chip in enumerate(chips):
            cb = 2 * chip[0] + chip[1]
            copy(3 + jn, cb, 1 - c, sibling).wait_recv()
            conv_copy(jn, cb, (chip[0], chip[1], c)).wait_recv()
        own_copy().wait_recv()
        conv_copy(3, b, sibling).wait_recv()
        for cp in started:
            cp.wait_send()

    return pl.pallas_call(
        body, name="gather_w_in",
        in_specs=[ANY, ANY], out_specs=[ANY, ANY],
        out_shape=[jax.ShapeDtypeStruct((N_CHIPS * rows_k, D), BF16), jax.ShapeDtypeStruct((N_CHIPS,) + conv_w.shape, F32)],
        scratch_shapes=[pltpu.SemaphoreType.DMA((11,)), pltpu.SemaphoreType.DMA((11,))],
        compiler_params=pltpu.CompilerParams(has_side_effects=True),
    )(shard, conv_w)


def _weight_copies(shard, land, send_sems, recv_sems, arrivals):
    x, y, c = _place()
    n_rows, n_cols = shard.shape
    peers = [(px, py, c) for px, py in _other_chips(x, y)] + [(x, y, 1 - c)]
    cps = []
    for jn, peer in enumerate(peers):
        at = 2 * peer[0] + peer[1] if arrivals else 2 * x + y
        if land.shape[1] == n_cols:
            dst = land.at[pl.ds(pl.multiple_of(at * n_rows, 16), n_rows)]
        else:
            dst = land.at[:, pl.ds(pl.multiple_of(at * n_cols, 128), n_cols)]
        cps.append(pltpu.make_async_remote_copy(src_ref=shard, dst_ref=dst, send_sem=send_sems.at[jn],
                                                recv_sem=recv_sems.at[jn], device_id=peer, device_id_type=MESH))
    return cps


def _weights_start(shards, after):
    n = len(shards)
    lands = [lax.empty((N_CHIPS * sh.shape[0], D) if sh.shape[1] == D else (D, N_CHIPS * sh.shape[1]), BF16)
             for sh in shards]

    def body(*refs):
        src, land = refs[:n], refs[n:2 * n]
        send_sems, recv_sems = refs[2 * n + 1:3 * n + 1], refs[3 * n + 1:4 * n + 1]
        for k in range(n):
            for send in _weight_copies(src[k], land[k], send_sems[k], recv_sems[k], False):
                send.start()
        refs[-1][...] = jnp.zeros_like(refs[-1])

    res = pl.pallas_call(
        body, name="weights_start",
        in_specs=[HBM] * (2 * n) + [ANY], out_specs=[SEM] * (2 * n) + [HBM] * (2 * n) + [VMEM],
        out_shape=[pltpu.SemaphoreType.DMA((4,))] * (2 * n)
        + [pltpu.HBM(a.shape, a.dtype) for a in (*shards, *lands)] + [jax.ShapeDtypeStruct((8, 128), F32)],
        input_output_aliases={i: i + 2 * n for i in range(2 * n)},
        compiler_params=pltpu.CompilerParams(has_side_effects=DATAFLOW),
    )(*[_hbm(a) for a in (*shards, *lands)], after)
    return [(res[k], res[n + k], res[2 * n + k], res[3 * n + k]) for k in range(n)], res[-1]


def _weights_wait(started, after, name):
    send_sems, recv_sems, shard, land = started

    def body(s_ref, l_ref, send_ref, recv_ref, after_ref, s_out, l_out):
        for cp in _weight_copies(s_ref, l_ref, send_ref, recv_ref, True):
            cp.wait_send()
            cp.wait_recv()

    return pl.pallas_call(
        body, name=name,
        in_specs=[HBM, HBM, SEM, SEM, ANY], out_specs=[HBM, HBM],
        out_shape=[pltpu.HBM(shard.shape, shard.dtype), pltpu.HBM(land.shape, land.dtype)],
        input_output_aliases={0: 0, 1: 1},
        compiler_params=pltpu.CompilerParams(has_side_effects=DATAFLOW),
    )(shard, land, send_sems, recv_sems, after)[1]


def _grad_copies(g_ref, land_ref, send_sems, recv_sems):
    x, y, c = _place()
    cps = []
    for d in range(1, 8):
        px, py, pc = x ^ (d >> 2), y ^ ((d >> 1) & 1), c ^ (d & 1)
        cps.append(pltpu.make_async_remote_copy(
            src_ref=g_ref.at[2 * px + py, pc], dst_ref=land_ref.at[d - 1], send_sem=send_sems.at[d - 1],
            recv_sem=recv_sems.at[d - 1], device_id=(px, py, pc), device_id_type=MESH))
    return cps


def _grads_start(grads_b, name):
    n = len(grads_b)
    lands = [lax.empty((7, g.shape[2], D), BF16) for g in grads_b]

    def body(*refs):
        g, land = refs[:n], refs[n:2 * n]
        send_sems, recv_sems = refs[2 * n:3 * n], refs[3 * n:4 * n]
        for k in range(n):
            for cp in _grad_copies(g[k], land[k], send_sems[k], recv_sems[k]):
                cp.start()
        refs[-1][...] = jnp.zeros_like(refs[-1])

    res = pl.pallas_call(
        body, name=name,
        in_specs=[HBM] * (2 * n), out_specs=[SEM] * (2 * n) + [HBM] * (2 * n) + [VMEM],
        out_shape=[pltpu.SemaphoreType.DMA((7,))] * (2 * n)
        + [pltpu.HBM(a.shape, a.dtype) for a in (*grads_b, *lands)] + [jax.ShapeDtypeStruct((8, 128), F32)],
        input_output_aliases={i: i + 2 * n for i in range(2 * n)},
        compiler_params=pltpu.CompilerParams(has_side_effects=DATAFLOW),
    )(*[_hbm(a) for a in (*grads_b, *lands)])
    return [(res[k], res[n + k], res[2 * n + k], res[3 * n + k]) for k in range(n)], res[-1]


def _grads_wait(started, after, name):
    n = len(started)

    def body(*refs):
        g, land = refs[:n], refs[n:2 * n]
        send_sems, recv_sems = refs[2 * n:3 * n], refs[3 * n:4 * n]
        for k in range(n):
            for cp in _grad_copies(g[k], land[k], send_sems[k], recv_sems[k]):
                cp.wait_send()
                cp.wait_recv()

    gs = [st[2] for st in started]
    lands = [st[3] for st in started]
    res = pl.pallas_call(
        body, name=name,
        in_specs=[HBM] * (2 * n) + [SEM] * (2 * n) + [ANY], out_specs=[HBM] * (2 * n),
        out_shape=[pltpu.HBM(a.shape, a.dtype) for a in (*gs, *lands)],
        input_output_aliases={i: i for i in range(2 * n)},
        compiler_params=pltpu.CompilerParams(has_side_effects=DATAFLOW),
    )(*gs, *lands, *[st[0] for st in started], *[st[1] for st in started], after)
    return res[n:]


def _sum_partials(grad4, got, cb, name, tr):
    h = grad4.shape[2]
    per_half = h // tr

    def body(cb_ref, g_ref, o_ref, out_ref):
        acc = g_ref[...]
        for j in range(7):
            acc = acc + o_ref[j].astype(F32)
        out_ref[...] = acc

    return pl.pallas_call(
        body, name=name,
        grid_spec=pltpu.PrefetchScalarGridSpec(
            num_scalar_prefetch=1, grid=(per_half,),
            in_specs=[pl.BlockSpec((None, None, tr, D), lambda i, cb_ref: (cb_ref[1], cb_ref[0], i, 0)),
                      pl.BlockSpec((7, tr, D), lambda i, cb_ref: (0, i, 0))],
            out_specs=pl.BlockSpec((tr, D), lambda i, cb_ref: (cb_ref[0] * per_half + i, 0))),
        out_shape=jax.ShapeDtypeStruct((2 * h, D), F32),
        compiler_params=_cp(("arbitrary",)),
    )(cb, grad4, got)


def _swap_halves(shards, name):
    n = len(shards)

    def body(*refs):
        out, send_sems, recv_sems = refs[n:2 * n], refs[2 * n], refs[2 * n + 1]
        x, y, c = _place()
        cps = []
        for k in range(n):
            h = shards[k].shape[0] // 2
            mine = out[k].at[pl.ds(pl.multiple_of(c * h, 8), h)]
            cp = pltpu.make_async_remote_copy(src_ref=mine, dst_ref=mine, send_sem=send_sems.at[k],
                                              recv_sem=recv_sems.at[k], device_id=(x, y, 1 - c), device_id_type=MESH)
            cp.start()
            cps.append(cp)
        for cp in cps:
            cp.wait()

    return pl.pallas_call(
        body, name=name,
        in_specs=[ANY] * n, out_specs=[ANY] * n,
        out_shape=[jax.ShapeDtypeStruct(sh.shape, F32) for sh in shards],
        input_output_aliases={k: k for k in range(n)},
        scratch_shapes=[pltpu.SemaphoreType.DMA((n,)), pltpu.SemaphoreType.DMA((n,))],
        compiler_params=pltpu.CompilerParams(has_side_effects=True),
    )(*shards)


def _small_copies(small_ref, land_ref, send_sems, recv_sems):
    x, y, c = _place()
    me = 4 * x + 2 * y + c
    cps = []
    for d in range(1, 8):
        px, py, pc = x ^ (d >> 2), y ^ ((d >> 1) & 1), c ^ (d & 1)
        cps.append(pltpu.make_async_remote_copy(
            src_ref=small_ref, dst_ref=land_ref.at[me], send_sem=send_sems.at[d - 1], recv_sem=recv_sems.at[d - 1],
            device_id=(px, py, pc), device_id_type=MESH))
    return cps


def _small_start(small):
    land = lax.empty((8,) + small.shape, F32)

    def body(s_ref, l_ref, send_sems, recv_sems, s_thru, l_thru, token):
        for cp in _small_copies(s_ref, l_ref, send_sems, recv_sems):
            cp.start()
        token[...] = jnp.zeros_like(token)

    res = pl.pallas_call(
        body, name="small_start",
        in_specs=[HBM, HBM], out_specs=[SEM, SEM, HBM, HBM, VMEM],
        out_shape=[pltpu.SemaphoreType.DMA((7,)), pltpu.SemaphoreType.DMA((7,)), pltpu.HBM(small.shape, F32),
                   pltpu.HBM(land.shape, F32), jax.ShapeDtypeStruct((8, 128), F32)],
        input_output_aliases={0: 2, 1: 3},
        compiler_params=pltpu.CompilerParams(has_side_effects=DATAFLOW),
    )(_hbm(small), _hbm(land))
    return res[:4], res[4]


def _small_wait(started, after):
    send_sems, recv_sems, small, land = started

    def body(s_ref, l_ref, send_ref, recv_ref, after_ref, s_out, l_out):
        for cp in _small_copies(s_ref, l_ref, send_ref, recv_ref):
            cp.wait_send()
            cp.wait_recv()

    return pl.pallas_call(
        body, name="small_wait",
        in_specs=[HBM, HBM, SEM, SEM, ANY], out_specs=[HBM, HBM],
        out_shape=[pltpu.HBM(small.shape, F32), pltpu.HBM(land.shape, F32)],
        input_output_aliases={0: 0, 1: 1},
        compiler_params=pltpu.CompilerParams(has_side_effects=DATAFLOW),
    )(small, land, send_sems, recv_sems, after)


def _small_sum(small, land, me):
    rows = small.shape[0]

    def body(me_ref, s_ref, l_ref, o_ref):
        acc = None
        for k in range(8):
            term = jnp.where(me_ref[0] == k, s_ref[...], l_ref[k])
            acc = term if k == 0 else acc + term
        o_ref[...] = acc

    return pl.pallas_call(
        body, name="small_sum",
        in_specs=[SMEM, VMEM, VMEM], out_specs=VMEM,
        out_shape=jax.ShapeDtypeStruct((rows, D), F32),
    )(me, small, land)


def _adamw(w, g, m, v, name, tr):
    rows, cols = w.shape

    def body(w_ref, g_ref, m_ref, v_ref, d_ref, nm_ref, nv_ref):
        g_ = g_ref[...]
        nm = ADAM_B1 * m_ref[...] + (1.0 - ADAM_B1) * g_
        nv = ADAM_B2 * v_ref[...] + (1.0 - ADAM_B2) * (g_ * g_)
        m_hat = nm / (1.0 - ADAM_B1 ** ADAM_STEP)
        v_hat = nv / (1.0 - ADAM_B2 ** ADAM_STEP)
        d_ref[...] = -ADAM_LR * (m_hat / (jnp.sqrt(v_hat) + ADAM_EPS) + ADAM_WD * w_ref[...])
        nm_ref[...] = nm
        nv_ref[...] = nv

    spec = pl.BlockSpec((tr, cols), lambda i: (i, 0))
    return pl.pallas_call(
        body, name=name, grid=(rows // tr,),
        in_specs=[spec] * 4, out_specs=[spec] * 3,
        out_shape=[jax.ShapeDtypeStruct((rows, cols), F32)] * 3,
        compiler_params=_cp(("parallel",)),
    )(w, g, m, v)


def _local_step(x, target, w_in_t, late_weights, norm_a_g, norm_b_g, sinks_a, ln1_g, ln1_b,
                conv_w, conv_b, ln2_g, ln2_b, slopes, on_grad, on_small):
    cwb = jnp.concatenate([conv_w, conv_b[None]], axis=0).reshape(4, 2, FF)

    proj, xb = _proj(x, w_in_t, "proj")
    o_a, lse_a = _attn_a_fwd(proj, sinks_a)
    fwd_b = [_attn_b_fwd(proj, slopes, r) for r in B_DILATIONS]
    w_o = late_weights(1, fwd_b[-1][1])
    o_b, lse_b, cat, z1, h1, h1b = _mix_ln1(x, o_a, [f[0] for f in fwd_b], [f[1] for f in fwd_b],
                                           norm_a_g, norm_b_g, w_o, ln1_g, ln1_b)
    w_up = late_weights(2, h1b)
    up, a, gate, a1 = _up_conv_gelu(h1b, w_up, cwb)
    w_down = late_weights(3, a)
    dz2, dz2b, st2 = _down_ln2_loss(a, w_down, h1, target, ln2_g, ln2_b)

    on_grad(3, *_grad_w(a, dz2b, "grad_w_down", tm=FF // 2))
    dup, dconv = _conv_gelu_bwd(_d_act(dz2b, w_down), up, gate, a1, cwb)
    on_grad(2, *_grad_w(dup, h1b, "grad_w_up", tm=FF // 2, lhs_halves=True))
    dz1, dz1b, st1 = _dh1_ln1_bwd(dz2, dup, w_up, z1, ln1_g)
    tok = on_grad(1, *_grad_w(cat, dz1b, "grad_w_o", tm=512))
    d_oa, d_ob, st_n = _dcat_rms_bwd(dz1b, w_o, o_a, o_b, norm_a_g + tok[0, 0], norm_b_g)
    dqa, dka, dva, dsink = _attn_a_bwd(proj, sinks_a, d_oa, o_a, lse_a)
    dconv = dconv.reshape(4, 2 * FF)
    tok = on_small(dict(loss=st2[2, 0:1], norm_a_g=st_n[0], norm_b_g=st_n[1], sinks_a=dsink[:, 0],
                        ln1_g=st1[0], ln1_b=st1[1], conv_w=dconv[0:3].reshape(-1), conv_b=dconv[3],
                        ln2_g=st2[0], ln2_b=st2[1]))
    slopes = slopes + tok[0, 0]
    bwd_b = None
    for r in reversed(B_DILATIONS):
        bwd_b = _attn_b_bwd(proj, slopes, d_ob, o_b, lse_b, r, bwd_b)
    dproj = _dproj_combine(dqa, dka, dva, bwd_b)
    tok = on_grad(0, *_grad_w(dproj, xb, "grad_w_in", tm=WA))
    return _grad_x(dz1, dproj, w_in_t, tok)


SMALL_ORDER = ("loss", "norm_a_g", "norm_b_g", "sinks_a", "ln1_g", "ln1_b", "conv_b", "ln2_g", "ln2_b", "conv_w")
SMALL_SIZES = dict(loss=1, norm_a_g=512, norm_b_g=512, sinks_a=8, ln1_g=D, ln1_b=D, conv_b=2 * FF, ln2_g=D, ln2_b=D,
                   conv_w=3 * 2 * FF)


def _pack(parts, rows):
    flat = jnp.concatenate([parts[k].reshape(-1).astype(F32) for k in parts])
    return jnp.pad(flat, (0, rows * D - flat.shape[0])).reshape(rows, D)


def _unpack(buf, names, sizes):
    flat = buf.reshape(-1)
    out, at = {}, 0
    for k in names:
        out[k] = flat[at:at + sizes[k]]
        at += sizes[k]
    return out


def kernel(x, w_in, norm_a_g, norm_b_g, sinks_a, w_o, ln1_g, ln1_b, w_up, conv_w, conv_b, w_down, ln2_g, ln2_b, loss_target, m_w_in, m_norm_a_g, m_norm_b_g, m_sinks_a, m_w_o, m_ln1_g, m_ln1_b, m_w_up, m_conv_w, m_conv_b, m_w_down, m_ln2_g, m_ln2_b, v_w_in, v_norm_a_g, v_norm_b_g, v_sinks_a, v_w_o, v_ln1_g, v_ln1_b, v_w_up, v_conv_w, v_conv_b, v_w_down, v_ln2_g, v_ln2_b):
    xi, yi, ci = _place()
    chip = (2 * xi + yi).astype(I32)
    core = ci.astype(I32)

    w_in_rows, m_w_in_rows, v_w_in_rows = w_in.T, m_w_in.T, v_w_in.T
    shards = (w_in_rows.astype(BF16), w_o.astype(BF16), w_up.astype(BF16), w_down.astype(BF16))
    w_in_t, conv_w4 = _gather_w_in(shards[0], conv_w)
    conv_w_f = conv_w4.transpose(1, 0, 2).reshape(3, 2 * FF)
    w_started, w_tok = _weights_start(shards[1:], conv_w4)
    slopes = jnp.asarray(SLOPES, F32) + w_tok[0, 0]

    halves_rows = [r // 2 for r in SHARD_ROWS]
    grads4, grads_b4, started = [None] * 4, [None] * 4, [None] * 4

    def on_grad(k, g, g_b):
        grads4[k] = g.reshape(N_CHIPS, 2, halves_rows[k], D)
        grads_b4[k] = g_b.reshape(N_CHIPS, 2, halves_rows[k], D)
        if k > 1:
            return None
        group = (1, 2, 3) if k == 1 else (0,)
        sts, tok = _grads_start([grads_b4[i] for i in group], f"grads_start_{k}")
        for i, st in zip(group, sts):
            started[i] = st
        return tok

    small_rows = 32
    small_started = []

    def on_small(parts):
        st, tok = _small_start(_pack({k: parts[k] for k in SMALL_ORDER}, small_rows))
        small_started.append(st)
        return tok

    gx = _local_step(
        x[0], loss_target[0], w_in_t, lambda k, after: _weights_wait(w_started[k - 1], after, f"weights_wait_{k}"),
        norm_a_g, norm_b_g, sinks_a, ln1_g, ln1_b, conv_w_f, conv_b, ln2_g, ln2_b, slopes, on_grad, on_small)

    tiles = (96, 128, 352, 176)
    core_chip = jnp.stack([core, chip])
    got = _grads_wait(started[1:], gx, "grads_wait_1")
    halves = [_sum_partials(grads4[k], got[k - 1], core_chip, f"sum_partials_{k}", tiles[k]) for k in (1, 2, 3)]
    g_w_o, g_w_up_rows, g_w_down = _swap_halves(halves, "swap_halves")
    g_w_up = g_w_up_rows.T
    delta, new_m, new_v = {}, {}, {}
    for k, g, tr in (("w_o", g_w_o, 128), ("w_up", g_w_up, 256), ("w_down", g_w_down, 176)):
        delta[k], new_m[k], new_v[k] = _adamw(dict(w_o=w_o, w_up=w_up, w_down=w_down)[k], g,
                                              dict(w_o=m_w_o, w_up=m_w_up, w_down=m_w_down)[k],
                                              dict(w_o=v_w_o, w_up=v_w_up, w_down=v_w_down)[k], f"adamw_{k}", tr)

    got = _grads_wait(started[:1], delta["w_up"], "grads_wait_0")
    half_in = _sum_partials(grads4[0], got[0], core_chip, "sum_partials_0", tiles[0])
    (g_w_in_rows,) = _swap_halves([half_in], "swap_halves_in")
    small_mine, small_land = _small_wait(small_started[0], g_w_in_rows)
    totals = _small_sum(small_mine, small_land, (4 * xi + 2 * yi + ci).astype(I32).reshape(1))
    tot = _unpack(totals, SMALL_ORDER, SMALL_SIZES)
    loss = tot["loss"][0]
    cols = 2 * FF // N_CHIPS
    g_conv_w = lax.dynamic_slice(tot["conv_w"].reshape(3, 2 * FF), (0, chip * cols), (3, cols))
    g_small = dict(norm_a_g=tot["norm_a_g"], norm_b_g=tot["norm_b_g"], sinks_a=tot["sinks_a"], ln1_g=tot["ln1_g"],
                   ln1_b=tot["ln1_b"], conv_w=g_conv_w, conv_b=tot["conv_b"], ln2_g=tot["ln2_g"], ln2_b=tot["ln2_b"])

    weights = dict(w_in=w_in, norm_a_g=norm_a_g, norm_b_g=norm_b_g, sinks_a=sinks_a, w_o=w_o, ln1_g=ln1_g, ln1_b=ln1_b,
                   w_up=w_up, conv_w=conv_w, conv_b=conv_b, w_down=w_down, ln2_g=ln2_g, ln2_b=ln2_b)
    ms = dict(w_in=m_w_in, norm_a_g=m_norm_a_g, norm_b_g=m_norm_b_g, sinks_a=m_sinks_a, w_o=m_w_o, ln1_g=m_ln1_g,
              ln1_b=m_ln1_b, w_up=m_w_up, conv_w=m_conv_w, conv_b=m_conv_b, w_down=m_w_down, ln2_g=m_ln2_g, ln2_b=m_ln2_b)
    vs = dict(w_in=v_w_in, norm_a_g=v_norm_a_g, norm_b_g=v_norm_b_g, sinks_a=v_sinks_a, w_o=v_w_o, ln1_g=v_ln1_g,
              ln1_b=v_ln1_b, w_up=v_w_up, conv_w=v_conv_w, conv_b=v_conv_b, w_down=v_w_down, ln2_g=v_ln2_g, ln2_b=v_ln2_b)
    order = list(weights)
    grad = dict(g_small, w_in=g_w_in_rows.T, w_o=g_w_o, w_up=g_w_up, w_down=g_w_down)

    delta["w_in"], new_m["w_in"], new_v["w_in"] = [
        a.T for a in _adamw(w_in_rows, g_w_in_rows, m_w_in_rows, v_w_in_rows, "adamw_w_in", 144)]
    small_names = [k for k in order if k not in delta]
    sizes = {k: weights[k].size for k in small_names}
    rows = 16
    packed = [_pack({k: src[k] for k in small_names}, rows) for src in (weights, grad, ms, vs)]
    for res, buf in zip((delta, new_m, new_v), _adamw(*packed, "adamw_small", rows)):
        for k, val in _unpack(buf, small_names, sizes).items():
            res[k] = val.reshape(weights[k].shape)

    return (loss, gx[None], *[grad[k] for k in order], *[delta[k] for k in order],
            *[new_m[k] for k in order], *[new_v[k] for k in order])
```

```python
import functools
import math

import jax
import jax.numpy as jnp
from jax import lax
from jax.experimental import pallas as pl
from jax.experimental.pallas import tpu as pltpu

F32, BF16, I32 = jnp.float32, jnp.bfloat16, jnp.int32

D = 1024
FF = 2816
HD = 64
NH = 8
WA, WB = 768, 1536
WIN = WA + WB
BLK = 128
ALPHA = 2.0 ** 0.25
LN_EPS, RMS_EPS = 1e-5, 1e-6
SCALE = 1.0 / math.sqrt(HD)
A_MAX_DIST, B_MAX_DIST = 127, 128
B_DILATIONS = (1, 4, 16)
SLOPES = tuple(2.0 ** (-(i + 1)) for i in range(NH))
SHARD_ROWS = (WIN // 4, D // 4, 2 * FF // 4, FF // 4)
N_CHIPS = 4
ADAM_LR, ADAM_B1, ADAM_B2, ADAM_EPS, ADAM_WD, ADAM_STEP = 0.001, 0.9, 0.999, 1e-08, 0.01, 10
MESH = pl.DeviceIdType.MESH
ANY = pl.BlockSpec(memory_space=pl.ANY)
SMEM = pl.BlockSpec(memory_space=pltpu.SMEM)
VMEM = pl.BlockSpec(memory_space=pltpu.VMEM)
HBM = pl.BlockSpec(memory_space=pltpu.HBM)
SEM = pl.BlockSpec(memory_space=pltpu.SEMAPHORE)
DATAFLOW = pltpu.SideEffectType.DATAFLOW_SIDE_EFFECTING


def _cp(sem, mb=48):
    return pltpu.CompilerParams(dimension_semantics=sem, vmem_limit_bytes=mb << 20)


def _nn(a, b):
    return lax.dot_general(a, b, (((1,), (0,)), ((), ())), preferred_element_type=F32)


def _nt(a, b):
    return lax.dot_general(a, b, (((1,), (1,)), ((), ())), preferred_element_type=F32)


def _tn(a, b):
    return lax.dot_general(a, b, (((0,), (0,)), ((), ())), preferred_element_type=F32)


def _resident(shape):
    n = len(shape)
    return pl.BlockSpec(shape, lambda *_: (0,) * n, pipeline_mode=pl.Buffered(1))


def _const(shape):
    n = len(shape)
    return pl.BlockSpec(shape, lambda *_: (0,) * n)


def _proj(x, w_t, name, tm=512):
    s = x.shape[0]
    n = w_t.shape[0]

    def body(x_ref, w_ref, o_ref, xb_ref):
        xb = x_ref[...].astype(BF16)
        xb_ref[...] = xb
        res = _nt(xb, w_ref[...])
        for g in range(n // 128):
            o_ref[g] = res[:, 128 * g:128 * (g + 1)]

    return pl.pallas_call(
        body, name=name, grid=(s // tm,),
        in_specs=[pl.BlockSpec((tm, D), lambda i: (i, 0)), _resident((n, D))],
        out_specs=[pl.BlockSpec((n // 128, tm, 128), lambda i: (0, i, 0)), pl.BlockSpec((tm, D), lambda i: (i, 0))],
        out_shape=[jax.ShapeDtypeStruct((n // 128, s, 128), F32), jax.ShapeDtypeStruct((s, D), BF16)],
        compiler_params=_cp(("parallel",)),
    )(x, w_t)


def _grad_w(lhs, rhs, name, tm, tk=2048, lhs_halves=False):
    s = rhs.shape[0]
    if lhs_halves:
        per_half = lhs.shape[2] // tm
        n = 2 * lhs.shape[2]
        lhs_spec = pl.BlockSpec((None, tk, tm), lambda i, k: (i // per_half, k, i % per_half))
    else:
        n = lhs.shape[1]
        lhs_spec = pl.BlockSpec((tk, tm), lambda i, k: (k, i))
    nk = s // tk

    def body(l_ref, r_ref, o_ref, ob_ref):
        k = pl.program_id(1)

        @pl.when(k == 0)
        def _():
            o_ref[...] = jnp.zeros_like(o_ref)

        o_ref[...] += _tn(l_ref[...], r_ref[...])

        @pl.when(k == nk - 1)
        def _():
            ob_ref[...] = o_ref[...].astype(BF16)

    return pl.pallas_call(
        body, name=name, grid=(n // tm, nk),
        in_specs=[lhs_spec, pl.BlockSpec((tk, D), lambda i, k: (k, 0))],
        out_specs=[pl.BlockSpec((tm, D), lambda i, k: (i, 0))] * 2,
        out_shape=[jax.ShapeDtypeStruct((n, D), F32), jax.ShapeDtypeStruct((n, D), BF16)],
        compiler_params=_cp(("parallel", "arbitrary")),
    )(lhs, rhs)


def _band_base(max_dist, dist_unit, first):
    row = lax.broadcasted_iota(I32, (BLK, 2 * BLK), 0)
    col = lax.broadcasted_iota(I32, (BLK, 2 * BLK), 1)
    dist = BLK + row - col
    ok = (dist >= 0) & (dist <= max_dist)
    if first:
        ok = ok & (col >= BLK)
    return jnp.where(ok, dist.astype(F32) * (-float(dist_unit)), -jnp.inf)


def _half_mask(shape, e):
    lane = lax.broadcasted_iota(I32, shape, 1)
    return (lane < HD) if e == 0 else (lane >= HD)


def _to_half(x, e, g):
    if g != e:
        x = pltpu.roll(x, HD, 1)
    return jnp.where(_half_mask(x.shape, g), x, 0.0)


def _stack_heads(scalars, tile):
    return jnp.concatenate([scalars[0] * tile, scalars[1] * tile], axis=0)


def _pair_fwd(q2, kb, vb, base, slopes, kv_heads, sinks):
    lo = _half_mask((BLK, 2 * HD), 0)
    if slopes is None:
        bias = base
    elif sinks is None:
        bias = _stack_heads(slopes, base)
    else:
        col0 = lax.broadcasted_iota(I32, base.shape, 1) == 0
        bias = jnp.concatenate([jnp.where(col0, sinks[e], slopes[e] * base) for e in (0, 1)], axis=0)
    qs = jnp.concatenate([_to_half(q2, e, kv_heads[e]) * SCALE for e in (0, 1)], axis=0).astype(BF16)
    s = _nt(qs, kb) + bias
    m = jnp.max(s, axis=1, keepdims=True)
    p = jnp.exp(s - m)
    l = jnp.sum(p, axis=1, keepdims=True)
    o = _nn(p.astype(BF16), vb) / l
    lse = m + jnp.log(l)
    halves = []
    for e in (0, 1):
        oh = o[e * BLK:(e + 1) * BLK]
        halves.append(pltpu.roll(oh, HD, 1) if kv_heads[e] != e else oh)
    o2 = jnp.where(lo, halves[0], halves[1])
    lse2 = jnp.where(lo, jnp.broadcast_to(lse[:BLK], (BLK, 2 * HD)), jnp.broadcast_to(lse[BLK:], (BLK, 2 * HD)))
    return o2, lse2


def _pair_bwd(q2, kb, vb, do2, o2, lse2, base, slopes, kv_heads, sinks):
    lo = _half_mask((BLK, 2 * HD), 0)
    prod = do2 * o2
    lses, deltas = [], []
    for e in (0, 1):
        hq = _half_mask((BLK, 2 * HD), e)
        lses.append(jnp.max(jnp.where(hq, lse2, -jnp.inf), axis=1, keepdims=True))
        deltas.append(jnp.sum(jnp.where(hq, prod, 0.0), axis=1, keepdims=True))
    lse = jnp.concatenate(lses, axis=0)
    delta = jnp.concatenate(deltas, axis=0)
    qs = jnp.concatenate([_to_half(q2, e, kv_heads[e]) * SCALE for e in (0, 1)], axis=0).astype(BF16)
    dos = jnp.concatenate([_to_half(do2, e, kv_heads[e]) for e in (0, 1)], axis=0).astype(BF16)
    p = jnp.exp(_nt(qs, kb) + (base if slopes is None else _stack_heads(slopes, base)) - lse)
    ds = (p * (_nt(dos, vb) - delta)).astype(BF16)
    dq = _nn(ds, kb) * SCALE
    halves = []
    for e in (0, 1):
        dqh = dq[e * BLK:(e + 1) * BLK]
        halves.append(pltpu.roll(dqh, HD, 1) if kv_heads[e] != e else dqh)
    dq2 = jnp.where(lo, halves[0], halves[1])
    dk2 = _tn(ds, qs)
    dv2 = _tn(p.astype(BF16), dos)
    dsinks = []
    if sinks is not None:
        for e in (0, 1):
            dsinks.append(jnp.sum(-jnp.exp(sinks[e] - lses[e]) * deltas[e], axis=0, keepdims=True))
    return dq2, dk2, dv2, dsinks


A_BLOCKS_PER_STEP = 2
A_BLOCKS_PER_STEP_BWD = 1


def _attn_a_fwd(proj, sinks):
    s = proj.shape[1]
    nq = A_BLOCKS_PER_STEP
    rows = BLK * nq
    steps = s // rows

    def body(sink_ref, q_ref, kp_ref, kc_ref, vp_ref, vc_ref, o_ref, lse_ref):
        n = pl.program_id(0)
        base_rest = _band_base(A_MAX_DIST, 1, False)
        base_0 = jnp.where(n > 0, base_rest, _band_base(A_MAX_DIST, 1, True))
        for i in range(nq):
            cur = pl.ds(i * BLK, BLK)
            k_prev = kc_ref[pl.ds((i - 1) * BLK, BLK), :] if i > 0 else kp_ref[...]
            v_prev = vc_ref[pl.ds((i - 1) * BLK, BLK), :] if i > 0 else vp_ref[...]
            first_key = lax.broadcasted_iota(I32, (2 * BLK, 128), 0) == 0
            kb = jnp.where(first_key, 0.0, jnp.concatenate([k_prev, kc_ref[cur, :]], axis=0)).astype(BF16)
            vb = jnp.where(first_key, 0.0, jnp.concatenate([v_prev, vc_ref[cur, :]], axis=0)).astype(BF16)
            for j in range(NH // 2):
                g = j // 2
                o2, lse2 = _pair_fwd(q_ref[j, cur, :], kb, vb, base_rest if i > 0 else base_0,
                                     (SLOPES[2 * j], SLOPES[2 * j + 1]), (g, g), (sink_ref[2 * j], sink_ref[2 * j + 1]))
                o_ref[j, cur, :] = o2
                lse_ref[j, cur, :] = lse2

    before = lambda n: jnp.maximum(n * nq - 1, 0)
    slab = lambda g: pl.BlockSpec((None, rows, 128), lambda n: (g, n, 0))
    edge = lambda g: pl.BlockSpec((None, BLK, 128), lambda n: (g, before(n), 0))
    quad = pl.BlockSpec((4, rows, 128), lambda n: (0, n, 0))
    return pl.pallas_call(
        body, name="attn_a_fwd", grid=(steps,),
        in_specs=[SMEM, quad, edge(4), slab(4), edge(5), slab(5)],
        out_specs=[quad, quad],
        out_shape=[jax.ShapeDtypeStruct((4, s, 128), F32)] * 2,
        compiler_params=_cp(("parallel",)),
    )(sinks, proj, proj, proj, proj, proj)


def _attn_a_bwd(proj, sinks, d_o, o, lse):
    s = proj.shape[1]
    nq = A_BLOCKS_PER_STEP_BWD
    rows = BLK * nq
    steps = s // rows

    def body(sink_ref, q_ref, kp_ref, kc_ref, vp_ref, vc_ref, do_ref, o_ref, lse_ref,
             dq_ref, dk_ref, dv_ref, dsink_ref, kcar, vcar):
        n = pl.program_id(0)

        @pl.when(n == 0)
        def _():
            kcar[...] = jnp.zeros_like(kcar)
            vcar[...] = jnp.zeros_like(vcar)
            dsink_ref[...] = jnp.zeros_like(dsink_ref)

        dk_ref[...] = kcar[...]
        dv_ref[...] = vcar[...]

        @pl.when(n < steps)
        def _():
            base_rest = _band_base(A_MAX_DIST, 1, False)
            base_0 = jnp.where(n > 0, base_rest, _band_base(A_MAX_DIST, 1, True))
            for i in range(nq):
                cur = pl.ds(i * BLK, BLK)
                k_prev = kc_ref[pl.ds((i - 1) * BLK, BLK), :] if i > 0 else kp_ref[...]
                v_prev = vc_ref[pl.ds((i - 1) * BLK, BLK), :] if i > 0 else vp_ref[...]
                kb = jnp.concatenate([k_prev, kc_ref[cur, :]], axis=0).astype(BF16)
                vb = jnp.concatenate([v_prev, vc_ref[cur, :]], axis=0).astype(BF16)
                dk_win = dv_win = None
                for j in range(NH // 2):
                    g = j // 2
                    dq2, dk2, dv2, dsk = _pair_bwd(q_ref[j, cur, :], kb, vb, do_ref[j, cur, :], o_ref[j, cur, :],
                                                   lse_ref[j, cur, :], base_rest if i > 0 else base_0,
                                                   (SLOPES[2 * j], SLOPES[2 * j + 1]), (g, g),
                                                   (sink_ref[2 * j], sink_ref[2 * j + 1]))
                    dq_ref[j, cur, :] = dq2
                    dk_win = dk2 if j == 0 else dk_win + dk2
                    dv_win = dv2 if j == 0 else dv_win + dv2
                    for e in (0, 1):
                        h = 2 * j + e
                        dsink_ref[h:h + 1, :] += jnp.broadcast_to(dsk[e], (1, 128))
                if i == 0:
                    last = pl.ds((nq - 1) * BLK, BLK)
                    dk_ref[last, :] += dk_win[:BLK]
                    dv_ref[last, :] += dv_win[:BLK]
                else:
                    kcar[pl.ds((i - 1) * BLK, BLK), :] += dk_win[:BLK]
                    vcar[pl.ds((i - 1) * BLK, BLK), :] += dv_win[:BLK]
                kcar[cur, :] = dk_win[BLK:]
                vcar[cur, :] = dv_win[BLK:]

    cur_step = lambda n: jnp.minimum(n, steps - 1)
    before = lambda n: jnp.maximum(cur_step(n) * nq - 1, 0)
    out_prev = lambda n: jnp.maximum(n - 1, 0)
    quad = pl.BlockSpec((4, rows, 128), lambda n: (0, cur_step(n), 0))
    slab = lambda g: pl.BlockSpec((None, rows, 128), lambda n: (g, cur_step(n), 0))
    edge = lambda g: pl.BlockSpec((None, BLK, 128), lambda n: (g, before(n), 0))
    return pl.pallas_call(
        body, name="attn_a_bwd", grid=(steps + 1,),
        in_specs=[SMEM, quad, edge(4), slab(4), edge(5), slab(5), quad, quad, quad],
        out_specs=[quad,
                   pl.BlockSpec((rows, 128), lambda n: (out_prev(n), 0)),
                   pl.BlockSpec((rows, 128), lambda n: (out_prev(n), 0)),
                   pl.BlockSpec((NH, 128), lambda n: (0, 0))],
        out_shape=[jax.ShapeDtypeStruct((4, s, 128), F32), jax.ShapeDtypeStruct((s, 128), F32),
                   jax.ShapeDtypeStruct((s, 128), F32), jax.ShapeDtypeStruct((NH, 128), F32)],
        scratch_shapes=[pltpu.VMEM((rows, 128), F32), pltpu.VMEM((rows, 128), F32)],
        compiler_params=_cp(("arbitrary",)),
    )(sinks, proj, proj, proj, proj, proj, d_o, o, lse)


def _stream(rho, i, r):
    start = i * BLK * r + rho
    return pl.ds(start, BLK, stride=r) if r > 1 else pl.ds(start, BLK)


def _for_streams(r, fn, side_by_side=4):
    if r <= side_by_side:
        for rho in range(r):
            fn(rho)
    else:
        def group(it, carry):
            for u in range(side_by_side):
                fn(side_by_side * it + u)
            return carry

        lax.fori_loop(0, r // side_by_side, group, 0)


B_BLOCKS_PER_STEP = {1: 8, 4: 2, 16: 1}
B_BLOCKS_PER_STEP_FWD = {1: 16, 4: 4, 16: 1}


def _attn_b_fwd(proj, slopes, r):
    s = proj.shape[1]
    nq = B_BLOCKS_PER_STEP_FWD[r]
    rows = BLK * r * nq
    steps = s // rows
    qc, kc, vc = WA // 128, WA // 128 + 4, WA // 128 + 8

    def body(slope_ref, q_ref, kp_ref, kc_ref, vp_ref, vc_ref, o_ref, lse_ref):
        j = pl.program_id(0)
        sb = pl.program_id(1)
        sl2 = (slope_ref[2 * j], slope_ref[2 * j + 1])
        bias_rest = _stack_heads(sl2, _band_base(B_MAX_DIST, r, False))
        bias_0 = jnp.where(sb > 0, bias_rest, _stack_heads(sl2, _band_base(B_MAX_DIST, r, True)))

        def stream(rho):
            for i in range(nq):
                cur = _stream(rho, i, r)
                k_prev = kc_ref[_stream(rho, i - 1, r), :] if i > 0 else kp_ref[_stream(rho, 0, r), :]
                v_prev = vc_ref[_stream(rho, i - 1, r), :] if i > 0 else vp_ref[_stream(rho, 0, r), :]
                kb = jnp.concatenate([k_prev, kc_ref[cur, :]], axis=0).astype(BF16)
                vb = jnp.concatenate([v_prev, vc_ref[cur, :]], axis=0).astype(BF16)
                o2, lse2 = _pair_fwd(q_ref[cur, :], kb, vb, bias_rest if i > 0 else bias_0, None, (0, 1), None)
                o_ref[cur, :] = o2
                lse_ref[cur, :] = lse2

        _for_streams(r, stream, side_by_side=16)

    before = lambda sb: jnp.maximum(sb * nq - 1, 0)
    return pl.pallas_call(
        body, name=f"attn_b_fwd_r{r}", grid=(NH // 2, steps),
        in_specs=[SMEM,
                  pl.BlockSpec((None, rows, 128), lambda j, sb: (qc + j, sb, 0)),
                  pl.BlockSpec((None, BLK * r, 128), lambda j, sb: (kc + j, before(sb), 0)),
                  pl.BlockSpec((None, rows, 128), lambda j, sb: (kc + j, sb, 0)),
                  pl.BlockSpec((None, BLK * r, 128), lambda j, sb: (vc + j, before(sb), 0)),
                  pl.BlockSpec((None, rows, 128), lambda j, sb: (vc + j, sb, 0))],
        out_specs=[pl.BlockSpec((None, rows, 128), lambda j, sb: (j, sb, 0))] * 2,
        out_shape=[jax.ShapeDtypeStruct((4, s, 128), F32)] * 2,
        compiler_params=_cp(("parallel", "parallel")),
    )(slopes, proj, proj, proj, proj, proj)


def _attn_b_bwd(proj, slopes, d_o, o, lse, r, so_far=None):
    s = proj.shape[1]
    nq = B_BLOCKS_PER_STEP[r]
    rows = BLK * r * nq
    steps = s // rows
    qc, kc, vc = WA // 128, WA // 128 + 4, WA // 128 + 8
    chained = so_far is not None

    def body(slope_ref, q_ref, kp_ref, kc_ref, vp_ref, vc_ref, do_ref, o_ref, lse_ref, *rest):
        if chained:
            pq_ref, pk_ref, pv_ref, dq_ref, dk_ref, dv_ref, kcar, vcar = rest
        else:
            dq_ref, dk_ref, dv_ref, kcar, vcar = rest
        j = pl.program_id(0)
        sb = pl.program_id(1)

        @pl.when(sb == 0)
        def _():
            kcar[...] = jnp.zeros_like(kcar)
            vcar[...] = jnp.zeros_like(vcar)

        if chained:
            dk_ref[...] = kcar[...] + pk_ref[...]
            dv_ref[...] = vcar[...] + pv_ref[...]
        else:
            dk_ref[...] = kcar[...]
            dv_ref[...] = vcar[...]

        @pl.when(sb < steps)
        def _():
            sl2 = (slope_ref[2 * j], slope_ref[2 * j + 1])
            bias_rest = _stack_heads(sl2, _band_base(B_MAX_DIST, r, False))
            bias_0 = jnp.where(sb > 0, bias_rest, _stack_heads(sl2, _band_base(B_MAX_DIST, r, True)))

            def stream(rho):
                for i in range(nq):
                    cur = _stream(rho, i, r)
                    k_prev = kc_ref[_stream(rho, i - 1, r), :] if i > 0 else kp_ref[_stream(rho, 0, r), :]
                    v_prev = vc_ref[_stream(rho, i - 1, r), :] if i > 0 else vp_ref[_stream(rho, 0, r), :]
                    kb = jnp.concatenate([k_prev, kc_ref[cur, :]], axis=0).astype(BF16)
                    vb = jnp.concatenate([v_prev, vc_ref[cur, :]], axis=0).astype(BF16)
                    dq2, dk2, dv2, _ = _pair_bwd(q_ref[cur, :], kb, vb, do_ref[cur, :], o_ref[cur, :], lse_ref[cur, :],
                                                 bias_rest if i > 0 else bias_0, None, (0, 1), None)
                    dq_ref[cur, :] = dq2 + pq_ref[cur, :] if chained else dq2
                    if i == 0:
                        last = _stream(rho, nq - 1, r)
                        dk_ref[last, :] += dk2[:BLK]
                        dv_ref[last, :] += dv2[:BLK]
                    else:
                        kcar[_stream(rho, i - 1, r), :] += dk2[:BLK]
                        vcar[_stream(rho, i - 1, r), :] += dv2[:BLK]
                    kcar[cur, :] = dk2[BLK:]
                    vcar[cur, :] = dv2[BLK:]

            _for_streams(r, stream, side_by_side=8)

    cur_step = lambda sb: jnp.minimum(sb, steps - 1)
    before = lambda sb: jnp.maximum(cur_step(sb) * nq - 1, 0)
    out_prev = lambda sb: jnp.maximum(sb - 1, 0)
    tile = lambda slab: pl.BlockSpec((None, rows, 128), lambda j, sb: (slab + j, cur_step(sb), 0))
    edge = lambda slab: pl.BlockSpec((None, BLK * r, 128), lambda j, sb: (slab + j, before(sb), 0))
    late = pl.BlockSpec((None, rows, 128), lambda j, sb: (j, out_prev(sb), 0))
    grads = [tile(0), late, late]
    return pl.pallas_call(
        body, name=f"attn_b_bwd_r{r}", grid=(NH // 2, steps + 1),
        in_specs=[SMEM, tile(qc), edge(kc), tile(kc), edge(vc), tile(vc), tile(0), tile(0), tile(0)]
        + (grads if chained else []),
        out_specs=grads,
        out_shape=[jax.ShapeDtypeStruct((4, s, 128), F32)] * 3,
        scratch_shapes=[pltpu.VMEM((rows, 128), F32), pltpu.VMEM((rows, 128), F32)],
        compiler_params=_cp(("parallel", "arbitrary")),
    )(slopes, proj, proj, proj, proj, proj, d_o, o, lse, *(so_far if chained else ()))


def _row(v):
    return v.reshape(1, -1)


def _layer_norm_stats(z):
    mu = jnp.mean(z, axis=-1, keepdims=True)
    zc = z - mu
    var = jnp.mean(zc * zc, axis=-1, keepdims=True)
    rstd = lax.rsqrt(var + LN_EPS)
    return zc * rstd, rstd


def _layer_norm_bwd(dh, zh, rstd, g):
    dzh = dh * g
    return rstd * (dzh - jnp.mean(dzh, axis=-1, keepdims=True) - zh * jnp.mean(dzh * zh, axis=-1, keepdims=True))


def _rms(o):
    return lax.rsqrt(jnp.mean(o * o, axis=-1, keepdims=True) + RMS_EPS)


def _mix_ln1(x, o_a, o_b, lse_b, norm_a_g, norm_b_g, w_o, ln1_g, ln1_b, tm=256):
    s = x.shape[0]

    def wide(ref):
        return jnp.concatenate([ref[j] for j in range(4)], axis=1)

    def body(x_ref, oa_ref, ob1, ob2, ob3, l1, l2, l3, ga_ref, gb_ref, wo_ref, g_ref, b_ref,
             obm_ref, lse_ref, cat_ref, z1_ref, h1_ref, h1b_ref):
        la, lb, lc = wide(l1), wide(l2), wide(l3)
        m = jnp.maximum(jnp.maximum(la, lb), lc)
        ea, eb, ec = jnp.exp(la - m), jnp.exp(lb - m), jnp.exp(lc - m)
        den = ea + eb + ec
        obm = (ea / den) * wide(ob1) + (eb / den) * wide(ob2) + (ec / den) * wide(ob3)
        lse = m + jnp.log(den)
        for j in range(4):
            obm_ref[j] = obm[:, 128 * j:128 * (j + 1)]
            lse_ref[j] = lse[:, 128 * j:128 * (j + 1)]
        oa = wide(oa_ref)
        na = oa * _rms(oa) * ga_ref[...]
        nb_ = obm * _rms(obm) * gb_ref[...]
        cat = jnp.concatenate([na, nb_], axis=1).astype(BF16)
        cat_ref[...] = cat
        z1 = ALPHA * x_ref[...] + _nn(cat, wo_ref[...])
        z1_ref[...] = z1
        zh, _ = _layer_norm_stats(z1)
        h1 = zh * g_ref[...] + b_ref[...]
        h1_ref[...] = h1
        h1b_ref[...] = h1.astype(BF16)

    t512 = pl.BlockSpec((4, tm, 128), lambda i: (0, i, 0))
    td = pl.BlockSpec((tm, D), lambda i: (i, 0))
    return pl.pallas_call(
        body, name="mix_ln1", grid=(s // tm,),
        in_specs=[td] + [t512] * 7 + [_const((1, 512))] * 2 + [_resident((D, D))] + [_const((1, D))] * 2,
        out_specs=[t512, t512, td, td, td, td],
        out_shape=[jax.ShapeDtypeStruct((4, s, 128), F32), jax.ShapeDtypeStruct((4, s, 128), F32),
                   jax.ShapeDtypeStruct((s, D), BF16), jax.ShapeDtypeStruct((s, D), F32),
                   jax.ShapeDtypeStruct((s, D), F32), jax.ShapeDtypeStruct((s, D), BF16)],
        compiler_params=_cp(("parallel",)),
    )(x, o_a, *o_b, *lse_b, _row(norm_a_g), _row(norm_b_g), w_o, _row(ln1_g), _row(ln1_b))


def _gelu_and_grad(x):
    c = math.sqrt(2.0 / math.pi)
    x2 = x * x
    cx = c * x
    t = jnp.tanh(cx * (1.0 + 0.044715 * x2))
    q = 1.0 + t
    g = (0.5 * x) * q
    dg = 0.5 * q + ((0.5 * cx) * (1.0 - t * t)) * (1.0 + (3.0 * 0.044715) * x2)
    return g, dg


def _shift_down(u, before):
    n = u.shape[0]
    ext = jnp.concatenate([before, u], axis=0)
    return pltpu.roll(ext, 1, 0)[8:], pltpu.roll(ext, 2, 0)[8:]


def _shift_up(u, after):
    n = u.shape[0]
    ext = jnp.concatenate([u, after], axis=0)
    return pltpu.roll(ext, n + 7, 0)[:n], pltpu.roll(ext, n + 6, 0)[:n]


def _up_proj(h1b, w_up, tm=512):
    s = h1b.shape[0]

    def body(h_ref, w_ref, o_ref):
        h = h_ref[...]
        for half in (0, 1):
            o_ref[half] = _nn(h, w_ref[:, half * FF:(half + 1) * FF])

    return pl.pallas_call(
        body, name="up_proj", grid=(s // tm,),
        in_specs=[pl.BlockSpec((tm, D), lambda i: (i, 0)), _resident((D, 2 * FF))],
        out_specs=pl.BlockSpec((2, tm, FF), lambda i: (0, i, 0)),
        out_shape=jax.ShapeDtypeStruct((2, s, FF), F32),
        compiler_params=_cp(("parallel",)),
    )(h1b, w_up)


def _conv_gelu(up, cwb, tm=256, tn=FF // 2, chunk_rows=16):
    s = up.shape[1]
    n_c = tm // chunk_rows

    def body(up_ref, c_ref, upb_ref, a_ref, g_ref, a1_ref, carry):
        @pl.when(pl.program_id(1) == 0)
        def _():
            carry[...] = jnp.zeros_like(carry)

        edge = (carry[0], carry[1])
        for c in range(n_c):
            rows = pl.ds(c * chunk_rows, chunk_rows)
            u, last = [], []
            for half in (0, 1):
                x = up_ref[half, rows, :]
                upb_ref[half, rows, :] = x.astype(BF16)
                r1, r2 = _shift_down(x, edge[half])
                u.append(r2 * c_ref[0, half:half + 1, :] + r1 * c_ref[1, half:half + 1, :]
                         + x * c_ref[2, half:half + 1, :] + c_ref[3, half:half + 1, :])
                last.append(x[chunk_rows - 8:])
            g, dg = _gelu_and_grad(u[0])
            a_ref[rows, :] = (g * u[1]).astype(BF16)
            g_ref[rows, :] = g.astype(BF16)
            a1_ref[rows, :] = (u[1] * dg).astype(BF16)
            edge = tuple(last)
        for half in (0, 1):
            carry[half] = edge[half]

    pair = pl.BlockSpec((2, tm, tn), lambda j, i: (0, i, j))
    tile = pl.BlockSpec((tm, tn), lambda j, i: (i, j))
    return pl.pallas_call(
        body, name="conv_gelu", grid=(FF // tn, s // tm),
        in_specs=[pair, pl.BlockSpec((4, 2, tn), lambda j, i: (0, 0, j))],
        out_specs=[pair, tile, tile, tile],
        out_shape=[jax.ShapeDtypeStruct((2, s, FF), BF16)] + [jax.ShapeDtypeStruct((s, FF), BF16)] * 3,
        scratch_shapes=[pltpu.VMEM((2, 8, tn), F32)],
        compiler_params=_cp(("parallel", "arbitrary")),
    )(up, cwb)


def _down_ln2_loss(a, w_down, h1, target, ln2_g, ln2_b, tm=512):
    s = a.shape[0]

    def body(a_ref, w_ref, h_ref, t_ref, g_ref, b_ref, dz_ref, dzb_ref, st_ref):
        @pl.when(pl.program_id(0) == 0)
        def _():
            st_ref[...] = jnp.zeros_like(st_ref)

        z2 = ALPHA * h_ref[...] + _nn(a_ref[...], w_ref[...])
        zh, rstd = _layer_norm_stats(z2)
        diff = zh * g_ref[...] + b_ref[...] - t_ref[...]
        part = 0.5 * jnp.sum(jnp.mean(diff * diff, axis=-1, keepdims=True), axis=0, keepdims=True)
        dy = diff * (1.0 / D)
        st_ref[0:1, :] += jnp.sum(dy * zh, axis=0, keepdims=True)
        st_ref[1:2, :] += jnp.sum(dy, axis=0, keepdims=True)
        st_ref[2:3, :] += jnp.broadcast_to(part, (1, D))
        dz = _layer_norm_bwd(dy, zh, rstd, g_ref[...])
        dz_ref[...] = dz
        dzb_ref[...] = dz.astype(BF16)

    td = pl.BlockSpec((tm, D), lambda i: (i, 0))
    return pl.pallas_call(
        body, name="down_ln2_loss", grid=(s // tm,),
        in_specs=[pl.BlockSpec((tm, FF), lambda i: (i, 0)), _resident((FF, D)), td, td, _const((1, D)), _const((1, D))],
        out_specs=[td, td, _const((8, D))],
        out_shape=[jax.ShapeDtypeStruct((s, D), F32), jax.ShapeDtypeStruct((s, D), BF16),
                   jax.ShapeDtypeStruct((8, D), F32)],
        compiler_params=_cp(("arbitrary",)),
    )(a, w_down, h1, target, _row(ln2_g), _row(ln2_b))


def _d_act(dz2b, w_down, tm=512):
    s = dz2b.shape[0]

    def body(dz_ref, w_ref, o_ref):
        o_ref[...] = _nt(dz_ref[...], w_ref[...])

    return pl.pallas_call(
        body, name="d_act", grid=(s // tm,),
        in_specs=[pl.BlockSpec((tm, D), lambda i: (i, 0)), _resident((FF, D))],
        out_specs=pl.BlockSpec((tm, FF), lambda i: (i, 0)),
        out_shape=jax.ShapeDtypeStruct((s, FF), F32),
        compiler_params=_cp(("parallel",)),
    )(dz2b, w_down)


def _conv_gelu_bwd(da, up, g, a1, cwb, tm=256, tn=FF // 2, chunk_rows=16):
    s = da.shape[0]
    n_i = s // tm
    n_c = tm // chunk_rows

    def body(da_ref, up_ref, g_ref, a1_ref, c_ref, dup_ref, dc_ref, carry):
        @pl.when(pl.program_id(1) == 0)
        def _():
            carry[...] = jnp.zeros_like(carry)
            dc_ref[...] = jnp.zeros_like(dc_ref)

        def fold(v):
            return jnp.sum(v.reshape(chunk_rows // 8, 8, v.shape[1]), axis=0)

        def chunk(cc, state):
            after, sums = state
            rows = pl.ds((n_c - 1 - cc) * chunk_rows, chunk_rows)
            da_c = da_ref[rows, :]
            dus = (da_c * a1_ref[rows, :].astype(F32), da_c * g_ref[rows, :].astype(F32))
            head, new_sums = [], []
            for half in (0, 1):
                du = dus[half]
                up = up_ref[half, rows, :].astype(F32)
                l1, l2 = _shift_up(du, after[half])
                dup = (du * c_ref[2, half:half + 1, :] + l1 * c_ref[1, half:half + 1, :]
                       + l2 * c_ref[0, half:half + 1, :])
                dup_ref[half, rows, :] = dup.astype(BF16)
                parts = (fold(l2 * up), fold(l1 * up), fold(du * up), fold(du))
                new_sums.append(parts if sums is None else tuple(a + b for a, b in zip(sums[half], parts)))
                head.append(du[:8])
            return tuple(head), new_sums

        state = ((carry[0], carry[1]), None)
        for cc in range(n_c):
            state = chunk(cc, state)
        head, sums = state
        for half in (0, 1):
            carry[half] = head[half]
            for k in range(4):
                dc_ref[k, half:half + 1, :] += jnp.sum(sums[half][k], axis=0, keepdims=True)

    rev = lambda ii: n_i - 1 - ii
    tile = pl.BlockSpec((tm, tn), lambda j, ii: (rev(ii), j))
    pair = pl.BlockSpec((2, tm, tn), lambda j, ii: (0, rev(ii), j))
    per_col = pl.BlockSpec((4, 2, tn), lambda j, ii: (0, 0, j))
    return pl.pallas_call(
        body, name="conv_gelu_bwd", grid=(FF // tn, n_i),
        in_specs=[tile, pair, tile, tile, per_col],
        out_specs=[pair, per_col],
        out_shape=[jax.ShapeDtypeStruct((2, s, FF), BF16), jax.ShapeDtypeStruct((4, 2, FF), F32)],
        scratch_shapes=[pltpu.VMEM((2, 8, tn), F32)],
        compiler_params=_cp(("parallel", "arbitrary")),
    )(da, up, g, a1, cwb)


def _dh1_ln1_bwd(dz2, dup, w_up, z1, ln1_g, tm=512):
    s = dz2.shape[0]

    def body(dz2_ref, dup_ref, w_ref, z1_ref, g_ref, dz1_ref, dz1b_ref, st_ref):
        @pl.when(pl.program_id(0) == 0)
        def _():
            st_ref[...] = jnp.zeros_like(st_ref)

        dh = ALPHA * dz2_ref[...] + _nt(dup_ref[0], w_ref[:, :FF]) + _nt(dup_ref[1], w_ref[:, FF:])
        zh, rstd = _layer_norm_stats(z1_ref[...])
        st_ref[0:1, :] += jnp.sum(dh * zh, axis=0, keepdims=True)
        st_ref[1:2, :] += jnp.sum(dh, axis=0, keepdims=True)
        dz = _layer_norm_bwd(dh, zh, rstd, g_ref[...])
        dz1_ref[...] = dz
        dz1b_ref[...] = dz.astype(BF16)

    td = pl.BlockSpec((tm, D), lambda i: (i, 0))
    return pl.pallas_call(
        body, name="dh1_ln1_bwd", grid=(s // tm,),
        in_specs=[td, pl.BlockSpec((2, tm, FF), lambda i: (0, i, 0)), _resident((D, 2 * FF)), td, _const((1, D))],
        out_specs=[td, td, _const((8, D))],
        out_shape=[jax.ShapeDtypeStruct((s, D), F32), jax.ShapeDtypeStruct((s, D), BF16),
                   jax.ShapeDtypeStruct((8, D), F32)],
        compiler_params=_cp(("arbitrary",), 58),
    )(dz2, dup, w_up, z1, _row(ln1_g))


def _dcat_rms_bwd(dz1b, w_o, o_a, o_b, norm_a_g, norm_b_g, tm=512):
    s = dz1b.shape[0]

    def body(dz_ref, w_ref, oa_ref, ob_ref, ga_ref, gb_ref, da_ref, db_ref, st_ref):
        @pl.when(pl.program_id(0) == 0)
        def _():
            st_ref[...] = jnp.zeros_like(st_ref)

        dcat = _nt(dz_ref[...], w_ref[...])
        for k, (o_ref, g_ref, d_ref) in enumerate(((oa_ref, ga_ref, da_ref), (ob_ref, gb_ref, db_ref))):
            o = jnp.concatenate([o_ref[j] for j in range(4)], axis=1)
            dn = dcat[:, 512 * k:512 * (k + 1)]
            rr = _rms(o)
            oh = o * rr
            st_ref[k:k + 1, :] += jnp.sum(dn * oh, axis=0, keepdims=True)
            doh = dn * g_ref[...]
            d_o = rr * (doh - oh * jnp.mean(doh * oh, axis=-1, keepdims=True))
            for j in range(4):
                d_ref[j] = d_o[:, 128 * j:128 * (j + 1)]

    t512 = pl.BlockSpec((4, tm, 128), lambda i: (0, i, 0))
    return pl.pallas_call(
        body, name="dcat_rms_bwd", grid=(s // tm,),
        in_specs=[pl.BlockSpec((tm, D), lambda i: (i, 0)), _resident((D, D)), t512, t512,
                  _const((1, 512)), _const((1, 512))],
        out_specs=[t512, t512, _const((8, 512))],
        out_shape=[jax.ShapeDtypeStruct((4, s, 128), F32), jax.ShapeDtypeStruct((4, s, 128), F32),
                   jax.ShapeDtypeStruct((8, 512), F32)],
        compiler_params=_cp(("arbitrary",)),
    )(dz1b, w_o, o_a, o_b, _row(norm_a_g), _row(norm_b_g))


def _dproj_combine(dqa, dka, dva, dqkv_b, tm=256):
    s = dka.shape[0]

    def body(qa, ka, va, qb, kb, vb, o_ref):
        for j in range(4):
            o_ref[:, 128 * j:128 * (j + 1)] = qa[j].astype(BF16)
            o_ref[:, 768 + 128 * j:768 + 128 * (j + 1)] = qb[j].astype(BF16)
            o_ref[:, 1280 + 128 * j:1280 + 128 * (j + 1)] = kb[j].astype(BF16)
            o_ref[:, 1792 + 128 * j:1792 + 128 * (j + 1)] = vb[j].astype(BF16)
        o_ref[:, 512:640] = ka[...].astype(BF16)
        o_ref[:, 640:768] = va[...].astype(BF16)

    t512 = pl.BlockSpec((4, tm, 128), lambda i: (0, i, 0))
    t128 = pl.BlockSpec((tm, 128), lambda i: (i, 0))
    return pl.pallas_call(
        body, name="dproj_combine", grid=(s // tm,),
        in_specs=[t512, t128, t128] + [t512] * 3,
        out_specs=pl.BlockSpec((tm, WIN), lambda i: (i, 0)),
        out_shape=jax.ShapeDtypeStruct((s, WIN), BF16),
        compiler_params=_cp(("parallel",)),
    )(dqa, dka, dva, *dqkv_b)


def _grad_x(dz1, dproj, w_in_t, zero, tm=512):
    s = dz1.shape[0]

    def body(dz_ref, dp_ref, w_ref, z_ref, o_ref):
        o_ref[...] = ALPHA * dz_ref[...] + _nn(dp_ref[...], w_ref[...]) + z_ref[0:1, 0:1]

    td = pl.BlockSpec((tm, D), lambda i: (i, 0))
    return pl.pallas_call(
        body, name="grad_x", grid=(s // tm,),
        in_specs=[td, pl.BlockSpec((tm, WIN), lambda i: (i, 0)), _resident((WIN, D)), _const((8, 128))],
        out_specs=td, out_shape=jax.ShapeDtypeStruct((s, D), F32),
        compiler_params=_cp(("parallel",)),
    )(dz1, dproj, w_in_t, zero)


def _place():
    return lax.axis_index("x"), lax.axis_index("y"), lax.axis_index("c")


def _other_chips(x, y):
    return [(1 - x, y), (x, 1 - y), (1 - x, 1 - y)]


def _hbm(a):
    return pltpu.with_memory_space_constraint(a, pltpu.HBM)


def _gather_w_in(shard, conv_w):
    rows_k = shard.shape[0]
    half = rows_k // 2

    def body(src, conv_src, out, conv_out, send_sems, recv_sems):
        x, y, c = _place()
        b = 2 * x + y
        sibling = (x, y, 1 - c)
        chips = _other_chips(x, y)

        def copy(idx, chip_b, core, to, first_hop=False):
            rows = out.at[pl.ds(pl.multiple_of(chip_b * rows_k + core * half, 16), half)]
            s_ref = src.at[pl.ds(pl.multiple_of(core * half, 16), half)] if first_hop else rows
            return pltpu.make_async_remote_copy(src_ref=s_ref, dst_ref=rows, send_sem=send_sems.at[idx],
                                                recv_sem=recv_sems.at[idx], device_id=to, device_id_type=MESH)

        def own_copy():
            return pltpu.make_async_remote_copy(
                src_ref=src, dst_ref=out.at[pl.ds(pl.multiple_of(b * rows_k, 16), rows_k)], send_sem=send_sems.at[6],
                recv_sem=recv_sems.at[6], device_id=sibling, device_id_type=MESH)

        def conv_copy(idx, chip_b, to):
            return pltpu.make_async_remote_copy(src_ref=conv_src, dst_ref=conv_out.at[chip_b],
                                                send_sem=send_sems.at[7 + idx], recv_sem=recv_sems.at[7 + idx],
                                                device_id=to, device_id_type=MESH)

        started = [own_copy(), conv_copy(3, b, sibling)]
        for jn, chip in enumerate(chips):
            started += [copy(jn, b, c, (chip[0], chip[1], c), first_hop=True), conv_copy(jn, b, (chip[0], chip[1], c))]
        for cp in started:
            cp.start()
        for jn, chip in enumerate(chips):
            cb = 2 * chip[0] + chip[1]
            copy(jn, cb, c, (chip[0], chip[1], c)).wait_recv()
            cp = copy(3 + jn, cb, c, sibling)
            cp.start()
            started.append(cp)
        for jn, chip in enumerate(chips):
            cb = 2 * chip[0] + chip[1]
            copy(3 + jn, cb, 1 - c, sibling).wait_recv()
            conv_copy(jn, cb, (chip[0], chip[1], c)).wait_recv()
        own_copy().wait_recv()
        conv_copy(3, b, sibling).wait_recv()
        for cp in started:
            cp.wait_send()

    return pl.pallas_call(
        body, name="gather_w_in",
        in_specs=[ANY, ANY], out_specs=[ANY, ANY],
        out_shape=[jax.ShapeDtypeStruct((N_CHIPS * rows_k, D), BF16), jax.ShapeDtypeStruct((N_CHIPS,) + conv_w.shape, F32)],
        scratch_shapes=[pltpu.SemaphoreType.DMA((11,)), pltpu.SemaphoreType.DMA((11,))],
        compiler_params=pltpu.CompilerParams(has_side_effects=True),
    )(shard, conv_w)


def _weight_copies(shard, land, send_sems, recv_sems, arrivals):
    x, y, c = _place()
    n_rows, n_cols = shard.shape
    peers = [(px, py, c) for px, py in _other_chips(x, y)] + [(x, y, 1 - c)]
    cps = []
    for jn, peer in enumerate(peers):
        at = 2 * peer[0] + peer[1] if arrivals else 2 * x + y
        if land.shape[1] == n_cols:
            dst = land.at[pl.ds(pl.multiple_of(at * n_rows, 16), n_rows)]
        else:
            dst = land.at[:, pl.ds(pl.multiple_of(at * n_cols, 128), n_cols)]
        cps.append(pltpu.make_async_remote_copy(src_ref=shard, dst_ref=dst, send_sem=send_sems.at[jn],
                                                recv_sem=recv_sems.at[jn], device_id=peer, device_id_type=MESH))
    return cps


def _weights_start(shards, after):
    n = len(shards)
    lands = [lax.empty((N_CHIPS * sh.shape[0], D) if sh.shape[1] == D else (D, N_CHIPS * sh.shape[1]), BF16)
             for sh in shards]

    def body(*refs):
        src, land = refs[:n], refs[n:2 * n]
        send_sems, recv_sems = refs[2 * n + 1:3 * n + 1], refs[3 * n + 1:4 * n + 1]
        for k in range(n):
            for send in _weight_copies(src[k], land[k], send_sems[k], recv_sems[k], False):
                send.start()
        refs[-1][...] = jnp.zeros_like(refs[-1])

    res = pl.pallas_call(
        body, name="weights_start",
        in_specs=[HBM] * (2 * n) + [ANY], out_specs=[SEM] * (2 * n) + [HBM] * (2 * n) + [VMEM],
        out_shape=[pltpu.SemaphoreType.DMA((4,))] * (2 * n)
        + [pltpu.HBM(a.shape, a.dtype) for a in (*shards, *lands)] + [jax.ShapeDtypeStruct((8, 128), F32)],
        input_output_aliases={i: i + 2 * n for i in range(2 * n)},
        compiler_params=pltpu.CompilerParams(has_side_effects=DATAFLOW),
    )(*[_hbm(a) for a in (*shards, *lands)], after)
    return [(res[k], res[n + k], res[2 * n + k], res[3 * n + k]) for k in range(n)], res[-1]


def _weights_wait(started, after, name):
    send_sems, recv_sems, shard, land = started

    def body(s_ref, l_ref, send_ref, recv_ref, after_ref, s_out, l_out):
        for cp in _weight_copies(s_ref, l_ref, send_ref, recv_ref, True):
            cp.wait_send()
            cp.wait_recv()

    return pl.pallas_call(
        body, name=name,
        in_specs=[HBM, HBM, SEM, SEM, ANY], out_specs=[HBM, HBM],
        out_shape=[pltpu.HBM(shard.shape, shard.dtype), pltpu.HBM(land.shape, land.dtype)],
        input_output_aliases={0: 0, 1: 1},
        compiler_params=pltpu.CompilerParams(has_side_effects=DATAFLOW),
    )(shard, land, send_sems, recv_sems, after)[1]


def _grad_copies(g_ref, land_ref, send_sems, recv_sems):
    x, y, c = _place()
    cps = []
    for d in range(1, 8):
        px, py, pc = x ^ (d >> 2), y ^ ((d >> 1) & 1), c ^ (d & 1)
        cps.append(pltpu.make_async_remote_copy(
            src_ref=g_ref.at[2 * px + py, pc], dst_ref=land_ref.at[d - 1], send_sem=send_sems.at[d - 1],
            recv_sem=recv_sems.at[d - 1], device_id=(px, py, pc), device_id_type=MESH))
    return cps


def _grads_start(grads_b, name):
    n = len(grads_b)
    lands = [lax.empty((7, g.shape[2], D), BF16) for g in grads_b]

    def body(*refs):
        g, land = refs[:n], refs[n:2 * n]
        send_sems, recv_sems = refs[2 * n:3 * n], refs[3 * n:4 * n]
        for k in range(n):
            for cp in _grad_copies(g[k], land[k], send_sems[k], recv_sems[k]):
                cp.start()
        refs[-1][...] = jnp.zeros_like(refs[-1])

    res = pl.pallas_call(
        body, name=name,
        in_specs=[HBM] * (2 * n), out_specs=[SEM] * (2 * n) + [HBM] * (2 * n) + [VMEM],
        out_shape=[pltpu.SemaphoreType.DMA((7,))] * (2 * n)
        + [pltpu.HBM(a.shape, a.dtype) for a in (*grads_b, *lands)] + [jax.ShapeDtypeStruct((8, 128), F32)],
        input_output_aliases={i: i + 2 * n for i in range(2 * n)},
        compiler_params=pltpu.CompilerParams(has_side_effects=DATAFLOW),
    )(*[_hbm(a) for a in (*grads_b, *lands)])
    return [(res[k], res[n + k], res[2 * n + k], res[3 * n + k]) for k in range(n)], res[-1]


def _grads_wait(started, after, name):
    n = len(started)

    def body(*refs):
        g, land = refs[:n], refs[n:2 * n]
        send_sems, recv_sems = refs[2 * n:3 * n], refs[3 * n:4 * n]
        for k in range(n):
            for cp in _grad_copies(g[k], land[k], send_sems[k], recv_sems[k]):
                cp.wait_send()
                cp.wait_recv()

    gs = [st[2] for st in started]
    lands = [st[3] for st in started]
    res = pl.pallas_call(
        body, name=name,
        in_specs=[HBM] * (2 * n) + [SEM] * (2 * n) + [ANY], out_specs=[HBM] * (2 * n),
        out_shape=[pltpu.HBM(a.shape, a.dtype) for a in (*gs, *lands)],
        input_output_aliases={i: i for i in range(2 * n)},
        compiler_params=pltpu.CompilerParams(has_side_effects=DATAFLOW),
    )(*gs, *lands, *[st[0] for st in started], *[st[1] for st in started], after)
    return res[n:]


def _sum_partials(grad4, got, cb, name, tr):
    h = grad4.shape[2]
    per_half = h // tr

    def body(cb_ref, g_ref, o_ref, out_ref):
        acc = g_ref[...]
        for j in range(7):
            acc = acc + o_ref[j].astype(F32)
        out_ref[...] = acc

    return pl.pallas_call(
        body, name=name,
        grid_spec=pltpu.PrefetchScalarGridSpec(
            num_scalar_prefetch=1, grid=(per_half,),
            in_specs=[pl.BlockSpec((None, None, tr, D), lambda i, cb_ref: (cb_ref[1], cb_ref[0], i, 0)),
                      pl.BlockSpec((7, tr, D), lambda i, cb_ref: (0, i, 0))],
            out_specs=pl.BlockSpec((tr, D), lambda i, cb_ref: (cb_ref[0] * per_half + i, 0))),
        out_shape=jax.ShapeDtypeStruct((2 * h, D), F32),
        compiler_params=_cp(("arbitrary",)),
    )(cb, grad4, got)


def _swap_halves(shards, name):
    n = len(shards)

    def body(*refs):
        out, send_sems, recv_sems = refs[n:2 * n], refs[2 * n], refs[2 * n + 1]
        x, y, c = _place()
        cps = []
        for k in range(n):
            h = shards[k].shape[0] // 2
            mine = out[k].at[pl.ds(pl.multiple_of(c * h, 8), h)]
            cp = pltpu.make_async_remote_copy(src_ref=mine, dst_ref=mine, send_sem=send_sems.at[k],
                                              recv_sem=recv_sems.at[k], device_id=(x, y, 1 - c), device_id_type=MESH)
            cp.start()
            cps.append(cp)
        for cp in cps:
            cp.wait()

    return pl.pallas_call(
        body, name=name,
        in_specs=[ANY] * n, out_specs=[ANY] * n,
        out_shape=[jax.ShapeDtypeStruct(sh.shape, F32) for sh in shards],
        input_output_aliases={k: k for k in range(n)},
        scratch_shapes=[pltpu.SemaphoreType.DMA((n,)), pltpu.SemaphoreType.DMA((n,))],
        compiler_params=pltpu.CompilerParams(has_side_effects=True),
    )(*shards)


def _small_copies(small_ref, land_ref, send_sems, recv_sems):
    x, y, c = _place()
    me = 4 * x + 2 * y + c
    cps = []
    for d in range(1, 8):
        px, py, pc = x ^ (d >> 2), y ^ ((d >> 1) & 1), c ^ (d & 1)
        cps.append(pltpu.make_async_remote_copy(
            src_ref=small_ref, dst_ref=land_ref.at[me], send_sem=send_sems.at[d - 1], recv_sem=recv_sems.at[d - 1],
            device_id=(px, py, pc), device_id_type=MESH))
    return cps


def _small_start(small):
    land = lax.empty((8,) + small.shape, F32)

    def body(s_ref, l_ref, send_sems, recv_sems, s_thru, l_thru, token):
        for cp in _small_copies(s_ref, l_ref, send_sems, recv_sems):
            cp.start()
        token[...] = jnp.zeros_like(token)

    res = pl.pallas_call(
        body, name="small_start",
        in_specs=[HBM, HBM], out_specs=[SEM, SEM, HBM, HBM, VMEM],
        out_shape=[pltpu.SemaphoreType.DMA((7,)), pltpu.SemaphoreType.DMA((7,)), pltpu.HBM(small.shape, F32),
                   pltpu.HBM(land.shape, F32), jax.ShapeDtypeStruct((8, 128), F32)],
        input_output_aliases={0: 2, 1: 3},
        compiler_params=pltpu.CompilerParams(has_side_effects=DATAFLOW),
    )(_hbm(small), _hbm(land))
    return res[:4], res[4]


def _small_wait(started, after):
    send_sems, recv_sems, small, land = started

    def body(s_ref, l_ref, send_ref, recv_ref, after_ref, s_out, l_out):
        for cp in _small_copies(s_ref, l_ref, send_ref, recv_ref):
            cp.wait_send()
            cp.wait_recv()

    return pl.pallas_call(
        body, name="small_wait",
        in_specs=[HBM, HBM, SEM, SEM, ANY], out_specs=[HBM, HBM],
        out_shape=[pltpu.HBM(small.shape, F32), pltpu.HBM(land.shape, F32)],
        input_output_aliases={0: 0, 1: 1},
        compiler_params=pltpu.CompilerParams(has_side_effects=DATAFLOW),
    )(small, land, send_sems, recv_sems, after)


def _small_sum(small, land, me):
    rows = small.shape[0]

    def body(me_ref, s_ref, l_ref, o_ref):
        acc = None
        for k in range(8):
            term = jnp.where(me_ref[0] == k, s_ref[...], l_ref[k])
            acc = term if k == 0 else acc + term
        o_ref[...] = acc

    return pl.pallas_call(
        body, name="small_sum",
        in_specs=[SMEM, VMEM, VMEM], out_specs=VMEM,
        out_shape=jax.ShapeDtypeStruct((rows, D), F32),
    )(me, small, land)


def _adamw(w, g, m, v, name, tr):
    rows, cols = w.shape

    def body(w_ref, g_ref, m_ref, v_ref, d_ref, nm_ref, nv_ref):
        g_ = g_ref[...]
        nm = ADAM_B1 * m_ref[...] + (1.0 - ADAM_B1) * g_
        nv = ADAM_B2 * v_ref[...] + (1.0 - ADAM_B2) * (g_ * g_)
        m_hat = nm / (1.0 - ADAM_B1 ** ADAM_STEP)
        v_hat = nv / (1.0 - ADAM_B2 ** ADAM_STEP)
        d_ref[...] = -ADAM_LR * (m_hat / (jnp.sqrt(v_hat) + ADAM_EPS) + ADAM_WD * w_ref[...])
        nm_ref[...] = nm
        nv_ref[...] = nv

    spec = pl.BlockSpec((tr, cols), lambda i: (i, 0))
    return pl.pallas_call(
        body, name=name, grid=(rows // tr,),
        in_specs=[spec] * 4, out_specs=[spec] * 3,
        out_shape=[jax.ShapeDtypeStruct((rows, cols), F32)] * 3,
        compiler_params=_cp(("parallel",)),
    )(w, g, m, v)


def _local_step(x, target, w_in_t, late_weights, norm_a_g, norm_b_g, sinks_a, ln1_g, ln1_b,
                conv_w, conv_b, ln2_g, ln2_b, slopes, on_grad, on_small):
    cwb = jnp.concatenate([conv_w, conv_b[None]], axis=0).reshape(4, 2, FF)

    proj, xb = _proj(x, w_in_t, "proj")
    o_a, lse_a = _attn_a_fwd(proj, sinks_a)
    fwd_b = [_attn_b_fwd(proj, slopes, r) for r in B_DILATIONS]
    w_o = late_weights(1, fwd_b[-1][1])
    o_b, lse_b, cat, z1, h1, h1b = _mix_ln1(x, o_a, [f[0] for f in fwd_b], [f[1] for f in fwd_b],
                                           norm_a_g, norm_b_g, w_o, ln1_g, ln1_b)
    w_up = late_weights(2, h1b)
    up, a, gate, a1 = _conv_gelu(_up_proj(h1b, w_up), cwb)
    w_down = late_weights(3, a)
    dz2, dz2b, st2 = _down_ln2_loss(a, w_down, h1, target, ln2_g, ln2_b)

    on_grad(3, *_grad_w(a, dz2b, "grad_w_down", tm=FF // 2))
    dup, dconv = _conv_gelu_bwd(_d_act(dz2b, w_down), up, gate, a1, cwb)
    on_grad(2, *_grad_w(dup, h1b, "grad_w_up", tm=FF // 2, lhs_halves=True))
    dz1, dz1b, st1 = _dh1_ln1_bwd(dz2, dup, w_up, z1, ln1_g)
    tok = on_grad(1, *_grad_w(cat, dz1b, "grad_w_o", tm=512))
    d_oa, d_ob, st_n = _dcat_rms_bwd(dz1b, w_o, o_a, o_b, norm_a_g + tok[0, 0], norm_b_g)
    dqa, dka, dva, dsink = _attn_a_bwd(proj, sinks_a, d_oa, o_a, lse_a)
    dconv = dconv.reshape(4, 2 * FF)
    tok = on_small(dict(loss=st2[2, 0:1], norm_a_g=st_n[0], norm_b_g=st_n[1], sinks_a=dsink[:, 0],
                        ln1_g=st1[0], ln1_b=st1[1], conv_w=dconv[0:3].reshape(-1), conv_b=dconv[3],
                        ln2_g=st2[0], ln2_b=st2[1]))
    slopes = slopes + tok[0, 0]
    bwd_b = None
    for r in reversed(B_DILATIONS):
        bwd_b = _attn_b_bwd(proj, slopes, d_ob, o_b, lse_b, r, bwd_b)
    dproj = _dproj_combine(dqa, dka, dva, bwd_b)
    tok = on_grad(0, *_grad_w(dproj, xb, "grad_w_in", tm=WA))
    return _grad_x(dz1, dproj, w_in_t, tok)


SMALL_ORDER = ("loss", "norm_a_g", "norm_b_g", "sinks_a", "ln1_g", "ln1_b", "conv_b", "ln2_g", "ln2_b", "conv_w")
SMALL_SIZES = dict(loss=1, norm_a_g=512, norm_b_g=512, sinks_a=8, ln1_g=D, ln1_b=D, conv_b=2 * FF, ln2_g=D, ln2_b=D,
                   conv_w=3 * 2 * FF)


def _pack(parts, rows):
    flat = jnp.concatenate([parts[k].reshape(-1).astype(F32) for k in parts])
    return jnp.pad(flat, (0, rows * D - flat.shape[0])).reshape(rows, D)


def _unpack(buf, names, sizes):
    flat = buf.reshape(-1)
    out, at = {}, 0
    for k in names:
        out[k] = flat[at:at + sizes[k]]
        at += sizes[k]
    return out


def kernel(x, w_in, norm_a_g, norm_b_g, sinks_a, w_o, ln1_g, ln1_b, w_up, conv_w, conv_b, w_down, ln2_g, ln2_b, loss_target, m_w_in, m_norm_a_g, m_norm_b_g, m_sinks_a, m_w_o, m_ln1_g, m_ln1_b, m_w_up, m_conv_w, m_conv_b, m_w_down, m_ln2_g, m_ln2_b, v_w_in, v_norm_a_g, v_norm_b_g, v_sinks_a, v_w_o, v_ln1_g, v_ln1_b, v_w_up, v_conv_w, v_conv_b, v_w_down, v_ln2_g, v_ln2_b):
    xi, yi, ci = _place()
    chip = (2 * xi + yi).astype(I32)
    core = ci.astype(I32)

    w_in_rows, m_w_in_rows, v_w_in_rows = w_in.T, m_w_in.T, v_w_in.T
    shards = (w_in_rows.astype(BF16), w_o.astype(BF16), w_up.astype(BF16), w_down.astype(BF16))
    w_in_t, conv_w4 = _gather_w_in(shards[0], conv_w)
    conv_w_f = conv_w4.transpose(1, 0, 2).reshape(3, 2 * FF)
    w_started, w_tok = _weights_start(shards[1:], conv_w4)
    slopes = jnp.asarray(SLOPES, F32) + w_tok[0, 0]

    halves_rows = [r // 2 for r in SHARD_ROWS]
    grads4, grads_b4, started = [None] * 4, [None] * 4, [None] * 4

    def on_grad(k, g, g_b):
        grads4[k] = g.reshape(N_CHIPS, 2, halves_rows[k], D)
        grads_b4[k] = g_b.reshape(N_CHIPS, 2, halves_rows[k], D)
        if k > 1:
            return None
        group = (1, 2, 3) if k == 1 else (0,)
        sts, tok = _grads_start([grads_b4[i] for i in group], f"grads_start_{k}")
        for i, st in zip(group, sts):
            started[i] = st
        return tok

    small_rows = 32
    small_started = []

    def on_small(parts):
        st, tok = _small_start(_pack({k: parts[k] for k in SMALL_ORDER}, small_rows))
        small_started.append(st)
        return tok

    gx = _local_step(
        x[0], loss_target[0], w_in_t, lambda k, after: _weights_wait(w_started[k - 1], after, f"weights_wait_{k}"),
        norm_a_g, norm_b_g, sinks_a, ln1_g, ln1_b, conv_w_f, conv_b, ln2_g, ln2_b, slopes, on_grad, on_small)

    tiles = (96, 128, 352, 176)
    core_chip = jnp.stack([core, chip])
    got = _grads_wait(started[1:], gx, "grads_wait_1")
    halves = [_sum_partials(grads4[k], got[k - 1], core_chip, f"sum_partials_{k}", tiles[k]) for k in (1, 2, 3)]
    g_w_o, g_w_up_rows, g_w_down = _swap_halves(halves, "swap_halves")
    g_w_up = g_w_up_rows.T
    delta, new_m, new_v = {}, {}, {}
    for k, g, tr in (("w_o", g_w_o, 128), ("w_up", g_w_up, 256), ("w_down", g_w_down, 176)):
        delta[k], new_m[k], new_v[k] = _adamw(dict(w_o=w_o, w_up=w_up, w_down=w_down)[k], g,
                                              dict(w_o=m_w_o, w_up=m_w_up, w_down=m_w_down)[k],
                                              dict(w_o=v_w_o, w_up=v_w_up, w_down=v_w_down)[k], f"adamw_{k}", tr)

    got = _grads_wait(started[:1], delta["w_up"], "grads_wait_0")
    half_in = _sum_partials(grads4[0], got[0], core_chip, "sum_partials_0", tiles[0])
    (g_w_in_rows,) = _swap_halves([half_in], "swap_halves_in")
    small_mine, small_land = _small_wait(small_started[0], g_w_in_rows)
    totals = _small_sum(small_mine, small_land, (4 * xi + 2 * yi + ci).astype(I32).reshape(1))
    tot = _unpack(totals, SMALL_ORDER, SMALL_SIZES)
    loss = tot["loss"][0]
    cols = 2 * FF // N_CHIPS
    g_conv_w = lax.dynamic_slice(tot["conv_w"].reshape(3, 2 * FF), (0, chip * cols), (3, cols))
    g_small = dict(norm_a_g=tot["norm_a_g"], norm_b_g=tot["norm_b_g"], sinks_a=tot["sinks_a"], ln1_g=tot["ln1_g"],
                   ln1_b=tot["ln1_b"], conv_w=g_conv_w, conv_b=tot["conv_b"], ln2_g=tot["ln2_g"], ln2_b=tot["ln2_b"])

    weights = dict(w_in=w_in, norm_a_g=norm_a_g, norm_b_g=norm_b_g, sinks_a=sinks_a, w_o=w_o, ln1_g=ln1_g, ln1_b=ln1_b,
                   w_up=w_up, conv_w=conv_w, conv_b=conv_b, w_down=w_down, ln2_g=ln2_g, ln2_b=ln2_b)
    ms = dict(w_in=m_w_in, norm_a_g=m_norm_a_g, norm_b_g=m_norm_b_g, sinks_a=m_sinks_a, w_o=m_w_o, ln1_g=m_ln1_g,
              ln1_b=m_ln1_b, w_up=m_w_up, conv_w=m_conv_w, conv_b=m_conv_b, w_down=m_w_down, ln2_g=m_ln2_g, ln2_b=m_ln2_b)
    vs = dict(w_in=v_w_in, norm_a_g=v_norm_a_g, norm_b_g=v_norm_b_g, sinks_a=v_sinks_a, w_o=v_w_o, ln1_g=v_ln1_g,
              ln1_b=v_ln1_b, w_up=v_w_up, conv_w=v_conv_w, conv_b=v_conv_b, w_down=v_w_down, ln2_g=v_ln2_g, ln2_b=v_ln2_b)
    order = list(weights)
    grad = dict(g_small, w_in=g_w_in_rows.T, w_o=g_w_o, w_up=g_w_up, w_down=g_w_down)

    delta["w_in"], new_m["w_in"], new_v["w_in"] = [
        a.T for a in _adamw(w_in_rows, g_w_in_rows, m_w_in_rows, v_w_in_rows, "adamw_w_in", 144)]
    small_names = [k for k in order if k not in delta]
    sizes = {k: weights[k].size for k in small_names}
    rows = 16
    packed = [_pack({k: src[k] for k in small_names}, rows) for src in (weights, grad, ms, vs)]
    for res, buf in zip((delta, new_m, new_v), _adamw(*packed, "adamw_small", rows)):
        for k, val in _unpack(buf, small_names, sizes).items():
            res[k] = val.reshape(weights[k].shape)

    return (loss, gx[None], *[grad[k] for k in order], *[delta[k] for k in order],
            *[new_m[k] for k in order], *[new_v[k] for k in order])
```

```python
import functools
import math

import jax
import jax.numpy as jnp
from jax import lax
from jax.experimental import pallas as pl
from jax.experimental.pallas import tpu as pltpu

F32, BF16, I32 = jnp.float32, jnp.bfloat16, jnp.int32

D = 1024
FF = 2816
HD = 64
NH = 8
WA, WB = 768, 1536
WIN = WA + WB
BLK = 128
ALPHA = 2.0 ** 0.25
LN_EPS, RMS_EPS = 1e-5, 1e-6
SCALE = 1.0 / math.sqrt(HD)
A_MAX_DIST, B_MAX_DIST = 127, 128
B_DILATIONS = (1, 4, 16)
SLOPES = tuple(2.0 ** (-(i + 1)) for i in range(NH))
SHARD_ROWS = (WIN // 4, D // 4, 2 * FF // 4, FF // 4)
N_CHIPS = 4
ADAM_LR, ADAM_B1, ADAM_B2, ADAM_EPS, ADAM_WD, ADAM_STEP = 0.001, 0.9, 0.999, 1e-08, 0.01, 10
MESH = pl.DeviceIdType.MESH
ANY = pl.BlockSpec(memory_space=pl.ANY)
SMEM = pl.BlockSpec(memory_space=pltpu.SMEM)
VMEM = pl.BlockSpec(memory_space=pltpu.VMEM)
HBM = pl.BlockSpec(memory_space=pltpu.HBM)
SEM = pl.BlockSpec(memory_space=pltpu.SEMAPHORE)
DATAFLOW = pltpu.SideEffectType.DATAFLOW_SIDE_EFFECTING


def _cp(sem, mb=48):
    return pltpu.CompilerParams(dimension_semantics=sem, vmem_limit_bytes=mb << 20)


def _nn(a, b):
    return lax.dot_general(a, b, (((1,), (0,)), ((), ())), preferred_element_type=F32)


def _nt(a, b):
    return lax.dot_general(a, b, (((1,), (1,)), ((), ())), preferred_element_type=F32)


def _tn(a, b):
    return lax.dot_general(a, b, (((0,), (0,)), ((), ())), preferred_element_type=F32)


def _resident(shape):
    n = len(shape)
    return pl.BlockSpec(shape, lambda *_: (0,) * n, pipeline_mode=pl.Buffered(1))


def _const(shape):
    n = len(shape)
    return pl.BlockSpec(shape, lambda *_: (0,) * n)


def _proj(x, w_t, name, tm=512):
    s = x.shape[0]
    n = w_t.shape[0]

    def body(x_ref, w_ref, o_ref, xb_ref):
        xb = x_ref[...].astype(BF16)
        xb_ref[...] = xb
        res = _nt(xb, w_ref[...])
        for g in range(n // 128):
            o_ref[g] = res[:, 128 * g:128 * (g + 1)]

    return pl.pallas_call(
        body, name=name, grid=(s // tm,),
        in_specs=[pl.BlockSpec((tm, D), lambda i: (i, 0)), _resident((n, D))],
        out_specs=[pl.BlockSpec((n // 128, tm, 128), lambda i: (0, i, 0)), pl.BlockSpec((tm, D), lambda i: (i, 0))],
        out_shape=[jax.ShapeDtypeStruct((n // 128, s, 128), F32), jax.ShapeDtypeStruct((s, D), BF16)],
        compiler_params=_cp(("parallel",)),
    )(x, w_t)


def _grad_w(lhs, rhs, name, tm, tk=2048, lhs_halves=False):
    s = rhs.shape[0]
    if lhs_halves:
        per_half = lhs.shape[2] // tm
        n = 2 * lhs.shape[2]
        lhs_spec = pl.BlockSpec((None, tk, tm), lambda i, k: (i // per_half, k, i % per_half))
    else:
        n = lhs.shape[1]
        lhs_spec = pl.BlockSpec((tk, tm), lambda i, k: (k, i))
    nk = s // tk

    def body(l_ref, r_ref, o_ref, ob_ref):
        k = pl.program_id(1)

        @pl.when(k == 0)
        def _():
            o_ref[...] = jnp.zeros_like(o_ref)

        o_ref[...] += _tn(l_ref[...], r_ref[...])

        @pl.when(k == nk - 1)
        def _():
            ob_ref[...] = o_ref[...].astype(BF16)

    return pl.pallas_call(
        body, name=name, grid=(n // tm, nk),
        in_specs=[lhs_spec, pl.BlockSpec((tk, D), lambda i, k: (k, 0))],
        out_specs=[pl.BlockSpec((tm, D), lambda i, k: (i, 0))] * 2,
        out_shape=[jax.ShapeDtypeStruct((n, D), F32), jax.ShapeDtypeStruct((n, D), BF16)],
        compiler_params=_cp(("parallel", "arbitrary")),
    )(lhs, rhs)


def _band_base(max_dist, dist_unit, first):
    row = lax.broadcasted_iota(I32, (BLK, 2 * BLK), 0)
    col = lax.broadcasted_iota(I32, (BLK, 2 * BLK), 1)
    dist = BLK + row - col
    ok = (dist >= 0) & (dist <= max_dist)
    if first:
        ok = ok & (col >= BLK)
    return jnp.where(ok, dist.astype(F32) * (-float(dist_unit)), -jnp.inf)


def _half_mask(shape, e):
    lane = lax.broadcasted_iota(I32, shape, 1)
    return (lane < HD) if e == 0 else (lane >= HD)


def _to_half(x, e, g):
    if g != e:
        x = pltpu.roll(x, HD, 1)
    return jnp.where(_half_mask(x.shape, g), x, 0.0)


def _stack_heads(scalars, tile):
    return jnp.concatenate([scalars[0] * tile, scalars[1] * tile], axis=0)


def _pair_fwd(q2, kb, vb, base, slopes, kv_heads, sinks):
    lo = _half_mask((BLK, 2 * HD), 0)
    if slopes is None:
        bias = base
    elif sinks is None:
        bias = _stack_heads(slopes, base)
    else:
        col0 = lax.broadcasted_iota(I32, base.shape, 1) == 0
        bias = jnp.concatenate([jnp.where(col0, sinks[e], slopes[e] * base) for e in (0, 1)], axis=0)
    qs = jnp.concatenate([_to_half(q2, e, kv_heads[e]) * SCALE for e in (0, 1)], axis=0).astype(BF16)
    s = _nt(qs, kb) + bias
    m = jnp.max(s, axis=1, keepdims=True)
    p = jnp.exp(s - m)
    l = jnp.sum(p, axis=1, keepdims=True)
    o = _nn(p.astype(BF16), vb) / l
    lse = m + jnp.log(l)
    halves = []
    for e in (0, 1):
        oh = o[e * BLK:(e + 1) * BLK]
        halves.append(pltpu.roll(oh, HD, 1) if kv_heads[e] != e else oh)
    o2 = jnp.where(lo, halves[0], halves[1])
    lse2 = jnp.where(lo, jnp.broadcast_to(lse[:BLK], (BLK, 2 * HD)), jnp.broadcast_to(lse[BLK:], (BLK, 2 * HD)))
    return o2, lse2


def _pair_bwd(q2, kb, vb, do2, o2, lse2, base, slopes, kv_heads, sinks):
    lo = _half_mask((BLK, 2 * HD), 0)
    prod = do2 * o2
    lses, deltas = [], []
    for e in (0, 1):
        hq = _half_mask((BLK, 2 * HD), e)
        lses.append(jnp.max(jnp.where(hq, lse2, -jnp.inf), axis=1, keepdims=True))
        deltas.append(jnp.sum(jnp.where(hq, prod, 0.0), axis=1, keepdims=True))
    lse = jnp.concatenate(lses, axis=0)
    delta = jnp.concatenate(deltas, axis=0)
    qs = jnp.concatenate([_to_half(q2, e, kv_heads[e]) * SCALE for e in (0, 1)], axis=0).astype(BF16)
    dos = jnp.concatenate([_to_half(do2, e, kv_heads[e]) for e in (0, 1)], axis=0).astype(BF16)
    p = jnp.exp(_nt(qs, kb) + (base if slopes is None else _stack_heads(slopes, base)) - lse)
    ds = (p * (_nt(dos, vb) - delta)).astype(BF16)
    dq = _nn(ds, kb) * SCALE
    halves = []
    for e in (0, 1):
        dqh = dq[e * BLK:(e + 1) * BLK]
        halves.append(pltpu.roll(dqh, HD, 1) if kv_heads[e] != e else dqh)
    dq2 = jnp.where(lo, halves[0], halves[1])
    dk2 = _tn(ds, qs)
    dv2 = _tn(p.astype(BF16), dos)
    dsinks = []
    if sinks is not None:
        for e in (0, 1):
            dsinks.append(jnp.sum(-jnp.exp(sinks[e] - lses[e]) * deltas[e], axis=0, keepdims=True))
    return dq2, dk2, dv2, dsinks


A_BLOCKS_PER_STEP = 2
A_BLOCKS_PER_STEP_BWD = 1


def _attn_a_fwd(proj, sinks):
    s = proj.shape[1]
    nq = A_BLOCKS_PER_STEP
    rows = BLK * nq
    steps = s // rows

    def body(sink_ref, q_ref, kp_ref, kc_ref, vp_ref, vc_ref, o_ref, lse_ref):
        n = pl.program_id(0)
        base_rest = _band_base(A_MAX_DIST, 1, False)
        base_0 = jnp.where(n > 0, base_rest, _band_base(A_MAX_DIST, 1, True))
        for i in range(nq):
            cur = pl.ds(i * BLK, BLK)
            k_prev = kc_ref[pl.ds((i - 1) * BLK, BLK), :] if i > 0 else kp_ref[...]
            v_prev = vc_ref[pl.ds((i - 1) * BLK, BLK), :] if i > 0 else vp_ref[...]
            first_key = lax.broadcasted_iota(I32, (2 * BLK, 128), 0) == 0
            kb = jnp.where(first_key, 0.0, jnp.concatenate([k_prev, kc_ref[cur, :]], axis=0)).astype(BF16)
            vb = jnp.where(first_key, 0.0, jnp.concatenate([v_prev, vc_ref[cur, :]], axis=0)).astype(BF16)
            for j in range(NH // 2):
                g = j // 2
                o2, lse2 = _pair_fwd(q_ref[j, cur, :], kb, vb, base_rest if i > 0 else base_0,
                                     (SLOPES[2 * j], SLOPES[2 * j + 1]), (g, g), (sink_ref[2 * j], sink_ref[2 * j + 1]))
                o_ref[j, cur, :] = o2
                lse_ref[j, cur, :] = lse2

    before = lambda n: jnp.maximum(n * nq - 1, 0)
    slab = lambda g: pl.BlockSpec((None, rows, 128), lambda n: (g, n, 0))
    edge = lambda g: pl.BlockSpec((None, BLK, 128), lambda n: (g, before(n), 0))
    quad = pl.BlockSpec((4, rows, 128), lambda n: (0, n, 0))
    return pl.pallas_call(
        body, name="attn_a_fwd", grid=(steps,),
        in_specs=[SMEM, quad, edge(4), slab(4), edge(5), slab(5)],
        out_specs=[quad, quad],
        out_shape=[jax.ShapeDtypeStruct((4, s, 128), F32)] * 2,
        compiler_params=_cp(("parallel",)),
    )(sinks, proj, proj, proj, proj, proj)


def _attn_a_bwd(proj, sinks, d_o, o, lse):
    s = proj.shape[1]
    nq = A_BLOCKS_PER_STEP_BWD
    rows = BLK * nq
    steps = s // rows

    def body(sink_ref, q_ref, kp_ref, kc_ref, vp_ref, vc_ref, do_ref, o_ref, lse_ref,
             dq_ref, dk_ref, dv_ref, dsink_ref, kcar, vcar):
        n = pl.program_id(0)

        @pl.when(n == 0)
        def _():
            kcar[...] = jnp.zeros_like(kcar)
            vcar[...] = jnp.zeros_like(vcar)
            dsink_ref[...] = jnp.zeros_like(dsink_ref)

        dk_ref[...] = kcar[...]
        dv_ref[...] = vcar[...]

        @pl.when(n < steps)
        def _():
            base_rest = _band_base(A_MAX_DIST, 1, False)
            base_0 = jnp.where(n > 0, base_rest, _band_base(A_MAX_DIST, 1, True))
            for i in range(nq):
                cur = pl.ds(i * BLK, BLK)
                k_prev = kc_ref[pl.ds((i - 1) * BLK, BLK), :] if i > 0 else kp_ref[...]
                v_prev = vc_ref[pl.ds((i - 1) * BLK, BLK), :] if i > 0 else vp_ref[...]
                kb = jnp.concatenate([k_prev, kc_ref[cur, :]], axis=0).astype(BF16)
                vb = jnp.concatenate([v_prev, vc_ref[cur, :]], axis=0).astype(BF16)
                dk_win = dv_win = None
                for j in range(NH // 2):
                    g = j // 2
                    dq2, dk2, dv2, dsk = _pair_bwd(q_ref[j, cur, :], kb, vb, do_ref[j, cur, :], o_ref[j, cur, :],
                                                   lse_ref[j, cur, :], base_rest if i > 0 else base_0,
                                                   (SLOPES[2 * j], SLOPES[2 * j + 1]), (g, g),
                                                   (sink_ref[2 * j], sink_ref[2 * j + 1]))
                    dq_ref[j, cur, :] = dq2
                    dk_win = dk2 if j == 0 else dk_win + dk2
                    dv_win = dv2 if j == 0 else dv_win + dv2
                    for e in (0, 1):
                        h = 2 * j + e
                        dsink_ref[h:h + 1, :] += jnp.broadcast_to(dsk[e], (1, 128))
                if i == 0:
                    last = pl.ds((nq - 1) * BLK, BLK)
                    dk_ref[last, :] += dk_win[:BLK]
                    dv_ref[last, :] += dv_win[:BLK]
                else:
                    kcar[pl.ds((i - 1) * BLK, BLK), :] += dk_win[:BLK]
                    vcar[pl.ds((i - 1) * BLK, BLK), :] += dv_win[:BLK]
                kcar[cur, :] = dk_win[BLK:]
                vcar[cur, :] = dv_win[BLK:]

    cur_step = lambda n: jnp.minimum(n, steps - 1)
    before = lambda n: jnp.maximum(cur_step(n) * nq - 1, 0)
    out_prev = lambda n: jnp.maximum(n - 1, 0)
    quad = pl.BlockSpec((4, rows, 128), lambda n: (0, cur_step(n), 0))
    slab = lambda g: pl.BlockSpec((None, rows, 128), lambda n: (g, cur_step(n), 0))
    edge = lambda g: pl.BlockSpec((None, BLK, 128), lambda n: (g, before(n), 0))
    return pl.pallas_call(
        body, name="attn_a_bwd", grid=(steps + 1,),
        in_specs=[SMEM, quad, edge(4), slab(4), edge(5), slab(5), quad, quad, quad],
        out_specs=[quad,
                   pl.BlockSpec((rows, 128), lambda n: (out_prev(n), 0)),
                   pl.BlockSpec((rows, 128), lambda n: (out_prev(n), 0)),
                   pl.BlockSpec((NH, 128), lambda n: (0, 0))],
        out_shape=[jax.ShapeDtypeStruct((4, s, 128), F32), jax.ShapeDtypeStruct((s, 128), F32),
                   jax.ShapeDtypeStruct((s, 128), F32), jax.ShapeDtypeStruct((NH, 128), F32)],
        scratch_shapes=[pltpu.VMEM((rows, 128), F32), pltpu.VMEM((rows, 128), F32)],
        compiler_params=_cp(("arbitrary",)),
    )(sinks, proj, proj, proj, proj, proj, d_o, o, lse)


def _stream(rho, i, r):
    start = i * BLK * r + rho
    return pl.ds(start, BLK, stride=r) if r > 1 else pl.ds(start, BLK)


def _for_streams(r, fn, side_by_side=4):
    if r <= side_by_side:
        for rho in range(r):
            fn(rho)
    else:
        def group(it, carry):
            for u in range(side_by_side):
                fn(side_by_side * it + u)
            return carry

        lax.fori_loop(0, r // side_by_side, group, 0)


B_BLOCKS_PER_STEP = {1: 8, 4: 2, 16: 1}
B_BLOCKS_PER_STEP_FWD = {1: 16, 4: 4, 16: 1}


def _attn_b_fwd(proj, slopes, r):
    s = proj.shape[1]
    nq = B_BLOCKS_PER_STEP_FWD[r]
    rows = BLK * r * nq
    steps = s // rows
    qc, kc, vc = WA // 128, WA // 128 + 4, WA // 128 + 8

    def body(slope_ref, q_ref, kp_ref, kc_ref, vp_ref, vc_ref, o_ref, lse_ref):
        j = pl.program_id(0)
        sb = pl.program_id(1)
        sl2 = (slope_ref[2 * j], slope_ref[2 * j + 1])
        bias_rest = _stack_heads(sl2, _band_base(B_MAX_DIST, r, False))
        bias_0 = jnp.where(sb > 0, bias_rest, _stack_heads(sl2, _band_base(B_MAX_DIST, r, True)))

        def stream(rho):
            for i in range(nq):
                cur = _stream(rho, i, r)
                k_prev = kc_ref[_stream(rho, i - 1, r), :] if i > 0 else kp_ref[_stream(rho, 0, r), :]
                v_prev = vc_ref[_stream(rho, i - 1, r), :] if i > 0 else vp_ref[_stream(rho, 0, r), :]
                kb = jnp.concatenate([k_prev, kc_ref[cur, :]], axis=0).astype(BF16)
                vb = jnp.concatenate([v_prev, vc_ref[cur, :]], axis=0).astype(BF16)
                o2, lse2 = _pair_fwd(q_ref[cur, :], kb, vb, bias_rest if i > 0 else bias_0, None, (0, 1), None)
                o_ref[cur, :] = o2
                lse_ref[cur, :] = lse2

        _for_streams(r, stream, side_by_side=16)

    before = lambda sb: jnp.maximum(sb * nq - 1, 0)
    return pl.pallas_call(
        body, name=f"attn_b_fwd_r{r}", grid=(NH // 2, steps),
        in_specs=[SMEM,
                  pl.BlockSpec((None, rows, 128), lambda j, sb: (qc + j, sb, 0)),
                  pl.BlockSpec((None, BLK * r, 128), lambda j, sb: (kc + j, before(sb), 0)),
                  pl.BlockSpec((None, rows, 128), lambda j, sb: (kc + j, sb, 0)),
                  pl.BlockSpec((None, BLK * r, 128), lambda j, sb: (vc + j, before(sb), 0)),
                  pl.BlockSpec((None, rows, 128), lambda j, sb: (vc + j, sb, 0))],
        out_specs=[pl.BlockSpec((None, rows, 128), lambda j, sb: (j, sb, 0))] * 2,
        out_shape=[jax.ShapeDtypeStruct((4, s, 128), F32)] * 2,
        compiler_params=_cp(("parallel", "parallel")),
    )(slopes, proj, proj, proj, proj, proj)


def _attn_b_bwd(proj, slopes, d_o, o, lse, r, so_far=None):
    s = proj.shape[1]
    nq = B_BLOCKS_PER_STEP[r]
    rows = BLK * r * nq
    steps = s // rows
    qc, kc, vc = WA // 128, WA // 128 + 4, WA // 128 + 8
    chained = so_far is not None

    def body(slope_ref, q_ref, kp_ref, kc_ref, vp_ref, vc_ref, do_ref, o_ref, lse_ref, *rest):
        if chained:
            pq_ref, pk_ref, pv_ref, dq_ref, dk_ref, dv_ref, kcar, vcar = rest
        else:
            dq_ref, dk_ref, dv_ref, kcar, vcar = rest
        j = pl.program_id(0)
        sb = pl.program_id(1)

        @pl.when(sb == 0)
        def _():
            kcar[...] = jnp.zeros_like(kcar)
            vcar[...] = jnp.zeros_like(vcar)

        if chained:
            dk_ref[...] = kcar[...] + pk_ref[...]
            dv_ref[...] = vcar[...] + pv_ref[...]
        else:
            dk_ref[...] = kcar[...]
            dv_ref[...] = vcar[...]

        @pl.when(sb < steps)
        def _():
            sl2 = (slope_ref[2 * j], slope_ref[2 * j + 1])
            bias_rest = _stack_heads(sl2, _band_base(B_MAX_DIST, r, False))
            bias_0 = jnp.where(sb > 0, bias_rest, _stack_heads(sl2, _band_base(B_MAX_DIST, r, True)))

            def stream(rho):
                for i in range(nq):
                    cur = _stream(rho, i, r)
                    k_prev = kc_ref[_stream(rho, i - 1, r), :] if i > 0 else kp_ref[_stream(rho, 0, r), :]
                    v_prev = vc_ref[_stream(rho, i - 1, r), :] if i > 0 else vp_ref[_stream(rho, 0, r), :]
                    kb = jnp.concatenate([k_prev, kc_ref[cur, :]], axis=0).astype(BF16)
                    vb = jnp.concatenate([v_prev, vc_ref[cur, :]], axis=0).astype(BF16)
                    dq2, dk2, dv2, _ = _pair_bwd(q_ref[cur, :], kb, vb, do_ref[cur, :], o_ref[cur, :], lse_ref[cur, :],
                                                 bias_rest if i > 0 else bias_0, None, (0, 1), None)
                    dq_ref[cur, :] = dq2 + pq_ref[cur, :] if chained else dq2
                    if i == 0:
                        last = _stream(rho, nq - 1, r)
                        dk_ref[last, :] += dk2[:BLK]
                        dv_ref[last, :] += dv2[:BLK]
                    else:
                        kcar[_stream(rho, i - 1, r), :] += dk2[:BLK]
                        vcar[_stream(rho, i - 1, r), :] += dv2[:BLK]
                    kcar[cur, :] = dk2[BLK:]
                    vcar[cur, :] = dv2[BLK:]

            _for_streams(r, stream, side_by_side=8)

    cur_step = lambda sb: jnp.minimum(sb, steps - 1)
    before = lambda sb: jnp.maximum(cur_step(sb) * nq - 1, 0)
    out_prev = lambda sb: jnp.maximum(sb - 1, 0)
    tile = lambda slab: pl.BlockSpec((None, rows, 128), lambda j, sb: (slab + j, cur_step(sb), 0))
    edge = lambda slab: pl.BlockSpec((None, BLK * r, 128), lambda j, sb: (slab + j, before(sb), 0))
    late = pl.BlockSpec((None, rows, 128), lambda j, sb: (j, out_prev(sb), 0))
    grads = [tile(0), late, late]
    return pl.pallas_call(
        body, name=f"attn_b_bwd_r{r}", grid=(NH // 2, steps + 1),
        in_specs=[SMEM, tile(qc), edge(kc), tile(kc), edge(vc), tile(vc), tile(0), tile(0), tile(0)]
        + (grads if chained else []),
        out_specs=grads,
        out_shape=[jax.ShapeDtypeStruct((4, s, 128), F32)] * 3,
        scratch_shapes=[pltpu.VMEM((rows, 128), F32), pltpu.VMEM((rows, 128), F32)],
        compiler_params=_cp(("parallel", "arbitrary")),
    )(slopes, proj, proj, proj, proj, proj, d_o, o, lse, *(so_far if chained else ()))


def _row(v):
    return v.reshape(1, -1)


def _layer_norm_stats(z):
    mu = jnp.mean(z, axis=-1, keepdims=True)
    zc = z - mu
    var = jnp.mean(zc * zc, axis=-1, keepdims=True)
    rstd = lax.rsqrt(var + LN_EPS)
    return zc * rstd, rstd


def _layer_norm_bwd(dh, zh, rstd, g):
    dzh = dh * g
    return rstd * (dzh - jnp.mean(dzh, axis=-1, keepdims=True) - zh * jnp.mean(dzh * zh, axis=-1, keepdims=True))


def _rms(o):
    return lax.rsqrt(jnp.mean(o * o, axis=-1, keepdims=True) + RMS_EPS)


def _mix_ln1(x, o_a, o_b, lse_b, norm_a_g, norm_b_g, w_o, ln1_g, ln1_b, tm=256):
    s = x.shape[0]

    def wide(ref):
        return jnp.concatenate([ref[j] for j in range(4)], axis=1)

    def body(x_ref, oa_ref, ob1, ob2, ob3, l1, l2, l3, ga_ref, gb_ref, wo_ref, g_ref, b_ref,
             obm_ref, lse_ref, cat_ref, z1_ref, h1_ref, h1b_ref):
        la, lb, lc = wide(l1), wide(l2), wide(l3)
        m = jnp.maximum(jnp.maximum(la, lb), lc)
        ea, eb, ec = jnp.exp(la - m), jnp.exp(lb - m), jnp.exp(lc - m)
        den = ea + eb + ec
        obm = (ea / den) * wide(ob1) + (eb / den) * wide(ob2) + (ec / den) * wide(ob3)
        lse = m + jnp.log(den)
        for j in range(4):
            obm_ref[j] = obm[:, 128 * j:128 * (j + 1)]
            lse_ref[j] = lse[:, 128 * j:128 * (j + 1)]
        oa = wide(oa_ref)
        na = oa * _rms(oa) * ga_ref[...]
        nb_ = obm * _rms(obm) * gb_ref[...]
        cat = jnp.concatenate([na, nb_], axis=1).astype(BF16)
        cat_ref[...] = cat
        z1 = ALPHA * x_ref[...] + _nn(cat, wo_ref[...])
        z1_ref[...] = z1
        zh, _ = _layer_norm_stats(z1)
        h1 = zh * g_ref[...] + b_ref[...]
        h1_ref[...] = h1
        h1b_ref[...] = h1.astype(BF16)

    t512 = pl.BlockSpec((4, tm, 128), lambda i: (0, i, 0))
    td = pl.BlockSpec((tm, D), lambda i: (i, 0))
    return pl.pallas_call(
        body, name="mix_ln1", grid=(s // tm,),
        in_specs=[td] + [t512] * 7 + [_const((1, 512))] * 2 + [_resident((D, D))] + [_const((1, D))] * 2,
        out_specs=[t512, t512, td, td, td, td],
        out_shape=[jax.ShapeDtypeStruct((4, s, 128), F32), jax.ShapeDtypeStruct((4, s, 128), F32),
                   jax.ShapeDtypeStruct((s, D), BF16), jax.ShapeDtypeStruct((s, D), F32),
                   jax.ShapeDtypeStruct((s, D), F32), jax.ShapeDtypeStruct((s, D), BF16)],
        compiler_params=_cp(("parallel",)),
    )(x, o_a, *o_b, *lse_b, _row(norm_a_g), _row(norm_b_g), w_o, _row(ln1_g), _row(ln1_b))


def _gelu_and_grad(x):
    c = math.sqrt(2.0 / math.pi)
    x2 = x * x
    cx = c * x
    t = jnp.tanh(cx * (1.0 + 0.044715 * x2))
    q = 1.0 + t
    g = (0.5 * x) * q
    dg = 0.5 * q + ((0.5 * cx) * (1.0 - t * t)) * (1.0 + (3.0 * 0.044715) * x2)
    return g, dg


def _shift_down(u, before):
    n = u.shape[0]
    ext = jnp.concatenate([before, u], axis=0)
    return pltpu.roll(ext, 1, 0)[8:], pltpu.roll(ext, 2, 0)[8:]


def _shift_up(u, after):
    n = u.shape[0]
    ext = jnp.concatenate([u, after], axis=0)
    return pltpu.roll(ext, n + 7, 0)[:n], pltpu.roll(ext, n + 6, 0)[:n]


def _up_proj(h1b, w_up, tm=512):
    s = h1b.shape[0]

    def body(h_ref, w_ref, o_ref):
        h = h_ref[...]
        for half in (0, 1):
            o_ref[half] = _nn(h, w_ref[:, half * FF:(half + 1) * FF]).astype(BF16)

    return pl.pallas_call(
        body, name="up_proj", grid=(s // tm,),
        in_specs=[pl.BlockSpec((tm, D), lambda i: (i, 0)), _resident((D, 2 * FF))],
        out_specs=pl.BlockSpec((2, tm, FF), lambda i: (0, i, 0)),
        out_shape=jax.ShapeDtypeStruct((2, s, FF), BF16),
        compiler_params=_cp(("parallel",)),
    )(h1b, w_up)


def _conv_gelu(up, cwb, tm=256, tn=FF // 2, chunk_rows=16):
    s = up.shape[1]
    n_c = tm // chunk_rows

    def body(up_ref, c_ref, a_ref, g_ref, a1_ref, carry):
        @pl.when(pl.program_id(1) == 0)
        def _():
            carry[...] = jnp.zeros_like(carry)

        edge = (carry[0], carry[1])
        for c in range(n_c):
            rows = pl.ds(c * chunk_rows, chunk_rows)
            u, last = [], []
            for half in (0, 1):
                x = up_ref[half, rows, :].astype(F32)
                r1, r2 = _shift_down(x, edge[half])
                u.append(r2 * c_ref[0, half:half + 1, :] + r1 * c_ref[1, half:half + 1, :]
                         + x * c_ref[2, half:half + 1, :] + c_ref[3, half:half + 1, :])
                last.append(x[chunk_rows - 8:])
            g, dg = _gelu_and_grad(u[0])
            a_ref[rows, :] = (g * u[1]).astype(BF16)
            g_ref[rows, :] = g.astype(BF16)
            a1_ref[rows, :] = (u[1] * dg).astype(BF16)
            edge = tuple(last)
        for half in (0, 1):
            carry[half] = edge[half]

    pair = pl.BlockSpec((2, tm, tn), lambda j, i: (0, i, j))
    tile = pl.BlockSpec((tm, tn), lambda j, i: (i, j))
    return pl.pallas_call(
        body, name="conv_gelu", grid=(FF // tn, s // tm),
        in_specs=[pair, pl.BlockSpec((4, 2, tn), lambda j, i: (0, 0, j))],
        out_specs=[tile, tile, tile],
        out_shape=[jax.ShapeDtypeStruct((s, FF), BF16)] * 3,
        scratch_shapes=[pltpu.VMEM((2, 8, tn), F32)],
        compiler_params=_cp(("parallel", "arbitrary")),
    )(up, cwb)


def _down_ln2_loss(a, w_down, h1, target, ln2_g, ln2_b, tm=512):
    s = a.shape[0]

    def body(a_ref, w_ref, h_ref, t_ref, g_ref, b_ref, dz_ref, dzb_ref, st_ref):
        @pl.when(pl.program_id(0) == 0)
        def _():
            st_ref[...] = jnp.zeros_like(st_ref)

        z2 = ALPHA * h_ref[...] + _nn(a_ref[...], w_ref[...])
        zh, rstd = _layer_norm_stats(z2)
        diff = zh * g_ref[...] + b_ref[...] - t_ref[...]
        part = 0.5 * jnp.sum(jnp.mean(diff * diff, axis=-1, keepdims=True), axis=0, keepdims=True)
        dy = diff * (1.0 / D)
        st_ref[0:1, :] += jnp.sum(dy * zh, axis=0, keepdims=True)
        st_ref[1:2, :] += jnp.sum(dy, axis=0, keepdims=True)
        st_ref[2:3, :] += jnp.broadcast_to(part, (1, D))
        dz = _layer_norm_bwd(dy, zh, rstd, g_ref[...])
        dz_ref[...] = dz
        dzb_ref[...] = dz.astype(BF16)

    td = pl.BlockSpec((tm, D), lambda i: (i, 0))
    return pl.pallas_call(
        body, name="down_ln2_loss", grid=(s // tm,),
        in_specs=[pl.BlockSpec((tm, FF), lambda i: (i, 0)), _resident((FF, D)), td, td, _const((1, D)), _const((1, D))],
        out_specs=[td, td, _const((8, D))],
        out_shape=[jax.ShapeDtypeStruct((s, D), F32), jax.ShapeDtypeStruct((s, D), BF16),
                   jax.ShapeDtypeStruct((8, D), F32)],
        compiler_params=_cp(("arbitrary",)),
    )(a, w_down, h1, target, _row(ln2_g), _row(ln2_b))


def _d_act(dz2b, w_down, tm=512):
    s = dz2b.shape[0]

    def body(dz_ref, w_ref, o_ref):
        o_ref[...] = _nt(dz_ref[...], w_ref[...])

    return pl.pallas_call(
        body, name="d_act", grid=(s // tm,),
        in_specs=[pl.BlockSpec((tm, D), lambda i: (i, 0)), _resident((FF, D))],
        out_specs=pl.BlockSpec((tm, FF), lambda i: (i, 0)),
        out_shape=jax.ShapeDtypeStruct((s, FF), F32),
        compiler_params=_cp(("parallel",)),
    )(dz2b, w_down)


def _conv_gelu_bwd(da, up, g, a1, cwb, tm=256, tn=FF // 2, chunk_rows=16):
    s = da.shape[0]
    n_i = s // tm
    n_c = tm // chunk_rows

    def body(da_ref, up_ref, g_ref, a1_ref, c_ref, dup_ref, dc_ref, carry):
        @pl.when(pl.program_id(1) == 0)
        def _():
            carry[...] = jnp.zeros_like(carry)
            dc_ref[...] = jnp.zeros_like(dc_ref)

        def fold(v):
            return jnp.sum(v.reshape(chunk_rows // 8, 8, v.shape[1]), axis=0)

        def chunk(cc, state):
            after, sums = state
            rows = pl.ds((n_c - 1 - cc) * chunk_rows, chunk_rows)
            da_c = da_ref[rows, :]
            dus = (da_c * a1_ref[rows, :].astype(F32), da_c * g_ref[rows, :].astype(F32))
            head, new_sums = [], []
            for half in (0, 1):
                du = dus[half]
                up = up_ref[half, rows, :].astype(F32)
                l1, l2 = _shift_up(du, after[half])
                dup = (du * c_ref[2, half:half + 1, :] + l1 * c_ref[1, half:half + 1, :]
                       + l2 * c_ref[0, half:half + 1, :])
                dup_ref[half, rows, :] = dup.astype(BF16)
                parts = (fold(l2 * up), fold(l1 * up), fold(du * up), fold(du))
                new_sums.append(parts if sums is None else tuple(a + b for a, b in zip(sums[half], parts)))
                head.append(du[:8])
            return tuple(head), new_sums

        state = ((carry[0], carry[1]), None)
        for cc in range(n_c):
            state = chunk(cc, state)
        head, sums = state
        for half in (0, 1):
            carry[half] = head[half]
            for k in range(4):
                dc_ref[k, half:half + 1, :] += jnp.sum(sums[half][k], axis=0, keepdims=True)

    rev = lambda ii: n_i - 1 - ii
    tile = pl.BlockSpec((tm, tn), lambda j, ii: (rev(ii), j))
    pair = pl.BlockSpec((2, tm, tn), lambda j, ii: (0, rev(ii), j))
    per_col = pl.BlockSpec((4, 2, tn), lambda j, ii: (0, 0, j))
    return pl.pallas_call(
        body, name="conv_gelu_bwd", grid=(FF // tn, n_i),
        in_specs=[tile, pair, tile, tile, per_col],
        out_specs=[pair, per_col],
        out_shape=[jax.ShapeDtypeStruct((2, s, FF), BF16), jax.ShapeDtypeStruct((4, 2, FF), F32)],
        scratch_shapes=[pltpu.VMEM((2, 8, tn), F32)],
        compiler_params=_cp(("parallel", "arbitrary")),
    )(da, up, g, a1, cwb)


def _dh1_ln1_bwd(dz2, dup, w_up, z1, ln1_g, tm=512):
    s = dz2.shape[0]

    def body(dz2_ref, dup_ref, w_ref, z1_ref, g_ref, dz1_ref, dz1b_ref, st_ref):
        @pl.when(pl.program_id(0) == 0)
        def _():
            st_ref[...] = jnp.zeros_like(st_ref)

        dh = ALPHA * dz2_ref[...] + _nt(dup_ref[0], w_ref[:, :FF]) + _nt(dup_ref[1], w_ref[:, FF:])
        zh, rstd = _layer_norm_stats(z1_ref[...])
        st_ref[0:1, :] += jnp.sum(dh * zh, axis=0, keepdims=True)
        st_ref[1:2, :] += jnp.sum(dh, axis=0, keepdims=True)
        dz = _layer_norm_bwd(dh, zh, rstd, g_ref[...])
        dz1_ref[...] = dz
        dz1b_ref[...] = dz.astype(BF16)

    td = pl.BlockSpec((tm, D), lambda i: (i, 0))
    return pl.pallas_call(
        body, name="dh1_ln1_bwd", grid=(s // tm,),
        in_specs=[td, pl.BlockSpec((2, tm, FF), lambda i: (0, i, 0)), _resident((D, 2 * FF)), td, _const((1, D))],
        out_specs=[td, td, _const((8, D))],
        out_shape=[jax.ShapeDtypeStruct((s, D), F32), jax.ShapeDtypeStruct((s, D), BF16),
                   jax.ShapeDtypeStruct((8, D), F32)],
        compiler_params=_cp(("arbitrary",), 58),
    )(dz2, dup, w_up, z1, _row(ln1_g))


def _dcat_rms_bwd(dz1b, w_o, o_a, o_b, norm_a_g, norm_b_g, tm=512):
    s = dz1b.shape[0]

    def body(dz_ref, w_ref, oa_ref, ob_ref, ga_ref, gb_ref, da_ref, db_ref, st_ref):
        @pl.when(pl.program_id(0) == 0)
        def _():
            st_ref[...] = jnp.zeros_like(st_ref)

        dcat = _nt(dz_ref[...], w_ref[...])
        for k, (o_ref, g_ref, d_ref) in enumerate(((oa_ref, ga_ref, da_ref), (ob_ref, gb_ref, db_ref))):
            o = jnp.concatenate([o_ref[j] for j in range(4)], axis=1)
            dn = dcat[:, 512 * k:512 * (k + 1)]
            rr = _rms(o)
            oh = o * rr
            st_ref[k:k + 1, :] += jnp.sum(dn * oh, axis=0, keepdims=True)
            doh = dn * g_ref[...]
            d_o = rr * (doh - oh * jnp.mean(doh * oh, axis=-1, keepdims=True))
            for j in range(4):
                d_ref[j] = d_o[:, 128 * j:128 * (j + 1)]

    t512 = pl.BlockSpec((4, tm, 128), lambda i: (0, i, 0))
    return pl.pallas_call(
        body, name="dcat_rms_bwd", grid=(s // tm,),
        in_specs=[pl.BlockSpec((tm, D), lambda i: (i, 0)), _resident((D, D)), t512, t512,
                  _const((1, 512)), _const((1, 512))],
        out_specs=[t512, t512, _const((8, 512))],
        out_shape=[jax.ShapeDtypeStruct((4, s, 128), F32), jax.ShapeDtypeStruct((4, s, 128), F32),
                   jax.ShapeDtypeStruct((8, 512), F32)],
        compiler_params=_cp(("arbitrary",)),
    )(dz1b, w_o, o_a, o_b, _row(norm_a_g), _row(norm_b_g))


def _dproj_combine(dqa, dka, dva, dqkv_b, tm=256):
    s = dka.shape[0]

    def body(qa, ka, va, qb, kb, vb, o_ref):
        for j in range(4):
            o_ref[:, 128 * j:128 * (j + 1)] = qa[j].astype(BF16)
            o_ref[:, 768 + 128 * j:768 + 128 * (j + 1)] = qb[j].astype(BF16)
            o_ref[:, 1280 + 128 * j:1280 + 128 * (j + 1)] = kb[j].astype(BF16)
            o_ref[:, 1792 + 128 * j:1792 + 128 * (j + 1)] = vb[j].astype(BF16)
        o_ref[:, 512:640] = ka[...].astype(BF16)
        o_ref[:, 640:768] = va[...].astype(BF16)

    t512 = pl.BlockSpec((4, tm, 128), lambda i: (0, i, 0))
    t128 = pl.BlockSpec((tm, 128), lambda i: (i, 0))
    return pl.pallas_call(
        body, name="dproj_combine", grid=(s // tm,),
        in_specs=[t512, t128, t128] + [t512] * 3,
        out_specs=pl.BlockSpec((tm, WIN), lambda i: (i, 0)),
        out_shape=jax.ShapeDtypeStruct((s, WIN), BF16),
        compiler_params=_cp(("parallel",)),
    )(dqa, dka, dva, *dqkv_b)


def _grad_x(dz1, dproj, w_in_t, zero, tm=512):
    s = dz1.shape[0]

    def body(dz_ref, dp_ref, w_ref, z_ref, o_ref):
        o_ref[...] = ALPHA * dz_ref[...] + _nn(dp_ref[...], w_ref[...]) + z_ref[0:1, 0:1]

    td = pl.BlockSpec((tm, D), lambda i: (i, 0))
    return pl.pallas_call(
        body, name="grad_x", grid=(s // tm,),
        in_specs=[td, pl.BlockSpec((tm, WIN), lambda i: (i, 0)), _resident((WIN, D)), _const((8, 128))],
        out_specs=td, out_shape=jax.ShapeDtypeStruct((s, D), F32),
        compiler_params=_cp(("parallel",)),
    )(dz1, dproj, w_in_t, zero)


def _place():
    return lax.axis_index("x"), lax.axis_index("y"), lax.axis_index("c")


def _other_chips(x, y):
    return [(1 - x, y), (x, 1 - y), (1 - x, 1 - y)]


def _hbm(a):
    return pltpu.with_memory_space_constraint(a, pltpu.HBM)


def _gather_w_in(shard, conv_w):
    rows_k = shard.shape[0]
    half = rows_k // 2

    def body(src, conv_src, out, conv_out, send_sems, recv_sems):
        x, y, c = _place()
        b = 2 * x + y
        sibling = (x, y, 1 - c)
        chips = _other_chips(x, y)

        def copy(idx, chip_b, core, to, first_hop=False):
            rows = out.at[pl.ds(pl.multiple_of(chip_b * rows_k + core * half, 16), half)]
            s_ref = src.at[pl.ds(pl.multiple_of(core * half, 16), half)] if first_hop else rows
            return pltpu.make_async_remote_copy(src_ref=s_ref, dst_ref=rows, send_sem=send_sems.at[idx],
                                                recv_sem=recv_sems.at[idx], device_id=to, device_id_type=MESH)

        def own_copy():
            return pltpu.make_async_remote_copy(
                src_ref=src, dst_ref=out.at[pl.ds(pl.multiple_of(b * rows_k, 16), rows_k)], send_sem=send_sems.at[6],
                recv_sem=recv_sems.at[6], device_id=sibling, device_id_type=MESH)

        def conv_copy(idx, chip_b, to):
            return pltpu.make_async_remote_copy(src_ref=conv_src, dst_ref=conv_out.at[chip_b],
                                                send_sem=send_sems.at[7 + idx], recv_sem=recv_sems.at[7 + idx],
                                                device_id=to, device_id_type=MESH)

        started = [own_copy(), conv_copy(3, b, sibling)]
        for jn, chip in enumerate(chips):
            started += [copy(jn, b, c, (chip[0], chip[1], c), first_hop=True), conv_copy(jn, b, (chip[0], chip[1], c))]
        for cp in started:
            cp.start()
        for jn, chip in enumerate(chips):
            cb = 2 * chip[0] + chip[1]
            copy(jn, cb, c, (chip[0], chip[1], c)).wait_recv()
            cp = copy(3 + jn, cb, c, sibling)
            cp.start()
            started.append(cp)
        for jn, chip in enumerate(chips):
            cb = 2 * chip[0] + chip[1]
            copy(3 + jn, cb, 1 - c, sibling).wait_recv()
            conv_copy(jn, cb, (chip[0], chip[1], c)).wait_recv()
        own_copy().wait_recv()
        conv_copy(3, b, sibling).wait_recv()
        for cp in started:
            cp.wait_send()

    return pl.pallas_call(
        body, name="gather_w_in",
        in_specs=[ANY, ANY], out_specs=[ANY, ANY],
        out_shape=[jax.ShapeDtypeStruct((N_CHIPS * rows_k, D), BF16), jax.ShapeDtypeStruct((N_CHIPS,) + conv_w.shape, F32)],
        scratch_shapes=[pltpu.SemaphoreType.DMA((11,)), pltpu.SemaphoreType.DMA((11,))],
        compiler_params=pltpu.CompilerParams(has_side_effects=True),
    )(shard, conv_w)


def _weight_copies(shard, land, send_sems, recv_sems, arrivals):
    x, y, c = _place()
    n_rows, n_cols = shard.shape
    peers = [(px, py, c) for px, py in _other_chips(x, y)] + [(x, y, 1 - c)]
    cps = []
    for jn, peer in enumerate(peers):
        at = 2 * peer[0] + peer[1] if arrivals else 2 * x + y
        if land.shape[1] == n_cols:
            dst = land.at[pl.ds(pl.multiple_of(at * n_rows, 16), n_rows)]
        else:
            dst = land.at[:, pl.ds(pl.multiple_of(at * n_cols, 128), n_cols)]
        cps.append(pltpu.make_async_remote_copy(src_ref=shard, dst_ref=dst, send_sem=send_sems.at[jn],
                                                recv_sem=recv_sems.at[jn], device_id=peer, device_id_type=MESH))
    return cps


def _weights_start(shards, after):
    n = len(shards)
    lands = [lax.empty((N_CHIPS * sh.shape[0], D) if sh.shape[1] == D else (D, N_CHIPS * sh.shape[1]), BF16)
             for sh in shards]

    def body(*refs):
        src, land = refs[:n], refs[n:2 * n]
        send_sems, recv_sems = refs[2 * n + 1:3 * n + 1], refs[3 * n + 1:4 * n + 1]
        for k in range(n):
            for send in _weight_copies(src[k], land[k], send_sems[k], recv_sems[k], False):
                send.start()
        refs[-1][...] = jnp.zeros_like(refs[-1])

    res = pl.pallas_call(
        body, name="weights_start",
        in_specs=[HBM] * (2 * n) + [ANY], out_specs=[SEM] * (2 * n) + [HBM] * (2 * n) + [VMEM],
        out_shape=[pltpu.SemaphoreType.DMA((4,))] * (2 * n)
        + [pltpu.HBM(a.shape, a.dtype) for a in (*shards, *lands)] + [jax.ShapeDtypeStruct((8, 128), F32)],
        input_output_aliases={i: i + 2 * n for i in range(2 * n)},
        compiler_params=pltpu.CompilerParams(has_side_effects=DATAFLOW),
    )(*[_hbm(a) for a in (*shards, *lands)], after)
    return [(res[k], res[n + k], res[2 * n + k], res[3 * n + k]) for k in range(n)], res[-1]


def _weights_wait(started, after, name):
    send_sems, recv_sems, shard, land = started

    def body(s_ref, l_ref, send_ref, recv_ref, after_ref, s_out, l_out):
        for cp in _weight_copies(s_ref, l_ref, send_ref, recv_ref, True):
            cp.wait_send()
            cp.wait_recv()

    return pl.pallas_call(
        body, name=name,
        in_specs=[HBM, HBM, SEM, SEM, ANY], out_specs=[HBM, HBM],
        out_shape=[pltpu.HBM(shard.shape, shard.dtype), pltpu.HBM(land.shape, land.dtype)],
        input_output_aliases={0: 0, 1: 1},
        compiler_params=pltpu.CompilerParams(has_side_effects=DATAFLOW),
    )(shard, land, send_sems, recv_sems, after)[1]


def _grad_copies(g_ref, land_ref, send_sems, recv_sems):
    x, y, c = _place()
    cps = []
    for d in range(1, 8):
        px, py, pc = x ^ (d >> 2), y ^ ((d >> 1) & 1), c ^ (d & 1)
        cps.append(pltpu.make_async_remote_copy(
            src_ref=g_ref.at[2 * px + py, pc], dst_ref=land_ref.at[d - 1], send_sem=send_sems.at[d - 1],
            recv_sem=recv_sems.at[d - 1], device_id=(px, py, pc), device_id_type=MESH))
    return cps


def _grads_start(grads_b, name):
    n = len(grads_b)
    lands = [lax.empty((7, g.shape[2], D), BF16) for g in grads_b]

    def body(*refs):
        g, land = refs[:n], refs[n:2 * n]
        send_sems, recv_sems = refs[2 * n:3 * n], refs[3 * n:4 * n]
        for k in range(n):
            for cp in _grad_copies(g[k], land[k], send_sems[k], recv_sems[k]):
                cp.start()
        refs[-1][...] = jnp.zeros_like(refs[-1])

    res = pl.pallas_call(
        body, name=name,
        in_specs=[HBM] * (2 * n), out_specs=[SEM] * (2 * n) + [HBM] * (2 * n) + [VMEM],
        out_shape=[pltpu.SemaphoreType.DMA((7,))] * (2 * n)
        + [pltpu.HBM(a.shape, a.dtype) for a in (*grads_b, *lands)] + [jax.ShapeDtypeStruct((8, 128), F32)],
        input_output_aliases={i: i + 2 * n for i in range(2 * n)},
        compiler_params=pltpu.CompilerParams(has_side_effects=DATAFLOW),
    )(*[_hbm(a) for a in (*grads_b, *lands)])
    return [(res[k], res[n + k], res[2 * n + k], res[3 * n + k]) for k in range(n)], res[-1]


def _grads_wait(started, after, name):
    n = len(started)

    def body(*refs):
        g, land = refs[:n], refs[n:2 * n]
        send_sems, recv_sems = refs[2 * n:3 * n], refs[3 * n:4 * n]
        for k in range(n):
            for cp in _grad_copies(g[k], land[k], send_sems[k], recv_sems[k]):
                cp.wait_send()
                cp.wait_recv()

    gs = [st[2] for st in started]
    lands = [st[3] for st in started]
    res = pl.pallas_call(
        body, name=name,
        in_specs=[HBM] * (2 * n) + [SEM] * (2 * n) + [ANY], out_specs=[HBM] * (2 * n),
        out_shape=[pltpu.HBM(a.shape, a.dtype) for a in (*gs, *lands)],
        input_output_aliases={i: i for i in range(2 * n)},
        compiler_params=pltpu.CompilerParams(has_side_effects=DATAFLOW),
    )(*gs, *lands, *[st[0] for st in started], *[st[1] for st in started], after)
    return res[n:]


def _sum_partials(grad4, got, cb, name, tr):
    h = grad4.shape[2]
    per_half = h // tr

    def body(cb_ref, g_ref, o_ref, out_ref):
        acc = g_ref[...]
        for j in range(7):
            acc = acc + o_ref[j].astype(F32)
        out_ref[...] = acc

    return pl.pallas_call(
        body, name=name,
        grid_spec=pltpu.PrefetchScalarGridSpec(
            num_scalar_prefetch=1, grid=(per_half,),
            in_specs=[pl.BlockSpec((None, None, tr, D), lambda i, cb_ref: (cb_ref[1], cb_ref[0], i, 0)),
                      pl.BlockSpec((7, tr, D), lambda i, cb_ref: (0, i, 0))],
            out_specs=pl.BlockSpec((tr, D), lambda i, cb_ref: (cb_ref[0] * per_half + i, 0))),
        out_shape=jax.ShapeDtypeStruct((2 * h, D), F32),
        compiler_params=_cp(("arbitrary",)),
    )(cb, grad4, got)


def _swap_halves(shards, name):
    n = len(shards)

    def body(*refs):
        out, send_sems, recv_sems = refs[n:2 * n], refs[2 * n], refs[2 * n + 1]
        x, y, c = _place()
        cps = []
        for k in range(n):
            h = shards[k].shape[0] // 2
            mine = out[k].at[pl.ds(pl.multiple_of(c * h, 8), h)]
            cp = pltpu.make_async_remote_copy(src_ref=mine, dst_ref=mine, send_sem=send_sems.at[k],
                                              recv_sem=recv_sems.at[k], device_id=(x, y, 1 - c), device_id_type=MESH)
            cp.start()
            cps.append(cp)
        for cp in cps:
            cp.wait()

    return pl.pallas_call(
        body, name=name,
        in_specs=[ANY] * n, out_specs=[ANY] * n,
        out_shape=[jax.ShapeDtypeStruct(sh.shape, F32) for sh in shards],
        input_output_aliases={k: k for k in range(n)},
        scratch_shapes=[pltpu.SemaphoreType.DMA((n,)), pltpu.SemaphoreType.DMA((n,))],
        compiler_params=pltpu.CompilerParams(has_side_effects=True),
    )(*shards)


def _small_copies(small_ref, land_ref, send_sems, recv_sems):
    x, y, c = _place()
    me = 4 * x + 2 * y + c
    cps = []
    for d in range(1, 8):
        px, py, pc = x ^ (d >> 2), y ^ ((d >> 1) & 1), c ^ (d & 1)
        cps.append(pltpu.make_async_remote_copy(
            src_ref=small_ref, dst_ref=land_ref.at[me], send_sem=send_sems.at[d - 1], recv_sem=recv_sems.at[d - 1],
            device_id=(px, py, pc), device_id_type=MESH))
    return cps


def _small_start(small):
    land = lax.empty((8,) + small.shape, F32)

    def body(s_ref, l_ref, send_sems, recv_sems, s_thru, l_thru, token):
        for cp in _small_copies(s_ref, l_ref, send_sems, recv_sems):
            cp.start()
        token[...] = jnp.zeros_like(token)

    res = pl.pallas_call(
        body, name="small_start",
        in_specs=[HBM, HBM], out_specs=[SEM, SEM, HBM, HBM, VMEM],
        out_shape=[pltpu.SemaphoreType.DMA((7,)), pltpu.SemaphoreType.DMA((7,)), pltpu.HBM(small.shape, F32),
                   pltpu.HBM(land.shape, F32), jax.ShapeDtypeStruct((8, 128), F32)],
        input_output_aliases={0: 2, 1: 3},
        compiler_params=pltpu.CompilerParams(has_side_effects=DATAFLOW),
    )(_hbm(small), _hbm(land))
    return res[:4], res[4]


def _small_wait(started, after):
    send_sems, recv_sems, small, land = started

    def body(s_ref, l_ref, send_ref, recv_ref, after_ref, s_out, l_out):
        for cp in _small_copies(s_ref, l_ref, send_ref, recv_ref):
            cp.wait_send()
            cp.wait_recv()

    return pl.pallas_call(
        body, name="small_wait",
        in_specs=[HBM, HBM, SEM, SEM, ANY], out_specs=[HBM, HBM],
        out_shape=[pltpu.HBM(small.shape, F32), pltpu.HBM(land.shape, F32)],
        input_output_aliases={0: 0, 1: 1},
        compiler_params=pltpu.CompilerParams(has_side_effects=DATAFLOW),
    )(small, land, send_sems, recv_sems, after)


def _small_sum(small, land, me):
    rows = small.shape[0]

    def body(me_ref, s_ref, l_ref, o_ref):
        acc = None
        for k in range(8):
            term = jnp.where(me_ref[0] == k, s_ref[...], l_ref[k])
            acc = term if k == 0 else acc + term
        o_ref[...] = acc

    return pl.pallas_call(
        body, name="small_sum",
        in_specs=[SMEM, VMEM, VMEM], out_specs=VMEM,
        out_shape=jax.ShapeDtypeStruct((rows, D), F32),
    )(me, small, land)


def _adamw(w, g, m, v, name, tr):
    rows, cols = w.shape

    def body(w_ref, g_ref, m_ref, v_ref, d_ref, nm_ref, nv_ref):
        g_ = g_ref[...]
        nm = ADAM_B1 * m_ref[...] + (1.0 - ADAM_B1) * g_
        nv = ADAM_B2 * v_ref[...] + (1.0 - ADAM_B2) * (g_ * g_)
        m_hat = nm / (1.0 - ADAM_B1 ** ADAM_STEP)
        v_hat = nv / (1.0 - ADAM_B2 ** ADAM_STEP)
        d_ref[...] = -ADAM_LR * (m_hat / (jnp.sqrt(v_hat) + ADAM_EPS) + ADAM_WD * w_ref[...])
        nm_ref[...] = nm
        nv_ref[...] = nv

    spec = pl.BlockSpec((tr, cols), lambda i: (i, 0))
    return pl.pallas_call(
        body, name=name, grid=(rows // tr,),
        in_specs=[spec] * 4, out_specs=[spec] * 3,
        out_shape=[jax.ShapeDtypeStruct((rows, cols), F32)] * 3,
        compiler_params=_cp(("parallel",)),
    )(w, g, m, v)


def _local_step(x, target, w_in_t, late_weights, norm_a_g, norm_b_g, sinks_a, ln1_g, ln1_b,
                conv_w, conv_b, ln2_g, ln2_b, slopes, on_grad, on_small):
    cwb = jnp.concatenate([conv_w, conv_b[None]], axis=0).reshape(4, 2, FF)

    proj, xb = _proj(x, w_in_t, "proj")
    o_a, lse_a = _attn_a_fwd(proj, sinks_a)
    fwd_b = [_attn_b_fwd(proj, slopes, r) for r in B_DILATIONS]
    w_o = late_weights(1, fwd_b[-1][1])
    o_b, lse_b, cat, z1, h1, h1b = _mix_ln1(x, o_a, [f[0] for f in fwd_b], [f[1] for f in fwd_b],
                                           norm_a_g, norm_b_g, w_o, ln1_g, ln1_b)
    w_up = late_weights(2, h1b)
    up = _up_proj(h1b, w_up)
    a, gate, a1 = _conv_gelu(up, cwb)
    w_down = late_weights(3, a)
    dz2, dz2b, st2 = _down_ln2_loss(a, w_down, h1, target, ln2_g, ln2_b)

    on_grad(3, *_grad_w(a, dz2b, "grad_w_down", tm=FF // 2))
    dup, dconv = _conv_gelu_bwd(_d_act(dz2b, w_down), up, gate, a1, cwb)
    on_grad(2, *_grad_w(dup, h1b, "grad_w_up", tm=FF // 2, lhs_halves=True))
    dz1, dz1b, st1 = _dh1_ln1_bwd(dz2, dup, w_up, z1, ln1_g)
    tok = on_grad(1, *_grad_w(cat, dz1b, "grad_w_o", tm=512))
    d_oa, d_ob, st_n = _dcat_rms_bwd(dz1b, w_o, o_a, o_b, norm_a_g + tok[0, 0], norm_b_g)
    dqa, dka, dva, dsink = _attn_a_bwd(proj, sinks_a, d_oa, o_a, lse_a)
    dconv = dconv.reshape(4, 2 * FF)
    tok = on_small(dict(loss=st2[2, 0:1], norm_a_g=st_n[0], norm_b_g=st_n[1], sinks_a=dsink[:, 0],
                        ln1_g=st1[0], ln1_b=st1[1], conv_w=dconv[0:3].reshape(-1), conv_b=dconv[3],
                        ln2_g=st2[0], ln2_b=st2[1]))
    slopes = slopes + tok[0, 0]
    bwd_b = None
    for r in reversed(B_DILATIONS):
        bwd_b = _attn_b_bwd(proj, slopes, d_ob, o_b, lse_b, r, bwd_b)
    dproj = _dproj_combine(dqa, dka, dva, bwd_b)
    tok = on_grad(0, *_grad_w(dproj, xb, "grad_w_in", tm=WA))
    return _grad_x(dz1, dproj, w_in_t, tok)


SMALL_ORDER = ("loss", "norm_a_g", "norm_b_g", "sinks_a", "ln1_g", "ln1_b", "conv_b", "ln2_g", "ln2_b", "conv_w")
SMALL_SIZES = dict(loss=1, norm_a_g=512, norm_b_g=512, sinks_a=8, ln1_g=D, ln1_b=D, conv_b=2 * FF, ln2_g=D, ln2_b=D,
                   conv_w=3 * 2 * FF)


def _pack(parts, rows):
    flat = jnp.concatenate([parts[k].reshape(-1).astype(F32) for k in parts])
    return jnp.pad(flat, (0, rows * D - flat.shape[0])).reshape(rows, D)


def _unpack(buf, names, sizes):
    flat = buf.reshape(-1)
    out, at = {}, 0
    for k in names:
        out[k] = flat[at:at + sizes[k]]
        at += sizes[k]
    return out


def kernel(x, w_in, norm_a_g, norm_b_g, sinks_a, w_o, ln1_g, ln1_b, w_up, conv_w, conv_b, w_down, ln2_g, ln2_b, loss_target, m_w_in, m_norm_a_g, m_norm_b_g, m_sinks_a, m_w_o, m_ln1_g, m_ln1_b, m_w_up, m_conv_w, m_conv_b, m_w_down, m_ln2_g, m_ln2_b, v_w_in, v_norm_a_g, v_norm_b_g, v_sinks_a, v_w_o, v_ln1_g, v_ln1_b, v_w_up, v_conv_w, v_conv_b, v_w_down, v_ln2_g, v_ln2_b):
    xi, yi, ci = _place()
    chip = (2 * xi + yi).astype(I32)
    core = ci.astype(I32)

    w_in_rows, m_w_in_rows, v_w_in_rows = w_in.T, m_w_in.T, v_w_in.T
    shards = (w_in_rows.astype(BF16), w_o.astype(BF16), w_up.astype(BF16), w_down.astype(BF16))
    w_in_t, conv_w4 = _gather_w_in(shards[0], conv_w)
    conv_w_f = conv_w4.transpose(1, 0, 2).reshape(3, 2 * FF)
    w_started, w_tok = _weights_start(shards[1:], conv_w4)
    slopes = jnp.asarray(SLOPES, F32) + w_tok[0, 0]

    halves_rows = [r // 2 for r in SHARD_ROWS]
    grads4, grads_b4, started = [None] * 4, [None] * 4, [None] * 4

    def on_grad(k, g, g_b):
        grads4[k] = g.reshape(N_CHIPS, 2, halves_rows[k], D)
        grads_b4[k] = g_b.reshape(N_CHIPS, 2, halves_rows[k], D)
        if k > 1:
            return None
        group = (1, 2, 3) if k == 1 else (0,)
        sts, tok = _grads_start([grads_b4[i] for i in group], f"grads_start_{k}")
        for i, st in zip(group, sts):
            started[i] = st
        return tok

    small_rows = 32
    small_started = []

    def on_small(parts):
        st, tok = _small_start(_pack({k: parts[k] for k in SMALL_ORDER}, small_rows))
        small_started.append(st)
        return tok

    gx = _local_step(
        x[0], loss_target[0], w_in_t, lambda k, after: _weights_wait(w_started[k - 1], after, f"weights_wait_{k}"),
        norm_a_g, norm_b_g, sinks_a, ln1_g, ln1_b, conv_w_f, conv_b, ln2_g, ln2_b, slopes, on_grad, on_small)

    tiles = (96, 128, 352, 176)
    core_chip = jnp.stack([core, chip])
    got = _grads_wait(started[1:], gx, "grads_wait_1")
    halves = [_sum_partials(grads4[k], got[k - 1], core_chip, f"sum_partials_{k}", tiles[k]) for k in (1, 2, 3)]
    g_w_o, g_w_up_rows, g_w_down = _swap_halves(halves, "swap_halves")
    g_w_up = g_w_up_rows.T
    delta, new_m, new_v = {}, {}, {}
    for k, g, tr in (("w_o", g_w_o, 128), ("w_up", g_w_up, 256), ("w_down", g_w_down, 176)):
        delta[k], new_m[k], new_v[k] = _adamw(dict(w_o=w_o, w_up=w_up, w_down=w_down)[k], g,
                                              dict(w_o=m_w_o, w_up=m_w_up, w_down=m_w_down)[k],
                                              dict(w_o=v_w_o, w_up=v_w_up, w_down=v_w_down)[k], f"adamw_{k}", tr)

    got = _grads_wait(started[:1], delta["w_up"], "grads_wait_0")
    half_in = _sum_partials(grads4[0], got[0], core_chip, "sum_partials_0", tiles[0])
    (g_w_in_rows,) = _swap_halves([half_in], "swap_halves_in")
    small_mine, small_land = _small_wait(small_started[0], g_w_in_rows)
    totals = _small_sum(small_mine, small_land, (4 * xi + 2 * yi + ci).astype(I32).reshape(1))
    tot = _unpack(totals, SMALL_ORDER, SMALL_SIZES)
    loss = tot["loss"][0]
    cols = 2 * FF // N_CHIPS
    g_conv_w = lax.dynamic_slice(tot["conv_w"].reshape(3, 2 * FF), (0, chip * cols), (3, cols))
    g_small = dict(norm_a_g=tot["norm_a_g"], norm_b_g=tot["norm_b_g"], sinks_a=tot["sinks_a"], ln1_g=tot["ln1_g"],
                   ln1_b=tot["ln1_b"], conv_w=g_conv_w, conv_b=tot["conv_b"], ln2_g=tot["ln2_g"], ln2_b=tot["ln2_b"])

    weights = dict(w_in=w_in, norm_a_g=norm_a_g, norm_b_g=norm_b_g, sinks_a=sinks_a, w_o=w_o, ln1_g=ln1_g, ln1_b=ln1_b,
                   w_up=w_up, conv_w=conv_w, conv_b=conv_b, w_down=w_down, ln2_g=ln2_g, ln2_b=ln2_b)
    ms = dict(w_in=m_w_in, norm_a_g=m_norm_a_g, norm_b_g=m_norm_b_g, sinks_a=m_sinks_a, w_o=m_w_o, ln1_g=m_ln1_g,
              ln1_b=m_ln1_b, w_up=m_w_up, conv_w=m_conv_w, conv_b=m_conv_b, w_down=m_w_down, ln2_g=m_ln2_g, ln2_b=m_ln2_b)
    vs = dict(w_in=v_w_in, norm_a_g=v_norm_a_g, norm_b_g=v_norm_b_g, sinks_a=v_sinks_a, w_o=v_w_o, ln1_g=v_ln1_g,
              ln1_b=v_ln1_b, w_up=v_w_up, conv_w=v_conv_w, conv_b=v_conv_b, w_down=v_w_down, ln2_g=v_ln2_g, ln2_b=v_ln2_b)
    order = list(weights)
    grad = dict(g_small, w_in=g_w_in_rows.T, w_o=g_w_o, w_up=g_w_up, w_down=g_w_down)

    delta["w_in"], new_m["w_in"], new_v["w_in"] = [
        a.T for a in _adamw(w_in_rows, g_w_in_rows, m_w_in_rows, v_w_in_rows, "adamw_w_in", 144)]
    small_names = [k for k in order if k not in delta]
    sizes = {k: weights[k].size for k in small_names}
    rows = 16
    packed = [_pack({k: src[k] for k in small_names}, rows) for src in (weights, grad, ms, vs)]
    for res, buf in zip((delta, new_m, new_v), _adamw(*packed, "adamw_small", rows)):
        for k, val in _unpack(buf, small_names, sizes).items():
            res[k] = val.reshape(weights[k].shape)

    return (loss, gx[None], *[grad[k] for k in order], *[delta[k] for k in order],
            *[new_m[k] for k in order], *[new_v[k] for k in order])
```

```python
import functools
import math

import jax
import jax.numpy as jnp
from jax import lax
from jax.experimental import pallas as pl
from jax.experimental.pallas import tpu as pltpu

F32, BF16, I32 = jnp.float32, jnp.bfloat16, jnp.int32

D = 1024
FF = 2816
HD = 64
NH = 8
WA, WB = 768, 1536
WIN = WA + WB
BLK = 128
ALPHA = 2.0 ** 0.25
LN_EPS, RMS_EPS = 1e-5, 1e-6
SCALE = 1.0 / math.sqrt(HD)
A_MAX_DIST, B_MAX_DIST = 127, 128
B_DILATIONS = (1, 4, 16)
SLOPES = tuple(2.0 ** (-(i + 1)) for i in range(NH))
SHARD_ROWS = (WIN // 4, D // 4, 2 * FF // 4, FF // 4)
N_CHIPS = 4
ADAM_LR, ADAM_B1, ADAM_B2, ADAM_EPS, ADAM_WD, ADAM_STEP = 0.001, 0.9, 0.999, 1e-08, 0.01, 10
MESH = pl.DeviceIdType.MESH
ANY = pl.BlockSpec(memory_space=pl.ANY)
SMEM = pl.BlockSpec(memory_space=pltpu.SMEM)
VMEM = pl.BlockSpec(memory_space=pltpu.VMEM)
HBM = pl.BlockSpec(memory_space=pltpu.HBM)
SEM = pl.BlockSpec(memory_space=pltpu.SEMAPHORE)
DATAFLOW = pltpu.SideEffectType.DATAFLOW_SIDE_EFFECTING


def _cp(sem, mb=48):
    return pltpu.CompilerParams(dimension_semantics=sem, vmem_limit_bytes=mb << 20)


def _nn(a, b):
    return lax.dot_general(a, b, (((1,), (0,)), ((), ())), preferred_element_type=F32)


def _nt(a, b):
    return lax.dot_general(a, b, (((1,), (1,)), ((), ())), preferred_element_type=F32)


def _tn(a, b):
    return lax.dot_general(a, b, (((0,), (0,)), ((), ())), preferred_element_type=F32)


def _resident(shape):
    n = len(shape)
    return pl.BlockSpec(shape, lambda *_: (0,) * n, pipeline_mode=pl.Buffered(1))


def _const(shape):
    n = len(shape)
    return pl.BlockSpec(shape, lambda *_: (0,) * n)


def _proj(x, w_t, name, tm=512):
    s = x.shape[0]
    n = w_t.shape[0]

    def body(x_ref, w_ref, o_ref, xb_ref):
        xb = x_ref[...].astype(BF16)
        xb_ref[...] = xb
        res = _nt(xb, w_ref[...])
        for g in range(n // 128):
            o_ref[g] = res[:, 128 * g:128 * (g + 1)]

    return pl.pallas_call(
        body, name=name, grid=(s // tm,),
        in_specs=[pl.BlockSpec((tm, D), lambda i: (i, 0)), _resident((n, D))],
        out_specs=[pl.BlockSpec((n // 128, tm, 128), lambda i: (0, i, 0)), pl.BlockSpec((tm, D), lambda i: (i, 0))],
        out_shape=[jax.ShapeDtypeStruct((n // 128, s, 128), F32), jax.ShapeDtypeStruct((s, D), BF16)],
        compiler_params=_cp(("parallel",)),
    )(x, w_t)


def _grad_w(lhs, rhs, name, tm, tk=2048, lhs_halves=False):
    s = rhs.shape[0]
    if lhs_halves:
        per_half = lhs.shape[2] // tm
        n = 2 * lhs.shape[2]
        lhs_spec = pl.BlockSpec((None, tk, tm), lambda i, k: (i // per_half, k, i % per_half))
    else:
        n = lhs.shape[1]
        lhs_spec = pl.BlockSpec((tk, tm), lambda i, k: (k, i))
    nk = s // tk

    def body(l_ref, r_ref, o_ref, ob_ref):
        k = pl.program_id(1)

        @pl.when(k == 0)
        def _():
            o_ref[...] = jnp.zeros_like(o_ref)

        o_ref[...] += _tn(l_ref[...], r_ref[...])

        @pl.when(k == nk - 1)
        def _():
            ob_ref[...] = o_ref[...].astype(BF16)

    return pl.pallas_call(
        body, name=name, grid=(n // tm, nk),
        in_specs=[lhs_spec, pl.BlockSpec((tk, D), lambda i, k: (k, 0))],
        out_specs=[pl.BlockSpec((tm, D), lambda i, k: (i, 0))] * 2,
        out_shape=[jax.ShapeDtypeStruct((n, D), F32), jax.ShapeDtypeStruct((n, D), BF16)],
        compiler_params=_cp(("parallel", "arbitrary")),
    )(lhs, rhs)


def _band_base(max_dist, dist_unit, first):
    row = lax.broadcasted_iota(I32, (BLK, 2 * BLK), 0)
    col = lax.broadcasted_iota(I32, (BLK, 2 * BLK), 1)
    dist = BLK + row - col
    ok = (dist >= 0) & (dist <= max_dist)
    if first:
        ok = ok & (col >= BLK)
    return jnp.where(ok, dist.astype(F32) * (-float(dist_unit)), -jnp.inf)


def _half_mask(shape, e):
    lane = lax.broadcasted_iota(I32, shape, 1)
    return (lane < HD) if e == 0 else (lane >= HD)


def _to_half(x, e, g):
    if g != e:
        x = pltpu.roll(x, HD, 1)
    return jnp.where(_half_mask(x.shape, g), x, 0.0)


def _stack_heads(scalars, tile):
    return jnp.concatenate([scalars[0] * tile, scalars[1] * tile], axis=0)


def _pair_fwd(q2, kb, vb, base, slopes, kv_heads, sinks):
    lo = _half_mask((BLK, 2 * HD), 0)
    if slopes is None:
        bias = base
    elif sinks is None:
        bias = _stack_heads(slopes, base)
    else:
        col0 = lax.broadcasted_iota(I32, base.shape, 1) == 0
        bias = jnp.concatenate([jnp.where(col0, sinks[e], slopes[e] * base) for e in (0, 1)], axis=0)
    qs = jnp.concatenate([_to_half(q2, e, kv_heads[e]) * SCALE for e in (0, 1)], axis=0).astype(BF16)
    s = _nt(qs, kb) + bias
    m = jnp.max(s, axis=1, keepdims=True)
    p = jnp.exp(s - m)
    l = jnp.sum(p, axis=1, keepdims=True)
    o = _nn(p.astype(BF16), vb) / l
    lse = m + jnp.log(l)
    halves = []
    for e in (0, 1):
        oh = o[e * BLK:(e + 1) * BLK]
        halves.append(pltpu.roll(oh, HD, 1) if kv_heads[e] != e else oh)
    o2 = jnp.where(lo, halves[0], halves[1])
    lse2 = jnp.where(lo, jnp.broadcast_to(lse[:BLK], (BLK, 2 * HD)), jnp.broadcast_to(lse[BLK:], (BLK, 2 * HD)))
    return o2, lse2


def _pair_bwd(q2, kb, vb, do2, o2, lse2, base, slopes, kv_heads, sinks):
    lo = _half_mask((BLK, 2 * HD), 0)
    prod = do2 * o2
    lses, deltas = [], []
    for e in (0, 1):
        hq = _half_mask((BLK, 2 * HD), e)
        lses.append(jnp.max(jnp.where(hq, lse2, -jnp.inf), axis=1, keepdims=True))
        deltas.append(jnp.sum(jnp.where(hq, prod, 0.0), axis=1, keepdims=True))
    lse = jnp.concatenate(lses, axis=0)
    delta = jnp.concatenate(deltas, axis=0)
    qs = jnp.concatenate([_to_half(q2, e, kv_heads[e]) * SCALE for e in (0, 1)], axis=0).astype(BF16)
    dos = jnp.concatenate([_to_half(do2, e, kv_heads[e]) for e in (0, 1)], axis=0).astype(BF16)
    p = jnp.exp(_nt(qs, kb) + (base if slopes is None else _stack_heads(slopes, base)) - lse)
    ds = (p * (_nt(dos, vb) - delta)).astype(BF16)
    dq = _nn(ds, kb) * SCALE
    halves = []
    for e in (0, 1):
        dqh = dq[e * BLK:(e + 1) * BLK]
        halves.append(pltpu.roll(dqh, HD, 1) if kv_heads[e] != e else dqh)
    dq2 = jnp.where(lo, halves[0], halves[1])
    dk2 = _tn(ds, qs)
    dv2 = _tn(p.astype(BF16), dos)
    dsinks = []
    if sinks is not None:
        for e in (0, 1):
            dsinks.append(jnp.sum(-jnp.exp(sinks[e] - lses[e]) * deltas[e], axis=0, keepdims=True))
    return dq2, dk2, dv2, dsinks


A_BLOCKS_PER_STEP = 2
A_BLOCKS_PER_STEP_BWD = 1


def _attn_a_fwd(proj, sinks):
    s = proj.shape[1]
    nq = A_BLOCKS_PER_STEP
    rows = BLK * nq
    steps = s // rows

    def body(sink_ref, q_ref, kp_ref, kc_ref, vp_ref, vc_ref, o_ref, lse_ref):
        n = pl.program_id(0)
        base_rest = _band_base(A_MAX_DIST, 1, False)
        base_0 = jnp.where(n > 0, base_rest, _band_base(A_MAX_DIST, 1, True))
        for i in range(nq):
            cur = pl.ds(i * BLK, BLK)
            k_prev = kc_ref[pl.ds((i - 1) * BLK, BLK), :] if i > 0 else kp_ref[...]
            v_prev = vc_ref[pl.ds((i - 1) * BLK, BLK), :] if i > 0 else vp_ref[...]
            first_key = lax.broadcasted_iota(I32, (2 * BLK, 128), 0) == 0
            kb = jnp.where(first_key, 0.0, jnp.concatenate([k_prev, kc_ref[cur, :]], axis=0)).astype(BF16)
            vb = jnp.where(first_key, 0.0, jnp.concatenate([v_prev, vc_ref[cur, :]], axis=0)).astype(BF16)
            for j in range(NH // 2):
                g = j // 2
                o2, lse2 = _pair_fwd(q_ref[j, cur, :], kb, vb, base_rest if i > 0 else base_0,
                                     (SLOPES[2 * j], SLOPES[2 * j + 1]), (g, g), (sink_ref[2 * j], sink_ref[2 * j + 1]))
                o_ref[j, cur, :] = o2
                lse_ref[j, cur, :] = lse2

    before = lambda n: jnp.maximum(n * nq - 1, 0)
    slab = lambda g: pl.BlockSpec((None, rows, 128), lambda n: (g, n, 0))
    edge = lambda g: pl.BlockSpec((None, BLK, 128), lambda n: (g, before(n), 0))
    quad = pl.BlockSpec((4, rows, 128), lambda n: (0, n, 0))
    return pl.pallas_call(
        body, name="attn_a_fwd", grid=(steps,),
        in_specs=[SMEM, quad, edge(4), slab(4), edge(5), slab(5)],
        out_specs=[quad, quad],
        out_shape=[jax.ShapeDtypeStruct((4, s, 128), F32)] * 2,
        compiler_params=_cp(("parallel",)),
    )(sinks, proj, proj, proj, proj, proj)


def _attn_a_bwd(proj, sinks, d_o, o, lse):
    s = proj.shape[1]
    nq = A_BLOCKS_PER_STEP_BWD
    rows = BLK * nq
    steps = s // rows

    def body(sink_ref, q_ref, kp_ref, kc_ref, vp_ref, vc_ref, do_ref, o_ref, lse_ref,
             dq_ref, dk_ref, dv_ref, dsink_ref, kcar, vcar):
        n = pl.program_id(0)

        @pl.when(n == 0)
        def _():
            kcar[...] = jnp.zeros_like(kcar)
            vcar[...] = jnp.zeros_like(vcar)
            dsink_ref[...] = jnp.zeros_like(dsink_ref)

        dk_ref[...] = kcar[...]
        dv_ref[...] = vcar[...]

        @pl.when(n < steps)
        def _():
            base_rest = _band_base(A_MAX_DIST, 1, False)
            base_0 = jnp.where(n > 0, base_rest, _band_base(A_MAX_DIST, 1, True))
            for i in range(nq):
                cur = pl.ds(i * BLK, BLK)
                k_prev = kc_ref[pl.ds((i - 1) * BLK, BLK), :] if i > 0 else kp_ref[...]
                v_prev = vc_ref[pl.ds((i - 1) * BLK, BLK), :] if i > 0 else vp_ref[...]
                kb = jnp.concatenate([k_prev, kc_ref[cur, :]], axis=0).astype(BF16)
                vb = jnp.concatenate([v_prev, vc_ref[cur, :]], axis=0).astype(BF16)
                dk_win = dv_win = None
                for j in range(NH // 2):
                    g = j // 2
                    dq2, dk2, dv2, dsk = _pair_bwd(q_ref[j, cur, :], kb, vb, do_ref[j, cur, :], o_ref[j, cur, :],
                                                   lse_ref[j, cur, :], base_rest if i > 0 else base_0,
                                                   (SLOPES[2 * j], SLOPES[2 * j + 1]), (g, g),
                                                   (sink_ref[2 * j], sink_ref[2 * j + 1]))
                    dq_ref[j, cur, :] = dq2
                    dk_win = dk2 if j == 0 else dk_win + dk2
                    dv_win = dv2 if j == 0 else dv_win + dv2
                    for e in (0, 1):
                        h = 2 * j + e
                        dsink_ref[h:h + 1, :] += jnp.broadcast_to(dsk[e], (1, 128))
                if i == 0:
                    last = pl.ds((nq - 1) * BLK, BLK)
                    dk_ref[last, :] += dk_win[:BLK]
                    dv_ref[last, :] += dv_win[:BLK]
                else:
                    kcar[pl.ds((i - 1) * BLK, BLK), :] += dk_win[:BLK]
                    vcar[pl.ds((i - 1) * BLK, BLK), :] += dv_win[:BLK]
                kcar[cur, :] = dk_win[BLK:]
                vcar[cur, :] = dv_win[BLK:]

    cur_step = lambda n: jnp.minimum(n, steps - 1)
    before = lambda n: jnp.maximum(cur_step(n) * nq - 1, 0)
    out_prev = lambda n: jnp.maximum(n - 1, 0)
    quad = pl.BlockSpec((4, rows, 128), lambda n: (0, cur_step(n), 0))
    slab = lambda g: pl.BlockSpec((None, rows, 128), lambda n: (g, cur_step(n), 0))
    edge = lambda g: pl.BlockSpec((None, BLK, 128), lambda n: (g, before(n), 0))
    return pl.pallas_call(
        body, name="attn_a_bwd", grid=(steps + 1,),
        in_specs=[SMEM, quad, edge(4), slab(4), edge(5), slab(5), quad, quad, quad],
        out_specs=[quad,
                   pl.BlockSpec((rows, 128), lambda n: (out_prev(n), 0)),
                   pl.BlockSpec((rows, 128), lambda n: (out_prev(n), 0)),
                   pl.BlockSpec((NH, 128), lambda n: (0, 0))],
        out_shape=[jax.ShapeDtypeStruct((4, s, 128), F32), jax.ShapeDtypeStruct((s, 128), F32),
                   jax.ShapeDtypeStruct((s, 128), F32), jax.ShapeDtypeStruct((NH, 128), F32)],
        scratch_shapes=[pltpu.VMEM((rows, 128), F32), pltpu.VMEM((rows, 128), F32)],
        compiler_params=_cp(("arbitrary",)),
    )(sinks, proj, proj, proj, proj, proj, d_o, o, lse)


def _stream(rho, i, r):
    start = i * BLK * r + rho
    return pl.ds(start, BLK, stride=r) if r > 1 else pl.ds(start, BLK)


def _for_streams(r, fn, side_by_side=4):
    if r <= side_by_side:
        for rho in range(r):
            fn(rho)
    else:
        def group(it, carry):
            for u in range(side_by_side):
                fn(side_by_side * it + u)
            return carry

        lax.fori_loop(0, r // side_by_side, group, 0)


B_BLOCKS_PER_STEP = {1: 8, 4: 2, 16: 1}
B_BLOCKS_PER_STEP_FWD = {1: 16, 4: 4, 16: 1}


def _attn_b_fwd(proj, slopes, r):
    s = proj.shape[1]
    nq = B_BLOCKS_PER_STEP_FWD[r]
    rows = BLK * r * nq
    steps = s // rows
    qc, kc, vc = WA // 128, WA // 128 + 4, WA // 128 + 8

    def body(slope_ref, q_ref, kp_ref, kc_ref, vp_ref, vc_ref, o_ref, lse_ref):
        j = pl.program_id(0)
        sb = pl.program_id(1)
        sl2 = (slope_ref[2 * j], slope_ref[2 * j + 1])
        bias_rest = _stack_heads(sl2, _band_base(B_MAX_DIST, r, False))
        bias_0 = jnp.where(sb > 0, bias_rest, _stack_heads(sl2, _band_base(B_MAX_DIST, r, True)))

        def stream(rho):
            for i in range(nq):
                cur = _stream(rho, i, r)
                k_prev = kc_ref[_stream(rho, i - 1, r), :] if i > 0 else kp_ref[_stream(rho, 0, r), :]
                v_prev = vc_ref[_stream(rho, i - 1, r), :] if i > 0 else vp_ref[_stream(rho, 0, r), :]
                kb = jnp.concatenate([k_prev, kc_ref[cur, :]], axis=0).astype(BF16)
                vb = jnp.concatenate([v_prev, vc_ref[cur, :]], axis=0).astype(BF16)
                o2, lse2 = _pair_fwd(q_ref[cur, :], kb, vb, bias_rest if i > 0 else bias_0, None, (0, 1), None)
                o_ref[cur, :] = o2
                lse_ref[cur, :] = lse2

        _for_streams(r, stream, side_by_side=16)

    before = lambda sb: jnp.maximum(sb * nq - 1, 0)
    return pl.pallas_call(
        body, name=f"attn_b_fwd_r{r}", grid=(NH // 2, steps),
        in_specs=[SMEM,
                  pl.BlockSpec((None, rows, 128), lambda j, sb: (qc + j, sb, 0)),
                  pl.BlockSpec((None, BLK * r, 128), lambda j, sb: (kc + j, before(sb), 0)),
                  pl.BlockSpec((None, rows, 128), lambda j, sb: (kc + j, sb, 0)),
                  pl.BlockSpec((None, BLK * r, 128), lambda j, sb: (vc + j, before(sb), 0)),
                  pl.BlockSpec((None, rows, 128), lambda j, sb: (vc + j, sb, 0))],
        out_specs=[pl.BlockSpec((None, rows, 128), lambda j, sb: (j, sb, 0))] * 2,
        out_shape=[jax.ShapeDtypeStruct((4, s, 128), F32)] * 2,
        compiler_params=_cp(("parallel", "parallel")),
    )(slopes, proj, proj, proj, proj, proj)


def _attn_b_bwd(proj, slopes, d_o, o, lse, r, so_far=None):
    s = proj.shape[1]
    nq = B_BLOCKS_PER_STEP[r]
    rows = BLK * r * nq
    steps = s // rows
    qc, kc, vc = WA // 128, WA // 128 + 4, WA // 128 + 8
    chained = so_far is not None

    def body(slope_ref, q_ref, kp_ref, kc_ref, vp_ref, vc_ref, do_ref, o_ref, lse_ref, *rest):
        if chained:
            pq_ref, pk_ref, pv_ref, dq_ref, dk_ref, dv_ref, kcar, vcar = rest
        else:
            dq_ref, dk_ref, dv_ref, kcar, vcar = rest
        j = pl.program_id(0)
        sb = pl.program_id(1)

        @pl.when(sb == 0)
        def _():
            kcar[...] = jnp.zeros_like(kcar)
            vcar[...] = jnp.zeros_like(vcar)

        if chained:
            dk_ref[...] = kcar[...] + pk_ref[...]
            dv_ref[...] = vcar[...] + pv_ref[...]
        else:
            dk_ref[...] = kcar[...]
            dv_ref[...] = vcar[...]

        @pl.when(sb < steps)
        def _():
            sl2 = (slope_ref[2 * j], slope_ref[2 * j + 1])
            bias_rest = _stack_heads(sl2, _band_base(B_MAX_DIST, r, False))
            bias_0 = jnp.where(sb > 0, bias_rest, _stack_heads(sl2, _band_base(B_MAX_DIST, r, True)))

            def stream(rho):
                for i in range(nq):
                    cur = _stream(rho, i, r)
                    k_prev = kc_ref[_stream(rho, i - 1, r), :] if i > 0 else kp_ref[_stream(rho, 0, r), :]
                    v_prev = vc_ref[_stream(rho, i - 1, r), :] if i > 0 else vp_ref[_stream(rho, 0, r), :]
                    kb = jnp.concatenate([k_prev, kc_ref[cur, :]], axis=0).astype(BF16)
                    vb = jnp.concatenate([v_prev, vc_ref[cur, :]], axis=0).astype(BF16)
                    dq2, dk2, dv2, _ = _pair_bwd(q_ref[cur, :], kb, vb, do_ref[cur, :], o_ref[cur, :], lse_ref[cur, :],
                                                 bias_rest if i > 0 else bias_0, None, (0, 1), None)
                    dq_ref[cur, :] = dq2 + pq_ref[cur, :] if chained else dq2
                    if i == 0:
                        last = _stream(rho, nq - 1, r)
                        dk_ref[last, :] += dk2[:BLK]
                        dv_ref[last, :] += dv2[:BLK]
                    else:
                        kcar[_stream(rho, i - 1, r), :] += dk2[:BLK]
                        vcar[_stream(rho, i - 1, r), :] += dv2[:BLK]
                    kcar[cur, :] = dk2[BLK:]
                    vcar[cur, :] = dv2[BLK:]

            _for_streams(r, stream, side_by_side=8)

    cur_step = lambda sb: jnp.minimum(sb, steps - 1)
    before = lambda sb: jnp.maximum(cur_step(sb) * nq - 1, 0)
    out_prev = lambda sb: jnp.maximum(sb - 1, 0)
    tile = lambda slab: pl.BlockSpec((None, rows, 128), lambda j, sb: (slab + j, cur_step(sb), 0))
    edge = lambda slab: pl.BlockSpec((None, BLK * r, 128), lambda j, sb: (slab + j, before(sb), 0))
    late = pl.BlockSpec((None, rows, 128), lambda j, sb: (j, out_prev(sb), 0))
    grads = [tile(0), late, late]
    return pl.pallas_call(
        body, name=f"attn_b_bwd_r{r}", grid=(NH // 2, steps + 1),
        in_specs=[SMEM, tile(qc), edge(kc), tile(kc), edge(vc), tile(vc), tile(0), tile(0), tile(0)]
        + (grads if chained else []),
        out_specs=grads,
        out_shape=[jax.ShapeDtypeStruct((4, s, 128), F32)] * 3,
        scratch_shapes=[pltpu.VMEM((rows, 128), F32), pltpu.VMEM((rows, 128), F32)],
        compiler_params=_cp(("parallel", "arbitrary")),
    )(slopes, proj, proj, proj, proj, proj, d_o, o, lse, *(so_far if chained else ()))


def _row(v):
    return v.reshape(1, -1)


def _layer_norm_stats(z):
    mu = jnp.mean(z, axis=-1, keepdims=True)
    zc = z - mu
    var = jnp.mean(zc * zc, axis=-1, keepdims=True)
    rstd = lax.rsqrt(var + LN_EPS)
    return zc * rstd, rstd


def _layer_norm_bwd(dh, zh, rstd, g):
    dzh = dh * g
    return rstd * (dzh - jnp.mean(dzh, axis=-1, keepdims=True) - zh * jnp.mean(dzh * zh, axis=-1, keepdims=True))


def _rms(o):
    return lax.rsqrt(jnp.mean(o * o, axis=-1, keepdims=True) + RMS_EPS)


def _mix_ln1(x, o_a, o_b, lse_b, norm_a_g, norm_b_g, w_o, ln1_g, ln1_b, tm=256):
    s = x.shape[0]

    def wide(ref):
        return jnp.concatenate([ref[j] for j in range(4)], axis=1)

    def body(x_ref, oa_ref, ob1, ob2, ob3, l1, l2, l3, ga_ref, gb_ref, wo_ref, g_ref, b_ref,
             obm_ref, lse_ref, cat_ref, z1_ref, h1_ref, h1b_ref):
        la, lb, lc = wide(l1), wide(l2), wide(l3)
        m = jnp.maximum(jnp.maximum(la, lb), lc)
        ea, eb, ec = jnp.exp(la - m), jnp.exp(lb - m), jnp.exp(lc - m)
        den = ea + eb + ec
        obm = (ea / den) * wide(ob1) + (eb / den) * wide(ob2) + (ec / den) * wide(ob3)
        lse = m + jnp.log(den)
        for j in range(4):
            obm_ref[j] = obm[:, 128 * j:128 * (j + 1)]
            lse_ref[j] = lse[:, 128 * j:128 * (j + 1)]
        oa = wide(oa_ref)
        na = oa * _rms(oa) * ga_ref[...]
        nb_ = obm * _rms(obm) * gb_ref[...]
        cat = jnp.concatenate([na, nb_], axis=1).astype(BF16)
        cat_ref[...] = cat
        z1 = ALPHA * x_ref[...] + _nn(cat, wo_ref[...])
        z1_ref[...] = z1
        zh, _ = _layer_norm_stats(z1)
        h1 = zh * g_ref[...] + b_ref[...]
        h1_ref[...] = h1
        h1b_ref[...] = h1.astype(BF16)

    t512 = pl.BlockSpec((4, tm, 128), lambda i: (0, i, 0))
    td = pl.BlockSpec((tm, D), lambda i: (i, 0))
    return pl.pallas_call(
        body, name="mix_ln1", grid=(s // tm,),
        in_specs=[td] + [t512] * 7 + [_const((1, 512))] * 2 + [_resident((D, D))] + [_const((1, D))] * 2,
        out_specs=[t512, t512, td, td, td, td],
        out_shape=[jax.ShapeDtypeStruct((4, s, 128), F32), jax.ShapeDtypeStruct((4, s, 128), F32),
                   jax.ShapeDtypeStruct((s, D), BF16), jax.ShapeDtypeStruct((s, D), F32),
                   jax.ShapeDtypeStruct((s, D), F32), jax.ShapeDtypeStruct((s, D), BF16)],
        compiler_params=_cp(("parallel",)),
    )(x, o_a, *o_b, *lse_b, _row(norm_a_g), _row(norm_b_g), w_o, _row(ln1_g), _row(ln1_b))


def _gelu_and_grad(x):
    c = math.sqrt(2.0 / math.pi)
    x2 = x * x
    s = 0.5 * jnp.tanh(x * ((c * 0.044715) * x2 + c)) + 0.5
    dg = s + (x * ((6.0 * c * 0.044715) * x2 + 2.0 * c)) * (s - s * s)
    return x * s, dg


def _shifted(u, rolled_edge, row, down):
    groups = [u[8 * i:8 * i + 8] for i in range(u.shape[0] // 8)]
    moved, edge = [], []
    for k in (1, 2):
        rolled = [pltpu.roll(g, k if down else 8 - k, 0) for g in groups]
        if down:
            other, wrapped = [rolled_edge[k - 1]] + rolled[:-1], row < k
        else:
            other, wrapped = rolled[1:] + [rolled_edge[k - 1]], row >= 8 - k
        moved.append(jnp.concatenate([jnp.where(wrapped, o, r) for o, r in zip(other, rolled)], axis=0))
        edge.append(rolled[-1] if down else rolled[0])
    return moved, tuple(edge)


def _up_proj(h1b, w_up, tm=512):
    s = h1b.shape[0]

    def body(h_ref, w_ref, o_ref):
        h = h_ref[...]
        for half in (0, 1):
            o_ref[half] = _nn(h, w_ref[:, half * FF:(half + 1) * FF]).astype(BF16)

    return pl.pallas_call(
        body, name="up_proj", grid=(s // tm,),
        in_specs=[pl.BlockSpec((tm, D), lambda i: (i, 0)), _resident((D, 2 * FF))],
        out_specs=pl.BlockSpec((2, tm, FF), lambda i: (0, i, 0)),
        out_shape=jax.ShapeDtypeStruct((2, s, FF), BF16),
        compiler_params=_cp(("parallel",)),
    )(h1b, w_up)


def _conv_gelu(up, cwb, tm=256, tn=FF // 2, chunk_rows=16):
    s = up.shape[1]
    n_c = tm // chunk_rows

    def body(up_ref, c_ref, a_ref, g_ref, a1_ref, carry):
        @pl.when(pl.program_id(1) == 0)
        def _():
            carry[...] = jnp.zeros_like(carry)

        row = lax.broadcasted_iota(jnp.int32, (8, tn), 0)
        edge = [(carry[half, 0], carry[half, 1]) for half in (0, 1)]
        for c in range(n_c):
            rows = pl.ds(c * chunk_rows, chunk_rows)
            u = []
            for half in (0, 1):
                x = up_ref[half, rows, :].astype(F32)
                (r1, r2), edge[half] = _shifted(x, edge[half], row, True)
                u.append(r2 * c_ref[0, half:half + 1, :] + r1 * c_ref[1, half:half + 1, :]
                         + x * c_ref[2, half:half + 1, :] + c_ref[3, half:half + 1, :])
            g, dg = _gelu_and_grad(u[0])
            a_ref[rows, :] = (g * u[1]).astype(BF16)
            g_ref[rows, :] = g.astype(BF16)
            a1_ref[rows, :] = (u[1] * dg).astype(BF16)
        for half in (0, 1):
            for k in (0, 1):
                carry[half, k] = edge[half][k]

    pair = pl.BlockSpec((2, tm, tn), lambda j, i: (0, i, j))
    tile = pl.BlockSpec((tm, tn), lambda j, i: (i, j))
    return pl.pallas_call(
        body, name="conv_gelu", grid=(FF // tn, s // tm),
        in_specs=[pair, pl.BlockSpec((4, 2, tn), lambda j, i: (0, 0, j))],
        out_specs=[tile, tile, tile],
        out_shape=[jax.ShapeDtypeStruct((s, FF), BF16)] * 3,
        scratch_shapes=[pltpu.VMEM((2, 2, 8, tn), F32)],
        compiler_params=_cp(("parallel", "arbitrary")),
    )(up, cwb)


def _down_ln2_loss(a, w_down, h1, target, ln2_g, ln2_b, tm=512):
    s = a.shape[0]

    def body(a_ref, w_ref, h_ref, t_ref, g_ref, b_ref, dz_ref, dzb_ref, st_ref):
        @pl.when(pl.program_id(0) == 0)
        def _():
            st_ref[...] = jnp.zeros_like(st_ref)

        z2 = ALPHA * h_ref[...] + _nn(a_ref[...], w_ref[...])
        zh, rstd = _layer_norm_stats(z2)
        diff = zh * g_ref[...] + b_ref[...] - t_ref[...]
        part = 0.5 * jnp.sum(jnp.mean(diff * diff, axis=-1, keepdims=True), axis=0, keepdims=True)
        dy = diff * (1.0 / D)
        st_ref[0:1, :] += jnp.sum(dy * zh, axis=0, keepdims=True)
        st_ref[1:2, :] += jnp.sum(dy, axis=0, keepdims=True)
        st_ref[2:3, :] += jnp.broadcast_to(part, (1, D))
        dz = _layer_norm_bwd(dy, zh, rstd, g_ref[...])
        dz_ref[...] = dz
        dzb_ref[...] = dz.astype(BF16)

    td = pl.BlockSpec((tm, D), lambda i: (i, 0))
    return pl.pallas_call(
        body, name="down_ln2_loss", grid=(s // tm,),
        in_specs=[pl.BlockSpec((tm, FF), lambda i: (i, 0)), _resident((FF, D)), td, td, _const((1, D)), _const((1, D))],
        out_specs=[td, td, _const((8, D))],
        out_shape=[jax.ShapeDtypeStruct((s, D), F32), jax.ShapeDtypeStruct((s, D), BF16),
                   jax.ShapeDtypeStruct((8, D), F32)],
        compiler_params=_cp(("arbitrary",)),
    )(a, w_down, h1, target, _row(ln2_g), _row(ln2_b))


def _d_act(dz2b, w_down, tm=512):
    s = dz2b.shape[0]

    def body(dz_ref, w_ref, o_ref):
        o_ref[...] = _nt(dz_ref[...], w_ref[...]).astype(BF16)

    return pl.pallas_call(
        body, name="d_act", grid=(s // tm,),
        in_specs=[pl.BlockSpec((tm, D), lambda i: (i, 0)), _resident((FF, D))],
        out_specs=pl.BlockSpec((tm, FF), lambda i: (i, 0)),
        out_shape=jax.ShapeDtypeStruct((s, FF), BF16),
        compiler_params=_cp(("parallel",)),
    )(dz2b, w_down)


def _conv_gelu_bwd(da, up, g, a1, cwb, tm=256, tn=FF // 2, chunk_rows=16):
    s = da.shape[0]
    n_i = s // tm
    n_c = tm // chunk_rows

    def body(da_ref, up_ref, g_ref, a1_ref, c_ref, dup_ref, dc_ref, carry):
        @pl.when(pl.program_id(1) == 0)
        def _():
            carry[...] = jnp.zeros_like(carry)
            dc_ref[...] = jnp.zeros_like(dc_ref)

        def fold(v):
            return jnp.sum(v.reshape(chunk_rows // 8, 8, v.shape[1]), axis=0)

        def chunk(cc, state):
            after, sums = state
            rows = pl.ds((n_c - 1 - cc) * chunk_rows, chunk_rows)
            da_c = da_ref[rows, :].astype(F32)
            dus = (da_c * a1_ref[rows, :].astype(F32), da_c * g_ref[rows, :].astype(F32))
            head, new_sums = [], []
            for half in (0, 1):
                du = dus[half]
                up = up_ref[half, rows, :].astype(F32)
                (l1, l2), first = _shifted(du, after[half], row, False)
                dup = (du * c_ref[2, half:half + 1, :] + l1 * c_ref[1, half:half + 1, :]
                       + l2 * c_ref[0, half:half + 1, :])
                dup_ref[half, rows, :] = dup.astype(BF16)
                parts = (fold(l2 * up), fold(l1 * up), fold(du * up), fold(du))
                new_sums.append(parts if sums is None else tuple(a + b for a, b in zip(sums[half], parts)))
                head.append(first)
            return tuple(head), new_sums

        row = lax.broadcasted_iota(jnp.int32, (8, tn), 0)
        state = (tuple((carry[half, 0], carry[half, 1]) for half in (0, 1)), None)
        for cc in range(n_c):
            state = chunk(cc, state)
        head, sums = state
        for half in (0, 1):
            for k in (0, 1):
                carry[half, k] = head[half][k]
            for k in range(4):
                dc_ref[k, half:half + 1, :] += jnp.sum(sums[half][k], axis=0, keepdims=True)

    rev = lambda ii: n_i - 1 - ii
    tile = pl.BlockSpec((tm, tn), lambda j, ii: (rev(ii), j))
    pair = pl.BlockSpec((2, tm, tn), lambda j, ii: (0, rev(ii), j))
    per_col = pl.BlockSpec((4, 2, tn), lambda j, ii: (0, 0, j))
    return pl.pallas_call(
        body, name="conv_gelu_bwd", grid=(FF // tn, n_i),
        in_specs=[tile, pair, tile, tile, per_col],
        out_specs=[pair, per_col],
        out_shape=[jax.ShapeDtypeStruct((2, s, FF), BF16), jax.ShapeDtypeStruct((4, 2, FF), F32)],
        scratch_shapes=[pltpu.VMEM((2, 2, 8, tn), F32)],
        compiler_params=_cp(("parallel", "arbitrary")),
    )(da, up, g, a1, cwb)


def _dh1_ln1_bwd(dz2, dup, w_up, z1, ln1_g, tm=512):
    s = dz2.shape[0]

    def body(dz2_ref, dup_ref, w_ref, z1_ref, g_ref, dz1_ref, dz1b_ref, st_ref):
        @pl.when(pl.program_id(0) == 0)
        def _():
            st_ref[...] = jnp.zeros_like(st_ref)

        dh = ALPHA * dz2_ref[...] + _nt(dup_ref[0], w_ref[:, :FF]) + _nt(dup_ref[1], w_ref[:, FF:])
        zh, rstd = _layer_norm_stats(z1_ref[...])
        st_ref[0:1, :] += jnp.sum(dh * zh, axis=0, keepdims=True)
        st_ref[1:2, :] += jnp.sum(dh, axis=0, keepdims=True)
        dz = _layer_norm_bwd(dh, zh, rstd, g_ref[...])
        dz1_ref[...] = dz
        dz1b_ref[...] = dz.astype(BF16)

    td = pl.BlockSpec((tm, D), lambda i: (i, 0))
    return pl.pallas_call(
        body, name="dh1_ln1_bwd", grid=(s // tm,),
        in_specs=[td, pl.BlockSpec((2, tm, FF), lambda i: (0, i, 0)), _resident((D, 2 * FF)), td, _const((1, D))],
        out_specs=[td, td, _const((8, D))],
        out_shape=[jax.ShapeDtypeStruct((s, D), F32), jax.ShapeDtypeStruct((s, D), BF16),
                   jax.ShapeDtypeStruct((8, D), F32)],
        compiler_params=_cp(("arbitrary",), 58),
    )(dz2, dup, w_up, z1, _row(ln1_g))


def _dcat_rms_bwd(dz1b, w_o, o_a, o_b, norm_a_g, norm_b_g, tm=512):
    s = dz1b.shape[0]

    def body(dz_ref, w_ref, oa_ref, ob_ref, ga_ref, gb_ref, da_ref, db_ref, st_ref):
        @pl.when(pl.program_id(0) == 0)
        def _():
            st_ref[...] = jnp.zeros_like(st_ref)

        dcat = _nt(dz_ref[...], w_ref[...])
        for k, (o_ref, g_ref, d_ref) in enumerate(((oa_ref, ga_ref, da_ref), (ob_ref, gb_ref, db_ref))):
            o = jnp.concatenate([o_ref[j] for j in range(4)], axis=1)
            dn = dcat[:, 512 * k:512 * (k + 1)]
            rr = _rms(o)
            oh = o * rr
            st_ref[k:k + 1, :] += jnp.sum(dn * oh, axis=0, keepdims=True)
            doh = dn * g_ref[...]
            d_o = rr * (doh - oh * jnp.mean(doh * oh, axis=-1, keepdims=True))
            for j in range(4):
                d_ref[j] = d_o[:, 128 * j:128 * (j + 1)]

    t512 = pl.BlockSpec((4, tm, 128), lambda i: (0, i, 0))
    return pl.pallas_call(
        body, name="dcat_rms_bwd", grid=(s // tm,),
        in_specs=[pl.BlockSpec((tm, D), lambda i: (i, 0)), _resident((D, D)), t512, t512,
                  _const((1, 512)), _const((1, 512))],
        out_specs=[t512, t512, _const((8, 512))],
        out_shape=[jax.ShapeDtypeStruct((4, s, 128), F32), jax.ShapeDtypeStruct((4, s, 128), F32),
                   jax.ShapeDtypeStruct((8, 512), F32)],
        compiler_params=_cp(("arbitrary",)),
    )(dz1b, w_o, o_a, o_b, _row(norm_a_g), _row(norm_b_g))


def _dproj_combine(dqa, dka, dva, dqkv_b, tm=256):
    s = dka.shape[0]

    def body(qa, ka, va, qb, kb, vb, o_ref):
        for j in range(4):
            o_ref[:, 128 * j:128 * (j + 1)] = qa[j].astype(BF16)
            o_ref[:, 768 + 128 * j:768 + 128 * (j + 1)] = qb[j].astype(BF16)
            o_ref[:, 1280 + 128 * j:1280 + 128 * (j + 1)] = kb[j].astype(BF16)
            o_ref[:, 1792 + 128 * j:1792 + 128 * (j + 1)] = vb[j].astype(BF16)
        o_ref[:, 512:640] = ka[...].astype(BF16)
        o_ref[:, 640:768] = va[...].astype(BF16)

    t512 = pl.BlockSpec((4, tm, 128), lambda i: (0, i, 0))
    t128 = pl.BlockSpec((tm, 128), lambda i: (i, 0))
    return pl.pallas_call(
        body, name="dproj_combine", grid=(s // tm,),
        in_specs=[t512, t128, t128] + [t512] * 3,
        out_specs=pl.BlockSpec((tm, WIN), lambda i: (i, 0)),
        out_shape=jax.ShapeDtypeStruct((s, WIN), BF16),
        compiler_params=_cp(("parallel",)),
    )(dqa, dka, dva, *dqkv_b)


def _grad_x(dz1, dproj, w_in_t, zero, tm=512):
    s = dz1.shape[0]

    def body(dz_ref, dp_ref, w_ref, z_ref, o_ref):
        o_ref[...] = ALPHA * dz_ref[...] + _nn(dp_ref[...], w_ref[...]) + z_ref[0:1, 0:1]

    td = pl.BlockSpec((tm, D), lambda i: (i, 0))
    return pl.pallas_call(
        body, name="grad_x", grid=(s // tm,),
        in_specs=[td, pl.BlockSpec((tm, WIN), lambda i: (i, 0)), _resident((WIN, D)), _const((8, 128))],
        out_specs=td, out_shape=jax.ShapeDtypeStruct((s, D), F32),
        compiler_params=_cp(("parallel",)),
    )(dz1, dproj, w_in_t, zero)


def _place():
    return lax.axis_index("x"), lax.axis_index("y"), lax.axis_index("c")


def _other_chips(x, y):
    return [(1 - x, y), (x, 1 - y), (1 - x, 1 - y)]


def _hbm(a):
    return pltpu.with_memory_space_constraint(a, pltpu.HBM)


def _gather_w_in(shard, conv_w):
    rows_k = shard.shape[0]
    half = rows_k // 2

    def body(src, conv_src, out, conv_out, send_sems, recv_sems):
        x, y, c = _place()
        b = 2 * x + y
        sibling = (x, y, 1 - c)
        chips = _other_chips(x, y)

        def copy(idx, chip_b, core, to, first_hop=False):
            rows = out.at[pl.ds(pl.multiple_of(chip_b * rows_k + core * half, 16), half)]
            s_ref = src.at[pl.ds(pl.multiple_of(core * half, 16), half)] if first_hop else rows
            return pltpu.make_async_remote_copy(src_ref=s_ref, dst_ref=rows, send_sem=send_sems.at[idx],
                                                recv_sem=recv_sems.at[idx], device_id=to, device_id_type=MESH)

        def own_copy():
            return pltpu.make_async_remote_copy(
                src_ref=src, dst_ref=out.at[pl.ds(pl.multiple_of(b * rows_k, 16), rows_k)], send_sem=send_sems.at[6],
                recv_sem=recv_sems.at[6], device_id=sibling, device_id_type=MESH)

        def conv_copy(idx, chip_b, to):
            return pltpu.make_async_remote_copy(src_ref=conv_src, dst_ref=conv_out.at[chip_b],
                                                send_sem=send_sems.at[7 + idx], recv_sem=recv_sems.at[7 + idx],
                                                device_id=to, device_id_type=MESH)

        started = [own_copy(), conv_copy(3, b, sibling)]
        for jn, chip in enumerate(chips):
            started += [copy(jn, b, c, (chip[0], chip[1], c), first_hop=True), conv_copy(jn, b, (chip[0], chip[1], c))]
        for cp in started:
            cp.start()
        for jn, chip in enumerate(chips):
            cb = 2 * chip[0] + chip[1]
            copy(jn, cb, c, (chip[0], chip[1], c)).wait_recv()
            cp = copy(3 + jn, cb, c, sibling)
            cp.start()
            started.append(cp)
        for jn, chip in enumerate(chips):
            cb = 2 * chip[0] + chip[1]
            copy(3 + jn, cb, 1 - c, sibling).wait_recv()
            conv_copy(jn, cb, (chip[0], chip[1], c)).wait_recv()
        own_copy().wait_recv()
        conv_copy(3, b, sibling).wait_recv()
        for cp in started:
            cp.wait_send()

    return pl.pallas_call(
        body, name="gather_w_in",
        in_specs=[ANY, ANY], out_specs=[ANY, ANY],
        out_shape=[jax.ShapeDtypeStruct((N_CHIPS * rows_k, D), BF16), jax.ShapeDtypeStruct((N_CHIPS,) + conv_w.shape, F32)],
        scratch_shapes=[pltpu.SemaphoreType.DMA((11,)), pltpu.SemaphoreType.DMA((11,))],
        compiler_params=pltpu.CompilerParams(has_side_effects=True),
    )(shard, conv_w)


def _weight_copies(shard, land, send_sems, recv_sems, arrivals):
    x, y, c = _place()
    n_rows, n_cols = shard.shape
    peers = [(px, py, c) for px, py in _other_chips(x, y)] + [(x, y, 1 - c)]
    cps = []
    for jn, peer in enumerate(peers):
        at = 2 * peer[0] + peer[1] if arrivals else 2 * x + y
        if land.shape[1] == n_cols:
            dst = land.at[pl.ds(pl.multiple_of(at * n_rows, 16), n_rows)]
        else:
            dst = land.at[:, pl.ds(pl.multiple_of(at * n_cols, 128), n_cols)]
        cps.append(pltpu.make_async_remote_copy(src_ref=shard, dst_ref=dst, send_sem=send_sems.at[jn],
                                                recv_sem=recv_sems.at[jn], device_id=peer, device_id_type=MESH))
    return cps


def _weights_start(shards, after):
    n = len(shards)
    lands = [lax.empty((N_CHIPS * sh.shape[0], D) if sh.shape[1] == D else (D, N_CHIPS * sh.shape[1]), BF16)
             for sh in shards]

    def body(*refs):
        src, land = refs[:n], refs[n:2 * n]
        send_sems, recv_sems = refs[2 * n + 1:3 * n + 1], refs[3 * n + 1:4 * n + 1]
        for k in range(n):
            for send in _weight_copies(src[k], land[k], send_sems[k], recv_sems[k], False):
                send.start()
        refs[-1][...] = jnp.zeros_like(refs[-1])

    res = pl.pallas_call(
        body, name="weights_start",
        in_specs=[HBM] * (2 * n) + [ANY], out_specs=[SEM] * (2 * n) + [HBM] * (2 * n) + [VMEM],
        out_shape=[pltpu.SemaphoreType.DMA((4,))] * (2 * n)
        + [pltpu.HBM(a.shape, a.dtype) for a in (*shards, *lands)] + [jax.ShapeDtypeStruct((8, 128), F32)],
        input_output_aliases={i: i + 2 * n for i in range(2 * n)},
        compiler_params=pltpu.CompilerParams(has_side_effects=DATAFLOW),
    )(*[_hbm(a) for a in (*shards, *lands)], after)
    return [(res[k], res[n + k], res[2 * n + k], res[3 * n + k]) for k in range(n)], res[-1]


def _weights_wait(started, after, name):
    send_sems, recv_sems, shard, land = started

    def body(s_ref, l_ref, send_ref, recv_ref, after_ref, s_out, l_out):
        for cp in _weight_copies(s_ref, l_ref, send_ref, recv_ref, True):
            cp.wait_send()
            cp.wait_recv()

    return pl.pallas_call(
        body, name=name,
        in_specs=[HBM, HBM, SEM, SEM, ANY], out_specs=[HBM, HBM],
        out_shape=[pltpu.HBM(shard.shape, shard.dtype), pltpu.HBM(land.shape, land.dtype)],
        input_output_aliases={0: 0, 1: 1},
        compiler_params=pltpu.CompilerParams(has_side_effects=DATAFLOW),
    )(shard, land, send_sems, recv_sems, after)[1]


def _grad_copies(g_ref, land_ref, send_sems, recv_sems):
    x, y, c = _place()
    cps = []
    for d in range(1, 8):
        px, py, pc = x ^ (d >> 2), y ^ ((d >> 1) & 1), c ^ (d & 1)
        cps.append(pltpu.make_async_remote_copy(
            src_ref=g_ref.at[2 * px + py, pc], dst_ref=land_ref.at[d - 1], send_sem=send_sems.at[d - 1],
            recv_sem=recv_sems.at[d - 1], device_id=(px, py, pc), device_id_type=MESH))
    return cps


def _grads_start(grads_b, name):
    n = len(grads_b)
    lands = [lax.empty((7, g.shape[2], D), BF16) for g in grads_b]

    def body(*refs):
        g, land = refs[:n], refs[n:2 * n]
        send_sems, recv_sems = refs[2 * n:3 * n], refs[3 * n:4 * n]
        for k in range(n):
            for cp in _grad_copies(g[k], land[k], send_sems[k], recv_sems[k]):
                cp.start()
        refs[-1][...] = jnp.zeros_like(refs[-1])

    res = pl.pallas_call(
        body, name=name,
        in_specs=[HBM] * (2 * n), out_specs=[SEM] * (2 * n) + [HBM] * (2 * n) + [VMEM],
        out_shape=[pltpu.SemaphoreType.DMA((7,))] * (2 * n)
        + [pltpu.HBM(a.shape, a.dtype) for a in (*grads_b, *lands)] + [jax.ShapeDtypeStruct((8, 128), F32)],
        input_output_aliases={i: i + 2 * n for i in range(2 * n)},
        compiler_params=pltpu.CompilerParams(has_side_effects=DATAFLOW),
    )(*[_hbm(a) for a in (*grads_b, *lands)])
    return [(res[k], res[n + k], res[2 * n + k], res[3 * n + k]) for k in range(n)], res[-1]


def _grads_wait(started, after, name):
    n = len(started)

    def body(*refs):
        g, land = refs[:n], refs[n:2 * n]
        send_sems, recv_sems = refs[2 * n:3 * n], refs[3 * n:4 * n]
        for k in range(n):
            for cp in _grad_copies(g[k], land[k], send_sems[k], recv_sems[k]):
                cp.wait_send()
                cp.wait_recv()

    gs = [st[2] for st in started]
    lands = [st[3] for st in started]
    res = pl.pallas_call(
        body, name=name,
        in_specs=[HBM] * (2 * n) + [SEM] * (2 * n) + [ANY], out_specs=[HBM] * (2 * n),
        out_shape=[pltpu.HBM(a.shape, a.dtype) for a in (*gs, *lands)],
        input_output_aliases={i: i for i in range(2 * n)},
        compiler_params=pltpu.CompilerParams(has_side_effects=DATAFLOW),
    )(*gs, *lands, *[st[0] for st in started], *[st[1] for st in started], after)
    return res[n:]


def _sum_partials(grad4, got, cb, name, tr):
    h = grad4.shape[2]
    per_half = h // tr

    def body(cb_ref, g_ref, o_ref, out_ref):
        acc = g_ref[...]
        for j in range(7):
            acc = acc + o_ref[j].astype(F32)
        out_ref[...] = acc

    return pl.pallas_call(
        body, name=name,
        grid_spec=pltpu.PrefetchScalarGridSpec(
            num_scalar_prefetch=1, grid=(per_half,),
            in_specs=[pl.BlockSpec((None, None, tr, D), lambda i, cb_ref: (cb_ref[1], cb_ref[0], i, 0)),
                      pl.BlockSpec((7, tr, D), lambda i, cb_ref: (0, i, 0))],
            out_specs=pl.BlockSpec((tr, D), lambda i, cb_ref: (cb_ref[0] * per_half + i, 0))),
        out_shape=jax.ShapeDtypeStruct((2 * h, D), F32),
        compiler_params=_cp(("arbitrary",)),
    )(cb, grad4, got)


def _swap_halves(shards, name):
    n = len(shards)

    def body(*refs):
        out, send_sems, recv_sems = refs[n:2 * n], refs[2 * n], refs[2 * n + 1]
        x, y, c = _place()
        cps = []
        for k in range(n):
            h = shards[k].shape[0] // 2
            mine = out[k].at[pl.ds(pl.multiple_of(c * h, 8), h)]
            cp = pltpu.make_async_remote_copy(src_ref=mine, dst_ref=mine, send_sem=send_sems.at[k],
                                              recv_sem=recv_sems.at[k], device_id=(x, y, 1 - c), device_id_type=MESH)
            cp.start()
            cps.append(cp)
        for cp in cps:
            cp.wait()

    return pl.pallas_call(
        body, name=name,
        in_specs=[ANY] * n, out_specs=[ANY] * n,
        out_shape=[jax.ShapeDtypeStruct(sh.shape, F32) for sh in shards],
        input_output_aliases={k: k for k in range(n)},
        scratch_shapes=[pltpu.SemaphoreType.DMA((n,)), pltpu.SemaphoreType.DMA((n,))],
        compiler_params=pltpu.CompilerParams(has_side_effects=True),
    )(*shards)


def _small_copies(small_ref, land_ref, send_sems, recv_sems):
    x, y, c = _place()
    me = 4 * x + 2 * y + c
    cps = []
    for d in range(1, 8):
        px, py, pc = x ^ (d >> 2), y ^ ((d >> 1) & 1), c ^ (d & 1)
        cps.append(pltpu.make_async_remote_copy(
            src_ref=small_ref, dst_ref=land_ref.at[me], send_sem=send_sems.at[d - 1], recv_sem=recv_sems.at[d - 1],
            device_id=(px, py, pc), device_id_type=MESH))
    return cps


def _small_start(small):
    land = lax.empty((8,) + small.shape, F32)

    def body(s_ref, l_ref, send_sems, recv_sems, s_thru, l_thru, token):
        for cp in _small_copies(s_ref, l_ref, send_sems, recv_sems):
            cp.start()
        token[...] = jnp.zeros_like(token)

    res = pl.pallas_call(
        body, name="small_start",
        in_specs=[HBM, HBM], out_specs=[SEM, SEM, HBM, HBM, VMEM],
        out_shape=[pltpu.SemaphoreType.DMA((7,)), pltpu.SemaphoreType.DMA((7,)), pltpu.HBM(small.shape, F32),
                   pltpu.HBM(land.shape, F32), jax.ShapeDtypeStruct((8, 128), F32)],
        input_output_aliases={0: 2, 1: 3},
        compiler_params=pltpu.CompilerParams(has_side_effects=DATAFLOW),
    )(_hbm(small), _hbm(land))
    return res[:4], res[4]


def _small_wait(started, after):
    send_sems, recv_sems, small, land = started

    def body(s_ref, l_ref, send_ref, recv_ref, after_ref, s_out, l_out):
        for cp in _small_copies(s_ref, l_ref, send_ref, recv_ref):
            cp.wait_send()
            cp.wait_recv()

    return pl.pallas_call(
        body, name="small_wait",
        in_specs=[HBM, HBM, SEM, SEM, ANY], out_specs=[HBM, HBM],
        out_shape=[pltpu.HBM(small.shape, F32), pltpu.HBM(land.shape, F32)],
        input_output_aliases={0: 0, 1: 1},
        compiler_params=pltpu.CompilerParams(has_side_effects=DATAFLOW),
    )(small, land, send_sems, recv_sems, after)


def _small_sum(small, land, me):
    rows = small.shape[0]

    def body(me_ref, s_ref, l_ref, o_ref):
        acc = None
        for k in range(8):
            term = jnp.where(me_ref[0] == k, s_ref[...], l_ref[k])
            acc = term if k == 0 else acc + term
        o_ref[...] = acc

    return pl.pallas_call(
        body, name="small_sum",
        in_specs=[SMEM, VMEM, VMEM], out_specs=VMEM,
        out_shape=jax.ShapeDtypeStruct((rows, D), F32),
    )(me, small, land)


def _adamw(w, g, m, v, name, tr):
    rows, cols = w.shape

    def body(w_ref, g_ref, m_ref, v_ref, d_ref, nm_ref, nv_ref):
        g_ = g_ref[...]
        nm = ADAM_B1 * m_ref[...] + (1.0 - ADAM_B1) * g_
        nv = ADAM_B2 * v_ref[...] + (1.0 - ADAM_B2) * (g_ * g_)
        m_hat = nm / (1.0 - ADAM_B1 ** ADAM_STEP)
        v_hat = nv / (1.0 - ADAM_B2 ** ADAM_STEP)
        d_ref[...] = -ADAM_LR * (m_hat / (jnp.sqrt(v_hat) + ADAM_EPS) + ADAM_WD * w_ref[...])
        nm_ref[...] = nm
        nv_ref[...] = nv

    spec = pl.BlockSpec((tr, cols), lambda i: (i, 0))
    return pl.pallas_call(
        body, name=name, grid=(rows // tr,),
        in_specs=[spec] * 4, out_specs=[spec] * 3,
        out_shape=[jax.ShapeDtypeStruct((rows, cols), F32)] * 3,
        compiler_params=_cp(("parallel",)),
    )(w, g, m, v)


def _local_step(x, target, w_in_t, late_weights, norm_a_g, norm_b_g, sinks_a, ln1_g, ln1_b,
                conv_w, conv_b, ln2_g, ln2_b, slopes, on_grad, on_small):
    cwb = jnp.concatenate([conv_w, conv_b[None]], axis=0).reshape(4, 2, FF)

    proj, xb = _proj(x, w_in_t, "proj")
    o_a, lse_a = _attn_a_fwd(proj, sinks_a)
    fwd_b = [_attn_b_fwd(proj, slopes, r) for r in B_DILATIONS]
    w_o = late_weights(1, fwd_b[-1][1])
    o_b, lse_b, cat, z1, h1, h1b = _mix_ln1(x, o_a, [f[0] for f in fwd_b], [f[1] for f in fwd_b],
                                           norm_a_g, norm_b_g, w_o, ln1_g, ln1_b)
    w_up = late_weights(2, h1b)
    up = _up_proj(h1b, w_up)
    a, gate, a1 = _conv_gelu(up, cwb)
    w_down = late_weights(3, a)
    dz2, dz2b, st2 = _down_ln2_loss(a, w_down, h1, target, ln2_g, ln2_b)

    on_grad(3, *_grad_w(a, dz2b, "grad_w_down", tm=FF // 2))
    dup, dconv = _conv_gelu_bwd(_d_act(dz2b, w_down), up, gate, a1, cwb)
    on_grad(2, *_grad_w(dup, h1b, "grad_w_up", tm=FF // 2, lhs_halves=True))
    dz1, dz1b, st1 = _dh1_ln1_bwd(dz2, dup, w_up, z1, ln1_g)
    tok = on_grad(1, *_grad_w(cat, dz1b, "grad_w_o", tm=512))
    d_oa, d_ob, st_n = _dcat_rms_bwd(dz1b, w_o, o_a, o_b, norm_a_g + tok[0, 0], norm_b_g)
    dqa, dka, dva, dsink = _attn_a_bwd(proj, sinks_a, d_oa, o_a, lse_a)
    dconv = dconv.reshape(4, 2 * FF)
    tok = on_small(dict(loss=st2[2, 0:1], norm_a_g=st_n[0], norm_b_g=st_n[1], sinks_a=dsink[:, 0],
                        ln1_g=st1[0], ln1_b=st1[1], conv_w=dconv[0:3].reshape(-1), conv_b=dconv[3],
                        ln2_g=st2[0], ln2_b=st2[1]))
    slopes = slopes + tok[0, 0]
    bwd_b = None
    for r in reversed(B_DILATIONS):
        bwd_b = _attn_b_bwd(proj, slopes, d_ob, o_b, lse_b, r, bwd_b)
    dproj = _dproj_combine(dqa, dka, dva, bwd_b)
    tok = on_grad(0, *_grad_w(dproj, xb, "grad_w_in", tm=WA))
    return _grad_x(dz1, dproj, w_in_t, tok)


SMALL_ORDER = ("loss", "norm_a_g", "norm_b_g", "sinks_a", "ln1_g", "ln1_b", "conv_b", "ln2_g", "ln2_b", "conv_w")
SMALL_SIZES = dict(loss=1, norm_a_g=512, norm_b_g=512, sinks_a=8, ln1_g=D, ln1_b=D, conv_b=2 * FF, ln2_g=D, ln2_b=D,
                   conv_w=3 * 2 * FF)


def _pack(parts, rows):
    flat = jnp.concatenate([parts[k].reshape(-1).astype(F32) for k in parts])
    return jnp.pad(flat, (0, rows * D - flat.shape[0])).reshape(rows, D)


def _unpack(buf, names, sizes):
    flat = buf.reshape(-1)
    out, at = {}, 0
    for k in names:
        out[k] = flat[at:at + sizes[k]]
        at += sizes[k]
    return out


def kernel(x, w_in, norm_a_g, norm_b_g, sinks_a, w_o, ln1_g, ln1_b, w_up, conv_w, conv_b, w_down, ln2_g, ln2_b, loss_target, m_w_in, m_norm_a_g, m_norm_b_g, m_sinks_a, m_w_o, m_ln1_g, m_ln1_b, m_w_up, m_conv_w, m_conv_b, m_w_down, m_ln2_g, m_ln2_b, v_w_in, v_norm_a_g, v_norm_b_g, v_sinks_a, v_w_o, v_ln1_g, v_ln1_b, v_w_up, v_conv_w, v_conv_b, v_w_down, v_ln2_g, v_ln2_b):
    xi, yi, ci = _place()
    chip = (2 * xi + yi).astype(I32)
    core = ci.astype(I32)

    w_in_rows, m_w_in_rows, v_w_in_rows = w_in.T, m_w_in.T, v_w_in.T
    shards = (w_in_rows.astype(BF16), w_o.astype(BF16), w_up.astype(BF16), w_down.astype(BF16))
    w_in_t, conv_w4 = _gather_w_in(shards[0], conv_w)
    conv_w_f = conv_w4.transpose(1, 0, 2).reshape(3, 2 * FF)
    w_started, w_tok = _weights_start(shards[1:], conv_w4)
    slopes = jnp.asarray(SLOPES, F32) + w_tok[0, 0]

    halves_rows = [r // 2 for r in SHARD_ROWS]
    grads4, grads_b4, started = [None] * 4, [None] * 4, [None] * 4

    def on_grad(k, g, g_b):
        grads4[k] = g.reshape(N_CHIPS, 2, halves_rows[k], D)
        grads_b4[k] = g_b.reshape(N_CHIPS, 2, halves_rows[k], D)
        if k > 1:
            return None
        group = (1, 2, 3) if k == 1 else (0,)
        sts, tok = _grads_start([grads_b4[i] for i in group], f"grads_start_{k}")
        for i, st in zip(group, sts):
            started[i] = st
        return tok

    small_rows = 32
    small_started = []

    def on_small(parts):
        st, tok = _small_start(_pack({k: parts[k] for k in SMALL_ORDER}, small_rows))
        small_started.append(st)
        return tok

    gx = _local_step(
        x[0], loss_target[0], w_in_t, lambda k, after: _weights_wait(w_started[k - 1], after, f"weights_wait_{k}"),
        norm_a_g, norm_b_g, sinks_a, ln1_g, ln1_b, conv_w_f, conv_b, ln2_g, ln2_b, slopes, on_grad, on_small)

    tiles = (96, 128, 352, 176)
    core_chip = jnp.stack([core, chip])
    got = _grads_wait(started[1:], gx, "grads_wait_1")
    halves = [_sum_partials(grads4[k], got[k - 1], core_chip, f"sum_partials_{k}", tiles[k]) for k in (1, 2, 3)]
    g_w_o, g_w_up_rows, g_w_down = _swap_halves(halves, "swap_halves")
    g_w_up = g_w_up_rows.T
    delta, new_m, new_v = {}, {}, {}
    for k, g, tr in (("w_o", g_w_o, 128), ("w_up", g_w_up, 256), ("w_down", g_w_down, 176)):
        delta[k], new_m[k], new_v[k] = _adamw(dict(w_o=w_o, w_up=w_up, w_down=w_down)[k], g,
                                              dict(w_o=m_w_o, w_up=m_w_up, w_down=m_w_down)[k],
                                              dict(w_o=v_w_o, w_up=v_w_up, w_down=v_w_down)[k], f"adamw_{k}", tr)

    got = _grads_wait(started[:1], delta["w_up"], "grads_wait_0")
    half_in = _sum_partials(grads4[0], got[0], core_chip, "sum_partials_0", tiles[0])
    (g_w_in_rows,) = _swap_halves([half_in], "swap_halves_in")
    small_mine, small_land = _small_wait(small_started[0], g_w_in_rows)
    totals = _small_sum(small_mine, small_land, (4 * xi + 2 * yi + ci).astype(I32).reshape(1))
    tot = _unpack(totals, SMALL_ORDER, SMALL_SIZES)
    loss = tot["loss"][0]
    cols = 2 * FF // N_CHIPS
    g_conv_w = lax.dynamic_slice(tot["conv_w"].reshape(3, 2 * FF), (0, chip * cols), (3, cols))
    g_small = dict(norm_a_g=tot["norm_a_g"], norm_b_g=tot["norm_b_g"], sinks_a=tot["sinks_a"], ln1_g=tot["ln1_g"],
                   ln1_b=tot["ln1_b"], conv_w=g_conv_w, conv_b=tot["conv_b"], ln2_g=tot["ln2_g"], ln2_b=tot["ln2_b"])

    weights = dict(w_in=w_in, norm_a_g=norm_a_g, norm_b_g=norm_b_g, sinks_a=sinks_a, w_o=w_o, ln1_g=ln1_g, ln1_b=ln1_b,
                   w_up=w_up, conv_w=conv_w, conv_b=conv_b, w_down=w_down, ln2_g=ln2_g, ln2_b=ln2_b)
    ms = dict(w_in=m_w_in, norm_a_g=m_norm_a_g, norm_b_g=m_norm_b_g, sinks_a=m_sinks_a, w_o=m_w_o, ln1_g=m_ln1_g,
              ln1_b=m_ln1_b, w_up=m_w_up, conv_w=m_conv_w, conv_b=m_conv_b, w_down=m_w_down, ln2_g=m_ln2_g, ln2_b=m_ln2_b)
    vs = dict(w_in=v_w_in, norm_a_g=v_norm_a_g, norm_b_g=v_norm_b_g, sinks_a=v_sinks_a, w_o=v_w_o, ln1_g=v_ln1_g,
              ln1_b=v_ln1_b, w_up=v_w_up, conv_w=v_conv_w, conv_b=v_conv_b, w_down=v_w_down, ln2_g=v_ln2_g, ln2_b=v_ln2_b)
    order = list(weights)
    grad = dict(g_small, w_in=g_w_in_rows.T, w_o=g_w_o, w_up=g_w_up, w_down=g_w_down)

    delta["w_in"], new_m["w_in"], new_v["w_in"] = [
        a.T for a in _adamw(w_in_rows, g_w_in_rows, m_w_in_rows, v_w_in_rows, "adamw_w_in", 144)]
    small_names = [k for k in order if k not in delta]
    sizes = {k: weights[k].size for k in small_names}
    rows = 16
    packed = [_pack({k: src[k] for k in small_names}, rows) for src in (weights, grad, ms, vs)]
    for res, buf in zip((delta, new_m, new_v), _adamw(*packed, "adamw_small", rows)):
        for k, val in _unpack(buf, small_names, sizes).items():
            res[k] = val.reshape(weights[k].shape)

    return (loss, gx[None], *[grad[k] for k in order], *[delta[k] for k in order],
            *[new_m[k] for k in order], *[new_v[k] for k in order])
```

```python
import functools
import math

import jax
import jax.numpy as jnp
from jax import lax
from jax.experimental import pallas as pl
from jax.experimental.pallas import tpu as pltpu

F32, BF16, I32 = jnp.float32, jnp.bfloat16, jnp.int32

D = 1024
FF = 2816
HD = 64
NH = 8
WA, WB = 768, 1536
WIN = WA + WB
BLK = 128
ALPHA = 2.0 ** 0.25
LN_EPS, RMS_EPS = 1e-5, 1e-6
SCALE = 1.0 / math.sqrt(HD)
A_MAX_DIST, B_MAX_DIST = 127, 128
B_DILATIONS = (1, 4, 16)
SLOPES = tuple(2.0 ** (-(i + 1)) for i in range(NH))
SHARD_ROWS = (WIN // 4, D // 4, 2 * FF // 4, FF // 4)
N_CHIPS = 4
ADAM_LR, ADAM_B1, ADAM_B2, ADAM_EPS, ADAM_WD, ADAM_STEP = 0.001, 0.9, 0.999, 1e-08, 0.01, 10
MESH = pl.DeviceIdType.MESH
ANY = pl.BlockSpec(memory_space=pl.ANY)
SMEM = pl.BlockSpec(memory_space=pltpu.SMEM)
VMEM = pl.BlockSpec(memory_space=pltpu.VMEM)
HBM = pl.BlockSpec(memory_space=pltpu.HBM)
SEM = pl.BlockSpec(memory_space=pltpu.SEMAPHORE)
DATAFLOW = pltpu.SideEffectType.DATAFLOW_SIDE_EFFECTING


def _cp(sem, mb=48):
    return pltpu.CompilerParams(dimension_semantics=sem, vmem_limit_bytes=mb << 20)


def _nn(a, b):
    return lax.dot_general(a, b, (((1,), (0,)), ((), ())), preferred_element_type=F32)


def _nt(a, b):
    return lax.dot_general(a, b, (((1,), (1,)), ((), ())), preferred_element_type=F32)


def _tn(a, b):
    return lax.dot_general(a, b, (((0,), (0,)), ((), ())), preferred_element_type=F32)


def _resident(shape):
    n = len(shape)
    return pl.BlockSpec(shape, lambda *_: (0,) * n, pipeline_mode=pl.Buffered(1))


def _const(shape):
    n = len(shape)
    return pl.BlockSpec(shape, lambda *_: (0,) * n)


def _proj(x, w_t, name, tm=512):
    s = x.shape[0]
    n = w_t.shape[0]

    def body(x_ref, w_ref, o_ref, xb_ref):
        xb = x_ref[...].astype(BF16)
        xb_ref[...] = xb
        res = _nt(xb, w_ref[...])
        for g in range(n // 128):
            o_ref[g] = res[:, 128 * g:128 * (g + 1)]

    return pl.pallas_call(
        body, name=name, grid=(s // tm,),
        in_specs=[pl.BlockSpec((tm, D), lambda i: (i, 0)), _resident((n, D))],
        out_specs=[pl.BlockSpec((n // 128, tm, 128), lambda i: (0, i, 0)), pl.BlockSpec((tm, D), lambda i: (i, 0))],
        out_shape=[jax.ShapeDtypeStruct((n // 128, s, 128), F32), jax.ShapeDtypeStruct((s, D), BF16)],
        compiler_params=_cp(("parallel",)),
    )(x, w_t)


def _grad_w(lhs, rhs, name, tm, tk=2048, lhs_halves=False):
    s = rhs.shape[0]
    if lhs_halves:
        per_half = lhs.shape[2] // tm
        n = 2 * lhs.shape[2]
        lhs_spec = pl.BlockSpec((None, tk, tm), lambda i, k: (i // per_half, k, i % per_half))
    else:
        n = lhs.shape[1]
        lhs_spec = pl.BlockSpec((tk, tm), lambda i, k: (k, i))
    nk = s // tk

    def body(l_ref, r_ref, o_ref, ob_ref):
        k = pl.program_id(1)

        @pl.when(k == 0)
        def _():
            o_ref[...] = jnp.zeros_like(o_ref)

        o_ref[...] += _tn(l_ref[...], r_ref[...])

        @pl.when(k == nk - 1)
        def _():
            ob_ref[...] = o_ref[...].astype(BF16)

    return pl.pallas_call(
        body, name=name, grid=(n // tm, nk),
        in_specs=[lhs_spec, pl.BlockSpec((tk, D), lambda i, k: (k, 0))],
        out_specs=[pl.BlockSpec((tm, D), lambda i, k: (i, 0))] * 2,
        out_shape=[jax.ShapeDtypeStruct((n, D), F32), jax.ShapeDtypeStruct((n, D), BF16)],
        compiler_params=_cp(("parallel", "arbitrary")),
    )(lhs, rhs)


def _band_base(max_dist, dist_unit, first):
    row = lax.broadcasted_iota(I32, (BLK, 2 * BLK), 0)
    col = lax.broadcasted_iota(I32, (BLK, 2 * BLK), 1)
    dist = BLK + row - col
    ok = (dist >= 0) & (dist <= max_dist)
    if first:
        ok = ok & (col >= BLK)
    return jnp.where(ok, dist.astype(F32) * (-float(dist_unit)), -jnp.inf)


def _half_mask(shape, e):
    lane = lax.broadcasted_iota(I32, shape, 1)
    return (lane < HD) if e == 0 else (lane >= HD)


def _to_half(x, e, g):
    if g != e:
        x = pltpu.roll(x, HD, 1)
    return jnp.where(_half_mask(x.shape, g), x, 0.0)


def _stack_heads(scalars, tile):
    return jnp.concatenate([scalars[0] * tile, scalars[1] * tile], axis=0)


def _pair_fwd(q2, kb, vb, base, slopes, kv_heads, sinks):
    lo = _half_mask((BLK, 2 * HD), 0)
    if slopes is None:
        bias = base
    elif sinks is None:
        bias = _stack_heads(slopes, base)
    else:
        col0 = lax.broadcasted_iota(I32, base.shape, 1) == 0
        bias = jnp.concatenate([jnp.where(col0, sinks[e], slopes[e] * base) for e in (0, 1)], axis=0)
    qs = jnp.concatenate([_to_half(q2, e, kv_heads[e]) * SCALE for e in (0, 1)], axis=0).astype(BF16)
    s = _nt(qs, kb) + bias
    m = jnp.max(s, axis=1, keepdims=True)
    p = jnp.exp(s - m)
    l = jnp.sum(p, axis=1, keepdims=True)
    o = _nn(p.astype(BF16), vb) / l
    lse = m + jnp.log(l)
    halves = []
    for e in (0, 1):
        oh = o[e * BLK:(e + 1) * BLK]
        halves.append(pltpu.roll(oh, HD, 1) if kv_heads[e] != e else oh)
    o2 = jnp.where(lo, halves[0], halves[1])
    lse2 = jnp.where(lo, jnp.broadcast_to(lse[:BLK], (BLK, 2 * HD)), jnp.broadcast_to(lse[BLK:], (BLK, 2 * HD)))
    return o2, lse2


def _pair_bwd(q2, kb, vb, do2, o2, lse2, base, slopes, kv_heads, sinks):
    lo = _half_mask((BLK, 2 * HD), 0)
    prod = do2 * o2
    lses, deltas = [], []
    for e in (0, 1):
        hq = _half_mask((BLK, 2 * HD), e)
        lses.append(jnp.max(jnp.where(hq, lse2, -jnp.inf), axis=1, keepdims=True))
        deltas.append(jnp.sum(jnp.where(hq, prod, 0.0), axis=1, keepdims=True))
    lse = jnp.concatenate(lses, axis=0)
    delta = jnp.concatenate(deltas, axis=0)
    qs = jnp.concatenate([_to_half(q2, e, kv_heads[e]) * SCALE for e in (0, 1)], axis=0).astype(BF16)
    dos = jnp.concatenate([_to_half(do2, e, kv_heads[e]) for e in (0, 1)], axis=0).astype(BF16)
    p = jnp.exp(_nt(qs, kb) + (base if slopes is None else _stack_heads(slopes, base)) - lse)
    ds = (p * (_nt(dos, vb) - delta)).astype(BF16)
    dq = _nn(ds, kb) * SCALE
    halves = []
    for e in (0, 1):
        dqh = dq[e * BLK:(e + 1) * BLK]
        halves.append(pltpu.roll(dqh, HD, 1) if kv_heads[e] != e else dqh)
    dq2 = jnp.where(lo, halves[0], halves[1])
    dk2 = _tn(ds, qs)
    dv2 = _tn(p.astype(BF16), dos)
    dsinks = []
    if sinks is not None:
        for e in (0, 1):
            dsinks.append(jnp.sum(-jnp.exp(sinks[e] - lses[e]) * deltas[e], axis=0, keepdims=True))
    return dq2, dk2, dv2, dsinks


A_BLOCKS_PER_STEP = 2
A_BLOCKS_PER_STEP_BWD = 1


def _attn_a_fwd(proj, sinks):
    s = proj.shape[1]
    nq = A_BLOCKS_PER_STEP
    rows = BLK * nq
    steps = s // rows

    def body(sink_ref, q_ref, kp_ref, kc_ref, vp_ref, vc_ref, o_ref, lse_ref):
        n = pl.program_id(0)
        base_rest = _band_base(A_MAX_DIST, 1, False)
        base_0 = jnp.where(n > 0, base_rest, _band_base(A_MAX_DIST, 1, True))
        for i in range(nq):
            cur = pl.ds(i * BLK, BLK)
            k_prev = kc_ref[pl.ds((i - 1) * BLK, BLK), :] if i > 0 else kp_ref[...]
            v_prev = vc_ref[pl.ds((i - 1) * BLK, BLK), :] if i > 0 else vp_ref[...]
            first_key = lax.broadcasted_iota(I32, (2 * BLK, 128), 0) == 0
            kb = jnp.where(first_key, 0.0, jnp.concatenate([k_prev, kc_ref[cur, :]], axis=0)).astype(BF16)
            vb = jnp.where(first_key, 0.0, jnp.concatenate([v_prev, vc_ref[cur, :]], axis=0)).astype(BF16)
            for j in range(NH // 2):
                g = j // 2
                o2, lse2 = _pair_fwd(q_ref[j, cur, :], kb, vb, base_rest if i > 0 else base_0,
                                     (SLOPES[2 * j], SLOPES[2 * j + 1]), (g, g), (sink_ref[2 * j], sink_ref[2 * j + 1]))
                o_ref[j, cur, :] = o2
                lse_ref[j, cur, :] = lse2

    before = lambda n: jnp.maximum(n * nq - 1, 0)
    slab = lambda g: pl.BlockSpec((None, rows, 128), lambda n: (g, n, 0))
    edge = lambda g: pl.BlockSpec((None, BLK, 128), lambda n: (g, before(n), 0))
    quad = pl.BlockSpec((4, rows, 128), lambda n: (0, n, 0))
    return pl.pallas_call(
        body, name="attn_a_fwd", grid=(steps,),
        in_specs=[SMEM, quad, edge(4), slab(4), edge(5), slab(5)],
        out_specs=[quad, quad],
        out_shape=[jax.ShapeDtypeStruct((4, s, 128), F32)] * 2,
        compiler_params=_cp(("parallel",)),
    )(sinks, proj, proj, proj, proj, proj)


def _attn_a_bwd(proj, sinks, d_o, o, lse):
    s = proj.shape[1]
    nq = A_BLOCKS_PER_STEP_BWD
    rows = BLK * nq
    steps = s // rows

    def body(sink_ref, q_ref, kp_ref, kc_ref, vp_ref, vc_ref, do_ref, o_ref, lse_ref,
             dq_ref, dk_ref, dv_ref, dsink_ref, kcar, vcar):
        n = pl.program_id(0)

        @pl.when(n == 0)
        def _():
            kcar[...] = jnp.zeros_like(kcar)
            vcar[...] = jnp.zeros_like(vcar)
            dsink_ref[...] = jnp.zeros_like(dsink_ref)

        dk_ref[...] = kcar[...]
        dv_ref[...] = vcar[...]

        @pl.when(n < steps)
        def _():
            base_rest = _band_base(A_MAX_DIST, 1, False)
            base_0 = jnp.where(n > 0, base_rest, _band_base(A_MAX_DIST, 1, True))
            for i in range(nq):
                cur = pl.ds(i * BLK, BLK)
                k_prev = kc_ref[pl.ds((i - 1) * BLK, BLK), :] if i > 0 else kp_ref[...]
                v_prev = vc_ref[pl.ds((i - 1) * BLK, BLK), :] if i > 0 else vp_ref[...]
                kb = jnp.concatenate([k_prev, kc_ref[cur, :]], axis=0).astype(BF16)
                vb = jnp.concatenate([v_prev, vc_ref[cur, :]], axis=0).astype(BF16)
                dk_win = dv_win = None
                for j in range(NH // 2):
                    g = j // 2
                    dq2, dk2, dv2, dsk = _pair_bwd(q_ref[j, cur, :], kb, vb, do_ref[j, cur, :], o_ref[j, cur, :],
                                                   lse_ref[j, cur, :], base_rest if i > 0 else base_0,
                                                   (SLOPES[2 * j], SLOPES[2 * j + 1]), (g, g),
                                                   (sink_ref[2 * j], sink_ref[2 * j + 1]))
                    dq_ref[j, cur, :] = dq2
                    dk_win = dk2 if j == 0 else dk_win + dk2
                    dv_win = dv2 if j == 0 else dv_win + dv2
                    for e in (0, 1):
                        h = 2 * j + e
                        dsink_ref[h:h + 1, :] += jnp.broadcast_to(dsk[e], (1, 128))
                if i == 0:
                    last = pl.ds((nq - 1) * BLK, BLK)
                    dk_ref[last, :] += dk_win[:BLK]
                    dv_ref[last, :] += dv_win[:BLK]
                else:
                    kcar[pl.ds((i - 1) * BLK, BLK), :] += dk_win[:BLK]
                    vcar[pl.ds((i - 1) * BLK, BLK), :] += dv_win[:BLK]
                kcar[cur, :] = dk_win[BLK:]
                vcar[cur, :] = dv_win[BLK:]

    cur_step = lambda n: jnp.minimum(n, steps - 1)
    before = lambda n: jnp.maximum(cur_step(n) * nq - 1, 0)
    out_prev = lambda n: jnp.maximum(n - 1, 0)
    quad = pl.BlockSpec((4, rows, 128), lambda n: (0, cur_step(n), 0))
    slab = lambda g: pl.BlockSpec((None, rows, 128), lambda n: (g, cur_step(n), 0))
    edge = lambda g: pl.BlockSpec((None, BLK, 128), lambda n: (g, before(n), 0))
    return pl.pallas_call(
        body, name="attn_a_bwd", grid=(steps + 1,),
        in_specs=[SMEM, quad, edge(4), slab(4), edge(5), slab(5), quad, quad, quad],
        out_specs=[quad,
                   pl.BlockSpec((rows, 128), lambda n: (out_prev(n), 0)),
                   pl.BlockSpec((rows, 128), lambda n: (out_prev(n), 0)),
                   pl.BlockSpec((NH, 128), lambda n: (0, 0))],
        out_shape=[jax.ShapeDtypeStruct((4, s, 128), F32), jax.ShapeDtypeStruct((s, 128), F32),
                   jax.ShapeDtypeStruct((s, 128), F32), jax.ShapeDtypeStruct((NH, 128), F32)],
        scratch_shapes=[pltpu.VMEM((rows, 128), F32), pltpu.VMEM((rows, 128), F32)],
        compiler_params=_cp(("arbitrary",)),
    )(sinks, proj, proj, proj, proj, proj, d_o, o, lse)


def _stream(rho, i, r):
    start = i * BLK * r + rho
    return pl.ds(start, BLK, stride=r) if r > 1 else pl.ds(start, BLK)


def _for_streams(r, fn, side_by_side=4):
    if r <= side_by_side:
        for rho in range(r):
            fn(rho)
    else:
        def group(it, carry):
            for u in range(side_by_side):
                fn(side_by_side * it + u)
            return carry

        lax.fori_loop(0, r // side_by_side, group, 0)


B_BLOCKS_PER_STEP = {1: 8, 4: 2, 16: 1}
B_BLOCKS_PER_STEP_FWD = {1: 16, 4: 4, 16: 1}


def _attn_b_fwd(proj, slopes, r):
    s = proj.shape[1]
    nq = B_BLOCKS_PER_STEP_FWD[r]
    rows = BLK * r * nq
    steps = s // rows
    qc, kc, vc = WA // 128, WA // 128 + 4, WA // 128 + 8

    def body(slope_ref, q_ref, kp_ref, kc_ref, vp_ref, vc_ref, o_ref, lse_ref):
        j = pl.program_id(0)
        sb = pl.program_id(1)
        sl2 = (slope_ref[2 * j], slope_ref[2 * j + 1])
        bias_rest = _stack_heads(sl2, _band_base(B_MAX_DIST, r, False))
        bias_0 = jnp.where(sb > 0, bias_rest, _stack_heads(sl2, _band_base(B_MAX_DIST, r, True)))

        def stream(rho):
            for i in range(nq):
                cur = _stream(rho, i, r)
                k_prev = kc_ref[_stream(rho, i - 1, r), :] if i > 0 else kp_ref[_stream(rho, 0, r), :]
                v_prev = vc_ref[_stream(rho, i - 1, r), :] if i > 0 else vp_ref[_stream(rho, 0, r), :]
                kb = jnp.concatenate([k_prev, kc_ref[cur, :]], axis=0).astype(BF16)
                vb = jnp.concatenate([v_prev, vc_ref[cur, :]], axis=0).astype(BF16)
                o2, lse2 = _pair_fwd(q_ref[cur, :], kb, vb, bias_rest if i > 0 else bias_0, None, (0, 1), None)
                o_ref[cur, :] = o2
                lse_ref[cur, :] = lse2

        _for_streams(r, stream, side_by_side=16)

    before = lambda sb: jnp.maximum(sb * nq - 1, 0)
    return pl.pallas_call(
        body, name=f"attn_b_fwd_r{r}", grid=(NH // 2, steps),
        in_specs=[SMEM,
                  pl.BlockSpec((None, rows, 128), lambda j, sb: (qc + j, sb, 0)),
                  pl.BlockSpec((None, BLK * r, 128), lambda j, sb: (kc + j, before(sb), 0)),
                  pl.BlockSpec((None, rows, 128), lambda j, sb: (kc + j, sb, 0)),
                  pl.BlockSpec((None, BLK * r, 128), lambda j, sb: (vc + j, before(sb), 0)),
                  pl.BlockSpec((None, rows, 128), lambda j, sb: (vc + j, sb, 0))],
        out_specs=[pl.BlockSpec((None, rows, 128), lambda j, sb: (j, sb, 0))] * 2,
        out_shape=[jax.ShapeDtypeStruct((4, s, 128), F32)] * 2,
        compiler_params=_cp(("parallel", "parallel")),
    )(slopes, proj, proj, proj, proj, proj)


def _attn_b_bwd(proj, slopes, d_o, o, lse, r, so_far=None):
    s = proj.shape[1]
    nq = B_BLOCKS_PER_STEP[r]
    rows = BLK * r * nq
    steps = s // rows
    qc, kc, vc = WA // 128, WA // 128 + 4, WA // 128 + 8
    chained = so_far is not None

    def body(slope_ref, q_ref, kp_ref, kc_ref, vp_ref, vc_ref, do_ref, o_ref, lse_ref, *rest):
        if chained:
            pq_ref, pk_ref, pv_ref, dq_ref, dk_ref, dv_ref, kcar, vcar = rest
        else:
            dq_ref, dk_ref, dv_ref, kcar, vcar = rest
        j = pl.program_id(0)
        sb = pl.program_id(1)

        @pl.when(sb == 0)
        def _():
            kcar[...] = jnp.zeros_like(kcar)
            vcar[...] = jnp.zeros_like(vcar)

        if chained:
            dk_ref[...] = kcar[...] + pk_ref[...]
            dv_ref[...] = vcar[...] + pv_ref[...]
        else:
            dk_ref[...] = kcar[...]
            dv_ref[...] = vcar[...]

        @pl.when(sb < steps)
        def _():
            sl2 = (slope_ref[2 * j], slope_ref[2 * j + 1])
            bias_rest = _stack_heads(sl2, _band_base(B_MAX_DIST, r, False))
            bias_0 = jnp.where(sb > 0, bias_rest, _stack_heads(sl2, _band_base(B_MAX_DIST, r, True)))

            def stream(rho):
                for i in range(nq):
                    cur = _stream(rho, i, r)
                    k_prev = kc_ref[_stream(rho, i - 1, r), :] if i > 0 else kp_ref[_stream(rho, 0, r), :]
                    v_prev = vc_ref[_stream(rho, i - 1, r), :] if i > 0 else vp_ref[_stream(rho, 0, r), :]
                    kb = jnp.concatenate([k_prev, kc_ref[cur, :]], axis=0).astype(BF16)
                    vb = jnp.concatenate([v_prev, vc_ref[cur, :]], axis=0).astype(BF16)
                    dq2, dk2, dv2, _ = _pair_bwd(q_ref[cur, :], kb, vb, do_ref[cur, :], o_ref[cur, :], lse_ref[cur, :],
                                                 bias_rest if i > 0 else bias_0, None, (0, 1), None)
                    dq_ref[cur, :] = dq2 + pq_ref[cur, :] if chained else dq2
                    if i == 0:
                        last = _stream(rho, nq - 1, r)
                        dk_ref[last, :] += dk2[:BLK]
                        dv_ref[last, :] += dv2[:BLK]
                    else:
                        kcar[_stream(rho, i - 1, r), :] += dk2[:BLK]
                        vcar[_stream(rho, i - 1, r), :] += dv2[:BLK]
                    kcar[cur, :] = dk2[BLK:]
                    vcar[cur, :] = dv2[BLK:]

            _for_streams(r, stream, side_by_side=8)

    cur_step = lambda sb: jnp.minimum(sb, steps - 1)
    before = lambda sb: jnp.maximum(cur_step(sb) * nq - 1, 0)
    out_prev = lambda sb: jnp.maximum(sb - 1, 0)
    tile = lambda slab: pl.BlockSpec((None, rows, 128), lambda j, sb: (slab + j, cur_step(sb), 0))
    edge = lambda slab: pl.BlockSpec((None, BLK * r, 128), lambda j, sb: (slab + j, before(sb), 0))
    late = pl.BlockSpec((None, rows, 128), lambda j, sb: (j, out_prev(sb), 0))
    grads = [tile(0), late, late]
    return pl.pallas_call(
        body, name=f"attn_b_bwd_r{r}", grid=(NH // 2, steps + 1),
        in_specs=[SMEM, tile(qc), edge(kc), tile(kc), edge(vc), tile(vc), tile(0), tile(0), tile(0)]
        + (grads if chained else []),
        out_specs=grads,
        out_shape=[jax.ShapeDtypeStruct((4, s, 128), F32)] * 3,
        scratch_shapes=[pltpu.VMEM((rows, 128), F32), pltpu.VMEM((rows, 128), F32)],
        compiler_params=_cp(("parallel", "arbitrary")),
    )(slopes, proj, proj, proj, proj, proj, d_o, o, lse, *(so_far if chained else ()))


def _row(v):
    return v.reshape(1, -1)


def _layer_norm_stats(z):
    mu = jnp.mean(z, axis=-1, keepdims=True)
    zc = z - mu
    var = jnp.mean(zc * zc, axis=-1, keepdims=True)
    rstd = lax.rsqrt(var + LN_EPS)
    return zc * rstd, rstd


def _layer_norm_bwd(dh, zh, rstd, g):
    dzh = dh * g
    return rstd * (dzh - jnp.mean(dzh, axis=-1, keepdims=True) - zh * jnp.mean(dzh * zh, axis=-1, keepdims=True))


def _rms(o):
    return lax.rsqrt(jnp.mean(o * o, axis=-1, keepdims=True) + RMS_EPS)


def _mix_ln1(x, o_a, o_b, lse_b, norm_a_g, norm_b_g, w_o, ln1_g, ln1_b, tm=256):
    s = x.shape[0]

    def wide(ref):
        return jnp.concatenate([ref[j] for j in range(4)], axis=1)

    def body(x_ref, oa_ref, ob1, ob2, ob3, l1, l2, l3, ga_ref, gb_ref, wo_ref, g_ref, b_ref,
             obm_ref, lse_ref, cat_ref, z1_ref, h1_ref, h1b_ref):
        la, lb, lc = wide(l1), wide(l2), wide(l3)
        m = jnp.maximum(jnp.maximum(la, lb), lc)
        ea, eb, ec = jnp.exp(la - m), jnp.exp(lb - m), jnp.exp(lc - m)
        den = ea + eb + ec
        obm = (ea / den) * wide(ob1) + (eb / den) * wide(ob2) + (ec / den) * wide(ob3)
        lse = m + jnp.log(den)
        for j in range(4):
            obm_ref[j] = obm[:, 128 * j:128 * (j + 1)]
            lse_ref[j] = lse[:, 128 * j:128 * (j + 1)]
        oa = wide(oa_ref)
        na = oa * _rms(oa) * ga_ref[...]
        nb_ = obm * _rms(obm) * gb_ref[...]
        cat = jnp.concatenate([na, nb_], axis=1).astype(BF16)
        cat_ref[...] = cat
        z1 = ALPHA * x_ref[...] + _nn(cat, wo_ref[...])
        z1_ref[...] = z1
        zh, _ = _layer_norm_stats(z1)
        h1 = zh * g_ref[...] + b_ref[...]
        h1_ref[...] = h1
        h1b_ref[...] = h1.astype(BF16)

    t512 = pl.BlockSpec((4, tm, 128), lambda i: (0, i, 0))
    td = pl.BlockSpec((tm, D), lambda i: (i, 0))
    return pl.pallas_call(
        body, name="mix_ln1", grid=(s // tm,),
        in_specs=[td] + [t512] * 7 + [_const((1, 512))] * 2 + [_resident((D, D))] + [_const((1, D))] * 2,
        out_specs=[t512, t512, td, td, td, td],
        out_shape=[jax.ShapeDtypeStruct((4, s, 128), F32), jax.ShapeDtypeStruct((4, s, 128), F32),
                   jax.ShapeDtypeStruct((s, D), BF16), jax.ShapeDtypeStruct((s, D), F32),
                   jax.ShapeDtypeStruct((s, D), F32), jax.ShapeDtypeStruct((s, D), BF16)],
        compiler_params=_cp(("parallel",)),
    )(x, o_a, *o_b, *lse_b, _row(norm_a_g), _row(norm_b_g), w_o, _row(ln1_g), _row(ln1_b))


def _gelu_and_grad(x):
    c = math.sqrt(2.0 / math.pi)
    x2 = x * x
    s = 0.5 * jnp.tanh(x * ((c * 0.044715) * x2 + c)) + 0.5
    dg = s + (x * ((6.0 * c * 0.044715) * x2 + 2.0 * c)) * (s - s * s)
    return x * s, dg


def _shifted(u, edge, row, down):
    groups = [u[8 * i:8 * i + 8] for i in range(u.shape[0] // 8)]
    others = [edge] + groups[:-1] if down else groups[1:] + [edge]
    moved = []
    for k in (1, 2):
        crossing = row >= 8 - k if down else row < k
        moved.append(jnp.concatenate([pltpu.roll(jnp.where(crossing, o, g), k if down else 8 - k, 0)
                                      for o, g in zip(others, groups)], axis=0))
    return moved


def _up_proj(h1b, w_up, tm=512):
    s = h1b.shape[0]

    def body(h_ref, w_ref, o_ref):
        h = h_ref[...]
        for half in (0, 1):
            o_ref[half] = _nn(h, w_ref[:, half * FF:(half + 1) * FF]).astype(BF16)

    return pl.pallas_call(
        body, name="up_proj", grid=(s // tm,),
        in_specs=[pl.BlockSpec((tm, D), lambda i: (i, 0)), _resident((D, 2 * FF))],
        out_specs=pl.BlockSpec((2, tm, FF), lambda i: (0, i, 0)),
        out_shape=jax.ShapeDtypeStruct((2, s, FF), BF16),
        compiler_params=_cp(("parallel",)),
    )(h1b, w_up)


def _conv_gelu(up, cwb, tm=256, tn=FF // 2, chunk_rows=16):
    s = up.shape[1]
    n_c = tm // chunk_rows

    def body(up_ref, c_ref, a_ref, g_ref, a1_ref, carry):
        @pl.when(pl.program_id(1) == 0)
        def _():
            carry[...] = jnp.zeros_like(carry)

        row = lax.broadcasted_iota(jnp.int32, (8, tn), 0)
        edge = [carry[0], carry[1]]
        for c in range(n_c):
            rows = pl.ds(c * chunk_rows, chunk_rows)
            u = []
            for half in (0, 1):
                x = up_ref[half, rows, :].astype(F32)
                r1, r2 = _shifted(x, edge[half], row, True)
                u.append(r2 * c_ref[0, half:half + 1, :] + r1 * c_ref[1, half:half + 1, :]
                         + x * c_ref[2, half:half + 1, :] + c_ref[3, half:half + 1, :])
                edge[half] = x[chunk_rows - 8:]
            g, dg = _gelu_and_grad(u[0])
            a_ref[rows, :] = (g * u[1]).astype(BF16)
            g_ref[rows, :] = g.astype(BF16)
            a1_ref[rows, :] = (u[1] * dg).astype(BF16)
        for half in (0, 1):
            carry[half] = edge[half]

    pair = pl.BlockSpec((2, tm, tn), lambda j, i: (0, i, j))
    tile = pl.BlockSpec((tm, tn), lambda j, i: (i, j))
    return pl.pallas_call(
        body, name="conv_gelu", grid=(FF // tn, s // tm),
        in_specs=[pair, pl.BlockSpec((4, 2, tn), lambda j, i: (0, 0, j))],
        out_specs=[tile, tile, tile],
        out_shape=[jax.ShapeDtypeStruct((s, FF), BF16)] * 3,
        scratch_shapes=[pltpu.VMEM((2, 8, tn), F32)],
        compiler_params=_cp(("parallel", "arbitrary")),
    )(up, cwb)


def _down_ln2_loss(a, w_down, h1, target, ln2_g, ln2_b, tm=512):
    s = a.shape[0]

    def body(a_ref, w_ref, h_ref, t_ref, g_ref, b_ref, dz_ref, dzb_ref, st_ref):
        @pl.when(pl.program_id(0) == 0)
        def _():
            st_ref[...] = jnp.zeros_like(st_ref)

        z2 = ALPHA * h_ref[...] + _nn(a_ref[...], w_ref[...])
        zh, rstd = _layer_norm_stats(z2)
        diff = zh * g_ref[...] + b_ref[...] - t_ref[...]
        part = 0.5 * jnp.sum(jnp.mean(diff * diff, axis=-1, keepdims=True), axis=0, keepdims=True)
        dy = diff * (1.0 / D)
        st_ref[0:1, :] += jnp.sum(dy * zh, axis=0, keepdims=True)
        st_ref[1:2, :] += jnp.sum(dy, axis=0, keepdims=True)
        st_ref[2:3, :] += jnp.broadcast_to(part, (1, D))
        dz = _layer_norm_bwd(dy, zh, rstd, g_ref[...])
        dz_ref[...] = dz
        dzb_ref[...] = dz.astype(BF16)

    td = pl.BlockSpec((tm, D), lambda i: (i, 0))
    return pl.pallas_call(
        body, name="down_ln2_loss", grid=(s // tm,),
        in_specs=[pl.BlockSpec((tm, FF), lambda i: (i, 0)), _resident((FF, D)), td, td, _const((1, D)), _const((1, D))],
        out_specs=[td, td, _const((8, D))],
        out_shape=[jax.ShapeDtypeStruct((s, D), F32), jax.ShapeDtypeStruct((s, D), BF16),
                   jax.ShapeDtypeStruct((8, D), F32)],
        compiler_params=_cp(("arbitrary",)),
    )(a, w_down, h1, target, _row(ln2_g), _row(ln2_b))


def _d_act(dz2b, w_down, tm=512):
    s = dz2b.shape[0]

    def body(dz_ref, w_ref, o_ref):
        o_ref[...] = _nt(dz_ref[...], w_ref[...]).astype(BF16)

    return pl.pallas_call(
        body, name="d_act", grid=(s // tm,),
        in_specs=[pl.BlockSpec((tm, D), lambda i: (i, 0)), _resident((FF, D))],
        out_specs=pl.BlockSpec((tm, FF), lambda i: (i, 0)),
        out_shape=jax.ShapeDtypeStruct((s, FF), BF16),
        compiler_params=_cp(("parallel",)),
    )(dz2b, w_down)


def _conv_gelu_bwd(da, up, g, a1, cwb, tm=256, tn=FF // 2, chunk_rows=16):
    s = da.shape[0]
    n_i = s // tm
    n_c = tm // chunk_rows

    def body(da_ref, up_ref, g_ref, a1_ref, c_ref, dup_ref, dc_ref, carry):
        @pl.when(pl.program_id(1) == 0)
        def _():
            carry[...] = jnp.zeros_like(carry)
            dc_ref[...] = jnp.zeros_like(dc_ref)

        def fold(v):
            return jnp.sum(v.reshape(chunk_rows // 8, 8, v.shape[1]), axis=0)

        def chunk(cc, state):
            after, sums = state
            rows = pl.ds((n_c - 1 - cc) * chunk_rows, chunk_rows)
            da_c = da_ref[rows, :].astype(F32)
            dus = (da_c * a1_ref[rows, :].astype(F32), da_c * g_ref[rows, :].astype(F32))
            head, new_sums = [], []
            for half in (0, 1):
                du = dus[half]
                up = up_ref[half, rows, :].astype(F32)
                l1, l2 = _shifted(du, after[half], row, False)
                dup = (du * c_ref[2, half:half + 1, :] + l1 * c_ref[1, half:half + 1, :]
                       + l2 * c_ref[0, half:half + 1, :])
                dup_ref[half, rows, :] = dup.astype(BF16)
                parts = (fold(l2 * up), fold(l1 * up), fold(du * up), fold(du))
                new_sums.append(parts if sums is None else tuple(a + b for a, b in zip(sums[half], parts)))
                head.append(du[:8])
            return tuple(head), new_sums

        row = lax.broadcasted_iota(jnp.int32, (8, tn), 0)
        state = ((carry[0], carry[1]), None)
        for cc in range(n_c):
            state = chunk(cc, state)
        head, sums = state
        for half in (0, 1):
            carry[half] = head[half]
            for k in range(4):
                dc_ref[k, half:half + 1, :] += jnp.sum(sums[half][k], axis=0, keepdims=True)

    rev = lambda ii: n_i - 1 - ii
    tile = pl.BlockSpec((tm, tn), lambda j, ii: (rev(ii), j))
    pair = pl.BlockSpec((2, tm, tn), lambda j, ii: (0, rev(ii), j))
    per_col = pl.BlockSpec((4, 2, tn), lambda j, ii: (0, 0, j))
    return pl.pallas_call(
        body, name="conv_gelu_bwd", grid=(FF // tn, n_i),
        in_specs=[tile, pair, tile, tile, per_col],
        out_specs=[pair, per_col],
        out_shape=[jax.ShapeDtypeStruct((2, s, FF), BF16), jax.ShapeDtypeStruct((4, 2, FF), F32)],
        scratch_shapes=[pltpu.VMEM((2, 8, tn), F32)],
        compiler_params=_cp(("parallel", "arbitrary")),
    )(da, up, g, a1, cwb)


def _dh1_ln1_bwd(dz2, dup, w_up, z1, ln1_g, tm=512):
    s = dz2.shape[0]

    def body(dz2_ref, dup_ref, w_ref, z1_ref, g_ref, dz1_ref, dz1b_ref, st_ref):
        @pl.when(pl.program_id(0) == 0)
        def _():
            st_ref[...] = jnp.zeros_like(st_ref)

        dh = ALPHA * dz2_ref[...] + _nt(dup_ref[0], w_ref[:, :FF]) + _nt(dup_ref[1], w_ref[:, FF:])
        zh, rstd = _layer_norm_stats(z1_ref[...])
        st_ref[0:1, :] += jnp.sum(dh * zh, axis=0, keepdims=True)
        st_ref[1:2, :] += jnp.sum(dh, axis=0, keepdims=True)
        dz = _layer_norm_bwd(dh, zh, rstd, g_ref[...])
        dz1_ref[...] = dz
        dz1b_ref[...] = dz.astype(BF16)

    td = pl.BlockSpec((tm, D), lambda i: (i, 0))
    return pl.pallas_call(
        body, name="dh1_ln1_bwd", grid=(s // tm,),
        in_specs=[td, pl.BlockSpec((2, tm, FF), lambda i: (0, i, 0)), _resident((D, 2 * FF)), td, _const((1, D))],
        out_specs=[td, td, _const((8, D))],
        out_shape=[jax.ShapeDtypeStruct((s, D), F32), jax.ShapeDtypeStruct((s, D), BF16),
                   jax.ShapeDtypeStruct((8, D), F32)],
        compiler_params=_cp(("arbitrary",), 58),
    )(dz2, dup, w_up, z1, _row(ln1_g))


def _dcat_rms_bwd(dz1b, w_o, o_a, o_b, norm_a_g, norm_b_g, tm=512):
    s = dz1b.shape[0]

    def body(dz_ref, w_ref, oa_ref, ob_ref, ga_ref, gb_ref, da_ref, db_ref, st_ref):
        @pl.when(pl.program_id(0) == 0)
        def _():
            st_ref[...] = jnp.zeros_like(st_ref)

        dcat = _nt(dz_ref[...], w_ref[...])
        for k, (o_ref, g_ref, d_ref) in enumerate(((oa_ref, ga_ref, da_ref), (ob_ref, gb_ref, db_ref))):
            o = jnp.concatenate([o_ref[j] for j in range(4)], axis=1)
            dn = dcat[:, 512 * k:512 * (k + 1)]
            rr = _rms(o)
            oh = o * rr
            st_ref[k:k + 1, :] += jnp.sum(dn * oh, axis=0, keepdims=True)
            doh = dn * g_ref[...]
            d_o = rr * (doh - oh * jnp.mean(doh * oh, axis=-1, keepdims=True))
            for j in range(4):
                d_ref[j] = d_o[:, 128 * j:128 * (j + 1)]

    t512 = pl.BlockSpec((4, tm, 128), lambda i: (0, i, 0))
    return pl.pallas_call(
        body, name="dcat_rms_bwd", grid=(s // tm,),
        in_specs=[pl.BlockSpec((tm, D), lambda i: (i, 0)), _resident((D, D)), t512, t512,
                  _const((1, 512)), _const((1, 512))],
        out_specs=[t512, t512, _const((8, 512))],
        out_shape=[jax.ShapeDtypeStruct((4, s, 128), F32), jax.ShapeDtypeStruct((4, s, 128), F32),
                   jax.ShapeDtypeStruct((8, 512), F32)],
        compiler_params=_cp(("arbitrary",)),
    )(dz1b, w_o, o_a, o_b, _row(norm_a_g), _row(norm_b_g))


def _dproj_combine(dqa, dka, dva, dqkv_b, tm=256):
    s = dka.shape[0]

    def body(qa, ka, va, qb, kb, vb, o_ref):
        for j in range(4):
            o_ref[:, 128 * j:128 * (j + 1)] = qa[j].astype(BF16)
            o_ref[:, 768 + 128 * j:768 + 128 * (j + 1)] = qb[j].astype(BF16)
            o_ref[:, 1280 + 128 * j:1280 + 128 * (j + 1)] = kb[j].astype(BF16)
            o_ref[:, 1792 + 128 * j:1792 + 128 * (j + 1)] = vb[j].astype(BF16)
        o_ref[:, 512:640] = ka[...].astype(BF16)
        o_ref[:, 640:768] = va[...].astype(BF16)

    t512 = pl.BlockSpec((4, tm, 128), lambda i: (0, i, 0))
    t128 = pl.BlockSpec((tm, 128), lambda i: (i, 0))
    return pl.pallas_call(
        body, name="dproj_combine", grid=(s // tm,),
        in_specs=[t512, t128, t128] + [t512] * 3,
        out_specs=pl.BlockSpec((tm, WIN), lambda i: (i, 0)),
        out_shape=jax.ShapeDtypeStruct((s, WIN), BF16),
        compiler_params=_cp(("parallel",)),
    )(dqa, dka, dva, *dqkv_b)


def _grad_x(dz1, dproj, w_in_t, zero, tm=512):
    s = dz1.shape[0]

    def body(dz_ref, dp_ref, w_ref, z_ref, o_ref):
        o_ref[...] = ALPHA * dz_ref[...] + _nn(dp_ref[...], w_ref[...]) + z_ref[0:1, 0:1]

    td = pl.BlockSpec((tm, D), lambda i: (i, 0))
    return pl.pallas_call(
        body, name="grad_x", grid=(s // tm,),
        in_specs=[td, pl.BlockSpec((tm, WIN), lambda i: (i, 0)), _resident((WIN, D)), _const((8, 128))],
        out_specs=td, out_shape=jax.ShapeDtypeStruct((s, D), F32),
        compiler_params=_cp(("parallel",)),
    )(dz1, dproj, w_in_t, zero)


def _place():
    return lax.axis_index("x"), lax.axis_index("y"), lax.axis_index("c")


def _other_chips(x, y):
    return [(1 - x, y), (x, 1 - y), (1 - x, 1 - y)]


def _hbm(a):
    return pltpu.with_memory_space_constraint(a, pltpu.HBM)


def _gather_w_in(shard, conv_w):
    rows_k = shard.shape[0]
    half = rows_k // 2

    def body(src, conv_src, out, conv_out, send_sems, recv_sems):
        x, y, c = _place()
        b = 2 * x + y
        sibling = (x, y, 1 - c)
        chips = _other_chips(x, y)

        def copy(idx, chip_b, core, to, first_hop=False):
            rows = out.at[pl.ds(pl.multiple_of(chip_b * rows_k + core * half, 16), half)]
            s_ref = src.at[pl.ds(pl.multiple_of(core * half, 16), half)] if first_hop else rows
            return pltpu.make_async_remote_copy(src_ref=s_ref, dst_ref=rows, send_sem=send_sems.at[idx],
                                                recv_sem=recv_sems.at[idx], device_id=to, device_id_type=MESH)

        def own_copy():
            return pltpu.make_async_remote_copy(
                src_ref=src, dst_ref=out.at[pl.ds(pl.multiple_of(b * rows_k, 16), rows_k)], send_sem=send_sems.at[6],
                recv_sem=recv_sems.at[6], device_id=sibling, device_id_type=MESH)

        def conv_copy(idx, chip_b, to):
            return pltpu.make_async_remote_copy(src_ref=conv_src, dst_ref=conv_out.at[chip_b],
                                                send_sem=send_sems.at[7 + idx], recv_sem=recv_sems.at[7 + idx],
                                                device_id=to, device_id_type=MESH)

        started = [own_copy(), conv_copy(3, b, sibling)]
        for jn, chip in enumerate(chips):
            started += [copy(jn, b, c, (chip[0], chip[1], c), first_hop=True), conv_copy(jn, b, (chip[0], chip[1], c))]
        for cp in started:
            cp.start()
        for jn, chip in enumerate(chips):
            cb = 2 * chip[0] + chip[1]
            copy(jn, cb, c, (chip[0], chip[1], c)).wait_recv()
            cp = copy(3 + jn, cb, c, sibling)
            cp.start()
            started.append(cp)
        for jn, chip in enumerate(chips):
            cb = 2 * chip[0] + chip[1]
            copy(3 + jn, cb, 1 - c, sibling).wait_recv()
            conv_copy(jn, cb, (chip[0], chip[1], c)).wait_recv()
        own_copy().wait_recv()
        conv_copy(3, b, sibling).wait_recv()
        for cp in started:
            cp.wait_send()

    return pl.pallas_call(
        body, name="gather_w_in",
        in_specs=[ANY, ANY], out_specs=[ANY, ANY],
        out_shape=[jax.ShapeDtypeStruct((N_CHIPS * rows_k, D), BF16), jax.ShapeDtypeStruct((N_CHIPS,) + conv_w.shape, F32)],
        scratch_shapes=[pltpu.SemaphoreType.DMA((11,)), pltpu.SemaphoreType.DMA((11,))],
        compiler_params=pltpu.CompilerParams(has_side_effects=True),
    )(shard, conv_w)


def _weight_copies(shard, land, send_sems, recv_sems, arrivals):
    x, y, c = _place()
    n_rows, n_cols = shard.shape
    peers = [(px, py, c) for px, py in _other_chips(x, y)] + [(x, y, 1 - c)]
    cps = []
    for jn, peer in enumerate(peers):
        at = 2 * peer[0] + peer[1] if arrivals else 2 * x + y
        if land.shape[1] == n_cols:
            dst = land.at[pl.ds(pl.multiple_of(at * n_rows, 16), n_rows)]
        else:
            dst = land.at[:, pl.ds(pl.multiple_of(at * n_cols, 128), n_cols)]
        cps.append(pltpu.make_async_remote_copy(src_ref=shard, dst_ref=dst, send_sem=send_sems.at[jn],
                                                recv_sem=recv_sems.at[jn], device_id=peer, device_id_type=MESH))
    return cps


def _weights_start(shards, after):
    n = len(shards)
    lands = [lax.empty((N_CHIPS * sh.shape[0], D) if sh.shape[1] == D else (D, N_CHIPS * sh.shape[1]), BF16)
             for sh in shards]

    def body(*refs):
        src, land = refs[:n], refs[n:2 * n]
        send_sems, recv_sems = refs[2 * n + 1:3 * n + 1], refs[3 * n + 1:4 * n + 1]
        for k in range(n):
            for send in _weight_copies(src[k], land[k], send_sems[k], recv_sems[k], False):
                send.start()
        refs[-1][...] = jnp.zeros_like(refs[-1])

    res = pl.pallas_call(
        body, name="weights_start",
        in_specs=[HBM] * (2 * n) + [ANY], out_specs=[SEM] * (2 * n) + [HBM] * (2 * n) + [VMEM],
        out_shape=[pltpu.SemaphoreType.DMA((4,))] * (2 * n)
        + [pltpu.HBM(a.shape, a.dtype) for a in (*shards, *lands)] + [jax.ShapeDtypeStruct((8, 128), F32)],
        input_output_aliases={i: i + 2 * n for i in range(2 * n)},
        compiler_params=pltpu.CompilerParams(has_side_effects=DATAFLOW),
    )(*[_hbm(a) for a in (*shards, *lands)], after)
    return [(res[k], res[n + k], res[2 * n + k], res[3 * n + k]) for k in range(n)], res[-1]


def _weights_wait(started, after, name):
    send_sems, recv_sems, shard, land = started

    def body(s_ref, l_ref, send_ref, recv_ref, after_ref, s_out, l_out):
        for cp in _weight_copies(s_ref, l_ref, send_ref, recv_ref, True):
            cp.wait_send()
            cp.wait_recv()

    return pl.pallas_call(
        body, name=name,
        in_specs=[HBM, HBM, SEM, SEM, ANY], out_specs=[HBM, HBM],
        out_shape=[pltpu.HBM(shard.shape, shard.dtype), pltpu.HBM(land.shape, land.dtype)],
        input_output_aliases={0: 0, 1: 1},
        compiler_params=pltpu.CompilerParams(has_side_effects=DATAFLOW),
    )(shard, land, send_sems, recv_sems, after)[1]


def _grad_copies(g_ref, land_ref, send_sems, recv_sems):
    x, y, c = _place()
    cps = []
    for d in range(1, 8):
        px, py, pc = x ^ (d >> 2), y ^ ((d >> 1) & 1), c ^ (d & 1)
        cps.append(pltpu.make_async_remote_copy(
            src_ref=g_ref.at[2 * px + py, pc], dst_ref=land_ref.at[d - 1], send_sem=send_sems.at[d - 1],
            recv_sem=recv_sems.at[d - 1], device_id=(px, py, pc), device_id_type=MESH))
    return cps


def _grads_start(grads_b, name):
    n = len(grads_b)
    lands = [lax.empty((7, g.shape[2], D), BF16) for g in grads_b]

    def body(*refs):
        g, land = refs[:n], refs[n:2 * n]
        send_sems, recv_sems = refs[2 * n:3 * n], refs[3 * n:4 * n]
        for k in range(n):
            for cp in _grad_copies(g[k], land[k], send_sems[k], recv_sems[k]):
                cp.start()
        refs[-1][...] = jnp.zeros_like(refs[-1])

    res = pl.pallas_call(
        body, name=name,
        in_specs=[HBM] * (2 * n), out_specs=[SEM] * (2 * n) + [HBM] * (2 * n) + [VMEM],
        out_shape=[pltpu.SemaphoreType.DMA((7,))] * (2 * n)
        + [pltpu.HBM(a.shape, a.dtype) for a in (*grads_b, *lands)] + [jax.ShapeDtypeStruct((8, 128), F32)],
        input_output_aliases={i: i + 2 * n for i in range(2 * n)},
        compiler_params=pltpu.CompilerParams(has_side_effects=DATAFLOW),
    )(*[_hbm(a) for a in (*grads_b, *lands)])
    return [(res[k], res[n + k], res[2 * n + k], res[3 * n + k]) for k in range(n)], res[-1]


def _grads_wait(started, after, name):
    n = len(started)

    def body(*refs):
        g, land = refs[:n], refs[n:2 * n]
        send_sems, recv_sems = refs[2 * n:3 * n], refs[3 * n:4 * n]
        for k in range(n):
            for cp in _grad_copies(g[k], land[k], send_sems[k], recv_sems[k]):
                cp.wait_send()
                cp.wait_recv()

    gs = [st[2] for st in started]
    lands = [st[3] for st in started]
    res = pl.pallas_call(
        body, name=name,
        in_specs=[HBM] * (2 * n) + [SEM] * (2 * n) + [ANY], out_specs=[HBM] * (2 * n),
        out_shape=[pltpu.HBM(a.shape, a.dtype) for a in (*gs, *lands)],
        input_output_aliases={i: i for i in range(2 * n)},
        compiler_params=pltpu.CompilerParams(has_side_effects=DATAFLOW),
    )(*gs, *lands, *[st[0] for st in started], *[st[1] for st in started], after)
    return res[n:]


def _sum_partials(grad4, got, cb, name, tr):
    h = grad4.shape[2]
    per_half = h // tr

    def body(cb_ref, g_ref, o_ref, out_ref):
        acc = g_ref[...]
        for j in range(7):
            acc = acc + o_ref[j].astype(F32)
        out_ref[...] = acc

    return pl.pallas_call(
        body, name=name,
        grid_spec=pltpu.PrefetchScalarGridSpec(
            num_scalar_prefetch=1, grid=(per_half,),
            in_specs=[pl.BlockSpec((None, None, tr, D), lambda i, cb_ref: (cb_ref[1], cb_ref[0], i, 0)),
                      pl.BlockSpec((7, tr, D), lambda i, cb_ref: (0, i, 0))],
            out_specs=pl.BlockSpec((tr, D), lambda i, cb_ref: (cb_ref[0] * per_half + i, 0))),
        out_shape=jax.ShapeDtypeStruct((2 * h, D), F32),
        compiler_params=_cp(("arbitrary",)),
    )(cb, grad4, got)


def _swap_halves(shards, name):
    n = len(shards)

    def body(*refs):
        out, send_sems, recv_sems = refs[n:2 * n], refs[2 * n], refs[2 * n + 1]
        x, y, c = _place()
        cps = []
        for k in range(n):
            h = shards[k].shape[0] // 2
            mine = out[k].at[pl.ds(pl.multiple_of(c * h, 8), h)]
            cp = pltpu.make_async_remote_copy(src_ref=mine, dst_ref=mine, send_sem=send_sems.at[k],
                                              recv_sem=recv_sems.at[k], device_id=(x, y, 1 - c), device_id_type=MESH)
            cp.start()
            cps.append(cp)
        for cp in cps:
            cp.wait()

    return pl.pallas_call(
        body, name=name,
        in_specs=[ANY] * n, out_specs=[ANY] * n,
        out_shape=[jax.ShapeDtypeStruct(sh.shape, F32) for sh in shards],
        input_output_aliases={k: k for k in range(n)},
        scratch_shapes=[pltpu.SemaphoreType.DMA((n,)), pltpu.SemaphoreType.DMA((n,))],
        compiler_params=pltpu.CompilerParams(has_side_effects=True),
    )(*shards)


def _small_copies(small_ref, land_ref, send_sems, recv_sems):
    x, y, c = _place()
    me = 4 * x + 2 * y + c
    cps = []
    for d in range(1, 8):
        px, py, pc = x ^ (d >> 2), y ^ ((d >> 1) & 1), c ^ (d & 1)
        cps.append(pltpu.make_async_remote_copy(
            src_ref=small_ref, dst_ref=land_ref.at[me], send_sem=send_sems.at[d - 1], recv_sem=recv_sems.at[d - 1],
            device_id=(px, py, pc), device_id_type=MESH))
    return cps


def _small_start(small):
    land = lax.empty((8,) + small.shape, F32)

    def body(s_ref, l_ref, send_sems, recv_sems, s_thru, l_thru, token):
        for cp in _small_copies(s_ref, l_ref, send_sems, recv_sems):
            cp.start()
        token[...] = jnp.zeros_like(token)

    res = pl.pallas_call(
        body, name="small_start",
        in_specs=[HBM, HBM], out_specs=[SEM, SEM, HBM, HBM, VMEM],
        out_shape=[pltpu.SemaphoreType.DMA((7,)), pltpu.SemaphoreType.DMA((7,)), pltpu.HBM(small.shape, F32),
                   pltpu.HBM(land.shape, F32), jax.ShapeDtypeStruct((8, 128), F32)],
        input_output_aliases={0: 2, 1: 3},
        compiler_params=pltpu.CompilerParams(has_side_effects=DATAFLOW),
    )(_hbm(small), _hbm(land))
    return res[:4], res[4]


def _small_wait(started, after):
    send_sems, recv_sems, small, land = started

    def body(s_ref, l_ref, send_ref, recv_ref, after_ref, s_out, l_out):
        for cp in _small_copies(s_ref, l_ref, send_ref, recv_ref):
            cp.wait_send()
            cp.wait_recv()

    return pl.pallas_call(
        body, name="small_wait",
        in_specs=[HBM, HBM, SEM, SEM, ANY], out_specs=[HBM, HBM],
        out_shape=[pltpu.HBM(small.shape, F32), pltpu.HBM(land.shape, F32)],
        input_output_aliases={0: 0, 1: 1},
        compiler_params=pltpu.CompilerParams(has_side_effects=DATAFLOW),
    )(small, land, send_sems, recv_sems, after)


def _small_sum(small, land, me):
    rows = small.shape[0]

    def body(me_ref, s_ref, l_ref, o_ref):
        acc = None
        for k in range(8):
            term = jnp.where(me_ref[0] == k, s_ref[...], l_ref[k])
            acc = term if k == 0 else acc + term
        o_ref[...] = acc

    return pl.pallas_call(
        body, name="small_sum",
        in_specs=[SMEM, VMEM, VMEM], out_specs=VMEM,
        out_shape=jax.ShapeDtypeStruct((rows, D), F32),
    )(me, small, land)


def _adamw(w, g, m, v, name, tr):
    rows, cols = w.shape

    def body(w_ref, g_ref, m_ref, v_ref, d_ref, nm_ref, nv_ref):
        g_ = g_ref[...]
        nm = ADAM_B1 * m_ref[...] + (1.0 - ADAM_B1) * g_
        nv = ADAM_B2 * v_ref[...] + (1.0 - ADAM_B2) * (g_ * g_)
        m_hat = nm / (1.0 - ADAM_B1 ** ADAM_STEP)
        v_hat = nv / (1.0 - ADAM_B2 ** ADAM_STEP)
        d_ref[...] = -ADAM_LR * (m_hat / (jnp.sqrt(v_hat) + ADAM_EPS) + ADAM_WD * w_ref[...])
        nm_ref[...] = nm
        nv_ref[...] = nv

    spec = pl.BlockSpec((tr, cols), lambda i: (i, 0))
    return pl.pallas_call(
        body, name=name, grid=(rows // tr,),
        in_specs=[spec] * 4, out_specs=[spec] * 3,
        out_shape=[jax.ShapeDtypeStruct((rows, cols), F32)] * 3,
        compiler_params=_cp(("parallel",)),
    )(w, g, m, v)


def _local_step(x, target, w_in_t, late_weights, norm_a_g, norm_b_g, sinks_a, ln1_g, ln1_b,
                conv_w, conv_b, ln2_g, ln2_b, slopes, on_grad, on_small):
    cwb = jnp.concatenate([conv_w, conv_b[None]], axis=0).reshape(4, 2, FF)

    proj, xb = _proj(x, w_in_t, "proj")
    o_a, lse_a = _attn_a_fwd(proj, sinks_a)
    fwd_b = [_attn_b_fwd(proj, slopes, r) for r in B_DILATIONS]
    w_o = late_weights(1, fwd_b[-1][1])
    o_b, lse_b, cat, z1, h1, h1b = _mix_ln1(x, o_a, [f[0] for f in fwd_b], [f[1] for f in fwd_b],
                                           norm_a_g, norm_b_g, w_o, ln1_g, ln1_b)
    w_up = late_weights(2, h1b)
    up = _up_proj(h1b, w_up)
    a, gate, a1 = _conv_gelu(up, cwb)
    w_down = late_weights(3, a)
    dz2, dz2b, st2 = _down_ln2_loss(a, w_down, h1, target, ln2_g, ln2_b)

    on_grad(3, *_grad_w(a, dz2b, "grad_w_down", tm=FF // 2))
    dup, dconv = _conv_gelu_bwd(_d_act(dz2b, w_down), up, gate, a1, cwb)
    on_grad(2, *_grad_w(dup, h1b, "grad_w_up", tm=FF // 2, lhs_halves=True))
    dz1, dz1b, st1 = _dh1_ln1_bwd(dz2, dup, w_up, z1, ln1_g)
    tok = on_grad(1, *_grad_w(cat, dz1b, "grad_w_o", tm=512))
    d_oa, d_ob, st_n = _dcat_rms_bwd(dz1b, w_o, o_a, o_b, norm_a_g + tok[0, 0], norm_b_g)
    dqa, dka, dva, dsink = _attn_a_bwd(proj, sinks_a, d_oa, o_a, lse_a)
    dconv = dconv.reshape(4, 2 * FF)
    tok = on_small(dict(loss=st2[2, 0:1], norm_a_g=st_n[0], norm_b_g=st_n[1], sinks_a=dsink[:, 0],
                        ln1_g=st1[0], ln1_b=st1[1], conv_w=dconv[0:3].reshape(-1), conv_b=dconv[3],
                        ln2_g=st2[0], ln2_b=st2[1]))
    slopes = slopes + tok[0, 0]
    bwd_b = None
    for r in reversed(B_DILATIONS):
        bwd_b = _attn_b_bwd(proj, slopes, d_ob, o_b, lse_b, r, bwd_b)
    dproj = _dproj_combine(dqa, dka, dva, bwd_b)
    tok = on_grad(0, *_grad_w(dproj, xb, "grad_w_in", tm=WA))
    return _grad_x(dz1, dproj, w_in_t, tok)


SMALL_ORDER = ("loss", "norm_a_g", "norm_b_g", "sinks_a", "ln1_g", "ln1_b", "conv_b", "ln2_g", "ln2_b", "conv_w")
SMALL_SIZES = dict(loss=1, norm_a_g=512, norm_b_g=512, sinks_a=8, ln1_g=D, ln1_b=D, conv_b=2 * FF, ln2_g=D, ln2_b=D,
                   conv_w=3 * 2 * FF)


def _pack(parts, rows):
    flat = jnp.concatenate([parts[k].reshape(-1).astype(F32) for k in parts])
    return jnp.pad(flat, (0, rows * D - flat.shape[0])).reshape(rows, D)


def _unpack(buf, names, sizes):
    flat = buf.reshape(-1)
    out, at = {}, 0
    for k in names:
        out[k] = flat[at:at + sizes[k]]
        at += sizes[k]
    return out


def kernel(x, w_in, norm_a_g, norm_b_g, sinks_a, w_o, ln1_g, ln1_b, w_up, conv_w, conv_b, w_down, ln2_g, ln2_b, loss_target, m_w_in, m_norm_a_g, m_norm_b_g, m_sinks_a, m_w_o, m_ln1_g, m_ln1_b, m_w_up, m_conv_w, m_conv_b, m_w_down, m_ln2_g, m_ln2_b, v_w_in, v_norm_a_g, v_norm_b_g, v_sinks_a, v_w_o, v_ln1_g, v_ln1_b, v_w_up, v_conv_w, v_conv_b, v_w_down, v_ln2_g, v_ln2_b):
    xi, yi, ci = _place()
    chip = (2 * xi + yi).astype(I32)
    core = ci.astype(I32)

    w_in_rows, m_w_in_rows, v_w_in_rows = w_in.T, m_w_in.T, v_w_in.T
    shards = (w_in_rows.astype(BF16), w_o.astype(BF16), w_up.astype(BF16), w_down.astype(BF16))
    w_in_t, conv_w4 = _gather_w_in(shards[0], conv_w)
    conv_w_f = conv_w4.transpose(1, 0, 2).reshape(3, 2 * FF)
    w_started, w_tok = _weights_start(shards[1:], conv_w4)
    slopes = jnp.asarray(SLOPES, F32) + w_tok[0, 0]

    halves_rows = [r // 2 for r in SHARD_ROWS]
    grads4, grads_b4, started = [None] * 4, [None] * 4, [None] * 4

    def on_grad(k, g, g_b):
        grads4[k] = g.reshape(N_CHIPS, 2, halves_rows[k], D)
        grads_b4[k] = g_b.reshape(N_CHIPS, 2, halves_rows[k], D)
        if k > 1:
            return None
        group = (1, 2, 3) if k == 1 else (0,)
        sts, tok = _grads_start([grads_b4[i] for i in group], f"grads_start_{k}")
        for i, st in zip(group, sts):
            started[i] = st
        return tok

    small_rows = 32
    small_started = []

    def on_small(parts):
        st, tok = _small_start(_pack({k: parts[k] for k in SMALL_ORDER}, small_rows))
        small_started.append(st)
        return tok

    gx = _local_step(
        x[0], loss_target[0], w_in_t, lambda k, after: _weights_wait(w_started[k - 1], after, f"weights_wait_{k}"),
        norm_a_g, norm_b_g, sinks_a, ln1_g, ln1_b, conv_w_f, conv_b, ln2_g, ln2_b, slopes, on_grad, on_small)

    tiles = (96, 128, 352, 176)
    core_chip = jnp.stack([core, chip])
    got = _grads_wait(started[1:], gx, "grads_wait_1")
    halves = [_sum_partials(grads4[k], got[k - 1], core_chip, f"sum_partials_{k}", tiles[k]) for k in (1, 2, 3)]
    g_w_o, g_w_up_rows, g_w_down = _swap_halves(halves, "swap_halves")
    g_w_up = g_w_up_rows.T
    delta, new_m, new_v = {}, {}, {}
    for k, g, tr in (("w_o", g_w_o, 128), ("w_up", g_w_up, 256), ("w_down", g_w_down, 176)):
        delta[k], new_m[k], new_v[k] = _adamw(dict(w_o=w_o, w_up=w_up, w_down=w_down)[k], g,
                                              dict(w_o=m_w_o, w_up=m_w_up, w_down=m_w_down)[k],
                                              dict(w_o=v_w_o, w_up=v_w_up, w_down=v_w_down)[k], f"adamw_{k}", tr)

    got = _grads_wait(started[:1], delta["w_up"], "grads_wait_0")
    half_in = _sum_partials(grads4[0], got[0], core_chip, "sum_partials_0", tiles[0])
    (g_w_in_rows,) = _swap_halves([half_in], "swap_halves_in")
    small_mine, small_land = _small_wait(small_started[0], g_w_in_rows)
    totals = _small_sum(small_mine, small_land, (4 * xi + 2 * yi + ci).astype(I32).reshape(1))
    tot = _unpack(totals, SMALL_ORDER, SMALL_SIZES)
    loss = tot["loss"][0]
    cols = 2 * FF // N_CHIPS
    g_conv_w = lax.dynamic_slice(tot["conv_w"].reshape(3, 2 * FF), (0, chip * cols), (3, cols))
    g_small = dict(norm_a_g=tot["norm_a_g"], norm_b_g=tot["norm_b_g"], sinks_a=tot["sinks_a"], ln1_g=tot["ln1_g"],
                   ln1_b=tot["ln1_b"], conv_w=g_conv_w, conv_b=tot["conv_b"], ln2_g=tot["ln2_g"], ln2_b=tot["ln2_b"])

    weights = dict(w_in=w_in, norm_a_g=norm_a_g, norm_b_g=norm_b_g, sinks_a=sinks_a, w_o=w_o, ln1_g=ln1_g, ln1_b=ln1_b,
                   w_up=w_up, conv_w=conv_w, conv_b=conv_b, w_down=w_down, ln2_g=ln2_g, ln2_b=ln2_b)
    ms = dict(w_in=m_w_in, norm_a_g=m_norm_a_g, norm_b_g=m_norm_b_g, sinks_a=m_sinks_a, w_o=m_w_o, ln1_g=m_ln1_g,
              ln1_b=m_ln1_b, w_up=m_w_up, conv_w=m_conv_w, conv_b=m_conv_b, w_down=m_w_down, ln2_g=m_ln2_g, ln2_b=m_ln2_b)
    vs = dict(w_in=v_w_in, norm_a_g=v_norm_a_g, norm_b_g=v_norm_b_g, sinks_a=v_sinks_a, w_o=v_w_o, ln1_g=v_ln1_g,
              ln1_b=v_ln1_b, w_up=v_w_up, conv_w=v_conv_w, conv_b=v_conv_b, w_down=v_w_down, ln2_g=v_ln2_g, ln2_b=v_ln2_b)
    order = list(weights)
    grad = dict(g_small, w_in=g_w_in_rows.T, w_o=g_w_o, w_up=g_w_up, w_down=g_w_down)

    delta["w_in"], new_m["w_in"], new_v["w_in"] = [
        a.T for a in _adamw(w_in_rows, g_w_in_rows, m_w_in_rows, v_w_in_rows, "adamw_w_in", 144)]
    small_names = [k for k in order if k not in delta]
    sizes = {k: weights[k].size for k in small_names}
    rows = 16
    packed = [_pack({k: src[k] for k in small_names}, rows) for src in (weights, grad, ms, vs)]
    for res, buf in zip((delta, new_m, new_v), _adamw(*packed, "adamw_small", rows)):
        for k, val in _unpack(buf, small_names, sizes).items():
            res[k] = val.reshape(weights[k].shape)

    return (loss, gx[None], *[grad[k] for k in order], *[delta[k] for k in order],
            *[new_m[k] for k in order], *[new_v[k] for k in order])
```

```python
import functools
import math

import jax
import jax.numpy as jnp
from jax import lax
from jax.experimental import pallas as pl
from jax.experimental.pallas import tpu as pltpu

F32, BF16, I32 = jnp.float32, jnp.bfloat16, jnp.int32

D = 1024
FF = 2816
HD = 64
NH = 8
WA, WB = 768, 1536
WIN = WA + WB
BLK = 128
ALPHA = 2.0 ** 0.25
LN_EPS, RMS_EPS = 1e-5, 1e-6
SCALE = 1.0 / math.sqrt(HD)
A_MAX_DIST, B_MAX_DIST = 127, 128
B_DILATIONS = (1, 4, 16)
SLOPES = tuple(2.0 ** (-(i + 1)) for i in range(NH))
SHARD_ROWS = (WIN // 4, D // 4, 2 * FF // 4, FF // 4)
N_CHIPS = 4
ADAM_LR, ADAM_B1, ADAM_B2, ADAM_EPS, ADAM_WD, ADAM_STEP = 0.001, 0.9, 0.999, 1e-08, 0.01, 10
MESH = pl.DeviceIdType.MESH
ANY = pl.BlockSpec(memory_space=pl.ANY)
SMEM = pl.BlockSpec(memory_space=pltpu.SMEM)
VMEM = pl.BlockSpec(memory_space=pltpu.VMEM)
HBM = pl.BlockSpec(memory_space=pltpu.HBM)
SEM = pl.BlockSpec(memory_space=pltpu.SEMAPHORE)
DATAFLOW = pltpu.SideEffectType.DATAFLOW_SIDE_EFFECTING


def _cp(sem, mb=48):
    return pltpu.CompilerParams(dimension_semantics=sem, vmem_limit_bytes=mb << 20)


def _nn(a, b):
    return lax.dot_general(a, b, (((1,), (0,)), ((), ())), preferred_element_type=F32)


def _nt(a, b):
    return lax.dot_general(a, b, (((1,), (1,)), ((), ())), preferred_element_type=F32)


def _tn(a, b):
    return lax.dot_general(a, b, (((0,), (0,)), ((), ())), preferred_element_type=F32)


def _resident(shape):
    n = len(shape)
    return pl.BlockSpec(shape, lambda *_: (0,) * n, pipeline_mode=pl.Buffered(1))


def _const(shape):
    n = len(shape)
    return pl.BlockSpec(shape, lambda *_: (0,) * n)


def _proj(x, w_t, name, tm=512):
    s = x.shape[0]
    n = w_t.shape[0]

    def body(x_ref, w_ref, o_ref, xb_ref):
        xb = x_ref[...].astype(BF16)
        xb_ref[...] = xb
        res = _nt(xb, w_ref[...])
        for g in range(n // 128):
            o_ref[g] = res[:, 128 * g:128 * (g + 1)]

    return pl.pallas_call(
        body, name=name, grid=(s // tm,),
        in_specs=[pl.BlockSpec((tm, D), lambda i: (i, 0)), _resident((n, D))],
        out_specs=[pl.BlockSpec((n // 128, tm, 128), lambda i: (0, i, 0)), pl.BlockSpec((tm, D), lambda i: (i, 0))],
        out_shape=[jax.ShapeDtypeStruct((n // 128, s, 128), F32), jax.ShapeDtypeStruct((s, D), BF16)],
        compiler_params=_cp(("parallel",)),
    )(x, w_t)


def _grad_w(lhs, rhs, name, tm, tk=2048, lhs_halves=False):
    s = rhs.shape[0]
    if lhs_halves:
        per_half = lhs.shape[2] // tm
        n = 2 * lhs.shape[2]
        lhs_spec = pl.BlockSpec((None, tk, tm), lambda i, k: (i // per_half, k, i % per_half))
    else:
        n = lhs.shape[1]
        lhs_spec = pl.BlockSpec((tk, tm), lambda i, k: (k, i))
    nk = s // tk

    def body(l_ref, r_ref, o_ref, ob_ref):
        k = pl.program_id(1)

        @pl.when(k == 0)
        def _():
            o_ref[...] = jnp.zeros_like(o_ref)

        o_ref[...] += _tn(l_ref[...], r_ref[...])

        @pl.when(k == nk - 1)
        def _():
            ob_ref[...] = o_ref[...].astype(BF16)

    return pl.pallas_call(
        body, name=name, grid=(n // tm, nk),
        in_specs=[lhs_spec, pl.BlockSpec((tk, D), lambda i, k: (k, 0))],
        out_specs=[pl.BlockSpec((tm, D), lambda i, k: (i, 0))] * 2,
        out_shape=[jax.ShapeDtypeStruct((n, D), F32), jax.ShapeDtypeStruct((n, D), BF16)],
        compiler_params=_cp(("parallel", "arbitrary")),
    )(lhs, rhs)


def _band_base(max_dist, dist_unit, first):
    row = lax.broadcasted_iota(I32, (BLK, 2 * BLK), 0)
    col = lax.broadcasted_iota(I32, (BLK, 2 * BLK), 1)
    dist = BLK + row - col
    ok = (dist >= 0) & (dist <= max_dist)
    if first:
        ok = ok & (col >= BLK)
    return jnp.where(ok, dist.astype(F32) * (-float(dist_unit)), -jnp.inf)


def _half_mask(shape, e):
    lane = lax.broadcasted_iota(I32, shape, 1)
    return (lane < HD) if e == 0 else (lane >= HD)


def _to_half(x, e, g):
    if g != e:
        x = pltpu.roll(x, HD, 1)
    return jnp.where(_half_mask(x.shape, g), x, 0.0)


def _stack_heads(scalars, tile):
    return jnp.concatenate([scalars[0] * tile, scalars[1] * tile], axis=0)


def _pair_fwd(q2, kb, vb, base, slopes, kv_heads, sinks):
    lo = _half_mask((BLK, 2 * HD), 0)
    if slopes is None:
        bias = base
    elif sinks is None:
        bias = _stack_heads(slopes, base)
    else:
        col0 = lax.broadcasted_iota(I32, base.shape, 1) == 0
        bias = jnp.concatenate([jnp.where(col0, sinks[e], slopes[e] * base) for e in (0, 1)], axis=0)
    qs = jnp.concatenate([_to_half(q2, e, kv_heads[e]) * SCALE for e in (0, 1)], axis=0).astype(BF16)
    s = _nt(qs, kb) + bias
    m = jnp.max(s, axis=1, keepdims=True)
    p = jnp.exp(s - m)
    l = jnp.sum(p, axis=1, keepdims=True)
    o = _nn(p.astype(BF16), vb) / l
    lse = m + jnp.log(l)
    halves = []
    for e in (0, 1):
        oh = o[e * BLK:(e + 1) * BLK]
        halves.append(pltpu.roll(oh, HD, 1) if kv_heads[e] != e else oh)
    o2 = jnp.where(lo, halves[0], halves[1])
    lse2 = jnp.where(lo, jnp.broadcast_to(lse[:BLK], (BLK, 2 * HD)), jnp.broadcast_to(lse[BLK:], (BLK, 2 * HD)))
    return o2, lse2


def _pair_bwd(q2, kb, vb, do2, o2, lse2, base, slopes, kv_heads, sinks):
    lo = _half_mask((BLK, 2 * HD), 0)
    prod = do2 * o2
    lses, deltas = [], []
    for e in (0, 1):
        hq = _half_mask((BLK, 2 * HD), e)
        lses.append(jnp.max(jnp.where(hq, lse2, -jnp.inf), axis=1, keepdims=True))
        deltas.append(jnp.sum(jnp.where(hq, prod, 0.0), axis=1, keepdims=True))
    lse = jnp.concatenate(lses, axis=0)
    delta = jnp.concatenate(deltas, axis=0)
    qs = jnp.concatenate([_to_half(q2, e, kv_heads[e]) * SCALE for e in (0, 1)], axis=0).astype(BF16)
    dos = jnp.concatenate([_to_half(do2, e, kv_heads[e]) for e in (0, 1)], axis=0).astype(BF16)
    p = jnp.exp(_nt(qs, kb) + (base if slopes is None else _stack_heads(slopes, base)) - lse)
    ds = (p * (_nt(dos, vb) - delta)).astype(BF16)
    dq = _nn(ds, kb) * SCALE
    halves = []
    for e in (0, 1):
        dqh = dq[e * BLK:(e + 1) * BLK]
        halves.append(pltpu.roll(dqh, HD, 1) if kv_heads[e] != e else dqh)
    dq2 = jnp.where(lo, halves[0], halves[1])
    dk2 = _tn(ds, qs)
    dv2 = _tn(p.astype(BF16), dos)
    dsinks = []
    if sinks is not None:
        for e in (0, 1):
            dsinks.append(jnp.sum(-jnp.exp(sinks[e] - lses[e]) * deltas[e], axis=0, keepdims=True))
    return dq2, dk2, dv2, dsinks


A_BLOCKS_PER_STEP = 2
A_BLOCKS_PER_STEP_BWD = 1


def _attn_a_fwd(proj, sinks):
    s = proj.shape[1]
    nq = A_BLOCKS_PER_STEP
    rows = BLK * nq
    steps = s // rows

    def body(sink_ref, q_ref, kp_ref, kc_ref, vp_ref, vc_ref, o_ref, lse_ref):
        n = pl.program_id(0)
        base_rest = _band_base(A_MAX_DIST, 1, False)
        base_0 = jnp.where(n > 0, base_rest, _band_base(A_MAX_DIST, 1, True))
        for i in range(nq):
            cur = pl.ds(i * BLK, BLK)
            k_prev = kc_ref[pl.ds((i - 1) * BLK, BLK), :] if i > 0 else kp_ref[...]
            v_prev = vc_ref[pl.ds((i - 1) * BLK, BLK), :] if i > 0 else vp_ref[...]
            first_key = lax.broadcasted_iota(I32, (2 * BLK, 128), 0) == 0
            kb = jnp.where(first_key, 0.0, jnp.concatenate([k_prev, kc_ref[cur, :]], axis=0)).astype(BF16)
            vb = jnp.where(first_key, 0.0, jnp.concatenate([v_prev, vc_ref[cur, :]], axis=0)).astype(BF16)
            for j in range(NH // 2):
                g = j // 2
                o2, lse2 = _pair_fwd(q_ref[j, cur, :], kb, vb, base_rest if i > 0 else base_0,
                                     (SLOPES[2 * j], SLOPES[2 * j + 1]), (g, g), (sink_ref[2 * j], sink_ref[2 * j + 1]))
                o_ref[j, cur, :] = o2
                lse_ref[j, cur, :] = lse2

    before = lambda n: jnp.maximum(n * nq - 1, 0)
    slab = lambda g: pl.BlockSpec((None, rows, 128), lambda n: (g, n, 0))
    edge = lambda g: pl.BlockSpec((None, BLK, 128), lambda n: (g, before(n), 0))
    quad = pl.BlockSpec((4, rows, 128), lambda n: (0, n, 0))
    return pl.pallas_call(
        body, name="attn_a_fwd", grid=(steps,),
        in_specs=[SMEM, quad, edge(4), slab(4), edge(5), slab(5)],
        out_specs=[quad, quad],
        out_shape=[jax.ShapeDtypeStruct((4, s, 128), F32)] * 2,
        compiler_params=_cp(("parallel",)),
    )(sinks, proj, proj, proj, proj, proj)


def _attn_a_bwd(proj, sinks, d_o, o, lse):
    s = proj.shape[1]
    nq = A_BLOCKS_PER_STEP_BWD
    rows = BLK * nq
    steps = s // rows

    def body(sink_ref, q_ref, kp_ref, kc_ref, vp_ref, vc_ref, do_ref, o_ref, lse_ref,
             dq_ref, dk_ref, dv_ref, dsink_ref, kcar, vcar):
        n = pl.program_id(0)

        @pl.when(n == 0)
        def _():
            kcar[...] = jnp.zeros_like(kcar)
            vcar[...] = jnp.zeros_like(vcar)
            dsink_ref[...] = jnp.zeros_like(dsink_ref)

        dk_ref[...] = kcar[...].astype(BF16)
        dv_ref[...] = vcar[...].astype(BF16)

        @pl.when(n < steps)
        def _():
            base_rest = _band_base(A_MAX_DIST, 1, False)
            base_0 = jnp.where(n > 0, base_rest, _band_base(A_MAX_DIST, 1, True))
            for i in range(nq):
                cur = pl.ds(i * BLK, BLK)
                k_prev = kc_ref[pl.ds((i - 1) * BLK, BLK), :] if i > 0 else kp_ref[...]
                v_prev = vc_ref[pl.ds((i - 1) * BLK, BLK), :] if i > 0 else vp_ref[...]
                kb = jnp.concatenate([k_prev, kc_ref[cur, :]], axis=0).astype(BF16)
                vb = jnp.concatenate([v_prev, vc_ref[cur, :]], axis=0).astype(BF16)
                dk_win = dv_win = None
                for j in range(NH // 2):
                    g = j // 2
                    dq2, dk2, dv2, dsk = _pair_bwd(q_ref[j, cur, :], kb, vb, do_ref[j, cur, :], o_ref[j, cur, :],
                                                   lse_ref[j, cur, :], base_rest if i > 0 else base_0,
                                                   (SLOPES[2 * j], SLOPES[2 * j + 1]), (g, g),
                                                   (sink_ref[2 * j], sink_ref[2 * j + 1]))
                    dq_ref[j, cur, :] = dq2.astype(BF16)
                    dk_win = dk2 if j == 0 else dk_win + dk2
                    dv_win = dv2 if j == 0 else dv_win + dv2
                    for e in (0, 1):
                        h = 2 * j + e
                        dsink_ref[h:h + 1, :] += jnp.broadcast_to(dsk[e], (1, 128))
                if i == 0:
                    last = pl.ds((nq - 1) * BLK, BLK)
                    dk_ref[last, :] = (kcar[last, :] + dk_win[:BLK]).astype(BF16)
                    dv_ref[last, :] = (vcar[last, :] + dv_win[:BLK]).astype(BF16)
                else:
                    kcar[pl.ds((i - 1) * BLK, BLK), :] += dk_win[:BLK]
                    vcar[pl.ds((i - 1) * BLK, BLK), :] += dv_win[:BLK]
                kcar[cur, :] = dk_win[BLK:]
                vcar[cur, :] = dv_win[BLK:]

    cur_step = lambda n: jnp.minimum(n, steps - 1)
    before = lambda n: jnp.maximum(cur_step(n) * nq - 1, 0)
    out_prev = lambda n: jnp.maximum(n - 1, 0)
    quad = pl.BlockSpec((4, rows, 128), lambda n: (0, cur_step(n), 0))
    slab = lambda g: pl.BlockSpec((None, rows, 128), lambda n: (g, cur_step(n), 0))
    edge = lambda g: pl.BlockSpec((None, BLK, 128), lambda n: (g, before(n), 0))
    return pl.pallas_call(
        body, name="attn_a_bwd", grid=(steps + 1,),
        in_specs=[SMEM, quad, edge(4), slab(4), edge(5), slab(5), quad, quad, quad],
        out_specs=[quad,
                   pl.BlockSpec((rows, 128), lambda n: (out_prev(n), 0)),
                   pl.BlockSpec((rows, 128), lambda n: (out_prev(n), 0)),
                   pl.BlockSpec((NH, 128), lambda n: (0, 0))],
        out_shape=[jax.ShapeDtypeStruct((4, s, 128), BF16), jax.ShapeDtypeStruct((s, 128), BF16),
                   jax.ShapeDtypeStruct((s, 128), BF16), jax.ShapeDtypeStruct((NH, 128), F32)],
        scratch_shapes=[pltpu.VMEM((rows, 128), F32), pltpu.VMEM((rows, 128), F32)],
        compiler_params=_cp(("arbitrary",)),
    )(sinks, proj, proj, proj, proj, proj, d_o, o, lse)


def _stream(rho, i, r):
    start = i * BLK * r + rho
    return pl.ds(start, BLK, stride=r) if r > 1 else pl.ds(start, BLK)


def _for_streams(r, fn, side_by_side=4):
    if r <= side_by_side:
        for rho in range(r):
            fn(rho)
    else:
        def group(it, carry):
            for u in range(side_by_side):
                fn(side_by_side * it + u)
            return carry

        lax.fori_loop(0, r // side_by_side, group, 0)


B_BLOCKS_PER_STEP = {1: 8, 4: 2, 16: 1}
B_BLOCKS_PER_STEP_FWD = {1: 16, 4: 4, 16: 1}


def _attn_b_fwd(proj, slopes, r):
    s = proj.shape[1]
    nq = B_BLOCKS_PER_STEP_FWD[r]
    rows = BLK * r * nq
    steps = s // rows
    qc, kc, vc = WA // 128, WA // 128 + 4, WA // 128 + 8

    def body(slope_ref, q_ref, kp_ref, kc_ref, vp_ref, vc_ref, o_ref, lse_ref):
        j = pl.program_id(0)
        sb = pl.program_id(1)
        sl2 = (slope_ref[2 * j], slope_ref[2 * j + 1])
        bias_rest = _stack_heads(sl2, _band_base(B_MAX_DIST, r, False))
        bias_0 = jnp.where(sb > 0, bias_rest, _stack_heads(sl2, _band_base(B_MAX_DIST, r, True)))

        def stream(rho):
            for i in range(nq):
                cur = _stream(rho, i, r)
                k_prev = kc_ref[_stream(rho, i - 1, r), :] if i > 0 else kp_ref[_stream(rho, 0, r), :]
                v_prev = vc_ref[_stream(rho, i - 1, r), :] if i > 0 else vp_ref[_stream(rho, 0, r), :]
                kb = jnp.concatenate([k_prev, kc_ref[cur, :]], axis=0).astype(BF16)
                vb = jnp.concatenate([v_prev, vc_ref[cur, :]], axis=0).astype(BF16)
                o2, lse2 = _pair_fwd(q_ref[cur, :], kb, vb, bias_rest if i > 0 else bias_0, None, (0, 1), None)
                o_ref[cur, :] = o2
                lse_ref[cur, :] = lse2

        _for_streams(r, stream, side_by_side=16)

    before = lambda sb: jnp.maximum(sb * nq - 1, 0)
    return pl.pallas_call(
        body, name=f"attn_b_fwd_r{r}", grid=(NH // 2, steps),
        in_specs=[SMEM,
                  pl.BlockSpec((None, rows, 128), lambda j, sb: (qc + j, sb, 0)),
                  pl.BlockSpec((None, BLK * r, 128), lambda j, sb: (kc + j, before(sb), 0)),
                  pl.BlockSpec((None, rows, 128), lambda j, sb: (kc + j, sb, 0)),
                  pl.BlockSpec((None, BLK * r, 128), lambda j, sb: (vc + j, before(sb), 0)),
                  pl.BlockSpec((None, rows, 128), lambda j, sb: (vc + j, sb, 0))],
        out_specs=[pl.BlockSpec((None, rows, 128), lambda j, sb: (j, sb, 0))] * 2,
        out_shape=[jax.ShapeDtypeStruct((4, s, 128), F32)] * 2,
        compiler_params=_cp(("parallel", "parallel")),
    )(slopes, proj, proj, proj, proj, proj)


def _attn_b_bwd(proj, slopes, d_o, o, lse, r, so_far=None, dtype=F32):
    s = proj.shape[1]
    nq = B_BLOCKS_PER_STEP[r]
    rows = BLK * r * nq
    steps = s // rows
    qc, kc, vc = WA // 128, WA // 128 + 4, WA // 128 + 8
    chained = so_far is not None

    def body(slope_ref, q_ref, kp_ref, kc_ref, vp_ref, vc_ref, do_ref, o_ref, lse_ref, *rest):
        pq_ref, pk_ref, pv_ref = rest[:3] if chained else (None, None, None)
        dq_ref, dk_ref, dv_ref, kcar, vcar = rest[-5:]
        j = pl.program_id(0)
        sb = pl.program_id(1)

        @pl.when(sb == 0)
        def _():
            kcar[...] = jnp.zeros_like(kcar)
            vcar[...] = jnp.zeros_like(vcar)

        def settled(car, p_ref, idx):
            return car[idx] + p_ref[idx] if chained else car[idx]

        dk_ref[...] = settled(kcar, pk_ref, ...).astype(dtype)
        dv_ref[...] = settled(vcar, pv_ref, ...).astype(dtype)

        @pl.when(sb < steps)
        def _():
            sl2 = (slope_ref[2 * j], slope_ref[2 * j + 1])
            bias_rest = _stack_heads(sl2, _band_base(B_MAX_DIST, r, False))
            bias_0 = jnp.where(sb > 0, bias_rest, _stack_heads(sl2, _band_base(B_MAX_DIST, r, True)))

            def stream(rho):
                for i in range(nq):
                    cur = _stream(rho, i, r)
                    k_prev = kc_ref[_stream(rho, i - 1, r), :] if i > 0 else kp_ref[_stream(rho, 0, r), :]
                    v_prev = vc_ref[_stream(rho, i - 1, r), :] if i > 0 else vp_ref[_stream(rho, 0, r), :]
                    kb = jnp.concatenate([k_prev, kc_ref[cur, :]], axis=0).astype(BF16)
                    vb = jnp.concatenate([v_prev, vc_ref[cur, :]], axis=0).astype(BF16)
                    dq2, dk2, dv2, _ = _pair_bwd(q_ref[cur, :], kb, vb, do_ref[cur, :], o_ref[cur, :], lse_ref[cur, :],
                                                 bias_rest if i > 0 else bias_0, None, (0, 1), None)
                    dq_ref[cur, :] = (dq2 + pq_ref[cur, :] if chained else dq2).astype(dtype)
                    if i == 0:
                        last = (_stream(rho, nq - 1, r), slice(None))
                        dk_ref[last] = (settled(kcar, pk_ref, last) + dk2[:BLK]).astype(dtype)
                        dv_ref[last] = (settled(vcar, pv_ref, last) + dv2[:BLK]).astype(dtype)
                    else:
                        kcar[_stream(rho, i - 1, r), :] += dk2[:BLK]
                        vcar[_stream(rho, i - 1, r), :] += dv2[:BLK]
                    kcar[cur, :] = dk2[BLK:]
                    vcar[cur, :] = dv2[BLK:]

            _for_streams(r, stream, side_by_side=8)

    cur_step = lambda sb: jnp.minimum(sb, steps - 1)
    before = lambda sb: jnp.maximum(cur_step(sb) * nq - 1, 0)
    out_prev = lambda sb: jnp.maximum(sb - 1, 0)
    tile = lambda slab: pl.BlockSpec((None, rows, 128), lambda j, sb: (slab + j, cur_step(sb), 0))
    edge = lambda slab: pl.BlockSpec((None, BLK * r, 128), lambda j, sb: (slab + j, before(sb), 0))
    late = pl.BlockSpec((None, rows, 128), lambda j, sb: (j, out_prev(sb), 0))
    grads = [tile(0), late, late]
    return pl.pallas_call(
        body, name=f"attn_b_bwd_r{r}", grid=(NH // 2, steps + 1),
        in_specs=[SMEM, tile(qc), edge(kc), tile(kc), edge(vc), tile(vc), tile(0), tile(0), tile(0)]
        + (grads if chained else []),
        out_specs=grads,
        out_shape=[jax.ShapeDtypeStruct((4, s, 128), dtype)] * 3,
        scratch_shapes=[pltpu.VMEM((rows, 128), F32), pltpu.VMEM((rows, 128), F32)],
        compiler_params=_cp(("parallel", "arbitrary")),
    )(slopes, proj, proj, proj, proj, proj, d_o, o, lse, *(so_far if chained else ()))


def _row(v):
    return v.reshape(1, -1)


def _layer_norm_stats(z):
    mu = jnp.mean(z, axis=-1, keepdims=True)
    zc = z - mu
    var = jnp.mean(zc * zc, axis=-1, keepdims=True)
    rstd = lax.rsqrt(var + LN_EPS)
    return zc * rstd, rstd


def _layer_norm_bwd(dh, zh, rstd, g):
    dzh = dh * g
    return rstd * (dzh - jnp.mean(dzh, axis=-1, keepdims=True) - zh * jnp.mean(dzh * zh, axis=-1, keepdims=True))


def _rms(o):
    return lax.rsqrt(jnp.mean(o * o, axis=-1, keepdims=True) + RMS_EPS)


def _mix_ln1(x, o_a, o_b, lse_b, norm_a_g, norm_b_g, w_o, ln1_g, ln1_b, tm=256):
    s = x.shape[0]

    def wide(ref):
        return jnp.concatenate([ref[j] for j in range(4)], axis=1)

    def body(x_ref, oa_ref, ob1, ob2, ob3, l1, l2, l3, ga_ref, gb_ref, wo_ref, g_ref, b_ref,
             obm_ref, lse_ref, cat_ref, z1_ref, h1_ref, h1b_ref):
        la, lb, lc = wide(l1), wide(l2), wide(l3)
        m = jnp.maximum(jnp.maximum(la, lb), lc)
        ea, eb, ec = jnp.exp(la - m), jnp.exp(lb - m), jnp.exp(lc - m)
        den = ea + eb + ec
        obm = (ea / den) * wide(ob1) + (eb / den) * wide(ob2) + (ec / den) * wide(ob3)
        lse = m + jnp.log(den)
        for j in range(4):
            obm_ref[j] = obm[:, 128 * j:128 * (j + 1)]
            lse_ref[j] = lse[:, 128 * j:128 * (j + 1)]
        oa = wide(oa_ref)
        na = oa * _rms(oa) * ga_ref[...]
        nb_ = obm * _rms(obm) * gb_ref[...]
        cat = jnp.concatenate([na, nb_], axis=1).astype(BF16)
        cat_ref[...] = cat
        z1 = ALPHA * x_ref[...] + _nn(cat, wo_ref[...])
        z1_ref[...] = z1
        zh, _ = _layer_norm_stats(z1)
        h1 = zh * g_ref[...] + b_ref[...]
        h1_ref[...] = h1
        h1b_ref[...] = h1.astype(BF16)

    t512 = pl.BlockSpec((4, tm, 128), lambda i: (0, i, 0))
    td = pl.BlockSpec((tm, D), lambda i: (i, 0))
    return pl.pallas_call(
        body, name="mix_ln1", grid=(s // tm,),
        in_specs=[td] + [t512] * 7 + [_const((1, 512))] * 2 + [_resident((D, D))] + [_const((1, D))] * 2,
        out_specs=[t512, t512, td, td, td, td],
        out_shape=[jax.ShapeDtypeStruct((4, s, 128), F32), jax.ShapeDtypeStruct((4, s, 128), F32),
                   jax.ShapeDtypeStruct((s, D), BF16), jax.ShapeDtypeStruct((s, D), F32),
                   jax.ShapeDtypeStruct((s, D), F32), jax.ShapeDtypeStruct((s, D), BF16)],
        compiler_params=_cp(("parallel",)),
    )(x, o_a, *o_b, *lse_b, _row(norm_a_g), _row(norm_b_g), w_o, _row(ln1_g), _row(ln1_b))


def _gelu_and_grad(x):
    c = math.sqrt(2.0 / math.pi)
    x2 = x * x
    s = 0.5 * jnp.tanh(x * ((c * 0.044715) * x2 + c)) + 0.5
    dg = s + (x * ((6.0 * c * 0.044715) * x2 + 2.0 * c)) * (s - s * s)
    return x * s, dg


def _shifted(u, edge, row, down):
    groups = [u[8 * i:8 * i + 8] for i in range(u.shape[0] // 8)]
    others = [edge] + groups[:-1] if down else groups[1:] + [edge]
    moved = []
    for k in (1, 2):
        crossing = row >= 8 - k if down else row < k
        moved.append(jnp.concatenate([pltpu.roll(jnp.where(crossing, o, g), k if down else 8 - k, 0)
                                      for o, g in zip(others, groups)], axis=0))
    return moved


def _up_proj(h1b, w_up, tm=512):
    s = h1b.shape[0]

    def body(h_ref, w_ref, o_ref):
        h = h_ref[...]
        for half in (0, 1):
            o_ref[half] = _nn(h, w_ref[:, half * FF:(half + 1) * FF]).astype(BF16)

    return pl.pallas_call(
        body, name="up_proj", grid=(s // tm,),
        in_specs=[pl.BlockSpec((tm, D), lambda i: (i, 0)), _resident((D, 2 * FF))],
        out_specs=pl.BlockSpec((2, tm, FF), lambda i: (0, i, 0)),
        out_shape=jax.ShapeDtypeStruct((2, s, FF), BF16),
        compiler_params=_cp(("parallel",)),
    )(h1b, w_up)


def _conv_gelu(up, cwb, tm=256, tn=FF // 2, chunk_rows=16):
    s = up.shape[1]
    n_c = tm // chunk_rows

    def body(up_ref, c_ref, a_ref, g_ref, a1_ref, carry):
        @pl.when(pl.program_id(1) == 0)
        def _():
            carry[...] = jnp.zeros_like(carry)

        row = lax.broadcasted_iota(jnp.int32, (8, tn), 0)
        edge = [carry[0], carry[1]]
        for c in range(n_c):
            rows = pl.ds(c * chunk_rows, chunk_rows)
            u = []
            for half in (0, 1):
                x = up_ref[half, rows, :].astype(F32)
                r1, r2 = _shifted(x, edge[half], row, True)
                u.append(r2 * c_ref[0, half:half + 1, :] + r1 * c_ref[1, half:half + 1, :]
                         + x * c_ref[2, half:half + 1, :] + c_ref[3, half:half + 1, :])
                edge[half] = x[chunk_rows - 8:]
            g, dg = _gelu_and_grad(u[0])
            a_ref[rows, :] = (g * u[1]).astype(BF16)
            g_ref[rows, :] = g.astype(BF16)
            a1_ref[rows, :] = (u[1] * dg).astype(BF16)
        for half in (0, 1):
            carry[half] = edge[half]

    pair = pl.BlockSpec((2, tm, tn), lambda j, i: (0, i, j))
    tile = pl.BlockSpec((tm, tn), lambda j, i: (i, j))
    return pl.pallas_call(
        body, name="conv_gelu", grid=(FF // tn, s // tm),
        in_specs=[pair, pl.BlockSpec((4, 2, tn), lambda j, i: (0, 0, j))],
        out_specs=[tile, tile, tile],
        out_shape=[jax.ShapeDtypeStruct((s, FF), BF16)] * 3,
        scratch_shapes=[pltpu.VMEM((2, 8, tn), F32)],
        compiler_params=_cp(("parallel", "arbitrary")),
    )(up, cwb)


def _down_ln2_loss(a, w_down, h1, target, ln2_g, ln2_b, tm=512):
    s = a.shape[0]

    def body(a_ref, w_ref, h_ref, t_ref, g_ref, b_ref, dz_ref, dzb_ref, st_ref):
        @pl.when(pl.program_id(0) == 0)
        def _():
            st_ref[...] = jnp.zeros_like(st_ref)

        z2 = ALPHA * h_ref[...] + _nn(a_ref[...], w_ref[...])
        zh, rstd = _layer_norm_stats(z2)
        diff = zh * g_ref[...] + b_ref[...] - t_ref[...]
        part = 0.5 * jnp.sum(jnp.mean(diff * diff, axis=-1, keepdims=True), axis=0, keepdims=True)
        dy = diff * (1.0 / D)
        st_ref[0:1, :] += jnp.sum(dy * zh, axis=0, keepdims=True)
        st_ref[1:2, :] += jnp.sum(dy, axis=0, keepdims=True)
        st_ref[2:3, :] += jnp.broadcast_to(part, (1, D))
        dz = _layer_norm_bwd(dy, zh, rstd, g_ref[...])
        dz_ref[...] = dz
        dzb_ref[...] = dz.astype(BF16)

    td = pl.BlockSpec((tm, D), lambda i: (i, 0))
    return pl.pallas_call(
        body, name="down_ln2_loss", grid=(s // tm,),
        in_specs=[pl.BlockSpec((tm, FF), lambda i: (i, 0)), _resident((FF, D)), td, td, _const((1, D)), _const((1, D))],
        out_specs=[td, td, _const((8, D))],
        out_shape=[jax.ShapeDtypeStruct((s, D), F32), jax.ShapeDtypeStruct((s, D), BF16),
                   jax.ShapeDtypeStruct((8, D), F32)],
        compiler_params=_cp(("arbitrary",)),
    )(a, w_down, h1, target, _row(ln2_g), _row(ln2_b))


def _d_act(dz2b, w_down, tm=512):
    s = dz2b.shape[0]

    def body(dz_ref, w_ref, o_ref):
        o_ref[...] = _nt(dz_ref[...], w_ref[...]).astype(BF16)

    return pl.pallas_call(
        body, name="d_act", grid=(s // tm,),
        in_specs=[pl.BlockSpec((tm, D), lambda i: (i, 0)), _resident((FF, D))],
        out_specs=pl.BlockSpec((tm, FF), lambda i: (i, 0)),
        out_shape=jax.ShapeDtypeStruct((s, FF), BF16),
        compiler_params=_cp(("parallel",)),
    )(dz2b, w_down)


def _conv_gelu_bwd(da, up, g, a1, cwb, tm=256, tn=FF // 2, chunk_rows=16):
    s = da.shape[0]
    n_i = s // tm
    n_c = tm // chunk_rows

    def body(da_ref, up_ref, g_ref, a1_ref, c_ref, dup_ref, dc_ref, carry):
        @pl.when(pl.program_id(1) == 0)
        def _():
            carry[...] = jnp.zeros_like(carry)
            dc_ref[...] = jnp.zeros_like(dc_ref)

        def fold(v):
            return jnp.sum(v.reshape(chunk_rows // 8, 8, v.shape[1]), axis=0)

        def chunk(cc, state):
            after, sums = state
            rows = pl.ds((n_c - 1 - cc) * chunk_rows, chunk_rows)
            da_c = da_ref[rows, :].astype(F32)
            dus = (da_c * a1_ref[rows, :].astype(F32), da_c * g_ref[rows, :].astype(F32))
            head, new_sums = [], []
            for half in (0, 1):
                du = dus[half]
                up = up_ref[half, rows, :].astype(F32)
                l1, l2 = _shifted(du, after[half], row, False)
                dup = (du * c_ref[2, half:half + 1, :] + l1 * c_ref[1, half:half + 1, :]
                       + l2 * c_ref[0, half:half + 1, :])
                dup_ref[half, rows, :] = dup.astype(BF16)
                parts = (fold(l2 * up), fold(l1 * up), fold(du * up), fold(du))
                new_sums.append(parts if sums is None else tuple(a + b for a, b in zip(sums[half], parts)))
                head.append(du[:8])
            return tuple(head), new_sums

        row = lax.broadcasted_iota(jnp.int32, (8, tn), 0)
        state = ((carry[0], carry[1]), None)
        for cc in range(n_c):
            state = chunk(cc, state)
        head, sums = state
        for half in (0, 1):
            carry[half] = head[half]
            for k in range(4):
                dc_ref[k, half:half + 1, :] += jnp.sum(sums[half][k], axis=0, keepdims=True)

    rev = lambda ii: n_i - 1 - ii
    tile = pl.BlockSpec((tm, tn), lambda j, ii: (rev(ii), j))
    pair = pl.BlockSpec((2, tm, tn), lambda j, ii: (0, rev(ii), j))
    per_col = pl.BlockSpec((4, 2, tn), lambda j, ii: (0, 0, j))
    return pl.pallas_call(
        body, name="conv_gelu_bwd", grid=(FF // tn, n_i),
        in_specs=[tile, pair, tile, tile, per_col],
        out_specs=[pair, per_col],
        out_shape=[jax.ShapeDtypeStruct((2, s, FF), BF16), jax.ShapeDtypeStruct((4, 2, FF), F32)],
        scratch_shapes=[pltpu.VMEM((2, 8, tn), F32)],
        compiler_params=_cp(("parallel", "arbitrary")),
    )(da, up, g, a1, cwb)


def _dh1_ln1_bwd(dz2, dup, w_up, z1, ln1_g, tm=512):
    s = dz2.shape[0]

    def body(dz2_ref, dup_ref, w_ref, z1_ref, g_ref, dz1_ref, dz1b_ref, st_ref):
        @pl.when(pl.program_id(0) == 0)
        def _():
            st_ref[...] = jnp.zeros_like(st_ref)

        dh = ALPHA * dz2_ref[...] + _nt(dup_ref[0], w_ref[:, :FF]) + _nt(dup_ref[1], w_ref[:, FF:])
        zh, rstd = _layer_norm_stats(z1_ref[...])
        st_ref[0:1, :] += jnp.sum(dh * zh, axis=0, keepdims=True)
        st_ref[1:2, :] += jnp.sum(dh, axis=0, keepdims=True)
        dz = _layer_norm_bwd(dh, zh, rstd, g_ref[...])
        dz1_ref[...] = dz
        dz1b_ref[...] = dz.astype(BF16)

    td = pl.BlockSpec((tm, D), lambda i: (i, 0))
    return pl.pallas_call(
        body, name="dh1_ln1_bwd", grid=(s // tm,),
        in_specs=[td, pl.BlockSpec((2, tm, FF), lambda i: (0, i, 0)), _resident((D, 2 * FF)), td, _const((1, D))],
        out_specs=[td, td, _const((8, D))],
        out_shape=[jax.ShapeDtypeStruct((s, D), F32), jax.ShapeDtypeStruct((s, D), BF16),
                   jax.ShapeDtypeStruct((8, D), F32)],
        compiler_params=_cp(("arbitrary",), 58),
    )(dz2, dup, w_up, z1, _row(ln1_g))


def _dcat_rms_bwd(dz1b, w_o, o_a, o_b, norm_a_g, norm_b_g, tm=512):
    s = dz1b.shape[0]

    def body(dz_ref, w_ref, oa_ref, ob_ref, ga_ref, gb_ref, da_ref, db_ref, st_ref):
        @pl.when(pl.program_id(0) == 0)
        def _():
            st_ref[...] = jnp.zeros_like(st_ref)

        dcat = _nt(dz_ref[...], w_ref[...])
        for k, (o_ref, g_ref, d_ref) in enumerate(((oa_ref, ga_ref, da_ref), (ob_ref, gb_ref, db_ref))):
            o = jnp.concatenate([o_ref[j] for j in range(4)], axis=1)
            dn = dcat[:, 512 * k:512 * (k + 1)]
            rr = _rms(o)
            oh = o * rr
            st_ref[k:k + 1, :] += jnp.sum(dn * oh, axis=0, keepdims=True)
            doh = dn * g_ref[...]
            d_o = rr * (doh - oh * jnp.mean(doh * oh, axis=-1, keepdims=True))
            for j in range(4):
                d_ref[j] = d_o[:, 128 * j:128 * (j + 1)]

    t512 = pl.BlockSpec((4, tm, 128), lambda i: (0, i, 0))
    return pl.pallas_call(
        body, name="dcat_rms_bwd", grid=(s // tm,),
        in_specs=[pl.BlockSpec((tm, D), lambda i: (i, 0)), _resident((D, D)), t512, t512,
                  _const((1, 512)), _const((1, 512))],
        out_specs=[t512, t512, _const((8, 512))],
        out_shape=[jax.ShapeDtypeStruct((4, s, 128), F32), jax.ShapeDtypeStruct((4, s, 128), F32),
                   jax.ShapeDtypeStruct((8, 512), F32)],
        compiler_params=_cp(("arbitrary",)),
    )(dz1b, w_o, o_a, o_b, _row(norm_a_g), _row(norm_b_g))


def _dproj_combine(dqa, dka, dva, dqkv_b, tm=256):
    s = dka.shape[0]

    def body(qa, ka, va, qb, kb, vb, o_ref):
        for j in range(4):
            o_ref[:, 128 * j:128 * (j + 1)] = qa[j]
            o_ref[:, 768 + 128 * j:768 + 128 * (j + 1)] = qb[j]
            o_ref[:, 1280 + 128 * j:1280 + 128 * (j + 1)] = kb[j]
            o_ref[:, 1792 + 128 * j:1792 + 128 * (j + 1)] = vb[j]
        o_ref[:, 512:640] = ka[...]
        o_ref[:, 640:768] = va[...]

    t512 = pl.BlockSpec((4, tm, 128), lambda i: (0, i, 0))
    t128 = pl.BlockSpec((tm, 128), lambda i: (i, 0))
    return pl.pallas_call(
        body, name="dproj_combine", grid=(s // tm,),
        in_specs=[t512, t128, t128] + [t512] * 3,
        out_specs=pl.BlockSpec((tm, WIN), lambda i: (i, 0)),
        out_shape=jax.ShapeDtypeStruct((s, WIN), BF16),
        compiler_params=_cp(("parallel",)),
    )(dqa, dka, dva, *dqkv_b)


def _grad_x(dz1, dproj, w_in_t, zero, tm=512):
    s = dz1.shape[0]

    def body(dz_ref, dp_ref, w_ref, z_ref, o_ref):
        o_ref[...] = ALPHA * dz_ref[...] + _nn(dp_ref[...], w_ref[...]) + z_ref[0:1, 0:1]

    td = pl.BlockSpec((tm, D), lambda i: (i, 0))
    return pl.pallas_call(
        body, name="grad_x", grid=(s // tm,),
        in_specs=[td, pl.BlockSpec((tm, WIN), lambda i: (i, 0)), _resident((WIN, D)), _const((8, 128))],
        out_specs=td, out_shape=jax.ShapeDtypeStruct((s, D), F32),
        compiler_params=_cp(("parallel",)),
    )(dz1, dproj, w_in_t, zero)


def _place():
    return lax.axis_index("x"), lax.axis_index("y"), lax.axis_index("c")


def _other_chips(x, y):
    return [(1 - x, y), (x, 1 - y), (1 - x, 1 - y)]


def _hbm(a):
    return pltpu.with_memory_space_constraint(a, pltpu.HBM)


def _gather_w_in(shard, conv_w):
    rows_k = shard.shape[0]
    half = rows_k // 2

    def body(src, conv_src, out, conv_out, send_sems, recv_sems):
        x, y, c = _place()
        b = 2 * x + y
        sibling = (x, y, 1 - c)
        chips = _other_chips(x, y)

        def copy(idx, chip_b, core, to, first_hop=False):
            rows = out.at[pl.ds(pl.multiple_of(chip_b * rows_k + core * half, 16), half)]
            s_ref = src.at[pl.ds(pl.multiple_of(core * half, 16), half)] if first_hop else rows
            return pltpu.make_async_remote_copy(src_ref=s_ref, dst_ref=rows, send_sem=send_sems.at[idx],
                                                recv_sem=recv_sems.at[idx], device_id=to, device_id_type=MESH)

        def own_copy():
            return pltpu.make_async_remote_copy(
                src_ref=src, dst_ref=out.at[pl.ds(pl.multiple_of(b * rows_k, 16), rows_k)], send_sem=send_sems.at[6],
                recv_sem=recv_sems.at[6], device_id=sibling, device_id_type=MESH)

        def conv_copy(idx, chip_b, to):
            return pltpu.make_async_remote_copy(src_ref=conv_src, dst_ref=conv_out.at[chip_b],
                                                send_sem=send_sems.at[7 + idx], recv_sem=recv_sems.at[7 + idx],
                                                device_id=to, device_id_type=MESH)

        started = [own_copy(), conv_copy(3, b, sibling)]
        for jn, chip in enumerate(chips):
            started += [copy(jn, b, c, (chip[0], chip[1], c), first_hop=True), conv_copy(jn, b, (chip[0], chip[1], c))]
        for cp in started:
            cp.start()
        for jn, chip in enumerate(chips):
            cb = 2 * chip[0] + chip[1]
            copy(jn, cb, c, (chip[0], chip[1], c)).wait_recv()
            cp = copy(3 + jn, cb, c, sibling)
            cp.start()
            started.append(cp)
        for jn, chip in enumerate(chips):
            cb = 2 * chip[0] + chip[1]
            copy(3 + jn, cb, 1 - c, sibling).wait_recv()
            conv_copy(jn, cb, (chip[0], chip[1], c)).wait_recv()
        own_copy().wait_recv()
        conv_copy(3, b, sibling).wait_recv()
        for cp in started:
            cp.wait_send()

    return pl.pallas_call(
        body, name="gather_w_in",
        in_specs=[ANY, ANY], out_specs=[ANY, ANY],
        out_shape=[jax.ShapeDtypeStruct((N_CHIPS * rows_k, D), BF16), jax.ShapeDtypeStruct((N_CHIPS,) + conv_w.shape, F32)],
        scratch_shapes=[pltpu.SemaphoreType.DMA((11,)), pltpu.SemaphoreType.DMA((11,))],
        compiler_params=pltpu.CompilerParams(has_side_effects=True),
    )(shard, conv_w)


def _weight_copies(shard, land, send_sems, recv_sems, arrivals):
    x, y, c = _place()
    n_rows, n_cols = shard.shape
    peers = [(px, py, c) for px, py in _other_chips(x, y)] + [(x, y, 1 - c)]
    cps = []
    for jn, peer in enumerate(peers):
        at = 2 * peer[0] + peer[1] if arrivals else 2 * x + y
        if land.shape[1] == n_cols:
            dst = land.at[pl.ds(pl.multiple_of(at * n_rows, 16), n_rows)]
        else:
            dst = land.at[:, pl.ds(pl.multiple_of(at * n_cols, 128), n_cols)]
        cps.append(pltpu.make_async_remote_copy(src_ref=shard, dst_ref=dst, send_sem=send_sems.at[jn],
                                                recv_sem=recv_sems.at[jn], device_id=peer, device_id_type=MESH))
    return cps


def _weights_start(shards, after):
    n = len(shards)
    lands = [lax.empty((N_CHIPS * sh.shape[0], D) if sh.shape[1] == D else (D, N_CHIPS * sh.shape[1]), BF16)
             for sh in shards]

    def body(*refs):
        src, land = refs[:n], refs[n:2 * n]
        send_sems, recv_sems = refs[2 * n + 1:3 * n + 1], refs[3 * n + 1:4 * n + 1]
        for k in range(n):
            for send in _weight_copies(src[k], land[k], send_sems[k], recv_sems[k], False):
                send.start()
        refs[-1][...] = jnp.zeros_like(refs[-1])

    res = pl.pallas_call(
        body, name="weights_start",
        in_specs=[HBM] * (2 * n) + [ANY], out_specs=[SEM] * (2 * n) + [HBM] * (2 * n) + [VMEM],
        out_shape=[pltpu.SemaphoreType.DMA((4,))] * (2 * n)
        + [pltpu.HBM(a.shape, a.dtype) for a in (*shards, *lands)] + [jax.ShapeDtypeStruct((8, 128), F32)],
        input_output_aliases={i: i + 2 * n for i in range(2 * n)},
        compiler_params=pltpu.CompilerParams(has_side_effects=DATAFLOW),
    )(*[_hbm(a) for a in (*shards, *lands)], after)
    return [(res[k], res[n + k], res[2 * n + k], res[3 * n + k]) for k in range(n)], res[-1]


def _weights_wait(started, after, name):
    send_sems, recv_sems, shard, land = started

    def body(s_ref, l_ref, send_ref, recv_ref, after_ref, s_out, l_out):
        for cp in _weight_copies(s_ref, l_ref, send_ref, recv_ref, True):
            cp.wait_send()
            cp.wait_recv()

    return pl.pallas_call(
        body, name=name,
        in_specs=[HBM, HBM, SEM, SEM, ANY], out_specs=[HBM, HBM],
        out_shape=[pltpu.HBM(shard.shape, shard.dtype), pltpu.HBM(land.shape, land.dtype)],
        input_output_aliases={0: 0, 1: 1},
        compiler_params=pltpu.CompilerParams(has_side_effects=DATAFLOW),
    )(shard, land, send_sems, recv_sems, after)[1]


def _grad_copies(g_ref, land_ref, send_sems, recv_sems):
    x, y, c = _place()
    cps = []
    for d in range(1, 8):
        px, py, pc = x ^ (d >> 2), y ^ ((d >> 1) & 1), c ^ (d & 1)
        cps.append(pltpu.make_async_remote_copy(
            src_ref=g_ref.at[2 * px + py, pc], dst_ref=land_ref.at[d - 1], send_sem=send_sems.at[d - 1],
            recv_sem=recv_sems.at[d - 1], device_id=(px, py, pc), device_id_type=MESH))
    return cps


def _grads_start(grads_b, name):
    n = len(grads_b)
    lands = [lax.empty((7, g.shape[2], D), BF16) for g in grads_b]

    def body(*refs):
        g, land = refs[:n], refs[n:2 * n]
        send_sems, recv_sems = refs[2 * n:3 * n], refs[3 * n:4 * n]
        for k in range(n):
            for cp in _grad_copies(g[k], land[k], send_sems[k], recv_sems[k]):
                cp.start()
        refs[-1][...] = jnp.zeros_like(refs[-1])

    res = pl.pallas_call(
        body, name=name,
        in_specs=[HBM] * (2 * n), out_specs=[SEM] * (2 * n) + [HBM] * (2 * n) + [VMEM],
        out_shape=[pltpu.SemaphoreType.DMA((7,))] * (2 * n)
        + [pltpu.HBM(a.shape, a.dtype) for a in (*grads_b, *lands)] + [jax.ShapeDtypeStruct((8, 128), F32)],
        input_output_aliases={i: i + 2 * n for i in range(2 * n)},
        compiler_params=pltpu.CompilerParams(has_side_effects=DATAFLOW),
    )(*[_hbm(a) for a in (*grads_b, *lands)])
    return [(res[k], res[n + k], res[2 * n + k], res[3 * n + k]) for k in range(n)], res[-1]


def _grads_wait(started, after, name):
    n = len(started)

    def body(*refs):
        g, land = refs[:n], refs[n:2 * n]
        send_sems, recv_sems = refs[2 * n:3 * n], refs[3 * n:4 * n]
        for k in range(n):
            for cp in _grad_copies(g[k], land[k], send_sems[k], recv_sems[k]):
                cp.wait_send()
                cp.wait_recv()

    gs = [st[2] for st in started]
    lands = [st[3] for st in started]
    res = pl.pallas_call(
        body, name=name,
        in_specs=[HBM] * (2 * n) + [SEM] * (2 * n) + [ANY], out_specs=[HBM] * (2 * n),
        out_shape=[pltpu.HBM(a.shape, a.dtype) for a in (*gs, *lands)],
        input_output_aliases={i: i for i in range(2 * n)},
        compiler_params=pltpu.CompilerParams(has_side_effects=DATAFLOW),
    )(*gs, *lands, *[st[0] for st in started], *[st[1] for st in started], after)
    return res[n:]


def _sum_partials(grad4, got, cb, name, tr):
    h = grad4.shape[2]
    per_half = h // tr

    def body(cb_ref, g_ref, o_ref, out_ref):
        acc = g_ref[...]
        for j in range(7):
            acc = acc + o_ref[j].astype(F32)
        out_ref[...] = acc

    return pl.pallas_call(
        body, name=name,
        grid_spec=pltpu.PrefetchScalarGridSpec(
            num_scalar_prefetch=1, grid=(per_half,),
            in_specs=[pl.BlockSpec((None, None, tr, D), lambda i, cb_ref: (cb_ref[1], cb_ref[0], i, 0)),
                      pl.BlockSpec((7, tr, D), lambda i, cb_ref: (0, i, 0))],
            out_specs=pl.BlockSpec((tr, D), lambda i, cb_ref: (cb_ref[0] * per_half + i, 0))),
        out_shape=jax.ShapeDtypeStruct((2 * h, D), F32),
        compiler_params=_cp(("arbitrary",)),
    )(cb, grad4, got)


def _swap_halves(shards, name):
    n = len(shards)

    def body(*refs):
        out, send_sems, recv_sems = refs[n:2 * n], refs[2 * n], refs[2 * n + 1]
        x, y, c = _place()
        cps = []
        for k in range(n):
            h = shards[k].shape[0] // 2
            mine = out[k].at[pl.ds(pl.multiple_of(c * h, 8), h)]
            cp = pltpu.make_async_remote_copy(src_ref=mine, dst_ref=mine, send_sem=send_sems.at[k],
                                              recv_sem=recv_sems.at[k], device_id=(x, y, 1 - c), device_id_type=MESH)
            cp.start()
            cps.append(cp)
        for cp in cps:
            cp.wait()

    return pl.pallas_call(
        body, name=name,
        in_specs=[ANY] * n, out_specs=[ANY] * n,
        out_shape=[jax.ShapeDtypeStruct(sh.shape, F32) for sh in shards],
        input_output_aliases={k: k for k in range(n)},
        scratch_shapes=[pltpu.SemaphoreType.DMA((n,)), pltpu.SemaphoreType.DMA((n,))],
        compiler_params=pltpu.CompilerParams(has_side_effects=True),
    )(*shards)


def _small_copies(small_ref, land_ref, send_sems, recv_sems):
    x, y, c = _place()
    me = 4 * x + 2 * y + c
    cps = []
    for d in range(1, 8):
        px, py, pc = x ^ (d >> 2), y ^ ((d >> 1) & 1), c ^ (d & 1)
        cps.append(pltpu.make_async_remote_copy(
            src_ref=small_ref, dst_ref=land_ref.at[me], send_sem=send_sems.at[d - 1], recv_sem=recv_sems.at[d - 1],
            device_id=(px, py, pc), device_id_type=MESH))
    return cps


def _small_start(small):
    land = lax.empty((8,) + small.shape, F32)

    def body(s_ref, l_ref, send_sems, recv_sems, s_thru, l_thru, token):
        for cp in _small_copies(s_ref, l_ref, send_sems, recv_sems):
            cp.start()
        token[...] = jnp.zeros_like(token)

    res = pl.pallas_call(
        body, name="small_start",
        in_specs=[HBM, HBM], out_specs=[SEM, SEM, HBM, HBM, VMEM],
        out_shape=[pltpu.SemaphoreType.DMA((7,)), pltpu.SemaphoreType.DMA((7,)), pltpu.HBM(small.shape, F32),
                   pltpu.HBM(land.shape, F32), jax.ShapeDtypeStruct((8, 128), F32)],
        input_output_aliases={0: 2, 1: 3},
        compiler_params=pltpu.CompilerParams(has_side_effects=DATAFLOW),
    )(_hbm(small), _hbm(land))
    return res[:4], res[4]


def _small_wait(started, after):
    send_sems, recv_sems, small, land = started

    def body(s_ref, l_ref, send_ref, recv_ref, after_ref, s_out, l_out):
        for cp in _small_copies(s_ref, l_ref, send_ref, recv_ref):
            cp.wait_send()
            cp.wait_recv()

    return pl.pallas_call(
        body, name="small_wait",
        in_specs=[HBM, HBM, SEM, SEM, ANY], out_specs=[HBM, HBM],
        out_shape=[pltpu.HBM(small.shape, F32), pltpu.HBM(land.shape, F32)],
        input_output_aliases={0: 0, 1: 1},
        compiler_params=pltpu.CompilerParams(has_side_effects=DATAFLOW),
    )(small, land, send_sems, recv_sems, after)


def _small_sum(small, land, me):
    rows = small.shape[0]

    def body(me_ref, s_ref, l_ref, o_ref):
        acc = None
        for k in range(8):
            term = jnp.where(me_ref[0] == k, s_ref[...], l_ref[k])
            acc = term if k == 0 else acc + term
        o_ref[...] = acc

    return pl.pallas_call(
        body, name="small_sum",
        in_specs=[SMEM, VMEM, VMEM], out_specs=VMEM,
        out_shape=jax.ShapeDtypeStruct((rows, D), F32),
    )(me, small, land)


def _adamw(w, g, m, v, name, tr):
    rows, cols = w.shape

    def body(w_ref, g_ref, m_ref, v_ref, d_ref, nm_ref, nv_ref):
        g_ = g_ref[...]
        nm = ADAM_B1 * m_ref[...] + (1.0 - ADAM_B1) * g_
        nv = ADAM_B2 * v_ref[...] + (1.0 - ADAM_B2) * (g_ * g_)
        m_hat = nm / (1.0 - ADAM_B1 ** ADAM_STEP)
        v_hat = nv / (1.0 - ADAM_B2 ** ADAM_STEP)
        d_ref[...] = -ADAM_LR * (m_hat / (jnp.sqrt(v_hat) + ADAM_EPS) + ADAM_WD * w_ref[...])
        nm_ref[...] = nm
        nv_ref[...] = nv

    spec = pl.BlockSpec((tr, cols), lambda i: (i, 0))
    return pl.pallas_call(
        body, name=name, grid=(rows // tr,),
        in_specs=[spec] * 4, out_specs=[spec] * 3,
        out_shape=[jax.ShapeDtypeStruct((rows, cols), F32)] * 3,
        compiler_params=_cp(("parallel",)),
    )(w, g, m, v)


def _local_step(x, target, w_in_t, late_weights, norm_a_g, norm_b_g, sinks_a, ln1_g, ln1_b,
                conv_w, conv_b, ln2_g, ln2_b, slopes, on_grad, on_small):
    cwb = jnp.concatenate([conv_w, conv_b[None]], axis=0).reshape(4, 2, FF)

    proj, xb = _proj(x, w_in_t, "proj")
    o_a, lse_a = _attn_a_fwd(proj, sinks_a)
    fwd_b = [_attn_b_fwd(proj, slopes, r) for r in B_DILATIONS]
    w_o = late_weights(1, fwd_b[-1][1])
    o_b, lse_b, cat, z1, h1, h1b = _mix_ln1(x, o_a, [f[0] for f in fwd_b], [f[1] for f in fwd_b],
                                           norm_a_g, norm_b_g, w_o, ln1_g, ln1_b)
    w_up = late_weights(2, h1b)
    up = _up_proj(h1b, w_up)
    a, gate, a1 = _conv_gelu(up, cwb)
    w_down = late_weights(3, a)
    dz2, dz2b, st2 = _down_ln2_loss(a, w_down, h1, target, ln2_g, ln2_b)

    on_grad(3, *_grad_w(a, dz2b, "grad_w_down", tm=FF // 2))
    dup, dconv = _conv_gelu_bwd(_d_act(dz2b, w_down), up, gate, a1, cwb)
    on_grad(2, *_grad_w(dup, h1b, "grad_w_up", tm=FF // 2, lhs_halves=True))
    dz1, dz1b, st1 = _dh1_ln1_bwd(dz2, dup, w_up, z1, ln1_g)
    tok = on_grad(1, *_grad_w(cat, dz1b, "grad_w_o", tm=512))
    d_oa, d_ob, st_n = _dcat_rms_bwd(dz1b, w_o, o_a, o_b, norm_a_g + tok[0, 0], norm_b_g)
    dqa, dka, dva, dsink = _attn_a_bwd(proj, sinks_a, d_oa, o_a, lse_a)
    dconv = dconv.reshape(4, 2 * FF)
    tok = on_small(dict(loss=st2[2, 0:1], norm_a_g=st_n[0], norm_b_g=st_n[1], sinks_a=dsink[:, 0],
                        ln1_g=st1[0], ln1_b=st1[1], conv_w=dconv[0:3].reshape(-1), conv_b=dconv[3],
                        ln2_g=st2[0], ln2_b=st2[1]))
    slopes = slopes + tok[0, 0]
    bwd_b = None
    for r in reversed(B_DILATIONS):
        bwd_b = _attn_b_bwd(proj, slopes, d_ob, o_b, lse_b, r, bwd_b, BF16 if r == 1 else F32)
    dproj = _dproj_combine(dqa, dka, dva, bwd_b)
    tok = on_grad(0, *_grad_w(dproj, xb, "grad_w_in", tm=WA))
    return _grad_x(dz1, dproj, w_in_t, tok)


SMALL_ORDER = ("loss", "norm_a_g", "norm_b_g", "sinks_a", "ln1_g", "ln1_b", "conv_b", "ln2_g", "ln2_b", "conv_w")
SMALL_SIZES = dict(loss=1, norm_a_g=512, norm_b_g=512, sinks_a=8, ln1_g=D, ln1_b=D, conv_b=2 * FF, ln2_g=D, ln2_b=D,
                   conv_w=3 * 2 * FF)


def _pack(parts, rows):
    flat = jnp.concatenate([parts[k].reshape(-1).astype(F32) for k in parts])
    return jnp.pad(flat, (0, rows * D - flat.shape[0])).reshape(rows, D)


def _unpack(buf, names, sizes):
    flat = buf.reshape(-1)
    out, at = {}, 0
    for k in names:
        out[k] = flat[at:at + sizes[k]]
        at += sizes[k]
    return out


def kernel(x, w_in, norm_a_g, norm_b_g, sinks_a, w_o, ln1_g, ln1_b, w_up, conv_w, conv_b, w_down, ln2_g, ln2_b, loss_target, m_w_in, m_norm_a_g, m_norm_b_g, m_sinks_a, m_w_o, m_ln1_g, m_ln1_b, m_w_up, m_conv_w, m_conv_b, m_w_down, m_ln2_g, m_ln2_b, v_w_in, v_norm_a_g, v_norm_b_g, v_sinks_a, v_w_o, v_ln1_g, v_ln1_b, v_w_up, v_conv_w, v_conv_b, v_w_down, v_ln2_g, v_ln2_b):
    xi, yi, ci = _place()
    chip = (2 * xi + yi).astype(I32)
    core = ci.astype(I32)

    w_in_rows, m_w_in_rows, v_w_in_rows = w_in.T, m_w_in.T, v_w_in.T
    shards = (w_in_rows.astype(BF16), w_o.astype(BF16), w_up.astype(BF16), w_down.astype(BF16))
    w_in_t, conv_w4 = _gather_w_in(shards[0], conv_w)
    conv_w_f = conv_w4.transpose(1, 0, 2).reshape(3, 2 * FF)
    w_started, w_tok = _weights_start(shards[1:], conv_w4)
    slopes = jnp.asarray(SLOPES, F32) + w_tok[0, 0]

    halves_rows = [r // 2 for r in SHARD_ROWS]
    grads4, grads_b4, started = [None] * 4, [None] * 4, [None] * 4

    def on_grad(k, g, g_b):
        grads4[k] = g.reshape(N_CHIPS, 2, halves_rows[k], D)
        grads_b4[k] = g_b.reshape(N_CHIPS, 2, halves_rows[k], D)
        if k > 1:
            return None
        group = (1, 2, 3) if k == 1 else (0,)
        sts, tok = _grads_start([grads_b4[i] for i in group], f"grads_start_{k}")
        for i, st in zip(group, sts):
            started[i] = st
        return tok

    small_rows = 32
    small_started = []

    def on_small(parts):
        st, tok = _small_start(_pack({k: parts[k] for k in SMALL_ORDER}, small_rows))
        small_started.append(st)
        return tok

    gx = _local_step(
        x[0], loss_target[0], w_in_t, lambda k, after: _weights_wait(w_started[k - 1], after, f"weights_wait_{k}"),
        norm_a_g, norm_b_g, sinks_a, ln1_g, ln1_b, conv_w_f, conv_b, ln2_g, ln2_b, slopes, on_grad, on_small)

    tiles = (96, 128, 352, 176)
    core_chip = jnp.stack([core, chip])
    got = _grads_wait(started[1:], gx, "grads_wait_1")
    halves = [_sum_partials(grads4[k], got[k - 1], core_chip, f"sum_partials_{k}", tiles[k]) for k in (1, 2, 3)]
    g_w_o, g_w_up_rows, g_w_down = _swap_halves(halves, "swap_halves")
    g_w_up = g_w_up_rows.T
    delta, new_m, new_v = {}, {}, {}
    for k, g, tr in (("w_o", g_w_o, 128), ("w_up", g_w_up, 256), ("w_down", g_w_down, 176)):
        delta[k], new_m[k], new_v[k] = _adamw(dict(w_o=w_o, w_up=w_up, w_down=w_down)[k], g,
                                              dict(w_o=m_w_o, w_up=m_w_up, w_down=m_w_down)[k],
                                              dict(w_o=v_w_o, w_up=v_w_up, w_down=v_w_down)[k], f"adamw_{k}", tr)

    got = _grads_wait(started[:1], delta["w_up"], "grads_wait_0")
    half_in = _sum_partials(grads4[0], got[0], core_chip, "sum_partials_0", tiles[0])
    (g_w_in_rows,) = _swap_halves([half_in], "swap_halves_in")
    small_mine, small_land = _small_wait(small_started[0], g_w_in_rows)
    totals = _small_sum(small_mine, small_land, (4 * xi + 2 * yi + ci).astype(I32).reshape(1))
    tot = _unpack(totals, SMALL_ORDER, SMALL_SIZES)
    loss = tot["loss"][0]
    cols = 2 * FF // N_CHIPS
    g_conv_w = lax.dynamic_slice(tot["conv_w"].reshape(3, 2 * FF), (0, chip * cols), (3, cols))
    g_small = dict(norm_a_g=tot["norm_a_g"], norm_b_g=tot["norm_b_g"], sinks_a=tot["sinks_a"], ln1_g=tot["ln1_g"],
                   ln1_b=tot["ln1_b"], conv_w=g_conv_w, conv_b=tot["conv_b"], ln2_g=tot["ln2_g"], ln2_b=tot["ln2_b"])

    weights = dict(w_in=w_in, norm_a_g=norm_a_g, norm_b_g=norm_b_g, sinks_a=sinks_a, w_o=w_o, ln1_g=ln1_g, ln1_b=ln1_b,
                   w_up=w_up, conv_w=conv_w, conv_b=conv_b, w_down=w_down, ln2_g=ln2_g, ln2_b=ln2_b)
    ms = dict(w_in=m_w_in, norm_a_g=m_norm_a_g, norm_b_g=m_norm_b_g, sinks_a=m_sinks_a, w_o=m_w_o, ln1_g=m_ln1_g,
              ln1_b=m_ln1_b, w_up=m_w_up, conv_w=m_conv_w, conv_b=m_conv_b, w_down=m_w_down, ln2_g=m_ln2_g, ln2_b=m_ln2_b)
    vs = dict(w_in=v_w_in, norm_a_g=v_norm_a_g, norm_b_g=v_norm_b_g, sinks_a=v_sinks_a, w_o=v_w_o, ln1_g=v_ln1_g,
              ln1_b=v_ln1_b, w_up=v_w_up, conv_w=v_conv_w, conv_b=v_conv_b, w_down=v_w_down, ln2_g=v_ln2_g, ln2_b=v_ln2_b)
    order = list(weights)
    grad = dict(g_small, w_in=g_w_in_rows.T, w_o=g_w_o, w_up=g_w_up, w_down=g_w_down)

    delta["w_in"], new_m["w_in"], new_v["w_in"] = [
        a.T for a in _adamw(w_in_rows, g_w_in_rows, m_w_in_rows, v_w_in_rows, "adamw_w_in", 144)]
    small_names = [k for k in order if k not in delta]
    sizes = {k: weights[k].size for k in small_names}
    rows = 16
    packed = [_pack({k: src[k] for k in small_names}, rows) for src in (weights, grad, ms, vs)]
    for res, buf in zip((delta, new_m, new_v), _adamw(*packed, "adamw_small", rows)):
        for k, val in _unpack(buf, small_names, sizes).items():
            res[k] = val.reshape(weights[k].shape)

    return (loss, gx[None], *[grad[k] for k in order], *[delta[k] for k in order],
            *[new_m[k] for k in order], *[new_v[k] for k in order])
```

```python
import functools
import math

import jax
import jax.numpy as jnp
from jax import lax
from jax.experimental import pallas as pl
from jax.experimental.pallas import tpu as pltpu

F32, BF16, I32 = jnp.float32, jnp.bfloat16, jnp.int32

D = 1024
FF = 2816
HD = 64
NH = 8
WA, WB = 768, 1536
WIN = WA + WB
BLK = 128
ALPHA = 2.0 ** 0.25
LN_EPS, RMS_EPS = 1e-5, 1e-6
SCALE = 1.0 / math.sqrt(HD)
A_MAX_DIST, B_MAX_DIST = 127, 128
B_DILATIONS = (1, 4, 16)
SLOPES = tuple(2.0 ** (-(i + 1)) for i in range(NH))
SHARD_ROWS = (WIN // 4, D // 4, 2 * FF // 4, FF // 4)
N_CHIPS = 4
ADAM_LR, ADAM_B1, ADAM_B2, ADAM_EPS, ADAM_WD, ADAM_STEP = 0.001, 0.9, 0.999, 1e-08, 0.01, 10
MESH = pl.DeviceIdType.MESH
ANY = pl.BlockSpec(memory_space=pl.ANY)
SMEM = pl.BlockSpec(memory_space=pltpu.SMEM)
VMEM = pl.BlockSpec(memory_space=pltpu.VMEM)
HBM = pl.BlockSpec(memory_space=pltpu.HBM)
SEM = pl.BlockSpec(memory_space=pltpu.SEMAPHORE)
DATAFLOW = pltpu.SideEffectType.DATAFLOW_SIDE_EFFECTING


def _cp(sem, mb=48):
    return pltpu.CompilerParams(dimension_semantics=sem, vmem_limit_bytes=mb << 20)


def _nn(a, b):
    return lax.dot_general(a, b, (((1,), (0,)), ((), ())), preferred_element_type=F32)


def _nt(a, b):
    return lax.dot_general(a, b, (((1,), (1,)), ((), ())), preferred_element_type=F32)


def _tn(a, b):
    return lax.dot_general(a, b, (((0,), (0,)), ((), ())), preferred_element_type=F32)


def _resident(shape):
    n = len(shape)
    return pl.BlockSpec(shape, lambda *_: (0,) * n, pipeline_mode=pl.Buffered(1))


def _const(shape):
    n = len(shape)
    return pl.BlockSpec(shape, lambda *_: (0,) * n)


def _proj(x, w_t, name, tm=512):
    s = x.shape[0]
    n = w_t.shape[0]

    def body(x_ref, w_ref, o_ref, xb_ref):
        xb = x_ref[...].astype(BF16)
        xb_ref[...] = xb
        res = _nt(xb, w_ref[...])
        for g in range(n // 128):
            o_ref[g] = res[:, 128 * g:128 * (g + 1)]

    return pl.pallas_call(
        body, name=name, grid=(s // tm,),
        in_specs=[pl.BlockSpec((tm, D), lambda i: (i, 0)), _resident((n, D))],
        out_specs=[pl.BlockSpec((n // 128, tm, 128), lambda i: (0, i, 0)), pl.BlockSpec((tm, D), lambda i: (i, 0))],
        out_shape=[jax.ShapeDtypeStruct((n // 128, s, 128), F32), jax.ShapeDtypeStruct((s, D), BF16)],
        compiler_params=_cp(("parallel",)),
    )(x, w_t)


def _grad_w(lhs, rhs, name, tm, tk=2048, lhs_halves=False):
    s = rhs.shape[0]
    if lhs_halves:
        per_half = lhs.shape[2] // tm
        n = 2 * lhs.shape[2]
        lhs_spec = pl.BlockSpec((None, tk, tm), lambda i, k: (i // per_half, k, i % per_half))
    else:
        n = lhs.shape[1]
        lhs_spec = pl.BlockSpec((tk, tm), lambda i, k: (k, i))
    nk = s // tk

    def body(l_ref, r_ref, o_ref, ob_ref):
        k = pl.program_id(1)

        @pl.when(k == 0)
        def _():
            o_ref[...] = jnp.zeros_like(o_ref)

        o_ref[...] += _tn(l_ref[...], r_ref[...])

        @pl.when(k == nk - 1)
        def _():
            ob_ref[...] = o_ref[...].astype(BF16)

    return pl.pallas_call(
        body, name=name, grid=(n // tm, nk),
        in_specs=[lhs_spec, pl.BlockSpec((tk, D), lambda i, k: (k, 0))],
        out_specs=[pl.BlockSpec((tm, D), lambda i, k: (i, 0))] * 2,
        out_shape=[jax.ShapeDtypeStruct((n, D), F32), jax.ShapeDtypeStruct((n, D), BF16)],
        compiler_params=_cp(("parallel", "arbitrary")),
    )(lhs, rhs)


def _band_base(max_dist, dist_unit, first):
    row = lax.broadcasted_iota(I32, (BLK, 2 * BLK), 0)
    col = lax.broadcasted_iota(I32, (BLK, 2 * BLK), 1)
    dist = BLK + row - col
    ok = (dist >= 0) & (dist <= max_dist)
    if first:
        ok = ok & (col >= BLK)
    return jnp.where(ok, dist.astype(F32) * (-float(dist_unit)), -jnp.inf)


def _half_mask(shape, e):
    lane = lax.broadcasted_iota(I32, shape, 1)
    return (lane < HD) if e == 0 else (lane >= HD)


def _to_half(x, e, g):
    if g != e:
        x = pltpu.roll(x, HD, 1)
    return jnp.where(_half_mask(x.shape, g), x, 0.0)


def _stack_heads(scalars, tile):
    return jnp.concatenate([scalars[0] * tile, scalars[1] * tile], axis=0)


def _pair_fwd(q2, kb, vb, base, slopes, kv_heads, sinks):
    lo = _half_mask((BLK, 2 * HD), 0)
    if slopes is None:
        bias = base
    elif sinks is None:
        bias = _stack_heads(slopes, base)
    else:
        col0 = lax.broadcasted_iota(I32, base.shape, 1) == 0
        bias = jnp.concatenate([jnp.where(col0, sinks[e], slopes[e] * base) for e in (0, 1)], axis=0)
    qs = jnp.concatenate([_to_half(q2, e, kv_heads[e]) * SCALE for e in (0, 1)], axis=0).astype(BF16)
    s = _nt(qs, kb) + bias
    m = jnp.max(s, axis=1, keepdims=True)
    p = jnp.exp(s - m)
    l = jnp.sum(p, axis=1, keepdims=True)
    o = _nn(p.astype(BF16), vb) / l
    lse = m + jnp.log(l)
    halves = []
    for e in (0, 1):
        oh = o[e * BLK:(e + 1) * BLK]
        halves.append(pltpu.roll(oh, HD, 1) if kv_heads[e] != e else oh)
    o2 = jnp.where(lo, halves[0], halves[1])
    lse2 = jnp.where(lo, jnp.broadcast_to(lse[:BLK], (BLK, 2 * HD)), jnp.broadcast_to(lse[BLK:], (BLK, 2 * HD)))
    return o2, lse2


def _pair_bwd(q2, kb, vb, do2, o2, lse2, base, slopes, kv_heads, sinks):
    lo = _half_mask((BLK, 2 * HD), 0)
    prod = do2 * o2
    lses, deltas = [], []
    for e in (0, 1):
        hq = _half_mask((BLK, 2 * HD), e)
        lses.append(jnp.max(jnp.where(hq, lse2, -jnp.inf), axis=1, keepdims=True))
        deltas.append(jnp.sum(jnp.where(hq, prod, 0.0), axis=1, keepdims=True))
    lse = jnp.concatenate(lses, axis=0)
    delta = jnp.concatenate(deltas, axis=0)
    qs = jnp.concatenate([_to_half(q2, e, kv_heads[e]) * SCALE for e in (0, 1)], axis=0).astype(BF16)
    dos = jnp.concatenate([_to_half(do2, e, kv_heads[e]) for e in (0, 1)], axis=0).astype(BF16)
    p = jnp.exp(_nt(qs, kb) + (base if slopes is None else _stack_heads(slopes, base)) - lse)
    ds = (p * (_nt(dos, vb) - delta)).astype(BF16)
    dq = _nn(ds, kb) * SCALE
    halves = []
    for e in (0, 1):
        dqh = dq[e * BLK:(e + 1) * BLK]
        halves.append(pltpu.roll(dqh, HD, 1) if kv_heads[e] != e else dqh)
    dq2 = jnp.where(lo, halves[0], halves[1])
    dk2 = _tn(ds, qs)
    dv2 = _tn(p.astype(BF16), dos)
    dsinks = []
    if sinks is not None:
        for e in (0, 1):
            dsinks.append(jnp.sum(-jnp.exp(sinks[e] - lses[e]) * deltas[e], axis=0, keepdims=True))
    return dq2, dk2, dv2, dsinks


A_BLOCKS_PER_STEP = 2
A_BLOCKS_PER_STEP_BWD = 1


def _attn_a_fwd(proj, sinks):
    s = proj.shape[1]
    nq = A_BLOCKS_PER_STEP
    rows = BLK * nq
    steps = s // rows

    def body(sink_ref, q_ref, kp_ref, kc_ref, vp_ref, vc_ref, o_ref, lse_ref):
        n = pl.program_id(0)
        base_rest = _band_base(A_MAX_DIST, 1, False)
        base_0 = jnp.where(n > 0, base_rest, _band_base(A_MAX_DIST, 1, True))
        for i in range(nq):
            cur = pl.ds(i * BLK, BLK)
            k_prev = kc_ref[pl.ds((i - 1) * BLK, BLK), :] if i > 0 else kp_ref[...]
            v_prev = vc_ref[pl.ds((i - 1) * BLK, BLK), :] if i > 0 else vp_ref[...]
            first_key = lax.broadcasted_iota(I32, (2 * BLK, 128), 0) == 0
            kb = jnp.where(first_key, 0.0, jnp.concatenate([k_prev, kc_ref[cur, :]], axis=0)).astype(BF16)
            vb = jnp.where(first_key, 0.0, jnp.concatenate([v_prev, vc_ref[cur, :]], axis=0)).astype(BF16)
            for j in range(NH // 2):
                g = j // 2
                o2, lse2 = _pair_fwd(q_ref[j, cur, :], kb, vb, base_rest if i > 0 else base_0,
                                     (SLOPES[2 * j], SLOPES[2 * j + 1]), (g, g), (sink_ref[2 * j], sink_ref[2 * j + 1]))
                o_ref[j, cur, :] = o2
                lse_ref[j, cur, :] = lse2

    before = lambda n: jnp.maximum(n * nq - 1, 0)
    slab = lambda g: pl.BlockSpec((None, rows, 128), lambda n: (g, n, 0))
    edge = lambda g: pl.BlockSpec((None, BLK, 128), lambda n: (g, before(n), 0))
    quad = pl.BlockSpec((4, rows, 128), lambda n: (0, n, 0))
    return pl.pallas_call(
        body, name="attn_a_fwd", grid=(steps,),
        in_specs=[SMEM, quad, edge(4), slab(4), edge(5), slab(5)],
        out_specs=[quad, quad],
        out_shape=[jax.ShapeDtypeStruct((4, s, 128), F32)] * 2,
        compiler_params=_cp(("parallel",)),
    )(sinks, proj, proj, proj, proj, proj)


def _attn_a_bwd(proj, sinks, d_o, o, lse):
    s = proj.shape[1]
    nq = A_BLOCKS_PER_STEP_BWD
    rows = BLK * nq
    steps = s // rows

    def body(sink_ref, q_ref, kp_ref, kc_ref, vp_ref, vc_ref, do_ref, o_ref, lse_ref,
             dq_ref, dk_ref, dv_ref, dsink_ref, kcar, vcar):
        n = pl.program_id(0)

        @pl.when(n == 0)
        def _():
            kcar[...] = jnp.zeros_like(kcar)
            vcar[...] = jnp.zeros_like(vcar)
            dsink_ref[...] = jnp.zeros_like(dsink_ref)

        dk_ref[...] = kcar[...].astype(BF16)
        dv_ref[...] = vcar[...].astype(BF16)

        @pl.when(n < steps)
        def _():
            base_rest = _band_base(A_MAX_DIST, 1, False)
            base_0 = jnp.where(n > 0, base_rest, _band_base(A_MAX_DIST, 1, True))
            for i in range(nq):
                cur = pl.ds(i * BLK, BLK)
                k_prev = kc_ref[pl.ds((i - 1) * BLK, BLK), :] if i > 0 else kp_ref[...]
                v_prev = vc_ref[pl.ds((i - 1) * BLK, BLK), :] if i > 0 else vp_ref[...]
                kb = jnp.concatenate([k_prev, kc_ref[cur, :]], axis=0).astype(BF16)
                vb = jnp.concatenate([v_prev, vc_ref[cur, :]], axis=0).astype(BF16)
                dk_win = dv_win = None
                for j in range(NH // 2):
                    g = j // 2
                    dq2, dk2, dv2, dsk = _pair_bwd(q_ref[j, cur, :], kb, vb, do_ref[j, cur, :], o_ref[j, cur, :],
                                                   lse_ref[j, cur, :], base_rest if i > 0 else base_0,
                                                   (SLOPES[2 * j], SLOPES[2 * j + 1]), (g, g),
                                                   (sink_ref[2 * j], sink_ref[2 * j + 1]))
                    dq_ref[j, cur, :] = dq2.astype(BF16)
                    dk_win = dk2 if j == 0 else dk_win + dk2
                    dv_win = dv2 if j == 0 else dv_win + dv2
                    for e in (0, 1):
                        h = 2 * j + e
                        dsink_ref[h:h + 1, :] += jnp.broadcast_to(dsk[e], (1, 128))
                if i == 0:
                    last = pl.ds((nq - 1) * BLK, BLK)
                    dk_ref[last, :] = (kcar[last, :] + dk_win[:BLK]).astype(BF16)
                    dv_ref[last, :] = (vcar[last, :] + dv_win[:BLK]).astype(BF16)
                else:
                    kcar[pl.ds((i - 1) * BLK, BLK), :] += dk_win[:BLK]
                    vcar[pl.ds((i - 1) * BLK, BLK), :] += dv_win[:BLK]
                kcar[cur, :] = dk_win[BLK:]
                vcar[cur, :] = dv_win[BLK:]

    cur_step = lambda n: jnp.minimum(n, steps - 1)
    before = lambda n: jnp.maximum(cur_step(n) * nq - 1, 0)
    out_prev = lambda n: jnp.maximum(n - 1, 0)
    quad = pl.BlockSpec((4, rows, 128), lambda n: (0, cur_step(n), 0))
    slab = lambda g: pl.BlockSpec((None, rows, 128), lambda n: (g, cur_step(n), 0))
    edge = lambda g: pl.BlockSpec((None, BLK, 128), lambda n: (g, before(n), 0))
    return pl.pallas_call(
        body, name="attn_a_bwd", grid=(steps + 1,),
        in_specs=[SMEM, quad, edge(4), slab(4), edge(5), slab(5), quad, quad, quad],
        out_specs=[quad,
                   pl.BlockSpec((rows, 128), lambda n: (out_prev(n), 0)),
                   pl.BlockSpec((rows, 128), lambda n: (out_prev(n), 0)),
                   pl.BlockSpec((NH, 128), lambda n: (0, 0))],
        out_shape=[jax.ShapeDtypeStruct((4, s, 128), BF16), jax.ShapeDtypeStruct((s, 128), BF16),
                   jax.ShapeDtypeStruct((s, 128), BF16), jax.ShapeDtypeStruct((NH, 128), F32)],
        scratch_shapes=[pltpu.VMEM((rows, 128), F32), pltpu.VMEM((rows, 128), F32)],
        compiler_params=_cp(("arbitrary",)),
    )(sinks, proj, proj, proj, proj, proj, d_o, o, lse)


def _stream(rho, i, r):
    start = i * BLK * r + rho
    return pl.ds(start, BLK, stride=r) if r > 1 else pl.ds(start, BLK)


def _for_streams(r, fn, side_by_side=4):
    if r <= side_by_side:
        for rho in range(r):
            fn(rho)
    else:
        def group(it, carry):
            for u in range(side_by_side):
                fn(side_by_side * it + u)
            return carry

        lax.fori_loop(0, r // side_by_side, group, 0)


B_BLOCKS_PER_STEP = {1: 8, 4: 2, 16: 1}
B_BLOCKS_PER_STEP_FWD = {1: 16, 4: 4, 16: 1}


def _attn_b_fwd(proj, slopes, r):
    s = proj.shape[1]
    nq = B_BLOCKS_PER_STEP_FWD[r]
    rows = BLK * r * nq
    steps = s // rows
    qc, kc, vc = WA // 128, WA // 128 + 4, WA // 128 + 8

    def body(slope_ref, q_ref, kp_ref, kc_ref, vp_ref, vc_ref, o_ref, lse_ref):
        j = pl.program_id(0)
        sb = pl.program_id(1)
        sl2 = (slope_ref[2 * j], slope_ref[2 * j + 1])
        bias_rest = _stack_heads(sl2, _band_base(B_MAX_DIST, r, False))
        bias_0 = jnp.where(sb > 0, bias_rest, _stack_heads(sl2, _band_base(B_MAX_DIST, r, True)))

        def stream(rho):
            for i in range(nq):
                cur = _stream(rho, i, r)
                k_prev = kc_ref[_stream(rho, i - 1, r), :] if i > 0 else kp_ref[_stream(rho, 0, r), :]
                v_prev = vc_ref[_stream(rho, i - 1, r), :] if i > 0 else vp_ref[_stream(rho, 0, r), :]
                kb = jnp.concatenate([k_prev, kc_ref[cur, :]], axis=0).astype(BF16)
                vb = jnp.concatenate([v_prev, vc_ref[cur, :]], axis=0).astype(BF16)
                o2, lse2 = _pair_fwd(q_ref[cur, :], kb, vb, bias_rest if i > 0 else bias_0, None, (0, 1), None)
                o_ref[cur, :] = o2
                lse_ref[cur, :] = lse2

        _for_streams(r, stream, side_by_side=16)

    before = lambda sb: jnp.maximum(sb * nq - 1, 0)
    return pl.pallas_call(
        body, name=f"attn_b_fwd_r{r}", grid=(NH // 2, steps),
        in_specs=[SMEM,
                  pl.BlockSpec((None, rows, 128), lambda j, sb: (qc + j, sb, 0)),
                  pl.BlockSpec((None, BLK * r, 128), lambda j, sb: (kc + j, before(sb), 0)),
                  pl.BlockSpec((None, rows, 128), lambda j, sb: (kc + j, sb, 0)),
                  pl.BlockSpec((None, BLK * r, 128), lambda j, sb: (vc + j, before(sb), 0)),
                  pl.BlockSpec((None, rows, 128), lambda j, sb: (vc + j, sb, 0))],
        out_specs=[pl.BlockSpec((None, rows, 128), lambda j, sb: (j, sb, 0))] * 2,
        out_shape=[jax.ShapeDtypeStruct((4, s, 128), F32)] * 2,
        compiler_params=_cp(("parallel", "parallel")),
    )(slopes, proj, proj, proj, proj, proj)


def _attn_b_bwd(proj, slopes, d_o, o, lse, r, so_far=None, dtype=F32):
    s = proj.shape[1]
    nq = B_BLOCKS_PER_STEP[r]
    rows = BLK * r * nq
    steps = s // rows
    qc, kc, vc = WA // 128, WA // 128 + 4, WA // 128 + 8
    chained = so_far is not None

    def body(slope_ref, q_ref, kp_ref, kc_ref, vp_ref, vc_ref, do_ref, o_ref, lse_ref, *rest):
        pq_ref, pk_ref, pv_ref = rest[:3] if chained else (None, None, None)
        dq_ref, dk_ref, dv_ref, kcar, vcar = rest[-5:]
        j = pl.program_id(0)
        sb = pl.program_id(1)

        @pl.when(sb == 0)
        def _():
            kcar[...] = jnp.zeros_like(kcar)
            vcar[...] = jnp.zeros_like(vcar)

        def settled(car, p_ref, idx):
            return car[idx] + p_ref[idx] if chained else car[idx]

        dk_ref[...] = settled(kcar, pk_ref, ...).astype(dtype)
        dv_ref[...] = settled(vcar, pv_ref, ...).astype(dtype)

        @pl.when(sb < steps)
        def _():
            sl2 = (slope_ref[2 * j], slope_ref[2 * j + 1])
            bias_rest = _stack_heads(sl2, _band_base(B_MAX_DIST, r, False))
            bias_0 = jnp.where(sb > 0, bias_rest, _stack_heads(sl2, _band_base(B_MAX_DIST, r, True)))

            def stream(rho):
                for i in range(nq):
                    cur = _stream(rho, i, r)
                    k_prev = kc_ref[_stream(rho, i - 1, r), :] if i > 0 else kp_ref[_stream(rho, 0, r), :]
                    v_prev = vc_ref[_stream(rho, i - 1, r), :] if i > 0 else vp_ref[_stream(rho, 0, r), :]
                    kb = jnp.concatenate([k_prev, kc_ref[cur, :]], axis=0).astype(BF16)
                    vb = jnp.concatenate([v_prev, vc_ref[cur, :]], axis=0).astype(BF16)
                    dq2, dk2, dv2, _ = _pair_bwd(q_ref[cur, :], kb, vb, do_ref[cur, :], o_ref[cur, :], lse_ref[cur, :],
                                                 bias_rest if i > 0 else bias_0, None, (0, 1), None)
                    dq_ref[cur, :] = (dq2 + pq_ref[cur, :] if chained else dq2).astype(dtype)
                    if i == 0:
                        last = (_stream(rho, nq - 1, r), slice(None))
                        dk_ref[last] = (settled(kcar, pk_ref, last) + dk2[:BLK]).astype(dtype)
                        dv_ref[last] = (settled(vcar, pv_ref, last) + dv2[:BLK]).astype(dtype)
                    else:
                        kcar[_stream(rho, i - 1, r), :] += dk2[:BLK]
                        vcar[_stream(rho, i - 1, r), :] += dv2[:BLK]
                    kcar[cur, :] = dk2[BLK:]
                    vcar[cur, :] = dv2[BLK:]

            _for_streams(r, stream, side_by_side=8)

    cur_step = lambda sb: jnp.minimum(sb, steps - 1)
    before = lambda sb: jnp.maximum(cur_step(sb) * nq - 1, 0)
    out_prev = lambda sb: jnp.maximum(sb - 1, 0)
    tile = lambda slab: pl.BlockSpec((None, rows, 128), lambda j, sb: (slab + j, cur_step(sb), 0))
    edge = lambda slab: pl.BlockSpec((None, BLK * r, 128), lambda j, sb: (slab + j, before(sb), 0))
    late = pl.BlockSpec((None, rows, 128), lambda j, sb: (j, out_prev(sb), 0))
    grads = [tile(0), late, late]
    return pl.pallas_call(
        body, name=f"attn_b_bwd_r{r}", grid=(NH // 2, steps + 1),
        in_specs=[SMEM, tile(qc), edge(kc), tile(kc), edge(vc), tile(vc), tile(0), tile(0), tile(0)]
        + (grads if chained else []),
        out_specs=grads,
        out_shape=[jax.ShapeDtypeStruct((4, s, 128), dtype)] * 3,
        scratch_shapes=[pltpu.VMEM((rows, 128), F32), pltpu.VMEM((rows, 128), F32)],
        compiler_params=_cp(("parallel", "arbitrary")),
    )(slopes, proj, proj, proj, proj, proj, d_o, o, lse, *(so_far if chained else ()))


def _row(v):
    return v.reshape(1, -1)


def _layer_norm_stats(z):
    mu = jnp.mean(z, axis=-1, keepdims=True)
    zc = z - mu
    var = jnp.mean(zc * zc, axis=-1, keepdims=True)
    rstd = lax.rsqrt(var + LN_EPS)
    return zc * rstd, rstd


def _layer_norm_bwd(dh, zh, rstd, g):
    dzh = dh * g
    return rstd * (dzh - jnp.mean(dzh, axis=-1, keepdims=True) - zh * jnp.mean(dzh * zh, axis=-1, keepdims=True))


def _rms(o):
    return lax.rsqrt(jnp.mean(o * o, axis=-1, keepdims=True) + RMS_EPS)


def _mix_ln1(x, o_a, o_b, lse_b, norm_a_g, norm_b_g, w_o, ln1_g, ln1_b, tm=256):
    s = x.shape[0]

    def wide(ref):
        return jnp.concatenate([ref[j] for j in range(4)], axis=1)

    def body(x_ref, oa_ref, ob1, ob2, ob3, l1, l2, l3, ga_ref, gb_ref, wo_ref, g_ref, b_ref,
             obm_ref, lse_ref, cat_ref, z1_ref, h1_ref, h1b_ref):
        la, lb, lc = wide(l1), wide(l2), wide(l3)
        m = jnp.maximum(jnp.maximum(la, lb), lc)
        ea, eb, ec = jnp.exp(la - m), jnp.exp(lb - m), jnp.exp(lc - m)
        den = ea + eb + ec
        obm = (ea / den) * wide(ob1) + (eb / den) * wide(ob2) + (ec / den) * wide(ob3)
        lse = m + jnp.log(den)
        for j in range(4):
            obm_ref[j] = obm[:, 128 * j:128 * (j + 1)]
            lse_ref[j] = lse[:, 128 * j:128 * (j + 1)]
        oa = wide(oa_ref)
        na = oa * _rms(oa) * ga_ref[...]
        nb_ = obm * _rms(obm) * gb_ref[...]
        cat = jnp.concatenate([na, nb_], axis=1).astype(BF16)
        cat_ref[...] = cat
        z1 = ALPHA * x_ref[...] + _nn(cat, wo_ref[...])
        z1_ref[...] = z1
        zh, _ = _layer_norm_stats(z1)
        h1 = zh * g_ref[...] + b_ref[...]
        h1_ref[...] = h1
        h1b_ref[...] = h1.astype(BF16)

    t512 = pl.BlockSpec((4, tm, 128), lambda i: (0, i, 0))
    td = pl.BlockSpec((tm, D), lambda i: (i, 0))
    return pl.pallas_call(
        body, name="mix_ln1", grid=(s // tm,),
        in_specs=[td] + [t512] * 7 + [_const((1, 512))] * 2 + [_resident((D, D))] + [_const((1, D))] * 2,
        out_specs=[t512, t512, td, td, td, td],
        out_shape=[jax.ShapeDtypeStruct((4, s, 128), F32), jax.ShapeDtypeStruct((4, s, 128), F32),
                   jax.ShapeDtypeStruct((s, D), BF16), jax.ShapeDtypeStruct((s, D), F32),
                   jax.ShapeDtypeStruct((s, D), F32), jax.ShapeDtypeStruct((s, D), BF16)],
        compiler_params=_cp(("parallel",)),
    )(x, o_a, *o_b, *lse_b, _row(norm_a_g), _row(norm_b_g), w_o, _row(ln1_g), _row(ln1_b))


def _gelu_and_grad(x):
    c = math.sqrt(2.0 / math.pi)
    x2 = x * x
    s = 0.5 * jnp.tanh(x * ((c * 0.044715) * x2 + c)) + 0.5
    dg = s + (x * ((6.0 * c * 0.044715) * x2 + 2.0 * c)) * (s - s * s)
    return x * s, dg


def _shifted(u, edge, row, down):
    groups = [u[8 * i:8 * i + 8] for i in range(u.shape[0] // 8)]
    others = [edge] + groups[:-1] if down else groups[1:] + [edge]
    moved = []
    for k in (1, 2):
        crossing = row >= 8 - k if down else row < k
        moved.append(jnp.concatenate([pltpu.roll(jnp.where(crossing, o, g), k if down else 8 - k, 0)
                                      for o, g in zip(others, groups)], axis=0))
    return moved


def _up_proj(h1b, w_up, tm=512):
    s = h1b.shape[0]

    def body(h_ref, w_ref, o_ref):
        h = h_ref[...]
        for half in (0, 1):
            o_ref[half] = _nn(h, w_ref[:, half * FF:(half + 1) * FF]).astype(BF16)

    return pl.pallas_call(
        body, name="up_proj", grid=(s // tm,),
        in_specs=[pl.BlockSpec((tm, D), lambda i: (i, 0)), _resident((D, 2 * FF))],
        out_specs=pl.BlockSpec((2, tm, FF), lambda i: (0, i, 0)),
        out_shape=jax.ShapeDtypeStruct((2, s, FF), BF16),
        compiler_params=_cp(("parallel",)),
    )(h1b, w_up)


def _conv_gelu(up, cwb, tm=256, tn=FF // 2, chunk_rows=16):
    s = up.shape[1]
    n_c = tm // chunk_rows

    def body(up_ref, c_ref, a_ref, g_ref, a1_ref, carry):
        @pl.when(pl.program_id(1) == 0)
        def _():
            carry[...] = jnp.zeros_like(carry)

        row = lax.broadcasted_iota(jnp.int32, (8, tn), 0)
        edge = [carry[0], carry[1]]
        for c in range(n_c):
            rows = pl.ds(c * chunk_rows, chunk_rows)
            u = []
            for half in (0, 1):
                x = up_ref[half, rows, :].astype(F32)
                r1, r2 = _shifted(x, edge[half], row, True)
                u.append(r2 * c_ref[0, half:half + 1, :] + r1 * c_ref[1, half:half + 1, :]
                         + x * c_ref[2, half:half + 1, :] + c_ref[3, half:half + 1, :])
                edge[half] = x[chunk_rows - 8:]
            g, dg = _gelu_and_grad(u[0])
            a_ref[rows, :] = (g * u[1]).astype(BF16)
            g_ref[rows, :] = g.astype(BF16)
            a1_ref[rows, :] = (u[1] * dg).astype(BF16)
        for half in (0, 1):
            carry[half] = edge[half]

    pair = pl.BlockSpec((2, tm, tn), lambda j, i: (0, i, j))
    tile = pl.BlockSpec((tm, tn), lambda j, i: (i, j))
    return pl.pallas_call(
        body, name="conv_gelu", grid=(FF // tn, s // tm),
        in_specs=[pair, pl.BlockSpec((4, 2, tn), lambda j, i: (0, 0, j))],
        out_specs=[tile, tile, tile],
        out_shape=[jax.ShapeDtypeStruct((s, FF), BF16)] * 3,
        scratch_shapes=[pltpu.VMEM((2, 8, tn), F32)],
        compiler_params=_cp(("parallel", "arbitrary")),
    )(up, cwb)


def _down_ln2_loss(a, w_down, h1, target, ln2_g, ln2_b, tm=512):
    s = a.shape[0]

    def body(a_ref, w_ref, h_ref, t_ref, g_ref, b_ref, dz_ref, dzb_ref, st_ref):
        @pl.when(pl.program_id(0) == 0)
        def _():
            st_ref[...] = jnp.zeros_like(st_ref)

        z2 = ALPHA * h_ref[...] + _nn(a_ref[...], w_ref[...])
        zh, rstd = _layer_norm_stats(z2)
        diff = zh * g_ref[...] + b_ref[...] - t_ref[...]
        part = 0.5 * jnp.sum(jnp.mean(diff * diff, axis=-1, keepdims=True), axis=0, keepdims=True)
        dy = diff * (1.0 / D)
        st_ref[0:1, :] += jnp.sum(dy * zh, axis=0, keepdims=True)
        st_ref[1:2, :] += jnp.sum(dy, axis=0, keepdims=True)
        st_ref[2:3, :] += jnp.broadcast_to(part, (1, D))
        dz = _layer_norm_bwd(dy, zh, rstd, g_ref[...])
        dz_ref[...] = dz
        dzb_ref[...] = dz.astype(BF16)

    td = pl.BlockSpec((tm, D), lambda i: (i, 0))
    return pl.pallas_call(
        body, name="down_ln2_loss", grid=(s // tm,),
        in_specs=[pl.BlockSpec((tm, FF), lambda i: (i, 0)), _resident((FF, D)), td, td, _const((1, D)), _const((1, D))],
        out_specs=[td, td, _const((8, D))],
        out_shape=[jax.ShapeDtypeStruct((s, D), F32), jax.ShapeDtypeStruct((s, D), BF16),
                   jax.ShapeDtypeStruct((8, D), F32)],
        compiler_params=_cp(("arbitrary",)),
    )(a, w_down, h1, target, _row(ln2_g), _row(ln2_b))


def _d_act(dz2b, w_down, tm=512):
    s = dz2b.shape[0]

    def body(dz_ref, w_ref, o_ref):
        o_ref[...] = _nt(dz_ref[...], w_ref[...]).astype(BF16)

    return pl.pallas_call(
        body, name="d_act", grid=(s // tm,),
        in_specs=[pl.BlockSpec((tm, D), lambda i: (i, 0)), _resident((FF, D))],
        out_specs=pl.BlockSpec((tm, FF), lambda i: (i, 0)),
        out_shape=jax.ShapeDtypeStruct((s, FF), BF16),
        compiler_params=_cp(("parallel",)),
    )(dz2b, w_down)


def _conv_gelu_bwd(da, up, g, a1, cwb, tm=256, tn=FF // 2, chunk_rows=16):
    s = da.shape[0]
    n_i = s // tm
    n_c = tm // chunk_rows

    def body(da_ref, up_ref, g_ref, a1_ref, c_ref, dup_ref, dc_ref, carry):
        @pl.when(pl.program_id(1) == 0)
        def _():
            carry[...] = jnp.zeros_like(carry)
            dc_ref[...] = jnp.zeros_like(dc_ref)

        def fold(v):
            return jnp.sum(v.reshape(chunk_rows // 8, 8, v.shape[1]), axis=0)

        def chunk(cc, state):
            after, sums = state
            rows = pl.ds((n_c - 1 - cc) * chunk_rows, chunk_rows)
            da_c = da_ref[rows, :].astype(F32)
            dus = (da_c * a1_ref[rows, :].astype(F32), da_c * g_ref[rows, :].astype(F32))
            head, new_sums = [], []
            for half in (0, 1):
                du = dus[half]
                up = up_ref[half, rows, :].astype(F32)
                l1, l2 = _shifted(du, after[half], row, False)
                dup = (du * c_ref[2, half:half + 1, :] + l1 * c_ref[1, half:half + 1, :]
                       + l2 * c_ref[0, half:half + 1, :])
                dup_ref[half, rows, :] = dup.astype(BF16)
                parts = (fold(l2 * up), fold(l1 * up), fold(du * up), fold(du))
                new_sums.append(parts if sums is None else tuple(a + b for a, b in zip(sums[half], parts)))
                head.append(du[:8])
            return tuple(head), new_sums

        row = lax.broadcasted_iota(jnp.int32, (8, tn), 0)
        state = ((carry[0], carry[1]), None)
        for cc in range(n_c):
            state = chunk(cc, state)
        head, sums = state
        for half in (0, 1):
            carry[half] = head[half]
            for k in range(4):
                dc_ref[k, half:half + 1, :] += jnp.sum(sums[half][k], axis=0, keepdims=True)

    rev = lambda ii: n_i - 1 - ii
    tile = pl.BlockSpec((tm, tn), lambda j, ii: (rev(ii), j))
    pair = pl.BlockSpec((2, tm, tn), lambda j, ii: (0, rev(ii), j))
    per_col = pl.BlockSpec((4, 2, tn), lambda j, ii: (0, 0, j))
    return pl.pallas_call(
        body, name="conv_gelu_bwd", grid=(FF // tn, n_i),
        in_specs=[tile, pair, tile, tile, per_col],
        out_specs=[pair, per_col],
        out_shape=[jax.ShapeDtypeStruct((2, s, FF), BF16), jax.ShapeDtypeStruct((4, 2, FF), F32)],
        scratch_shapes=[pltpu.VMEM((2, 8, tn), F32)],
        compiler_params=_cp(("parallel", "arbitrary")),
    )(da, up, g, a1, cwb)


def _dh1_ln1_bwd(dz2, dup, w_up, z1, ln1_g, tm=512):
    s = dz2.shape[0]

    def body(dz2_ref, dup_ref, w_ref, z1_ref, g_ref, dz1_ref, dz1b_ref, st_ref):
        @pl.when(pl.program_id(0) == 0)
        def _():
            st_ref[...] = jnp.zeros_like(st_ref)

        dh = ALPHA * dz2_ref[...] + _nt(dup_ref[0], w_ref[:, :FF]) + _nt(dup_ref[1], w_ref[:, FF:])
        zh, rstd = _layer_norm_stats(z1_ref[...])
        st_ref[0:1, :] += jnp.sum(dh * zh, axis=0, keepdims=True)
        st_ref[1:2, :] += jnp.sum(dh, axis=0, keepdims=True)
        dz = _layer_norm_bwd(dh, zh, rstd, g_ref[...])
        dz1_ref[...] = dz
        dz1b_ref[...] = dz.astype(BF16)

    td = pl.BlockSpec((tm, D), lambda i: (i, 0))
    return pl.pallas_call(
        body, name="dh1_ln1_bwd", grid=(s // tm,),
        in_specs=[td, pl.BlockSpec((2, tm, FF), lambda i: (0, i, 0)), _resident((D, 2 * FF)), td, _const((1, D))],
        out_specs=[td, td, _const((8, D))],
        out_shape=[jax.ShapeDtypeStruct((s, D), F32), jax.ShapeDtypeStruct((s, D), BF16),
                   jax.ShapeDtypeStruct((8, D), F32)],
        compiler_params=_cp(("arbitrary",), 58),
    )(dz2, dup, w_up, z1, _row(ln1_g))


def _dcat_rms_bwd(dz1b, w_o, o_a, o_b, norm_a_g, norm_b_g, tm=512):
    s = dz1b.shape[0]

    def body(dz_ref, w_ref, oa_ref, ob_ref, ga_ref, gb_ref, da_ref, db_ref, st_ref):
        @pl.when(pl.program_id(0) == 0)
        def _():
            st_ref[...] = jnp.zeros_like(st_ref)

        dcat = _nt(dz_ref[...], w_ref[...])
        for k, (o_ref, g_ref, d_ref) in enumerate(((oa_ref, ga_ref, da_ref), (ob_ref, gb_ref, db_ref))):
            o = jnp.concatenate([o_ref[j] for j in range(4)], axis=1)
            dn = dcat[:, 512 * k:512 * (k + 1)]
            rr = _rms(o)
            oh = o * rr
            st_ref[k:k + 1, :] += jnp.sum(dn * oh, axis=0, keepdims=True)
            doh = dn * g_ref[...]
            d_o = rr * (doh - oh * jnp.mean(doh * oh, axis=-1, keepdims=True))
            for j in range(4):
                d_ref[j] = d_o[:, 128 * j:128 * (j + 1)]

    t512 = pl.BlockSpec((4, tm, 128), lambda i: (0, i, 0))
    return pl.pallas_call(
        body, name="dcat_rms_bwd", grid=(s // tm,),
        in_specs=[pl.BlockSpec((tm, D), lambda i: (i, 0)), _resident((D, D)), t512, t512,
                  _const((1, 512)), _const((1, 512))],
        out_specs=[t512, t512, _const((8, 512))],
        out_shape=[jax.ShapeDtypeStruct((4, s, 128), F32), jax.ShapeDtypeStruct((4, s, 128), F32),
                   jax.ShapeDtypeStruct((8, 512), F32)],
        compiler_params=_cp(("arbitrary",)),
    )(dz1b, w_o, o_a, o_b, _row(norm_a_g), _row(norm_b_g))


def _grad_w_in(dparts, xb, tk=2048):
    s = xb.shape[0]
    nk = s // tk

    def body(qa, ka, va, qb, kb, vb, x_ref, o_ref, ob_ref):
        i = pl.program_id(0)
        k = pl.program_id(1)

        @pl.when(k == 0)
        def _():
            o_ref[...] = jnp.zeros_like(o_ref)

        def add(blocks):
            o_ref[...] += _tn(jnp.concatenate(blocks, axis=1), x_ref[...])

        pl.when(i == 0)(lambda: add([qa[j] for j in range(4)] + [ka[...], va[...]]))
        pl.when(i == 1)(lambda: add([qb[j] for j in range(4)] + [kb[0], kb[1]]))
        pl.when(i == 2)(lambda: add([kb[0], kb[1]] + [vb[j] for j in range(4)]))

        @pl.when(k == nk - 1)
        def _():
            ob_ref[...] = o_ref[...].astype(BF16)

    def during(tile):
        return lambda i, k: jnp.where(i == tile, k, jnp.where(i < tile, 0, nk - 1))

    quad = lambda tile: pl.BlockSpec((4, tk, 128), lambda i, k: (0, during(tile)(i, k), 0))
    one = pl.BlockSpec((tk, 128), lambda i, k: (during(0)(i, k), 0))
    kb_spec = pl.BlockSpec((2, tk, 128), lambda i, k: (jnp.where(i == 2, 1, 0), jnp.where(i == 0, 0, k), 0))
    return pl.pallas_call(
        body, name="grad_w_in", grid=(3, nk),
        in_specs=[quad(0), one, one, quad(1), kb_spec, quad(2), pl.BlockSpec((tk, D), lambda i, k: (k, 0))],
        out_specs=[pl.BlockSpec((WA, D), lambda i, k: (i, 0))] * 2,
        out_shape=[jax.ShapeDtypeStruct((WIN, D), F32), jax.ShapeDtypeStruct((WIN, D), BF16)],
        compiler_params=_cp(("parallel", "arbitrary"), mb=56),
    )(*dparts, xb)


def _grad_x(dz1, dparts, w_in_t, zero, tm=512):
    s = dz1.shape[0]

    def body(dz_ref, qa, ka, va, qb, kb, vb, w_ref, z_ref, o_ref):
        dp = jnp.concatenate([qa[j] for j in range(4)] + [ka[...], va[...]]
                             + [ref[j] for ref in (qb, kb, vb) for j in range(4)], axis=1)
        o_ref[...] = ALPHA * dz_ref[...] + _nn(dp, w_ref[...]) + z_ref[0:1, 0:1]

    td = pl.BlockSpec((tm, D), lambda i: (i, 0))
    quad = pl.BlockSpec((4, tm, 128), lambda i: (0, i, 0))
    one = pl.BlockSpec((tm, 128), lambda i: (i, 0))
    return pl.pallas_call(
        body, name="grad_x", grid=(s // tm,),
        in_specs=[td, quad, one, one, quad, quad, quad, _resident((WIN, D)), _const((8, 128))],
        out_specs=td, out_shape=jax.ShapeDtypeStruct((s, D), F32),
        compiler_params=_cp(("parallel",)),
    )(dz1, *dparts, w_in_t, zero)


def _place():
    return lax.axis_index("x"), lax.axis_index("y"), lax.axis_index("c")


def _other_chips(x, y):
    return [(1 - x, y), (x, 1 - y), (1 - x, 1 - y)]


def _hbm(a):
    return pltpu.with_memory_space_constraint(a, pltpu.HBM)


def _gather_w_in(shard, conv_w):
    rows_k = shard.shape[0]
    half = rows_k // 2

    def body(src, conv_src, out, conv_out, send_sems, recv_sems):
        x, y, c = _place()
        b = 2 * x + y
        sibling = (x, y, 1 - c)
        chips = _other_chips(x, y)

        def copy(idx, chip_b, core, to, first_hop=False):
            rows = out.at[pl.ds(pl.multiple_of(chip_b * rows_k + core * half, 16), half)]
            s_ref = src.at[pl.ds(pl.multiple_of(core * half, 16), half)] if first_hop else rows
            return pltpu.make_async_remote_copy(src_ref=s_ref, dst_ref=rows, send_sem=send_sems.at[idx],
                                                recv_sem=recv_sems.at[idx], device_id=to, device_id_type=MESH)

        def own_copy():
            return pltpu.make_async_remote_copy(
                src_ref=src, dst_ref=out.at[pl.ds(pl.multiple_of(b * rows_k, 16), rows_k)], send_sem=send_sems.at[6],
                recv_sem=recv_sems.at[6], device_id=sibling, device_id_type=MESH)

        def conv_copy(idx, chip_b, to):
            return pltpu.make_async_remote_copy(src_ref=conv_src, dst_ref=conv_out.at[chip_b],
                                                send_sem=send_sems.at[7 + idx], recv_sem=recv_sems.at[7 + idx],
                                                device_id=to, device_id_type=MESH)

        started = [own_copy(), conv_copy(3, b, sibling)]
        for jn, chip in enumerate(chips):
            started += [copy(jn, b, c, (chip[0], chip[1], c), first_hop=True), conv_copy(jn, b, (chip[0], chip[1], c))]
        for cp in started:
            cp.start()
        for jn, chip in enumerate(chips):
            cb = 2 * chip[0] + chip[1]
            copy(jn, cb, c, (chip[0], chip[1], c)).wait_recv()
            cp = copy(3 + jn, cb, c, sibling)
            cp.start()
            started.append(cp)
        for jn, chip in enumerate(chips):
            cb = 2 * chip[0] + chip[1]
            copy(3 + jn, cb, 1 - c, sibling).wait_recv()
            conv_copy(jn, cb, (chip[0], chip[1], c)).wait_recv()
        own_copy().wait_recv()
        conv_copy(3, b, sibling).wait_recv()
        for cp in started:
            cp.wait_send()

    return pl.pallas_call(
        body, name="gather_w_in",
        in_specs=[ANY, ANY], out_specs=[ANY, ANY],
        out_shape=[jax.ShapeDtypeStruct((N_CHIPS * rows_k, D), BF16), jax.ShapeDtypeStruct((N_CHIPS,) + conv_w.shape, F32)],
        scratch_shapes=[pltpu.SemaphoreType.DMA((11,)), pltpu.SemaphoreType.DMA((11,))],
        compiler_params=pltpu.CompilerParams(has_side_effects=True),
    )(shard, conv_w)


def _weight_copies(shard, land, send_sems, recv_sems, arrivals):
    x, y, c = _place()
    n_rows, n_cols = shard.shape
    peers = [(px, py, c) for px, py in _other_chips(x, y)] + [(x, y, 1 - c)]
    cps = []
    for jn, peer in enumerate(peers):
        at = 2 * peer[0] + peer[1] if arrivals else 2 * x + y
        if land.shape[1] == n_cols:
            dst = land.at[pl.ds(pl.multiple_of(at * n_rows, 16), n_rows)]
        else:
            dst = land.at[:, pl.ds(pl.multiple_of(at * n_cols, 128), n_cols)]
        cps.append(pltpu.make_async_remote_copy(src_ref=shard, dst_ref=dst, send_sem=send_sems.at[jn],
                                                recv_sem=recv_sems.at[jn], device_id=peer, device_id_type=MESH))
    return cps


def _weights_start(shards, after):
    n = len(shards)
    lands = [lax.empty((N_CHIPS * sh.shape[0], D) if sh.shape[1] == D else (D, N_CHIPS * sh.shape[1]), BF16)
             for sh in shards]

    def body(*refs):
        src, land = refs[:n], refs[n:2 * n]
        send_sems, recv_sems = refs[2 * n + 1:3 * n + 1], refs[3 * n + 1:4 * n + 1]
        for k in range(n):
            for send in _weight_copies(src[k], land[k], send_sems[k], recv_sems[k], False):
                send.start()
        refs[-1][...] = jnp.zeros_like(refs[-1])

    res = pl.pallas_call(
        body, name="weights_start",
        in_specs=[HBM] * (2 * n) + [ANY], out_specs=[SEM] * (2 * n) + [HBM] * (2 * n) + [VMEM],
        out_shape=[pltpu.SemaphoreType.DMA((4,))] * (2 * n)
        + [pltpu.HBM(a.shape, a.dtype) for a in (*shards, *lands)] + [jax.ShapeDtypeStruct((8, 128), F32)],
        input_output_aliases={i: i + 2 * n for i in range(2 * n)},
        compiler_params=pltpu.CompilerParams(has_side_effects=DATAFLOW),
    )(*[_hbm(a) for a in (*shards, *lands)], after)
    return [(res[k], res[n + k], res[2 * n + k], res[3 * n + k]) for k in range(n)], res[-1]


def _weights_wait(started, after, name):
    send_sems, recv_sems, shard, land = started

    def body(s_ref, l_ref, send_ref, recv_ref, after_ref, s_out, l_out):
        for cp in _weight_copies(s_ref, l_ref, send_ref, recv_ref, True):
            cp.wait_send()
            cp.wait_recv()

    return pl.pallas_call(
        body, name=name,
        in_specs=[HBM, HBM, SEM, SEM, ANY], out_specs=[HBM, HBM],
        out_shape=[pltpu.HBM(shard.shape, shard.dtype), pltpu.HBM(land.shape, land.dtype)],
        input_output_aliases={0: 0, 1: 1},
        compiler_params=pltpu.CompilerParams(has_side_effects=DATAFLOW),
    )(shard, land, send_sems, recv_sems, after)[1]


def _grad_copies(g_ref, land_ref, send_sems, recv_sems):
    x, y, c = _place()
    cps = []
    for d in range(1, 8):
        px, py, pc = x ^ (d >> 2), y ^ ((d >> 1) & 1), c ^ (d & 1)
        cps.append(pltpu.make_async_remote_copy(
            src_ref=g_ref.at[2 * px + py, pc], dst_ref=land_ref.at[d - 1], send_sem=send_sems.at[d - 1],
            recv_sem=recv_sems.at[d - 1], device_id=(px, py, pc), device_id_type=MESH))
    return cps


def _grads_start(grads_b, name):
    n = len(grads_b)
    lands = [lax.empty((7, g.shape[2], D), BF16) for g in grads_b]

    def body(*refs):
        g, land = refs[:n], refs[n:2 * n]
        send_sems, recv_sems = refs[2 * n:3 * n], refs[3 * n:4 * n]
        for k in range(n):
            for cp in _grad_copies(g[k], land[k], send_sems[k], recv_sems[k]):
                cp.start()
        refs[-1][...] = jnp.zeros_like(refs[-1])

    res = pl.pallas_call(
        body, name=name,
        in_specs=[HBM] * (2 * n), out_specs=[SEM] * (2 * n) + [HBM] * (2 * n) + [VMEM],
        out_shape=[pltpu.SemaphoreType.DMA((7,))] * (2 * n)
        + [pltpu.HBM(a.shape, a.dtype) for a in (*grads_b, *lands)] + [jax.ShapeDtypeStruct((8, 128), F32)],
        input_output_aliases={i: i + 2 * n for i in range(2 * n)},
        compiler_params=pltpu.CompilerParams(has_side_effects=DATAFLOW),
    )(*[_hbm(a) for a in (*grads_b, *lands)])
    return [(res[k], res[n + k], res[2 * n + k], res[3 * n + k]) for k in range(n)], res[-1]


def _grads_wait(started, after, name):
    n = len(started)

    def body(*refs):
        g, land = refs[:n], refs[n:2 * n]
        send_sems, recv_sems = refs[2 * n:3 * n], refs[3 * n:4 * n]
        for k in range(n):
            for cp in _grad_copies(g[k], land[k], send_sems[k], recv_sems[k]):
                cp.wait_send()
                cp.wait_recv()

    gs = [st[2] for st in started]
    lands = [st[3] for st in started]
    res = pl.pallas_call(
        body, name=name,
        in_specs=[HBM] * (2 * n) + [SEM] * (2 * n) + [ANY], out_specs=[HBM] * (2 * n),
        out_shape=[pltpu.HBM(a.shape, a.dtype) for a in (*gs, *lands)],
        input_output_aliases={i: i for i in range(2 * n)},
        compiler_params=pltpu.CompilerParams(has_side_effects=DATAFLOW),
    )(*gs, *lands, *[st[0] for st in started], *[st[1] for st in started], after)
    return res[n:]


def _sum_partials(grad4, got, cb, name, tr):
    h = grad4.shape[2]
    per_half = h // tr

    def body(cb_ref, g_ref, o_ref, out_ref):
        acc = g_ref[...]
        for j in range(7):
            acc = acc + o_ref[j].astype(F32)
        out_ref[...] = acc

    return pl.pallas_call(
        body, name=name,
        grid_spec=pltpu.PrefetchScalarGridSpec(
            num_scalar_prefetch=1, grid=(per_half,),
            in_specs=[pl.BlockSpec((None, None, tr, D), lambda i, cb_ref: (cb_ref[1], cb_ref[0], i, 0)),
                      pl.BlockSpec((7, tr, D), lambda i, cb_ref: (0, i, 0))],
            out_specs=pl.BlockSpec((tr, D), lambda i, cb_ref: (cb_ref[0] * per_half + i, 0))),
        out_shape=jax.ShapeDtypeStruct((2 * h, D), F32),
        compiler_params=_cp(("arbitrary",)),
    )(cb, grad4, got)


def _swap_halves(shards, name):
    n = len(shards)

    def body(*refs):
        out, send_sems, recv_sems = refs[n:2 * n], refs[2 * n], refs[2 * n + 1]
        x, y, c = _place()
        cps = []
        for k in range(n):
            h = shards[k].shape[0] // 2
            mine = out[k].at[pl.ds(pl.multiple_of(c * h, 8), h)]
            cp = pltpu.make_async_remote_copy(src_ref=mine, dst_ref=mine, send_sem=send_sems.at[k],
                                              recv_sem=recv_sems.at[k], device_id=(x, y, 1 - c), device_id_type=MESH)
            cp.start()
            cps.append(cp)
        for cp in cps:
            cp.wait()

    return pl.pallas_call(
        body, name=name,
        in_specs=[ANY] * n, out_specs=[ANY] * n,
        out_shape=[jax.ShapeDtypeStruct(sh.shape, F32) for sh in shards],
        input_output_aliases={k: k for k in range(n)},
        scratch_shapes=[pltpu.SemaphoreType.DMA((n,)), pltpu.SemaphoreType.DMA((n,))],
        compiler_params=pltpu.CompilerParams(has_side_effects=True),
    )(*shards)


def _small_copies(small_ref, land_ref, send_sems, recv_sems):
    x, y, c = _place()
    me = 4 * x + 2 * y + c
    cps = []
    for d in range(1, 8):
        px, py, pc = x ^ (d >> 2), y ^ ((d >> 1) & 1), c ^ (d & 1)
        cps.append(pltpu.make_async_remote_copy(
            src_ref=small_ref, dst_ref=land_ref.at[me], send_sem=send_sems.at[d - 1], recv_sem=recv_sems.at[d - 1],
            device_id=(px, py, pc), device_id_type=MESH))
    return cps


def _small_start(small):
    land = lax.empty((8,) + small.shape, F32)

    def body(s_ref, l_ref, send_sems, recv_sems, s_thru, l_thru, token):
        for cp in _small_copies(s_ref, l_ref, send_sems, recv_sems):
            cp.start()
        token[...] = jnp.zeros_like(token)

    res = pl.pallas_call(
        body, name="small_start",
        in_specs=[HBM, HBM], out_specs=[SEM, SEM, HBM, HBM, VMEM],
        out_shape=[pltpu.SemaphoreType.DMA((7,)), pltpu.SemaphoreType.DMA((7,)), pltpu.HBM(small.shape, F32),
                   pltpu.HBM(land.shape, F32), jax.ShapeDtypeStruct((8, 128), F32)],
        input_output_aliases={0: 2, 1: 3},
        compiler_params=pltpu.CompilerParams(has_side_effects=DATAFLOW),
    )(_hbm(small), _hbm(land))
    return res[:4], res[4]


def _small_wait(started, after):
    send_sems, recv_sems, small, land = started

    def body(s_ref, l_ref, send_ref, recv_ref, after_ref, s_out, l_out):
        for cp in _small_copies(s_ref, l_ref, send_ref, recv_ref):
            cp.wait_send()
            cp.wait_recv()

    return pl.pallas_call(
        body, name="small_wait",
        in_specs=[HBM, HBM, SEM, SEM, ANY], out_specs=[HBM, HBM],
        out_shape=[pltpu.HBM(small.shape, F32), pltpu.HBM(land.shape, F32)],
        input_output_aliases={0: 0, 1: 1},
        compiler_params=pltpu.CompilerParams(has_side_effects=DATAFLOW),
    )(small, land, send_sems, recv_sems, after)


def _small_sum(small, land, me):
    rows = small.shape[0]

    def body(me_ref, s_ref, l_ref, o_ref):
        acc = None
        for k in range(8):
            term = jnp.where(me_ref[0] == k, s_ref[...], l_ref[k])
            acc = term if k == 0 else acc + term
        o_ref[...] = acc

    return pl.pallas_call(
        body, name="small_sum",
        in_specs=[SMEM, VMEM, VMEM], out_specs=VMEM,
        out_shape=jax.ShapeDtypeStruct((rows, D), F32),
    )(me, small, land)


def _adamw(w, g, m, v, name, tr):
    rows, cols = w.shape

    def body(w_ref, g_ref, m_ref, v_ref, d_ref, nm_ref, nv_ref):
        g_ = g_ref[...]
        nm = ADAM_B1 * m_ref[...] + (1.0 - ADAM_B1) * g_
        nv = ADAM_B2 * v_ref[...] + (1.0 - ADAM_B2) * (g_ * g_)
        m_hat = nm / (1.0 - ADAM_B1 ** ADAM_STEP)
        v_hat = nv / (1.0 - ADAM_B2 ** ADAM_STEP)
        d_ref[...] = -ADAM_LR * (m_hat / (jnp.sqrt(v_hat) + ADAM_EPS) + ADAM_WD * w_ref[...])
        nm_ref[...] = nm
        nv_ref[...] = nv

    spec = pl.BlockSpec((tr, cols), lambda i: (i, 0))
    return pl.pallas_call(
        body, name=name, grid=(rows // tr,),
        in_specs=[spec] * 4, out_specs=[spec] * 3,
        out_shape=[jax.ShapeDtypeStruct((rows, cols), F32)] * 3,
        compiler_params=_cp(("parallel",)),
    )(w, g, m, v)


def _local_step(x, target, w_in_t, late_weights, norm_a_g, norm_b_g, sinks_a, ln1_g, ln1_b,
                conv_w, conv_b, ln2_g, ln2_b, slopes, on_grad, on_small):
    cwb = jnp.concatenate([conv_w, conv_b[None]], axis=0).reshape(4, 2, FF)

    proj, xb = _proj(x, w_in_t, "proj")
    o_a, lse_a = _attn_a_fwd(proj, sinks_a)
    fwd_b = [_attn_b_fwd(proj, slopes, r) for r in B_DILATIONS]
    w_o = late_weights(1, fwd_b[-1][1])
    o_b, lse_b, cat, z1, h1, h1b = _mix_ln1(x, o_a, [f[0] for f in fwd_b], [f[1] for f in fwd_b],
                                           norm_a_g, norm_b_g, w_o, ln1_g, ln1_b)
    w_up = late_weights(2, h1b)
    up = _up_proj(h1b, w_up)
    a, gate, a1 = _conv_gelu(up, cwb)
    w_down = late_weights(3, a)
    dz2, dz2b, st2 = _down_ln2_loss(a, w_down, h1, target, ln2_g, ln2_b)

    on_grad(3, *_grad_w(a, dz2b, "grad_w_down", tm=FF // 2))
    dup, dconv = _conv_gelu_bwd(_d_act(dz2b, w_down), up, gate, a1, cwb)
    on_grad(2, *_grad_w(dup, h1b, "grad_w_up", tm=FF // 2, lhs_halves=True))
    dz1, dz1b, st1 = _dh1_ln1_bwd(dz2, dup, w_up, z1, ln1_g)
    tok = on_grad(1, *_grad_w(cat, dz1b, "grad_w_o", tm=512))
    d_oa, d_ob, st_n = _dcat_rms_bwd(dz1b, w_o, o_a, o_b, norm_a_g + tok[0, 0], norm_b_g)
    dqa, dka, dva, dsink = _attn_a_bwd(proj, sinks_a, d_oa, o_a, lse_a)
    dconv = dconv.reshape(4, 2 * FF)
    tok = on_small(dict(loss=st2[2, 0:1], norm_a_g=st_n[0], norm_b_g=st_n[1], sinks_a=dsink[:, 0],
                        ln1_g=st1[0], ln1_b=st1[1], conv_w=dconv[0:3].reshape(-1), conv_b=dconv[3],
                        ln2_g=st2[0], ln2_b=st2[1]))
    slopes = slopes + tok[0, 0]
    bwd_b = None
    for r in reversed(B_DILATIONS):
        bwd_b = _attn_b_bwd(proj, slopes, d_ob, o_b, lse_b, r, bwd_b, BF16 if r == 1 else F32)
    dparts = (dqa, dka, dva, *bwd_b)
    tok = on_grad(0, *_grad_w_in(dparts, xb))
    return _grad_x(dz1, dparts, w_in_t, tok)


SMALL_ORDER = ("loss", "norm_a_g", "norm_b_g", "sinks_a", "ln1_g", "ln1_b", "conv_b", "ln2_g", "ln2_b", "conv_w")
SMALL_SIZES = dict(loss=1, norm_a_g=512, norm_b_g=512, sinks_a=8, ln1_g=D, ln1_b=D, conv_b=2 * FF, ln2_g=D, ln2_b=D,
                   conv_w=3 * 2 * FF)


def _pack(parts, rows):
    flat = jnp.concatenate([parts[k].reshape(-1).astype(F32) for k in parts])
    return jnp.pad(flat, (0, rows * D - flat.shape[0])).reshape(rows, D)


def _unpack(buf, names, sizes):
    flat = buf.reshape(-1)
    out, at = {}, 0
    for k in names:
        out[k] = flat[at:at + sizes[k]]
        at += sizes[k]
    return out


def kernel(x, w_in, norm_a_g, norm_b_g, sinks_a, w_o, ln1_g, ln1_b, w_up, conv_w, conv_b, w_down, ln2_g, ln2_b, loss_target, m_w_in, m_norm_a_g, m_norm_b_g, m_sinks_a, m_w_o, m_ln1_g, m_ln1_b, m_w_up, m_conv_w, m_conv_b, m_w_down, m_ln2_g, m_ln2_b, v_w_in, v_norm_a_g, v_norm_b_g, v_sinks_a, v_w_o, v_ln1_g, v_ln1_b, v_w_up, v_conv_w, v_conv_b, v_w_down, v_ln2_g, v_ln2_b):
    xi, yi, ci = _place()
    chip = (2 * xi + yi).astype(I32)
    core = ci.astype(I32)

    w_in_rows, m_w_in_rows, v_w_in_rows = w_in.T, m_w_in.T, v_w_in.T
    shards = (w_in_rows.astype(BF16), w_o.astype(BF16), w_up.astype(BF16), w_down.astype(BF16))
    w_in_t, conv_w4 = _gather_w_in(shards[0], conv_w)
    conv_w_f = conv_w4.transpose(1, 0, 2).reshape(3, 2 * FF)
    w_started, w_tok = _weights_start(shards[1:], conv_w4)
    slopes = jnp.asarray(SLOPES, F32) + w_tok[0, 0]

    halves_rows = [r // 2 for r in SHARD_ROWS]
    grads4, grads_b4, started = [None] * 4, [None] * 4, [None] * 4

    def on_grad(k, g, g_b):
        grads4[k] = g.reshape(N_CHIPS, 2, halves_rows[k], D)
        grads_b4[k] = g_b.reshape(N_CHIPS, 2, halves_rows[k], D)
        if k > 1:
            return None
        group = (1, 2, 3) if k == 1 else (0,)
        sts, tok = _grads_start([grads_b4[i] for i in group], f"grads_start_{k}")
        for i, st in zip(group, sts):
            started[i] = st
        return tok

    small_rows = 32
    small_started = []

    def on_small(parts):
        st, tok = _small_start(_pack({k: parts[k] for k in SMALL_ORDER}, small_rows))
        small_started.append(st)
        return tok

    gx = _local_step(
        x[0], loss_target[0], w_in_t, lambda k, after: _weights_wait(w_started[k - 1], after, f"weights_wait_{k}"),
        norm_a_g, norm_b_g, sinks_a, ln1_g, ln1_b, conv_w_f, conv_b, ln2_g, ln2_b, slopes, on_grad, on_small)

    tiles = (96, 128, 352, 176)
    core_chip = jnp.stack([core, chip])
    got = _grads_wait(started[1:], gx, "grads_wait_1")
    halves = [_sum_partials(grads4[k], got[k - 1], core_chip, f"sum_partials_{k}", tiles[k]) for k in (1, 2, 3)]
    g_w_o, g_w_up_rows, g_w_down = _swap_halves(halves, "swap_halves")
    g_w_up = g_w_up_rows.T
    delta, new_m, new_v = {}, {}, {}
    for k, g, tr in (("w_o", g_w_o, 128), ("w_up", g_w_up, 256), ("w_down", g_w_down, 176)):
        delta[k], new_m[k], new_v[k] = _adamw(dict(w_o=w_o, w_up=w_up, w_down=w_down)[k], g,
                                              dict(w_o=m_w_o, w_up=m_w_up, w_down=m_w_down)[k],
                                              dict(w_o=v_w_o, w_up=v_w_up, w_down=v_w_down)[k], f"adamw_{k}", tr)

    got = _grads_wait(started[:1], delta["w_up"], "grads_wait_0")
    half_in = _sum_partials(grads4[0], got[0], core_chip, "sum_partials_0", tiles[0])
    (g_w_in_rows,) = _swap_halves([half_in], "swap_halves_in")
    small_mine, small_land = _small_wait(small_started[0], g_w_in_rows)
    totals = _small_sum(small_mine, small_land, (4 * xi + 2 * yi + ci).astype(I32).reshape(1))
    tot = _unpack(totals, SMALL_ORDER, SMALL_SIZES)
    loss = tot["loss"][0]
    cols = 2 * FF // N_CHIPS
    g_conv_w = lax.dynamic_slice(tot["conv_w"].reshape(3, 2 * FF), (0, chip * cols), (3, cols))
    g_small = dict(norm_a_g=tot["norm_a_g"], norm_b_g=tot["norm_b_g"], sinks_a=tot["sinks_a"], ln1_g=tot["ln1_g"],
                   ln1_b=tot["ln1_b"], conv_w=g_conv_w, conv_b=tot["conv_b"], ln2_g=tot["ln2_g"], ln2_b=tot["ln2_b"])

    weights = dict(w_in=w_in, norm_a_g=norm_a_g, norm_b_g=norm_b_g, sinks_a=sinks_a, w_o=w_o, ln1_g=ln1_g, ln1_b=ln1_b,
                   w_up=w_up, conv_w=conv_w, conv_b=conv_b, w_down=w_down, ln2_g=ln2_g, ln2_b=ln2_b)
    ms = dict(w_in=m_w_in, norm_a_g=m_norm_a_g, norm_b_g=m_norm_b_g, sinks_a=m_sinks_a, w_o=m_w_o, ln1_g=m_ln1_g,
              ln1_b=m_ln1_b, w_up=m_w_up, conv_w=m_conv_w, conv_b=m_conv_b, w_down=m_w_down, ln2_g=m_ln2_g, ln2_b=m_ln2_b)
    vs = dict(w_in=v_w_in, norm_a_g=v_norm_a_g, norm_b_g=v_norm_b_g, sinks_a=v_sinks_a, w_o=v_w_o, ln1_g=v_ln1_g,
              ln1_b=v_ln1_b, w_up=v_w_up, conv_w=v_conv_w, conv_b=v_conv_b, w_down=v_w_down, ln2_g=v_ln2_g, ln2_b=v_ln2_b)
    order = list(weights)
    grad = dict(g_small, w_in=g_w_in_rows.T, w_o=g_w_o, w_up=g_w_up, w_down=g_w_down)

    delta["w_in"], new_m["w_in"], new_v["w_in"] = [
        a.T for a in _adamw(w_in_rows, g_w_in_rows, m_w_in_rows, v_w_in_rows, "adamw_w_in", 144)]
    small_names = [k for k in order if k not in delta]
    sizes = {k: weights[k].size for k in small_names}
    rows = 16
    packed = [_pack({k: src[k] for k in small_names}, rows) for src in (weights, grad, ms, vs)]
    for res, buf in zip((delta, new_m, new_v), _adamw(*packed, "adamw_small", rows)):
        for k, val in _unpack(buf, small_names, sizes).items():
            res[k] = val.reshape(weights[k].shape)

    return (loss, gx[None], *[grad[k] for k in order], *[delta[k] for k in order],
            *[new_m[k] for k in order], *[new_v[k] for k in order])
```

```python
import functools
import math

import jax
import jax.numpy as jnp
from jax import lax
from jax.experimental import pallas as pl
from jax.experimental.pallas import tpu as pltpu

F32, BF16, I32 = jnp.float32, jnp.bfloat16, jnp.int32

D = 1024
FF = 2816
HD = 64
NH = 8
WA, WB = 768, 1536
WIN = WA + WB
BLK = 128
ALPHA = 2.0 ** 0.25
LN_EPS, RMS_EPS = 1e-5, 1e-6
SCALE = 1.0 / math.sqrt(HD)
A_MAX_DIST, B_MAX_DIST = 127, 128
B_DILATIONS = (1, 4, 16)
SLOPES = tuple(2.0 ** (-(i + 1)) for i in range(NH))
SHARD_ROWS = (WIN // 4, D // 4, 2 * FF // 4, FF // 4)
N_CHIPS = 4
ADAM_LR, ADAM_B1, ADAM_B2, ADAM_EPS, ADAM_WD, ADAM_STEP = 0.001, 0.9, 0.999, 1e-08, 0.01, 10
MESH = pl.DeviceIdType.MESH
ANY = pl.BlockSpec(memory_space=pl.ANY)
SMEM = pl.BlockSpec(memory_space=pltpu.SMEM)
VMEM = pl.BlockSpec(memory_space=pltpu.VMEM)
HBM = pl.BlockSpec(memory_space=pltpu.HBM)
SEM = pl.BlockSpec(memory_space=pltpu.SEMAPHORE)
DATAFLOW = pltpu.SideEffectType.DATAFLOW_SIDE_EFFECTING


def _cp(sem, mb=48):
    return pltpu.CompilerParams(dimension_semantics=sem, vmem_limit_bytes=mb << 20)


def _nn(a, b):
    return lax.dot_general(a, b, (((1,), (0,)), ((), ())), preferred_element_type=F32)


def _nt(a, b):
    return lax.dot_general(a, b, (((1,), (1,)), ((), ())), preferred_element_type=F32)


def _tn(a, b):
    return lax.dot_general(a, b, (((0,), (0,)), ((), ())), preferred_element_type=F32)


def _resident(shape):
    n = len(shape)
    return pl.BlockSpec(shape, lambda *_: (0,) * n, pipeline_mode=pl.Buffered(1))


def _const(shape):
    n = len(shape)
    return pl.BlockSpec(shape, lambda *_: (0,) * n)


def _proj(x, w_t, name, tm=512):
    s = x.shape[0]
    n = w_t.shape[0]

    def body(x_ref, w_ref, o_ref, xb_ref):
        xb = x_ref[...].astype(BF16)
        xb_ref[...] = xb
        res = _nt(xb, w_ref[...])
        for g in range(n // 128):
            o_ref[g] = res[:, 128 * g:128 * (g + 1)]

    return pl.pallas_call(
        body, name=name, grid=(s // tm,),
        in_specs=[pl.BlockSpec((tm, D), lambda i: (i, 0)), _resident((n, D))],
        out_specs=[pl.BlockSpec((n // 128, tm, 128), lambda i: (0, i, 0)), pl.BlockSpec((tm, D), lambda i: (i, 0))],
        out_shape=[jax.ShapeDtypeStruct((n // 128, s, 128), F32), jax.ShapeDtypeStruct((s, D), BF16)],
        compiler_params=_cp(("parallel",)),
    )(x, w_t)


def _grad_w(lhs, rhs, name, tm, tk=2048, lhs_halves=False):
    s = rhs.shape[0]
    if lhs_halves:
        per_half = lhs.shape[2] // tm
        n = 2 * lhs.shape[2]
        lhs_spec = pl.BlockSpec((None, tk, tm), lambda i, k: (i // per_half, k, i % per_half))
    else:
        n = lhs.shape[1]
        lhs_spec = pl.BlockSpec((tk, tm), lambda i, k: (k, i))
    nk = s // tk

    def body(l_ref, r_ref, o_ref, ob_ref):
        k = pl.program_id(1)

        @pl.when(k == 0)
        def _():
            o_ref[...] = jnp.zeros_like(o_ref)

        o_ref[...] += _tn(l_ref[...], r_ref[...])

        @pl.when(k == nk - 1)
        def _():
            ob_ref[...] = o_ref[...].astype(BF16)

    return pl.pallas_call(
        body, name=name, grid=(n // tm, nk),
        in_specs=[lhs_spec, pl.BlockSpec((tk, D), lambda i, k: (k, 0))],
        out_specs=[pl.BlockSpec((tm, D), lambda i, k: (i, 0))] * 2,
        out_shape=[jax.ShapeDtypeStruct((n, D), F32), jax.ShapeDtypeStruct((n, D), BF16)],
        compiler_params=_cp(("parallel", "arbitrary")),
    )(lhs, rhs)


def _band_base(max_dist, dist_unit, first):
    row = lax.broadcasted_iota(I32, (BLK, 2 * BLK), 0)
    col = lax.broadcasted_iota(I32, (BLK, 2 * BLK), 1)
    dist = BLK + row - col
    ok = (dist >= 0) & (dist <= max_dist)
    if first:
        ok = ok & (col >= BLK)
    return jnp.where(ok, dist.astype(F32) * (-float(dist_unit)), -jnp.inf)


def _half_mask(shape, e):
    lane = lax.broadcasted_iota(I32, shape, 1)
    return (lane < HD) if e == 0 else (lane >= HD)


def _to_half(x, e, g):
    if g != e:
        x = pltpu.roll(x, HD, 1)
    return jnp.where(_half_mask(x.shape, g), x, 0.0)


def _stack_heads(scalars, tile):
    return jnp.concatenate([scalars[0] * tile, scalars[1] * tile], axis=0)


def _pair_fwd(q2, kb, vb, base, slopes, kv_heads, sinks):
    lo = _half_mask((BLK, 2 * HD), 0)
    if slopes is None:
        bias = base
    elif sinks is None:
        bias = _stack_heads(slopes, base)
    else:
        col0 = lax.broadcasted_iota(I32, base.shape, 1) == 0
        bias = jnp.concatenate([jnp.where(col0, sinks[e], slopes[e] * base) for e in (0, 1)], axis=0)
    qs = jnp.concatenate([_to_half(q2, e, kv_heads[e]) * SCALE for e in (0, 1)], axis=0).astype(BF16)
    s = _nt(qs, kb) + bias
    m = jnp.max(s, axis=1, keepdims=True)
    p = jnp.exp(s - m)
    l = jnp.sum(p, axis=1, keepdims=True)
    o = _nn(p.astype(BF16), vb) / l
    lse = m + jnp.log(l)
    halves = []
    for e in (0, 1):
        oh = o[e * BLK:(e + 1) * BLK]
        halves.append(pltpu.roll(oh, HD, 1) if kv_heads[e] != e else oh)
    o2 = jnp.where(lo, halves[0], halves[1])
    lse2 = jnp.where(lo, jnp.broadcast_to(lse[:BLK], (BLK, 2 * HD)), jnp.broadcast_to(lse[BLK:], (BLK, 2 * HD)))
    return o2, lse2


def _pair_bwd(q2, kb, vb, do2, o2, lse2, base, slopes, kv_heads, sinks):
    lo = _half_mask((BLK, 2 * HD), 0)
    prod = do2 * o2
    lses, deltas = [], []
    for e in (0, 1):
        hq = _half_mask((BLK, 2 * HD), e)
        lses.append(jnp.max(jnp.where(hq, lse2, -jnp.inf), axis=1, keepdims=True))
        deltas.append(jnp.sum(jnp.where(hq, prod, 0.0), axis=1, keepdims=True))
    lse = jnp.concatenate(lses, axis=0)
    delta = jnp.concatenate(deltas, axis=0)
    qs = jnp.concatenate([_to_half(q2, e, kv_heads[e]) * SCALE for e in (0, 1)], axis=0).astype(BF16)
    dos = jnp.concatenate([_to_half(do2, e, kv_heads[e]) for e in (0, 1)], axis=0).astype(BF16)
    p = jnp.exp(_nt(qs, kb) + (base if slopes is None else _stack_heads(slopes, base)) - lse)
    ds = (p * (_nt(dos, vb) - delta)).astype(BF16)
    dq = _nn(ds, kb) * SCALE
    halves = []
    for e in (0, 1):
        dqh = dq[e * BLK:(e + 1) * BLK]
        halves.append(pltpu.roll(dqh, HD, 1) if kv_heads[e] != e else dqh)
    dq2 = jnp.where(lo, halves[0], halves[1])
    dk2 = _tn(ds, qs)
    dv2 = _tn(p.astype(BF16), dos)
    dsinks = []
    if sinks is not None:
        for e in (0, 1):
            dsinks.append(jnp.sum(-jnp.exp(sinks[e] - lses[e]) * deltas[e], axis=0, keepdims=True))
    return dq2, dk2, dv2, dsinks


A_BLOCKS_PER_STEP = 2
A_BLOCKS_PER_STEP_BWD = 1


def _attn_a_fwd(proj, sinks):
    s = proj.shape[1]
    nq = A_BLOCKS_PER_STEP
    rows = BLK * nq
    steps = s // rows

    def body(sink_ref, q_ref, kp_ref, kc_ref, vp_ref, vc_ref, o_ref, lse_ref):
        n = pl.program_id(0)
        base_rest = _band_base(A_MAX_DIST, 1, False)
        base_0 = jnp.where(n > 0, base_rest, _band_base(A_MAX_DIST, 1, True))
        for i in range(nq):
            cur = pl.ds(i * BLK, BLK)
            k_prev = kc_ref[pl.ds((i - 1) * BLK, BLK), :] if i > 0 else kp_ref[...]
            v_prev = vc_ref[pl.ds((i - 1) * BLK, BLK), :] if i > 0 else vp_ref[...]
            first_key = lax.broadcasted_iota(I32, (2 * BLK, 128), 0) == 0
            kb = jnp.where(first_key, 0.0, jnp.concatenate([k_prev, kc_ref[cur, :]], axis=0)).astype(BF16)
            vb = jnp.where(first_key, 0.0, jnp.concatenate([v_prev, vc_ref[cur, :]], axis=0)).astype(BF16)
            for j in range(NH // 2):
                g = j // 2
                o2, lse2 = _pair_fwd(q_ref[j, cur, :], kb, vb, base_rest if i > 0 else base_0,
                                     (SLOPES[2 * j], SLOPES[2 * j + 1]), (g, g), (sink_ref[2 * j], sink_ref[2 * j + 1]))
                o_ref[j, cur, :] = o2
                lse_ref[j, cur, :] = lse2

    before = lambda n: jnp.maximum(n * nq - 1, 0)
    slab = lambda g: pl.BlockSpec((None, rows, 128), lambda n: (g, n, 0))
    edge = lambda g: pl.BlockSpec((None, BLK, 128), lambda n: (g, before(n), 0))
    quad = pl.BlockSpec((4, rows, 128), lambda n: (0, n, 0))
    return pl.pallas_call(
        body, name="attn_a_fwd", grid=(steps,),
        in_specs=[SMEM, quad, edge(4), slab(4), edge(5), slab(5)],
        out_specs=[quad, quad],
        out_shape=[jax.ShapeDtypeStruct((4, s, 128), F32)] * 2,
        compiler_params=_cp(("parallel",)),
    )(sinks, proj, proj, proj, proj, proj)


def _attn_a_bwd(proj, sinks, d_o, o, lse):
    s = proj.shape[1]
    nq = A_BLOCKS_PER_STEP_BWD
    rows = BLK * nq
    steps = s // rows

    def body(sink_ref, q_ref, kp_ref, kc_ref, vp_ref, vc_ref, do_ref, o_ref, lse_ref,
             dq_ref, dk_ref, dv_ref, dsink_ref, kcar, vcar):
        n = pl.program_id(0)

        @pl.when(n == 0)
        def _():
            kcar[...] = jnp.zeros_like(kcar)
            vcar[...] = jnp.zeros_like(vcar)
            dsink_ref[...] = jnp.zeros_like(dsink_ref)

        dk_ref[...] = kcar[...].astype(BF16)
        dv_ref[...] = vcar[...].astype(BF16)

        @pl.when(n < steps)
        def _():
            base_rest = _band_base(A_MAX_DIST, 1, False)
            base_0 = jnp.where(n > 0, base_rest, _band_base(A_MAX_DIST, 1, True))
            for i in range(nq):
                cur = pl.ds(i * BLK, BLK)
                k_prev = kc_ref[pl.ds((i - 1) * BLK, BLK), :] if i > 0 else kp_ref[...]
                v_prev = vc_ref[pl.ds((i - 1) * BLK, BLK), :] if i > 0 else vp_ref[...]
                kb = jnp.concatenate([k_prev, kc_ref[cur, :]], axis=0).astype(BF16)
                vb = jnp.concatenate([v_prev, vc_ref[cur, :]], axis=0).astype(BF16)
                dk_win = dv_win = None
                for j in range(NH // 2):
                    g = j // 2
                    dq2, dk2, dv2, dsk = _pair_bwd(q_ref[j, cur, :], kb, vb, do_ref[j, cur, :], o_ref[j, cur, :],
                                                   lse_ref[j, cur, :], base_rest if i > 0 else base_0,
                                                   (SLOPES[2 * j], SLOPES[2 * j + 1]), (g, g),
                                                   (sink_ref[2 * j], sink_ref[2 * j + 1]))
                    dq_ref[j, cur, :] = dq2.astype(BF16)
                    dk_win = dk2 if j == 0 else dk_win + dk2
                    dv_win = dv2 if j == 0 else dv_win + dv2
                    for e in (0, 1):
                        h = 2 * j + e
                        dsink_ref[h:h + 1, :] += jnp.broadcast_to(dsk[e], (1, 128))
                if i == 0:
                    last = pl.ds((nq - 1) * BLK, BLK)
                    dk_ref[last, :] = (kcar[last, :] + dk_win[:BLK]).astype(BF16)
                    dv_ref[last, :] = (vcar[last, :] + dv_win[:BLK]).astype(BF16)
                else:
                    kcar[pl.ds((i - 1) * BLK, BLK), :] += dk_win[:BLK]
                    vcar[pl.ds((i - 1) * BLK, BLK), :] += dv_win[:BLK]
                kcar[cur, :] = dk_win[BLK:]
                vcar[cur, :] = dv_win[BLK:]

    cur_step = lambda n: jnp.minimum(n, steps - 1)
    before = lambda n: jnp.maximum(cur_step(n) * nq - 1, 0)
    out_prev = lambda n: jnp.maximum(n - 1, 0)
    quad = pl.BlockSpec((4, rows, 128), lambda n: (0, cur_step(n), 0))
    slab = lambda g: pl.BlockSpec((None, rows, 128), lambda n: (g, cur_step(n), 0))
    edge = lambda g: pl.BlockSpec((None, BLK, 128), lambda n: (g, before(n), 0))
    return pl.pallas_call(
        body, name="attn_a_bwd", grid=(steps + 1,),
        in_specs=[SMEM, quad, edge(4), slab(4), edge(5), slab(5), quad, quad, quad],
        out_specs=[quad,
                   pl.BlockSpec((rows, 128), lambda n: (out_prev(n), 0)),
                   pl.BlockSpec((rows, 128), lambda n: (out_prev(n), 0)),
                   pl.BlockSpec((NH, 128), lambda n: (0, 0))],
        out_shape=[jax.ShapeDtypeStruct((4, s, 128), BF16), jax.ShapeDtypeStruct((s, 128), BF16),
                   jax.ShapeDtypeStruct((s, 128), BF16), jax.ShapeDtypeStruct((NH, 128), F32)],
        scratch_shapes=[pltpu.VMEM((rows, 128), F32), pltpu.VMEM((rows, 128), F32)],
        compiler_params=_cp(("arbitrary",)),
    )(sinks, proj, proj, proj, proj, proj, d_o, o, lse)


def _stream(rho, i, r):
    start = i * BLK * r + rho
    return pl.ds(start, BLK, stride=r) if r > 1 else pl.ds(start, BLK)


def _for_streams(r, fn, side_by_side=4):
    if r <= side_by_side:
        for rho in range(r):
            fn(rho)
    else:
        def group(it, carry):
            for u in range(side_by_side):
                fn(side_by_side * it + u)
            return carry

        lax.fori_loop(0, r // side_by_side, group, 0)


B_BLOCKS_PER_STEP = {1: 8, 4: 2, 16: 1}
B_BLOCKS_PER_STEP_FWD = {1: 16, 4: 4, 16: 1}


def _attn_b_fwd(proj, slopes, r, so_far=None):
    s = proj.shape[1]
    nq = B_BLOCKS_PER_STEP_FWD[r]
    rows = BLK * r * nq
    steps = s // rows
    qc, kc, vc = WA // 128, WA // 128 + 4, WA // 128 + 8
    chained = so_far is not None

    def body(slope_ref, q_ref, kp_ref, kc_ref, vp_ref, vc_ref, *rest):
        po_ref, pl_ref = rest[:2] if chained else (None, None)
        o_ref, lse_ref = rest[-2:]
        j = pl.program_id(0)
        sb = pl.program_id(1)
        sl2 = (slope_ref[2 * j], slope_ref[2 * j + 1])
        bias_rest = _stack_heads(sl2, _band_base(B_MAX_DIST, r, False))
        bias_0 = jnp.where(sb > 0, bias_rest, _stack_heads(sl2, _band_base(B_MAX_DIST, r, True)))

        def stream(rho):
            for i in range(nq):
                cur = _stream(rho, i, r)
                k_prev = kc_ref[_stream(rho, i - 1, r), :] if i > 0 else kp_ref[_stream(rho, 0, r), :]
                v_prev = vc_ref[_stream(rho, i - 1, r), :] if i > 0 else vp_ref[_stream(rho, 0, r), :]
                kb = jnp.concatenate([k_prev, kc_ref[cur, :]], axis=0).astype(BF16)
                vb = jnp.concatenate([v_prev, vc_ref[cur, :]], axis=0).astype(BF16)
                o2, lse2 = _pair_fwd(q_ref[cur, :], kb, vb, bias_rest if i > 0 else bias_0, None, (0, 1), None)
                if chained:
                    lse1 = pl_ref[cur, :]
                    m = jnp.maximum(lse1, lse2)
                    e1, e2 = jnp.exp(lse1 - m), jnp.exp(lse2 - m)
                    den = e1 + e2
                    o2 = (e1 * po_ref[cur, :] + e2 * o2) * (1.0 / den)
                    lse2 = m + jnp.log(den)
                o_ref[cur, :] = o2
                lse_ref[cur, :] = lse2

        _for_streams(r, stream, side_by_side=16)

    before = lambda sb: jnp.maximum(sb * nq - 1, 0)
    result = pl.BlockSpec((None, rows, 128), lambda j, sb: (j, sb, 0))
    return pl.pallas_call(
        body, name=f"attn_b_fwd_r{r}", grid=(NH // 2, steps),
        in_specs=[SMEM,
                  pl.BlockSpec((None, rows, 128), lambda j, sb: (qc + j, sb, 0)),
                  pl.BlockSpec((None, BLK * r, 128), lambda j, sb: (kc + j, before(sb), 0)),
                  pl.BlockSpec((None, rows, 128), lambda j, sb: (kc + j, sb, 0)),
                  pl.BlockSpec((None, BLK * r, 128), lambda j, sb: (vc + j, before(sb), 0)),
                  pl.BlockSpec((None, rows, 128), lambda j, sb: (vc + j, sb, 0))] + ([result] * 2 if chained else []),
        out_specs=[result] * 2,
        out_shape=[jax.ShapeDtypeStruct((4, s, 128), F32)] * 2,
        compiler_params=_cp(("parallel", "parallel")),
    )(slopes, proj, proj, proj, proj, proj, *(so_far if chained else ()))


def _attn_b_bwd(proj, slopes, d_o, o, lse, r, so_far=None, dtype=F32):
    s = proj.shape[1]
    nq = B_BLOCKS_PER_STEP[r]
    rows = BLK * r * nq
    steps = s // rows
    qc, kc, vc = WA // 128, WA // 128 + 4, WA // 128 + 8
    chained = so_far is not None

    def body(slope_ref, q_ref, kp_ref, kc_ref, vp_ref, vc_ref, do_ref, o_ref, lse_ref, *rest):
        pq_ref, pk_ref, pv_ref = rest[:3] if chained else (None, None, None)
        dq_ref, dk_ref, dv_ref, kcar, vcar = rest[-5:]
        j = pl.program_id(0)
        sb = pl.program_id(1)

        @pl.when(sb == 0)
        def _():
            kcar[...] = jnp.zeros_like(kcar)
            vcar[...] = jnp.zeros_like(vcar)

        def settled(car, p_ref, idx):
            return car[idx] + p_ref[idx] if chained else car[idx]

        dk_ref[...] = settled(kcar, pk_ref, ...).astype(dtype)
        dv_ref[...] = settled(vcar, pv_ref, ...).astype(dtype)

        @pl.when(sb < steps)
        def _():
            sl2 = (slope_ref[2 * j], slope_ref[2 * j + 1])
            bias_rest = _stack_heads(sl2, _band_base(B_MAX_DIST, r, False))
            bias_0 = jnp.where(sb > 0, bias_rest, _stack_heads(sl2, _band_base(B_MAX_DIST, r, True)))

            def stream(rho):
                for i in range(nq):
                    cur = _stream(rho, i, r)
                    k_prev = kc_ref[_stream(rho, i - 1, r), :] if i > 0 else kp_ref[_stream(rho, 0, r), :]
                    v_prev = vc_ref[_stream(rho, i - 1, r), :] if i > 0 else vp_ref[_stream(rho, 0, r), :]
                    kb = jnp.concatenate([k_prev, kc_ref[cur, :]], axis=0).astype(BF16)
                    vb = jnp.concatenate([v_prev, vc_ref[cur, :]], axis=0).astype(BF16)
                    dq2, dk2, dv2, _ = _pair_bwd(q_ref[cur, :], kb, vb, do_ref[cur, :], o_ref[cur, :], lse_ref[cur, :],
                                                 bias_rest if i > 0 else bias_0, None, (0, 1), None)
                    dq_ref[cur, :] = (dq2 + pq_ref[cur, :] if chained else dq2).astype(dtype)
                    if i == 0:
                        last = (_stream(rho, nq - 1, r), slice(None))
                        dk_ref[last] = (settled(kcar, pk_ref, last) + dk2[:BLK]).astype(dtype)
                        dv_ref[last] = (settled(vcar, pv_ref, last) + dv2[:BLK]).astype(dtype)
                    else:
                        kcar[_stream(rho, i - 1, r), :] += dk2[:BLK]
                        vcar[_stream(rho, i - 1, r), :] += dv2[:BLK]
                    kcar[cur, :] = dk2[BLK:]
                    vcar[cur, :] = dv2[BLK:]

            _for_streams(r, stream, side_by_side=8)

    cur_step = lambda sb: jnp.minimum(sb, steps - 1)
    before = lambda sb: jnp.maximum(cur_step(sb) * nq - 1, 0)
    out_prev = lambda sb: jnp.maximum(sb - 1, 0)
    tile = lambda slab: pl.BlockSpec((None, rows, 128), lambda j, sb: (slab + j, cur_step(sb), 0))
    edge = lambda slab: pl.BlockSpec((None, BLK * r, 128), lambda j, sb: (slab + j, before(sb), 0))
    late = pl.BlockSpec((None, rows, 128), lambda j, sb: (j, out_prev(sb), 0))
    grads = [tile(0), late, late]
    return pl.pallas_call(
        body, name=f"attn_b_bwd_r{r}", grid=(NH // 2, steps + 1),
        in_specs=[SMEM, tile(qc), edge(kc), tile(kc), edge(vc), tile(vc), tile(0), tile(0), tile(0)]
        + (grads if chained else []),
        out_specs=grads,
        out_shape=[jax.ShapeDtypeStruct((4, s, 128), dtype)] * 3,
        scratch_shapes=[pltpu.VMEM((rows, 128), F32), pltpu.VMEM((rows, 128), F32)],
        compiler_params=_cp(("parallel", "arbitrary")),
    )(slopes, proj, proj, proj, proj, proj, d_o, o, lse, *(so_far if chained else ()))


def _row(v):
    return v.reshape(1, -1)


def _layer_norm_stats(z):
    mu = jnp.mean(z, axis=-1, keepdims=True)
    zc = z - mu
    var = jnp.mean(zc * zc, axis=-1, keepdims=True)
    rstd = lax.rsqrt(var + LN_EPS)
    return zc * rstd, rstd


def _layer_norm_bwd(dh, zh, rstd, g):
    dzh = dh * g
    return rstd * (dzh - jnp.mean(dzh, axis=-1, keepdims=True) - zh * jnp.mean(dzh * zh, axis=-1, keepdims=True))


def _rms(o):
    return lax.rsqrt(jnp.mean(o * o, axis=-1, keepdims=True) + RMS_EPS)


def _mix_ln1(x, o_a, o_b, norm_a_g, norm_b_g, w_o, ln1_g, ln1_b, tm=256):
    s = x.shape[0]

    def wide(ref):
        return jnp.concatenate([ref[j] for j in range(4)], axis=1)

    def body(x_ref, oa_ref, ob_ref, ga_ref, gb_ref, wo_ref, g_ref, b_ref, cat_ref, z1_ref, h1_ref, h1b_ref):
        oa, ob = wide(oa_ref), wide(ob_ref)
        na = oa * _rms(oa) * ga_ref[...]
        nb_ = ob * _rms(ob) * gb_ref[...]
        cat = jnp.concatenate([na, nb_], axis=1).astype(BF16)
        cat_ref[...] = cat
        z1 = ALPHA * x_ref[...] + _nn(cat, wo_ref[...])
        z1_ref[...] = z1
        zh, _ = _layer_norm_stats(z1)
        h1 = zh * g_ref[...] + b_ref[...]
        h1_ref[...] = h1
        h1b_ref[...] = h1.astype(BF16)

    t512 = pl.BlockSpec((4, tm, 128), lambda i: (0, i, 0))
    td = pl.BlockSpec((tm, D), lambda i: (i, 0))
    return pl.pallas_call(
        body, name="mix_ln1", grid=(s // tm,),
        in_specs=[td] + [t512] * 2 + [_const((1, 512))] * 2 + [_resident((D, D))] + [_const((1, D))] * 2,
        out_specs=[td, td, td, td],
        out_shape=[jax.ShapeDtypeStruct((s, D), BF16), jax.ShapeDtypeStruct((s, D), F32),
                   jax.ShapeDtypeStruct((s, D), F32), jax.ShapeDtypeStruct((s, D), BF16)],
        compiler_params=_cp(("parallel",)),
    )(x, o_a, o_b, _row(norm_a_g), _row(norm_b_g), w_o, _row(ln1_g), _row(ln1_b))


def _gelu_and_grad(x):
    c = math.sqrt(2.0 / math.pi)
    x2 = x * x
    s = 0.5 * jnp.tanh(x * ((c * 0.044715) * x2 + c)) + 0.5
    dg = s + (x * ((6.0 * c * 0.044715) * x2 + 2.0 * c)) * (s - s * s)
    return x * s, dg


def _shifted(u, edge, row, down):
    groups = [u[8 * i:8 * i + 8] for i in range(u.shape[0] // 8)]
    others = [edge] + groups[:-1] if down else groups[1:] + [edge]
    moved = []
    for k in (1, 2):
        crossing = row >= 8 - k if down else row < k
        moved.append(jnp.concatenate([pltpu.roll(jnp.where(crossing, o, g), k if down else 8 - k, 0)
                                      for o, g in zip(others, groups)], axis=0))
    return moved


def _up_proj(h1b, w_up, tm=512):
    s = h1b.shape[0]

    def body(h_ref, w_ref, o_ref):
        h = h_ref[...]
        for half in (0, 1):
            o_ref[half] = _nn(h, w_ref[:, half * FF:(half + 1) * FF]).astype(BF16)

    return pl.pallas_call(
        body, name="up_proj", grid=(s // tm,),
        in_specs=[pl.BlockSpec((tm, D), lambda i: (i, 0)), _resident((D, 2 * FF))],
        out_specs=pl.BlockSpec((2, tm, FF), lambda i: (0, i, 0)),
        out_shape=jax.ShapeDtypeStruct((2, s, FF), BF16),
        compiler_params=_cp(("parallel",)),
    )(h1b, w_up)


def _conv_gelu(up, cwb, tm=256, tn=FF // 2, chunk_rows=16):
    s = up.shape[1]
    n_c = tm // chunk_rows

    def body(up_ref, c_ref, a_ref, g_ref, a1_ref, carry):
        @pl.when(pl.program_id(1) == 0)
        def _():
            carry[...] = jnp.zeros_like(carry)

        row = lax.broadcasted_iota(jnp.int32, (8, tn), 0)
        edge = [carry[0], carry[1]]
        for c in range(n_c):
            rows = pl.ds(c * chunk_rows, chunk_rows)
            u = []
            for half in (0, 1):
                x = up_ref[half, rows, :].astype(F32)
                r1, r2 = _shifted(x, edge[half], row, True)
                u.append(r2 * c_ref[0, half:half + 1, :] + r1 * c_ref[1, half:half + 1, :]
                         + x * c_ref[2, half:half + 1, :] + c_ref[3, half:half + 1, :])
                edge[half] = x[chunk_rows - 8:]
            g, dg = _gelu_and_grad(u[0])
            a_ref[rows, :] = (g * u[1]).astype(BF16)
            g_ref[rows, :] = g.astype(BF16)
            a1_ref[rows, :] = (u[1] * dg).astype(BF16)
        for half in (0, 1):
            carry[half] = edge[half]

    pair = pl.BlockSpec((2, tm, tn), lambda j, i: (0, i, j))
    tile = pl.BlockSpec((tm, tn), lambda j, i: (i, j))
    return pl.pallas_call(
        body, name="conv_gelu", grid=(FF // tn, s // tm),
        in_specs=[pair, pl.BlockSpec((4, 2, tn), lambda j, i: (0, 0, j))],
        out_specs=[tile, tile, tile],
        out_shape=[jax.ShapeDtypeStruct((s, FF), BF16)] * 3,
        scratch_shapes=[pltpu.VMEM((2, 8, tn), F32)],
        compiler_params=_cp(("parallel", "arbitrary")),
    )(up, cwb)


def _down_ln2_loss(a, w_down, h1, target, ln2_g, ln2_b, tm=512):
    s = a.shape[0]

    def body(a_ref, w_ref, h_ref, t_ref, g_ref, b_ref, dz_ref, dzb_ref, st_ref):
        @pl.when(pl.program_id(0) == 0)
        def _():
            st_ref[...] = jnp.zeros_like(st_ref)

        z2 = ALPHA * h_ref[...] + _nn(a_ref[...], w_ref[...])
        zh, rstd = _layer_norm_stats(z2)
        diff = zh * g_ref[...] + b_ref[...] - t_ref[...]
        part = 0.5 * jnp.sum(jnp.mean(diff * diff, axis=-1, keepdims=True), axis=0, keepdims=True)
        dy = diff * (1.0 / D)
        st_ref[0:1, :] += jnp.sum(dy * zh, axis=0, keepdims=True)
        st_ref[1:2, :] += jnp.sum(dy, axis=0, keepdims=True)
        st_ref[2:3, :] += jnp.broadcast_to(part, (1, D))
        dz = _layer_norm_bwd(dy, zh, rstd, g_ref[...])
        dz_ref[...] = dz
        dzb_ref[...] = dz.astype(BF16)

    td = pl.BlockSpec((tm, D), lambda i: (i, 0))
    return pl.pallas_call(
        body, name="down_ln2_loss", grid=(s // tm,),
        in_specs=[pl.BlockSpec((tm, FF), lambda i: (i, 0)), _resident((FF, D)), td, td, _const((1, D)), _const((1, D))],
        out_specs=[td, td, _const((8, D))],
        out_shape=[jax.ShapeDtypeStruct((s, D), F32), jax.ShapeDtypeStruct((s, D), BF16),
                   jax.ShapeDtypeStruct((8, D), F32)],
        compiler_params=_cp(("arbitrary",)),
    )(a, w_down, h1, target, _row(ln2_g), _row(ln2_b))


def _d_act(dz2b, w_down, tm=512):
    s = dz2b.shape[0]

    def body(dz_ref, w_ref, o_ref):
        o_ref[...] = _nt(dz_ref[...], w_ref[...]).astype(BF16)

    return pl.pallas_call(
        body, name="d_act", grid=(s // tm,),
        in_specs=[pl.BlockSpec((tm, D), lambda i: (i, 0)), _resident((FF, D))],
        out_specs=pl.BlockSpec((tm, FF), lambda i: (i, 0)),
        out_shape=jax.ShapeDtypeStruct((s, FF), BF16),
        compiler_params=_cp(("parallel",)),
    )(dz2b, w_down)


def _conv_gelu_bwd(da, up, g, a1, cwb, tm=256, tn=FF // 2, chunk_rows=16):
    s = da.shape[0]
    n_i = s // tm
    n_c = tm // chunk_rows

    def body(da_ref, up_ref, g_ref, a1_ref, c_ref, dup_ref, dc_ref, carry):
        @pl.when(pl.program_id(1) == 0)
        def _():
            carry[...] = jnp.zeros_like(carry)
            dc_ref[...] = jnp.zeros_like(dc_ref)

        def fold(v):
            return jnp.sum(v.reshape(chunk_rows // 8, 8, v.shape[1]), axis=0)

        def chunk(cc, state):
            after, sums = state
            rows = pl.ds((n_c - 1 - cc) * chunk_rows, chunk_rows)
            da_c = da_ref[rows, :].astype(F32)
            dus = (da_c * a1_ref[rows, :].astype(F32), da_c * g_ref[rows, :].astype(F32))
            head, new_sums = [], []
            for half in (0, 1):
                du = dus[half]
                up = up_ref[half, rows, :].astype(F32)
                l1, l2 = _shifted(du, after[half], row, False)
                dup = (du * c_ref[2, half:half + 1, :] + l1 * c_ref[1, half:half + 1, :]
                       + l2 * c_ref[0, half:half + 1, :])
                dup_ref[half, rows, :] = dup.astype(BF16)
                parts = (fold(l2 * up), fold(l1 * up), fold(du * up), fold(du))
                new_sums.append(parts if sums is None else tuple(a + b for a, b in zip(sums[half], parts)))
                head.append(du[:8])
            return tuple(head), new_sums

        row = lax.broadcasted_iota(jnp.int32, (8, tn), 0)
        state = ((carry[0], carry[1]), None)
        for cc in range(n_c):
            state = chunk(cc, state)
        head, sums = state
        for half in (0, 1):
            carry[half] = head[half]
            for k in range(4):
                dc_ref[k, half:half + 1, :] += jnp.sum(sums[half][k], axis=0, keepdims=True)

    rev = lambda ii: n_i - 1 - ii
    tile = pl.BlockSpec((tm, tn), lambda j, ii: (rev(ii), j))
    pair = pl.BlockSpec((2, tm, tn), lambda j, ii: (0, rev(ii), j))
    per_col = pl.BlockSpec((4, 2, tn), lambda j, ii: (0, 0, j))
    return pl.pallas_call(
        body, name="conv_gelu_bwd", grid=(FF // tn, n_i),
        in_specs=[tile, pair, tile, tile, per_col],
        out_specs=[pair, per_col],
        out_shape=[jax.ShapeDtypeStruct((2, s, FF), BF16), jax.ShapeDtypeStruct((4, 2, FF), F32)],
        scratch_shapes=[pltpu.VMEM((2, 8, tn), F32)],
        compiler_params=_cp(("parallel", "arbitrary")),
    )(da, up, g, a1, cwb)


def _dh1_ln1_bwd(dz2, dup, w_up, z1, ln1_g, tm=512):
    s = dz2.shape[0]

    def body(dz2_ref, dup_ref, w_ref, z1_ref, g_ref, dz1_ref, dz1b_ref, st_ref):
        @pl.when(pl.program_id(0) == 0)
        def _():
            st_ref[...] = jnp.zeros_like(st_ref)

        dh = ALPHA * dz2_ref[...] + _nt(dup_ref[0], w_ref[:, :FF]) + _nt(dup_ref[1], w_ref[:, FF:])
        zh, rstd = _layer_norm_stats(z1_ref[...])
        st_ref[0:1, :] += jnp.sum(dh * zh, axis=0, keepdims=True)
        st_ref[1:2, :] += jnp.sum(dh, axis=0, keepdims=True)
        dz = _layer_norm_bwd(dh, zh, rstd, g_ref[...])
        dz1_ref[...] = dz
        dz1b_ref[...] = dz.astype(BF16)

    td = pl.BlockSpec((tm, D), lambda i: (i, 0))
    return pl.pallas_call(
        body, name="dh1_ln1_bwd", grid=(s // tm,),
        in_specs=[td, pl.BlockSpec((2, tm, FF), lambda i: (0, i, 0)), _resident((D, 2 * FF)), td, _const((1, D))],
        out_specs=[td, td, _const((8, D))],
        out_shape=[jax.ShapeDtypeStruct((s, D), F32), jax.ShapeDtypeStruct((s, D), BF16),
                   jax.ShapeDtypeStruct((8, D), F32)],
        compiler_params=_cp(("arbitrary",), 58),
    )(dz2, dup, w_up, z1, _row(ln1_g))


def _dcat_rms_bwd(dz1b, w_o, o_a, o_b, norm_a_g, norm_b_g, tm=512):
    s = dz1b.shape[0]

    def body(dz_ref, w_ref, oa_ref, ob_ref, ga_ref, gb_ref, da_ref, db_ref, st_ref):
        @pl.when(pl.program_id(0) == 0)
        def _():
            st_ref[...] = jnp.zeros_like(st_ref)

        dcat = _nt(dz_ref[...], w_ref[...])
        for k, (o_ref, g_ref, d_ref) in enumerate(((oa_ref, ga_ref, da_ref), (ob_ref, gb_ref, db_ref))):
            o = jnp.concatenate([o_ref[j] for j in range(4)], axis=1)
            dn = dcat[:, 512 * k:512 * (k + 1)]
            rr = _rms(o)
            oh = o * rr
            st_ref[k:k + 1, :] += jnp.sum(dn * oh, axis=0, keepdims=True)
            doh = dn * g_ref[...]
            d_o = rr * (doh - oh * jnp.mean(doh * oh, axis=-1, keepdims=True))
            for j in range(4):
                d_ref[j] = d_o[:, 128 * j:128 * (j + 1)]

    t512 = pl.BlockSpec((4, tm, 128), lambda i: (0, i, 0))
    return pl.pallas_call(
        body, name="dcat_rms_bwd", grid=(s // tm,),
        in_specs=[pl.BlockSpec((tm, D), lambda i: (i, 0)), _resident((D, D)), t512, t512,
                  _const((1, 512)), _const((1, 512))],
        out_specs=[t512, t512, _const((8, 512))],
        out_shape=[jax.ShapeDtypeStruct((4, s, 128), F32), jax.ShapeDtypeStruct((4, s, 128), F32),
                   jax.ShapeDtypeStruct((8, 512), F32)],
        compiler_params=_cp(("arbitrary",)),
    )(dz1b, w_o, o_a, o_b, _row(norm_a_g), _row(norm_b_g))


def _grad_w_in(dparts, xb, tk=2048):
    s = xb.shape[0]
    nk = s // tk

    def body(qa, ka, va, qb, kb, vb, x_ref, o_ref, ob_ref):
        i = pl.program_id(0)
        k = pl.program_id(1)

        @pl.when(k == 0)
        def _():
            o_ref[...] = jnp.zeros_like(o_ref)

        def add(blocks):
            o_ref[...] += _tn(jnp.concatenate(blocks, axis=1), x_ref[...])

        pl.when(i == 0)(lambda: add([qa[j] for j in range(4)] + [ka[...], va[...]]))
        pl.when(i == 1)(lambda: add([qb[j] for j in range(4)] + [kb[0], kb[1]]))
        pl.when(i == 2)(lambda: add([kb[0], kb[1]] + [vb[j] for j in range(4)]))

        @pl.when(k == nk - 1)
        def _():
            ob_ref[...] = o_ref[...].astype(BF16)

    def during(tile):
        return lambda i, k: jnp.where(i == tile, k, jnp.where(i < tile, 0, nk - 1))

    quad = lambda tile: pl.BlockSpec((4, tk, 128), lambda i, k: (0, during(tile)(i, k), 0))
    one = pl.BlockSpec((tk, 128), lambda i, k: (during(0)(i, k), 0))
    kb_spec = pl.BlockSpec((2, tk, 128), lambda i, k: (jnp.where(i == 2, 1, 0), jnp.where(i == 0, 0, k), 0))
    return pl.pallas_call(
        body, name="grad_w_in", grid=(3, nk),
        in_specs=[quad(0), one, one, quad(1), kb_spec, quad(2), pl.BlockSpec((tk, D), lambda i, k: (k, 0))],
        out_specs=[pl.BlockSpec((WA, D), lambda i, k: (i, 0))] * 2,
        out_shape=[jax.ShapeDtypeStruct((WIN, D), F32), jax.ShapeDtypeStruct((WIN, D), BF16)],
        compiler_params=_cp(("parallel", "arbitrary"), mb=56),
    )(*dparts, xb)


def _grad_x(dz1, dparts, w_in_t, zero, tm=512):
    s = dz1.shape[0]

    def body(dz_ref, qa, ka, va, qb, kb, vb, w_ref, z_ref, o_ref):
        dp = jnp.concatenate([qa[j] for j in range(4)] + [ka[...], va[...]]
                             + [ref[j] for ref in (qb, kb, vb) for j in range(4)], axis=1)
        o_ref[...] = ALPHA * dz_ref[...] + _nn(dp, w_ref[...]) + z_ref[0:1, 0:1]

    td = pl.BlockSpec((tm, D), lambda i: (i, 0))
    quad = pl.BlockSpec((4, tm, 128), lambda i: (0, i, 0))
    one = pl.BlockSpec((tm, 128), lambda i: (i, 0))
    return pl.pallas_call(
        body, name="grad_x", grid=(s // tm,),
        in_specs=[td, quad, one, one, quad, quad, quad, _resident((WIN, D)), _const((8, 128))],
        out_specs=td, out_shape=jax.ShapeDtypeStruct((s, D), F32),
        compiler_params=_cp(("parallel",)),
    )(dz1, *dparts, w_in_t, zero)


def _place():
    return lax.axis_index("x"), lax.axis_index("y"), lax.axis_index("c")


def _other_chips(x, y):
    return [(1 - x, y), (x, 1 - y), (1 - x, 1 - y)]


def _hbm(a):
    return pltpu.with_memory_space_constraint(a, pltpu.HBM)


def _gather_w_in(shard, conv_w):
    rows_k = shard.shape[0]
    half = rows_k // 2

    def body(src, conv_src, out, conv_out, send_sems, recv_sems):
        x, y, c = _place()
        b = 2 * x + y
        sibling = (x, y, 1 - c)
        chips = _other_chips(x, y)

        def copy(idx, chip_b, core, to, first_hop=False):
            rows = out.at[pl.ds(pl.multiple_of(chip_b * rows_k + core * half, 16), half)]
            s_ref = src.at[pl.ds(pl.multiple_of(core * half, 16), half)] if first_hop else rows
            return pltpu.make_async_remote_copy(src_ref=s_ref, dst_ref=rows, send_sem=send_sems.at[idx],
                                                recv_sem=recv_sems.at[idx], device_id=to, device_id_type=MESH)

        def own_copy():
            return pltpu.make_async_remote_copy(
                src_ref=src, dst_ref=out.at[pl.ds(pl.multiple_of(b * rows_k, 16), rows_k)], send_sem=send_sems.at[6],
                recv_sem=recv_sems.at[6], device_id=sibling, device_id_type=MESH)

        def conv_copy(idx, chip_b, to):
            return pltpu.make_async_remote_copy(src_ref=conv_src, dst_ref=conv_out.at[chip_b],
                                                send_sem=send_sems.at[7 + idx], recv_sem=recv_sems.at[7 + idx],
                                                device_id=to, device_id_type=MESH)

        started = [own_copy(), conv_copy(3, b, sibling)]
        for jn, chip in enumerate(chips):
            started += [copy(jn, b, c, (chip[0], chip[1], c), first_hop=True), conv_copy(jn, b, (chip[0], chip[1], c))]
        for cp in started:
            cp.start()
        for jn, chip in enumerate(chips):
            cb = 2 * chip[0] + chip[1]
            copy(jn, cb, c, (chip[0], chip[1], c)).wait_recv()
            cp = copy(3 + jn, cb, c, sibling)
            cp.start()
            started.append(cp)
        for jn, chip in enumerate(chips):
            cb = 2 * chip[0] + chip[1]
            copy(3 + jn, cb, 1 - c, sibling).wait_recv()
            conv_copy(jn, cb, (chip[0], chip[1], c)).wait_recv()
        own_copy().wait_recv()
        conv_copy(3, b, sibling).wait_recv()
        for cp in started:
            cp.wait_send()

    return pl.pallas_call(
        body, name="gather_w_in",
        in_specs=[ANY, ANY], out_specs=[ANY, ANY],
        out_shape=[jax.ShapeDtypeStruct((N_CHIPS * rows_k, D), BF16), jax.ShapeDtypeStruct((N_CHIPS,) + conv_w.shape, F32)],
        scratch_shapes=[pltpu.SemaphoreType.DMA((11,)), pltpu.SemaphoreType.DMA((11,))],
        compiler_params=pltpu.CompilerParams(has_side_effects=True),
    )(shard, conv_w)


def _weight_copies(shard, land, send_sems, recv_sems, arrivals):
    x, y, c = _place()
    n_rows, n_cols = shard.shape
    peers = [(px, py, c) for px, py in _other_chips(x, y)] + [(x, y, 1 - c)]
    cps = []
    for jn, peer in enumerate(peers):
        at = 2 * peer[0] + peer[1] if arrivals else 2 * x + y
        if land.shape[1] == n_cols:
            dst = land.at[pl.ds(pl.multiple_of(at * n_rows, 16), n_rows)]
        else:
            dst = land.at[:, pl.ds(pl.multiple_of(at * n_cols, 128), n_cols)]
        cps.append(pltpu.make_async_remote_copy(src_ref=shard, dst_ref=dst, send_sem=send_sems.at[jn],
                                                recv_sem=recv_sems.at[jn], device_id=peer, device_id_type=MESH))
    return cps


def _weights_start(shards, after):
    n = len(shards)
    lands = [lax.empty((N_CHIPS * sh.shape[0], D) if sh.shape[1] == D else (D, N_CHIPS * sh.shape[1]), BF16)
             for sh in shards]

    def body(*refs):
        src, land = refs[:n], refs[n:2 * n]
        send_sems, recv_sems = refs[2 * n + 1:3 * n + 1], refs[3 * n + 1:4 * n + 1]
        for k in range(n):
            for send in _weight_copies(src[k], land[k], send_sems[k], recv_sems[k], False):
                send.start()
        refs[-1][...] = jnp.zeros_like(refs[-1])

    res = pl.pallas_call(
        body, name="weights_start",
        in_specs=[HBM] * (2 * n) + [ANY], out_specs=[SEM] * (2 * n) + [HBM] * (2 * n) + [VMEM],
        out_shape=[pltpu.SemaphoreType.DMA((4,))] * (2 * n)
        + [pltpu.HBM(a.shape, a.dtype) for a in (*shards, *lands)] + [jax.ShapeDtypeStruct((8, 128), F32)],
        input_output_aliases={i: i + 2 * n for i in range(2 * n)},
        compiler_params=pltpu.CompilerParams(has_side_effects=DATAFLOW),
    )(*[_hbm(a) for a in (*shards, *lands)], after)
    return [(res[k], res[n + k], res[2 * n + k], res[3 * n + k]) for k in range(n)], res[-1]


def _weights_wait(started, after, name):
    send_sems, recv_sems, shard, land = started

    def body(s_ref, l_ref, send_ref, recv_ref, after_ref, s_out, l_out):
        for cp in _weight_copies(s_ref, l_ref, send_ref, recv_ref, True):
            cp.wait_send()
            cp.wait_recv()

    return pl.pallas_call(
        body, name=name,
        in_specs=[HBM, HBM, SEM, SEM, ANY], out_specs=[HBM, HBM],
        out_shape=[pltpu.HBM(shard.shape, shard.dtype), pltpu.HBM(land.shape, land.dtype)],
        input_output_aliases={0: 0, 1: 1},
        compiler_params=pltpu.CompilerParams(has_side_effects=DATAFLOW),
    )(shard, land, send_sems, recv_sems, after)[1]


def _grad_copies(g_ref, land_ref, send_sems, recv_sems):
    x, y, c = _place()
    cps = []
    for d in range(1, 8):
        px, py, pc = x ^ (d >> 2), y ^ ((d >> 1) & 1), c ^ (d & 1)
        cps.append(pltpu.make_async_remote_copy(
            src_ref=g_ref.at[2 * px + py, pc], dst_ref=land_ref.at[d - 1], send_sem=send_sems.at[d - 1],
            recv_sem=recv_sems.at[d - 1], device_id=(px, py, pc), device_id_type=MESH))
    return cps


def _grads_start(grads_b, name):
    n = len(grads_b)
    lands = [lax.empty((7, g.shape[2], D), BF16) for g in grads_b]

    def body(*refs):
        g, land = refs[:n], refs[n:2 * n]
        send_sems, recv_sems = refs[2 * n:3 * n], refs[3 * n:4 * n]
        for k in range(n):
            for cp in _grad_copies(g[k], land[k], send_sems[k], recv_sems[k]):
                cp.start()
        refs[-1][...] = jnp.zeros_like(refs[-1])

    res = pl.pallas_call(
        body, name=name,
        in_specs=[HBM] * (2 * n), out_specs=[SEM] * (2 * n) + [HBM] * (2 * n) + [VMEM],
        out_shape=[pltpu.SemaphoreType.DMA((7,))] * (2 * n)
        + [pltpu.HBM(a.shape, a.dtype) for a in (*grads_b, *lands)] + [jax.ShapeDtypeStruct((8, 128), F32)],
        input_output_aliases={i: i + 2 * n for i in range(2 * n)},
        compiler_params=pltpu.CompilerParams(has_side_effects=DATAFLOW),
    )(*[_hbm(a) for a in (*grads_b, *lands)])
    return [(res[k], res[n + k], res[2 * n + k], res[3 * n + k]) for k in range(n)], res[-1]


def _grads_wait(started, after, name):
    n = len(started)

    def body(*refs):
        g, land = refs[:n], refs[n:2 * n]
        send_sems, recv_sems = refs[2 * n:3 * n], refs[3 * n:4 * n]
        for k in range(n):
            for cp in _grad_copies(g[k], land[k], send_sems[k], recv_sems[k]):
                cp.wait_send()
                cp.wait_recv()

    gs = [st[2] for st in started]
    lands = [st[3] for st in started]
    res = pl.pallas_call(
        body, name=name,
        in_specs=[HBM] * (2 * n) + [SEM] * (2 * n) + [ANY], out_specs=[HBM] * (2 * n),
        out_shape=[pltpu.HBM(a.shape, a.dtype) for a in (*gs, *lands)],
        input_output_aliases={i: i for i in range(2 * n)},
        compiler_params=pltpu.CompilerParams(has_side_effects=DATAFLOW),
    )(*gs, *lands, *[st[0] for st in started], *[st[1] for st in started], after)
    return res[n:]


def _sum_partials(grad4, got, cb, name, tr):
    h = grad4.shape[2]
    per_half = h // tr

    def body(cb_ref, g_ref, o_ref, out_ref):
        acc = g_ref[...]
        for j in range(7):
            acc = acc + o_ref[j].astype(F32)
        out_ref[...] = acc

    return pl.pallas_call(
        body, name=name,
        grid_spec=pltpu.PrefetchScalarGridSpec(
            num_scalar_prefetch=1, grid=(per_half,),
            in_specs=[pl.BlockSpec((None, None, tr, D), lambda i, cb_ref: (cb_ref[1], cb_ref[0], i, 0)),
                      pl.BlockSpec((7, tr, D), lambda i, cb_ref: (0, i, 0))],
            out_specs=pl.BlockSpec((tr, D), lambda i, cb_ref: (cb_ref[0] * per_half + i, 0))),
        out_shape=jax.ShapeDtypeStruct((2 * h, D), F32),
        compiler_params=_cp(("arbitrary",)),
    )(cb, grad4, got)


def _swap_halves(shards, name):
    n = len(shards)

    def body(*refs):
        out, send_sems, recv_sems = refs[n:2 * n], refs[2 * n], refs[2 * n + 1]
        x, y, c = _place()
        cps = []
        for k in range(n):
            h = shards[k].shape[0] // 2
            mine = out[k].at[pl.ds(pl.multiple_of(c * h, 8), h)]
            cp = pltpu.make_async_remote_copy(src_ref=mine, dst_ref=mine, send_sem=send_sems.at[k],
                                              recv_sem=recv_sems.at[k], device_id=(x, y, 1 - c), device_id_type=MESH)
            cp.start()
            cps.append(cp)
        for cp in cps:
            cp.wait()

    return pl.pallas_call(
        body, name=name,
        in_specs=[ANY] * n, out_specs=[ANY] * n,
        out_shape=[jax.ShapeDtypeStruct(sh.shape, F32) for sh in shards],
        input_output_aliases={k: k for k in range(n)},
        scratch_shapes=[pltpu.SemaphoreType.DMA((n,)), pltpu.SemaphoreType.DMA((n,))],
        compiler_params=pltpu.CompilerParams(has_side_effects=True),
    )(*shards)


def _small_copies(small_ref, land_ref, send_sems, recv_sems):
    x, y, c = _place()
    me = 4 * x + 2 * y + c
    cps = []
    for d in range(1, 8):
        px, py, pc = x ^ (d >> 2), y ^ ((d >> 1) & 1), c ^ (d & 1)
        cps.append(pltpu.make_async_remote_copy(
            src_ref=small_ref, dst_ref=land_ref.at[me], send_sem=send_sems.at[d - 1], recv_sem=recv_sems.at[d - 1],
            device_id=(px, py, pc), device_id_type=MESH))
    return cps


def _small_start(small):
    land = lax.empty((8,) + small.shape, F32)

    def body(s_ref, l_ref, send_sems, recv_sems, s_thru, l_thru, token):
        for cp in _small_copies(s_ref, l_ref, send_sems, recv_sems):
            cp.start()
        token[...] = jnp.zeros_like(token)

    res = pl.pallas_call(
        body, name="small_start",
        in_specs=[HBM, HBM], out_specs=[SEM, SEM, HBM, HBM, VMEM],
        out_shape=[pltpu.SemaphoreType.DMA((7,)), pltpu.SemaphoreType.DMA((7,)), pltpu.HBM(small.shape, F32),
                   pltpu.HBM(land.shape, F32), jax.ShapeDtypeStruct((8, 128), F32)],
        input_output_aliases={0: 2, 1: 3},
        compiler_params=pltpu.CompilerParams(has_side_effects=DATAFLOW),
    )(_hbm(small), _hbm(land))
    return res[:4], res[4]


def _small_wait(started, after):
    send_sems, recv_sems, small, land = started

    def body(s_ref, l_ref, send_ref, recv_ref, after_ref, s_out, l_out):
        for cp in _small_copies(s_ref, l_ref, send_ref, recv_ref):
            cp.wait_send()
            cp.wait_recv()

    return pl.pallas_call(
        body, name="small_wait",
        in_specs=[HBM, HBM, SEM, SEM, ANY], out_specs=[HBM, HBM],
        out_shape=[pltpu.HBM(small.shape, F32), pltpu.HBM(land.shape, F32)],
        input_output_aliases={0: 0, 1: 1},
        compiler_params=pltpu.CompilerParams(has_side_effects=DATAFLOW),
    )(small, land, send_sems, recv_sems, after)


def _small_sum(small, land, me):
    rows = small.shape[0]

    def body(me_ref, s_ref, l_ref, o_ref):
        acc = None
        for k in range(8):
            term = jnp.where(me_ref[0] == k, s_ref[...], l_ref[k])
            acc = term if k == 0 else acc + term
        o_ref[...] = acc

    return pl.pallas_call(
        body, name="small_sum",
        in_specs=[SMEM, VMEM, VMEM], out_specs=VMEM,
        out_shape=jax.ShapeDtypeStruct((rows, D), F32),
    )(me, small, land)


def _adamw(w, g, m, v, name, tr):
    rows, cols = w.shape

    def body(w_ref, g_ref, m_ref, v_ref, d_ref, nm_ref, nv_ref):
        g_ = g_ref[...]
        nm = ADAM_B1 * m_ref[...] + (1.0 - ADAM_B1) * g_
        nv = ADAM_B2 * v_ref[...] + (1.0 - ADAM_B2) * (g_ * g_)
        m_hat = nm / (1.0 - ADAM_B1 ** ADAM_STEP)
        v_hat = nv / (1.0 - ADAM_B2 ** ADAM_STEP)
        d_ref[...] = -ADAM_LR * (m_hat / (jnp.sqrt(v_hat) + ADAM_EPS) + ADAM_WD * w_ref[...])
        nm_ref[...] = nm
        nv_ref[...] = nv

    spec = pl.BlockSpec((tr, cols), lambda i: (i, 0))
    return pl.pallas_call(
        body, name=name, grid=(rows // tr,),
        in_specs=[spec] * 4, out_specs=[spec] * 3,
        out_shape=[jax.ShapeDtypeStruct((rows, cols), F32)] * 3,
        compiler_params=_cp(("parallel",)),
    )(w, g, m, v)


def _local_step(x, target, w_in_t, late_weights, norm_a_g, norm_b_g, sinks_a, ln1_g, ln1_b,
                conv_w, conv_b, ln2_g, ln2_b, slopes, on_grad, on_small):
    cwb = jnp.concatenate([conv_w, conv_b[None]], axis=0).reshape(4, 2, FF)

    proj, xb = _proj(x, w_in_t, "proj")
    o_a, lse_a = _attn_a_fwd(proj, sinks_a)
    fwd_b = None
    for r in reversed(B_DILATIONS):
        fwd_b = _attn_b_fwd(proj, slopes, r, fwd_b)
    o_b, lse_b = fwd_b
    w_o = late_weights(1, lse_b)
    cat, z1, h1, h1b = _mix_ln1(x, o_a, o_b, norm_a_g, norm_b_g, w_o, ln1_g, ln1_b)
    w_up = late_weights(2, h1b)
    up = _up_proj(h1b, w_up)
    a, gate, a1 = _conv_gelu(up, cwb)
    w_down = late_weights(3, a)
    dz2, dz2b, st2 = _down_ln2_loss(a, w_down, h1, target, ln2_g, ln2_b)

    on_grad(3, *_grad_w(a, dz2b, "grad_w_down", tm=FF // 2))
    dup, dconv = _conv_gelu_bwd(_d_act(dz2b, w_down), up, gate, a1, cwb)
    on_grad(2, *_grad_w(dup, h1b, "grad_w_up", tm=FF // 2, lhs_halves=True))
    dz1, dz1b, st1 = _dh1_ln1_bwd(dz2, dup, w_up, z1, ln1_g)
    tok = on_grad(1, *_grad_w(cat, dz1b, "grad_w_o", tm=512))
    d_oa, d_ob, st_n = _dcat_rms_bwd(dz1b, w_o, o_a, o_b, norm_a_g + tok[0, 0], norm_b_g)
    dqa, dka, dva, dsink = _attn_a_bwd(proj, sinks_a, d_oa, o_a, lse_a)
    dconv = dconv.reshape(4, 2 * FF)
    tok = on_small(dict(loss=st2[2, 0:1], norm_a_g=st_n[0], norm_b_g=st_n[1], sinks_a=dsink[:, 0],
                        ln1_g=st1[0], ln1_b=st1[1], conv_w=dconv[0:3].reshape(-1), conv_b=dconv[3],
                        ln2_g=st2[0], ln2_b=st2[1]))
    slopes = slopes + tok[0, 0]
    bwd_b = None
    for r in reversed(B_DILATIONS):
        bwd_b = _attn_b_bwd(proj, slopes, d_ob, o_b, lse_b, r, bwd_b, BF16 if r == 1 else F32)
    dparts = (dqa, dka, dva, *bwd_b)
    tok = on_grad(0, *_grad_w_in(dparts, xb))
    return _grad_x(dz1, dparts, w_in_t, tok)


SMALL_ORDER = ("loss", "norm_a_g", "norm_b_g", "sinks_a", "ln1_g", "ln1_b", "conv_b", "ln2_g", "ln2_b", "conv_w")
SMALL_SIZES = dict(loss=1, norm_a_g=512, norm_b_g=512, sinks_a=8, ln1_g=D, ln1_b=D, conv_b=2 * FF, ln2_g=D, ln2_b=D,
                   conv_w=3 * 2 * FF)


def _pack(parts, rows):
    flat = jnp.concatenate([parts[k].reshape(-1).astype(F32) for k in parts])
    return jnp.pad(flat, (0, rows * D - flat.shape[0])).reshape(rows, D)


def _unpack(buf, names, sizes):
    flat = buf.reshape(-1)
    out, at = {}, 0
    for k in names:
        out[k] = flat[at:at + sizes[k]]
        at += sizes[k]
    return out


def kernel(x, w_in, norm_a_g, norm_b_g, sinks_a, w_o, ln1_g, ln1_b, w_up, conv_w, conv_b, w_down, ln2_g, ln2_b, loss_target, m_w_in, m_norm_a_g, m_norm_b_g, m_sinks_a, m_w_o, m_ln1_g, m_ln1_b, m_w_up, m_conv_w, m_conv_b, m_w_down, m_ln2_g, m_ln2_b, v_w_in, v_norm_a_g, v_norm_b_g, v_sinks_a, v_w_o, v_ln1_g, v_ln1_b, v_w_up, v_conv_w, v_conv_b, v_w_down, v_ln2_g, v_ln2_b):
    xi, yi, ci = _place()
    chip = (2 * xi + yi).astype(I32)
    core = ci.astype(I32)

    w_in_rows, m_w_in_rows, v_w_in_rows = w_in.T, m_w_in.T, v_w_in.T
    shards = (w_in_rows.astype(BF16), w_o.astype(BF16), w_up.astype(BF16), w_down.astype(BF16))
    w_in_t, conv_w4 = _gather_w_in(shards[0], conv_w)
    conv_w_f = conv_w4.transpose(1, 0, 2).reshape(3, 2 * FF)
    w_started, w_tok = _weights_start(shards[1:], conv_w4)
    slopes = jnp.asarray(SLOPES, F32) + w_tok[0, 0]

    halves_rows = [r // 2 for r in SHARD_ROWS]
    grads4, grads_b4, started = [None] * 4, [None] * 4, [None] * 4

    def on_grad(k, g, g_b):
        grads4[k] = g.reshape(N_CHIPS, 2, halves_rows[k], D)
        grads_b4[k] = g_b.reshape(N_CHIPS, 2, halves_rows[k], D)
        if k > 1:
            return None
        group = (1, 2, 3) if k == 1 else (0,)
        sts, tok = _grads_start([grads_b4[i] for i in group], f"grads_start_{k}")
        for i, st in zip(group, sts):
            started[i] = st
        return tok

    small_rows = 32
    small_started = []

    def on_small(parts):
        st, tok = _small_start(_pack({k: parts[k] for k in SMALL_ORDER}, small_rows))
        small_started.append(st)
        return tok

    gx = _local_step(
        x[0], loss_target[0], w_in_t, lambda k, after: _weights_wait(w_started[k - 1], after, f"weights_wait_{k}"),
        norm_a_g, norm_b_g, sinks_a, ln1_g, ln1_b, conv_w_f, conv_b, ln2_g, ln2_b, slopes, on_grad, on_small)

    tiles = (96, 128, 352, 176)
    core_chip = jnp.stack([core, chip])
    got = _grads_wait(started[1:], gx, "grads_wait_1")
    halves = [_sum_partials(grads4[k], got[k - 1], core_chip, f"sum_partials_{k}", tiles[k]) for k in (1, 2, 3)]
    g_w_o, g_w_up_rows, g_w_down = _swap_halves(halves, "swap_halves")
    g_w_up = g_w_up_rows.T
    delta, new_m, new_v = {}, {}, {}
    for k, g, tr in (("w_o", g_w_o, 128), ("w_up", g_w_up, 256), ("w_down", g_w_down, 176)):
        delta[k], new_m[k], new_v[k] = _adamw(dict(w_o=w_o, w_up=w_up, w_down=w_down)[k], g,
                                              dict(w_o=m_w_o, w_up=m_w_up, w_down=m_w_down)[k],
                                              dict(w_o=v_w_o, w_up=v_w_up, w_down=v_w_down)[k], f"adamw_{k}", tr)

    got = _grads_wait(started[:1], delta["w_up"], "grads_wait_0")
    half_in = _sum_partials(grads4[0], got[0], core_chip, "sum_partials_0", tiles[0])
    (g_w_in_rows,) = _swap_halves([half_in], "swap_halves_in")
    small_mine, small_land = _small_wait(small_started[0], g_w_in_rows)
    totals = _small_sum(small_mine, small_land, (4 * xi + 2 * yi + ci).astype(I32).reshape(1))
    tot = _unpack(totals, SMALL_ORDER, SMALL_SIZES)
    loss = tot["loss"][0]
    cols = 2 * FF // N_CHIPS
    g_conv_w = lax.dynamic_slice(tot["conv_w"].reshape(3, 2 * FF), (0, chip * cols), (3, cols))
    g_small = dict(norm_a_g=tot["norm_a_g"], norm_b_g=tot["norm_b_g"], sinks_a=tot["sinks_a"], ln1_g=tot["ln1_g"],
                   ln1_b=tot["ln1_b"], conv_w=g_conv_w, conv_b=tot["conv_b"], ln2_g=tot["ln2_g"], ln2_b=tot["ln2_b"])

    weights = dict(w_in=w_in, norm_a_g=norm_a_g, norm_b_g=norm_b_g, sinks_a=sinks_a, w_o=w_o, ln1_g=ln1_g, ln1_b=ln1_b,
                   w_up=w_up, conv_w=conv_w, conv_b=conv_b, w_down=w_down, ln2_g=ln2_g, ln2_b=ln2_b)
    ms = dict(w_in=m_w_in, norm_a_g=m_norm_a_g, norm_b_g=m_norm_b_g, sinks_a=m_sinks_a, w_o=m_w_o, ln1_g=m_ln1_g,
              ln1_b=m_ln1_b, w_up=m_w_up, conv_w=m_conv_w, conv_b=m_conv_b, w_down=m_w_down, ln2_g=m_ln2_g, ln2_b=m_ln2_b)
    vs = dict(w_in=v_w_in, norm_a_g=v_norm_a_g, norm_b_g=v_norm_b_g, sinks_a=v_sinks_a, w_o=v_w_o, ln1_g=v_ln1_g,
              ln1_b=v_ln1_b, w_up=v_w_up, conv_w=v_conv_w, conv_b=v_conv_b, w_down=v_w_down, ln2_g=v_ln2_g, ln2_b=v_ln2_b)
    order = list(weights)
    grad = dict(g_small, w_in=g_w_in_rows.T, w_o=g_w_o, w_up=g_w_up, w_down=g_w_down)

    delta["w_in"], new_m["w_in"], new_v["w_in"] = [
        a.T for a in _adamw(w_in_rows, g_w_in_rows, m_w_in_rows, v_w_in_rows, "adamw_w_in", 144)]
    small_names = [k for k in order if k not in delta]
    sizes = {k: weights[k].size for k in small_names}
    rows = 16
    packed = [_pack({k: src[k] for k in small_names}, rows) for src in (weights, grad, ms, vs)]
    for res, buf in zip((delta, new_m, new_v), _adamw(*packed, "adamw_small", rows)):
        for k, val in _unpack(buf, small_names, sizes).items():
            res[k] = val.reshape(weights[k].shape)

    return (loss, gx[None], *[grad[k] for k in order], *[delta[k] for k in order],
            *[new_m[k] for k in order], *[new_v[k] for k in order])
```

```python
import functools
import math

import jax
import jax.numpy as jnp
from jax import lax
from jax.experimental import pallas as pl
from jax.experimental.pallas import tpu as pltpu

F32, BF16, I32 = jnp.float32, jnp.bfloat16, jnp.int32

D = 1024
FF = 2816
HD = 64
NH = 8
WA, WB = 768, 1536
WIN = WA + WB
BLK = 128
ALPHA = 2.0 ** 0.25
LN_EPS, RMS_EPS = 1e-5, 1e-6
SCALE = 1.0 / math.sqrt(HD)
A_MAX_DIST, B_MAX_DIST = 127, 128
B_DILATIONS = (1, 4, 16)
SLOPES = tuple(2.0 ** (-(i + 1)) for i in range(NH))
SHARD_ROWS = (WIN // 4, D // 4, 2 * FF // 4, FF // 4)
N_CHIPS = 4
ADAM_LR, ADAM_B1, ADAM_B2, ADAM_EPS, ADAM_WD, ADAM_STEP = 0.001, 0.9, 0.999, 1e-08, 0.01, 10
MESH = pl.DeviceIdType.MESH
ANY = pl.BlockSpec(memory_space=pl.ANY)
SMEM = pl.BlockSpec(memory_space=pltpu.SMEM)
VMEM = pl.BlockSpec(memory_space=pltpu.VMEM)
HBM = pl.BlockSpec(memory_space=pltpu.HBM)
SEM = pl.BlockSpec(memory_space=pltpu.SEMAPHORE)
DATAFLOW = pltpu.SideEffectType.DATAFLOW_SIDE_EFFECTING


def _cp(sem, mb=48):
    return pltpu.CompilerParams(dimension_semantics=sem, vmem_limit_bytes=mb << 20)


def _nn(a, b):
    return lax.dot_general(a, b, (((1,), (0,)), ((), ())), preferred_element_type=F32)


def _nt(a, b):
    return lax.dot_general(a, b, (((1,), (1,)), ((), ())), preferred_element_type=F32)


def _tn(a, b):
    return lax.dot_general(a, b, (((0,), (0,)), ((), ())), preferred_element_type=F32)


def _resident(shape):
    n = len(shape)
    return pl.BlockSpec(shape, lambda *_: (0,) * n, pipeline_mode=pl.Buffered(1))


def _const(shape):
    n = len(shape)
    return pl.BlockSpec(shape, lambda *_: (0,) * n)


def _proj(x, w_t, name, tm=512):
    s = x.shape[0]
    n = w_t.shape[0]

    def body(x_ref, w_ref, o_ref, xb_ref):
        xb = x_ref[...].astype(BF16)
        xb_ref[...] = xb
        res = _nt(xb, w_ref[...])
        for g in range(n // 128):
            o_ref[g] = res[:, 128 * g:128 * (g + 1)]

    return pl.pallas_call(
        body, name=name, grid=(s // tm,),
        in_specs=[pl.BlockSpec((tm, D), lambda i: (i, 0)), _resident((n, D))],
        out_specs=[pl.BlockSpec((n // 128, tm, 128), lambda i: (0, i, 0)), pl.BlockSpec((tm, D), lambda i: (i, 0))],
        out_shape=[jax.ShapeDtypeStruct((n // 128, s, 128), F32), jax.ShapeDtypeStruct((s, D), BF16)],
        compiler_params=_cp(("parallel",)),
    )(x, w_t)


def _grad_w(lhs, rhs, name, tm, tk=2048, lhs_halves=False):
    s = rhs.shape[0]
    if lhs_halves:
        per_half = lhs.shape[2] // tm
        n = 2 * lhs.shape[2]
        lhs_spec = pl.BlockSpec((None, tk, tm), lambda i, k: (i // per_half, k, i % per_half))
    else:
        n = lhs.shape[1]
        lhs_spec = pl.BlockSpec((tk, tm), lambda i, k: (k, i))
    nk = s // tk

    def body(l_ref, r_ref, o_ref, ob_ref):
        k = pl.program_id(1)

        @pl.when(k == 0)
        def _():
            o_ref[...] = jnp.zeros_like(o_ref)

        o_ref[...] += _tn(l_ref[...], r_ref[...])

        @pl.when(k == nk - 1)
        def _():
            ob_ref[...] = o_ref[...].astype(BF16)

    return pl.pallas_call(
        body, name=name, grid=(n // tm, nk),
        in_specs=[lhs_spec, pl.BlockSpec((tk, D), lambda i, k: (k, 0))],
        out_specs=[pl.BlockSpec((tm, D), lambda i, k: (i, 0))] * 2,
        out_shape=[pltpu.HBM((n, D), F32), pltpu.HBM((n, D), BF16)],
        compiler_params=_cp(("parallel", "arbitrary")),
    )(lhs, rhs)


def _band_base(max_dist, dist_unit, first):
    row = lax.broadcasted_iota(I32, (BLK, 2 * BLK), 0)
    col = lax.broadcasted_iota(I32, (BLK, 2 * BLK), 1)
    dist = BLK + row - col
    ok = (dist >= 0) & (dist <= max_dist)
    if first:
        ok = ok & (col >= BLK)
    return jnp.where(ok, dist.astype(F32) * (-float(dist_unit)), -jnp.inf)


def _half_mask(shape, e):
    lane = lax.broadcasted_iota(I32, shape, 1)
    return (lane < HD) if e == 0 else (lane >= HD)


def _to_half(x, e, g):
    if g != e:
        x = pltpu.roll(x, HD, 1)
    return jnp.where(_half_mask(x.shape, g), x, 0.0)


def _stack_heads(scalars, tile):
    return jnp.concatenate([scalars[0] * tile, scalars[1] * tile], axis=0)


def _pair_fwd(q2, kb, vb, base, slopes, kv_heads, sinks):
    lo = _half_mask((BLK, 2 * HD), 0)
    if slopes is None:
        bias = base
    elif sinks is None:
        bias = _stack_heads(slopes, base)
    else:
        col0 = lax.broadcasted_iota(I32, base.shape, 1) == 0
        bias = jnp.concatenate([jnp.where(col0, sinks[e], slopes[e] * base) for e in (0, 1)], axis=0)
    qs = jnp.concatenate([_to_half(q2, e, kv_heads[e]) * SCALE for e in (0, 1)], axis=0).astype(BF16)
    s = _nt(qs, kb) + bias
    m = jnp.max(s, axis=1, keepdims=True)
    p = jnp.exp(s - m)
    l = jnp.sum(p, axis=1, keepdims=True)
    o = _nn(p.astype(BF16), vb) / l
    lse = m + jnp.log(l)
    halves = []
    for e in (0, 1):
        oh = o[e * BLK:(e + 1) * BLK]
        halves.append(pltpu.roll(oh, HD, 1) if kv_heads[e] != e else oh)
    o2 = jnp.where(lo, halves[0], halves[1])
    lse2 = jnp.where(lo, jnp.broadcast_to(lse[:BLK], (BLK, 2 * HD)), jnp.broadcast_to(lse[BLK:], (BLK, 2 * HD)))
    return o2, lse2


def _pair_bwd(q2, kb, vb, do2, o2, lse2, base, slopes, kv_heads, sinks):
    lo = _half_mask((BLK, 2 * HD), 0)
    prod = do2 * o2
    lses, deltas = [], []
    for e in (0, 1):
        hq = _half_mask((BLK, 2 * HD), e)
        lses.append(jnp.max(jnp.where(hq, lse2, -jnp.inf), axis=1, keepdims=True))
        deltas.append(jnp.sum(jnp.where(hq, prod, 0.0), axis=1, keepdims=True))
    lse = jnp.concatenate(lses, axis=0)
    delta = jnp.concatenate(deltas, axis=0)
    qs = jnp.concatenate([_to_half(q2, e, kv_heads[e]) * SCALE for e in (0, 1)], axis=0).astype(BF16)
    dos = jnp.concatenate([_to_half(do2, e, kv_heads[e]) for e in (0, 1)], axis=0).astype(BF16)
    p = jnp.exp(_nt(qs, kb) + (base if slopes is None else _stack_heads(slopes, base)) - lse)
    ds = (p * (_nt(dos, vb) - delta)).astype(BF16)
    dq = _nn(ds, kb) * SCALE
    halves = []
    for e in (0, 1):
        dqh = dq[e * BLK:(e + 1) * BLK]
        halves.append(pltpu.roll(dqh, HD, 1) if kv_heads[e] != e else dqh)
    dq2 = jnp.where(lo, halves[0], halves[1])
    dk2 = _tn(ds, qs)
    dv2 = _tn(p.astype(BF16), dos)
    dsinks = []
    if sinks is not None:
        for e in (0, 1):
            dsinks.append(jnp.sum(-jnp.exp(sinks[e] - lses[e]) * deltas[e], axis=0, keepdims=True))
    return dq2, dk2, dv2, dsinks


A_BLOCKS_PER_STEP = 2
A_BLOCKS_PER_STEP_BWD = 1


def _attn_a_fwd(proj, sinks):
    s = proj.shape[1]
    nq = A_BLOCKS_PER_STEP
    rows = BLK * nq
    steps = s // rows

    def body(sink_ref, q_ref, kp_ref, kc_ref, vp_ref, vc_ref, o_ref, lse_ref):
        n = pl.program_id(0)
        base_rest = _band_base(A_MAX_DIST, 1, False)
        base_0 = jnp.where(n > 0, base_rest, _band_base(A_MAX_DIST, 1, True))
        for i in range(nq):
            cur = pl.ds(i * BLK, BLK)
            k_prev = kc_ref[pl.ds((i - 1) * BLK, BLK), :] if i > 0 else kp_ref[...]
            v_prev = vc_ref[pl.ds((i - 1) * BLK, BLK), :] if i > 0 else vp_ref[...]
            first_key = lax.broadcasted_iota(I32, (2 * BLK, 128), 0) == 0
            kb = jnp.where(first_key, 0.0, jnp.concatenate([k_prev, kc_ref[cur, :]], axis=0)).astype(BF16)
            vb = jnp.where(first_key, 0.0, jnp.concatenate([v_prev, vc_ref[cur, :]], axis=0)).astype(BF16)
            for j in range(NH // 2):
                g = j // 2
                o2, lse2 = _pair_fwd(q_ref[j, cur, :], kb, vb, base_rest if i > 0 else base_0,
                                     (SLOPES[2 * j], SLOPES[2 * j + 1]), (g, g), (sink_ref[2 * j], sink_ref[2 * j + 1]))
                o_ref[j, cur, :] = o2
                lse_ref[j, cur, :] = lse2

    before = lambda n: jnp.maximum(n * nq - 1, 0)
    slab = lambda g: pl.BlockSpec((None, rows, 128), lambda n: (g, n, 0))
    edge = lambda g: pl.BlockSpec((None, BLK, 128), lambda n: (g, before(n), 0))
    quad = pl.BlockSpec((4, rows, 128), lambda n: (0, n, 0))
    return pl.pallas_call(
        body, name="attn_a_fwd", grid=(steps,),
        in_specs=[SMEM, quad, edge(4), slab(4), edge(5), slab(5)],
        out_specs=[quad, quad],
        out_shape=[jax.ShapeDtypeStruct((4, s, 128), F32)] * 2,
        compiler_params=_cp(("parallel",)),
    )(sinks, proj, proj, proj, proj, proj)


def _attn_a_bwd(proj, sinks, d_o, o, lse):
    s = proj.shape[1]
    nq = A_BLOCKS_PER_STEP_BWD
    rows = BLK * nq
    steps = s // rows

    def body(sink_ref, q_ref, kp_ref, kc_ref, vp_ref, vc_ref, do_ref, o_ref, lse_ref,
             dq_ref, dk_ref, dv_ref, dsink_ref, kcar, vcar):
        n = pl.program_id(0)

        @pl.when(n == 0)
        def _():
            kcar[...] = jnp.zeros_like(kcar)
            vcar[...] = jnp.zeros_like(vcar)
            dsink_ref[...] = jnp.zeros_like(dsink_ref)

        dk_ref[...] = kcar[...].astype(BF16)
        dv_ref[...] = vcar[...].astype(BF16)

        @pl.when(n < steps)
        def _():
            base_rest = _band_base(A_MAX_DIST, 1, False)
            base_0 = jnp.where(n > 0, base_rest, _band_base(A_MAX_DIST, 1, True))
            for i in range(nq):
                cur = pl.ds(i * BLK, BLK)
                k_prev = kc_ref[pl.ds((i - 1) * BLK, BLK), :] if i > 0 else kp_ref[...]
                v_prev = vc_ref[pl.ds((i - 1) * BLK, BLK), :] if i > 0 else vp_ref[...]
                kb = jnp.concatenate([k_prev, kc_ref[cur, :]], axis=0).astype(BF16)
                vb = jnp.concatenate([v_prev, vc_ref[cur, :]], axis=0).astype(BF16)
                dk_win = dv_win = None
                for j in range(NH // 2):
                    g = j // 2
                    dq2, dk2, dv2, dsk = _pair_bwd(q_ref[j, cur, :], kb, vb, do_ref[j, cur, :], o_ref[j, cur, :],
                                                   lse_ref[j, cur, :], base_rest if i > 0 else base_0,
                                                   (SLOPES[2 * j], SLOPES[2 * j + 1]), (g, g),
                                                   (sink_ref[2 * j], sink_ref[2 * j + 1]))
                    dq_ref[j, cur, :] = dq2.astype(BF16)
                    dk_win = dk2 if j == 0 else dk_win + dk2
                    dv_win = dv2 if j == 0 else dv_win + dv2
                    for e in (0, 1):
                        h = 2 * j + e
                        dsink_ref[h:h + 1, :] += jnp.broadcast_to(dsk[e], (1, 128))
                if i == 0:
                    last = pl.ds((nq - 1) * BLK, BLK)
                    dk_ref[last, :] = (kcar[last, :] + dk_win[:BLK]).astype(BF16)
                    dv_ref[last, :] = (vcar[last, :] + dv_win[:BLK]).astype(BF16)
                else:
                    kcar[pl.ds((i - 1) * BLK, BLK), :] += dk_win[:BLK]
                    vcar[pl.ds((i - 1) * BLK, BLK), :] += dv_win[:BLK]
                kcar[cur, :] = dk_win[BLK:]
                vcar[cur, :] = dv_win[BLK:]

    cur_step = lambda n: jnp.minimum(n, steps - 1)
    before = lambda n: jnp.maximum(cur_step(n) * nq - 1, 0)
    out_prev = lambda n: jnp.maximum(n - 1, 0)
    quad = pl.BlockSpec((4, rows, 128), lambda n: (0, cur_step(n), 0))
    slab = lambda g: pl.BlockSpec((None, rows, 128), lambda n: (g, cur_step(n), 0))
    edge = lambda g: pl.BlockSpec((None, BLK, 128), lambda n: (g, before(n), 0))
    return pl.pallas_call(
        body, name="attn_a_bwd", grid=(steps + 1,),
        in_specs=[SMEM, quad, edge(4), slab(4), edge(5), slab(5), quad, quad, quad],
        out_specs=[quad,
                   pl.BlockSpec((rows, 128), lambda n: (out_prev(n), 0)),
                   pl.BlockSpec((rows, 128), lambda n: (out_prev(n), 0)),
                   pl.BlockSpec((NH, 128), lambda n: (0, 0))],
        out_shape=[pltpu.HBM((4, s, 128), BF16), pltpu.HBM((s, 128), BF16), pltpu.HBM((s, 128), BF16),
                   jax.ShapeDtypeStruct((NH, 128), F32)],
        scratch_shapes=[pltpu.VMEM((rows, 128), F32), pltpu.VMEM((rows, 128), F32)],
        compiler_params=_cp(("arbitrary",)),
    )(sinks, proj, proj, proj, proj, proj, d_o, o, lse)


def _stream(rho, i, r):
    start = i * BLK * r + rho
    return pl.ds(start, BLK, stride=r) if r > 1 else pl.ds(start, BLK)


def _for_streams(r, fn, side_by_side=4):
    if r <= side_by_side:
        for rho in range(r):
            fn(rho)
    else:
        def group(it, carry):
            for u in range(side_by_side):
                fn(side_by_side * it + u)
            return carry

        lax.fori_loop(0, r // side_by_side, group, 0)


B_BLOCKS_PER_STEP = {1: 8, 4: 2, 16: 1}
B_BLOCKS_PER_STEP_FWD = {1: 16, 4: 4, 16: 1}


def _attn_b_fwd(proj, slopes, r, so_far=None):
    s = proj.shape[1]
    nq = B_BLOCKS_PER_STEP_FWD[r]
    rows = BLK * r * nq
    steps = s // rows
    qc, kc, vc = WA // 128, WA // 128 + 4, WA // 128 + 8
    chained = so_far is not None

    def body(slope_ref, q_ref, kp_ref, kc_ref, vp_ref, vc_ref, *rest):
        po_ref, pl_ref = rest[:2] if chained else (None, None)
        o_ref, lse_ref = rest[-2:]
        j = pl.program_id(0)
        sb = pl.program_id(1)
        sl2 = (slope_ref[2 * j], slope_ref[2 * j + 1])
        bias_rest = _stack_heads(sl2, _band_base(B_MAX_DIST, r, False))
        bias_0 = jnp.where(sb > 0, bias_rest, _stack_heads(sl2, _band_base(B_MAX_DIST, r, True)))

        def stream(rho):
            for i in range(nq):
                cur = _stream(rho, i, r)
                k_prev = kc_ref[_stream(rho, i - 1, r), :] if i > 0 else kp_ref[_stream(rho, 0, r), :]
                v_prev = vc_ref[_stream(rho, i - 1, r), :] if i > 0 else vp_ref[_stream(rho, 0, r), :]
                kb = jnp.concatenate([k_prev, kc_ref[cur, :]], axis=0).astype(BF16)
                vb = jnp.concatenate([v_prev, vc_ref[cur, :]], axis=0).astype(BF16)
                o2, lse2 = _pair_fwd(q_ref[cur, :], kb, vb, bias_rest if i > 0 else bias_0, None, (0, 1), None)
                if chained:
                    lse1 = pl_ref[cur, :]
                    m = jnp.maximum(lse1, lse2)
                    e1, e2 = jnp.exp(lse1 - m), jnp.exp(lse2 - m)
                    den = e1 + e2
                    o2 = (e1 * po_ref[cur, :] + e2 * o2) * (1.0 / den)
                    lse2 = m + jnp.log(den)
                o_ref[cur, :] = o2
                lse_ref[cur, :] = lse2

        _for_streams(r, stream, side_by_side=16)

    before = lambda sb: jnp.maximum(sb * nq - 1, 0)
    result = pl.BlockSpec((None, rows, 128), lambda j, sb: (j, sb, 0))
    return pl.pallas_call(
        body, name=f"attn_b_fwd_r{r}", grid=(NH // 2, steps),
        in_specs=[SMEM,
                  pl.BlockSpec((None, rows, 128), lambda j, sb: (qc + j, sb, 0)),
                  pl.BlockSpec((None, BLK * r, 128), lambda j, sb: (kc + j, before(sb), 0)),
                  pl.BlockSpec((None, rows, 128), lambda j, sb: (kc + j, sb, 0)),
                  pl.BlockSpec((None, BLK * r, 128), lambda j, sb: (vc + j, before(sb), 0)),
                  pl.BlockSpec((None, rows, 128), lambda j, sb: (vc + j, sb, 0))] + ([result] * 2 if chained else []),
        out_specs=[result] * 2,
        out_shape=[jax.ShapeDtypeStruct((4, s, 128), F32)] * 2,
        compiler_params=_cp(("parallel", "parallel")),
    )(slopes, proj, proj, proj, proj, proj, *(so_far if chained else ()))


def _attn_b_bwd(proj, slopes, d_o, o, lse, r, so_far=None, dtype=F32):
    s = proj.shape[1]
    nq = B_BLOCKS_PER_STEP[r]
    rows = BLK * r * nq
    steps = s // rows
    qc, kc, vc = WA // 128, WA // 128 + 4, WA // 128 + 8
    chained = so_far is not None

    def body(slope_ref, q_ref, kp_ref, kc_ref, vp_ref, vc_ref, do_ref, o_ref, lse_ref, *rest):
        pq_ref, pk_ref, pv_ref = rest[:3] if chained else (None, None, None)
        dq_ref, dk_ref, dv_ref, kcar, vcar = rest[-5:]
        j = pl.program_id(0)
        sb = pl.program_id(1)

        @pl.when(sb == 0)
        def _():
            kcar[...] = jnp.zeros_like(kcar)
            vcar[...] = jnp.zeros_like(vcar)

        def settled(car, p_ref, idx):
            return car[idx] + p_ref[idx] if chained else car[idx]

        dk_ref[...] = settled(kcar, pk_ref, ...).astype(dtype)
        dv_ref[...] = settled(vcar, pv_ref, ...).astype(dtype)

        @pl.when(sb < steps)
        def _():
            sl2 = (slope_ref[2 * j], slope_ref[2 * j + 1])
            bias_rest = _stack_heads(sl2, _band_base(B_MAX_DIST, r, False))
            bias_0 = jnp.where(sb > 0, bias_rest, _stack_heads(sl2, _band_base(B_MAX_DIST, r, True)))

            def stream(rho):
                for i in range(nq):
                    cur = _stream(rho, i, r)
                    k_prev = kc_ref[_stream(rho, i - 1, r), :] if i > 0 else kp_ref[_stream(rho, 0, r), :]
                    v_prev = vc_ref[_stream(rho, i - 1, r), :] if i > 0 else vp_ref[_stream(rho, 0, r), :]
                    kb = jnp.concatenate([k_prev, kc_ref[cur, :]], axis=0).astype(BF16)
                    vb = jnp.concatenate([v_prev, vc_ref[cur, :]], axis=0).astype(BF16)
                    dq2, dk2, dv2, _ = _pair_bwd(q_ref[cur, :], kb, vb, do_ref[cur, :], o_ref[cur, :], lse_ref[cur, :],
                                                 bias_rest if i > 0 else bias_0, None, (0, 1), None)
                    dq_ref[cur, :] = (dq2 + pq_ref[cur, :] if chained else dq2).astype(dtype)
                    if i == 0:
                        last = (_stream(rho, nq - 1, r), slice(None))
                        dk_ref[last] = (settled(kcar, pk_ref, last) + dk2[:BLK]).astype(dtype)
                        dv_ref[last] = (settled(vcar, pv_ref, last) + dv2[:BLK]).astype(dtype)
                    else:
                        kcar[_stream(rho, i - 1, r), :] += dk2[:BLK]
                        vcar[_stream(rho, i - 1, r), :] += dv2[:BLK]
                    kcar[cur, :] = dk2[BLK:]
                    vcar[cur, :] = dv2[BLK:]

            _for_streams(r, stream, side_by_side=8)

    cur_step = lambda sb: jnp.minimum(sb, steps - 1)
    before = lambda sb: jnp.maximum(cur_step(sb) * nq - 1, 0)
    out_prev = lambda sb: jnp.maximum(sb - 1, 0)
    tile = lambda slab: pl.BlockSpec((None, rows, 128), lambda j, sb: (slab + j, cur_step(sb), 0))
    edge = lambda slab: pl.BlockSpec((None, BLK * r, 128), lambda j, sb: (slab + j, before(sb), 0))
    late = pl.BlockSpec((None, rows, 128), lambda j, sb: (j, out_prev(sb), 0))
    grads = [tile(0), late, late]
    return pl.pallas_call(
        body, name=f"attn_b_bwd_r{r}", grid=(NH // 2, steps + 1),
        in_specs=[SMEM, tile(qc), edge(kc), tile(kc), edge(vc), tile(vc), tile(0), tile(0), tile(0)]
        + (grads if chained else []),
        out_specs=grads,
        out_shape=[pltpu.HBM((4, s, 128), dtype)] * 3,
        scratch_shapes=[pltpu.VMEM((rows, 128), F32), pltpu.VMEM((rows, 128), F32)],
        compiler_params=_cp(("parallel", "arbitrary")),
    )(slopes, proj, proj, proj, proj, proj, d_o, o, lse, *(so_far if chained else ()))


def _row(v):
    return v.reshape(1, -1)


def _layer_norm_stats(z):
    mu = jnp.mean(z, axis=-1, keepdims=True)
    zc = z - mu
    var = jnp.mean(zc * zc, axis=-1, keepdims=True)
    rstd = lax.rsqrt(var + LN_EPS)
    return zc * rstd, rstd


def _layer_norm_bwd(dh, zh, rstd, g):
    dzh = dh * g
    return rstd * (dzh - jnp.mean(dzh, axis=-1, keepdims=True) - zh * jnp.mean(dzh * zh, axis=-1, keepdims=True))


def _rms(o):
    return lax.rsqrt(jnp.mean(o * o, axis=-1, keepdims=True) + RMS_EPS)


def _mix_ln1(x, o_a, o_b, norm_a_g, norm_b_g, w_o, ln1_g, ln1_b, tm=256):
    s = x.shape[0]

    def wide(ref):
        return jnp.concatenate([ref[j] for j in range(4)], axis=1)

    def body(x_ref, oa_ref, ob_ref, ga_ref, gb_ref, wo_ref, g_ref, b_ref, cat_ref, z1_ref, h1_ref, h1b_ref):
        oa, ob = wide(oa_ref), wide(ob_ref)
        na = oa * _rms(oa) * ga_ref[...]
        nb_ = ob * _rms(ob) * gb_ref[...]
        cat = jnp.concatenate([na, nb_], axis=1).astype(BF16)
        cat_ref[...] = cat
        z1 = ALPHA * x_ref[...] + _nn(cat, wo_ref[...])
        z1_ref[...] = z1
        zh, _ = _layer_norm_stats(z1)
        h1 = zh * g_ref[...] + b_ref[...]
        h1_ref[...] = h1
        h1b_ref[...] = h1.astype(BF16)

    t512 = pl.BlockSpec((4, tm, 128), lambda i: (0, i, 0))
    td = pl.BlockSpec((tm, D), lambda i: (i, 0))
    return pl.pallas_call(
        body, name="mix_ln1", grid=(s // tm,),
        in_specs=[td] + [t512] * 2 + [_const((1, 512))] * 2 + [_resident((D, D))] + [_const((1, D))] * 2,
        out_specs=[td, td, td, td],
        out_shape=[jax.ShapeDtypeStruct((s, D), BF16), jax.ShapeDtypeStruct((s, D), F32),
                   jax.ShapeDtypeStruct((s, D), F32), jax.ShapeDtypeStruct((s, D), BF16)],
        compiler_params=_cp(("parallel",)),
    )(x, o_a, o_b, _row(norm_a_g), _row(norm_b_g), w_o, _row(ln1_g), _row(ln1_b))


def _gelu_and_grad(x):
    c = math.sqrt(2.0 / math.pi)
    x2 = x * x
    s = 0.5 * jnp.tanh(x * ((c * 0.044715) * x2 + c)) + 0.5
    dg = s + (x * ((6.0 * c * 0.044715) * x2 + 2.0 * c)) * (s - s * s)
    return x * s, dg


def _shifted(u, edge, row, down):
    groups = [u[8 * i:8 * i + 8] for i in range(u.shape[0] // 8)]
    others = [edge] + groups[:-1] if down else groups[1:] + [edge]
    moved = []
    for k in (1, 2):
        crossing = row >= 8 - k if down else row < k
        moved.append(jnp.concatenate([pltpu.roll(jnp.where(crossing, o, g), k if down else 8 - k, 0)
                                      for o, g in zip(others, groups)], axis=0))
    return moved


def _up_proj(h1b, w_up, tm=512):
    s = h1b.shape[0]

    def body(h_ref, w_ref, o_ref):
        h = h_ref[...]
        for half in (0, 1):
            o_ref[half] = _nn(h, w_ref[:, half * FF:(half + 1) * FF]).astype(BF16)

    return pl.pallas_call(
        body, name="up_proj", grid=(s // tm,),
        in_specs=[pl.BlockSpec((tm, D), lambda i: (i, 0)), _resident((D, 2 * FF))],
        out_specs=pl.BlockSpec((2, tm, FF), lambda i: (0, i, 0)),
        out_shape=jax.ShapeDtypeStruct((2, s, FF), BF16),
        compiler_params=_cp(("parallel",)),
    )(h1b, w_up)


def _conv_gelu(up, cwb, tm=256, tn=FF // 2, chunk_rows=16):
    s = up.shape[1]
    n_c = tm // chunk_rows

    def body(up_ref, c_ref, a_ref, g_ref, a1_ref, carry):
        @pl.when(pl.program_id(1) == 0)
        def _():
            carry[...] = jnp.zeros_like(carry)

        row = lax.broadcasted_iota(jnp.int32, (8, tn), 0)
        edge = [carry[0], carry[1]]
        for c in range(n_c):
            rows = pl.ds(c * chunk_rows, chunk_rows)
            u = []
            for half in (0, 1):
                x = up_ref[half, rows, :].astype(F32)
                r1, r2 = _shifted(x, edge[half], row, True)
                u.append(r2 * c_ref[0, half:half + 1, :] + r1 * c_ref[1, half:half + 1, :]
                         + x * c_ref[2, half:half + 1, :] + c_ref[3, half:half + 1, :])
                edge[half] = x[chunk_rows - 8:]
            g, dg = _gelu_and_grad(u[0])
            a_ref[rows, :] = (g * u[1]).astype(BF16)
            g_ref[rows, :] = g.astype(BF16)
            a1_ref[rows, :] = (u[1] * dg).astype(BF16)
        for half in (0, 1):
            carry[half] = edge[half]

    pair = pl.BlockSpec((2, tm, tn), lambda j, i: (0, i, j))
    tile = pl.BlockSpec((tm, tn), lambda j, i: (i, j))
    return pl.pallas_call(
        body, name="conv_gelu", grid=(FF // tn, s // tm),
        in_specs=[pair, pl.BlockSpec((4, 2, tn), lambda j, i: (0, 0, j))],
        out_specs=[tile, tile, tile],
        out_shape=[jax.ShapeDtypeStruct((s, FF), BF16)] * 3,
        scratch_shapes=[pltpu.VMEM((2, 8, tn), F32)],
        compiler_params=_cp(("parallel", "arbitrary")),
    )(up, cwb)


def _down_ln2_loss(a, w_down, h1, target, ln2_g, ln2_b, tm=512):
    s = a.shape[0]

    def body(a_ref, w_ref, h_ref, t_ref, g_ref, b_ref, dz_ref, dzb_ref, st_ref):
        @pl.when(pl.program_id(0) == 0)
        def _():
            st_ref[...] = jnp.zeros_like(st_ref)

        z2 = ALPHA * h_ref[...] + _nn(a_ref[...], w_ref[...])
        zh, rstd = _layer_norm_stats(z2)
        diff = zh * g_ref[...] + b_ref[...] - t_ref[...]
        part = 0.5 * jnp.sum(jnp.mean(diff * diff, axis=-1, keepdims=True), axis=0, keepdims=True)
        dy = diff * (1.0 / D)
        st_ref[0:1, :] += jnp.sum(dy * zh, axis=0, keepdims=True)
        st_ref[1:2, :] += jnp.sum(dy, axis=0, keepdims=True)
        st_ref[2:3, :] += jnp.broadcast_to(part, (1, D))
        dz = _layer_norm_bwd(dy, zh, rstd, g_ref[...])
        dz_ref[...] = dz
        dzb_ref[...] = dz.astype(BF16)

    td = pl.BlockSpec((tm, D), lambda i: (i, 0))
    return pl.pallas_call(
        body, name="down_ln2_loss", grid=(s // tm,),
        in_specs=[pl.BlockSpec((tm, FF), lambda i: (i, 0)), _resident((FF, D)), td, td, _const((1, D)), _const((1, D))],
        out_specs=[td, td, _const((8, D))],
        out_shape=[jax.ShapeDtypeStruct((s, D), F32), jax.ShapeDtypeStruct((s, D), BF16),
                   jax.ShapeDtypeStruct((8, D), F32)],
        compiler_params=_cp(("arbitrary",)),
    )(a, w_down, h1, target, _row(ln2_g), _row(ln2_b))


def _d_act(dz2b, w_down, tm=512):
    s = dz2b.shape[0]

    def body(dz_ref, w_ref, o_ref):
        o_ref[...] = _nt(dz_ref[...], w_ref[...]).astype(BF16)

    return pl.pallas_call(
        body, name="d_act", grid=(s // tm,),
        in_specs=[pl.BlockSpec((tm, D), lambda i: (i, 0)), _resident((FF, D))],
        out_specs=pl.BlockSpec((tm, FF), lambda i: (i, 0)),
        out_shape=jax.ShapeDtypeStruct((s, FF), BF16),
        compiler_params=_cp(("parallel",)),
    )(dz2b, w_down)


def _conv_gelu_bwd(da, up, g, a1, cwb, tm=256, tn=FF // 2, chunk_rows=16):
    s = da.shape[0]
    n_i = s // tm
    n_c = tm // chunk_rows

    def body(da_ref, up_ref, g_ref, a1_ref, c_ref, dup_ref, dc_ref, carry):
        @pl.when(pl.program_id(1) == 0)
        def _():
            carry[...] = jnp.zeros_like(carry)
            dc_ref[...] = jnp.zeros_like(dc_ref)

        def fold(v):
            return jnp.sum(v.reshape(chunk_rows // 8, 8, v.shape[1]), axis=0)

        def chunk(cc, state):
            after, sums = state
            rows = pl.ds((n_c - 1 - cc) * chunk_rows, chunk_rows)
            da_c = da_ref[rows, :].astype(F32)
            dus = (da_c * a1_ref[rows, :].astype(F32), da_c * g_ref[rows, :].astype(F32))
            head, new_sums = [], []
            for half in (0, 1):
                du = dus[half]
                up = up_ref[half, rows, :].astype(F32)
                l1, l2 = _shifted(du, after[half], row, False)
                dup = (du * c_ref[2, half:half + 1, :] + l1 * c_ref[1, half:half + 1, :]
                       + l2 * c_ref[0, half:half + 1, :])
                dup_ref[half, rows, :] = dup.astype(BF16)
                parts = (fold(l2 * up), fold(l1 * up), fold(du * up), fold(du))
                new_sums.append(parts if sums is None else tuple(a + b for a, b in zip(sums[half], parts)))
                head.append(du[:8])
            return tuple(head), new_sums

        row = lax.broadcasted_iota(jnp.int32, (8, tn), 0)
        state = ((carry[0], carry[1]), None)
        for cc in range(n_c):
            state = chunk(cc, state)
        head, sums = state
        for half in (0, 1):
            carry[half] = head[half]
            for k in range(4):
                dc_ref[k, half:half + 1, :] += jnp.sum(sums[half][k], axis=0, keepdims=True)

    rev = lambda ii: n_i - 1 - ii
    tile = pl.BlockSpec((tm, tn), lambda j, ii: (rev(ii), j))
    pair = pl.BlockSpec((2, tm, tn), lambda j, ii: (0, rev(ii), j))
    per_col = pl.BlockSpec((4, 2, tn), lambda j, ii: (0, 0, j))
    return pl.pallas_call(
        body, name="conv_gelu_bwd", grid=(FF // tn, n_i),
        in_specs=[tile, pair, tile, tile, per_col],
        out_specs=[pair, per_col],
        out_shape=[jax.ShapeDtypeStruct((2, s, FF), BF16), jax.ShapeDtypeStruct((4, 2, FF), F32)],
        scratch_shapes=[pltpu.VMEM((2, 8, tn), F32)],
        compiler_params=_cp(("parallel", "arbitrary")),
    )(da, up, g, a1, cwb)


def _dh1_ln1_bwd(dz2, dup, w_up, z1, ln1_g, tm=512):
    s = dz2.shape[0]

    def body(dz2_ref, dup_ref, w_ref, z1_ref, g_ref, dz1_ref, dz1b_ref, st_ref):
        @pl.when(pl.program_id(0) == 0)
        def _():
            st_ref[...] = jnp.zeros_like(st_ref)

        dh = ALPHA * dz2_ref[...] + _nt(dup_ref[0], w_ref[:, :FF]) + _nt(dup_ref[1], w_ref[:, FF:])
        zh, rstd = _layer_norm_stats(z1_ref[...])
        st_ref[0:1, :] += jnp.sum(dh * zh, axis=0, keepdims=True)
        st_ref[1:2, :] += jnp.sum(dh, axis=0, keepdims=True)
        dz = _layer_norm_bwd(dh, zh, rstd, g_ref[...])
        dz1_ref[...] = dz
        dz1b_ref[...] = dz.astype(BF16)

    td = pl.BlockSpec((tm, D), lambda i: (i, 0))
    return pl.pallas_call(
        body, name="dh1_ln1_bwd", grid=(s // tm,),
        in_specs=[td, pl.BlockSpec((2, tm, FF), lambda i: (0, i, 0)), _resident((D, 2 * FF)), td, _const((1, D))],
        out_specs=[td, td, _const((8, D))],
        out_shape=[jax.ShapeDtypeStruct((s, D), F32), jax.ShapeDtypeStruct((s, D), BF16),
                   jax.ShapeDtypeStruct((8, D), F32)],
        compiler_params=_cp(("arbitrary",), 58),
    )(dz2, dup, w_up, z1, _row(ln1_g))


def _dcat_rms_bwd(dz1b, w_o, o_a, o_b, norm_a_g, norm_b_g, tm=512):
    s = dz1b.shape[0]

    def body(dz_ref, w_ref, oa_ref, ob_ref, ga_ref, gb_ref, da_ref, db_ref, st_ref):
        @pl.when(pl.program_id(0) == 0)
        def _():
            st_ref[...] = jnp.zeros_like(st_ref)

        dcat = _nt(dz_ref[...], w_ref[...])
        for k, (o_ref, g_ref, d_ref) in enumerate(((oa_ref, ga_ref, da_ref), (ob_ref, gb_ref, db_ref))):
            o = jnp.concatenate([o_ref[j] for j in range(4)], axis=1)
            dn = dcat[:, 512 * k:512 * (k + 1)]
            rr = _rms(o)
            oh = o * rr
            st_ref[k:k + 1, :] += jnp.sum(dn * oh, axis=0, keepdims=True)
            doh = dn * g_ref[...]
            d_o = rr * (doh - oh * jnp.mean(doh * oh, axis=-1, keepdims=True))
            for j in range(4):
                d_ref[j] = d_o[:, 128 * j:128 * (j + 1)]

    t512 = pl.BlockSpec((4, tm, 128), lambda i: (0, i, 0))
    return pl.pallas_call(
        body, name="dcat_rms_bwd", grid=(s // tm,),
        in_specs=[pl.BlockSpec((tm, D), lambda i: (i, 0)), _resident((D, D)), t512, t512,
                  _const((1, 512)), _const((1, 512))],
        out_specs=[t512, t512, _const((8, 512))],
        out_shape=[jax.ShapeDtypeStruct((4, s, 128), F32), jax.ShapeDtypeStruct((4, s, 128), F32),
                   jax.ShapeDtypeStruct((8, 512), F32)],
        compiler_params=_cp(("arbitrary",)),
    )(dz1b, w_o, o_a, o_b, _row(norm_a_g), _row(norm_b_g))


def _grad_w_in(dparts, xb, tk=2048):
    s = xb.shape[0]
    nk = s // tk

    def body(qa, ka, va, qb, kb, vb, x_ref, o_ref, ob_ref):
        i = pl.program_id(0)
        k = pl.program_id(1)

        @pl.when(k == 0)
        def _():
            o_ref[...] = jnp.zeros_like(o_ref)

        def add(blocks):
            o_ref[...] += _tn(jnp.concatenate(blocks, axis=1), x_ref[...])

        pl.when(i == 0)(lambda: add([qa[j] for j in range(4)] + [ka[...], va[...]]))
        pl.when(i == 1)(lambda: add([qb[j] for j in range(4)] + [kb[0], kb[1]]))
        pl.when(i == 2)(lambda: add([kb[0], kb[1]] + [vb[j] for j in range(4)]))

        @pl.when(k == nk - 1)
        def _():
            ob_ref[...] = o_ref[...].astype(BF16)

    def during(tile):
        return lambda i, k: jnp.where(i == tile, k, jnp.where(i < tile, 0, nk - 1))

    quad = lambda tile: pl.BlockSpec((4, tk, 128), lambda i, k: (0, during(tile)(i, k), 0))
    one = pl.BlockSpec((tk, 128), lambda i, k: (during(0)(i, k), 0))
    kb_spec = pl.BlockSpec((2, tk, 128), lambda i, k: (jnp.where(i == 2, 1, 0), jnp.where(i == 0, 0, k), 0))
    return pl.pallas_call(
        body, name="grad_w_in", grid=(3, nk),
        in_specs=[quad(0), one, one, quad(1), kb_spec, quad(2), pl.BlockSpec((tk, D), lambda i, k: (k, 0))],
        out_specs=[pl.BlockSpec((WA, D), lambda i, k: (i, 0))] * 2,
        out_shape=[pltpu.HBM((WIN, D), F32), pltpu.HBM((WIN, D), BF16)],
        compiler_params=_cp(("parallel", "arbitrary"), mb=56),
    )(*dparts, xb)


def _grad_x(dz1, dparts, w_in_t, zero, tm=512):
    s = dz1.shape[0]

    def body(dz_ref, qa, ka, va, qb, kb, vb, w_ref, z_ref, o_ref):
        dp = jnp.concatenate([qa[j] for j in range(4)] + [ka[...], va[...]]
                             + [ref[j] for ref in (qb, kb, vb) for j in range(4)], axis=1)
        o_ref[...] = ALPHA * dz_ref[...] + _nn(dp, w_ref[...]) + z_ref[0:1, 0:1]

    td = pl.BlockSpec((tm, D), lambda i: (i, 0))
    quad = pl.BlockSpec((4, tm, 128), lambda i: (0, i, 0))
    one = pl.BlockSpec((tm, 128), lambda i: (i, 0))
    return pl.pallas_call(
        body, name="grad_x", grid=(s // tm,),
        in_specs=[td, quad, one, one, quad, quad, quad, _resident((WIN, D)), _const((8, 128))],
        out_specs=td, out_shape=jax.ShapeDtypeStruct((s, D), F32),
        compiler_params=_cp(("parallel",)),
    )(dz1, *dparts, w_in_t, zero)


def _place():
    return lax.axis_index("x"), lax.axis_index("y"), lax.axis_index("c")


def _other_chips(x, y):
    return [(1 - x, y), (x, 1 - y), (1 - x, 1 - y)]


def _hbm(a):
    return pltpu.with_memory_space_constraint(a, pltpu.HBM)


def _gather_w_in(shard, conv_w):
    rows_k = shard.shape[0]
    half = rows_k // 2

    def body(src, conv_src, out, conv_out, send_sems, recv_sems):
        x, y, c = _place()
        b = 2 * x + y
        sibling = (x, y, 1 - c)
        chips = _other_chips(x, y)

        def copy(idx, chip_b, core, to, first_hop=False):
            rows = out.at[pl.ds(pl.multiple_of(chip_b * rows_k + core * half, 16), half)]
            s_ref = src.at[pl.ds(pl.multiple_of(core * half, 16), half)] if first_hop else rows
            return pltpu.make_async_remote_copy(src_ref=s_ref, dst_ref=rows, send_sem=send_sems.at[idx],
                                                recv_sem=recv_sems.at[idx], device_id=to, device_id_type=MESH)

        def own_copy():
            return pltpu.make_async_remote_copy(
                src_ref=src, dst_ref=out.at[pl.ds(pl.multiple_of(b * rows_k, 16), rows_k)], send_sem=send_sems.at[6],
                recv_sem=recv_sems.at[6], device_id=sibling, device_id_type=MESH)

        def conv_copy(idx, chip_b, to):
            return pltpu.make_async_remote_copy(src_ref=conv_src, dst_ref=conv_out.at[chip_b],
                                                send_sem=send_sems.at[7 + idx], recv_sem=recv_sems.at[7 + idx],
                                                device_id=to, device_id_type=MESH)

        started = [own_copy(), conv_copy(3, b, sibling)]
        for jn, chip in enumerate(chips):
            started += [copy(jn, b, c, (chip[0], chip[1], c), first_hop=True), conv_copy(jn, b, (chip[0], chip[1], c))]
        for cp in started:
            cp.start()
        for jn, chip in enumerate(chips):
            cb = 2 * chip[0] + chip[1]
            copy(jn, cb, c, (chip[0], chip[1], c)).wait_recv()
            cp = copy(3 + jn, cb, c, sibling)
            cp.start()
            started.append(cp)
        for jn, chip in enumerate(chips):
            cb = 2 * chip[0] + chip[1]
            copy(3 + jn, cb, 1 - c, sibling).wait_recv()
            conv_copy(jn, cb, (chip[0], chip[1], c)).wait_recv()
        own_copy().wait_recv()
        conv_copy(3, b, sibling).wait_recv()
        for cp in started:
            cp.wait_send()

    return pl.pallas_call(
        body, name="gather_w_in",
        in_specs=[ANY, ANY], out_specs=[ANY, ANY],
        out_shape=[jax.ShapeDtypeStruct((N_CHIPS * rows_k, D), BF16), jax.ShapeDtypeStruct((N_CHIPS,) + conv_w.shape, F32)],
        scratch_shapes=[pltpu.SemaphoreType.DMA((11,)), pltpu.SemaphoreType.DMA((11,))],
        compiler_params=pltpu.CompilerParams(has_side_effects=True),
    )(shard, conv_w)


def _weight_copies(shard, land, send_sems, recv_sems, arrivals):
    x, y, c = _place()
    n_rows, n_cols = shard.shape
    peers = [(px, py, c) for px, py in _other_chips(x, y)] + [(x, y, 1 - c)]
    cps = []
    for jn, peer in enumerate(peers):
        at = 2 * peer[0] + peer[1] if arrivals else 2 * x + y
        if land.shape[1] == n_cols:
            dst = land.at[pl.ds(pl.multiple_of(at * n_rows, 16), n_rows)]
        else:
            dst = land.at[:, pl.ds(pl.multiple_of(at * n_cols, 128), n_cols)]
        cps.append(pltpu.make_async_remote_copy(src_ref=shard, dst_ref=dst, send_sem=send_sems.at[jn],
                                                recv_sem=recv_sems.at[jn], device_id=peer, device_id_type=MESH))
    return cps


def _weights_start(shards, after):
    n = len(shards)
    lands = [lax.empty((N_CHIPS * sh.shape[0], D) if sh.shape[1] == D else (D, N_CHIPS * sh.shape[1]), BF16)
             for sh in shards]

    def body(*refs):
        src, land = refs[:n], refs[n:2 * n]
        send_sems, recv_sems = refs[2 * n + 1:3 * n + 1], refs[3 * n + 1:4 * n + 1]
        for k in range(n):
            for send in _weight_copies(src[k], land[k], send_sems[k], recv_sems[k], False):
                send.start()
        refs[-1][...] = jnp.zeros_like(refs[-1])

    res = pl.pallas_call(
        body, name="weights_start",
        in_specs=[HBM] * (2 * n) + [ANY], out_specs=[SEM] * (2 * n) + [HBM] * (2 * n) + [VMEM],
        out_shape=[pltpu.SemaphoreType.DMA((4,))] * (2 * n)
        + [pltpu.HBM(a.shape, a.dtype) for a in (*shards, *lands)] + [jax.ShapeDtypeStruct((8, 128), F32)],
        input_output_aliases={i: i + 2 * n for i in range(2 * n)},
        compiler_params=pltpu.CompilerParams(has_side_effects=DATAFLOW),
    )(*[_hbm(a) for a in (*shards, *lands)], after)
    return [(res[k], res[n + k], res[2 * n + k], res[3 * n + k]) for k in range(n)], res[-1]


def _weights_wait(started, after, name):
    send_sems, recv_sems, shard, land = started

    def body(s_ref, l_ref, send_ref, recv_ref, after_ref, s_out, l_out):
        for cp in _weight_copies(s_ref, l_ref, send_ref, recv_ref, True):
            cp.wait_send()
            cp.wait_recv()

    return _hbm(pl.pallas_call(
        body, name=name,
        in_specs=[HBM, HBM, SEM, SEM, ANY], out_specs=[HBM, HBM],
        out_shape=[pltpu.HBM(shard.shape, shard.dtype), pltpu.HBM(land.shape, land.dtype)],
        input_output_aliases={0: 0, 1: 1},
        compiler_params=pltpu.CompilerParams(has_side_effects=DATAFLOW),
    )(shard, land, send_sems, recv_sems, after)[1])


def _grad_copies(g_ref, land_ref, send_sems, recv_sems):
    x, y, c = _place()
    cps = []
    for d in range(1, 8):
        px, py, pc = x ^ (d >> 2), y ^ ((d >> 1) & 1), c ^ (d & 1)
        cps.append(pltpu.make_async_remote_copy(
            src_ref=g_ref.at[2 * px + py, pc], dst_ref=land_ref.at[d - 1], send_sem=send_sems.at[d - 1],
            recv_sem=recv_sems.at[d - 1], device_id=(px, py, pc), device_id_type=MESH))
    return cps


def _grads_start(grads_b, name):
    n = len(grads_b)
    lands = [lax.empty((7, g.shape[2], D), BF16) for g in grads_b]

    def body(*refs):
        g, land = refs[:n], refs[n:2 * n]
        send_sems, recv_sems = refs[2 * n:3 * n], refs[3 * n:4 * n]
        for k in range(n):
            for cp in _grad_copies(g[k], land[k], send_sems[k], recv_sems[k]):
                cp.start()
        refs[-1][...] = jnp.zeros_like(refs[-1])

    res = pl.pallas_call(
        body, name=name,
        in_specs=[HBM] * (2 * n), out_specs=[SEM] * (2 * n) + [HBM] * (2 * n) + [VMEM],
        out_shape=[pltpu.SemaphoreType.DMA((7,))] * (2 * n)
        + [pltpu.HBM(a.shape, a.dtype) for a in (*grads_b, *lands)] + [jax.ShapeDtypeStruct((8, 128), F32)],
        input_output_aliases={i: i + 2 * n for i in range(2 * n)},
        compiler_params=pltpu.CompilerParams(has_side_effects=DATAFLOW),
    )(*[_hbm(a) for a in (*grads_b, *lands)])
    return [(res[k], res[n + k], res[2 * n + k], res[3 * n + k]) for k in range(n)], res[-1]


def _grads_wait(started, after, name):
    n = len(started)

    def body(*refs):
        g, land = refs[:n], refs[n:2 * n]
        send_sems, recv_sems = refs[2 * n:3 * n], refs[3 * n:4 * n]
        for k in range(n):
            for cp in _grad_copies(g[k], land[k], send_sems[k], recv_sems[k]):
                cp.wait_send()
                cp.wait_recv()

    gs = [st[2] for st in started]
    lands = [st[3] for st in started]
    res = pl.pallas_call(
        body, name=name,
        in_specs=[HBM] * (2 * n) + [SEM] * (2 * n) + [ANY], out_specs=[HBM] * (2 * n),
        out_shape=[pltpu.HBM(a.shape, a.dtype) for a in (*gs, *lands)],
        input_output_aliases={i: i for i in range(2 * n)},
        compiler_params=pltpu.CompilerParams(has_side_effects=DATAFLOW),
    )(*gs, *lands, *[st[0] for st in started], *[st[1] for st in started], after)
    return res[n:]


def _sum_partials(grad4, got, cb, name, tr):
    h = grad4.shape[2]
    per_half = h // tr

    def body(cb_ref, g_ref, o_ref, out_ref):
        acc = g_ref[...]
        for j in range(7):
            acc = acc + o_ref[j].astype(F32)
        out_ref[...] = acc

    return pl.pallas_call(
        body, name=name,
        grid_spec=pltpu.PrefetchScalarGridSpec(
            num_scalar_prefetch=1, grid=(per_half,),
            in_specs=[pl.BlockSpec((None, None, tr, D), lambda i, cb_ref: (cb_ref[1], cb_ref[0], i, 0)),
                      pl.BlockSpec((7, tr, D), lambda i, cb_ref: (0, i, 0))],
            out_specs=pl.BlockSpec((tr, D), lambda i, cb_ref: (cb_ref[0] * per_half + i, 0))),
        out_shape=pltpu.HBM((2 * h, D), F32),
        compiler_params=_cp(("arbitrary",)),
    )(cb, grad4, got)


def _swap_halves(shards, name):
    n = len(shards)

    def body(*refs):
        out, send_sems, recv_sems = refs[n:2 * n], refs[2 * n], refs[2 * n + 1]
        x, y, c = _place()
        cps = []
        for k in range(n):
            h = shards[k].shape[0] // 2
            mine = out[k].at[pl.ds(pl.multiple_of(c * h, 8), h)]
            cp = pltpu.make_async_remote_copy(src_ref=mine, dst_ref=mine, send_sem=send_sems.at[k],
                                              recv_sem=recv_sems.at[k], device_id=(x, y, 1 - c), device_id_type=MESH)
            cp.start()
            cps.append(cp)
        for cp in cps:
            cp.wait()

    return pl.pallas_call(
        body, name=name,
        in_specs=[ANY] * n, out_specs=[ANY] * n,
        out_shape=[jax.ShapeDtypeStruct(sh.shape, F32) for sh in shards],
        input_output_aliases={k: k for k in range(n)},
        scratch_shapes=[pltpu.SemaphoreType.DMA((n,)), pltpu.SemaphoreType.DMA((n,))],
        compiler_params=pltpu.CompilerParams(has_side_effects=True),
    )(*shards)


def _small_copies(small_ref, land_ref, send_sems, recv_sems):
    x, y, c = _place()
    me = 4 * x + 2 * y + c
    cps = []
    for d in range(1, 8):
        px, py, pc = x ^ (d >> 2), y ^ ((d >> 1) & 1), c ^ (d & 1)
        cps.append(pltpu.make_async_remote_copy(
            src_ref=small_ref, dst_ref=land_ref.at[me], send_sem=send_sems.at[d - 1], recv_sem=recv_sems.at[d - 1],
            device_id=(px, py, pc), device_id_type=MESH))
    return cps


def _small_start(small):
    land = lax.empty((8,) + small.shape, F32)

    def body(s_ref, l_ref, send_sems, recv_sems, s_thru, l_thru, token):
        for cp in _small_copies(s_ref, l_ref, send_sems, recv_sems):
            cp.start()
        token[...] = jnp.zeros_like(token)

    res = pl.pallas_call(
        body, name="small_start",
        in_specs=[HBM, HBM], out_specs=[SEM, SEM, HBM, HBM, VMEM],
        out_shape=[pltpu.SemaphoreType.DMA((7,)), pltpu.SemaphoreType.DMA((7,)), pltpu.HBM(small.shape, F32),
                   pltpu.HBM(land.shape, F32), jax.ShapeDtypeStruct((8, 128), F32)],
        input_output_aliases={0: 2, 1: 3},
        compiler_params=pltpu.CompilerParams(has_side_effects=DATAFLOW),
    )(_hbm(small), _hbm(land))
    return res[:4], res[4]


def _small_wait(started, after):
    send_sems, recv_sems, small, land = started

    def body(s_ref, l_ref, send_ref, recv_ref, after_ref, s_out, l_out):
        for cp in _small_copies(s_ref, l_ref, send_ref, recv_ref):
            cp.wait_send()
            cp.wait_recv()

    return pl.pallas_call(
        body, name="small_wait",
        in_specs=[HBM, HBM, SEM, SEM, ANY], out_specs=[HBM, HBM],
        out_shape=[pltpu.HBM(small.shape, F32), pltpu.HBM(land.shape, F32)],
        input_output_aliases={0: 0, 1: 1},
        compiler_params=pltpu.CompilerParams(has_side_effects=DATAFLOW),
    )(small, land, send_sems, recv_sems, after)


def _small_sum(small, land, me):
    rows = small.shape[0]

    def body(me_ref, s_ref, l_ref, o_ref):
        acc = None
        for k in range(8):
            term = jnp.where(me_ref[0] == k, s_ref[...], l_ref[k])
            acc = term if k == 0 else acc + term
        o_ref[...] = acc

    return pl.pallas_call(
        body, name="small_sum",
        in_specs=[SMEM, VMEM, VMEM], out_specs=VMEM,
        out_shape=jax.ShapeDtypeStruct((rows, D), F32),
    )(me, small, land)


def _adamw(w, g, m, v, name, tr):
    rows, cols = w.shape

    def body(w_ref, g_ref, m_ref, v_ref, d_ref, nm_ref, nv_ref):
        g_ = g_ref[...]
        nm = ADAM_B1 * m_ref[...] + (1.0 - ADAM_B1) * g_
        nv = ADAM_B2 * v_ref[...] + (1.0 - ADAM_B2) * (g_ * g_)
        m_hat = nm / (1.0 - ADAM_B1 ** ADAM_STEP)
        v_hat = nv / (1.0 - ADAM_B2 ** ADAM_STEP)
        d_ref[...] = -ADAM_LR * (m_hat / (jnp.sqrt(v_hat) + ADAM_EPS) + ADAM_WD * w_ref[...])
        nm_ref[...] = nm
        nv_ref[...] = nv

    spec = pl.BlockSpec((tr, cols), lambda i: (i, 0))
    return pl.pallas_call(
        body, name=name, grid=(rows // tr,),
        in_specs=[spec] * 4, out_specs=[spec] * 3,
        out_shape=[jax.ShapeDtypeStruct((rows, cols), F32)] * 3,
        compiler_params=_cp(("parallel",)),
    )(*[_hbm(a) for a in (w, g, m, v)])


def _local_step(x, target, w_in_t, late_weights, norm_a_g, norm_b_g, sinks_a, ln1_g, ln1_b,
                conv_w, conv_b, ln2_g, ln2_b, slopes, on_grad, on_small):
    cwb = jnp.concatenate([conv_w, conv_b[None]], axis=0).reshape(4, 2, FF)

    proj, xb = _proj(x, w_in_t, "proj")
    o_a, lse_a = _attn_a_fwd(proj, sinks_a)
    fwd_b = None
    for r in reversed(B_DILATIONS):
        fwd_b = _attn_b_fwd(proj, slopes, r, fwd_b)
    o_b, lse_b = fwd_b
    w_o = late_weights(1, lse_b)
    cat, z1, h1, h1b = _mix_ln1(x, o_a, o_b, norm_a_g, norm_b_g, w_o, ln1_g, ln1_b)
    w_up = late_weights(2, h1b)
    up = _up_proj(h1b, w_up)
    a, gate, a1 = _conv_gelu(up, cwb)
    w_down = late_weights(3, a)
    dz2, dz2b, st2 = _down_ln2_loss(a, w_down, h1, target, ln2_g, ln2_b)

    on_grad(3, *_grad_w(a, dz2b, "grad_w_down", tm=FF // 2))
    dup, dconv = _conv_gelu_bwd(_d_act(dz2b, w_down), up, gate, a1, cwb)
    on_grad(2, *_grad_w(dup, h1b, "grad_w_up", tm=FF // 2, lhs_halves=True))
    dz1, dz1b, st1 = _dh1_ln1_bwd(dz2, dup, w_up, z1, ln1_g)
    tok = on_grad(1, *_grad_w(cat, dz1b, "grad_w_o", tm=512))
    d_oa, d_ob, st_n = _dcat_rms_bwd(dz1b, w_o, o_a, o_b, norm_a_g + tok[0, 0], norm_b_g)
    dqa, dka, dva, dsink = _attn_a_bwd(proj, sinks_a, d_oa, o_a, lse_a)
    dconv = dconv.reshape(4, 2 * FF)
    tok = on_small(dict(loss=st2[2, 0:1], norm_a_g=st_n[0], norm_b_g=st_n[1], sinks_a=dsink[:, 0],
                        ln1_g=st1[0], ln1_b=st1[1], conv_w=dconv[0:3].reshape(-1), conv_b=dconv[3],
                        ln2_g=st2[0], ln2_b=st2[1]))
    slopes = slopes + tok[0, 0]
    bwd_b = None
    for r in reversed(B_DILATIONS):
        bwd_b = _attn_b_bwd(proj, slopes, d_ob, o_b, lse_b, r, bwd_b, BF16 if r == 1 else F32)
    dparts = tuple(_hbm(a) for a in (dqa, dka, dva, *bwd_b))
    tok = on_grad(0, *_grad_w_in(dparts, xb))
    return _grad_x(dz1, dparts, w_in_t, tok)


SMALL_ORDER = ("loss", "norm_a_g", "norm_b_g", "sinks_a", "ln1_g", "ln1_b", "conv_b", "ln2_g", "ln2_b", "conv_w")
SMALL_SIZES = dict(loss=1, norm_a_g=512, norm_b_g=512, sinks_a=8, ln1_g=D, ln1_b=D, conv_b=2 * FF, ln2_g=D, ln2_b=D,
                   conv_w=3 * 2 * FF)


def _pack(parts, rows):
    flat = jnp.concatenate([parts[k].reshape(-1).astype(F32) for k in parts])
    return jnp.pad(flat, (0, rows * D - flat.shape[0])).reshape(rows, D)


def _unpack(buf, names, sizes):
    flat = buf.reshape(-1)
    out, at = {}, 0
    for k in names:
        out[k] = flat[at:at + sizes[k]]
        at += sizes[k]
    return out


def kernel(x, w_in, norm_a_g, norm_b_g, sinks_a, w_o, ln1_g, ln1_b, w_up, conv_w, conv_b, w_down, ln2_g, ln2_b, loss_target, m_w_in, m_norm_a_g, m_norm_b_g, m_sinks_a, m_w_o, m_ln1_g, m_ln1_b, m_w_up, m_conv_w, m_conv_b, m_w_down, m_ln2_g, m_ln2_b, v_w_in, v_norm_a_g, v_norm_b_g, v_sinks_a, v_w_o, v_ln1_g, v_ln1_b, v_w_up, v_conv_w, v_conv_b, v_w_down, v_ln2_g, v_ln2_b):
    xi, yi, ci = _place()
    chip = (2 * xi + yi).astype(I32)
    core = ci.astype(I32)

    w_in_rows, m_w_in_rows, v_w_in_rows = w_in.T, m_w_in.T, v_w_in.T
    shards = (w_in_rows.astype(BF16), w_o.astype(BF16), w_up.astype(BF16), w_down.astype(BF16))
    w_in_t, conv_w4 = _gather_w_in(shards[0], conv_w)
    conv_w_f = conv_w4.transpose(1, 0, 2).reshape(3, 2 * FF)
    w_started, w_tok = _weights_start(shards[1:], conv_w4)
    slopes = jnp.asarray(SLOPES, F32) + w_tok[0, 0]

    halves_rows = [r // 2 for r in SHARD_ROWS]
    grads4, grads_b4, started = [None] * 4, [None] * 4, [None] * 4

    def on_grad(k, g, g_b):
        grads4[k] = g.reshape(N_CHIPS, 2, halves_rows[k], D)
        grads_b4[k] = g_b.reshape(N_CHIPS, 2, halves_rows[k], D)
        if k > 1:
            return None
        group = (1, 2, 3) if k == 1 else (0,)
        sts, tok = _grads_start([grads_b4[i] for i in group], f"grads_start_{k}")
        for i, st in zip(group, sts):
            started[i] = st
        return tok

    small_rows = 32
    small_started = []

    def on_small(parts):
        st, tok = _small_start(_pack({k: parts[k] for k in SMALL_ORDER}, small_rows))
        small_started.append(st)
        return tok

    gx = _local_step(
        x[0], loss_target[0], w_in_t, lambda k, after: _weights_wait(w_started[k - 1], after, f"weights_wait_{k}"),
        norm_a_g, norm_b_g, sinks_a, ln1_g, ln1_b, conv_w_f, conv_b, ln2_g, ln2_b, slopes, on_grad, on_small)

    tiles = (96, 128, 352, 176)
    core_chip = jnp.stack([core, chip])
    got = _grads_wait(started[1:], gx, "grads_wait_1")
    halves = [_sum_partials(grads4[k], got[k - 1], core_chip, f"sum_partials_{k}", tiles[k]) for k in (1, 2, 3)]
    g_w_o, g_w_up_rows, g_w_down = _swap_halves(halves, "swap_halves")
    g_w_up = g_w_up_rows.T
    delta, new_m, new_v = {}, {}, {}
    for k, g, tr in (("w_o", g_w_o, 128), ("w_up", g_w_up, 256), ("w_down", g_w_down, 176)):
        delta[k], new_m[k], new_v[k] = _adamw(dict(w_o=w_o, w_up=w_up, w_down=w_down)[k], g,
                                              dict(w_o=m_w_o, w_up=m_w_up, w_down=m_w_down)[k],
                                              dict(w_o=v_w_o, w_up=v_w_up, w_down=v_w_down)[k], f"adamw_{k}", tr)

    got = _grads_wait(started[:1], delta["w_up"], "grads_wait_0")
    half_in = _sum_partials(grads4[0], got[0], core_chip, "sum_partials_0", tiles[0])
    (g_w_in_rows,) = _swap_halves([half_in], "swap_halves_in")
    small_mine, small_land = _small_wait(small_started[0], g_w_in_rows)
    totals = _small_sum(small_mine, small_land, (4 * xi + 2 * yi + ci).astype(I32).reshape(1))
    tot = _unpack(totals, SMALL_ORDER, SMALL_SIZES)
    loss = tot["loss"][0]
    cols = 2 * FF // N_CHIPS
    g_conv_w = lax.dynamic_slice(tot["conv_w"].reshape(3, 2 * FF), (0, chip * cols), (3, cols))
    g_small = dict(norm_a_g=tot["norm_a_g"], norm_b_g=tot["norm_b_g"], sinks_a=tot["sinks_a"], ln1_g=tot["ln1_g"],
                   ln1_b=tot["ln1_b"], conv_w=g_conv_w, conv_b=tot["conv_b"], ln2_g=tot["ln2_g"], ln2_b=tot["ln2_b"])

    weights = dict(w_in=w_in, norm_a_g=norm_a_g, norm_b_g=norm_b_g, sinks_a=sinks_a, w_o=w_o, ln1_g=ln1_g, ln1_b=ln1_b,
                   w_up=w_up, conv_w=conv_w, conv_b=conv_b, w_down=w_down, ln2_g=ln2_g, ln2_b=ln2_b)
    ms = dict(w_in=m_w_in, norm_a_g=m_norm_a_g, norm_b_g=m_norm_b_g, sinks_a=m_sinks_a, w_o=m_w_o, ln1_g=m_ln1_g,
              ln1_b=m_ln1_b, w_up=m_w_up, conv_w=m_conv_w, conv_b=m_conv_b, w_down=m_w_down, ln2_g=m_ln2_g, ln2_b=m_ln2_b)
    vs = dict(w_in=v_w_in, norm_a_g=v_norm_a_g, norm_b_g=v_norm_b_g, sinks_a=v_sinks_a, w_o=v_w_o, ln1_g=v_ln1_g,
              ln1_b=v_ln1_b, w_up=v_w_up, conv_w=v_conv_w, conv_b=v_conv_b, w_down=v_w_down, ln2_g=v_ln2_g, ln2_b=v_ln2_b)
    order = list(weights)
    grad = dict(g_small, w_in=g_w_in_rows.T, w_o=g_w_o, w_up=g_w_up, w_down=g_w_down)

    delta["w_in"], new_m["w_in"], new_v["w_in"] = [
        a.T for a in _adamw(w_in_rows, g_w_in_rows, m_w_in_rows, v_w_in_rows, "adamw_w_in", 144)]
    small_names = [k for k in order if k not in delta]
    sizes = {k: weights[k].size for k in small_names}
    rows = 16
    packed = [_pack({k: src[k] for k in small_names}, rows) for src in (weights, grad, ms, vs)]
    for res, buf in zip((delta, new_m, new_v), _adamw(*packed, "adamw_small", rows)):
        for k, val in _unpack(buf, small_names, sizes).items():
            res[k] = val.reshape(weights[k].shape)

    return (loss, gx[None], *[grad[k] for k in order], *[delta[k] for k in order],
            *[new_m[k] for k in order], *[new_v[k] for k in order])
```

```python
import functools
import math

import jax
import jax.numpy as jnp
from jax import lax
from jax.experimental import pallas as pl
from jax.experimental.pallas import tpu as pltpu

F32, BF16, I32 = jnp.float32, jnp.bfloat16, jnp.int32

D = 1024
FF = 2816
HD = 64
NH = 8
WA, WB = 768, 1536
WIN = WA + WB
BLK = 128
ALPHA = 2.0 ** 0.25
LN_EPS, RMS_EPS = 1e-5, 1e-6
SCALE = 1.0 / math.sqrt(HD)
A_MAX_DIST, B_MAX_DIST = 127, 128
B_DILATIONS = (1, 4, 16)
SLOPES = tuple(2.0 ** (-(i + 1)) for i in range(NH))
SHARD_ROWS = (WIN // 4, D // 4, 2 * FF // 4, FF // 4)
N_CHIPS = 4
ADAM_LR, ADAM_B1, ADAM_B2, ADAM_EPS, ADAM_WD, ADAM_STEP = 0.001, 0.9, 0.999, 1e-08, 0.01, 10
MESH = pl.DeviceIdType.MESH
ANY = pl.BlockSpec(memory_space=pl.ANY)
SMEM = pl.BlockSpec(memory_space=pltpu.SMEM)
VMEM = pl.BlockSpec(memory_space=pltpu.VMEM)
HBM = pl.BlockSpec(memory_space=pltpu.HBM)
SEM = pl.BlockSpec(memory_space=pltpu.SEMAPHORE)
DATAFLOW = pltpu.SideEffectType.DATAFLOW_SIDE_EFFECTING


def _cp(sem, mb=48):
    return pltpu.CompilerParams(dimension_semantics=sem, vmem_limit_bytes=mb << 20)


def _nn(a, b):
    return lax.dot_general(a, b, (((1,), (0,)), ((), ())), preferred_element_type=F32)


def _nt(a, b):
    return lax.dot_general(a, b, (((1,), (1,)), ((), ())), preferred_element_type=F32)


def _tn(a, b):
    return lax.dot_general(a, b, (((0,), (0,)), ((), ())), preferred_element_type=F32)


def _resident(shape):
    n = len(shape)
    return pl.BlockSpec(shape, lambda *_: (0,) * n, pipeline_mode=pl.Buffered(1))


def _const(shape):
    n = len(shape)
    return pl.BlockSpec(shape, lambda *_: (0,) * n)


def _proj(x, w_t, name, tm=512):
    s = x.shape[0]
    n = w_t.shape[0]

    def body(x_ref, w_ref, o_ref, xb_ref):
        xb = x_ref[...].astype(BF16)
        xb_ref[...] = xb
        res = _nt(xb, w_ref[...])
        for g in range(n // 128):
            o_ref[g] = res[:, 128 * g:128 * (g + 1)]

    return pl.pallas_call(
        body, name=name, grid=(s // tm,),
        in_specs=[pl.BlockSpec((tm, D), lambda i: (i, 0)), _resident((n, D))],
        out_specs=[pl.BlockSpec((n // 128, tm, 128), lambda i: (0, i, 0)), pl.BlockSpec((tm, D), lambda i: (i, 0))],
        out_shape=[jax.ShapeDtypeStruct((n // 128, s, 128), F32), jax.ShapeDtypeStruct((s, D), BF16)],
        compiler_params=_cp(("parallel",)),
    )(x, w_t)


def _grad_w(lhs, rhs, name, tm, tk=2048, lhs_halves=False):
    s = rhs.shape[0]
    if lhs_halves:
        per_half = lhs.shape[2] // tm
        n = 2 * lhs.shape[2]
        lhs_spec = pl.BlockSpec((None, tk, tm), lambda i, k: (i // per_half, k, i % per_half))
    else:
        n = lhs.shape[1]
        lhs_spec = pl.BlockSpec((tk, tm), lambda i, k: (k, i))
    nk = s // tk

    def body(l_ref, r_ref, o_ref, ob_ref):
        k = pl.program_id(1)

        @pl.when(k == 0)
        def _():
            o_ref[...] = jnp.zeros_like(o_ref)

        o_ref[...] += _tn(l_ref[...], r_ref[...])

        @pl.when(k == nk - 1)
        def _():
            ob_ref[...] = o_ref[...].astype(BF16)

    return pl.pallas_call(
        body, name=name, grid=(n // tm, nk),
        in_specs=[lhs_spec, pl.BlockSpec((tk, D), lambda i, k: (k, 0))],
        out_specs=[pl.BlockSpec((tm, D), lambda i, k: (i, 0))] * 2,
        out_shape=[pltpu.HBM((n, D), F32), pltpu.HBM((n, D), BF16)],
        compiler_params=_cp(("parallel", "arbitrary")),
    )(lhs, rhs)


def _band_base(max_dist, dist_unit, first):
    row = lax.broadcasted_iota(I32, (BLK, 2 * BLK), 0)
    col = lax.broadcasted_iota(I32, (BLK, 2 * BLK), 1)
    dist = BLK + row - col
    ok = (dist >= 0) & (dist <= max_dist)
    if first:
        ok = ok & (col >= BLK)
    return jnp.where(ok, dist.astype(F32) * (-float(dist_unit)), -jnp.inf)


def _half_mask(shape, e):
    lane = lax.broadcasted_iota(I32, shape, 1)
    return (lane < HD) if e == 0 else (lane >= HD)


def _to_half(x, e, g):
    if g != e:
        x = pltpu.roll(x, HD, 1)
    return jnp.where(_half_mask(x.shape, g), x, 0.0)


def _stack_heads(scalars, tile):
    return jnp.concatenate([scalars[0] * tile, scalars[1] * tile], axis=0)


def _pair_fwd(q2, kb, vb, base, slopes, kv_heads, sinks):
    lo = _half_mask((BLK, 2 * HD), 0)
    if slopes is None:
        bias = base
    elif sinks is None:
        bias = _stack_heads(slopes, base)
    else:
        col0 = lax.broadcasted_iota(I32, base.shape, 1) == 0
        bias = jnp.concatenate([jnp.where(col0, sinks[e], slopes[e] * base) for e in (0, 1)], axis=0)
    qs = jnp.concatenate([_to_half(q2, e, kv_heads[e]) * SCALE for e in (0, 1)], axis=0).astype(BF16)
    s = _nt(qs, kb) + bias
    m = jnp.max(s, axis=1, keepdims=True)
    p = jnp.exp(s - m)
    l = jnp.sum(p, axis=1, keepdims=True)
    o = _nn(p.astype(BF16), vb) / l
    lse = m + jnp.log(l)
    halves = []
    for e in (0, 1):
        oh = o[e * BLK:(e + 1) * BLK]
        halves.append(pltpu.roll(oh, HD, 1) if kv_heads[e] != e else oh)
    o2 = jnp.where(lo, halves[0], halves[1])
    lse2 = jnp.where(lo, jnp.broadcast_to(lse[:BLK], (BLK, 2 * HD)), jnp.broadcast_to(lse[BLK:], (BLK, 2 * HD)))
    return o2, lse2


def _pair_bwd(q2, kb, vb, do2, o2, lse2, base, slopes, kv_heads, sinks):
    lo = _half_mask((BLK, 2 * HD), 0)
    prod = do2 * o2
    lses, deltas = [], []
    for e in (0, 1):
        hq = _half_mask((BLK, 2 * HD), e)
        lses.append(jnp.max(jnp.where(hq, lse2, -jnp.inf), axis=1, keepdims=True))
        deltas.append(jnp.sum(jnp.where(hq, prod, 0.0), axis=1, keepdims=True))
    lse = jnp.concatenate(lses, axis=0)
    delta = jnp.concatenate(deltas, axis=0)
    qs = jnp.concatenate([_to_half(q2, e, kv_heads[e]) * SCALE for e in (0, 1)], axis=0).astype(BF16)
    dos = jnp.concatenate([_to_half(do2, e, kv_heads[e]) for e in (0, 1)], axis=0).astype(BF16)
    p = jnp.exp(_nt(qs, kb) + (base if slopes is None else _stack_heads(slopes, base)) - lse)
    ds = (p * (_nt(dos, vb) - delta)).astype(BF16)
    dq = _nn(ds, kb) * SCALE
    halves = []
    for e in (0, 1):
        dqh = dq[e * BLK:(e + 1) * BLK]
        halves.append(pltpu.roll(dqh, HD, 1) if kv_heads[e] != e else dqh)
    dq2 = jnp.where(lo, halves[0], halves[1])
    dk2 = _tn(ds, qs)
    dv2 = _tn(p.astype(BF16), dos)
    dsinks = []
    if sinks is not None:
        for e in (0, 1):
            dsinks.append(jnp.sum(-jnp.exp(sinks[e] - lses[e]) * deltas[e], axis=0, keepdims=True))
    return dq2, dk2, dv2, dsinks


A_BLOCKS_PER_STEP = 2
A_BLOCKS_PER_STEP_BWD = 1


def _attn_a_fwd(proj, sinks):
    s = proj.shape[1]
    nq = A_BLOCKS_PER_STEP
    rows = BLK * nq
    steps = s // rows

    def body(sink_ref, q_ref, kp_ref, kc_ref, vp_ref, vc_ref, o_ref, lse_ref):
        n = pl.program_id(0)
        base_rest = _band_base(A_MAX_DIST, 1, False)
        base_0 = jnp.where(n > 0, base_rest, _band_base(A_MAX_DIST, 1, True))
        for i in range(nq):
            cur = pl.ds(i * BLK, BLK)
            k_prev = kc_ref[pl.ds((i - 1) * BLK, BLK), :] if i > 0 else kp_ref[...]
            v_prev = vc_ref[pl.ds((i - 1) * BLK, BLK), :] if i > 0 else vp_ref[...]
            first_key = lax.broadcasted_iota(I32, (2 * BLK, 128), 0) == 0
            kb = jnp.where(first_key, 0.0, jnp.concatenate([k_prev, kc_ref[cur, :]], axis=0)).astype(BF16)
            vb = jnp.where(first_key, 0.0, jnp.concatenate([v_prev, vc_ref[cur, :]], axis=0)).astype(BF16)
            for j in range(NH // 2):
                g = j // 2
                o2, lse2 = _pair_fwd(q_ref[j, cur, :], kb, vb, base_rest if i > 0 else base_0,
                                     (SLOPES[2 * j], SLOPES[2 * j + 1]), (g, g), (sink_ref[2 * j], sink_ref[2 * j + 1]))
                o_ref[j, cur, :] = o2
                lse_ref[j, cur, :] = lse2

    before = lambda n: jnp.maximum(n * nq - 1, 0)
    slab = lambda g: pl.BlockSpec((None, rows, 128), lambda n: (g, n, 0))
    edge = lambda g: pl.BlockSpec((None, BLK, 128), lambda n: (g, before(n), 0))
    quad = pl.BlockSpec((4, rows, 128), lambda n: (0, n, 0))
    return pl.pallas_call(
        body, name="attn_a_fwd", grid=(steps,),
        in_specs=[SMEM, quad, edge(4), slab(4), edge(5), slab(5)],
        out_specs=[quad, quad],
        out_shape=[jax.ShapeDtypeStruct((4, s, 128), F32)] * 2,
        compiler_params=_cp(("parallel",)),
    )(sinks, proj, proj, proj, proj, proj)


def _attn_a_bwd(proj, sinks, d_o, o, lse):
    s = proj.shape[1]
    nq = A_BLOCKS_PER_STEP_BWD
    rows = BLK * nq
    steps = s // rows

    def body(sink_ref, q_ref, kp_ref, kc_ref, vp_ref, vc_ref, do_ref, o_ref, lse_ref,
             dq_ref, dk_ref, dv_ref, dsink_ref, kcar, vcar):
        n = pl.program_id(0)

        @pl.when(n == 0)
        def _():
            kcar[...] = jnp.zeros_like(kcar)
            vcar[...] = jnp.zeros_like(vcar)
            dsink_ref[...] = jnp.zeros_like(dsink_ref)

        dk_ref[...] = kcar[...].astype(BF16)
        dv_ref[...] = vcar[...].astype(BF16)

        @pl.when(n < steps)
        def _():
            base_rest = _band_base(A_MAX_DIST, 1, False)
            base_0 = jnp.where(n > 0, base_rest, _band_base(A_MAX_DIST, 1, True))
            for i in range(nq):
                cur = pl.ds(i * BLK, BLK)
                k_prev = kc_ref[pl.ds((i - 1) * BLK, BLK), :] if i > 0 else kp_ref[...]
                v_prev = vc_ref[pl.ds((i - 1) * BLK, BLK), :] if i > 0 else vp_ref[...]
                kb = jnp.concatenate([k_prev, kc_ref[cur, :]], axis=0).astype(BF16)
                vb = jnp.concatenate([v_prev, vc_ref[cur, :]], axis=0).astype(BF16)
                dk_win = dv_win = None
                for j in range(NH // 2):
                    g = j // 2
                    dq2, dk2, dv2, dsk = _pair_bwd(q_ref[j, cur, :], kb, vb, do_ref[j, cur, :], o_ref[j, cur, :],
                                                   lse_ref[j, cur, :], base_rest if i > 0 else base_0,
                                                   (SLOPES[2 * j], SLOPES[2 * j + 1]), (g, g),
                                                   (sink_ref[2 * j], sink_ref[2 * j + 1]))
                    dq_ref[j, cur, :] = dq2.astype(BF16)
                    dk_win = dk2 if j == 0 else dk_win + dk2
                    dv_win = dv2 if j == 0 else dv_win + dv2
                    for e in (0, 1):
                        h = 2 * j + e
                        dsink_ref[h:h + 1, :] += jnp.broadcast_to(dsk[e], (1, 128))
                if i == 0:
                    last = pl.ds((nq - 1) * BLK, BLK)
                    dk_ref[last, :] = (kcar[last, :] + dk_win[:BLK]).astype(BF16)
                    dv_ref[last, :] = (vcar[last, :] + dv_win[:BLK]).astype(BF16)
                else:
                    kcar[pl.ds((i - 1) * BLK, BLK), :] += dk_win[:BLK]
                    vcar[pl.ds((i - 1) * BLK, BLK), :] += dv_win[:BLK]
                kcar[cur, :] = dk_win[BLK:]
                vcar[cur, :] = dv_win[BLK:]

    cur_step = lambda n: jnp.minimum(n, steps - 1)
    before = lambda n: jnp.maximum(cur_step(n) * nq - 1, 0)
    out_prev = lambda n: jnp.maximum(n - 1, 0)
    quad = pl.BlockSpec((4, rows, 128), lambda n: (0, cur_step(n), 0))
    slab = lambda g: pl.BlockSpec((None, rows, 128), lambda n: (g, cur_step(n), 0))
    edge = lambda g: pl.BlockSpec((None, BLK, 128), lambda n: (g, before(n), 0))
    return pl.pallas_call(
        body, name="attn_a_bwd", grid=(steps + 1,),
        in_specs=[SMEM, quad, edge(4), slab(4), edge(5), slab(5), quad, quad, quad],
        out_specs=[quad,
                   pl.BlockSpec((rows, 128), lambda n: (out_prev(n), 0)),
                   pl.BlockSpec((rows, 128), lambda n: (out_prev(n), 0)),
                   pl.BlockSpec((NH, 128), lambda n: (0, 0))],
        out_shape=[pltpu.HBM((4, s, 128), BF16), pltpu.HBM((s, 128), BF16), pltpu.HBM((s, 128), BF16),
                   jax.ShapeDtypeStruct((NH, 128), F32)],
        scratch_shapes=[pltpu.VMEM((rows, 128), F32), pltpu.VMEM((rows, 128), F32)],
        compiler_params=_cp(("arbitrary",)),
    )(sinks, proj, proj, proj, proj, proj, d_o, o, lse)


def _stream(rho, i, r):
    start = i * BLK * r + rho
    return pl.ds(start, BLK, stride=r) if r > 1 else pl.ds(start, BLK)


def _for_streams(r, fn, side_by_side=4):
    if r <= side_by_side:
        for rho in range(r):
            fn(rho)
    else:
        def group(it, carry):
            for u in range(side_by_side):
                fn(side_by_side * it + u)
            return carry

        lax.fori_loop(0, r // side_by_side, group, 0)


B_BLOCKS_PER_STEP = {1: 8, 4: 2, 16: 1}
B_BLOCKS_PER_STEP_FWD = {1: 16, 4: 4, 16: 1}


def _attn_b_fwd(proj, slopes, r, so_far=None):
    s = proj.shape[1]
    nq = B_BLOCKS_PER_STEP_FWD[r]
    rows = BLK * r * nq
    steps = s // rows
    qc, kc, vc = WA // 128, WA // 128 + 4, WA // 128 + 8
    chained = so_far is not None

    def body(slope_ref, q_ref, kp_ref, kc_ref, vp_ref, vc_ref, *rest):
        po_ref, pl_ref = rest[:2] if chained else (None, None)
        o_ref, lse_ref = rest[-2:]
        j = pl.program_id(0)
        sb = pl.program_id(1)
        sl2 = (slope_ref[2 * j], slope_ref[2 * j + 1])
        bias_rest = _stack_heads(sl2, _band_base(B_MAX_DIST, r, False))
        bias_0 = jnp.where(sb > 0, bias_rest, _stack_heads(sl2, _band_base(B_MAX_DIST, r, True)))

        def stream(rho):
            for i in range(nq):
                cur = _stream(rho, i, r)
                k_prev = kc_ref[_stream(rho, i - 1, r), :] if i > 0 else kp_ref[_stream(rho, 0, r), :]
                v_prev = vc_ref[_stream(rho, i - 1, r), :] if i > 0 else vp_ref[_stream(rho, 0, r), :]
                kb = jnp.concatenate([k_prev, kc_ref[cur, :]], axis=0).astype(BF16)
                vb = jnp.concatenate([v_prev, vc_ref[cur, :]], axis=0).astype(BF16)
                o2, lse2 = _pair_fwd(q_ref[cur, :], kb, vb, bias_rest if i > 0 else bias_0, None, (0, 1), None)
                if chained:
                    lse1 = pl_ref[cur, :]
                    m = jnp.maximum(lse1, lse2)
                    e1, e2 = jnp.exp(lse1 - m), jnp.exp(lse2 - m)
                    den = e1 + e2
                    o2 = (e1 * po_ref[cur, :] + e2 * o2) * (1.0 / den)
                    lse2 = m + jnp.log(den)
                o_ref[cur, :] = o2
                lse_ref[cur, :] = lse2

        _for_streams(r, stream, side_by_side=16)

    before = lambda sb: jnp.maximum(sb * nq - 1, 0)
    result = pl.BlockSpec((None, rows, 128), lambda j, sb: (j, sb, 0))
    return pl.pallas_call(
        body, name=f"attn_b_fwd_r{r}", grid=(NH // 2, steps),
        in_specs=[SMEM,
                  pl.BlockSpec((None, rows, 128), lambda j, sb: (qc + j, sb, 0)),
                  pl.BlockSpec((None, BLK * r, 128), lambda j, sb: (kc + j, before(sb), 0)),
                  pl.BlockSpec((None, rows, 128), lambda j, sb: (kc + j, sb, 0)),
                  pl.BlockSpec((None, BLK * r, 128), lambda j, sb: (vc + j, before(sb), 0)),
                  pl.BlockSpec((None, rows, 128), lambda j, sb: (vc + j, sb, 0))] + ([result] * 2 if chained else []),
        out_specs=[result] * 2,
        out_shape=[jax.ShapeDtypeStruct((4, s, 128), F32)] * 2,
        compiler_params=_cp(("parallel", "parallel")),
    )(slopes, proj, proj, proj, proj, proj, *(so_far if chained else ()))


def _attn_b_bwd(proj, slopes, d_o, o, lse, r, so_far=None, dtype=F32):
    s = proj.shape[1]
    nq = B_BLOCKS_PER_STEP[r]
    rows = BLK * r * nq
    steps = s // rows
    qc, kc, vc = WA // 128, WA // 128 + 4, WA // 128 + 8
    chained = so_far is not None

    def body(slope_ref, q_ref, kp_ref, kc_ref, vp_ref, vc_ref, do_ref, o_ref, lse_ref, *rest):
        pq_ref, pk_ref, pv_ref = rest[:3] if chained else (None, None, None)
        dq_ref, dk_ref, dv_ref, kcar, vcar = rest[-5:]
        j = pl.program_id(0)
        sb = pl.program_id(1)

        @pl.when(sb == 0)
        def _():
            kcar[...] = jnp.zeros_like(kcar)
            vcar[...] = jnp.zeros_like(vcar)

        def settled(car, p_ref, idx):
            return car[idx] + p_ref[idx] if chained else car[idx]

        dk_ref[...] = settled(kcar, pk_ref, ...).astype(dtype)
        dv_ref[...] = settled(vcar, pv_ref, ...).astype(dtype)

        @pl.when(sb < steps)
        def _():
            sl2 = (slope_ref[2 * j], slope_ref[2 * j + 1])
            bias_rest = _stack_heads(sl2, _band_base(B_MAX_DIST, r, False))
            bias_0 = jnp.where(sb > 0, bias_rest, _stack_heads(sl2, _band_base(B_MAX_DIST, r, True)))

            def stream(rho):
                for i in range(nq):
                    cur = _stream(rho, i, r)
                    k_prev = kc_ref[_stream(rho, i - 1, r), :] if i > 0 else kp_ref[_stream(rho, 0, r), :]
                    v_prev = vc_ref[_stream(rho, i - 1, r), :] if i > 0 else vp_ref[_stream(rho, 0, r), :]
                    kb = jnp.concatenate([k_prev, kc_ref[cur, :]], axis=0).astype(BF16)
                    vb = jnp.concatenate([v_prev, vc_ref[cur, :]], axis=0).astype(BF16)
                    dq2, dk2, dv2, _ = _pair_bwd(q_ref[cur, :], kb, vb, do_ref[cur, :], o_ref[cur, :], lse_ref[cur, :],
                                                 bias_rest if i > 0 else bias_0, None, (0, 1), None)
                    dq_ref[cur, :] = (dq2 + pq_ref[cur, :] if chained else dq2).astype(dtype)
                    if i == 0:
                        last = (_stream(rho, nq - 1, r), slice(None))
                        dk_ref[last] = (settled(kcar, pk_ref, last) + dk2[:BLK]).astype(dtype)
                        dv_ref[last] = (settled(vcar, pv_ref, last) + dv2[:BLK]).astype(dtype)
                    else:
                        kcar[_stream(rho, i - 1, r), :] += dk2[:BLK]
                        vcar[_stream(rho, i - 1, r), :] += dv2[:BLK]
                    kcar[cur, :] = dk2[BLK:]
                    vcar[cur, :] = dv2[BLK:]

            _for_streams(r, stream, side_by_side=8)

    cur_step = lambda sb: jnp.minimum(sb, steps - 1)
    before = lambda sb: jnp.maximum(cur_step(sb) * nq - 1, 0)
    out_prev = lambda sb: jnp.maximum(sb - 1, 0)
    tile = lambda slab: pl.BlockSpec((None, rows, 128), lambda j, sb: (slab + j, cur_step(sb), 0))
    edge = lambda slab: pl.BlockSpec((None, BLK * r, 128), lambda j, sb: (slab + j, before(sb), 0))
    late = pl.BlockSpec((None, rows, 128), lambda j, sb: (j, out_prev(sb), 0))
    grads = [tile(0), late, late]
    return pl.pallas_call(
        body, name=f"attn_b_bwd_r{r}", grid=(NH // 2, steps + 1),
        in_specs=[SMEM, tile(qc), edge(kc), tile(kc), edge(vc), tile(vc), tile(0), tile(0), tile(0)]
        + (grads if chained else []),
        out_specs=grads,
        out_shape=[pltpu.HBM((4, s, 128), dtype)] * 3,
        scratch_shapes=[pltpu.VMEM((rows, 128), F32), pltpu.VMEM((rows, 128), F32)],
        compiler_params=_cp(("parallel", "arbitrary")),
    )(slopes, proj, proj, proj, proj, proj, d_o, o, lse, *(so_far if chained else ()))


def _row(v):
    return v.reshape(1, -1)


def _layer_norm_stats(z):
    mu = jnp.mean(z, axis=-1, keepdims=True)
    zc = z - mu
    var = jnp.mean(zc * zc, axis=-1, keepdims=True)
    rstd = lax.rsqrt(var + LN_EPS)
    return zc * rstd, rstd


def _layer_norm_bwd(dh, zh, rstd, g):
    dzh = dh * g
    return rstd * (dzh - jnp.mean(dzh, axis=-1, keepdims=True) - zh * jnp.mean(dzh * zh, axis=-1, keepdims=True))


def _rms(o):
    return lax.rsqrt(jnp.mean(o * o, axis=-1, keepdims=True) + RMS_EPS)


def _mix_ln1(x, o_a, o_b, norm_a_g, norm_b_g, w_o, ln1_g, ln1_b, tm=256):
    s = x.shape[0]

    def wide(ref):
        return jnp.concatenate([ref[j] for j in range(4)], axis=1)

    def body(x_ref, oa_ref, ob_ref, ga_ref, gb_ref, wo_ref, g_ref, b_ref, cat_ref, z1_ref, h1_ref, h1b_ref):
        oa, ob = wide(oa_ref), wide(ob_ref)
        na = oa * _rms(oa) * ga_ref[...]
        nb_ = ob * _rms(ob) * gb_ref[...]
        cat = jnp.concatenate([na, nb_], axis=1).astype(BF16)
        cat_ref[...] = cat
        z1 = ALPHA * x_ref[...] + _nn(cat, wo_ref[...])
        z1_ref[...] = z1
        zh, _ = _layer_norm_stats(z1)
        h1 = zh * g_ref[...] + b_ref[...]
        h1_ref[...] = h1
        h1b_ref[...] = h1.astype(BF16)

    t512 = pl.BlockSpec((4, tm, 128), lambda i: (0, i, 0))
    td = pl.BlockSpec((tm, D), lambda i: (i, 0))
    return pl.pallas_call(
        body, name="mix_ln1", grid=(s // tm,),
        in_specs=[td] + [t512] * 2 + [_const((1, 512))] * 2 + [_resident((D, D))] + [_const((1, D))] * 2,
        out_specs=[td, td, td, td],
        out_shape=[jax.ShapeDtypeStruct((s, D), BF16), jax.ShapeDtypeStruct((s, D), F32),
                   jax.ShapeDtypeStruct((s, D), F32), jax.ShapeDtypeStruct((s, D), BF16)],
        compiler_params=_cp(("parallel",)),
    )(x, o_a, o_b, _row(norm_a_g), _row(norm_b_g), w_o, _row(ln1_g), _row(ln1_b))


def _gelu_and_grad(x):
    c = math.sqrt(2.0 / math.pi)
    x2 = x * x
    s = 0.5 * jnp.tanh(x * ((c * 0.044715) * x2 + c)) + 0.5
    dg = s + (x * ((6.0 * c * 0.044715) * x2 + 2.0 * c)) * (s - s * s)
    return x * s, dg


def _shifted(u, edge, row, down):
    groups = [u[8 * i:8 * i + 8] for i in range(u.shape[0] // 8)]
    others = [edge] + groups[:-1] if down else groups[1:] + [edge]
    moved = []
    for k in (1, 2):
        crossing = row >= 8 - k if down else row < k
        moved.append(jnp.concatenate([pltpu.roll(jnp.where(crossing, o, g), k if down else 8 - k, 0)
                                      for o, g in zip(others, groups)], axis=0))
    return moved


def _up_proj(h1b, w_up, tm=512):
    s = h1b.shape[0]

    def body(h_ref, w_ref, o_ref):
        h = h_ref[...]
        for half in (0, 1):
            o_ref[half] = _nn(h, w_ref[:, half * FF:(half + 1) * FF]).astype(BF16)

    return pl.pallas_call(
        body, name="up_proj", grid=(s // tm,),
        in_specs=[pl.BlockSpec((tm, D), lambda i: (i, 0)), _resident((D, 2 * FF))],
        out_specs=pl.BlockSpec((2, tm, FF), lambda i: (0, i, 0)),
        out_shape=jax.ShapeDtypeStruct((2, s, FF), BF16),
        compiler_params=_cp(("parallel",)),
    )(h1b, w_up)


def _conv_gelu(up, cwb, tm=256, tn=FF // 2, chunk_rows=16):
    s = up.shape[1]
    n_c = tm // chunk_rows

    def body(up_ref, c_ref, a_ref, g_ref, a1_ref, carry):
        @pl.when(pl.program_id(1) == 0)
        def _():
            carry[...] = jnp.zeros_like(carry)

        row = lax.broadcasted_iota(jnp.int32, (8, tn), 0)
        edge = [carry[0], carry[1]]
        for c in range(n_c):
            rows = pl.ds(c * chunk_rows, chunk_rows)
            u = []
            for half in (0, 1):
                x = up_ref[half, rows, :].astype(F32)
                r1, r2 = _shifted(x, edge[half], row, True)
                u.append(r2 * c_ref[0, half:half + 1, :] + r1 * c_ref[1, half:half + 1, :]
                         + x * c_ref[2, half:half + 1, :] + c_ref[3, half:half + 1, :])
                edge[half] = x[chunk_rows - 8:]
            g, dg = _gelu_and_grad(u[0])
            a_ref[rows, :] = (g * u[1]).astype(BF16)
            g_ref[rows, :] = g.astype(BF16)
            a1_ref[rows, :] = (u[1] * dg).astype(BF16)
        for half in (0, 1):
            carry[half] = edge[half]

    pair = pl.BlockSpec((2, tm, tn), lambda j, i: (0, i, j))
    tile = pl.BlockSpec((tm, tn), lambda j, i: (i, j))
    return pl.pallas_call(
        body, name="conv_gelu", grid=(FF // tn, s // tm),
        in_specs=[pair, pl.BlockSpec((4, 2, tn), lambda j, i: (0, 0, j))],
        out_specs=[tile, tile, tile],
        out_shape=[jax.ShapeDtypeStruct((s, FF), BF16)] * 3,
        scratch_shapes=[pltpu.VMEM((2, 8, tn), F32)],
        compiler_params=_cp(("parallel", "arbitrary")),
    )(up, cwb)


def _down_ln2_loss(a, w_down, h1, target, ln2_g, ln2_b, tm=512):
    s = a.shape[0]

    def body(a_ref, w_ref, h_ref, t_ref, g_ref, b_ref, dz_ref, dzb_ref, st_ref):
        @pl.when(pl.program_id(0) == 0)
        def _():
            st_ref[...] = jnp.zeros_like(st_ref)

        z2 = ALPHA * h_ref[...] + _nn(a_ref[...], w_ref[...])
        zh, rstd = _layer_norm_stats(z2)
        diff = zh * g_ref[...] + b_ref[...] - t_ref[...]
        part = 0.5 * jnp.sum(jnp.mean(diff * diff, axis=-1, keepdims=True), axis=0, keepdims=True)
        dy = diff * (1.0 / D)
        st_ref[0:1, :] += jnp.sum(dy * zh, axis=0, keepdims=True)
        st_ref[1:2, :] += jnp.sum(dy, axis=0, keepdims=True)
        st_ref[2:3, :] += jnp.broadcast_to(part, (1, D))
        dz = _layer_norm_bwd(dy, zh, rstd, g_ref[...])
        dz_ref[...] = dz
        dzb_ref[...] = dz.astype(BF16)

    td = pl.BlockSpec((tm, D), lambda i: (i, 0))
    return pl.pallas_call(
        body, name="down_ln2_loss", grid=(s // tm,),
        in_specs=[pl.BlockSpec((tm, FF), lambda i: (i, 0)), _resident((FF, D)), td, td, _const((1, D)), _const((1, D))],
        out_specs=[td, td, _const((8, D))],
        out_shape=[jax.ShapeDtypeStruct((s, D), F32), jax.ShapeDtypeStruct((s, D), BF16),
                   jax.ShapeDtypeStruct((8, D), F32)],
        compiler_params=_cp(("arbitrary",)),
    )(a, w_down, h1, target, _row(ln2_g), _row(ln2_b))


def _d_act(dz2b, w_down, tm=512):
    s = dz2b.shape[0]

    def body(dz_ref, w_ref, o_ref):
        o_ref[...] = _nt(dz_ref[...], w_ref[...]).astype(BF16)

    return pl.pallas_call(
        body, name="d_act", grid=(s // tm,),
        in_specs=[pl.BlockSpec((tm, D), lambda i: (i, 0)), _resident((FF, D))],
        out_specs=pl.BlockSpec((tm, FF), lambda i: (i, 0)),
        out_shape=jax.ShapeDtypeStruct((s, FF), BF16),
        compiler_params=_cp(("parallel",)),
    )(dz2b, w_down)


def _conv_gelu_bwd(da, up, g, a1, cwb, tm=256, tn=FF // 2, chunk_rows=16):
    s = da.shape[0]
    n_i = s // tm
    n_c = tm // chunk_rows

    def body(da_ref, up_ref, g_ref, a1_ref, c_ref, dup_ref, dc_ref, carry):
        @pl.when(pl.program_id(1) == 0)
        def _():
            carry[...] = jnp.zeros_like(carry)
            dc_ref[...] = jnp.zeros_like(dc_ref)

        def fold(v):
            return jnp.sum(v.reshape(chunk_rows // 8, 8, v.shape[1]), axis=0)

        def chunk(cc, state):
            after, sums = state
            rows = pl.ds((n_c - 1 - cc) * chunk_rows, chunk_rows)
            da_c = da_ref[rows, :].astype(F32)
            dus = (da_c * a1_ref[rows, :].astype(F32), da_c * g_ref[rows, :].astype(F32))
            head, new_sums = [], []
            for half in (0, 1):
                du = dus[half]
                up = up_ref[half, rows, :].astype(F32)
                l1, l2 = _shifted(du, after[half], row, False)
                dup = (du * c_ref[2, half:half + 1, :] + l1 * c_ref[1, half:half + 1, :]
                       + l2 * c_ref[0, half:half + 1, :])
                dup_ref[half, rows, :] = dup.astype(BF16)
                parts = (fold(l2 * up), fold(l1 * up), fold(du * up), fold(du))
                new_sums.append(parts if sums is None else tuple(a + b for a, b in zip(sums[half], parts)))
                head.append(du[:8])
            return tuple(head), new_sums

        row = lax.broadcasted_iota(jnp.int32, (8, tn), 0)
        state = ((carry[0], carry[1]), None)
        for cc in range(n_c):
            state = chunk(cc, state)
        head, sums = state
        for half in (0, 1):
            carry[half] = head[half]
            for k in range(4):
                dc_ref[k, half:half + 1, :] += jnp.sum(sums[half][k], axis=0, keepdims=True)

    rev = lambda ii: n_i - 1 - ii
    tile = pl.BlockSpec((tm, tn), lambda j, ii: (rev(ii), j))
    pair = pl.BlockSpec((2, tm, tn), lambda j, ii: (0, rev(ii), j))
    per_col = pl.BlockSpec((4, 2, tn), lambda j, ii: (0, 0, j))
    return pl.pallas_call(
        body, name="conv_gelu_bwd", grid=(FF // tn, n_i),
        in_specs=[tile, pair, tile, tile, per_col],
        out_specs=[pair, per_col],
        out_shape=[jax.ShapeDtypeStruct((2, s, FF), BF16), jax.ShapeDtypeStruct((4, 2, FF), F32)],
        scratch_shapes=[pltpu.VMEM((2, 8, tn), F32)],
        compiler_params=_cp(("parallel", "arbitrary")),
    )(da, up, g, a1, cwb)


def _dh1_ln1_bwd(dz2, dup, w_up, z1, ln1_g, tm=512):
    s = dz2.shape[0]

    def body(dz2_ref, dup_ref, w_ref, z1_ref, g_ref, dz1_ref, dz1b_ref, st_ref):
        @pl.when(pl.program_id(0) == 0)
        def _():
            st_ref[...] = jnp.zeros_like(st_ref)

        dh = ALPHA * dz2_ref[...] + _nt(dup_ref[0], w_ref[:, :FF]) + _nt(dup_ref[1], w_ref[:, FF:])
        zh, rstd = _layer_norm_stats(z1_ref[...])
        st_ref[0:1, :] += jnp.sum(dh * zh, axis=0, keepdims=True)
        st_ref[1:2, :] += jnp.sum(dh, axis=0, keepdims=True)
        dz = _layer_norm_bwd(dh, zh, rstd, g_ref[...])
        dz1_ref[...] = dz
        dz1b_ref[...] = dz.astype(BF16)

    td = pl.BlockSpec((tm, D), lambda i: (i, 0))
    return pl.pallas_call(
        body, name="dh1_ln1_bwd", grid=(s // tm,),
        in_specs=[td, pl.BlockSpec((2, tm, FF), lambda i: (0, i, 0)), _resident((D, 2 * FF)), td, _const((1, D))],
        out_specs=[td, td, _const((8, D))],
        out_shape=[jax.ShapeDtypeStruct((s, D), F32), jax.ShapeDtypeStruct((s, D), BF16),
                   jax.ShapeDtypeStruct((8, D), F32)],
        compiler_params=_cp(("arbitrary",), 58),
    )(dz2, dup, w_up, z1, _row(ln1_g))


def _dcat_rms_bwd(dz1b, w_o, o_a, o_b, norm_a_g, norm_b_g, tm=512):
    s = dz1b.shape[0]

    def body(dz_ref, w_ref, oa_ref, ob_ref, ga_ref, gb_ref, da_ref, db_ref, st_ref):
        @pl.when(pl.program_id(0) == 0)
        def _():
            st_ref[...] = jnp.zeros_like(st_ref)

        dcat = _nt(dz_ref[...], w_ref[...])
        for k, (o_ref, g_ref, d_ref) in enumerate(((oa_ref, ga_ref, da_ref), (ob_ref, gb_ref, db_ref))):
            o = jnp.concatenate([o_ref[j] for j in range(4)], axis=1)
            dn = dcat[:, 512 * k:512 * (k + 1)]
            rr = _rms(o)
            oh = o * rr
            st_ref[k:k + 1, :] += jnp.sum(dn * oh, axis=0, keepdims=True)
            doh = dn * g_ref[...]
            d_o = rr * (doh - oh * jnp.mean(doh * oh, axis=-1, keepdims=True))
            for j in range(4):
                d_ref[j] = d_o[:, 128 * j:128 * (j + 1)]

    t512 = pl.BlockSpec((4, tm, 128), lambda i: (0, i, 0))
    return pl.pallas_call(
        body, name="dcat_rms_bwd", grid=(s // tm,),
        in_specs=[pl.BlockSpec((tm, D), lambda i: (i, 0)), _resident((D, D)), t512, t512,
                  _const((1, 512)), _const((1, 512))],
        out_specs=[t512, t512, _const((8, 512))],
        out_shape=[jax.ShapeDtypeStruct((4, s, 128), F32), jax.ShapeDtypeStruct((4, s, 128), F32),
                   jax.ShapeDtypeStruct((8, 512), F32)],
        compiler_params=_cp(("arbitrary",)),
    )(dz1b, w_o, o_a, o_b, _row(norm_a_g), _row(norm_b_g))


def _grad_w_in(dparts, xb, tk=2048):
    s = xb.shape[0]
    nk = s // tk

    def body(qa, ka, va, qb, kb, vb, x_ref, o_ref, ob_ref):
        i = pl.program_id(0)
        k = pl.program_id(1)

        @pl.when(k == 0)
        def _():
            o_ref[...] = jnp.zeros_like(o_ref)

        def add(blocks):
            o_ref[...] += _tn(jnp.concatenate(blocks, axis=1), x_ref[...])

        pl.when(i == 0)(lambda: add([qa[j] for j in range(4)] + [ka[...], va[...]]))
        pl.when(i == 1)(lambda: add([qb[j] for j in range(4)] + [kb[0], kb[1]]))
        pl.when(i == 2)(lambda: add([kb[0], kb[1]] + [vb[j] for j in range(4)]))

        @pl.when(k == nk - 1)
        def _():
            ob_ref[...] = o_ref[...].astype(BF16)

    def during(tile):
        return lambda i, k: jnp.where(i == tile, k, jnp.where(i < tile, 0, nk - 1))

    quad = lambda tile: pl.BlockSpec((4, tk, 128), lambda i, k: (0, during(tile)(i, k), 0))
    one = pl.BlockSpec((tk, 128), lambda i, k: (during(0)(i, k), 0))
    kb_spec = pl.BlockSpec((2, tk, 128), lambda i, k: (jnp.where(i == 2, 1, 0), jnp.where(i == 0, 0, k), 0))
    return pl.pallas_call(
        body, name="grad_w_in", grid=(3, nk),
        in_specs=[quad(0), one, one, quad(1), kb_spec, quad(2), pl.BlockSpec((tk, D), lambda i, k: (k, 0))],
        out_specs=[pl.BlockSpec((WA, D), lambda i, k: (i, 0))] * 2,
        out_shape=[pltpu.HBM((WIN, D), F32), pltpu.HBM((WIN, D), BF16)],
        compiler_params=_cp(("parallel", "arbitrary"), mb=56),
    )(*dparts, xb)


def _grad_x(dz1, dparts, w_in_t, zero, tm=512):
    s = dz1.shape[0]

    def body(dz_ref, qa, ka, va, qb, kb, vb, w_ref, z_ref, o_ref):
        dp = jnp.concatenate([qa[j] for j in range(4)] + [ka[...], va[...]]
                             + [ref[j] for ref in (qb, kb, vb) for j in range(4)], axis=1)
        o_ref[...] = ALPHA * dz_ref[...] + _nn(dp, w_ref[...]) + z_ref[0:1, 0:1]

    td = pl.BlockSpec((tm, D), lambda i: (i, 0))
    quad = pl.BlockSpec((4, tm, 128), lambda i: (0, i, 0))
    one = pl.BlockSpec((tm, 128), lambda i: (i, 0))
    return pl.pallas_call(
        body, name="grad_x", grid=(s // tm,),
        in_specs=[td, quad, one, one, quad, quad, quad, _resident((WIN, D)), _const((8, 128))],
        out_specs=td, out_shape=jax.ShapeDtypeStruct((s, D), F32),
        compiler_params=_cp(("parallel",)),
    )(dz1, *dparts, w_in_t, zero)


def _place():
    return lax.axis_index("x"), lax.axis_index("y"), lax.axis_index("c")


def _other_chips(x, y):
    return [(1 - x, y), (x, 1 - y), (1 - x, 1 - y)]


def _hbm(a):
    return pltpu.with_memory_space_constraint(a, pltpu.HBM)


def _gather_w_in(shard, conv_w):
    rows_k = shard.shape[0]
    half = rows_k // 2

    def body(src, conv_src, out, conv_out, send_sems, recv_sems):
        x, y, c = _place()
        b = 2 * x + y
        sibling = (x, y, 1 - c)
        chips = _other_chips(x, y)

        def copy(idx, chip_b, core, to, first_hop=False):
            rows = out.at[pl.ds(pl.multiple_of(chip_b * rows_k + core * half, 16), half)]
            s_ref = src.at[pl.ds(pl.multiple_of(core * half, 16), half)] if first_hop else rows
            return pltpu.make_async_remote_copy(src_ref=s_ref, dst_ref=rows, send_sem=send_sems.at[idx],
                                                recv_sem=recv_sems.at[idx], device_id=to, device_id_type=MESH)

        def own_copy():
            return pltpu.make_async_remote_copy(
                src_ref=src, dst_ref=out.at[pl.ds(pl.multiple_of(b * rows_k, 16), rows_k)], send_sem=send_sems.at[6],
                recv_sem=recv_sems.at[6], device_id=sibling, device_id_type=MESH)

        def conv_copy(idx, chip_b, to):
            return pltpu.make_async_remote_copy(src_ref=conv_src, dst_ref=conv_out.at[chip_b],
                                                send_sem=send_sems.at[7 + idx], recv_sem=recv_sems.at[7 + idx],
                                                device_id=to, device_id_type=MESH)

        started = [own_copy(), conv_copy(3, b, sibling)]
        for jn, chip in enumerate(chips):
            started += [copy(jn, b, c, (chip[0], chip[1], c), first_hop=True), conv_copy(jn, b, (chip[0], chip[1], c))]
        for cp in started:
            cp.start()
        for jn, chip in enumerate(chips):
            cb = 2 * chip[0] + chip[1]
            copy(jn, cb, c, (chip[0], chip[1], c)).wait_recv()
            cp = copy(3 + jn, cb, c, sibling)
            cp.start()
            started.append(cp)
        for jn, chip in enumerate(chips):
            cb = 2 * chip[0] + chip[1]
            copy(3 + jn, cb, 1 - c, sibling).wait_recv()
            conv_copy(jn, cb, (chip[0], chip[1], c)).wait_recv()
        own_copy().wait_recv()
        conv_copy(3, b, sibling).wait_recv()
        for cp in started:
            cp.wait_send()

    return pl.pallas_call(
        body, name="gather_w_in",
        in_specs=[ANY, ANY], out_specs=[ANY, ANY],
        out_shape=[jax.ShapeDtypeStruct((N_CHIPS * rows_k, D), BF16), jax.ShapeDtypeStruct((N_CHIPS,) + conv_w.shape, F32)],
        scratch_shapes=[pltpu.SemaphoreType.DMA((11,)), pltpu.SemaphoreType.DMA((11,))],
        compiler_params=pltpu.CompilerParams(has_side_effects=True),
    )(shard, conv_w)


def _weight_copies(shard, land, send_sems, recv_sems, arrivals):
    x, y, c = _place()
    n_rows, n_cols = shard.shape
    peers = [(px, py, c) for px, py in _other_chips(x, y)] + [(x, y, 1 - c)]
    cps = []
    for jn, peer in enumerate(peers):
        at = 2 * peer[0] + peer[1] if arrivals else 2 * x + y
        if land.shape[1] == n_cols:
            dst = land.at[pl.ds(pl.multiple_of(at * n_rows, 16), n_rows)]
        else:
            dst = land.at[:, pl.ds(pl.multiple_of(at * n_cols, 128), n_cols)]
        cps.append(pltpu.make_async_remote_copy(src_ref=shard, dst_ref=dst, send_sem=send_sems.at[jn],
                                                recv_sem=recv_sems.at[jn], device_id=peer, device_id_type=MESH))
    return cps


def _weights_start(shards, after):
    n = len(shards)
    lands = [lax.empty((N_CHIPS * sh.shape[0], D) if sh.shape[1] == D else (D, N_CHIPS * sh.shape[1]), BF16)
             for sh in shards]

    def body(*refs):
        src, land = refs[:n], refs[n:2 * n]
        send_sems, recv_sems = refs[2 * n + 1:3 * n + 1], refs[3 * n + 1:4 * n + 1]
        for k in range(n):
            for send in _weight_copies(src[k], land[k], send_sems[k], recv_sems[k], False):
                send.start()
        refs[-1][...] = jnp.zeros_like(refs[-1])

    res = pl.pallas_call(
        body, name="weights_start",
        in_specs=[HBM] * (2 * n) + [ANY], out_specs=[SEM] * (2 * n) + [HBM] * (2 * n) + [VMEM],
        out_shape=[pltpu.SemaphoreType.DMA((4,))] * (2 * n)
        + [pltpu.HBM(a.shape, a.dtype) for a in (*shards, *lands)] + [jax.ShapeDtypeStruct((8, 128), F32)],
        input_output_aliases={i: i + 2 * n for i in range(2 * n)},
        compiler_params=pltpu.CompilerParams(has_side_effects=DATAFLOW),
    )(*[_hbm(a) for a in (*shards, *lands)], after)
    return [(res[k], res[n + k], res[2 * n + k], res[3 * n + k]) for k in range(n)], res[-1]


def _weights_wait(started, after, name):
    send_sems, recv_sems, shard, land = started

    def body(s_ref, l_ref, send_ref, recv_ref, after_ref, s_out, l_out):
        for cp in _weight_copies(s_ref, l_ref, send_ref, recv_ref, True):
            cp.wait_send()
            cp.wait_recv()

    return pl.pallas_call(
        body, name=name,
        in_specs=[HBM, HBM, SEM, SEM, ANY], out_specs=[HBM, HBM],
        out_shape=[pltpu.HBM(shard.shape, shard.dtype), pltpu.HBM(land.shape, land.dtype)],
        input_output_aliases={0: 0, 1: 1},
        compiler_params=pltpu.CompilerParams(has_side_effects=DATAFLOW),
    )(shard, land, send_sems, recv_sems, after)[1]


def _grad_copies(g_ref, land_ref, send_sems, recv_sems):
    x, y, c = _place()
    cps = []
    for d in range(1, 8):
        px, py, pc = x ^ (d >> 2), y ^ ((d >> 1) & 1), c ^ (d & 1)
        cps.append(pltpu.make_async_remote_copy(
            src_ref=g_ref.at[2 * px + py, pc], dst_ref=land_ref.at[d - 1], send_sem=send_sems.at[d - 1],
            recv_sem=recv_sems.at[d - 1], device_id=(px, py, pc), device_id_type=MESH))
    return cps


def _grads_start(grads_b, name):
    n = len(grads_b)
    lands = [lax.empty((7, g.shape[2], D), BF16) for g in grads_b]

    def body(*refs):
        g, land = refs[:n], refs[n:2 * n]
        send_sems, recv_sems = refs[2 * n:3 * n], refs[3 * n:4 * n]
        for k in range(n):
            for cp in _grad_copies(g[k], land[k], send_sems[k], recv_sems[k]):
                cp.start()
        refs[-1][...] = jnp.zeros_like(refs[-1])

    res = pl.pallas_call(
        body, name=name,
        in_specs=[HBM] * (2 * n), out_specs=[SEM] * (2 * n) + [HBM] * (2 * n) + [VMEM],
        out_shape=[pltpu.SemaphoreType.DMA((7,))] * (2 * n)
        + [pltpu.HBM(a.shape, a.dtype) for a in (*grads_b, *lands)] + [jax.ShapeDtypeStruct((8, 128), F32)],
        input_output_aliases={i: i + 2 * n for i in range(2 * n)},
        compiler_params=pltpu.CompilerParams(has_side_effects=DATAFLOW),
    )(*[_hbm(a) for a in (*grads_b, *lands)])
    return [(res[k], res[n + k], res[2 * n + k], res[3 * n + k]) for k in range(n)], res[-1]


def _grads_wait(started, after, name):
    n = len(started)

    def body(*refs):
        g, land = refs[:n], refs[n:2 * n]
        send_sems, recv_sems = refs[2 * n:3 * n], refs[3 * n:4 * n]
        for k in range(n):
            for cp in _grad_copies(g[k], land[k], send_sems[k], recv_sems[k]):
                cp.wait_send()
                cp.wait_recv()

    gs = [st[2] for st in started]
    lands = [st[3] for st in started]
    res = pl.pallas_call(
        body, name=name,
        in_specs=[HBM] * (2 * n) + [SEM] * (2 * n) + [ANY], out_specs=[HBM] * (2 * n),
        out_shape=[pltpu.HBM(a.shape, a.dtype) for a in (*gs, *lands)],
        input_output_aliases={i: i for i in range(2 * n)},
        compiler_params=pltpu.CompilerParams(has_side_effects=DATAFLOW),
    )(*gs, *lands, *[st[0] for st in started], *[st[1] for st in started], after)
    return res[n:]


def _sum_partials(grad4, got, cb, name, tr):
    h = grad4.shape[2]
    per_half = h // tr

    def body(cb_ref, g_ref, o_ref, out_ref):
        acc = g_ref[...]
        for j in range(7):
            acc = acc + o_ref[j].astype(F32)
        out_ref[...] = acc

    return pl.pallas_call(
        body, name=name,
        grid_spec=pltpu.PrefetchScalarGridSpec(
            num_scalar_prefetch=1, grid=(per_half,),
            in_specs=[pl.BlockSpec((None, None, tr, D), lambda i, cb_ref: (cb_ref[1], cb_ref[0], i, 0)),
                      pl.BlockSpec((7, tr, D), lambda i, cb_ref: (0, i, 0))],
            out_specs=pl.BlockSpec((tr, D), lambda i, cb_ref: (cb_ref[0] * per_half + i, 0))),
        out_shape=pltpu.HBM((2 * h, D), F32),
        compiler_params=_cp(("arbitrary",)),
    )(cb, grad4, _hbm(got))


def _swap_halves(shards, name):
    n = len(shards)

    def body(*refs):
        out, send_sems, recv_sems = refs[n:2 * n], refs[2 * n], refs[2 * n + 1]
        x, y, c = _place()
        cps = []
        for k in range(n):
            h = shards[k].shape[0] // 2
            mine = out[k].at[pl.ds(pl.multiple_of(c * h, 8), h)]
            cp = pltpu.make_async_remote_copy(src_ref=mine, dst_ref=mine, send_sem=send_sems.at[k],
                                              recv_sem=recv_sems.at[k], device_id=(x, y, 1 - c), device_id_type=MESH)
            cp.start()
            cps.append(cp)
        for cp in cps:
            cp.wait()

    return pl.pallas_call(
        body, name=name,
        in_specs=[ANY] * n, out_specs=[ANY] * n,
        out_shape=[jax.ShapeDtypeStruct(sh.shape, F32) for sh in shards],
        input_output_aliases={k: k for k in range(n)},
        scratch_shapes=[pltpu.SemaphoreType.DMA((n,)), pltpu.SemaphoreType.DMA((n,))],
        compiler_params=pltpu.CompilerParams(has_side_effects=True),
    )(*shards)


def _small_copies(small_ref, land_ref, send_sems, recv_sems):
    x, y, c = _place()
    me = 4 * x + 2 * y + c
    cps = []
    for d in range(1, 8):
        px, py, pc = x ^ (d >> 2), y ^ ((d >> 1) & 1), c ^ (d & 1)
        cps.append(pltpu.make_async_remote_copy(
            src_ref=small_ref, dst_ref=land_ref.at[me], send_sem=send_sems.at[d - 1], recv_sem=recv_sems.at[d - 1],
            device_id=(px, py, pc), device_id_type=MESH))
    return cps


def _small_start(small):
    land = lax.empty((8,) + small.shape, F32)

    def body(s_ref, l_ref, send_sems, recv_sems, s_thru, l_thru, token):
        for cp in _small_copies(s_ref, l_ref, send_sems, recv_sems):
            cp.start()
        token[...] = jnp.zeros_like(token)

    res = pl.pallas_call(
        body, name="small_start",
        in_specs=[HBM, HBM], out_specs=[SEM, SEM, HBM, HBM, VMEM],
        out_shape=[pltpu.SemaphoreType.DMA((7,)), pltpu.SemaphoreType.DMA((7,)), pltpu.HBM(small.shape, F32),
                   pltpu.HBM(land.shape, F32), jax.ShapeDtypeStruct((8, 128), F32)],
        input_output_aliases={0: 2, 1: 3},
        compiler_params=pltpu.CompilerParams(has_side_effects=DATAFLOW),
    )(_hbm(small), _hbm(land))
    return res[:4], res[4]


def _small_wait(started, after):
    send_sems, recv_sems, small, land = started

    def body(s_ref, l_ref, send_ref, recv_ref, after_ref, s_out, l_out):
        for cp in _small_copies(s_ref, l_ref, send_ref, recv_ref):
            cp.wait_send()
            cp.wait_recv()

    return pl.pallas_call(
        body, name="small_wait",
        in_specs=[HBM, HBM, SEM, SEM, ANY], out_specs=[HBM, HBM],
        out_shape=[pltpu.HBM(small.shape, F32), pltpu.HBM(land.shape, F32)],
        input_output_aliases={0: 0, 1: 1},
        compiler_params=pltpu.CompilerParams(has_side_effects=DATAFLOW),
    )(small, land, send_sems, recv_sems, after)


def _small_sum(small, land, me):
    rows = small.shape[0]

    def body(me_ref, s_ref, l_ref, o_ref):
        acc = None
        for k in range(8):
            term = jnp.where(me_ref[0] == k, s_ref[...], l_ref[k])
            acc = term if k == 0 else acc + term
        o_ref[...] = acc

    return pl.pallas_call(
        body, name="small_sum",
        in_specs=[SMEM, VMEM, VMEM], out_specs=VMEM,
        out_shape=jax.ShapeDtypeStruct((rows, D), F32),
    )(me, small, land)


def _adamw(w, g, m, v, name, tr, g_transposed=False):
    rows, cols = w.shape

    def body(w_ref, g_ref, m_ref, v_ref, d_ref, nm_ref, nv_ref, *gt_ref):
        g_ = g_ref[...]
        if g_transposed:
            g_ = g_.T
            gt_ref[0][...] = g_
        nm = ADAM_B1 * m_ref[...] + (1.0 - ADAM_B1) * g_
        nv = ADAM_B2 * v_ref[...] + (1.0 - ADAM_B2) * (g_ * g_)
        m_hat = nm / (1.0 - ADAM_B1 ** ADAM_STEP)
        v_hat = nv / (1.0 - ADAM_B2 ** ADAM_STEP)
        d_ref[...] = -ADAM_LR * (m_hat / (jnp.sqrt(v_hat) + ADAM_EPS) + ADAM_WD * w_ref[...])
        nm_ref[...] = nm
        nv_ref[...] = nv

    spec = pl.BlockSpec((tr, cols), lambda i: (i, 0))
    g_spec = pl.BlockSpec((cols, tr), lambda i: (0, i)) if g_transposed else spec
    n_out = 4 if g_transposed else 3
    return pl.pallas_call(
        body, name=name, grid=(rows // tr,),
        in_specs=[spec, g_spec, spec, spec], out_specs=[spec] * n_out,
        out_shape=[jax.ShapeDtypeStruct((rows, cols), F32)] * n_out,
        compiler_params=_cp(("parallel",)),
    )(*[_hbm(a) for a in (w, g, m, v)])


def _local_step(x, target, w_in_t, late_weights, norm_a_g, norm_b_g, sinks_a, ln1_g, ln1_b,
                conv_w, conv_b, ln2_g, ln2_b, slopes, on_grad, on_small):
    cwb = jnp.concatenate([conv_w, conv_b[None]], axis=0).reshape(4, 2, FF)

    proj, xb = _proj(x, w_in_t, "proj")
    o_a, lse_a = _attn_a_fwd(proj, sinks_a)
    fwd_b = None
    for r in reversed(B_DILATIONS):
        fwd_b = _attn_b_fwd(proj, slopes, r, fwd_b)
    o_b, lse_b = fwd_b
    w_o = late_weights(1, lse_b)
    cat, z1, h1, h1b = _mix_ln1(x, o_a, o_b, norm_a_g, norm_b_g, w_o, ln1_g, ln1_b)
    w_up = late_weights(2, h1b)
    up = _up_proj(h1b, w_up)
    a, gate, a1 = _conv_gelu(up, cwb)
    w_down = late_weights(3, a)
    dz2, dz2b, st2 = _down_ln2_loss(a, w_down, h1, target, ln2_g, ln2_b)

    on_grad(3, *_grad_w(a, dz2b, "grad_w_down", tm=FF // 2))
    dup, dconv = _conv_gelu_bwd(_d_act(dz2b, w_down), up, gate, a1, cwb)
    on_grad(2, *_grad_w(dup, h1b, "grad_w_up", tm=FF // 2, lhs_halves=True))
    dz1, dz1b, st1 = _dh1_ln1_bwd(dz2, dup, w_up, z1, ln1_g)
    tok = on_grad(1, *_grad_w(cat, dz1b, "grad_w_o", tm=512))
    d_oa, d_ob, st_n = _dcat_rms_bwd(dz1b, w_o, o_a, o_b, norm_a_g + tok[0, 0], norm_b_g)
    dqa, dka, dva, dsink = _attn_a_bwd(proj, sinks_a, d_oa, o_a, lse_a)
    dconv = dconv.reshape(4, 2 * FF)
    tok = on_small(dict(loss=st2[2, 0:1], norm_a_g=st_n[0], norm_b_g=st_n[1], sinks_a=dsink[:, 0],
                        ln1_g=st1[0], ln1_b=st1[1], conv_w=dconv[0:3].reshape(-1), conv_b=dconv[3],
                        ln2_g=st2[0], ln2_b=st2[1]))
    slopes = slopes + tok[0, 0]
    bwd_b = None
    for r in reversed(B_DILATIONS):
        bwd_b = _attn_b_bwd(proj, slopes, d_ob, o_b, lse_b, r, bwd_b, BF16 if r == 1 else F32)
    dparts = tuple(_hbm(a) for a in (dqa, dka, dva, *bwd_b))
    tok = on_grad(0, *_grad_w_in(dparts, xb))
    return _grad_x(dz1, dparts, w_in_t, tok)


SMALL_ORDER = ("loss", "norm_a_g", "norm_b_g", "sinks_a", "ln1_g", "ln1_b", "conv_b", "ln2_g", "ln2_b", "conv_w")
SMALL_SIZES = dict(loss=1, norm_a_g=512, norm_b_g=512, sinks_a=8, ln1_g=D, ln1_b=D, conv_b=2 * FF, ln2_g=D, ln2_b=D,
                   conv_w=3 * 2 * FF)


def _pack(parts, rows):
    flat = jnp.concatenate([parts[k].reshape(-1).astype(F32) for k in parts])
    return jnp.pad(flat, (0, rows * D - flat.shape[0])).reshape(rows, D)


def _unpack(buf, names, sizes):
    flat = buf.reshape(-1)
    out, at = {}, 0
    for k in names:
        out[k] = flat[at:at + sizes[k]]
        at += sizes[k]
    return out


def kernel(x, w_in, norm_a_g, norm_b_g, sinks_a, w_o, ln1_g, ln1_b, w_up, conv_w, conv_b, w_down, ln2_g, ln2_b, loss_target, m_w_in, m_norm_a_g, m_norm_b_g, m_sinks_a, m_w_o, m_ln1_g, m_ln1_b, m_w_up, m_conv_w, m_conv_b, m_w_down, m_ln2_g, m_ln2_b, v_w_in, v_norm_a_g, v_norm_b_g, v_sinks_a, v_w_o, v_ln1_g, v_ln1_b, v_w_up, v_conv_w, v_conv_b, v_w_down, v_ln2_g, v_ln2_b):
    xi, yi, ci = _place()
    chip = (2 * xi + yi).astype(I32)
    core = ci.astype(I32)

    w_in_rows, m_w_in_rows, v_w_in_rows = w_in.T, m_w_in.T, v_w_in.T
    shards = (w_in_rows.astype(BF16), w_o.astype(BF16), w_up.astype(BF16), w_down.astype(BF16))
    w_in_t, conv_w4 = _gather_w_in(shards[0], conv_w)
    conv_w_f = conv_w4.transpose(1, 0, 2).reshape(3, 2 * FF)
    w_started, w_tok = _weights_start(shards[1:], conv_w4)
    slopes = jnp.asarray(SLOPES, F32) + w_tok[0, 0]

    halves_rows = [r // 2 for r in SHARD_ROWS]
    grads4, grads_b4, started = [None] * 4, [None] * 4, [None] * 4

    def on_grad(k, g, g_b):
        grads4[k] = g.reshape(N_CHIPS, 2, halves_rows[k], D)
        grads_b4[k] = g_b.reshape(N_CHIPS, 2, halves_rows[k], D)
        if k > 1:
            return None
        group = (1, 2, 3) if k == 1 else (0,)
        sts, tok = _grads_start([grads_b4[i] for i in group], f"grads_start_{k}")
        for i, st in zip(group, sts):
            started[i] = st
        return tok

    small_rows = 32
    small_started = []

    def on_small(parts):
        st, tok = _small_start(_pack({k: parts[k] for k in SMALL_ORDER}, small_rows))
        small_started.append(st)
        return tok

    gx = _local_step(
        x[0], loss_target[0], w_in_t, lambda k, after: _weights_wait(w_started[k - 1], after, f"weights_wait_{k}"),
        norm_a_g, norm_b_g, sinks_a, ln1_g, ln1_b, conv_w_f, conv_b, ln2_g, ln2_b, slopes, on_grad, on_small)

    tiles = (96, 128, 352, 176)
    core_chip = jnp.stack([core, chip])
    got = _grads_wait(started[1:], gx, "grads_wait_1")
    halves = [_sum_partials(grads4[k], got[k - 1], core_chip, f"sum_partials_{k}", tiles[k]) for k in (1, 2, 3)]
    g_w_o, g_w_up_rows, g_w_down = _swap_halves(halves, "swap_halves")
    delta, new_m, new_v = {}, {}, {}
    for k, g, tr in (("w_o", g_w_o, 128), ("w_down", g_w_down, 176)):
        delta[k], new_m[k], new_v[k] = _adamw(dict(w_o=w_o, w_down=w_down)[k], g, dict(w_o=m_w_o, w_down=m_w_down)[k],
                                              dict(w_o=v_w_o, w_down=v_w_down)[k], f"adamw_{k}", tr)
    delta["w_up"], new_m["w_up"], new_v["w_up"], g_w_up = _adamw(w_up, g_w_up_rows, m_w_up, v_w_up, "adamw_w_up", 256,
                                                                 g_transposed=True)

    got = _grads_wait(started[:1], delta["w_up"], "grads_wait_0")
    half_in = _sum_partials(grads4[0], got[0], core_chip, "sum_partials_0", tiles[0])
    (g_w_in_rows,) = _swap_halves([half_in], "swap_halves_in")
    small_mine, small_land = _small_wait(small_started[0], g_w_in_rows)
    totals = _small_sum(small_mine, small_land, (4 * xi + 2 * yi + ci).astype(I32).reshape(1))
    tot = _unpack(totals, SMALL_ORDER, SMALL_SIZES)
    loss = tot["loss"][0]
    cols = 2 * FF // N_CHIPS
    g_conv_w = lax.dynamic_slice(tot["conv_w"].reshape(3, 2 * FF), (0, chip * cols), (3, cols))
    g_small = dict(norm_a_g=tot["norm_a_g"], norm_b_g=tot["norm_b_g"], sinks_a=tot["sinks_a"], ln1_g=tot["ln1_g"],
                   ln1_b=tot["ln1_b"], conv_w=g_conv_w, conv_b=tot["conv_b"], ln2_g=tot["ln2_g"], ln2_b=tot["ln2_b"])

    weights = dict(w_in=w_in, norm_a_g=norm_a_g, norm_b_g=norm_b_g, sinks_a=sinks_a, w_o=w_o, ln1_g=ln1_g, ln1_b=ln1_b,
                   w_up=w_up, conv_w=conv_w, conv_b=conv_b, w_down=w_down, ln2_g=ln2_g, ln2_b=ln2_b)
    ms = dict(w_in=m_w_in, norm_a_g=m_norm_a_g, norm_b_g=m_norm_b_g, sinks_a=m_sinks_a, w_o=m_w_o, ln1_g=m_ln1_g,
              ln1_b=m_ln1_b, w_up=m_w_up, conv_w=m_conv_w, conv_b=m_conv_b, w_down=m_w_down, ln2_g=m_ln2_g, ln2_b=m_ln2_b)
    vs = dict(w_in=v_w_in, norm_a_g=v_norm_a_g, norm_b_g=v_norm_b_g, sinks_a=v_sinks_a, w_o=v_w_o, ln1_g=v_ln1_g,
              ln1_b=v_ln1_b, w_up=v_w_up, conv_w=v_conv_w, conv_b=v_conv_b, w_down=v_w_down, ln2_g=v_ln2_g, ln2_b=v_ln2_b)
    order = list(weights)
    grad = dict(g_small, w_in=g_w_in_rows.T, w_o=g_w_o, w_up=g_w_up, w_down=g_w_down)

    delta["w_in"], new_m["w_in"], new_v["w_in"] = [
        a.T for a in _adamw(w_in_rows, g_w_in_rows, m_w_in_rows, v_w_in_rows, "adamw_w_in", 144)]
    small_names = [k for k in order if k not in delta]
    sizes = {k: weights[k].size for k in small_names}
    rows = 16
    packed = [_pack({k: src[k] for k in small_names}, rows) for src in (weights, grad, ms, vs)]
    for res, buf in zip((delta, new_m, new_v), _adamw(*packed, "adamw_small", rows)):
        for k, val in _unpack(buf, small_names, sizes).items():
            res[k] = val.reshape(weights[k].shape)

    return (loss, gx[None], *[grad[k] for k in order], *[delta[k] for k in order],
            *[new_m[k] for k in order], *[new_v[k] for k in order])
```

```python
import functools
import math

import jax
import jax.numpy as jnp
from jax import lax
from jax.experimental import pallas as pl
from jax.experimental.pallas import tpu as pltpu

F32, BF16, I32 = jnp.float32, jnp.bfloat16, jnp.int32

D = 1024
FF = 2816
HD = 64
NH = 8
WA, WB = 768, 1536
WIN = WA + WB
BLK = 128
ALPHA = 2.0 ** 0.25
LN_EPS, RMS_EPS = 1e-5, 1e-6
SCALE = 1.0 / math.sqrt(HD)
A_MAX_DIST, B_MAX_DIST = 127, 128
B_DILATIONS = (1, 4, 16)
SLOPES = tuple(2.0 ** (-(i + 1)) for i in range(NH))
SHARD_ROWS = (WIN // 4, D // 4, 2 * FF // 4, FF // 4)
N_CHIPS = 4
ADAM_LR, ADAM_B1, ADAM_B2, ADAM_EPS, ADAM_WD, ADAM_STEP = 0.001, 0.9, 0.999, 1e-08, 0.01, 10
MESH = pl.DeviceIdType.MESH
ANY = pl.BlockSpec(memory_space=pl.ANY)
SMEM = pl.BlockSpec(memory_space=pltpu.SMEM)
VMEM = pl.BlockSpec(memory_space=pltpu.VMEM)
HBM = pl.BlockSpec(memory_space=pltpu.HBM)
SEM = pl.BlockSpec(memory_space=pltpu.SEMAPHORE)
DATAFLOW = pltpu.SideEffectType.DATAFLOW_SIDE_EFFECTING


def _cp(sem, mb=48):
    return pltpu.CompilerParams(dimension_semantics=sem, vmem_limit_bytes=mb << 20)


def _nn(a, b):
    return lax.dot_general(a, b, (((1,), (0,)), ((), ())), preferred_element_type=F32)


def _nt(a, b):
    return lax.dot_general(a, b, (((1,), (1,)), ((), ())), preferred_element_type=F32)


def _tn(a, b):
    return lax.dot_general(a, b, (((0,), (0,)), ((), ())), preferred_element_type=F32)


def _resident(shape):
    n = len(shape)
    return pl.BlockSpec(shape, lambda *_: (0,) * n, pipeline_mode=pl.Buffered(1))


def _const(shape):
    n = len(shape)
    return pl.BlockSpec(shape, lambda *_: (0,) * n)


def _proj(x, w_t, name, tm=512):
    s = x.shape[0]
    n = w_t.shape[0]

    def body(x_ref, w_ref, o_ref, xb_ref):
        xb = x_ref[...].astype(BF16)
        xb_ref[...] = xb
        res = _nt(xb, w_ref[...])
        for g in range(n // 128):
            o_ref[g] = res[:, 128 * g:128 * (g + 1)]

    return pl.pallas_call(
        body, name=name, grid=(s // tm,),
        in_specs=[pl.BlockSpec((tm, D), lambda i: (i, 0)), _resident((n, D))],
        out_specs=[pl.BlockSpec((n // 128, tm, 128), lambda i: (0, i, 0)), pl.BlockSpec((tm, D), lambda i: (i, 0))],
        out_shape=[jax.ShapeDtypeStruct((n // 128, s, 128), F32), jax.ShapeDtypeStruct((s, D), BF16)],
        compiler_params=_cp(("parallel",)),
    )(x, w_t)


def _grad_w(lhs, rhs, name, tm, tk=2048, lhs_halves=False):
    s = rhs.shape[0]
    if lhs_halves:
        per_half = lhs.shape[2] // tm
        n = 2 * lhs.shape[2]
        lhs_spec = pl.BlockSpec((None, tk, tm), lambda i, k: (i // per_half, k, i % per_half))
    else:
        n = lhs.shape[1]
        lhs_spec = pl.BlockSpec((tk, tm), lambda i, k: (k, i))
    nk = s // tk

    def body(l_ref, r_ref, o_ref, ob_ref):
        k = pl.program_id(1)

        @pl.when(k == 0)
        def _():
            o_ref[...] = jnp.zeros_like(o_ref)

        o_ref[...] += _tn(l_ref[...], r_ref[...])

        @pl.when(k == nk - 1)
        def _():
            ob_ref[...] = o_ref[...].astype(BF16)

    return pl.pallas_call(
        body, name=name, grid=(n // tm, nk),
        in_specs=[lhs_spec, pl.BlockSpec((tk, D), lambda i, k: (k, 0))],
        out_specs=[pl.BlockSpec((tm, D), lambda i, k: (i, 0))] * 2,
        out_shape=[pltpu.HBM((n, D), F32), pltpu.HBM((n, D), BF16)],
        compiler_params=_cp(("parallel", "arbitrary")),
    )(lhs, rhs)


def _band_base(max_dist, dist_unit, first):
    row = lax.broadcasted_iota(I32, (BLK, 2 * BLK), 0)
    col = lax.broadcasted_iota(I32, (BLK, 2 * BLK), 1)
    dist = BLK + row - col
    ok = (dist >= 0) & (dist <= max_dist)
    if first:
        ok = ok & (col >= BLK)
    return jnp.where(ok, dist.astype(F32) * (-float(dist_unit)), -jnp.inf)


def _half_mask(shape, e):
    lane = lax.broadcasted_iota(I32, shape, 1)
    return (lane < HD) if e == 0 else (lane >= HD)


def _to_half(x, e, g):
    if g != e:
        x = pltpu.roll(x, HD, 1)
    return jnp.where(_half_mask(x.shape, g), x, 0.0)


def _stack_heads(scalars, tile):
    return jnp.concatenate([scalars[0] * tile, scalars[1] * tile], axis=0)


def _pair_fwd(q2, kb, vb, base, slopes, kv_heads, sinks):
    lo = _half_mask((BLK, 2 * HD), 0)
    if slopes is None:
        bias = base
    elif sinks is None:
        bias = _stack_heads(slopes, base)
    else:
        col0 = lax.broadcasted_iota(I32, base.shape, 1) == 0
        bias = jnp.concatenate([jnp.where(col0, sinks[e], slopes[e] * base) for e in (0, 1)], axis=0)
    qs = jnp.concatenate([_to_half(q2, e, kv_heads[e]) * SCALE for e in (0, 1)], axis=0).astype(BF16)
    s = _nt(qs, kb) + bias
    m = jnp.max(s, axis=1, keepdims=True)
    p = jnp.exp(s - m)
    l = jnp.sum(p, axis=1, keepdims=True)
    o = _nn(p.astype(BF16), vb) / l
    lse = m + jnp.log(l)
    halves = []
    for e in (0, 1):
        oh = o[e * BLK:(e + 1) * BLK]
        halves.append(pltpu.roll(oh, HD, 1) if kv_heads[e] != e else oh)
    o2 = jnp.where(lo, halves[0], halves[1])
    lse2 = jnp.where(lo, jnp.broadcast_to(lse[:BLK], (BLK, 2 * HD)), jnp.broadcast_to(lse[BLK:], (BLK, 2 * HD)))
    return o2, lse2


def _pair_bwd(q2, kb, vb, do2, o2, lse2, base, slopes, kv_heads, sinks):
    lo = _half_mask((BLK, 2 * HD), 0)
    prod = do2 * o2
    lses, deltas = [], []
    for e in (0, 1):
        hq = _half_mask((BLK, 2 * HD), e)
        lses.append(jnp.max(jnp.where(hq, lse2, -jnp.inf), axis=1, keepdims=True))
        deltas.append(jnp.sum(jnp.where(hq, prod, 0.0), axis=1, keepdims=True))
    lse = jnp.concatenate(lses, axis=0)
    delta = jnp.concatenate(deltas, axis=0)
    qs = jnp.concatenate([_to_half(q2, e, kv_heads[e]) * SCALE for e in (0, 1)], axis=0).astype(BF16)
    dos = jnp.concatenate([_to_half(do2, e, kv_heads[e]) for e in (0, 1)], axis=0).astype(BF16)
    p = jnp.exp(_nt(qs, kb) + (base if slopes is None else _stack_heads(slopes, base)) - lse)
    ds = (p * (_nt(dos, vb) - delta)).astype(BF16)
    dq = _nn(ds, kb) * SCALE
    halves = []
    for e in (0, 1):
        dqh = dq[e * BLK:(e + 1) * BLK]
        halves.append(pltpu.roll(dqh, HD, 1) if kv_heads[e] != e else dqh)
    dq2 = jnp.where(lo, halves[0], halves[1])
    dk2 = _tn(ds, qs)
    dv2 = _tn(p.astype(BF16), dos)
    dsinks = []
    if sinks is not None:
        for e in (0, 1):
            dsinks.append(jnp.sum(-jnp.exp(sinks[e] - lses[e]) * deltas[e], axis=0, keepdims=True))
    return dq2, dk2, dv2, dsinks


A_BLOCKS_PER_STEP = 2
A_BLOCKS_PER_STEP_BWD = 1


def _attn_a_fwd(proj, sinks):
    s = proj.shape[1]
    nq = A_BLOCKS_PER_STEP
    rows = BLK * nq
    steps = s // rows

    def body(sink_ref, q_ref, kp_ref, kc_ref, vp_ref, vc_ref, o_ref, lse_ref):
        n = pl.program_id(0)
        base_rest = _band_base(A_MAX_DIST, 1, False)
        base_0 = jnp.where(n > 0, base_rest, _band_base(A_MAX_DIST, 1, True))
        for i in range(nq):
            cur = pl.ds(i * BLK, BLK)
            k_prev = kc_ref[pl.ds((i - 1) * BLK, BLK), :] if i > 0 else kp_ref[...]
            v_prev = vc_ref[pl.ds((i - 1) * BLK, BLK), :] if i > 0 else vp_ref[...]
            first_key = lax.broadcasted_iota(I32, (2 * BLK, 128), 0) == 0
            kb = jnp.where(first_key, 0.0, jnp.concatenate([k_prev, kc_ref[cur, :]], axis=0)).astype(BF16)
            vb = jnp.where(first_key, 0.0, jnp.concatenate([v_prev, vc_ref[cur, :]], axis=0)).astype(BF16)
            for j in range(NH // 2):
                g = j // 2
                o2, lse2 = _pair_fwd(q_ref[j, cur, :], kb, vb, base_rest if i > 0 else base_0,
                                     (SLOPES[2 * j], SLOPES[2 * j + 1]), (g, g), (sink_ref[2 * j], sink_ref[2 * j + 1]))
                o_ref[j, cur, :] = o2
                lse_ref[j, cur, :] = lse2

    before = lambda n: jnp.maximum(n * nq - 1, 0)
    slab = lambda g: pl.BlockSpec((None, rows, 128), lambda n: (g, n, 0))
    edge = lambda g: pl.BlockSpec((None, BLK, 128), lambda n: (g, before(n), 0))
    quad = pl.BlockSpec((4, rows, 128), lambda n: (0, n, 0))
    return pl.pallas_call(
        body, name="attn_a_fwd", grid=(steps,),
        in_specs=[SMEM, quad, edge(4), slab(4), edge(5), slab(5)],
        out_specs=[quad, quad],
        out_shape=[jax.ShapeDtypeStruct((4, s, 128), F32)] * 2,
        compiler_params=_cp(("parallel",)),
    )(sinks, proj, proj, proj, proj, proj)


def _attn_a_bwd(proj, sinks, d_o, o, lse):
    s = proj.shape[1]
    nq = A_BLOCKS_PER_STEP_BWD
    rows = BLK * nq
    steps = s // rows

    def body(sink_ref, q_ref, kp_ref, kc_ref, vp_ref, vc_ref, do_ref, o_ref, lse_ref,
             dq_ref, dk_ref, dv_ref, dsink_ref, kcar, vcar):
        n = pl.program_id(0)

        @pl.when(n == 0)
        def _():
            kcar[...] = jnp.zeros_like(kcar)
            vcar[...] = jnp.zeros_like(vcar)
            dsink_ref[...] = jnp.zeros_like(dsink_ref)

        dk_ref[...] = kcar[...].astype(BF16)
        dv_ref[...] = vcar[...].astype(BF16)

        @pl.when(n < steps)
        def _():
            base_rest = _band_base(A_MAX_DIST, 1, False)
            base_0 = jnp.where(n > 0, base_rest, _band_base(A_MAX_DIST, 1, True))
            for i in range(nq):
                cur = pl.ds(i * BLK, BLK)
                k_prev = kc_ref[pl.ds((i - 1) * BLK, BLK), :] if i > 0 else kp_ref[...]
                v_prev = vc_ref[pl.ds((i - 1) * BLK, BLK), :] if i > 0 else vp_ref[...]
                kb = jnp.concatenate([k_prev, kc_ref[cur, :]], axis=0).astype(BF16)
                vb = jnp.concatenate([v_prev, vc_ref[cur, :]], axis=0).astype(BF16)
                dk_win = dv_win = None
                for j in range(NH // 2):
                    g = j // 2
                    dq2, dk2, dv2, dsk = _pair_bwd(q_ref[j, cur, :], kb, vb, do_ref[j, cur, :], o_ref[j, cur, :],
                                                   lse_ref[j, cur, :], base_rest if i > 0 else base_0,
                                                   (SLOPES[2 * j], SLOPES[2 * j + 1]), (g, g),
                                                   (sink_ref[2 * j], sink_ref[2 * j + 1]))
                    dq_ref[j, cur, :] = dq2.astype(BF16)
                    dk_win = dk2 if j == 0 else dk_win + dk2
                    dv_win = dv2 if j == 0 else dv_win + dv2
                    for e in (0, 1):
                        h = 2 * j + e
                        dsink_ref[h:h + 1, :] += jnp.broadcast_to(dsk[e], (1, 128))
                if i == 0:
                    last = pl.ds((nq - 1) * BLK, BLK)
                    dk_ref[last, :] = (kcar[last, :] + dk_win[:BLK]).astype(BF16)
                    dv_ref[last, :] = (vcar[last, :] + dv_win[:BLK]).astype(BF16)
                else:
                    kcar[pl.ds((i - 1) * BLK, BLK), :] += dk_win[:BLK]
                    vcar[pl.ds((i - 1) * BLK, BLK), :] += dv_win[:BLK]
                kcar[cur, :] = dk_win[BLK:]
                vcar[cur, :] = dv_win[BLK:]

    cur_step = lambda n: jnp.minimum(n, steps - 1)
    before = lambda n: jnp.maximum(cur_step(n) * nq - 1, 0)
    out_prev = lambda n: jnp.maximum(n - 1, 0)
    quad = pl.BlockSpec((4, rows, 128), lambda n: (0, cur_step(n), 0))
    slab = lambda g: pl.BlockSpec((None, rows, 128), lambda n: (g, cur_step(n), 0))
    edge = lambda g: pl.BlockSpec((None, BLK, 128), lambda n: (g, before(n), 0))
    return pl.pallas_call(
        body, name="attn_a_bwd", grid=(steps + 1,),
        in_specs=[SMEM, quad, edge(4), slab(4), edge(5), slab(5), quad, quad, quad],
        out_specs=[quad,
                   pl.BlockSpec((rows, 128), lambda n: (out_prev(n), 0)),
                   pl.BlockSpec((rows, 128), lambda n: (out_prev(n), 0)),
                   pl.BlockSpec((NH, 128), lambda n: (0, 0))],
        out_shape=[pltpu.HBM((4, s, 128), BF16), pltpu.HBM((s, 128), BF16), pltpu.HBM((s, 128), BF16),
                   jax.ShapeDtypeStruct((NH, 128), F32)],
        scratch_shapes=[pltpu.VMEM((rows, 128), F32), pltpu.VMEM((rows, 128), F32)],
        compiler_params=_cp(("arbitrary",)),
    )(sinks, proj, proj, proj, proj, proj, d_o, o, lse)


def _stream(rho, i, r):
    start = i * BLK * r + rho
    return pl.ds(start, BLK, stride=r) if r > 1 else pl.ds(start, BLK)


def _for_streams(r, fn, side_by_side=4):
    if r <= side_by_side:
        for rho in range(r):
            fn(rho)
    else:
        def group(it, carry):
            for u in range(side_by_side):
                fn(side_by_side * it + u)
            return carry

        lax.fori_loop(0, r // side_by_side, group, 0)


B_BLOCKS_PER_STEP = {1: 8, 4: 2, 16: 1}
B_BLOCKS_PER_STEP_FWD = {1: 16, 4: 4, 16: 1}


def _attn_b_fwd(proj, slopes, r, so_far=None):
    s = proj.shape[1]
    nq = B_BLOCKS_PER_STEP_FWD[r]
    rows = BLK * r * nq
    steps = s // rows
    qc, kc, vc = WA // 128, WA // 128 + 4, WA // 128 + 8
    chained = so_far is not None

    def body(slope_ref, q_ref, kp_ref, kc_ref, vp_ref, vc_ref, *rest):
        po_ref, pl_ref = rest[:2] if chained else (None, None)
        o_ref, lse_ref = rest[-2:]
        j = pl.program_id(0)
        sb = pl.program_id(1)
        sl2 = (slope_ref[2 * j], slope_ref[2 * j + 1])
        bias_rest = _stack_heads(sl2, _band_base(B_MAX_DIST, r, False))
        bias_0 = jnp.where(sb > 0, bias_rest, _stack_heads(sl2, _band_base(B_MAX_DIST, r, True)))

        def stream(rho):
            for i in range(nq):
                cur = _stream(rho, i, r)
                k_prev = kc_ref[_stream(rho, i - 1, r), :] if i > 0 else kp_ref[_stream(rho, 0, r), :]
                v_prev = vc_ref[_stream(rho, i - 1, r), :] if i > 0 else vp_ref[_stream(rho, 0, r), :]
                kb = jnp.concatenate([k_prev, kc_ref[cur, :]], axis=0).astype(BF16)
                vb = jnp.concatenate([v_prev, vc_ref[cur, :]], axis=0).astype(BF16)
                o2, lse2 = _pair_fwd(q_ref[cur, :], kb, vb, bias_rest if i > 0 else bias_0, None, (0, 1), None)
                if chained:
                    lse1 = pl_ref[cur, :]
                    m = jnp.maximum(lse1, lse2)
                    e1, e2 = jnp.exp(lse1 - m), jnp.exp(lse2 - m)
                    den = e1 + e2
                    o2 = (e1 * po_ref[cur, :] + e2 * o2) * (1.0 / den)
                    lse2 = m + jnp.log(den)
                o_ref[cur, :] = o2
                lse_ref[cur, :] = lse2

        _for_streams(r, stream, side_by_side=16)

    before = lambda sb: jnp.maximum(sb * nq - 1, 0)
    result = pl.BlockSpec((None, rows, 128), lambda j, sb: (j, sb, 0))
    return pl.pallas_call(
        body, name=f"attn_b_fwd_r{r}", grid=(NH // 2, steps),
        in_specs=[SMEM,
                  pl.BlockSpec((None, rows, 128), lambda j, sb: (qc + j, sb, 0)),
                  pl.BlockSpec((None, BLK * r, 128), lambda j, sb: (kc + j, before(sb), 0)),
                  pl.BlockSpec((None, rows, 128), lambda j, sb: (kc + j, sb, 0)),
                  pl.BlockSpec((None, BLK * r, 128), lambda j, sb: (vc + j, before(sb), 0)),
                  pl.BlockSpec((None, rows, 128), lambda j, sb: (vc + j, sb, 0))] + ([result] * 2 if chained else []),
        out_specs=[result] * 2,
        out_shape=[jax.ShapeDtypeStruct((4, s, 128), F32)] * 2,
        compiler_params=_cp(("parallel", "parallel")),
    )(slopes, proj, proj, proj, proj, proj, *(so_far if chained else ()))


def _attn_b_bwd(proj, slopes, d_o, o, lse, r, so_far=None, dtype=F32):
    s = proj.shape[1]
    nq = B_BLOCKS_PER_STEP[r]
    rows = BLK * r * nq
    steps = s // rows
    qc, kc, vc = WA // 128, WA // 128 + 4, WA // 128 + 8
    chained = so_far is not None

    def body(slope_ref, q_ref, kp_ref, kc_ref, vp_ref, vc_ref, do_ref, o_ref, lse_ref, *rest):
        pq_ref, pk_ref, pv_ref = rest[:3] if chained else (None, None, None)
        dq_ref, dk_ref, dv_ref, kcar, vcar = rest[-5:]
        j = pl.program_id(0)
        sb = pl.program_id(1)

        @pl.when(sb == 0)
        def _():
            kcar[...] = jnp.zeros_like(kcar)
            vcar[...] = jnp.zeros_like(vcar)

        def settled(car, p_ref, idx):
            return car[idx] + p_ref[idx] if chained else car[idx]

        dk_ref[...] = settled(kcar, pk_ref, ...).astype(dtype)
        dv_ref[...] = settled(vcar, pv_ref, ...).astype(dtype)

        @pl.when(sb < steps)
        def _():
            sl2 = (slope_ref[2 * j], slope_ref[2 * j + 1])
            bias_rest = _stack_heads(sl2, _band_base(B_MAX_DIST, r, False))
            bias_0 = jnp.where(sb > 0, bias_rest, _stack_heads(sl2, _band_base(B_MAX_DIST, r, True)))

            def stream(rho):
                for i in range(nq):
                    cur = _stream(rho, i, r)
                    k_prev = kc_ref[_stream(rho, i - 1, r), :] if i > 0 else kp_ref[_stream(rho, 0, r), :]
                    v_prev = vc_ref[_stream(rho, i - 1, r), :] if i > 0 else vp_ref[_stream(rho, 0, r), :]
                    kb = jnp.concatenate([k_prev, kc_ref[cur, :]], axis=0).astype(BF16)
                    vb = jnp.concatenate([v_prev, vc_ref[cur, :]], axis=0).astype(BF16)
                    dq2, dk2, dv2, _ = _pair_bwd(q_ref[cur, :], kb, vb, do_ref[cur, :], o_ref[cur, :], lse_ref[cur, :],
                                                 bias_rest if i > 0 else bias_0, None, (0, 1), None)
                    dq_ref[cur, :] = (dq2 + pq_ref[cur, :] if chained else dq2).astype(dtype)
                    if i == 0:
                        last = (_stream(rho, nq - 1, r), slice(None))
                        dk_ref[last] = (settled(kcar, pk_ref, last) + dk2[:BLK]).astype(dtype)
                        dv_ref[last] = (settled(vcar, pv_ref, last) + dv2[:BLK]).astype(dtype)
                    else:
                        kcar[_stream(rho, i - 1, r), :] += dk2[:BLK]
                        vcar[_stream(rho, i - 1, r), :] += dv2[:BLK]
                    kcar[cur, :] = dk2[BLK:]
                    vcar[cur, :] = dv2[BLK:]

            _for_streams(r, stream, side_by_side=8)

    cur_step = lambda sb: jnp.minimum(sb, steps - 1)
    before = lambda sb: jnp.maximum(cur_step(sb) * nq - 1, 0)
    out_prev = lambda sb: jnp.maximum(sb - 1, 0)
    tile = lambda slab: pl.BlockSpec((None, rows, 128), lambda j, sb: (slab + j, cur_step(sb), 0))
    edge = lambda slab: pl.BlockSpec((None, BLK * r, 128), lambda j, sb: (slab + j, before(sb), 0))
    late = pl.BlockSpec((None, rows, 128), lambda j, sb: (j, out_prev(sb), 0))
    grads = [tile(0), late, late]
    return pl.pallas_call(
        body, name=f"attn_b_bwd_r{r}", grid=(NH // 2, steps + 1),
        in_specs=[SMEM, tile(qc), edge(kc), tile(kc), edge(vc), tile(vc), tile(0), tile(0), tile(0)]
        + (grads if chained else []),
        out_specs=grads,
        out_shape=[pltpu.HBM((4, s, 128), dtype)] * 3,
        scratch_shapes=[pltpu.VMEM((rows, 128), F32), pltpu.VMEM((rows, 128), F32)],
        compiler_params=_cp(("parallel", "arbitrary")),
    )(slopes, proj, proj, proj, proj, proj, d_o, o, lse, *(so_far if chained else ()))


def _row(v):
    return v.reshape(1, -1)


def _layer_norm_stats(z):
    mu = jnp.mean(z, axis=-1, keepdims=True)
    zc = z - mu
    var = jnp.mean(zc * zc, axis=-1, keepdims=True)
    rstd = lax.rsqrt(var + LN_EPS)
    return zc * rstd, rstd


def _layer_norm_bwd(dh, zh, rstd, g):
    dzh = dh * g
    return rstd * (dzh - jnp.mean(dzh, axis=-1, keepdims=True) - zh * jnp.mean(dzh * zh, axis=-1, keepdims=True))


def _rms(o):
    return lax.rsqrt(jnp.mean(o * o, axis=-1, keepdims=True) + RMS_EPS)


def _mix_ln1(x, o_a, o_b, norm_a_g, norm_b_g, w_o, ln1_g, ln1_b, tm=256):
    s = x.shape[0]

    def wide(ref):
        return jnp.concatenate([ref[j] for j in range(4)], axis=1)

    def body(x_ref, oa_ref, ob_ref, ga_ref, gb_ref, wo_ref, g_ref, b_ref, cat_ref, z1_ref, h1_ref, h1b_ref):
        oa, ob = wide(oa_ref), wide(ob_ref)
        na = oa * _rms(oa) * ga_ref[...]
        nb_ = ob * _rms(ob) * gb_ref[...]
        cat = jnp.concatenate([na, nb_], axis=1).astype(BF16)
        cat_ref[...] = cat
        z1 = ALPHA * x_ref[...] + _nn(cat, wo_ref[...])
        z1_ref[...] = z1
        zh, _ = _layer_norm_stats(z1)
        h1 = zh * g_ref[...] + b_ref[...]
        h1_ref[...] = h1
        h1b_ref[...] = h1.astype(BF16)

    t512 = pl.BlockSpec((4, tm, 128), lambda i: (0, i, 0))
    td = pl.BlockSpec((tm, D), lambda i: (i, 0))
    return pl.pallas_call(
        body, name="mix_ln1", grid=(s // tm,),
        in_specs=[td] + [t512] * 2 + [_const((1, 512))] * 2 + [_resident((D, D))] + [_const((1, D))] * 2,
        out_specs=[td, td, td, td],
        out_shape=[jax.ShapeDtypeStruct((s, D), BF16), jax.ShapeDtypeStruct((s, D), F32),
                   jax.ShapeDtypeStruct((s, D), F32), jax.ShapeDtypeStruct((s, D), BF16)],
        compiler_params=_cp(("parallel",)),
    )(x, o_a, o_b, _row(norm_a_g), _row(norm_b_g), w_o, _row(ln1_g), _row(ln1_b))


def _gelu_and_grad(x):
    c = math.sqrt(2.0 / math.pi)
    x2 = x * x
    s = 0.5 * jnp.tanh(x * ((c * 0.044715) * x2 + c)) + 0.5
    dg = s + (x * ((6.0 * c * 0.044715) * x2 + 2.0 * c)) * (s - s * s)
    return x * s, dg


def _shifted(u, edge, row, down):
    groups = [u[8 * i:8 * i + 8] for i in range(u.shape[0] // 8)]
    others = [edge] + groups[:-1] if down else groups[1:] + [edge]
    moved = []
    for k in (1, 2):
        crossing = row >= 8 - k if down else row < k
        moved.append(jnp.concatenate([pltpu.roll(jnp.where(crossing, o, g), k if down else 8 - k, 0)
                                      for o, g in zip(others, groups)], axis=0))
    return moved


def _up_proj(h1b, w_up, tm=512):
    s = h1b.shape[0]

    def body(h_ref, w_ref, o_ref):
        h = h_ref[...]
        for half in (0, 1):
            o_ref[half] = _nn(h, w_ref[:, half * FF:(half + 1) * FF]).astype(BF16)

    return pl.pallas_call(
        body, name="up_proj", grid=(s // tm,),
        in_specs=[pl.BlockSpec((tm, D), lambda i: (i, 0)), _resident((D, 2 * FF))],
        out_specs=pl.BlockSpec((2, tm, FF), lambda i: (0, i, 0)),
        out_shape=jax.ShapeDtypeStruct((2, s, FF), BF16),
        compiler_params=_cp(("parallel",)),
    )(h1b, w_up)


def _conv_gelu(up, cwb, tm=256, tn=FF // 2, chunk_rows=16):
    s = up.shape[1]
    n_c = tm // chunk_rows

    def body(up_ref, c_ref, a_ref, g_ref, a1_ref, carry):
        @pl.when(pl.program_id(1) == 0)
        def _():
            carry[...] = jnp.zeros_like(carry)

        row = lax.broadcasted_iota(jnp.int32, (8, tn), 0)
        edge = [carry[0], carry[1]]
        for c in range(n_c):
            rows = pl.ds(c * chunk_rows, chunk_rows)
            u = []
            for half in (0, 1):
                x = up_ref[half, rows, :].astype(F32)
                r1, r2 = _shifted(x, edge[half], row, True)
                u.append(r2 * c_ref[0, half:half + 1, :] + r1 * c_ref[1, half:half + 1, :]
                         + x * c_ref[2, half:half + 1, :] + c_ref[3, half:half + 1, :])
                edge[half] = x[chunk_rows - 8:]
            g, dg = _gelu_and_grad(u[0])
            a_ref[rows, :] = (g * u[1]).astype(BF16)
            g_ref[rows, :] = g.astype(BF16)
            a1_ref[rows, :] = (u[1] * dg).astype(BF16)
        for half in (0, 1):
            carry[half] = edge[half]

    pair = pl.BlockSpec((2, tm, tn), lambda j, i: (0, i, j))
    tile = pl.BlockSpec((tm, tn), lambda j, i: (i, j))
    return pl.pallas_call(
        body, name="conv_gelu", grid=(FF // tn, s // tm),
        in_specs=[pair, pl.BlockSpec((4, 2, tn), lambda j, i: (0, 0, j))],
        out_specs=[tile, tile, tile],
        out_shape=[jax.ShapeDtypeStruct((s, FF), BF16)] * 3,
        scratch_shapes=[pltpu.VMEM((2, 8, tn), F32)],
        compiler_params=_cp(("parallel", "arbitrary")),
    )(up, cwb)


def _down_ln2_loss(a, w_down, h1, target, ln2_g, ln2_b, tm=512):
    s = a.shape[0]

    def body(a_ref, w_ref, h_ref, t_ref, g_ref, b_ref, dz_ref, dzb_ref, st_ref):
        @pl.when(pl.program_id(0) == 0)
        def _():
            st_ref[...] = jnp.zeros_like(st_ref)

        z2 = ALPHA * h_ref[...] + _nn(a_ref[...], w_ref[...])
        zh, rstd = _layer_norm_stats(z2)
        diff = zh * g_ref[...] + b_ref[...] - t_ref[...]
        part = 0.5 * jnp.sum(jnp.mean(diff * diff, axis=-1, keepdims=True), axis=0, keepdims=True)
        dy = diff * (1.0 / D)
        st_ref[0:1, :] += jnp.sum(dy * zh, axis=0, keepdims=True)
        st_ref[1:2, :] += jnp.sum(dy, axis=0, keepdims=True)
        st_ref[2:3, :] += jnp.broadcast_to(part, (1, D))
        dz = _layer_norm_bwd(dy, zh, rstd, g_ref[...])
        dz_ref[...] = dz
        dzb_ref[...] = dz.astype(BF16)

    td = pl.BlockSpec((tm, D), lambda i: (i, 0))
    return pl.pallas_call(
        body, name="down_ln2_loss", grid=(s // tm,),
        in_specs=[pl.BlockSpec((tm, FF), lambda i: (i, 0)), _resident((FF, D)), td, td, _const((1, D)), _const((1, D))],
        out_specs=[td, td, _const((8, D))],
        out_shape=[jax.ShapeDtypeStruct((s, D), F32), jax.ShapeDtypeStruct((s, D), BF16),
                   jax.ShapeDtypeStruct((8, D), F32)],
        compiler_params=_cp(("arbitrary",)),
    )(a, w_down, h1, target, _row(ln2_g), _row(ln2_b))


def _d_act(dz2b, w_down, tm=512):
    s = dz2b.shape[0]

    def body(dz_ref, w_ref, o_ref):
        o_ref[...] = _nt(dz_ref[...], w_ref[...]).astype(BF16)

    return pl.pallas_call(
        body, name="d_act", grid=(s // tm,),
        in_specs=[pl.BlockSpec((tm, D), lambda i: (i, 0)), _resident((FF, D))],
        out_specs=pl.BlockSpec((tm, FF), lambda i: (i, 0)),
        out_shape=jax.ShapeDtypeStruct((s, FF), BF16),
        compiler_params=_cp(("parallel",)),
    )(dz2b, w_down)


def _conv_gelu_bwd(da, up, g, a1, cwb, tm=256, tn=FF // 2, chunk_rows=16):
    s = da.shape[0]
    n_i = s // tm
    n_c = tm // chunk_rows

    def body(da_ref, up_ref, g_ref, a1_ref, c_ref, dup_ref, dc_ref, carry):
        @pl.when(pl.program_id(1) == 0)
        def _():
            carry[...] = jnp.zeros_like(carry)
            dc_ref[...] = jnp.zeros_like(dc_ref)

        def fold(v):
            return jnp.sum(v.reshape(chunk_rows // 8, 8, v.shape[1]), axis=0)

        def chunk(cc, state):
            after, sums = state
            rows = pl.ds((n_c - 1 - cc) * chunk_rows, chunk_rows)
            da_c = da_ref[rows, :].astype(F32)
            dus = (da_c * a1_ref[rows, :].astype(F32), da_c * g_ref[rows, :].astype(F32))
            head, new_sums = [], []
            for half in (0, 1):
                du = dus[half]
                up = up_ref[half, rows, :].astype(F32)
                l1, l2 = _shifted(du, after[half], row, False)
                dup = (du * c_ref[2, half:half + 1, :] + l1 * c_ref[1, half:half + 1, :]
                       + l2 * c_ref[0, half:half + 1, :])
                dup_ref[half, rows, :] = dup.astype(BF16)
                parts = (fold(l2 * up), fold(l1 * up), fold(du * up), fold(du))
                new_sums.append(parts if sums is None else tuple(a + b for a, b in zip(sums[half], parts)))
                head.append(du[:8])
            return tuple(head), new_sums

        row = lax.broadcasted_iota(jnp.int32, (8, tn), 0)
        state = ((carry[0], carry[1]), None)
        for cc in range(n_c):
            state = chunk(cc, state)
        head, sums = state
        for half in (0, 1):
            carry[half] = head[half]
            for k in range(4):
                dc_ref[k, half:half + 1, :] += jnp.sum(sums[half][k], axis=0, keepdims=True)

    rev = lambda ii: n_i - 1 - ii
    tile = pl.BlockSpec((tm, tn), lambda j, ii: (rev(ii), j))
    pair = pl.BlockSpec((2, tm, tn), lambda j, ii: (0, rev(ii), j))
    per_col = pl.BlockSpec((4, 2, tn), lambda j, ii: (0, 0, j))
    return pl.pallas_call(
        body, name="conv_gelu_bwd", grid=(FF // tn, n_i),
        in_specs=[tile, pair, tile, tile, per_col],
        out_specs=[pair, per_col],
        out_shape=[jax.ShapeDtypeStruct((2, s, FF), BF16), jax.ShapeDtypeStruct((4, 2, FF), F32)],
        scratch_shapes=[pltpu.VMEM((2, 8, tn), F32)],
        compiler_params=_cp(("parallel", "arbitrary")),
    )(da, up, g, a1, cwb)


def _dh1_ln1_bwd(dz2, dup, w_up, z1, ln1_g, tm=512):
    s = dz2.shape[0]

    def body(dz2_ref, dup_ref, w_ref, z1_ref, g_ref, dz1_ref, dz1b_ref, st_ref):
        @pl.when(pl.program_id(0) == 0)
        def _():
            st_ref[...] = jnp.zeros_like(st_ref)

        dh = ALPHA * dz2_ref[...] + _nt(dup_ref[0], w_ref[:, :FF]) + _nt(dup_ref[1], w_ref[:, FF:])
        zh, rstd = _layer_norm_stats(z1_ref[...])
        st_ref[0:1, :] += jnp.sum(dh * zh, axis=0, keepdims=True)
        st_ref[1:2, :] += jnp.sum(dh, axis=0, keepdims=True)
        dz = _layer_norm_bwd(dh, zh, rstd, g_ref[...])
        dz1_ref[...] = dz
        dz1b_ref[...] = dz.astype(BF16)

    td = pl.BlockSpec((tm, D), lambda i: (i, 0))
    return pl.pallas_call(
        body, name="dh1_ln1_bwd", grid=(s // tm,),
        in_specs=[td, pl.BlockSpec((2, tm, FF), lambda i: (0, i, 0)), _resident((D, 2 * FF)), td, _const((1, D))],
        out_specs=[td, td, _const((8, D))],
        out_shape=[jax.ShapeDtypeStruct((s, D), F32), jax.ShapeDtypeStruct((s, D), BF16),
                   jax.ShapeDtypeStruct((8, D), F32)],
        compiler_params=_cp(("arbitrary",), 58),
    )(dz2, dup, w_up, z1, _row(ln1_g))


def _dcat_rms_bwd(dz1b, w_o, o_a, o_b, norm_a_g, norm_b_g, tm=512):
    s = dz1b.shape[0]

    def body(dz_ref, w_ref, oa_ref, ob_ref, ga_ref, gb_ref, da_ref, db_ref, st_ref):
        @pl.when(pl.program_id(0) == 0)
        def _():
            st_ref[...] = jnp.zeros_like(st_ref)

        dcat = _nt(dz_ref[...], w_ref[...])
        for k, (o_ref, g_ref, d_ref) in enumerate(((oa_ref, ga_ref, da_ref), (ob_ref, gb_ref, db_ref))):
            o = jnp.concatenate([o_ref[j] for j in range(4)], axis=1)
            dn = dcat[:, 512 * k:512 * (k + 1)]
            rr = _rms(o)
            oh = o * rr
            st_ref[k:k + 1, :] += jnp.sum(dn * oh, axis=0, keepdims=True)
            doh = dn * g_ref[...]
            d_o = rr * (doh - oh * jnp.mean(doh * oh, axis=-1, keepdims=True))
            for j in range(4):
                d_ref[j] = d_o[:, 128 * j:128 * (j + 1)]

    t512 = pl.BlockSpec((4, tm, 128), lambda i: (0, i, 0))
    return pl.pallas_call(
        body, name="dcat_rms_bwd", grid=(s // tm,),
        in_specs=[pl.BlockSpec((tm, D), lambda i: (i, 0)), _resident((D, D)), t512, t512,
                  _const((1, 512)), _const((1, 512))],
        out_specs=[t512, t512, _const((8, 512))],
        out_shape=[jax.ShapeDtypeStruct((4, s, 128), F32), jax.ShapeDtypeStruct((4, s, 128), F32),
                   jax.ShapeDtypeStruct((8, 512), F32)],
        compiler_params=_cp(("arbitrary",)),
    )(dz1b, w_o, o_a, o_b, _row(norm_a_g), _row(norm_b_g))


def _grad_w_in(dparts, xb, tk=2048):
    s = xb.shape[0]
    nk = s // tk

    def body(qa, ka, va, qb, kb, vb, x_ref, o_ref, ob_ref):
        i = pl.program_id(0)
        k = pl.program_id(1)

        @pl.when(k == 0)
        def _():
            o_ref[...] = jnp.zeros_like(o_ref)

        def add(blocks):
            o_ref[...] += _tn(jnp.concatenate(blocks, axis=1), x_ref[...])

        pl.when(i == 0)(lambda: add([qa[j] for j in range(4)] + [ka[...], va[...]]))
        pl.when(i == 1)(lambda: add([qb[j] for j in range(4)] + [kb[0], kb[1]]))
        pl.when(i == 2)(lambda: add([kb[0], kb[1]] + [vb[j] for j in range(4)]))

        @pl.when(k == nk - 1)
        def _():
            ob_ref[...] = o_ref[...].astype(BF16)

    def during(tile):
        return lambda i, k: jnp.where(i == tile, k, jnp.where(i < tile, 0, nk - 1))

    quad = lambda tile: pl.BlockSpec((4, tk, 128), lambda i, k: (0, during(tile)(i, k), 0))
    one = pl.BlockSpec((tk, 128), lambda i, k: (during(0)(i, k), 0))
    kb_spec = pl.BlockSpec((2, tk, 128), lambda i, k: (jnp.where(i == 2, 1, 0), jnp.where(i == 0, 0, k), 0))
    return pl.pallas_call(
        body, name="grad_w_in", grid=(3, nk),
        in_specs=[quad(0), one, one, quad(1), kb_spec, quad(2), pl.BlockSpec((tk, D), lambda i, k: (k, 0))],
        out_specs=[pl.BlockSpec((WA, D), lambda i, k: (i, 0))] * 2,
        out_shape=[pltpu.HBM((WIN, D), F32), pltpu.HBM((WIN, D), BF16)],
        compiler_params=_cp(("parallel", "arbitrary"), mb=56),
    )(*dparts, xb)


def _grad_x(dz1, dparts, w_in_t, zero, tm=512):
    s = dz1.shape[0]

    def body(dz_ref, qa, ka, va, qb, kb, vb, w_ref, z_ref, o_ref):
        dp = jnp.concatenate([qa[j] for j in range(4)] + [ka[...], va[...]]
                             + [ref[j] for ref in (qb, kb, vb) for j in range(4)], axis=1)
        o_ref[...] = ALPHA * dz_ref[...] + _nn(dp, w_ref[...]) + z_ref[0:1, 0:1]

    td = pl.BlockSpec((tm, D), lambda i: (i, 0))
    quad = pl.BlockSpec((4, tm, 128), lambda i: (0, i, 0))
    one = pl.BlockSpec((tm, 128), lambda i: (i, 0))
    return pl.pallas_call(
        body, name="grad_x", grid=(s // tm,),
        in_specs=[td, quad, one, one, quad, quad, quad, _resident((WIN, D)), _const((8, 128))],
        out_specs=td, out_shape=jax.ShapeDtypeStruct((s, D), F32),
        compiler_params=_cp(("parallel",)),
    )(dz1, *dparts, w_in_t, zero)


def _place():
    return lax.axis_index("x"), lax.axis_index("y"), lax.axis_index("c")


def _other_chips(x, y):
    return [(1 - x, y), (x, 1 - y), (1 - x, 1 - y)]


def _hbm(a):
    return pltpu.with_memory_space_constraint(a, pltpu.HBM)


def _gather_w_in(shard, conv_w):
    rows_k = shard.shape[0]
    half = rows_k // 2

    def body(src, conv_src, out, conv_out, send_sems, recv_sems):
        x, y, c = _place()
        b = 2 * x + y
        sibling = (x, y, 1 - c)
        chips = _other_chips(x, y)

        def copy(idx, chip_b, core, to, first_hop=False):
            rows = out.at[pl.ds(pl.multiple_of(chip_b * rows_k + core * half, 16), half)]
            s_ref = src.at[pl.ds(pl.multiple_of(core * half, 16), half)] if first_hop else rows
            return pltpu.make_async_remote_copy(src_ref=s_ref, dst_ref=rows, send_sem=send_sems.at[idx],
                                                recv_sem=recv_sems.at[idx], device_id=to, device_id_type=MESH)

        def own_copy():
            return pltpu.make_async_remote_copy(
                src_ref=src, dst_ref=out.at[pl.ds(pl.multiple_of(b * rows_k, 16), rows_k)], send_sem=send_sems.at[6],
                recv_sem=recv_sems.at[6], device_id=sibling, device_id_type=MESH)

        def conv_copy(idx, chip_b, to):
            return pltpu.make_async_remote_copy(src_ref=conv_src, dst_ref=conv_out.at[chip_b],
                                                send_sem=send_sems.at[7 + idx], recv_sem=recv_sems.at[7 + idx],
                                                device_id=to, device_id_type=MESH)

        started = [own_copy(), conv_copy(3, b, sibling)]
        for jn, chip in enumerate(chips):
            started += [copy(jn, b, c, (chip[0], chip[1], c), first_hop=True), conv_copy(jn, b, (chip[0], chip[1], c))]
        for cp in started:
            cp.start()
        for jn, chip in enumerate(chips):
            cb = 2 * chip[0] + chip[1]
            copy(jn, cb, c, (chip[0], chip[1], c)).wait_recv()
            cp = copy(3 + jn, cb, c, sibling)
            cp.start()
            started.append(cp)
        for jn, chip in enumerate(chips):
            cb = 2 * chip[0] + chip[1]
            copy(3 + jn, cb, 1 - c, sibling).wait_recv()
            conv_copy(jn, cb, (chip[0], chip[1], c)).wait_recv()
        own_copy().wait_recv()
        conv_copy(3, b, sibling).wait_recv()
        for cp in started:
            cp.wait_send()

    return pl.pallas_call(
        body, name="gather_w_in",
        in_specs=[ANY, ANY], out_specs=[ANY, ANY],
        out_shape=[jax.ShapeDtypeStruct((N_CHIPS * rows_k, D), BF16), jax.ShapeDtypeStruct((N_CHIPS,) + conv_w.shape, F32)],
        scratch_shapes=[pltpu.SemaphoreType.DMA((11,)), pltpu.SemaphoreType.DMA((11,))],
        compiler_params=pltpu.CompilerParams(has_side_effects=True),
    )(shard, conv_w)


def _weight_copies(shard, land, send_sems, recv_sems, arrivals):
    x, y, c = _place()
    n_rows, n_cols = shard.shape
    peers = [(px, py, c) for px, py in _other_chips(x, y)] + [(x, y, 1 - c)]
    cps = []
    for jn, peer in enumerate(peers):
        at = 2 * peer[0] + peer[1] if arrivals else 2 * x + y
        if land.shape[1] == n_cols:
            dst = land.at[pl.ds(pl.multiple_of(at * n_rows, 16), n_rows)]
        else:
            dst = land.at[:, pl.ds(pl.multiple_of(at * n_cols, 128), n_cols)]
        cps.append(pltpu.make_async_remote_copy(src_ref=shard, dst_ref=dst, send_sem=send_sems.at[jn],
                                                recv_sem=recv_sems.at[jn], device_id=peer, device_id_type=MESH))
    return cps


def _weights_start(shards, after):
    n = len(shards)
    lands = [lax.empty((N_CHIPS * sh.shape[0], D) if sh.shape[1] == D else (D, N_CHIPS * sh.shape[1]), BF16)
             for sh in shards]

    def body(*refs):
        src, land = refs[:n], refs[n:2 * n]
        send_sems, recv_sems = refs[2 * n + 1:3 * n + 1], refs[3 * n + 1:4 * n + 1]
        for k in range(n):
            for send in _weight_copies(src[k], land[k], send_sems[k], recv_sems[k], False):
                send.start()
        refs[-1][...] = jnp.zeros_like(refs[-1])

    res = pl.pallas_call(
        body, name="weights_start",
        in_specs=[HBM] * (2 * n) + [ANY], out_specs=[SEM] * (2 * n) + [HBM] * (2 * n) + [VMEM],
        out_shape=[pltpu.SemaphoreType.DMA((4,))] * (2 * n)
        + [pltpu.HBM(a.shape, a.dtype) for a in (*shards, *lands)] + [jax.ShapeDtypeStruct((8, 128), F32)],
        input_output_aliases={i: i + 2 * n for i in range(2 * n)},
        compiler_params=pltpu.CompilerParams(has_side_effects=DATAFLOW),
    )(*[_hbm(a) for a in (*shards, *lands)], after)
    return [(res[k], res[n + k], res[2 * n + k], res[3 * n + k]) for k in range(n)], res[-1]


def _weights_wait(started, after, name):
    send_sems, recv_sems, shard, land = started

    def body(s_ref, l_ref, send_ref, recv_ref, after_ref, s_out, l_out):
        for cp in _weight_copies(s_ref, l_ref, send_ref, recv_ref, True):
            cp.wait_send()
            cp.wait_recv()

    return pl.pallas_call(
        body, name=name,
        in_specs=[HBM, HBM, SEM, SEM, ANY], out_specs=[HBM, HBM],
        out_shape=[pltpu.HBM(shard.shape, shard.dtype), pltpu.HBM(land.shape, land.dtype)],
        input_output_aliases={0: 0, 1: 1},
        compiler_params=pltpu.CompilerParams(has_side_effects=DATAFLOW),
    )(shard, land, send_sems, recv_sems, after)[1]


def _grad_copies(g_ref, land_ref, send_sems, recv_sems):
    x, y, c = _place()
    cps = []
    for d in range(1, 8):
        px, py, pc = x ^ (d >> 2), y ^ ((d >> 1) & 1), c ^ (d & 1)
        cps.append(pltpu.make_async_remote_copy(
            src_ref=g_ref.at[2 * px + py, pc], dst_ref=land_ref.at[d - 1], send_sem=send_sems.at[d - 1],
            recv_sem=recv_sems.at[d - 1], device_id=(px, py, pc), device_id_type=MESH))
    return cps


def _grads_start(grads_b, name):
    n = len(grads_b)
    lands = [lax.empty((7, g.shape[2], D), BF16) for g in grads_b]

    def body(*refs):
        g, land = refs[:n], refs[n:2 * n]
        send_sems, recv_sems = refs[2 * n:3 * n], refs[3 * n:4 * n]
        for k in range(n):
            for cp in _grad_copies(g[k], land[k], send_sems[k], recv_sems[k]):
                cp.start()
        refs[-1][...] = jnp.zeros_like(refs[-1])

    res = pl.pallas_call(
        body, name=name,
        in_specs=[HBM] * (2 * n), out_specs=[SEM] * (2 * n) + [HBM] * (2 * n) + [VMEM],
        out_shape=[pltpu.SemaphoreType.DMA((7,))] * (2 * n)
        + [pltpu.HBM(a.shape, a.dtype) for a in (*grads_b, *lands)] + [jax.ShapeDtypeStruct((8, 128), F32)],
        input_output_aliases={i: i + 2 * n for i in range(2 * n)},
        compiler_params=pltpu.CompilerParams(has_side_effects=DATAFLOW),
    )(*[_hbm(a) for a in (*grads_b, *lands)])
    return [(res[k], res[n + k], res[2 * n + k], res[3 * n + k]) for k in range(n)], res[-1]


def _grads_wait(started, after, name):
    n = len(started)

    def body(*refs):
        g, land = refs[:n], refs[n:2 * n]
        send_sems, recv_sems = refs[2 * n:3 * n], refs[3 * n:4 * n]
        for k in range(n):
            for cp in _grad_copies(g[k], land[k], send_sems[k], recv_sems[k]):
                cp.wait_send()
                cp.wait_recv()

    gs = [st[2] for st in started]
    lands = [st[3] for st in started]
    res = pl.pallas_call(
        body, name=name,
        in_specs=[HBM] * (2 * n) + [SEM] * (2 * n) + [ANY], out_specs=[HBM] * (2 * n),
        out_shape=[pltpu.HBM(a.shape, a.dtype) for a in (*gs, *lands)],
        input_output_aliases={i: i for i in range(2 * n)},
        compiler_params=pltpu.CompilerParams(has_side_effects=DATAFLOW),
    )(*gs, *lands, *[st[0] for st in started], *[st[1] for st in started], after)
    return res[n:]


def _sum_partials(grad4, got, cb, name, tr):
    h = grad4.shape[2]
    per_half = h // tr

    def body(cb_ref, g_ref, o_ref, out_ref):
        acc = g_ref[...]
        for j in range(7):
            acc = acc + o_ref[j].astype(F32)
        out_ref[...] = acc

    return pl.pallas_call(
        body, name=name,
        grid_spec=pltpu.PrefetchScalarGridSpec(
            num_scalar_prefetch=1, grid=(per_half,),
            in_specs=[pl.BlockSpec((None, None, tr, D), lambda i, cb_ref: (cb_ref[1], cb_ref[0], i, 0)),
                      pl.BlockSpec((7, tr, D), lambda i, cb_ref: (0, i, 0))],
            out_specs=pl.BlockSpec((tr, D), lambda i, cb_ref: (cb_ref[0] * per_half + i, 0))),
        out_shape=pltpu.HBM((2 * h, D), F32),
        compiler_params=_cp(("arbitrary",)),
    )(cb, grad4, _hbm(got))


def _swap_halves(shards, name):
    n = len(shards)

    def body(*refs):
        out, send_sems, recv_sems = refs[n:2 * n], refs[2 * n], refs[2 * n + 1]
        x, y, c = _place()
        cps = []
        for k in range(n):
            h = shards[k].shape[0] // 2
            mine = out[k].at[pl.ds(pl.multiple_of(c * h, 8), h)]
            cp = pltpu.make_async_remote_copy(src_ref=mine, dst_ref=mine, send_sem=send_sems.at[k],
                                              recv_sem=recv_sems.at[k], device_id=(x, y, 1 - c), device_id_type=MESH)
            cp.start()
            cps.append(cp)
        for cp in cps:
            cp.wait()

    return pl.pallas_call(
        body, name=name,
        in_specs=[ANY] * n, out_specs=[ANY] * n,
        out_shape=[jax.ShapeDtypeStruct(sh.shape, F32) for sh in shards],
        input_output_aliases={k: k for k in range(n)},
        scratch_shapes=[pltpu.SemaphoreType.DMA((n,)), pltpu.SemaphoreType.DMA((n,))],
        compiler_params=pltpu.CompilerParams(has_side_effects=True),
    )(*shards)


def _small_copies(small_ref, land_ref, send_sems, recv_sems):
    x, y, c = _place()
    me = 4 * x + 2 * y + c
    cps = []
    for d in range(1, 8):
        px, py, pc = x ^ (d >> 2), y ^ ((d >> 1) & 1), c ^ (d & 1)
        cps.append(pltpu.make_async_remote_copy(
            src_ref=small_ref, dst_ref=land_ref.at[me], send_sem=send_sems.at[d - 1], recv_sem=recv_sems.at[d - 1],
            device_id=(px, py, pc), device_id_type=MESH))
    return cps


def _small_start(small):
    land = lax.empty((8,) + small.shape, F32)

    def body(s_ref, l_ref, send_sems, recv_sems, s_thru, l_thru, token):
        for cp in _small_copies(s_ref, l_ref, send_sems, recv_sems):
            cp.start()
        token[...] = jnp.zeros_like(token)

    res = pl.pallas_call(
        body, name="small_start",
        in_specs=[HBM, HBM], out_specs=[SEM, SEM, HBM, HBM, VMEM],
        out_shape=[pltpu.SemaphoreType.DMA((7,)), pltpu.SemaphoreType.DMA((7,)), pltpu.HBM(small.shape, F32),
                   pltpu.HBM(land.shape, F32), jax.ShapeDtypeStruct((8, 128), F32)],
        input_output_aliases={0: 2, 1: 3},
        compiler_params=pltpu.CompilerParams(has_side_effects=DATAFLOW),
    )(_hbm(small), _hbm(land))
    return res[:4], res[4]


def _small_wait(started, after):
    send_sems, recv_sems, small, land = started

    def body(s_ref, l_ref, send_ref, recv_ref, after_ref, s_out, l_out):
        for cp in _small_copies(s_ref, l_ref, send_ref, recv_ref):
            cp.wait_send()
            cp.wait_recv()

    return pl.pallas_call(
        body, name="small_wait",
        in_specs=[HBM, HBM, SEM, SEM, ANY], out_specs=[HBM, HBM],
        out_shape=[pltpu.HBM(small.shape, F32), pltpu.HBM(land.shape, F32)],
        input_output_aliases={0: 0, 1: 1},
        compiler_params=pltpu.CompilerParams(has_side_effects=DATAFLOW),
    )(small, land, send_sems, recv_sems, after)


def _small_sum(small, land, me):
    rows = small.shape[0]

    def body(me_ref, s_ref, l_ref, o_ref):
        acc = None
        for k in range(8):
            term = jnp.where(me_ref[0] == k, s_ref[...], l_ref[k])
            acc = term if k == 0 else acc + term
        o_ref[...] = acc

    return pl.pallas_call(
        body, name="small_sum",
        in_specs=[SMEM, VMEM, VMEM], out_specs=VMEM,
        out_shape=jax.ShapeDtypeStruct((rows, D), F32),
    )(me, small, land)


def _adamw(w, g, m, v, name, tr, g_transposed=False):
    rows, cols = w.shape

    def body(w_ref, g_ref, m_ref, v_ref, d_ref, nm_ref, nv_ref, *gt_ref):
        g_ = g_ref[...]
        if g_transposed:
            g_ = g_.T
            gt_ref[0][...] = g_
        nm = ADAM_B1 * m_ref[...] + (1.0 - ADAM_B1) * g_
        nv = ADAM_B2 * v_ref[...] + (1.0 - ADAM_B2) * (g_ * g_)
        m_hat = nm / (1.0 - ADAM_B1 ** ADAM_STEP)
        v_hat = nv / (1.0 - ADAM_B2 ** ADAM_STEP)
        d_ref[...] = -ADAM_LR * (m_hat / (jnp.sqrt(v_hat) + ADAM_EPS) + ADAM_WD * w_ref[...])
        nm_ref[...] = nm
        nv_ref[...] = nv

    spec = pl.BlockSpec((tr, cols), lambda i: (i, 0))
    g_spec = pl.BlockSpec((cols, tr), lambda i: (0, i)) if g_transposed else spec
    n_out = 4 if g_transposed else 3
    return pl.pallas_call(
        body, name=name, grid=(rows // tr,),
        in_specs=[spec, g_spec, spec, spec], out_specs=[spec] * n_out,
        out_shape=[jax.ShapeDtypeStruct((rows, cols), F32)] * n_out,
        compiler_params=_cp(("parallel",)),
    )(*[_hbm(a) for a in (w, g, m, v)])


def _local_step(x, target, w_in_t, late_weights, norm_a_g, norm_b_g, sinks_a, ln1_g, ln1_b,
                conv_w, conv_b, ln2_g, ln2_b, slopes, on_grad, on_small):
    cwb = jnp.concatenate([conv_w, conv_b[None]], axis=0).reshape(4, 2, FF)

    proj, xb = _proj(x, w_in_t, "proj")
    o_a, lse_a = _attn_a_fwd(proj, sinks_a)
    fwd_b = None
    for r in reversed(B_DILATIONS):
        fwd_b = _attn_b_fwd(proj, slopes, r, fwd_b)
    o_b, lse_b = fwd_b
    w_o = late_weights(1, lse_b)
    cat, z1, h1, h1b = _mix_ln1(x, o_a, o_b, norm_a_g, norm_b_g, w_o, ln1_g, ln1_b)
    w_up = late_weights(2, h1b)
    up = _up_proj(h1b, w_up)
    a, gate, a1 = _conv_gelu(up, cwb)
    w_down = late_weights(3, a)
    dz2, dz2b, st2 = _down_ln2_loss(a, w_down, h1, target, ln2_g, ln2_b)

    on_grad(3, *_grad_w(a, dz2b, "grad_w_down", tm=FF // 2))
    dup, dconv = _conv_gelu_bwd(_d_act(dz2b, w_down), up, gate, a1, cwb)
    on_grad(2, *_grad_w(dup, h1b, "grad_w_up", tm=FF // 2, lhs_halves=True))
    dz1, dz1b, st1 = _dh1_ln1_bwd(dz2, dup, w_up, z1, ln1_g)
    tok = on_grad(1, *_grad_w(cat, dz1b, "grad_w_o", tm=512))
    d_oa, d_ob, st_n = _dcat_rms_bwd(dz1b, w_o, o_a, o_b, norm_a_g + tok[0, 0], norm_b_g)
    dqa, dka, dva, dsink = _attn_a_bwd(proj, sinks_a, d_oa, o_a, lse_a)
    dconv = dconv.reshape(4, 2 * FF)
    tok = on_small(dict(loss=st2[2, 0:1], norm_a_g=st_n[0], norm_b_g=st_n[1], sinks_a=dsink[:, 0],
                        ln1_g=st1[0], ln1_b=st1[1], conv_w=dconv[0:3].reshape(-1), conv_b=dconv[3],
                        ln2_g=st2[0], ln2_b=st2[1]))
    slopes = slopes + tok[0, 0]
    bwd_b = None
    for r in reversed(B_DILATIONS):
        bwd_b = _attn_b_bwd(proj, slopes, d_ob, o_b, lse_b, r, bwd_b, BF16 if r == 1 else F32)
    dparts = tuple(_hbm(a) for a in (dqa, dka, dva, *bwd_b))
    tok = on_grad(0, *_grad_w_in(dparts, xb))
    return _grad_x(dz1, dparts, w_in_t, tok)


SMALL_ORDER = ("loss", "norm_a_g", "norm_b_g", "sinks_a", "ln1_g", "ln1_b", "conv_b", "ln2_g", "ln2_b", "conv_w")
SMALL_SIZES = dict(loss=1, norm_a_g=512, norm_b_g=512, sinks_a=8, ln1_g=D, ln1_b=D, conv_b=2 * FF, ln2_g=D, ln2_b=D,
                   conv_w=3 * 2 * FF)


def _pack(parts, rows):
    flat = jnp.concatenate([parts[k].reshape(-1).astype(F32) for k in parts])
    return jnp.pad(flat, (0, rows * D - flat.shape[0])).reshape(rows, D)


def _unpack(buf, names, sizes):
    flat = buf.reshape(-1)
    out, at = {}, 0
    for k in names:
        out[k] = flat[at:at + sizes[k]]
        at += sizes[k]
    return out


def kernel(x, w_in, norm_a_g, norm_b_g, sinks_a, w_o, ln1_g, ln1_b, w_up, conv_w, conv_b, w_down, ln2_g, ln2_b, loss_target, m_w_in, m_norm_a_g, m_norm_b_g, m_sinks_a, m_w_o, m_ln1_g, m_ln1_b, m_w_up, m_conv_w, m_conv_b, m_w_down, m_ln2_g, m_ln2_b, v_w_in, v_norm_a_g, v_norm_b_g, v_sinks_a, v_w_o, v_ln1_g, v_ln1_b, v_w_up, v_conv_w, v_conv_b, v_w_down, v_ln2_g, v_ln2_b):
    xi, yi, ci = _place()
    chip = (2 * xi + yi).astype(I32)
    core = ci.astype(I32)

    shards = (w_in.T.astype(BF16), w_o.astype(BF16), w_up.astype(BF16), w_down.astype(BF16))
    w_in_t, conv_w4 = _gather_w_in(shards[0], conv_w)
    conv_w_f = conv_w4.transpose(1, 0, 2).reshape(3, 2 * FF)
    w_started, w_tok = _weights_start(shards[1:], conv_w4)
    slopes = jnp.asarray(SLOPES, F32) + w_tok[0, 0]

    halves_rows = [r // 2 for r in SHARD_ROWS]
    grads4, grads_b4, started = [None] * 4, [None] * 4, [None] * 4

    def on_grad(k, g, g_b):
        grads4[k] = g.reshape(N_CHIPS, 2, halves_rows[k], D)
        grads_b4[k] = g_b.reshape(N_CHIPS, 2, halves_rows[k], D)
        if k > 1:
            return None
        group = (1, 2, 3) if k == 1 else (0,)
        sts, tok = _grads_start([grads_b4[i] for i in group], f"grads_start_{k}")
        for i, st in zip(group, sts):
            started[i] = st
        return tok

    small_rows = 32
    small_started = []

    def on_small(parts):
        st, tok = _small_start(_pack({k: parts[k] for k in SMALL_ORDER}, small_rows))
        small_started.append(st)
        return tok

    gx = _local_step(
        x[0], loss_target[0], w_in_t, lambda k, after: _weights_wait(w_started[k - 1], after, f"weights_wait_{k}"),
        norm_a_g, norm_b_g, sinks_a, ln1_g, ln1_b, conv_w_f, conv_b, ln2_g, ln2_b, slopes, on_grad, on_small)

    tiles = (96, 128, 352, 176)
    core_chip = jnp.stack([core, chip])
    got = _grads_wait(started[1:], gx, "grads_wait_1")
    halves = [_sum_partials(grads4[k], got[k - 1], core_chip, f"sum_partials_{k}", tiles[k]) for k in (1, 2, 3)]
    g_w_o, g_w_up_rows, g_w_down = _swap_halves(halves, "swap_halves")
    delta, new_m, new_v = {}, {}, {}
    for k, g, tr in (("w_o", g_w_o, 128), ("w_down", g_w_down, 176)):
        delta[k], new_m[k], new_v[k] = _adamw(dict(w_o=w_o, w_down=w_down)[k], g, dict(w_o=m_w_o, w_down=m_w_down)[k],
                                              dict(w_o=v_w_o, w_down=v_w_down)[k], f"adamw_{k}", tr)
    delta["w_up"], new_m["w_up"], new_v["w_up"], g_w_up = _adamw(w_up, g_w_up_rows, m_w_up, v_w_up, "adamw_w_up", 256,
                                                                 g_transposed=True)

    got = _grads_wait(started[:1], delta["w_up"], "grads_wait_0")
    half_in = _sum_partials(grads4[0], got[0], core_chip, "sum_partials_0", tiles[0])
    (g_w_in_rows,) = _swap_halves([half_in], "swap_halves_in")
    small_mine, small_land = _small_wait(small_started[0], g_w_in_rows)
    totals = _small_sum(small_mine, small_land, (4 * xi + 2 * yi + ci).astype(I32).reshape(1))
    tot = _unpack(totals, SMALL_ORDER, SMALL_SIZES)
    loss = tot["loss"][0]
    cols = 2 * FF // N_CHIPS
    g_conv_w = lax.dynamic_slice(tot["conv_w"].reshape(3, 2 * FF), (0, chip * cols), (3, cols))
    g_small = dict(norm_a_g=tot["norm_a_g"], norm_b_g=tot["norm_b_g"], sinks_a=tot["sinks_a"], ln1_g=tot["ln1_g"],
                   ln1_b=tot["ln1_b"], conv_w=g_conv_w, conv_b=tot["conv_b"], ln2_g=tot["ln2_g"], ln2_b=tot["ln2_b"])

    weights = dict(w_in=w_in, norm_a_g=norm_a_g, norm_b_g=norm_b_g, sinks_a=sinks_a, w_o=w_o, ln1_g=ln1_g, ln1_b=ln1_b,
                   w_up=w_up, conv_w=conv_w, conv_b=conv_b, w_down=w_down, ln2_g=ln2_g, ln2_b=ln2_b)
    ms = dict(w_in=m_w_in, norm_a_g=m_norm_a_g, norm_b_g=m_norm_b_g, sinks_a=m_sinks_a, w_o=m_w_o, ln1_g=m_ln1_g,
              ln1_b=m_ln1_b, w_up=m_w_up, conv_w=m_conv_w, conv_b=m_conv_b, w_down=m_w_down, ln2_g=m_ln2_g, ln2_b=m_ln2_b)
    vs = dict(w_in=v_w_in, norm_a_g=v_norm_a_g, norm_b_g=v_norm_b_g, sinks_a=v_sinks_a, w_o=v_w_o, ln1_g=v_ln1_g,
              ln1_b=v_ln1_b, w_up=v_w_up, conv_w=v_conv_w, conv_b=v_conv_b, w_down=v_w_down, ln2_g=v_ln2_g, ln2_b=v_ln2_b)
    order = list(weights)
    delta["w_in"], new_m["w_in"], new_v["w_in"], g_w_in = _adamw(w_in, g_w_in_rows, m_w_in, v_w_in, "adamw_w_in", 256,
                                                                 g_transposed=True)
    grad = dict(g_small, w_in=g_w_in, w_o=g_w_o, w_up=g_w_up, w_down=g_w_down)
    small_names = [k for k in order if k not in delta]
    sizes = {k: weights[k].size for k in small_names}
    rows = 16
    packed = [_pack({k: src[k] for k in small_names}, rows) for src in (weights, grad, ms, vs)]
    for res, buf in zip((delta, new_m, new_v), _adamw(*packed, "adamw_small", rows)):
        for k, val in _unpack(buf, small_names, sizes).items():
            res[k] = val.reshape(weights[k].shape)

    return (loss, gx[None], *[grad[k] for k in order], *[delta[k] for k in order],
            *[new_m[k] for k in order], *[new_v[k] for k in order])
```

```python
import functools
import math

import jax
import jax.numpy as jnp
from jax import lax
from jax.experimental import pallas as pl
from jax.experimental.pallas import tpu as pltpu

F32, BF16, I32 = jnp.float32, jnp.bfloat16, jnp.int32

D = 1024
FF = 2816
HD = 64
NH = 8
WA, WB = 768, 1536
WIN = WA + WB
BLK = 128
ALPHA = 2.0 ** 0.25
LN_EPS, RMS_EPS = 1e-5, 1e-6
SCALE = 1.0 / math.sqrt(HD)
A_MAX_DIST, B_MAX_DIST = 127, 128
B_DILATIONS = (1, 4, 16)
SLOPES = tuple(2.0 ** (-(i + 1)) for i in range(NH))
SHARD_ROWS = (WIN // 4, D // 4, 2 * FF // 4, FF // 4)
N_CHIPS = 4
ADAM_LR, ADAM_B1, ADAM_B2, ADAM_EPS, ADAM_WD, ADAM_STEP = 0.001, 0.9, 0.999, 1e-08, 0.01, 10
MESH = pl.DeviceIdType.MESH
ANY = pl.BlockSpec(memory_space=pl.ANY)
SMEM = pl.BlockSpec(memory_space=pltpu.SMEM)
VMEM = pl.BlockSpec(memory_space=pltpu.VMEM)
HBM = pl.BlockSpec(memory_space=pltpu.HBM)
SEM = pl.BlockSpec(memory_space=pltpu.SEMAPHORE)
DATAFLOW = pltpu.SideEffectType.DATAFLOW_SIDE_EFFECTING


def _cp(sem, mb=48):
    return pltpu.CompilerParams(dimension_semantics=sem, vmem_limit_bytes=mb << 20)


def _nn(a, b):
    return lax.dot_general(a, b, (((1,), (0,)), ((), ())), preferred_element_type=F32)


def _nt(a, b):
    return lax.dot_general(a, b, (((1,), (1,)), ((), ())), preferred_element_type=F32)


def _tn(a, b):
    return lax.dot_general(a, b, (((0,), (0,)), ((), ())), preferred_element_type=F32)


def _resident(shape):
    n = len(shape)
    return pl.BlockSpec(shape, lambda *_: (0,) * n, pipeline_mode=pl.Buffered(1))


def _const(shape):
    n = len(shape)
    return pl.BlockSpec(shape, lambda *_: (0,) * n)


def _proj(x, w_t, name, tm=512):
    s = x.shape[0]
    n = w_t.shape[0]

    def body(x_ref, w_ref, o_ref, xb_ref):
        xb = x_ref[...].astype(BF16)
        xb_ref[...] = xb
        res = _nt(xb, w_ref[...])
        for g in range(n // 128):
            o_ref[g] = res[:, 128 * g:128 * (g + 1)]

    return pl.pallas_call(
        body, name=name, grid=(s // tm,),
        in_specs=[pl.BlockSpec((tm, D), lambda i: (i, 0)), _resident((n, D))],
        out_specs=[pl.BlockSpec((n // 128, tm, 128), lambda i: (0, i, 0)), pl.BlockSpec((tm, D), lambda i: (i, 0))],
        out_shape=[jax.ShapeDtypeStruct((n // 128, s, 128), F32), jax.ShapeDtypeStruct((s, D), BF16)],
        compiler_params=_cp(("parallel",)),
    )(x, w_t)


def _grad_w(lhs, rhs, name, tm, tk=2048, lhs_halves=False):
    s = rhs.shape[0]
    if lhs_halves:
        per_half = lhs.shape[2] // tm
        n = 2 * lhs.shape[2]
        lhs_spec = pl.BlockSpec((None, tk, tm), lambda i, k: (i // per_half, k, i % per_half))
    else:
        n = lhs.shape[1]
        lhs_spec = pl.BlockSpec((tk, tm), lambda i, k: (k, i))
    nk = s // tk

    def body(l_ref, r_ref, o_ref, ob_ref):
        k = pl.program_id(1)

        @pl.when(k == 0)
        def _():
            o_ref[...] = jnp.zeros_like(o_ref)

        o_ref[...] += _tn(l_ref[...], r_ref[...])

        @pl.when(k == nk - 1)
        def _():
            ob_ref[...] = o_ref[...].astype(BF16)

    return pl.pallas_call(
        body, name=name, grid=(n // tm, nk),
        in_specs=[lhs_spec, pl.BlockSpec((tk, D), lambda i, k: (k, 0))],
        out_specs=[pl.BlockSpec((tm, D), lambda i, k: (i, 0))] * 2,
        out_shape=[pltpu.HBM((n, D), F32), pltpu.HBM((n, D), BF16)],
        compiler_params=_cp(("parallel", "arbitrary")),
    )(lhs, rhs)


def _band_base(max_dist, dist_unit, first):
    row = lax.broadcasted_iota(I32, (BLK, 2 * BLK), 0)
    col = lax.broadcasted_iota(I32, (BLK, 2 * BLK), 1)
    dist = BLK + row - col
    ok = (dist >= 0) & (dist <= max_dist)
    if first:
        ok = ok & (col >= BLK)
    return jnp.where(ok, dist.astype(F32) * (-float(dist_unit)), -jnp.inf)


def _half_mask(shape, e):
    lane = lax.broadcasted_iota(I32, shape, 1)
    return (lane < HD) if e == 0 else (lane >= HD)


def _to_half(x, e, g):
    if g != e:
        x = pltpu.roll(x, HD, 1)
    return jnp.where(_half_mask(x.shape, g), x, 0.0)


def _stack_heads(scalars, tile):
    return jnp.concatenate([scalars[0] * tile, scalars[1] * tile], axis=0)


def _pair_fwd(q2, kb, vb, base, slopes, kv_heads, sinks):
    lo = _half_mask((BLK, 2 * HD), 0)
    if slopes is None:
        bias = base
    elif sinks is None:
        bias = _stack_heads(slopes, base)
    else:
        col0 = lax.broadcasted_iota(I32, base.shape, 1) == 0
        bias = jnp.concatenate([jnp.where(col0, sinks[e], slopes[e] * base) for e in (0, 1)], axis=0)
    qs = jnp.concatenate([_to_half(q2, e, kv_heads[e]) * SCALE for e in (0, 1)], axis=0).astype(BF16)
    s = _nt(qs, kb) + bias
    m = jnp.max(s, axis=1, keepdims=True)
    p = jnp.exp(s - m)
    l = jnp.sum(p, axis=1, keepdims=True)
    o = _nn(p.astype(BF16), vb) / l
    lse = m + jnp.log(l)
    halves = []
    for e in (0, 1):
        oh = o[e * BLK:(e + 1) * BLK]
        halves.append(pltpu.roll(oh, HD, 1) if kv_heads[e] != e else oh)
    o2 = jnp.where(lo, halves[0], halves[1])
    lse2 = jnp.where(lo, jnp.broadcast_to(lse[:BLK], (BLK, 2 * HD)), jnp.broadcast_to(lse[BLK:], (BLK, 2 * HD)))
    return o2, lse2


def _pair_bwd(q2, kb, vb, do2, o2, lse2, base, slopes, kv_heads, sinks):
    lo = _half_mask((BLK, 2 * HD), 0)
    prod = do2 * o2
    lses, deltas = [], []
    for e in (0, 1):
        hq = _half_mask((BLK, 2 * HD), e)
        lses.append(jnp.max(jnp.where(hq, lse2, -jnp.inf), axis=1, keepdims=True))
        deltas.append(jnp.sum(jnp.where(hq, prod, 0.0), axis=1, keepdims=True))
    lse = jnp.concatenate(lses, axis=0)
    delta = jnp.concatenate(deltas, axis=0)
    qs = jnp.concatenate([_to_half(q2, e, kv_heads[e]) * SCALE for e in (0, 1)], axis=0).astype(BF16)
    dos = jnp.concatenate([_to_half(do2, e, kv_heads[e]) for e in (0, 1)], axis=0).astype(BF16)
    p = jnp.exp(_nt(qs, kb) + (base if slopes is None else _stack_heads(slopes, base)) - lse)
    ds = (p * (_nt(dos, vb) - delta)).astype(BF16)
    dq = _nn(ds, kb) * SCALE
    halves = []
    for e in (0, 1):
        dqh = dq[e * BLK:(e + 1) * BLK]
        halves.append(pltpu.roll(dqh, HD, 1) if kv_heads[e] != e else dqh)
    dq2 = jnp.where(lo, halves[0], halves[1])
    dk2 = _tn(ds, qs)
    dv2 = _tn(p.astype(BF16), dos)
    dsinks = []
    if sinks is not None:
        for e in (0, 1):
            dsinks.append(jnp.sum(-jnp.exp(sinks[e] - lses[e]) * deltas[e], axis=0, keepdims=True))
    return dq2, dk2, dv2, dsinks


A_BLOCKS_PER_STEP = 2
A_BLOCKS_PER_STEP_BWD = 1


def _attn_a_fwd(proj, sinks):
    s = proj.shape[1]
    nq = A_BLOCKS_PER_STEP
    rows = BLK * nq
    steps = s // rows

    def body(sink_ref, q_ref, kp_ref, kc_ref, vp_ref, vc_ref, o_ref, lse_ref):
        n = pl.program_id(0)
        base_rest = _band_base(A_MAX_DIST, 1, False)
        base_0 = jnp.where(n > 0, base_rest, _band_base(A_MAX_DIST, 1, True))
        for i in range(nq):
            cur = pl.ds(i * BLK, BLK)
            k_prev = kc_ref[pl.ds((i - 1) * BLK, BLK), :] if i > 0 else kp_ref[...]
            v_prev = vc_ref[pl.ds((i - 1) * BLK, BLK), :] if i > 0 else vp_ref[...]
            first_key = lax.broadcasted_iota(I32, (2 * BLK, 128), 0) == 0
            kb = jnp.where(first_key, 0.0, jnp.concatenate([k_prev, kc_ref[cur, :]], axis=0)).astype(BF16)
            vb = jnp.where(first_key, 0.0, jnp.concatenate([v_prev, vc_ref[cur, :]], axis=0)).astype(BF16)
            for j in range(NH // 2):
                g = j // 2
                o2, lse2 = _pair_fwd(q_ref[j, cur, :], kb, vb, base_rest if i > 0 else base_0,
                                     (SLOPES[2 * j], SLOPES[2 * j + 1]), (g, g), (sink_ref[2 * j], sink_ref[2 * j + 1]))
                o_ref[j, cur, :] = o2
                lse_ref[j, cur, :] = lse2

    before = lambda n: jnp.maximum(n * nq - 1, 0)
    slab = lambda g: pl.BlockSpec((None, rows, 128), lambda n: (g, n, 0))
    edge = lambda g: pl.BlockSpec((None, BLK, 128), lambda n: (g, before(n), 0))
    quad = pl.BlockSpec((4, rows, 128), lambda n: (0, n, 0))
    return pl.pallas_call(
        body, name="attn_a_fwd", grid=(steps,),
        in_specs=[SMEM, quad, edge(4), slab(4), edge(5), slab(5)],
        out_specs=[quad, quad],
        out_shape=[jax.ShapeDtypeStruct((4, s, 128), F32)] * 2,
        compiler_params=_cp(("parallel",)),
    )(sinks, proj, proj, proj, proj, proj)


def _attn_a_bwd(proj, sinks, d_o, o, lse):
    s = proj.shape[1]
    nq = A_BLOCKS_PER_STEP_BWD
    rows = BLK * nq
    steps = s // rows

    def body(sink_ref, q_ref, kp_ref, kc_ref, vp_ref, vc_ref, do_ref, o_ref, lse_ref,
             dq_ref, dk_ref, dv_ref, dsink_ref, kcar, vcar):
        n = pl.program_id(0)

        @pl.when(n == 0)
        def _():
            kcar[...] = jnp.zeros_like(kcar)
            vcar[...] = jnp.zeros_like(vcar)
            dsink_ref[...] = jnp.zeros_like(dsink_ref)

        dk_ref[...] = kcar[...].astype(BF16)
        dv_ref[...] = vcar[...].astype(BF16)

        @pl.when(n < steps)
        def _():
            base_rest = _band_base(A_MAX_DIST, 1, False)
            base_0 = jnp.where(n > 0, base_rest, _band_base(A_MAX_DIST, 1, True))
            for i in range(nq):
                cur = pl.ds(i * BLK, BLK)
                k_prev = kc_ref[pl.ds((i - 1) * BLK, BLK), :] if i > 0 else kp_ref[...]
                v_prev = vc_ref[pl.ds((i - 1) * BLK, BLK), :] if i > 0 else vp_ref[...]
                kb = jnp.concatenate([k_prev, kc_ref[cur, :]], axis=0).astype(BF16)
                vb = jnp.concatenate([v_prev, vc_ref[cur, :]], axis=0).astype(BF16)
                dk_win = dv_win = None
                for j in range(NH // 2):
                    g = j // 2
                    dq2, dk2, dv2, dsk = _pair_bwd(q_ref[j, cur, :], kb, vb, do_ref[j, cur, :], o_ref[j, cur, :],
                                                   lse_ref[j, cur, :], base_rest if i > 0 else base_0,
                                                   (SLOPES[2 * j], SLOPES[2 * j + 1]), (g, g),
                                                   (sink_ref[2 * j], sink_ref[2 * j + 1]))
                    dq_ref[j, cur, :] = dq2.astype(BF16)
                    dk_win = dk2 if j == 0 else dk_win + dk2
                    dv_win = dv2 if j == 0 else dv_win + dv2
                    for e in (0, 1):
                        h = 2 * j + e
                        dsink_ref[h:h + 1, :] += jnp.broadcast_to(dsk[e], (1, 128))
                if i == 0:
                    last = pl.ds((nq - 1) * BLK, BLK)
                    dk_ref[last, :] = (kcar[last, :] + dk_win[:BLK]).astype(BF16)
                    dv_ref[last, :] = (vcar[last, :] + dv_win[:BLK]).astype(BF16)
                else:
                    kcar[pl.ds((i - 1) * BLK, BLK), :] += dk_win[:BLK]
                    vcar[pl.ds((i - 1) * BLK, BLK), :] += dv_win[:BLK]
                kcar[cur, :] = dk_win[BLK:]
                vcar[cur, :] = dv_win[BLK:]

    cur_step = lambda n: jnp.minimum(n, steps - 1)
    before = lambda n: jnp.maximum(cur_step(n) * nq - 1, 0)
    out_prev = lambda n: jnp.maximum(n - 1, 0)
    quad = pl.BlockSpec((4, rows, 128), lambda n: (0, cur_step(n), 0))
    slab = lambda g: pl.BlockSpec((None, rows, 128), lambda n: (g, cur_step(n), 0))
    edge = lambda g: pl.BlockSpec((None, BLK, 128), lambda n: (g, before(n), 0))
    return pl.pallas_call(
        body, name="attn_a_bwd", grid=(steps + 1,),
        in_specs=[SMEM, quad, edge(4), slab(4), edge(5), slab(5), quad, quad, quad],
        out_specs=[quad,
                   pl.BlockSpec((rows, 128), lambda n: (out_prev(n), 0)),
                   pl.BlockSpec((rows, 128), lambda n: (out_prev(n), 0)),
                   pl.BlockSpec((NH, 128), lambda n: (0, 0))],
        out_shape=[pltpu.HBM((4, s, 128), BF16), pltpu.HBM((s, 128), BF16), pltpu.HBM((s, 128), BF16),
                   jax.ShapeDtypeStruct((NH, 128), F32)],
        scratch_shapes=[pltpu.VMEM((rows, 128), F32), pltpu.VMEM((rows, 128), F32)],
        compiler_params=_cp(("arbitrary",)),
    )(sinks, proj, proj, proj, proj, proj, d_o, o, lse)


def _stream(rho, i, r):
    start = i * BLK * r + rho
    return pl.ds(start, BLK, stride=r) if r > 1 else pl.ds(start, BLK)


def _for_streams(r, fn, side_by_side=4):
    if r <= side_by_side:
        for rho in range(r):
            fn(rho)
    else:
        def group(it, carry):
            for u in range(side_by_side):
                fn(side_by_side * it + u)
            return carry

        lax.fori_loop(0, r // side_by_side, group, 0)


B_BLOCKS_PER_STEP = {1: 8, 4: 2, 16: 1}
B_BLOCKS_PER_STEP_FWD = {1: 16, 4: 4, 16: 1}


def _attn_b_fwd(proj, slopes, r, so_far=None):
    s = proj.shape[1]
    nq = B_BLOCKS_PER_STEP_FWD[r]
    rows = BLK * r * nq
    steps = s // rows
    qc, kc, vc = WA // 128, WA // 128 + 4, WA // 128 + 8
    chained = so_far is not None

    def body(slope_ref, q_ref, kp_ref, kc_ref, vp_ref, vc_ref, *rest):
        po_ref, pl_ref = rest[:2] if chained else (None, None)
        o_ref, lse_ref = rest[-2:]
        j = pl.program_id(0)
        sb = pl.program_id(1)
        sl2 = (slope_ref[2 * j], slope_ref[2 * j + 1])
        bias_rest = _stack_heads(sl2, _band_base(B_MAX_DIST, r, False))
        bias_0 = jnp.where(sb > 0, bias_rest, _stack_heads(sl2, _band_base(B_MAX_DIST, r, True)))

        def stream(rho):
            for i in range(nq):
                cur = _stream(rho, i, r)
                k_prev = kc_ref[_stream(rho, i - 1, r), :] if i > 0 else kp_ref[_stream(rho, 0, r), :]
                v_prev = vc_ref[_stream(rho, i - 1, r), :] if i > 0 else vp_ref[_stream(rho, 0, r), :]
                kb = jnp.concatenate([k_prev, kc_ref[cur, :]], axis=0).astype(BF16)
                vb = jnp.concatenate([v_prev, vc_ref[cur, :]], axis=0).astype(BF16)
                o2, lse2 = _pair_fwd(q_ref[cur, :], kb, vb, bias_rest if i > 0 else bias_0, None, (0, 1), None)
                if chained:
                    lse1 = pl_ref[cur, :]
                    m = jnp.maximum(lse1, lse2)
                    e1, e2 = jnp.exp(lse1 - m), jnp.exp(lse2 - m)
                    den = e1 + e2
                    o2 = (e1 * po_ref[cur, :] + e2 * o2) * (1.0 / den)
                    lse2 = m + jnp.log(den)
                o_ref[cur, :] = o2
                lse_ref[cur, :] = lse2

        _for_streams(r, stream, side_by_side=16)

    before = lambda sb: jnp.maximum(sb * nq - 1, 0)
    result = pl.BlockSpec((None, rows, 128), lambda j, sb: (j, sb, 0))
    return pl.pallas_call(
        body, name=f"attn_b_fwd_r{r}", grid=(NH // 2, steps),
        in_specs=[SMEM,
                  pl.BlockSpec((None, rows, 128), lambda j, sb: (qc + j, sb, 0)),
                  pl.BlockSpec((None, BLK * r, 128), lambda j, sb: (kc + j, before(sb), 0)),
                  pl.BlockSpec((None, rows, 128), lambda j, sb: (kc + j, sb, 0)),
                  pl.BlockSpec((None, BLK * r, 128), lambda j, sb: (vc + j, before(sb), 0)),
                  pl.BlockSpec((None, rows, 128), lambda j, sb: (vc + j, sb, 0))] + ([result] * 2 if chained else []),
        out_specs=[result] * 2,
        out_shape=[jax.ShapeDtypeStruct((4, s, 128), F32)] * 2,
        compiler_params=_cp(("parallel", "parallel")),
    )(slopes, proj, proj, proj, proj, proj, *(so_far if chained else ()))


def _attn_b_bwd(proj, slopes, d_o, o, lse, r, so_far=None, dtype=F32):
    s = proj.shape[1]
    nq = B_BLOCKS_PER_STEP[r]
    rows = BLK * r * nq
    steps = s // rows
    qc, kc, vc = WA // 128, WA // 128 + 4, WA // 128 + 8
    chained = so_far is not None

    def body(slope_ref, q_ref, kp_ref, kc_ref, vp_ref, vc_ref, do_ref, o_ref, lse_ref, *rest):
        pq_ref, pk_ref, pv_ref = rest[:3] if chained else (None, None, None)
        dq_ref, dk_ref, dv_ref, kcar, vcar = rest[-5:]
        j = pl.program_id(0)
        sb = pl.program_id(1)

        @pl.when(sb == 0)
        def _():
            kcar[...] = jnp.zeros_like(kcar)
            vcar[...] = jnp.zeros_like(vcar)

        def settled(car, p_ref, idx):
            return car[idx] + p_ref[idx] if chained else car[idx]

        dk_ref[...] = settled(kcar, pk_ref, ...).astype(dtype)
        dv_ref[...] = settled(vcar, pv_ref, ...).astype(dtype)

        @pl.when(sb < steps)
        def _():
            sl2 = (slope_ref[2 * j], slope_ref[2 * j + 1])
            bias_rest = _stack_heads(sl2, _band_base(B_MAX_DIST, r, False))
            bias_0 = jnp.where(sb > 0, bias_rest, _stack_heads(sl2, _band_base(B_MAX_DIST, r, True)))

            def stream(rho):
                for i in range(nq):
                    cur = _stream(rho, i, r)
                    k_prev = kc_ref[_stream(rho, i - 1, r), :] if i > 0 else kp_ref[_stream(rho, 0, r), :]
                    v_prev = vc_ref[_stream(rho, i - 1, r), :] if i > 0 else vp_ref[_stream(rho, 0, r), :]
                    kb = jnp.concatenate([k_prev, kc_ref[cur, :]], axis=0).astype(BF16)
                    vb = jnp.concatenate([v_prev, vc_ref[cur, :]], axis=0).astype(BF16)
                    dq2, dk2, dv2, _ = _pair_bwd(q_ref[cur, :], kb, vb, do_ref[cur, :], o_ref[cur, :], lse_ref[cur, :],
                                                 bias_rest if i > 0 else bias_0, None, (0, 1), None)
                    dq_ref[cur, :] = (dq2 + pq_ref[cur, :] if chained else dq2).astype(dtype)
                    if i == 0:
                        last = (_stream(rho, nq - 1, r), slice(None))
                        dk_ref[last] = (settled(kcar, pk_ref, last) + dk2[:BLK]).astype(dtype)
                        dv_ref[last] = (settled(vcar, pv_ref, last) + dv2[:BLK]).astype(dtype)
                    else:
                        kcar[_stream(rho, i - 1, r), :] += dk2[:BLK]
                        vcar[_stream(rho, i - 1, r), :] += dv2[:BLK]
                    kcar[cur, :] = dk2[BLK:]
                    vcar[cur, :] = dv2[BLK:]

            _for_streams(r, stream, side_by_side=8)

    cur_step = lambda sb: jnp.minimum(sb, steps - 1)
    before = lambda sb: jnp.maximum(cur_step(sb) * nq - 1, 0)
    out_prev = lambda sb: jnp.maximum(sb - 1, 0)
    tile = lambda slab: pl.BlockSpec((None, rows, 128), lambda j, sb: (slab + j, cur_step(sb), 0))
    edge = lambda slab: pl.BlockSpec((None, BLK * r, 128), lambda j, sb: (slab + j, before(sb), 0))
    late = pl.BlockSpec((None, rows, 128), lambda j, sb: (j, out_prev(sb), 0))
    grads = [tile(0), late, late]
    return pl.pallas_call(
        body, name=f"attn_b_bwd_r{r}", grid=(NH // 2, steps + 1),
        in_specs=[SMEM, tile(qc), edge(kc), tile(kc), edge(vc), tile(vc), tile(0), tile(0), tile(0)]
        + (grads if chained else []),
        out_specs=grads,
        out_shape=[pltpu.HBM((4, s, 128), dtype)] * 3,
        scratch_shapes=[pltpu.VMEM((rows, 128), F32), pltpu.VMEM((rows, 128), F32)],
        compiler_params=_cp(("parallel", "arbitrary")),
    )(slopes, proj, proj, proj, proj, proj, d_o, o, lse, *(so_far if chained else ()))


def _row(v):
    return v.reshape(1, -1)


def _layer_norm_stats(z):
    mu = jnp.mean(z, axis=-1, keepdims=True)
    zc = z - mu
    var = jnp.mean(zc * zc, axis=-1, keepdims=True)
    rstd = lax.rsqrt(var + LN_EPS)
    return zc * rstd, rstd


def _layer_norm_bwd(dh, zh, rstd, g):
    dzh = dh * g
    return rstd * (dzh - jnp.mean(dzh, axis=-1, keepdims=True) - zh * jnp.mean(dzh * zh, axis=-1, keepdims=True))


def _rms(o):
    return lax.rsqrt(jnp.mean(o * o, axis=-1, keepdims=True) + RMS_EPS)


def _mix_ln1(x, o_a, o_b, norm_a_g, norm_b_g, w_o, ln1_g, ln1_b, tm=512):
    s = x.shape[0]

    def wide(ref):
        return jnp.concatenate([ref[j] for j in range(4)], axis=1)

    def body(x_ref, oa_ref, ob_ref, ga_ref, gb_ref, wo_ref, g_ref, b_ref, cat_ref, z1_ref, h1_ref, h1b_ref):
        oa, ob = wide(oa_ref), wide(ob_ref)
        na = oa * _rms(oa) * ga_ref[...]
        nb_ = ob * _rms(ob) * gb_ref[...]
        cat = jnp.concatenate([na, nb_], axis=1).astype(BF16)
        cat_ref[...] = cat
        z1 = ALPHA * x_ref[...] + _nn(cat, wo_ref[...])
        z1_ref[...] = z1
        zh, _ = _layer_norm_stats(z1)
        h1 = zh * g_ref[...] + b_ref[...]
        h1_ref[...] = h1
        h1b_ref[...] = h1.astype(BF16)

    t512 = pl.BlockSpec((4, tm, 128), lambda i: (0, i, 0))
    td = pl.BlockSpec((tm, D), lambda i: (i, 0))
    return pl.pallas_call(
        body, name="mix_ln1", grid=(s // tm,),
        in_specs=[td] + [t512] * 2 + [_const((1, 512))] * 2 + [_resident((D, D))] + [_const((1, D))] * 2,
        out_specs=[td, td, td, td],
        out_shape=[jax.ShapeDtypeStruct((s, D), BF16), jax.ShapeDtypeStruct((s, D), F32),
                   jax.ShapeDtypeStruct((s, D), F32), jax.ShapeDtypeStruct((s, D), BF16)],
        compiler_params=_cp(("parallel",)),
    )(x, o_a, o_b, _row(norm_a_g), _row(norm_b_g), w_o, _row(ln1_g), _row(ln1_b))


def _gelu_and_grad(x):
    c = math.sqrt(2.0 / math.pi)
    x2 = x * x
    s = 0.5 * jnp.tanh(x * ((c * 0.044715) * x2 + c)) + 0.5
    dg = s + (x * ((6.0 * c * 0.044715) * x2 + 2.0 * c)) * (s - s * s)
    return x * s, dg


def _shifted(u, edge, row, down):
    groups = [u[8 * i:8 * i + 8] for i in range(u.shape[0] // 8)]
    others = [edge] + groups[:-1] if down else groups[1:] + [edge]
    moved = []
    for k in (1, 2):
        crossing = row >= 8 - k if down else row < k
        moved.append(jnp.concatenate([pltpu.roll(jnp.where(crossing, o, g), k if down else 8 - k, 0)
                                      for o, g in zip(others, groups)], axis=0))
    return moved


def _up_proj(h1b, w_up, tm=512):
    s = h1b.shape[0]

    def body(h_ref, w_ref, o_ref):
        h = h_ref[...]
        for half in (0, 1):
            o_ref[half] = _nn(h, w_ref[:, half * FF:(half + 1) * FF]).astype(BF16)

    return pl.pallas_call(
        body, name="up_proj", grid=(s // tm,),
        in_specs=[pl.BlockSpec((tm, D), lambda i: (i, 0)), _resident((D, 2 * FF))],
        out_specs=pl.BlockSpec((2, tm, FF), lambda i: (0, i, 0)),
        out_shape=jax.ShapeDtypeStruct((2, s, FF), BF16),
        compiler_params=_cp(("parallel",)),
    )(h1b, w_up)


def _conv_gelu(up, cwb, tm=512, tn=FF // 2, chunk_rows=16):
    s = up.shape[1]
    n_c = tm // chunk_rows

    def body(up_ref, c_ref, a_ref, g_ref, a1_ref, carry):
        @pl.when(pl.program_id(1) == 0)
        def _():
            carry[...] = jnp.zeros_like(carry)

        row = lax.broadcasted_iota(jnp.int32, (8, tn), 0)
        edge = [carry[0], carry[1]]
        for c in range(n_c):
            rows = pl.ds(c * chunk_rows, chunk_rows)
            u = []
            for half in (0, 1):
                x = up_ref[half, rows, :].astype(F32)
                r1, r2 = _shifted(x, edge[half], row, True)
                u.append(r2 * c_ref[0, half:half + 1, :] + r1 * c_ref[1, half:half + 1, :]
                         + x * c_ref[2, half:half + 1, :] + c_ref[3, half:half + 1, :])
                edge[half] = x[chunk_rows - 8:]
            g, dg = _gelu_and_grad(u[0])
            a_ref[rows, :] = (g * u[1]).astype(BF16)
            g_ref[rows, :] = g.astype(BF16)
            a1_ref[rows, :] = (u[1] * dg).astype(BF16)
        for half in (0, 1):
            carry[half] = edge[half]

    pair = pl.BlockSpec((2, tm, tn), lambda j, i: (0, i, j))
    tile = pl.BlockSpec((tm, tn), lambda j, i: (i, j))
    return pl.pallas_call(
        body, name="conv_gelu", grid=(FF // tn, s // tm),
        in_specs=[pair, pl.BlockSpec((4, 2, tn), lambda j, i: (0, 0, j))],
        out_specs=[tile, tile, tile],
        out_shape=[jax.ShapeDtypeStruct((s, FF), BF16)] * 3,
        scratch_shapes=[pltpu.VMEM((2, 8, tn), F32)],
        compiler_params=_cp(("parallel", "arbitrary")),
    )(up, cwb)


def _down_ln2_loss(a, w_down, h1, target, ln2_g, ln2_b, tm=512):
    s = a.shape[0]

    def body(a_ref, w_ref, h_ref, t_ref, g_ref, b_ref, dz_ref, dzb_ref, st_ref):
        @pl.when(pl.program_id(0) == 0)
        def _():
            st_ref[...] = jnp.zeros_like(st_ref)

        z2 = ALPHA * h_ref[...] + _nn(a_ref[...], w_ref[...])
        zh, rstd = _layer_norm_stats(z2)
        diff = zh * g_ref[...] + b_ref[...] - t_ref[...]
        part = 0.5 * jnp.sum(jnp.mean(diff * diff, axis=-1, keepdims=True), axis=0, keepdims=True)
        dy = diff * (1.0 / D)
        st_ref[0:1, :] += jnp.sum(dy * zh, axis=0, keepdims=True)
        st_ref[1:2, :] += jnp.sum(dy, axis=0, keepdims=True)
        st_ref[2:3, :] += jnp.broadcast_to(part, (1, D))
        dz = _layer_norm_bwd(dy, zh, rstd, g_ref[...])
        dz_ref[...] = dz
        dzb_ref[...] = dz.astype(BF16)

    td = pl.BlockSpec((tm, D), lambda i: (i, 0))
    return pl.pallas_call(
        body, name="down_ln2_loss", grid=(s // tm,),
        in_specs=[pl.BlockSpec((tm, FF), lambda i: (i, 0)), _resident((FF, D)), td, td, _const((1, D)), _const((1, D))],
        out_specs=[td, td, _const((8, D))],
        out_shape=[jax.ShapeDtypeStruct((s, D), F32), jax.ShapeDtypeStruct((s, D), BF16),
                   jax.ShapeDtypeStruct((8, D), F32)],
        compiler_params=_cp(("arbitrary",)),
    )(a, w_down, h1, target, _row(ln2_g), _row(ln2_b))


def _d_act(dz2b, w_down, tm=512):
    s = dz2b.shape[0]

    def body(dz_ref, w_ref, o_ref):
        o_ref[...] = _nt(dz_ref[...], w_ref[...]).astype(BF16)

    return pl.pallas_call(
        body, name="d_act", grid=(s // tm,),
        in_specs=[pl.BlockSpec((tm, D), lambda i: (i, 0)), _resident((FF, D))],
        out_specs=pl.BlockSpec((tm, FF), lambda i: (i, 0)),
        out_shape=jax.ShapeDtypeStruct((s, FF), BF16),
        compiler_params=_cp(("parallel",)),
    )(dz2b, w_down)


def _conv_gelu_bwd(da, up, g, a1, cwb, tm=512, tn=FF // 2, chunk_rows=16):
    s = da.shape[0]
    n_i = s // tm
    n_c = tm // chunk_rows

    def body(da_ref, up_ref, g_ref, a1_ref, c_ref, dup_ref, dc_ref, carry):
        @pl.when(pl.program_id(1) == 0)
        def _():
            carry[...] = jnp.zeros_like(carry)
            dc_ref[...] = jnp.zeros_like(dc_ref)

        def fold(v):
            return jnp.sum(v.reshape(chunk_rows // 8, 8, v.shape[1]), axis=0)

        def chunk(cc, state):
            after, sums = state
            rows = pl.ds((n_c - 1 - cc) * chunk_rows, chunk_rows)
            da_c = da_ref[rows, :].astype(F32)
            dus = (da_c * a1_ref[rows, :].astype(F32), da_c * g_ref[rows, :].astype(F32))
            head, new_sums = [], []
            for half in (0, 1):
                du = dus[half]
                up = up_ref[half, rows, :].astype(F32)
                l1, l2 = _shifted(du, after[half], row, False)
                dup = (du * c_ref[2, half:half + 1, :] + l1 * c_ref[1, half:half + 1, :]
                       + l2 * c_ref[0, half:half + 1, :])
                dup_ref[half, rows, :] = dup.astype(BF16)
                parts = (fold(l2 * up), fold(l1 * up), fold(du * up), fold(du))
                new_sums.append(parts if sums is None else tuple(a + b for a, b in zip(sums[half], parts)))
                head.append(du[:8])
            return tuple(head), new_sums

        row = lax.broadcasted_iota(jnp.int32, (8, tn), 0)
        state = ((carry[0], carry[1]), None)
        for cc in range(n_c):
            state = chunk(cc, state)
        head, sums = state
        for half in (0, 1):
            carry[half] = head[half]
            for k in range(4):
                dc_ref[k, half:half + 1, :] += jnp.sum(sums[half][k], axis=0, keepdims=True)

    rev = lambda ii: n_i - 1 - ii
    tile = pl.BlockSpec((tm, tn), lambda j, ii: (rev(ii), j))
    pair = pl.BlockSpec((2, tm, tn), lambda j, ii: (0, rev(ii), j))
    per_col = pl.BlockSpec((4, 2, tn), lambda j, ii: (0, 0, j))
    return pl.pallas_call(
        body, name="conv_gelu_bwd", grid=(FF // tn, n_i),
        in_specs=[tile, pair, tile, tile, per_col],
        out_specs=[pair, per_col],
        out_shape=[jax.ShapeDtypeStruct((2, s, FF), BF16), jax.ShapeDtypeStruct((4, 2, FF), F32)],
        scratch_shapes=[pltpu.VMEM((2, 8, tn), F32)],
        compiler_params=_cp(("parallel", "arbitrary")),
    )(da, up, g, a1, cwb)


def _dh1_ln1_bwd(dz2, dup, w_up, z1, ln1_g, tm=512):
    s = dz2.shape[0]

    def body(dz2_ref, dup_ref, w_ref, z1_ref, g_ref, dz1_ref, dz1b_ref, st_ref):
        @pl.when(pl.program_id(0) == 0)
        def _():
            st_ref[...] = jnp.zeros_like(st_ref)

        dh = ALPHA * dz2_ref[...] + _nt(dup_ref[0], w_ref[:, :FF]) + _nt(dup_ref[1], w_ref[:, FF:])
        zh, rstd = _layer_norm_stats(z1_ref[...])
        st_ref[0:1, :] += jnp.sum(dh * zh, axis=0, keepdims=True)
        st_ref[1:2, :] += jnp.sum(dh, axis=0, keepdims=True)
        dz = _layer_norm_bwd(dh, zh, rstd, g_ref[...])
        dz1_ref[...] = dz
        dz1b_ref[...] = dz.astype(BF16)

    td = pl.BlockSpec((tm, D), lambda i: (i, 0))
    return pl.pallas_call(
        body, name="dh1_ln1_bwd", grid=(s // tm,),
        in_specs=[td, pl.BlockSpec((2, tm, FF), lambda i: (0, i, 0)), _resident((D, 2 * FF)), td, _const((1, D))],
        out_specs=[td, td, _const((8, D))],
        out_shape=[jax.ShapeDtypeStruct((s, D), F32), jax.ShapeDtypeStruct((s, D), BF16),
                   jax.ShapeDtypeStruct((8, D), F32)],
        compiler_params=_cp(("arbitrary",), 58),
    )(dz2, dup, w_up, z1, _row(ln1_g))


def _dcat_rms_bwd(dz1b, w_o, o_a, o_b, norm_a_g, norm_b_g, tm=512):
    s = dz1b.shape[0]

    def body(dz_ref, w_ref, oa_ref, ob_ref, ga_ref, gb_ref, da_ref, db_ref, st_ref):
        @pl.when(pl.program_id(0) == 0)
        def _():
            st_ref[...] = jnp.zeros_like(st_ref)

        dcat = _nt(dz_ref[...], w_ref[...])
        for k, (o_ref, g_ref, d_ref) in enumerate(((oa_ref, ga_ref, da_ref), (ob_ref, gb_ref, db_ref))):
            o = jnp.concatenate([o_ref[j] for j in range(4)], axis=1)
            dn = dcat[:, 512 * k:512 * (k + 1)]
            rr = _rms(o)
            oh = o * rr
            st_ref[k:k + 1, :] += jnp.sum(dn * oh, axis=0, keepdims=True)
            doh = dn * g_ref[...]
            d_o = rr * (doh - oh * jnp.mean(doh * oh, axis=-1, keepdims=True))
            for j in range(4):
                d_ref[j] = d_o[:, 128 * j:128 * (j + 1)]

    t512 = pl.BlockSpec((4, tm, 128), lambda i: (0, i, 0))
    return pl.pallas_call(
        body, name="dcat_rms_bwd", grid=(s // tm,),
        in_specs=[pl.BlockSpec((tm, D), lambda i: (i, 0)), _resident((D, D)), t512, t512,
                  _const((1, 512)), _const((1, 512))],
        out_specs=[t512, t512, _const((8, 512))],
        out_shape=[jax.ShapeDtypeStruct((4, s, 128), F32), jax.ShapeDtypeStruct((4, s, 128), F32),
                   jax.ShapeDtypeStruct((8, 512), F32)],
        compiler_params=_cp(("arbitrary",)),
    )(dz1b, w_o, o_a, o_b, _row(norm_a_g), _row(norm_b_g))


def _grad_w_in(dparts, xb, tk=2048):
    s = xb.shape[0]
    nk = s // tk

    def body(qa, ka, va, qb, kb, vb, x_ref, o_ref, ob_ref):
        i = pl.program_id(0)
        k = pl.program_id(1)

        @pl.when(k == 0)
        def _():
            o_ref[...] = jnp.zeros_like(o_ref)

        def add(blocks):
            o_ref[...] += _tn(jnp.concatenate(blocks, axis=1), x_ref[...])

        pl.when(i == 0)(lambda: add([qa[j] for j in range(4)] + [ka[...], va[...]]))
        pl.when(i == 1)(lambda: add([qb[j] for j in range(4)] + [kb[0], kb[1]]))
        pl.when(i == 2)(lambda: add([kb[0], kb[1]] + [vb[j] for j in range(4)]))

        @pl.when(k == nk - 1)
        def _():
            ob_ref[...] = o_ref[...].astype(BF16)

    def during(tile):
        return lambda i, k: jnp.where(i == tile, k, jnp.where(i < tile, 0, nk - 1))

    quad = lambda tile: pl.BlockSpec((4, tk, 128), lambda i, k: (0, during(tile)(i, k), 0))
    one = pl.BlockSpec((tk, 128), lambda i, k: (during(0)(i, k), 0))
    kb_spec = pl.BlockSpec((2, tk, 128), lambda i, k: (jnp.where(i == 2, 1, 0), jnp.where(i == 0, 0, k), 0))
    return pl.pallas_call(
        body, name="grad_w_in", grid=(3, nk),
        in_specs=[quad(0), one, one, quad(1), kb_spec, quad(2), pl.BlockSpec((tk, D), lambda i, k: (k, 0))],
        out_specs=[pl.BlockSpec((WA, D), lambda i, k: (i, 0))] * 2,
        out_shape=[pltpu.HBM((WIN, D), F32), pltpu.HBM((WIN, D), BF16)],
        compiler_params=_cp(("parallel", "arbitrary"), mb=56),
    )(*dparts, xb)


def _grad_x(dz1, dparts, w_in_t, zero, tm=512):
    s = dz1.shape[0]

    def body(dz_ref, qa, ka, va, qb, kb, vb, w_ref, z_ref, o_ref):
        dp = jnp.concatenate([qa[j] for j in range(4)] + [ka[...], va[...]]
                             + [ref[j] for ref in (qb, kb, vb) for j in range(4)], axis=1)
        o_ref[...] = ALPHA * dz_ref[...] + _nn(dp, w_ref[...]) + z_ref[0:1, 0:1]

    td = pl.BlockSpec((tm, D), lambda i: (i, 0))
    quad = pl.BlockSpec((4, tm, 128), lambda i: (0, i, 0))
    one = pl.BlockSpec((tm, 128), lambda i: (i, 0))
    return pl.pallas_call(
        body, name="grad_x", grid=(s // tm,),
        in_specs=[td, quad, one, one, quad, quad, quad, _resident((WIN, D)), _const((8, 128))],
        out_specs=td, out_shape=jax.ShapeDtypeStruct((s, D), F32),
        compiler_params=_cp(("parallel",)),
    )(dz1, *dparts, w_in_t, zero)


def _place():
    return lax.axis_index("x"), lax.axis_index("y"), lax.axis_index("c")


def _other_chips(x, y):
    return [(1 - x, y), (x, 1 - y), (1 - x, 1 - y)]


def _hbm(a):
    return pltpu.with_memory_space_constraint(a, pltpu.HBM)


def _gather_w_in(shard, conv_w):
    rows_k = shard.shape[0]
    half = rows_k // 2

    def body(src, conv_src, out, conv_out, send_sems, recv_sems):
        x, y, c = _place()
        b = 2 * x + y
        sibling = (x, y, 1 - c)
        chips = _other_chips(x, y)

        def copy(idx, chip_b, core, to, first_hop=False):
            rows = out.at[pl.ds(pl.multiple_of(chip_b * rows_k + core * half, 16), half)]
            s_ref = src.at[pl.ds(pl.multiple_of(core * half, 16), half)] if first_hop else rows
            return pltpu.make_async_remote_copy(src_ref=s_ref, dst_ref=rows, send_sem=send_sems.at[idx],
                                                recv_sem=recv_sems.at[idx], device_id=to, device_id_type=MESH)

        def own_copy():
            return pltpu.make_async_remote_copy(
                src_ref=src, dst_ref=out.at[pl.ds(pl.multiple_of(b * rows_k, 16), rows_k)], send_sem=send_sems.at[6],
                recv_sem=recv_sems.at[6], device_id=sibling, device_id_type=MESH)

        def conv_copy(idx, chip_b, to):
            return pltpu.make_async_remote_copy(src_ref=conv_src, dst_ref=conv_out.at[chip_b],
                                                send_sem=send_sems.at[7 + idx], recv_sem=recv_sems.at[7 + idx],
                                                device_id=to, device_id_type=MESH)

        started = [own_copy(), conv_copy(3, b, sibling)]
        for jn, chip in enumerate(chips):
            started += [copy(jn, b, c, (chip[0], chip[1], c), first_hop=True), conv_copy(jn, b, (chip[0], chip[1], c))]
        for cp in started:
            cp.start()
        for jn, chip in enumerate(chips):
            cb = 2 * chip[0] + chip[1]
            copy(jn, cb, c, (chip[0], chip[1], c)).wait_recv()
            cp = copy(3 + jn, cb, c, sibling)
            cp.start()
            started.append(cp)
        for jn, chip in enumerate(chips):
            cb = 2 * chip[0] + chip[1]
            copy(3 + jn, cb, 1 - c, sibling).wait_recv()
            conv_copy(jn, cb, (chip[0], chip[1], c)).wait_recv()
        own_copy().wait_recv()
        conv_copy(3, b, sibling).wait_recv()
        for cp in started:
            cp.wait_send()

    return pl.pallas_call(
        body, name="gather_w_in",
        in_specs=[ANY, ANY], out_specs=[ANY, ANY],
        out_shape=[jax.ShapeDtypeStruct((N_CHIPS * rows_k, D), BF16), jax.ShapeDtypeStruct((N_CHIPS,) + conv_w.shape, F32)],
        scratch_shapes=[pltpu.SemaphoreType.DMA((11,)), pltpu.SemaphoreType.DMA((11,))],
        compiler_params=pltpu.CompilerParams(has_side_effects=True),
    )(shard, conv_w)


def _weight_copies(shard, land, send_sems, recv_sems, arrivals):
    x, y, c = _place()
    n_rows, n_cols = shard.shape
    peers = [(px, py, c) for px, py in _other_chips(x, y)] + [(x, y, 1 - c)]
    cps = []
    for jn, peer in enumerate(peers):
        at = 2 * peer[0] + peer[1] if arrivals else 2 * x + y
        if land.shape[1] == n_cols:
            dst = land.at[pl.ds(pl.multiple_of(at * n_rows, 16), n_rows)]
        else:
            dst = land.at[:, pl.ds(pl.multiple_of(at * n_cols, 128), n_cols)]
        cps.append(pltpu.make_async_remote_copy(src_ref=shard, dst_ref=dst, send_sem=send_sems.at[jn],
                                                recv_sem=recv_sems.at[jn], device_id=peer, device_id_type=MESH))
    return cps


def _weights_start(shards, after):
    n = len(shards)
    lands = [lax.empty((N_CHIPS * sh.shape[0], D) if sh.shape[1] == D else (D, N_CHIPS * sh.shape[1]), BF16)
             for sh in shards]

    def body(*refs):
        src, land = refs[:n], refs[n:2 * n]
        send_sems, recv_sems = refs[2 * n + 1:3 * n + 1], refs[3 * n + 1:4 * n + 1]
        for k in range(n):
            for send in _weight_copies(src[k], land[k], send_sems[k], recv_sems[k], False):
                send.start()
        refs[-1][...] = jnp.zeros_like(refs[-1])

    res = pl.pallas_call(
        body, name="weights_start",
        in_specs=[HBM] * (2 * n) + [ANY], out_specs=[SEM] * (2 * n) + [HBM] * (2 * n) + [VMEM],
        out_shape=[pltpu.SemaphoreType.DMA((4,))] * (2 * n)
        + [pltpu.HBM(a.shape, a.dtype) for a in (*shards, *lands)] + [jax.ShapeDtypeStruct((8, 128), F32)],
        input_output_aliases={i: i + 2 * n for i in range(2 * n)},
        compiler_params=pltpu.CompilerParams(has_side_effects=DATAFLOW),
    )(*[_hbm(a) for a in (*shards, *lands)], after)
    return [(res[k], res[n + k], res[2 * n + k], res[3 * n + k]) for k in range(n)], res[-1]


def _weights_wait(started, after, name):
    send_sems, recv_sems, shard, land = started

    def body(s_ref, l_ref, send_ref, recv_ref, after_ref, s_out, l_out):
        for cp in _weight_copies(s_ref, l_ref, send_ref, recv_ref, True):
            cp.wait_send()
            cp.wait_recv()

    return pl.pallas_call(
        body, name=name,
        in_specs=[HBM, HBM, SEM, SEM, ANY], out_specs=[HBM, HBM],
        out_shape=[pltpu.HBM(shard.shape, shard.dtype), pltpu.HBM(land.shape, land.dtype)],
        input_output_aliases={0: 0, 1: 1},
        compiler_params=pltpu.CompilerParams(has_side_effects=DATAFLOW),
    )(shard, land, send_sems, recv_sems, after)[1]


def _grad_copies(g_ref, land_ref, send_sems, recv_sems):
    x, y, c = _place()
    cps = []
    for d in range(1, 8):
        px, py, pc = x ^ (d >> 2), y ^ ((d >> 1) & 1), c ^ (d & 1)
        cps.append(pltpu.make_async_remote_copy(
            src_ref=g_ref.at[2 * px + py, pc], dst_ref=land_ref.at[d - 1], send_sem=send_sems.at[d - 1],
            recv_sem=recv_sems.at[d - 1], device_id=(px, py, pc), device_id_type=MESH))
    return cps


def _grads_start(grads_b, name):
    n = len(grads_b)
    lands = [lax.empty((7, g.shape[2], D), BF16) for g in grads_b]

    def body(*refs):
        g, land = refs[:n], refs[n:2 * n]
        send_sems, recv_sems = refs[2 * n:3 * n], refs[3 * n:4 * n]
        for k in range(n):
            for cp in _grad_copies(g[k], land[k], send_sems[k], recv_sems[k]):
                cp.start()
        refs[-1][...] = jnp.zeros_like(refs[-1])

    res = pl.pallas_call(
        body, name=name,
        in_specs=[HBM] * (2 * n), out_specs=[SEM] * (2 * n) + [HBM] * (2 * n) + [VMEM],
        out_shape=[pltpu.SemaphoreType.DMA((7,))] * (2 * n)
        + [pltpu.HBM(a.shape, a.dtype) for a in (*grads_b, *lands)] + [jax.ShapeDtypeStruct((8, 128), F32)],
        input_output_aliases={i: i + 2 * n for i in range(2 * n)},
        compiler_params=pltpu.CompilerParams(has_side_effects=DATAFLOW),
    )(*[_hbm(a) for a in (*grads_b, *lands)])
    return [(res[k], res[n + k], res[2 * n + k], res[3 * n + k]) for k in range(n)], res[-1]


def _grads_wait(started, after, name):
    n = len(started)

    def body(*refs):
        g, land = refs[:n], refs[n:2 * n]
        send_sems, recv_sems = refs[2 * n:3 * n], refs[3 * n:4 * n]
        for k in range(n):
            for cp in _grad_copies(g[k], land[k], send_sems[k], recv_sems[k]):
                cp.wait_send()
                cp.wait_recv()

    gs = [st[2] for st in started]
    lands = [st[3] for st in started]
    res = pl.pallas_call(
        body, name=name,
        in_specs=[HBM] * (2 * n) + [SEM] * (2 * n) + [ANY], out_specs=[HBM] * (2 * n),
        out_shape=[pltpu.HBM(a.shape, a.dtype) for a in (*gs, *lands)],
        input_output_aliases={i: i for i in range(2 * n)},
        compiler_params=pltpu.CompilerParams(has_side_effects=DATAFLOW),
    )(*gs, *lands, *[st[0] for st in started], *[st[1] for st in started], after)
    return res[n:]


def _sum_partials(grad4, got, cb, name, tr):
    h = grad4.shape[2]
    per_half = h // tr

    def body(cb_ref, g_ref, o_ref, out_ref):
        acc = g_ref[...]
        for j in range(7):
            acc = acc + o_ref[j].astype(F32)
        out_ref[...] = acc

    return pl.pallas_call(
        body, name=name,
        grid_spec=pltpu.PrefetchScalarGridSpec(
            num_scalar_prefetch=1, grid=(per_half,),
            in_specs=[pl.BlockSpec((None, None, tr, D), lambda i, cb_ref: (cb_ref[1], cb_ref[0], i, 0)),
                      pl.BlockSpec((7, tr, D), lambda i, cb_ref: (0, i, 0))],
            out_specs=pl.BlockSpec((tr, D), lambda i, cb_ref: (cb_ref[0] * per_half + i, 0))),
        out_shape=pltpu.HBM((2 * h, D), F32),
        compiler_params=_cp(("arbitrary",)),
    )(cb, grad4, _hbm(got))


def _swap_halves(shards, name):
    n = len(shards)

    def body(*refs):
        out, send_sems, recv_sems = refs[n:2 * n], refs[2 * n], refs[2 * n + 1]
        x, y, c = _place()
        cps = []
        for k in range(n):
            h = shards[k].shape[0] // 2
            mine = out[k].at[pl.ds(pl.multiple_of(c * h, 8), h)]
            cp = pltpu.make_async_remote_copy(src_ref=mine, dst_ref=mine, send_sem=send_sems.at[k],
                                              recv_sem=recv_sems.at[k], device_id=(x, y, 1 - c), device_id_type=MESH)
            cp.start()
            cps.append(cp)
        for cp in cps:
            cp.wait()

    return pl.pallas_call(
        body, name=name,
        in_specs=[ANY] * n, out_specs=[ANY] * n,
        out_shape=[jax.ShapeDtypeStruct(sh.shape, F32) for sh in shards],
        input_output_aliases={k: k for k in range(n)},
        scratch_shapes=[pltpu.SemaphoreType.DMA((n,)), pltpu.SemaphoreType.DMA((n,))],
        compiler_params=pltpu.CompilerParams(has_side_effects=True),
    )(*shards)


def _small_copies(small_ref, land_ref, send_sems, recv_sems):
    x, y, c = _place()
    me = 4 * x + 2 * y + c
    cps = []
    for d in range(1, 8):
        px, py, pc = x ^ (d >> 2), y ^ ((d >> 1) & 1), c ^ (d & 1)
        cps.append(pltpu.make_async_remote_copy(
            src_ref=small_ref, dst_ref=land_ref.at[me], send_sem=send_sems.at[d - 1], recv_sem=recv_sems.at[d - 1],
            device_id=(px, py, pc), device_id_type=MESH))
    return cps


def _small_start(small):
    land = lax.empty((8,) + small.shape, F32)

    def body(s_ref, l_ref, send_sems, recv_sems, s_thru, l_thru, token):
        for cp in _small_copies(s_ref, l_ref, send_sems, recv_sems):
            cp.start()
        token[...] = jnp.zeros_like(token)

    res = pl.pallas_call(
        body, name="small_start",
        in_specs=[HBM, HBM], out_specs=[SEM, SEM, HBM, HBM, VMEM],
        out_shape=[pltpu.SemaphoreType.DMA((7,)), pltpu.SemaphoreType.DMA((7,)), pltpu.HBM(small.shape, F32),
                   pltpu.HBM(land.shape, F32), jax.ShapeDtypeStruct((8, 128), F32)],
        input_output_aliases={0: 2, 1: 3},
        compiler_params=pltpu.CompilerParams(has_side_effects=DATAFLOW),
    )(_hbm(small), _hbm(land))
    return res[:4], res[4]


def _small_wait(started, after):
    send_sems, recv_sems, small, land = started

    def body(s_ref, l_ref, send_ref, recv_ref, after_ref, s_out, l_out):
        for cp in _small_copies(s_ref, l_ref, send_ref, recv_ref):
            cp.wait_send()
            cp.wait_recv()

    return pl.pallas_call(
        body, name="small_wait",
        in_specs=[HBM, HBM, SEM, SEM, ANY], out_specs=[HBM, HBM],
        out_shape=[pltpu.HBM(small.shape, F32), pltpu.HBM(land.shape, F32)],
        input_output_aliases={0: 0, 1: 1},
        compiler_params=pltpu.CompilerParams(has_side_effects=DATAFLOW),
    )(small, land, send_sems, recv_sems, after)


def _small_sum(small, land, me):
    rows = small.shape[0]

    def body(me_ref, s_ref, l_ref, o_ref):
        acc = None
        for k in range(8):
            term = jnp.where(me_ref[0] == k, s_ref[...], l_ref[k])
            acc = term if k == 0 else acc + term
        o_ref[...] = acc

    return pl.pallas_call(
        body, name="small_sum",
        in_specs=[SMEM, VMEM, VMEM], out_specs=VMEM,
        out_shape=jax.ShapeDtypeStruct((rows, D), F32),
    )(me, small, land)


def _adamw(w, g, m, v, name, tr, g_transposed=False):
    rows, cols = w.shape

    def body(w_ref, g_ref, m_ref, v_ref, d_ref, nm_ref, nv_ref, *gt_ref):
        g_ = g_ref[...]
        if g_transposed:
            g_ = g_.T
            gt_ref[0][...] = g_
        nm = ADAM_B1 * m_ref[...] + (1.0 - ADAM_B1) * g_
        nv = ADAM_B2 * v_ref[...] + (1.0 - ADAM_B2) * (g_ * g_)
        m_hat = nm / (1.0 - ADAM_B1 ** ADAM_STEP)
        v_hat = nv / (1.0 - ADAM_B2 ** ADAM_STEP)
        d_ref[...] = -ADAM_LR * (m_hat / (jnp.sqrt(v_hat) + ADAM_EPS) + ADAM_WD * w_ref[...])
        nm_ref[...] = nm
        nv_ref[...] = nv

    spec = pl.BlockSpec((tr, cols), lambda i: (i, 0))
    g_spec = pl.BlockSpec((cols, tr), lambda i: (0, i)) if g_transposed else spec
    n_out = 4 if g_transposed else 3
    return pl.pallas_call(
        body, name=name, grid=(rows // tr,),
        in_specs=[spec, g_spec, spec, spec], out_specs=[spec] * n_out,
        out_shape=[jax.ShapeDtypeStruct((rows, cols), F32)] * n_out,
        compiler_params=_cp(("parallel",)),
    )(*[_hbm(a) for a in (w, g, m, v)])


def _local_step(x, target, w_in_t, late_weights, norm_a_g, norm_b_g, sinks_a, ln1_g, ln1_b,
                conv_w, conv_b, ln2_g, ln2_b, slopes, on_grad, on_small):
    cwb = jnp.concatenate([conv_w, conv_b[None]], axis=0).reshape(4, 2, FF)

    proj, xb = _proj(x, w_in_t, "proj")
    o_a, lse_a = _attn_a_fwd(proj, sinks_a)
    fwd_b = None
    for r in reversed(B_DILATIONS):
        fwd_b = _attn_b_fwd(proj, slopes, r, fwd_b)
    o_b, lse_b = fwd_b
    w_o = late_weights(1, lse_b)
    cat, z1, h1, h1b = _mix_ln1(x, o_a, o_b, norm_a_g, norm_b_g, w_o, ln1_g, ln1_b)
    w_up = late_weights(2, h1b)
    up = _up_proj(h1b, w_up)
    a, gate, a1 = _conv_gelu(up, cwb)
    w_down = late_weights(3, a)
    dz2, dz2b, st2 = _down_ln2_loss(a, w_down, h1, target, ln2_g, ln2_b)

    on_grad(3, *_grad_w(a, dz2b, "grad_w_down", tm=FF // 2))
    dup, dconv = _conv_gelu_bwd(_d_act(dz2b, w_down), up, gate, a1, cwb)
    on_grad(2, *_grad_w(dup, h1b, "grad_w_up", tm=FF // 2, lhs_halves=True))
    dz1, dz1b, st1 = _dh1_ln1_bwd(dz2, dup, w_up, z1, ln1_g)
    tok = on_grad(1, *_grad_w(cat, dz1b, "grad_w_o", tm=512))
    d_oa, d_ob, st_n = _dcat_rms_bwd(dz1b, w_o, o_a, o_b, norm_a_g + tok[0, 0], norm_b_g)
    dqa, dka, dva, dsink = _attn_a_bwd(proj, sinks_a, d_oa, o_a, lse_a)
    dconv = dconv.reshape(4, 2 * FF)
    tok = on_small(dict(loss=st2[2, 0:1], norm_a_g=st_n[0], norm_b_g=st_n[1], sinks_a=dsink[:, 0],
                        ln1_g=st1[0], ln1_b=st1[1], conv_w=dconv[0:3].reshape(-1), conv_b=dconv[3],
                        ln2_g=st2[0], ln2_b=st2[1]))
    slopes = slopes + tok[0, 0]
    bwd_b = None
    for r in reversed(B_DILATIONS):
        bwd_b = _attn_b_bwd(proj, slopes, d_ob, o_b, lse_b, r, bwd_b, BF16 if r == 1 else F32)
    dparts = tuple(_hbm(a) for a in (dqa, dka, dva, *bwd_b))
    tok = on_grad(0, *_grad_w_in(dparts, xb))
    return _grad_x(dz1, dparts, w_in_t, tok)


SMALL_ORDER = ("loss", "norm_a_g", "norm_b_g", "sinks_a", "ln1_g", "ln1_b", "conv_b", "ln2_g", "ln2_b", "conv_w")
SMALL_SIZES = dict(loss=1, norm_a_g=512, norm_b_g=512, sinks_a=8, ln1_g=D, ln1_b=D, conv_b=2 * FF, ln2_g=D, ln2_b=D,
                   conv_w=3 * 2 * FF)


def _pack(parts, rows):
    flat = jnp.concatenate([parts[k].reshape(-1).astype(F32) for k in parts])
    return jnp.pad(flat, (0, rows * D - flat.shape[0])).reshape(rows, D)


def _unpack(buf, names, sizes):
    flat = buf.reshape(-1)
    out, at = {}, 0
    for k in names:
        out[k] = flat[at:at + sizes[k]]
        at += sizes[k]
    return out


def kernel(x, w_in, norm_a_g, norm_b_g, sinks_a, w_o, ln1_g, ln1_b, w_up, conv_w, conv_b, w_down, ln2_g, ln2_b, loss_target, m_w_in, m_norm_a_g, m_norm_b_g, m_sinks_a, m_w_o, m_ln1_g, m_ln1_b, m_w_up, m_conv_w, m_conv_b, m_w_down, m_ln2_g, m_ln2_b, v_w_in, v_norm_a_g, v_norm_b_g, v_sinks_a, v_w_o, v_ln1_g, v_ln1_b, v_w_up, v_conv_w, v_conv_b, v_w_down, v_ln2_g, v_ln2_b):
    xi, yi, ci = _place()
    chip = (2 * xi + yi).astype(I32)
    core = ci.astype(I32)

    w_in_rows, m_w_in_rows, v_w_in_rows = w_in.T, m_w_in.T, v_w_in.T
    shards = (w_in_rows.astype(BF16), w_o.astype(BF16), w_up.astype(BF16), w_down.astype(BF16))
    w_in_t, conv_w4 = _gather_w_in(shards[0], conv_w)
    conv_w_f = conv_w4.transpose(1, 0, 2).reshape(3, 2 * FF)
    w_started, w_tok = _weights_start(shards[1:], conv_w4)
    slopes = jnp.asarray(SLOPES, F32) + w_tok[0, 0]

    halves_rows = [r // 2 for r in SHARD_ROWS]
    grads4, grads_b4, started = [None] * 4, [None] * 4, [None] * 4

    def on_grad(k, g, g_b):
        grads4[k] = g.reshape(N_CHIPS, 2, halves_rows[k], D)
        grads_b4[k] = g_b.reshape(N_CHIPS, 2, halves_rows[k], D)
        if k > 1:
            return None
        group = (1, 2, 3) if k == 1 else (0,)
        sts, tok = _grads_start([grads_b4[i] for i in group], f"grads_start_{k}")
        for i, st in zip(group, sts):
            started[i] = st
        return tok

    small_rows = 32
    small_started = []

    def on_small(parts):
        st, tok = _small_start(_pack({k: parts[k] for k in SMALL_ORDER}, small_rows))
        small_started.append(st)
        return tok

    gx = _local_step(
        x[0], loss_target[0], w_in_t, lambda k, after: _weights_wait(w_started[k - 1], after, f"weights_wait_{k}"),
        norm_a_g, norm_b_g, sinks_a, ln1_g, ln1_b, conv_w_f, conv_b, ln2_g, ln2_b, slopes, on_grad, on_small)

    tiles = (96, 128, 352, 176)
    core_chip = jnp.stack([core, chip])
    got = _grads_wait(started[1:], gx, "grads_wait_1")
    halves = [_sum_partials(grads4[k], got[k - 1], core_chip, f"sum_partials_{k}", tiles[k]) for k in (1, 2, 3)]
    g_w_o, g_w_up_rows, g_w_down = _swap_halves(halves, "swap_halves")
    delta, new_m, new_v = {}, {}, {}
    for k, g, tr in (("w_o", g_w_o, 128), ("w_down", g_w_down, 176)):
        delta[k], new_m[k], new_v[k] = _adamw(dict(w_o=w_o, w_down=w_down)[k], g, dict(w_o=m_w_o, w_down=m_w_down)[k],
                                              dict(w_o=v_w_o, w_down=v_w_down)[k], f"adamw_{k}", tr)
    delta["w_up"], new_m["w_up"], new_v["w_up"], g_w_up = _adamw(w_up, g_w_up_rows, m_w_up, v_w_up, "adamw_w_up", 256,
                                                                 g_transposed=True)

    got = _grads_wait(started[:1], delta["w_up"], "grads_wait_0")
    half_in = _sum_partials(grads4[0], got[0], core_chip, "sum_partials_0", tiles[0])
    (g_w_in_rows,) = _swap_halves([half_in], "swap_halves_in")
    small_mine, small_land = _small_wait(small_started[0], g_w_in_rows)
    totals = _small_sum(small_mine, small_land, (4 * xi + 2 * yi + ci).astype(I32).reshape(1))
    tot = _unpack(totals, SMALL_ORDER, SMALL_SIZES)
    loss = tot["loss"][0]
    cols = 2 * FF // N_CHIPS
    g_conv_w = lax.dynamic_slice(tot["conv_w"].reshape(3, 2 * FF), (0, chip * cols), (3, cols))
    g_small = dict(norm_a_g=tot["norm_a_g"], norm_b_g=tot["norm_b_g"], sinks_a=tot["sinks_a"], ln1_g=tot["ln1_g"],
                   ln1_b=tot["ln1_b"], conv_w=g_conv_w, conv_b=tot["conv_b"], ln2_g=tot["ln2_g"], ln2_b=tot["ln2_b"])

    weights = dict(w_in=w_in, norm_a_g=norm_a_g, norm_b_g=norm_b_g, sinks_a=sinks_a, w_o=w_o, ln1_g=ln1_g, ln1_b=ln1_b,
                   w_up=w_up, conv_w=conv_w, conv_b=conv_b, w_down=w_down, ln2_g=ln2_g, ln2_b=ln2_b)
    ms = dict(w_in=m_w_in, norm_a_g=m_norm_a_g, norm_b_g=m_norm_b_g, sinks_a=m_sinks_a, w_o=m_w_o, ln1_g=m_ln1_g,
              ln1_b=m_ln1_b, w_up=m_w_up, conv_w=m_conv_w, conv_b=m_conv_b, w_down=m_w_down, ln2_g=m_ln2_g, ln2_b=m_ln2_b)
    vs = dict(w_in=v_w_in, norm_a_g=v_norm_a_g, norm_b_g=v_norm_b_g, sinks_a=v_sinks_a, w_o=v_w_o, ln1_g=v_ln1_g,
              ln1_b=v_ln1_b, w_up=v_w_up, conv_w=v_conv_w, conv_b=v_conv_b, w_down=v_w_down, ln2_g=v_ln2_g, ln2_b=v_ln2_b)
    order = list(weights)
    grad = dict(g_small, w_in=g_w_in_rows.T, w_o=g_w_o, w_up=g_w_up, w_down=g_w_down)

    delta["w_in"], new_m["w_in"], new_v["w_in"] = [
        a.T for a in _adamw(w_in_rows, g_w_in_rows, m_w_in_rows, v_w_in_rows, "adamw_w_in", 144)]
    small_names = [k for k in order if k not in delta]
    sizes = {k: weights[k].size for k in small_names}
    rows = 16
    packed = [_pack({k: src[k] for k in small_names}, rows) for src in (weights, grad, ms, vs)]
    for res, buf in zip((delta, new_m, new_v), _adamw(*packed, "adamw_small", rows)):
        for k, val in _unpack(buf, small_names, sizes).items():
            res[k] = val.reshape(weights[k].shape)

    return (loss, gx[None], *[grad[k] for k in order], *[delta[k] for k in order],
            *[new_m[k] for k in order], *[new_v[k] for k in order])
```

```python
import functools
import math

import jax
import jax.numpy as jnp
from jax import lax
from jax.experimental import pallas as pl
from jax.experimental.pallas import tpu as pltpu

F32, BF16, I32 = jnp.float32, jnp.bfloat16, jnp.int32

D = 1024
FF = 2816
HD = 64
NH = 8
WA, WB = 768, 1536
WIN = WA + WB
BLK = 128
ALPHA = 2.0 ** 0.25
LN_EPS, RMS_EPS = 1e-5, 1e-6
SCALE = 1.0 / math.sqrt(HD)
A_MAX_DIST, B_MAX_DIST = 127, 128
B_DILATIONS = (1, 4, 16)
SLOPES = tuple(2.0 ** (-(i + 1)) for i in range(NH))
SHARD_ROWS = (WIN // 4, D // 4, 2 * FF // 4, FF // 4)
N_CHIPS = 4
ADAM_LR, ADAM_B1, ADAM_B2, ADAM_EPS, ADAM_WD, ADAM_STEP = 0.001, 0.9, 0.999, 1e-08, 0.01, 10
MESH = pl.DeviceIdType.MESH
ANY = pl.BlockSpec(memory_space=pl.ANY)
SMEM = pl.BlockSpec(memory_space=pltpu.SMEM)
VMEM = pl.BlockSpec(memory_space=pltpu.VMEM)
HBM = pl.BlockSpec(memory_space=pltpu.HBM)
SEM = pl.BlockSpec(memory_space=pltpu.SEMAPHORE)
DATAFLOW = pltpu.SideEffectType.DATAFLOW_SIDE_EFFECTING


def _cp(sem, mb=48):
    return pltpu.CompilerParams(dimension_semantics=sem, vmem_limit_bytes=mb << 20)


def _nn(a, b):
    return lax.dot_general(a, b, (((1,), (0,)), ((), ())), preferred_element_type=F32)


def _nt(a, b):
    return lax.dot_general(a, b, (((1,), (1,)), ((), ())), preferred_element_type=F32)


def _tn(a, b):
    return lax.dot_general(a, b, (((0,), (0,)), ((), ())), preferred_element_type=F32)


def _resident(shape):
    n = len(shape)
    return pl.BlockSpec(shape, lambda *_: (0,) * n, pipeline_mode=pl.Buffered(1))


def _const(shape):
    n = len(shape)
    return pl.BlockSpec(shape, lambda *_: (0,) * n)


def _proj(x, w_t, name, tm=512):
    s = x.shape[0]
    n = w_t.shape[0]

    def body(x_ref, w_ref, o_ref, xb_ref):
        xb = x_ref[...].astype(BF16)
        xb_ref[...] = xb
        res = _nt(xb, w_ref[...])
        for g in range(n // 128):
            o_ref[g] = res[:, 128 * g:128 * (g + 1)]

    return pl.pallas_call(
        body, name=name, grid=(s // tm,),
        in_specs=[pl.BlockSpec((tm, D), lambda i: (i, 0)), _resident((n, D))],
        out_specs=[pl.BlockSpec((n // 128, tm, 128), lambda i: (0, i, 0)), pl.BlockSpec((tm, D), lambda i: (i, 0))],
        out_shape=[jax.ShapeDtypeStruct((n // 128, s, 128), F32), jax.ShapeDtypeStruct((s, D), BF16)],
        compiler_params=_cp(("parallel",)),
    )(x, w_t)


def _grad_w(lhs, rhs, name, tm, tk=2048, lhs_halves=False):
    s = rhs.shape[0]
    if lhs_halves:
        per_half = lhs.shape[2] // tm
        n = 2 * lhs.shape[2]
        lhs_spec = pl.BlockSpec((None, tk, tm), lambda i, k: (i // per_half, k, i % per_half))
    else:
        n = lhs.shape[1]
        lhs_spec = pl.BlockSpec((tk, tm), lambda i, k: (k, i))
    nk = s // tk

    def body(l_ref, r_ref, o_ref, ob_ref):
        k = pl.program_id(1)

        @pl.when(k == 0)
        def _():
            o_ref[...] = jnp.zeros_like(o_ref)

        o_ref[...] += _tn(l_ref[...], r_ref[...])

        @pl.when(k == nk - 1)
        def _():
            ob_ref[...] = o_ref[...].astype(BF16)

    return pl.pallas_call(
        body, name=name, grid=(n // tm, nk),
        in_specs=[lhs_spec, pl.BlockSpec((tk, D), lambda i, k: (k, 0))],
        out_specs=[pl.BlockSpec((tm, D), lambda i, k: (i, 0))] * 2,
        out_shape=[pltpu.HBM((n, D), F32), pltpu.HBM((n, D), BF16)],
        compiler_params=_cp(("parallel", "arbitrary")),
    )(lhs, rhs)


def _band_base(max_dist, dist_unit, first):
    row = lax.broadcasted_iota(I32, (BLK, 2 * BLK), 0)
    col = lax.broadcasted_iota(I32, (BLK, 2 * BLK), 1)
    dist = BLK + row - col
    ok = (dist >= 0) & (dist <= max_dist)
    if first:
        ok = ok & (col >= BLK)
    return jnp.where(ok, dist.astype(F32) * (-float(dist_unit)), -jnp.inf)


def _half_mask(shape, e):
    lane = lax.broadcasted_iota(I32, shape, 1)
    return (lane < HD) if e == 0 else (lane >= HD)


def _to_half(x, e, g):
    if g != e:
        x = pltpu.roll(x, HD, 1)
    return jnp.where(_half_mask(x.shape, g), x, 0.0)


def _stack_heads(scalars, tile):
    return jnp.concatenate([scalars[0] * tile, scalars[1] * tile], axis=0)


def _pair_fwd(q2, kb, vb, base, slopes, kv_heads, sinks):
    lo = _half_mask((BLK, 2 * HD), 0)
    if slopes is None:
        bias = base
    elif sinks is None:
        bias = _stack_heads(slopes, base)
    else:
        col0 = lax.broadcasted_iota(I32, base.shape, 1) == 0
        bias = jnp.concatenate([jnp.where(col0, sinks[e], slopes[e] * base) for e in (0, 1)], axis=0)
    qs = jnp.concatenate([_to_half(q2, e, kv_heads[e]) * SCALE for e in (0, 1)], axis=0).astype(BF16)
    s = _nt(qs, kb) + bias
    m = jnp.max(s, axis=1, keepdims=True)
    p = jnp.exp(s - m)
    l = jnp.sum(p, axis=1, keepdims=True)
    o = _nn(p.astype(BF16), vb) / l
    lse = m + jnp.log(l)
    halves = []
    for e in (0, 1):
        oh = o[e * BLK:(e + 1) * BLK]
        halves.append(pltpu.roll(oh, HD, 1) if kv_heads[e] != e else oh)
    o2 = jnp.where(lo, halves[0], halves[1])
    lse2 = jnp.where(lo, jnp.broadcast_to(lse[:BLK], (BLK, 2 * HD)), jnp.broadcast_to(lse[BLK:], (BLK, 2 * HD)))
    return o2, lse2


def _pair_bwd(q2, kb, vb, do2, o2, lse2, base, slopes, kv_heads, sinks):
    lo = _half_mask((BLK, 2 * HD), 0)
    prod = do2 * o2
    lses, deltas = [], []
    for e in (0, 1):
        hq = _half_mask((BLK, 2 * HD), e)
        lses.append(jnp.max(jnp.where(hq, lse2, -jnp.inf), axis=1, keepdims=True))
        deltas.append(jnp.sum(jnp.where(hq, prod, 0.0), axis=1, keepdims=True))
    lse = jnp.concatenate(lses, axis=0)
    delta = jnp.concatenate(deltas, axis=0)
    qs = jnp.concatenate([_to_half(q2, e, kv_heads[e]) * SCALE for e in (0, 1)], axis=0).astype(BF16)
    dos = jnp.concatenate([_to_half(do2, e, kv_heads[e]) for e in (0, 1)], axis=0).astype(BF16)
    p = jnp.exp(_nt(qs, kb) + (base if slopes is None else _stack_heads(slopes, base)) - lse)
    ds = (p * (_nt(dos, vb) - delta)).astype(BF16)
    dq = _nn(ds, kb) * SCALE
    halves = []
    for e in (0, 1):
        dqh = dq[e * BLK:(e + 1) * BLK]
        halves.append(pltpu.roll(dqh, HD, 1) if kv_heads[e] != e else dqh)
    dq2 = jnp.where(lo, halves[0], halves[1])
    dk2 = _tn(ds, qs)
    dv2 = _tn(p.astype(BF16), dos)
    dsinks = []
    if sinks is not None:
        for e in (0, 1):
            dsinks.append(jnp.sum(-jnp.exp(sinks[e] - lses[e]) * deltas[e], axis=0, keepdims=True))
    return dq2, dk2, dv2, dsinks


A_BLOCKS_PER_STEP = 2
A_BLOCKS_PER_STEP_BWD = 1


def _attn_a_fwd(proj, sinks):
    s = proj.shape[1]
    nq = A_BLOCKS_PER_STEP
    rows = BLK * nq
    steps = s // rows

    def body(sink_ref, q_ref, kp_ref, kc_ref, vp_ref, vc_ref, o_ref, lse_ref):
        n = pl.program_id(0)
        base_rest = _band_base(A_MAX_DIST, 1, False)
        base_0 = jnp.where(n > 0, base_rest, _band_base(A_MAX_DIST, 1, True))
        for i in range(nq):
            cur = pl.ds(i * BLK, BLK)
            k_prev = kc_ref[pl.ds((i - 1) * BLK, BLK), :] if i > 0 else kp_ref[...]
            v_prev = vc_ref[pl.ds((i - 1) * BLK, BLK), :] if i > 0 else vp_ref[...]
            first_key = lax.broadcasted_iota(I32, (2 * BLK, 128), 0) == 0
            kb = jnp.where(first_key, 0.0, jnp.concatenate([k_prev, kc_ref[cur, :]], axis=0)).astype(BF16)
            vb = jnp.where(first_key, 0.0, jnp.concatenate([v_prev, vc_ref[cur, :]], axis=0)).astype(BF16)
            for j in range(NH // 2):
                g = j // 2
                o2, lse2 = _pair_fwd(q_ref[j, cur, :], kb, vb, base_rest if i > 0 else base_0,
                                     (SLOPES[2 * j], SLOPES[2 * j + 1]), (g, g), (sink_ref[2 * j], sink_ref[2 * j + 1]))
                o_ref[j, cur, :] = o2
                lse_ref[j, cur, :] = lse2

    before = lambda n: jnp.maximum(n * nq - 1, 0)
    slab = lambda g: pl.BlockSpec((None, rows, 128), lambda n: (g, n, 0))
    edge = lambda g: pl.BlockSpec((None, BLK, 128), lambda n: (g, before(n), 0))
    quad = pl.BlockSpec((4, rows, 128), lambda n: (0, n, 0))
    return pl.pallas_call(
        body, name="attn_a_fwd", grid=(steps,),
        in_specs=[SMEM, quad, edge(4), slab(4), edge(5), slab(5)],
        out_specs=[quad, quad],
        out_shape=[jax.ShapeDtypeStruct((4, s, 128), F32)] * 2,
        compiler_params=_cp(("parallel",)),
    )(sinks, proj, proj, proj, proj, proj)


def _attn_a_bwd(proj, sinks, d_o, o, lse):
    s = proj.shape[1]
    nq = A_BLOCKS_PER_STEP_BWD
    rows = BLK * nq
    steps = s // rows

    def body(sink_ref, q_ref, kp_ref, kc_ref, vp_ref, vc_ref, do_ref, o_ref, lse_ref,
             dq_ref, dk_ref, dv_ref, dsink_ref, kcar, vcar):
        n = pl.program_id(0)

        @pl.when(n == 0)
        def _():
            kcar[...] = jnp.zeros_like(kcar)
            vcar[...] = jnp.zeros_like(vcar)
            dsink_ref[...] = jnp.zeros_like(dsink_ref)

        dk_ref[...] = kcar[...].astype(BF16)
        dv_ref[...] = vcar[...].astype(BF16)

        @pl.when(n < steps)
        def _():
            base_rest = _band_base(A_MAX_DIST, 1, False)
            base_0 = jnp.where(n > 0, base_rest, _band_base(A_MAX_DIST, 1, True))
            for i in range(nq):
                cur = pl.ds(i * BLK, BLK)
                k_prev = kc_ref[pl.ds((i - 1) * BLK, BLK), :] if i > 0 else kp_ref[...]
                v_prev = vc_ref[pl.ds((i - 1) * BLK, BLK), :] if i > 0 else vp_ref[...]
                kb = jnp.concatenate([k_prev, kc_ref[cur, :]], axis=0).astype(BF16)
                vb = jnp.concatenate([v_prev, vc_ref[cur, :]], axis=0).astype(BF16)
                dk_win = dv_win = None
                for j in range(NH // 2):
                    g = j // 2
                    dq2, dk2, dv2, dsk = _pair_bwd(q_ref[j, cur, :], kb, vb, do_ref[j, cur, :], o_ref[j, cur, :],
                                                   lse_ref[j, cur, :], base_rest if i > 0 else base_0,
                                                   (SLOPES[2 * j], SLOPES[2 * j + 1]), (g, g),
                                                   (sink_ref[2 * j], sink_ref[2 * j + 1]))
                    dq_ref[j, cur, :] = dq2.astype(BF16)
                    dk_win = dk2 if j == 0 else dk_win + dk2
                    dv_win = dv2 if j == 0 else dv_win + dv2
                    for e in (0, 1):
                        h = 2 * j + e
                        dsink_ref[h:h + 1, :] += jnp.broadcast_to(dsk[e], (1, 128))
                if i == 0:
                    last = pl.ds((nq - 1) * BLK, BLK)
                    dk_ref[last, :] = (kcar[last, :] + dk_win[:BLK]).astype(BF16)
                    dv_ref[last, :] = (vcar[last, :] + dv_win[:BLK]).astype(BF16)
                else:
                    kcar[pl.ds((i - 1) * BLK, BLK), :] += dk_win[:BLK]
                    vcar[pl.ds((i - 1) * BLK, BLK), :] += dv_win[:BLK]
                kcar[cur, :] = dk_win[BLK:]
                vcar[cur, :] = dv_win[BLK:]

    cur_step = lambda n: jnp.minimum(n, steps - 1)
    before = lambda n: jnp.maximum(cur_step(n) * nq - 1, 0)
    out_prev = lambda n: jnp.maximum(n - 1, 0)
    quad = pl.BlockSpec((4, rows, 128), lambda n: (0, cur_step(n), 0))
    slab = lambda g: pl.BlockSpec((None, rows, 128), lambda n: (g, cur_step(n), 0))
    edge = lambda g: pl.BlockSpec((None, BLK, 128), lambda n: (g, before(n), 0))
    return pl.pallas_call(
        body, name="attn_a_bwd", grid=(steps + 1,),
        in_specs=[SMEM, quad, edge(4), slab(4), edge(5), slab(5), quad, quad, quad],
        out_specs=[quad,
                   pl.BlockSpec((rows, 128), lambda n: (out_prev(n), 0)),
                   pl.BlockSpec((rows, 128), lambda n: (out_prev(n), 0)),
                   pl.BlockSpec((NH, 128), lambda n: (0, 0))],
        out_shape=[pltpu.HBM((4, s, 128), BF16), pltpu.HBM((s, 128), BF16), pltpu.HBM((s, 128), BF16),
                   jax.ShapeDtypeStruct((NH, 128), F32)],
        scratch_shapes=[pltpu.VMEM((rows, 128), F32), pltpu.VMEM((rows, 128), F32)],
        compiler_params=_cp(("arbitrary",)),
    )(sinks, proj, proj, proj, proj, proj, d_o, o, lse)


def _stream(rho, i, r):
    start = i * BLK * r + rho
    return pl.ds(start, BLK, stride=r) if r > 1 else pl.ds(start, BLK)


def _for_streams(r, fn, side_by_side=4):
    if r <= side_by_side:
        for rho in range(r):
            fn(rho)
    else:
        def group(it, carry):
            for u in range(side_by_side):
                fn(side_by_side * it + u)
            return carry

        lax.fori_loop(0, r // side_by_side, group, 0)


B_BLOCKS_PER_STEP = {1: 16, 4: 4, 16: 1}
B_BLOCKS_PER_STEP_FWD = {1: 32, 4: 8, 16: 2}


def _attn_b_fwd(proj, slopes, r, so_far=None):
    s = proj.shape[1]
    nq = B_BLOCKS_PER_STEP_FWD[r]
    rows = BLK * r * nq
    steps = s // rows
    qc, kc, vc = WA // 128, WA // 128 + 4, WA // 128 + 8
    chained = so_far is not None

    def body(slope_ref, q_ref, kp_ref, kc_ref, vp_ref, vc_ref, *rest):
        po_ref, pl_ref = rest[:2] if chained else (None, None)
        o_ref, lse_ref = rest[-2:]
        j = pl.program_id(0)
        sb = pl.program_id(1)
        sl2 = (slope_ref[2 * j], slope_ref[2 * j + 1])
        bias_rest = _stack_heads(sl2, _band_base(B_MAX_DIST, r, False))
        bias_0 = jnp.where(sb > 0, bias_rest, _stack_heads(sl2, _band_base(B_MAX_DIST, r, True)))

        def stream(rho):
            for i in range(nq):
                cur = _stream(rho, i, r)
                k_prev = kc_ref[_stream(rho, i - 1, r), :] if i > 0 else kp_ref[_stream(rho, 0, r), :]
                v_prev = vc_ref[_stream(rho, i - 1, r), :] if i > 0 else vp_ref[_stream(rho, 0, r), :]
                kb = jnp.concatenate([k_prev, kc_ref[cur, :]], axis=0).astype(BF16)
                vb = jnp.concatenate([v_prev, vc_ref[cur, :]], axis=0).astype(BF16)
                o2, lse2 = _pair_fwd(q_ref[cur, :], kb, vb, bias_rest if i > 0 else bias_0, None, (0, 1), None)
                if chained:
                    lse1 = pl_ref[cur, :]
                    m = jnp.maximum(lse1, lse2)
                    e1, e2 = jnp.exp(lse1 - m), jnp.exp(lse2 - m)
                    den = e1 + e2
                    o2 = (e1 * po_ref[cur, :] + e2 * o2) * (1.0 / den)
                    lse2 = m + jnp.log(den)
                o_ref[cur, :] = o2
                lse_ref[cur, :] = lse2

        _for_streams(r, stream, side_by_side=16)

    before = lambda sb: jnp.maximum(sb * nq - 1, 0)
    result = pl.BlockSpec((None, rows, 128), lambda j, sb: (j, sb, 0))
    return pl.pallas_call(
        body, name=f"attn_b_fwd_r{r}", grid=(NH // 2, steps),
        in_specs=[SMEM,
                  pl.BlockSpec((None, rows, 128), lambda j, sb: (qc + j, sb, 0)),
                  pl.BlockSpec((None, BLK * r, 128), lambda j, sb: (kc + j, before(sb), 0)),
                  pl.BlockSpec((None, rows, 128), lambda j, sb: (kc + j, sb, 0)),
                  pl.BlockSpec((None, BLK * r, 128), lambda j, sb: (vc + j, before(sb), 0)),
                  pl.BlockSpec((None, rows, 128), lambda j, sb: (vc + j, sb, 0))] + ([result] * 2 if chained else []),
        out_specs=[result] * 2,
        out_shape=[jax.ShapeDtypeStruct((4, s, 128), F32)] * 2,
        compiler_params=_cp(("parallel", "parallel")),
    )(slopes, proj, proj, proj, proj, proj, *(so_far if chained else ()))


def _attn_b_bwd(proj, slopes, d_o, o, lse, r, so_far=None, dtype=F32):
    s = proj.shape[1]
    nq = B_BLOCKS_PER_STEP[r]
    rows = BLK * r * nq
    steps = s // rows
    qc, kc, vc = WA // 128, WA // 128 + 4, WA // 128 + 8
    chained = so_far is not None

    def body(slope_ref, q_ref, kp_ref, kc_ref, vp_ref, vc_ref, do_ref, o_ref, lse_ref, *rest):
        pq_ref, pk_ref, pv_ref = rest[:3] if chained else (None, None, None)
        dq_ref, dk_ref, dv_ref, kcar, vcar = rest[-5:]
        j = pl.program_id(0)
        sb = pl.program_id(1)

        @pl.when(sb == 0)
        def _():
            kcar[...] = jnp.zeros_like(kcar)
            vcar[...] = jnp.zeros_like(vcar)

        def settled(car, p_ref, idx):
            return car[idx] + p_ref[idx] if chained else car[idx]

        dk_ref[...] = settled(kcar, pk_ref, ...).astype(dtype)
        dv_ref[...] = settled(vcar, pv_ref, ...).astype(dtype)

        @pl.when(sb < steps)
        def _():
            sl2 = (slope_ref[2 * j], slope_ref[2 * j + 1])
            bias_rest = _stack_heads(sl2, _band_base(B_MAX_DIST, r, False))
            bias_0 = jnp.where(sb > 0, bias_rest, _stack_heads(sl2, _band_base(B_MAX_DIST, r, True)))

            def stream(rho):
                for i in range(nq):
                    cur = _stream(rho, i, r)
                    k_prev = kc_ref[_stream(rho, i - 1, r), :] if i > 0 else kp_ref[_stream(rho, 0, r), :]
                    v_prev = vc_ref[_stream(rho, i - 1, r), :] if i > 0 else vp_ref[_stream(rho, 0, r), :]
                    kb = jnp.concatenate([k_prev, kc_ref[cur, :]], axis=0).astype(BF16)
                    vb = jnp.concatenate([v_prev, vc_ref[cur, :]], axis=0).astype(BF16)
                    dq2, dk2, dv2, _ = _pair_bwd(q_ref[cur, :], kb, vb, do_ref[cur, :], o_ref[cur, :], lse_ref[cur, :],
                                                 bias_rest if i > 0 else bias_0, None, (0, 1), None)
                    dq_ref[cur, :] = (dq2 + pq_ref[cur, :] if chained else dq2).astype(dtype)
                    if i == 0:
                        last = (_stream(rho, nq - 1, r), slice(None))
                        dk_ref[last] = (settled(kcar, pk_ref, last) + dk2[:BLK]).astype(dtype)
                        dv_ref[last] = (settled(vcar, pv_ref, last) + dv2[:BLK]).astype(dtype)
                    else:
                        kcar[_stream(rho, i - 1, r), :] += dk2[:BLK]
                        vcar[_stream(rho, i - 1, r), :] += dv2[:BLK]
                    kcar[cur, :] = dk2[BLK:]
                    vcar[cur, :] = dv2[BLK:]

            _for_streams(r, stream, side_by_side=8)

    cur_step = lambda sb: jnp.minimum(sb, steps - 1)
    before = lambda sb: jnp.maximum(cur_step(sb) * nq - 1, 0)
    out_prev = lambda sb: jnp.maximum(sb - 1, 0)
    tile = lambda slab: pl.BlockSpec((None, rows, 128), lambda j, sb: (slab + j, cur_step(sb), 0))
    edge = lambda slab: pl.BlockSpec((None, BLK * r, 128), lambda j, sb: (slab + j, before(sb), 0))
    late = pl.BlockSpec((None, rows, 128), lambda j, sb: (j, out_prev(sb), 0))
    grads = [tile(0), late, late]
    return pl.pallas_call(
        body, name=f"attn_b_bwd_r{r}", grid=(NH // 2, steps + 1),
        in_specs=[SMEM, tile(qc), edge(kc), tile(kc), edge(vc), tile(vc), tile(0), tile(0), tile(0)]
        + (grads if chained else []),
        out_specs=grads,
        out_shape=[pltpu.HBM((4, s, 128), dtype)] * 3,
        scratch_shapes=[pltpu.VMEM((rows, 128), F32), pltpu.VMEM((rows, 128), F32)],
        compiler_params=_cp(("parallel", "arbitrary")),
    )(slopes, proj, proj, proj, proj, proj, d_o, o, lse, *(so_far if chained else ()))


def _row(v):
    return v.reshape(1, -1)


def _layer_norm_stats(z):
    mu = jnp.mean(z, axis=-1, keepdims=True)
    zc = z - mu
    var = jnp.mean(zc * zc, axis=-1, keepdims=True)
    rstd = lax.rsqrt(var + LN_EPS)
    return zc * rstd, rstd


def _layer_norm_bwd(dh, zh, rstd, g):
    dzh = dh * g
    return rstd * (dzh - jnp.mean(dzh, axis=-1, keepdims=True) - zh * jnp.mean(dzh * zh, axis=-1, keepdims=True))


def _rms(o):
    return lax.rsqrt(jnp.mean(o * o, axis=-1, keepdims=True) + RMS_EPS)


def _mix_ln1(x, o_a, o_b, norm_a_g, norm_b_g, w_o, ln1_g, ln1_b, tm=512):
    s = x.shape[0]

    def wide(ref):
        return jnp.concatenate([ref[j] for j in range(4)], axis=1)

    def body(x_ref, oa_ref, ob_ref, ga_ref, gb_ref, wo_ref, g_ref, b_ref, cat_ref, z1_ref, h1_ref, h1b_ref):
        oa, ob = wide(oa_ref), wide(ob_ref)
        na = oa * _rms(oa) * ga_ref[...]
        nb_ = ob * _rms(ob) * gb_ref[...]
        cat = jnp.concatenate([na, nb_], axis=1).astype(BF16)
        cat_ref[...] = cat
        z1 = ALPHA * x_ref[...] + _nn(cat, wo_ref[...])
        z1_ref[...] = z1
        zh, _ = _layer_norm_stats(z1)
        h1 = zh * g_ref[...] + b_ref[...]
        h1_ref[...] = h1
        h1b_ref[...] = h1.astype(BF16)

    t512 = pl.BlockSpec((4, tm, 128), lambda i: (0, i, 0))
    td = pl.BlockSpec((tm, D), lambda i: (i, 0))
    return pl.pallas_call(
        body, name="mix_ln1", grid=(s // tm,),
        in_specs=[td] + [t512] * 2 + [_const((1, 512))] * 2 + [_resident((D, D))] + [_const((1, D))] * 2,
        out_specs=[td, td, td, td],
        out_shape=[jax.ShapeDtypeStruct((s, D), BF16), jax.ShapeDtypeStruct((s, D), F32),
                   jax.ShapeDtypeStruct((s, D), F32), jax.ShapeDtypeStruct((s, D), BF16)],
        compiler_params=_cp(("parallel",)),
    )(x, o_a, o_b, _row(norm_a_g), _row(norm_b_g), w_o, _row(ln1_g), _row(ln1_b))


def _gelu_and_grad(x):
    c = math.sqrt(2.0 / math.pi)
    x2 = x * x
    s = 0.5 * jnp.tanh(x * ((c * 0.044715) * x2 + c)) + 0.5
    dg = s + (x * ((6.0 * c * 0.044715) * x2 + 2.0 * c)) * (s - s * s)
    return x * s, dg


def _shifted(u, edge, row, down):
    groups = [u[8 * i:8 * i + 8] for i in range(u.shape[0] // 8)]
    others = [edge] + groups[:-1] if down else groups[1:] + [edge]
    moved = []
    for k in (1, 2):
        crossing = row >= 8 - k if down else row < k
        moved.append(jnp.concatenate([pltpu.roll(jnp.where(crossing, o, g), k if down else 8 - k, 0)
                                      for o, g in zip(others, groups)], axis=0))
    return moved


def _up_proj(h1b, w_up, tm=512):
    s = h1b.shape[0]

    def body(h_ref, w_ref, o_ref):
        h = h_ref[...]
        for half in (0, 1):
            o_ref[half] = _nn(h, w_ref[:, half * FF:(half + 1) * FF]).astype(BF16)

    return pl.pallas_call(
        body, name="up_proj", grid=(s // tm,),
        in_specs=[pl.BlockSpec((tm, D), lambda i: (i, 0)), _resident((D, 2 * FF))],
        out_specs=pl.BlockSpec((2, tm, FF), lambda i: (0, i, 0)),
        out_shape=jax.ShapeDtypeStruct((2, s, FF), BF16),
        compiler_params=_cp(("parallel",)),
    )(h1b, w_up)


def _conv_gelu(up, cwb, tm=512, tn=FF // 2, chunk_rows=16):
    s = up.shape[1]
    n_c = tm // chunk_rows

    def body(up_ref, c_ref, a_ref, g_ref, a1_ref, carry):
        @pl.when(pl.program_id(1) == 0)
        def _():
            carry[...] = jnp.zeros_like(carry)

        row = lax.broadcasted_iota(jnp.int32, (8, tn), 0)
        edge = [carry[0], carry[1]]
        for c in range(n_c):
            rows = pl.ds(c * chunk_rows, chunk_rows)
            u = []
            for half in (0, 1):
                x = up_ref[half, rows, :].astype(F32)
                r1, r2 = _shifted(x, edge[half], row, True)
                u.append(r2 * c_ref[0, half:half + 1, :] + r1 * c_ref[1, half:half + 1, :]
                         + x * c_ref[2, half:half + 1, :] + c_ref[3, half:half + 1, :])
                edge[half] = x[chunk_rows - 8:]
            g, dg = _gelu_and_grad(u[0])
            a_ref[rows, :] = (g * u[1]).astype(BF16)
            g_ref[rows, :] = g.astype(BF16)
            a1_ref[rows, :] = (u[1] * dg).astype(BF16)
        for half in (0, 1):
            carry[half] = edge[half]

    pair = pl.BlockSpec((2, tm, tn), lambda j, i: (0, i, j))
    tile = pl.BlockSpec((tm, tn), lambda j, i: (i, j))
    return pl.pallas_call(
        body, name="conv_gelu", grid=(FF // tn, s // tm),
        in_specs=[pair, pl.BlockSpec((4, 2, tn), lambda j, i: (0, 0, j))],
        out_specs=[tile, tile, tile],
        out_shape=[jax.ShapeDtypeStruct((s, FF), BF16)] * 3,
        scratch_shapes=[pltpu.VMEM((2, 8, tn), F32)],
        compiler_params=_cp(("parallel", "arbitrary")),
    )(up, cwb)


def _down_ln2_loss(a, w_down, h1, target, ln2_g, ln2_b, tm=512):
    s = a.shape[0]

    def body(a_ref, w_ref, h_ref, t_ref, g_ref, b_ref, dz_ref, dzb_ref, st_ref):
        @pl.when(pl.program_id(0) == 0)
        def _():
            st_ref[...] = jnp.zeros_like(st_ref)

        z2 = ALPHA * h_ref[...] + _nn(a_ref[...], w_ref[...])
        zh, rstd = _layer_norm_stats(z2)
        diff = zh * g_ref[...] + b_ref[...] - t_ref[...]
        part = 0.5 * jnp.sum(jnp.mean(diff * diff, axis=-1, keepdims=True), axis=0, keepdims=True)
        dy = diff * (1.0 / D)
        st_ref[0:1, :] += jnp.sum(dy * zh, axis=0, keepdims=True)
        st_ref[1:2, :] += jnp.sum(dy, axis=0, keepdims=True)
        st_ref[2:3, :] += jnp.broadcast_to(part, (1, D))
        dz = _layer_norm_bwd(dy, zh, rstd, g_ref[...])
        dz_ref[...] = dz
        dzb_ref[...] = dz.astype(BF16)

    td = pl.BlockSpec((tm, D), lambda i: (i, 0))
    return pl.pallas_call(
        body, name="down_ln2_loss", grid=(s // tm,),
        in_specs=[pl.BlockSpec((tm, FF), lambda i: (i, 0)), _resident((FF, D)), td, td, _const((1, D)), _const((1, D))],
        out_specs=[td, td, _const((8, D))],
        out_shape=[jax.ShapeDtypeStruct((s, D), F32), jax.ShapeDtypeStruct((s, D), BF16),
                   jax.ShapeDtypeStruct((8, D), F32)],
        compiler_params=_cp(("arbitrary",)),
    )(a, w_down, h1, target, _row(ln2_g), _row(ln2_b))


def _d_act(dz2b, w_down, tm=512):
    s = dz2b.shape[0]

    def body(dz_ref, w_ref, o_ref):
        o_ref[...] = _nt(dz_ref[...], w_ref[...]).astype(BF16)

    return pl.pallas_call(
        body, name="d_act", grid=(s // tm,),
        in_specs=[pl.BlockSpec((tm, D), lambda i: (i, 0)), _resident((FF, D))],
        out_specs=pl.BlockSpec((tm, FF), lambda i: (i, 0)),
        out_shape=jax.ShapeDtypeStruct((s, FF), BF16),
        compiler_params=_cp(("parallel",)),
    )(dz2b, w_down)


def _conv_gelu_bwd(da, up, g, a1, cwb, tm=512, tn=FF // 2, chunk_rows=16):
    s = da.shape[0]
    n_i = s // tm
    n_c = tm // chunk_rows

    def body(da_ref, up_ref, g_ref, a1_ref, c_ref, dup_ref, dc_ref, carry):
        @pl.when(pl.program_id(1) == 0)
        def _():
            carry[...] = jnp.zeros_like(carry)
            dc_ref[...] = jnp.zeros_like(dc_ref)

        def fold(v):
            return jnp.sum(v.reshape(chunk_rows // 8, 8, v.shape[1]), axis=0)

        def chunk(cc, state):
            after, sums = state
            rows = pl.ds((n_c - 1 - cc) * chunk_rows, chunk_rows)
            da_c = da_ref[rows, :].astype(F32)
            dus = (da_c * a1_ref[rows, :].astype(F32), da_c * g_ref[rows, :].astype(F32))
            head, new_sums = [], []
            for half in (0, 1):
                du = dus[half]
                up = up_ref[half, rows, :].astype(F32)
                l1, l2 = _shifted(du, after[half], row, False)
                dup = (du * c_ref[2, half:half + 1, :] + l1 * c_ref[1, half:half + 1, :]
                       + l2 * c_ref[0, half:half + 1, :])
                dup_ref[half, rows, :] = dup.astype(BF16)
                parts = (fold(l2 * up), fold(l1 * up), fold(du * up), fold(du))
                new_sums.append(parts if sums is None else tuple(a + b for a, b in zip(sums[half], parts)))
                head.append(du[:8])
            return tuple(head), new_sums

        row = lax.broadcasted_iota(jnp.int32, (8, tn), 0)
        state = ((carry[0], carry[1]), None)
        for cc in range(n_c):
            state = chunk(cc, state)
        head, sums = state
        for half in (0, 1):
            carry[half] = head[half]
            for k in range(4):
                dc_ref[k, half:half + 1, :] += jnp.sum(sums[half][k], axis=0, keepdims=True)

    rev = lambda ii: n_i - 1 - ii
    tile = pl.BlockSpec((tm, tn), lambda j, ii: (rev(ii), j))
    pair = pl.BlockSpec((2, tm, tn), lambda j, ii: (0, rev(ii), j))
    per_col = pl.BlockSpec((4, 2, tn), lambda j, ii: (0, 0, j))
    return pl.pallas_call(
        body, name="conv_gelu_bwd", grid=(FF // tn, n_i),
        in_specs=[tile, pair, tile, tile, per_col],
        out_specs=[pair, per_col],
        out_shape=[jax.ShapeDtypeStruct((2, s, FF), BF16), jax.ShapeDtypeStruct((4, 2, FF), F32)],
        scratch_shapes=[pltpu.VMEM((2, 8, tn), F32)],
        compiler_params=_cp(("parallel", "arbitrary")),
    )(da, up, g, a1, cwb)


def _dh1_ln1_bwd(dz2, dup, w_up, z1, ln1_g, tm=512):
    s = dz2.shape[0]

    def body(dz2_ref, dup_ref, w_ref, z1_ref, g_ref, dz1_ref, dz1b_ref, st_ref):
        @pl.when(pl.program_id(0) == 0)
        def _():
            st_ref[...] = jnp.zeros_like(st_ref)

        dh = ALPHA * dz2_ref[...] + _nt(dup_ref[0], w_ref[:, :FF]) + _nt(dup_ref[1], w_ref[:, FF:])
        zh, rstd = _layer_norm_stats(z1_ref[...])
        st_ref[0:1, :] += jnp.sum(dh * zh, axis=0, keepdims=True)
        st_ref[1:2, :] += jnp.sum(dh, axis=0, keepdims=True)
        dz = _layer_norm_bwd(dh, zh, rstd, g_ref[...])
        dz1_ref[...] = dz
        dz1b_ref[...] = dz.astype(BF16)

    td = pl.BlockSpec((tm, D), lambda i: (i, 0))
    return pl.pallas_call(
        body, name="dh1_ln1_bwd", grid=(s // tm,),
        in_specs=[td, pl.BlockSpec((2, tm, FF), lambda i: (0, i, 0)), _resident((D, 2 * FF)), td, _const((1, D))],
        out_specs=[td, td, _const((8, D))],
        out_shape=[jax.ShapeDtypeStruct((s, D), F32), jax.ShapeDtypeStruct((s, D), BF16),
                   jax.ShapeDtypeStruct((8, D), F32)],
        compiler_params=_cp(("arbitrary",), 58),
    )(dz2, dup, w_up, z1, _row(ln1_g))


def _dcat_rms_bwd(dz1b, w_o, o_a, o_b, norm_a_g, norm_b_g, tm=512):
    s = dz1b.shape[0]

    def body(dz_ref, w_ref, oa_ref, ob_ref, ga_ref, gb_ref, da_ref, db_ref, st_ref):
        @pl.when(pl.program_id(0) == 0)
        def _():
            st_ref[...] = jnp.zeros_like(st_ref)

        dcat = _nt(dz_ref[...], w_ref[...])
        for k, (o_ref, g_ref, d_ref) in enumerate(((oa_ref, ga_ref, da_ref), (ob_ref, gb_ref, db_ref))):
            o = jnp.concatenate([o_ref[j] for j in range(4)], axis=1)
            dn = dcat[:, 512 * k:512 * (k + 1)]
            rr = _rms(o)
            oh = o * rr
            st_ref[k:k + 1, :] += jnp.sum(dn * oh, axis=0, keepdims=True)
            doh = dn * g_ref[...]
            d_o = rr * (doh - oh * jnp.mean(doh * oh, axis=-1, keepdims=True))
            for j in range(4):
                d_ref[j] = d_o[:, 128 * j:128 * (j + 1)]

    t512 = pl.BlockSpec((4, tm, 128), lambda i: (0, i, 0))
    return pl.pallas_call(
        body, name="dcat_rms_bwd", grid=(s // tm,),
        in_specs=[pl.BlockSpec((tm, D), lambda i: (i, 0)), _resident((D, D)), t512, t512,
                  _const((1, 512)), _const((1, 512))],
        out_specs=[t512, t512, _const((8, 512))],
        out_shape=[jax.ShapeDtypeStruct((4, s, 128), F32), jax.ShapeDtypeStruct((4, s, 128), F32),
                   jax.ShapeDtypeStruct((8, 512), F32)],
        compiler_params=_cp(("arbitrary",)),
    )(dz1b, w_o, o_a, o_b, _row(norm_a_g), _row(norm_b_g))


def _grad_w_in(dparts, xb, tk=2048):
    s = xb.shape[0]
    nk = s // tk

    def body(qa, ka, va, qb, kb, vb, x_ref, o_ref, ob_ref):
        i = pl.program_id(0)
        k = pl.program_id(1)

        @pl.when(k == 0)
        def _():
            o_ref[...] = jnp.zeros_like(o_ref)

        def add(blocks):
            o_ref[...] += _tn(jnp.concatenate(blocks, axis=1), x_ref[...])

        pl.when(i == 0)(lambda: add([qa[j] for j in range(4)] + [ka[...], va[...]]))
        pl.when(i == 1)(lambda: add([qb[j] for j in range(4)] + [kb[0], kb[1]]))
        pl.when(i == 2)(lambda: add([kb[0], kb[1]] + [vb[j] for j in range(4)]))

        @pl.when(k == nk - 1)
        def _():
            ob_ref[...] = o_ref[...].astype(BF16)

    def during(tile):
        return lambda i, k: jnp.where(i == tile, k, jnp.where(i < tile, 0, nk - 1))

    quad = lambda tile: pl.BlockSpec((4, tk, 128), lambda i, k: (0, during(tile)(i, k), 0))
    one = pl.BlockSpec((tk, 128), lambda i, k: (during(0)(i, k), 0))
    kb_spec = pl.BlockSpec((2, tk, 128), lambda i, k: (jnp.where(i == 2, 1, 0), jnp.where(i == 0, 0, k), 0))
    return pl.pallas_call(
        body, name="grad_w_in", grid=(3, nk),
        in_specs=[quad(0), one, one, quad(1), kb_spec, quad(2), pl.BlockSpec((tk, D), lambda i, k: (k, 0))],
        out_specs=[pl.BlockSpec((WA, D), lambda i, k: (i, 0))] * 2,
        out_shape=[pltpu.HBM((WIN, D), F32), pltpu.HBM((WIN, D), BF16)],
        compiler_params=_cp(("parallel", "arbitrary"), mb=56),
    )(*dparts, xb)


def _grad_x(dz1, dparts, w_in_t, zero, tm=512):
    s = dz1.shape[0]

    def body(dz_ref, qa, ka, va, qb, kb, vb, w_ref, z_ref, o_ref):
        dp = jnp.concatenate([qa[j] for j in range(4)] + [ka[...], va[...]]
                             + [ref[j] for ref in (qb, kb, vb) for j in range(4)], axis=1)
        o_ref[...] = ALPHA * dz_ref[...] + _nn(dp, w_ref[...]) + z_ref[0:1, 0:1]

    td = pl.BlockSpec((tm, D), lambda i: (i, 0))
    quad = pl.BlockSpec((4, tm, 128), lambda i: (0, i, 0))
    one = pl.BlockSpec((tm, 128), lambda i: (i, 0))
    return pl.pallas_call(
        body, name="grad_x", grid=(s // tm,),
        in_specs=[td, quad, one, one, quad, quad, quad, _resident((WIN, D)), _const((8, 128))],
        out_specs=td, out_shape=jax.ShapeDtypeStruct((s, D), F32),
        compiler_params=_cp(("parallel",)),
    )(dz1, *dparts, w_in_t, zero)


def _place():
    return lax.axis_index("x"), lax.axis_index("y"), lax.axis_index("c")


def _other_chips(x, y):
    return [(1 - x, y), (x, 1 - y), (1 - x, 1 - y)]


def _hbm(a):
    return pltpu.with_memory_space_constraint(a, pltpu.HBM)


def _gather_w_in(shard, conv_w):
    rows_k = shard.shape[0]
    half = rows_k // 2

    def body(src, conv_src, out, conv_out, send_sems, recv_sems):
        x, y, c = _place()
        b = 2 * x + y
        sibling = (x, y, 1 - c)
        chips = _other_chips(x, y)

        def copy(idx, chip_b, core, to, first_hop=False):
            rows = out.at[pl.ds(pl.multiple_of(chip_b * rows_k + core * half, 16), half)]
            s_ref = src.at[pl.ds(pl.multiple_of(core * half, 16), half)] if first_hop else rows
            return pltpu.make_async_remote_copy(src_ref=s_ref, dst_ref=rows, send_sem=send_sems.at[idx],
                                                recv_sem=recv_sems.at[idx], device_id=to, device_id_type=MESH)

        def own_copy():
            return pltpu.make_async_remote_copy(
                src_ref=src, dst_ref=out.at[pl.ds(pl.multiple_of(b * rows_k, 16), rows_k)], send_sem=send_sems.at[6],
                recv_sem=recv_sems.at[6], device_id=sibling, device_id_type=MESH)

        def conv_copy(idx, chip_b, to):
            return pltpu.make_async_remote_copy(src_ref=conv_src, dst_ref=conv_out.at[chip_b],
                                                send_sem=send_sems.at[7 + idx], recv_sem=recv_sems.at[7 + idx],
                                                device_id=to, device_id_type=MESH)

        started = [own_copy(), conv_copy(3, b, sibling)]
        for jn, chip in enumerate(chips):
            started += [copy(jn, b, c, (chip[0], chip[1], c), first_hop=True), conv_copy(jn, b, (chip[0], chip[1], c))]
        for cp in started:
            cp.start()
        for jn, chip in enumerate(chips):
            cb = 2 * chip[0] + chip[1]
            copy(jn, cb, c, (chip[0], chip[1], c)).wait_recv()
            cp = copy(3 + jn, cb, c, sibling)
            cp.start()
            started.append(cp)
        for jn, chip in enumerate(chips):
            cb = 2 * chip[0] + chip[1]
            copy(3 + jn, cb, 1 - c, sibling).wait_recv()
            conv_copy(jn, cb, (chip[0], chip[1], c)).wait_recv()
        own_copy().wait_recv()
        conv_copy(3, b, sibling).wait_recv()
        for cp in started:
            cp.wait_send()

    return pl.pallas_call(
        body, name="gather_w_in",
        in_specs=[ANY, ANY], out_specs=[ANY, ANY],
        out_shape=[jax.ShapeDtypeStruct((N_CHIPS * rows_k, D), BF16), jax.ShapeDtypeStruct((N_CHIPS,) + conv_w.shape, F32)],
        scratch_shapes=[pltpu.SemaphoreType.DMA((11,)), pltpu.SemaphoreType.DMA((11,))],
        compiler_params=pltpu.CompilerParams(has_side_effects=True),
    )(shard, conv_w)


def _weight_copies(shard, land, send_sems, recv_sems, arrivals):
    x, y, c = _place()
    n_rows, n_cols = shard.shape
    peers = [(px, py, c) for px, py in _other_chips(x, y)] + [(x, y, 1 - c)]
    cps = []
    for jn, peer in enumerate(peers):
        at = 2 * peer[0] + peer[1] if arrivals else 2 * x + y
        if land.shape[1] == n_cols:
            dst = land.at[pl.ds(pl.multiple_of(at * n_rows, 16), n_rows)]
        else:
            dst = land.at[:, pl.ds(pl.multiple_of(at * n_cols, 128), n_cols)]
        cps.append(pltpu.make_async_remote_copy(src_ref=shard, dst_ref=dst, send_sem=send_sems.at[jn],
                                                recv_sem=recv_sems.at[jn], device_id=peer, device_id_type=MESH))
    return cps


def _weights_start(shards, after):
    n = len(shards)
    lands = [lax.empty((N_CHIPS * sh.shape[0], D) if sh.shape[1] == D else (D, N_CHIPS * sh.shape[1]), BF16)
             for sh in shards]

    def body(*refs):
        src, land = refs[:n], refs[n:2 * n]
        send_sems, recv_sems = refs[2 * n + 1:3 * n + 1], refs[3 * n + 1:4 * n + 1]
        for k in range(n):
            for send in _weight_copies(src[k], land[k], send_sems[k], recv_sems[k], False):
                send.start()
        refs[-1][...] = jnp.zeros_like(refs[-1])

    res = pl.pallas_call(
        body, name="weights_start",
        in_specs=[HBM] * (2 * n) + [ANY], out_specs=[SEM] * (2 * n) + [HBM] * (2 * n) + [VMEM],
        out_shape=[pltpu.SemaphoreType.DMA((4,))] * (2 * n)
        + [pltpu.HBM(a.shape, a.dtype) for a in (*shards, *lands)] + [jax.ShapeDtypeStruct((8, 128), F32)],
        input_output_aliases={i: i + 2 * n for i in range(2 * n)},
        compiler_params=pltpu.CompilerParams(has_side_effects=DATAFLOW),
    )(*[_hbm(a) for a in (*shards, *lands)], after)
    return [(res[k], res[n + k], res[2 * n + k], res[3 * n + k]) for k in range(n)], res[-1]


def _weights_wait(started, after, name):
    send_sems, recv_sems, shard, land = started

    def body(s_ref, l_ref, send_ref, recv_ref, after_ref, s_out, l_out):
        for cp in _weight_copies(s_ref, l_ref, send_ref, recv_ref, True):
            cp.wait_send()
            cp.wait_recv()

    return pl.pallas_call(
        body, name=name,
        in_specs=[HBM, HBM, SEM, SEM, ANY], out_specs=[HBM, HBM],
        out_shape=[pltpu.HBM(shard.shape, shard.dtype), pltpu.HBM(land.shape, land.dtype)],
        input_output_aliases={0: 0, 1: 1},
        compiler_params=pltpu.CompilerParams(has_side_effects=DATAFLOW),
    )(shard, land, send_sems, recv_sems, after)[1]


def _grad_copies(g_ref, land_ref, send_sems, recv_sems):
    x, y, c = _place()
    cps = []
    for d in range(1, 8):
        px, py, pc = x ^ (d >> 2), y ^ ((d >> 1) & 1), c ^ (d & 1)
        cps.append(pltpu.make_async_remote_copy(
            src_ref=g_ref.at[2 * px + py, pc], dst_ref=land_ref.at[d - 1], send_sem=send_sems.at[d - 1],
            recv_sem=recv_sems.at[d - 1], device_id=(px, py, pc), device_id_type=MESH))
    return cps


def _grads_start(grads_b, name):
    n = len(grads_b)
    lands = [lax.empty((7, g.shape[2], D), BF16) for g in grads_b]

    def body(*refs):
        g, land = refs[:n], refs[n:2 * n]
        send_sems, recv_sems = refs[2 * n:3 * n], refs[3 * n:4 * n]
        for k in range(n):
            for cp in _grad_copies(g[k], land[k], send_sems[k], recv_sems[k]):
                cp.start()
        refs[-1][...] = jnp.zeros_like(refs[-1])

    res = pl.pallas_call(
        body, name=name,
        in_specs=[HBM] * (2 * n), out_specs=[SEM] * (2 * n) + [HBM] * (2 * n) + [VMEM],
        out_shape=[pltpu.SemaphoreType.DMA((7,))] * (2 * n)
        + [pltpu.HBM(a.shape, a.dtype) for a in (*grads_b, *lands)] + [jax.ShapeDtypeStruct((8, 128), F32)],
        input_output_aliases={i: i + 2 * n for i in range(2 * n)},
        compiler_params=pltpu.CompilerParams(has_side_effects=DATAFLOW),
    )(*[_hbm(a) for a in (*grads_b, *lands)])
    return [(res[k], res[n + k], res[2 * n + k], res[3 * n + k]) for k in range(n)], res[-1]


def _grads_wait(started, after, name):
    n = len(started)

    def body(*refs):
        g, land = refs[:n], refs[n:2 * n]
        send_sems, recv_sems = refs[2 * n:3 * n], refs[3 * n:4 * n]
        for k in range(n):
            for cp in _grad_copies(g[k], land[k], send_sems[k], recv_sems[k]):
                cp.wait_send()
                cp.wait_recv()

    gs = [st[2] for st in started]
    lands = [st[3] for st in started]
    res = pl.pallas_call(
        body, name=name,
        in_specs=[HBM] * (2 * n) + [SEM] * (2 * n) + [ANY], out_specs=[HBM] * (2 * n),
        out_shape=[pltpu.HBM(a.shape, a.dtype) for a in (*gs, *lands)],
        input_output_aliases={i: i for i in range(2 * n)},
        compiler_params=pltpu.CompilerParams(has_side_effects=DATAFLOW),
    )(*gs, *lands, *[st[0] for st in started], *[st[1] for st in started], after)
    return res[n:]


def _sum_partials(grad4, got, cb, name, tr):
    h = grad4.shape[2]
    per_half = h // tr

    def body(cb_ref, g_ref, o_ref, out_ref):
        acc = g_ref[...]
        for j in range(7):
            acc = acc + o_ref[j].astype(F32)
        out_ref[...] = acc

    return pl.pallas_call(
        body, name=name,
        grid_spec=pltpu.PrefetchScalarGridSpec(
            num_scalar_prefetch=1, grid=(per_half,),
            in_specs=[pl.BlockSpec((None, None, tr, D), lambda i, cb_ref: (cb_ref[1], cb_ref[0], i, 0)),
                      pl.BlockSpec((7, tr, D), lambda i, cb_ref: (0, i, 0))],
            out_specs=pl.BlockSpec((tr, D), lambda i, cb_ref: (cb_ref[0] * per_half + i, 0))),
        out_shape=pltpu.HBM((2 * h, D), F32),
        compiler_params=_cp(("arbitrary",)),
    )(cb, grad4, _hbm(got))


def _swap_halves(shards, name):
    n = len(shards)

    def body(*refs):
        out, send_sems, recv_sems = refs[n:2 * n], refs[2 * n], refs[2 * n + 1]
        x, y, c = _place()
        cps = []
        for k in range(n):
            h = shards[k].shape[0] // 2
            mine = out[k].at[pl.ds(pl.multiple_of(c * h, 8), h)]
            cp = pltpu.make_async_remote_copy(src_ref=mine, dst_ref=mine, send_sem=send_sems.at[k],
                                              recv_sem=recv_sems.at[k], device_id=(x, y, 1 - c), device_id_type=MESH)
            cp.start()
            cps.append(cp)
        for cp in cps:
            cp.wait()

    return pl.pallas_call(
        body, name=name,
        in_specs=[ANY] * n, out_specs=[ANY] * n,
        out_shape=[jax.ShapeDtypeStruct(sh.shape, F32) for sh in shards],
        input_output_aliases={k: k for k in range(n)},
        scratch_shapes=[pltpu.SemaphoreType.DMA((n,)), pltpu.SemaphoreType.DMA((n,))],
        compiler_params=pltpu.CompilerParams(has_side_effects=True),
    )(*shards)


def _small_copies(small_ref, land_ref, send_sems, recv_sems):
    x, y, c = _place()
    me = 4 * x + 2 * y + c
    cps = []
    for d in range(1, 8):
        px, py, pc = x ^ (d >> 2), y ^ ((d >> 1) & 1), c ^ (d & 1)
        cps.append(pltpu.make_async_remote_copy(
            src_ref=small_ref, dst_ref=land_ref.at[me], send_sem=send_sems.at[d - 1], recv_sem=recv_sems.at[d - 1],
            device_id=(px, py, pc), device_id_type=MESH))
    return cps


def _small_start(small):
    land = lax.empty((8,) + small.shape, F32)

    def body(s_ref, l_ref, send_sems, recv_sems, s_thru, l_thru, token):
        for cp in _small_copies(s_ref, l_ref, send_sems, recv_sems):
            cp.start()
        token[...] = jnp.zeros_like(token)

    res = pl.pallas_call(
        body, name="small_start",
        in_specs=[HBM, HBM], out_specs=[SEM, SEM, HBM, HBM, VMEM],
        out_shape=[pltpu.SemaphoreType.DMA((7,)), pltpu.SemaphoreType.DMA((7,)), pltpu.HBM(small.shape, F32),
                   pltpu.HBM(land.shape, F32), jax.ShapeDtypeStruct((8, 128), F32)],
        input_output_aliases={0: 2, 1: 3},
        compiler_params=pltpu.CompilerParams(has_side_effects=DATAFLOW),
    )(_hbm(small), _hbm(land))
    return res[:4], res[4]


def _small_wait(started, after):
    send_sems, recv_sems, small, land = started

    def body(s_ref, l_ref, send_ref, recv_ref, after_ref, s_out, l_out):
        for cp in _small_copies(s_ref, l_ref, send_ref, recv_ref):
            cp.wait_send()
            cp.wait_recv()

    return pl.pallas_call(
        body, name="small_wait",
        in_specs=[HBM, HBM, SEM, SEM, ANY], out_specs=[HBM, HBM],
        out_shape=[pltpu.HBM(small.shape, F32), pltpu.HBM(land.shape, F32)],
        input_output_aliases={0: 0, 1: 1},
        compiler_params=pltpu.CompilerParams(has_side_effects=DATAFLOW),
    )(small, land, send_sems, recv_sems, after)


def _small_sum(small, land, me):
    rows = small.shape[0]

    def body(me_ref, s_ref, l_ref, o_ref):
        acc = None
        for k in range(8):
            term = jnp.where(me_ref[0] == k, s_ref[...], l_ref[k])
            acc = term if k == 0 else acc + term
        o_ref[...] = acc

    return pl.pallas_call(
        body, name="small_sum",
        in_specs=[SMEM, VMEM, VMEM], out_specs=VMEM,
        out_shape=jax.ShapeDtypeStruct((rows, D), F32),
    )(me, small, land)


def _adamw(w, g, m, v, name, tr, g_transposed=False):
    rows, cols = w.shape

    def body(w_ref, g_ref, m_ref, v_ref, d_ref, nm_ref, nv_ref, *gt_ref):
        g_ = g_ref[...]
        if g_transposed:
            g_ = g_.T
            gt_ref[0][...] = g_
        nm = ADAM_B1 * m_ref[...] + (1.0 - ADAM_B1) * g_
        nv = ADAM_B2 * v_ref[...] + (1.0 - ADAM_B2) * (g_ * g_)
        m_hat = nm / (1.0 - ADAM_B1 ** ADAM_STEP)
        v_hat = nv / (1.0 - ADAM_B2 ** ADAM_STEP)
        d_ref[...] = -ADAM_LR * (m_hat / (jnp.sqrt(v_hat) + ADAM_EPS) + ADAM_WD * w_ref[...])
        nm_ref[...] = nm
        nv_ref[...] = nv

    spec = pl.BlockSpec((tr, cols), lambda i: (i, 0))
    g_spec = pl.BlockSpec((cols, tr), lambda i: (0, i)) if g_transposed else spec
    n_out = 4 if g_transposed else 3
    return pl.pallas_call(
        body, name=name, grid=(rows // tr,),
        in_specs=[spec, g_spec, spec, spec], out_specs=[spec] * n_out,
        out_shape=[jax.ShapeDtypeStruct((rows, cols), F32)] * n_out,
        compiler_params=_cp(("parallel",)),
    )(*[_hbm(a) for a in (w, g, m, v)])


def _local_step(x, target, w_in_t, late_weights, norm_a_g, norm_b_g, sinks_a, ln1_g, ln1_b,
                conv_w, conv_b, ln2_g, ln2_b, slopes, on_grad, on_small):
    cwb = jnp.concatenate([conv_w, conv_b[None]], axis=0).reshape(4, 2, FF)

    proj, xb = _proj(x, w_in_t, "proj")
    o_a, lse_a = _attn_a_fwd(proj, sinks_a)
    fwd_b = None
    for r in reversed(B_DILATIONS):
        fwd_b = _attn_b_fwd(proj, slopes, r, fwd_b)
    o_b, lse_b = fwd_b
    w_o = late_weights(1, lse_b)
    cat, z1, h1, h1b = _mix_ln1(x, o_a, o_b, norm_a_g, norm_b_g, w_o, ln1_g, ln1_b)
    w_up = late_weights(2, h1b)
    up = _up_proj(h1b, w_up)
    a, gate, a1 = _conv_gelu(up, cwb)
    w_down = late_weights(3, a)
    dz2, dz2b, st2 = _down_ln2_loss(a, w_down, h1, target, ln2_g, ln2_b)

    on_grad(3, *_grad_w(a, dz2b, "grad_w_down", tm=FF // 2))
    dup, dconv = _conv_gelu_bwd(_d_act(dz2b, w_down), up, gate, a1, cwb)
    on_grad(2, *_grad_w(dup, h1b, "grad_w_up", tm=FF // 2, lhs_halves=True))
    dz1, dz1b, st1 = _dh1_ln1_bwd(dz2, dup, w_up, z1, ln1_g)
    tok = on_grad(1, *_grad_w(cat, dz1b, "grad_w_o", tm=512))
    d_oa, d_ob, st_n = _dcat_rms_bwd(dz1b, w_o, o_a, o_b, norm_a_g + tok[0, 0], norm_b_g)
    dqa, dka, dva, dsink = _attn_a_bwd(proj, sinks_a, d_oa, o_a, lse_a)
    dconv = dconv.reshape(4, 2 * FF)
    tok = on_small(dict(loss=st2[2, 0:1], norm_a_g=st_n[0], norm_b_g=st_n[1], sinks_a=dsink[:, 0],
                        ln1_g=st1[0], ln1_b=st1[1], conv_w=dconv[0:3].reshape(-1), conv_b=dconv[3],
                        ln2_g=st2[0], ln2_b=st2[1]))
    slopes = slopes + tok[0, 0]
    bwd_b = None
    for r in reversed(B_DILATIONS):
        bwd_b = _attn_b_bwd(proj, slopes, d_ob, o_b, lse_b, r, bwd_b, BF16 if r == 1 else F32)
    dparts = tuple(_hbm(a) for a in (dqa, dka, dva, *bwd_b))
    tok = on_grad(0, *_grad_w_in(dparts, xb))
    return _grad_x(dz1, dparts, w_in_t, tok)


SMALL_ORDER = ("loss", "norm_a_g", "norm_b_g", "sinks_a", "ln1_g", "ln1_b", "conv_b", "ln2_g", "ln2_b", "conv_w")
SMALL_SIZES = dict(loss=1, norm_a_g=512, norm_b_g=512, sinks_a=8, ln1_g=D, ln1_b=D, conv_b=2 * FF, ln2_g=D, ln2_b=D,
                   conv_w=3 * 2 * FF)


def _pack(parts, rows):
    flat = jnp.concatenate([parts[k].reshape(-1).astype(F32) for k in parts])
    return jnp.pad(flat, (0, rows * D - flat.shape[0])).reshape(rows, D)


def _unpack(buf, names, sizes):
    flat = buf.reshape(-1)
    out, at = {}, 0
    for k in names:
        out[k] = flat[at:at + sizes[k]]
        at += sizes[k]
    return out


def kernel(x, w_in, norm_a_g, norm_b_g, sinks_a, w_o, ln1_g, ln1_b, w_up, conv_w, conv_b, w_down, ln2_g, ln2_b, loss_target, m_w_in, m_norm_a_g, m_norm_b_g, m_sinks_a, m_w_o, m_ln1_g, m_ln1_b, m_w_up, m_conv_w, m_conv_b, m_w_down, m_ln2_g, m_ln2_b, v_w_in, v_norm_a_g, v_norm_b_g, v_sinks_a, v_w_o, v_ln1_g, v_ln1_b, v_w_up, v_conv_w, v_conv_b, v_w_down, v_ln2_g, v_ln2_b):
    xi, yi, ci = _place()
    chip = (2 * xi + yi).astype(I32)
    core = ci.astype(I32)

    w_in_rows, m_w_in_rows, v_w_in_rows = w_in.T, m_w_in.T, v_w_in.T
    shards = (w_in_rows.astype(BF16), w_o.astype(BF16), w_up.astype(BF16), w_down.astype(BF16))
    w_in_t, conv_w4 = _gather_w_in(shards[0], conv_w)
    conv_w_f = conv_w4.transpose(1, 0, 2).reshape(3, 2 * FF)
    w_started, w_tok = _weights_start(shards[1:], conv_w4)
    slopes = jnp.asarray(SLOPES, F32) + w_tok[0, 0]

    halves_rows = [r // 2 for r in SHARD_ROWS]
    grads4, grads_b4, started = [None] * 4, [None] * 4, [None] * 4

    def on_grad(k, g, g_b):
        grads4[k] = g.reshape(N_CHIPS, 2, halves_rows[k], D)
        grads_b4[k] = g_b.reshape(N_CHIPS, 2, halves_rows[k], D)
        if k > 1:
            return None
        group = (1, 2, 3) if k == 1 else (0,)
        sts, tok = _grads_start([grads_b4[i] for i in group], f"grads_start_{k}")
        for i, st in zip(group, sts):
            started[i] = st
        return tok

    small_rows = 32
    small_started = []

    def on_small(parts):
        st, tok = _small_start(_pack({k: parts[k] for k in SMALL_ORDER}, small_rows))
        small_started.append(st)
        return tok

    gx = _local_step(
        x[0], loss_target[0], w_in_t, lambda k, after: _weights_wait(w_started[k - 1], after, f"weights_wait_{k}"),
        norm_a_g, norm_b_g, sinks_a, ln1_g, ln1_b, conv_w_f, conv_b, ln2_g, ln2_b, slopes, on_grad, on_small)

    tiles = (96, 128, 352, 176)
    core_chip = jnp.stack([core, chip])
    got = _grads_wait(started[1:], gx, "grads_wait_1")
    halves = [_sum_partials(grads4[k], got[k - 1], core_chip, f"sum_partials_{k}", tiles[k]) for k in (1, 2, 3)]
    g_w_o, g_w_up_rows, g_w_down = _swap_halves(halves, "swap_halves")
    delta, new_m, new_v = {}, {}, {}
    for k, g, tr in (("w_o", g_w_o, 128), ("w_down", g_w_down, 176)):
        delta[k], new_m[k], new_v[k] = _adamw(dict(w_o=w_o, w_down=w_down)[k], g, dict(w_o=m_w_o, w_down=m_w_down)[k],
                                              dict(w_o=v_w_o, w_down=v_w_down)[k], f"adamw_{k}", tr)
    delta["w_up"], new_m["w_up"], new_v["w_up"], g_w_up = _adamw(w_up, g_w_up_rows, m_w_up, v_w_up, "adamw_w_up", 256,
                                                                 g_transposed=True)

    got = _grads_wait(started[:1], delta["w_up"], "grads_wait_0")
    half_in = _sum_partials(grads4[0], got[0], core_chip, "sum_partials_0", tiles[0])
    (g_w_in_rows,) = _swap_halves([half_in], "swap_halves_in")
    small_mine, small_land = _small_wait(small_started[0], g_w_in_rows)
    totals = _small_sum(small_mine, small_land, (4 * xi + 2 * yi + ci).astype(I32).reshape(1))
    tot = _unpack(totals, SMALL_ORDER, SMALL_SIZES)
    loss = tot["loss"][0]
    cols = 2 * FF // N_CHIPS
    g_conv_w = lax.dynamic_slice(tot["conv_w"].reshape(3, 2 * FF), (0, chip * cols), (3, cols))
    g_small = dict(norm_a_g=tot["norm_a_g"], norm_b_g=tot["norm_b_g"], sinks_a=tot["sinks_a"], ln1_g=tot["ln1_g"],
                   ln1_b=tot["ln1_b"], conv_w=g_conv_w, conv_b=tot["conv_b"], ln2_g=tot["ln2_g"], ln2_b=tot["ln2_b"])

    weights = dict(w_in=w_in, norm_a_g=norm_a_g, norm_b_g=norm_b_g, sinks_a=sinks_a, w_o=w_o, ln1_g=ln1_g, ln1_b=ln1_b,
                   w_up=w_up, conv_w=conv_w, conv_b=conv_b, w_down=w_down, ln2_g=ln2_g, ln2_b=ln2_b)
    ms = dict(w_in=m_w_in, norm_a_g=m_norm_a_g, norm_b_g=m_norm_b_g, sinks_a=m_sinks_a, w_o=m_w_o, ln1_g=m_ln1_g,
              ln1_b=m_ln1_b, w_up=m_w_up, conv_w=m_conv_w, conv_b=m_conv_b, w_down=m_w_down, ln2_g=m_ln2_g, ln2_b=m_ln2_b)
    vs = dict(w_in=v_w_in, norm_a_g=v_norm_a_g, norm_b_g=v_norm_b_g, sinks_a=v_sinks_a, w_o=v_w_o, ln1_g=v_ln1_g,
              ln1_b=v_ln1_b, w_up=v_w_up, conv_w=v_conv_w, conv_b=v_conv_b, w_down=v_w_down, ln2_g=v_ln2_g, ln2_b=v_ln2_b)
    order = list(weights)
    grad = dict(g_small, w_in=g_w_in_rows.T, w_o=g_w_o, w_up=g_w_up, w_down=g_w_down)

    delta["w_in"], new_m["w_in"], new_v["w_in"] = [
        a.T for a in _adamw(w_in_rows, g_w_in_rows, m_w_in_rows, v_w_in_rows, "adamw_w_in", 144)]
    small_names = [k for k in order if k not in delta]
    sizes = {k: weights[k].size for k in small_names}
    rows = 16
    packed = [_pack({k: src[k] for k in small_names}, rows) for src in (weights, grad, ms, vs)]
    for res, buf in zip((delta, new_m, new_v), _adamw(*packed, "adamw_small", rows)):
        for k, val in _unpack(buf, small_names, sizes).items():
            res[k] = val.reshape(weights[k].shape)

    return (loss, gx[None], *[grad[k] for k in order], *[delta[k] for k in order],
            *[new_m[k] for k in order], *[new_v[k] for k in order])
```

```python
import functools
import math

import jax
import jax.numpy as jnp
from jax import lax
from jax.experimental import pallas as pl
from jax.experimental.pallas import tpu as pltpu

F32, BF16, I32 = jnp.float32, jnp.bfloat16, jnp.int32

D = 1024
FF = 2816
HD = 64
NH = 8
WA, WB = 768, 1536
WIN = WA + WB
BLK = 128
ALPHA = 2.0 ** 0.25
LN_EPS, RMS_EPS = 1e-5, 1e-6
SCALE = 1.0 / math.sqrt(HD)
A_MAX_DIST, B_MAX_DIST = 127, 128
B_DILATIONS = (1, 4, 16)
SLOPES = tuple(2.0 ** (-(i + 1)) for i in range(NH))
SHARD_ROWS = (WIN // 4, D // 4, 2 * FF // 4, FF // 4)
N_CHIPS = 4
ADAM_LR, ADAM_B1, ADAM_B2, ADAM_EPS, ADAM_WD, ADAM_STEP = 0.001, 0.9, 0.999, 1e-08, 0.01, 10
MESH = pl.DeviceIdType.MESH
ANY = pl.BlockSpec(memory_space=pl.ANY)
SMEM = pl.BlockSpec(memory_space=pltpu.SMEM)
VMEM = pl.BlockSpec(memory_space=pltpu.VMEM)
HBM = pl.BlockSpec(memory_space=pltpu.HBM)
SEM = pl.BlockSpec(memory_space=pltpu.SEMAPHORE)
DATAFLOW = pltpu.SideEffectType.DATAFLOW_SIDE_EFFECTING


def _cp(sem, mb=48):
    return pltpu.CompilerParams(dimension_semantics=sem, vmem_limit_bytes=mb << 20)


def _nn(a, b):
    return lax.dot_general(a, b, (((1,), (0,)), ((), ())), preferred_element_type=F32)


def _nt(a, b):
    return lax.dot_general(a, b, (((1,), (1,)), ((), ())), preferred_element_type=F32)


def _tn(a, b):
    return lax.dot_general(a, b, (((0,), (0,)), ((), ())), preferred_element_type=F32)


def _resident(shape):
    n = len(shape)
    return pl.BlockSpec(shape, lambda *_: (0,) * n, pipeline_mode=pl.Buffered(1))


def _const(shape):
    n = len(shape)
    return pl.BlockSpec(shape, lambda *_: (0,) * n)


def _proj(x, w_t, name, tm=512):
    s = x.shape[0]
    n = w_t.shape[0]

    def body(x_ref, w_ref, o_ref, xb_ref):
        xb = x_ref[...].astype(BF16)
        xb_ref[...] = xb
        res = _nt(xb, w_ref[...])
        for g in range(n // 128):
            o_ref[g] = res[:, 128 * g:128 * (g + 1)]

    return pl.pallas_call(
        body, name=name, grid=(s // tm,),
        in_specs=[pl.BlockSpec((tm, D), lambda i: (i, 0)), _resident((n, D))],
        out_specs=[pl.BlockSpec((n // 128, tm, 128), lambda i: (0, i, 0)), pl.BlockSpec((tm, D), lambda i: (i, 0))],
        out_shape=[jax.ShapeDtypeStruct((n // 128, s, 128), F32), jax.ShapeDtypeStruct((s, D), BF16)],
        compiler_params=_cp(("parallel",)),
    )(x, w_t)


def _grad_w(lhs, rhs, name, tm, tk=2048, lhs_halves=False):
    s = rhs.shape[0]
    if lhs_halves:
        per_half = lhs.shape[2] // tm
        n = 2 * lhs.shape[2]
        lhs_spec = pl.BlockSpec((None, tk, tm), lambda i, k: (i // per_half, k, i % per_half))
    else:
        n = lhs.shape[1]
        lhs_spec = pl.BlockSpec((tk, tm), lambda i, k: (k, i))
    nk = s // tk

    def body(l_ref, r_ref, o_ref, ob_ref):
        k = pl.program_id(1)

        @pl.when(k == 0)
        def _():
            o_ref[...] = jnp.zeros_like(o_ref)

        o_ref[...] += _tn(l_ref[...], r_ref[...])

        @pl.when(k == nk - 1)
        def _():
            ob_ref[...] = o_ref[...].astype(BF16)

    return pl.pallas_call(
        body, name=name, grid=(n // tm, nk),
        in_specs=[lhs_spec, pl.BlockSpec((tk, D), lambda i, k: (k, 0))],
        out_specs=[pl.BlockSpec((tm, D), lambda i, k: (i, 0))] * 2,
        out_shape=[pltpu.HBM((n, D), F32), pltpu.HBM((n, D), BF16)],
        compiler_params=_cp(("parallel", "arbitrary")),
    )(lhs, rhs)


def _band_base(max_dist, dist_unit, first):
    row = lax.broadcasted_iota(I32, (BLK, 2 * BLK), 0)
    col = lax.broadcasted_iota(I32, (BLK, 2 * BLK), 1)
    dist = BLK + row - col
    ok = (dist >= 0) & (dist <= max_dist)
    if first:
        ok = ok & (col >= BLK)
    return jnp.where(ok, dist.astype(F32) * (-float(dist_unit)), -jnp.inf)


def _half_mask(shape, e):
    lane = lax.broadcasted_iota(I32, shape, 1)
    return (lane < HD) if e == 0 else (lane >= HD)


def _to_half(x, e, g):
    if g != e:
        x = pltpu.roll(x, HD, 1)
    return jnp.where(_half_mask(x.shape, g), x, 0.0)


def _stack_heads(scalars, tile):
    return jnp.concatenate([scalars[0] * tile, scalars[1] * tile], axis=0)


def _pair_fwd(q2, kb, vb, base, slopes, kv_heads, sinks):
    lo = _half_mask((BLK, 2 * HD), 0)
    if slopes is None:
        bias = base
    elif sinks is None:
        bias = _stack_heads(slopes, base)
    else:
        col0 = lax.broadcasted_iota(I32, base.shape, 1) == 0
        bias = jnp.concatenate([jnp.where(col0, sinks[e], slopes[e] * base) for e in (0, 1)], axis=0)
    qs = jnp.concatenate([_to_half(q2, e, kv_heads[e]) * SCALE for e in (0, 1)], axis=0).astype(BF16)
    s = _nt(qs, kb) + bias
    m = jnp.max(s, axis=1, keepdims=True)
    p = jnp.exp(s - m)
    l = jnp.sum(p, axis=1, keepdims=True)
    o = _nn(p.astype(BF16), vb) / l
    lse = m + jnp.log(l)
    halves = []
    for e in (0, 1):
        oh = o[e * BLK:(e + 1) * BLK]
        halves.append(pltpu.roll(oh, HD, 1) if kv_heads[e] != e else oh)
    o2 = jnp.where(lo, halves[0], halves[1])
    lse2 = jnp.where(lo, jnp.broadcast_to(lse[:BLK], (BLK, 2 * HD)), jnp.broadcast_to(lse[BLK:], (BLK, 2 * HD)))
    return o2, lse2


def _pair_bwd(q2, kb, vb, do2, o2, lse2, base, slopes, kv_heads, sinks):
    lo = _half_mask((BLK, 2 * HD), 0)
    prod = do2 * o2
    lses, deltas = [], []
    for e in (0, 1):
        hq = _half_mask((BLK, 2 * HD), e)
        lses.append(jnp.max(jnp.where(hq, lse2, -jnp.inf), axis=1, keepdims=True))
        deltas.append(jnp.sum(jnp.where(hq, prod, 0.0), axis=1, keepdims=True))
    lse = jnp.concatenate(lses, axis=0)
    delta = jnp.concatenate(deltas, axis=0)
    qs = jnp.concatenate([_to_half(q2, e, kv_heads[e]) * SCALE for e in (0, 1)], axis=0).astype(BF16)
    dos = jnp.concatenate([_to_half(do2, e, kv_heads[e]) for e in (0, 1)], axis=0).astype(BF16)
    p = jnp.exp(_nt(qs, kb) + (base if slopes is None else _stack_heads(slopes, base)) - lse)
    ds = (p * (_nt(dos, vb) - delta)).astype(BF16)
    dq = _nn(ds, kb) * SCALE
    halves = []
    for e in (0, 1):
        dqh = dq[e * BLK:(e + 1) * BLK]
        halves.append(pltpu.roll(dqh, HD, 1) if kv_heads[e] != e else dqh)
    dq2 = jnp.where(lo, halves[0], halves[1])
    dk2 = _tn(ds, qs)
    dv2 = _tn(p.astype(BF16), dos)
    dsinks = []
    if sinks is not None:
        for e in (0, 1):
            dsinks.append(jnp.sum(-jnp.exp(sinks[e] - lses[e]) * deltas[e], axis=0, keepdims=True))
    return dq2, dk2, dv2, dsinks


A_BLOCKS_PER_STEP = 2
A_BLOCKS_PER_STEP_BWD = 1


def _attn_a_fwd(proj, sinks):
    s = proj.shape[1]
    nq = A_BLOCKS_PER_STEP
    rows = BLK * nq
    steps = s // rows

    def body(sink_ref, q_ref, kp_ref, kc_ref, vp_ref, vc_ref, o_ref, lse_ref):
        n = pl.program_id(0)
        base_rest = _band_base(A_MAX_DIST, 1, False)
        base_0 = jnp.where(n > 0, base_rest, _band_base(A_MAX_DIST, 1, True))
        for i in range(nq):
            cur = pl.ds(i * BLK, BLK)
            k_prev = kc_ref[pl.ds((i - 1) * BLK, BLK), :] if i > 0 else kp_ref[...]
            v_prev = vc_ref[pl.ds((i - 1) * BLK, BLK), :] if i > 0 else vp_ref[...]
            first_key = lax.broadcasted_iota(I32, (2 * BLK, 128), 0) == 0
            kb = jnp.where(first_key, 0.0, jnp.concatenate([k_prev, kc_ref[cur, :]], axis=0)).astype(BF16)
            vb = jnp.where(first_key, 0.0, jnp.concatenate([v_prev, vc_ref[cur, :]], axis=0)).astype(BF16)
            for j in range(NH // 2):
                g = j // 2
                o2, lse2 = _pair_fwd(q_ref[j, cur, :], kb, vb, base_rest if i > 0 else base_0,
                                     (SLOPES[2 * j], SLOPES[2 * j + 1]), (g, g), (sink_ref[2 * j], sink_ref[2 * j + 1]))
                o_ref[j, cur, :] = o2
                lse_ref[j, cur, :] = lse2

    before = lambda n: jnp.maximum(n * nq - 1, 0)
    slab = lambda g: pl.BlockSpec((None, rows, 128), lambda n: (g, n, 0))
    edge = lambda g: pl.BlockSpec((None, BLK, 128), lambda n: (g, before(n), 0))
    quad = pl.BlockSpec((4, rows, 128), lambda n: (0, n, 0))
    return pl.pallas_call(
        body, name="attn_a_fwd", grid=(steps,),
        in_specs=[SMEM, quad, edge(4), slab(4), edge(5), slab(5)],
        out_specs=[quad, quad],
        out_shape=[jax.ShapeDtypeStruct((4, s, 128), F32)] * 2,
        compiler_params=_cp(("parallel",)),
    )(sinks, proj, proj, proj, proj, proj)


def _attn_a_bwd(proj, sinks, d_o, o, lse):
    s = proj.shape[1]
    nq = A_BLOCKS_PER_STEP_BWD
    rows = BLK * nq
    steps = s // rows

    def body(sink_ref, q_ref, kp_ref, kc_ref, vp_ref, vc_ref, do_ref, o_ref, lse_ref,
             dq_ref, dk_ref, dv_ref, dsink_ref, kcar, vcar):
        n = pl.program_id(0)

        @pl.when(n == 0)
        def _():
            kcar[...] = jnp.zeros_like(kcar)
            vcar[...] = jnp.zeros_like(vcar)
            dsink_ref[...] = jnp.zeros_like(dsink_ref)

        dk_ref[...] = kcar[...].astype(BF16)
        dv_ref[...] = vcar[...].astype(BF16)

        @pl.when(n < steps)
        def _():
            base_rest = _band_base(A_MAX_DIST, 1, False)
            base_0 = jnp.where(n > 0, base_rest, _band_base(A_MAX_DIST, 1, True))
            for i in range(nq):
                cur = pl.ds(i * BLK, BLK)
                k_prev = kc_ref[pl.ds((i - 1) * BLK, BLK), :] if i > 0 else kp_ref[...]
                v_prev = vc_ref[pl.ds((i - 1) * BLK, BLK), :] if i > 0 else vp_ref[...]
                kb = jnp.concatenate([k_prev, kc_ref[cur, :]], axis=0).astype(BF16)
                vb = jnp.concatenate([v_prev, vc_ref[cur, :]], axis=0).astype(BF16)
                dk_win = dv_win = None
                for j in range(NH // 2):
                    g = j // 2
                    dq2, dk2, dv2, dsk = _pair_bwd(q_ref[j, cur, :], kb, vb, do_ref[j, cur, :], o_ref[j, cur, :],
                                                   lse_ref[j, cur, :], base_rest if i > 0 else base_0,
                                                   (SLOPES[2 * j], SLOPES[2 * j + 1]), (g, g),
                                                   (sink_ref[2 * j], sink_ref[2 * j + 1]))
                    dq_ref[j, cur, :] = dq2.astype(BF16)
                    dk_win = dk2 if j == 0 else dk_win + dk2
                    dv_win = dv2 if j == 0 else dv_win + dv2
                    for e in (0, 1):
                        h = 2 * j + e
                        dsink_ref[h:h + 1, :] += jnp.broadcast_to(dsk[e], (1, 128))
                if i == 0:
                    last = pl.ds((nq - 1) * BLK, BLK)
                    dk_ref[last, :] = (kcar[last, :] + dk_win[:BLK]).astype(BF16)
                    dv_ref[last, :] = (vcar[last, :] + dv_win[:BLK]).astype(BF16)
                else:
                    kcar[pl.ds((i - 1) * BLK, BLK), :] += dk_win[:BLK]
                    vcar[pl.ds((i - 1) * BLK, BLK), :] += dv_win[:BLK]
                kcar[cur, :] = dk_win[BLK:]
                vcar[cur, :] = dv_win[BLK:]

    cur_step = lambda n: jnp.minimum(n, steps - 1)
    before = lambda n: jnp.maximum(cur_step(n) * nq - 1, 0)
    out_prev = lambda n: jnp.maximum(n - 1, 0)
    quad = pl.BlockSpec((4, rows, 128), lambda n: (0, cur_step(n), 0))
    slab = lambda g: pl.BlockSpec((None, rows, 128), lambda n: (g, cur_step(n), 0))
    edge = lambda g: pl.BlockSpec((None, BLK, 128), lambda n: (g, before(n), 0))
    return pl.pallas_call(
        body, name="attn_a_bwd", grid=(steps + 1,),
        in_specs=[SMEM, quad, edge(4), slab(4), edge(5), slab(5), quad, quad, quad],
        out_specs=[quad,
                   pl.BlockSpec((rows, 128), lambda n: (out_prev(n), 0)),
                   pl.BlockSpec((rows, 128), lambda n: (out_prev(n), 0)),
                   pl.BlockSpec((NH, 128), lambda n: (0, 0))],
        out_shape=[pltpu.HBM((4, s, 128), BF16), pltpu.HBM((s, 128), BF16), pltpu.HBM((s, 128), BF16),
                   jax.ShapeDtypeStruct((NH, 128), F32)],
        scratch_shapes=[pltpu.VMEM((rows, 128), F32), pltpu.VMEM((rows, 128), F32)],
        compiler_params=_cp(("arbitrary",)),
    )(sinks, proj, proj, proj, proj, proj, d_o, o, lse)


def _stream(rho, i, r):
    start = i * BLK * r + rho
    return pl.ds(start, BLK, stride=r) if r > 1 else pl.ds(start, BLK)


def _for_streams(r, fn, side_by_side=4):
    if r <= side_by_side:
        for rho in range(r):
            fn(rho)
    else:
        def group(it, carry):
            for u in range(side_by_side):
                fn(side_by_side * it + u)
            return carry

        lax.fori_loop(0, r // side_by_side, group, 0)


B_BLOCKS_PER_STEP = {1: 8, 4: 2, 16: 1}
B_BLOCKS_PER_STEP_FWD = {1: 16, 4: 4, 16: 1}


def _attn_b_fwd(proj, slopes, r, so_far=None):
    s = proj.shape[1]
    nq = B_BLOCKS_PER_STEP_FWD[r]
    rows = BLK * r * nq
    steps = s // rows
    qc, kc, vc = WA // 128, WA // 128 + 4, WA // 128 + 8
    chained = so_far is not None

    def body(slope_ref, q_ref, kp_ref, kc_ref, vp_ref, vc_ref, *rest):
        po_ref, pl_ref = rest[:2] if chained else (None, None)
        o_ref, lse_ref = rest[-2:]
        j = pl.program_id(0)
        sb = pl.program_id(1)
        sl2 = (slope_ref[2 * j], slope_ref[2 * j + 1])
        bias_rest = _stack_heads(sl2, _band_base(B_MAX_DIST, r, False))
        bias_0 = jnp.where(sb > 0, bias_rest, _stack_heads(sl2, _band_base(B_MAX_DIST, r, True)))

        def stream(rho):
            for i in range(nq):
                cur = _stream(rho, i, r)
                k_prev = kc_ref[_stream(rho, i - 1, r), :] if i > 0 else kp_ref[_stream(rho, 0, r), :]
                v_prev = vc_ref[_stream(rho, i - 1, r), :] if i > 0 else vp_ref[_stream(rho, 0, r), :]
                kb = jnp.concatenate([k_prev, kc_ref[cur, :]], axis=0).astype(BF16)
                vb = jnp.concatenate([v_prev, vc_ref[cur, :]], axis=0).astype(BF16)
                o2, lse2 = _pair_fwd(q_ref[cur, :], kb, vb, bias_rest if i > 0 else bias_0, None, (0, 1), None)
                if chained:
                    lse1 = pl_ref[cur, :]
                    m = jnp.maximum(lse1, lse2)
                    e1, e2 = jnp.exp(lse1 - m), jnp.exp(lse2 - m)
                    den = e1 + e2
                    o2 = (e1 * po_ref[cur, :] + e2 * o2) * (1.0 / den)
                    lse2 = m + jnp.log(den)
                o_ref[cur, :] = o2
                lse_ref[cur, :] = lse2

        _for_streams(r, stream, side_by_side=16)

    before = lambda sb: jnp.maximum(sb * nq - 1, 0)
    result = pl.BlockSpec((None, rows, 128), lambda j, sb: (j, sb, 0))
    return pl.pallas_call(
        body, name=f"attn_b_fwd_r{r}", grid=(NH // 2, steps),
        in_specs=[SMEM,
                  pl.BlockSpec((None, rows, 128), lambda j, sb: (qc + j, sb, 0)),
                  pl.BlockSpec((None, BLK * r, 128), lambda j, sb: (kc + j, before(sb), 0)),
                  pl.BlockSpec((None, rows, 128), lambda j, sb: (kc + j, sb, 0)),
                  pl.BlockSpec((None, BLK * r, 128), lambda j, sb: (vc + j, before(sb), 0)),
                  pl.BlockSpec((None, rows, 128), lambda j, sb: (vc + j, sb, 0))] + ([result] * 2 if chained else []),
        out_specs=[result] * 2,
        out_shape=[jax.ShapeDtypeStruct((4, s, 128), F32)] * 2,
        compiler_params=_cp(("parallel", "parallel")),
    )(slopes, proj, proj, proj, proj, proj, *(so_far if chained else ()))


def _attn_b_bwd(proj, slopes, d_o, o, lse, r, so_far=None, dtype=F32):
    s = proj.shape[1]
    nq = B_BLOCKS_PER_STEP[r]
    rows = BLK * r * nq
    steps = s // rows
    qc, kc, vc = WA // 128, WA // 128 + 4, WA // 128 + 8
    chained = so_far is not None

    def body(slope_ref, q_ref, kp_ref, kc_ref, vp_ref, vc_ref, do_ref, o_ref, lse_ref, *rest):
        pq_ref, pk_ref, pv_ref = rest[:3] if chained else (None, None, None)
        dq_ref, dk_ref, dv_ref, kcar, vcar = rest[-5:]
        j = pl.program_id(0)
        sb = pl.program_id(1)

        @pl.when(sb == 0)
        def _():
            kcar[...] = jnp.zeros_like(kcar)
            vcar[...] = jnp.zeros_like(vcar)

        def settled(car, p_ref, idx):
            return car[idx] + p_ref[idx] if chained else car[idx]

        dk_ref[...] = settled(kcar, pk_ref, ...).astype(dtype)
        dv_ref[...] = settled(vcar, pv_ref, ...).astype(dtype)

        @pl.when(sb < steps)
        def _():
            sl2 = (slope_ref[2 * j], slope_ref[2 * j + 1])
            bias_rest = _stack_heads(sl2, _band_base(B_MAX_DIST, r, False))
            bias_0 = jnp.where(sb > 0, bias_rest, _stack_heads(sl2, _band_base(B_MAX_DIST, r, True)))

            def stream(rho):
                for i in range(nq):
                    cur = _stream(rho, i, r)
                    k_prev = kc_ref[_stream(rho, i - 1, r), :] if i > 0 else kp_ref[_stream(rho, 0, r), :]
                    v_prev = vc_ref[_stream(rho, i - 1, r), :] if i > 0 else vp_ref[_stream(rho, 0, r), :]
                    kb = jnp.concatenate([k_prev, kc_ref[cur, :]], axis=0).astype(BF16)
                    vb = jnp.concatenate([v_prev, vc_ref[cur, :]], axis=0).astype(BF16)
                    dq2, dk2, dv2, _ = _pair_bwd(q_ref[cur, :], kb, vb, do_ref[cur, :], o_ref[cur, :], lse_ref[cur, :],
                                                 bias_rest if i > 0 else bias_0, None, (0, 1), None)
                    dq_ref[cur, :] = (dq2 + pq_ref[cur, :] if chained else dq2).astype(dtype)
                    if i == 0:
                        last = (_stream(rho, nq - 1, r), slice(None))
                        dk_ref[last] = (settled(kcar, pk_ref, last) + dk2[:BLK]).astype(dtype)
                        dv_ref[last] = (settled(vcar, pv_ref, last) + dv2[:BLK]).astype(dtype)
                    else:
                        kcar[_stream(rho, i - 1, r), :] += dk2[:BLK]
                        vcar[_stream(rho, i - 1, r), :] += dv2[:BLK]
                    kcar[cur, :] = dk2[BLK:]
                    vcar[cur, :] = dv2[BLK:]

            _for_streams(r, stream, side_by_side=8)

    cur_step = lambda sb: jnp.minimum(sb, steps - 1)
    before = lambda sb: jnp.maximum(cur_step(sb) * nq - 1, 0)
    out_prev = lambda sb: jnp.maximum(sb - 1, 0)
    tile = lambda slab: pl.BlockSpec((None, rows, 128), lambda j, sb: (slab + j, cur_step(sb), 0))
    edge = lambda slab: pl.BlockSpec((None, BLK * r, 128), lambda j, sb: (slab + j, before(sb), 0))
    late = pl.BlockSpec((None, rows, 128), lambda j, sb: (j, out_prev(sb), 0))
    grads = [tile(0), late, late]
    return pl.pallas_call(
        body, name=f"attn_b_bwd_r{r}", grid=(NH // 2, steps + 1),
        in_specs=[SMEM, tile(qc), edge(kc), tile(kc), edge(vc), tile(vc), tile(0), tile(0), tile(0)]
        + (grads if chained else []),
        out_specs=grads,
        out_shape=[pltpu.HBM((4, s, 128), dtype)] * 3,
        scratch_shapes=[pltpu.VMEM((rows, 128), F32), pltpu.VMEM((rows, 128), F32)],
        compiler_params=_cp(("parallel", "arbitrary")),
    )(slopes, proj, proj, proj, proj, proj, d_o, o, lse, *(so_far if chained else ()))


def _row(v):
    return v.reshape(1, -1)


def _layer_norm_stats(z):
    mu = jnp.mean(z, axis=-1, keepdims=True)
    zc = z - mu
    var = jnp.mean(zc * zc, axis=-1, keepdims=True)
    rstd = lax.rsqrt(var + LN_EPS)
    return zc * rstd, rstd


def _layer_norm_bwd(dh, zh, rstd, g):
    dzh = dh * g
    return rstd * (dzh - jnp.mean(dzh, axis=-1, keepdims=True) - zh * jnp.mean(dzh * zh, axis=-1, keepdims=True))


def _rms(o):
    return lax.rsqrt(jnp.mean(o * o, axis=-1, keepdims=True) + RMS_EPS)


def _mix_ln1(x, o_a, o_b, norm_a_g, norm_b_g, w_o, ln1_g, ln1_b, tm=512):
    s = x.shape[0]

    def wide(ref):
        return jnp.concatenate([ref[j] for j in range(4)], axis=1)

    def body(x_ref, oa_ref, ob_ref, ga_ref, gb_ref, wo_ref, g_ref, b_ref, cat_ref, z1_ref, h1_ref, h1b_ref):
        oa, ob = wide(oa_ref), wide(ob_ref)
        na = oa * _rms(oa) * ga_ref[...]
        nb_ = ob * _rms(ob) * gb_ref[...]
        cat = jnp.concatenate([na, nb_], axis=1).astype(BF16)
        cat_ref[...] = cat
        z1 = ALPHA * x_ref[...] + _nn(cat, wo_ref[...])
        z1_ref[...] = z1
        zh, _ = _layer_norm_stats(z1)
        h1 = zh * g_ref[...] + b_ref[...]
        h1_ref[...] = h1
        h1b_ref[...] = h1.astype(BF16)

    t512 = pl.BlockSpec((4, tm, 128), lambda i: (0, i, 0))
    td = pl.BlockSpec((tm, D), lambda i: (i, 0))
    return pl.pallas_call(
        body, name="mix_ln1", grid=(s // tm,),
        in_specs=[td] + [t512] * 2 + [_const((1, 512))] * 2 + [_resident((D, D))] + [_const((1, D))] * 2,
        out_specs=[td, td, td, td],
        out_shape=[jax.ShapeDtypeStruct((s, D), BF16), jax.ShapeDtypeStruct((s, D), F32),
                   jax.ShapeDtypeStruct((s, D), F32), jax.ShapeDtypeStruct((s, D), BF16)],
        compiler_params=_cp(("parallel",)),
    )(x, o_a, o_b, _row(norm_a_g), _row(norm_b_g), w_o, _row(ln1_g), _row(ln1_b))


def _gelu_and_grad(x):
    c = math.sqrt(2.0 / math.pi)
    x2 = x * x
    s = 0.5 * jnp.tanh(x * ((c * 0.044715) * x2 + c)) + 0.5
    dg = s + (x * ((6.0 * c * 0.044715) * x2 + 2.0 * c)) * (s - s * s)
    return x * s, dg


def _shifted(u, edge, row, down):
    groups = [u[8 * i:8 * i + 8] for i in range(u.shape[0] // 8)]
    others = [edge] + groups[:-1] if down else groups[1:] + [edge]
    moved = []
    for k in (1, 2):
        crossing = row >= 8 - k if down else row < k
        moved.append(jnp.concatenate([pltpu.roll(jnp.where(crossing, o, g), k if down else 8 - k, 0)
                                      for o, g in zip(others, groups)], axis=0))
    return moved


def _up_proj(h1b, w_up, tm=512):
    s = h1b.shape[0]

    def body(h_ref, w_ref, o_ref):
        h = h_ref[...]
        for half in (0, 1):
            o_ref[half] = _nn(h, w_ref[:, half * FF:(half + 1) * FF]).astype(BF16)

    return pl.pallas_call(
        body, name="up_proj", grid=(s // tm,),
        in_specs=[pl.BlockSpec((tm, D), lambda i: (i, 0)), _resident((D, 2 * FF))],
        out_specs=pl.BlockSpec((2, tm, FF), lambda i: (0, i, 0)),
        out_shape=jax.ShapeDtypeStruct((2, s, FF), BF16),
        compiler_params=_cp(("parallel",)),
    )(h1b, w_up)


def _conv_gelu(up, cwb, tm=256, tn=FF, chunk_rows=8):
    s = up.shape[1]
    n_c = tm // chunk_rows

    def body(up_ref, c_ref, a_ref, g_ref, a1_ref, carry):
        @pl.when(pl.program_id(1) == 0)
        def _():
            carry[...] = jnp.zeros_like(carry)

        row = lax.broadcasted_iota(jnp.int32, (8, tn), 0)
        edge = [carry[0], carry[1]]
        for c in range(n_c):
            rows = pl.ds(c * chunk_rows, chunk_rows)
            u = []
            for half in (0, 1):
                x = up_ref[half, rows, :].astype(F32)
                r1, r2 = _shifted(x, edge[half], row, True)
                u.append(r2 * c_ref[0, half:half + 1, :] + r1 * c_ref[1, half:half + 1, :]
                         + x * c_ref[2, half:half + 1, :] + c_ref[3, half:half + 1, :])
                edge[half] = x[chunk_rows - 8:]
            g, dg = _gelu_and_grad(u[0])
            a_ref[rows, :] = (g * u[1]).astype(BF16)
            g_ref[rows, :] = g.astype(BF16)
            a1_ref[rows, :] = (u[1] * dg).astype(BF16)
        for half in (0, 1):
            carry[half] = edge[half]

    pair = pl.BlockSpec((2, tm, tn), lambda j, i: (0, i, j))
    tile = pl.BlockSpec((tm, tn), lambda j, i: (i, j))
    return pl.pallas_call(
        body, name="conv_gelu", grid=(FF // tn, s // tm),
        in_specs=[pair, pl.BlockSpec((4, 2, tn), lambda j, i: (0, 0, j))],
        out_specs=[tile, tile, tile],
        out_shape=[jax.ShapeDtypeStruct((s, FF), BF16)] * 3,
        scratch_shapes=[pltpu.VMEM((2, 8, tn), F32)],
        compiler_params=_cp(("parallel", "arbitrary")),
    )(up, cwb)


def _down_ln2_loss(a, w_down, h1, target, ln2_g, ln2_b, tm=512):
    s = a.shape[0]

    def body(a_ref, w_ref, h_ref, t_ref, g_ref, b_ref, dz_ref, dzb_ref, st_ref):
        @pl.when(pl.program_id(0) == 0)
        def _():
            st_ref[...] = jnp.zeros_like(st_ref)

        z2 = ALPHA * h_ref[...] + _nn(a_ref[...], w_ref[...])
        zh, rstd = _layer_norm_stats(z2)
        diff = zh * g_ref[...] + b_ref[...] - t_ref[...]
        part = 0.5 * jnp.sum(jnp.mean(diff * diff, axis=-1, keepdims=True), axis=0, keepdims=True)
        dy = diff * (1.0 / D)
        st_ref[0:1, :] += jnp.sum(dy * zh, axis=0, keepdims=True)
        st_ref[1:2, :] += jnp.sum(dy, axis=0, keepdims=True)
        st_ref[2:3, :] += jnp.broadcast_to(part, (1, D))
        dz = _layer_norm_bwd(dy, zh, rstd, g_ref[...])
        dz_ref[...] = dz
        dzb_ref[...] = dz.astype(BF16)

    td = pl.BlockSpec((tm, D), lambda i: (i, 0))
    return pl.pallas_call(
        body, name="down_ln2_loss", grid=(s // tm,),
        in_specs=[pl.BlockSpec((tm, FF), lambda i: (i, 0)), _resident((FF, D)), td, td, _const((1, D)), _const((1, D))],
        out_specs=[td, td, _const((8, D))],
        out_shape=[jax.ShapeDtypeStruct((s, D), F32), jax.ShapeDtypeStruct((s, D), BF16),
                   jax.ShapeDtypeStruct((8, D), F32)],
        compiler_params=_cp(("arbitrary",)),
    )(a, w_down, h1, target, _row(ln2_g), _row(ln2_b))


def _d_act(dz2b, w_down, tm=512):
    s = dz2b.shape[0]

    def body(dz_ref, w_ref, o_ref):
        o_ref[...] = _nt(dz_ref[...], w_ref[...]).astype(BF16)

    return pl.pallas_call(
        body, name="d_act", grid=(s // tm,),
        in_specs=[pl.BlockSpec((tm, D), lambda i: (i, 0)), _resident((FF, D))],
        out_specs=pl.BlockSpec((tm, FF), lambda i: (i, 0)),
        out_shape=jax.ShapeDtypeStruct((s, FF), BF16),
        compiler_params=_cp(("parallel",)),
    )(dz2b, w_down)


def _conv_gelu_bwd(da, up, g, a1, cwb, tm=256, tn=FF, chunk_rows=16):
    s = da.shape[0]
    n_i = s // tm
    n_c = tm // chunk_rows

    def body(da_ref, up_ref, g_ref, a1_ref, c_ref, dup_ref, dc_ref, carry):
        @pl.when(pl.program_id(1) == 0)
        def _():
            carry[...] = jnp.zeros_like(carry)
            dc_ref[...] = jnp.zeros_like(dc_ref)

        def fold(v):
            return jnp.sum(v.reshape(chunk_rows // 8, 8, v.shape[1]), axis=0)

        def chunk(cc, state):
            after, sums = state
            rows = pl.ds((n_c - 1 - cc) * chunk_rows, chunk_rows)
            da_c = da_ref[rows, :].astype(F32)
            dus = (da_c * a1_ref[rows, :].astype(F32), da_c * g_ref[rows, :].astype(F32))
            head, new_sums = [], []
            for half in (0, 1):
                du = dus[half]
                up = up_ref[half, rows, :].astype(F32)
                l1, l2 = _shifted(du, after[half], row, False)
                dup = (du * c_ref[2, half:half + 1, :] + l1 * c_ref[1, half:half + 1, :]
                       + l2 * c_ref[0, half:half + 1, :])
                dup_ref[half, rows, :] = dup.astype(BF16)
                parts = (fold(l2 * up), fold(l1 * up), fold(du * up), fold(du))
                new_sums.append(parts if sums is None else tuple(a + b for a, b in zip(sums[half], parts)))
                head.append(du[:8])
            return tuple(head), new_sums

        row = lax.broadcasted_iota(jnp.int32, (8, tn), 0)
        state = ((carry[0], carry[1]), None)
        for cc in range(n_c):
            state = chunk(cc, state)
        head, sums = state
        for half in (0, 1):
            carry[half] = head[half]
            for k in range(4):
                dc_ref[k, half:half + 1, :] += jnp.sum(sums[half][k], axis=0, keepdims=True)

    rev = lambda ii: n_i - 1 - ii
    tile = pl.BlockSpec((tm, tn), lambda j, ii: (rev(ii), j))
    pair = pl.BlockSpec((2, tm, tn), lambda j, ii: (0, rev(ii), j))
    per_col = pl.BlockSpec((4, 2, tn), lambda j, ii: (0, 0, j))
    return pl.pallas_call(
        body, name="conv_gelu_bwd", grid=(FF // tn, n_i),
        in_specs=[tile, pair, tile, tile, per_col],
        out_specs=[pair, per_col],
        out_shape=[jax.ShapeDtypeStruct((2, s, FF), BF16), jax.ShapeDtypeStruct((4, 2, FF), F32)],
        scratch_shapes=[pltpu.VMEM((2, 8, tn), F32)],
        compiler_params=_cp(("parallel", "arbitrary")),
    )(da, up, g, a1, cwb)


def _dh1_ln1_bwd(dz2, dup, w_up, z1, ln1_g, tm=512):
    s = dz2.shape[0]

    def body(dz2_ref, dup_ref, w_ref, z1_ref, g_ref, dz1_ref, dz1b_ref, st_ref):
        @pl.when(pl.program_id(0) == 0)
        def _():
            st_ref[...] = jnp.zeros_like(st_ref)

        dh = ALPHA * dz2_ref[...] + _nt(dup_ref[0], w_ref[:, :FF]) + _nt(dup_ref[1], w_ref[:, FF:])
        zh, rstd = _layer_norm_stats(z1_ref[...])
        st_ref[0:1, :] += jnp.sum(dh * zh, axis=0, keepdims=True)
        st_ref[1:2, :] += jnp.sum(dh, axis=0, keepdims=True)
        dz = _layer_norm_bwd(dh, zh, rstd, g_ref[...])
        dz1_ref[...] = dz
        dz1b_ref[...] = dz.astype(BF16)

    td = pl.BlockSpec((tm, D), lambda i: (i, 0))
    return pl.pallas_call(
        body, name="dh1_ln1_bwd", grid=(s // tm,),
        in_specs=[td, pl.BlockSpec((2, tm, FF), lambda i: (0, i, 0)), _resident((D, 2 * FF)), td, _const((1, D))],
        out_specs=[td, td, _const((8, D))],
        out_shape=[jax.ShapeDtypeStruct((s, D), F32), jax.ShapeDtypeStruct((s, D), BF16),
                   jax.ShapeDtypeStruct((8, D), F32)],
        compiler_params=_cp(("arbitrary",), 58),
    )(dz2, dup, w_up, z1, _row(ln1_g))


def _dcat_rms_bwd(dz1b, w_o, o_a, o_b, norm_a_g, norm_b_g, tm=512):
    s = dz1b.shape[0]

    def body(dz_ref, w_ref, oa_ref, ob_ref, ga_ref, gb_ref, da_ref, db_ref, st_ref):
        @pl.when(pl.program_id(0) == 0)
        def _():
            st_ref[...] = jnp.zeros_like(st_ref)

        dcat = _nt(dz_ref[...], w_ref[...])
        for k, (o_ref, g_ref, d_ref) in enumerate(((oa_ref, ga_ref, da_ref), (ob_ref, gb_ref, db_ref))):
            o = jnp.concatenate([o_ref[j] for j in range(4)], axis=1)
            dn = dcat[:, 512 * k:512 * (k + 1)]
            rr = _rms(o)
            oh = o * rr
            st_ref[k:k + 1, :] += jnp.sum(dn * oh, axis=0, keepdims=True)
            doh = dn * g_ref[...]
            d_o = rr * (doh - oh * jnp.mean(doh * oh, axis=-1, keepdims=True))
            for j in range(4):
                d_ref[j] = d_o[:, 128 * j:128 * (j + 1)]

    t512 = pl.BlockSpec((4, tm, 128), lambda i: (0, i, 0))
    return pl.pallas_call(
        body, name="dcat_rms_bwd", grid=(s // tm,),
        in_specs=[pl.BlockSpec((tm, D), lambda i: (i, 0)), _resident((D, D)), t512, t512,
                  _const((1, 512)), _const((1, 512))],
        out_specs=[t512, t512, _const((8, 512))],
        out_shape=[jax.ShapeDtypeStruct((4, s, 128), F32), jax.ShapeDtypeStruct((4, s, 128), F32),
                   jax.ShapeDtypeStruct((8, 512), F32)],
        compiler_params=_cp(("arbitrary",)),
    )(dz1b, w_o, o_a, o_b, _row(norm_a_g), _row(norm_b_g))


def _grad_w_in(dparts, xb, tk=2048):
    s = xb.shape[0]
    nk = s // tk

    def body(qa, ka, va, qb, kb, vb, x_ref, o_ref, ob_ref):
        i = pl.program_id(0)
        k = pl.program_id(1)

        @pl.when(k == 0)
        def _():
            o_ref[...] = jnp.zeros_like(o_ref)

        def add(blocks):
            o_ref[...] += _tn(jnp.concatenate(blocks, axis=1), x_ref[...])

        pl.when(i == 0)(lambda: add([qa[j] for j in range(4)] + [ka[...], va[...]]))
        pl.when(i == 1)(lambda: add([qb[j] for j in range(4)] + [kb[0], kb[1]]))
        pl.when(i == 2)(lambda: add([kb[0], kb[1]] + [vb[j] for j in range(4)]))

        @pl.when(k == nk - 1)
        def _():
            ob_ref[...] = o_ref[...].astype(BF16)

    def during(tile):
        return lambda i, k: jnp.where(i == tile, k, jnp.where(i < tile, 0, nk - 1))

    quad = lambda tile: pl.BlockSpec((4, tk, 128), lambda i, k: (0, during(tile)(i, k), 0))
    one = pl.BlockSpec((tk, 128), lambda i, k: (during(0)(i, k), 0))
    kb_spec = pl.BlockSpec((2, tk, 128), lambda i, k: (jnp.where(i == 2, 1, 0), jnp.where(i == 0, 0, k), 0))
    return pl.pallas_call(
        body, name="grad_w_in", grid=(3, nk),
        in_specs=[quad(0), one, one, quad(1), kb_spec, quad(2), pl.BlockSpec((tk, D), lambda i, k: (k, 0))],
        out_specs=[pl.BlockSpec((WA, D), lambda i, k: (i, 0))] * 2,
        out_shape=[pltpu.HBM((WIN, D), F32), pltpu.HBM((WIN, D), BF16)],
        compiler_params=_cp(("parallel", "arbitrary"), mb=56),
    )(*dparts, xb)


def _grad_x(dz1, dparts, w_in_t, zero, tm=512):
    s = dz1.shape[0]

    def body(dz_ref, qa, ka, va, qb, kb, vb, w_ref, z_ref, o_ref):
        dp = jnp.concatenate([qa[j] for j in range(4)] + [ka[...], va[...]]
                             + [ref[j] for ref in (qb, kb, vb) for j in range(4)], axis=1)
        o_ref[...] = ALPHA * dz_ref[...] + _nn(dp, w_ref[...]) + z_ref[0:1, 0:1]

    td = pl.BlockSpec((tm, D), lambda i: (i, 0))
    quad = pl.BlockSpec((4, tm, 128), lambda i: (0, i, 0))
    one = pl.BlockSpec((tm, 128), lambda i: (i, 0))
    return pl.pallas_call(
        body, name="grad_x", grid=(s // tm,),
        in_specs=[td, quad, one, one, quad, quad, quad, _resident((WIN, D)), _const((8, 128))],
        out_specs=td, out_shape=jax.ShapeDtypeStruct((s, D), F32),
        compiler_params=_cp(("parallel",)),
    )(dz1, *dparts, w_in_t, zero)


def _place():
    return lax.axis_index("x"), lax.axis_index("y"), lax.axis_index("c")


def _other_chips(x, y):
    return [(1 - x, y), (x, 1 - y), (1 - x, 1 - y)]


def _hbm(a):
    return pltpu.with_memory_space_constraint(a, pltpu.HBM)


def _gather_w_in(shard, conv_w):
    rows_k = shard.shape[0]
    half = rows_k // 2

    def body(src, conv_src, out, conv_out, send_sems, recv_sems):
        x, y, c = _place()
        b = 2 * x + y
        sibling = (x, y, 1 - c)
        chips = _other_chips(x, y)

        def copy(idx, chip_b, core, to, first_hop=False):
            rows = out.at[pl.ds(pl.multiple_of(chip_b * rows_k + core * half, 16), half)]
            s_ref = src.at[pl.ds(pl.multiple_of(core * half, 16), half)] if first_hop else rows
            return pltpu.make_async_remote_copy(src_ref=s_ref, dst_ref=rows, send_sem=send_sems.at[idx],
                                                recv_sem=recv_sems.at[idx], device_id=to, device_id_type=MESH)

        def own_copy():
            return pltpu.make_async_remote_copy(
                src_ref=src, dst_ref=out.at[pl.ds(pl.multiple_of(b * rows_k, 16), rows_k)], send_sem=send_sems.at[6],
                recv_sem=recv_sems.at[6], device_id=sibling, device_id_type=MESH)

        def conv_copy(idx, chip_b, to):
            return pltpu.make_async_remote_copy(src_ref=conv_src, dst_ref=conv_out.at[chip_b],
                                                send_sem=send_sems.at[7 + idx], recv_sem=recv_sems.at[7 + idx],
                                                device_id=to, device_id_type=MESH)

        started = [own_copy(), conv_copy(3, b, sibling)]
        for jn, chip in enumerate(chips):
            started += [copy(jn, b, c, (chip[0], chip[1], c), first_hop=True), conv_copy(jn, b, (chip[0], chip[1], c))]
        for cp in started:
            cp.start()
        for jn, chip in enumerate(chips):
            cb = 2 * chip[0] + chip[1]
            copy(jn, cb, c, (chip[0], chip[1], c)).wait_recv()
            cp = copy(3 + jn, cb, c, sibling)
            cp.start()
            started.append(cp)
        for jn, chip in enumerate(chips):
            cb = 2 * chip[0] + chip[1]
            copy(3 + jn, cb, 1 - c, sibling).wait_recv()
            conv_copy(jn, cb, (chip[0], chip[1], c)).wait_recv()
        own_copy().wait_recv()
        conv_copy(3, b, sibling).wait_recv()
        for cp in started:
            cp.wait_send()

    return pl.pallas_call(
        body, name="gather_w_in",
        in_specs=[ANY, ANY], out_specs=[ANY, ANY],
        out_shape=[jax.ShapeDtypeStruct((N_CHIPS * rows_k, D), BF16), jax.ShapeDtypeStruct((N_CHIPS,) + conv_w.shape, F32)],
        scratch_shapes=[pltpu.SemaphoreType.DMA((11,)), pltpu.SemaphoreType.DMA((11,))],
        compiler_params=pltpu.CompilerParams(has_side_effects=True),
    )(shard, conv_w)


def _weight_copies(shard, land, send_sems, recv_sems, arrivals):
    x, y, c = _place()
    n_rows, n_cols = shard.shape
    peers = [(px, py, c) for px, py in _other_chips(x, y)] + [(x, y, 1 - c)]
    cps = []
    for jn, peer in enumerate(peers):
        at = 2 * peer[0] + peer[1] if arrivals else 2 * x + y
        if land.shape[1] == n_cols:
            dst = land.at[pl.ds(pl.multiple_of(at * n_rows, 16), n_rows)]
        else:
            dst = land.at[:, pl.ds(pl.multiple_of(at * n_cols, 128), n_cols)]
        cps.append(pltpu.make_async_remote_copy(src_ref=shard, dst_ref=dst, send_sem=send_sems.at[jn],
                                                recv_sem=recv_sems.at[jn], device_id=peer, device_id_type=MESH))
    return cps


def _weights_start(shards, after):
    n = len(shards)
    lands = [lax.empty((N_CHIPS * sh.shape[0], D) if sh.shape[1] == D else (D, N_CHIPS * sh.shape[1]), BF16)
             for sh in shards]

    def body(*refs):
        src, land = refs[:n], refs[n:2 * n]
        send_sems, recv_sems = refs[2 * n + 1:3 * n + 1], refs[3 * n + 1:4 * n + 1]
        for k in range(n):
            for send in _weight_copies(src[k], land[k], send_sems[k], recv_sems[k], False):
                send.start()
        refs[-1][...] = jnp.zeros_like(refs[-1])

    res = pl.pallas_call(
        body, name="weights_start",
        in_specs=[HBM] * (2 * n) + [ANY], out_specs=[SEM] * (2 * n) + [HBM] * (2 * n) + [VMEM],
        out_shape=[pltpu.SemaphoreType.DMA((4,))] * (2 * n)
        + [pltpu.HBM(a.shape, a.dtype) for a in (*shards, *lands)] + [jax.ShapeDtypeStruct((8, 128), F32)],
        input_output_aliases={i: i + 2 * n for i in range(2 * n)},
        compiler_params=pltpu.CompilerParams(has_side_effects=DATAFLOW),
    )(*[_hbm(a) for a in (*shards, *lands)], after)
    return [(res[k], res[n + k], res[2 * n + k], res[3 * n + k]) for k in range(n)], res[-1]


def _weights_wait(started, after, name):
    send_sems, recv_sems, shard, land = started

    def body(s_ref, l_ref, send_ref, recv_ref, after_ref, s_out, l_out):
        for cp in _weight_copies(s_ref, l_ref, send_ref, recv_ref, True):
            cp.wait_send()
            cp.wait_recv()

    return pl.pallas_call(
        body, name=name,
        in_specs=[HBM, HBM, SEM, SEM, ANY], out_specs=[HBM, HBM],
        out_shape=[pltpu.HBM(shard.shape, shard.dtype), pltpu.HBM(land.shape, land.dtype)],
        input_output_aliases={0: 0, 1: 1},
        compiler_params=pltpu.CompilerParams(has_side_effects=DATAFLOW),
    )(shard, land, send_sems, recv_sems, after)[1]


def _grad_copies(g_ref, land_ref, send_sems, recv_sems):
    x, y, c = _place()
    cps = []
    for d in range(1, 8):
        px, py, pc = x ^ (d >> 2), y ^ ((d >> 1) & 1), c ^ (d & 1)
        cps.append(pltpu.make_async_remote_copy(
            src_ref=g_ref.at[2 * px + py, pc], dst_ref=land_ref.at[d - 1], send_sem=send_sems.at[d - 1],
            recv_sem=recv_sems.at[d - 1], device_id=(px, py, pc), device_id_type=MESH))
    return cps


def _grads_start(grads_b, name):
    n = len(grads_b)
    lands = [lax.empty((7, g.shape[2], D), BF16) for g in grads_b]

    def body(*refs):
        g, land = refs[:n], refs[n:2 * n]
        send_sems, recv_sems = refs[2 * n:3 * n], refs[3 * n:4 * n]
        for k in range(n):
            for cp in _grad_copies(g[k], land[k], send_sems[k], recv_sems[k]):
                cp.start()
        refs[-1][...] = jnp.zeros_like(refs[-1])

    res = pl.pallas_call(
        body, name=name,
        in_specs=[HBM] * (2 * n), out_specs=[SEM] * (2 * n) + [HBM] * (2 * n) + [VMEM],
        out_shape=[pltpu.SemaphoreType.DMA((7,))] * (2 * n)
        + [pltpu.HBM(a.shape, a.dtype) for a in (*grads_b, *lands)] + [jax.ShapeDtypeStruct((8, 128), F32)],
        input_output_aliases={i: i + 2 * n for i in range(2 * n)},
        compiler_params=pltpu.CompilerParams(has_side_effects=DATAFLOW),
    )(*[_hbm(a) for a in (*grads_b, *lands)])
    return [(res[k], res[n + k], res[2 * n + k], res[3 * n + k]) for k in range(n)], res[-1]


def _grads_wait(started, after, name):
    n = len(started)

    def body(*refs):
        g, land = refs[:n], refs[n:2 * n]
        send_sems, recv_sems = refs[2 * n:3 * n], refs[3 * n:4 * n]
        for k in range(n):
            for cp in _grad_copies(g[k], land[k], send_sems[k], recv_sems[k]):
                cp.wait_send()
                cp.wait_recv()

    gs = [st[2] for st in started]
    lands = [st[3] for st in started]
    res = pl.pallas_call(
        body, name=name,
        in_specs=[HBM] * (2 * n) + [SEM] * (2 * n) + [ANY], out_specs=[HBM] * (2 * n),
        out_shape=[pltpu.HBM(a.shape, a.dtype) for a in (*gs, *lands)],
        input_output_aliases={i: i for i in range(2 * n)},
        compiler_params=pltpu.CompilerParams(has_side_effects=DATAFLOW),
    )(*gs, *lands, *[st[0] for st in started], *[st[1] for st in started], after)
    return res[n:]


def _sum_partials(grad4, got, cb, name, tr):
    h = grad4.shape[2]
    per_half = h // tr

    def body(cb_ref, g_ref, o_ref, out_ref):
        acc = g_ref[...]
        for j in range(7):
            acc = acc + o_ref[j].astype(F32)
        out_ref[...] = acc

    return pl.pallas_call(
        body, name=name,
        grid_spec=pltpu.PrefetchScalarGridSpec(
            num_scalar_prefetch=1, grid=(per_half,),
            in_specs=[pl.BlockSpec((None, None, tr, D), lambda i, cb_ref: (cb_ref[1], cb_ref[0], i, 0)),
                      pl.BlockSpec((7, tr, D), lambda i, cb_ref: (0, i, 0))],
            out_specs=pl.BlockSpec((tr, D), lambda i, cb_ref: (cb_ref[0] * per_half + i, 0))),
        out_shape=pltpu.HBM((2 * h, D), F32),
        compiler_params=_cp(("arbitrary",)),
    )(cb, grad4, _hbm(got))


def _swap_halves(shards, name):
    n = len(shards)

    def body(*refs):
        out, send_sems, recv_sems = refs[n:2 * n], refs[2 * n], refs[2 * n + 1]
        x, y, c = _place()
        cps = []
        for k in range(n):
            h = shards[k].shape[0] // 2
            mine = out[k].at[pl.ds(pl.multiple_of(c * h, 8), h)]
            cp = pltpu.make_async_remote_copy(src_ref=mine, dst_ref=mine, send_sem=send_sems.at[k],
                                              recv_sem=recv_sems.at[k], device_id=(x, y, 1 - c), device_id_type=MESH)
            cp.start()
            cps.append(cp)
        for cp in cps:
            cp.wait()

    return pl.pallas_call(
        body, name=name,
        in_specs=[ANY] * n, out_specs=[ANY] * n,
        out_shape=[jax.ShapeDtypeStruct(sh.shape, F32) for sh in shards],
        input_output_aliases={k: k for k in range(n)},
        scratch_shapes=[pltpu.SemaphoreType.DMA((n,)), pltpu.SemaphoreType.DMA((n,))],
        compiler_params=pltpu.CompilerParams(has_side_effects=True),
    )(*shards)


def _small_copies(small_ref, land_ref, send_sems, recv_sems):
    x, y, c = _place()
    me = 4 * x + 2 * y + c
    cps = []
    for d in range(1, 8):
        px, py, pc = x ^ (d >> 2), y ^ ((d >> 1) & 1), c ^ (d & 1)
        cps.append(pltpu.make_async_remote_copy(
            src_ref=small_ref, dst_ref=land_ref.at[me], send_sem=send_sems.at[d - 1], recv_sem=recv_sems.at[d - 1],
            device_id=(px, py, pc), device_id_type=MESH))
    return cps


def _small_start(small):
    land = lax.empty((8,) + small.shape, F32)

    def body(s_ref, l_ref, send_sems, recv_sems, s_thru, l_thru, token):
        for cp in _small_copies(s_ref, l_ref, send_sems, recv_sems):
            cp.start()
        token[...] = jnp.zeros_like(token)

    res = pl.pallas_call(
        body, name="small_start",
        in_specs=[HBM, HBM], out_specs=[SEM, SEM, HBM, HBM, VMEM],
        out_shape=[pltpu.SemaphoreType.DMA((7,)), pltpu.SemaphoreType.DMA((7,)), pltpu.HBM(small.shape, F32),
                   pltpu.HBM(land.shape, F32), jax.ShapeDtypeStruct((8, 128), F32)],
        input_output_aliases={0: 2, 1: 3},
        compiler_params=pltpu.CompilerParams(has_side_effects=DATAFLOW),
    )(_hbm(small), _hbm(land))
    return res[:4], res[4]


def _small_wait(started, after):
    send_sems, recv_sems, small, land = started

    def body(s_ref, l_ref, send_ref, recv_ref, after_ref, s_out, l_out):
        for cp in _small_copies(s_ref, l_ref, send_ref, recv_ref):
            cp.wait_send()
            cp.wait_recv()

    return pl.pallas_call(
        body, name="small_wait",
        in_specs=[HBM, HBM, SEM, SEM, ANY], out_specs=[HBM, HBM],
        out_shape=[pltpu.HBM(small.shape, F32), pltpu.HBM(land.shape, F32)],
        input_output_aliases={0: 0, 1: 1},
        compiler_params=pltpu.CompilerParams(has_side_effects=DATAFLOW),
    )(small, land, send_sems, recv_sems, after)


def _small_sum(small, land, me):
    rows = small.shape[0]

    def body(me_ref, s_ref, l_ref, o_ref):
        acc = None
        for k in range(8):
            term = jnp.where(me_ref[0] == k, s_ref[...], l_ref[k])
            acc = term if k == 0 else acc + term
        o_ref[...] = acc

    return pl.pallas_call(
        body, name="small_sum",
        in_specs=[SMEM, VMEM, VMEM], out_specs=VMEM,
        out_shape=jax.ShapeDtypeStruct((rows, D), F32),
    )(me, small, land)


def _adamw(w, g, m, v, name, tr, g_transposed=False):
    rows, cols = w.shape

    def body(w_ref, g_ref, m_ref, v_ref, d_ref, nm_ref, nv_ref, *gt_ref):
        g_ = g_ref[...]
        if g_transposed:
            g_ = g_.T
            gt_ref[0][...] = g_
        nm = ADAM_B1 * m_ref[...] + (1.0 - ADAM_B1) * g_
        nv = ADAM_B2 * v_ref[...] + (1.0 - ADAM_B2) * (g_ * g_)
        m_hat = nm / (1.0 - ADAM_B1 ** ADAM_STEP)
        v_hat = nv / (1.0 - ADAM_B2 ** ADAM_STEP)
        d_ref[...] = -ADAM_LR * (m_hat / (jnp.sqrt(v_hat) + ADAM_EPS) + ADAM_WD * w_ref[...])
        nm_ref[...] = nm
        nv_ref[...] = nv

    spec = pl.BlockSpec((tr, cols), lambda i: (i, 0))
    g_spec = pl.BlockSpec((cols, tr), lambda i: (0, i)) if g_transposed else spec
    n_out = 4 if g_transposed else 3
    return pl.pallas_call(
        body, name=name, grid=(rows // tr,),
        in_specs=[spec, g_spec, spec, spec], out_specs=[spec] * n_out,
        out_shape=[jax.ShapeDtypeStruct((rows, cols), F32)] * n_out,
        compiler_params=_cp(("parallel",)),
    )(*[_hbm(a) for a in (w, g, m, v)])


def _local_step(x, target, w_in_t, late_weights, norm_a_g, norm_b_g, sinks_a, ln1_g, ln1_b,
                conv_w, conv_b, ln2_g, ln2_b, slopes, on_grad, on_small):
    cwb = jnp.concatenate([conv_w, conv_b[None]], axis=0).reshape(4, 2, FF)

    proj, xb = _proj(x, w_in_t, "proj")
    o_a, lse_a = _attn_a_fwd(proj, sinks_a)
    fwd_b = None
    for r in reversed(B_DILATIONS):
        fwd_b = _attn_b_fwd(proj, slopes, r, fwd_b)
    o_b, lse_b = fwd_b
    w_o = late_weights(1, lse_b)
    cat, z1, h1, h1b = _mix_ln1(x, o_a, o_b, norm_a_g, norm_b_g, w_o, ln1_g, ln1_b)
    w_up = late_weights(2, h1b)
    up = _up_proj(h1b, w_up)
    a, gate, a1 = _conv_gelu(up, cwb)
    w_down = late_weights(3, a)
    dz2, dz2b, st2 = _down_ln2_loss(a, w_down, h1, target, ln2_g, ln2_b)

    on_grad(3, *_grad_w(a, dz2b, "grad_w_down", tm=FF // 2))
    dup, dconv = _conv_gelu_bwd(_d_act(dz2b, w_down), up, gate, a1, cwb)
    on_grad(2, *_grad_w(dup, h1b, "grad_w_up", tm=FF // 2, lhs_halves=True))
    dz1, dz1b, st1 = _dh1_ln1_bwd(dz2, dup, w_up, z1, ln1_g)
    tok = on_grad(1, *_grad_w(cat, dz1b, "grad_w_o", tm=512))
    d_oa, d_ob, st_n = _dcat_rms_bwd(dz1b, w_o, o_a, o_b, norm_a_g + tok[0, 0], norm_b_g)
    dqa, dka, dva, dsink = _attn_a_bwd(proj, sinks_a, d_oa, o_a, lse_a)
    dconv = dconv.reshape(4, 2 * FF)
    tok = on_small(dict(loss=st2[2, 0:1], norm_a_g=st_n[0], norm_b_g=st_n[1], sinks_a=dsink[:, 0],
                        ln1_g=st1[0], ln1_b=st1[1], conv_w=dconv[0:3].reshape(-1), conv_b=dconv[3],
                        ln2_g=st2[0], ln2_b=st2[1]))
    slopes = slopes + tok[0, 0]
    bwd_b = None
    for r in reversed(B_DILATIONS):
        bwd_b = _attn_b_bwd(proj, slopes, d_ob, o_b, lse_b, r, bwd_b, BF16 if r == 1 else F32)
    dparts = tuple(_hbm(a) for a in (dqa, dka, dva, *bwd_b))
    tok = on_grad(0, *_grad_w_in(dparts, xb))
    return _grad_x(dz1, dparts, w_in_t, tok)


SMALL_ORDER = ("loss", "norm_a_g", "norm_b_g", "sinks_a", "ln1_g", "ln1_b", "conv_b", "ln2_g", "ln2_b", "conv_w")
SMALL_SIZES = dict(loss=1, norm_a_g=512, norm_b_g=512, sinks_a=8, ln1_g=D, ln1_b=D, conv_b=2 * FF, ln2_g=D, ln2_b=D,
                   conv_w=3 * 2 * FF)


def _pack(parts, rows):
    flat = jnp.concatenate([parts[k].reshape(-1).astype(F32) for k in parts])
    return jnp.pad(flat, (0, rows * D - flat.shape[0])).reshape(rows, D)


def _unpack(buf, names, sizes):
    flat = buf.reshape(-1)
    out, at = {}, 0
    for k in names:
        out[k] = flat[at:at + sizes[k]]
        at += sizes[k]
    return out


def kernel(x, w_in, norm_a_g, norm_b_g, sinks_a, w_o, ln1_g, ln1_b, w_up, conv_w, conv_b, w_down, ln2_g, ln2_b, loss_target, m_w_in, m_norm_a_g, m_norm_b_g, m_sinks_a, m_w_o, m_ln1_g, m_ln1_b, m_w_up, m_conv_w, m_conv_b, m_w_down, m_ln2_g, m_ln2_b, v_w_in, v_norm_a_g, v_norm_b_g, v_sinks_a, v_w_o, v_ln1_g, v_ln1_b, v_w_up, v_conv_w, v_conv_b, v_w_down, v_ln2_g, v_ln2_b):
    xi, yi, ci = _place()
    chip = (2 * xi + yi).astype(I32)
    core = ci.astype(I32)

    w_in_rows, m_w_in_rows, v_w_in_rows = w_in.T, m_w_in.T, v_w_in.T
    shards = (w_in_rows.astype(BF16), w_o.astype(BF16), w_up.astype(BF16), w_down.astype(BF16))
    w_in_t, conv_w4 = _gather_w_in(shards[0], conv_w)
    conv_w_f = conv_w4.transpose(1, 0, 2).reshape(3, 2 * FF)
    w_started, w_tok = _weights_start(shards[1:], conv_w4)
    slopes = jnp.asarray(SLOPES, F32) + w_tok[0, 0]

    halves_rows = [r // 2 for r in SHARD_ROWS]
    grads4, grads_b4, started = [None] * 4, [None] * 4, [None] * 4

    def on_grad(k, g, g_b):
        grads4[k] = g.reshape(N_CHIPS, 2, halves_rows[k], D)
        grads_b4[k] = g_b.reshape(N_CHIPS, 2, halves_rows[k], D)
        if k > 1:
            return None
        group = (1, 2, 3) if k == 1 else (0,)
        sts, tok = _grads_start([grads_b4[i] for i in group], f"grads_start_{k}")
        for i, st in zip(group, sts):
            started[i] = st
        return tok

    small_rows = 32
    small_started = []

    def on_small(parts):
        st, tok = _small_start(_pack({k: parts[k] for k in SMALL_ORDER}, small_rows))
        small_started.append(st)
        return tok

    gx = _local_step(
        x[0], loss_target[0], w_in_t, lambda k, after: _weights_wait(w_started[k - 1], after, f"weights_wait_{k}"),
        norm_a_g, norm_b_g, sinks_a, ln1_g, ln1_b, conv_w_f, conv_b, ln2_g, ln2_b, slopes, on_grad, on_small)

    tiles = (96, 128, 352, 176)
    core_chip = jnp.stack([core, chip])
    got = _grads_wait(started[1:], gx, "grads_wait_1")
    halves = [_sum_partials(grads4[k], got[k - 1], core_chip, f"sum_partials_{k}", tiles[k]) for k in (1, 2, 3)]
    g_w_o, g_w_up_rows, g_w_down = _swap_halves(halves, "swap_halves")
    delta, new_m, new_v = {}, {}, {}
    for k, g, tr in (("w_o", g_w_o, 128), ("w_down", g_w_down, 176)):
        delta[k], new_m[k], new_v[k] = _adamw(dict(w_o=w_o, w_down=w_down)[k], g, dict(w_o=m_w_o, w_down=m_w_down)[k],
                                              dict(w_o=v_w_o, w_down=v_w_down)[k], f"adamw_{k}", tr)
    delta["w_up"], new_m["w_up"], new_v["w_up"], g_w_up = _adamw(w_up, g_w_up_rows, m_w_up, v_w_up, "adamw_w_up", 256,
                                                                 g_transposed=True)

    got = _grads_wait(started[:1], delta["w_up"], "grads_wait_0")
    half_in = _sum_partials(grads4[0], got[0], core_chip, "sum_partials_0", tiles[0])
    (g_w_in_rows,) = _swap_halves([half_in], "swap_halves_in")
    small_mine, small_land = _small_wait(small_started[0], g_w_in_rows)
    totals = _small_sum(small_mine, small_land, (4 * xi + 2 * yi + ci).astype(I32).reshape(1))
    tot = _unpack(totals, SMALL_ORDER, SMALL_SIZES)
    loss = tot["loss"][0]
    cols = 2 * FF // N_CHIPS
    g_conv_w = lax.dynamic_slice(tot["conv_w"].reshape(3, 2 * FF), (0, chip * cols), (3, cols))
    g_small = dict(norm_a_g=tot["norm_a_g"], norm_b_g=tot["norm_b_g"], sinks_a=tot["sinks_a"], ln1_g=tot["ln1_g"],
                   ln1_b=tot["ln1_b"], conv_w=g_conv_w, conv_b=tot["conv_b"], ln2_g=tot["ln2_g"], ln2_b=tot["ln2_b"])

    weights = dict(w_in=w_in, norm_a_g=norm_a_g, norm_b_g=norm_b_g, sinks_a=sinks_a, w_o=w_o, ln1_g=ln1_g, ln1_b=ln1_b,
                   w_up=w_up, conv_w=conv_w, conv_b=conv_b, w_down=w_down, ln2_g=ln2_g, ln2_b=ln2_b)
    ms = dict(w_in=m_w_in, norm_a_g=m_norm_a_g, norm_b_g=m_norm_b_g, sinks_a=m_sinks_a, w_o=m_w_o, ln1_g=m_ln1_g,
              ln1_b=m_ln1_b, w_up=m_w_up, conv_w=m_conv_w, conv_b=m_conv_b, w_down=m_w_down, ln2_g=m_ln2_g, ln2_b=m_ln2_b)
    vs = dict(w_in=v_w_in, norm_a_g=v_norm_a_g, norm_b_g=v_norm_b_g, sinks_a=v_sinks_a, w_o=v_w_o, ln1_g=v_ln1_g,
              ln1_b=v_ln1_b, w_up=v_w_up, conv_w=v_conv_w, conv_b=v_conv_b, w_down=v_w_down, ln2_g=v_ln2_g, ln2_b=v_ln2_b)
    order = list(weights)
    grad = dict(g_small, w_in=g_w_in_rows.T, w_o=g_w_o, w_up=g_w_up, w_down=g_w_down)

    delta["w_in"], new_m["w_in"], new_v["w_in"] = [
        a.T for a in _adamw(w_in_rows, g_w_in_rows, m_w_in_rows, v_w_in_rows, "adamw_w_in", 144)]
    small_names = [k for k in order if k not in delta]
    sizes = {k: weights[k].size for k in small_names}
    rows = 16
    packed = [_pack({k: src[k] for k in small_names}, rows) for src in (weights, grad, ms, vs)]
    for res, buf in zip((delta, new_m, new_v), _adamw(*packed, "adamw_small", rows)):
        for k, val in _unpack(buf, small_names, sizes).items():
            res[k] = val.reshape(weights[k].shape)

    return (loss, gx[None], *[grad[k] for k in order], *[delta[k] for k in order],
            *[new_m[k] for k in order], *[new_v[k] for k in order])
```

```python
import functools
import math

import jax
import jax.numpy as jnp
from jax import lax
from jax.experimental import pallas as pl
from jax.experimental.pallas import tpu as pltpu

F32, BF16, I32 = jnp.float32, jnp.bfloat16, jnp.int32

D = 1024
FF = 2816
HD = 64
NH = 8
WA, WB = 768, 1536
WIN = WA + WB
BLK = 128
ALPHA = 2.0 ** 0.25
LN_EPS, RMS_EPS = 1e-5, 1e-6
SCALE = 1.0 / math.sqrt(HD)
A_MAX_DIST, B_MAX_DIST = 127, 128
B_DILATIONS = (1, 4, 16)
SLOPES = tuple(2.0 ** (-(i + 1)) for i in range(NH))
SHARD_ROWS = (WIN // 4, D // 4, 2 * FF // 4, FF // 4)
N_CHIPS = 4
ADAM_LR, ADAM_B1, ADAM_B2, ADAM_EPS, ADAM_WD, ADAM_STEP = 0.001, 0.9, 0.999, 1e-08, 0.01, 10
MESH = pl.DeviceIdType.MESH
ANY = pl.BlockSpec(memory_space=pl.ANY)
SMEM = pl.BlockSpec(memory_space=pltpu.SMEM)
VMEM = pl.BlockSpec(memory_space=pltpu.VMEM)
HBM = pl.BlockSpec(memory_space=pltpu.HBM)
SEM = pl.BlockSpec(memory_space=pltpu.SEMAPHORE)
DATAFLOW = pltpu.SideEffectType.DATAFLOW_SIDE_EFFECTING


def _cp(sem, mb=48):
    return pltpu.CompilerParams(dimension_semantics=sem, vmem_limit_bytes=mb << 20)


def _nn(a, b):
    return lax.dot_general(a, b, (((1,), (0,)), ((), ())), preferred_element_type=F32)


def _nt(a, b):
    return lax.dot_general(a, b, (((1,), (1,)), ((), ())), preferred_element_type=F32)


def _tn(a, b):
    return lax.dot_general(a, b, (((0,), (0,)), ((), ())), preferred_element_type=F32)


def _resident(shape):
    n = len(shape)
    return pl.BlockSpec(shape, lambda *_: (0,) * n, pipeline_mode=pl.Buffered(1))


def _const(shape):
    n = len(shape)
    return pl.BlockSpec(shape, lambda *_: (0,) * n)


def _proj(x, w_t, name, tm=512):
    s = x.shape[0]
    n = w_t.shape[0]

    def body(x_ref, w_ref, o_ref, xb_ref):
        xb = x_ref[...].astype(BF16)
        xb_ref[...] = xb
        res = _nt(xb, w_ref[...])
        for g in range(n // 128):
            o_ref[g] = res[:, 128 * g:128 * (g + 1)]

    return pl.pallas_call(
        body, name=name, grid=(s // tm,),
        in_specs=[pl.BlockSpec((tm, D), lambda i: (i, 0)), _resident((n, D))],
        out_specs=[pl.BlockSpec((n // 128, tm, 128), lambda i: (0, i, 0)), pl.BlockSpec((tm, D), lambda i: (i, 0))],
        out_shape=[jax.ShapeDtypeStruct((n // 128, s, 128), F32), jax.ShapeDtypeStruct((s, D), BF16)],
        compiler_params=_cp(("parallel",)),
    )(x, w_t)


def _grad_w(lhs, rhs, name, tm, tk=2048, lhs_halves=False):
    s = rhs.shape[0]
    if lhs_halves:
        per_half = lhs.shape[2] // tm
        n = 2 * lhs.shape[2]
        lhs_spec = pl.BlockSpec((None, tk, tm), lambda i, k: (i // per_half, k, i % per_half))
    else:
        n = lhs.shape[1]
        lhs_spec = pl.BlockSpec((tk, tm), lambda i, k: (k, i))
    nk = s // tk

    def body(l_ref, r_ref, o_ref, ob_ref):
        k = pl.program_id(1)

        @pl.when(k == 0)
        def _():
            o_ref[...] = jnp.zeros_like(o_ref)

        o_ref[...] += _tn(l_ref[...], r_ref[...])

        @pl.when(k == nk - 1)
        def _():
            ob_ref[...] = o_ref[...].astype(BF16)

    return pl.pallas_call(
        body, name=name, grid=(n // tm, nk),
        in_specs=[lhs_spec, pl.BlockSpec((tk, D), lambda i, k: (k, 0))],
        out_specs=[pl.BlockSpec((tm, D), lambda i, k: (i, 0))] * 2,
        out_shape=[pltpu.HBM((n, D), F32), pltpu.HBM((n, D), BF16)],
        compiler_params=_cp(("parallel", "arbitrary")),
    )(lhs, rhs)


def _band_base(max_dist, dist_unit, first):
    row = lax.broadcasted_iota(I32, (BLK, 2 * BLK), 0)
    col = lax.broadcasted_iota(I32, (BLK, 2 * BLK), 1)
    dist = BLK + row - col
    ok = (dist >= 0) & (dist <= max_dist)
    if first:
        ok = ok & (col >= BLK)
    return jnp.where(ok, dist.astype(F32) * (-float(dist_unit)), -jnp.inf)


def _half_mask(shape, e):
    lane = lax.broadcasted_iota(I32, shape, 1)
    return (lane < HD) if e == 0 else (lane >= HD)


def _to_half(x, e, g):
    if g != e:
        x = pltpu.roll(x, HD, 1)
    return jnp.where(_half_mask(x.shape, g), x, 0.0)


def _stack_heads(scalars, tile):
    return jnp.concatenate([scalars[0] * tile, scalars[1] * tile], axis=0)


def _pair_fwd(q2, kb, vb, base, slopes, kv_heads, sinks):
    lo = _half_mask((BLK, 2 * HD), 0)
    if slopes is None:
        bias = base
    elif sinks is None:
        bias = _stack_heads(slopes, base)
    else:
        col0 = lax.broadcasted_iota(I32, base.shape, 1) == 0
        bias = jnp.concatenate([jnp.where(col0, sinks[e], slopes[e] * base) for e in (0, 1)], axis=0)
    qs = jnp.concatenate([_to_half(q2, e, kv_heads[e]) * SCALE for e in (0, 1)], axis=0).astype(BF16)
    s = _nt(qs, kb) + bias
    m = jnp.max(s, axis=1, keepdims=True)
    p = jnp.exp(s - m)
    l = jnp.sum(p, axis=1, keepdims=True)
    o = _nn(p.astype(BF16), vb) / l
    lse = m + jnp.log(l)
    halves = []
    for e in (0, 1):
        oh = o[e * BLK:(e + 1) * BLK]
        halves.append(pltpu.roll(oh, HD, 1) if kv_heads[e] != e else oh)
    o2 = jnp.where(lo, halves[0], halves[1])
    lse2 = jnp.where(lo, jnp.broadcast_to(lse[:BLK], (BLK, 2 * HD)), jnp.broadcast_to(lse[BLK:], (BLK, 2 * HD)))
    return o2, lse2


def _pair_bwd(q2, kb, vb, do2, o2, lse2, base, slopes, kv_heads, sinks):
    lo = _half_mask((BLK, 2 * HD), 0)
    prod = do2 * o2
    lses, deltas = [], []
    for e in (0, 1):
        hq = _half_mask((BLK, 2 * HD), e)
        lses.append(jnp.max(jnp.where(hq, lse2, -jnp.inf), axis=1, keepdims=True))
        deltas.append(jnp.sum(jnp.where(hq, prod, 0.0), axis=1, keepdims=True))
    lse = jnp.concatenate(lses, axis=0)
    delta = jnp.concatenate(deltas, axis=0)
    qs = jnp.concatenate([_to_half(q2, e, kv_heads[e]) * SCALE for e in (0, 1)], axis=0).astype(BF16)
    dos = jnp.concatenate([_to_half(do2, e, kv_heads[e]) for e in (0, 1)], axis=0).astype(BF16)
    p = jnp.exp(_nt(qs, kb) + (base if slopes is None else _stack_heads(slopes, base)) - lse)
    ds = (p * (_nt(dos, vb) - delta)).astype(BF16)
    dq = _nn(ds, kb) * SCALE
    halves = []
    for e in (0, 1):
        dqh = dq[e * BLK:(e + 1) * BLK]
        halves.append(pltpu.roll(dqh, HD, 1) if kv_heads[e] != e else dqh)
    dq2 = jnp.where(lo, halves[0], halves[1])
    dk2 = _tn(ds, qs)
    dv2 = _tn(p.astype(BF16), dos)
    dsinks = []
    if sinks is not None:
        for e in (0, 1):
            dsinks.append(jnp.sum(-jnp.exp(sinks[e] - lses[e]) * deltas[e], axis=0, keepdims=True))
    return dq2, dk2, dv2, dsinks


A_BLOCKS_PER_STEP = 2
A_BLOCKS_PER_STEP_BWD = 1


def _attn_a_fwd(proj, sinks):
    s = proj.shape[1]
    nq = A_BLOCKS_PER_STEP
    rows = BLK * nq
    steps = s // rows

    def body(sink_ref, q_ref, kp_ref, kc_ref, vp_ref, vc_ref, o_ref, lse_ref):
        n = pl.program_id(0)
        base_rest = _band_base(A_MAX_DIST, 1, False)
        base_0 = jnp.where(n > 0, base_rest, _band_base(A_MAX_DIST, 1, True))
        for i in range(nq):
            cur = pl.ds(i * BLK, BLK)
            k_prev = kc_ref[pl.ds((i - 1) * BLK, BLK), :] if i > 0 else kp_ref[...]
            v_prev = vc_ref[pl.ds((i - 1) * BLK, BLK), :] if i > 0 else vp_ref[...]
            first_key = lax.broadcasted_iota(I32, (2 * BLK, 128), 0) == 0
            kb = jnp.where(first_key, 0.0, jnp.concatenate([k_prev, kc_ref[cur, :]], axis=0)).astype(BF16)
            vb = jnp.where(first_key, 0.0, jnp.concatenate([v_prev, vc_ref[cur, :]], axis=0)).astype(BF16)
            for j in range(NH // 2):
                g = j // 2
                o2, lse2 = _pair_fwd(q_ref[j, cur, :], kb, vb, base_rest if i > 0 else base_0,
                                     (SLOPES[2 * j], SLOPES[2 * j + 1]), (g, g), (sink_ref[2 * j], sink_ref[2 * j + 1]))
                o_ref[j, cur, :] = o2
                lse_ref[j, cur, :] = lse2

    before = lambda n: jnp.maximum(n * nq - 1, 0)
    slab = lambda g: pl.BlockSpec((None, rows, 128), lambda n: (g, n, 0))
    edge = lambda g: pl.BlockSpec((None, BLK, 128), lambda n: (g, before(n), 0))
    quad = pl.BlockSpec((4, rows, 128), lambda n: (0, n, 0))
    return pl.pallas_call(
        body, name="attn_a_fwd", grid=(steps,),
        in_specs=[SMEM, quad, edge(4), slab(4), edge(5), slab(5)],
        out_specs=[quad, quad],
        out_shape=[jax.ShapeDtypeStruct((4, s, 128), F32)] * 2,
        compiler_params=_cp(("parallel",)),
    )(sinks, proj, proj, proj, proj, proj)


def _attn_a_bwd(proj, sinks, d_o, o, lse):
    s = proj.shape[1]
    nq = A_BLOCKS_PER_STEP_BWD
    rows = BLK * nq
    steps = s // rows

    def body(sink_ref, q_ref, kp_ref, kc_ref, vp_ref, vc_ref, do_ref, o_ref, lse_ref,
             dq_ref, dk_ref, dv_ref, dsink_ref, kcar, vcar):
        n = pl.program_id(0)

        @pl.when(n == 0)
        def _():
            kcar[...] = jnp.zeros_like(kcar)
            vcar[...] = jnp.zeros_like(vcar)
            dsink_ref[...] = jnp.zeros_like(dsink_ref)

        dk_ref[...] = kcar[...].astype(BF16)
        dv_ref[...] = vcar[...].astype(BF16)

        @pl.when(n < steps)
        def _():
            base_rest = _band_base(A_MAX_DIST, 1, False)
            base_0 = jnp.where(n > 0, base_rest, _band_base(A_MAX_DIST, 1, True))
            for i in range(nq):
                cur = pl.ds(i * BLK, BLK)
                k_prev = kc_ref[pl.ds((i - 1) * BLK, BLK), :] if i > 0 else kp_ref[...]
                v_prev = vc_ref[pl.ds((i - 1) * BLK, BLK), :] if i > 0 else vp_ref[...]
                kb = jnp.concatenate([k_prev, kc_ref[cur, :]], axis=0).astype(BF16)
                vb = jnp.concatenate([v_prev, vc_ref[cur, :]], axis=0).astype(BF16)
                dk_win = dv_win = None
                for j in range(NH // 2):
                    g = j // 2
                    dq2, dk2, dv2, dsk = _pair_bwd(q_ref[j, cur, :], kb, vb, do_ref[j, cur, :], o_ref[j, cur, :],
                                                   lse_ref[j, cur, :], base_rest if i > 0 else base_0,
                                                   (SLOPES[2 * j], SLOPES[2 * j + 1]), (g, g),
                                                   (sink_ref[2 * j], sink_ref[2 * j + 1]))
                    dq_ref[j, cur, :] = dq2.astype(BF16)
                    dk_win = dk2 if j == 0 else dk_win + dk2
                    dv_win = dv2 if j == 0 else dv_win + dv2
                    for e in (0, 1):
                        h = 2 * j + e
                        dsink_ref[h:h + 1, :] += jnp.broadcast_to(dsk[e], (1, 128))
                if i == 0:
                    last = pl.ds((nq - 1) * BLK, BLK)
                    dk_ref[last, :] = (kcar[last, :] + dk_win[:BLK]).astype(BF16)
                    dv_ref[last, :] = (vcar[last, :] + dv_win[:BLK]).astype(BF16)
                else:
                    kcar[pl.ds((i - 1) * BLK, BLK), :] += dk_win[:BLK]
                    vcar[pl.ds((i - 1) * BLK, BLK), :] += dv_win[:BLK]
                kcar[cur, :] = dk_win[BLK:]
                vcar[cur, :] = dv_win[BLK:]

    cur_step = lambda n: jnp.minimum(n, steps - 1)
    before = lambda n: jnp.maximum(cur_step(n) * nq - 1, 0)
    out_prev = lambda n: jnp.maximum(n - 1, 0)
    quad = pl.BlockSpec((4, rows, 128), lambda n: (0, cur_step(n), 0))
    slab = lambda g: pl.BlockSpec((None, rows, 128), lambda n: (g, cur_step(n), 0))
    edge = lambda g: pl.BlockSpec((None, BLK, 128), lambda n: (g, before(n), 0))
    return pl.pallas_call(
        body, name="attn_a_bwd", grid=(steps + 1,),
        in_specs=[SMEM, quad, edge(4), slab(4), edge(5), slab(5), quad, quad, quad],
        out_specs=[quad,
                   pl.BlockSpec((rows, 128), lambda n: (out_prev(n), 0)),
                   pl.BlockSpec((rows, 128), lambda n: (out_prev(n), 0)),
                   pl.BlockSpec((NH, 128), lambda n: (0, 0))],
        out_shape=[pltpu.HBM((4, s, 128), BF16), pltpu.HBM((s, 128), BF16), pltpu.HBM((s, 128), BF16),
                   jax.ShapeDtypeStruct((NH, 128), F32)],
        scratch_shapes=[pltpu.VMEM((rows, 128), F32), pltpu.VMEM((rows, 128), F32)],
        compiler_params=_cp(("arbitrary",)),
    )(sinks, proj, proj, proj, proj, proj, d_o, o, lse)


def _stream(rho, i, r):
    start = i * BLK * r + rho
    return pl.ds(start, BLK, stride=r) if r > 1 else pl.ds(start, BLK)


def _for_streams(r, fn, side_by_side=4):
    if r <= side_by_side:
        for rho in range(r):
            fn(rho)
    else:
        def group(it, carry):
            for u in range(side_by_side):
                fn(side_by_side * it + u)
            return carry

        lax.fori_loop(0, r // side_by_side, group, 0)


B_BLOCKS_PER_STEP = {1: 8, 4: 2, 16: 1}
B_BLOCKS_PER_STEP_FWD = {1: 16, 4: 4, 16: 1}


def _attn_b_fwd(proj, slopes, r, so_far=None):
    s = proj.shape[1]
    nq = B_BLOCKS_PER_STEP_FWD[r]
    rows = BLK * r * nq
    steps = s // rows
    qc, kc, vc = WA // 128, WA // 128 + 4, WA // 128 + 8
    chained = so_far is not None

    def body(slope_ref, q_ref, kp_ref, kc_ref, vp_ref, vc_ref, *rest):
        po_ref, pl_ref = rest[:2] if chained else (None, None)
        o_ref, lse_ref = rest[-2:]
        j = pl.program_id(0)
        sb = pl.program_id(1)
        sl2 = (slope_ref[2 * j], slope_ref[2 * j + 1])
        bias_rest = _stack_heads(sl2, _band_base(B_MAX_DIST, r, False))
        bias_0 = jnp.where(sb > 0, bias_rest, _stack_heads(sl2, _band_base(B_MAX_DIST, r, True)))

        def stream(rho):
            for i in range(nq):
                cur = _stream(rho, i, r)
                k_prev = kc_ref[_stream(rho, i - 1, r), :] if i > 0 else kp_ref[_stream(rho, 0, r), :]
                v_prev = vc_ref[_stream(rho, i - 1, r), :] if i > 0 else vp_ref[_stream(rho, 0, r), :]
                kb = jnp.concatenate([k_prev, kc_ref[cur, :]], axis=0).astype(BF16)
                vb = jnp.concatenate([v_prev, vc_ref[cur, :]], axis=0).astype(BF16)
                o2, lse2 = _pair_fwd(q_ref[cur, :], kb, vb, bias_rest if i > 0 else bias_0, None, (0, 1), None)
                if chained:
                    lse1 = pl_ref[cur, :]
                    m = jnp.maximum(lse1, lse2)
                    e1, e2 = jnp.exp(lse1 - m), jnp.exp(lse2 - m)
                    den = e1 + e2
                    o2 = (e1 * po_ref[cur, :] + e2 * o2) * (1.0 / den)
                    lse2 = m + jnp.log(den)
                o_ref[cur, :] = o2
                lse_ref[cur, :] = lse2

        _for_streams(r, stream, side_by_side=16)

    before = lambda sb: jnp.maximum(sb * nq - 1, 0)
    result = pl.BlockSpec((None, rows, 128), lambda j, sb: (j, sb, 0))
    return pl.pallas_call(
        body, name=f"attn_b_fwd_r{r}", grid=(NH // 2, steps),
        in_specs=[SMEM,
                  pl.BlockSpec((None, rows, 128), lambda j, sb: (qc + j, sb, 0)),
                  pl.BlockSpec((None, BLK * r, 128), lambda j, sb: (kc + j, before(sb), 0)),
                  pl.BlockSpec((None, rows, 128), lambda j, sb: (kc + j, sb, 0)),
                  pl.BlockSpec((None, BLK * r, 128), lambda j, sb: (vc + j, before(sb), 0)),
                  pl.BlockSpec((None, rows, 128), lambda j, sb: (vc + j, sb, 0))] + ([result] * 2 if chained else []),
        out_specs=[result] * 2,
        out_shape=[jax.ShapeDtypeStruct((4, s, 128), F32)] * 2,
        compiler_params=_cp(("parallel", "parallel")),
    )(slopes, proj, proj, proj, proj, proj, *(so_far if chained else ()))


def _attn_b_bwd(proj, slopes, d_o, o, lse, r, so_far=None, dtype=F32):
    s = proj.shape[1]
    nq = B_BLOCKS_PER_STEP[r]
    rows = BLK * r * nq
    steps = s // rows
    qc, kc, vc = WA // 128, WA // 128 + 4, WA // 128 + 8
    chained = so_far is not None

    def body(slope_ref, q_ref, kp_ref, kc_ref, vp_ref, vc_ref, do_ref, o_ref, lse_ref, *rest):
        pq_ref, pk_ref, pv_ref = rest[:3] if chained else (None, None, None)
        dq_ref, dk_ref, dv_ref, kcar, vcar = rest[-5:]
        j = pl.program_id(0)
        sb = pl.program_id(1)

        @pl.when(sb == 0)
        def _():
            kcar[...] = jnp.zeros_like(kcar)
            vcar[...] = jnp.zeros_like(vcar)

        def settled(car, p_ref, idx):
            return car[idx] + p_ref[idx] if chained else car[idx]

        dk_ref[...] = settled(kcar, pk_ref, ...).astype(dtype)
        dv_ref[...] = settled(vcar, pv_ref, ...).astype(dtype)

        @pl.when(sb < steps)
        def _():
            sl2 = (slope_ref[2 * j], slope_ref[2 * j + 1])
            bias_rest = _stack_heads(sl2, _band_base(B_MAX_DIST, r, False))
            bias_0 = jnp.where(sb > 0, bias_rest, _stack_heads(sl2, _band_base(B_MAX_DIST, r, True)))

            def stream(rho):
                for i in range(nq):
                    cur = _stream(rho, i, r)
                    k_prev = kc_ref[_stream(rho, i - 1, r), :] if i > 0 else kp_ref[_stream(rho, 0, r), :]
                    v_prev = vc_ref[_stream(rho, i - 1, r), :] if i > 0 else vp_ref[_stream(rho, 0, r), :]
                    kb = jnp.concatenate([k_prev, kc_ref[cur, :]], axis=0).astype(BF16)
                    vb = jnp.concatenate([v_prev, vc_ref[cur, :]], axis=0).astype(BF16)
                    dq2, dk2, dv2, _ = _pair_bwd(q_ref[cur, :], kb, vb, do_ref[cur, :], o_ref[cur, :], lse_ref[cur, :],
                                                 bias_rest if i > 0 else bias_0, None, (0, 1), None)
                    dq_ref[cur, :] = (dq2 + pq_ref[cur, :] if chained else dq2).astype(dtype)
                    if i == 0:
                        last = (_stream(rho, nq - 1, r), slice(None))
                        dk_ref[last] = (settled(kcar, pk_ref, last) + dk2[:BLK]).astype(dtype)
                        dv_ref[last] = (settled(vcar, pv_ref, last) + dv2[:BLK]).astype(dtype)
                    else:
                        kcar[_stream(rho, i - 1, r), :] += dk2[:BLK]
                        vcar[_stream(rho, i - 1, r), :] += dv2[:BLK]
                    kcar[cur, :] = dk2[BLK:]
                    vcar[cur, :] = dv2[BLK:]

            _for_streams(r, stream, side_by_side=4)

    cur_step = lambda sb: jnp.minimum(sb, steps - 1)
    before = lambda sb: jnp.maximum(cur_step(sb) * nq - 1, 0)
    out_prev = lambda sb: jnp.maximum(sb - 1, 0)
    tile = lambda slab: pl.BlockSpec((None, rows, 128), lambda j, sb: (slab + j, cur_step(sb), 0))
    edge = lambda slab: pl.BlockSpec((None, BLK * r, 128), lambda j, sb: (slab + j, before(sb), 0))
    late = pl.BlockSpec((None, rows, 128), lambda j, sb: (j, out_prev(sb), 0))
    grads = [tile(0), late, late]
    return pl.pallas_call(
        body, name=f"attn_b_bwd_r{r}", grid=(NH // 2, steps + 1),
        in_specs=[SMEM, tile(qc), edge(kc), tile(kc), edge(vc), tile(vc), tile(0), tile(0), tile(0)]
        + (grads if chained else []),
        out_specs=grads,
        out_shape=[pltpu.HBM((4, s, 128), dtype)] * 3,
        scratch_shapes=[pltpu.VMEM((rows, 128), F32), pltpu.VMEM((rows, 128), F32)],
        compiler_params=_cp(("parallel", "arbitrary")),
    )(slopes, proj, proj, proj, proj, proj, d_o, o, lse, *(so_far if chained else ()))


def _row(v):
    return v.reshape(1, -1)


def _layer_norm_stats(z):
    mu = jnp.mean(z, axis=-1, keepdims=True)
    zc = z - mu
    var = jnp.mean(zc * zc, axis=-1, keepdims=True)
    rstd = lax.rsqrt(var + LN_EPS)
    return zc * rstd, rstd


def _layer_norm_bwd(dh, zh, rstd, g):
    dzh = dh * g
    return rstd * (dzh - jnp.mean(dzh, axis=-1, keepdims=True) - zh * jnp.mean(dzh * zh, axis=-1, keepdims=True))


def _rms(o):
    return lax.rsqrt(jnp.mean(o * o, axis=-1, keepdims=True) + RMS_EPS)


def _mix_ln1(x, o_a, o_b, norm_a_g, norm_b_g, w_o, ln1_g, ln1_b, tm=512):
    s = x.shape[0]

    def wide(ref):
        return jnp.concatenate([ref[j] for j in range(4)], axis=1)

    def body(x_ref, oa_ref, ob_ref, ga_ref, gb_ref, wo_ref, g_ref, b_ref, cat_ref, z1_ref, h1_ref, h1b_ref):
        oa, ob = wide(oa_ref), wide(ob_ref)
        na = oa * _rms(oa) * ga_ref[...]
        nb_ = ob * _rms(ob) * gb_ref[...]
        cat = jnp.concatenate([na, nb_], axis=1).astype(BF16)
        cat_ref[...] = cat
        z1 = ALPHA * x_ref[...] + _nn(cat, wo_ref[...])
        z1_ref[...] = z1
        zh, _ = _layer_norm_stats(z1)
        h1 = zh * g_ref[...] + b_ref[...]
        h1_ref[...] = h1
        h1b_ref[...] = h1.astype(BF16)

    t512 = pl.BlockSpec((4, tm, 128), lambda i: (0, i, 0))
    td = pl.BlockSpec((tm, D), lambda i: (i, 0))
    return pl.pallas_call(
        body, name="mix_ln1", grid=(s // tm,),
        in_specs=[td] + [t512] * 2 + [_const((1, 512))] * 2 + [_resident((D, D))] + [_const((1, D))] * 2,
        out_specs=[td, td, td, td],
        out_shape=[jax.ShapeDtypeStruct((s, D), BF16), jax.ShapeDtypeStruct((s, D), F32),
                   jax.ShapeDtypeStruct((s, D), F32), jax.ShapeDtypeStruct((s, D), BF16)],
        compiler_params=_cp(("parallel",)),
    )(x, o_a, o_b, _row(norm_a_g), _row(norm_b_g), w_o, _row(ln1_g), _row(ln1_b))


def _gelu_and_grad(x):
    c = math.sqrt(2.0 / math.pi)
    x2 = x * x
    s = 0.5 * jnp.tanh(x * ((c * 0.044715) * x2 + c)) + 0.5
    dg = s + (x * ((6.0 * c * 0.044715) * x2 + 2.0 * c)) * (s - s * s)
    return x * s, dg


def _shifted(u, edge, row, down):
    groups = [u[8 * i:8 * i + 8] for i in range(u.shape[0] // 8)]
    others = [edge] + groups[:-1] if down else groups[1:] + [edge]
    moved = []
    for k in (1, 2):
        crossing = row >= 8 - k if down else row < k
        moved.append(jnp.concatenate([pltpu.roll(jnp.where(crossing, o, g), k if down else 8 - k, 0)
                                      for o, g in zip(others, groups)], axis=0))
    return moved


def _up_proj(h1b, w_up, tm=512):
    s = h1b.shape[0]

    def body(h_ref, w_ref, o_ref):
        h = h_ref[...]
        for half in (0, 1):
            o_ref[half] = _nn(h, w_ref[:, half * FF:(half + 1) * FF]).astype(BF16)

    return pl.pallas_call(
        body, name="up_proj", grid=(s // tm,),
        in_specs=[pl.BlockSpec((tm, D), lambda i: (i, 0)), _resident((D, 2 * FF))],
        out_specs=pl.BlockSpec((2, tm, FF), lambda i: (0, i, 0)),
        out_shape=jax.ShapeDtypeStruct((2, s, FF), BF16),
        compiler_params=_cp(("parallel",)),
    )(h1b, w_up)


def _conv_gelu(up, cwb, tm=512, tn=FF // 2, chunk_rows=16):
    s = up.shape[1]
    n_c = tm // chunk_rows

    def body(up_ref, c_ref, a_ref, g_ref, a1_ref, carry):
        @pl.when(pl.program_id(1) == 0)
        def _():
            carry[...] = jnp.zeros_like(carry)

        row = lax.broadcasted_iota(jnp.int32, (8, tn), 0)
        edge = [carry[0], carry[1]]
        for c in range(n_c):
            rows = pl.ds(c * chunk_rows, chunk_rows)
            u = []
            for half in (0, 1):
                x = up_ref[half, rows, :].astype(F32)
                r1, r2 = _shifted(x, edge[half], row, True)
                u.append(r2 * c_ref[0, half:half + 1, :] + r1 * c_ref[1, half:half + 1, :]
                         + x * c_ref[2, half:half + 1, :] + c_ref[3, half:half + 1, :])
                edge[half] = x[chunk_rows - 8:]
            g, dg = _gelu_and_grad(u[0])
            a_ref[rows, :] = (g * u[1]).astype(BF16)
            g_ref[rows, :] = g.astype(BF16)
            a1_ref[rows, :] = (u[1] * dg).astype(BF16)
        for half in (0, 1):
            carry[half] = edge[half]

    pair = pl.BlockSpec((2, tm, tn), lambda j, i: (0, i, j))
    tile = pl.BlockSpec((tm, tn), lambda j, i: (i, j))
    return pl.pallas_call(
        body, name="conv_gelu", grid=(FF // tn, s // tm),
        in_specs=[pair, pl.BlockSpec((4, 2, tn), lambda j, i: (0, 0, j))],
        out_specs=[tile, tile, tile],
        out_shape=[jax.ShapeDtypeStruct((s, FF), BF16)] * 3,
        scratch_shapes=[pltpu.VMEM((2, 8, tn), F32)],
        compiler_params=_cp(("parallel", "arbitrary")),
    )(up, cwb)


def _down_ln2_loss(a, w_down, h1, target, ln2_g, ln2_b, tm=512):
    s = a.shape[0]

    def body(a_ref, w_ref, h_ref, t_ref, g_ref, b_ref, dz_ref, dzb_ref, st_ref):
        @pl.when(pl.program_id(0) == 0)
        def _():
            st_ref[...] = jnp.zeros_like(st_ref)

        z2 = ALPHA * h_ref[...] + _nn(a_ref[...], w_ref[...])
        zh, rstd = _layer_norm_stats(z2)
        diff = zh * g_ref[...] + b_ref[...] - t_ref[...]
        part = 0.5 * jnp.sum(jnp.mean(diff * diff, axis=-1, keepdims=True), axis=0, keepdims=True)
        dy = diff * (1.0 / D)
        st_ref[0:1, :] += jnp.sum(dy * zh, axis=0, keepdims=True)
        st_ref[1:2, :] += jnp.sum(dy, axis=0, keepdims=True)
        st_ref[2:3, :] += jnp.broadcast_to(part, (1, D))
        dz = _layer_norm_bwd(dy, zh, rstd, g_ref[...])
        dz_ref[...] = dz
        dzb_ref[...] = dz.astype(BF16)

    td = pl.BlockSpec((tm, D), lambda i: (i, 0))
    return pl.pallas_call(
        body, name="down_ln2_loss", grid=(s // tm,),
        in_specs=[pl.BlockSpec((tm, FF), lambda i: (i, 0)), _resident((FF, D)), td, td, _const((1, D)), _const((1, D))],
        out_specs=[td, td, _const((8, D))],
        out_shape=[jax.ShapeDtypeStruct((s, D), F32), jax.ShapeDtypeStruct((s, D), BF16),
                   jax.ShapeDtypeStruct((8, D), F32)],
        compiler_params=_cp(("arbitrary",)),
    )(a, w_down, h1, target, _row(ln2_g), _row(ln2_b))


def _d_act(dz2b, w_down, tm=512):
    s = dz2b.shape[0]

    def body(dz_ref, w_ref, o_ref):
        o_ref[...] = _nt(dz_ref[...], w_ref[...]).astype(BF16)

    return pl.pallas_call(
        body, name="d_act", grid=(s // tm,),
        in_specs=[pl.BlockSpec((tm, D), lambda i: (i, 0)), _resident((FF, D))],
        out_specs=pl.BlockSpec((tm, FF), lambda i: (i, 0)),
        out_shape=jax.ShapeDtypeStruct((s, FF), BF16),
        compiler_params=_cp(("parallel",)),
    )(dz2b, w_down)


def _conv_gelu_bwd(da, up, g, a1, cwb, tm=512, tn=FF // 2, chunk_rows=16):
    s = da.shape[0]
    n_i = s // tm
    n_c = tm // chunk_rows

    def body(da_ref, up_ref, g_ref, a1_ref, c_ref, dup_ref, dc_ref, carry):
        @pl.when(pl.program_id(1) == 0)
        def _():
            carry[...] = jnp.zeros_like(carry)
            dc_ref[...] = jnp.zeros_like(dc_ref)

        def fold(v):
            return jnp.sum(v.reshape(chunk_rows // 8, 8, v.shape[1]), axis=0)

        def chunk(cc, state):
            after, sums = state
            rows = pl.ds((n_c - 1 - cc) * chunk_rows, chunk_rows)
            da_c = da_ref[rows, :].astype(F32)
            dus = (da_c * a1_ref[rows, :].astype(F32), da_c * g_ref[rows, :].astype(F32))
            head, new_sums = [], []
            for half in (0, 1):
                du = dus[half]
                up = up_ref[half, rows, :].astype(F32)
                l1, l2 = _shifted(du, after[half], row, False)
                dup = (du * c_ref[2, half:half + 1, :] + l1 * c_ref[1, half:half + 1, :]
                       + l2 * c_ref[0, half:half + 1, :])
                dup_ref[half, rows, :] = dup.astype(BF16)
                parts = (fold(l2 * up), fold(l1 * up), fold(du * up), fold(du))
                new_sums.append(parts if sums is None else tuple(a + b for a, b in zip(sums[half], parts)))
                head.append(du[:8])
            return tuple(head), new_sums

        row = lax.broadcasted_iota(jnp.int32, (8, tn), 0)
        state = ((carry[0], carry[1]), None)
        for cc in range(n_c):
            state = chunk(cc, state)
        head, sums = state
        for half in (0, 1):
            carry[half] = head[half]
            for k in range(4):
                dc_ref[k, half:half + 1, :] += jnp.sum(sums[half][k], axis=0, keepdims=True)

    rev = lambda ii: n_i - 1 - ii
    tile = pl.BlockSpec((tm, tn), lambda j, ii: (rev(ii), j))
    pair = pl.BlockSpec((2, tm, tn), lambda j, ii: (0, rev(ii), j))
    per_col = pl.BlockSpec((4, 2, tn), lambda j, ii: (0, 0, j))
    return pl.pallas_call(
        body, name="conv_gelu_bwd", grid=(FF // tn, n_i),
        in_specs=[tile, pair, tile, tile, per_col],
        out_specs=[pair, per_col],
        out_shape=[jax.ShapeDtypeStruct((2, s, FF), BF16), jax.ShapeDtypeStruct((4, 2, FF), F32)],
        scratch_shapes=[pltpu.VMEM((2, 8, tn), F32)],
        compiler_params=_cp(("parallel", "arbitrary")),
    )(da, up, g, a1, cwb)


def _dh1_ln1_bwd(dz2, dup, w_up, z1, ln1_g, tm=512):
    s = dz2.shape[0]

    def body(dz2_ref, dup_ref, w_ref, z1_ref, g_ref, dz1_ref, dz1b_ref, st_ref):
        @pl.when(pl.program_id(0) == 0)
        def _():
            st_ref[...] = jnp.zeros_like(st_ref)

        dh = ALPHA * dz2_ref[...] + _nt(dup_ref[0], w_ref[:, :FF]) + _nt(dup_ref[1], w_ref[:, FF:])
        zh, rstd = _layer_norm_stats(z1_ref[...])
        st_ref[0:1, :] += jnp.sum(dh * zh, axis=0, keepdims=True)
        st_ref[1:2, :] += jnp.sum(dh, axis=0, keepdims=True)
        dz = _layer_norm_bwd(dh, zh, rstd, g_ref[...])
        dz1_ref[...] = dz
        dz1b_ref[...] = dz.astype(BF16)

    td = pl.BlockSpec((tm, D), lambda i: (i, 0))
    return pl.pallas_call(
        body, name="dh1_ln1_bwd", grid=(s // tm,),
        in_specs=[td, pl.BlockSpec((2, tm, FF), lambda i: (0, i, 0)), _resident((D, 2 * FF)), td, _const((1, D))],
        out_specs=[td, td, _const((8, D))],
        out_shape=[jax.ShapeDtypeStruct((s, D), F32), jax.ShapeDtypeStruct((s, D), BF16),
                   jax.ShapeDtypeStruct((8, D), F32)],
        compiler_params=_cp(("arbitrary",), 58),
    )(dz2, dup, w_up, z1, _row(ln1_g))


def _dcat_rms_bwd(dz1b, w_o, o_a, o_b, norm_a_g, norm_b_g, tm=512):
    s = dz1b.shape[0]

    def body(dz_ref, w_ref, oa_ref, ob_ref, ga_ref, gb_ref, da_ref, db_ref, st_ref):
        @pl.when(pl.program_id(0) == 0)
        def _():
            st_ref[...] = jnp.zeros_like(st_ref)

        dcat = _nt(dz_ref[...], w_ref[...])
        for k, (o_ref, g_ref, d_ref) in enumerate(((oa_ref, ga_ref, da_ref), (ob_ref, gb_ref, db_ref))):
            o = jnp.concatenate([o_ref[j] for j in range(4)], axis=1)
            dn = dcat[:, 512 * k:512 * (k + 1)]
            rr = _rms(o)
            oh = o * rr
            st_ref[k:k + 1, :] += jnp.sum(dn * oh, axis=0, keepdims=True)
            doh = dn * g_ref[...]
            d_o = rr * (doh - oh * jnp.mean(doh * oh, axis=-1, keepdims=True))
            for j in range(4):
                d_ref[j] = d_o[:, 128 * j:128 * (j + 1)]

    t512 = pl.BlockSpec((4, tm, 128), lambda i: (0, i, 0))
    return pl.pallas_call(
        body, name="dcat_rms_bwd", grid=(s // tm,),
        in_specs=[pl.BlockSpec((tm, D), lambda i: (i, 0)), _resident((D, D)), t512, t512,
                  _const((1, 512)), _const((1, 512))],
        out_specs=[t512, t512, _const((8, 512))],
        out_shape=[jax.ShapeDtypeStruct((4, s, 128), F32), jax.ShapeDtypeStruct((4, s, 128), F32),
                   jax.ShapeDtypeStruct((8, 512), F32)],
        compiler_params=_cp(("arbitrary",)),
    )(dz1b, w_o, o_a, o_b, _row(norm_a_g), _row(norm_b_g))


def _grad_w_in(dparts, xb, tk=2048):
    s = xb.shape[0]
    nk = s // tk

    def body(qa, ka, va, qb, kb, vb, x_ref, o_ref, ob_ref):
        i = pl.program_id(0)
        k = pl.program_id(1)

        @pl.when(k == 0)
        def _():
            o_ref[...] = jnp.zeros_like(o_ref)

        def add(blocks):
            o_ref[...] += _tn(jnp.concatenate(blocks, axis=1), x_ref[...])

        pl.when(i == 0)(lambda: add([qa[j] for j in range(4)] + [ka[...], va[...]]))
        pl.when(i == 1)(lambda: add([qb[j] for j in range(4)] + [kb[0], kb[1]]))
        pl.when(i == 2)(lambda: add([kb[0], kb[1]] + [vb[j] for j in range(4)]))

        @pl.when(k == nk - 1)
        def _():
            ob_ref[...] = o_ref[...].astype(BF16)

    def during(tile):
        return lambda i, k: jnp.where(i == tile, k, jnp.where(i < tile, 0, nk - 1))

    quad = lambda tile: pl.BlockSpec((4, tk, 128), lambda i, k: (0, during(tile)(i, k), 0))
    one = pl.BlockSpec((tk, 128), lambda i, k: (during(0)(i, k), 0))
    kb_spec = pl.BlockSpec((2, tk, 128), lambda i, k: (jnp.where(i == 2, 1, 0), jnp.where(i == 0, 0, k), 0))
    return pl.pallas_call(
        body, name="grad_w_in", grid=(3, nk),
        in_specs=[quad(0), one, one, quad(1), kb_spec, quad(2), pl.BlockSpec((tk, D), lambda i, k: (k, 0))],
        out_specs=[pl.BlockSpec((WA, D), lambda i, k: (i, 0))] * 2,
        out_shape=[pltpu.HBM((WIN, D), F32), pltpu.HBM((WIN, D), BF16)],
        compiler_params=_cp(("parallel", "arbitrary"), mb=56),
    )(*dparts, xb)


def _grad_x(dz1, dparts, w_in_t, zero, tm=512):
    s = dz1.shape[0]

    def body(dz_ref, qa, ka, va, qb, kb, vb, w_ref, z_ref, o_ref):
        dp = jnp.concatenate([qa[j] for j in range(4)] + [ka[...], va[...]]
                             + [ref[j] for ref in (qb, kb, vb) for j in range(4)], axis=1)
        o_ref[...] = ALPHA * dz_ref[...] + _nn(dp, w_ref[...]) + z_ref[0:1, 0:1]

    td = pl.BlockSpec((tm, D), lambda i: (i, 0))
    quad = pl.BlockSpec((4, tm, 128), lambda i: (0, i, 0))
    one = pl.BlockSpec((tm, 128), lambda i: (i, 0))
    return pl.pallas_call(
        body, name="grad_x", grid=(s // tm,),
        in_specs=[td, quad, one, one, quad, quad, quad, _resident((WIN, D)), _const((8, 128))],
        out_specs=td, out_shape=jax.ShapeDtypeStruct((s, D), F32),
        compiler_params=_cp(("parallel",)),
    )(dz1, *dparts, w_in_t, zero)


def _place():
    return lax.axis_index("x"), lax.axis_index("y"), lax.axis_index("c")


def _other_chips(x, y):
    return [(1 - x, y), (x, 1 - y), (1 - x, 1 - y)]


def _hbm(a):
    return pltpu.with_memory_space_constraint(a, pltpu.HBM)


def _gather_w_in(shard, conv_w):
    rows_k = shard.shape[0]
    half = rows_k // 2

    def body(src, conv_src, out, conv_out, send_sems, recv_sems):
        x, y, c = _place()
        b = 2 * x + y
        sibling = (x, y, 1 - c)
        chips = _other_chips(x, y)

        def copy(idx, chip_b, core, to, first_hop=False):
            rows = out.at[pl.ds(pl.multiple_of(chip_b * rows_k + core * half, 16), half)]
            s_ref = src.at[pl.ds(pl.multiple_of(core * half, 16), half)] if first_hop else rows
            return pltpu.make_async_remote_copy(src_ref=s_ref, dst_ref=rows, send_sem=send_sems.at[idx],
                                                recv_sem=recv_sems.at[idx], device_id=to, device_id_type=MESH)

        def own_copy():
            return pltpu.make_async_remote_copy(
                src_ref=src, dst_ref=out.at[pl.ds(pl.multiple_of(b * rows_k, 16), rows_k)], send_sem=send_sems.at[6],
                recv_sem=recv_sems.at[6], device_id=sibling, device_id_type=MESH)

        def conv_copy(idx, chip_b, to):
            return pltpu.make_async_remote_copy(src_ref=conv_src, dst_ref=conv_out.at[chip_b],
                                                send_sem=send_sems.at[7 + idx], recv_sem=recv_sems.at[7 + idx],
                                                device_id=to, device_id_type=MESH)

        started = [own_copy(), conv_copy(3, b, sibling)]
        for jn, chip in enumerate(chips):
            started += [copy(jn, b, c, (chip[0], chip[1], c), first_hop=True), conv_copy(jn, b, (chip[0], chip[1], c))]
        for cp in started:
            cp.start()
        for jn, chip in enumerate(chips):
            cb = 2 * chip[0] + chip[1]
            copy(jn, cb, c, (chip[0], chip[1], c)).wait_recv()
            cp = copy(3 + jn, cb, c, sibling)
            cp.start()
            started.append(cp)
        for jn, chip in enumerate(chips):
            cb = 2 * chip[0] + chip[1]
            copy(3 + jn, cb, 1 - c, sibling).wait_recv()
            conv_copy(jn, cb, (chip[0], chip[1], c)).wait_recv()
        own_copy().wait_recv()
        conv_copy(3, b, sibling).wait_recv()
        for cp in started:
            cp.wait_send()

    return pl.pallas_call(
        body, name="gather_w_in",
        in_specs=[ANY, ANY], out_specs=[ANY, ANY],
        out_shape=[jax.ShapeDtypeStruct((N_CHIPS * rows_k, D), BF16), jax.ShapeDtypeStruct((N_CHIPS,) + conv_w.shape, F32)],
        scratch_shapes=[pltpu.SemaphoreType.DMA((11,)), pltpu.SemaphoreType.DMA((11,))],
        compiler_params=pltpu.CompilerParams(has_side_effects=True),
    )(shard, conv_w)


def _weight_copies(shard, land, send_sems, recv_sems, arrivals):
    x, y, c = _place()
    n_rows, n_cols = shard.shape
    peers = [(px, py, c) for px, py in _other_chips(x, y)] + [(x, y, 1 - c)]
    cps = []
    for jn, peer in enumerate(peers):
        at = 2 * peer[0] + peer[1] if arrivals else 2 * x + y
        if land.shape[1] == n_cols:
            dst = land.at[pl.ds(pl.multiple_of(at * n_rows, 16), n_rows)]
        else:
            dst = land.at[:, pl.ds(pl.multiple_of(at * n_cols, 128), n_cols)]
        cps.append(pltpu.make_async_remote_copy(src_ref=shard, dst_ref=dst, send_sem=send_sems.at[jn],
                                                recv_sem=recv_sems.at[jn], device_id=peer, device_id_type=MESH))
    return cps


def _weights_start(shards, after):
    n = len(shards)
    lands = [lax.empty((N_CHIPS * sh.shape[0], D) if sh.shape[1] == D else (D, N_CHIPS * sh.shape[1]), BF16)
             for sh in shards]

    def body(*refs):
        src, land = refs[:n], refs[n:2 * n]
        send_sems, recv_sems = refs[2 * n + 1:3 * n + 1], refs[3 * n + 1:4 * n + 1]
        for k in range(n):
            for send in _weight_copies(src[k], land[k], send_sems[k], recv_sems[k], False):
                send.start()
        refs[-1][...] = jnp.zeros_like(refs[-1])

    res = pl.pallas_call(
        body, name="weights_start",
        in_specs=[HBM] * (2 * n) + [ANY], out_specs=[SEM] * (2 * n) + [HBM] * (2 * n) + [VMEM],
        out_shape=[pltpu.SemaphoreType.DMA((4,))] * (2 * n)
        + [pltpu.HBM(a.shape, a.dtype) for a in (*shards, *lands)] + [jax.ShapeDtypeStruct((8, 128), F32)],
        input_output_aliases={i: i + 2 * n for i in range(2 * n)},
        compiler_params=pltpu.CompilerParams(has_side_effects=DATAFLOW),
    )(*[_hbm(a) for a in (*shards, *lands)], after)
    return [(res[k], res[n + k], res[2 * n + k], res[3 * n + k]) for k in range(n)], res[-1]


def _weights_wait(started, after, name):
    send_sems, recv_sems, shard, land = started

    def body(s_ref, l_ref, send_ref, recv_ref, after_ref, s_out, l_out):
        for cp in _weight_copies(s_ref, l_ref, send_ref, recv_ref, True):
            cp.wait_send()
            cp.wait_recv()

    return pl.pallas_call(
        body, name=name,
        in_specs=[HBM, HBM, SEM, SEM, ANY], out_specs=[HBM, HBM],
        out_shape=[pltpu.HBM(shard.shape, shard.dtype), pltpu.HBM(land.shape, land.dtype)],
        input_output_aliases={0: 0, 1: 1},
        compiler_params=pltpu.CompilerParams(has_side_effects=DATAFLOW),
    )(shard, land, send_sems, recv_sems, after)[1]


def _grad_copies(g_ref, land_ref, send_sems, recv_sems):
    x, y, c = _place()
    cps = []
    for d in range(1, 8):
        px, py, pc = x ^ (d >> 2), y ^ ((d >> 1) & 1), c ^ (d & 1)
        cps.append(pltpu.make_async_remote_copy(
            src_ref=g_ref.at[2 * px + py, pc], dst_ref=land_ref.at[d - 1], send_sem=send_sems.at[d - 1],
            recv_sem=recv_sems.at[d - 1], device_id=(px, py, pc), device_id_type=MESH))
    return cps


def _grads_start(grads_b, name):
    n = len(grads_b)
    lands = [lax.empty((7, g.shape[2], D), BF16) for g in grads_b]

    def body(*refs):
        g, land = refs[:n], refs[n:2 * n]
        send_sems, recv_sems = refs[2 * n:3 * n], refs[3 * n:4 * n]
        for k in range(n):
            for cp in _grad_copies(g[k], land[k], send_sems[k], recv_sems[k]):
                cp.start()
        refs[-1][...] = jnp.zeros_like(refs[-1])

    res = pl.pallas_call(
        body, name=name,
        in_specs=[HBM] * (2 * n), out_specs=[SEM] * (2 * n) + [HBM] * (2 * n) + [VMEM],
        out_shape=[pltpu.SemaphoreType.DMA((7,))] * (2 * n)
        + [pltpu.HBM(a.shape, a.dtype) for a in (*grads_b, *lands)] + [jax.ShapeDtypeStruct((8, 128), F32)],
        input_output_aliases={i: i + 2 * n for i in range(2 * n)},
        compiler_params=pltpu.CompilerParams(has_side_effects=DATAFLOW),
    )(*[_hbm(a) for a in (*grads_b, *lands)])
    return [(res[k], res[n + k], res[2 * n + k], res[3 * n + k]) for k in range(n)], res[-1]


def _grads_wait(started, after, name):
    n = len(started)

    def body(*refs):
        g, land = refs[:n], refs[n:2 * n]
        send_sems, recv_sems = refs[2 * n:3 * n], refs[3 * n:4 * n]
        for k in range(n):
            for cp in _grad_copies(g[k], land[k], send_sems[k], recv_sems[k]):
                cp.wait_send()
                cp.wait_recv()

    gs = [st[2] for st in started]
    lands = [st[3] for st in started]
    res = pl.pallas_call(
        body, name=name,
        in_specs=[HBM] * (2 * n) + [SEM] * (2 * n) + [ANY], out_specs=[HBM] * (2 * n),
        out_shape=[pltpu.HBM(a.shape, a.dtype) for a in (*gs, *lands)],
        input_output_aliases={i: i for i in range(2 * n)},
        compiler_params=pltpu.CompilerParams(has_side_effects=DATAFLOW),
    )(*gs, *lands, *[st[0] for st in started], *[st[1] for st in started], after)
    return res[n:]


def _sum_partials(grad4, got, cb, name, tr):
    h = grad4.shape[2]
    per_half = h // tr

    def body(cb_ref, g_ref, o_ref, out_ref):
        acc = g_ref[...]
        for j in range(7):
            acc = acc + o_ref[j].astype(F32)
        out_ref[...] = acc

    return pl.pallas_call(
        body, name=name,
        grid_spec=pltpu.PrefetchScalarGridSpec(
            num_scalar_prefetch=1, grid=(per_half,),
            in_specs=[pl.BlockSpec((None, None, tr, D), lambda i, cb_ref: (cb_ref[1], cb_ref[0], i, 0)),
                      pl.BlockSpec((7, tr, D), lambda i, cb_ref: (0, i, 0))],
            out_specs=pl.BlockSpec((tr, D), lambda i, cb_ref: (cb_ref[0] * per_half + i, 0))),
        out_shape=pltpu.HBM((2 * h, D), F32),
        compiler_params=_cp(("arbitrary",)),
    )(cb, grad4, _hbm(got))


def _swap_halves(shards, name):
    n = len(shards)

    def body(*refs):
        out, send_sems, recv_sems = refs[n:2 * n], refs[2 * n], refs[2 * n + 1]
        x, y, c = _place()
        cps = []
        for k in range(n):
            h = shards[k].shape[0] // 2
            mine = out[k].at[pl.ds(pl.multiple_of(c * h, 8), h)]
            cp = pltpu.make_async_remote_copy(src_ref=mine, dst_ref=mine, send_sem=send_sems.at[k],
                                              recv_sem=recv_sems.at[k], device_id=(x, y, 1 - c), device_id_type=MESH)
            cp.start()
            cps.append(cp)
        for cp in cps:
            cp.wait()

    return pl.pallas_call(
        body, name=name,
        in_specs=[ANY] * n, out_specs=[ANY] * n,
        out_shape=[jax.ShapeDtypeStruct(sh.shape, F32) for sh in shards],
        input_output_aliases={k: k for k in range(n)},
        scratch_shapes=[pltpu.SemaphoreType.DMA((n,)), pltpu.SemaphoreType.DMA((n,))],
        compiler_params=pltpu.CompilerParams(has_side_effects=True),
    )(*shards)


def _small_copies(small_ref, land_ref, send_sems, recv_sems):
    x, y, c = _place()
    me = 4 * x + 2 * y + c
    cps = []
    for d in range(1, 8):
        px, py, pc = x ^ (d >> 2), y ^ ((d >> 1) & 1), c ^ (d & 1)
        cps.append(pltpu.make_async_remote_copy(
            src_ref=small_ref, dst_ref=land_ref.at[me], send_sem=send_sems.at[d - 1], recv_sem=recv_sems.at[d - 1],
            device_id=(px, py, pc), device_id_type=MESH))
    return cps


def _small_start(small):
    land = lax.empty((8,) + small.shape, F32)

    def body(s_ref, l_ref, send_sems, recv_sems, s_thru, l_thru, token):
        for cp in _small_copies(s_ref, l_ref, send_sems, recv_sems):
            cp.start()
        token[...] = jnp.zeros_like(token)

    res = pl.pallas_call(
        body, name="small_start",
        in_specs=[HBM, HBM], out_specs=[SEM, SEM, HBM, HBM, VMEM],
        out_shape=[pltpu.SemaphoreType.DMA((7,)), pltpu.SemaphoreType.DMA((7,)), pltpu.HBM(small.shape, F32),
                   pltpu.HBM(land.shape, F32), jax.ShapeDtypeStruct((8, 128), F32)],
        input_output_aliases={0: 2, 1: 3},
        compiler_params=pltpu.CompilerParams(has_side_effects=DATAFLOW),
    )(_hbm(small), _hbm(land))
    return res[:4], res[4]


def _small_wait(started, after):
    send_sems, recv_sems, small, land = started

    def body(s_ref, l_ref, send_ref, recv_ref, after_ref, s_out, l_out):
        for cp in _small_copies(s_ref, l_ref, send_ref, recv_ref):
            cp.wait_send()
            cp.wait_recv()

    return pl.pallas_call(
        body, name="small_wait",
        in_specs=[HBM, HBM, SEM, SEM, ANY], out_specs=[HBM, HBM],
        out_shape=[pltpu.HBM(small.shape, F32), pltpu.HBM(land.shape, F32)],
        input_output_aliases={0: 0, 1: 1},
        compiler_params=pltpu.CompilerParams(has_side_effects=DATAFLOW),
    )(small, land, send_sems, recv_sems, after)


def _small_sum(small, land, me):
    rows = small.shape[0]

    def body(me_ref, s_ref, l_ref, o_ref):
        acc = None
        for k in range(8):
            term = jnp.where(me_ref[0] == k, s_ref[...], l_ref[k])
            acc = term if k == 0 else acc + term
        o_ref[...] = acc

    return pl.pallas_call(
        body, name="small_sum",
        in_specs=[SMEM, VMEM, VMEM], out_specs=VMEM,
        out_shape=jax.ShapeDtypeStruct((rows, D), F32),
    )(me, small, land)


def _adamw(w, g, m, v, name, tr, g_transposed=False):
    rows, cols = w.shape

    def body(w_ref, g_ref, m_ref, v_ref, d_ref, nm_ref, nv_ref, *gt_ref):
        g_ = g_ref[...]
        if g_transposed:
            g_ = g_.T
            gt_ref[0][...] = g_
        nm = ADAM_B1 * m_ref[...] + (1.0 - ADAM_B1) * g_
        nv = ADAM_B2 * v_ref[...] + (1.0 - ADAM_B2) * (g_ * g_)
        m_hat = nm / (1.0 - ADAM_B1 ** ADAM_STEP)
        v_hat = nv / (1.0 - ADAM_B2 ** ADAM_STEP)
        d_ref[...] = -ADAM_LR * (m_hat / (jnp.sqrt(v_hat) + ADAM_EPS) + ADAM_WD * w_ref[...])
        nm_ref[...] = nm
        nv_ref[...] = nv

    spec = pl.BlockSpec((tr, cols), lambda i: (i, 0))
    g_spec = pl.BlockSpec((cols, tr), lambda i: (0, i)) if g_transposed else spec
    n_out = 4 if g_transposed else 3
    return pl.pallas_call(
        body, name=name, grid=(rows // tr,),
        in_specs=[spec, g_spec, spec, spec], out_specs=[spec] * n_out,
        out_shape=[jax.ShapeDtypeStruct((rows, cols), F32)] * n_out,
        compiler_params=_cp(("parallel",)),
    )(*[_hbm(a) for a in (w, g, m, v)])


def _local_step(x, target, w_in_t, late_weights, norm_a_g, norm_b_g, sinks_a, ln1_g, ln1_b,
                conv_w, conv_b, ln2_g, ln2_b, slopes, on_grad, on_small):
    cwb = jnp.concatenate([conv_w, conv_b[None]], axis=0).reshape(4, 2, FF)

    proj, xb = _proj(x, w_in_t, "proj")
    o_a, lse_a = _attn_a_fwd(proj, sinks_a)
    fwd_b = None
    for r in reversed(B_DILATIONS):
        fwd_b = _attn_b_fwd(proj, slopes, r, fwd_b)
    o_b, lse_b = fwd_b
    w_o = late_weights(1, lse_b)
    cat, z1, h1, h1b = _mix_ln1(x, o_a, o_b, norm_a_g, norm_b_g, w_o, ln1_g, ln1_b)
    w_up = late_weights(2, h1b)
    up = _up_proj(h1b, w_up)
    a, gate, a1 = _conv_gelu(up, cwb)
    w_down = late_weights(3, a)
    dz2, dz2b, st2 = _down_ln2_loss(a, w_down, h1, target, ln2_g, ln2_b)

    on_grad(3, *_grad_w(a, dz2b, "grad_w_down", tm=FF // 2))
    dup, dconv = _conv_gelu_bwd(_d_act(dz2b, w_down), up, gate, a1, cwb)
    on_grad(2, *_grad_w(dup, h1b, "grad_w_up", tm=FF // 2, lhs_halves=True))
    dz1, dz1b, st1 = _dh1_ln1_bwd(dz2, dup, w_up, z1, ln1_g)
    tok = on_grad(1, *_grad_w(cat, dz1b, "grad_w_o", tm=512))
    d_oa, d_ob, st_n = _dcat_rms_bwd(dz1b, w_o, o_a, o_b, norm_a_g + tok[0, 0], norm_b_g)
    dqa, dka, dva, dsink = _attn_a_bwd(proj, sinks_a, d_oa, o_a, lse_a)
    dconv = dconv.reshape(4, 2 * FF)
    tok = on_small(dict(loss=st2[2, 0:1], norm_a_g=st_n[0], norm_b_g=st_n[1], sinks_a=dsink[:, 0],
                        ln1_g=st1[0], ln1_b=st1[1], conv_w=dconv[0:3].reshape(-1), conv_b=dconv[3],
                        ln2_g=st2[0], ln2_b=st2[1]))
    slopes = slopes + tok[0, 0]
    bwd_b = None
    for r in reversed(B_DILATIONS):
        bwd_b = _attn_b_bwd(proj, slopes, d_ob, o_b, lse_b, r, bwd_b, BF16 if r == 1 else F32)
    dparts = tuple(_hbm(a) for a in (dqa, dka, dva, *bwd_b))
    tok = on_grad(0, *_grad_w_in(dparts, xb))
    return _grad_x(dz1, dparts, w_in_t, tok)


SMALL_ORDER = ("loss", "norm_a_g", "norm_b_g", "sinks_a", "ln1_g", "ln1_b", "conv_b", "ln2_g", "ln2_b", "conv_w")
SMALL_SIZES = dict(loss=1, norm_a_g=512, norm_b_g=512, sinks_a=8, ln1_g=D, ln1_b=D, conv_b=2 * FF, ln2_g=D, ln2_b=D,
                   conv_w=3 * 2 * FF)


def _pack(parts, rows):
    flat = jnp.concatenate([parts[k].reshape(-1).astype(F32) for k in parts])
    return jnp.pad(flat, (0, rows * D - flat.shape[0])).reshape(rows, D)


def _unpack(buf, names, sizes):
    flat = buf.reshape(-1)
    out, at = {}, 0
    for k in names:
        out[k] = flat[at:at + sizes[k]]
        at += sizes[k]
    return out


def kernel(x, w_in, norm_a_g, norm_b_g, sinks_a, w_o, ln1_g, ln1_b, w_up, conv_w, conv_b, w_down, ln2_g, ln2_b, loss_target, m_w_in, m_norm_a_g, m_norm_b_g, m_sinks_a, m_w_o, m_ln1_g, m_ln1_b, m_w_up, m_conv_w, m_conv_b, m_w_down, m_ln2_g, m_ln2_b, v_w_in, v_norm_a_g, v_norm_b_g, v_sinks_a, v_w_o, v_ln1_g, v_ln1_b, v_w_up, v_conv_w, v_conv_b, v_w_down, v_ln2_g, v_ln2_b):
    xi, yi, ci = _place()
    chip = (2 * xi + yi).astype(I32)
    core = ci.astype(I32)

    w_in_rows, m_w_in_rows, v_w_in_rows = w_in.T, m_w_in.T, v_w_in.T
    shards = (w_in_rows.astype(BF16), w_o.astype(BF16), w_up.astype(BF16), w_down.astype(BF16))
    w_in_t, conv_w4 = _gather_w_in(shards[0], conv_w)
    conv_w_f = conv_w4.transpose(1, 0, 2).reshape(3, 2 * FF)
    w_started, w_tok = _weights_start(shards[1:], conv_w4)
    slopes = jnp.asarray(SLOPES, F32) + w_tok[0, 0]

    halves_rows = [r // 2 for r in SHARD_ROWS]
    grads4, grads_b4, started = [None] * 4, [None] * 4, [None] * 4

    def on_grad(k, g, g_b):
        grads4[k] = g.reshape(N_CHIPS, 2, halves_rows[k], D)
        grads_b4[k] = g_b.reshape(N_CHIPS, 2, halves_rows[k], D)
        if k > 1:
            return None
        group = (1, 2, 3) if k == 1 else (0,)
        sts, tok = _grads_start([grads_b4[i] for i in group], f"grads_start_{k}")
        for i, st in zip(group, sts):
            started[i] = st
        return tok

    small_rows = 32
    small_started = []

    def on_small(parts):
        st, tok = _small_start(_pack({k: parts[k] for k in SMALL_ORDER}, small_rows))
        small_started.append(st)
        return tok

    gx = _local_step(
        x[0], loss_target[0], w_in_t, lambda k, after: _weights_wait(w_started[k - 1], after, f"weights_wait_{k}"),
        norm_a_g, norm_b_g, sinks_a, ln1_g, ln1_b, conv_w_f, conv_b, ln2_g, ln2_b, slopes, on_grad, on_small)

    tiles = (96, 128, 352, 176)
    core_chip = jnp.stack([core, chip])
    got = _grads_wait(started[1:], gx, "grads_wait_1")
    halves = [_sum_partials(grads4[k], got[k - 1], core_chip, f"sum_partials_{k}", tiles[k]) for k in (1, 2, 3)]
    g_w_o, g_w_up_rows, g_w_down = _swap_halves(halves, "swap_halves")
    delta, new_m, new_v = {}, {}, {}
    for k, g, tr in (("w_o", g_w_o, 128), ("w_down", g_w_down, 176)):
        delta[k], new_m[k], new_v[k] = _adamw(dict(w_o=w_o, w_down=w_down)[k], g, dict(w_o=m_w_o, w_down=m_w_down)[k],
                                              dict(w_o=v_w_o, w_down=v_w_down)[k], f"adamw_{k}", tr)
    delta["w_up"], new_m["w_up"], new_v["w_up"], g_w_up = _adamw(w_up, g_w_up_rows, m_w_up, v_w_up, "adamw_w_up", 256,
                                                                 g_transposed=True)

    got = _grads_wait(started[:1], delta["w_up"], "grads_wait_0")
    half_in = _sum_partials(grads4[0], got[0], core_chip, "sum_partials_0", tiles[0])
    (g_w_in_rows,) = _swap_halves([half_in], "swap_halves_in")
    small_mine, small_land = _small_wait(small_started[0], g_w_in_rows)
    totals = _small_sum(small_mine, small_land, (4 * xi + 2 * yi + ci).astype(I32).reshape(1))
    tot = _unpack(totals, SMALL_ORDER, SMALL_SIZES)
    loss = tot["loss"][0]
    cols = 2 * FF // N_CHIPS
    g_conv_w = lax.dynamic_slice(tot["conv_w"].reshape(3, 2 * FF), (0, chip * cols), (3, cols))
    g_small = dict(norm_a_g=tot["norm_a_g"], norm_b_g=tot["norm_b_g"], sinks_a=tot["sinks_a"], ln1_g=tot["ln1_g"],
                   ln1_b=tot["ln1_b"], conv_w=g_conv_w, conv_b=tot["conv_b"], ln2_g=tot["ln2_g"], ln2_b=tot["ln2_b"])

    weights = dict(w_in=w_in, norm_a_g=norm_a_g, norm_b_g=norm_b_g, sinks_a=sinks_a, w_o=w_o, ln1_g=ln1_g, ln1_b=ln1_b,
                   w_up=w_up, conv_w=conv_w, conv_b=conv_b, w_down=w_down, ln2_g=ln2_g, ln2_b=ln2_b)
    ms = dict(w_in=m_w_in, norm_a_g=m_norm_a_g, norm_b_g=m_norm_b_g, sinks_a=m_sinks_a, w_o=m_w_o, ln1_g=m_ln1_g,
              ln1_b=m_ln1_b, w_up=m_w_up, conv_w=m_conv_w, conv_b=m_conv_b, w_down=m_w_down, ln2_g=m_ln2_g, ln2_b=m_ln2_b)
    vs = dict(w_in=v_w_in, norm_a_g=v_norm_a_g, norm_b_g=v_norm_b_g, sinks_a=v_sinks_a, w_o=v_w_o, ln1_g=v_ln1_g,
              ln1_b=v_ln1_b, w_up=v_w_up, conv_w=v_conv_w, conv_b=v_conv_b, w_down=v_w_down, ln2_g=v_ln2_g, ln2_b=v_ln2_b)
    order = list(weights)
    grad = dict(g_small, w_in=g_w_in_rows.T, w_o=g_w_o, w_up=g_w_up, w_down=g_w_down)

    delta["w_in"], new_m["w_in"], new_v["w_in"] = [
        a.T for a in _adamw(w_in_rows, g_w_in_rows, m_w_in_rows, v_w_in_rows, "adamw_w_in", 144)]
    small_names = [k for k in order if k not in delta]
    sizes = {k: weights[k].size for k in small_names}
    rows = 16
    packed = [_pack({k: src[k] for k in small_names}, rows) for src in (weights, grad, ms, vs)]
    for res, buf in zip((delta, new_m, new_v), _adamw(*packed, "adamw_small", rows)):
        for k, val in _unpack(buf, small_names, sizes).items():
            res[k] = val.reshape(weights[k].shape)

    return (loss, gx[None], *[grad[k] for k in order], *[delta[k] for k in order],
            *[new_m[k] for k in order], *[new_v[k] for k in order])
```

```python
import functools
import math

import jax
import jax.numpy as jnp
from jax import lax
from jax.experimental import pallas as pl
from jax.experimental.pallas import tpu as pltpu

F32, BF16, I32 = jnp.float32, jnp.bfloat16, jnp.int32

D = 1024
FF = 2816
HD = 64
NH = 8
WA, WB = 768, 1536
WIN = WA + WB
BLK = 128
ALPHA = 2.0 ** 0.25
LN_EPS, RMS_EPS = 1e-5, 1e-6
SCALE = 1.0 / math.sqrt(HD)
A_MAX_DIST, B_MAX_DIST = 127, 128
B_DILATIONS = (1, 4, 16)
SLOPES = tuple(2.0 ** (-(i + 1)) for i in range(NH))
SHARD_ROWS = (WIN // 4, D // 4, 2 * FF // 4, FF // 4)
N_CHIPS = 4
ADAM_LR, ADAM_B1, ADAM_B2, ADAM_EPS, ADAM_WD, ADAM_STEP = 0.001, 0.9, 0.999, 1e-08, 0.01, 10
MESH = pl.DeviceIdType.MESH
ANY = pl.BlockSpec(memory_space=pl.ANY)
SMEM = pl.BlockSpec(memory_space=pltpu.SMEM)
VMEM = pl.BlockSpec(memory_space=pltpu.VMEM)
HBM = pl.BlockSpec(memory_space=pltpu.HBM)
SEM = pl.BlockSpec(memory_space=pltpu.SEMAPHORE)
DATAFLOW = pltpu.SideEffectType.DATAFLOW_SIDE_EFFECTING


def _cp(sem, mb=48):
    return pltpu.CompilerParams(dimension_semantics=sem, vmem_limit_bytes=mb << 20)


def _nn(a, b):
    return lax.dot_general(a, b, (((1,), (0,)), ((), ())), preferred_element_type=F32)


def _nt(a, b):
    return lax.dot_general(a, b, (((1,), (1,)), ((), ())), preferred_element_type=F32)


def _tn(a, b):
    return lax.dot_general(a, b, (((0,), (0,)), ((), ())), preferred_element_type=F32)


def _resident(shape):
    n = len(shape)
    return pl.BlockSpec(shape, lambda *_: (0,) * n, pipeline_mode=pl.Buffered(1))


def _const(shape):
    n = len(shape)
    return pl.BlockSpec(shape, lambda *_: (0,) * n)


def _proj(x, w_t, name, tm=512):
    s = x.shape[0]
    n = w_t.shape[0]

    def body(x_ref, w_ref, o_ref, xb_ref):
        xb = x_ref[...].astype(BF16)
        xb_ref[...] = xb
        res = _nt(xb, w_ref[...])
        for g in range(n // 128):
            o_ref[g] = res[:, 128 * g:128 * (g + 1)]

    return pl.pallas_call(
        body, name=name, grid=(s // tm,),
        in_specs=[pl.BlockSpec((tm, D), lambda i: (i, 0)), _resident((n, D))],
        out_specs=[pl.BlockSpec((n // 128, tm, 128), lambda i: (0, i, 0)), pl.BlockSpec((tm, D), lambda i: (i, 0))],
        out_shape=[jax.ShapeDtypeStruct((n // 128, s, 128), F32), jax.ShapeDtypeStruct((s, D), BF16)],
        compiler_params=_cp(("parallel",)),
    )(x, w_t)


def _grad_w(lhs, rhs, name, tm, tk=2048, lhs_halves=False):
    s = rhs.shape[0]
    if lhs_halves:
        per_half = lhs.shape[2] // tm
        n = 2 * lhs.shape[2]
        lhs_spec = pl.BlockSpec((None, tk, tm), lambda i, k: (i // per_half, k, i % per_half))
    else:
        n = lhs.shape[1]
        lhs_spec = pl.BlockSpec((tk, tm), lambda i, k: (k, i))
    nk = s // tk

    def body(l_ref, r_ref, o_ref, ob_ref):
        k = pl.program_id(1)

        @pl.when(k == 0)
        def _():
            o_ref[...] = jnp.zeros_like(o_ref)

        o_ref[...] += _tn(l_ref[...], r_ref[...])

        @pl.when(k == nk - 1)
        def _():
            ob_ref[...] = o_ref[...].astype(BF16)

    return pl.pallas_call(
        body, name=name, grid=(n // tm, nk),
        in_specs=[lhs_spec, pl.BlockSpec((tk, D), lambda i, k: (k, 0))],
        out_specs=[pl.BlockSpec((tm, D), lambda i, k: (i, 0))] * 2,
        out_shape=[pltpu.HBM((n, D), F32), pltpu.HBM((n, D), BF16)],
        compiler_params=_cp(("parallel", "arbitrary")),
    )(lhs, rhs)


def _band_base(max_dist, dist_unit, first):
    row = lax.broadcasted_iota(I32, (BLK, 2 * BLK), 0)
    col = lax.broadcasted_iota(I32, (BLK, 2 * BLK), 1)
    dist = BLK + row - col
    ok = (dist >= 0) & (dist <= max_dist)
    if first:
        ok = ok & (col >= BLK)
    return jnp.where(ok, dist.astype(F32) * (-float(dist_unit)), -jnp.inf)


def _half_mask(shape, e):
    lane = lax.broadcasted_iota(I32, shape, 1)
    return (lane < HD) if e == 0 else (lane >= HD)


def _to_half(x, e, g):
    if g != e:
        x = pltpu.roll(x, HD, 1)
    return jnp.where(_half_mask(x.shape, g), x, 0.0)


def _stack_heads(scalars, tile):
    return jnp.concatenate([scalars[0] * tile, scalars[1] * tile], axis=0)


def _pair_fwd(q2, kb, vb, base, slopes, kv_heads, sinks):
    lo = _half_mask((BLK, 2 * HD), 0)
    if slopes is None:
        bias = base
    elif sinks is None:
        bias = _stack_heads(slopes, base)
    else:
        col0 = lax.broadcasted_iota(I32, base.shape, 1) == 0
        bias = jnp.concatenate([jnp.where(col0, sinks[e], slopes[e] * base) for e in (0, 1)], axis=0)
    qs = jnp.concatenate([_to_half(q2, e, kv_heads[e]) * SCALE for e in (0, 1)], axis=0).astype(BF16)
    s = _nt(qs, kb) + bias
    m = jnp.max(s, axis=1, keepdims=True)
    p = jnp.exp(s - m)
    l = jnp.sum(p, axis=1, keepdims=True)
    o = _nn(p.astype(BF16), vb) / l
    lse = m + jnp.log(l)
    halves = []
    for e in (0, 1):
        oh = o[e * BLK:(e + 1) * BLK]
        halves.append(pltpu.roll(oh, HD, 1) if kv_heads[e] != e else oh)
    o2 = jnp.where(lo, halves[0], halves[1])
    lse2 = jnp.where(lo, jnp.broadcast_to(lse[:BLK], (BLK, 2 * HD)), jnp.broadcast_to(lse[BLK:], (BLK, 2 * HD)))
    return o2, lse2


def _pair_bwd(q2, kb, vb, do2, o2, lse2, base, slopes, kv_heads, sinks):
    lo = _half_mask((BLK, 2 * HD), 0)
    prod = do2 * o2
    lses, deltas = [], []
    for e in (0, 1):
        hq = _half_mask((BLK, 2 * HD), e)
        lses.append(jnp.max(jnp.where(hq, lse2, -jnp.inf), axis=1, keepdims=True))
        deltas.append(jnp.sum(jnp.where(hq, prod, 0.0), axis=1, keepdims=True))
    lse = jnp.concatenate(lses, axis=0)
    delta = jnp.concatenate(deltas, axis=0)
    qs = jnp.concatenate([_to_half(q2, e, kv_heads[e]) * SCALE for e in (0, 1)], axis=0).astype(BF16)
    dos = jnp.concatenate([_to_half(do2, e, kv_heads[e]) for e in (0, 1)], axis=0).astype(BF16)
    p = jnp.exp(_nt(qs, kb) + (base if slopes is None else _stack_heads(slopes, base)) - lse)
    ds = (p * (_nt(dos, vb) - delta)).astype(BF16)
    dq = _nn(ds, kb) * SCALE
    halves = []
    for e in (0, 1):
        dqh = dq[e * BLK:(e + 1) * BLK]
        halves.append(pltpu.roll(dqh, HD, 1) if kv_heads[e] != e else dqh)
    dq2 = jnp.where(lo, halves[0], halves[1])
    dk2 = _tn(ds, qs)
    dv2 = _tn(p.astype(BF16), dos)
    dsinks = []
    if sinks is not None:
        for e in (0, 1):
            dsinks.append(jnp.sum(-jnp.exp(sinks[e] - lses[e]) * deltas[e], axis=0, keepdims=True))
    return dq2, dk2, dv2, dsinks


A_BLOCKS_PER_STEP = 2
A_BLOCKS_PER_STEP_BWD = 1


def _attn_a_fwd(proj, sinks):
    s = proj.shape[1]
    nq = A_BLOCKS_PER_STEP
    rows = BLK * nq
    steps = s // rows

    def body(sink_ref, q_ref, kp_ref, kc_ref, vp_ref, vc_ref, o_ref, lse_ref):
        n = pl.program_id(0)
        base_rest = _band_base(A_MAX_DIST, 1, False)
        base_0 = jnp.where(n > 0, base_rest, _band_base(A_MAX_DIST, 1, True))
        for i in range(nq):
            cur = pl.ds(i * BLK, BLK)
            k_prev = kc_ref[pl.ds((i - 1) * BLK, BLK), :] if i > 0 else kp_ref[...]
            v_prev = vc_ref[pl.ds((i - 1) * BLK, BLK), :] if i > 0 else vp_ref[...]
            first_key = lax.broadcasted_iota(I32, (2 * BLK, 128), 0) == 0
            kb = jnp.where(first_key, 0.0, jnp.concatenate([k_prev, kc_ref[cur, :]], axis=0)).astype(BF16)
            vb = jnp.where(first_key, 0.0, jnp.concatenate([v_prev, vc_ref[cur, :]], axis=0)).astype(BF16)
            for j in range(NH // 2):
                g = j // 2
                o2, lse2 = _pair_fwd(q_ref[j, cur, :], kb, vb, base_rest if i > 0 else base_0,
                                     (SLOPES[2 * j], SLOPES[2 * j + 1]), (g, g), (sink_ref[2 * j], sink_ref[2 * j + 1]))
                o_ref[j, cur, :] = o2
                lse_ref[j, cur, :] = lse2

    before = lambda n: jnp.maximum(n * nq - 1, 0)
    slab = lambda g: pl.BlockSpec((None, rows, 128), lambda n: (g, n, 0))
    edge = lambda g: pl.BlockSpec((None, BLK, 128), lambda n: (g, before(n), 0))
    quad = pl.BlockSpec((4, rows, 128), lambda n: (0, n, 0))
    return pl.pallas_call(
        body, name="attn_a_fwd", grid=(steps,),
        in_specs=[SMEM, quad, edge(4), slab(4), edge(5), slab(5)],
        out_specs=[quad, quad],
        out_shape=[jax.ShapeDtypeStruct((4, s, 128), F32)] * 2,
        compiler_params=_cp(("parallel",)),
    )(sinks, proj, proj, proj, proj, proj)


def _attn_a_bwd(proj, sinks, d_o, o, lse):
    s = proj.shape[1]
    nq = A_BLOCKS_PER_STEP_BWD
    rows = BLK * nq
    steps = s // rows

    def body(sink_ref, q_ref, kp_ref, kc_ref, vp_ref, vc_ref, do_ref, o_ref, lse_ref,
             dq_ref, dk_ref, dv_ref, dsink_ref, kcar, vcar):
        n = pl.program_id(0)

        @pl.when(n == 0)
        def _():
            kcar[...] = jnp.zeros_like(kcar)
            vcar[...] = jnp.zeros_like(vcar)
            dsink_ref[...] = jnp.zeros_like(dsink_ref)

        dk_ref[...] = kcar[...].astype(BF16)
        dv_ref[...] = vcar[...].astype(BF16)

        @pl.when(n < steps)
        def _():
            base_rest = _band_base(A_MAX_DIST, 1, False)
            base_0 = jnp.where(n > 0, base_rest, _band_base(A_MAX_DIST, 1, True))
            for i in range(nq):
                cur = pl.ds(i * BLK, BLK)
                k_prev = kc_ref[pl.ds((i - 1) * BLK, BLK), :] if i > 0 else kp_ref[...]
                v_prev = vc_ref[pl.ds((i - 1) * BLK, BLK), :] if i > 0 else vp_ref[...]
                kb = jnp.concatenate([k_prev, kc_ref[cur, :]], axis=0).astype(BF16)
                vb = jnp.concatenate([v_prev, vc_ref[cur, :]], axis=0).astype(BF16)
                dk_win = dv_win = None
                for j in range(NH // 2):
                    g = j // 2
                    dq2, dk2, dv2, dsk = _pair_bwd(q_ref[j, cur, :], kb, vb, do_ref[j, cur, :], o_ref[j, cur, :],
                                                   lse_ref[j, cur, :], base_rest if i > 0 else base_0,
                                                   (SLOPES[2 * j], SLOPES[2 * j + 1]), (g, g),
                                                   (sink_ref[2 * j], sink_ref[2 * j + 1]))
                    dq_ref[j, cur, :] = dq2.astype(BF16)
                    dk_win = dk2 if j == 0 else dk_win + dk2
                    dv_win = dv2 if j == 0 else dv_win + dv2
                    for e in (0, 1):
                        h = 2 * j + e
                        dsink_ref[h:h + 1, :] += jnp.broadcast_to(dsk[e], (1, 128))
                if i == 0:
                    last = pl.ds((nq - 1) * BLK, BLK)
                    dk_ref[last, :] = (kcar[last, :] + dk_win[:BLK]).astype(BF16)
                    dv_ref[last, :] = (vcar[last, :] + dv_win[:BLK]).astype(BF16)
                else:
                    kcar[pl.ds((i - 1) * BLK, BLK), :] += dk_win[:BLK]
                    vcar[pl.ds((i - 1) * BLK, BLK), :] += dv_win[:BLK]
                kcar[cur, :] = dk_win[BLK:]
                vcar[cur, :] = dv_win[BLK:]

    cur_step = lambda n: jnp.minimum(n, steps - 1)
    before = lambda n: jnp.maximum(cur_step(n) * nq - 1, 0)
    out_prev = lambda n: jnp.maximum(n - 1, 0)
    quad = pl.BlockSpec((4, rows, 128), lambda n: (0, cur_step(n), 0))
    slab = lambda g: pl.BlockSpec((None, rows, 128), lambda n: (g, cur_step(n), 0))
    edge = lambda g: pl.BlockSpec((None, BLK, 128), lambda n: (g, before(n), 0))
    return pl.pallas_call(
        body, name="attn_a_bwd", grid=(steps + 1,),
        in_specs=[SMEM, quad, edge(4), slab(4), edge(5), slab(5), quad, quad, quad],
        out_specs=[quad,
                   pl.BlockSpec((rows, 128), lambda n: (out_prev(n), 0)),
                   pl.BlockSpec((rows, 128), lambda n: (out_prev(n), 0)),
                   pl.BlockSpec((NH, 128), lambda n: (0, 0))],
        out_shape=[pltpu.HBM((4, s, 128), BF16), pltpu.HBM((s, 128), BF16), pltpu.HBM((s, 128), BF16),
                   jax.ShapeDtypeStruct((NH, 128), F32)],
        scratch_shapes=[pltpu.VMEM((rows, 128), F32), pltpu.VMEM((rows, 128), F32)],
        compiler_params=_cp(("arbitrary",)),
    )(sinks, proj, proj, proj, proj, proj, d_o, o, lse)


def _stream(rho, i, r):
    start = i * BLK * r + rho
    return pl.ds(start, BLK, stride=r) if r > 1 else pl.ds(start, BLK)


def _for_streams(r, fn, side_by_side=4):
    if r <= side_by_side:
        for rho in range(r):
            fn(rho)
    else:
        def group(it, carry):
            for u in range(side_by_side):
                fn(side_by_side * it + u)
            return carry

        lax.fori_loop(0, r // side_by_side, group, 0)


B_BLOCKS_PER_STEP = {1: 8, 4: 2, 16: 1}
B_BLOCKS_PER_STEP_FWD = {1: 16, 4: 4, 16: 1}


def _attn_b_fwd(proj, slopes, r, so_far=None):
    s = proj.shape[1]
    nq = B_BLOCKS_PER_STEP_FWD[r]
    rows = BLK * r * nq
    steps = s // rows
    qc, kc, vc = WA // 128, WA // 128 + 4, WA // 128 + 8
    chained = so_far is not None

    def body(slope_ref, q_ref, kp_ref, kc_ref, vp_ref, vc_ref, *rest):
        po_ref, pl_ref = rest[:2] if chained else (None, None)
        o_ref, lse_ref = rest[-2:]
        j = pl.program_id(0)
        sb = pl.program_id(1)
        sl2 = (slope_ref[2 * j], slope_ref[2 * j + 1])
        bias_rest = _stack_heads(sl2, _band_base(B_MAX_DIST, r, False))
        bias_0 = jnp.where(sb > 0, bias_rest, _stack_heads(sl2, _band_base(B_MAX_DIST, r, True)))

        def stream(rho):
            for i in range(nq):
                cur = _stream(rho, i, r)
                k_prev = kc_ref[_stream(rho, i - 1, r), :] if i > 0 else kp_ref[_stream(rho, 0, r), :]
                v_prev = vc_ref[_stream(rho, i - 1, r), :] if i > 0 else vp_ref[_stream(rho, 0, r), :]
                kb = jnp.concatenate([k_prev, kc_ref[cur, :]], axis=0).astype(BF16)
                vb = jnp.concatenate([v_prev, vc_ref[cur, :]], axis=0).astype(BF16)
                o2, lse2 = _pair_fwd(q_ref[cur, :], kb, vb, bias_rest if i > 0 else bias_0, None, (0, 1), None)
                if chained:
                    lse1 = pl_ref[cur, :]
                    m = jnp.maximum(lse1, lse2)
                    e1, e2 = jnp.exp(lse1 - m), jnp.exp(lse2 - m)
                    den = e1 + e2
                    o2 = (e1 * po_ref[cur, :] + e2 * o2) * (1.0 / den)
                    lse2 = m + jnp.log(den)
                o_ref[cur, :] = o2
                lse_ref[cur, :] = lse2

        _for_streams(r, stream, side_by_side=16)

    before = lambda sb: jnp.maximum(sb * nq - 1, 0)
    result = pl.BlockSpec((None, rows, 128), lambda j, sb: (j, sb, 0))
    return pl.pallas_call(
        body, name=f"attn_b_fwd_r{r}", grid=(NH // 2, steps),
        in_specs=[SMEM,
                  pl.BlockSpec((None, rows, 128), lambda j, sb: (qc + j, sb, 0)),
                  pl.BlockSpec((None, BLK * r, 128), lambda j, sb: (kc + j, before(sb), 0)),
                  pl.BlockSpec((None, rows, 128), lambda j, sb: (kc + j, sb, 0)),
                  pl.BlockSpec((None, BLK * r, 128), lambda j, sb: (vc + j, before(sb), 0)),
                  pl.BlockSpec((None, rows, 128), lambda j, sb: (vc + j, sb, 0))] + ([result] * 2 if chained else []),
        out_specs=[result] * 2,
        out_shape=[jax.ShapeDtypeStruct((4, s, 128), F32)] * 2,
        compiler_params=_cp(("parallel", "parallel")),
    )(slopes, proj, proj, proj, proj, proj, *(so_far if chained else ()))


def _attn_b_bwd(proj, slopes, d_o, o, lse, r, so_far=None, dtype=F32):
    s = proj.shape[1]
    nq = B_BLOCKS_PER_STEP[r]
    rows = BLK * r * nq
    steps = s // rows
    qc, kc, vc = WA // 128, WA // 128 + 4, WA // 128 + 8
    chained = so_far is not None

    def body(slope_ref, q_ref, kp_ref, kc_ref, vp_ref, vc_ref, do_ref, o_ref, lse_ref, *rest):
        pq_ref, pk_ref, pv_ref = rest[:3] if chained else (None, None, None)
        dq_ref, dk_ref, dv_ref, kcar, vcar = rest[-5:]
        j = pl.program_id(0)
        sb = pl.program_id(1)

        @pl.when(sb == 0)
        def _():
            kcar[...] = jnp.zeros_like(kcar)
            vcar[...] = jnp.zeros_like(vcar)

        def settled(car, p_ref, idx):
            return car[idx] + p_ref[idx] if chained else car[idx]

        dk_ref[...] = settled(kcar, pk_ref, ...).astype(dtype)
        dv_ref[...] = settled(vcar, pv_ref, ...).astype(dtype)

        @pl.when(sb < steps)
        def _():
            sl2 = (slope_ref[2 * j], slope_ref[2 * j + 1])
            bias_rest = _stack_heads(sl2, _band_base(B_MAX_DIST, r, False))
            bias_0 = jnp.where(sb > 0, bias_rest, _stack_heads(sl2, _band_base(B_MAX_DIST, r, True)))

            def stream(rho):
                for i in range(nq):
                    cur = _stream(rho, i, r)
                    k_prev = kc_ref[_stream(rho, i - 1, r), :] if i > 0 else kp_ref[_stream(rho, 0, r), :]
                    v_prev = vc_ref[_stream(rho, i - 1, r), :] if i > 0 else vp_ref[_stream(rho, 0, r), :]
                    kb = jnp.concatenate([k_prev, kc_ref[cur, :]], axis=0).astype(BF16)
                    vb = jnp.concatenate([v_prev, vc_ref[cur, :]], axis=0).astype(BF16)
                    dq2, dk2, dv2, _ = _pair_bwd(q_ref[cur, :], kb, vb, do_ref[cur, :], o_ref[cur, :], lse_ref[cur, :],
                                                 bias_rest if i > 0 else bias_0, None, (0, 1), None)
                    dq_ref[cur, :] = (dq2 + pq_ref[cur, :] if chained else dq2).astype(dtype)
                    if i == 0:
                        last = (_stream(rho, nq - 1, r), slice(None))
                        dk_ref[last] = (settled(kcar, pk_ref, last) + dk2[:BLK]).astype(dtype)
                        dv_ref[last] = (settled(vcar, pv_ref, last) + dv2[:BLK]).astype(dtype)
                    else:
                        kcar[_stream(rho, i - 1, r), :] += dk2[:BLK]
                        vcar[_stream(rho, i - 1, r), :] += dv2[:BLK]
                    kcar[cur, :] = dk2[BLK:]
                    vcar[cur, :] = dv2[BLK:]

            _for_streams(r, stream, side_by_side=8)

    cur_step = lambda sb: jnp.minimum(sb, steps - 1)
    before = lambda sb: jnp.maximum(cur_step(sb) * nq - 1, 0)
    out_prev = lambda sb: jnp.maximum(sb - 1, 0)
    tile = lambda slab: pl.BlockSpec((None, rows, 128), lambda j, sb: (slab + j, cur_step(sb), 0))
    edge = lambda slab: pl.BlockSpec((None, BLK * r, 128), lambda j, sb: (slab + j, before(sb), 0))
    late = pl.BlockSpec((None, rows, 128), lambda j, sb: (j, out_prev(sb), 0))
    grads = [tile(0), late, late]
    return pl.pallas_call(
        body, name=f"attn_b_bwd_r{r}", grid=(NH // 2, steps + 1),
        in_specs=[SMEM, tile(qc), edge(kc), tile(kc), edge(vc), tile(vc), tile(0), tile(0), tile(0)]
        + (grads if chained else []),
        out_specs=grads,
        out_shape=[pltpu.HBM((4, s, 128), dtype)] * 3,
        scratch_shapes=[pltpu.VMEM((rows, 128), F32), pltpu.VMEM((rows, 128), F32)],
        compiler_params=_cp(("parallel", "arbitrary")),
    )(slopes, proj, proj, proj, proj, proj, d_o, o, lse, *(so_far if chained else ()))


def _row(v):
    return v.reshape(1, -1)


def _layer_norm_stats(z):
    mu = jnp.mean(z, axis=-1, keepdims=True)
    zc = z - mu
    var = jnp.mean(zc * zc, axis=-1, keepdims=True)
    rstd = lax.rsqrt(var + LN_EPS)
    return zc * rstd, rstd


def _layer_norm_bwd(dh, zh, rstd, g):
    dzh = dh * g
    return rstd * (dzh - jnp.mean(dzh, axis=-1, keepdims=True) - zh * jnp.mean(dzh * zh, axis=-1, keepdims=True))


def _rms(o):
    return lax.rsqrt(jnp.mean(o * o, axis=-1, keepdims=True) + RMS_EPS)


def _mix_ln1(x, o_a, o_b, norm_a_g, norm_b_g, w_o, ln1_g, ln1_b, tm=512):
    s = x.shape[0]

    def wide(ref):
        return jnp.concatenate([ref[j] for j in range(4)], axis=1)

    def body(x_ref, oa_ref, ob_ref, ga_ref, gb_ref, wo_ref, g_ref, b_ref, cat_ref, z1_ref, h1_ref, h1b_ref):
        oa, ob = wide(oa_ref), wide(ob_ref)
        na = oa * _rms(oa) * ga_ref[...]
        nb_ = ob * _rms(ob) * gb_ref[...]
        cat = jnp.concatenate([na, nb_], axis=1).astype(BF16)
        cat_ref[...] = cat
        z1 = ALPHA * x_ref[...] + _nn(cat, wo_ref[...])
        z1_ref[...] = z1
        zh, _ = _layer_norm_stats(z1)
        h1 = zh * g_ref[...] + b_ref[...]
        h1_ref[...] = h1
        h1b_ref[...] = h1.astype(BF16)

    t512 = pl.BlockSpec((4, tm, 128), lambda i: (0, i, 0))
    td = pl.BlockSpec((tm, D), lambda i: (i, 0))
    return pl.pallas_call(
        body, name="mix_ln1", grid=(s // tm,),
        in_specs=[td] + [t512] * 2 + [_const((1, 512))] * 2 + [_resident((D, D))] + [_const((1, D))] * 2,
        out_specs=[td, td, td, td],
        out_shape=[jax.ShapeDtypeStruct((s, D), BF16), jax.ShapeDtypeStruct((s, D), F32),
                   jax.ShapeDtypeStruct((s, D), F32), jax.ShapeDtypeStruct((s, D), BF16)],
        compiler_params=_cp(("parallel",)),
    )(x, o_a, o_b, _row(norm_a_g), _row(norm_b_g), w_o, _row(ln1_g), _row(ln1_b))


def _gelu_and_grad(x):
    c = math.sqrt(2.0 / math.pi)
    x2 = x * x
    s = 0.5 * jnp.tanh(x * ((c * 0.044715) * x2 + c)) + 0.5
    dg = s + (x * ((6.0 * c * 0.044715) * x2 + 2.0 * c)) * (s - s * s)
    return x * s, dg


def _shifted(u, edge, row, down):
    groups = [u[8 * i:8 * i + 8] for i in range(u.shape[0] // 8)]
    others = [edge] + groups[:-1] if down else groups[1:] + [edge]
    moved = []
    for k in (1, 2):
        crossing = row >= 8 - k if down else row < k
        moved.append(jnp.concatenate([pltpu.roll(jnp.where(crossing, o, g), k if down else 8 - k, 0)
                                      for o, g in zip(others, groups)], axis=0))
    return moved


def _up_proj(h1b, w_up, tm=512):
    s = h1b.shape[0]

    def body(h_ref, w_ref, o_ref):
        h = h_ref[...]
        for half in (0, 1):
            o_ref[half] = _nn(h, w_ref[:, half * FF:(half + 1) * FF]).astype(BF16)

    return pl.pallas_call(
        body, name="up_proj", grid=(s // tm,),
        in_specs=[pl.BlockSpec((tm, D), lambda i: (i, 0)), _resident((D, 2 * FF))],
        out_specs=pl.BlockSpec((2, tm, FF), lambda i: (0, i, 0)),
        out_shape=jax.ShapeDtypeStruct((2, s, FF), BF16),
        compiler_params=_cp(("parallel",)),
    )(h1b, w_up)


def _conv_gelu(up, cwb, tm=512, tn=FF // 2, chunk_rows=16):
    s = up.shape[1]
    n_c = tm // chunk_rows

    def body(up_ref, c_ref, a_ref, g_ref, a1_ref, carry):
        @pl.when(pl.program_id(1) == 0)
        def _():
            carry[...] = jnp.zeros_like(carry)

        row = lax.broadcasted_iota(jnp.int32, (8, tn), 0)
        edge = [carry[0], carry[1]]
        for c in range(n_c):
            rows = pl.ds(c * chunk_rows, chunk_rows)
            u = []
            for half in (0, 1):
                x = up_ref[half, rows, :].astype(F32)
                r1, r2 = _shifted(x, edge[half], row, True)
                u.append(r2 * c_ref[0, half:half + 1, :] + r1 * c_ref[1, half:half + 1, :]
                         + x * c_ref[2, half:half + 1, :] + c_ref[3, half:half + 1, :])
                edge[half] = x[chunk_rows - 8:]
            g, dg = _gelu_and_grad(u[0])
            a_ref[rows, :] = (g * u[1]).astype(BF16)
            g_ref[rows, :] = g.astype(BF16)
            a1_ref[rows, :] = (u[1] * dg).astype(BF16)
        for half in (0, 1):
            carry[half] = edge[half]

    pair = pl.BlockSpec((2, tm, tn), lambda j, i: (0, i, j))
    tile = pl.BlockSpec((tm, tn), lambda j, i: (i, j))
    return pl.pallas_call(
        body, name="conv_gelu", grid=(FF // tn, s // tm),
        in_specs=[pair, pl.BlockSpec((4, 2, tn), lambda j, i: (0, 0, j))],
        out_specs=[tile, tile, tile],
        out_shape=[jax.ShapeDtypeStruct((s, FF), BF16)] * 3,
        scratch_shapes=[pltpu.VMEM((2, 8, tn), F32)],
        compiler_params=_cp(("parallel", "arbitrary")),
    )(up, cwb)


def _down_ln2_loss(a, w_down, h1, target, ln2_g, ln2_b, tm=512):
    s = a.shape[0]

    def body(a_ref, w_ref, h_ref, t_ref, g_ref, b_ref, dz_ref, dzb_ref, st_ref):
        @pl.when(pl.program_id(0) == 0)
        def _():
            st_ref[...] = jnp.zeros_like(st_ref)

        z2 = ALPHA * h_ref[...] + _nn(a_ref[...], w_ref[...])
        zh, rstd = _layer_norm_stats(z2)
        diff = zh * g_ref[...] + b_ref[...] - t_ref[...]
        part = 0.5 * jnp.sum(jnp.mean(diff * diff, axis=-1, keepdims=True), axis=0, keepdims=True)
        dy = diff * (1.0 / D)
        st_ref[0:1, :] += jnp.sum(dy * zh, axis=0, keepdims=True)
        st_ref[1:2, :] += jnp.sum(dy, axis=0, keepdims=True)
        st_ref[2:3, :] += jnp.broadcast_to(part, (1, D))
        dz = _layer_norm_bwd(dy, zh, rstd, g_ref[...])
        dz_ref[...] = dz
        dzb_ref[...] = dz.astype(BF16)

    td = pl.BlockSpec((tm, D), lambda i: (i, 0))
    return pl.pallas_call(
        body, name="down_ln2_loss", grid=(s // tm,),
        in_specs=[pl.BlockSpec((tm, FF), lambda i: (i, 0)), _resident((FF, D)), td, td, _const((1, D)), _const((1, D))],
        out_specs=[td, td, _const((8, D))],
        out_shape=[jax.ShapeDtypeStruct((s, D), F32), jax.ShapeDtypeStruct((s, D), BF16),
                   jax.ShapeDtypeStruct((8, D), F32)],
        compiler_params=_cp(("arbitrary",)),
    )(a, w_down, h1, target, _row(ln2_g), _row(ln2_b))


def _d_act(dz2b, w_down, tm=512):
    s = dz2b.shape[0]

    def body(dz_ref, w_ref, o_ref):
        o_ref[...] = _nt(dz_ref[...], w_ref[...]).astype(BF16)

    return pl.pallas_call(
        body, name="d_act", grid=(s // tm,),
        in_specs=[pl.BlockSpec((tm, D), lambda i: (i, 0)), _resident((FF, D))],
        out_specs=pl.BlockSpec((tm, FF), lambda i: (i, 0)),
        out_shape=jax.ShapeDtypeStruct((s, FF), BF16),
        compiler_params=_cp(("parallel",)),
    )(dz2b, w_down)


def _conv_gelu_bwd(da, up, g, a1, cwb, tm=512, tn=FF // 2, chunk_rows=16):
    s = da.shape[0]
    n_i = s // tm
    n_c = tm // chunk_rows
    n_steps = (FF // tn) * n_i

    def body(da_hbm, up_hbm, g_hbm, a1_hbm, c_ref, dup_ref, dc_ref, carry, da_buf, up_buf, g_buf, a1_buf, sems):
        t = pl.program_id(0) * n_i + pl.program_id(1)

        def fetch(step):
            slot = step % 3
            rows = pl.ds(pl.multiple_of((n_i - 1 - step % n_i) * tm, tm), tm)
            cols = pl.ds(pl.multiple_of((step // n_i) * tn, 128), tn)
            return [pltpu.make_async_copy(da_hbm.at[rows, cols], da_buf.at[slot], sems.at[slot, 0]),
                    pltpu.make_async_copy(up_hbm.at[:, rows, cols], up_buf.at[slot], sems.at[slot, 1]),
                    pltpu.make_async_copy(g_hbm.at[rows, cols], g_buf.at[slot], sems.at[slot, 2]),
                    pltpu.make_async_copy(a1_hbm.at[rows, cols], a1_buf.at[slot], sems.at[slot, 3])]

        @pl.when(t == 0)
        def _():
            for cp in fetch(t) + fetch(t + 1):
                cp.start()

        @pl.when(t + 2 < n_steps)
        def _():
            for cp in fetch(t + 2):
                cp.start()

        for cp in fetch(t):
            cp.wait()
        da_ref, up_ref, g_ref, a1_ref = (buf.at[t % 3] for buf in (da_buf, up_buf, g_buf, a1_buf))

        @pl.when(pl.program_id(1) == 0)
        def _():
            carry[...] = jnp.zeros_like(carry)
            dc_ref[...] = jnp.zeros_like(dc_ref)

        def fold(v):
            return jnp.sum(v.reshape(chunk_rows // 8, 8, v.shape[1]), axis=0)

        def chunk(cc, state):
            after, sums = state
            rows = pl.ds((n_c - 1 - cc) * chunk_rows, chunk_rows)
            da_c = da_ref[rows, :].astype(F32)
            dus = (da_c * a1_ref[rows, :].astype(F32), da_c * g_ref[rows, :].astype(F32))
            head, new_sums = [], []
            for half in (0, 1):
                du = dus[half]
                up = up_ref[half, rows, :].astype(F32)
                l1, l2 = _shifted(du, after[half], row, False)
                dup = (du * c_ref[2, half:half + 1, :] + l1 * c_ref[1, half:half + 1, :]
                       + l2 * c_ref[0, half:half + 1, :])
                dup_ref[half, rows, :] = dup.astype(BF16)
                parts = (fold(l2 * up), fold(l1 * up), fold(du * up), fold(du))
                new_sums.append(parts if sums is None else tuple(a + b for a, b in zip(sums[half], parts)))
                head.append(du[:8])
            return tuple(head), new_sums

        row = lax.broadcasted_iota(jnp.int32, (8, tn), 0)
        state = ((carry[0], carry[1]), None)
        for cc in range(n_c):
            state = chunk(cc, state)
        head, sums = state
        for half in (0, 1):
            carry[half] = head[half]
            for k in range(4):
                dc_ref[k, half:half + 1, :] += jnp.sum(sums[half][k], axis=0, keepdims=True)

    pair = pl.BlockSpec((2, tm, tn), lambda j, ii: (0, n_i - 1 - ii, j))
    per_col = pl.BlockSpec((4, 2, tn), lambda j, ii: (0, 0, j))
    return pl.pallas_call(
        body, name="conv_gelu_bwd", grid=(FF // tn, n_i),
        in_specs=[ANY, ANY, ANY, ANY, per_col],
        out_specs=[pair, per_col],
        out_shape=[jax.ShapeDtypeStruct((2, s, FF), BF16), jax.ShapeDtypeStruct((4, 2, FF), F32)],
        scratch_shapes=[pltpu.VMEM((2, 8, tn), F32), pltpu.VMEM((3, tm, tn), BF16), pltpu.VMEM((3, 2, tm, tn), BF16),
                        pltpu.VMEM((3, tm, tn), BF16), pltpu.VMEM((3, tm, tn), BF16), pltpu.SemaphoreType.DMA((3, 4))],
        compiler_params=_cp(("arbitrary", "arbitrary")),
    )(da, up, g, a1, cwb)


def _dh1_ln1_bwd(dz2, dup, w_up, z1, ln1_g, tm=512):
    s = dz2.shape[0]

    def body(dz2_ref, dup_ref, w_ref, z1_ref, g_ref, dz1_ref, dz1b_ref, st_ref):
        @pl.when(pl.program_id(0) == 0)
        def _():
            st_ref[...] = jnp.zeros_like(st_ref)

        dh = ALPHA * dz2_ref[...] + _nt(dup_ref[0], w_ref[:, :FF]) + _nt(dup_ref[1], w_ref[:, FF:])
        zh, rstd = _layer_norm_stats(z1_ref[...])
        st_ref[0:1, :] += jnp.sum(dh * zh, axis=0, keepdims=True)
        st_ref[1:2, :] += jnp.sum(dh, axis=0, keepdims=True)
        dz = _layer_norm_bwd(dh, zh, rstd, g_ref[...])
        dz1_ref[...] = dz
        dz1b_ref[...] = dz.astype(BF16)

    td = pl.BlockSpec((tm, D), lambda i: (i, 0))
    return pl.pallas_call(
        body, name="dh1_ln1_bwd", grid=(s // tm,),
        in_specs=[td, pl.BlockSpec((2, tm, FF), lambda i: (0, i, 0)), _resident((D, 2 * FF)), td, _const((1, D))],
        out_specs=[td, td, _const((8, D))],
        out_shape=[jax.ShapeDtypeStruct((s, D), F32), jax.ShapeDtypeStruct((s, D), BF16),
                   jax.ShapeDtypeStruct((8, D), F32)],
        compiler_params=_cp(("arbitrary",), 58),
    )(dz2, dup, w_up, z1, _row(ln1_g))


def _dcat_rms_bwd(dz1b, w_o, o_a, o_b, norm_a_g, norm_b_g, tm=512):
    s = dz1b.shape[0]

    def body(dz_ref, w_ref, oa_ref, ob_ref, ga_ref, gb_ref, da_ref, db_ref, st_ref):
        @pl.when(pl.program_id(0) == 0)
        def _():
            st_ref[...] = jnp.zeros_like(st_ref)

        dcat = _nt(dz_ref[...], w_ref[...])
        for k, (o_ref, g_ref, d_ref) in enumerate(((oa_ref, ga_ref, da_ref), (ob_ref, gb_ref, db_ref))):
            o = jnp.concatenate([o_ref[j] for j in range(4)], axis=1)
            dn = dcat[:, 512 * k:512 * (k + 1)]
            rr = _rms(o)
            oh = o * rr
            st_ref[k:k + 1, :] += jnp.sum(dn * oh, axis=0, keepdims=True)
            doh = dn * g_ref[...]
            d_o = rr * (doh - oh * jnp.mean(doh * oh, axis=-1, keepdims=True))
            for j in range(4):
                d_ref[j] = d_o[:, 128 * j:128 * (j + 1)]

    t512 = pl.BlockSpec((4, tm, 128), lambda i: (0, i, 0))
    return pl.pallas_call(
        body, name="dcat_rms_bwd", grid=(s // tm,),
        in_specs=[pl.BlockSpec((tm, D), lambda i: (i, 0)), _resident((D, D)), t512, t512,
                  _const((1, 512)), _const((1, 512))],
        out_specs=[t512, t512, _const((8, 512))],
        out_shape=[jax.ShapeDtypeStruct((4, s, 128), F32), jax.ShapeDtypeStruct((4, s, 128), F32),
                   jax.ShapeDtypeStruct((8, 512), F32)],
        compiler_params=_cp(("arbitrary",)),
    )(dz1b, w_o, o_a, o_b, _row(norm_a_g), _row(norm_b_g))


def _grad_w_in(dparts, xb, tk=2048):
    s = xb.shape[0]
    nk = s // tk

    def body(qa, ka, va, qb, kb, vb, x_ref, o_ref, ob_ref):
        i = pl.program_id(0)
        k = pl.program_id(1)

        @pl.when(k == 0)
        def _():
            o_ref[...] = jnp.zeros_like(o_ref)

        def add(blocks):
            o_ref[...] += _tn(jnp.concatenate(blocks, axis=1), x_ref[...])

        pl.when(i == 0)(lambda: add([qa[j] for j in range(4)] + [ka[...], va[...]]))
        pl.when(i == 1)(lambda: add([qb[j] for j in range(4)] + [kb[0], kb[1]]))
        pl.when(i == 2)(lambda: add([kb[0], kb[1]] + [vb[j] for j in range(4)]))

        @pl.when(k == nk - 1)
        def _():
            ob_ref[...] = o_ref[...].astype(BF16)

    def during(tile):
        return lambda i, k: jnp.where(i == tile, k, jnp.where(i < tile, 0, nk - 1))

    quad = lambda tile: pl.BlockSpec((4, tk, 128), lambda i, k: (0, during(tile)(i, k), 0))
    one = pl.BlockSpec((tk, 128), lambda i, k: (during(0)(i, k), 0))
    kb_spec = pl.BlockSpec((2, tk, 128), lambda i, k: (jnp.where(i == 2, 1, 0), jnp.where(i == 0, 0, k), 0))
    return pl.pallas_call(
        body, name="grad_w_in", grid=(3, nk),
        in_specs=[quad(0), one, one, quad(1), kb_spec, quad(2), pl.BlockSpec((tk, D), lambda i, k: (k, 0))],
        out_specs=[pl.BlockSpec((WA, D), lambda i, k: (i, 0))] * 2,
        out_shape=[pltpu.HBM((WIN, D), F32), pltpu.HBM((WIN, D), BF16)],
        compiler_params=_cp(("parallel", "arbitrary"), mb=56),
    )(*dparts, xb)


def _grad_x(dz1, dparts, w_in_t, zero, tm=512):
    s = dz1.shape[0]

    def body(dz_ref, qa, ka, va, qb, kb, vb, w_ref, z_ref, o_ref):
        dp = jnp.concatenate([qa[j] for j in range(4)] + [ka[...], va[...]]
                             + [ref[j] for ref in (qb, kb, vb) for j in range(4)], axis=1)
        o_ref[...] = ALPHA * dz_ref[...] + _nn(dp, w_ref[...]) + z_ref[0:1, 0:1]

    td = pl.BlockSpec((tm, D), lambda i: (i, 0))
    quad = pl.BlockSpec((4, tm, 128), lambda i: (0, i, 0))
    one = pl.BlockSpec((tm, 128), lambda i: (i, 0))
    return pl.pallas_call(
        body, name="grad_x", grid=(s // tm,),
        in_specs=[td, quad, one, one, quad, quad, quad, _resident((WIN, D)), _const((8, 128))],
        out_specs=td, out_shape=jax.ShapeDtypeStruct((s, D), F32),
        compiler_params=_cp(("parallel",)),
    )(dz1, *dparts, w_in_t, zero)


def _place():
    return lax.axis_index("x"), lax.axis_index("y"), lax.axis_index("c")


def _other_chips(x, y):
    return [(1 - x, y), (x, 1 - y), (1 - x, 1 - y)]


def _hbm(a):
    return pltpu.with_memory_space_constraint(a, pltpu.HBM)


def _gather_w_in(shard, conv_w):
    rows_k = shard.shape[0]
    half = rows_k // 2

    def body(src, conv_src, out, conv_out, send_sems, recv_sems):
        x, y, c = _place()
        b = 2 * x + y
        sibling = (x, y, 1 - c)
        chips = _other_chips(x, y)

        def copy(idx, chip_b, core, to, first_hop=False):
            rows = out.at[pl.ds(pl.multiple_of(chip_b * rows_k + core * half, 16), half)]
            s_ref = src.at[pl.ds(pl.multiple_of(core * half, 16), half)] if first_hop else rows
            return pltpu.make_async_remote_copy(src_ref=s_ref, dst_ref=rows, send_sem=send_sems.at[idx],
                                                recv_sem=recv_sems.at[idx], device_id=to, device_id_type=MESH)

        def own_copy():
            return pltpu.make_async_remote_copy(
                src_ref=src, dst_ref=out.at[pl.ds(pl.multiple_of(b * rows_k, 16), rows_k)], send_sem=send_sems.at[6],
                recv_sem=recv_sems.at[6], device_id=sibling, device_id_type=MESH)

        def conv_copy(idx, chip_b, to):
            return pltpu.make_async_remote_copy(src_ref=conv_src, dst_ref=conv_out.at[chip_b],
                                                send_sem=send_sems.at[7 + idx], recv_sem=recv_sems.at[7 + idx],
                                                device_id=to, device_id_type=MESH)

        started = [own_copy(), conv_copy(3, b, sibling)]
        for jn, chip in enumerate(chips):
            started += [copy(jn, b, c, (chip[0], chip[1], c), first_hop=True), conv_copy(jn, b, (chip[0], chip[1], c))]
        for cp in started:
            cp.start()
        for jn, chip in enumerate(chips):
            cb = 2 * chip[0] + chip[1]
            copy(jn, cb, c, (chip[0], chip[1], c)).wait_recv()
            cp = copy(3 + jn, cb, c, sibling)
            cp.start()
            started.append(cp)
        for jn, chip in enumerate(chips):
            cb = 2 * chip[0] + chip[1]
            copy(3 + jn, cb, 1 - c, sibling).wait_recv()
            conv_copy(jn, cb, (chip[0], chip[1], c)).wait_recv()
        own_copy().wait_recv()
        conv_copy(3, b, sibling).wait_recv()
        for cp in started:
            cp.wait_send()

    return pl.pallas_call(
        body, name="gather_w_in",
        in_specs=[ANY, ANY], out_specs=[ANY, ANY],
        out_shape=[jax.ShapeDtypeStruct((N_CHIPS * rows_k, D), BF16), jax.ShapeDtypeStruct((N_CHIPS,) + conv_w.shape, F32)],
        scratch_shapes=[pltpu.SemaphoreType.DMA((11,)), pltpu.SemaphoreType.DMA((11,))],
        compiler_params=pltpu.CompilerParams(has_side_effects=True),
    )(shard, conv_w)


def _weight_copies(shard, land, send_sems, recv_sems, arrivals):
    x, y, c = _place()
    n_rows, n_cols = shard.shape
    peers = [(px, py, c) for px, py in _other_chips(x, y)] + [(x, y, 1 - c)]
    cps = []
    for jn, peer in enumerate(peers):
        at = 2 * peer[0] + peer[1] if arrivals else 2 * x + y
        if land.shape[1] == n_cols:
            dst = land.at[pl.ds(pl.multiple_of(at * n_rows, 16), n_rows)]
        else:
            dst = land.at[:, pl.ds(pl.multiple_of(at * n_cols, 128), n_cols)]
        cps.append(pltpu.make_async_remote_copy(src_ref=shard, dst_ref=dst, send_sem=send_sems.at[jn],
                                                recv_sem=recv_sems.at[jn], device_id=peer, device_id_type=MESH))
    return cps


def _weights_start(shards, after):
    n = len(shards)
    lands = [lax.empty((N_CHIPS * sh.shape[0], D) if sh.shape[1] == D else (D, N_CHIPS * sh.shape[1]), BF16)
             for sh in shards]

    def body(*refs):
        src, land = refs[:n], refs[n:2 * n]
        send_sems, recv_sems = refs[2 * n + 1:3 * n + 1], refs[3 * n + 1:4 * n + 1]
        for k in range(n):
            for send in _weight_copies(src[k], land[k], send_sems[k], recv_sems[k], False):
                send.start()
        refs[-1][...] = jnp.zeros_like(refs[-1])

    res = pl.pallas_call(
        body, name="weights_start",
        in_specs=[HBM] * (2 * n) + [ANY], out_specs=[SEM] * (2 * n) + [HBM] * (2 * n) + [VMEM],
        out_shape=[pltpu.SemaphoreType.DMA((4,))] * (2 * n)
        + [pltpu.HBM(a.shape, a.dtype) for a in (*shards, *lands)] + [jax.ShapeDtypeStruct((8, 128), F32)],
        input_output_aliases={i: i + 2 * n for i in range(2 * n)},
        compiler_params=pltpu.CompilerParams(has_side_effects=DATAFLOW),
    )(*[_hbm(a) for a in (*shards, *lands)], after)
    return [(res[k], res[n + k], res[2 * n + k], res[3 * n + k]) for k in range(n)], res[-1]


def _weights_wait(started, after, name):
    send_sems, recv_sems, shard, land = started

    def body(s_ref, l_ref, send_ref, recv_ref, after_ref, s_out, l_out):
        for cp in _weight_copies(s_ref, l_ref, send_ref, recv_ref, True):
            cp.wait_send()
            cp.wait_recv()

    return pl.pallas_call(
        body, name=name,
        in_specs=[HBM, HBM, SEM, SEM, ANY], out_specs=[HBM, HBM],
        out_shape=[pltpu.HBM(shard.shape, shard.dtype), pltpu.HBM(land.shape, land.dtype)],
        input_output_aliases={0: 0, 1: 1},
        compiler_params=pltpu.CompilerParams(has_side_effects=DATAFLOW),
    )(shard, land, send_sems, recv_sems, after)[1]


def _grad_copies(g_ref, land_ref, send_sems, recv_sems):
    x, y, c = _place()
    cps = []
    for d in range(1, 8):
        px, py, pc = x ^ (d >> 2), y ^ ((d >> 1) & 1), c ^ (d & 1)
        cps.append(pltpu.make_async_remote_copy(
            src_ref=g_ref.at[2 * px + py, pc], dst_ref=land_ref.at[d - 1], send_sem=send_sems.at[d - 1],
            recv_sem=recv_sems.at[d - 1], device_id=(px, py, pc), device_id_type=MESH))
    return cps


def _grads_start(grads_b, name):
    n = len(grads_b)
    lands = [lax.empty((7, g.shape[2], D), BF16) for g in grads_b]

    def body(*refs):
        g, land = refs[:n], refs[n:2 * n]
        send_sems, recv_sems = refs[2 * n:3 * n], refs[3 * n:4 * n]
        for k in range(n):
            for cp in _grad_copies(g[k], land[k], send_sems[k], recv_sems[k]):
                cp.start()
        refs[-1][...] = jnp.zeros_like(refs[-1])

    res = pl.pallas_call(
        body, name=name,
        in_specs=[HBM] * (2 * n), out_specs=[SEM] * (2 * n) + [HBM] * (2 * n) + [VMEM],
        out_shape=[pltpu.SemaphoreType.DMA((7,))] * (2 * n)
        + [pltpu.HBM(a.shape, a.dtype) for a in (*grads_b, *lands)] + [jax.ShapeDtypeStruct((8, 128), F32)],
        input_output_aliases={i: i + 2 * n for i in range(2 * n)},
        compiler_params=pltpu.CompilerParams(has_side_effects=DATAFLOW),
    )(*[_hbm(a) for a in (*grads_b, *lands)])
    return [(res[k], res[n + k], res[2 * n + k], res[3 * n + k]) for k in range(n)], res[-1]


def _grads_wait(started, after, name):
    n = len(started)

    def body(*refs):
        g, land = refs[:n], refs[n:2 * n]
        send_sems, recv_sems = refs[2 * n:3 * n], refs[3 * n:4 * n]
        for k in range(n):
            for cp in _grad_copies(g[k], land[k], send_sems[k], recv_sems[k]):
                cp.wait_send()
                cp.wait_recv()

    gs = [st[2] for st in started]
    lands = [st[3] for st in started]
    res = pl.pallas_call(
        body, name=name,
        in_specs=[HBM] * (2 * n) + [SEM] * (2 * n) + [ANY], out_specs=[HBM] * (2 * n),
        out_shape=[pltpu.HBM(a.shape, a.dtype) for a in (*gs, *lands)],
        input_output_aliases={i: i for i in range(2 * n)},
        compiler_params=pltpu.CompilerParams(has_side_effects=DATAFLOW),
    )(*gs, *lands, *[st[0] for st in started], *[st[1] for st in started], after)
    return res[n:]


def _sum_partials(grad4, got, cb, name, tr):
    h = grad4.shape[2]
    per_half = h // tr

    def body(cb_ref, g_ref, o_ref, out_ref):
        acc = g_ref[...]
        for j in range(7):
            acc = acc + o_ref[j].astype(F32)
        out_ref[...] = acc

    return pl.pallas_call(
        body, name=name,
        grid_spec=pltpu.PrefetchScalarGridSpec(
            num_scalar_prefetch=1, grid=(per_half,),
            in_specs=[pl.BlockSpec((None, None, tr, D), lambda i, cb_ref: (cb_ref[1], cb_ref[0], i, 0)),
                      pl.BlockSpec((7, tr, D), lambda i, cb_ref: (0, i, 0))],
            out_specs=pl.BlockSpec((tr, D), lambda i, cb_ref: (cb_ref[0] * per_half + i, 0))),
        out_shape=pltpu.HBM((2 * h, D), F32),
        compiler_params=_cp(("arbitrary",)),
    )(cb, grad4, _hbm(got))


def _swap_halves(shards, name):
    n = len(shards)

    def body(*refs):
        out, send_sems, recv_sems = refs[n:2 * n], refs[2 * n], refs[2 * n + 1]
        x, y, c = _place()
        cps = []
        for k in range(n):
            h = shards[k].shape[0] // 2
            mine = out[k].at[pl.ds(pl.multiple_of(c * h, 8), h)]
            cp = pltpu.make_async_remote_copy(src_ref=mine, dst_ref=mine, send_sem=send_sems.at[k],
                                              recv_sem=recv_sems.at[k], device_id=(x, y, 1 - c), device_id_type=MESH)
            cp.start()
            cps.append(cp)
        for cp in cps:
            cp.wait()

    return pl.pallas_call(
        body, name=name,
        in_specs=[ANY] * n, out_specs=[ANY] * n,
        out_shape=[jax.ShapeDtypeStruct(sh.shape, F32) for sh in shards],
        input_output_aliases={k: k for k in range(n)},
        scratch_shapes=[pltpu.SemaphoreType.DMA((n,)), pltpu.SemaphoreType.DMA((n,))],
        compiler_params=pltpu.CompilerParams(has_side_effects=True),
    )(*shards)


def _small_copies(small_ref, land_ref, send_sems, recv_sems):
    x, y, c = _place()
    me = 4 * x + 2 * y + c
    cps = []
    for d in range(1, 8):
        px, py, pc = x ^ (d >> 2), y ^ ((d >> 1) & 1), c ^ (d & 1)
        cps.append(pltpu.make_async_remote_copy(
            src_ref=small_ref, dst_ref=land_ref.at[me], send_sem=send_sems.at[d - 1], recv_sem=recv_sems.at[d - 1],
            device_id=(px, py, pc), device_id_type=MESH))
    return cps


def _small_start(small):
    land = lax.empty((8,) + small.shape, F32)

    def body(s_ref, l_ref, send_sems, recv_sems, s_thru, l_thru, token):
        for cp in _small_copies(s_ref, l_ref, send_sems, recv_sems):
            cp.start()
        token[...] = jnp.zeros_like(token)

    res = pl.pallas_call(
        body, name="small_start",
        in_specs=[HBM, HBM], out_specs=[SEM, SEM, HBM, HBM, VMEM],
        out_shape=[pltpu.SemaphoreType.DMA((7,)), pltpu.SemaphoreType.DMA((7,)), pltpu.HBM(small.shape, F32),
                   pltpu.HBM(land.shape, F32), jax.ShapeDtypeStruct((8, 128), F32)],
        input_output_aliases={0: 2, 1: 3},
        compiler_params=pltpu.CompilerParams(has_side_effects=DATAFLOW),
    )(_hbm(small), _hbm(land))
    return res[:4], res[4]


def _small_wait(started, after):
    send_sems, recv_sems, small, land = started

    def body(s_ref, l_ref, send_ref, recv_ref, after_ref, s_out, l_out):
        for cp in _small_copies(s_ref, l_ref, send_ref, recv_ref):
            cp.wait_send()
            cp.wait_recv()

    return pl.pallas_call(
        body, name="small_wait",
        in_specs=[HBM, HBM, SEM, SEM, ANY], out_specs=[HBM, HBM],
        out_shape=[pltpu.HBM(small.shape, F32), pltpu.HBM(land.shape, F32)],
        input_output_aliases={0: 0, 1: 1},
        compiler_params=pltpu.CompilerParams(has_side_effects=DATAFLOW),
    )(small, land, send_sems, recv_sems, after)


def _small_sum(small, land, me):
    rows = small.shape[0]

    def body(me_ref, s_ref, l_ref, o_ref):
        acc = None
        for k in range(8):
            term = jnp.where(me_ref[0] == k, s_ref[...], l_ref[k])
            acc = term if k == 0 else acc + term
        o_ref[...] = acc

    return pl.pallas_call(
        body, name="small_sum",
        in_specs=[SMEM, VMEM, VMEM], out_specs=VMEM,
        out_shape=jax.ShapeDtypeStruct((rows, D), F32),
    )(me, small, land)


def _adamw(w, g, m, v, name, tr, g_transposed=False):
    rows, cols = w.shape

    def body(w_ref, g_ref, m_ref, v_ref, d_ref, nm_ref, nv_ref, *gt_ref):
        g_ = g_ref[...]
        if g_transposed:
            g_ = g_.T
            gt_ref[0][...] = g_
        nm = ADAM_B1 * m_ref[...] + (1.0 - ADAM_B1) * g_
        nv = ADAM_B2 * v_ref[...] + (1.0 - ADAM_B2) * (g_ * g_)
        m_hat = nm / (1.0 - ADAM_B1 ** ADAM_STEP)
        v_hat = nv / (1.0 - ADAM_B2 ** ADAM_STEP)
        d_ref[...] = -ADAM_LR * (m_hat / (jnp.sqrt(v_hat) + ADAM_EPS) + ADAM_WD * w_ref[...])
        nm_ref[...] = nm
        nv_ref[...] = nv

    spec = pl.BlockSpec((tr, cols), lambda i: (i, 0))
    g_spec = pl.BlockSpec((cols, tr), lambda i: (0, i)) if g_transposed else spec
    n_out = 4 if g_transposed else 3
    return pl.pallas_call(
        body, name=name, grid=(rows // tr,),
        in_specs=[spec, g_spec, spec, spec], out_specs=[spec] * n_out,
        out_shape=[jax.ShapeDtypeStruct((rows, cols), F32)] * n_out,
        compiler_params=_cp(("parallel",)),
    )(*[_hbm(a) for a in (w, g, m, v)])


def _local_step(x, target, w_in_t, late_weights, norm_a_g, norm_b_g, sinks_a, ln1_g, ln1_b,
                conv_w, conv_b, ln2_g, ln2_b, slopes, on_grad, on_small):
    cwb = jnp.concatenate([conv_w, conv_b[None]], axis=0).reshape(4, 2, FF)

    proj, xb = _proj(x, w_in_t, "proj")
    o_a, lse_a = _attn_a_fwd(proj, sinks_a)
    fwd_b = None
    for r in reversed(B_DILATIONS):
        fwd_b = _attn_b_fwd(proj, slopes, r, fwd_b)
    o_b, lse_b = fwd_b
    w_o = late_weights(1, lse_b)
    cat, z1, h1, h1b = _mix_ln1(x, o_a, o_b, norm_a_g, norm_b_g, w_o, ln1_g, ln1_b)
    w_up = late_weights(2, h1b)
    up = _up_proj(h1b, w_up)
    a, gate, a1 = _conv_gelu(up, cwb)
    w_down = late_weights(3, a)
    dz2, dz2b, st2 = _down_ln2_loss(a, w_down, h1, target, ln2_g, ln2_b)

    on_grad(3, *_grad_w(a, dz2b, "grad_w_down", tm=FF // 2))
    dup, dconv = _conv_gelu_bwd(_d_act(dz2b, w_down), up, gate, a1, cwb)
    on_grad(2, *_grad_w(dup, h1b, "grad_w_up", tm=FF // 2, lhs_halves=True))
    dz1, dz1b, st1 = _dh1_ln1_bwd(dz2, dup, w_up, z1, ln1_g)
    tok = on_grad(1, *_grad_w(cat, dz1b, "grad_w_o", tm=512))
    d_oa, d_ob, st_n = _dcat_rms_bwd(dz1b, w_o, o_a, o_b, norm_a_g + tok[0, 0], norm_b_g)
    dqa, dka, dva, dsink = _attn_a_bwd(proj, sinks_a, d_oa, o_a, lse_a)
    dconv = dconv.reshape(4, 2 * FF)
    tok = on_small(dict(loss=st2[2, 0:1], norm_a_g=st_n[0], norm_b_g=st_n[1], sinks_a=dsink[:, 0],
                        ln1_g=st1[0], ln1_b=st1[1], conv_w=dconv[0:3].reshape(-1), conv_b=dconv[3],
                        ln2_g=st2[0], ln2_b=st2[1]))
    slopes = slopes + tok[0, 0]
    bwd_b = None
    for r in reversed(B_DILATIONS):
        bwd_b = _attn_b_bwd(proj, slopes, d_ob, o_b, lse_b, r, bwd_b, BF16 if r == 1 else F32)
    dparts = tuple(_hbm(a) for a in (dqa, dka, dva, *bwd_b))
    tok = on_grad(0, *_grad_w_in(dparts, xb))
    return _grad_x(dz1, dparts, w_in_t, tok)


SMALL_ORDER = ("loss", "norm_a_g", "norm_b_g", "sinks_a", "ln1_g", "ln1_b", "conv_b", "ln2_g", "ln2_b", "conv_w")
SMALL_SIZES = dict(loss=1, norm_a_g=512, norm_b_g=512, sinks_a=8, ln1_g=D, ln1_b=D, conv_b=2 * FF, ln2_g=D, ln2_b=D,
                   conv_w=3 * 2 * FF)


def _pack(parts, rows):
    flat = jnp.concatenate([parts[k].reshape(-1).astype(F32) for k in parts])
    return jnp.pad(flat, (0, rows * D - flat.shape[0])).reshape(rows, D)


def _unpack(buf, names, sizes):
    flat = buf.reshape(-1)
    out, at = {}, 0
    for k in names:
        out[k] = flat[at:at + sizes[k]]
        at += sizes[k]
    return out


def kernel(x, w_in, norm_a_g, norm_b_g, sinks_a, w_o, ln1_g, ln1_b, w_up, conv_w, conv_b, w_down, ln2_g, ln2_b, loss_target, m_w_in, m_norm_a_g, m_norm_b_g, m_sinks_a, m_w_o, m_ln1_g, m_ln1_b, m_w_up, m_conv_w, m_conv_b, m_w_down, m_ln2_g, m_ln2_b, v_w_in, v_norm_a_g, v_norm_b_g, v_sinks_a, v_w_o, v_ln1_g, v_ln1_b, v_w_up, v_conv_w, v_conv_b, v_w_down, v_ln2_g, v_ln2_b):
    xi, yi, ci = _place()
    chip = (2 * xi + yi).astype(I32)
    core = ci.astype(I32)

    w_in_rows, m_w_in_rows, v_w_in_rows = w_in.T, m_w_in.T, v_w_in.T
    shards = (w_in_rows.astype(BF16), w_o.astype(BF16), w_up.astype(BF16), w_down.astype(BF16))
    w_in_t, conv_w4 = _gather_w_in(shards[0], conv_w)
    conv_w_f = conv_w4.transpose(1, 0, 2).reshape(3, 2 * FF)
    w_started, w_tok = _weights_start(shards[1:], conv_w4)
    slopes = jnp.asarray(SLOPES, F32) + w_tok[0, 0]

    halves_rows = [r // 2 for r in SHARD_ROWS]
    grads4, grads_b4, started = [None] * 4, [None] * 4, [None] * 4

    def on_grad(k, g, g_b):
        grads4[k] = g.reshape(N_CHIPS, 2, halves_rows[k], D)
        grads_b4[k] = g_b.reshape(N_CHIPS, 2, halves_rows[k], D)
        if k > 1:
            return None
        group = (1, 2, 3) if k == 1 else (0,)
        sts, tok = _grads_start([grads_b4[i] for i in group], f"grads_start_{k}")
        for i, st in zip(group, sts):
            started[i] = st
        return tok

    small_rows = 32
    small_started = []

    def on_small(parts):
        st, tok = _small_start(_pack({k: parts[k] for k in SMALL_ORDER}, small_rows))
        small_started.append(st)
        return tok

    gx = _local_step(
        x[0], loss_target[0], w_in_t, lambda k, after: _weights_wait(w_started[k - 1], after, f"weights_wait_{k}"),
        norm_a_g, norm_b_g, sinks_a, ln1_g, ln1_b, conv_w_f, conv_b, ln2_g, ln2_b, slopes, on_grad, on_small)

    tiles = (96, 128, 352, 176)
    core_chip = jnp.stack([core, chip])
    got = _grads_wait(started[1:], gx, "grads_wait_1")
    halves = [_sum_partials(grads4[k], got[k - 1], core_chip, f"sum_partials_{k}", tiles[k]) for k in (1, 2, 3)]
    g_w_o, g_w_up_rows, g_w_down = _swap_halves(halves, "swap_halves")
    delta, new_m, new_v = {}, {}, {}
    for k, g, tr in (("w_o", g_w_o, 128), ("w_down", g_w_down, 176)):
        delta[k], new_m[k], new_v[k] = _adamw(dict(w_o=w_o, w_down=w_down)[k], g, dict(w_o=m_w_o, w_down=m_w_down)[k],
                                              dict(w_o=v_w_o, w_down=v_w_down)[k], f"adamw_{k}", tr)
    delta["w_up"], new_m["w_up"], new_v["w_up"], g_w_up = _adamw(w_up, g_w_up_rows, m_w_up, v_w_up, "adamw_w_up", 256,
                                                                 g_transposed=True)

    got = _grads_wait(started[:1], delta["w_up"], "grads_wait_0")
    half_in = _sum_partials(grads4[0], got[0], core_chip, "sum_partials_0", tiles[0])
    (g_w_in_rows,) = _swap_halves([half_in], "swap_halves_in")
    small_mine, small_land = _small_wait(small_started[0], g_w_in_rows)
    totals = _small_sum(small_mine, small_land, (4 * xi + 2 * yi + ci).astype(I32).reshape(1))
    tot = _unpack(totals, SMALL_ORDER, SMALL_SIZES)
    loss = tot["loss"][0]
    cols = 2 * FF // N_CHIPS
    g_conv_w = lax.dynamic_slice(tot["conv_w"].reshape(3, 2 * FF), (0, chip * cols), (3, cols))
    g_small = dict(norm_a_g=tot["norm_a_g"], norm_b_g=tot["norm_b_g"], sinks_a=tot["sinks_a"], ln1_g=tot["ln1_g"],
                   ln1_b=tot["ln1_b"], conv_w=g_conv_w, conv_b=tot["conv_b"], ln2_g=tot["ln2_g"], ln2_b=tot["ln2_b"])

    weights = dict(w_in=w_in, norm_a_g=norm_a_g, norm_b_g=norm_b_g, sinks_a=sinks_a, w_o=w_o, ln1_g=ln1_g, ln1_b=ln1_b,
                   w_up=w_up, conv_w=conv_w, conv_b=conv_b, w_down=w_down, ln2_g=ln2_g, ln2_b=ln2_b)
    ms = dict(w_in=m_w_in, norm_a_g=m_norm_a_g, norm_b_g=m_norm_b_g, sinks_a=m_sinks_a, w_o=m_w_o, ln1_g=m_ln1_g,
              ln1_b=m_ln1_b, w_up=m_w_up, conv_w=m_conv_w, conv_b=m_conv_b, w_down=m_w_down, ln2_g=m_ln2_g, ln2_b=m_ln2_b)
    vs = dict(w_in=v_w_in, norm_a_g=v_norm_a_g, norm_b_g=v_norm_b_g, sinks_a=v_sinks_a, w_o=v_w_o, ln1_g=v_ln1_g,
              ln1_b=v_ln1_b, w_up=v_w_up, conv_w=v_conv_w, conv_b=v_conv_b, w_down=v_w_down, ln2_g=v_ln2_g, ln2_b=v_ln2_b)
    order = list(weights)
    grad = dict(g_small, w_in=g_w_in_rows.T, w_o=g_w_o, w_up=g_w_up, w_down=g_w_down)

    delta["w_in"], new_m["w_in"], new_v["w_in"] = [
        a.T for a in _adamw(w_in_rows, g_w_in_rows, m_w_in_rows, v_w_in_rows, "adamw_w_in", 144)]
    small_names = [k for k in order if k not in delta]
    sizes = {k: weights[k].size for k in small_names}
    rows = 16
    packed = [_pack({k: src[k] for k in small_names}, rows) for src in (weights, grad, ms, vs)]
    for res, buf in zip((delta, new_m, new_v), _adamw(*packed, "adamw_small", rows)):
        for k, val in _unpack(buf, small_names, sizes).items():
            res[k] = val.reshape(weights[k].shape)

    return (loss, gx[None], *[grad[k] for k in order], *[delta[k] for k in order],
            *[new_m[k] for k in order], *[new_v[k] for k in order])
```

```python
import functools
import math

import jax
import jax.numpy as jnp
from jax import lax
from jax.experimental import pallas as pl
from jax.experimental.pallas import tpu as pltpu

F32, BF16, I32 = jnp.float32, jnp.bfloat16, jnp.int32

D = 1024
FF = 2816
HD = 64
NH = 8
WA, WB = 768, 1536
WIN = WA + WB
BLK = 128
ALPHA = 2.0 ** 0.25
LN_EPS, RMS_EPS = 1e-5, 1e-6
SCALE = 1.0 / math.sqrt(HD)
A_MAX_DIST, B_MAX_DIST = 127, 128
B_DILATIONS = (1, 4, 16)
SLOPES = tuple(2.0 ** (-(i + 1)) for i in range(NH))
SHARD_ROWS = (WIN // 4, D // 4, 2 * FF // 4, FF // 4)
N_CHIPS = 4
ADAM_LR, ADAM_B1, ADAM_B2, ADAM_EPS, ADAM_WD, ADAM_STEP = 0.001, 0.9, 0.999, 1e-08, 0.01, 10
MESH = pl.DeviceIdType.MESH
ANY = pl.BlockSpec(memory_space=pl.ANY)
SMEM = pl.BlockSpec(memory_space=pltpu.SMEM)
VMEM = pl.BlockSpec(memory_space=pltpu.VMEM)
HBM = pl.BlockSpec(memory_space=pltpu.HBM)
SEM = pl.BlockSpec(memory_space=pltpu.SEMAPHORE)
DATAFLOW = pltpu.SideEffectType.DATAFLOW_SIDE_EFFECTING


def _cp(sem, mb=48):
    return pltpu.CompilerParams(dimension_semantics=sem, vmem_limit_bytes=mb << 20)


def _nn(a, b):
    return lax.dot_general(a, b, (((1,), (0,)), ((), ())), preferred_element_type=F32)


def _nt(a, b):
    return lax.dot_general(a, b, (((1,), (1,)), ((), ())), preferred_element_type=F32)


def _tn(a, b):
    return lax.dot_general(a, b, (((0,), (0,)), ((), ())), preferred_element_type=F32)


def _resident(shape):
    n = len(shape)
    return pl.BlockSpec(shape, lambda *_: (0,) * n, pipeline_mode=pl.Buffered(1))


def _const(shape):
    n = len(shape)
    return pl.BlockSpec(shape, lambda *_: (0,) * n)


def _proj(x, w_t, name, tm=512):
    s = x.shape[0]
    n = w_t.shape[0]

    def body(x_ref, w_ref, o_ref, xb_ref):
        xb = x_ref[...].astype(BF16)
        xb_ref[...] = xb
        res = _nt(xb, w_ref[...])
        for g in range(n // 128):
            o_ref[g] = res[:, 128 * g:128 * (g + 1)]

    return pl.pallas_call(
        body, name=name, grid=(s // tm,),
        in_specs=[pl.BlockSpec((tm, D), lambda i: (i, 0)), _resident((n, D))],
        out_specs=[pl.BlockSpec((n // 128, tm, 128), lambda i: (0, i, 0)), pl.BlockSpec((tm, D), lambda i: (i, 0))],
        out_shape=[jax.ShapeDtypeStruct((n // 128, s, 128), F32), jax.ShapeDtypeStruct((s, D), BF16)],
        compiler_params=_cp(("parallel",)),
    )(x, w_t)


def _grad_w(lhs, rhs, name, tm, tk=2048, lhs_halves=False):
    s = rhs.shape[0]
    if lhs_halves:
        per_half = lhs.shape[2] // tm
        n = 2 * lhs.shape[2]
        lhs_spec = pl.BlockSpec((None, tk, tm), lambda i, k: (i // per_half, k, i % per_half))
    else:
        n = lhs.shape[1]
        lhs_spec = pl.BlockSpec((tk, tm), lambda i, k: (k, i))
    nk = s // tk

    def body(l_ref, r_ref, o_ref, ob_ref):
        k = pl.program_id(1)

        @pl.when(k == 0)
        def _():
            o_ref[...] = jnp.zeros_like(o_ref)

        o_ref[...] += _tn(l_ref[...], r_ref[...])

        @pl.when(k == nk - 1)
        def _():
            ob_ref[...] = o_ref[...].astype(BF16)

    return pl.pallas_call(
        body, name=name, grid=(n // tm, nk),
        in_specs=[lhs_spec, pl.BlockSpec((tk, D), lambda i, k: (k, 0))],
        out_specs=[pl.BlockSpec((tm, D), lambda i, k: (i, 0))] * 2,
        out_shape=[pltpu.HBM((n, D), F32), pltpu.HBM((n, D), BF16)],
        compiler_params=_cp(("parallel", "arbitrary")),
    )(lhs, rhs)


def _band_base(max_dist, dist_unit, first):
    row = lax.broadcasted_iota(I32, (BLK, 2 * BLK), 0)
    col = lax.broadcasted_iota(I32, (BLK, 2 * BLK), 1)
    dist = BLK + row - col
    ok = (dist >= 0) & (dist <= max_dist)
    if first:
        ok = ok & (col >= BLK)
    return jnp.where(ok, dist.astype(F32) * (-float(dist_unit)), -jnp.inf)


def _half_mask(shape, e):
    lane = lax.broadcasted_iota(I32, shape, 1)
    return (lane < HD) if e == 0 else (lane >= HD)


def _to_half(x, e, g):
    if g != e:
        x = pltpu.roll(x, HD, 1)
    return jnp.where(_half_mask(x.shape, g), x, 0.0)


def _stack_heads(scalars, tile):
    return jnp.concatenate([scalars[0] * tile, scalars[1] * tile], axis=0)


def _pair_fwd(q2, kb, vb, base, slopes, kv_heads, sinks):
    lo = _half_mask((BLK, 2 * HD), 0)
    if slopes is None:
        bias = base
    elif sinks is None:
        bias = _stack_heads(slopes, base)
    else:
        col0 = lax.broadcasted_iota(I32, base.shape, 1) == 0
        bias = jnp.concatenate([jnp.where(col0, sinks[e], slopes[e] * base) for e in (0, 1)], axis=0)
    qs = jnp.concatenate([_to_half(q2, e, kv_heads[e]) * SCALE for e in (0, 1)], axis=0).astype(BF16)
    s = _nt(qs, kb) + bias
    m = jnp.max(s, axis=1, keepdims=True)
    p = jnp.exp(s - m)
    l = jnp.sum(p, axis=1, keepdims=True)
    o = _nn(p.astype(BF16), vb) / l
    lse = m + jnp.log(l)
    halves = []
    for e in (0, 1):
        oh = o[e * BLK:(e + 1) * BLK]
        halves.append(pltpu.roll(oh, HD, 1) if kv_heads[e] != e else oh)
    o2 = jnp.where(lo, halves[0], halves[1])
    lse2 = jnp.where(lo, jnp.broadcast_to(lse[:BLK], (BLK, 2 * HD)), jnp.broadcast_to(lse[BLK:], (BLK, 2 * HD)))
    return o2, lse2


def _pair_bwd(q2, kb, vb, do2, o2, lse2, base, slopes, kv_heads, sinks):
    lo = _half_mask((BLK, 2 * HD), 0)
    prod = do2 * o2
    lses, deltas = [], []
    for e in (0, 1):
        hq = _half_mask((BLK, 2 * HD), e)
        lses.append(jnp.max(jnp.where(hq, lse2, -jnp.inf), axis=1, keepdims=True))
        deltas.append(jnp.sum(jnp.where(hq, prod, 0.0), axis=1, keepdims=True))
    lse = jnp.concatenate(lses, axis=0)
    delta = jnp.concatenate(deltas, axis=0)
    qs = jnp.concatenate([_to_half(q2, e, kv_heads[e]) * SCALE for e in (0, 1)], axis=0).astype(BF16)
    dos = jnp.concatenate([_to_half(do2, e, kv_heads[e]) for e in (0, 1)], axis=0).astype(BF16)
    p = jnp.exp(_nt(qs, kb) + (base if slopes is None else _stack_heads(slopes, base)) - lse)
    ds = (p * (_nt(dos, vb) - delta)).astype(BF16)
    dq = _nn(ds, kb) * SCALE
    halves = []
    for e in (0, 1):
        dqh = dq[e * BLK:(e + 1) * BLK]
        halves.append(pltpu.roll(dqh, HD, 1) if kv_heads[e] != e else dqh)
    dq2 = jnp.where(lo, halves[0], halves[1])
    dk2 = _tn(ds, qs)
    dv2 = _tn(p.astype(BF16), dos)
    dsinks = []
    if sinks is not None:
        for e in (0, 1):
            dsinks.append(jnp.sum(-jnp.exp(sinks[e] - lses[e]) * deltas[e], axis=0, keepdims=True))
    return dq2, dk2, dv2, dsinks


A_BLOCKS_PER_STEP = 2
A_BLOCKS_PER_STEP_BWD = 1


def _attn_a_fwd(proj, sinks):
    s = proj.shape[1]
    nq = A_BLOCKS_PER_STEP
    rows = BLK * nq
    steps = s // rows

    def body(sink_ref, q_ref, kp_ref, kc_ref, vp_ref, vc_ref, o_ref, lse_ref):
        n = pl.program_id(0)
        base_rest = _band_base(A_MAX_DIST, 1, False)
        base_0 = jnp.where(n > 0, base_rest, _band_base(A_MAX_DIST, 1, True))
        for i in range(nq):
            cur = pl.ds(i * BLK, BLK)
            k_prev = kc_ref[pl.ds((i - 1) * BLK, BLK), :] if i > 0 else kp_ref[...]
            v_prev = vc_ref[pl.ds((i - 1) * BLK, BLK), :] if i > 0 else vp_ref[...]
            first_key = lax.broadcasted_iota(I32, (2 * BLK, 128), 0) == 0
            kb = jnp.where(first_key, 0.0, jnp.concatenate([k_prev, kc_ref[cur, :]], axis=0)).astype(BF16)
            vb = jnp.where(first_key, 0.0, jnp.concatenate([v_prev, vc_ref[cur, :]], axis=0)).astype(BF16)
            for j in range(NH // 2):
                g = j // 2
                o2, lse2 = _pair_fwd(q_ref[j, cur, :], kb, vb, base_rest if i > 0 else base_0,
                                     (SLOPES[2 * j], SLOPES[2 * j + 1]), (g, g), (sink_ref[2 * j], sink_ref[2 * j + 1]))
                o_ref[j, cur, :] = o2
                lse_ref[j, cur, :] = lse2

    before = lambda n: jnp.maximum(n * nq - 1, 0)
    slab = lambda g: pl.BlockSpec((None, rows, 128), lambda n: (g, n, 0))
    edge = lambda g: pl.BlockSpec((None, BLK, 128), lambda n: (g, before(n), 0))
    quad = pl.BlockSpec((4, rows, 128), lambda n: (0, n, 0))
    return pl.pallas_call(
        body, name="attn_a_fwd", grid=(steps,),
        in_specs=[SMEM, quad, edge(4), slab(4), edge(5), slab(5)],
        out_specs=[quad, quad],
        out_shape=[jax.ShapeDtypeStruct((4, s, 128), F32)] * 2,
        compiler_params=_cp(("parallel",)),
    )(sinks, proj, proj, proj, proj, proj)


def _attn_a_bwd(proj, sinks, d_o, o, lse):
    s = proj.shape[1]
    nq = A_BLOCKS_PER_STEP_BWD
    rows = BLK * nq
    steps = s // rows

    def body(sink_ref, q_ref, kp_ref, kc_ref, vp_ref, vc_ref, do_ref, o_ref, lse_ref,
             dq_ref, dk_ref, dv_ref, dsink_ref, kcar, vcar):
        n = pl.program_id(0)

        @pl.when(n == 0)
        def _():
            kcar[...] = jnp.zeros_like(kcar)
            vcar[...] = jnp.zeros_like(vcar)
            dsink_ref[...] = jnp.zeros_like(dsink_ref)

        dk_ref[...] = kcar[...].astype(BF16)
        dv_ref[...] = vcar[...].astype(BF16)

        @pl.when(n < steps)
        def _():
            base_rest = _band_base(A_MAX_DIST, 1, False)
            base_0 = jnp.where(n > 0, base_rest, _band_base(A_MAX_DIST, 1, True))
            for i in range(nq):
                cur = pl.ds(i * BLK, BLK)
                k_prev = kc_ref[pl.ds((i - 1) * BLK, BLK), :] if i > 0 else kp_ref[...]
                v_prev = vc_ref[pl.ds((i - 1) * BLK, BLK), :] if i > 0 else vp_ref[...]
                kb = jnp.concatenate([k_prev, kc_ref[cur, :]], axis=0).astype(BF16)
                vb = jnp.concatenate([v_prev, vc_ref[cur, :]], axis=0).astype(BF16)
                dk_win = dv_win = None
                for j in range(NH // 2):
                    g = j // 2
                    dq2, dk2, dv2, dsk = _pair_bwd(q_ref[j, cur, :], kb, vb, do_ref[j, cur, :], o_ref[j, cur, :],
                                                   lse_ref[j, cur, :], base_rest if i > 0 else base_0,
                                                   (SLOPES[2 * j], SLOPES[2 * j + 1]), (g, g),
                                                   (sink_ref[2 * j], sink_ref[2 * j + 1]))
                    dq_ref[j, cur, :] = dq2.astype(BF16)
                    dk_win = dk2 if j == 0 else dk_win + dk2
                    dv_win = dv2 if j == 0 else dv_win + dv2
                    for e in (0, 1):
                        h = 2 * j + e
                        dsink_ref[h:h + 1, :] += jnp.broadcast_to(dsk[e], (1, 128))
                if i == 0:
                    last = pl.ds((nq - 1) * BLK, BLK)
                    dk_ref[last, :] = (kcar[last, :] + dk_win[:BLK]).astype(BF16)
                    dv_ref[last, :] = (vcar[last, :] + dv_win[:BLK]).astype(BF16)
                else:
                    kcar[pl.ds((i - 1) * BLK, BLK), :] += dk_win[:BLK]
                    vcar[pl.ds((i - 1) * BLK, BLK), :] += dv_win[:BLK]
                kcar[cur, :] = dk_win[BLK:]
                vcar[cur, :] = dv_win[BLK:]

    cur_step = lambda n: jnp.minimum(n, steps - 1)
    before = lambda n: jnp.maximum(cur_step(n) * nq - 1, 0)
    out_prev = lambda n: jnp.maximum(n - 1, 0)
    quad = pl.BlockSpec((4, rows, 128), lambda n: (0, cur_step(n), 0))
    slab = lambda g: pl.BlockSpec((None, rows, 128), lambda n: (g, cur_step(n), 0))
    edge = lambda g: pl.BlockSpec((None, BLK, 128), lambda n: (g, before(n), 0))
    return pl.pallas_call(
        body, name="attn_a_bwd", grid=(steps + 1,),
        in_specs=[SMEM, quad, edge(4), slab(4), edge(5), slab(5), quad, quad, quad],
        out_specs=[quad,
                   pl.BlockSpec((rows, 128), lambda n: (out_prev(n), 0)),
                   pl.BlockSpec((rows, 128), lambda n: (out_prev(n), 0)),
                   pl.BlockSpec((NH, 128), lambda n: (0, 0))],
        out_shape=[pltpu.HBM((4, s, 128), BF16), pltpu.HBM((s, 128), BF16), pltpu.HBM((s, 128), BF16),
                   jax.ShapeDtypeStruct((NH, 128), F32)],
        scratch_shapes=[pltpu.VMEM((rows, 128), F32), pltpu.VMEM((rows, 128), F32)],
        compiler_params=_cp(("arbitrary",)),
    )(sinks, proj, proj, proj, proj, proj, d_o, o, lse)


def _stream(rho, i, r):
    start = i * BLK * r + rho
    return pl.ds(start, BLK, stride=r) if r > 1 else pl.ds(start, BLK)


def _for_streams(r, fn, side_by_side=4):
    if r <= side_by_side:
        for rho in range(r):
            fn(rho)
    else:
        def group(it, carry):
            for u in range(side_by_side):
                fn(side_by_side * it + u)
            return carry

        lax.fori_loop(0, r // side_by_side, group, 0)


B_BLOCKS_PER_STEP = {1: 8, 4: 2, 16: 1}
B_BLOCKS_PER_STEP_FWD = {1: 16, 4: 4, 16: 1}


def _attn_b_fwd(proj, slopes, r, so_far=None):
    s = proj.shape[1]
    nq = B_BLOCKS_PER_STEP_FWD[r]
    rows = BLK * r * nq
    steps = s // rows
    qc, kc, vc = WA // 128, WA // 128 + 4, WA // 128 + 8
    chained = so_far is not None

    def body(slope_ref, q_ref, kp_ref, kc_ref, vp_ref, vc_ref, *rest):
        po_ref, pl_ref = rest[:2] if chained else (None, None)
        o_ref, lse_ref = rest[-2:]
        j = pl.program_id(0)
        sb = pl.program_id(1)
        sl2 = (slope_ref[2 * j], slope_ref[2 * j + 1])
        bias_rest = _stack_heads(sl2, _band_base(B_MAX_DIST, r, False))
        bias_0 = jnp.where(sb > 0, bias_rest, _stack_heads(sl2, _band_base(B_MAX_DIST, r, True)))

        def stream(rho):
            for i in range(nq):
                cur = _stream(rho, i, r)
                k_prev = kc_ref[_stream(rho, i - 1, r), :] if i > 0 else kp_ref[_stream(rho, 0, r), :]
                v_prev = vc_ref[_stream(rho, i - 1, r), :] if i > 0 else vp_ref[_stream(rho, 0, r), :]
                kb = jnp.concatenate([k_prev, kc_ref[cur, :]], axis=0).astype(BF16)
                vb = jnp.concatenate([v_prev, vc_ref[cur, :]], axis=0).astype(BF16)
                o2, lse2 = _pair_fwd(q_ref[cur, :], kb, vb, bias_rest if i > 0 else bias_0, None, (0, 1), None)
                if chained:
                    lse1 = pl_ref[cur, :]
                    m = jnp.maximum(lse1, lse2)
                    e1, e2 = jnp.exp(lse1 - m), jnp.exp(lse2 - m)
                    den = e1 + e2
                    o2 = (e1 * po_ref[cur, :] + e2 * o2) * (1.0 / den)
                    lse2 = m + jnp.log(den)
                o_ref[cur, :] = o2
                lse_ref[cur, :] = lse2

        _for_streams(r, stream, side_by_side=16)

    before = lambda sb: jnp.maximum(sb * nq - 1, 0)
    result = pl.BlockSpec((None, rows, 128), lambda j, sb: (j, sb, 0))
    return pl.pallas_call(
        body, name=f"attn_b_fwd_r{r}", grid=(NH // 2, steps),
        in_specs=[SMEM,
                  pl.BlockSpec((None, rows, 128), lambda j, sb: (qc + j, sb, 0)),
                  pl.BlockSpec((None, BLK * r, 128), lambda j, sb: (kc + j, before(sb), 0)),
                  pl.BlockSpec((None, rows, 128), lambda j, sb: (kc + j, sb, 0)),
                  pl.BlockSpec((None, BLK * r, 128), lambda j, sb: (vc + j, before(sb), 0)),
                  pl.BlockSpec((None, rows, 128), lambda j, sb: (vc + j, sb, 0))] + ([result] * 2 if chained else []),
        out_specs=[result] * 2,
        out_shape=[jax.ShapeDtypeStruct((4, s, 128), F32)] * 2,
        compiler_params=_cp(("parallel", "parallel")),
    )(slopes, proj, proj, proj, proj, proj, *(so_far if chained else ()))


def _attn_b_bwd(proj, slopes, d_o, o, lse, r, so_far=None, dtype=F32):
    s = proj.shape[1]
    nq = B_BLOCKS_PER_STEP[r]
    rows = BLK * r * nq
    steps = s // rows
    qc, kc, vc = WA // 128, WA // 128 + 4, WA // 128 + 8
    chained = so_far is not None

    def body(slope_ref, q_ref, kp_ref, kc_ref, vp_ref, vc_ref, do_ref, o_ref, lse_ref, *rest):
        pq_ref, pk_ref, pv_ref = rest[:3] if chained else (None, None, None)
        dq_ref, dk_ref, dv_ref, kcar, vcar = rest[-5:]
        j = pl.program_id(0)
        sb = pl.program_id(1)

        @pl.when(sb == 0)
        def _():
            kcar[...] = jnp.zeros_like(kcar)
            vcar[...] = jnp.zeros_like(vcar)

        def settled(car, p_ref, idx):
            return car[idx] + p_ref[idx] if chained else car[idx]

        dk_ref[...] = settled(kcar, pk_ref, ...).astype(dtype)
        dv_ref[...] = settled(vcar, pv_ref, ...).astype(dtype)

        @pl.when(sb < steps)
        def _():
            sl2 = (slope_ref[2 * j], slope_ref[2 * j + 1])
            bias_rest = _stack_heads(sl2, _band_base(B_MAX_DIST, r, False))
            bias_0 = jnp.where(sb > 0, bias_rest, _stack_heads(sl2, _band_base(B_MAX_DIST, r, True)))

            def stream(rho):
                for i in range(nq):
                    cur = _stream(rho, i, r)
                    k_prev = kc_ref[_stream(rho, i - 1, r), :] if i > 0 else kp_ref[_stream(rho, 0, r), :]
                    v_prev = vc_ref[_stream(rho, i - 1, r), :] if i > 0 else vp_ref[_stream(rho, 0, r), :]
                    kb = jnp.concatenate([k_prev, kc_ref[cur, :]], axis=0).astype(BF16)
                    vb = jnp.concatenate([v_prev, vc_ref[cur, :]], axis=0).astype(BF16)
                    dq2, dk2, dv2, _ = _pair_bwd(q_ref[cur, :], kb, vb, do_ref[cur, :], o_ref[cur, :], lse_ref[cur, :],
                                                 bias_rest if i > 0 else bias_0, None, (0, 1), None)
                    dq_ref[cur, :] = (dq2 + pq_ref[cur, :] if chained else dq2).astype(dtype)
                    if i == 0:
                        last = (_stream(rho, nq - 1, r), slice(None))
                        dk_ref[last] = (settled(kcar, pk_ref, last) + dk2[:BLK]).astype(dtype)
                        dv_ref[last] = (settled(vcar, pv_ref, last) + dv2[:BLK]).astype(dtype)
                    else:
                        kcar[_stream(rho, i - 1, r), :] += dk2[:BLK]
                        vcar[_stream(rho, i - 1, r), :] += dv2[:BLK]
                    kcar[cur, :] = dk2[BLK:]
                    vcar[cur, :] = dv2[BLK:]

            _for_streams(r, stream, side_by_side=8)

    cur_step = lambda sb: jnp.minimum(sb, steps - 1)
    before = lambda sb: jnp.maximum(cur_step(sb) * nq - 1, 0)
    out_prev = lambda sb: jnp.maximum(sb - 1, 0)
    tile = lambda slab: pl.BlockSpec((None, rows, 128), lambda j, sb: (slab + j, cur_step(sb), 0))
    edge = lambda slab: pl.BlockSpec((None, BLK * r, 128), lambda j, sb: (slab + j, before(sb), 0))
    late = pl.BlockSpec((None, rows, 128), lambda j, sb: (j, out_prev(sb), 0))
    grads = [tile(0), late, late]
    return pl.pallas_call(
        body, name=f"attn_b_bwd_r{r}", grid=(NH // 2, steps + 1),
        in_specs=[SMEM, tile(qc), edge(kc), tile(kc), edge(vc), tile(vc), tile(0), tile(0), tile(0)]
        + (grads if chained else []),
        out_specs=grads,
        out_shape=[pltpu.HBM((4, s, 128), dtype)] * 3,
        scratch_shapes=[pltpu.VMEM((rows, 128), F32), pltpu.VMEM((rows, 128), F32)],
        compiler_params=_cp(("parallel", "arbitrary")),
    )(slopes, proj, proj, proj, proj, proj, d_o, o, lse, *(so_far if chained else ()))


def _row(v):
    return v.reshape(1, -1)


def _layer_norm_stats(z):
    mu = jnp.mean(z, axis=-1, keepdims=True)
    zc = z - mu
    var = jnp.mean(zc * zc, axis=-1, keepdims=True)
    rstd = lax.rsqrt(var + LN_EPS)
    return zc * rstd, rstd


def _layer_norm_bwd(dh, zh, rstd, g):
    dzh = dh * g
    return rstd * (dzh - jnp.mean(dzh, axis=-1, keepdims=True) - zh * jnp.mean(dzh * zh, axis=-1, keepdims=True))


def _rms(o):
    return lax.rsqrt(jnp.mean(o * o, axis=-1, keepdims=True) + RMS_EPS)


def _mix_ln1(x, o_a, o_b, norm_a_g, norm_b_g, w_o, ln1_g, ln1_b, tm=512):
    s = x.shape[0]

    def wide(ref):
        return jnp.concatenate([ref[j] for j in range(4)], axis=1)

    def body(x_ref, oa_ref, ob_ref, ga_ref, gb_ref, wo_ref, g_ref, b_ref, cat_ref, z1_ref, h1_ref, h1b_ref):
        oa, ob = wide(oa_ref), wide(ob_ref)
        na = oa * _rms(oa) * ga_ref[...]
        nb_ = ob * _rms(ob) * gb_ref[...]
        cat = jnp.concatenate([na, nb_], axis=1).astype(BF16)
        cat_ref[...] = cat
        z1 = ALPHA * x_ref[...] + _nn(cat, wo_ref[...])
        z1_ref[...] = z1
        zh, _ = _layer_norm_stats(z1)
        h1 = zh * g_ref[...] + b_ref[...]
        h1_ref[...] = h1
        h1b_ref[...] = h1.astype(BF16)

    t512 = pl.BlockSpec((4, tm, 128), lambda i: (0, i, 0))
    td = pl.BlockSpec((tm, D), lambda i: (i, 0))
    return pl.pallas_call(
        body, name="mix_ln1", grid=(s // tm,),
        in_specs=[td] + [t512] * 2 + [_const((1, 512))] * 2 + [_resident((D, D))] + [_const((1, D))] * 2,
        out_specs=[td, td, td, td],
        out_shape=[jax.ShapeDtypeStruct((s, D), BF16), jax.ShapeDtypeStruct((s, D), F32),
                   jax.ShapeDtypeStruct((s, D), F32), jax.ShapeDtypeStruct((s, D), BF16)],
        compiler_params=_cp(("parallel",)),
    )(x, o_a, o_b, _row(norm_a_g), _row(norm_b_g), w_o, _row(ln1_g), _row(ln1_b))


def _gelu_and_grad(x):
    c = math.sqrt(2.0 / math.pi)
    x2 = x * x
    s = 0.5 * jnp.tanh(x * ((c * 0.044715) * x2 + c)) + 0.5
    dg = s + (x * ((6.0 * c * 0.044715) * x2 + 2.0 * c)) * (s - s * s)
    return x * s, dg


def _shifted(u, edge, row, down):
    groups = [u[8 * i:8 * i + 8] for i in range(u.shape[0] // 8)]
    others = [edge] + groups[:-1] if down else groups[1:] + [edge]
    moved = []
    for k in (1, 2):
        crossing = row >= 8 - k if down else row < k
        moved.append(jnp.concatenate([pltpu.roll(jnp.where(crossing, o, g), k if down else 8 - k, 0)
                                      for o, g in zip(others, groups)], axis=0))
    return moved


def _up_proj(h1b, w_up, tm=512):
    s = h1b.shape[0]

    def body(h_ref, w_ref, o_ref):
        h = h_ref[...]
        for half in (0, 1):
            o_ref[half] = _nn(h, w_ref[:, half * FF:(half + 1) * FF]).astype(BF16)

    return pl.pallas_call(
        body, name="up_proj", grid=(s // tm,),
        in_specs=[pl.BlockSpec((tm, D), lambda i: (i, 0)), _resident((D, 2 * FF))],
        out_specs=pl.BlockSpec((2, tm, FF), lambda i: (0, i, 0)),
        out_shape=jax.ShapeDtypeStruct((2, s, FF), BF16),
        compiler_params=_cp(("parallel",)),
    )(h1b, w_up)


def _conv_gelu(up, cwb, tm=512, tn=FF // 2, chunk_rows=16):
    s = up.shape[1]
    n_c = tm // chunk_rows

    def body(up_ref, c_ref, a_ref, g_ref, a1_ref, carry):
        @pl.when(pl.program_id(1) == 0)
        def _():
            carry[...] = jnp.zeros_like(carry)

        row = lax.broadcasted_iota(jnp.int32, (8, tn), 0)
        edge = [carry[0], carry[1]]
        for c in range(n_c):
            rows = pl.ds(c * chunk_rows, chunk_rows)
            u = []
            for half in (0, 1):
                x = up_ref[half, rows, :].astype(F32)
                r1, r2 = _shifted(x, edge[half], row, True)
                u.append(r2 * c_ref[0, half:half + 1, :] + r1 * c_ref[1, half:half + 1, :]
                         + x * c_ref[2, half:half + 1, :] + c_ref[3, half:half + 1, :])
                edge[half] = x[chunk_rows - 8:]
            g, dg = _gelu_and_grad(u[0])
            a_ref[rows, :] = (g * u[1]).astype(BF16)
            g_ref[rows, :] = g.astype(BF16)
            a1_ref[rows, :] = (u[1] * dg).astype(BF16)
        for half in (0, 1):
            carry[half] = edge[half]

    pair = pl.BlockSpec((2, tm, tn), lambda j, i: (0, i, j))
    tile = pl.BlockSpec((tm, tn), lambda j, i: (i, j))
    return pl.pallas_call(
        body, name="conv_gelu", grid=(FF // tn, s // tm),
        in_specs=[pair, pl.BlockSpec((4, 2, tn), lambda j, i: (0, 0, j))],
        out_specs=[tile, tile, tile],
        out_shape=[jax.ShapeDtypeStruct((s, FF), BF16)] * 3,
        scratch_shapes=[pltpu.VMEM((2, 8, tn), F32)],
        compiler_params=_cp(("parallel", "arbitrary")),
    )(up, cwb)


def _down_ln2_loss(a, w_down, h1, target, ln2_g, ln2_b, tm=512):
    s = a.shape[0]

    def body(a_ref, w_ref, h_ref, t_ref, g_ref, b_ref, dz_ref, dzb_ref, st_ref):
        @pl.when(pl.program_id(0) == 0)
        def _():
            st_ref[...] = jnp.zeros_like(st_ref)

        z2 = ALPHA * h_ref[...] + _nn(a_ref[...], w_ref[...])
        zh, rstd = _layer_norm_stats(z2)
        diff = zh * g_ref[...] + b_ref[...] - t_ref[...]
        part = 0.5 * jnp.sum(jnp.mean(diff * diff, axis=-1, keepdims=True), axis=0, keepdims=True)
        dy = diff * (1.0 / D)
        st_ref[0:1, :] += jnp.sum(dy * zh, axis=0, keepdims=True)
        st_ref[1:2, :] += jnp.sum(dy, axis=0, keepdims=True)
        st_ref[2:3, :] += jnp.broadcast_to(part, (1, D))
        dz = _layer_norm_bwd(dy, zh, rstd, g_ref[...])
        dz_ref[...] = dz
        dzb_ref[...] = dz.astype(BF16)

    td = pl.BlockSpec((tm, D), lambda i: (i, 0))
    return pl.pallas_call(
        body, name="down_ln2_loss", grid=(s // tm,),
        in_specs=[pl.BlockSpec((tm, FF), lambda i: (i, 0)), _resident((FF, D)), td, td, _const((1, D)), _const((1, D))],
        out_specs=[td, td, _const((8, D))],
        out_shape=[jax.ShapeDtypeStruct((s, D), F32), jax.ShapeDtypeStruct((s, D), BF16),
                   jax.ShapeDtypeStruct((8, D), F32)],
        compiler_params=_cp(("arbitrary",)),
    )(a, w_down, h1, target, _row(ln2_g), _row(ln2_b))


def _d_act(dz2b, w_down, tm=512):
    s = dz2b.shape[0]

    def body(dz_ref, w_ref, o_ref):
        o_ref[...] = _nt(dz_ref[...], w_ref[...]).astype(BF16)

    return pl.pallas_call(
        body, name="d_act", grid=(s // tm,),
        in_specs=[pl.BlockSpec((tm, D), lambda i: (i, 0)), _resident((FF, D))],
        out_specs=pl.BlockSpec((tm, FF), lambda i: (i, 0)),
        out_shape=jax.ShapeDtypeStruct((s, FF), BF16),
        compiler_params=_cp(("parallel",)),
    )(dz2b, w_down)


def _conv_gelu_bwd(da, up, g, a1, cwb, tm=512, tn=FF // 2, chunk_rows=16):
    s = da.shape[0]
    n_i = s // tm
    n_c = tm // chunk_rows

    def body(da_ref, up_ref, g_ref, a1_ref, c_ref, dup_ref, dc_ref, carry):
        @pl.when(pl.program_id(1) == 0)
        def _():
            carry[...] = jnp.zeros_like(carry)
            dc_ref[...] = jnp.zeros_like(dc_ref)

        def fold(v):
            return jnp.sum(v.reshape(chunk_rows // 8, 8, v.shape[1]), axis=0)

        def chunk(cc, state):
            after, sums = state
            rows = pl.ds((n_c - 1 - cc) * chunk_rows, chunk_rows)
            da_c = da_ref[rows, :]
            dus = ((da_c * a1_ref[rows, :]).astype(F32), (da_c * g_ref[rows, :]).astype(F32))
            head, new_sums = [], []
            for half in (0, 1):
                du = dus[half]
                up = up_ref[half, rows, :].astype(F32)
                l1, l2 = _shifted(du, after[half], row, False)
                dup = (du * c_ref[2, half:half + 1, :] + l1 * c_ref[1, half:half + 1, :]
                       + l2 * c_ref[0, half:half + 1, :])
                dup_ref[half, rows, :] = dup.astype(BF16)
                parts = (fold(l2 * up), fold(l1 * up), fold(du * up), fold(du))
                new_sums.append(parts if sums is None else tuple(a + b for a, b in zip(sums[half], parts)))
                head.append(du[:8])
            return tuple(head), new_sums

        row = lax.broadcasted_iota(jnp.int32, (8, tn), 0)
        state = ((carry[0], carry[1]), None)
        for cc in range(n_c):
            state = chunk(cc, state)
        head, sums = state
        for half in (0, 1):
            carry[half] = head[half]
            for k in range(4):
                dc_ref[k, half:half + 1, :] += jnp.sum(sums[half][k], axis=0, keepdims=True)

    rev = lambda ii: n_i - 1 - ii
    tile = pl.BlockSpec((tm, tn), lambda j, ii: (rev(ii), j))
    pair = pl.BlockSpec((2, tm, tn), lambda j, ii: (0, rev(ii), j))
    per_col = pl.BlockSpec((4, 2, tn), lambda j, ii: (0, 0, j))
    return pl.pallas_call(
        body, name="conv_gelu_bwd", grid=(FF // tn, n_i),
        in_specs=[tile, pair, tile, tile, per_col],
        out_specs=[pair, per_col],
        out_shape=[jax.ShapeDtypeStruct((2, s, FF), BF16), jax.ShapeDtypeStruct((4, 2, FF), F32)],
        scratch_shapes=[pltpu.VMEM((2, 8, tn), F32)],
        compiler_params=_cp(("parallel", "arbitrary")),
    )(da, up, g, a1, cwb)


def _dh1_ln1_bwd(dz2, dup, w_up, z1, ln1_g, tm=512):
    s = dz2.shape[0]

    def body(dz2_ref, dup_ref, w_ref, z1_ref, g_ref, dz1_ref, dz1b_ref, st_ref):
        @pl.when(pl.program_id(0) == 0)
        def _():
            st_ref[...] = jnp.zeros_like(st_ref)

        dh = ALPHA * dz2_ref[...] + _nt(dup_ref[0], w_ref[:, :FF]) + _nt(dup_ref[1], w_ref[:, FF:])
        zh, rstd = _layer_norm_stats(z1_ref[...])
        st_ref[0:1, :] += jnp.sum(dh * zh, axis=0, keepdims=True)
        st_ref[1:2, :] += jnp.sum(dh, axis=0, keepdims=True)
        dz = _layer_norm_bwd(dh, zh, rstd, g_ref[...])
        dz1_ref[...] = dz
        dz1b_ref[...] = dz.astype(BF16)

    td = pl.BlockSpec((tm, D), lambda i: (i, 0))
    return pl.pallas_call(
        body, name="dh1_ln1_bwd", grid=(s // tm,),
        in_specs=[td, pl.BlockSpec((2, tm, FF), lambda i: (0, i, 0)), _resident((D, 2 * FF)), td, _const((1, D))],
        out_specs=[td, td, _const((8, D))],
        out_shape=[jax.ShapeDtypeStruct((s, D), F32), jax.ShapeDtypeStruct((s, D), BF16),
                   jax.ShapeDtypeStruct((8, D), F32)],
        compiler_params=_cp(("arbitrary",), 58),
    )(dz2, dup, w_up, z1, _row(ln1_g))


def _dcat_rms_bwd(dz1b, w_o, o_a, o_b, norm_a_g, norm_b_g, tm=512):
    s = dz1b.shape[0]

    def body(dz_ref, w_ref, oa_ref, ob_ref, ga_ref, gb_ref, da_ref, db_ref, st_ref):
        @pl.when(pl.program_id(0) == 0)
        def _():
            st_ref[...] = jnp.zeros_like(st_ref)

        dcat = _nt(dz_ref[...], w_ref[...])
        for k, (o_ref, g_ref, d_ref) in enumerate(((oa_ref, ga_ref, da_ref), (ob_ref, gb_ref, db_ref))):
            o = jnp.concatenate([o_ref[j] for j in range(4)], axis=1)
            dn = dcat[:, 512 * k:512 * (k + 1)]
            rr = _rms(o)
            oh = o * rr
            st_ref[k:k + 1, :] += jnp.sum(dn * oh, axis=0, keepdims=True)
            doh = dn * g_ref[...]
            d_o = rr * (doh - oh * jnp.mean(doh * oh, axis=-1, keepdims=True))
            for j in range(4):
                d_ref[j] = d_o[:, 128 * j:128 * (j + 1)]

    t512 = pl.BlockSpec((4, tm, 128), lambda i: (0, i, 0))
    return pl.pallas_call(
        body, name="dcat_rms_bwd", grid=(s // tm,),
        in_specs=[pl.BlockSpec((tm, D), lambda i: (i, 0)), _resident((D, D)), t512, t512,
                  _const((1, 512)), _const((1, 512))],
        out_specs=[t512, t512, _const((8, 512))],
        out_shape=[jax.ShapeDtypeStruct((4, s, 128), F32), jax.ShapeDtypeStruct((4, s, 128), F32),
                   jax.ShapeDtypeStruct((8, 512), F32)],
        compiler_params=_cp(("arbitrary",)),
    )(dz1b, w_o, o_a, o_b, _row(norm_a_g), _row(norm_b_g))


def _grad_w_in(dparts, xb, tk=2048):
    s = xb.shape[0]
    nk = s // tk

    def body(qa, ka, va, qb, kb, vb, x_ref, o_ref, ob_ref):
        i = pl.program_id(0)
        k = pl.program_id(1)

        @pl.when(k == 0)
        def _():
            o_ref[...] = jnp.zeros_like(o_ref)

        def add(blocks):
            o_ref[...] += _tn(jnp.concatenate(blocks, axis=1), x_ref[...])

        pl.when(i == 0)(lambda: add([qa[j] for j in range(4)] + [ka[...], va[...]]))
        pl.when(i == 1)(lambda: add([qb[j] for j in range(4)] + [kb[0], kb[1]]))
        pl.when(i == 2)(lambda: add([kb[0], kb[1]] + [vb[j] for j in range(4)]))

        @pl.when(k == nk - 1)
        def _():
            ob_ref[...] = o_ref[...].astype(BF16)

    def during(tile):
        return lambda i, k: jnp.where(i == tile, k, jnp.where(i < tile, 0, nk - 1))

    quad = lambda tile: pl.BlockSpec((4, tk, 128), lambda i, k: (0, during(tile)(i, k), 0))
    one = pl.BlockSpec((tk, 128), lambda i, k: (during(0)(i, k), 0))
    kb_spec = pl.BlockSpec((2, tk, 128), lambda i, k: (jnp.where(i == 2, 1, 0), jnp.where(i == 0, 0, k), 0))
    return pl.pallas_call(
        body, name="grad_w_in", grid=(3, nk),
        in_specs=[quad(0), one, one, quad(1), kb_spec, quad(2), pl.BlockSpec((tk, D), lambda i, k: (k, 0))],
        out_specs=[pl.BlockSpec((WA, D), lambda i, k: (i, 0))] * 2,
        out_shape=[pltpu.HBM((WIN, D), F32), pltpu.HBM((WIN, D), BF16)],
        compiler_params=_cp(("parallel", "arbitrary"), mb=56),
    )(*dparts, xb)


def _grad_x(dz1, dparts, w_in_t, zero, tm=512):
    s = dz1.shape[0]

    def body(dz_ref, qa, ka, va, qb, kb, vb, w_ref, z_ref, o_ref):
        dp = jnp.concatenate([qa[j] for j in range(4)] + [ka[...], va[...]]
                             + [ref[j] for ref in (qb, kb, vb) for j in range(4)], axis=1)
        o_ref[...] = ALPHA * dz_ref[...] + _nn(dp, w_ref[...]) + z_ref[0:1, 0:1]

    td = pl.BlockSpec((tm, D), lambda i: (i, 0))
    quad = pl.BlockSpec((4, tm, 128), lambda i: (0, i, 0))
    one = pl.BlockSpec((tm, 128), lambda i: (i, 0))
    return pl.pallas_call(
        body, name="grad_x", grid=(s // tm,),
        in_specs=[td, quad, one, one, quad, quad, quad, _resident((WIN, D)), _const((8, 128))],
        out_specs=td, out_shape=jax.ShapeDtypeStruct((s, D), F32),
        compiler_params=_cp(("parallel",)),
    )(dz1, *dparts, w_in_t, zero)


def _place():
    return lax.axis_index("x"), lax.axis_index("y"), lax.axis_index("c")


def _other_chips(x, y):
    return [(1 - x, y), (x, 1 - y), (1 - x, 1 - y)]


def _hbm(a):
    return pltpu.with_memory_space_constraint(a, pltpu.HBM)


def _gather_w_in(shard, conv_w):
    rows_k = shard.shape[0]
    half = rows_k // 2

    def body(src, conv_src, out, conv_out, send_sems, recv_sems):
        x, y, c = _place()
        b = 2 * x + y
        sibling = (x, y, 1 - c)
        chips = _other_chips(x, y)

        def copy(idx, chip_b, core, to, first_hop=False):
            rows = out.at[pl.ds(pl.multiple_of(chip_b * rows_k + core * half, 16), half)]
            s_ref = src.at[pl.ds(pl.multiple_of(core * half, 16), half)] if first_hop else rows
            return pltpu.make_async_remote_copy(src_ref=s_ref, dst_ref=rows, send_sem=send_sems.at[idx],
                                                recv_sem=recv_sems.at[idx], device_id=to, device_id_type=MESH)

        def own_copy():
            return pltpu.make_async_remote_copy(
                src_ref=src, dst_ref=out.at[pl.ds(pl.multiple_of(b * rows_k, 16), rows_k)], send_sem=send_sems.at[6],
                recv_sem=recv_sems.at[6], device_id=sibling, device_id_type=MESH)

        def conv_copy(idx, chip_b, to):
            return pltpu.make_async_remote_copy(src_ref=conv_src, dst_ref=conv_out.at[chip_b],
                                                send_sem=send_sems.at[7 + idx], recv_sem=recv_sems.at[7 + idx],
                                                device_id=to, device_id_type=MESH)

        started = [own_copy(), conv_copy(3, b, sibling)]
        for jn, chip in enumerate(chips):
            started += [copy(jn, b, c, (chip[0], chip[1], c), first_hop=True), conv_copy(jn, b, (chip[0], chip[1], c))]
        for cp in started:
            cp.start()
        for jn, chip in enumerate(chips):
            cb = 2 * chip[0] + chip[1]
            copy(jn, cb, c, (chip[0], chip[1], c)).wait_recv()
            cp = copy(3 + jn, cb, c, sibling)
            cp.start()
            started.append(cp)
        for jn, chip in enumerate(chips):
            cb = 2 * chip[0] + chip[1]
            copy(3 + jn, cb, 1 - c, sibling).wait_recv()
            conv_copy(jn, cb, (chip[0], chip[1], c)).wait_recv()
        own_copy().wait_recv()
        conv_copy(3, b, sibling).wait_recv()
        for cp in started:
            cp.wait_send()

    return pl.pallas_call(
        body, name="gather_w_in",
        in_specs=[ANY, ANY], out_specs=[ANY, ANY],
        out_shape=[jax.ShapeDtypeStruct((N_CHIPS * rows_k, D), BF16), jax.ShapeDtypeStruct((N_CHIPS,) + conv_w.shape, F32)],
        scratch_shapes=[pltpu.SemaphoreType.DMA((11,)), pltpu.SemaphoreType.DMA((11,))],
        compiler_params=pltpu.CompilerParams(has_side_effects=True),
    )(shard, conv_w)


def _weight_copies(shard, land, send_sems, recv_sems, arrivals):
    x, y, c = _place()
    n_rows, n_cols = shard.shape
    peers = [(px, py, c) for px, py in _other_chips(x, y)] + [(x, y, 1 - c)]
    cps = []
    for jn, peer in enumerate(peers):
        at = 2 * peer[0] + peer[1] if arrivals else 2 * x + y
        if land.shape[1] == n_cols:
            dst = land.at[pl.ds(pl.multiple_of(at * n_rows, 16), n_rows)]
        else:
            dst = land.at[:, pl.ds(pl.multiple_of(at * n_cols, 128), n_cols)]
        cps.append(pltpu.make_async_remote_copy(src_ref=shard, dst_ref=dst, send_sem=send_sems.at[jn],
                                                recv_sem=recv_sems.at[jn], device_id=peer, device_id_type=MESH))
    return cps


def _weights_start(shards, after):
    n = len(shards)
    lands = [lax.empty((N_CHIPS * sh.shape[0], D) if sh.shape[1] == D else (D, N_CHIPS * sh.shape[1]), BF16)
             for sh in shards]

    def body(*refs):
        src, land = refs[:n], refs[n:2 * n]
        send_sems, recv_sems = refs[2 * n + 1:3 * n + 1], refs[3 * n + 1:4 * n + 1]
        for k in range(n):
            for send in _weight_copies(src[k], land[k], send_sems[k], recv_sems[k], False):
                send.start()
        refs[-1][...] = jnp.zeros_like(refs[-1])

    res = pl.pallas_call(
        body, name="weights_start",
        in_specs=[HBM] * (2 * n) + [ANY], out_specs=[SEM] * (2 * n) + [HBM] * (2 * n) + [VMEM],
        out_shape=[pltpu.SemaphoreType.DMA((4,))] * (2 * n)
        + [pltpu.HBM(a.shape, a.dtype) for a in (*shards, *lands)] + [jax.ShapeDtypeStruct((8, 128), F32)],
        input_output_aliases={i: i + 2 * n for i in range(2 * n)},
        compiler_params=pltpu.CompilerParams(has_side_effects=DATAFLOW),
    )(*[_hbm(a) for a in (*shards, *lands)], after)
    return [(res[k], res[n + k], res[2 * n + k], res[3 * n + k]) for k in range(n)], res[-1]


def _weights_wait(started, after, name):
    send_sems, recv_sems, shard, land = started

    def body(s_ref, l_ref, send_ref, recv_ref, after_ref, s_out, l_out):
        for cp in _weight_copies(s_ref, l_ref, send_ref, recv_ref, True):
            cp.wait_send()
            cp.wait_recv()

    return pl.pallas_call(
        body, name=name,
        in_specs=[HBM, HBM, SEM, SEM, ANY], out_specs=[HBM, HBM],
        out_shape=[pltpu.HBM(shard.shape, shard.dtype), pltpu.HBM(land.shape, land.dtype)],
        input_output_aliases={0: 0, 1: 1},
        compiler_params=pltpu.CompilerParams(has_side_effects=DATAFLOW),
    )(shard, land, send_sems, recv_sems, after)[1]


def _grad_copies(g_ref, land_ref, send_sems, recv_sems):
    x, y, c = _place()
    cps = []
    for d in range(1, 8):
        px, py, pc = x ^ (d >> 2), y ^ ((d >> 1) & 1), c ^ (d & 1)
        cps.append(pltpu.make_async_remote_copy(
            src_ref=g_ref.at[2 * px + py, pc], dst_ref=land_ref.at[d - 1], send_sem=send_sems.at[d - 1],
            recv_sem=recv_sems.at[d - 1], device_id=(px, py, pc), device_id_type=MESH))
    return cps


def _grads_start(grads_b, name):
    n = len(grads_b)
    lands = [lax.empty((7, g.shape[2], D), BF16) for g in grads_b]

    def body(*refs):
        g, land = refs[:n], refs[n:2 * n]
        send_sems, recv_sems = refs[2 * n:3 * n], refs[3 * n:4 * n]
        for k in range(n):
            for cp in _grad_copies(g[k], land[k], send_sems[k], recv_sems[k]):
                cp.start()
        refs[-1][...] = jnp.zeros_like(refs[-1])

    res = pl.pallas_call(
        body, name=name,
        in_specs=[HBM] * (2 * n), out_specs=[SEM] * (2 * n) + [HBM] * (2 * n) + [VMEM],
        out_shape=[pltpu.SemaphoreType.DMA((7,))] * (2 * n)
        + [pltpu.HBM(a.shape, a.dtype) for a in (*grads_b, *lands)] + [jax.ShapeDtypeStruct((8, 128), F32)],
        input_output_aliases={i: i + 2 * n for i in range(2 * n)},
        compiler_params=pltpu.CompilerParams(has_side_effects=DATAFLOW),
    )(*[_hbm(a) for a in (*grads_b, *lands)])
    return [(res[k], res[n + k], res[2 * n + k], res[3 * n + k]) for k in range(n)], res[-1]


def _grads_wait(started, after, name):
    n = len(started)

    def body(*refs):
        g, land = refs[:n], refs[n:2 * n]
        send_sems, recv_sems = refs[2 * n:3 * n], refs[3 * n:4 * n]
        for k in range(n):
            for cp in _grad_copies(g[k], land[k], send_sems[k], recv_sems[k]):
                cp.wait_send()
                cp.wait_recv()

    gs = [st[2] for st in started]
    lands = [st[3] for st in started]
    res = pl.pallas_call(
        body, name=name,
        in_specs=[HBM] * (2 * n) + [SEM] * (2 * n) + [ANY], out_specs=[HBM] * (2 * n),
        out_shape=[pltpu.HBM(a.shape, a.dtype) for a in (*gs, *lands)],
        input_output_aliases={i: i for i in range(2 * n)},
        compiler_params=pltpu.CompilerParams(has_side_effects=DATAFLOW),
    )(*gs, *lands, *[st[0] for st in started], *[st[1] for st in started], after)
    return res[n:]


def _sum_partials(grad4, got, cb, name, tr):
    h = grad4.shape[2]
    per_half = h // tr

    def body(cb_ref, g_ref, o_ref, out_ref):
        acc = g_ref[...]
        for j in range(7):
            acc = acc + o_ref[j].astype(F32)
        out_ref[...] = acc

    return pl.pallas_call(
        body, name=name,
        grid_spec=pltpu.PrefetchScalarGridSpec(
            num_scalar_prefetch=1, grid=(per_half,),
            in_specs=[pl.BlockSpec((None, None, tr, D), lambda i, cb_ref: (cb_ref[1], cb_ref[0], i, 0)),
                      pl.BlockSpec((7, tr, D), lambda i, cb_ref: (0, i, 0))],
            out_specs=pl.BlockSpec((tr, D), lambda i, cb_ref: (cb_ref[0] * per_half + i, 0))),
        out_shape=pltpu.HBM((2 * h, D), F32),
        compiler_params=_cp(("arbitrary",)),
    )(cb, grad4, _hbm(got))


def _swap_halves(shards, name):
    n = len(shards)

    def body(*refs):
        out, send_sems, recv_sems = refs[n:2 * n], refs[2 * n], refs[2 * n + 1]
        x, y, c = _place()
        cps = []
        for k in range(n):
            h = shards[k].shape[0] // 2
            mine = out[k].at[pl.ds(pl.multiple_of(c * h, 8), h)]
            cp = pltpu.make_async_remote_copy(src_ref=mine, dst_ref=mine, send_sem=send_sems.at[k],
                                              recv_sem=recv_sems.at[k], device_id=(x, y, 1 - c), device_id_type=MESH)
            cp.start()
            cps.append(cp)
        for cp in cps:
            cp.wait()

    return pl.pallas_call(
        body, name=name,
        in_specs=[ANY] * n, out_specs=[ANY] * n,
        out_shape=[jax.ShapeDtypeStruct(sh.shape, F32) for sh in shards],
        input_output_aliases={k: k for k in range(n)},
        scratch_shapes=[pltpu.SemaphoreType.DMA((n,)), pltpu.SemaphoreType.DMA((n,))],
        compiler_params=pltpu.CompilerParams(has_side_effects=True),
    )(*shards)


def _small_copies(small_ref, land_ref, send_sems, recv_sems):
    x, y, c = _place()
    me = 4 * x + 2 * y + c
    cps = []
    for d in range(1, 8):
        px, py, pc = x ^ (d >> 2), y ^ ((d >> 1) & 1), c ^ (d & 1)
        cps.append(pltpu.make_async_remote_copy(
            src_ref=small_ref, dst_ref=land_ref.at[me], send_sem=send_sems.at[d - 1], recv_sem=recv_sems.at[d - 1],
            device_id=(px, py, pc), device_id_type=MESH))
    return cps


def _small_start(small):
    land = lax.empty((8,) + small.shape, F32)

    def body(s_ref, l_ref, send_sems, recv_sems, s_thru, l_thru, token):
        for cp in _small_copies(s_ref, l_ref, send_sems, recv_sems):
            cp.start()
        token[...] = jnp.zeros_like(token)

    res = pl.pallas_call(
        body, name="small_start",
        in_specs=[HBM, HBM], out_specs=[SEM, SEM, HBM, HBM, VMEM],
        out_shape=[pltpu.SemaphoreType.DMA((7,)), pltpu.SemaphoreType.DMA((7,)), pltpu.HBM(small.shape, F32),
                   pltpu.HBM(land.shape, F32), jax.ShapeDtypeStruct((8, 128), F32)],
        input_output_aliases={0: 2, 1: 3},
        compiler_params=pltpu.CompilerParams(has_side_effects=DATAFLOW),
    )(_hbm(small), _hbm(land))
    return res[:4], res[4]


def _small_wait(started, after):
    send_sems, recv_sems, small, land = started

    def body(s_ref, l_ref, send_ref, recv_ref, after_ref, s_out, l_out):
        for cp in _small_copies(s_ref, l_ref, send_ref, recv_ref):
            cp.wait_send()
            cp.wait_recv()

    return pl.pallas_call(
        body, name="small_wait",
        in_specs=[HBM, HBM, SEM, SEM, ANY], out_specs=[HBM, HBM],
        out_shape=[pltpu.HBM(small.shape, F32), pltpu.HBM(land.shape, F32)],
        input_output_aliases={0: 0, 1: 1},
        compiler_params=pltpu.CompilerParams(has_side_effects=DATAFLOW),
    )(small, land, send_sems, recv_sems, after)


def _small_sum(small, land, me):
    rows = small.shape[0]

    def body(me_ref, s_ref, l_ref, o_ref):
        acc = None
        for k in range(8):
            term = jnp.where(me_ref[0] == k, s_ref[...], l_ref[k])
            acc = term if k == 0 else acc + term
        o_ref[...] = acc

    return pl.pallas_call(
        body, name="small_sum",
        in_specs=[SMEM, VMEM, VMEM], out_specs=VMEM,
        out_shape=jax.ShapeDtypeStruct((rows, D), F32),
    )(me, small, land)


def _adamw(w, g, m, v, name, tr, g_transposed=False):
    rows, cols = w.shape

    def body(w_ref, g_ref, m_ref, v_ref, d_ref, nm_ref, nv_ref, *gt_ref):
        g_ = g_ref[...]
        if g_transposed:
            g_ = g_.T
            gt_ref[0][...] = g_
        nm = ADAM_B1 * m_ref[...] + (1.0 - ADAM_B1) * g_
        nv = ADAM_B2 * v_ref[...] + (1.0 - ADAM_B2) * (g_ * g_)
        m_hat = nm / (1.0 - ADAM_B1 ** ADAM_STEP)
        v_hat = nv / (1.0 - ADAM_B2 ** ADAM_STEP)
        d_ref[...] = -ADAM_LR * (m_hat / (jnp.sqrt(v_hat) + ADAM_EPS) + ADAM_WD * w_ref[...])
        nm_ref[...] = nm
        nv_ref[...] = nv

    spec = pl.BlockSpec((tr, cols), lambda i: (i, 0))
    g_spec = pl.BlockSpec((cols, tr), lambda i: (0, i)) if g_transposed else spec
    n_out = 4 if g_transposed else 3
    return pl.pallas_call(
        body, name=name, grid=(rows // tr,),
        in_specs=[spec, g_spec, spec, spec], out_specs=[spec] * n_out,
        out_shape=[jax.ShapeDtypeStruct((rows, cols), F32)] * n_out,
        compiler_params=_cp(("parallel",)),
    )(*[_hbm(a) for a in (w, g, m, v)])


def _local_step(x, target, w_in_t, late_weights, norm_a_g, norm_b_g, sinks_a, ln1_g, ln1_b,
                conv_w, conv_b, ln2_g, ln2_b, slopes, on_grad, on_small):
    cwb = jnp.concatenate([conv_w, conv_b[None]], axis=0).reshape(4, 2, FF)

    proj, xb = _proj(x, w_in_t, "proj")
    o_a, lse_a = _attn_a_fwd(proj, sinks_a)
    fwd_b = None
    for r in reversed(B_DILATIONS):
        fwd_b = _attn_b_fwd(proj, slopes, r, fwd_b)
    o_b, lse_b = fwd_b
    w_o = late_weights(1, lse_b)
    cat, z1, h1, h1b = _mix_ln1(x, o_a, o_b, norm_a_g, norm_b_g, w_o, ln1_g, ln1_b)
    w_up = late_weights(2, h1b)
    up = _up_proj(h1b, w_up)
    a, gate, a1 = _conv_gelu(up, cwb)
    w_down = late_weights(3, a)
    dz2, dz2b, st2 = _down_ln2_loss(a, w_down, h1, target, ln2_g, ln2_b)

    on_grad(3, *_grad_w(a, dz2b, "grad_w_down", tm=FF // 2))
    dup, dconv = _conv_gelu_bwd(_d_act(dz2b, w_down), up, gate, a1, cwb)
    on_grad(2, *_grad_w(dup, h1b, "grad_w_up", tm=FF // 2, lhs_halves=True))
    dz1, dz1b, st1 = _dh1_ln1_bwd(dz2, dup, w_up, z1, ln1_g)
    tok = on_grad(1, *_grad_w(cat, dz1b, "grad_w_o", tm=512))
    d_oa, d_ob, st_n = _dcat_rms_bwd(dz1b, w_o, o_a, o_b, norm_a_g + tok[0, 0], norm_b_g)
    dqa, dka, dva, dsink = _attn_a_bwd(proj, sinks_a, d_oa, o_a, lse_a)
    dconv = dconv.reshape(4, 2 * FF)
    tok = on_small(dict(loss=st2[2, 0:1], norm_a_g=st_n[0], norm_b_g=st_n[1], sinks_a=dsink[:, 0],
                        ln1_g=st1[0], ln1_b=st1[1], conv_w=dconv[0:3].reshape(-1), conv_b=dconv[3],
                        ln2_g=st2[0], ln2_b=st2[1]))
    slopes = slopes + tok[0, 0]
    bwd_b = None
    for r in reversed(B_DILATIONS):
        bwd_b = _attn_b_bwd(proj, slopes, d_ob, o_b, lse_b, r, bwd_b, BF16 if r == 1 else F32)
    dparts = tuple(_hbm(a) for a in (dqa, dka, dva, *bwd_b))
    tok = on_grad(0, *_grad_w_in(dparts, xb))
    return _grad_x(dz1, dparts, w_in_t, tok)


SMALL_ORDER = ("loss", "norm_a_g", "norm_b_g", "sinks_a", "ln1_g", "ln1_b", "conv_b", "ln2_g", "ln2_b", "conv_w")
SMALL_SIZES = dict(loss=1, norm_a_g=512, norm_b_g=512, sinks_a=8, ln1_g=D, ln1_b=D, conv_b=2 * FF, ln2_g=D, ln2_b=D,
                   conv_w=3 * 2 * FF)


def _pack(parts, rows):
    flat = jnp.concatenate([parts[k].reshape(-1).astype(F32) for k in parts])
    return jnp.pad(flat, (0, rows * D - flat.shape[0])).reshape(rows, D)


def _unpack(buf, names, sizes):
    flat = buf.reshape(-1)
    out, at = {}, 0
    for k in names:
        out[k] = flat[at:at + sizes[k]]
        at += sizes[k]
    return out


def kernel(x, w_in, norm_a_g, norm_b_g, sinks_a, w_o, ln1_g, ln1_b, w_up, conv_w, conv_b, w_down, ln2_g, ln2_b, loss_target, m_w_in, m_norm_a_g, m_norm_b_g, m_sinks_a, m_w_o, m_ln1_g, m_ln1_b, m_w_up, m_conv_w, m_conv_b, m_w_down, m_ln2_g, m_ln2_b, v_w_in, v_norm_a_g, v_norm_b_g, v_sinks_a, v_w_o, v_ln1_g, v_ln1_b, v_w_up, v_conv_w, v_conv_b, v_w_down, v_ln2_g, v_ln2_b):
    xi, yi, ci = _place()
    chip = (2 * xi + yi).astype(I32)
    core = ci.astype(I32)

    w_in_rows, m_w_in_rows, v_w_in_rows = w_in.T, m_w_in.T, v_w_in.T
    shards = (w_in_rows.astype(BF16), w_o.astype(BF16), w_up.astype(BF16), w_down.astype(BF16))
    w_in_t, conv_w4 = _gather_w_in(shards[0], conv_w)
    conv_w_f = conv_w4.transpose(1, 0, 2).reshape(3, 2 * FF)
    w_started, w_tok = _weights_start(shards[1:], conv_w4)
    slopes = jnp.asarray(SLOPES, F32) + w_tok[0, 0]

    halves_rows = [r // 2 for r in SHARD_ROWS]
    grads4, grads_b4, started = [None] * 4, [None] * 4, [None] * 4

    def on_grad(k, g, g_b):
        grads4[k] = g.reshape(N_CHIPS, 2, halves_rows[k], D)
        grads_b4[k] = g_b.reshape(N_CHIPS, 2, halves_rows[k], D)
        if k > 1:
            return None
        group = (1, 2, 3) if k == 1 else (0,)
        sts, tok = _grads_start([grads_b4[i] for i in group], f"grads_start_{k}")
        for i, st in zip(group, sts):
            started[i] = st
        return tok

    small_rows = 32
    small_started = []

    def on_small(parts):
        st, tok = _small_start(_pack({k: parts[k] for k in SMALL_ORDER}, small_rows))
        small_started.append(st)
        return tok

    gx = _local_step(
        x[0], loss_target[0], w_in_t, lambda k, after: _weights_wait(w_started[k - 1], after, f"weights_wait_{k}"),
        norm_a_g, norm_b_g, sinks_a, ln1_g, ln1_b, conv_w_f, conv_b, ln2_g, ln2_b, slopes, on_grad, on_small)

    tiles = (96, 128, 352, 176)
    core_chip = jnp.stack([core, chip])
    got = _grads_wait(started[1:], gx, "grads_wait_1")
    halves = [_sum_partials(grads4[k], got[k - 1], core_chip, f"sum_partials_{k}", tiles[k]) for k in (1, 2, 3)]
    g_w_o, g_w_up_rows, g_w_down = _swap_halves(halves, "swap_halves")
    delta, new_m, new_v = {}, {}, {}
    for k, g, tr in (("w_o", g_w_o, 128), ("w_down", g_w_down, 176)):
        delta[k], new_m[k], new_v[k] = _adamw(dict(w_o=w_o, w_down=w_down)[k], g, dict(w_o=m_w_o, w_down=m_w_down)[k],
                                              dict(w_o=v_w_o, w_down=v_w_down)[k], f"adamw_{k}", tr)
    delta["w_up"], new_m["w_up"], new_v["w_up"], g_w_up = _adamw(w_up, g_w_up_rows, m_w_up, v_w_up, "adamw_w_up", 256,
                                                                 g_transposed=True)

    got = _grads_wait(started[:1], delta["w_up"], "grads_wait_0")
    half_in = _sum_partials(grads4[0], got[0], core_chip, "sum_partials_0", tiles[0])
    (g_w_in_rows,) = _swap_halves([half_in], "swap_halves_in")
    small_mine, small_land = _small_wait(small_started[0], g_w_in_rows)
    totals = _small_sum(small_mine, small_land, (4 * xi + 2 * yi + ci).astype(I32).reshape(1))
    tot = _unpack(totals, SMALL_ORDER, SMALL_SIZES)
    loss = tot["loss"][0]
    cols = 2 * FF // N_CHIPS
    g_conv_w = lax.dynamic_slice(tot["conv_w"].reshape(3, 2 * FF), (0, chip * cols), (3, cols))
    g_small = dict(norm_a_g=tot["norm_a_g"], norm_b_g=tot["norm_b_g"], sinks_a=tot["sinks_a"], ln1_g=tot["ln1_g"],
                   ln1_b=tot["ln1_b"], conv_w=g_conv_w, conv_b=tot["conv_b"], ln2_g=tot["ln2_g"], ln2_b=tot["ln2_b"])

    weights = dict(w_in=w_in, norm_a_g=norm_a_g, norm_b_g=norm_b_g, sinks_a=sinks_a, w_o=w_o, ln1_g=ln1_g, ln1_b=ln1_b,
                   w_up=w_up, conv_w=conv_w, conv_b=conv_b, w_down=w_down, ln2_g=ln2_g, ln2_b=ln2_b)
    ms = dict(w_in=m_w_in, norm_a_g=m_norm_a_g, norm_b_g=m_norm_b_g, sinks_a=m_sinks_a, w_o=m_w_o, ln1_g=m_ln1_g,
              ln1_b=m_ln1_b, w_up=m_w_up, conv_w=m_conv_w, conv_b=m_conv_b, w_down=m_w_down, ln2_g=m_ln2_g, ln2_b=m_ln2_b)
    vs = dict(w_in=v_w_in, norm_a_g=v_norm_a_g, norm_b_g=v_norm_b_g, sinks_a=v_sinks_a, w_o=v_w_o, ln1_g=v_ln1_g,
              ln1_b=v_ln1_b, w_up=v_w_up, conv_w=v_conv_w, conv_b=v_conv_b, w_down=v_w_down, ln2_g=v_ln2_g, ln2_b=v_ln2_b)
    order = list(weights)
    grad = dict(g_small, w_in=g_w_in_rows.T, w_o=g_w_o, w_up=g_w_up, w_down=g_w_down)

    delta["w_in"], new_m["w_in"], new_v["w_in"] = [
        a.T for a in _adamw(w_in_rows, g_w_in_rows, m_w_in_rows, v_w_in_rows, "adamw_w_in", 144)]
    small_names = [k for k in order if k not in delta]
    sizes = {k: weights[k].size for k in small_names}
    rows = 16
    packed = [_pack({k: src[k] for k in small_names}, rows) for src in (weights, grad, ms, vs)]
    for res, buf in zip((delta, new_m, new_v), _adamw(*packed, "adamw_small", rows)):
        for k, val in _unpack(buf, small_names, sizes).items():
            res[k] = val.reshape(weights[k].shape)

    return (loss, gx[None], *[grad[k] for k in order], *[delta[k] for k in order],
            *[new_m[k] for k in order], *[new_v[k] for k in order])
```

```python
import functools
import math

import jax
import jax.numpy as jnp
from jax import lax
from jax.experimental import pallas as pl
from jax.experimental.pallas import tpu as pltpu

F32, BF16, I32 = jnp.float32, jnp.bfloat16, jnp.int32

D = 1024
FF = 2816
HD = 64
NH = 8
WA, WB = 768, 1536
WIN = WA + WB
BLK = 128
ALPHA = 2.0 ** 0.25
LN_EPS, RMS_EPS = 1e-5, 1e-6
SCALE = 1.0 / math.sqrt(HD)
A_MAX_DIST, B_MAX_DIST = 127, 128
B_DILATIONS = (1, 4, 16)
SLOPES = tuple(2.0 ** (-(i + 1)) for i in range(NH))
SHARD_ROWS = (WIN // 4, D // 4, 2 * FF // 4, FF // 4)
N_CHIPS = 4
ADAM_LR, ADAM_B1, ADAM_B2, ADAM_EPS, ADAM_WD, ADAM_STEP = 0.001, 0.9, 0.999, 1e-08, 0.01, 10
MESH = pl.DeviceIdType.MESH
ANY = pl.BlockSpec(memory_space=pl.ANY)
SMEM = pl.BlockSpec(memory_space=pltpu.SMEM)
VMEM = pl.BlockSpec(memory_space=pltpu.VMEM)
HBM = pl.BlockSpec(memory_space=pltpu.HBM)
SEM = pl.BlockSpec(memory_space=pltpu.SEMAPHORE)
DATAFLOW = pltpu.SideEffectType.DATAFLOW_SIDE_EFFECTING


def _cp(sem, mb=48):
    return pltpu.CompilerParams(dimension_semantics=sem, vmem_limit_bytes=mb << 20)


def _nn(a, b):
    return lax.dot_general(a, b, (((1,), (0,)), ((), ())), preferred_element_type=F32)


def _nt(a, b):
    return lax.dot_general(a, b, (((1,), (1,)), ((), ())), preferred_element_type=F32)


def _tn(a, b):
    return lax.dot_general(a, b, (((0,), (0,)), ((), ())), preferred_element_type=F32)


def _resident(shape):
    n = len(shape)
    return pl.BlockSpec(shape, lambda *_: (0,) * n, pipeline_mode=pl.Buffered(1))


def _const(shape):
    n = len(shape)
    return pl.BlockSpec(shape, lambda *_: (0,) * n)


def _proj(x, w_t, name, tm=512):
    s = x.shape[0]
    n = w_t.shape[0]

    def body(x_ref, w_ref, o_ref, xb_ref):
        xb = x_ref[...].astype(BF16)
        xb_ref[...] = xb
        res = _nt(xb, w_ref[...])
        for g in range(n // 128):
            o_ref[g] = res[:, 128 * g:128 * (g + 1)]

    return pl.pallas_call(
        body, name=name, grid=(s // tm,),
        in_specs=[pl.BlockSpec((tm, D), lambda i: (i, 0)), _resident((n, D))],
        out_specs=[pl.BlockSpec((n // 128, tm, 128), lambda i: (0, i, 0)), pl.BlockSpec((tm, D), lambda i: (i, 0))],
        out_shape=[jax.ShapeDtypeStruct((n // 128, s, 128), F32), jax.ShapeDtypeStruct((s, D), BF16)],
        compiler_params=_cp(("parallel",)),
    )(x, w_t)


def _grad_w(lhs, rhs, name, tm, tk=2048, lhs_halves=False):
    s = rhs.shape[0]
    if lhs_halves:
        per_half = lhs.shape[2] // tm
        n = 2 * lhs.shape[2]
        lhs_spec = pl.BlockSpec((None, tk, tm), lambda i, k: (i // per_half, k, i % per_half))
    else:
        n = lhs.shape[1]
        lhs_spec = pl.BlockSpec((tk, tm), lambda i, k: (k, i))
    nk = s // tk

    def body(l_ref, r_ref, o_ref, ob_ref):
        k = pl.program_id(1)

        @pl.when(k == 0)
        def _():
            o_ref[...] = jnp.zeros_like(o_ref)

        o_ref[...] += _tn(l_ref[...], r_ref[...])

        @pl.when(k == nk - 1)
        def _():
            ob_ref[...] = o_ref[...].astype(BF16)

    return pl.pallas_call(
        body, name=name, grid=(n // tm, nk),
        in_specs=[lhs_spec, pl.BlockSpec((tk, D), lambda i, k: (k, 0))],
        out_specs=[pl.BlockSpec((tm, D), lambda i, k: (i, 0))] * 2,
        out_shape=[pltpu.HBM((n, D), F32), pltpu.HBM((n, D), BF16)],
        compiler_params=_cp(("parallel", "arbitrary")),
    )(lhs, rhs)


def _band_base(max_dist, dist_unit, first):
    row = lax.broadcasted_iota(I32, (BLK, 2 * BLK), 0)
    col = lax.broadcasted_iota(I32, (BLK, 2 * BLK), 1)
    dist = BLK + row - col
    ok = (dist >= 0) & (dist <= max_dist)
    if first:
        ok = ok & (col >= BLK)
    return jnp.where(ok, dist.astype(F32) * (-float(dist_unit)), -jnp.inf)


def _half_mask(shape, e):
    lane = lax.broadcasted_iota(I32, shape, 1)
    return (lane < HD) if e == 0 else (lane >= HD)


def _to_half(x, e, g):
    if g != e:
        x = pltpu.roll(x, HD, 1)
    return jnp.where(_half_mask(x.shape, g), x, 0.0)


def _stack_heads(scalars, tile):
    return jnp.concatenate([scalars[0] * tile, scalars[1] * tile], axis=0)


def _pair_fwd(q2, kb, vb, base, slopes, kv_heads, sinks):
    lo = _half_mask((BLK, 2 * HD), 0)
    if slopes is None:
        bias = base
    elif sinks is None:
        bias = _stack_heads(slopes, base)
    else:
        col0 = lax.broadcasted_iota(I32, base.shape, 1) == 0
        bias = jnp.concatenate([jnp.where(col0, sinks[e], slopes[e] * base) for e in (0, 1)], axis=0)
    qs = jnp.concatenate([_to_half(q2, e, kv_heads[e]) * SCALE for e in (0, 1)], axis=0).astype(BF16)
    s = _nt(qs, kb) + bias
    m = jnp.max(s, axis=1, keepdims=True)
    p = jnp.exp(s - m)
    l = jnp.sum(p, axis=1, keepdims=True)
    o = _nn(p.astype(BF16), vb) / l
    lse = m + jnp.log(l)
    halves = []
    for e in (0, 1):
        oh = o[e * BLK:(e + 1) * BLK]
        halves.append(pltpu.roll(oh, HD, 1) if kv_heads[e] != e else oh)
    o2 = jnp.where(lo, halves[0], halves[1])
    lse2 = jnp.where(lo, jnp.broadcast_to(lse[:BLK], (BLK, 2 * HD)), jnp.broadcast_to(lse[BLK:], (BLK, 2 * HD)))
    return o2, lse2


def _pair_bwd(q2, kb, vb, do2, o2, lse2, base, slopes, kv_heads, sinks):
    lo = _half_mask((BLK, 2 * HD), 0)
    prod = do2 * o2
    lses, deltas = [], []
    for e in (0, 1):
        hq = _half_mask((BLK, 2 * HD), e)
        lses.append(jnp.max(jnp.where(hq, lse2, -jnp.inf), axis=1, keepdims=True))
        deltas.append(jnp.sum(jnp.where(hq, prod, 0.0), axis=1, keepdims=True))
    lse = jnp.concatenate(lses, axis=0)
    delta = jnp.concatenate(deltas, axis=0)
    qs = jnp.concatenate([_to_half(q2, e, kv_heads[e]) * SCALE for e in (0, 1)], axis=0).astype(BF16)
    dos = jnp.concatenate([_to_half(do2, e, kv_heads[e]) for e in (0, 1)], axis=0).astype(BF16)
    p = jnp.exp(_nt(qs, kb) + (base if slopes is None else _stack_heads(slopes, base)) - lse)
    ds = (p * (_nt(dos, vb) - delta)).astype(BF16)
    dq = _nn(ds, kb) * SCALE
    halves = []
    for e in (0, 1):
        dqh = dq[e * BLK:(e + 1) * BLK]
        halves.append(pltpu.roll(dqh, HD, 1) if kv_heads[e] != e else dqh)
    dq2 = jnp.where(lo, halves[0], halves[1])
    dk2 = _tn(ds, qs)
    dv2 = _tn(p.astype(BF16), dos)
    dsinks = []
    if sinks is not None:
        for e in (0, 1):
            dsinks.append(jnp.sum(-jnp.exp(sinks[e] - lses[e]) * deltas[e], axis=0, keepdims=True))
    return dq2, dk2, dv2, dsinks


A_BLOCKS_PER_STEP = 2
A_BLOCKS_PER_STEP_BWD = 1


def _attn_a_fwd(proj, sinks):
    s = proj.shape[1]
    nq = A_BLOCKS_PER_STEP
    rows = BLK * nq
    steps = s // rows

    def body(sink_ref, q_ref, kp_ref, kc_ref, vp_ref, vc_ref, o_ref, lse_ref):
        n = pl.program_id(0)
        base_rest = _band_base(A_MAX_DIST, 1, False)
        base_0 = jnp.where(n > 0, base_rest, _band_base(A_MAX_DIST, 1, True))
        for i in range(nq):
            cur = pl.ds(i * BLK, BLK)
            k_prev = kc_ref[pl.ds((i - 1) * BLK, BLK), :] if i > 0 else kp_ref[...]
            v_prev = vc_ref[pl.ds((i - 1) * BLK, BLK), :] if i > 0 else vp_ref[...]
            first_key = lax.broadcasted_iota(I32, (2 * BLK, 128), 0) == 0
            kb = jnp.where(first_key, 0.0, jnp.concatenate([k_prev, kc_ref[cur, :]], axis=0)).astype(BF16)
            vb = jnp.where(first_key, 0.0, jnp.concatenate([v_prev, vc_ref[cur, :]], axis=0)).astype(BF16)
            for j in range(NH // 2):
                g = j // 2
                o2, lse2 = _pair_fwd(q_ref[j, cur, :], kb, vb, base_rest if i > 0 else base_0,
                                     (SLOPES[2 * j], SLOPES[2 * j + 1]), (g, g), (sink_ref[2 * j], sink_ref[2 * j + 1]))
                o_ref[j, cur, :] = o2
                lse_ref[j, cur, :] = lse2

    before = lambda n: jnp.maximum(n * nq - 1, 0)
    slab = lambda g: pl.BlockSpec((None, rows, 128), lambda n: (g, n, 0))
    edge = lambda g: pl.BlockSpec((None, BLK, 128), lambda n: (g, before(n), 0))
    quad = pl.BlockSpec((4, rows, 128), lambda n: (0, n, 0))
    return pl.pallas_call(
        body, name="attn_a_fwd", grid=(steps,),
        in_specs=[SMEM, quad, edge(4), slab(4), edge(5), slab(5)],
        out_specs=[quad, quad],
        out_shape=[jax.ShapeDtypeStruct((4, s, 128), F32)] * 2,
        compiler_params=_cp(("parallel",)),
    )(sinks, proj, proj, proj, proj, proj)


def _attn_a_bwd(proj, sinks, d_o, o, lse):
    s = proj.shape[1]
    nq = A_BLOCKS_PER_STEP_BWD
    rows = BLK * nq
    steps = s // rows

    def body(sink_ref, q_ref, kp_ref, kc_ref, vp_ref, vc_ref, do_ref, o_ref, lse_ref,
             dq_ref, dk_ref, dv_ref, dsink_ref, kcar, vcar):
        n = pl.program_id(0)

        @pl.when(n == 0)
        def _():
            kcar[...] = jnp.zeros_like(kcar)
            vcar[...] = jnp.zeros_like(vcar)
            dsink_ref[...] = jnp.zeros_like(dsink_ref)

        dk_ref[...] = kcar[...].astype(BF16)
        dv_ref[...] = vcar[...].astype(BF16)

        @pl.when(n < steps)
        def _():
            base_rest = _band_base(A_MAX_DIST, 1, False)
            base_0 = jnp.where(n > 0, base_rest, _band_base(A_MAX_DIST, 1, True))
            for i in range(nq):
                cur = pl.ds(i * BLK, BLK)
                k_prev = kc_ref[pl.ds((i - 1) * BLK, BLK), :] if i > 0 else kp_ref[...]
                v_prev = vc_ref[pl.ds((i - 1) * BLK, BLK), :] if i > 0 else vp_ref[...]
                kb = jnp.concatenate([k_prev, kc_ref[cur, :]], axis=0).astype(BF16)
                vb = jnp.concatenate([v_prev, vc_ref[cur, :]], axis=0).astype(BF16)
                dk_win = dv_win = None
                for j in range(NH // 2):
                    g = j // 2
                    dq2, dk2, dv2, dsk = _pair_bwd(q_ref[j, cur, :], kb, vb, do_ref[j, cur, :], o_ref[j, cur, :],
                                                   lse_ref[j, cur, :], base_rest if i > 0 else base_0,
                                                   (SLOPES[2 * j], SLOPES[2 * j + 1]), (g, g),
                                                   (sink_ref[2 * j], sink_ref[2 * j + 1]))
                    dq_ref[j, cur, :] = dq2.astype(BF16)
                    dk_win = dk2 if j == 0 else dk_win + dk2
                    dv_win = dv2 if j == 0 else dv_win + dv2
                    for e in (0, 1):
                        h = 2 * j + e
                        dsink_ref[h:h + 1, :] += jnp.broadcast_to(dsk[e], (1, 128))
                if i == 0:
                    last = pl.ds((nq - 1) * BLK, BLK)
                    dk_ref[last, :] = (kcar[last, :] + dk_win[:BLK]).astype(BF16)
                    dv_ref[last, :] = (vcar[last, :] + dv_win[:BLK]).astype(BF16)
                else:
                    kcar[pl.ds((i - 1) * BLK, BLK), :] += dk_win[:BLK]
                    vcar[pl.ds((i - 1) * BLK, BLK), :] += dv_win[:BLK]
                kcar[cur, :] = dk_win[BLK:]
                vcar[cur, :] = dv_win[BLK:]

    cur_step = lambda n: jnp.minimum(n, steps - 1)
    before = lambda n: jnp.maximum(cur_step(n) * nq - 1, 0)
    out_prev = lambda n: jnp.maximum(n - 1, 0)
    quad = pl.BlockSpec((4, rows, 128), lambda n: (0, cur_step(n), 0))
    slab = lambda g: pl.BlockSpec((None, rows, 128), lambda n: (g, cur_step(n), 0))
    edge = lambda g: pl.BlockSpec((None, BLK, 128), lambda n: (g, before(n), 0))
    return pl.pallas_call(
        body, name="attn_a_bwd", grid=(steps + 1,),
        in_specs=[SMEM, quad, edge(4), slab(4), edge(5), slab(5), quad, quad, quad],
        out_specs=[quad,
                   pl.BlockSpec((rows, 128), lambda n: (out_prev(n), 0)),
                   pl.BlockSpec((rows, 128), lambda n: (out_prev(n), 0)),
                   pl.BlockSpec((NH, 128), lambda n: (0, 0))],
        out_shape=[pltpu.HBM((4, s, 128), BF16), pltpu.HBM((s, 128), BF16), pltpu.HBM((s, 128), BF16),
                   jax.ShapeDtypeStruct((NH, 128), F32)],
        scratch_shapes=[pltpu.VMEM((rows, 128), F32), pltpu.VMEM((rows, 128), F32)],
        compiler_params=_cp(("arbitrary",)),
    )(sinks, proj, proj, proj, proj, proj, d_o, o, lse)


def _stream(rho, i, r):
    start = i * BLK * r + rho
    return pl.ds(start, BLK, stride=r) if r > 1 else pl.ds(start, BLK)


def _for_streams(r, fn, side_by_side=4):
    if r <= side_by_side:
        for rho in range(r):
            fn(rho)
    else:
        def group(it, carry):
            for u in range(side_by_side):
                fn(side_by_side * it + u)
            return carry

        lax.fori_loop(0, r // side_by_side, group, 0)


B_BLOCKS_PER_STEP = {1: 8, 4: 2, 16: 1}
B_BLOCKS_PER_STEP_FWD = {1: 16, 4: 4, 16: 1}


def _attn_b_fwd(proj, slopes, r, so_far=None):
    s = proj.shape[1]
    nq = B_BLOCKS_PER_STEP_FWD[r]
    rows = BLK * r * nq
    steps = s // rows
    qc, kc, vc = WA // 128, WA // 128 + 4, WA // 128 + 8
    chained = so_far is not None

    def body(slope_ref, q_ref, kp_ref, kc_ref, vp_ref, vc_ref, *rest):
        po_ref, pl_ref = rest[:2] if chained else (None, None)
        o_ref, lse_ref = rest[-2:]
        j = pl.program_id(0)
        sb = pl.program_id(1)
        sl2 = (slope_ref[2 * j], slope_ref[2 * j + 1])
        bias_rest = _stack_heads(sl2, _band_base(B_MAX_DIST, r, False))
        bias_0 = jnp.where(sb > 0, bias_rest, _stack_heads(sl2, _band_base(B_MAX_DIST, r, True)))

        def stream(rho):
            for i in range(nq):
                cur = _stream(rho, i, r)
                k_prev = kc_ref[_stream(rho, i - 1, r), :] if i > 0 else kp_ref[_stream(rho, 0, r), :]
                v_prev = vc_ref[_stream(rho, i - 1, r), :] if i > 0 else vp_ref[_stream(rho, 0, r), :]
                kb = jnp.concatenate([k_prev, kc_ref[cur, :]], axis=0).astype(BF16)
                vb = jnp.concatenate([v_prev, vc_ref[cur, :]], axis=0).astype(BF16)
                o2, lse2 = _pair_fwd(q_ref[cur, :], kb, vb, bias_rest if i > 0 else bias_0, None, (0, 1), None)
                if chained:
                    lse1 = pl_ref[cur, :]
                    m = jnp.maximum(lse1, lse2)
                    e1, e2 = jnp.exp(lse1 - m), jnp.exp(lse2 - m)
                    den = e1 + e2
                    o2 = (e1 * po_ref[cur, :] + e2 * o2) * (1.0 / den)
                    lse2 = m + jnp.log(den)
                o_ref[cur, :] = o2
                lse_ref[cur, :] = lse2

        _for_streams(r, stream, side_by_side=16)

    before = lambda sb: jnp.maximum(sb * nq - 1, 0)
    result = pl.BlockSpec((None, rows, 128), lambda j, sb: (j, sb, 0))
    return pl.pallas_call(
        body, name=f"attn_b_fwd_r{r}", grid=(NH // 2, steps),
        in_specs=[SMEM,
                  pl.BlockSpec((None, rows, 128), lambda j, sb: (qc + j, sb, 0)),
                  pl.BlockSpec((None, BLK * r, 128), lambda j, sb: (kc + j, before(sb), 0)),
                  pl.BlockSpec((None, rows, 128), lambda j, sb: (kc + j, sb, 0)),
                  pl.BlockSpec((None, BLK * r, 128), lambda j, sb: (vc + j, before(sb), 0)),
                  pl.BlockSpec((None, rows, 128), lambda j, sb: (vc + j, sb, 0))] + ([result] * 2 if chained else []),
        out_specs=[result] * 2,
        out_shape=[jax.ShapeDtypeStruct((4, s, 128), F32)] * 2,
        compiler_params=_cp(("parallel", "parallel")),
    )(slopes, proj, proj, proj, proj, proj, *(so_far if chained else ()))


def _attn_b_bwd(proj, slopes, d_o, o, lse, r, so_far=None, dtype=F32):
    s = proj.shape[1]
    nq = B_BLOCKS_PER_STEP[r]
    rows = BLK * r * nq
    steps = s // rows
    qc, kc, vc = WA // 128, WA // 128 + 4, WA // 128 + 8
    chained = so_far is not None
    by_stream = nq == 1 and r > 1

    def body(slope_ref, q_ref, kp_ref, kc_ref, vp_ref, vc_ref, do_ref, o_ref, lse_ref, *rest):
        pq_ref, pk_ref, pv_ref = rest[:3] if chained else (None, None, None)
        dq_ref, dk_ref, dv_ref, kcar, vcar = rest[-5:]
        j = pl.program_id(0)
        sb = pl.program_id(1)

        @pl.when(sb == 0)
        def _():
            kcar[...] = jnp.zeros_like(kcar)
            vcar[...] = jnp.zeros_like(vcar)

        def car_rows(rho, i):
            return pl.ds(pl.multiple_of(rho * BLK, BLK), BLK) if by_stream else _stream(rho, i, r)

        def settled(car, p_ref, idx, car_idx=None):
            mine = car[idx if car_idx is None else car_idx]
            return mine + p_ref[idx] if chained else mine

        if by_stream:
            @pl.when(sb == steps)
            def _():
                def out(rho):
                    rows, kept = (_stream(rho, 0, r), slice(None)), (car_rows(rho, 0), slice(None))
                    dk_ref[rows] = settled(kcar, pk_ref, rows, kept).astype(dtype)
                    dv_ref[rows] = settled(vcar, pv_ref, rows, kept).astype(dtype)

                _for_streams(r, out, side_by_side=8)
        else:
            dk_ref[...] = settled(kcar, pk_ref, ...).astype(dtype)
            dv_ref[...] = settled(vcar, pv_ref, ...).astype(dtype)

        @pl.when(sb < steps)
        def _():
            sl2 = (slope_ref[2 * j], slope_ref[2 * j + 1])
            bias_rest = _stack_heads(sl2, _band_base(B_MAX_DIST, r, False))
            bias_0 = jnp.where(sb > 0, bias_rest, _stack_heads(sl2, _band_base(B_MAX_DIST, r, True)))

            def stream(rho):
                for i in range(nq):
                    cur = _stream(rho, i, r)
                    k_prev = kc_ref[_stream(rho, i - 1, r), :] if i > 0 else kp_ref[_stream(rho, 0, r), :]
                    v_prev = vc_ref[_stream(rho, i - 1, r), :] if i > 0 else vp_ref[_stream(rho, 0, r), :]
                    kb = jnp.concatenate([k_prev, kc_ref[cur, :]], axis=0).astype(BF16)
                    vb = jnp.concatenate([v_prev, vc_ref[cur, :]], axis=0).astype(BF16)
                    dq2, dk2, dv2, _ = _pair_bwd(q_ref[cur, :], kb, vb, do_ref[cur, :], o_ref[cur, :], lse_ref[cur, :],
                                                 bias_rest if i > 0 else bias_0, None, (0, 1), None)
                    dq_ref[cur, :] = (dq2 + pq_ref[cur, :] if chained else dq2).astype(dtype)
                    if i == 0:
                        last, kept = (_stream(rho, nq - 1, r), slice(None)), (car_rows(rho, nq - 1), slice(None))
                        dk_ref[last] = (settled(kcar, pk_ref, last, kept) + dk2[:BLK]).astype(dtype)
                        dv_ref[last] = (settled(vcar, pv_ref, last, kept) + dv2[:BLK]).astype(dtype)
                    else:
                        kcar[car_rows(rho, i - 1), :] += dk2[:BLK]
                        vcar[car_rows(rho, i - 1), :] += dv2[:BLK]
                    kcar[car_rows(rho, i), :] = dk2[BLK:]
                    vcar[car_rows(rho, i), :] = dv2[BLK:]

            _for_streams(r, stream, side_by_side=8)

    cur_step = lambda sb: jnp.minimum(sb, steps - 1)
    before = lambda sb: jnp.maximum(cur_step(sb) * nq - 1, 0)
    out_prev = lambda sb: jnp.maximum(sb - 1, 0)
    tile = lambda slab: pl.BlockSpec((None, rows, 128), lambda j, sb: (slab + j, cur_step(sb), 0))
    edge = lambda slab: pl.BlockSpec((None, BLK * r, 128), lambda j, sb: (slab + j, before(sb), 0))
    late = pl.BlockSpec((None, rows, 128), lambda j, sb: (j, out_prev(sb), 0))
    grads = [tile(0), late, late]
    return pl.pallas_call(
        body, name=f"attn_b_bwd_r{r}", grid=(NH // 2, steps + 1),
        in_specs=[SMEM, tile(qc), edge(kc), tile(kc), edge(vc), tile(vc), tile(0), tile(0), tile(0)]
        + (grads if chained else []),
        out_specs=grads,
        out_shape=[pltpu.HBM((4, s, 128), dtype)] * 3,
        scratch_shapes=[pltpu.VMEM((rows, 128), F32), pltpu.VMEM((rows, 128), F32)],
        compiler_params=_cp(("parallel", "arbitrary")),
    )(slopes, proj, proj, proj, proj, proj, d_o, o, lse, *(so_far if chained else ()))


def _row(v):
    return v.reshape(1, -1)


def _layer_norm_stats(z):
    mu = jnp.mean(z, axis=-1, keepdims=True)
    zc = z - mu
    var = jnp.mean(zc * zc, axis=-1, keepdims=True)
    rstd = lax.rsqrt(var + LN_EPS)
    return zc * rstd, rstd


def _layer_norm_bwd(dh, zh, rstd, g):
    dzh = dh * g
    return rstd * (dzh - jnp.mean(dzh, axis=-1, keepdims=True) - zh * jnp.mean(dzh * zh, axis=-1, keepdims=True))


def _rms(o):
    return lax.rsqrt(jnp.mean(o * o, axis=-1, keepdims=True) + RMS_EPS)


def _mix_ln1(x, o_a, o_b, norm_a_g, norm_b_g, w_o, ln1_g, ln1_b, tm=512):
    s = x.shape[0]

    def wide(ref):
        return jnp.concatenate([ref[j] for j in range(4)], axis=1)

    def body(x_ref, oa_ref, ob_ref, ga_ref, gb_ref, wo_ref, g_ref, b_ref, cat_ref, z1_ref, h1_ref, h1b_ref):
        oa, ob = wide(oa_ref), wide(ob_ref)
        na = oa * _rms(oa) * ga_ref[...]
        nb_ = ob * _rms(ob) * gb_ref[...]
        cat = jnp.concatenate([na, nb_], axis=1).astype(BF16)
        cat_ref[...] = cat
        z1 = ALPHA * x_ref[...] + _nn(cat, wo_ref[...])
        z1_ref[...] = z1
        zh, _ = _layer_norm_stats(z1)
        h1 = zh * g_ref[...] + b_ref[...]
        h1_ref[...] = h1
        h1b_ref[...] = h1.astype(BF16)

    t512 = pl.BlockSpec((4, tm, 128), lambda i: (0, i, 0))
    td = pl.BlockSpec((tm, D), lambda i: (i, 0))
    return pl.pallas_call(
        body, name="mix_ln1", grid=(s // tm,),
        in_specs=[td] + [t512] * 2 + [_const((1, 512))] * 2 + [_resident((D, D))] + [_const((1, D))] * 2,
        out_specs=[td, td, td, td],
        out_shape=[jax.ShapeDtypeStruct((s, D), BF16), jax.ShapeDtypeStruct((s, D), F32),
                   jax.ShapeDtypeStruct((s, D), F32), jax.ShapeDtypeStruct((s, D), BF16)],
        compiler_params=_cp(("parallel",)),
    )(x, o_a, o_b, _row(norm_a_g), _row(norm_b_g), w_o, _row(ln1_g), _row(ln1_b))


def _gelu_and_grad(x):
    c = math.sqrt(2.0 / math.pi)
    x2 = x * x
    s = 0.5 * jnp.tanh(x * ((c * 0.044715) * x2 + c)) + 0.5
    dg = s + (x * ((6.0 * c * 0.044715) * x2 + 2.0 * c)) * (s - s * s)
    return x * s, dg


def _shifted(u, edge, row, down):
    groups = [u[8 * i:8 * i + 8] for i in range(u.shape[0] // 8)]
    others = [edge] + groups[:-1] if down else groups[1:] + [edge]
    moved = []
    for k in (1, 2):
        crossing = row >= 8 - k if down else row < k
        moved.append(jnp.concatenate([pltpu.roll(jnp.where(crossing, o, g), k if down else 8 - k, 0)
                                      for o, g in zip(others, groups)], axis=0))
    return moved


def _up_proj(h1b, w_up, tm=512):
    s = h1b.shape[0]

    def body(h_ref, w_ref, o_ref):
        h = h_ref[...]
        for half in (0, 1):
            o_ref[half] = _nn(h, w_ref[:, half * FF:(half + 1) * FF]).astype(BF16)

    return pl.pallas_call(
        body, name="up_proj", grid=(s // tm,),
        in_specs=[pl.BlockSpec((tm, D), lambda i: (i, 0)), _resident((D, 2 * FF))],
        out_specs=pl.BlockSpec((2, tm, FF), lambda i: (0, i, 0)),
        out_shape=jax.ShapeDtypeStruct((2, s, FF), BF16),
        compiler_params=_cp(("parallel",)),
    )(h1b, w_up)


def _conv_gelu(up, cwb, tm=512, tn=FF // 2, chunk_rows=16):
    s = up.shape[1]
    n_c = tm // chunk_rows

    def body(up_ref, c_ref, a_ref, g_ref, a1_ref, carry):
        @pl.when(pl.program_id(1) == 0)
        def _():
            carry[...] = jnp.zeros_like(carry)

        row = lax.broadcasted_iota(jnp.int32, (8, tn), 0)
        edge = [carry[0], carry[1]]
        for c in range(n_c):
            rows = pl.ds(c * chunk_rows, chunk_rows)
            u = []
            for half in (0, 1):
                x = up_ref[half, rows, :].astype(F32)
                r1, r2 = _shifted(x, edge[half], row, True)
                u.append(r2 * c_ref[0, half:half + 1, :] + r1 * c_ref[1, half:half + 1, :]
                         + x * c_ref[2, half:half + 1, :] + c_ref[3, half:half + 1, :])
                edge[half] = x[chunk_rows - 8:]
            g, dg = _gelu_and_grad(u[0])
            a_ref[rows, :] = (g * u[1]).astype(BF16)
            g_ref[rows, :] = g.astype(BF16)
            a1_ref[rows, :] = (u[1] * dg).astype(BF16)
        for half in (0, 1):
            carry[half] = edge[half]

    pair = pl.BlockSpec((2, tm, tn), lambda j, i: (0, i, j))
    tile = pl.BlockSpec((tm, tn), lambda j, i: (i, j))
    return pl.pallas_call(
        body, name="conv_gelu", grid=(FF // tn, s // tm),
        in_specs=[pair, pl.BlockSpec((4, 2, tn), lambda j, i: (0, 0, j))],
        out_specs=[tile, tile, tile],
        out_shape=[jax.ShapeDtypeStruct((s, FF), BF16)] * 3,
        scratch_shapes=[pltpu.VMEM((2, 8, tn), F32)],
        compiler_params=_cp(("parallel", "arbitrary")),
    )(up, cwb)


def _down_ln2_loss(a, w_down, h1, target, ln2_g, ln2_b, tm=512):
    s = a.shape[0]

    def body(a_ref, w_ref, h_ref, t_ref, g_ref, b_ref, dz_ref, dzb_ref, st_ref):
        @pl.when(pl.program_id(0) == 0)
        def _():
            st_ref[...] = jnp.zeros_like(st_ref)

        z2 = ALPHA * h_ref[...] + _nn(a_ref[...], w_ref[...])
        zh, rstd = _layer_norm_stats(z2)
        diff = zh * g_ref[...] + b_ref[...] - t_ref[...]
        part = 0.5 * jnp.sum(jnp.mean(diff * diff, axis=-1, keepdims=True), axis=0, keepdims=True)
        dy = diff * (1.0 / D)
        st_ref[0:1, :] += jnp.sum(dy * zh, axis=0, keepdims=True)
        st_ref[1:2, :] += jnp.sum(dy, axis=0, keepdims=True)
        st_ref[2:3, :] += jnp.broadcast_to(part, (1, D))
        dz = _layer_norm_bwd(dy, zh, rstd, g_ref[...])
        dz_ref[...] = dz
        dzb_ref[...] = dz.astype(BF16)

    td = pl.BlockSpec((tm, D), lambda i: (i, 0))
    return pl.pallas_call(
        body, name="down_ln2_loss", grid=(s // tm,),
        in_specs=[pl.BlockSpec((tm, FF), lambda i: (i, 0)), _resident((FF, D)), td, td, _const((1, D)), _const((1, D))],
        out_specs=[td, td, _const((8, D))],
        out_shape=[jax.ShapeDtypeStruct((s, D), F32), jax.ShapeDtypeStruct((s, D), BF16),
                   jax.ShapeDtypeStruct((8, D), F32)],
        compiler_params=_cp(("arbitrary",)),
    )(a, w_down, h1, target, _row(ln2_g), _row(ln2_b))


def _d_act(dz2b, w_down, tm=512):
    s = dz2b.shape[0]

    def body(dz_ref, w_ref, o_ref):
        o_ref[...] = _nt(dz_ref[...], w_ref[...]).astype(BF16)

    return pl.pallas_call(
        body, name="d_act", grid=(s // tm,),
        in_specs=[pl.BlockSpec((tm, D), lambda i: (i, 0)), _resident((FF, D))],
        out_specs=pl.BlockSpec((tm, FF), lambda i: (i, 0)),
        out_shape=jax.ShapeDtypeStruct((s, FF), BF16),
        compiler_params=_cp(("parallel",)),
    )(dz2b, w_down)


def _conv_gelu_bwd(da, up, g, a1, cwb, tm=512, tn=FF // 2, chunk_rows=16):
    s = da.shape[0]
    n_i = s // tm
    n_c = tm // chunk_rows

    def body(da_ref, up_ref, g_ref, a1_ref, c_ref, dup_ref, dc_ref, carry):
        @pl.when(pl.program_id(1) == 0)
        def _():
            carry[...] = jnp.zeros_like(carry)
            dc_ref[...] = jnp.zeros_like(dc_ref)

        def fold(v):
            return jnp.sum(v.reshape(chunk_rows // 8, 8, v.shape[1]), axis=0)

        def chunk(cc, state):
            after, sums = state
            rows = pl.ds((n_c - 1 - cc) * chunk_rows, chunk_rows)
            da_c = da_ref[rows, :]
            dus = ((da_c * a1_ref[rows, :]).astype(F32), (da_c * g_ref[rows, :]).astype(F32))
            head, new_sums = [], []
            for half in (0, 1):
                du = dus[half]
                up = up_ref[half, rows, :].astype(F32)
                l1, l2 = _shifted(du, after[half], row, False)
                dup = (du * c_ref[2, half:half + 1, :] + l1 * c_ref[1, half:half + 1, :]
                       + l2 * c_ref[0, half:half + 1, :])
                dup_ref[half, rows, :] = dup.astype(BF16)
                parts = (fold(l2 * up), fold(l1 * up), fold(du * up), fold(du))
                new_sums.append(parts if sums is None else tuple(a + b for a, b in zip(sums[half], parts)))
                head.append(du[:8])
            return tuple(head), new_sums

        row = lax.broadcasted_iota(jnp.int32, (8, tn), 0)
        state = ((carry[0], carry[1]), None)
        for cc in range(n_c):
            state = chunk(cc, state)
        head, sums = state
        for half in (0, 1):
            carry[half] = head[half]
            for k in range(4):
                dc_ref[k, half:half + 1, :] += jnp.sum(sums[half][k], axis=0, keepdims=True)

    rev = lambda ii: n_i - 1 - ii
    tile = pl.BlockSpec((tm, tn), lambda j, ii: (rev(ii), j))
    pair = pl.BlockSpec((2, tm, tn), lambda j, ii: (0, rev(ii), j))
    per_col = pl.BlockSpec((4, 2, tn), lambda j, ii: (0, 0, j))
    return pl.pallas_call(
        body, name="conv_gelu_bwd", grid=(FF // tn, n_i),
        in_specs=[tile, pair, tile, tile, per_col],
        out_specs=[pair, per_col],
        out_shape=[jax.ShapeDtypeStruct((2, s, FF), BF16), jax.ShapeDtypeStruct((4, 2, FF), F32)],
        scratch_shapes=[pltpu.VMEM((2, 8, tn), F32)],
        compiler_params=_cp(("parallel", "arbitrary")),
    )(da, up, g, a1, cwb)


def _dh1_ln1_bwd(dz2, dup, w_up, z1, ln1_g, tm=512):
    s = dz2.shape[0]

    def body(dz2_ref, dup_ref, w_ref, z1_ref, g_ref, dz1_ref, dz1b_ref, st_ref):
        @pl.when(pl.program_id(0) == 0)
        def _():
            st_ref[...] = jnp.zeros_like(st_ref)

        dh = ALPHA * dz2_ref[...] + _nt(dup_ref[0], w_ref[:, :FF]) + _nt(dup_ref[1], w_ref[:, FF:])
        zh, rstd = _layer_norm_stats(z1_ref[...])
        st_ref[0:1, :] += jnp.sum(dh * zh, axis=0, keepdims=True)
        st_ref[1:2, :] += jnp.sum(dh, axis=0, keepdims=True)
        dz = _layer_norm_bwd(dh, zh, rstd, g_ref[...])
        dz1_ref[...] = dz
        dz1b_ref[...] = dz.astype(BF16)

    td = pl.BlockSpec((tm, D), lambda i: (i, 0))
    return pl.pallas_call(
        body, name="dh1_ln1_bwd", grid=(s // tm,),
        in_specs=[td, pl.BlockSpec((2, tm, FF), lambda i: (0, i, 0)), _resident((D, 2 * FF)), td, _const((1, D))],
        out_specs=[td, td, _const((8, D))],
        out_shape=[jax.ShapeDtypeStruct((s, D), F32), jax.ShapeDtypeStruct((s, D), BF16),
                   jax.ShapeDtypeStruct((8, D), F32)],
        compiler_params=_cp(("arbitrary",), 58),
    )(dz2, dup, w_up, z1, _row(ln1_g))


def _dcat_rms_bwd(dz1b, w_o, o_a, o_b, norm_a_g, norm_b_g, tm=512):
    s = dz1b.shape[0]

    def body(dz_ref, w_ref, oa_ref, ob_ref, ga_ref, gb_ref, da_ref, db_ref, st_ref):
        @pl.when(pl.program_id(0) == 0)
        def _():
            st_ref[...] = jnp.zeros_like(st_ref)

        dcat = _nt(dz_ref[...], w_ref[...])
        for k, (o_ref, g_ref, d_ref) in enumerate(((oa_ref, ga_ref, da_ref), (ob_ref, gb_ref, db_ref))):
            o = jnp.concatenate([o_ref[j] for j in range(4)], axis=1)
            dn = dcat[:, 512 * k:512 * (k + 1)]
            rr = _rms(o)
            oh = o * rr
            st_ref[k:k + 1, :] += jnp.sum(dn * oh, axis=0, keepdims=True)
            doh = dn * g_ref[...]
            d_o = rr * (doh - oh * jnp.mean(doh * oh, axis=-1, keepdims=True))
            for j in range(4):
                d_ref[j] = d_o[:, 128 * j:128 * (j + 1)]

    t512 = pl.BlockSpec((4, tm, 128), lambda i: (0, i, 0))
    return pl.pallas_call(
        body, name="dcat_rms_bwd", grid=(s // tm,),
        in_specs=[pl.BlockSpec((tm, D), lambda i: (i, 0)), _resident((D, D)), t512, t512,
                  _const((1, 512)), _const((1, 512))],
        out_specs=[t512, t512, _const((8, 512))],
        out_shape=[jax.ShapeDtypeStruct((4, s, 128), F32), jax.ShapeDtypeStruct((4, s, 128), F32),
                   jax.ShapeDtypeStruct((8, 512), F32)],
        compiler_params=_cp(("arbitrary",)),
    )(dz1b, w_o, o_a, o_b, _row(norm_a_g), _row(norm_b_g))


def _grad_w_in(dparts, xb, tk=2048):
    s = xb.shape[0]
    nk = s // tk

    def body(qa, ka, va, qb, kb, vb, x_ref, o_ref, ob_ref):
        i = pl.program_id(0)
        k = pl.program_id(1)

        @pl.when(k == 0)
        def _():
            o_ref[...] = jnp.zeros_like(o_ref)

        def add(blocks):
            o_ref[...] += _tn(jnp.concatenate(blocks, axis=1), x_ref[...])

        pl.when(i == 0)(lambda: add([qa[j] for j in range(4)] + [ka[...], va[...]]))
        pl.when(i == 1)(lambda: add([qb[j] for j in range(4)] + [kb[0], kb[1]]))
        pl.when(i == 2)(lambda: add([kb[0], kb[1]] + [vb[j] for j in range(4)]))

        @pl.when(k == nk - 1)
        def _():
            ob_ref[...] = o_ref[...].astype(BF16)

    def during(tile):
        return lambda i, k: jnp.where(i == tile, k, jnp.where(i < tile, 0, nk - 1))

    quad = lambda tile: pl.BlockSpec((4, tk, 128), lambda i, k: (0, during(tile)(i, k), 0))
    one = pl.BlockSpec((tk, 128), lambda i, k: (during(0)(i, k), 0))
    kb_spec = pl.BlockSpec((2, tk, 128), lambda i, k: (jnp.where(i == 2, 1, 0), jnp.where(i == 0, 0, k), 0))
    return pl.pallas_call(
        body, name="grad_w_in", grid=(3, nk),
        in_specs=[quad(0), one, one, quad(1), kb_spec, quad(2), pl.BlockSpec((tk, D), lambda i, k: (k, 0))],
        out_specs=[pl.BlockSpec((WA, D), lambda i, k: (i, 0))] * 2,
        out_shape=[pltpu.HBM((WIN, D), F32), pltpu.HBM((WIN, D), BF16)],
        compiler_params=_cp(("parallel", "arbitrary"), mb=56),
    )(*dparts, xb)


def _grad_x(dz1, dparts, w_in_t, zero, tm=512):
    s = dz1.shape[0]

    def body(dz_ref, qa, ka, va, qb, kb, vb, w_ref, z_ref, o_ref):
        dp = jnp.concatenate([qa[j] for j in range(4)] + [ka[...], va[...]]
                             + [ref[j] for ref in (qb, kb, vb) for j in range(4)], axis=1)
        o_ref[...] = ALPHA * dz_ref[...] + _nn(dp, w_ref[...]) + z_ref[0:1, 0:1]

    td = pl.BlockSpec((tm, D), lambda i: (i, 0))
    quad = pl.BlockSpec((4, tm, 128), lambda i: (0, i, 0))
    one = pl.BlockSpec((tm, 128), lambda i: (i, 0))
    return pl.pallas_call(
        body, name="grad_x", grid=(s // tm,),
        in_specs=[td, quad, one, one, quad, quad, quad, _resident((WIN, D)), _const((8, 128))],
        out_specs=td, out_shape=jax.ShapeDtypeStruct((s, D), F32),
        compiler_params=_cp(("parallel",)),
    )(dz1, *dparts, w_in_t, zero)


def _place():
    return lax.axis_index("x"), lax.axis_index("y"), lax.axis_index("c")


def _other_chips(x, y):
    return [(1 - x, y), (x, 1 - y), (1 - x, 1 - y)]


def _hbm(a):
    return pltpu.with_memory_space_constraint(a, pltpu.HBM)


def _gather_w_in(shard, conv_w):
    rows_k = shard.shape[0]
    half = rows_k // 2

    def body(src, conv_src, out, conv_out, send_sems, recv_sems):
        x, y, c = _place()
        b = 2 * x + y
        sibling = (x, y, 1 - c)
        chips = _other_chips(x, y)

        def copy(idx, chip_b, core, to, first_hop=False):
            rows = out.at[pl.ds(pl.multiple_of(chip_b * rows_k + core * half, 16), half)]
            s_ref = src.at[pl.ds(pl.multiple_of(core * half, 16), half)] if first_hop else rows
            return pltpu.make_async_remote_copy(src_ref=s_ref, dst_ref=rows, send_sem=send_sems.at[idx],
                                                recv_sem=recv_sems.at[idx], device_id=to, device_id_type=MESH)

        def own_copy():
            return pltpu.make_async_remote_copy(
                src_ref=src, dst_ref=out.at[pl.ds(pl.multiple_of(b * rows_k, 16), rows_k)], send_sem=send_sems.at[6],
                recv_sem=recv_sems.at[6], device_id=sibling, device_id_type=MESH)

        def conv_copy(idx, chip_b, to):
            return pltpu.make_async_remote_copy(src_ref=conv_src, dst_ref=conv_out.at[chip_b],
                                                send_sem=send_sems.at[7 + idx], recv_sem=recv_sems.at[7 + idx],
                                                device_id=to, device_id_type=MESH)

        started = [own_copy(), conv_copy(3, b, sibling)]
        for jn, chip in enumerate(chips):
            started += [copy(jn, b, c, (chip[0], chip[1], c), first_hop=True), conv_copy(jn, b, (chip[0], chip[1], c))]
        for cp in started:
            cp.start()
        for jn, chip in enumerate(chips):
            cb = 2 * chip[0] + chip[1]
            copy(jn, cb, c, (chip[0], chip[1], c)).wait_recv()
            cp = copy(3 + jn, cb, c, sibling)
            cp.start()
            started.append(cp)
        for jn, chip in enumerate(chips):
            cb = 2 * chip[0] + chip[1]
            copy(3 + jn, cb, 1 - c, sibling).wait_recv()
            conv_copy(jn, cb, (chip[0], chip[1], c)).wait_recv()
        own_copy().wait_recv()
        conv_copy(3, b, sibling).wait_recv()
        for cp in started:
            cp.wait_send()

    return pl.pallas_call(
        body, name="gather_w_in",
        in_specs=[ANY, ANY], out_specs=[ANY, ANY],
        out_shape=[jax.ShapeDtypeStruct((N_CHIPS * rows_k, D), BF16), jax.ShapeDtypeStruct((N_CHIPS,) + conv_w.shape, F32)],
        scratch_shapes=[pltpu.SemaphoreType.DMA((11,)), pltpu.SemaphoreType.DMA((11,))],
        compiler_params=pltpu.CompilerParams(has_side_effects=True),
    )(shard, conv_w)


def _weight_copies(shard, land, send_sems, recv_sems, arrivals):
    x, y, c = _place()
    n_rows, n_cols = shard.shape
    peers = [(px, py, c) for px, py in _other_chips(x, y)] + [(x, y, 1 - c)]
    cps = []
    for jn, peer in enumerate(peers):
        at = 2 * peer[0] + peer[1] if arrivals else 2 * x + y
        if land.shape[1] == n_cols:
            dst = land.at[pl.ds(pl.multiple_of(at * n_rows, 16), n_rows)]
        else:
            dst = land.at[:, pl.ds(pl.multiple_of(at * n_cols, 128), n_cols)]
        cps.append(pltpu.make_async_remote_copy(src_ref=shard, dst_ref=dst, send_sem=send_sems.at[jn],
                                                recv_sem=recv_sems.at[jn], device_id=peer, device_id_type=MESH))
    return cps


def _weights_start(shards, after):
    n = len(shards)
    lands = [lax.empty((N_CHIPS * sh.shape[0], D) if sh.shape[1] == D else (D, N_CHIPS * sh.shape[1]), BF16)
             for sh in shards]

    def body(*refs):
        src, land = refs[:n], refs[n:2 * n]
        send_sems, recv_sems = refs[2 * n + 1:3 * n + 1], refs[3 * n + 1:4 * n + 1]
        for k in range(n):
            for send in _weight_copies(src[k], land[k], send_sems[k], recv_sems[k], False):
                send.start()
        refs[-1][...] = jnp.zeros_like(refs[-1])

    res = pl.pallas_call(
        body, name="weights_start",
        in_specs=[HBM] * (2 * n) + [ANY], out_specs=[SEM] * (2 * n) + [HBM] * (2 * n) + [VMEM],
        out_shape=[pltpu.SemaphoreType.DMA((4,))] * (2 * n)
        + [pltpu.HBM(a.shape, a.dtype) for a in (*shards, *lands)] + [jax.ShapeDtypeStruct((8, 128), F32)],
        input_output_aliases={i: i + 2 * n for i in range(2 * n)},
        compiler_params=pltpu.CompilerParams(has_side_effects=DATAFLOW),
    )(*[_hbm(a) for a in (*shards, *lands)], after)
    return [(res[k], res[n + k], res[2 * n + k], res[3 * n + k]) for k in range(n)], res[-1]


def _weights_wait(started, after, name):
    send_sems, recv_sems, shard, land = started

    def body(s_ref, l_ref, send_ref, recv_ref, after_ref, s_out, l_out):
        for cp in _weight_copies(s_ref, l_ref, send_ref, recv_ref, True):
            cp.wait_send()
            cp.wait_recv()

    return pl.pallas_call(
        body, name=name,
        in_specs=[HBM, HBM, SEM, SEM, ANY], out_specs=[HBM, HBM],
        out_shape=[pltpu.HBM(shard.shape, shard.dtype), pltpu.HBM(land.shape, land.dtype)],
        input_output_aliases={0: 0, 1: 1},
        compiler_params=pltpu.CompilerParams(has_side_effects=DATAFLOW),
    )(shard, land, send_sems, recv_sems, after)[1]


def _grad_copies(g_ref, land_ref, send_sems, recv_sems):
    x, y, c = _place()
    cps = []
    for d in range(1, 8):
        px, py, pc = x ^ (d >> 2), y ^ ((d >> 1) & 1), c ^ (d & 1)
        cps.append(pltpu.make_async_remote_copy(
            src_ref=g_ref.at[2 * px + py, pc], dst_ref=land_ref.at[d - 1], send_sem=send_sems.at[d - 1],
            recv_sem=recv_sems.at[d - 1], device_id=(px, py, pc), device_id_type=MESH))
    return cps


def _grads_start(grads_b, name):
    n = len(grads_b)
    lands = [lax.empty((7, g.shape[2], D), BF16) for g in grads_b]

    def body(*refs):
        g, land = refs[:n], refs[n:2 * n]
        send_sems, recv_sems = refs[2 * n:3 * n], refs[3 * n:4 * n]
        for k in range(n):
            for cp in _grad_copies(g[k], land[k], send_sems[k], recv_sems[k]):
                cp.start()
        refs[-1][...] = jnp.zeros_like(refs[-1])

    res = pl.pallas_call(
        body, name=name,
        in_specs=[HBM] * (2 * n), out_specs=[SEM] * (2 * n) + [HBM] * (2 * n) + [VMEM],
        out_shape=[pltpu.SemaphoreType.DMA((7,))] * (2 * n)
        + [pltpu.HBM(a.shape, a.dtype) for a in (*grads_b, *lands)] + [jax.ShapeDtypeStruct((8, 128), F32)],
        input_output_aliases={i: i + 2 * n for i in range(2 * n)},
        compiler_params=pltpu.CompilerParams(has_side_effects=DATAFLOW),
    )(*[_hbm(a) for a in (*grads_b, *lands)])
    return [(res[k], res[n + k], res[2 * n + k], res[3 * n + k]) for k in range(n)], res[-1]


def _grads_wait(started, after, name):
    n = len(started)

    def body(*refs):
        g, land = refs[:n], refs[n:2 * n]
        send_sems, recv_sems = refs[2 * n:3 * n], refs[3 * n:4 * n]
        for k in range(n):
            for cp in _grad_copies(g[k], land[k], send_sems[k], recv_sems[k]):
                cp.wait_send()
                cp.wait_recv()

    gs = [st[2] for st in started]
    lands = [st[3] for st in started]
    res = pl.pallas_call(
        body, name=name,
        in_specs=[HBM] * (2 * n) + [SEM] * (2 * n) + [ANY], out_specs=[HBM] * (2 * n),
        out_shape=[pltpu.HBM(a.shape, a.dtype) for a in (*gs, *lands)],
        input_output_aliases={i: i for i in range(2 * n)},
        compiler_params=pltpu.CompilerParams(has_side_effects=DATAFLOW),
    )(*gs, *lands, *[st[0] for st in started], *[st[1] for st in started], after)
    return res[n:]


def _sum_partials(grad4, got, cb, name, tr):
    h = grad4.shape[2]
    per_half = h // tr

    def body(cb_ref, g_ref, o_ref, out_ref):
        acc = g_ref[...]
        for j in range(7):
            acc = acc + o_ref[j].astype(F32)
        out_ref[...] = acc

    return pl.pallas_call(
        body, name=name,
        grid_spec=pltpu.PrefetchScalarGridSpec(
            num_scalar_prefetch=1, grid=(per_half,),
            in_specs=[pl.BlockSpec((None, None, tr, D), lambda i, cb_ref: (cb_ref[1], cb_ref[0], i, 0)),
                      pl.BlockSpec((7, tr, D), lambda i, cb_ref: (0, i, 0))],
            out_specs=pl.BlockSpec((tr, D), lambda i, cb_ref: (cb_ref[0] * per_half + i, 0))),
        out_shape=pltpu.HBM((2 * h, D), F32),
        compiler_params=_cp(("arbitrary",)),
    )(cb, grad4, _hbm(got))


def _swap_halves(shards, name):
    n = len(shards)

    def body(*refs):
        out, send_sems, recv_sems = refs[n:2 * n], refs[2 * n], refs[2 * n + 1]
        x, y, c = _place()
        cps = []
        for k in range(n):
            h = shards[k].shape[0] // 2
            mine = out[k].at[pl.ds(pl.multiple_of(c * h, 8), h)]
            cp = pltpu.make_async_remote_copy(src_ref=mine, dst_ref=mine, send_sem=send_sems.at[k],
                                              recv_sem=recv_sems.at[k], device_id=(x, y, 1 - c), device_id_type=MESH)
            cp.start()
            cps.append(cp)
        for cp in cps:
            cp.wait()

    return pl.pallas_call(
        body, name=name,
        in_specs=[ANY] * n, out_specs=[ANY] * n,
        out_shape=[jax.ShapeDtypeStruct(sh.shape, F32) for sh in shards],
        input_output_aliases={k: k for k in range(n)},
        scratch_shapes=[pltpu.SemaphoreType.DMA((n,)), pltpu.SemaphoreType.DMA((n,))],
        compiler_params=pltpu.CompilerParams(has_side_effects=True),
    )(*shards)


def _small_copies(small_ref, land_ref, send_sems, recv_sems):
    x, y, c = _place()
    me = 4 * x + 2 * y + c
    cps = []
    for d in range(1, 8):
        px, py, pc = x ^ (d >> 2), y ^ ((d >> 1) & 1), c ^ (d & 1)
        cps.append(pltpu.make_async_remote_copy(
            src_ref=small_ref, dst_ref=land_ref.at[me], send_sem=send_sems.at[d - 1], recv_sem=recv_sems.at[d - 1],
            device_id=(px, py, pc), device_id_type=MESH))
    return cps


def _small_start(small):
    land = lax.empty((8,) + small.shape, F32)

    def body(s_ref, l_ref, send_sems, recv_sems, s_thru, l_thru, token):
        for cp in _small_copies(s_ref, l_ref, send_sems, recv_sems):
            cp.start()
        token[...] = jnp.zeros_like(token)

    res = pl.pallas_call(
        body, name="small_start",
        in_specs=[HBM, HBM], out_specs=[SEM, SEM, HBM, HBM, VMEM],
        out_shape=[pltpu.SemaphoreType.DMA((7,)), pltpu.SemaphoreType.DMA((7,)), pltpu.HBM(small.shape, F32),
                   pltpu.HBM(land.shape, F32), jax.ShapeDtypeStruct((8, 128), F32)],
        input_output_aliases={0: 2, 1: 3},
        compiler_params=pltpu.CompilerParams(has_side_effects=DATAFLOW),
    )(_hbm(small), _hbm(land))
    return res[:4], res[4]


def _small_wait(started, after):
    send_sems, recv_sems, small, land = started

    def body(s_ref, l_ref, send_ref, recv_ref, after_ref, s_out, l_out):
        for cp in _small_copies(s_ref, l_ref, send_ref, recv_ref):
            cp.wait_send()
            cp.wait_recv()

    return pl.pallas_call(
        body, name="small_wait",
        in_specs=[HBM, HBM, SEM, SEM, ANY], out_specs=[HBM, HBM],
        out_shape=[pltpu.HBM(small.shape, F32), pltpu.HBM(land.shape, F32)],
        input_output_aliases={0: 0, 1: 1},
        compiler_params=pltpu.CompilerParams(has_side_effects=DATAFLOW),
    )(small, land, send_sems, recv_sems, after)


def _small_sum(small, land, me):
    rows = small.shape[0]

    def body(me_ref, s_ref, l_ref, o_ref):
        acc = None
        for k in range(8):
            term = jnp.where(me_ref[0] == k, s_ref[...], l_ref[k])
            acc = term if k == 0 else acc + term
        o_ref[...] = acc

    return pl.pallas_call(
        body, name="small_sum",
        in_specs=[SMEM, VMEM, VMEM], out_specs=VMEM,
        out_shape=jax.ShapeDtypeStruct((rows, D), F32),
    )(me, small, land)


def _adamw(w, g, m, v, name, tr, g_transposed=False):
    rows, cols = w.shape

    def body(w_ref, g_ref, m_ref, v_ref, d_ref, nm_ref, nv_ref, *gt_ref):
        g_ = g_ref[...]
        if g_transposed:
            g_ = g_.T
            gt_ref[0][...] = g_
        nm = ADAM_B1 * m_ref[...] + (1.0 - ADAM_B1) * g_
        nv = ADAM_B2 * v_ref[...] + (1.0 - ADAM_B2) * (g_ * g_)
        m_hat = nm / (1.0 - ADAM_B1 ** ADAM_STEP)
        v_hat = nv / (1.0 - ADAM_B2 ** ADAM_STEP)
        d_ref[...] = -ADAM_LR * (m_hat / (jnp.sqrt(v_hat) + ADAM_EPS) + ADAM_WD * w_ref[...])
        nm_ref[...] = nm
        nv_ref[...] = nv

    spec = pl.BlockSpec((tr, cols), lambda i: (i, 0))
    g_spec = pl.BlockSpec((cols, tr), lambda i: (0, i)) if g_transposed else spec
    n_out = 4 if g_transposed else 3
    return pl.pallas_call(
        body, name=name, grid=(rows // tr,),
        in_specs=[spec, g_spec, spec, spec], out_specs=[spec] * n_out,
        out_shape=[jax.ShapeDtypeStruct((rows, cols), F32)] * n_out,
        compiler_params=_cp(("parallel",)),
    )(*[_hbm(a) for a in (w, g, m, v)])


def _local_step(x, target, w_in_t, late_weights, norm_a_g, norm_b_g, sinks_a, ln1_g, ln1_b,
                conv_w, conv_b, ln2_g, ln2_b, slopes, on_grad, on_small):
    cwb = jnp.concatenate([conv_w, conv_b[None]], axis=0).reshape(4, 2, FF)

    proj, xb = _proj(x, w_in_t, "proj")
    o_a, lse_a = _attn_a_fwd(proj, sinks_a)
    fwd_b = None
    for r in reversed(B_DILATIONS):
        fwd_b = _attn_b_fwd(proj, slopes, r, fwd_b)
    o_b, lse_b = fwd_b
    w_o = late_weights(1, lse_b)
    cat, z1, h1, h1b = _mix_ln1(x, o_a, o_b, norm_a_g, norm_b_g, w_o, ln1_g, ln1_b)
    w_up = late_weights(2, h1b)
    up = _up_proj(h1b, w_up)
    a, gate, a1 = _conv_gelu(up, cwb)
    w_down = late_weights(3, a)
    dz2, dz2b, st2 = _down_ln2_loss(a, w_down, h1, target, ln2_g, ln2_b)

    on_grad(3, *_grad_w(a, dz2b, "grad_w_down", tm=FF // 2))
    dup, dconv = _conv_gelu_bwd(_d_act(dz2b, w_down), up, gate, a1, cwb)
    on_grad(2, *_grad_w(dup, h1b, "grad_w_up", tm=FF // 2, lhs_halves=True))
    dz1, dz1b, st1 = _dh1_ln1_bwd(dz2, dup, w_up, z1, ln1_g)
    tok = on_grad(1, *_grad_w(cat, dz1b, "grad_w_o", tm=512))
    d_oa, d_ob, st_n = _dcat_rms_bwd(dz1b, w_o, o_a, o_b, norm_a_g + tok[0, 0], norm_b_g)
    dqa, dka, dva, dsink = _attn_a_bwd(proj, sinks_a, d_oa, o_a, lse_a)
    dconv = dconv.reshape(4, 2 * FF)
    tok = on_small(dict(loss=st2[2, 0:1], norm_a_g=st_n[0], norm_b_g=st_n[1], sinks_a=dsink[:, 0],
                        ln1_g=st1[0], ln1_b=st1[1], conv_w=dconv[0:3].reshape(-1), conv_b=dconv[3],
                        ln2_g=st2[0], ln2_b=st2[1]))
    slopes = slopes + tok[0, 0]
    bwd_b = None
    for r in reversed(B_DILATIONS):
        bwd_b = _attn_b_bwd(proj, slopes, d_ob, o_b, lse_b, r, bwd_b, BF16 if r == 1 else F32)
    dparts = tuple(_hbm(a) for a in (dqa, dka, dva, *bwd_b))
    tok = on_grad(0, *_grad_w_in(dparts, xb))
    return _grad_x(dz1, dparts, w_in_t, tok)


SMALL_ORDER = ("loss", "norm_a_g", "norm_b_g", "sinks_a", "ln1_g", "ln1_b", "conv_b", "ln2_g", "ln2_b", "conv_w")
SMALL_SIZES = dict(loss=1, norm_a_g=512, norm_b_g=512, sinks_a=8, ln1_g=D, ln1_b=D, conv_b=2 * FF, ln2_g=D, ln2_b=D,
                   conv_w=3 * 2 * FF)


def _pack(parts, rows):
    flat = jnp.concatenate([parts[k].reshape(-1).astype(F32) for k in parts])
    return jnp.pad(flat, (0, rows * D - flat.shape[0])).reshape(rows, D)


def _unpack(buf, names, sizes):
    flat = buf.reshape(-1)
    out, at = {}, 0
    for k in names:
        out[k] = flat[at:at + sizes[k]]
        at += sizes[k]
    return out


def kernel(x, w_in, norm_a_g, norm_b_g, sinks_a, w_o, ln1_g, ln1_b, w_up, conv_w, conv_b, w_down, ln2_g, ln2_b, loss_target, m_w_in, m_norm_a_g, m_norm_b_g, m_sinks_a, m_w_o, m_ln1_g, m_ln1_b, m_w_up, m_conv_w, m_conv_b, m_w_down, m_ln2_g, m_ln2_b, v_w_in, v_norm_a_g, v_norm_b_g, v_sinks_a, v_w_o, v_ln1_g, v_ln1_b, v_w_up, v_conv_w, v_conv_b, v_w_down, v_ln2_g, v_ln2_b):
    xi, yi, ci = _place()
    chip = (2 * xi + yi).astype(I32)
    core = ci.astype(I32)

    w_in_rows, m_w_in_rows, v_w_in_rows = w_in.T, m_w_in.T, v_w_in.T
    shards = (w_in_rows.astype(BF16), w_o.astype(BF16), w_up.astype(BF16), w_down.astype(BF16))
    w_in_t, conv_w4 = _gather_w_in(shards[0], conv_w)
    conv_w_f = conv_w4.transpose(1, 0, 2).reshape(3, 2 * FF)
    w_started, w_tok = _weights_start(shards[1:], conv_w4)
    slopes = jnp.asarray(SLOPES, F32) + w_tok[0, 0]

    halves_rows = [r // 2 for r in SHARD_ROWS]
    grads4, grads_b4, started = [None] * 4, [None] * 4, [None] * 4

    def on_grad(k, g, g_b):
        grads4[k] = g.reshape(N_CHIPS, 2, halves_rows[k], D)
        grads_b4[k] = g_b.reshape(N_CHIPS, 2, halves_rows[k], D)
        if k > 1:
            return None
        group = (1, 2, 3) if k == 1 else (0,)
        sts, tok = _grads_start([grads_b4[i] for i in group], f"grads_start_{k}")
        for i, st in zip(group, sts):
            started[i] = st
        return tok

    small_rows = 32
    small_started = []

    def on_small(parts):
        st, tok = _small_start(_pack({k: parts[k] for k in SMALL_ORDER}, small_rows))
        small_started.append(st)
        return tok

    gx = _local_step(
        x[0], loss_target[0], w_in_t, lambda k, after: _weights_wait(w_started[k - 1], after, f"weights_wait_{k}"),
        norm_a_g, norm_b_g, sinks_a, ln1_g, ln1_b, conv_w_f, conv_b, ln2_g, ln2_b, slopes, on_grad, on_small)

    tiles = (96, 128, 352, 176)
    core_chip = jnp.stack([core, chip])
    got = _grads_wait(started[1:], gx, "grads_wait_1")
    halves = [_sum_partials(grads4[k], got[k - 1], core_chip, f"sum_partials_{k}", tiles[k]) for k in (1, 2, 3)]
    g_w_o, g_w_up_rows, g_w_down = _swap_halves(halves, "swap_halves")
    delta, new_m, new_v = {}, {}, {}
    for k, g, tr in (("w_o", g_w_o, 128), ("w_down", g_w_down, 176)):
        delta[k], new_m[k], new_v[k] = _adamw(dict(w_o=w_o, w_down=w_down)[k], g, dict(w_o=m_w_o, w_down=m_w_down)[k],
                                              dict(w_o=v_w_o, w_down=v_w_down)[k], f"adamw_{k}", tr)
    delta["w_up"], new_m["w_up"], new_v["w_up"], g_w_up = _adamw(w_up, g_w_up_rows, m_w_up, v_w_up, "adamw_w_up", 256,
                                                                 g_transposed=True)

    got = _grads_wait(started[:1], delta["w_up"], "grads_wait_0")
    half_in = _sum_partials(grads4[0], got[0], core_chip, "sum_partials_0", tiles[0])
    (g_w_in_rows,) = _swap_halves([half_in], "swap_halves_in")
    small_mine, small_land = _small_wait(small_started[0], g_w_in_rows)
    totals = _small_sum(small_mine, small_land, (4 * xi + 2 * yi + ci).astype(I32).reshape(1))
    tot = _unpack(totals, SMALL_ORDER, SMALL_SIZES)
    loss = tot["loss"][0]
    cols = 2 * FF // N_CHIPS
    g_conv_w = lax.dynamic_slice(tot["conv_w"].reshape(3, 2 * FF), (0, chip * cols), (3, cols))
    g_small = dict(norm_a_g=tot["norm_a_g"], norm_b_g=tot["norm_b_g"], sinks_a=tot["sinks_a"], ln1_g=tot["ln1_g"],
                   ln1_b=tot["ln1_b"], conv_w=g_conv_w, conv_b=tot["conv_b"], ln2_g=tot["ln2_g"], ln2_b=tot["ln2_b"])

    weights = dict(w_in=w_in, norm_a_g=norm_a_g, norm_b_g=norm_b_g, sinks_a=sinks_a, w_o=w_o, ln1_g=ln1_g, ln1_b=ln1_b,
                   w_up=w_up, conv_w=conv_w, conv_b=conv_b, w_down=w_down, ln2_g=ln2_g, ln2_b=ln2_b)
    ms = dict(w_in=m_w_in, norm_a_g=m_norm_a_g, norm_b_g=m_norm_b_g, sinks_a=m_sinks_a, w_o=m_w_o, ln1_g=m_ln1_g,
              ln1_b=m_ln1_b, w_up=m_w_up, conv_w=m_conv_w, conv_b=m_conv_b, w_down=m_w_down, ln2_g=m_ln2_g, ln2_b=m_ln2_b)
    vs = dict(w_in=v_w_in, norm_a_g=v_norm_a_g, norm_b_g=v_norm_b_g, sinks_a=v_sinks_a, w_o=v_w_o, ln1_g=v_ln1_g,
              ln1_b=v_ln1_b, w_up=v_w_up, conv_w=v_conv_w, conv_b=v_conv_b, w_down=v_w_down, ln2_g=v_ln2_g, ln2_b=v_ln2_b)
    order = list(weights)
    grad = dict(g_small, w_in=g_w_in_rows.T, w_o=g_w_o, w_up=g_w_up, w_down=g_w_down)

    delta["w_in"], new_m["w_in"], new_v["w_in"] = [
        a.T for a in _adamw(w_in_rows, g_w_in_rows, m_w_in_rows, v_w_in_rows, "adamw_w_in", 144)]
    small_names = [k for k in order if k not in delta]
    sizes = {k: weights[k].size for k in small_names}
    rows = 16
    packed = [_pack({k: src[k] for k in small_names}, rows) for src in (weights, grad, ms, vs)]
    for res, buf in zip((delta, new_m, new_v), _adamw(*packed, "adamw_small", rows)):
        for k, val in _unpack(buf, small_names, sizes).items():
            res[k] = val.reshape(weights[k].shape)

    return (loss, gx[None], *[grad[k] for k in order], *[delta[k] for k in order],
            *[new_m[k] for k in order], *[new_v[k] for k in order])
```
